```python
import jax, jax.numpy as jnp
from jax import lax
import numpy as np

D_MODEL = 1024
BATCH = 8
SEQ = 2048
DEPTH = 1

SB_HEADS = 8
SB_HEAD_DIM = 64
SB_WIDTH = SB_HEADS * SB_HEAD_DIM
MLA_HEADS = 4
MLA_NOPE_DIM = 128
MLA_ROPE_DIM = 64
MLA_QK_DIM = MLA_NOPE_DIM + MLA_ROPE_DIM
MLA_V_DIM = 128
MLA_Q_RANK = 384
MLA_KV_RANK = 256
MLA_WIDTH = MLA_HEADS * MLA_V_DIM
MIX_WIDTH = SB_WIDTH + MLA_WIDTH
IN_PROJ_WIDTH = 3 * SB_WIDTH + MLA_Q_RANK + MLA_KV_RANK + MLA_ROPE_DIM
D_FF = -(-8 * D_MODEL // (3 * 256)) * 256
N_MOD = 6
BLOCK_Q = 128
ROPE_THETA = 10000.0
EPS = 1e-6

kernel_name = "hymba_sb_mla_adaln_layer"


def rms_norm(x, g):
    xf = x.astype(jnp.float32)
    y = xf * lax.rsqrt(jnp.mean(xf * xf, axis=-1, keepdims=True) + EPS)
    return (y * g.astype(jnp.float32)).astype(x.dtype)


def apply_rope(x, positions):
    half = x.shape[-1] // 2
    freqs = 1.0 / (ROPE_THETA ** (jnp.arange(half, dtype=jnp.float32) / half))
    ang = positions.astype(jnp.float32)[:, :, None, None] * freqs
    cos, sin = jnp.cos(ang), jnp.sin(ang)
    xf = x.astype(jnp.float32)
    x1, x2 = xf[..., :half], xf[..., half:]
    return jnp.concatenate([x1 * cos - x2 * sin, x2 * cos + x1 * sin], axis=-1).astype(x.dtype)


def stick_breaking_attention(q, k, v):
    S = q.shape[1]
    scale = SB_HEAD_DIM ** -0.5
    outs = []
    for i in range(S // BLOCK_Q):
        q0 = i * BLOCK_Q
        kend = q0 + BLOCK_Q
        z = jnp.einsum('bthd,bshd->bhts', q[:, q0:kend], k[:, :kend]).astype(jnp.float32) * scale
        t_idx = q0 + jnp.arange(BLOCK_Q)[:, None]
        s_idx = jnp.arange(kend)[None, :]
        strict = s_idx < t_idx
        log_beta = jax.nn.log_sigmoid(z)
        log_1m = jnp.where(strict, jax.nn.log_sigmoid(-z), 0.0)
        after = lax.cumsum(log_1m, axis=3, reverse=True) - log_1m
        w = jnp.where(strict, jnp.exp(log_beta + after), 0.0)
        outs.append(jnp.einsum('bhts,bshd->bthd', w.astype(v.dtype), v[:, :kend]))
    return jnp.concatenate(outs, axis=1)


def causal_softmax_attention(q, k, v):
    S = q.shape[1]
    scale = MLA_QK_DIM ** -0.5
    outs = []
    for i in range(S // BLOCK_Q):
        q0 = i * BLOCK_Q
        kend = q0 + BLOCK_Q
        s = jnp.einsum('bthd,bshd->bhts', q[:, q0:kend], k[:, :kend]).astype(jnp.float32) * scale
        causal = jnp.arange(kend)[None, :] <= (q0 + jnp.arange(BLOCK_Q))[:, None]
        p = jax.nn.softmax(jnp.where(causal, s, -jnp.inf), axis=-1)
        outs.append(jnp.einsum('bhts,bshd->bthd', p.astype(v.dtype), v[:, :kend]))
    return jnp.concatenate(outs, axis=1)


def _fwd_setup_inputs(seed: int = 0) -> dict:
    key = jax.random.key(seed)
    ks = jax.random.split(key, 24)
    f32 = jnp.float32

    def w(k, shape, fan_in):
        return jax.random.normal(k, shape, f32) * (fan_in ** -0.5)

    def gain(k, n):
        return 1.0 + 0.01 * jax.random.normal(k, (DEPTH, n), f32)

    x = jax.random.normal(ks[0], (BATCH, SEQ, D_MODEL), f32)
    c = jax.random.normal(ks[1], (BATCH, D_MODEL), f32)
    offsets = jax.random.randint(ks[2], (BATCH, 1), 0, 512, dtype=jnp.int32)
    positions = offsets + jnp.arange(SEQ, dtype=jnp.int32)[None, :]
    return {
        "x": x,
        "c": c,
        "positions": positions,
        "w_ada": w(ks[3], (DEPTH, D_MODEL, N_MOD * D_MODEL), D_MODEL) * 0.5,
        "b_ada": 0.01 * jax.random.normal(ks[4], (DEPTH, N_MOD * D_MODEL), f32),
        "norm_attn": gain(ks[5], D_MODEL),
        "norm_ffn": gain(ks[6], D_MODEL),
        "w_in": w(ks[7], (DEPTH, D_MODEL, IN_PROJ_WIDTH), D_MODEL),
        "q_a_norm": gain(ks[8], MLA_Q_RANK),
        "w_q_up": w(ks[9], (DEPTH, MLA_Q_RANK, MLA_HEADS * MLA_QK_DIM), MLA_Q_RANK),
        "kv_a_norm": gain(ks[10], MLA_KV_RANK),
        "w_kv_up": w(ks[11], (DEPTH, MLA_KV_RANK, MLA_HEADS * (MLA_NOPE_DIM + MLA_V_DIM)), MLA_KV_RANK),
        "q_norm": gain(ks[12], MLA_QK_DIM),
        "k_nope_norm": gain(ks[13], MLA_NOPE_DIM),
        "k_rope_norm": gain(ks[14], MLA_ROPE_DIM),
        "out_norm_sb": gain(ks[15], SB_WIDTH),
        "out_norm_mla": gain(ks[16], MLA_WIDTH),
        "w_out": w(ks[17], (DEPTH, MIX_WIDTH, D_MODEL), MIX_WIDTH),
        "w_gate": w(ks[18], (DEPTH, D_MODEL, D_FF), D_MODEL),
        "w_up": w(ks[19], (DEPTH, D_MODEL, D_FF), D_MODEL),
        "w_down": w(ks[20], (DEPTH, D_FF, D_MODEL), D_FF),
    }


def _fwd_reference(x, c, positions, w_ada, b_ada, norm_attn, norm_ffn, w_in, q_a_norm, w_q_up,
              kv_a_norm, w_kv_up, q_norm, k_nope_norm, k_rope_norm, out_norm_sb, out_norm_mla,
              w_out, w_gate, w_up, w_down):
    B, S, _ = x.shape
    for l in range(DEPTH):
        mod = jax.nn.silu(c) @ w_ada[l] + b_ada[l]
        shift1, scale1, gate1, shift2, scale2, gate2 = [m[:, None, :] for m in jnp.split(mod, N_MOD, axis=-1)]

        h = rms_norm(x, norm_attn[l]) * (1.0 + scale1) + shift1
        proj = h @ w_in[l]
        cuts = np.cumsum([SB_WIDTH, SB_WIDTH, SB_WIDTH, MLA_Q_RANK, MLA_KV_RANK]).tolist()
        q_sb, k_sb, v_sb, c_q, c_kv, k_rope = jnp.split(proj, cuts, axis=-1)

        q_sb = q_sb.reshape(B, S, SB_HEADS, SB_HEAD_DIM)
        k_sb = k_sb.reshape(B, S, SB_HEADS, SB_HEAD_DIM)
        v_sb = v_sb.reshape(B, S, SB_HEADS, SB_HEAD_DIM)
        o_sb = stick_breaking_attention(q_sb, k_sb, v_sb).reshape(B, S, SB_WIDTH)

        q = (rms_norm(c_q, q_a_norm[l]) @ w_q_up[l]).reshape(B, S, MLA_HEADS, MLA_QK_DIM)
        kv = (rms_norm(c_kv, kv_a_norm[l]) @ w_kv_up[l]).reshape(B, S, MLA_HEADS, MLA_NOPE_DIM + MLA_V_DIM)
        k_nope, v_mla = kv[..., :MLA_NOPE_DIM], kv[..., MLA_NOPE_DIM:]
        q = rms_norm(q, q_norm[l])
        k_nope = rms_norm(k_nope, k_nope_norm[l])
        k_rope = rms_norm(k_rope, k_rope_norm[l])[:, :, None, :]
        q = jnp.concatenate([q[..., :MLA_NOPE_DIM], apply_rope(q[..., MLA_NOPE_DIM:], positions)], axis=-1)
        k_rope = jnp.broadcast_to(apply_rope(k_rope, positions), (B, S, MLA_HEADS, MLA_ROPE_DIM))
        k = jnp.concatenate([k_nope, k_rope], axis=-1)
        o_mla = causal_softmax_attention(q, k, v_mla).reshape(B, S, MLA_WIDTH)

        mixed = jnp.concatenate([rms_norm(o_sb, out_norm_sb[l]), rms_norm(o_mla, out_norm_mla[l])], axis=-1)
        x = x + gate1 * (mixed @ w_out[l])

        h = rms_norm(x, norm_ffn[l]) * (1.0 + scale2) + shift2
        ffn = (jax.nn.silu(h @ w_gate[l]) * (h @ w_up[l])) @ w_down[l]
        x = x + gate2 * ffn
    return x


import jax as _jax
import jax.numpy as _jnp

TWIN_FORMAT = 'train_step'
FWD_PARAMS = ['x', 'c', 'positions', 'w_ada', 'b_ada', 'norm_attn', 'norm_ffn', 'w_in', 'q_a_norm', 'w_q_up', 'kv_a_norm', 'w_kv_up', 'q_norm', 'k_nope_norm', 'k_rope_norm', 'out_norm_sb', 'out_norm_mla', 'w_out', 'w_gate', 'w_up', 'w_down']
TWIN_WEIGHTS = ['w_ada', 'b_ada', 'norm_attn', 'norm_ffn', 'w_in', 'q_a_norm', 'w_q_up', 'kv_a_norm', 'w_kv_up', 'q_norm', 'k_nope_norm', 'k_rope_norm', 'out_norm_sb', 'out_norm_mla', 'w_out', 'w_gate', 'w_up', 'w_down']
TWIN_DIFF_INPUT = 'x'
TWIN_INPUTS = ['x', 'c', 'positions', 'w_ada', 'b_ada', 'norm_attn', 'norm_ffn', 'w_in', 'q_a_norm', 'w_q_up', 'kv_a_norm', 'w_kv_up', 'q_norm', 'k_nope_norm', 'k_rope_norm', 'out_norm_sb', 'out_norm_mla', 'w_out', 'w_gate', 'w_up', 'w_down', 'loss_target', 'm_w_ada', 'm_b_ada', 'm_norm_attn', 'm_norm_ffn', 'm_w_in', 'm_q_a_norm', 'm_w_q_up', 'm_kv_a_norm', 'm_w_kv_up', 'm_q_norm', 'm_k_nope_norm', 'm_k_rope_norm', 'm_out_norm_sb', 'm_out_norm_mla', 'm_w_out', 'm_w_gate', 'm_w_up', 'm_w_down', 'v_w_ada', 'v_b_ada', 'v_norm_attn', 'v_norm_ffn', 'v_w_in', 'v_q_a_norm', 'v_w_q_up', 'v_kv_a_norm', 'v_w_kv_up', 'v_q_norm', 'v_k_nope_norm', 'v_k_rope_norm', 'v_out_norm_sb', 'v_out_norm_mla', 'v_w_out', 'v_w_gate', 'v_w_up', 'v_w_down']
TWIN_OUTPUTS = ['loss', 'grad_x', 'grad_w_ada', 'grad_b_ada', 'grad_norm_attn', 'grad_norm_ffn', 'grad_w_in', 'grad_q_a_norm', 'grad_w_q_up', 'grad_kv_a_norm', 'grad_w_kv_up', 'grad_q_norm', 'grad_k_nope_norm', 'grad_k_rope_norm', 'grad_out_norm_sb', 'grad_out_norm_mla', 'grad_w_out', 'grad_w_gate', 'grad_w_up', 'grad_w_down', 'delta_w_ada', 'delta_b_ada', 'delta_norm_attn', 'delta_norm_ffn', 'delta_w_in', 'delta_q_a_norm', 'delta_w_q_up', 'delta_kv_a_norm', 'delta_w_kv_up', 'delta_q_norm', 'delta_k_nope_norm', 'delta_k_rope_norm', 'delta_out_norm_sb', 'delta_out_norm_mla', 'delta_w_out', 'delta_w_gate', 'delta_w_up', 'delta_w_down', 'new_m_w_ada', 'new_m_b_ada', 'new_m_norm_attn', 'new_m_norm_ffn', 'new_m_w_in', 'new_m_q_a_norm', 'new_m_w_q_up', 'new_m_kv_a_norm', 'new_m_w_kv_up', 'new_m_q_norm', 'new_m_k_nope_norm', 'new_m_k_rope_norm', 'new_m_out_norm_sb', 'new_m_out_norm_mla', 'new_m_w_out', 'new_m_w_gate', 'new_m_w_up', 'new_m_w_down', 'new_v_w_ada', 'new_v_b_ada', 'new_v_norm_attn', 'new_v_norm_ffn', 'new_v_w_in', 'new_v_q_a_norm', 'new_v_w_q_up', 'new_v_kv_a_norm', 'new_v_w_kv_up', 'new_v_q_norm', 'new_v_k_nope_norm', 'new_v_k_rope_norm', 'new_v_out_norm_sb', 'new_v_out_norm_mla', 'new_v_w_out', 'new_v_w_gate', 'new_v_w_up', 'new_v_w_down']
TWIN_LEAF_KINDS = {'loss': 'loss', 'grad_x': 'grad_x', 'grad_w_ada': 'grad_w', 'grad_b_ada': 'grad_w', 'grad_norm_attn': 'grad_w', 'grad_norm_ffn': 'grad_w', 'grad_w_in': 'grad_w', 'grad_q_a_norm': 'grad_w', 'grad_w_q_up': 'grad_w', 'grad_kv_a_norm': 'grad_w', 'grad_w_kv_up': 'grad_w', 'grad_q_norm': 'grad_w', 'grad_k_nope_norm': 'grad_w', 'grad_k_rope_norm': 'grad_w', 'grad_out_norm_sb': 'grad_w', 'grad_out_norm_mla': 'grad_w', 'grad_w_out': 'grad_w', 'grad_w_gate': 'grad_w', 'grad_w_up': 'grad_w', 'grad_w_down': 'grad_w', 'delta_w_ada': 'delta_w', 'delta_b_ada': 'delta_w', 'delta_norm_attn': 'delta_w', 'delta_norm_ffn': 'delta_w', 'delta_w_in': 'delta_w', 'delta_q_a_norm': 'delta_w', 'delta_w_q_up': 'delta_w', 'delta_kv_a_norm': 'delta_w', 'delta_w_kv_up': 'delta_w', 'delta_q_norm': 'delta_w', 'delta_k_nope_norm': 'delta_w', 'delta_k_rope_norm': 'delta_w', 'delta_out_norm_sb': 'delta_w', 'delta_out_norm_mla': 'delta_w', 'delta_w_out': 'delta_w', 'delta_w_gate': 'delta_w', 'delta_w_up': 'delta_w', 'delta_w_down': 'delta_w', 'new_m_w_ada': 'new_m', 'new_m_b_ada': 'new_m', 'new_m_norm_attn': 'new_m', 'new_m_norm_ffn': 'new_m', 'new_m_w_in': 'new_m', 'new_m_q_a_norm': 'new_m', 'new_m_w_q_up': 'new_m', 'new_m_kv_a_norm': 'new_m', 'new_m_w_kv_up': 'new_m', 'new_m_q_norm': 'new_m', 'new_m_k_nope_norm': 'new_m', 'new_m_k_rope_norm': 'new_m', 'new_m_out_norm_sb': 'new_m', 'new_m_out_norm_mla': 'new_m', 'new_m_w_out': 'new_m', 'new_m_w_gate': 'new_m', 'new_m_w_up': 'new_m', 'new_m_w_down': 'new_m', 'new_v_w_ada': 'new_v', 'new_v_b_ada': 'new_v', 'new_v_norm_attn': 'new_v', 'new_v_norm_ffn': 'new_v', 'new_v_w_in': 'new_v', 'new_v_q_a_norm': 'new_v', 'new_v_w_q_up': 'new_v', 'new_v_kv_a_norm': 'new_v', 'new_v_w_kv_up': 'new_v', 'new_v_q_norm': 'new_v', 'new_v_k_nope_norm': 'new_v', 'new_v_k_rope_norm': 'new_v', 'new_v_out_norm_sb': 'new_v', 'new_v_out_norm_mla': 'new_v', 'new_v_w_out': 'new_v', 'new_v_w_gate': 'new_v', 'new_v_w_up': 'new_v', 'new_v_w_down': 'new_v'}


def _forward(args):
    return _fwd_reference(*[args[k] for k in FWD_PARAMS])


def _output_shape():
    out = _jax.eval_shape(lambda: _forward(_fwd_setup_inputs(0)))
    return out.shape, out.dtype

N_MICROBATCH = 1
ADAM_LR = 0.001
ADAM_B1 = 0.9
ADAM_B2 = 0.999
ADAM_EPS = 1e-08
ADAM_WD = 0.01
ADAM_STEP = 10
PER_EXAMPLE_BATCH_AXIS = {'x': 0, 'c': 0, 'positions': 0, 'loss_target': 0}
SHARED_INPUTS = []
_WEIGHT_DTYPES = {'w_ada': _jnp.float32, 'b_ada': _jnp.float32, 'norm_attn': _jnp.float32, 'norm_ffn': _jnp.float32, 'w_in': _jnp.float32, 'q_a_norm': _jnp.float32, 'w_q_up': _jnp.float32, 'kv_a_norm': _jnp.float32, 'w_kv_up': _jnp.float32, 'q_norm': _jnp.float32, 'k_nope_norm': _jnp.float32, 'k_rope_norm': _jnp.float32, 'out_norm_sb': _jnp.float32, 'out_norm_mla': _jnp.float32, 'w_out': _jnp.float32, 'w_gate': _jnp.float32, 'w_up': _jnp.float32, 'w_down': _jnp.float32}
MOMENT_SCALE = {'w_ada': 1.058136e+00, 'b_ada': 1.893490e+00, 'norm_attn': 8.482564e-02, 'norm_ffn': 1.716741e+00, 'w_in': 4.685141e-01, 'q_a_norm': 5.306866e-02, 'w_q_up': 3.871677e-02, 'kv_a_norm': 1.223147e+00, 'w_kv_up': 6.184623e-01, 'q_norm': 7.669708e-02, 'k_nope_norm': 9.456979e-02, 'k_rope_norm': 4.877102e-02, 'out_norm_sb': 1.623432e+00, 'out_norm_mla': 1.951233e+00, 'w_out': 6.794536e-01, 'w_gate': 5.504884e-02, 'w_up': 4.777046e-02, 'w_down': 7.329093e-02}


def _to_microbatches(a, axis):
    t = _jnp.moveaxis(a, axis, 0)
    t = t.reshape((N_MICROBATCH, t.shape[0] // N_MICROBATCH) + t.shape[1:])
    return _jnp.moveaxis(t, 1, axis + 1)


def setup_inputs(seed: int = 0) -> dict:
    inp = _fwd_setup_inputs(seed)
    key = _jax.random.fold_in(_jax.random.key(seed), 7919)
    shape, _ = _output_shape()
    out = dict(inp)
    out["loss_target"] = _jax.random.normal(_jax.random.fold_in(key, 0), shape, _jnp.float32)
    for i, name in enumerate(TWIN_WEIGHTS):
        w = inp[name].astype(_jnp.float32)
        if MOMENT_SCALE is None:
            s = _jnp.sqrt(_jnp.mean(_jnp.square(w)) + 1e-30)
        else:
            s = MOMENT_SCALE[name]
        km, kv = _jax.random.split(_jax.random.fold_in(key, i + 1))
        out[name] = w
        out["m_" + name] = s * _jax.random.normal(km, w.shape, _jnp.float32)
        out["v_" + name] = (s * s) * _jax.random.uniform(kv, w.shape, _jnp.float32, 0.5, 1.5)
    if N_MICROBATCH > 1:
        for name, axis in PER_EXAMPLE_BATCH_AXIS.items():
            out[name] = _to_microbatches(out[name], axis)
    return {'x': out['x'], 'c': out['c'], 'positions': out['positions'], 'w_ada': out['w_ada'], 'b_ada': out['b_ada'], 'norm_attn': out['norm_attn'], 'norm_ffn': out['norm_ffn'], 'w_in': out['w_in'], 'q_a_norm': out['q_a_norm'], 'w_q_up': out['w_q_up'], 'kv_a_norm': out['kv_a_norm'], 'w_kv_up': out['w_kv_up'], 'q_norm': out['q_norm'], 'k_nope_norm': out['k_nope_norm'], 'k_rope_norm': out['k_rope_norm'], 'out_norm_sb': out['out_norm_sb'], 'out_norm_mla': out['out_norm_mla'], 'w_out': out['w_out'], 'w_gate': out['w_gate'], 'w_up': out['w_up'], 'w_down': out['w_down'], 'loss_target': out['loss_target'], 'm_w_ada': out['m_w_ada'], 'm_b_ada': out['m_b_ada'], 'm_norm_attn': out['m_norm_attn'], 'm_norm_ffn': out['m_norm_ffn'], 'm_w_in': out['m_w_in'], 'm_q_a_norm': out['m_q_a_norm'], 'm_w_q_up': out['m_w_q_up'], 'm_kv_a_norm': out['m_kv_a_norm'], 'm_w_kv_up': out['m_w_kv_up'], 'm_q_norm': out['m_q_norm'], 'm_k_nope_norm': out['m_k_nope_norm'], 'm_k_rope_norm': out['m_k_rope_norm'], 'm_out_norm_sb': out['m_out_norm_sb'], 'm_out_norm_mla': out['m_out_norm_mla'], 'm_w_out': out['m_w_out'], 'm_w_gate': out['m_w_gate'], 'm_w_up': out['m_w_up'], 'm_w_down': out['m_w_down'], 'v_w_ada': out['v_w_ada'], 'v_b_ada': out['v_b_ada'], 'v_norm_attn': out['v_norm_attn'], 'v_norm_ffn': out['v_norm_ffn'], 'v_w_in': out['v_w_in'], 'v_q_a_norm': out['v_q_a_norm'], 'v_w_q_up': out['v_w_q_up'], 'v_kv_a_norm': out['v_kv_a_norm'], 'v_w_kv_up': out['v_w_kv_up'], 'v_q_norm': out['v_q_norm'], 'v_k_nope_norm': out['v_k_nope_norm'], 'v_k_rope_norm': out['v_k_rope_norm'], 'v_out_norm_sb': out['v_out_norm_sb'], 'v_out_norm_mla': out['v_out_norm_mla'], 'v_w_out': out['v_w_out'], 'v_w_gate': out['v_w_gate'], 'v_w_up': out['v_w_up'], 'v_w_down': out['v_w_down']}


def _loss(weights, diff, rest, loss_target):
    with _jax.named_scope("forward"):
        args = {**rest, TWIN_DIFF_INPUT: diff, **{k: w.astype(_WEIGHT_DTYPES[k]) for k, w in weights.items()}}
        y = _forward(args)
    with _jax.named_scope("loss_head"):
        err = _jnp.square(y.astype(_jnp.float32) - loss_target)
        return 0.5 * _jnp.sum(_jnp.mean(err, axis=-1)) if err.ndim else 0.5 * err


def _adamw(w, g, m, v):
    m = ADAM_B1 * m + (1.0 - ADAM_B1) * g
    v = ADAM_B2 * v + (1.0 - ADAM_B2) * _jnp.square(g)
    m_hat = m / (1.0 - ADAM_B1 ** ADAM_STEP)
    v_hat = v / (1.0 - ADAM_B2 ** ADAM_STEP)
    delta = -ADAM_LR * (m_hat / (_jnp.sqrt(v_hat) + ADAM_EPS) + ADAM_WD * w)
    return delta, m, v


def reference(x, c, positions, w_ada, b_ada, norm_attn, norm_ffn, w_in, q_a_norm, w_q_up, kv_a_norm, w_kv_up, q_norm, k_nope_norm, k_rope_norm, out_norm_sb, out_norm_mla, w_out, w_gate, w_up, w_down, loss_target, m_w_ada, m_b_ada, m_norm_attn, m_norm_ffn, m_w_in, m_q_a_norm, m_w_q_up, m_kv_a_norm, m_w_kv_up, m_q_norm, m_k_nope_norm, m_k_rope_norm, m_out_norm_sb, m_out_norm_mla, m_w_out, m_w_gate, m_w_up, m_w_down, v_w_ada, v_b_ada, v_norm_attn, v_norm_ffn, v_w_in, v_q_a_norm, v_w_q_up, v_kv_a_norm, v_w_kv_up, v_q_norm, v_k_nope_norm, v_k_rope_norm, v_out_norm_sb, v_out_norm_mla, v_w_out, v_w_gate, v_w_up, v_w_down):
    given = dict(x=x, c=c, positions=positions, w_ada=w_ada, b_ada=b_ada, norm_attn=norm_attn, norm_ffn=norm_ffn, w_in=w_in, q_a_norm=q_a_norm, w_q_up=w_q_up, kv_a_norm=kv_a_norm, w_kv_up=w_kv_up, q_norm=q_norm, k_nope_norm=k_nope_norm, k_rope_norm=k_rope_norm, out_norm_sb=out_norm_sb, out_norm_mla=out_norm_mla, w_out=w_out, w_gate=w_gate, w_up=w_up, w_down=w_down, loss_target=loss_target, m_w_ada=m_w_ada, m_b_ada=m_b_ada, m_norm_attn=m_norm_attn, m_norm_ffn=m_norm_ffn, m_w_in=m_w_in, m_q_a_norm=m_q_a_norm, m_w_q_up=m_w_q_up, m_kv_a_norm=m_kv_a_norm, m_w_kv_up=m_w_kv_up, m_q_norm=m_q_norm, m_k_nope_norm=m_k_nope_norm, m_k_rope_norm=m_k_rope_norm, m_out_norm_sb=m_out_norm_sb, m_out_norm_mla=m_out_norm_mla, m_w_out=m_w_out, m_w_gate=m_w_gate, m_w_up=m_w_up, m_w_down=m_w_down, v_w_ada=v_w_ada, v_b_ada=v_b_ada, v_norm_attn=v_norm_attn, v_norm_ffn=v_norm_ffn, v_w_in=v_w_in, v_q_a_norm=v_q_a_norm, v_w_q_up=v_w_q_up, v_kv_a_norm=v_kv_a_norm, v_w_kv_up=v_w_kv_up, v_q_norm=v_q_norm, v_k_nope_norm=v_k_nope_norm, v_k_rope_norm=v_k_rope_norm, v_out_norm_sb=v_out_norm_sb, v_out_norm_mla=v_out_norm_mla, v_w_out=v_w_out, v_w_gate=v_w_gate, v_w_up=v_w_up, v_w_down=v_w_down)
    weights = {n: given[n] for n in TWIN_WEIGHTS}
    shared = {n: given[n] for n in SHARED_INPUTS}
    per_example = {n: given[n] for n in ['x', 'c', 'positions']}
    grad_fn = _jax.value_and_grad(_loss, argnums=(0, 1))

    def one_microbatch(ex, loss_target):
        ex = dict(ex)
        diff = ex.pop(TWIN_DIFF_INPUT)
        return grad_fn(weights, diff, {**shared, **ex}, loss_target)

    if N_MICROBATCH == 1:
        loss, (grad_w, grad_x) = one_microbatch(per_example, given["loss_target"])
    else:
        def body(carry, xs):
            loss_sum, grad_sum = carry
            l_k, (gw_k, gx_k) = one_microbatch(xs[0], xs[1])
            with _jax.named_scope("update"):
                return (loss_sum + l_k, _jax.tree.map(_jnp.add, grad_sum, gw_k)), gx_k

        init = (_jnp.zeros((), _jnp.float32), _jax.tree.map(_jnp.zeros_like, weights))
        (loss, grad_w), grad_x = _jax.lax.scan(body, init, (per_example, given["loss_target"]))
    with _jax.named_scope("update"):
        delta_w, new_m, new_v = {}, {}, {}
        for n in TWIN_WEIGHTS:
            delta_w[n], new_m[n], new_v[n] = _adamw(weights[n], grad_w[n], given["m_" + n], given["v_" + n])
    return (loss, grad_x, *[grad_w[n] for n in TWIN_WEIGHTS], *[delta_w[n] for n in TWIN_WEIGHTS],
            *[new_m[n] for n in TWIN_WEIGHTS], *[new_v[n] for n in TWIN_WEIGHTS])
```

```python
import functools
import math

import numpy as np
import jax
import jax.numpy as jnp
from jax import lax
from jax.experimental import pallas as pl
from jax.experimental.pallas import tpu as pltpu

F32 = jnp.float32
BF16 = jnp.bfloat16
MESH = pl.DeviceIdType.MESH
ANY = pl.BlockSpec(memory_space=pl.ANY)

D_MODEL = 1024
SB_HEADS = 8
SB_HEAD_DIM = 64
SB_WIDTH = 512
MLA_HEADS = 4
MLA_NOPE = 128
MLA_ROPE = 64
MLA_QK = 192
MLA_V = 128
MLA_Q_RANK = 384
MLA_KV_RANK = 256
D_FF = 2816
N_MOD = 6
ROPE_THETA = 10000.0
EPS = 1e-6
LANES = 128

ADAM_LR = 0.001
ADAM_B1 = 0.9
ADAM_B2 = 0.999
ADAM_EPS = 1e-08
ADAM_WD = 0.01
ADAM_STEP = 10

N_CHIPS = 4
N_DEV = 8
ROW_TILE = 256
ATT_BLK = 256
MM_VMEM_LIMIT = 48 * 1024 * 1024
PACK_W = 1024
PACK_ALIGN = 2 * 16 * PACK_W


def _mm(a, b, mode, name, tm, tn):
    if mode == "nn":
        (m, k), n = a.shape, b.shape[1]
        a_spec = pl.BlockSpec((tm, k), lambda j, i: (i, 0))
        b_spec = pl.BlockSpec((k, tn), lambda j, i: (0, j))
        dims = (((1,), (0,)), ((), ()))
    elif mode == "nt":
        (m, k), n = a.shape, b.shape[0]
        a_spec = pl.BlockSpec((tm, k), lambda j, i: (i, 0))
        b_spec = pl.BlockSpec((tn, k), lambda j, i: (j, 0))
        dims = (((1,), (1,)), ((), ()))
    else:
        (k, m), n = a.shape, b.shape[1]
        a_spec = pl.BlockSpec((k, tm), lambda j, i: (0, i))
        b_spec = pl.BlockSpec((k, tn), lambda j, i: (0, j))
        dims = (((0,), (0,)), ((), ()))
    assert m % tm == 0 and n % tn == 0, (name, m, n, tm, tn)

    def body(a_ref, b_ref, o_ref):
        o_ref[...] = lax.dot_general(a_ref[...].astype(BF16), b_ref[...].astype(BF16), dims,
                                     preferred_element_type=F32)

    return pl.pallas_call(
        body, name=name, grid=(n // tn, m // tm),
        in_specs=[a_spec, b_spec],
        out_specs=pl.BlockSpec((tm, tn), lambda j, i: (i, j)),
        out_shape=jax.ShapeDtypeStruct((m, n), F32),
        compiler_params=pltpu.CompilerParams(dimension_semantics=("arbitrary", "arbitrary"),
                                             vmem_limit_bytes=MM_VMEM_LIMIT),
    )(a, b)


def _make_linear(name, tk_w, tn_w):
    @jax.custom_vjp
    def op(a, w):
        return _mm(a, w.astype(BF16), "nn", name + "_fwd", ROW_TILE, w.shape[1])

    def fwd(a, w):
        return op(a, w), (a, w)

    def bwd(res, dy):
        a, w = res
        da = _mm(dy, w.astype(BF16), "nt", name + "_dx", ROW_TILE, w.shape[0])
        dw = _mm(a, dy, "tn", name + "_dw", tk_w, tn_w)
        return da, dw

    op.defvjp(fwd, bwd)
    return op


def _row_spec(arr, tb):
    return pl.BlockSpec((tb, arr.shape[1]), lambda i: (i, 0))


def _full_spec(arr):
    return pl.BlockSpec(arr.shape, lambda i: (0, 0))


def _make_rowwise(name, f, n_rows, n_params, out_cols, diff_rows):
    n_out = len(out_cols)

    def call_fwd(rows, params):
        t = rows[0].shape[0]

        def body(*refs):
            ins = [r[...] for r in refs[:n_rows + n_params]]
            outs = f(*ins)
            for o_ref, o in zip(refs[n_rows + n_params:], outs):
                o_ref[...] = o

        return pl.pallas_call(
            body, name=name + "_fwd", grid=(t // ROW_TILE,),
            in_specs=[_row_spec(a, ROW_TILE) for a in rows] + [_full_spec(p) for p in params],
            out_specs=[pl.BlockSpec((ROW_TILE, n), lambda i: (i, 0)) for n in out_cols],
            out_shape=[jax.ShapeDtypeStruct((t, n), F32) for n in out_cols],
            compiler_params=pltpu.CompilerParams(dimension_semantics=("arbitrary",)),
        )(*rows, *params)

    def call_bwd(rows, params, cts):
        t = rows[0].shape[0]
        d_rows = [a for a, d in zip(rows, diff_rows) if d]
        n_in = n_rows + n_params + n_out

        def body(*refs):
            ins = [r[...] for r in refs[:n_rows + n_params]]
            ct = tuple(r[...] for r in refs[n_rows + n_params:n_in])
            _, vjp = jax.vjp(f, *ins)
            grads = vjp(ct)
            out_refs = refs[n_in:]
            g_rows = [g for g, d in zip(grads[:n_rows], diff_rows) if d]
            for o_ref, g in zip(out_refs[:len(g_rows)], g_rows):
                o_ref[...] = g
            p_refs = out_refs[len(g_rows):]

            if p_refs:
                @pl.when(pl.program_id(0) == 0)
                def _():
                    for p_ref in p_refs:
                        p_ref[...] = jnp.zeros_like(p_ref)

                for p_ref, g in zip(p_refs, grads[n_rows:]):
                    p_ref[...] += g

        return pl.pallas_call(
            body, name=name + "_bwd", grid=(t // ROW_TILE,),
            in_specs=[_row_spec(a, ROW_TILE) for a in rows] + [_full_spec(p) for p in params]
            + [_row_spec(c, ROW_TILE) for c in cts],
            out_specs=[_row_spec(a, ROW_TILE) for a in d_rows] + [_full_spec(p) for p in params],
            out_shape=[jax.ShapeDtypeStruct(a.shape, F32) for a in d_rows]
            + [jax.ShapeDtypeStruct(p.shape, F32) for p in params],
            compiler_params=pltpu.CompilerParams(dimension_semantics=("arbitrary",)),
        )(*rows, *params, *cts)

    @jax.custom_vjp
    def op(*args):
        return tuple(call_fwd(args[:n_rows], args[n_rows:]))

    def fwd(*args):
        return op(*args), args

    def bwd(args, cts):
        rows, params = args[:n_rows], args[n_rows:]
        outs = call_bwd(rows, params, cts)
        it = iter(outs)
        g_rows = [next(it) if d else jnp.zeros_like(a) for a, d in zip(rows, diff_rows)]
        return tuple(g_rows) + tuple(it)

    op.defvjp(fwd, bwd)
    return op


def _rms(x, g, n):
    return x * lax.rsqrt(jnp.sum(x * x, axis=-1, keepdims=True) * (1.0 / n) + EPS) * g


def _f_pre_attn(x, g, scale, shift):
    return (_rms(x, g, D_MODEL) * (1.0 + scale) + shift,)


def _f_mla_a(cq, ckv, gq, gkv):
    return _rms(cq, gq, MLA_Q_RANK), _rms(ckv, gkv, MLA_KV_RANK)


@jax.custom_vjp
def _split_lanes(x):
    return tuple(x[:, i * LANES:(i + 1) * LANES] for i in range(x.shape[1] // LANES))


def _split_lanes_fwd(x):
    return _split_lanes(x), None


def _split_lanes_bwd(_, cts):
    return (jnp.concatenate(cts, axis=1),)


_split_lanes.defvjp(_split_lanes_fwd, _split_lanes_bwd)


def _f_mla_b(qall, kn_all, kr, kr_sw, cos, sin, gqn, gqr, gqr_sw, gkn, gkr, gkr_sw):
    q = _split_lanes(qall)
    kn = _split_lanes(kn_all)
    qn_o, qr_o, kn_o = [], [], []
    for h in range(MLA_HEADS):
        qn, qr, qs = q[h], q[MLA_HEADS + h], q[2 * MLA_HEADS + h]
        ss = jnp.sum(qn * qn, axis=-1, keepdims=True) + jnp.sum(qr * qr, axis=-1, keepdims=True)
        rs = lax.rsqrt(ss * (1.0 / MLA_QK) + EPS)
        qn_o.append(qn * rs * gqn)
        qr_o.append((qr * rs * gqr) * cos + (qs * rs * gqr_sw) * sin)
        kn_o.append(_rms(kn[h], gkn, MLA_NOPE))
    rs = lax.rsqrt(jnp.sum(kr * kr, axis=-1, keepdims=True) * (1.0 / MLA_ROPE) + EPS)
    kr_o = (kr * rs * gkr) * cos + (kr_sw * rs * gkr_sw) * sin
    return (jnp.concatenate(qn_o, axis=1), jnp.concatenate(qr_o, axis=1), jnp.concatenate(kn_o, axis=1), kr_o)


def _f_post_attn(o_sb, o_mla, g_sb, g_mla):
    return (jnp.concatenate([_rms(o_sb, g_sb, SB_WIDTH), _rms(o_mla, g_mla, SB_WIDTH)], axis=1),)


def _f_pre_ffn(x, attn, gate, g, scale, shift):
    x2 = x + gate * attn
    return x2, _rms(x2, g, D_MODEL) * (1.0 + scale) + shift


def _f_swiglu(gt, up):
    return (gt / (1.0 + jnp.exp(-gt)) * up,)


def _f_loss(x2, ffn, target, gate):
    err = x2 + gate * ffn - target
    return (jnp.sum(err * err, axis=-1, keepdims=True) * (1.0 / D_MODEL),)


def _rope_tables(pos_col, freqs, sign):
    t = pos_col.shape[0]

    def body(p_ref, f_ref, s_ref, cos_ref, sin_ref):
        ang = p_ref[...].astype(F32) * f_ref[...]
        live = jnp.abs(s_ref[...])
        cos_ref[...] = jnp.cos(ang) * live
        sin_ref[...] = jnp.sin(ang) * s_ref[...]

    return pl.pallas_call(
        body, name="rope_tables", grid=(t // ROW_TILE,),
        in_specs=[pl.BlockSpec((ROW_TILE, 1), lambda i: (i, 0)), _full_spec(freqs), _full_spec(sign)],
        out_specs=[pl.BlockSpec((ROW_TILE, LANES), lambda i: (i, 0))] * 2,
        out_shape=[jax.ShapeDtypeStruct((t, LANES), F32)] * 2,
    )(pos_col, freqs, sign)


def _hi_lo_dot(x, tri):
    hi = x.astype(BF16)
    lo = (x - hi.astype(F32)).astype(BF16)
    return (jnp.dot(hi, tri, preferred_element_type=F32) + jnp.dot(lo, tri, preferred_element_type=F32))


def _tri(cmp):
    r = lax.broadcasted_iota(jnp.int32, (ATT_BLK, ATT_BLK), 0)
    c = lax.broadcasted_iota(jnp.int32, (ATT_BLK, ATT_BLK), 1)
    return cmp(r, c).astype(BF16)


def _nt(a, b):
    return lax.dot_general(a, b, (((1,), (1,)), ((), ())), preferred_element_type=F32)


def _tn(a, b):
    return lax.dot_general(a, b, (((0,), (0,)), ((), ())), preferred_element_type=F32)


def _sb_logs(z, valid):
    e = jnp.exp(-jnp.abs(z))
    sp = jnp.log(1.0 + e)
    lb = jnp.minimum(z, 0.0) - sp
    l1m = jnp.where(valid, jnp.minimum(-z, 0.0) - sp, 0.0)
    return lb, l1m, e


def _sb_fwd(q, k, v):
    t = q.shape[0]
    nq = t // ATT_BLK
    scale = SB_HEAD_DIM ** -0.5

    def body(q_ref, k_ref, v_ref, o_ref, tot_ref):
        qi = pl.program_id(1)
        lane = lax.broadcasted_iota(jnp.int32, (ATT_BLK, LANES), 1)
        row = lax.broadcasted_iota(jnp.int32, (ATT_BLK, ATT_BLK), 0)
        col = lax.broadcasted_iota(jnp.int32, (ATT_BLK, ATT_BLK), 1)
        tri = _tri(lambda r, c: r > c)
        qv = q_ref[...] * scale
        acc = jnp.zeros((ATT_BLK, LANES), F32)
        for hh in range(2):
            mine = (lane // SB_HEAD_DIM) == hh
            qm = jnp.where(mine, qv, 0.0).astype(BF16)

            def step(j, carry, qm=qm, mine=mine):
                acc_h, run = carry
                kb = qi - j
                off = pl.multiple_of(kb * ATT_BLK, ATT_BLK)
                kk = k_ref[pl.ds(off, ATT_BLK), :].astype(BF16)
                vv = jnp.where(mine, v_ref[pl.ds(off, ATT_BLK), :], 0.0).astype(BF16)
                z = _nt(qm, kk)
                valid = (kb * ATT_BLK + col) < (qi * ATT_BLK + row)
                lb, l1m, _ = _sb_logs(z, valid)
                after = _hi_lo_dot(l1m, tri) + run
                w = jnp.where(valid, jnp.exp(lb + after), 0.0)
                acc_h = acc_h + jnp.dot(w.astype(BF16), vv, preferred_element_type=F32)
                return acc_h, run + jnp.sum(l1m, axis=-1, keepdims=True)

            acc, run = lax.fori_loop(0, qi + 1, step, (acc, jnp.zeros((ATT_BLK, 1), F32)))
            tot_ref[:, hh * LANES:(hh + 1) * LANES] = jnp.broadcast_to(run, (ATT_BLK, LANES))
        o_ref[...] = acc

    return pl.pallas_call(
        body, name="sb_attn_fwd", grid=(SB_HEADS // 2, nq),
        in_specs=[pl.BlockSpec((ATT_BLK, LANES), lambda p, i: (i, p)),
                  pl.BlockSpec((t, LANES), lambda p, i: (0, p)),
                  pl.BlockSpec((t, LANES), lambda p, i: (0, p))],
        out_specs=[pl.BlockSpec((ATT_BLK, LANES), lambda p, i: (i, p)),
                   pl.BlockSpec((ATT_BLK, 2 * LANES), lambda p, i: (i, p))],
        out_shape=[jax.ShapeDtypeStruct((t, SB_WIDTH), F32), jax.ShapeDtypeStruct((t, SB_HEADS * LANES), F32)],
        compiler_params=pltpu.CompilerParams(dimension_semantics=("arbitrary", "arbitrary")),
    )(q, k, v)


def _sb_bwd(q, k, v, tot, do):
    t = q.shape[0]
    nq = t // ATT_BLK
    scale = SB_HEAD_DIM ** -0.5

    def body(q_ref, k_ref, v_ref, tot_ref, do_ref, dq_ref, dk_ref, dv_ref):
        qi = pl.program_id(1)

        @pl.when(qi == 0)
        def _():
            dk_ref[...] = jnp.zeros_like(dk_ref)
            dv_ref[...] = jnp.zeros_like(dv_ref)

        lane = lax.broadcasted_iota(jnp.int32, (ATT_BLK, LANES), 1)
        row = lax.broadcasted_iota(jnp.int32, (ATT_BLK, ATT_BLK), 0)
        col = lax.broadcasted_iota(jnp.int32, (ATT_BLK, ATT_BLK), 1)
        tri_incl = _tri(lambda r, c: r <= c)
        tri_lt = _tri(lambda r, c: r < c)
        qv = q_ref[...] * scale
        dov = do_ref[...]
        dq = jnp.zeros((ATT_BLK, LANES), F32)
        for hh in range(2):
            mine = (lane // SB_HEAD_DIM) == hh
            qm = jnp.where(mine, qv, 0.0).astype(BF16)
            dom = jnp.where(mine, dov, 0.0).astype(BF16)
            tot_h = tot_ref[:, hh * LANES:hh * LANES + 1]

            def step(kb, carry, qm=qm, dom=dom, mine=mine, tot_h=tot_h):
                dq_h, pre, c_de = carry
                off = pl.multiple_of(kb * ATT_BLK, ATT_BLK)
                kk = jnp.where(mine, k_ref[pl.ds(off, ATT_BLK), :], 0.0).astype(BF16)
                vv = v_ref[pl.ds(off, ATT_BLK), :].astype(BF16)
                z = _nt(qm, kk)
                valid = (kb * ATT_BLK + col) < (qi * ATT_BLK + row)
                lb, l1m, e = _sb_logs(z, valid)
                after = tot_h - (_hi_lo_dot(l1m, tri_incl) + pre)
                w = jnp.where(valid, jnp.exp(lb + after), 0.0)
                d_e = w * _nt(dom, vv)
                dl1m = _hi_lo_dot(d_e, tri_lt) + c_de
                inv = 1.0 / (1.0 + e)
                sig = jnp.where(z >= 0.0, inv, e * inv)
                dz = jnp.where(valid, d_e * (1.0 - sig) - dl1m * sig, 0.0).astype(BF16)
                dq_h = dq_h + jnp.dot(dz, kk, preferred_element_type=F32)
                dk_ref[pl.ds(off, ATT_BLK), :] += _tn(dz, qm)
                dv_ref[pl.ds(off, ATT_BLK), :] += _tn(w.astype(BF16), dom)
                return (dq_h, pre + jnp.sum(l1m, axis=-1, keepdims=True),
                        c_de + jnp.sum(d_e, axis=-1, keepdims=True))

            zero = jnp.zeros((ATT_BLK, 1), F32)
            dq, _, _ = lax.fori_loop(0, qi + 1, step, (dq, zero, zero))
        dq_ref[...] = dq * scale

    return pl.pallas_call(
        body, name="sb_attn_bwd", grid=(SB_HEADS // 2, nq),
        in_specs=[pl.BlockSpec((ATT_BLK, LANES), lambda p, i: (i, p)),
                  pl.BlockSpec((t, LANES), lambda p, i: (0, p)),
                  pl.BlockSpec((t, LANES), lambda p, i: (0, p)),
                  pl.BlockSpec((ATT_BLK, 2 * LANES), lambda p, i: (i, p)),
                  pl.BlockSpec((ATT_BLK, LANES), lambda p, i: (i, p))],
        out_specs=[pl.BlockSpec((ATT_BLK, LANES), lambda p, i: (i, p)),
                   pl.BlockSpec((t, LANES), lambda p, i: (0, p)),
                   pl.BlockSpec((t, LANES), lambda p, i: (0, p))],
        out_shape=[jax.ShapeDtypeStruct((t, SB_WIDTH), F32)] * 3,
        compiler_params=pltpu.CompilerParams(dimension_semantics=("arbitrary", "arbitrary")),
    )(q, k, v, tot, do)


@jax.custom_vjp
def _sb_attention(q, k, v):
    return _sb_fwd(q, k, v)[0]


def _sb_attention_fwd(q, k, v):
    o, tot = _sb_fwd(q, k, v)
    return o, (q, k, v, tot)


def _sb_attention_bwd(res, do):
    return tuple(_sb_bwd(*res, do))


_sb_attention.defvjp(_sb_attention_fwd, _sb_attention_bwd)


def _mla_fwd(qn, qr, kn, kr, v):
    t = qn.shape[0]
    nq = t // ATT_BLK
    scale = MLA_QK ** -0.5

    def body(qn_ref, qr_ref, kn_ref, kr_ref, v_ref, o_ref, lse_ref):
        qi = pl.program_id(1)
        row = lax.broadcasted_iota(jnp.int32, (ATT_BLK, ATT_BLK), 0)
        col = lax.broadcasted_iota(jnp.int32, (ATT_BLK, ATT_BLK), 1)
        qnb = qn_ref[...].astype(BF16)
        qrb = qr_ref[...].astype(BF16)

        def step(kb, carry):
            acc, m, l = carry
            off = pl.multiple_of(kb * ATT_BLK, ATT_BLK)
            s = (_nt(qnb, kn_ref[pl.ds(off, ATT_BLK), :].astype(BF16))
                 + _nt(qrb, kr_ref[pl.ds(off, ATT_BLK), :].astype(BF16))) * scale
            s = jnp.where((kb * ATT_BLK + col) <= (qi * ATT_BLK + row), s, -jnp.inf)
            m_new = jnp.maximum(m, jnp.max(s, axis=-1, keepdims=True))
            p = jnp.exp(s - m_new)
            alpha = jnp.exp(m - m_new)
            acc = acc * alpha + jnp.dot(p.astype(BF16), v_ref[pl.ds(off, ATT_BLK), :].astype(BF16),
                                        preferred_element_type=F32)
            return acc, m_new, l * alpha + jnp.sum(p, axis=-1, keepdims=True)

        acc, m, l = lax.fori_loop(0, qi + 1, step, (jnp.zeros((ATT_BLK, LANES), F32),
                                                   jnp.full((ATT_BLK, 1), -jnp.inf, F32),
                                                   jnp.zeros((ATT_BLK, 1), F32)))
        o_ref[...] = acc / l
        lse_ref[...] = jnp.broadcast_to(m + jnp.log(l), (ATT_BLK, LANES))

    blk = pl.BlockSpec((ATT_BLK, LANES), lambda h, i: (i, h))
    full = pl.BlockSpec((t, LANES), lambda h, i: (0, h))
    return pl.pallas_call(
        body, name="mla_attn_fwd", grid=(MLA_HEADS, nq),
        in_specs=[blk, blk, full, pl.BlockSpec((t, LANES), lambda h, i: (0, 0)), full],
        out_specs=[blk, blk],
        out_shape=[jax.ShapeDtypeStruct((t, MLA_HEADS * LANES), F32)] * 2,
        compiler_params=pltpu.CompilerParams(dimension_semantics=("arbitrary", "arbitrary")),
    )(qn, qr, kn, kr, v)


def _mla_bwd(qn, qr, kn, kr, v, o, lse, do):
    t = qn.shape[0]
    nq = t // ATT_BLK
    scale = MLA_QK ** -0.5

    def body(qn_ref, qr_ref, kn_ref, kr_ref, v_ref, o_ref, lse_ref, do_ref,
             dqn_ref, dqr_ref, dkn_ref, dkr_ref, dv_ref):
        h = pl.program_id(0)
        qi = pl.program_id(1)

        @pl.when(qi == 0)
        def _():
            dkn_ref[...] = jnp.zeros_like(dkn_ref)
            dv_ref[...] = jnp.zeros_like(dv_ref)

        @pl.when((qi == 0) & (h == 0))
        def _():
            dkr_ref[...] = jnp.zeros_like(dkr_ref)

        row = lax.broadcasted_iota(jnp.int32, (ATT_BLK, ATT_BLK), 0)
        col = lax.broadcasted_iota(jnp.int32, (ATT_BLK, ATT_BLK), 1)
        qnb = qn_ref[...].astype(BF16)
        qrb = qr_ref[...].astype(BF16)
        dov = do_ref[...]
        dob = dov.astype(BF16)
        delta = jnp.sum(dov * o_ref[...], axis=-1, keepdims=True)
        lse_v = lse_ref[:, 0:1]

        def step(kb, carry):
            dqn, dqr = carry
            off = pl.multiple_of(kb * ATT_BLK, ATT_BLK)
            knb = kn_ref[pl.ds(off, ATT_BLK), :].astype(BF16)
            krb = kr_ref[pl.ds(off, ATT_BLK), :].astype(BF16)
            vb = v_ref[pl.ds(off, ATT_BLK), :].astype(BF16)
            s = (_nt(qnb, knb) + _nt(qrb, krb)) * scale
            p = jnp.where((kb * ATT_BLK + col) <= (qi * ATT_BLK + row), jnp.exp(s - lse_v), 0.0)
            dv_ref[pl.ds(off, ATT_BLK), :] += _tn(p.astype(BF16), dob)
            ds = (p * (_nt(dob, vb) - delta) * scale).astype(BF16)
            dkn_ref[pl.ds(off, ATT_BLK), :] += _tn(ds, qnb)
            dkr_ref[pl.ds(off, ATT_BLK), :] += _tn(ds, qrb)
            return (dqn + jnp.dot(ds, knb, preferred_element_type=F32),
                    dqr + jnp.dot(ds, krb, preferred_element_type=F32))

        zero = jnp.zeros((ATT_BLK, LANES), F32)
        dqn, dqr = lax.fori_loop(0, qi + 1, step, (zero, zero))
        dqn_ref[...] = dqn
        dqr_ref[...] = dqr

    blk = pl.BlockSpec((ATT_BLK, LANES), lambda h, i: (i, h))
    full = pl.BlockSpec((t, LANES), lambda h, i: (0, h))
    shared = pl.BlockSpec((t, LANES), lambda h, i: (0, 0))
    wide = jax.ShapeDtypeStruct((t, MLA_HEADS * LANES), F32)
    return pl.pallas_call(
        body, name="mla_attn_bwd", grid=(MLA_HEADS, nq),
        in_specs=[blk, blk, full, shared, full, blk, blk, blk],
        out_specs=[blk, blk, full, shared, full],
        out_shape=[wide, wide, wide, jax.ShapeDtypeStruct((t, LANES), F32), wide],
        compiler_params=pltpu.CompilerParams(dimension_semantics=("arbitrary", "arbitrary")),
    )(qn, qr, kn, kr, v, o, lse, do)


@jax.custom_vjp
def _mla_attention(qn, qr, kn, kr, v):
    return _mla_fwd(qn, qr, kn, kr, v)[0]


def _mla_attention_fwd(qn, qr, kn, kr, v):
    o, lse = _mla_fwd(qn, qr, kn, kr, v)
    return o, (qn, qr, kn, kr, v, o, lse)


def _mla_attention_bwd(res, do):
    return tuple(_mla_bwd(*res, do))


_mla_attention.defvjp(_mla_attention_fwd, _mla_attention_bwd)


def _split_cols(x, cuts):
    cuts = tuple(cuts)

    @jax.custom_vjp
    def op(x):
        return tuple(x[:, a:b] for a, b in zip((0,) + cuts, cuts + (x.shape[1],)))

    def fwd(x):
        return op(x), None

    def bwd(_, cts):
        return (jnp.concatenate(cts, axis=1),)

    op.defvjp(fwd, bwd)
    return op(x)


def _swap_halves(w):
    half = w.shape[-1] // 2
    return jnp.concatenate([w[..., half:], w[..., :half]], axis=-1)


def _pad_lanes(w):
    return jnp.concatenate([w, jnp.zeros(w.shape[:-1] + (LANES - w.shape[-1],), w.dtype)], axis=-1)


def _local_loss(x, mod, p, cos, sin, target):
    shift1, scale1, gate1, shift2, scale2, gate2 = [mod[:, i * D_MODEL:(i + 1) * D_MODEL] for i in range(N_MOD)]

    w_in = p["w_in"]
    k_rope_w = w_in[:, 2176:2240]
    w_in_ext = jnp.concatenate([w_in[:, :2176], _pad_lanes(k_rope_w), _pad_lanes(_swap_halves(k_rope_w)),
                                jnp.zeros((D_MODEL, LANES), F32)], axis=1)
    (h1,) = _make_rowwise("pre_attn", _f_pre_attn, 1, 3, [D_MODEL], [True])(x, p["norm_attn"], scale1, shift1)
    proj = _make_linear("in_proj", 512, 640)(h1, w_in_ext)
    q_sb, k_sb, v_sb, cq, ckv, kr, kr_sw, _ = _split_cols(proj, (512, 1024, 1536, 1920, 2176, 2304, 2432))

    o_sb = _sb_attention(q_sb, k_sb, v_sb)

    wq = p["w_q_up"].reshape(MLA_Q_RANK, MLA_HEADS, MLA_QK)
    wq_n, wq_r = wq[:, :, :MLA_NOPE], wq[:, :, MLA_NOPE:]
    w_q_ext = jnp.concatenate([wq_n.reshape(MLA_Q_RANK, -1), _pad_lanes(wq_r).reshape(MLA_Q_RANK, -1),
                               _pad_lanes(_swap_halves(wq_r)).reshape(MLA_Q_RANK, -1)], axis=1)
    wkv = p["w_kv_up"].reshape(MLA_KV_RANK, MLA_HEADS, MLA_NOPE + MLA_V)
    w_kv_ext = jnp.concatenate([wkv[:, :, :MLA_NOPE].reshape(MLA_KV_RANK, -1),
                                wkv[:, :, MLA_NOPE:].reshape(MLA_KV_RANK, -1)], axis=1)
    cqn, ckvn = _make_rowwise("mla_a", _f_mla_a, 2, 2, [MLA_Q_RANK, MLA_KV_RANK], [True, True])(
        cq, ckv, p["q_a_norm"], p["kv_a_norm"])
    qall = _make_linear("q_up", 384, 768)(cqn, w_q_ext)
    kvall = _make_linear("kv_up", 256, 1024)(ckvn, w_kv_ext)
    kn_all, v_mla = _split_cols(kvall, (512,))
    gq = p["q_norm"]
    gkr = p["k_rope_norm"]
    qn, qr, kn, krr = _make_rowwise("mla_b", _f_mla_b, 6, 6, [512, 512, 512, LANES],
                                    [True, True, True, True, False, False])(
        qall, kn_all, kr, kr_sw, cos, sin,
        gq[:, :MLA_NOPE], _pad_lanes(gq[:, MLA_NOPE:]), _pad_lanes(_swap_halves(gq[:, MLA_NOPE:])),
        p["k_nope_norm"], _pad_lanes(gkr), _pad_lanes(_swap_halves(gkr)))
    o_mla = _mla_attention(qn, qr, kn, krr, v_mla)

    (mixed,) = _make_rowwise("post_attn", _f_post_attn, 2, 2, [D_MODEL], [True, True])(
        o_sb, o_mla, p["out_norm_sb"], p["out_norm_mla"])
    attn = _make_linear("out_proj", 512, 512)(mixed, p["w_out"])

    x2, h2 = _make_rowwise("pre_ffn", _f_pre_ffn, 2, 4, [D_MODEL, D_MODEL], [True, True])(
        x, attn, gate1, p["norm_ffn"], scale2, shift2)
    gt = _make_linear("ffn_gate", 512, 256)(h2, p["w_gate"])
    up = _make_linear("ffn_up", 512, 256)(h2, p["w_up"])
    (act,) = _make_rowwise("swiglu", _f_swiglu, 2, 0, [D_FF], [True, True])(gt, up)
    ffn = _make_linear("ffn_down", 256, 1024)(act, p["w_down"])
    (row_loss,) = _make_rowwise("loss", _f_loss, 3, 1, [1], [True, True, False])(x2, ffn, target, gate2)
    return 0.5 * jnp.sum(row_loss)


def _my_place():
    return lax.axis_index("x"), lax.axis_index("y"), lax.axis_index("c")


def _all_gather_small(block, name):
    m_per, n = block.shape

    def body(x_ref, out_ref, send_sems, recv_sems, local_sem):
        x, y, c = _my_place()
        me, sibling = (x, y, c), (x, y, 1 - c)
        chips = [(1 - x, y), (x, 1 - y), (1 - x, 1 - y)]

        def rows(px, py, pc):
            return out_ref.at[pl.ds((4 * px + 2 * py + pc) * m_per, m_per), :]

        def copy(k, blk, to, src=None):
            return pltpu.make_async_remote_copy(
                src_ref=rows(*blk) if src is None else src, dst_ref=rows(*blk),
                send_sem=send_sems.at[k], recv_sem=recv_sems.at[k], device_id=to, device_id_type=MESH)

        mine = pltpu.make_async_copy(x_ref, rows(*me), local_sem)
        mine.start()
        first = [copy(0, me, sibling, src=x_ref)]
        first += [copy(1 + j, me, (*chip, c), src=x_ref) for j, chip in enumerate(chips)]
        for cp in first:
            cp.start()
        passed = [copy(4 + j, (*chip, c), sibling) for j, chip in enumerate(chips)]
        for j, chip in enumerate(chips):
            copy(1 + j, (*chip, c), me).wait_recv()
            passed[j].start()
        copy(0, sibling, me).wait_recv()
        for j, chip in enumerate(chips):
            copy(4 + j, (*chip, 1 - c), me).wait_recv()
        for cp in first + passed:
            cp.wait_send()
        mine.wait()

    return pl.pallas_call(
        body, name=name,
        out_shape=jax.ShapeDtypeStruct((N_DEV * m_per, n), block.dtype),
        in_specs=[pl.BlockSpec(memory_space=pltpu.VMEM)],
        out_specs=pl.BlockSpec(memory_space=pltpu.VMEM),
        scratch_shapes=[pltpu.SemaphoreType.DMA((7,)), pltpu.SemaphoreType.DMA((7,)), pltpu.SemaphoreType.DMA],
    )(block)


def _gather_weights(packed):
    _, hh, ww = packed.shape

    def body(w_ref, out_ref, send_sems, recv_sems, local_sem):
        x, y, c = _my_place()
        sibling = (x, y, 1 - c)
        chips = [(1 - x, y), (x, 1 - y), (1 - x, 1 - y)]

        def blk(px, py, half):
            return out_ref.at[2 * px + py, half]

        def copy(k, src, dst, to):
            return pltpu.make_async_remote_copy(src_ref=src, dst_ref=dst, send_sem=send_sems.at[k],
                                                recv_sem=recv_sems.at[k], device_id=to, device_id_type=MESH)

        mine = pltpu.make_async_copy(w_ref, out_ref.at[2 * x + y], local_sem)
        mine.start()
        first = [copy(j, w_ref.at[c], blk(x, y, c), (*chip, c)) for j, chip in enumerate(chips)]
        for cp in first:
            cp.start()
        passed = [copy(3 + j, blk(*chip, c), blk(*chip, c), sibling) for j, chip in enumerate(chips)]
        for j, chip in enumerate(chips):
            copy(j, blk(*chip, c), blk(*chip, c), (*chip, c)).wait_recv()
            passed[j].start()
        for j, chip in enumerate(chips):
            copy(3 + j, blk(*chip, 1 - c), blk(*chip, 1 - c), sibling).wait_recv()
        for cp in first + passed:
            cp.wait_send()
        mine.wait()

    return pl.pallas_call(
        body, name="gather_weights",
        out_shape=jax.ShapeDtypeStruct((N_CHIPS, 2, hh, ww), packed.dtype),
        in_specs=[ANY], out_specs=ANY,
        scratch_shapes=[pltpu.SemaphoreType.DMA((6,)), pltpu.SemaphoreType.DMA((6,)), pltpu.SemaphoreType.DMA],
    )(packed)


def _pair_exchange(grads):
    nj, _, hh, ww = grads.shape

    def body(g_ref, mine_ref, theirs_ref, send_sems, recv_sems, local_sems):
        x, y, c = _my_place()
        sibling = (x, y, 1 - c)
        sends = [pltpu.make_async_remote_copy(src_ref=g_ref.at[j, 1 - c], dst_ref=theirs_ref.at[j],
                                              send_sem=send_sems.at[j], recv_sem=recv_sems.at[j],
                                              device_id=sibling, device_id_type=MESH) for j in range(nj)]
        keeps = [pltpu.make_async_copy(g_ref.at[j, c], mine_ref.at[j], local_sems.at[j]) for j in range(nj)]
        for cp in sends + keeps:
            cp.start()
        for cp in sends:
            cp.wait_recv()
        for cp in sends:
            cp.wait_send()
        for cp in keeps:
            cp.wait()

    out = jax.ShapeDtypeStruct((nj, hh, ww), grads.dtype)
    return pl.pallas_call(
        body, name="grad_pair_exchange", out_shape=[out, out],
        in_specs=[ANY], out_specs=[ANY, ANY],
        scratch_shapes=[pltpu.SemaphoreType.DMA((nj,)), pltpu.SemaphoreType.DMA((nj,)),
                        pltpu.SemaphoreType.DMA((nj,))],
    )(grads)


def _chip_scatter(pair_sums):
    nj, hh, ww = pair_sums.shape

    def body(s_ref, out_ref, send_sems, recv_sems, local_sem):
        x, y, c = _my_place()
        chips = [(1 - x, y), (x, 1 - y), (1 - x, 1 - y)]
        sends = [pltpu.make_async_remote_copy(src_ref=s_ref.at[2 * cx + cy], dst_ref=out_ref.at[j],
                                              send_sem=send_sems.at[j], recv_sem=recv_sems.at[j],
                                              device_id=(cx, cy, c), device_id_type=MESH)
                 for j, (cx, cy) in enumerate(chips)]
        keep = pltpu.make_async_copy(s_ref.at[2 * x + y], out_ref.at[nj - 1], local_sem)
        keep.start()
        for cp in sends:
            cp.start()
        for cp in sends:
            cp.wait_recv()
        for cp in sends:
            cp.wait_send()
        keep.wait()

    return pl.pallas_call(
        body, name="grad_chip_scatter", out_shape=jax.ShapeDtypeStruct((nj, hh, ww), pair_sums.dtype),
        in_specs=[ANY], out_specs=ANY,
        scratch_shapes=[pltpu.SemaphoreType.DMA((nj - 1,)), pltpu.SemaphoreType.DMA((nj - 1,)),
                        pltpu.SemaphoreType.DMA],
    )(pair_sums)


def _sibling_join(half):
    hh, ww = half.shape

    def body(h_ref, out_ref, send_sem, recv_sem, local_sem):
        x, y, c = _my_place()
        send = pltpu.make_async_remote_copy(src_ref=h_ref, dst_ref=out_ref.at[c], send_sem=send_sem,
                                            recv_sem=recv_sem, device_id=(x, y, 1 - c), device_id_type=MESH)
        keep = pltpu.make_async_copy(h_ref, out_ref.at[c], local_sem)
        send.start()
        keep.start()
        pltpu.make_async_remote_copy(src_ref=h_ref, dst_ref=out_ref.at[1 - c], send_sem=send_sem,
                                     recv_sem=recv_sem, device_id=(x, y, 1 - c), device_id_type=MESH).wait_recv()
        send.wait_send()
        keep.wait()

    return pl.pallas_call(
        body, name="grad_sibling_join", out_shape=jax.ShapeDtypeStruct((2, hh, ww), half.dtype),
        in_specs=[ANY], out_specs=ANY,
        scratch_shapes=[pltpu.SemaphoreType.DMA, pltpu.SemaphoreType.DMA, pltpu.SemaphoreType.DMA],
    )(half)


def _add_to_bf16(a, b):
    nj, hh, ww = a.shape
    spec = pl.BlockSpec((1, ROW_TILE, ww), lambda j, i: (j, i, 0))

    def body(a_ref, b_ref, o_ref):
        o_ref[...] = (a_ref[...] + b_ref[...]).astype(BF16)

    return pl.pallas_call(body, name="grad_pair_sum", grid=(nj, hh // ROW_TILE), in_specs=[spec, spec],
                          out_specs=spec, out_shape=jax.ShapeDtypeStruct(a.shape, BF16))(a, b)


def _sum_parts(parts):
    nj, hh, ww = parts.shape

    def body(p_ref, o_ref):
        acc = p_ref[0].astype(F32)
        for j in range(1, nj):
            acc = acc + p_ref[j].astype(F32)
        o_ref[...] = acc

    return pl.pallas_call(body, name="grad_chip_sum", grid=(hh // ROW_TILE,),
                          in_specs=[pl.BlockSpec((nj, ROW_TILE, ww), lambda i: (0, i, 0))],
                          out_specs=pl.BlockSpec((ROW_TILE, ww), lambda i: (i, 0)),
                          out_shape=jax.ShapeDtypeStruct((hh, ww), F32))(parts)


def _silu(v):
    return v / (1.0 + jnp.exp(-v))


def _ada_fwd(c_all, w_shard, b_shard):
    def body(c_ref, w_ref, b_ref, o_ref):
        o_ref[...] = jnp.dot(_silu(c_ref[...]), w_ref[...], precision=lax.Precision.HIGHEST,
                             preferred_element_type=F32) + b_ref[...]

    return pl.pallas_call(body, name="ada_fwd", out_shape=jax.ShapeDtypeStruct((c_all.shape[0], w_shard.shape[1]), F32),
                          compiler_params=pltpu.CompilerParams(vmem_limit_bytes=MM_VMEM_LIMIT))(c_all, w_shard, b_shard)


def _ada_bwd(c_all, dmod_cols):
    def body(c_ref, d_ref, o_ref):
        o_ref[...] = lax.dot_general(_silu(c_ref[...]), d_ref[...], (((0,), (0,)), ((), ())),
                                     precision=lax.Precision.HIGHEST, preferred_element_type=F32)

    return pl.pallas_call(body, name="ada_bwd", out_shape=jax.ShapeDtypeStruct((c_all.shape[1], dmod_cols.shape[1]), F32),
                          compiler_params=pltpu.CompilerParams(vmem_limit_bytes=MM_VMEM_LIMIT))(c_all, dmod_cols)


def _adamw_math(w, g, m, v):
    m = ADAM_B1 * m + (1.0 - ADAM_B1) * g
    v = ADAM_B2 * v + (1.0 - ADAM_B2) * (g * g)
    m_hat = m / (1.0 - ADAM_B1 ** ADAM_STEP)
    v_hat = v / (1.0 - ADAM_B2 ** ADAM_STEP)
    delta = -ADAM_LR * (m_hat / (jnp.sqrt(v_hat) + ADAM_EPS) + ADAM_WD * w)
    return delta, m, v


def _adamw(w, g, m, v, name):
    r, ccols = w.shape
    tr = max(d for d in range(8, ROW_TILE + 1, 8) if r % d == 0)
    spec = pl.BlockSpec((tr, ccols), lambda i: (i, 0))

    def body(w_ref, g_ref, m_ref, v_ref, d_ref, nm_ref, nv_ref):
        d_ref[...], nm_ref[...], nv_ref[...] = _adamw_math(w_ref[...], g_ref[...], m_ref[...], v_ref[...])

    return pl.pallas_call(body, name=name, grid=(r // tr,), in_specs=[spec] * 4, out_specs=[spec] * 3,
                          out_shape=[jax.ShapeDtypeStruct(w.shape, F32)] * 3,
                          compiler_params=pltpu.CompilerParams(vmem_limit_bytes=MM_VMEM_LIMIT))(w, g, m, v)


def _adamw_small(w, g_all, m, v):
    def body(w_ref, g_ref, m_ref, v_ref, gs_ref, d_ref, nm_ref, nv_ref):
        g = g_ref[0]
        for d in range(1, N_DEV):
            g = g + g_ref[d]
        gs_ref[...] = g
        d_ref[...], nm_ref[...], nv_ref[...] = _adamw_math(w_ref[...], g, m_ref[...], v_ref[...])

    return pl.pallas_call(body, name="adamw_small", out_shape=[jax.ShapeDtypeStruct(w.shape, F32)] * 4)(w, g_all, m, v)


BIG = ("w_in", "w_q_up", "w_kv_up", "w_out", "w_gate", "w_up", "w_down")
ROW_SHARDED = ("w_out", "w_down")
SMALL = ("b_ada", "norm_attn", "norm_ffn", "q_a_norm", "kv_a_norm", "q_norm", "k_nope_norm", "k_rope_norm",
         "out_norm_sb", "out_norm_mla")
WEIGHTS = ("w_ada", "b_ada", "norm_attn", "norm_ffn", "w_in", "q_a_norm", "w_q_up", "kv_a_norm", "w_kv_up",
           "q_norm", "k_nope_norm", "k_rope_norm", "out_norm_sb", "out_norm_mla", "w_out", "w_gate", "w_up",
           "w_down")


def _pack(shards):
    lead = shards[0].shape[0]
    flat = jnp.concatenate([s.reshape(lead, -1) for s in shards], axis=1)
    n = flat.shape[1]
    n_pad = -(-n // PACK_ALIGN) * PACK_ALIGN
    flat = jnp.concatenate([flat, jnp.zeros((lead, n_pad - n), flat.dtype)], axis=1)
    return flat.reshape(lead, 2, n_pad // (2 * PACK_W), PACK_W)


def _unpack(packed, shapes):
    lead = packed.shape[0]
    flat = packed.reshape(lead, -1)
    out, off = [], 0
    for shp in shapes:
        n = int(np.prod(shp))
        out.append(flat[:, off:off + n].reshape((lead,) + tuple(shp)))
        off += n
    return out


def _to_shards(name, full):
    if name in ROW_SHARDED:
        return full.reshape(N_CHIPS, full.shape[0] // N_CHIPS, full.shape[1])
    return full.reshape(full.shape[0], N_CHIPS, full.shape[1] // N_CHIPS).transpose(1, 0, 2)


def _from_shards(name, sh):
    if name in ROW_SHARDED:
        return sh.reshape(-1, sh.shape[2])
    return sh.transpose(1, 0, 2).reshape(sh.shape[1], -1)


def kernel(x, c, positions, w_ada, b_ada, norm_attn, norm_ffn, w_in, q_a_norm, w_q_up, kv_a_norm, w_kv_up, q_norm, k_nope_norm, k_rope_norm, out_norm_sb, out_norm_mla, w_out, w_gate, w_up, w_down, loss_target, m_w_ada, m_b_ada, m_norm_attn, m_norm_ffn, m_w_in, m_q_a_norm, m_w_q_up, m_kv_a_norm, m_w_kv_up, m_q_norm, m_k_nope_norm, m_k_rope_norm, m_out_norm_sb, m_out_norm_mla, m_w_out, m_w_gate, m_w_up, m_w_down, v_w_ada, v_b_ada, v_norm_attn, v_norm_ffn, v_w_in, v_q_a_norm, v_w_q_up, v_kv_a_norm, v_w_kv_up, v_q_norm, v_k_nope_norm, v_k_rope_norm, v_out_norm_sb, v_out_norm_mla, v_w_out, v_w_gate, v_w_up, v_w_down):
    local = dict(locals())
    w = {n: local[n][0] for n in WEIGHTS}
    m = {n: local["m_" + n][0] for n in WEIGHTS}
    v = {n: local["v_" + n][0] for n in WEIGHTS}
    small = {n: w[n].reshape(1, -1) for n in SMALL}
    ix, iy, ic = _my_place()
    chip = 2 * ix + iy
    dev = 2 * chip + ic
    xs, target = x[0], loss_target[0]
    seq = xs.shape[0]

    shard_shapes = [w[n].shape for n in BIG]
    packed = _pack([w[n].astype(BF16)[None] for n in BIG])[0]
    gathered = _gather_weights(packed)
    full = {n: _from_shards(n, s).astype(F32) for n, s in zip(BIG, _unpack(gathered, shard_shapes))}

    c_all = _all_gather_small(c.reshape(8, LANES), "gather_c").reshape(N_DEV, D_MODEL)
    ada_cols = w["w_ada"].shape[1]
    b_cols = lax.dynamic_slice_in_dim(small["b_ada"], chip * ada_cols, ada_cols, axis=1)
    mod_cols = _ada_fwd(c_all, w["w_ada"], b_cols)
    mod_all = _all_gather_small(mod_cols, "gather_mod").reshape(N_CHIPS, 2, N_DEV, ada_cols)
    mod = lax.dynamic_index_in_dim(mod_all[:, 0], dev, axis=1, keepdims=False).reshape(1, N_MOD * D_MODEL)

    half = MLA_ROPE // 2
    freqs = 1.0 / (ROPE_THETA ** (np.arange(half, dtype=np.float32) / half))
    zeros = np.zeros(LANES - MLA_ROPE, np.float32)
    freqs_row = jnp.asarray(np.concatenate([freqs, freqs, zeros]).astype(np.float32)[None])
    sign_row = jnp.asarray(np.concatenate([-np.ones(half), np.ones(half), zeros]).astype(np.float32)[None])
    cos, sin = _rope_tables(positions.reshape(seq, 1), freqs_row, sign_row)

    params = dict(full)
    params.update({n: small[n] for n in SMALL if n != "b_ada"})
    loss_part, (gx, gmod, gp) = jax.value_and_grad(_local_loss, argnums=(0, 1, 2))(xs, mod, params, cos, sin, target)
    loss = lax.psum(loss_part, ("x", "y", "c"))

    gpacked = _pack([_to_shards(n, gp[n]) for n in BIG])
    mine, theirs = _pair_exchange(gpacked)
    parts = _chip_scatter(_add_to_bf16(mine, theirs))
    gshard = _sibling_join(_sum_parts(parts))
    g = dict(zip(BIG, [a[0] for a in _unpack(gshard[None], shard_shapes)]))

    small_names = [n for n in SMALL if n != "b_ada"]
    small_vec = jnp.concatenate([gmod] + [gp[n] for n in small_names], axis=1)
    n_small = small_vec.shape[1]
    small_all = _all_gather_small(small_vec.reshape(8, n_small // 8), "gather_small").reshape(N_DEV, 8, n_small // 8)

    def pack_small(d):
        return jnp.concatenate([d[n].reshape(1, -1) for n in SMALL], axis=1).reshape(8, n_small // 8)

    gs, ds, ms, vs = _adamw_small(pack_small(w), small_all, pack_small(m), pack_small(v))
    sizes = [w[n].size for n in SMALL]
    offs = np.concatenate([[0], np.cumsum(sizes)])

    def unpack_small(a):
        flat = a.reshape(-1)
        return {n: flat[offs[i]:offs[i + 1]].reshape(w[n].shape) for i, n in enumerate(SMALL)}

    g.update(unpack_small(gs))
    delta, new_m, new_v = unpack_small(ds), unpack_small(ms), unpack_small(vs)

    dmod_all = small_all.reshape(N_DEV, n_small)[:, :N_MOD * D_MODEL]
    g["w_ada"] = _ada_bwd(c_all, lax.dynamic_slice_in_dim(dmod_all, chip * ada_cols, ada_cols, axis=1))

    for n in ("w_ada",) + BIG:
        delta[n], new_m[n], new_v[n] = _adamw(w[n], g[n], m[n], v[n], "adamw_" + n)

    def outs(d):
        return [d[n][None] for n in WEIGHTS]

    return (loss, gx[None], *outs(g), *outs(delta), *outs(new_m), *outs(new_v))
```

```python
import functools
import math

import numpy as np
import jax
import jax.numpy as jnp
from jax import lax
from jax.experimental import pallas as pl
from jax.experimental.pallas import tpu as pltpu

F32 = jnp.float32
BF16 = jnp.bfloat16
MESH = pl.DeviceIdType.MESH
ANY = pl.BlockSpec(memory_space=pl.ANY)

D_MODEL = 1024
SB_HEADS = 8
SB_HEAD_DIM = 64
SB_WIDTH = 512
MLA_HEADS = 4
MLA_NOPE = 128
MLA_ROPE = 64
MLA_QK = 192
MLA_V = 128
MLA_Q_RANK = 384
MLA_KV_RANK = 256
D_FF = 2816
N_MOD = 6
ROPE_THETA = 10000.0
EPS = 1e-6
LANES = 128

ADAM_LR = 0.001
ADAM_B1 = 0.9
ADAM_B2 = 0.999
ADAM_EPS = 1e-08
ADAM_WD = 0.01
ADAM_STEP = 10

N_CHIPS = 4
N_DEV = 8
ROW_TILE = 256
ATT_BLK = 256
MM_VMEM_LIMIT = 48 * 1024 * 1024
FF_SHARD = D_FF // N_CHIPS
FF_SHARD_PAD = 768


def _mm(a, b, mode, name, tm, tn, out_dtype=F32):
    if mode == "nn":
        (m, k), n = a.shape, b.shape[1]
        a_spec = pl.BlockSpec((tm, k), lambda j, i: (i, 0))
        b_spec = pl.BlockSpec((k, tn), lambda j, i: (0, j))
        dims = (((1,), (0,)), ((), ()))
    elif mode == "nt":
        (m, k), n = a.shape, b.shape[0]
        a_spec = pl.BlockSpec((tm, k), lambda j, i: (i, 0))
        b_spec = pl.BlockSpec((tn, k), lambda j, i: (j, 0))
        dims = (((1,), (1,)), ((), ()))
    else:
        (k, m), n = a.shape, b.shape[1]
        a_spec = pl.BlockSpec((k, tm), lambda j, i: (0, i))
        b_spec = pl.BlockSpec((k, tn), lambda j, i: (0, j))
        dims = (((0,), (0,)), ((), ()))
    assert m % tm == 0 and n % tn == 0, (name, m, n, tm, tn)

    def body(a_ref, b_ref, o_ref):
        o_ref[...] = lax.dot_general(a_ref[...].astype(BF16), b_ref[...].astype(BF16), dims,
                                     preferred_element_type=F32).astype(out_dtype)

    return pl.pallas_call(
        body, name=name, grid=(n // tn, m // tm),
        in_specs=[a_spec, b_spec],
        out_specs=pl.BlockSpec((tm, tn), lambda j, i: (i, j)),
        out_shape=jax.ShapeDtypeStruct((m, n), out_dtype),
        compiler_params=pltpu.CompilerParams(dimension_semantics=("arbitrary", "arbitrary"),
                                             vmem_limit_bytes=MM_VMEM_LIMIT),
    )(a, b)


def _make_linear(name, tk_w, tn_w):
    @jax.custom_vjp
    def op(a, w):
        return _mm(a, w, "nn", name + "_fwd", ROW_TILE, w.shape[1])

    def fwd(a, w):
        return op(a, w), (a, w)

    def bwd(res, dy):
        a, w = res
        da = _mm(dy, w, "nt", name + "_dx", ROW_TILE, w.shape[0])
        dw = _mm(a, dy, "tn", name + "_dw", tk_w, tn_w, out_dtype=BF16)
        return da, dw

    op.defvjp(fwd, bwd)
    return op


def _make_linear_sharded(name, tk_w):
    def call_fwd(a, w):
        t, k = a.shape
        n_sh, _, cc = w.shape

        def body(a_ref, w_ref, o_ref):
            o_ref[...] = jnp.dot(a_ref[...].astype(BF16), w_ref[...], preferred_element_type=F32)

        return pl.pallas_call(
            body, name=name + "_fwd", grid=(n_sh, t // ROW_TILE),
            in_specs=[pl.BlockSpec((ROW_TILE, k), lambda j, i: (i, 0)),
                      pl.BlockSpec((None, k, cc), lambda j, i: (j, 0, 0))],
            out_specs=pl.BlockSpec((ROW_TILE, cc), lambda j, i: (i, j)),
            out_shape=jax.ShapeDtypeStruct((t, n_sh * cc), F32),
            compiler_params=pltpu.CompilerParams(dimension_semantics=("arbitrary", "arbitrary"),
                                                 vmem_limit_bytes=MM_VMEM_LIMIT),
        )(a, w)

    def call_dx(dy, w):
        t = dy.shape[0]
        n_sh, k, cc = w.shape

        def body(dy_ref, w_ref, o_ref):
            acc = jnp.zeros((ROW_TILE, k), F32)
            for j in range(n_sh):
                acc = acc + _nt(dy_ref[:, j * cc:(j + 1) * cc].astype(BF16), w_ref[j])
            o_ref[...] = acc

        return pl.pallas_call(
            body, name=name + "_dx", grid=(t // ROW_TILE,),
            in_specs=[pl.BlockSpec((ROW_TILE, n_sh * cc), lambda i: (i, 0)),
                      pl.BlockSpec((n_sh, k, cc), lambda i: (0, 0, 0))],
            out_specs=pl.BlockSpec((ROW_TILE, k), lambda i: (i, 0)),
            out_shape=jax.ShapeDtypeStruct((t, k), F32),
            compiler_params=pltpu.CompilerParams(dimension_semantics=("arbitrary",),
                                                 vmem_limit_bytes=MM_VMEM_LIMIT),
        )(dy, w)

    def call_dw(a, dy, w):
        t, k = a.shape
        n_sh, _, cc = w.shape

        def body(a_ref, dy_ref, o_ref):
            o_ref[...] = _tn(a_ref[...].astype(BF16), dy_ref[...].astype(BF16)).astype(BF16)

        return pl.pallas_call(
            body, name=name + "_dw", grid=(n_sh, k // tk_w),
            in_specs=[pl.BlockSpec((t, tk_w), lambda j, i: (0, i)),
                      pl.BlockSpec((t, cc), lambda j, i: (0, j))],
            out_specs=pl.BlockSpec((None, tk_w, cc), lambda j, i: (j, i, 0)),
            out_shape=jax.ShapeDtypeStruct(w.shape, BF16),
            compiler_params=pltpu.CompilerParams(dimension_semantics=("arbitrary", "arbitrary"),
                                                 vmem_limit_bytes=MM_VMEM_LIMIT),
        )(a, dy)

    @jax.custom_vjp
    def op(a, w):
        return call_fwd(a, w)

    def fwd(a, w):
        return op(a, w), (a, w)

    def bwd(res, dy):
        a, w = res
        return call_dx(dy, w), call_dw(a, dy, w)

    op.defvjp(fwd, bwd)
    return op


def _row_spec(arr, tb):
    return pl.BlockSpec((tb, arr.shape[1]), lambda i: (i, 0))


def _full_spec(arr):
    return pl.BlockSpec(arr.shape, lambda i: (0, 0))


def _make_rowwise(name, f, n_rows, n_params, out_cols, diff_rows):
    n_out = len(out_cols)

    def call_fwd(rows, params):
        t = rows[0].shape[0]

        def body(*refs):
            ins = [r[...] for r in refs[:n_rows + n_params]]
            outs = f(*ins)
            for o_ref, o in zip(refs[n_rows + n_params:], outs):
                o_ref[...] = o

        return pl.pallas_call(
            body, name=name + "_fwd", grid=(t // ROW_TILE,),
            in_specs=[_row_spec(a, ROW_TILE) for a in rows] + [_full_spec(p) for p in params],
            out_specs=[pl.BlockSpec((ROW_TILE, n), lambda i: (i, 0)) for n in out_cols],
            out_shape=[jax.ShapeDtypeStruct((t, n), F32) for n in out_cols],
            compiler_params=pltpu.CompilerParams(dimension_semantics=("arbitrary",)),
        )(*rows, *params)

    def call_bwd(rows, params, cts):
        t = rows[0].shape[0]
        d_rows = [a for a, d in zip(rows, diff_rows) if d]
        n_in = n_rows + n_params + n_out

        def body(*refs):
            ins = [r[...] for r in refs[:n_rows + n_params]]
            ct = tuple(r[...] for r in refs[n_rows + n_params:n_in])
            _, vjp = jax.vjp(f, *ins)
            grads = vjp(ct)
            out_refs = refs[n_in:]
            g_rows = [g for g, d in zip(grads[:n_rows], diff_rows) if d]
            for o_ref, g in zip(out_refs[:len(g_rows)], g_rows):
                o_ref[...] = g
            p_refs = out_refs[len(g_rows):]

            if p_refs:
                @pl.when(pl.program_id(0) == 0)
                def _():
                    for p_ref in p_refs:
                        p_ref[...] = jnp.zeros_like(p_ref)

                for p_ref, g in zip(p_refs, grads[n_rows:]):
                    p_ref[...] += g

        return pl.pallas_call(
            body, name=name + "_bwd", grid=(t // ROW_TILE,),
            in_specs=[_row_spec(a, ROW_TILE) for a in rows] + [_full_spec(p) for p in params]
            + [_row_spec(c, ROW_TILE) for c in cts],
            out_specs=[_row_spec(a, ROW_TILE) for a in d_rows] + [_full_spec(p) for p in params],
            out_shape=[jax.ShapeDtypeStruct(a.shape, F32) for a in d_rows]
            + [jax.ShapeDtypeStruct(p.shape, F32) for p in params],
            compiler_params=pltpu.CompilerParams(dimension_semantics=("arbitrary",)),
        )(*rows, *params, *cts)

    @jax.custom_vjp
    def op(*args):
        return tuple(call_fwd(args[:n_rows], args[n_rows:]))

    def fwd(*args):
        return op(*args), args

    def bwd(args, cts):
        rows, params = args[:n_rows], args[n_rows:]
        outs = call_bwd(rows, params, cts)
        it = iter(outs)
        g_rows = [next(it) if d else jnp.zeros_like(a) for a, d in zip(rows, diff_rows)]
        return tuple(g_rows) + tuple(it)

    op.defvjp(fwd, bwd)
    return op


def _rms(x, g, n):
    return x * lax.rsqrt(jnp.sum(x * x, axis=-1, keepdims=True) * (1.0 / n) + EPS) * g


def _f_pre_attn(x, g, scale, shift):
    return (_rms(x, g, D_MODEL) * (1.0 + scale) + shift,)


def _f_mla_a(cq, ckv, gq, gkv):
    return _rms(cq, gq, MLA_Q_RANK), _rms(ckv, gkv, MLA_KV_RANK)


@jax.custom_vjp
def _split_lanes(x):
    return tuple(x[:, i * LANES:(i + 1) * LANES] for i in range(x.shape[1] // LANES))


def _split_lanes_fwd(x):
    return _split_lanes(x), None


def _split_lanes_bwd(_, cts):
    return (jnp.concatenate(cts, axis=1),)


_split_lanes.defvjp(_split_lanes_fwd, _split_lanes_bwd)


def _f_mla_b(qall, kn_all, kr, kr_sw, cos, sin, gqn, gqr, gqr_sw, gkn, gkr, gkr_sw):
    q = _split_lanes(qall)
    kn = _split_lanes(kn_all)
    qn_o, qr_o, kn_o = [], [], []
    for h in range(MLA_HEADS):
        qn, qr, qs = q[h], q[MLA_HEADS + h], q[2 * MLA_HEADS + h]
        ss = jnp.sum(qn * qn, axis=-1, keepdims=True) + jnp.sum(qr * qr, axis=-1, keepdims=True)
        rs = lax.rsqrt(ss * (1.0 / MLA_QK) + EPS)
        qn_o.append(qn * rs * gqn)
        qr_o.append((qr * rs * gqr) * cos + (qs * rs * gqr_sw) * sin)
        kn_o.append(_rms(kn[h], gkn, MLA_NOPE))
    rs = lax.rsqrt(jnp.sum(kr * kr, axis=-1, keepdims=True) * (1.0 / MLA_ROPE) + EPS)
    kr_o = (kr * rs * gkr) * cos + (kr_sw * rs * gkr_sw) * sin
    return (jnp.concatenate(qn_o, axis=1), jnp.concatenate(qr_o, axis=1), jnp.concatenate(kn_o, axis=1), kr_o)


def _f_post_attn(o_sb, o_mla, g_sb, g_mla):
    return (jnp.concatenate([_rms(o_sb, g_sb, SB_WIDTH), _rms(o_mla, g_mla, SB_WIDTH)], axis=1),)


def _f_pre_ffn(x, attn, gate, g, scale, shift):
    x2 = x + gate * attn
    return x2, _rms(x2, g, D_MODEL) * (1.0 + scale) + shift


def _f_swiglu(gt, up):
    return (gt / (1.0 + jnp.exp(-gt)) * up,)


def _f_loss(x2, ffn, target, gate):
    err = x2 + gate * ffn - target
    return (jnp.sum(err * err, axis=-1, keepdims=True) * (1.0 / D_MODEL),)


def _rope_tables(pos_col, freqs, sign):
    t = pos_col.shape[0]

    def body(p_ref, f_ref, s_ref, cos_ref, sin_ref):
        ang = p_ref[...].astype(F32) * f_ref[...]
        live = jnp.abs(s_ref[...])
        cos_ref[...] = jnp.cos(ang) * live
        sin_ref[...] = jnp.sin(ang) * s_ref[...]

    return pl.pallas_call(
        body, name="rope_tables", grid=(t // ROW_TILE,),
        in_specs=[pl.BlockSpec((ROW_TILE, 1), lambda i: (i, 0)), _full_spec(freqs), _full_spec(sign)],
        out_specs=[pl.BlockSpec((ROW_TILE, LANES), lambda i: (i, 0))] * 2,
        out_shape=[jax.ShapeDtypeStruct((t, LANES), F32)] * 2,
    )(pos_col, freqs, sign)


def _hi_lo_dot(x, tri):
    hi = x.astype(BF16)
    lo = (x - hi.astype(F32)).astype(BF16)
    return (jnp.dot(hi, tri, preferred_element_type=F32) + jnp.dot(lo, tri, preferred_element_type=F32))


def _tri(cmp):
    r = lax.broadcasted_iota(jnp.int32, (ATT_BLK, ATT_BLK), 0)
    c = lax.broadcasted_iota(jnp.int32, (ATT_BLK, ATT_BLK), 1)
    return cmp(r, c).astype(BF16)


def _nt(a, b):
    return lax.dot_general(a, b, (((1,), (1,)), ((), ())), preferred_element_type=F32)


def _tn(a, b):
    return lax.dot_general(a, b, (((0,), (0,)), ((), ())), preferred_element_type=F32)


def _sb_logs(z, valid):
    e = jnp.exp(-jnp.abs(z))
    sp = jnp.log(1.0 + e)
    lb = jnp.minimum(z, 0.0) - sp
    l1m = jnp.where(valid, jnp.minimum(-z, 0.0) - sp, 0.0)
    return lb, l1m, e


def _sb_fwd(q, k, v):
    t = q.shape[0]
    nq = t // ATT_BLK
    scale = SB_HEAD_DIM ** -0.5

    def body(q_ref, k_ref, v_ref, o_ref, tot_ref):
        qi = pl.program_id(1)
        lane = lax.broadcasted_iota(jnp.int32, (ATT_BLK, LANES), 1)
        row = lax.broadcasted_iota(jnp.int32, (ATT_BLK, ATT_BLK), 0)
        col = lax.broadcasted_iota(jnp.int32, (ATT_BLK, ATT_BLK), 1)
        tri = _tri(lambda r, c: r > c)
        qv = q_ref[...] * scale
        acc = jnp.zeros((ATT_BLK, LANES), F32)
        for hh in range(2):
            mine = (lane // SB_HEAD_DIM) == hh
            qm = jnp.where(mine, qv, 0.0).astype(BF16)

            def step(j, carry, qm=qm, mine=mine):
                acc_h, run = carry
                kb = qi - j
                off = pl.multiple_of(kb * ATT_BLK, ATT_BLK)
                kk = k_ref[pl.ds(off, ATT_BLK), :].astype(BF16)
                vv = jnp.where(mine, v_ref[pl.ds(off, ATT_BLK), :], 0.0).astype(BF16)
                z = _nt(qm, kk)
                valid = (kb * ATT_BLK + col) < (qi * ATT_BLK + row)
                lb, l1m, _ = _sb_logs(z, valid)
                after = _hi_lo_dot(l1m, tri) + run
                w = jnp.where(valid, jnp.exp(lb + after), 0.0)
                acc_h = acc_h + jnp.dot(w.astype(BF16), vv, preferred_element_type=F32)
                return acc_h, run + jnp.sum(l1m, axis=-1, keepdims=True)

            acc, run = lax.fori_loop(0, qi + 1, step, (acc, jnp.zeros((ATT_BLK, 1), F32)))
            tot_ref[:, hh * LANES:(hh + 1) * LANES] = jnp.broadcast_to(run, (ATT_BLK, LANES))
        o_ref[...] = acc

    return pl.pallas_call(
        body, name="sb_attn_fwd", grid=(SB_HEADS // 2, nq),
        in_specs=[pl.BlockSpec((ATT_BLK, LANES), lambda p, i: (i, p)),
                  pl.BlockSpec((t, LANES), lambda p, i: (0, p)),
                  pl.BlockSpec((t, LANES), lambda p, i: (0, p))],
        out_specs=[pl.BlockSpec((ATT_BLK, LANES), lambda p, i: (i, p)),
                   pl.BlockSpec((ATT_BLK, 2 * LANES), lambda p, i: (i, p))],
        out_shape=[jax.ShapeDtypeStruct((t, SB_WIDTH), F32), jax.ShapeDtypeStruct((t, SB_HEADS * LANES), F32)],
        compiler_params=pltpu.CompilerParams(dimension_semantics=("arbitrary", "arbitrary")),
    )(q, k, v)


def _sb_bwd(q, k, v, tot, do):
    t = q.shape[0]
    nq = t // ATT_BLK
    scale = SB_HEAD_DIM ** -0.5

    def body(q_ref, k_ref, v_ref, tot_ref, do_ref, dq_ref, dk_ref, dv_ref):
        qi = pl.program_id(1)

        @pl.when(qi == 0)
        def _():
            dk_ref[...] = jnp.zeros_like(dk_ref)
            dv_ref[...] = jnp.zeros_like(dv_ref)

        lane = lax.broadcasted_iota(jnp.int32, (ATT_BLK, LANES), 1)
        row = lax.broadcasted_iota(jnp.int32, (ATT_BLK, ATT_BLK), 0)
        col = lax.broadcasted_iota(jnp.int32, (ATT_BLK, ATT_BLK), 1)
        tri_incl = _tri(lambda r, c: r <= c)
        tri_lt = _tri(lambda r, c: r < c)
        qv = q_ref[...] * scale
        dov = do_ref[...]
        dq = jnp.zeros((ATT_BLK, LANES), F32)
        for hh in range(2):
            mine = (lane // SB_HEAD_DIM) == hh
            qm = jnp.where(mine, qv, 0.0).astype(BF16)
            dom = jnp.where(mine, dov, 0.0).astype(BF16)
            tot_h = tot_ref[:, hh * LANES:hh * LANES + 1]

            def step(kb, carry, qm=qm, dom=dom, mine=mine, tot_h=tot_h):
                dq_h, pre, c_de = carry
                off = pl.multiple_of(kb * ATT_BLK, ATT_BLK)
                kk = jnp.where(mine, k_ref[pl.ds(off, ATT_BLK), :], 0.0).astype(BF16)
                vv = v_ref[pl.ds(off, ATT_BLK), :].astype(BF16)
                z = _nt(qm, kk)
                valid = (kb * ATT_BLK + col) < (qi * ATT_BLK + row)
                lb, l1m, e = _sb_logs(z, valid)
                after = tot_h - (_hi_lo_dot(l1m, tri_incl) + pre)
                w = jnp.where(valid, jnp.exp(lb + after), 0.0)
                d_e = w * _nt(dom, vv)
                dl1m = _hi_lo_dot(d_e, tri_lt) + c_de
                inv = 1.0 / (1.0 + e)
                sig = jnp.where(z >= 0.0, inv, e * inv)
                dz = jnp.where(valid, d_e * (1.0 - sig) - dl1m * sig, 0.0).astype(BF16)
                dq_h = dq_h + jnp.dot(dz, kk, preferred_element_type=F32)
                dk_ref[pl.ds(off, ATT_BLK), :] += _tn(dz, qm)
                dv_ref[pl.ds(off, ATT_BLK), :] += _tn(w.astype(BF16), dom)
                return (dq_h, pre + jnp.sum(l1m, axis=-1, keepdims=True),
                        c_de + jnp.sum(d_e, axis=-1, keepdims=True))

            zero = jnp.zeros((ATT_BLK, 1), F32)
            dq, _, _ = lax.fori_loop(0, qi + 1, step, (dq, zero, zero))
        dq_ref[...] = dq * scale

    return pl.pallas_call(
        body, name="sb_attn_bwd", grid=(SB_HEADS // 2, nq),
        in_specs=[pl.BlockSpec((ATT_BLK, LANES), lambda p, i: (i, p)),
                  pl.BlockSpec((t, LANES), lambda p, i: (0, p)),
                  pl.BlockSpec((t, LANES), lambda p, i: (0, p)),
                  pl.BlockSpec((ATT_BLK, 2 * LANES), lambda p, i: (i, p)),
                  pl.BlockSpec((ATT_BLK, LANES), lambda p, i: (i, p))],
        out_specs=[pl.BlockSpec((ATT_BLK, LANES), lambda p, i: (i, p)),
                   pl.BlockSpec((t, LANES), lambda p, i: (0, p)),
                   pl.BlockSpec((t, LANES), lambda p, i: (0, p))],
        out_shape=[jax.ShapeDtypeStruct((t, SB_WIDTH), F32)] * 3,
        compiler_params=pltpu.CompilerParams(dimension_semantics=("arbitrary", "arbitrary")),
    )(q, k, v, tot, do)


@jax.custom_vjp
def _sb_attention(q, k, v):
    return _sb_fwd(q, k, v)[0]


def _sb_attention_fwd(q, k, v):
    o, tot = _sb_fwd(q, k, v)
    return o, (q, k, v, tot)


def _sb_attention_bwd(res, do):
    return tuple(_sb_bwd(*res, do))


_sb_attention.defvjp(_sb_attention_fwd, _sb_attention_bwd)


def _mla_fwd(qn, qr, kn, kr, v):
    t = qn.shape[0]
    nq = t // ATT_BLK
    scale = MLA_QK ** -0.5

    def body(qn_ref, qr_ref, kn_ref, kr_ref, v_ref, o_ref, lse_ref):
        qi = pl.program_id(1)
        row = lax.broadcasted_iota(jnp.int32, (ATT_BLK, ATT_BLK), 0)
        col = lax.broadcasted_iota(jnp.int32, (ATT_BLK, ATT_BLK), 1)
        qnb = qn_ref[...].astype(BF16)
        qrb = qr_ref[...].astype(BF16)

        def step(kb, carry):
            acc, m, l = carry
            off = pl.multiple_of(kb * ATT_BLK, ATT_BLK)
            s = (_nt(qnb, kn_ref[pl.ds(off, ATT_BLK), :].astype(BF16))
                 + _nt(qrb, kr_ref[pl.ds(off, ATT_BLK), :].astype(BF16))) * scale
            s = jnp.where((kb * ATT_BLK + col) <= (qi * ATT_BLK + row), s, -jnp.inf)
            m_new = jnp.maximum(m, jnp.max(s, axis=-1, keepdims=True))
            p = jnp.exp(s - m_new)
            alpha = jnp.exp(m - m_new)
            acc = acc * alpha + jnp.dot(p.astype(BF16), v_ref[pl.ds(off, ATT_BLK), :].astype(BF16),
                                        preferred_element_type=F32)
            return acc, m_new, l * alpha + jnp.sum(p, axis=-1, keepdims=True)

        acc, m, l = lax.fori_loop(0, qi + 1, step, (jnp.zeros((ATT_BLK, LANES), F32),
                                                   jnp.full((ATT_BLK, 1), -jnp.inf, F32),
                                                   jnp.zeros((ATT_BLK, 1), F32)))
        o_ref[...] = acc / l
        lse_ref[...] = jnp.broadcast_to(m + jnp.log(l), (ATT_BLK, LANES))

    blk = pl.BlockSpec((ATT_BLK, LANES), lambda h, i: (i, h))
    full = pl.BlockSpec((t, LANES), lambda h, i: (0, h))
    return pl.pallas_call(
        body, name="mla_attn_fwd", grid=(MLA_HEADS, nq),
        in_specs=[blk, blk, full, pl.BlockSpec((t, LANES), lambda h, i: (0, 0)), full],
        out_specs=[blk, blk],
        out_shape=[jax.ShapeDtypeStruct((t, MLA_HEADS * LANES), F32)] * 2,
        compiler_params=pltpu.CompilerParams(dimension_semantics=("arbitrary", "arbitrary")),
    )(qn, qr, kn, kr, v)


def _mla_bwd(qn, qr, kn, kr, v, o, lse, do):
    t = qn.shape[0]
    nq = t // ATT_BLK
    scale = MLA_QK ** -0.5

    def body(qn_ref, qr_ref, kn_ref, kr_ref, v_ref, o_ref, lse_ref, do_ref,
             dqn_ref, dqr_ref, dkn_ref, dkr_ref, dv_ref):
        h = pl.program_id(0)
        qi = pl.program_id(1)

        @pl.when(qi == 0)
        def _():
            dkn_ref[...] = jnp.zeros_like(dkn_ref)
            dv_ref[...] = jnp.zeros_like(dv_ref)

        @pl.when((qi == 0) & (h == 0))
        def _():
            dkr_ref[...] = jnp.zeros_like(dkr_ref)

        row = lax.broadcasted_iota(jnp.int32, (ATT_BLK, ATT_BLK), 0)
        col = lax.broadcasted_iota(jnp.int32, (ATT_BLK, ATT_BLK), 1)
        qnb = qn_ref[...].astype(BF16)
        qrb = qr_ref[...].astype(BF16)
        dov = do_ref[...]
        dob = dov.astype(BF16)
        delta = jnp.sum(dov * o_ref[...], axis=-1, keepdims=True)
        lse_v = lse_ref[:, 0:1]

        def step(kb, carry):
            dqn, dqr = carry
            off = pl.multiple_of(kb * ATT_BLK, ATT_BLK)
            knb = kn_ref[pl.ds(off, ATT_BLK), :].astype(BF16)
            krb = kr_ref[pl.ds(off, ATT_BLK), :].astype(BF16)
            vb = v_ref[pl.ds(off, ATT_BLK), :].astype(BF16)
            s = (_nt(qnb, knb) + _nt(qrb, krb)) * scale
            p = jnp.where((kb * ATT_BLK + col) <= (qi * ATT_BLK + row), jnp.exp(s - lse_v), 0.0)
            dv_ref[pl.ds(off, ATT_BLK), :] += _tn(p.astype(BF16), dob)
            ds = (p * (_nt(dob, vb) - delta) * scale).astype(BF16)
            dkn_ref[pl.ds(off, ATT_BLK), :] += _tn(ds, qnb)
            dkr_ref[pl.ds(off, ATT_BLK), :] += _tn(ds, qrb)
            return (dqn + jnp.dot(ds, knb, preferred_element_type=F32),
                    dqr + jnp.dot(ds, krb, preferred_element_type=F32))

        zero = jnp.zeros((ATT_BLK, LANES), F32)
        dqn, dqr = lax.fori_loop(0, qi + 1, step, (zero, zero))
        dqn_ref[...] = dqn
        dqr_ref[...] = dqr

    blk = pl.BlockSpec((ATT_BLK, LANES), lambda h, i: (i, h))
    full = pl.BlockSpec((t, LANES), lambda h, i: (0, h))
    shared = pl.BlockSpec((t, LANES), lambda h, i: (0, 0))
    wide = jax.ShapeDtypeStruct((t, MLA_HEADS * LANES), F32)
    return pl.pallas_call(
        body, name="mla_attn_bwd", grid=(MLA_HEADS, nq),
        in_specs=[blk, blk, full, shared, full, blk, blk, blk],
        out_specs=[blk, blk, full, shared, full],
        out_shape=[wide, wide, wide, jax.ShapeDtypeStruct((t, LANES), F32), wide],
        compiler_params=pltpu.CompilerParams(dimension_semantics=("arbitrary", "arbitrary")),
    )(qn, qr, kn, kr, v, o, lse, do)


@jax.custom_vjp
def _mla_attention(qn, qr, kn, kr, v):
    return _mla_fwd(qn, qr, kn, kr, v)[0]


def _mla_attention_fwd(qn, qr, kn, kr, v):
    o, lse = _mla_fwd(qn, qr, kn, kr, v)
    return o, (qn, qr, kn, kr, v, o, lse)


def _mla_attention_bwd(res, do):
    return tuple(_mla_bwd(*res, do))


_mla_attention.defvjp(_mla_attention_fwd, _mla_attention_bwd)


def _split_cols(x, cuts):
    cuts = tuple(cuts)

    @jax.custom_vjp
    def op(x):
        return tuple(x[:, a:b] for a, b in zip((0,) + cuts, cuts + (x.shape[1],)))

    def fwd(x):
        return op(x), None

    def bwd(_, cts):
        return (jnp.concatenate(cts, axis=1),)

    op.defvjp(fwd, bwd)
    return op(x)


def _swap_halves(w):
    half = w.shape[-1] // 2
    return jnp.concatenate([w[..., half:], w[..., :half]], axis=-1)


def _pad_lanes(w):
    return jnp.concatenate([w, jnp.zeros(w.shape[:-1] + (LANES - w.shape[-1],), w.dtype)], axis=-1)


def _join_cols(shards):
    return shards.transpose(1, 0, 2).reshape(shards.shape[1], -1)


def _local_loss(x, mod, p, cos, sin, target):
    shift1, scale1, gate1, shift2, scale2, gate2 = [mod[:, i * D_MODEL:(i + 1) * D_MODEL] for i in range(N_MOD)]

    w_in = _join_cols(p["w_in"])
    k_rope_w = w_in[:, 2176:2240]
    w_in_ext = jnp.concatenate([w_in[:, :2176], _pad_lanes(k_rope_w), _pad_lanes(_swap_halves(k_rope_w)),
                                jnp.zeros((D_MODEL, LANES), w_in.dtype)], axis=1)
    (h1,) = _make_rowwise("pre_attn", _f_pre_attn, 1, 3, [D_MODEL], [True])(x, p["norm_attn"], scale1, shift1)
    proj = _make_linear("in_proj", 512, 640)(h1, w_in_ext)
    q_sb, k_sb, v_sb, cq, ckv, kr, kr_sw, _ = _split_cols(proj, (512, 1024, 1536, 1920, 2176, 2304, 2432))

    o_sb = _sb_attention(q_sb, k_sb, v_sb)

    wq = _join_cols(p["w_q_up"]).reshape(MLA_Q_RANK, MLA_HEADS, MLA_QK)
    wq_n, wq_r = wq[:, :, :MLA_NOPE], wq[:, :, MLA_NOPE:]
    w_q_ext = jnp.concatenate([wq_n.reshape(MLA_Q_RANK, -1), _pad_lanes(wq_r).reshape(MLA_Q_RANK, -1),
                               _pad_lanes(_swap_halves(wq_r)).reshape(MLA_Q_RANK, -1)], axis=1)
    wkv = _join_cols(p["w_kv_up"]).reshape(MLA_KV_RANK, MLA_HEADS, MLA_NOPE + MLA_V)
    w_kv_ext = jnp.concatenate([wkv[:, :, :MLA_NOPE].reshape(MLA_KV_RANK, -1),
                                wkv[:, :, MLA_NOPE:].reshape(MLA_KV_RANK, -1)], axis=1)
    cqn, ckvn = _make_rowwise("mla_a", _f_mla_a, 2, 2, [MLA_Q_RANK, MLA_KV_RANK], [True, True])(
        cq, ckv, p["q_a_norm"], p["kv_a_norm"])
    qall = _make_linear("q_up", 384, 768)(cqn, w_q_ext)
    kvall = _make_linear("kv_up", 256, 1024)(ckvn, w_kv_ext)
    kn_all, v_mla = _split_cols(kvall, (512,))
    gq = p["q_norm"]
    gkr = p["k_rope_norm"]
    qn, qr, kn, krr = _make_rowwise("mla_b", _f_mla_b, 6, 6, [512, 512, 512, LANES],
                                    [True, True, True, True, False, False])(
        qall, kn_all, kr, kr_sw, cos, sin,
        gq[:, :MLA_NOPE], _pad_lanes(gq[:, MLA_NOPE:]), _pad_lanes(_swap_halves(gq[:, MLA_NOPE:])),
        p["k_nope_norm"], _pad_lanes(gkr), _pad_lanes(_swap_halves(gkr)))
    o_mla = _mla_attention(qn, qr, kn, krr, v_mla)

    (mixed,) = _make_rowwise("post_attn", _f_post_attn, 2, 2, [D_MODEL], [True, True])(
        o_sb, o_mla, p["out_norm_sb"], p["out_norm_mla"])
    attn = _make_linear("out_proj", 512, 512)(mixed, p["w_out"].reshape(D_MODEL, D_MODEL))

    x2, h2 = _make_rowwise("pre_ffn", _f_pre_ffn, 2, 4, [D_MODEL, D_MODEL], [True, True])(
        x, attn, gate1, p["norm_ffn"], scale2, shift2)
    gt = _make_linear_sharded("ffn_gate", 512)(h2, p["w_gate"])
    up = _make_linear_sharded("ffn_up", 512)(h2, p["w_up"])
    (act,) = _make_rowwise("swiglu", _f_swiglu, 2, 0, [N_CHIPS * FF_SHARD_PAD], [True, True])(gt, up)
    ffn = _make_linear("ffn_down", 256, 1024)(act, p["w_down"].reshape(N_CHIPS * FF_SHARD_PAD, D_MODEL))
    (row_loss,) = _make_rowwise("loss", _f_loss, 3, 1, [1], [True, True, False])(x2, ffn, target, gate2)
    return 0.5 * jnp.sum(row_loss)


def _my_place():
    return lax.axis_index("x"), lax.axis_index("y"), lax.axis_index("c")


def _all_gather_small(block, name):
    m_per, n = block.shape

    def body(x_ref, out_ref, send_sems, recv_sems, local_sem):
        x, y, c = _my_place()
        me, sibling = (x, y, c), (x, y, 1 - c)
        chips = [(1 - x, y), (x, 1 - y), (1 - x, 1 - y)]

        def rows(px, py, pc):
            return out_ref.at[pl.ds((4 * px + 2 * py + pc) * m_per, m_per), :]

        def copy(k, blk, to, src=None):
            return pltpu.make_async_remote_copy(
                src_ref=rows(*blk) if src is None else src, dst_ref=rows(*blk),
                send_sem=send_sems.at[k], recv_sem=recv_sems.at[k], device_id=to, device_id_type=MESH)

        mine = pltpu.make_async_copy(x_ref, rows(*me), local_sem)
        mine.start()
        first = [copy(0, me, sibling, src=x_ref)]
        first += [copy(1 + j, me, (*chip, c), src=x_ref) for j, chip in enumerate(chips)]
        for cp in first:
            cp.start()
        passed = [copy(4 + j, (*chip, c), sibling) for j, chip in enumerate(chips)]
        for j, chip in enumerate(chips):
            copy(1 + j, (*chip, c), me).wait_recv()
            passed[j].start()
        copy(0, sibling, me).wait_recv()
        for j, chip in enumerate(chips):
            copy(4 + j, (*chip, 1 - c), me).wait_recv()
        for cp in first + passed:
            cp.wait_send()
        mine.wait()

    return pl.pallas_call(
        body, name=name,
        out_shape=jax.ShapeDtypeStruct((N_DEV * m_per, n), block.dtype),
        in_specs=[pl.BlockSpec(memory_space=pltpu.VMEM)],
        out_specs=pl.BlockSpec(memory_space=pltpu.VMEM),
        scratch_shapes=[pltpu.SemaphoreType.DMA((7,)), pltpu.SemaphoreType.DMA((7,)), pltpu.SemaphoreType.DMA],
    )(block)


BIG = ("w_in", "w_q_up", "w_kv_up", "w_out", "w_gate", "w_up", "w_down")
HALF_AXIS = {"w_in": 0, "w_q_up": 0, "w_kv_up": 0, "w_out": 0, "w_gate": 0, "w_up": 0, "w_down": 1}


def _half(ref, h, axis, lead=()):
    trail = ref.shape[len(lead):]
    idx = list(lead) + [slice(None)] * len(trail)
    at = len(trail) - 2 + axis
    n2 = trail[at] // 2
    idx[len(lead) + at] = pl.ds(h * n2, n2)
    return ref.at[tuple(idx)]


def _half_shape(shape, axis):
    shape = list(shape)
    shape[len(shape) - 2 + axis] //= 2
    return tuple(shape)


def _remote(src, dst, send_sems, recv_sems, k, to):
    return pltpu.make_async_remote_copy(src_ref=src, dst_ref=dst, send_sem=send_sems.at[k],
                                        recv_sem=recv_sems.at[k], device_id=to, device_id_type=MESH)


def _gather_weights(shards):
    n_w = len(shards)
    axes = [HALF_AXIS[n] for n in BIG]

    def body(*refs):
        w_refs, out_refs = refs[:n_w], refs[n_w:2 * n_w]
        send_sems, recv_sems, local_sems = refs[2 * n_w:]
        x, y, c = _my_place()
        sibling = (x, y, 1 - c)
        chips = [(1 - x, y), (x, 1 - y), (1 - x, 1 - y)]
        me = 2 * x + y
        mine = [pltpu.make_async_copy(w, o.at[me], local_sems.at[i]) for i, (w, o) in enumerate(zip(w_refs, out_refs))]
        for cp in mine:
            cp.start()
        first = [_remote(_half(w_refs[i], c, axes[i]), _half(out_refs[i], c, axes[i], (me,)),
                         send_sems, recv_sems, 6 * i + j, (*chip, c))
                 for i in range(n_w) for j, chip in enumerate(chips)]
        for cp in first:
            cp.start()
        passed = []
        for j, (cx, cy) in enumerate(chips):
            for i in range(n_w):
                blk = _half(out_refs[i], c, axes[i], (2 * cx + cy,))
                _remote(blk, blk, send_sems, recv_sems, 6 * i + j, (cx, cy, c)).wait_recv()
                cp = _remote(blk, blk, send_sems, recv_sems, 6 * i + 3 + j, sibling)
                cp.start()
                passed.append(cp)
        for j, (cx, cy) in enumerate(chips):
            for i in range(n_w):
                blk = _half(out_refs[i], 1 - c, axes[i], (2 * cx + cy,))
                _remote(blk, blk, send_sems, recv_sems, 6 * i + 3 + j, sibling).wait_recv()
        for cp in first + passed:
            cp.wait_send()
        for cp in mine:
            cp.wait()

    return pl.pallas_call(
        body, name="gather_weights",
        out_shape=[jax.ShapeDtypeStruct((N_CHIPS,) + s.shape, s.dtype) for s in shards],
        in_specs=[ANY] * n_w, out_specs=[ANY] * n_w,
        scratch_shapes=[pltpu.SemaphoreType.DMA((6 * n_w,)), pltpu.SemaphoreType.DMA((6 * n_w,)),
                        pltpu.SemaphoreType.DMA((n_w,))],
    )(*shards)


def _pair_exchange(grads):
    n_w = len(grads)
    axes = [HALF_AXIS[n] for n in BIG]

    def body(*refs):
        g_refs, t_refs = refs[:n_w], refs[n_w:2 * n_w]
        send_sems, recv_sems = refs[2 * n_w:]
        x, y, c = _my_place()
        sends = [_remote(_half(g_refs[i], 1 - c, axes[i]), t_refs[i], send_sems, recv_sems, i, (x, y, 1 - c))
                 for i in range(n_w)]
        for cp in sends:
            cp.start()
        for cp in sends:
            cp.wait_recv()
        for cp in sends:
            cp.wait_send()

    return pl.pallas_call(
        body, name="grad_pair_exchange",
        out_shape=[jax.ShapeDtypeStruct(_half_shape(g.shape, a), g.dtype) for g, a in zip(grads, axes)],
        in_specs=[ANY] * n_w, out_specs=[ANY] * n_w,
        scratch_shapes=[pltpu.SemaphoreType.DMA((n_w,)), pltpu.SemaphoreType.DMA((n_w,))],
    )(*grads)


def _chip_scatter(pair_sums):
    n_w = len(pair_sums)

    def body(*refs):
        s_refs, p_refs = refs[:n_w], refs[n_w:2 * n_w]
        send_sems, recv_sems = refs[2 * n_w:]
        x, y, c = _my_place()
        chips = [(1 - x, y), (x, 1 - y), (1 - x, 1 - y)]
        sends = [_remote(s_refs[i].at[2 * cx + cy], p_refs[i].at[j], send_sems, recv_sems, 3 * i + j, (cx, cy, c))
                 for i in range(n_w) for j, (cx, cy) in enumerate(chips)]
        for cp in sends:
            cp.start()
        for cp in sends:
            cp.wait_recv()
        for cp in sends:
            cp.wait_send()

    return pl.pallas_call(
        body, name="grad_chip_scatter",
        out_shape=[jax.ShapeDtypeStruct((N_CHIPS - 1,) + s.shape[1:], s.dtype) for s in pair_sums],
        in_specs=[ANY] * n_w, out_specs=[ANY] * n_w,
        scratch_shapes=[pltpu.SemaphoreType.DMA((3 * n_w,)), pltpu.SemaphoreType.DMA((3 * n_w,))],
    )(*pair_sums)


def _sibling_join(halves):
    n_w = len(halves)

    def body(*refs):
        s_refs, j_refs = refs[:n_w], refs[n_w:2 * n_w]
        send_sems, recv_sems = refs[2 * n_w:]
        x, y, c = _my_place()
        sends = [_remote(s_refs[i], j_refs[i], send_sems, recv_sems, i, (x, y, 1 - c)) for i in range(n_w)]
        for cp in sends:
            cp.start()
        for cp in sends:
            cp.wait_recv()
        for cp in sends:
            cp.wait_send()

    return pl.pallas_call(
        body, name="grad_sibling_join",
        out_shape=[jax.ShapeDtypeStruct(s.shape, s.dtype) for s in halves],
        in_specs=[ANY] * n_w, out_specs=[ANY] * n_w,
        scratch_shapes=[pltpu.SemaphoreType.DMA((n_w,)), pltpu.SemaphoreType.DMA((n_w,))],
    )(*halves)


def _row_tile(rows, mult=16):
    return max(d for d in range(mult, ROW_TILE + 1, mult) if rows % d == 0)


def _pair_sum(place, g, theirs, axis, name):
    nj, rr, cc = theirs.shape
    tr = _row_tile(rr)
    nb = rr // tr
    if axis == 0:
        g_map = lambda j, i, pr: (j, pr[0] * nb + i, 0)
    else:
        g_map = lambda j, i, pr: (j, i, pr[0])

    def body(pr, g_ref, t_ref, o_ref):
        o_ref[...] = (g_ref[...].astype(F32) + t_ref[...].astype(F32)).astype(BF16)

    spec = pl.BlockSpec((None, tr, cc), lambda j, i, pr: (j, i, 0))
    return pl.pallas_call(
        body, name=name,
        grid_spec=pltpu.PrefetchScalarGridSpec(
            num_scalar_prefetch=1, grid=(nj, nb),
            in_specs=[pl.BlockSpec((None, tr, cc), g_map), spec], out_specs=spec),
        out_shape=jax.ShapeDtypeStruct(theirs.shape, BF16))(place, g, theirs)


def _chip_sum(place, pair_sums, parts, name):
    _, rr, cc = parts.shape
    tr = _row_tile(rr)

    def body(pr, h_ref, p_ref, o_ref):
        acc = p_ref[0].astype(F32)
        for j in range(1, N_CHIPS - 1):
            acc = acc + p_ref[j].astype(F32)
        o_ref[...] = (acc + h_ref[...].astype(F32)).astype(BF16)

    return pl.pallas_call(
        body, name=name,
        grid_spec=pltpu.PrefetchScalarGridSpec(
            num_scalar_prefetch=1, grid=(rr // tr,),
            in_specs=[pl.BlockSpec((None, tr, cc), lambda i, pr: (pr[1], i, 0)),
                      pl.BlockSpec((N_CHIPS - 1, tr, cc), lambda i, pr: (0, i, 0))],
            out_specs=pl.BlockSpec((tr, cc), lambda i, pr: (i, 0))),
        out_shape=jax.ShapeDtypeStruct((rr, cc), BF16))(place, pair_sums, parts)


def _silu(v):
    return v / (1.0 + jnp.exp(-v))


def _ada_fwd(c_all, w_shard, b_shard):
    def body(c_ref, w_ref, b_ref, o_ref):
        o_ref[...] = jnp.dot(_silu(c_ref[...]), w_ref[...], precision=lax.Precision.HIGHEST,
                             preferred_element_type=F32) + b_ref[...]

    return pl.pallas_call(body, name="ada_fwd", out_shape=jax.ShapeDtypeStruct((c_all.shape[0], w_shard.shape[1]), F32),
                          compiler_params=pltpu.CompilerParams(vmem_limit_bytes=MM_VMEM_LIMIT))(c_all, w_shard, b_shard)


def _ada_bwd(c_all, dmod_cols):
    def body(c_ref, d_ref, o_ref):
        o_ref[...] = lax.dot_general(_silu(c_ref[...]), d_ref[...], (((0,), (0,)), ((), ())),
                                     precision=lax.Precision.HIGHEST, preferred_element_type=F32)

    return pl.pallas_call(body, name="ada_bwd", out_shape=jax.ShapeDtypeStruct((c_all.shape[1], dmod_cols.shape[1]), F32),
                          compiler_params=pltpu.CompilerParams(vmem_limit_bytes=MM_VMEM_LIMIT))(c_all, dmod_cols)


def _adamw_math(w, g, m, v):
    m = ADAM_B1 * m + (1.0 - ADAM_B1) * g
    v = ADAM_B2 * v + (1.0 - ADAM_B2) * (g * g)
    m_hat = m / (1.0 - ADAM_B1 ** ADAM_STEP)
    v_hat = v / (1.0 - ADAM_B2 ** ADAM_STEP)
    delta = -ADAM_LR * (m_hat / (jnp.sqrt(v_hat) + ADAM_EPS) + ADAM_WD * w)
    return delta, m, v


def _adamw(w, g, m, v, name):
    r, ccols = w.shape
    tr = max(d for d in range(8, ROW_TILE + 1, 8) if r % d == 0)
    spec = pl.BlockSpec((tr, ccols), lambda i: (i, 0))

    def body(w_ref, g_ref, m_ref, v_ref, d_ref, nm_ref, nv_ref):
        d_ref[...], nm_ref[...], nv_ref[...] = _adamw_math(w_ref[...], g_ref[...], m_ref[...], v_ref[...])

    return pl.pallas_call(body, name=name, grid=(r // tr,), in_specs=[spec] * 4, out_specs=[spec] * 3,
                          out_shape=[jax.ShapeDtypeStruct(w.shape, F32)] * 3,
                          compiler_params=pltpu.CompilerParams(vmem_limit_bytes=MM_VMEM_LIMIT))(w, g, m, v)


def _adamw_small(w, g_all, m, v):
    def body(w_ref, g_ref, m_ref, v_ref, gs_ref, d_ref, nm_ref, nv_ref):
        g = g_ref[0]
        for d in range(1, N_DEV):
            g = g + g_ref[d]
        gs_ref[...] = g
        d_ref[...], nm_ref[...], nv_ref[...] = _adamw_math(w_ref[...], g, m_ref[...], v_ref[...])

    return pl.pallas_call(body, name="adamw_small", out_shape=[jax.ShapeDtypeStruct(w.shape, F32)] * 4)(w, g_all, m, v)


def _adamw_halves(place, w, own, sib, m, v, axis, name):
    r, cc = w.shape
    if axis == 0:
        rows, gc = own.shape[0], own.shape[1]
        tr = _row_tile(rows)
        nb = rows // tr
        w_spec = pl.BlockSpec((tr, cc), lambda h, i, pr: (h * nb + i, 0))
        g_spec = pl.BlockSpec((tr, gc), lambda h, i, pr: (i, 0))
    else:
        tr = _row_tile(r)
        nb = r // tr
        gc = own.shape[1]
        w_spec = pl.BlockSpec((tr, gc), lambda h, i, pr: (i, h))
        g_spec = pl.BlockSpec((tr, gc), lambda h, i, pr: (i, 0))
    wc = w_spec.block_shape[1]

    def body(pr, w_ref, o_ref, s_ref, m_ref, v_ref, g_ref, d_ref, nm_ref, nv_ref):
        g = jnp.where(pl.program_id(0) == pr[0], o_ref[...], s_ref[...]).astype(F32)[:, :wc]
        g_ref[...] = g
        d_ref[...], nm_ref[...], nv_ref[...] = _adamw_math(w_ref[...], g, m_ref[...], v_ref[...])

    return pl.pallas_call(
        body, name=name,
        grid_spec=pltpu.PrefetchScalarGridSpec(
            num_scalar_prefetch=1, grid=(2, nb),
            in_specs=[w_spec, g_spec, g_spec, w_spec, w_spec], out_specs=[w_spec] * 4),
        out_shape=[jax.ShapeDtypeStruct(w.shape, F32)] * 4,
        compiler_params=pltpu.CompilerParams(vmem_limit_bytes=MM_VMEM_LIMIT))(place, w, own, sib, m, v)


SMALL = ("b_ada", "norm_attn", "norm_ffn", "q_a_norm", "kv_a_norm", "q_norm", "k_nope_norm", "k_rope_norm",
         "out_norm_sb", "out_norm_mla")
WEIGHTS = ("w_ada", "b_ada", "norm_attn", "norm_ffn", "w_in", "q_a_norm", "w_q_up", "kv_a_norm", "w_kv_up",
           "q_norm", "k_nope_norm", "k_rope_norm", "out_norm_sb", "out_norm_mla", "w_out", "w_gate", "w_up",
           "w_down")


def kernel(x, c, positions, w_ada, b_ada, norm_attn, norm_ffn, w_in, q_a_norm, w_q_up, kv_a_norm, w_kv_up, q_norm, k_nope_norm, k_rope_norm, out_norm_sb, out_norm_mla, w_out, w_gate, w_up, w_down, loss_target, m_w_ada, m_b_ada, m_norm_attn, m_norm_ffn, m_w_in, m_q_a_norm, m_w_q_up, m_kv_a_norm, m_w_kv_up, m_q_norm, m_k_nope_norm, m_k_rope_norm, m_out_norm_sb, m_out_norm_mla, m_w_out, m_w_gate, m_w_up, m_w_down, v_w_ada, v_b_ada, v_norm_attn, v_norm_ffn, v_w_in, v_q_a_norm, v_w_q_up, v_kv_a_norm, v_w_kv_up, v_q_norm, v_k_nope_norm, v_k_rope_norm, v_out_norm_sb, v_out_norm_mla, v_w_out, v_w_gate, v_w_up, v_w_down):
    local = dict(locals())
    w = {n: local[n][0] for n in WEIGHTS}
    m = {n: local["m_" + n][0] for n in WEIGHTS}
    v = {n: local["v_" + n][0] for n in WEIGHTS}
    small = {n: w[n].reshape(1, -1) for n in SMALL}
    ix, iy, ic = _my_place()
    chip = 2 * ix + iy
    dev = 2 * chip + ic
    xs, target = x[0], loss_target[0]
    seq = xs.shape[0]

    ff_pad = FF_SHARD_PAD - FF_SHARD
    pads = {"w_gate": ((0, 0), (0, ff_pad)), "w_up": ((0, 0), (0, ff_pad)), "w_down": ((0, ff_pad), (0, 0))}
    shards = [jnp.pad(w[n].astype(BF16), pads[n]) if n in pads else w[n].astype(BF16) for n in BIG]
    gathered = dict(zip(BIG, _gather_weights(shards)))

    c_all = _all_gather_small(c.reshape(8, LANES), "gather_c").reshape(N_DEV, D_MODEL)
    ada_cols = w["w_ada"].shape[1]
    b_cols = lax.dynamic_slice_in_dim(small["b_ada"], chip * ada_cols, ada_cols, axis=1)
    mod_cols = _ada_fwd(c_all, w["w_ada"], b_cols)
    mod_all = _all_gather_small(mod_cols, "gather_mod").reshape(N_CHIPS, 2, N_DEV, ada_cols)
    mod = lax.dynamic_index_in_dim(mod_all[:, 0], dev, axis=1, keepdims=False).reshape(1, N_MOD * D_MODEL)

    half = MLA_ROPE // 2
    freqs = 1.0 / (ROPE_THETA ** (np.arange(half, dtype=np.float32) / half))
    zeros = np.zeros(LANES - MLA_ROPE, np.float32)
    freqs_row = jnp.asarray(np.concatenate([freqs, freqs, zeros]).astype(np.float32)[None])
    sign_row = jnp.asarray(np.concatenate([-np.ones(half), np.ones(half), zeros]).astype(np.float32)[None])
    cos, sin = _rope_tables(positions.reshape(seq, 1), freqs_row, sign_row)

    params = dict(gathered)
    params.update({n: small[n] for n in SMALL if n != "b_ada"})
    loss_part, (gx, gmod, gp) = jax.value_and_grad(_local_loss, argnums=(0, 1, 2))(xs, mod, params, cos, sin, target)
    loss = lax.psum(loss_part, ("x", "y", "c"))

    place = jnp.stack([ic, chip]).astype(jnp.int32)
    axes = [HALF_AXIS[n] for n in BIG]
    grads = [gp[n] for n in BIG]
    theirs = _pair_exchange(grads)
    pair_sums = [_pair_sum(place, gr, th, a, "grad_pair_sum_" + n) for n, gr, th, a in zip(BIG, grads, theirs, axes)]
    parts = _chip_scatter(pair_sums)
    own = [_chip_sum(place, ps, pt, "grad_chip_sum_" + n) for n, ps, pt in zip(BIG, pair_sums, parts)]
    sib = _sibling_join(own)
    g = {}

    small_names = [n for n in SMALL if n != "b_ada"]
    small_vec = jnp.concatenate([gmod] + [gp[n] for n in small_names], axis=1)
    n_small = small_vec.shape[1]
    small_all = _all_gather_small(small_vec.reshape(8, n_small // 8), "gather_small").reshape(N_DEV, 8, n_small // 8)

    def pack_small(d):
        return jnp.concatenate([d[n].reshape(1, -1) for n in SMALL], axis=1).reshape(8, n_small // 8)

    gs, ds, ms, vs = _adamw_small(pack_small(w), small_all, pack_small(m), pack_small(v))
    sizes = [w[n].size for n in SMALL]
    offs = np.concatenate([[0], np.cumsum(sizes)])

    def unpack_small(a):
        flat = a.reshape(-1)
        return {n: flat[offs[i]:offs[i + 1]].reshape(w[n].shape) for i, n in enumerate(SMALL)}

    g.update(unpack_small(gs))
    delta, new_m, new_v = unpack_small(ds), unpack_small(ms), unpack_small(vs)

    dmod_all = small_all.reshape(N_DEV, n_small)[:, :N_MOD * D_MODEL]
    g["w_ada"] = _ada_bwd(c_all, lax.dynamic_slice_in_dim(dmod_all, chip * ada_cols, ada_cols, axis=1))

    delta["w_ada"], new_m["w_ada"], new_v["w_ada"] = _adamw(w["w_ada"], g["w_ada"], m["w_ada"], v["w_ada"], "adamw_w_ada")
    for n, o, s, a in zip(BIG, own, sib, axes):
        g[n], delta[n], new_m[n], new_v[n] = _adamw_halves(place, w[n], o, s, m[n], v[n], a, "adamw_" + n)

    def outs(d):
        return [d[n][None] for n in WEIGHTS]

    return (loss, gx[None], *outs(g), *outs(delta), *outs(new_m), *outs(new_v))
```

```python
import functools
import math

import numpy as np
import jax
import jax.numpy as jnp
from jax import lax
from jax.experimental import pallas as pl
from jax.experimental.pallas import tpu as pltpu

F32 = jnp.float32
BF16 = jnp.bfloat16
MESH = pl.DeviceIdType.MESH
ANY = pl.BlockSpec(memory_space=pl.ANY)

D_MODEL = 1024
SB_HEADS = 8
SB_HEAD_DIM = 64
SB_WIDTH = 512
MLA_HEADS = 4
MLA_NOPE = 128
MLA_ROPE = 64
MLA_QK = 192
MLA_V = 128
MLA_Q_RANK = 384
MLA_KV_RANK = 256
D_FF = 2816
N_MOD = 6
ROPE_THETA = 10000.0
EPS = 1e-6
LANES = 128

ADAM_LR = 0.001
ADAM_B1 = 0.9
ADAM_B2 = 0.999
ADAM_EPS = 1e-08
ADAM_WD = 0.01
ADAM_STEP = 10

N_CHIPS = 4
N_DEV = 8
ROW_TILE = 256
ATT_BLK = 256
MM_VMEM_LIMIT = 48 * 1024 * 1024
FF_SHARD = D_FF // N_CHIPS
FF_SHARD_PAD = 768


def _mm(a, b, mode, name, tm, tn, out_dtype=F32):
    if mode == "nn":
        (m, k), n = a.shape, b.shape[1]
        a_spec = pl.BlockSpec((tm, k), lambda j, i: (i, 0))
        b_spec = pl.BlockSpec((k, tn), lambda j, i: (0, j))
        dims = (((1,), (0,)), ((), ()))
    elif mode == "nt":
        (m, k), n = a.shape, b.shape[0]
        a_spec = pl.BlockSpec((tm, k), lambda j, i: (i, 0))
        b_spec = pl.BlockSpec((tn, k), lambda j, i: (j, 0))
        dims = (((1,), (1,)), ((), ()))
    else:
        (k, m), n = a.shape, b.shape[1]
        a_spec = pl.BlockSpec((k, tm), lambda j, i: (0, i))
        b_spec = pl.BlockSpec((k, tn), lambda j, i: (0, j))
        dims = (((0,), (0,)), ((), ()))
    assert m % tm == 0 and n % tn == 0, (name, m, n, tm, tn)

    def body(a_ref, b_ref, o_ref):
        o_ref[...] = lax.dot_general(a_ref[...].astype(BF16), b_ref[...].astype(BF16), dims,
                                     preferred_element_type=F32).astype(out_dtype)

    return pl.pallas_call(
        body, name=name, grid=(n // tn, m // tm),
        in_specs=[a_spec, b_spec],
        out_specs=pl.BlockSpec((tm, tn), lambda j, i: (i, j)),
        out_shape=jax.ShapeDtypeStruct((m, n), out_dtype),
        compiler_params=pltpu.CompilerParams(dimension_semantics=("arbitrary", "arbitrary"),
                                             vmem_limit_bytes=MM_VMEM_LIMIT),
    )(a, b)


def _make_linear(name, tk_w, tn_w):
    @jax.custom_vjp
    def op(a, w):
        return _mm(a, w, "nn", name + "_fwd", ROW_TILE, w.shape[1])

    def fwd(a, w):
        return op(a, w), (a, w)

    def bwd(res, dy):
        a, w = res
        da = _mm(dy, w, "nt", name + "_dx", ROW_TILE, w.shape[0])
        dw = _mm(a, dy, "tn", name + "_dw", tk_w, tn_w, out_dtype=BF16)
        return da, dw

    op.defvjp(fwd, bwd)
    return op


def _make_linear_sharded(name, tk_w):
    def call_fwd(a, w):
        t, k = a.shape
        n_sh, _, cc = w.shape

        def body(a_ref, w_ref, o_ref):
            o_ref[...] = jnp.dot(a_ref[...].astype(BF16), w_ref[...], preferred_element_type=F32)

        return pl.pallas_call(
            body, name=name + "_fwd", grid=(n_sh, t // ROW_TILE),
            in_specs=[pl.BlockSpec((ROW_TILE, k), lambda j, i: (i, 0)),
                      pl.BlockSpec((None, k, cc), lambda j, i: (j, 0, 0))],
            out_specs=pl.BlockSpec((ROW_TILE, cc), lambda j, i: (i, j)),
            out_shape=jax.ShapeDtypeStruct((t, n_sh * cc), F32),
            compiler_params=pltpu.CompilerParams(dimension_semantics=("arbitrary", "arbitrary"),
                                                 vmem_limit_bytes=MM_VMEM_LIMIT),
        )(a, w)

    def call_dx(dy, w):
        t = dy.shape[0]
        n_sh, k, cc = w.shape

        def body(dy_ref, w_ref, o_ref):
            acc = jnp.zeros((ROW_TILE, k), F32)
            for j in range(n_sh):
                acc = acc + _nt(dy_ref[:, j * cc:(j + 1) * cc].astype(BF16), w_ref[j])
            o_ref[...] = acc

        return pl.pallas_call(
            body, name=name + "_dx", grid=(t // ROW_TILE,),
            in_specs=[pl.BlockSpec((ROW_TILE, n_sh * cc), lambda i: (i, 0)),
                      pl.BlockSpec((n_sh, k, cc), lambda i: (0, 0, 0))],
            out_specs=pl.BlockSpec((ROW_TILE, k), lambda i: (i, 0)),
            out_shape=jax.ShapeDtypeStruct((t, k), F32),
            compiler_params=pltpu.CompilerParams(dimension_semantics=("arbitrary",),
                                                 vmem_limit_bytes=MM_VMEM_LIMIT),
        )(dy, w)

    def call_dw(a, dy, w):
        t, k = a.shape
        n_sh, _, cc = w.shape

        def body(a_ref, dy_ref, o_ref):
            o_ref[...] = _tn(a_ref[...].astype(BF16), dy_ref[...].astype(BF16)).astype(BF16)

        return pl.pallas_call(
            body, name=name + "_dw", grid=(n_sh, k // tk_w),
            in_specs=[pl.BlockSpec((t, tk_w), lambda j, i: (0, i)),
                      pl.BlockSpec((t, cc), lambda j, i: (0, j))],
            out_specs=pl.BlockSpec((None, tk_w, cc), lambda j, i: (j, i, 0)),
            out_shape=jax.ShapeDtypeStruct(w.shape, BF16),
            compiler_params=pltpu.CompilerParams(dimension_semantics=("arbitrary", "arbitrary"),
                                                 vmem_limit_bytes=MM_VMEM_LIMIT),
        )(a, dy)

    @jax.custom_vjp
    def op(a, w):
        return call_fwd(a, w)

    def fwd(a, w):
        return op(a, w), (a, w)

    def bwd(res, dy):
        a, w = res
        return call_dx(dy, w), call_dw(a, dy, w)

    op.defvjp(fwd, bwd)
    return op


def _row_spec(arr, tb):
    return pl.BlockSpec((tb, arr.shape[1]), lambda i: (i, 0))


def _full_spec(arr):
    return pl.BlockSpec(arr.shape, lambda i: (0, 0))


def _make_rowwise(name, f, n_rows, n_params, out_cols, diff_rows):
    n_out = len(out_cols)

    def call_fwd(rows, params):
        t = rows[0].shape[0]

        def body(*refs):
            ins = [r[...] for r in refs[:n_rows + n_params]]
            outs = f(*ins)
            for o_ref, o in zip(refs[n_rows + n_params:], outs):
                o_ref[...] = o

        return pl.pallas_call(
            body, name=name + "_fwd", grid=(t // ROW_TILE,),
            in_specs=[_row_spec(a, ROW_TILE) for a in rows] + [_full_spec(p) for p in params],
            out_specs=[pl.BlockSpec((ROW_TILE, n), lambda i: (i, 0)) for n in out_cols],
            out_shape=[jax.ShapeDtypeStruct((t, n), F32) for n in out_cols],
            compiler_params=pltpu.CompilerParams(dimension_semantics=("arbitrary",),
                                                 vmem_limit_bytes=MM_VMEM_LIMIT),
        )(*rows, *params)

    def call_bwd(rows, params, cts):
        t = rows[0].shape[0]
        d_rows = [a for a, d in zip(rows, diff_rows) if d]
        n_in = n_rows + n_params + n_out

        def body(*refs):
            ins = [r[...] for r in refs[:n_rows + n_params]]
            ct = tuple(r[...] for r in refs[n_rows + n_params:n_in])
            _, vjp = jax.vjp(f, *ins)
            grads = vjp(ct)
            out_refs = refs[n_in:]
            g_rows = [g for g, d in zip(grads[:n_rows], diff_rows) if d]
            for o_ref, g in zip(out_refs[:len(g_rows)], g_rows):
                o_ref[...] = g
            p_refs = out_refs[len(g_rows):]

            if p_refs:
                @pl.when(pl.program_id(0) == 0)
                def _():
                    for p_ref in p_refs:
                        p_ref[...] = jnp.zeros_like(p_ref)

                for p_ref, g in zip(p_refs, grads[n_rows:]):
                    p_ref[...] += g

        return pl.pallas_call(
            body, name=name + "_bwd", grid=(t // ROW_TILE,),
            in_specs=[_row_spec(a, ROW_TILE) for a in rows] + [_full_spec(p) for p in params]
            + [_row_spec(c, ROW_TILE) for c in cts],
            out_specs=[_row_spec(a, ROW_TILE) for a in d_rows] + [_full_spec(p) for p in params],
            out_shape=[jax.ShapeDtypeStruct(a.shape, F32) for a in d_rows]
            + [jax.ShapeDtypeStruct(p.shape, F32) for p in params],
            compiler_params=pltpu.CompilerParams(dimension_semantics=("arbitrary",),
                                                 vmem_limit_bytes=MM_VMEM_LIMIT),
        )(*rows, *params, *cts)

    @jax.custom_vjp
    def op(*args):
        return tuple(call_fwd(args[:n_rows], args[n_rows:]))

    def fwd(*args):
        return op(*args), args

    def bwd(args, cts):
        rows, params = args[:n_rows], args[n_rows:]
        outs = call_bwd(rows, params, cts)
        it = iter(outs)
        g_rows = [next(it) if d else jnp.zeros_like(a) for a, d in zip(rows, diff_rows)]
        return tuple(g_rows) + tuple(it)

    op.defvjp(fwd, bwd)
    return op


def _rms(x, g, n):
    return x * lax.rsqrt(jnp.sum(x * x, axis=-1, keepdims=True) * (1.0 / n) + EPS) * g


def _f_pre_attn(x, g, scale, shift):
    return (_rms(x, g, D_MODEL) * (1.0 + scale) + shift,)


def _f_mla_a(cq, ckv, gq, gkv):
    return _rms(cq, gq, MLA_Q_RANK), _rms(ckv, gkv, MLA_KV_RANK)


@jax.custom_vjp
def _split_lanes(x):
    return tuple(x[:, i * LANES:(i + 1) * LANES] for i in range(x.shape[1] // LANES))


def _split_lanes_fwd(x):
    return _split_lanes(x), None


def _split_lanes_bwd(_, cts):
    return (jnp.concatenate(cts, axis=1),)


_split_lanes.defvjp(_split_lanes_fwd, _split_lanes_bwd)


def _f_mla_b(qall, kn_all, kr, kr_sw, cos, sin, gqn, gqr, gqr_sw, gkn, gkr, gkr_sw):
    q = _split_lanes(qall)
    kn = _split_lanes(kn_all)
    qn_o, qr_o, kn_o = [], [], []
    for h in range(MLA_HEADS):
        qn, qr, qs = q[h], q[MLA_HEADS + h], q[2 * MLA_HEADS + h]
        ss = jnp.sum(qn * qn, axis=-1, keepdims=True) + jnp.sum(qr * qr, axis=-1, keepdims=True)
        rs = lax.rsqrt(ss * (1.0 / MLA_QK) + EPS)
        qn_o.append(qn * rs * gqn)
        qr_o.append((qr * rs * gqr) * cos + (qs * rs * gqr_sw) * sin)
        kn_o.append(_rms(kn[h], gkn, MLA_NOPE))
    rs = lax.rsqrt(jnp.sum(kr * kr, axis=-1, keepdims=True) * (1.0 / MLA_ROPE) + EPS)
    kr_o = (kr * rs * gkr) * cos + (kr_sw * rs * gkr_sw) * sin
    return (jnp.concatenate(qn_o, axis=1), jnp.concatenate(qr_o, axis=1), jnp.concatenate(kn_o, axis=1), kr_o)


def _f_post_attn(o_sb, o_mla, g_sb, g_mla):
    return (jnp.concatenate([_rms(o_sb, g_sb, SB_WIDTH), _rms(o_mla, g_mla, SB_WIDTH)], axis=1),)


def _f_pre_ffn(x, attn, gate, g, scale, shift):
    x2 = x + gate * attn
    return x2, _rms(x2, g, D_MODEL) * (1.0 + scale) + shift


def _f_swiglu(gt, up):
    return (gt / (1.0 + jnp.exp(-gt)) * up,)


def _f_loss(x2, ffn, target, gate):
    err = x2 + gate * ffn - target
    return (jnp.sum(err * err, axis=-1, keepdims=True) * (1.0 / D_MODEL),)


def _rope_tables(pos_col, freqs, sign):
    t = pos_col.shape[0]

    def body(p_ref, f_ref, s_ref, cos_ref, sin_ref):
        ang = p_ref[...].astype(F32) * f_ref[...]
        live = jnp.abs(s_ref[...])
        cos_ref[...] = jnp.cos(ang) * live
        sin_ref[...] = jnp.sin(ang) * s_ref[...]

    return pl.pallas_call(
        body, name="rope_tables", grid=(t // ROW_TILE,),
        in_specs=[pl.BlockSpec((ROW_TILE, 1), lambda i: (i, 0)), _full_spec(freqs), _full_spec(sign)],
        out_specs=[pl.BlockSpec((ROW_TILE, LANES), lambda i: (i, 0))] * 2,
        out_shape=[jax.ShapeDtypeStruct((t, LANES), F32)] * 2,
    )(pos_col, freqs, sign)


def _hi_lo_dot(x, tri):
    hi = x.astype(BF16)
    lo = (x - hi.astype(F32)).astype(BF16)
    return (jnp.dot(hi, tri, preferred_element_type=F32) + jnp.dot(lo, tri, preferred_element_type=F32))


def _tri(cmp):
    r = lax.broadcasted_iota(jnp.int32, (ATT_BLK, ATT_BLK), 0)
    c = lax.broadcasted_iota(jnp.int32, (ATT_BLK, ATT_BLK), 1)
    return cmp(r, c).astype(BF16)


def _nt(a, b):
    return lax.dot_general(a, b, (((1,), (1,)), ((), ())), preferred_element_type=F32)


def _tn(a, b):
    return lax.dot_general(a, b, (((0,), (0,)), ((), ())), preferred_element_type=F32)


def _sb_logs(z):
    lb = jnp.minimum(z, 0.0) - jnp.log(1.0 + jnp.exp(-jnp.abs(z)))
    return lb, lb - z


def _sb_fwd(q, k, v):
    t = q.shape[0]
    nq = t // ATT_BLK
    scale = SB_HEAD_DIM ** -0.5

    def body(q_ref, k_ref, v_ref, o_ref, tot_ref):
        qi = pl.program_id(1)
        lane = lax.broadcasted_iota(jnp.int32, (ATT_BLK, LANES), 1)
        tri = _tri(lambda r, c: r > c)
        qv = q_ref[...] * scale
        heads = [(lane // SB_HEAD_DIM) == hh for hh in range(2)]
        qms = [jnp.where(mine, qv, 0.0).astype(BF16) for mine in heads]

        def block(kb, carry, diagonal):
            acc, runs = carry[0], carry[1:]
            off = pl.multiple_of(kb * ATT_BLK, ATT_BLK)
            kk = k_ref[pl.ds(off, ATT_BLK), :].astype(BF16)
            v_blk = v_ref[pl.ds(off, ATT_BLK), :]
            both = range(2)
            if diagonal:
                valid = (lax.broadcasted_iota(jnp.int32, (ATT_BLK, ATT_BLK), 1)
                         < lax.broadcasted_iota(jnp.int32, (ATT_BLK, ATT_BLK), 0))
            zs = [_nt(qms[hh], kk) for hh in both]
            vvs = [jnp.where(heads[hh], v_blk, 0.0).astype(BF16) for hh in both]
            logs = [_sb_logs(z) for z in zs]
            l1ms = [jnp.where(valid, lg[1], 0.0) for lg in logs] if diagonal else [lg[1] for lg in logs]
            afters = [_hi_lo_dot(l1ms[hh], tri) for hh in both]
            ws = [jnp.exp(logs[hh][0] + (afters[hh] + runs[hh])) for hh in both]
            if diagonal:
                ws = [jnp.where(valid, w, 0.0) for w in ws]
            acc = acc + jnp.dot(ws[0].astype(BF16), vvs[0], preferred_element_type=F32) + jnp.dot(
                ws[1].astype(BF16), vvs[1], preferred_element_type=F32)
            return (acc, *[runs[hh] + jnp.sum(l1ms[hh], axis=-1, keepdims=True) for hh in both])

        zero = jnp.zeros((ATT_BLK, 1), F32)
        carry = block(qi, (jnp.zeros((ATT_BLK, LANES), F32), zero, zero), True)
        carry = lax.fori_loop(0, qi, lambda j, cr: block(qi - 1 - j, cr, False), carry)
        o_ref[...] = carry[0]
        for hh in range(2):
            tot_ref[:, hh * LANES:(hh + 1) * LANES] = jnp.broadcast_to(carry[1 + hh], (ATT_BLK, LANES))

    return pl.pallas_call(
        body, name="sb_attn_fwd", grid=(SB_HEADS // 2, nq),
        in_specs=[pl.BlockSpec((ATT_BLK, LANES), lambda p, i: (i, p)),
                  pl.BlockSpec((t, LANES), lambda p, i: (0, p)),
                  pl.BlockSpec((t, LANES), lambda p, i: (0, p))],
        out_specs=[pl.BlockSpec((ATT_BLK, LANES), lambda p, i: (i, p)),
                   pl.BlockSpec((ATT_BLK, 2 * LANES), lambda p, i: (i, p))],
        out_shape=[jax.ShapeDtypeStruct((t, SB_WIDTH), F32), jax.ShapeDtypeStruct((t, SB_HEADS * LANES), F32)],
        compiler_params=pltpu.CompilerParams(dimension_semantics=("arbitrary", "arbitrary")),
    )(q, k, v)


def _sb_bwd(q, k, v, tot, do):
    t = q.shape[0]
    nq = t // ATT_BLK
    scale = SB_HEAD_DIM ** -0.5

    def body(q_ref, k_ref, v_ref, tot_ref, do_ref, dq_ref, dk_ref, dv_ref):
        qi = pl.program_id(1)

        @pl.when(qi == 0)
        def _():
            dk_ref[...] = jnp.zeros_like(dk_ref)
            dv_ref[...] = jnp.zeros_like(dv_ref)

        lane = lax.broadcasted_iota(jnp.int32, (ATT_BLK, LANES), 1)
        tri_incl = _tri(lambda r, c: r <= c)
        tri_lt = _tri(lambda r, c: r < c)
        qv = q_ref[...] * scale
        dov = do_ref[...]
        heads = [(lane // SB_HEAD_DIM) == hh for hh in range(2)]
        qms = [jnp.where(mine, qv, 0.0).astype(BF16) for mine in heads]
        doms = [jnp.where(mine, dov, 0.0).astype(BF16) for mine in heads]
        tots = [tot_ref[:, hh * LANES:hh * LANES + 1] for hh in range(2)]

        def block(kb, carry, diagonal):
            dq = carry[0]
            off = pl.multiple_of(kb * ATT_BLK, ATT_BLK)
            k_blk = k_ref[pl.ds(off, ATT_BLK), :]
            vv = v_ref[pl.ds(off, ATT_BLK), :].astype(BF16)
            both = range(2)
            pres, c_des = [carry[1], carry[3]], [carry[2], carry[4]]
            if diagonal:
                valid = (lax.broadcasted_iota(jnp.int32, (ATT_BLK, ATT_BLK), 1)
                         < lax.broadcasted_iota(jnp.int32, (ATT_BLK, ATT_BLK), 0))
            kks = [jnp.where(heads[hh], k_blk, 0.0).astype(BF16) for hh in both]
            zs = [_nt(qms[hh], kks[hh]) for hh in both]
            dws = [_nt(doms[hh], vv) for hh in both]
            logs = [_sb_logs(z) for z in zs]
            lbs = [lg[0] for lg in logs]
            l1m_all = [lg[1] for lg in logs]
            l1ms = [jnp.where(valid, a, 0.0) for a in l1m_all] if diagonal else l1m_all
            prefix = [_hi_lo_dot(l1ms[hh], tri_incl) for hh in both]
            ws = [jnp.exp(lbs[hh] + (tots[hh] - (prefix[hh] + pres[hh]))) for hh in both]
            if diagonal:
                ws = [jnp.where(valid, w, 0.0) for w in ws]
            d_es = [ws[hh] * dws[hh] for hh in both]
            dv = _tn(ws[0].astype(BF16), doms[0]) + _tn(ws[1].astype(BF16), doms[1])
            dl1ms = [_hi_lo_dot(d_es[hh], tri_lt) + c_des[hh] for hh in both]
            dzs = [d_es[hh] * jnp.exp(l1m_all[hh]) - dl1ms[hh] * jnp.exp(lbs[hh]) for hh in both]
            if diagonal:
                dzs = [jnp.where(valid, dz, 0.0) for dz in dzs]
            dzs = [dz.astype(BF16) for dz in dzs]
            dq = dq + jnp.dot(dzs[0], kks[0], preferred_element_type=F32) + jnp.dot(dzs[1], kks[1],
                                                                                    preferred_element_type=F32)
            dk_ref[pl.ds(off, ATT_BLK), :] += _tn(dzs[0], qms[0]) + _tn(dzs[1], qms[1])
            dv_ref[pl.ds(off, ATT_BLK), :] += dv
            return (dq,
                    pres[0] + jnp.sum(l1ms[0], axis=-1, keepdims=True), c_des[0] + jnp.sum(d_es[0], axis=-1, keepdims=True),
                    pres[1] + jnp.sum(l1ms[1], axis=-1, keepdims=True), c_des[1] + jnp.sum(d_es[1], axis=-1, keepdims=True))

        zero = jnp.zeros((ATT_BLK, 1), F32)
        carry = lax.fori_loop(0, qi, lambda kb, cr: block(kb, cr, False),
                              (jnp.zeros((ATT_BLK, LANES), F32), zero, zero, zero, zero))
        carry = block(qi, carry, True)
        dq_ref[...] = carry[0] * scale

    return pl.pallas_call(
        body, name="sb_attn_bwd", grid=(SB_HEADS // 2, nq),
        in_specs=[pl.BlockSpec((ATT_BLK, LANES), lambda p, i: (i, p)),
                  pl.BlockSpec((t, LANES), lambda p, i: (0, p)),
                  pl.BlockSpec((t, LANES), lambda p, i: (0, p)),
                  pl.BlockSpec((ATT_BLK, 2 * LANES), lambda p, i: (i, p)),
                  pl.BlockSpec((ATT_BLK, LANES), lambda p, i: (i, p))],
        out_specs=[pl.BlockSpec((ATT_BLK, LANES), lambda p, i: (i, p)),
                   pl.BlockSpec((t, LANES), lambda p, i: (0, p)),
                   pl.BlockSpec((t, LANES), lambda p, i: (0, p))],
        out_shape=[jax.ShapeDtypeStruct((t, SB_WIDTH), F32)] * 3,
        compiler_params=pltpu.CompilerParams(dimension_semantics=("arbitrary", "arbitrary")),
    )(q, k, v, tot, do)


@jax.custom_vjp
def _sb_attention(q, k, v):
    return _sb_fwd(q, k, v)[0]


def _sb_attention_fwd(q, k, v):
    o, tot = _sb_fwd(q, k, v)
    return o, (q, k, v, tot)


def _sb_attention_bwd(res, do):
    return tuple(_sb_bwd(*res, do))


_sb_attention.defvjp(_sb_attention_fwd, _sb_attention_bwd)


def _mla_fwd(qn, qr, kn, kr, v):
    t = qn.shape[0]
    nq = t // ATT_BLK
    scale = MLA_QK ** -0.5

    def body(qn_ref, qr_ref, kn_ref, kr_ref, v_ref, o_ref, lse_ref):
        qi = pl.program_id(1)
        lanes = [slice(hh * LANES, (hh + 1) * LANES) for hh in range(2)]
        qnb = [qn_ref[:, sl].astype(BF16) for sl in lanes]
        qrb = [qr_ref[:, sl].astype(BF16) for sl in lanes]

        def block(kb, carry, diagonal):
            off = pl.multiple_of(kb * ATT_BLK, ATT_BLK)
            krb = kr_ref[pl.ds(off, ATT_BLK), :].astype(BF16)
            both = range(2)
            accs, ms, ls = [carry[0], carry[3]], [carry[1], carry[4]], [carry[2], carry[5]]
            ss = [(_nt(qnb[hh], kn_ref[pl.ds(off, ATT_BLK), lanes[hh]].astype(BF16)) + _nt(qrb[hh], krb)) * scale
                  for hh in both]
            if diagonal:
                causal = (lax.broadcasted_iota(jnp.int32, (ATT_BLK, ATT_BLK), 1)
                          <= lax.broadcasted_iota(jnp.int32, (ATT_BLK, ATT_BLK), 0))
                ss = [jnp.where(causal, s, -jnp.inf) for s in ss]
            m_new = [jnp.maximum(ms[hh], jnp.max(ss[hh], axis=-1, keepdims=True)) for hh in both]
            ps = [jnp.exp(ss[hh] - m_new[hh]) for hh in both]
            alphas = [jnp.exp(ms[hh] - m_new[hh]) for hh in both]
            pvs = [jnp.dot(ps[hh].astype(BF16), v_ref[pl.ds(off, ATT_BLK), lanes[hh]].astype(BF16),
                           preferred_element_type=F32) for hh in both]
            out = []
            for hh in both:
                out += [accs[hh] * alphas[hh] + pvs[hh], m_new[hh],
                        ls[hh] * alphas[hh] + jnp.sum(ps[hh], axis=-1, keepdims=True)]
            return tuple(out)

        init = (jnp.zeros((ATT_BLK, LANES), F32), jnp.full((ATT_BLK, 1), -jnp.inf, F32), jnp.zeros((ATT_BLK, 1), F32))
        carry = block(qi, init + init, True)
        carry = lax.fori_loop(0, qi, lambda kb, cr: block(kb, cr, False), carry)
        for hh in range(2):
            acc, m, l = carry[3 * hh:3 * hh + 3]
            o_ref[:, lanes[hh]] = acc / l
            lse_ref[:, lanes[hh]] = jnp.broadcast_to(m + jnp.log(l), (ATT_BLK, LANES))

    blk = pl.BlockSpec((ATT_BLK, 2 * LANES), lambda p, i: (i, p))
    full = pl.BlockSpec((t, 2 * LANES), lambda p, i: (0, p))
    return pl.pallas_call(
        body, name="mla_attn_fwd", grid=(MLA_HEADS // 2, nq),
        in_specs=[blk, blk, full, pl.BlockSpec((t, LANES), lambda p, i: (0, 0)), full],
        out_specs=[blk, blk],
        out_shape=[jax.ShapeDtypeStruct((t, MLA_HEADS * LANES), F32)] * 2,
        compiler_params=pltpu.CompilerParams(dimension_semantics=("arbitrary", "arbitrary")),
    )(qn, qr, kn, kr, v)


def _mla_bwd(qn, qr, kn, kr, v, o, lse, do):
    t = qn.shape[0]
    nq = t // ATT_BLK
    scale = MLA_QK ** -0.5

    def body(qn_ref, qr_ref, kn_ref, kr_ref, v_ref, o_ref, lse_ref, do_ref,
             dqn_ref, dqr_ref, dkn_ref, dkr_ref, dv_ref):
        pair = pl.program_id(0)
        qi = pl.program_id(1)

        @pl.when(qi == 0)
        def _():
            dkn_ref[...] = jnp.zeros_like(dkn_ref)
            dv_ref[...] = jnp.zeros_like(dv_ref)

        @pl.when((qi == 0) & (pair == 0))
        def _():
            dkr_ref[...] = jnp.zeros_like(dkr_ref)

        lanes = [slice(hh * LANES, (hh + 1) * LANES) for hh in range(2)]
        qnb = [qn_ref[:, sl].astype(BF16) for sl in lanes]
        qrb = [qr_ref[:, sl].astype(BF16) for sl in lanes]
        dob = [do_ref[:, sl].astype(BF16) for sl in lanes]
        delta = [jnp.sum(do_ref[:, sl] * o_ref[:, sl], axis=-1, keepdims=True) for sl in lanes]
        lse_v = [lse_ref[:, hh * LANES:hh * LANES + 1] for hh in range(2)]

        def block(kb, carry, diagonal):
            off = pl.multiple_of(kb * ATT_BLK, ATT_BLK)
            krb = kr_ref[pl.ds(off, ATT_BLK), :].astype(BF16)
            both = range(2)
            knb = [kn_ref[pl.ds(off, ATT_BLK), lanes[hh]].astype(BF16) for hh in both]
            vb = [v_ref[pl.ds(off, ATT_BLK), lanes[hh]].astype(BF16) for hh in both]
            ss = [_nt(qnb[hh], knb[hh]) + _nt(qrb[hh], krb) for hh in both]
            dps = [_nt(dob[hh], vb[hh]) for hh in both]
            ps = [jnp.exp(ss[hh] * scale - lse_v[hh]) for hh in both]
            if diagonal:
                causal = (lax.broadcasted_iota(jnp.int32, (ATT_BLK, ATT_BLK), 1)
                          <= lax.broadcasted_iota(jnp.int32, (ATT_BLK, ATT_BLK), 0))
                ps = [jnp.where(causal, p, 0.0) for p in ps]
            dss = [(ps[hh] * (dps[hh] - delta[hh]) * scale).astype(BF16) for hh in both]
            for hh in both:
                dv_ref[pl.ds(off, ATT_BLK), lanes[hh]] += _tn(ps[hh].astype(BF16), dob[hh])
            for hh in both:
                dkn_ref[pl.ds(off, ATT_BLK), lanes[hh]] += _tn(dss[hh], qnb[hh])
            dkr_ref[pl.ds(off, ATT_BLK), :] += _tn(dss[0], qrb[0]) + _tn(dss[1], qrb[1])
            out = []
            for hh in both:
                out += [carry[2 * hh] + jnp.dot(dss[hh], knb[hh], preferred_element_type=F32),
                        carry[2 * hh + 1] + jnp.dot(dss[hh], krb, preferred_element_type=F32)]
            return tuple(out)

        zero = jnp.zeros((ATT_BLK, LANES), F32)
        carry = lax.fori_loop(0, qi, lambda kb, cr: block(kb, cr, False), (zero, zero, zero, zero))
        carry = block(qi, carry, True)
        for hh in range(2):
            dqn_ref[:, lanes[hh]] = carry[2 * hh]
            dqr_ref[:, lanes[hh]] = carry[2 * hh + 1]

    blk = pl.BlockSpec((ATT_BLK, 2 * LANES), lambda p, i: (i, p))
    full = pl.BlockSpec((t, 2 * LANES), lambda p, i: (0, p))
    shared = pl.BlockSpec((t, LANES), lambda p, i: (0, 0))
    wide = jax.ShapeDtypeStruct((t, MLA_HEADS * LANES), F32)
    return pl.pallas_call(
        body, name="mla_attn_bwd", grid=(MLA_HEADS // 2, nq),
        in_specs=[blk, blk, full, shared, full, blk, blk, blk],
        out_specs=[blk, blk, full, shared, full],
        out_shape=[wide, wide, wide, jax.ShapeDtypeStruct((t, LANES), F32), wide],
        compiler_params=pltpu.CompilerParams(dimension_semantics=("arbitrary", "arbitrary")),
    )(qn, qr, kn, kr, v, o, lse, do)


@jax.custom_vjp
def _mla_attention(qn, qr, kn, kr, v):
    return _mla_fwd(qn, qr, kn, kr, v)[0]


def _mla_attention_fwd(qn, qr, kn, kr, v):
    o, lse = _mla_fwd(qn, qr, kn, kr, v)
    return o, (qn, qr, kn, kr, v, o, lse)


def _mla_attention_bwd(res, do):
    return tuple(_mla_bwd(*res, do))


_mla_attention.defvjp(_mla_attention_fwd, _mla_attention_bwd)


def _split_cols(x, cuts):
    cuts = tuple(cuts)

    @jax.custom_vjp
    def op(x):
        return tuple(x[:, a:b] for a, b in zip((0,) + cuts, cuts + (x.shape[1],)))

    def fwd(x):
        return op(x), None

    def bwd(_, cts):
        return (jnp.concatenate(cts, axis=1),)

    op.defvjp(fwd, bwd)
    return op(x)


def _swap_halves(w):
    half = w.shape[-1] // 2
    return jnp.concatenate([w[..., half:], w[..., :half]], axis=-1)


def _pad_lanes(w):
    return jnp.concatenate([w, jnp.zeros(w.shape[:-1] + (LANES - w.shape[-1],), w.dtype)], axis=-1)


def _join_cols(shards):
    return shards.transpose(1, 0, 2).reshape(shards.shape[1], -1)


def _local_loss(x, mod, p, cos, sin, target):
    shift1, scale1, gate1, shift2, scale2, gate2 = [mod[:, i * D_MODEL:(i + 1) * D_MODEL] for i in range(N_MOD)]

    w_in = _join_cols(p["w_in"])
    k_rope_w = w_in[:, 2176:2240]
    w_in_ext = jnp.concatenate([w_in[:, :2176], _pad_lanes(k_rope_w), _pad_lanes(_swap_halves(k_rope_w)),
                                jnp.zeros((D_MODEL, LANES), w_in.dtype)], axis=1)
    (h1,) = _make_rowwise("pre_attn", _f_pre_attn, 1, 3, [D_MODEL], [True])(x, p["norm_attn"], scale1, shift1)
    proj = _make_linear("in_proj", 512, 640)(h1, w_in_ext)
    q_sb, k_sb, v_sb, cq, ckv, kr, kr_sw, _ = _split_cols(proj, (512, 1024, 1536, 1920, 2176, 2304, 2432))

    o_sb = _sb_attention(q_sb, k_sb, v_sb)

    wq = _join_cols(p["w_q_up"]).reshape(MLA_Q_RANK, MLA_HEADS, MLA_QK)
    wq_n, wq_r = wq[:, :, :MLA_NOPE], wq[:, :, MLA_NOPE:]
    w_q_ext = jnp.concatenate([wq_n.reshape(MLA_Q_RANK, -1), _pad_lanes(wq_r).reshape(MLA_Q_RANK, -1),
                               _pad_lanes(_swap_halves(wq_r)).reshape(MLA_Q_RANK, -1)], axis=1)
    wkv = _join_cols(p["w_kv_up"]).reshape(MLA_KV_RANK, MLA_HEADS, MLA_NOPE + MLA_V)
    w_kv_ext = jnp.concatenate([wkv[:, :, :MLA_NOPE].reshape(MLA_KV_RANK, -1),
                                wkv[:, :, MLA_NOPE:].reshape(MLA_KV_RANK, -1)], axis=1)
    cqn, ckvn = _make_rowwise("mla_a", _f_mla_a, 2, 2, [MLA_Q_RANK, MLA_KV_RANK], [True, True])(
        cq, ckv, p["q_a_norm"], p["kv_a_norm"])
    qall = _make_linear("q_up", 384, 768)(cqn, w_q_ext)
    kvall = _make_linear("kv_up", 256, 1024)(ckvn, w_kv_ext)
    kn_all, v_mla = _split_cols(kvall, (512,))
    gq = p["q_norm"]
    gkr = p["k_rope_norm"]
    qn, qr, kn, krr = _make_rowwise("mla_b", _f_mla_b, 6, 6, [512, 512, 512, LANES],
                                    [True, True, True, True, False, False])(
        qall, kn_all, kr, kr_sw, cos, sin,
        gq[:, :MLA_NOPE], _pad_lanes(gq[:, MLA_NOPE:]), _pad_lanes(_swap_halves(gq[:, MLA_NOPE:])),
        p["k_nope_norm"], _pad_lanes(gkr), _pad_lanes(_swap_halves(gkr)))
    o_mla = _mla_attention(qn, qr, kn, krr, v_mla)

    (mixed,) = _make_rowwise("post_attn", _f_post_attn, 2, 2, [D_MODEL], [True, True])(
        o_sb, o_mla, p["out_norm_sb"], p["out_norm_mla"])
    attn = _make_linear("out_proj", 512, 512)(mixed, p["w_out"].reshape(D_MODEL, D_MODEL))

    x2, h2 = _make_rowwise("pre_ffn", _f_pre_ffn, 2, 4, [D_MODEL, D_MODEL], [True, True])(
        x, attn, gate1, p["norm_ffn"], scale2, shift2)
    gt = _make_linear_sharded("ffn_gate", 512)(h2, p["w_gate"])
    up = _make_linear_sharded("ffn_up", 512)(h2, p["w_up"])
    (act,) = _make_rowwise("swiglu", _f_swiglu, 2, 0, [N_CHIPS * FF_SHARD_PAD], [True, True])(gt, up)
    ffn = _make_linear("ffn_down", 256, 1024)(act, p["w_down"].reshape(N_CHIPS * FF_SHARD_PAD, D_MODEL))
    (row_loss,) = _make_rowwise("loss", _f_loss, 3, 1, [1], [True, True, False])(x2, ffn, target, gate2)
    return 0.5 * jnp.sum(row_loss)


def _my_place():
    return lax.axis_index("x"), lax.axis_index("y"), lax.axis_index("c")


def _all_gather_small(block, name):
    m_per, n = block.shape

    def body(x_ref, out_ref, send_sems, recv_sems, local_sem):
        x, y, c = _my_place()
        me, sibling = (x, y, c), (x, y, 1 - c)
        chips = [(1 - x, y), (x, 1 - y), (1 - x, 1 - y)]

        def rows(px, py, pc):
            return out_ref.at[pl.ds((4 * px + 2 * py + pc) * m_per, m_per), :]

        def copy(k, blk, to, src=None):
            return pltpu.make_async_remote_copy(
                src_ref=rows(*blk) if src is None else src, dst_ref=rows(*blk),
                send_sem=send_sems.at[k], recv_sem=recv_sems.at[k], device_id=to, device_id_type=MESH)

        mine = pltpu.make_async_copy(x_ref, rows(*me), local_sem)
        mine.start()
        first = [copy(0, me, sibling, src=x_ref)]
        first += [copy(1 + j, me, (*chip, c), src=x_ref) for j, chip in enumerate(chips)]
        for cp in first:
            cp.start()
        passed = [copy(4 + j, (*chip, c), sibling) for j, chip in enumerate(chips)]
        for j, chip in enumerate(chips):
            copy(1 + j, (*chip, c), me).wait_recv()
            passed[j].start()
        copy(0, sibling, me).wait_recv()
        for j, chip in enumerate(chips):
            copy(4 + j, (*chip, 1 - c), me).wait_recv()
        for cp in first + passed:
            cp.wait_send()
        mine.wait()

    return pl.pallas_call(
        body, name=name,
        out_shape=jax.ShapeDtypeStruct((N_DEV * m_per, n), block.dtype),
        in_specs=[pl.BlockSpec(memory_space=pltpu.VMEM)],
        out_specs=pl.BlockSpec(memory_space=pltpu.VMEM),
        scratch_shapes=[pltpu.SemaphoreType.DMA((7,)), pltpu.SemaphoreType.DMA((7,)), pltpu.SemaphoreType.DMA],
    )(block)


BIG = ("w_in", "w_q_up", "w_kv_up", "w_out", "w_gate", "w_up", "w_down")
HALF_AXIS = {"w_in": 0, "w_q_up": 0, "w_kv_up": 0, "w_out": 0, "w_gate": 0, "w_up": 0, "w_down": 1}


def _half(ref, h, axis, lead=()):
    trail = ref.shape[len(lead):]
    idx = list(lead) + [slice(None)] * len(trail)
    at = len(trail) - 2 + axis
    n2 = trail[at] // 2
    idx[len(lead) + at] = pl.ds(h * n2, n2)
    return ref.at[tuple(idx)]


def _half_shape(shape, axis):
    shape = list(shape)
    shape[len(shape) - 2 + axis] //= 2
    return tuple(shape)


def _remote(src, dst, send_sems, recv_sems, k, to):
    return pltpu.make_async_remote_copy(src_ref=src, dst_ref=dst, send_sem=send_sems.at[k],
                                        recv_sem=recv_sems.at[k], device_id=to, device_id_type=MESH)


def _gather_weights(shards):
    n_w = len(shards)
    axes = [HALF_AXIS[n] for n in BIG]

    def body(*refs):
        w_refs, out_refs = refs[:n_w], refs[n_w:2 * n_w]
        send_sems, recv_sems, local_sems = refs[2 * n_w:]
        x, y, c = _my_place()
        sibling = (x, y, 1 - c)
        chips = [(1 - x, y), (x, 1 - y), (1 - x, 1 - y)]
        me = 2 * x + y
        mine = [pltpu.make_async_copy(w, o.at[me], local_sems.at[i]) for i, (w, o) in enumerate(zip(w_refs, out_refs))]
        for cp in mine:
            cp.start()
        first = [_remote(_half(w_refs[i], c, axes[i]), _half(out_refs[i], c, axes[i], (me,)),
                         send_sems, recv_sems, 6 * i + j, (*chip, c))
                 for i in range(n_w) for j, chip in enumerate(chips)]
        for cp in first:
            cp.start()
        passed = []
        for j, (cx, cy) in enumerate(chips):
            for i in range(n_w):
                blk = _half(out_refs[i], c, axes[i], (2 * cx + cy,))
                _remote(blk, blk, send_sems, recv_sems, 6 * i + j, (cx, cy, c)).wait_recv()
                cp = _remote(blk, blk, send_sems, recv_sems, 6 * i + 3 + j, sibling)
                cp.start()
                passed.append(cp)
        for j, (cx, cy) in enumerate(chips):
            for i in range(n_w):
                blk = _half(out_refs[i], 1 - c, axes[i], (2 * cx + cy,))
                _remote(blk, blk, send_sems, recv_sems, 6 * i + 3 + j, sibling).wait_recv()
        for cp in first + passed:
            cp.wait_send()
        for cp in mine:
            cp.wait()

    return pl.pallas_call(
        body, name="gather_weights",
        out_shape=[jax.ShapeDtypeStruct((N_CHIPS,) + s.shape, s.dtype) for s in shards],
        in_specs=[ANY] * n_w, out_specs=[ANY] * n_w,
        scratch_shapes=[pltpu.SemaphoreType.DMA((6 * n_w,)), pltpu.SemaphoreType.DMA((6 * n_w,)),
                        pltpu.SemaphoreType.DMA((n_w,))],
    )(*shards)


def _pair_exchange(grads):
    n_w = len(grads)
    axes = [HALF_AXIS[n] for n in BIG]

    def body(*refs):
        g_refs, t_refs = refs[:n_w], refs[n_w:2 * n_w]
        send_sems, recv_sems = refs[2 * n_w:]
        x, y, c = _my_place()
        sends = [_remote(_half(g_refs[i], 1 - c, axes[i]), t_refs[i], send_sems, recv_sems, i, (x, y, 1 - c))
                 for i in range(n_w)]
        for cp in sends:
            cp.start()
        for cp in sends:
            cp.wait_recv()
        for cp in sends:
            cp.wait_send()

    return pl.pallas_call(
        body, name="grad_pair_exchange",
        out_shape=[jax.ShapeDtypeStruct(_half_shape(g.shape, a), g.dtype) for g, a in zip(grads, axes)],
        in_specs=[ANY] * n_w, out_specs=[ANY] * n_w,
        scratch_shapes=[pltpu.SemaphoreType.DMA((n_w,)), pltpu.SemaphoreType.DMA((n_w,))],
    )(*grads)


def _chip_scatter(pair_sums):
    n_w = len(pair_sums)

    def body(*refs):
        s_refs, p_refs = refs[:n_w], refs[n_w:2 * n_w]
        send_sems, recv_sems = refs[2 * n_w:]
        x, y, c = _my_place()
        chips = [(1 - x, y), (x, 1 - y), (1 - x, 1 - y)]
        sends = [_remote(s_refs[i].at[2 * cx + cy], p_refs[i].at[j], send_sems, recv_sems, 3 * i + j, (cx, cy, c))
                 for i in range(n_w) for j, (cx, cy) in enumerate(chips)]
        for cp in sends:
            cp.start()
        for cp in sends:
            cp.wait_recv()
        for cp in sends:
            cp.wait_send()

    return pl.pallas_call(
        body, name="grad_chip_scatter",
        out_shape=[jax.ShapeDtypeStruct((N_CHIPS - 1,) + s.shape[1:], s.dtype) for s in pair_sums],
        in_specs=[ANY] * n_w, out_specs=[ANY] * n_w,
        scratch_shapes=[pltpu.SemaphoreType.DMA((3 * n_w,)), pltpu.SemaphoreType.DMA((3 * n_w,))],
    )(*pair_sums)


def _sibling_join(halves):
    n_w = len(halves)

    def body(*refs):
        s_refs, j_refs = refs[:n_w], refs[n_w:2 * n_w]
        send_sems, recv_sems = refs[2 * n_w:]
        x, y, c = _my_place()
        sends = [_remote(s_refs[i], j_refs[i], send_sems, recv_sems, i, (x, y, 1 - c)) for i in range(n_w)]
        for cp in sends:
            cp.start()
        for cp in sends:
            cp.wait_recv()
        for cp in sends:
            cp.wait_send()

    return pl.pallas_call(
        body, name="grad_sibling_join",
        out_shape=[jax.ShapeDtypeStruct(s.shape, s.dtype) for s in halves],
        in_specs=[ANY] * n_w, out_specs=[ANY] * n_w,
        scratch_shapes=[pltpu.SemaphoreType.DMA((n_w,)), pltpu.SemaphoreType.DMA((n_w,))],
    )(*halves)


def _row_tile(rows, mult=16):
    return max(d for d in range(mult, ROW_TILE + 1, mult) if rows % d == 0)


def _pair_sum(place, g, theirs, axis, name):
    nj, rr, cc = theirs.shape
    tr = _row_tile(rr)
    nb = rr // tr
    if axis == 0:
        g_map = lambda j, i, pr: (j, pr[0] * nb + i, 0)
    else:
        g_map = lambda j, i, pr: (j, i, pr[0])

    def body(pr, g_ref, t_ref, o_ref):
        o_ref[...] = (g_ref[...].astype(F32) + t_ref[...].astype(F32)).astype(BF16)

    spec = pl.BlockSpec((None, tr, cc), lambda j, i, pr: (j, i, 0))
    return pl.pallas_call(
        body, name=name,
        grid_spec=pltpu.PrefetchScalarGridSpec(
            num_scalar_prefetch=1, grid=(nj, nb),
            in_specs=[pl.BlockSpec((None, tr, cc), g_map), spec], out_specs=spec),
        out_shape=jax.ShapeDtypeStruct(theirs.shape, BF16))(place, g, theirs)


def _chip_sum(place, pair_sums, parts, name):
    _, rr, cc = parts.shape
    tr = _row_tile(rr)

    def body(pr, h_ref, p_ref, o_ref):
        acc = p_ref[0].astype(F32)
        for j in range(1, N_CHIPS - 1):
            acc = acc + p_ref[j].astype(F32)
        o_ref[...] = (acc + h_ref[...].astype(F32)).astype(BF16)

    return pl.pallas_call(
        body, name=name,
        grid_spec=pltpu.PrefetchScalarGridSpec(
            num_scalar_prefetch=1, grid=(rr // tr,),
            in_specs=[pl.BlockSpec((None, tr, cc), lambda i, pr: (pr[1], i, 0)),
                      pl.BlockSpec((N_CHIPS - 1, tr, cc), lambda i, pr: (0, i, 0))],
            out_specs=pl.BlockSpec((tr, cc), lambda i, pr: (i, 0))),
        out_shape=jax.ShapeDtypeStruct((rr, cc), BF16))(place, pair_sums, parts)


def _silu(v):
    return v / (1.0 + jnp.exp(-v))


def _ada_fwd(c_all, w_shard, b_shard):
    def body(c_ref, w_ref, b_ref, o_ref):
        o_ref[...] = jnp.dot(_silu(c_ref[...]), w_ref[...], precision=lax.Precision.HIGHEST,
                             preferred_element_type=F32) + b_ref[...]

    return pl.pallas_call(body, name="ada_fwd", out_shape=jax.ShapeDtypeStruct((c_all.shape[0], w_shard.shape[1]), F32),
                          compiler_params=pltpu.CompilerParams(vmem_limit_bytes=MM_VMEM_LIMIT))(c_all, w_shard, b_shard)


def _ada_bwd(c_all, dmod_cols):
    def body(c_ref, d_ref, o_ref):
        o_ref[...] = lax.dot_general(_silu(c_ref[...]), d_ref[...], (((0,), (0,)), ((), ())),
                                     precision=lax.Precision.HIGHEST, preferred_element_type=F32)

    return pl.pallas_call(body, name="ada_bwd", out_shape=jax.ShapeDtypeStruct((c_all.shape[1], dmod_cols.shape[1]), F32),
                          compiler_params=pltpu.CompilerParams(vmem_limit_bytes=MM_VMEM_LIMIT))(c_all, dmod_cols)


def _adamw_math(w, g, m, v):
    m = ADAM_B1 * m + (1.0 - ADAM_B1) * g
    v = ADAM_B2 * v + (1.0 - ADAM_B2) * (g * g)
    m_hat = m / (1.0 - ADAM_B1 ** ADAM_STEP)
    v_hat = v / (1.0 - ADAM_B2 ** ADAM_STEP)
    delta = -ADAM_LR * (m_hat / (jnp.sqrt(v_hat) + ADAM_EPS) + ADAM_WD * w)
    return delta, m, v


def _adamw(w, g, m, v, name):
    r, ccols = w.shape
    tr = max(d for d in range(8, ROW_TILE + 1, 8) if r % d == 0)
    spec = pl.BlockSpec((tr, ccols), lambda i: (i, 0))

    def body(w_ref, g_ref, m_ref, v_ref, d_ref, nm_ref, nv_ref):
        d_ref[...], nm_ref[...], nv_ref[...] = _adamw_math(w_ref[...], g_ref[...], m_ref[...], v_ref[...])

    return pl.pallas_call(body, name=name, grid=(r // tr,), in_specs=[spec] * 4, out_specs=[spec] * 3,
                          out_shape=[jax.ShapeDtypeStruct(w.shape, F32)] * 3,
                          compiler_params=pltpu.CompilerParams(vmem_limit_bytes=MM_VMEM_LIMIT))(w, g, m, v)


def _adamw_small(w, g_all, m, v):
    def body(w_ref, g_ref, m_ref, v_ref, gs_ref, d_ref, nm_ref, nv_ref):
        g = g_ref[0]
        for d in range(1, N_DEV):
            g = g + g_ref[d]
        gs_ref[...] = g
        d_ref[...], nm_ref[...], nv_ref[...] = _adamw_math(w_ref[...], g, m_ref[...], v_ref[...])

    return pl.pallas_call(body, name="adamw_small", out_shape=[jax.ShapeDtypeStruct(w.shape, F32)] * 4)(w, g_all, m, v)


def _adamw_halves(place, w, own, sib, m, v, axis, name):
    r, cc = w.shape
    if axis == 0:
        rows, gc = own.shape[0], own.shape[1]
        tr = _row_tile(rows)
        nb = rows // tr
        w_spec = pl.BlockSpec((tr, cc), lambda h, i, pr: (h * nb + i, 0))
        g_spec = pl.BlockSpec((tr, gc), lambda h, i, pr: (i, 0))
    else:
        tr = _row_tile(r)
        nb = r // tr
        gc = own.shape[1]
        w_spec = pl.BlockSpec((tr, gc), lambda h, i, pr: (i, h))
        g_spec = pl.BlockSpec((tr, gc), lambda h, i, pr: (i, 0))
    wc = w_spec.block_shape[1]

    def body(pr, w_ref, o_ref, s_ref, m_ref, v_ref, g_ref, d_ref, nm_ref, nv_ref):
        g = jnp.where(pl.program_id(0) == pr[0], o_ref[...], s_ref[...]).astype(F32)[:, :wc]
        g_ref[...] = g
        d_ref[...], nm_ref[...], nv_ref[...] = _adamw_math(w_ref[...], g, m_ref[...], v_ref[...])

    return pl.pallas_call(
        body, name=name,
        grid_spec=pltpu.PrefetchScalarGridSpec(
            num_scalar_prefetch=1, grid=(2, nb),
            in_specs=[w_spec, g_spec, g_spec, w_spec, w_spec], out_specs=[w_spec] * 4),
        out_shape=[jax.ShapeDtypeStruct(w.shape, F32)] * 4,
        compiler_params=pltpu.CompilerParams(vmem_limit_bytes=MM_VMEM_LIMIT))(place, w, own, sib, m, v)


SMALL = ("b_ada", "norm_attn", "norm_ffn", "q_a_norm", "kv_a_norm", "q_norm", "k_nope_norm", "k_rope_norm",
         "out_norm_sb", "out_norm_mla")
WEIGHTS = ("w_ada", "b_ada", "norm_attn", "norm_ffn", "w_in", "q_a_norm", "w_q_up", "kv_a_norm", "w_kv_up",
           "q_norm", "k_nope_norm", "k_rope_norm", "out_norm_sb", "out_norm_mla", "w_out", "w_gate", "w_up",
           "w_down")


def kernel(x, c, positions, w_ada, b_ada, norm_attn, norm_ffn, w_in, q_a_norm, w_q_up, kv_a_norm, w_kv_up, q_norm, k_nope_norm, k_rope_norm, out_norm_sb, out_norm_mla, w_out, w_gate, w_up, w_down, loss_target, m_w_ada, m_b_ada, m_norm_attn, m_norm_ffn, m_w_in, m_q_a_norm, m_w_q_up, m_kv_a_norm, m_w_kv_up, m_q_norm, m_k_nope_norm, m_k_rope_norm, m_out_norm_sb, m_out_norm_mla, m_w_out, m_w_gate, m_w_up, m_w_down, v_w_ada, v_b_ada, v_norm_attn, v_norm_ffn, v_w_in, v_q_a_norm, v_w_q_up, v_kv_a_norm, v_w_kv_up, v_q_norm, v_k_nope_norm, v_k_rope_norm, v_out_norm_sb, v_out_norm_mla, v_w_out, v_w_gate, v_w_up, v_w_down):
    local = dict(locals())
    w = {n: local[n][0] for n in WEIGHTS}
    m = {n: local["m_" + n][0] for n in WEIGHTS}
    v = {n: local["v_" + n][0] for n in WEIGHTS}
    small = {n: w[n].reshape(1, -1) for n in SMALL}
    ix, iy, ic = _my_place()
    chip = 2 * ix + iy
    dev = 2 * chip + ic
    xs, target = x[0], loss_target[0]
    seq = xs.shape[0]

    ff_pad = FF_SHARD_PAD - FF_SHARD
    pads = {"w_gate": ((0, 0), (0, ff_pad)), "w_up": ((0, 0), (0, ff_pad)), "w_down": ((0, ff_pad), (0, 0))}
    shards = [jnp.pad(w[n].astype(BF16), pads[n]) if n in pads else w[n].astype(BF16) for n in BIG]
    gathered = dict(zip(BIG, _gather_weights(shards)))

    c_all = _all_gather_small(c.reshape(8, LANES), "gather_c").reshape(N_DEV, D_MODEL)
    ada_cols = w["w_ada"].shape[1]
    b_cols = lax.dynamic_slice_in_dim(small["b_ada"], chip * ada_cols, ada_cols, axis=1)
    mod_cols = _ada_fwd(c_all, w["w_ada"], b_cols)
    mod_all = _all_gather_small(mod_cols, "gather_mod").reshape(N_CHIPS, 2, N_DEV, ada_cols)
    mod = lax.dynamic_index_in_dim(mod_all[:, 0], dev, axis=1, keepdims=False).reshape(1, N_MOD * D_MODEL)

    half = MLA_ROPE // 2
    freqs = 1.0 / (ROPE_THETA ** (np.arange(half, dtype=np.float32) / half))
    zeros = np.zeros(LANES - MLA_ROPE, np.float32)
    freqs_row = jnp.asarray(np.concatenate([freqs, freqs, zeros]).astype(np.float32)[None])
    sign_row = jnp.asarray(np.concatenate([-np.ones(half), np.ones(half), zeros]).astype(np.float32)[None])
    cos, sin = _rope_tables(positions.reshape(seq, 1), freqs_row, sign_row)

    params = dict(gathered)
    params.update({n: small[n] for n in SMALL if n != "b_ada"})
    loss_part, (gx, gmod, gp) = jax.value_and_grad(_local_loss, argnums=(0, 1, 2))(xs, mod, params, cos, sin, target)
    loss = lax.psum(loss_part, ("x", "y", "c"))

    place = jnp.stack([ic, chip]).astype(jnp.int32)
    axes = [HALF_AXIS[n] for n in BIG]
    grads = [gp[n] for n in BIG]
    theirs = _pair_exchange(grads)
    pair_sums = [_pair_sum(place, gr, th, a, "grad_pair_sum_" + n) for n, gr, th, a in zip(BIG, grads, theirs, axes)]
    parts = _chip_scatter(pair_sums)
    own = [_chip_sum(place, ps, pt, "grad_chip_sum_" + n) for n, ps, pt in zip(BIG, pair_sums, parts)]
    sib = _sibling_join(own)
    g = {}

    small_names = [n for n in SMALL if n != "b_ada"]
    small_vec = jnp.concatenate([gmod] + [gp[n] for n in small_names], axis=1)
    n_small = small_vec.shape[1]
    small_all = _all_gather_small(small_vec.reshape(8, n_small // 8), "gather_small").reshape(N_DEV, 8, n_small // 8)

    def pack_small(d):
        return jnp.concatenate([d[n].reshape(1, -1) for n in SMALL], axis=1).reshape(8, n_small // 8)

    gs, ds, ms, vs = _adamw_small(pack_small(w), small_all, pack_small(m), pack_small(v))
    sizes = [w[n].size for n in SMALL]
    offs = np.concatenate([[0], np.cumsum(sizes)])

    def unpack_small(a):
        flat = a.reshape(-1)
        return {n: flat[offs[i]:offs[i + 1]].reshape(w[n].shape) for i, n in enumerate(SMALL)}

    g.update(unpack_small(gs))
    delta, new_m, new_v = unpack_small(ds), unpack_small(ms), unpack_small(vs)

    dmod_all = small_all.reshape(N_DEV, n_small)[:, :N_MOD * D_MODEL]
    g["w_ada"] = _ada_bwd(c_all, lax.dynamic_slice_in_dim(dmod_all, chip * ada_cols, ada_cols, axis=1))

    delta["w_ada"], new_m["w_ada"], new_v["w_ada"] = _adamw(w["w_ada"], g["w_ada"], m["w_ada"], v["w_ada"], "adamw_w_ada")
    for n, o, s, a in zip(BIG, own, sib, axes):
        g[n], delta[n], new_m[n], new_v[n] = _adamw_halves(place, w[n], o, s, m[n], v[n], a, "adamw_" + n)

    def outs(d):
        return [d[n][None] for n in WEIGHTS]

    return (loss, gx[None], *outs(g), *outs(delta), *outs(new_m), *outs(new_v))
```

```python
import functools
import math

import numpy as np
import jax
import jax.numpy as jnp
from jax import lax
from jax.experimental import pallas as pl
from jax.experimental.pallas import tpu as pltpu

F32 = jnp.float32
BF16 = jnp.bfloat16
MESH = pl.DeviceIdType.MESH
ANY = pl.BlockSpec(memory_space=pl.ANY)

D_MODEL = 1024
SB_HEADS = 8
SB_HEAD_DIM = 64
SB_WIDTH = 512
MLA_HEADS = 4
MLA_NOPE = 128
MLA_ROPE = 64
MLA_QK = 192
MLA_V = 128
MLA_Q_RANK = 384
MLA_KV_RANK = 256
D_FF = 2816
N_MOD = 6
ROPE_THETA = 10000.0
EPS = 1e-6
LANES = 128

ADAM_LR = 0.001
ADAM_B1 = 0.9
ADAM_B2 = 0.999
ADAM_EPS = 1e-08
ADAM_WD = 0.01
ADAM_STEP = 10

N_CHIPS = 4
N_DEV = 8
ROW_TILE = 256
ATT_BLK = 256
MM_VMEM_LIMIT = 48 * 1024 * 1024
FF_SHARD = D_FF // N_CHIPS
FF_SHARD_PAD = 768


def _mm(a, b, mode, name, tm, tn, out_dtype=F32):
    if mode == "nn":
        (m, k), n = a.shape, b.shape[1]
        a_spec = pl.BlockSpec((tm, k), lambda j, i: (i, 0))
        b_spec = pl.BlockSpec((k, tn), lambda j, i: (0, j))
        dims = (((1,), (0,)), ((), ()))
    elif mode == "nt":
        (m, k), n = a.shape, b.shape[0]
        a_spec = pl.BlockSpec((tm, k), lambda j, i: (i, 0))
        b_spec = pl.BlockSpec((tn, k), lambda j, i: (j, 0))
        dims = (((1,), (1,)), ((), ()))
    else:
        (k, m), n = a.shape, b.shape[1]
        a_spec = pl.BlockSpec((k, tm), lambda j, i: (0, i))
        b_spec = pl.BlockSpec((k, tn), lambda j, i: (0, j))
        dims = (((0,), (0,)), ((), ()))
    assert m % tm == 0 and n % tn == 0, (name, m, n, tm, tn)

    def body(a_ref, b_ref, o_ref):
        o_ref[...] = lax.dot_general(a_ref[...].astype(BF16), b_ref[...].astype(BF16), dims,
                                     preferred_element_type=F32).astype(out_dtype)

    return pl.pallas_call(
        body, name=name, grid=(n // tn, m // tm),
        in_specs=[a_spec, b_spec],
        out_specs=pl.BlockSpec((tm, tn), lambda j, i: (i, j)),
        out_shape=jax.ShapeDtypeStruct((m, n), out_dtype),
        compiler_params=pltpu.CompilerParams(dimension_semantics=("arbitrary", "arbitrary"),
                                             vmem_limit_bytes=MM_VMEM_LIMIT),
    )(a, b)


def _make_linear(name, tk_w, tn_w):
    @jax.custom_vjp
    def op(a, w):
        return _mm(a, w, "nn", name + "_fwd", ROW_TILE, w.shape[1])

    def fwd(a, w):
        return op(a, w), (a, w)

    def bwd(res, dy):
        a, w = res
        da = _mm(dy, w, "nt", name + "_dx", ROW_TILE, w.shape[0])
        dw = _mm(a, dy, "tn", name + "_dw", tk_w, tn_w, out_dtype=BF16)
        return da, dw

    op.defvjp(fwd, bwd)
    return op


def _make_linear_sharded(name, tk_w):
    def call_fwd(a, w):
        t, k = a.shape
        n_sh, _, cc = w.shape

        def body(a_ref, w_ref, o_ref):
            o_ref[...] = jnp.dot(a_ref[...].astype(BF16), w_ref[...], preferred_element_type=F32)

        return pl.pallas_call(
            body, name=name + "_fwd", grid=(n_sh, t // ROW_TILE),
            in_specs=[pl.BlockSpec((ROW_TILE, k), lambda j, i: (i, 0)),
                      pl.BlockSpec((None, k, cc), lambda j, i: (j, 0, 0))],
            out_specs=pl.BlockSpec((ROW_TILE, cc), lambda j, i: (i, j)),
            out_shape=jax.ShapeDtypeStruct((t, n_sh * cc), F32),
            compiler_params=pltpu.CompilerParams(dimension_semantics=("arbitrary", "arbitrary"),
                                                 vmem_limit_bytes=MM_VMEM_LIMIT),
        )(a, w)

    def call_dx(dy, w):
        t = dy.shape[0]
        n_sh, k, cc = w.shape

        def body(dy_ref, w_ref, o_ref):
            acc = jnp.zeros((ROW_TILE, k), F32)
            for j in range(n_sh):
                acc = acc + _nt(dy_ref[:, j * cc:(j + 1) * cc].astype(BF16), w_ref[j])
            o_ref[...] = acc

        return pl.pallas_call(
            body, name=name + "_dx", grid=(t // ROW_TILE,),
            in_specs=[pl.BlockSpec((ROW_TILE, n_sh * cc), lambda i: (i, 0)),
                      pl.BlockSpec((n_sh, k, cc), lambda i: (0, 0, 0))],
            out_specs=pl.BlockSpec((ROW_TILE, k), lambda i: (i, 0)),
            out_shape=jax.ShapeDtypeStruct((t, k), F32),
            compiler_params=pltpu.CompilerParams(dimension_semantics=("arbitrary",),
                                                 vmem_limit_bytes=MM_VMEM_LIMIT),
        )(dy, w)

    def call_dw(a, dy, w):
        t, k = a.shape
        n_sh, _, cc = w.shape

        def body(a_ref, dy_ref, o_ref):
            o_ref[...] = _tn(a_ref[...].astype(BF16), dy_ref[...].astype(BF16)).astype(BF16)

        return pl.pallas_call(
            body, name=name + "_dw", grid=(n_sh, k // tk_w),
            in_specs=[pl.BlockSpec((t, tk_w), lambda j, i: (0, i)),
                      pl.BlockSpec((t, cc), lambda j, i: (0, j))],
            out_specs=pl.BlockSpec((None, tk_w, cc), lambda j, i: (j, i, 0)),
            out_shape=jax.ShapeDtypeStruct(w.shape, BF16),
            compiler_params=pltpu.CompilerParams(dimension_semantics=("arbitrary", "arbitrary"),
                                                 vmem_limit_bytes=MM_VMEM_LIMIT),
        )(a, dy)

    @jax.custom_vjp
    def op(a, w):
        return call_fwd(a, w)

    def fwd(a, w):
        return op(a, w), (a, w)

    def bwd(res, dy):
        a, w = res
        return call_dx(dy, w), call_dw(a, dy, w)

    op.defvjp(fwd, bwd)
    return op


def _row_spec(arr, tb):
    return pl.BlockSpec((tb, arr.shape[1]), lambda i: (i, 0))


def _full_spec(arr):
    return pl.BlockSpec(arr.shape, lambda i: (0, 0))


def _make_rowwise(name, f, n_rows, n_params, out_cols, diff_rows):
    n_out = len(out_cols)

    def call_fwd(rows, params):
        t = rows[0].shape[0]

        def body(*refs):
            ins = [r[...] for r in refs[:n_rows + n_params]]
            outs = f(*ins)
            for o_ref, o in zip(refs[n_rows + n_params:], outs):
                o_ref[...] = o

        return pl.pallas_call(
            body, name=name + "_fwd", grid=(t // ROW_TILE,),
            in_specs=[_row_spec(a, ROW_TILE) for a in rows] + [_full_spec(p) for p in params],
            out_specs=[pl.BlockSpec((ROW_TILE, n), lambda i: (i, 0)) for n in out_cols],
            out_shape=[jax.ShapeDtypeStruct((t, n), F32) for n in out_cols],
            compiler_params=pltpu.CompilerParams(dimension_semantics=("arbitrary",),
                                                 vmem_limit_bytes=MM_VMEM_LIMIT),
        )(*rows, *params)

    def call_bwd(rows, params, cts):
        t = rows[0].shape[0]
        d_rows = [a for a, d in zip(rows, diff_rows) if d]
        n_in = n_rows + n_params + n_out

        def body(*refs):
            ins = [r[...] for r in refs[:n_rows + n_params]]
            ct = tuple(r[...] for r in refs[n_rows + n_params:n_in])
            _, vjp = jax.vjp(f, *ins)
            grads = vjp(ct)
            out_refs = refs[n_in:]
            g_rows = [g for g, d in zip(grads[:n_rows], diff_rows) if d]
            for o_ref, g in zip(out_refs[:len(g_rows)], g_rows):
                o_ref[...] = g
            p_refs = out_refs[len(g_rows):]

            if p_refs:
                @pl.when(pl.program_id(0) == 0)
                def _():
                    for p_ref in p_refs:
                        p_ref[...] = jnp.zeros_like(p_ref)

                for p_ref, g in zip(p_refs, grads[n_rows:]):
                    p_ref[...] += g

        return pl.pallas_call(
            body, name=name + "_bwd", grid=(t // ROW_TILE,),
            in_specs=[_row_spec(a, ROW_TILE) for a in rows] + [_full_spec(p) for p in params]
            + [_row_spec(c, ROW_TILE) for c in cts],
            out_specs=[_row_spec(a, ROW_TILE) for a in d_rows] + [_full_spec(p) for p in params],
            out_shape=[jax.ShapeDtypeStruct(a.shape, F32) for a in d_rows]
            + [jax.ShapeDtypeStruct(p.shape, F32) for p in params],
            compiler_params=pltpu.CompilerParams(dimension_semantics=("arbitrary",),
                                                 vmem_limit_bytes=MM_VMEM_LIMIT),
        )(*rows, *params, *cts)

    @jax.custom_vjp
    def op(*args):
        return tuple(call_fwd(args[:n_rows], args[n_rows:]))

    def fwd(*args):
        return op(*args), args

    def bwd(args, cts):
        rows, params = args[:n_rows], args[n_rows:]
        outs = call_bwd(rows, params, cts)
        it = iter(outs)
        g_rows = [next(it) if d else jnp.zeros_like(a) for a, d in zip(rows, diff_rows)]
        return tuple(g_rows) + tuple(it)

    op.defvjp(fwd, bwd)
    return op


def _rms(x, g, n):
    return x * lax.rsqrt(jnp.sum(x * x, axis=-1, keepdims=True) * (1.0 / n) + EPS) * g


def _f_pre_attn(x, g, scale, shift):
    return (_rms(x, g, D_MODEL) * (1.0 + scale) + shift,)


def _f_mla_a(cq, ckv, gq, gkv):
    return _rms(cq, gq, MLA_Q_RANK), _rms(ckv, gkv, MLA_KV_RANK)


@jax.custom_vjp
def _split_lanes(x):
    return tuple(x[:, i * LANES:(i + 1) * LANES] for i in range(x.shape[1] // LANES))


def _split_lanes_fwd(x):
    return _split_lanes(x), None


def _split_lanes_bwd(_, cts):
    return (jnp.concatenate(cts, axis=1),)


_split_lanes.defvjp(_split_lanes_fwd, _split_lanes_bwd)


def _f_mla_b(qall, kn_all, kr, kr_sw, cos, sin, gqn, gqr, gqr_sw, gkn, gkr, gkr_sw):
    q = _split_lanes(qall)
    kn = _split_lanes(kn_all)
    qn_o, qr_o, kn_o = [], [], []
    for h in range(MLA_HEADS):
        qn, qr, qs = q[h], q[MLA_HEADS + h], q[2 * MLA_HEADS + h]
        ss = jnp.sum(qn * qn, axis=-1, keepdims=True) + jnp.sum(qr * qr, axis=-1, keepdims=True)
        rs = lax.rsqrt(ss * (1.0 / MLA_QK) + EPS)
        qn_o.append(qn * rs * gqn)
        qr_o.append((qr * rs * gqr) * cos + (qs * rs * gqr_sw) * sin)
        kn_o.append(_rms(kn[h], gkn, MLA_NOPE))
    rs = lax.rsqrt(jnp.sum(kr * kr, axis=-1, keepdims=True) * (1.0 / MLA_ROPE) + EPS)
    kr_o = (kr * rs * gkr) * cos + (kr_sw * rs * gkr_sw) * sin
    return (jnp.concatenate(qn_o, axis=1), jnp.concatenate(qr_o, axis=1), jnp.concatenate(kn_o, axis=1), kr_o)


def _f_post_attn(o_sb, o_mla, g_sb, g_mla):
    return (jnp.concatenate([_rms(o_sb, g_sb, SB_WIDTH), _rms(o_mla, g_mla, SB_WIDTH)], axis=1),)


def _f_pre_ffn(x, attn, gate, g, scale, shift):
    x2 = x + gate * attn
    return x2, _rms(x2, g, D_MODEL) * (1.0 + scale) + shift


def _f_swiglu(gt, up):
    return (gt / (1.0 + jnp.exp(-gt)) * up,)


def _f_loss(x2, ffn, target, gate):
    err = x2 + gate * ffn - target
    return (jnp.sum(err * err, axis=-1, keepdims=True) * (1.0 / D_MODEL),)


def _rope_tables(pos_col, freqs, sign):
    t = pos_col.shape[0]

    def body(p_ref, f_ref, s_ref, cos_ref, sin_ref):
        ang = p_ref[...].astype(F32) * f_ref[...]
        live = jnp.abs(s_ref[...])
        cos_ref[...] = jnp.cos(ang) * live
        sin_ref[...] = jnp.sin(ang) * s_ref[...]

    return pl.pallas_call(
        body, name="rope_tables", grid=(t // ROW_TILE,),
        in_specs=[pl.BlockSpec((ROW_TILE, 1), lambda i: (i, 0)), _full_spec(freqs), _full_spec(sign)],
        out_specs=[pl.BlockSpec((ROW_TILE, LANES), lambda i: (i, 0))] * 2,
        out_shape=[jax.ShapeDtypeStruct((t, LANES), F32)] * 2,
    )(pos_col, freqs, sign)


def _hi_lo_dot(x, tri):
    hi = x.astype(BF16)
    lo = (x - hi.astype(F32)).astype(BF16)
    return (jnp.dot(hi, tri, preferred_element_type=F32) + jnp.dot(lo, tri, preferred_element_type=F32))


def _tri(cmp):
    r = lax.broadcasted_iota(jnp.int32, (ATT_BLK, ATT_BLK), 0)
    c = lax.broadcasted_iota(jnp.int32, (ATT_BLK, ATT_BLK), 1)
    return cmp(r, c).astype(BF16)


def _nt(a, b):
    return lax.dot_general(a, b, (((1,), (1,)), ((), ())), preferred_element_type=F32)


def _tn(a, b):
    return lax.dot_general(a, b, (((0,), (0,)), ((), ())), preferred_element_type=F32)


def _sb_logs(z):
    lb = jnp.minimum(z, 0.0) - jnp.log(1.0 + jnp.exp(-jnp.abs(z)))
    return lb, lb - z


def _sb_fwd(q, k, v):
    t = q.shape[0]
    nq = t // ATT_BLK
    scale = SB_HEAD_DIM ** -0.5

    def body(q_ref, k_ref, v_ref, o_ref, tot_ref):
        qi = pl.program_id(1)
        lane = lax.broadcasted_iota(jnp.int32, (ATT_BLK, LANES), 1)
        tri = _tri(lambda r, c: r > c)
        qv = q_ref[...] * scale
        heads = [(lane // SB_HEAD_DIM) == hh for hh in range(2)]
        qms = [jnp.where(mine, qv, 0.0).astype(BF16) for mine in heads]

        def block(kb, carry, diagonal):
            acc, runs = carry[0], carry[1:]
            off = pl.multiple_of(kb * ATT_BLK, ATT_BLK)
            kk = k_ref[pl.ds(off, ATT_BLK), :].astype(BF16)
            v_blk = v_ref[pl.ds(off, ATT_BLK), :]
            both = range(2)
            if diagonal:
                valid = (lax.broadcasted_iota(jnp.int32, (ATT_BLK, ATT_BLK), 1)
                         < lax.broadcasted_iota(jnp.int32, (ATT_BLK, ATT_BLK), 0))
            zs = [_nt(qms[hh], kk) for hh in both]
            vvs = [jnp.where(heads[hh], v_blk, 0.0).astype(BF16) for hh in both]
            logs = [_sb_logs(z) for z in zs]
            l1ms = [jnp.where(valid, lg[1], 0.0) for lg in logs] if diagonal else [lg[1] for lg in logs]
            afters = [_hi_lo_dot(l1ms[hh], tri) for hh in both]
            ws = [jnp.exp(logs[hh][0] + (afters[hh] + runs[hh])) for hh in both]
            if diagonal:
                ws = [jnp.where(valid, w, 0.0) for w in ws]
            acc = acc + jnp.dot(ws[0].astype(BF16), vvs[0], preferred_element_type=F32) + jnp.dot(
                ws[1].astype(BF16), vvs[1], preferred_element_type=F32)
            return (acc, *[runs[hh] + jnp.sum(l1ms[hh], axis=-1, keepdims=True) for hh in both])

        zero = jnp.zeros((ATT_BLK, 1), F32)
        carry = block(qi, (jnp.zeros((ATT_BLK, LANES), F32), zero, zero), True)
        carry = lax.fori_loop(0, qi, lambda j, cr: block(qi - 1 - j, cr, False), carry)
        o_ref[...] = carry[0]
        for hh in range(2):
            tot_ref[:, hh * LANES:(hh + 1) * LANES] = jnp.broadcast_to(carry[1 + hh], (ATT_BLK, LANES))

    return pl.pallas_call(
        body, name="sb_attn_fwd", grid=(SB_HEADS // 2, nq),
        in_specs=[pl.BlockSpec((ATT_BLK, LANES), lambda p, i: (i, p)),
                  pl.BlockSpec((t, LANES), lambda p, i: (0, p)),
                  pl.BlockSpec((t, LANES), lambda p, i: (0, p))],
        out_specs=[pl.BlockSpec((ATT_BLK, LANES), lambda p, i: (i, p)),
                   pl.BlockSpec((ATT_BLK, 2 * LANES), lambda p, i: (i, p))],
        out_shape=[jax.ShapeDtypeStruct((t, SB_WIDTH), F32), jax.ShapeDtypeStruct((t, SB_HEADS * LANES), F32)],
        compiler_params=pltpu.CompilerParams(dimension_semantics=("arbitrary", "arbitrary")),
    )(q, k, v)


def _sb_bwd(q, k, v, tot, do):
    t = q.shape[0]
    nq = t // ATT_BLK
    scale = SB_HEAD_DIM ** -0.5

    def body(q_ref, k_ref, v_ref, tot_ref, do_ref, dq_ref, dk_ref, dv_ref):
        qi = pl.program_id(1)

        @pl.when(qi == 0)
        def _():
            dk_ref[...] = jnp.zeros_like(dk_ref)
            dv_ref[...] = jnp.zeros_like(dv_ref)

        lane = lax.broadcasted_iota(jnp.int32, (ATT_BLK, LANES), 1)
        tri_incl = _tri(lambda r, c: r <= c)
        tri_lt = _tri(lambda r, c: r < c)
        qv = q_ref[...] * scale
        dov = do_ref[...]
        heads = [(lane // SB_HEAD_DIM) == hh for hh in range(2)]
        qms = [jnp.where(mine, qv, 0.0).astype(BF16) for mine in heads]
        doms = [jnp.where(mine, dov, 0.0).astype(BF16) for mine in heads]
        tots = [tot_ref[:, hh * LANES:hh * LANES + 1] for hh in range(2)]

        def block(kb, carry, diagonal):
            dq = carry[0]
            off = pl.multiple_of(kb * ATT_BLK, ATT_BLK)
            k_blk = k_ref[pl.ds(off, ATT_BLK), :]
            vv = v_ref[pl.ds(off, ATT_BLK), :].astype(BF16)
            both = range(2)
            pres, c_des = [carry[1], carry[3]], [carry[2], carry[4]]
            if diagonal:
                valid = (lax.broadcasted_iota(jnp.int32, (ATT_BLK, ATT_BLK), 1)
                         < lax.broadcasted_iota(jnp.int32, (ATT_BLK, ATT_BLK), 0))
            kks = [jnp.where(heads[hh], k_blk, 0.0).astype(BF16) for hh in both]
            zs = [_nt(qms[hh], kks[hh]) for hh in both]
            dws = [_nt(doms[hh], vv) for hh in both]
            logs = [_sb_logs(z) for z in zs]
            lbs = [lg[0] for lg in logs]
            l1m_all = [lg[1] for lg in logs]
            l1ms = [jnp.where(valid, a, 0.0) for a in l1m_all] if diagonal else l1m_all
            prefix = [_hi_lo_dot(l1ms[hh], tri_incl) for hh in both]
            ws = [jnp.exp(lbs[hh] + (tots[hh] - (prefix[hh] + pres[hh]))) for hh in both]
            if diagonal:
                ws = [jnp.where(valid, w, 0.0) for w in ws]
            d_es = [ws[hh] * dws[hh] for hh in both]
            dv = _tn(ws[0].astype(BF16), doms[0]) + _tn(ws[1].astype(BF16), doms[1])
            dl1ms = [_hi_lo_dot(d_es[hh], tri_lt) + c_des[hh] for hh in both]
            dzs = [d_es[hh] * jnp.exp(l1m_all[hh]) - dl1ms[hh] * jnp.exp(lbs[hh]) for hh in both]
            if diagonal:
                dzs = [jnp.where(valid, dz, 0.0) for dz in dzs]
            dzs = [dz.astype(BF16) for dz in dzs]
            dq = dq + jnp.dot(dzs[0], kks[0], preferred_element_type=F32) + jnp.dot(dzs[1], kks[1],
                                                                                    preferred_element_type=F32)
            dk_ref[pl.ds(off, ATT_BLK), :] += _tn(dzs[0], qms[0]) + _tn(dzs[1], qms[1])
            dv_ref[pl.ds(off, ATT_BLK), :] += dv
            return (dq,
                    pres[0] + jnp.sum(l1ms[0], axis=-1, keepdims=True), c_des[0] + jnp.sum(d_es[0], axis=-1, keepdims=True),
                    pres[1] + jnp.sum(l1ms[1], axis=-1, keepdims=True), c_des[1] + jnp.sum(d_es[1], axis=-1, keepdims=True))

        zero = jnp.zeros((ATT_BLK, 1), F32)
        carry = lax.fori_loop(0, qi, lambda kb, cr: block(kb, cr, False),
                              (jnp.zeros((ATT_BLK, LANES), F32), zero, zero, zero, zero))
        carry = block(qi, carry, True)
        dq_ref[...] = carry[0] * scale

    return pl.pallas_call(
        body, name="sb_attn_bwd", grid=(SB_HEADS // 2, nq),
        in_specs=[pl.BlockSpec((ATT_BLK, LANES), lambda p, i: (i, p)),
                  pl.BlockSpec((t, LANES), lambda p, i: (0, p)),
                  pl.BlockSpec((t, LANES), lambda p, i: (0, p)),
                  pl.BlockSpec((ATT_BLK, 2 * LANES), lambda p, i: (i, p)),
                  pl.BlockSpec((ATT_BLK, LANES), lambda p, i: (i, p))],
        out_specs=[pl.BlockSpec((ATT_BLK, LANES), lambda p, i: (i, p)),
                   pl.BlockSpec((t, LANES), lambda p, i: (0, p)),
                   pl.BlockSpec((t, LANES), lambda p, i: (0, p))],
        out_shape=[jax.ShapeDtypeStruct((t, SB_WIDTH), F32)] * 3,
        compiler_params=pltpu.CompilerParams(dimension_semantics=("arbitrary", "arbitrary")),
    )(q, k, v, tot, do)


@jax.custom_vjp
def _sb_attention(q, k, v):
    return _sb_fwd(q, k, v)[0]


def _sb_attention_fwd(q, k, v):
    o, tot = _sb_fwd(q, k, v)
    return o, (q, k, v, tot)


def _sb_attention_bwd(res, do):
    return tuple(_sb_bwd(*res, do))


_sb_attention.defvjp(_sb_attention_fwd, _sb_attention_bwd)


def _mla_fwd(qn, qr, kn, kr, v):
    t = qn.shape[0]
    nq = t // ATT_BLK
    scale = MLA_QK ** -0.5

    def body(qn_ref, qr_ref, kn_ref, kr_ref, v_ref, o_ref, lse_ref):
        qi = pl.program_id(1)
        lanes = [slice(hh * LANES, (hh + 1) * LANES) for hh in range(2)]
        qnb = [qn_ref[:, sl].astype(BF16) for sl in lanes]
        qrb = [qr_ref[:, sl].astype(BF16) for sl in lanes]

        def block(kb, carry, diagonal):
            off = pl.multiple_of(kb * ATT_BLK, ATT_BLK)
            krb = kr_ref[pl.ds(off, ATT_BLK), :].astype(BF16)
            both = range(2)
            accs, ms, ls = [carry[0], carry[3]], [carry[1], carry[4]], [carry[2], carry[5]]
            ss = [(_nt(qnb[hh], kn_ref[pl.ds(off, ATT_BLK), lanes[hh]].astype(BF16)) + _nt(qrb[hh], krb)) * scale
                  for hh in both]
            if diagonal:
                causal = (lax.broadcasted_iota(jnp.int32, (ATT_BLK, ATT_BLK), 1)
                          <= lax.broadcasted_iota(jnp.int32, (ATT_BLK, ATT_BLK), 0))
                ss = [jnp.where(causal, s, -jnp.inf) for s in ss]
            m_new = [jnp.maximum(ms[hh], jnp.max(ss[hh], axis=-1, keepdims=True)) for hh in both]
            ps = [jnp.exp(ss[hh] - m_new[hh]) for hh in both]
            alphas = [jnp.exp(ms[hh] - m_new[hh]) for hh in both]
            pvs = [jnp.dot(ps[hh].astype(BF16), v_ref[pl.ds(off, ATT_BLK), lanes[hh]].astype(BF16),
                           preferred_element_type=F32) for hh in both]
            out = []
            for hh in both:
                out += [accs[hh] * alphas[hh] + pvs[hh], m_new[hh],
                        ls[hh] * alphas[hh] + jnp.sum(ps[hh], axis=-1, keepdims=True)]
            return tuple(out)

        init = (jnp.zeros((ATT_BLK, LANES), F32), jnp.full((ATT_BLK, 1), -jnp.inf, F32), jnp.zeros((ATT_BLK, 1), F32))
        carry = block(qi, init + init, True)
        carry = lax.fori_loop(0, qi, lambda kb, cr: block(kb, cr, False), carry)
        for hh in range(2):
            acc, m, l = carry[3 * hh:3 * hh + 3]
            o_ref[:, lanes[hh]] = acc / l
            lse_ref[:, lanes[hh]] = jnp.broadcast_to(m + jnp.log(l), (ATT_BLK, LANES))

    blk = pl.BlockSpec((ATT_BLK, 2 * LANES), lambda p, i: (i, p))
    full = pl.BlockSpec((t, 2 * LANES), lambda p, i: (0, p))
    return pl.pallas_call(
        body, name="mla_attn_fwd", grid=(MLA_HEADS // 2, nq),
        in_specs=[blk, blk, full, pl.BlockSpec((t, LANES), lambda p, i: (0, 0)), full],
        out_specs=[blk, blk],
        out_shape=[jax.ShapeDtypeStruct((t, MLA_HEADS * LANES), F32)] * 2,
        compiler_params=pltpu.CompilerParams(dimension_semantics=("arbitrary", "arbitrary")),
    )(qn, qr, kn, kr, v)


def _mla_bwd(qn, qr, kn, kr, v, o, lse, do):
    t = qn.shape[0]
    nq = t // ATT_BLK
    scale = MLA_QK ** -0.5

    def body(qn_ref, qr_ref, kn_ref, kr_ref, v_ref, o_ref, lse_ref, do_ref,
             dqn_ref, dqr_ref, dkn_ref, dkr_ref, dv_ref):
        pair = pl.program_id(0)
        qi = pl.program_id(1)

        @pl.when(qi == 0)
        def _():
            dkn_ref[...] = jnp.zeros_like(dkn_ref)
            dv_ref[...] = jnp.zeros_like(dv_ref)

        @pl.when((qi == 0) & (pair == 0))
        def _():
            dkr_ref[...] = jnp.zeros_like(dkr_ref)

        lanes = [slice(hh * LANES, (hh + 1) * LANES) for hh in range(2)]
        qnb = [qn_ref[:, sl].astype(BF16) for sl in lanes]
        qrb = [qr_ref[:, sl].astype(BF16) for sl in lanes]
        dob = [do_ref[:, sl].astype(BF16) for sl in lanes]
        delta = [jnp.sum(do_ref[:, sl] * o_ref[:, sl], axis=-1, keepdims=True) for sl in lanes]
        lse_v = [lse_ref[:, hh * LANES:hh * LANES + 1] for hh in range(2)]

        def block(kb, carry, diagonal):
            off = pl.multiple_of(kb * ATT_BLK, ATT_BLK)
            krb = kr_ref[pl.ds(off, ATT_BLK), :].astype(BF16)
            both = range(2)
            knb = [kn_ref[pl.ds(off, ATT_BLK), lanes[hh]].astype(BF16) for hh in both]
            vb = [v_ref[pl.ds(off, ATT_BLK), lanes[hh]].astype(BF16) for hh in both]
            ss = [_nt(qnb[hh], knb[hh]) + _nt(qrb[hh], krb) for hh in both]
            dps = [_nt(dob[hh], vb[hh]) for hh in both]
            ps = [jnp.exp(ss[hh] * scale - lse_v[hh]) for hh in both]
            if diagonal:
                causal = (lax.broadcasted_iota(jnp.int32, (ATT_BLK, ATT_BLK), 1)
                          <= lax.broadcasted_iota(jnp.int32, (ATT_BLK, ATT_BLK), 0))
                ps = [jnp.where(causal, p, 0.0) for p in ps]
            dss = [(ps[hh] * (dps[hh] - delta[hh]) * scale).astype(BF16) for hh in both]
            for hh in both:
                dv_ref[pl.ds(off, ATT_BLK), lanes[hh]] += _tn(ps[hh].astype(BF16), dob[hh])
            for hh in both:
                dkn_ref[pl.ds(off, ATT_BLK), lanes[hh]] += _tn(dss[hh], qnb[hh])
            dkr_ref[pl.ds(off, ATT_BLK), :] += _tn(dss[0], qrb[0]) + _tn(dss[1], qrb[1])
            out = []
            for hh in both:
                out += [carry[2 * hh] + jnp.dot(dss[hh], knb[hh], preferred_element_type=F32),
                        carry[2 * hh + 1] + jnp.dot(dss[hh], krb, preferred_element_type=F32)]
            return tuple(out)

        zero = jnp.zeros((ATT_BLK, LANES), F32)
        carry = lax.fori_loop(0, qi, lambda kb, cr: block(kb, cr, False), (zero, zero, zero, zero))
        carry = block(qi, carry, True)
        for hh in range(2):
            dqn_ref[:, lanes[hh]] = carry[2 * hh]
            dqr_ref[:, lanes[hh]] = carry[2 * hh + 1]

    blk = pl.BlockSpec((ATT_BLK, 2 * LANES), lambda p, i: (i, p))
    full = pl.BlockSpec((t, 2 * LANES), lambda p, i: (0, p))
    shared = pl.BlockSpec((t, LANES), lambda p, i: (0, 0))
    wide = jax.ShapeDtypeStruct((t, MLA_HEADS * LANES), F32)
    return pl.pallas_call(
        body, name="mla_attn_bwd", grid=(MLA_HEADS // 2, nq),
        in_specs=[blk, blk, full, shared, full, blk, blk, blk],
        out_specs=[blk, blk, full, shared, full],
        out_shape=[wide, wide, wide, jax.ShapeDtypeStruct((t, LANES), F32), wide],
        compiler_params=pltpu.CompilerParams(dimension_semantics=("arbitrary", "arbitrary")),
    )(qn, qr, kn, kr, v, o, lse, do)


@jax.custom_vjp
def _mla_attention(qn, qr, kn, kr, v):
    return _mla_fwd(qn, qr, kn, kr, v)[0]


def _mla_attention_fwd(qn, qr, kn, kr, v):
    o, lse = _mla_fwd(qn, qr, kn, kr, v)
    return o, (qn, qr, kn, kr, v, o, lse)


def _mla_attention_bwd(res, do):
    return tuple(_mla_bwd(*res, do))


_mla_attention.defvjp(_mla_attention_fwd, _mla_attention_bwd)


def _split_cols(x, cuts):
    cuts = tuple(cuts)

    @jax.custom_vjp
    def op(x):
        return tuple(x[:, a:b] for a, b in zip((0,) + cuts, cuts + (x.shape[1],)))

    def fwd(x):
        return op(x), None

    def bwd(_, cts):
        return (jnp.concatenate(cts, axis=1),)

    op.defvjp(fwd, bwd)
    return op(x)


def _swap_halves(w):
    half = w.shape[-1] // 2
    return jnp.concatenate([w[..., half:], w[..., :half]], axis=-1)


def _pad_lanes(w):
    return jnp.concatenate([w, jnp.zeros(w.shape[:-1] + (LANES - w.shape[-1],), w.dtype)], axis=-1)


def _join_cols(shards):
    return shards.transpose(1, 0, 2).reshape(shards.shape[1], -1)


def _mod_parts(mod):
    return [mod[:, i * D_MODEL:(i + 1) * D_MODEL] for i in range(N_MOD)]


def _local_loss(x, mod, p, cos, sin, target):
    return _ffn_stage(x, _mixing_stage(x, mod, p, cos, sin), mod, p, target)


def _mixing_stage(x, mod, p, cos, sin):
    shift1, scale1 = _mod_parts(mod)[:2]

    w_in = _join_cols(p["w_in"])
    k_rope_w = w_in[:, 2176:2240]
    w_in_ext = jnp.concatenate([w_in[:, :2176], _pad_lanes(k_rope_w), _pad_lanes(_swap_halves(k_rope_w)),
                                jnp.zeros((D_MODEL, LANES), w_in.dtype)], axis=1)
    (h1,) = _make_rowwise("pre_attn", _f_pre_attn, 1, 3, [D_MODEL], [True])(x, p["norm_attn"], scale1, shift1)
    proj = _make_linear("in_proj", 512, 640)(h1, w_in_ext)
    q_sb, k_sb, v_sb, cq, ckv, kr, kr_sw, _ = _split_cols(proj, (512, 1024, 1536, 1920, 2176, 2304, 2432))

    o_sb = _sb_attention(q_sb, k_sb, v_sb)

    wq = _join_cols(p["w_q_up"]).reshape(MLA_Q_RANK, MLA_HEADS, MLA_QK)
    wq_n, wq_r = wq[:, :, :MLA_NOPE], wq[:, :, MLA_NOPE:]
    w_q_ext = jnp.concatenate([wq_n.reshape(MLA_Q_RANK, -1), _pad_lanes(wq_r).reshape(MLA_Q_RANK, -1),
                               _pad_lanes(_swap_halves(wq_r)).reshape(MLA_Q_RANK, -1)], axis=1)
    wkv = _join_cols(p["w_kv_up"]).reshape(MLA_KV_RANK, MLA_HEADS, MLA_NOPE + MLA_V)
    w_kv_ext = jnp.concatenate([wkv[:, :, :MLA_NOPE].reshape(MLA_KV_RANK, -1),
                                wkv[:, :, MLA_NOPE:].reshape(MLA_KV_RANK, -1)], axis=1)
    cqn, ckvn = _make_rowwise("mla_a", _f_mla_a, 2, 2, [MLA_Q_RANK, MLA_KV_RANK], [True, True])(
        cq, ckv, p["q_a_norm"], p["kv_a_norm"])
    qall = _make_linear("q_up", 384, 768)(cqn, w_q_ext)
    kvall = _make_linear("kv_up", 256, 1024)(ckvn, w_kv_ext)
    kn_all, v_mla = _split_cols(kvall, (512,))
    gq = p["q_norm"]
    gkr = p["k_rope_norm"]
    qn, qr, kn, krr = _make_rowwise("mla_b", _f_mla_b, 6, 6, [512, 512, 512, LANES],
                                    [True, True, True, True, False, False])(
        qall, kn_all, kr, kr_sw, cos, sin,
        gq[:, :MLA_NOPE], _pad_lanes(gq[:, MLA_NOPE:]), _pad_lanes(_swap_halves(gq[:, MLA_NOPE:])),
        p["k_nope_norm"], _pad_lanes(gkr), _pad_lanes(_swap_halves(gkr)))
    o_mla = _mla_attention(qn, qr, kn, krr, v_mla)

    (mixed,) = _make_rowwise("post_attn", _f_post_attn, 2, 2, [D_MODEL], [True, True])(
        o_sb, o_mla, p["out_norm_sb"], p["out_norm_mla"])
    return mixed


def _ffn_stage(x, mixed, mod, p, target):
    _, _, gate1, shift2, scale2, gate2 = _mod_parts(mod)
    attn = _make_linear("out_proj", 512, 512)(mixed, p["w_out"].reshape(D_MODEL, D_MODEL))

    x2, h2 = _make_rowwise("pre_ffn", _f_pre_ffn, 2, 4, [D_MODEL, D_MODEL], [True, True])(
        x, attn, gate1, p["norm_ffn"], scale2, shift2)
    gt = _make_linear_sharded("ffn_gate", 512)(h2, p["w_gate"])
    up = _make_linear_sharded("ffn_up", 512)(h2, p["w_up"])
    (act,) = _make_rowwise("swiglu", _f_swiglu, 2, 0, [N_CHIPS * FF_SHARD_PAD], [True, True])(gt, up)
    ffn = _make_linear("ffn_down", 256, 1024)(act, p["w_down"].reshape(N_CHIPS * FF_SHARD_PAD, D_MODEL))
    (row_loss,) = _make_rowwise("loss", _f_loss, 3, 1, [1], [True, True, False])(x2, ffn, target, gate2)
    return 0.5 * jnp.sum(row_loss)


def _my_place():
    return lax.axis_index("x"), lax.axis_index("y"), lax.axis_index("c")


def _all_gather_small(block, name):
    m_per, n = block.shape

    def body(x_ref, out_ref, send_sems, recv_sems, local_sem):
        x, y, c = _my_place()
        me, sibling = (x, y, c), (x, y, 1 - c)
        chips = [(1 - x, y), (x, 1 - y), (1 - x, 1 - y)]

        def rows(px, py, pc):
            return out_ref.at[pl.ds((4 * px + 2 * py + pc) * m_per, m_per), :]

        def copy(k, blk, to, src=None):
            return pltpu.make_async_remote_copy(
                src_ref=rows(*blk) if src is None else src, dst_ref=rows(*blk),
                send_sem=send_sems.at[k], recv_sem=recv_sems.at[k], device_id=to, device_id_type=MESH)

        mine = pltpu.make_async_copy(x_ref, rows(*me), local_sem)
        mine.start()
        first = [copy(0, me, sibling, src=x_ref)]
        first += [copy(1 + j, me, (*chip, c), src=x_ref) for j, chip in enumerate(chips)]
        for cp in first:
            cp.start()
        passed = [copy(4 + j, (*chip, c), sibling) for j, chip in enumerate(chips)]
        for j, chip in enumerate(chips):
            copy(1 + j, (*chip, c), me).wait_recv()
            passed[j].start()
        copy(0, sibling, me).wait_recv()
        for j, chip in enumerate(chips):
            copy(4 + j, (*chip, 1 - c), me).wait_recv()
        for cp in first + passed:
            cp.wait_send()
        mine.wait()

    return pl.pallas_call(
        body, name=name,
        out_shape=jax.ShapeDtypeStruct((N_DEV * m_per, n), block.dtype),
        in_specs=[pl.BlockSpec(memory_space=pltpu.VMEM)],
        out_specs=pl.BlockSpec(memory_space=pltpu.VMEM),
        scratch_shapes=[pltpu.SemaphoreType.DMA((7,)), pltpu.SemaphoreType.DMA((7,)), pltpu.SemaphoreType.DMA],
    )(block)


EARLY = ("w_in", "w_q_up", "w_kv_up")
LATE = ("w_out", "w_gate", "w_up", "w_down")
BIG = EARLY + LATE
HALF_AXIS = {"w_in": 0, "w_q_up": 0, "w_kv_up": 0, "w_out": 0, "w_gate": 0, "w_up": 0, "w_down": 1}


def _half(ref, h, axis, lead=()):
    trail = ref.shape[len(lead):]
    idx = list(lead) + [slice(None)] * len(trail)
    at = len(trail) - 2 + axis
    n2 = trail[at] // 2
    idx[len(lead) + at] = pl.ds(h * n2, n2)
    return ref.at[tuple(idx)]


def _half_shape(shape, axis):
    shape = list(shape)
    shape[len(shape) - 2 + axis] //= 2
    return tuple(shape)


def _remote(src, dst, send_sems, recv_sems, k, to):
    return pltpu.make_async_remote_copy(src_ref=src, dst_ref=dst, send_sem=send_sems.at[k],
                                        recv_sem=recv_sems.at[k], device_id=to, device_id_type=MESH)


def _gather_weights(names, shards):
    n_w = len(shards)
    axes = [HALF_AXIS[n] for n in names]

    def body(*refs):
        w_refs, out_refs, token = refs[:n_w], refs[n_w:2 * n_w], refs[2 * n_w]
        send_sems, recv_sems, local_sems = refs[2 * n_w + 1:]
        token[...] = jnp.zeros_like(token)
        x, y, c = _my_place()
        sibling = (x, y, 1 - c)
        chips = [(1 - x, y), (x, 1 - y), (1 - x, 1 - y)]
        me = 2 * x + y
        mine =[pltpu.make_async_copy(w, o.at[me], local_sems.at[i]) for i, (w, o) in enumerate(zip(w_refs, out_refs))]
        for cp in mine:
            cp.start()
        first = [_remote(_half(w_refs[i], c, axes[i]), _half(out_refs[i], c, axes[i], (me,)),
                         send_sems, recv_sems, 6 * i + j, (*chip, c))
                 for i in range(n_w) for j, chip in enumerate(chips)]
        for cp in first:
            cp.start()
        passed = []
        for j, (cx, cy) in enumerate(chips):
            for i in range(n_w):
                blk = _half(out_refs[i], c, axes[i], (2 * cx + cy,))
                _remote(blk, blk, send_sems, recv_sems, 6 * i + j, (cx, cy, c)).wait_recv()
                cp = _remote(blk, blk, send_sems, recv_sems, 6 * i + 3 + j, sibling)
                cp.start()
                passed.append(cp)
        for j, (cx, cy) in enumerate(chips):
            for i in range(n_w):
                blk = _half(out_refs[i], 1 - c, axes[i], (2 * cx + cy,))
                _remote(blk, blk, send_sems, recv_sems, 6 * i + 3 + j, sibling).wait_recv()
        for cp in first + passed:
            cp.wait_send()
        for cp in mine:
            cp.wait()

    outs = pl.pallas_call(
        body, name="gather_weights",
        out_shape=[jax.ShapeDtypeStruct((N_CHIPS,) + s.shape, s.dtype) for s in shards]
        + [jax.ShapeDtypeStruct((8, LANES), F32)],
        in_specs=[ANY] * n_w, out_specs=[ANY] * n_w + [pl.BlockSpec(memory_space=pltpu.VMEM)],
        scratch_shapes=[pltpu.SemaphoreType.DMA((6 * n_w,)), pltpu.SemaphoreType.DMA((6 * n_w,)),
                        pltpu.SemaphoreType.DMA((n_w,))],
    )(*shards)
    return outs[:n_w], outs[n_w]


def _pair_exchange(names, grads, call_name):
    n_w = len(grads)
    axes = [HALF_AXIS[n] for n in names]

    def body(*refs):
        g_refs, t_refs = refs[:n_w], refs[n_w:2 * n_w]
        send_sems, recv_sems = refs[2 * n_w:]
        x, y, c = _my_place()
        sends = [_remote(_half(g_refs[i], 1 - c, axes[i]), t_refs[i], send_sems, recv_sems, i, (x, y, 1 - c))
                 for i in range(n_w)]
        for cp in sends:
            cp.start()
        for cp in sends:
            cp.wait_recv()
        for cp in sends:
            cp.wait_send()

    return pl.pallas_call(
        body, name=call_name,
        out_shape=[jax.ShapeDtypeStruct(_half_shape(g.shape, a), g.dtype) for g, a in zip(grads, axes)],
        in_specs=[ANY] * n_w, out_specs=[ANY] * n_w,
        scratch_shapes=[pltpu.SemaphoreType.DMA((n_w,)), pltpu.SemaphoreType.DMA((n_w,))],
    )(*grads)


def _chip_scatter(pair_sums):
    n_w = len(pair_sums)

    def body(*refs):
        s_refs, p_refs = refs[:n_w], refs[n_w:2 * n_w]
        send_sems, recv_sems = refs[2 * n_w:]
        x, y, c = _my_place()
        chips = [(1 - x, y), (x, 1 - y), (1 - x, 1 - y)]
        sends = [_remote(s_refs[i].at[2 * cx + cy], p_refs[i].at[j], send_sems, recv_sems, 3 * i + j, (cx, cy, c))
                 for i in range(n_w) for j, (cx, cy) in enumerate(chips)]
        for cp in sends:
            cp.start()
        for cp in sends:
            cp.wait_recv()
        for cp in sends:
            cp.wait_send()

    return pl.pallas_call(
        body, name="grad_chip_scatter",
        out_shape=[jax.ShapeDtypeStruct((N_CHIPS - 1,) + s.shape[1:], s.dtype) for s in pair_sums],
        in_specs=[ANY] * n_w, out_specs=[ANY] * n_w,
        scratch_shapes=[pltpu.SemaphoreType.DMA((3 * n_w,)), pltpu.SemaphoreType.DMA((3 * n_w,))],
    )(*pair_sums)


def _sibling_join(halves):
    n_w = len(halves)

    def body(*refs):
        s_refs, j_refs = refs[:n_w], refs[n_w:2 * n_w]
        send_sems, recv_sems = refs[2 * n_w:]
        x, y, c = _my_place()
        sends = [_remote(s_refs[i], j_refs[i], send_sems, recv_sems, i, (x, y, 1 - c)) for i in range(n_w)]
        for cp in sends:
            cp.start()
        for cp in sends:
            cp.wait_recv()
        for cp in sends:
            cp.wait_send()

    return pl.pallas_call(
        body, name="grad_sibling_join",
        out_shape=[jax.ShapeDtypeStruct(s.shape, s.dtype) for s in halves],
        in_specs=[ANY] * n_w, out_specs=[ANY] * n_w,
        scratch_shapes=[pltpu.SemaphoreType.DMA((n_w,)), pltpu.SemaphoreType.DMA((n_w,))],
    )(*halves)


HBM_SPEC = pl.BlockSpec(memory_space=pltpu.HBM)
SEM_SPEC = pl.BlockSpec(memory_space=pltpu.SEMAPHORE)
DATAFLOW = pltpu.SideEffectType.DATAFLOW_SIDE_EFFECTING


def _in_hbm(a):
    return pltpu.with_memory_space_constraint(a, pltpu.HBM)


def _exchange_start(name, srcs, lands, plan, n_copies):
    n = len(srcs)

    def body(*refs):
        src_refs, land_refs = refs[:n], refs[n:2 * n]
        send_sems, recv_sems = refs[2 * n], refs[2 * n + 1]
        token = refs[-1]
        for k, (src, dst, to) in enumerate(plan(src_refs, land_refs)):
            _remote(src, dst, send_sems, recv_sems, k, to).start()
        token[...] = jnp.zeros_like(token)

    outs = pl.pallas_call(
        body, name=name,
        out_shape=(pltpu.SemaphoreType.DMA((n_copies,)), pltpu.SemaphoreType.DMA((n_copies,)),
                   *[pltpu.HBM(a.shape, a.dtype) for a in srcs], *[pltpu.HBM(a.shape, a.dtype) for a in lands],
                   jax.ShapeDtypeStruct((8, LANES), F32)),
        in_specs=[HBM_SPEC] * (2 * n),
        out_specs=(SEM_SPEC, SEM_SPEC, *[HBM_SPEC] * (2 * n), pl.BlockSpec(memory_space=pltpu.VMEM)),
        input_output_aliases={i: 2 + i for i in range(2 * n)},
        compiler_params=pltpu.CompilerParams(has_side_effects=DATAFLOW),
    )(*[_in_hbm(a) for a in srcs], *[_in_hbm(a) for a in lands])
    return outs[0], outs[1], outs[2:2 + n], outs[2 + n:2 + 2 * n], outs[-1]


def _exchange_wait(name, started, plan, after):
    send_sems, recv_sems, srcs, lands, _ = started
    n = len(srcs)

    def body(*refs):
        src_refs, land_refs = refs[:n], refs[n:2 * n]
        s_sems, r_sems = refs[2 * n], refs[2 * n + 1]
        for k, (src, dst, to) in enumerate(plan(src_refs, land_refs)):
            cp = _remote(src, dst, s_sems, r_sems, k, to)
            cp.wait_send()
            cp.wait_recv()

    outs = pl.pallas_call(
        body, name=name,
        out_shape=tuple(pltpu.HBM(a.shape, a.dtype) for a in list(srcs) + list(lands)),
        in_specs=[HBM_SPEC] * (2 * n) + [SEM_SPEC, SEM_SPEC, ANY],
        out_specs=tuple([HBM_SPEC] * (2 * n)),
        input_output_aliases={i: i for i in range(2 * n)},
        compiler_params=pltpu.CompilerParams(has_side_effects=DATAFLOW),
    )(*srcs, *lands, send_sems, recv_sems, after)
    return outs[:n], outs[n:]


def _late_gather_plan(src_refs, land_refs):
    x, y, c = _my_place()
    chips = [(1 - x, y), (x, 1 - y), (1 - x, 1 - y)]
    return [(src, land.at[2 * x + y], (cx, cy, c)) for src, land in zip(src_refs, land_refs) for cx, cy in chips]


def _late_scatter_plan(src_refs, land_refs):
    x, y, c = _my_place()
    chips = [(1 - x, y), (x, 1 - y), (1 - x, 1 - y)]
    return [(src.at[2 * cx + cy], land.at[j], (cx, cy, c))
            for src, land in zip(src_refs, land_refs) for j, (cx, cy) in enumerate(chips)]


def _row_tile(rows, mult=16):
    return max(d for d in range(mult, ROW_TILE + 1, mult) if rows % d == 0)


def _pair_sum(place, g, theirs, axis, name):
    nj, rr, cc = theirs.shape
    tr = _row_tile(rr)
    nb = rr // tr
    if axis == 0:
        g_map = lambda j, i, pr: (j, pr[0] * nb + i, 0)
    else:
        g_map = lambda j, i, pr: (j, i, pr[0])

    def body(pr, g_ref, t_ref, o_ref):
        o_ref[...] = (g_ref[...].astype(F32) + t_ref[...].astype(F32)).astype(BF16)

    spec = pl.BlockSpec((None, tr, cc), lambda j, i, pr: (j, i, 0))
    return pl.pallas_call(
        body, name=name,
        grid_spec=pltpu.PrefetchScalarGridSpec(
            num_scalar_prefetch=1, grid=(nj, nb),
            in_specs=[pl.BlockSpec((None, tr, cc), g_map), spec], out_specs=spec),
        out_shape=jax.ShapeDtypeStruct(theirs.shape, BF16))(place, g, theirs)


def _chip_sum(place, pair_sums, parts, name):
    _, rr, cc = parts.shape
    tr = _row_tile(rr)

    def body(pr, h_ref, p_ref, o_ref):
        acc = p_ref[0].astype(F32)
        for j in range(1, N_CHIPS - 1):
            acc = acc + p_ref[j].astype(F32)
        o_ref[...] = (acc + h_ref[...].astype(F32)).astype(BF16)

    return pl.pallas_call(
        body, name=name,
        grid_spec=pltpu.PrefetchScalarGridSpec(
            num_scalar_prefetch=1, grid=(rr // tr,),
            in_specs=[pl.BlockSpec((None, tr, cc), lambda i, pr: (pr[1], i, 0)),
                      pl.BlockSpec((N_CHIPS - 1, tr, cc), lambda i, pr: (0, i, 0))],
            out_specs=pl.BlockSpec((tr, cc), lambda i, pr: (i, 0))),
        out_shape=jax.ShapeDtypeStruct((rr, cc), BF16))(place, pair_sums, parts)


def _silu(v):
    return v / (1.0 + jnp.exp(-v))


def _ada_fwd(c_all, w_shard, b_shard):
    def body(c_ref, w_ref, b_ref, o_ref):
        o_ref[...] = jnp.dot(_silu(c_ref[...]), w_ref[...], precision=lax.Precision.HIGHEST,
                             preferred_element_type=F32) + b_ref[...]

    return pl.pallas_call(body, name="ada_fwd", out_shape=jax.ShapeDtypeStruct((c_all.shape[0], w_shard.shape[1]), F32),
                          compiler_params=pltpu.CompilerParams(vmem_limit_bytes=MM_VMEM_LIMIT))(c_all, w_shard, b_shard)


def _ada_bwd(c_all, dmod_cols):
    def body(c_ref, d_ref, o_ref):
        o_ref[...] = lax.dot_general(_silu(c_ref[...]), d_ref[...], (((0,), (0,)), ((), ())),
                                     precision=lax.Precision.HIGHEST, preferred_element_type=F32)

    return pl.pallas_call(body, name="ada_bwd", out_shape=jax.ShapeDtypeStruct((c_all.shape[1], dmod_cols.shape[1]), F32),
                          compiler_params=pltpu.CompilerParams(vmem_limit_bytes=MM_VMEM_LIMIT))(c_all, dmod_cols)


def _adamw_math(w, g, m, v):
    m = ADAM_B1 * m + (1.0 - ADAM_B1) * g
    v = ADAM_B2 * v + (1.0 - ADAM_B2) * (g * g)
    m_hat = m / (1.0 - ADAM_B1 ** ADAM_STEP)
    v_hat = v / (1.0 - ADAM_B2 ** ADAM_STEP)
    delta = -ADAM_LR * (m_hat / (jnp.sqrt(v_hat) + ADAM_EPS) + ADAM_WD * w)
    return delta, m, v


def _adamw(w, g, m, v, name):
    r, ccols = w.shape
    tr = max(d for d in range(8, ROW_TILE + 1, 8) if r % d == 0)
    spec = pl.BlockSpec((tr, ccols), lambda i: (i, 0))

    def body(w_ref, g_ref, m_ref, v_ref, d_ref, nm_ref, nv_ref):
        d_ref[...], nm_ref[...], nv_ref[...] = _adamw_math(w_ref[...], g_ref[...], m_ref[...], v_ref[...])

    return pl.pallas_call(body, name=name, grid=(r // tr,), in_specs=[spec] * 4, out_specs=[spec] * 3,
                          out_shape=[jax.ShapeDtypeStruct(w.shape, F32)] * 3,
                          compiler_params=pltpu.CompilerParams(vmem_limit_bytes=MM_VMEM_LIMIT))(w, g, m, v)


def _adamw_small(w, g_all, m, v):
    def body(w_ref, g_ref, m_ref, v_ref, gs_ref, d_ref, nm_ref, nv_ref):
        g = g_ref[0]
        for d in range(1, N_DEV):
            g = g + g_ref[d]
        gs_ref[...] = g
        d_ref[...], nm_ref[...], nv_ref[...] = _adamw_math(w_ref[...], g, m_ref[...], v_ref[...])

    return pl.pallas_call(body, name="adamw_small", out_shape=[jax.ShapeDtypeStruct(w.shape, F32)] * 4)(w, g_all, m, v)


def _adamw_halves(place, w, own, sib, m, v, axis, name):
    r, cc = w.shape
    if axis == 0:
        rows, gc = own.shape[0], own.shape[1]
        tr = _row_tile(rows)
        nb = rows // tr
        w_spec = pl.BlockSpec((tr, cc), lambda h, i, pr: (h * nb + i, 0))
        g_spec = pl.BlockSpec((tr, gc), lambda h, i, pr: (i, 0))
    else:
        tr = _row_tile(r)
        nb = r // tr
        gc = own.shape[1]
        w_spec = pl.BlockSpec((tr, gc), lambda h, i, pr: (i, h))
        g_spec = pl.BlockSpec((tr, gc), lambda h, i, pr: (i, 0))
    wc = w_spec.block_shape[1]

    def body(pr, w_ref, o_ref, s_ref, m_ref, v_ref, g_ref, d_ref, nm_ref, nv_ref):
        g = jnp.where(pl.program_id(0) == pr[0], o_ref[...], s_ref[...]).astype(F32)[:, :wc]
        g_ref[...] = g
        d_ref[...], nm_ref[...], nv_ref[...] = _adamw_math(w_ref[...], g, m_ref[...], v_ref[...])

    return pl.pallas_call(
        body, name=name,
        grid_spec=pltpu.PrefetchScalarGridSpec(
            num_scalar_prefetch=1, grid=(2, nb),
            in_specs=[w_spec, g_spec, g_spec, w_spec, w_spec], out_specs=[w_spec] * 4),
        out_shape=[jax.ShapeDtypeStruct(w.shape, F32)] * 4,
        compiler_params=pltpu.CompilerParams(vmem_limit_bytes=MM_VMEM_LIMIT))(place, w, own, sib, m, v)


SMALL = ("b_ada", "norm_attn", "norm_ffn", "q_a_norm", "kv_a_norm", "q_norm", "k_nope_norm", "k_rope_norm",
         "out_norm_sb", "out_norm_mla")
WEIGHTS = ("w_ada", "b_ada", "norm_attn", "norm_ffn", "w_in", "q_a_norm", "w_q_up", "kv_a_norm", "w_kv_up",
           "q_norm", "k_nope_norm", "k_rope_norm", "out_norm_sb", "out_norm_mla", "w_out", "w_gate", "w_up",
           "w_down")


def kernel(x, c, positions, w_ada, b_ada, norm_attn, norm_ffn, w_in, q_a_norm, w_q_up, kv_a_norm, w_kv_up, q_norm, k_nope_norm, k_rope_norm, out_norm_sb, out_norm_mla, w_out, w_gate, w_up, w_down, loss_target, m_w_ada, m_b_ada, m_norm_attn, m_norm_ffn, m_w_in, m_q_a_norm, m_w_q_up, m_kv_a_norm, m_w_kv_up, m_q_norm, m_k_nope_norm, m_k_rope_norm, m_out_norm_sb, m_out_norm_mla, m_w_out, m_w_gate, m_w_up, m_w_down, v_w_ada, v_b_ada, v_norm_attn, v_norm_ffn, v_w_in, v_q_a_norm, v_w_q_up, v_kv_a_norm, v_w_kv_up, v_q_norm, v_k_nope_norm, v_k_rope_norm, v_out_norm_sb, v_out_norm_mla, v_w_out, v_w_gate, v_w_up, v_w_down):
    local = dict(locals())
    w = {n: local[n][0] for n in WEIGHTS}
    m = {n: local["m_" + n][0] for n in WEIGHTS}
    v = {n: local["v_" + n][0] for n in WEIGHTS}
    small = {n: w[n].reshape(1, -1) for n in SMALL}
    ix, iy, ic = _my_place()
    chip = 2 * ix + iy
    dev = 2 * chip + ic
    xs, target = x[0], loss_target[0]
    seq = xs.shape[0]

    ff_pad = FF_SHARD_PAD - FF_SHARD
    pads = {"w_gate": ((0, 0), (0, ff_pad)), "w_up": ((0, 0), (0, ff_pad)), "w_down": ((0, ff_pad), (0, 0))}
    shards = {n: jnp.pad(w[n].astype(BF16), pads[n]) if n in pads else w[n].astype(BF16) for n in BIG}
    early, early_done = _gather_weights(EARLY, [shards[n] for n in EARLY])
    gathered = dict(zip(EARLY, early))
    slot = chip + early_done[0, 0].astype(jnp.int32)
    lands = [lax.dynamic_update_index_in_dim(lax.empty((N_CHIPS,) + shards[n].shape, BF16), shards[n], slot, 0)
             for n in LATE]
    late_gather = _exchange_start("gather_late_start", [shards[n] for n in LATE], lands, _late_gather_plan,
                                  3 * len(LATE))

    c_all = _all_gather_small(c.reshape(8, LANES), "gather_c").reshape(N_DEV, D_MODEL)
    ada_cols = w["w_ada"].shape[1]
    b_cols = lax.dynamic_slice_in_dim(small["b_ada"], chip * ada_cols, ada_cols, axis=1)
    mod_cols = _ada_fwd(c_all, w["w_ada"], b_cols)
    mod_all = _all_gather_small(mod_cols, "gather_mod").reshape(N_CHIPS, 2, N_DEV, ada_cols)
    mod = lax.dynamic_index_in_dim(mod_all[:, 0], dev, axis=1, keepdims=False).reshape(1, N_MOD * D_MODEL)

    half = MLA_ROPE // 2
    freqs = 1.0 / (ROPE_THETA ** (np.arange(half, dtype=np.float32) / half))
    zeros = np.zeros(LANES - MLA_ROPE, np.float32)
    freqs_row = jnp.asarray(np.concatenate([freqs, freqs, zeros]).astype(np.float32)[None])
    sign_row = jnp.asarray(np.concatenate([-np.ones(half), np.ones(half), zeros]).astype(np.float32)[None])
    cos, sin = _rope_tables(positions.reshape(seq, 1), freqs_row, sign_row)

    place = jnp.stack([ic, chip]).astype(jnp.int32)
    small_params = {n: small[n] for n in SMALL if n != "b_ada"}
    mod = mod + late_gather[4][0, 0]

    def pair_sums_of(names, grads, call_name):
        theirs = _pair_exchange(names, grads, call_name)
        return [_pair_sum(place, gr, th, HALF_AXIS[n], "grad_pair_sum_" + n) for n, gr, th in zip(names, grads, theirs)]

    p1 = {**{n: gathered[n] for n in EARLY}, **small_params}
    mixed, mixing_vjp = jax.vjp(lambda x_, mod_, p_: _mixing_stage(x_, mod_, p_, cos, sin), xs, mod, p1)
    _, landed = _exchange_wait("gather_late_wait", late_gather, _late_gather_plan, mixed)
    p2 = {**dict(zip(LATE, landed)), **small_params}
    loss_part, ffn_vjp = jax.vjp(lambda x_, mixed_, mod_, p_: _ffn_stage(x_, mixed_, mod_, p_, target), xs, mixed, mod, p2)
    gx2, gmixed, gmod2, gp2 = ffn_vjp(jnp.ones((), F32))
    late_sums = pair_sums_of(LATE, [gp2[n] for n in LATE], "grad_pair_exchange_late")
    late_scatter = _exchange_start(
        "grad_scatter_late_start", late_sums,
        [lax.empty((N_CHIPS - 1,) + s.shape[1:], BF16) for s in late_sums], _late_scatter_plan, 3 * len(LATE))
    gx1, gmod1, gp1 = mixing_vjp(gmixed + late_scatter[4][0, 0])
    gx = gx1 + gx2
    gmod = gmod1 + gmod2
    gp = {n: gp1[n] + gp2[n] for n in small_params}
    loss = lax.psum(loss_part, ("x", "y", "c"))

    early_sums = pair_sums_of(EARLY, [gp1[n] for n in EARLY], "grad_pair_exchange_early")
    early_parts = _chip_scatter(early_sums)
    late_sums, late_parts = _exchange_wait("grad_scatter_late_wait", late_scatter, _late_scatter_plan, gx)
    own = [_chip_sum(place, ps, pt, "grad_chip_sum_" + n)
           for n, ps, pt in zip(BIG, early_sums + list(late_sums), list(early_parts) + list(late_parts))]
    sib = _sibling_join(own)
    axes = [HALF_AXIS[n] for n in BIG]
    g = {}

    small_names = [n for n in SMALL if n != "b_ada"]
    small_vec = jnp.concatenate([gmod] + [gp[n] for n in small_names], axis=1)
    n_small = small_vec.shape[1]
    small_all = _all_gather_small(small_vec.reshape(8, n_small // 8), "gather_small").reshape(N_DEV, 8, n_small // 8)

    def pack_small(d):
        return jnp.concatenate([d[n].reshape(1, -1) for n in SMALL], axis=1).reshape(8, n_small // 8)

    gs, ds, ms, vs = _adamw_small(pack_small(w), small_all, pack_small(m), pack_small(v))
    sizes = [w[n].size for n in SMALL]
    offs = np.concatenate([[0], np.cumsum(sizes)])

    def unpack_small(a):
        flat = a.reshape(-1)
        return {n: flat[offs[i]:offs[i + 1]].reshape(w[n].shape) for i, n in enumerate(SMALL)}

    g.update(unpack_small(gs))
    delta, new_m, new_v = unpack_small(ds), unpack_small(ms), unpack_small(vs)

    dmod_all = small_all.reshape(N_DEV, n_small)[:, :N_MOD * D_MODEL]
    g["w_ada"] = _ada_bwd(c_all, lax.dynamic_slice_in_dim(dmod_all, chip * ada_cols, ada_cols, axis=1))

    delta["w_ada"], new_m["w_ada"], new_v["w_ada"] = _adamw(w["w_ada"], g["w_ada"], m["w_ada"], v["w_ada"], "adamw_w_ada")
    for n, o, s, a in zip(BIG, own, sib, axes):
        g[n], delta[n], new_m[n], new_v[n] = _adamw_halves(place, w[n], o, s, m[n], v[n], a, "adamw_" + n)

    def outs(d):
        return [d[n][None] for n in WEIGHTS]

    return (loss, gx[None], *outs(g), *outs(delta), *outs(new_m), *outs(new_v))
```

```python
import functools
import math

import numpy as np
import jax
import jax.numpy as jnp
from jax import lax
from jax.experimental import pallas as pl
from jax.experimental.pallas import tpu as pltpu

F32 = jnp.float32
BF16 = jnp.bfloat16
MESH = pl.DeviceIdType.MESH
ANY = pl.BlockSpec(memory_space=pl.ANY)

D_MODEL = 1024
SB_HEADS = 8
SB_HEAD_DIM = 64
SB_WIDTH = 512
MLA_HEADS = 4
MLA_NOPE = 128
MLA_ROPE = 64
MLA_QK = 192
MLA_V = 128
MLA_Q_RANK = 384
MLA_KV_RANK = 256
D_FF = 2816
N_MOD = 6
ROPE_THETA = 10000.0
EPS = 1e-6
LANES = 128

ADAM_LR = 0.001
ADAM_B1 = 0.9
ADAM_B2 = 0.999
ADAM_EPS = 1e-08
ADAM_WD = 0.01
ADAM_STEP = 10

N_CHIPS = 4
N_DEV = 8
ROW_TILE = 256
ATT_BLK = 256
MM_VMEM_LIMIT = 48 * 1024 * 1024
FF_SHARD = D_FF // N_CHIPS
FF_SHARD_PAD = 768


def _mm(a, b, mode, name, tm, tn, out_dtype=F32):
    if mode == "nn":
        (m, k), n = a.shape, b.shape[1]
        a_spec = pl.BlockSpec((tm, k), lambda j, i: (i, 0))
        b_spec = pl.BlockSpec((k, tn), lambda j, i: (0, j))
        dims = (((1,), (0,)), ((), ()))
    elif mode == "nt":
        (m, k), n = a.shape, b.shape[0]
        a_spec = pl.BlockSpec((tm, k), lambda j, i: (i, 0))
        b_spec = pl.BlockSpec((tn, k), lambda j, i: (j, 0))
        dims = (((1,), (1,)), ((), ()))
    else:
        (k, m), n = a.shape, b.shape[1]
        a_spec = pl.BlockSpec((k, tm), lambda j, i: (0, i))
        b_spec = pl.BlockSpec((k, tn), lambda j, i: (0, j))
        dims = (((0,), (0,)), ((), ()))
    assert m % tm == 0 and n % tn == 0, (name, m, n, tm, tn)

    def body(a_ref, b_ref, o_ref):
        o_ref[...] = lax.dot_general(a_ref[...].astype(BF16), b_ref[...].astype(BF16), dims,
                                     preferred_element_type=F32).astype(out_dtype)

    return pl.pallas_call(
        body, name=name, grid=(n // tn, m // tm),
        in_specs=[a_spec, b_spec],
        out_specs=pl.BlockSpec((tm, tn), lambda j, i: (i, j)),
        out_shape=jax.ShapeDtypeStruct((m, n), out_dtype),
        compiler_params=pltpu.CompilerParams(dimension_semantics=("arbitrary", "arbitrary"),
                                             vmem_limit_bytes=MM_VMEM_LIMIT),
    )(a, b)


def _make_linear(name, tk_w, tn_w):
    @jax.custom_vjp
    def op(a, w):
        return _mm(a, w, "nn", name + "_fwd", ROW_TILE, w.shape[1])

    def fwd(a, w):
        return op(a, w), (a, w)

    def bwd(res, dy):
        a, w = res
        da = _mm(dy, w, "nt", name + "_dx", ROW_TILE, w.shape[0])
        dw = _mm(a, dy, "tn", name + "_dw", tk_w, tn_w, out_dtype=BF16)
        return da, dw

    op.defvjp(fwd, bwd)
    return op


def _make_linear_sharded(name, tk_w):
    def call_fwd(a, w):
        t, k = a.shape
        n_sh, _, cc = w.shape

        def body(a_ref, w_ref, o_ref):
            o_ref[...] = jnp.dot(a_ref[...].astype(BF16), w_ref[...], preferred_element_type=F32)

        return pl.pallas_call(
            body, name=name + "_fwd", grid=(n_sh, t // ROW_TILE),
            in_specs=[pl.BlockSpec((ROW_TILE, k), lambda j, i: (i, 0)),
                      pl.BlockSpec((None, k, cc), lambda j, i: (j, 0, 0))],
            out_specs=pl.BlockSpec((ROW_TILE, cc), lambda j, i: (i, j)),
            out_shape=jax.ShapeDtypeStruct((t, n_sh * cc), F32),
            compiler_params=pltpu.CompilerParams(dimension_semantics=("arbitrary", "arbitrary"),
                                                 vmem_limit_bytes=MM_VMEM_LIMIT),
        )(a, w)

    def call_dx(dy, w):
        t = dy.shape[0]
        n_sh, k, cc = w.shape

        def body(dy_ref, w_ref, o_ref):
            acc = jnp.zeros((ROW_TILE, k), F32)
            for j in range(n_sh):
                acc = acc + _nt(dy_ref[:, j * cc:(j + 1) * cc].astype(BF16), w_ref[j])
            o_ref[...] = acc

        return pl.pallas_call(
            body, name=name + "_dx", grid=(t // ROW_TILE,),
            in_specs=[pl.BlockSpec((ROW_TILE, n_sh * cc), lambda i: (i, 0)),
                      pl.BlockSpec((n_sh, k, cc), lambda i: (0, 0, 0))],
            out_specs=pl.BlockSpec((ROW_TILE, k), lambda i: (i, 0)),
            out_shape=jax.ShapeDtypeStruct((t, k), F32),
            compiler_params=pltpu.CompilerParams(dimension_semantics=("arbitrary",),
                                                 vmem_limit_bytes=MM_VMEM_LIMIT),
        )(dy, w)

    def call_dw(a, dy, w):
        t, k = a.shape
        n_sh, _, cc = w.shape

        def body(a_ref, dy_ref, o_ref):
            o_ref[...] = _tn(a_ref[...].astype(BF16), dy_ref[...].astype(BF16)).astype(BF16)

        return pl.pallas_call(
            body, name=name + "_dw", grid=(n_sh, k // tk_w),
            in_specs=[pl.BlockSpec((t, tk_w), lambda j, i: (0, i)),
                      pl.BlockSpec((t, cc), lambda j, i: (0, j))],
            out_specs=pl.BlockSpec((None, tk_w, cc), lambda j, i: (j, i, 0)),
            out_shape=jax.ShapeDtypeStruct(w.shape, BF16),
            compiler_params=pltpu.CompilerParams(dimension_semantics=("arbitrary", "arbitrary"),
                                                 vmem_limit_bytes=MM_VMEM_LIMIT),
        )(a, dy)

    @jax.custom_vjp
    def op(a, w):
        return call_fwd(a, w)

    def fwd(a, w):
        return op(a, w), (a, w)

    def bwd(res, dy):
        a, w = res
        return call_dx(dy, w), call_dw(a, dy, w)

    op.defvjp(fwd, bwd)
    return op


def _row_spec(arr, tb):
    return pl.BlockSpec((tb, arr.shape[1]), lambda i: (i, 0))


def _full_spec(arr):
    return pl.BlockSpec(arr.shape, lambda i: (0, 0))


def _make_rowwise(name, f, n_rows, n_params, out_cols, diff_rows):
    n_out = len(out_cols)

    def call_fwd(rows, params):
        t = rows[0].shape[0]

        def body(*refs):
            ins = [r[...] for r in refs[:n_rows + n_params]]
            outs = f(*ins)
            for o_ref, o in zip(refs[n_rows + n_params:], outs):
                o_ref[...] = o

        return pl.pallas_call(
            body, name=name + "_fwd", grid=(t // ROW_TILE,),
            in_specs=[_row_spec(a, ROW_TILE) for a in rows] + [_full_spec(p) for p in params],
            out_specs=[pl.BlockSpec((ROW_TILE, n), lambda i: (i, 0)) for n in out_cols],
            out_shape=[jax.ShapeDtypeStruct((t, n), F32) for n in out_cols],
            compiler_params=pltpu.CompilerParams(dimension_semantics=("arbitrary",),
                                                 vmem_limit_bytes=MM_VMEM_LIMIT),
        )(*rows, *params)

    def call_bwd(rows, params, cts):
        t = rows[0].shape[0]
        d_rows = [a for a, d in zip(rows, diff_rows) if d]
        n_in = n_rows + n_params + n_out

        def body(*refs):
            ins = [r[...] for r in refs[:n_rows + n_params]]
            ct = tuple(r[...] for r in refs[n_rows + n_params:n_in])
            _, vjp = jax.vjp(f, *ins)
            grads = vjp(ct)
            out_refs = refs[n_in:]
            g_rows = [g for g, d in zip(grads[:n_rows], diff_rows) if d]
            for o_ref, g in zip(out_refs[:len(g_rows)], g_rows):
                o_ref[...] = g
            p_refs = out_refs[len(g_rows):]

            if p_refs:
                @pl.when(pl.program_id(0) == 0)
                def _():
                    for p_ref in p_refs:
                        p_ref[...] = jnp.zeros_like(p_ref)

                for p_ref, g in zip(p_refs, grads[n_rows:]):
                    p_ref[...] += g

        return pl.pallas_call(
            body, name=name + "_bwd", grid=(t // ROW_TILE,),
            in_specs=[_row_spec(a, ROW_TILE) for a in rows] + [_full_spec(p) for p in params]
            + [_row_spec(c, ROW_TILE) for c in cts],
            out_specs=[_row_spec(a, ROW_TILE) for a in d_rows] + [_full_spec(p) for p in params],
            out_shape=[jax.ShapeDtypeStruct(a.shape, F32) for a in d_rows]
            + [jax.ShapeDtypeStruct(p.shape, F32) for p in params],
            compiler_params=pltpu.CompilerParams(dimension_semantics=("arbitrary",),
                                                 vmem_limit_bytes=MM_VMEM_LIMIT),
        )(*rows, *params, *cts)

    @jax.custom_vjp
    def op(*args):
        return tuple(call_fwd(args[:n_rows], args[n_rows:]))

    def fwd(*args):
        return op(*args), args

    def bwd(args, cts):
        rows, params = args[:n_rows], args[n_rows:]
        outs = call_bwd(rows, params, cts)
        it = iter(outs)
        g_rows = [next(it) if d else jnp.zeros_like(a) for a, d in zip(rows, diff_rows)]
        return tuple(g_rows) + tuple(it)

    op.defvjp(fwd, bwd)
    return op


def _rms(x, g, n):
    return x * lax.rsqrt(jnp.sum(x * x, axis=-1, keepdims=True) * (1.0 / n) + EPS) * g


def _f_pre_attn(x, g, scale, shift):
    return (_rms(x, g, D_MODEL) * (1.0 + scale) + shift,)


def _f_mla_a(cq, ckv, gq, gkv):
    return _rms(cq, gq, MLA_Q_RANK), _rms(ckv, gkv, MLA_KV_RANK)


@jax.custom_vjp
def _split_lanes(x):
    return tuple(x[:, i * LANES:(i + 1) * LANES] for i in range(x.shape[1] // LANES))


def _split_lanes_fwd(x):
    return _split_lanes(x), None


def _split_lanes_bwd(_, cts):
    return (jnp.concatenate(cts, axis=1),)


_split_lanes.defvjp(_split_lanes_fwd, _split_lanes_bwd)


def _f_mla_b(qall, kn_all, kr, kr_sw, cos, sin, gqn, gqr, gqr_sw, gkn, gkr, gkr_sw):
    q = _split_lanes(qall)
    kn = _split_lanes(kn_all)
    qn_o, qr_o, kn_o = [], [], []
    for h in range(MLA_HEADS):
        qn, qr, qs = q[h], q[MLA_HEADS + h], q[2 * MLA_HEADS + h]
        ss = jnp.sum(qn * qn, axis=-1, keepdims=True) + jnp.sum(qr * qr, axis=-1, keepdims=True)
        rs = lax.rsqrt(ss * (1.0 / MLA_QK) + EPS)
        qn_o.append(qn * rs * gqn)
        qr_o.append((qr * rs * gqr) * cos + (qs * rs * gqr_sw) * sin)
        kn_o.append(_rms(kn[h], gkn, MLA_NOPE))
    rs = lax.rsqrt(jnp.sum(kr * kr, axis=-1, keepdims=True) * (1.0 / MLA_ROPE) + EPS)
    kr_o = (kr * rs * gkr) * cos + (kr_sw * rs * gkr_sw) * sin
    return (jnp.concatenate(qn_o, axis=1), jnp.concatenate(qr_o, axis=1), jnp.concatenate(kn_o, axis=1), kr_o)


def _f_post_attn(o_sb, o_mla, g_sb, g_mla):
    return (jnp.concatenate([_rms(o_sb, g_sb, SB_WIDTH), _rms(o_mla, g_mla, SB_WIDTH)], axis=1),)


def _f_pre_ffn(x, attn, gate, g, scale, shift):
    x2 = x + gate * attn
    return x2, _rms(x2, g, D_MODEL) * (1.0 + scale) + shift


def _f_swiglu(gt, up):
    return (gt / (1.0 + jnp.exp(-gt)) * up,)


def _f_loss(x2, ffn, target, gate):
    err = x2 + gate * ffn - target
    return (jnp.sum(err * err, axis=-1, keepdims=True) * (1.0 / D_MODEL),)


def _rope_tables(pos_col, freqs, sign):
    t = pos_col.shape[0]

    def body(p_ref, f_ref, s_ref, cos_ref, sin_ref):
        ang = p_ref[...].astype(F32) * f_ref[...]
        live = jnp.abs(s_ref[...])
        cos_ref[...] = jnp.cos(ang) * live
        sin_ref[...] = jnp.sin(ang) * s_ref[...]

    return pl.pallas_call(
        body, name="rope_tables", grid=(t // ROW_TILE,),
        in_specs=[pl.BlockSpec((ROW_TILE, 1), lambda i: (i, 0)), _full_spec(freqs), _full_spec(sign)],
        out_specs=[pl.BlockSpec((ROW_TILE, LANES), lambda i: (i, 0))] * 2,
        out_shape=[jax.ShapeDtypeStruct((t, LANES), F32)] * 2,
    )(pos_col, freqs, sign)


def _hi_lo_dot(x, tri):
    hi = x.astype(BF16)
    lo = (x - hi.astype(F32)).astype(BF16)
    return (jnp.dot(hi, tri, preferred_element_type=F32) + jnp.dot(lo, tri, preferred_element_type=F32))


def _tri(cmp):
    r = lax.broadcasted_iota(jnp.int32, (ATT_BLK, ATT_BLK), 0)
    c = lax.broadcasted_iota(jnp.int32, (ATT_BLK, ATT_BLK), 1)
    return cmp(r, c).astype(BF16)


def _nt(a, b):
    return lax.dot_general(a, b, (((1,), (1,)), ((), ())), preferred_element_type=F32)


def _tn(a, b):
    return lax.dot_general(a, b, (((0,), (0,)), ((), ())), preferred_element_type=F32)


def _sb_logs(z):
    lb = jnp.minimum(z, 0.0) - jnp.log(1.0 + jnp.exp(-jnp.abs(z)))
    return lb, lb - z


def _sb_fwd(q, k, v):
    t = q.shape[0]
    nq = t // ATT_BLK
    scale = SB_HEAD_DIM ** -0.5

    def body(q_ref, k_ref, v_ref, o_ref, tot_ref):
        qi = pl.program_id(1)
        lane = lax.broadcasted_iota(jnp.int32, (ATT_BLK, LANES), 1)
        tri = _tri(lambda r, c: r > c)
        qv = q_ref[...] * scale
        heads = [(lane // SB_HEAD_DIM) == hh for hh in range(2)]
        qms = [jnp.where(mine, qv, 0.0).astype(BF16) for mine in heads]

        def block(kb, carry, diagonal):
            acc, runs = carry[0], carry[1:]
            off = pl.multiple_of(kb * ATT_BLK, ATT_BLK)
            kk = k_ref[pl.ds(off, ATT_BLK), :].astype(BF16)
            v_blk = v_ref[pl.ds(off, ATT_BLK), :]
            both = range(2)
            if diagonal:
                valid = (lax.broadcasted_iota(jnp.int32, (ATT_BLK, ATT_BLK), 1)
                         < lax.broadcasted_iota(jnp.int32, (ATT_BLK, ATT_BLK), 0))
            zs = [_nt(qms[hh], kk) for hh in both]
            vvs = [jnp.where(heads[hh], v_blk, 0.0).astype(BF16) for hh in both]
            logs = [_sb_logs(z) for z in zs]
            l1ms = [jnp.where(valid, lg[1], 0.0) for lg in logs] if diagonal else [lg[1] for lg in logs]
            afters = [_hi_lo_dot(l1ms[hh], tri) for hh in both]
            ws = [jnp.exp(logs[hh][0] + (afters[hh] + runs[hh])) for hh in both]
            if diagonal:
                ws = [jnp.where(valid, w, 0.0) for w in ws]
            acc = acc + jnp.dot(ws[0].astype(BF16), vvs[0], preferred_element_type=F32) + jnp.dot(
                ws[1].astype(BF16), vvs[1], preferred_element_type=F32)
            return (acc, *[runs[hh] + jnp.sum(l1ms[hh], axis=-1, keepdims=True) for hh in both])

        zero = jnp.zeros((ATT_BLK, 1), F32)
        carry = block(qi, (jnp.zeros((ATT_BLK, LANES), F32), zero, zero), True)
        carry = lax.fori_loop(0, qi, lambda j, cr: block(qi - 1 - j, cr, False), carry)
        o_ref[...] = carry[0]
        for hh in range(2):
            tot_ref[:, hh * LANES:(hh + 1) * LANES] = jnp.broadcast_to(carry[1 + hh], (ATT_BLK, LANES))

    return pl.pallas_call(
        body, name="sb_attn_fwd", grid=(SB_HEADS // 2, nq),
        in_specs=[pl.BlockSpec((ATT_BLK, LANES), lambda p, i: (i, p)),
                  pl.BlockSpec((t, LANES), lambda p, i: (0, p)),
                  pl.BlockSpec((t, LANES), lambda p, i: (0, p))],
        out_specs=[pl.BlockSpec((ATT_BLK, LANES), lambda p, i: (i, p)),
                   pl.BlockSpec((ATT_BLK, 2 * LANES), lambda p, i: (i, p))],
        out_shape=[jax.ShapeDtypeStruct((t, SB_WIDTH), F32), jax.ShapeDtypeStruct((t, SB_HEADS * LANES), F32)],
        compiler_params=pltpu.CompilerParams(dimension_semantics=("arbitrary", "arbitrary")),
    )(q, k, v)


def _sb_bwd(q, k, v, tot, do):
    t = q.shape[0]
    nq = t // ATT_BLK
    scale = SB_HEAD_DIM ** -0.5

    def body(q_ref, k_ref, v_ref, tot_ref, do_ref, dq_ref, dk_ref, dv_ref):
        qi = pl.program_id(1)

        @pl.when(qi == 0)
        def _():
            dk_ref[...] = jnp.zeros_like(dk_ref)
            dv_ref[...] = jnp.zeros_like(dv_ref)

        lane = lax.broadcasted_iota(jnp.int32, (ATT_BLK, LANES), 1)
        tri_incl = _tri(lambda r, c: r <= c)
        tri_lt = _tri(lambda r, c: r < c)
        qv = q_ref[...] * scale
        dov = do_ref[...]
        heads = [(lane // SB_HEAD_DIM) == hh for hh in range(2)]
        qms = [jnp.where(mine, qv, 0.0).astype(BF16) for mine in heads]
        doms = [jnp.where(mine, dov, 0.0).astype(BF16) for mine in heads]
        tots = [tot_ref[:, hh * LANES:hh * LANES + 1] for hh in range(2)]

        def block(kb, carry, diagonal):
            dq = carry[0]
            off = pl.multiple_of(kb * ATT_BLK, ATT_BLK)
            k_blk = k_ref[pl.ds(off, ATT_BLK), :]
            vv = v_ref[pl.ds(off, ATT_BLK), :].astype(BF16)
            both = range(2)
            pres, c_des = [carry[1], carry[3]], [carry[2], carry[4]]
            if diagonal:
                valid = (lax.broadcasted_iota(jnp.int32, (ATT_BLK, ATT_BLK), 1)
                         < lax.broadcasted_iota(jnp.int32, (ATT_BLK, ATT_BLK), 0))
            kks = [jnp.where(heads[hh], k_blk, 0.0).astype(BF16) for hh in both]
            zs = [_nt(qms[hh], kks[hh]) for hh in both]
            dws = [_nt(doms[hh], vv) for hh in both]
            logs = [_sb_logs(z) for z in zs]
            lbs = [lg[0] for lg in logs]
            l1m_all = [lg[1] for lg in logs]
            l1ms = [jnp.where(valid, a, 0.0) for a in l1m_all] if diagonal else l1m_all
            prefix = [_hi_lo_dot(l1ms[hh], tri_incl) for hh in both]
            ws = [jnp.exp(lbs[hh] + (tots[hh] - (prefix[hh] + pres[hh]))) for hh in both]
            if diagonal:
                ws = [jnp.where(valid, w, 0.0) for w in ws]
            d_es = [ws[hh] * dws[hh] for hh in both]
            dv = _tn(ws[0].astype(BF16), doms[0]) + _tn(ws[1].astype(BF16), doms[1])
            dl1ms = [_hi_lo_dot(d_es[hh], tri_lt) + c_des[hh] for hh in both]
            dzs = [d_es[hh] * jnp.exp(l1m_all[hh]) - dl1ms[hh] * jnp.exp(lbs[hh]) for hh in both]
            if diagonal:
                dzs = [jnp.where(valid, dz, 0.0) for dz in dzs]
            dzs = [dz.astype(BF16) for dz in dzs]
            dq = dq + jnp.dot(dzs[0], kks[0], preferred_element_type=F32) + jnp.dot(dzs[1], kks[1],
                                                                                    preferred_element_type=F32)
            dk_ref[pl.ds(off, ATT_BLK), :] += _tn(dzs[0], qms[0]) + _tn(dzs[1], qms[1])
            dv_ref[pl.ds(off, ATT_BLK), :] += dv
            return (dq,
                    pres[0] + jnp.sum(l1ms[0], axis=-1, keepdims=True), c_des[0] + jnp.sum(d_es[0], axis=-1, keepdims=True),
                    pres[1] + jnp.sum(l1ms[1], axis=-1, keepdims=True), c_des[1] + jnp.sum(d_es[1], axis=-1, keepdims=True))

        zero = jnp.zeros((ATT_BLK, 1), F32)
        carry = lax.fori_loop(0, qi, lambda kb, cr: block(kb, cr, False),
                              (jnp.zeros((ATT_BLK, LANES), F32), zero, zero, zero, zero))
        carry = block(qi, carry, True)
        dq_ref[...] = carry[0] * scale

    return pl.pallas_call(
        body, name="sb_attn_bwd", grid=(SB_HEADS // 2, nq),
        in_specs=[pl.BlockSpec((ATT_BLK, LANES), lambda p, i: (i, p)),
                  pl.BlockSpec((t, LANES), lambda p, i: (0, p)),
                  pl.BlockSpec((t, LANES), lambda p, i: (0, p)),
                  pl.BlockSpec((ATT_BLK, 2 * LANES), lambda p, i: (i, p)),
                  pl.BlockSpec((ATT_BLK, LANES), lambda p, i: (i, p))],
        out_specs=[pl.BlockSpec((ATT_BLK, LANES), lambda p, i: (i, p)),
                   pl.BlockSpec((t, LANES), lambda p, i: (0, p)),
                   pl.BlockSpec((t, LANES), lambda p, i: (0, p))],
        out_shape=[jax.ShapeDtypeStruct((t, SB_WIDTH), F32)] * 3,
        compiler_params=pltpu.CompilerParams(dimension_semantics=("arbitrary", "arbitrary")),
    )(q, k, v, tot, do)


@jax.custom_vjp
def _sb_attention(q, k, v):
    return _sb_fwd(q, k, v)[0]


def _sb_attention_fwd(q, k, v):
    o, tot = _sb_fwd(q, k, v)
    return o, (q, k, v, tot)


def _sb_attention_bwd(res, do):
    return tuple(_sb_bwd(*res, do))


_sb_attention.defvjp(_sb_attention_fwd, _sb_attention_bwd)


def _mla_fwd(qn, qr, kn, kr, v):
    t = qn.shape[0]
    nq = t // ATT_BLK
    scale = MLA_QK ** -0.5

    def body(qn_ref, qr_ref, kn_ref, kr_ref, v_ref, o_ref, lse_ref):
        qi = pl.program_id(1)
        lanes = [slice(hh * LANES, (hh + 1) * LANES) for hh in range(2)]
        qnb = [qn_ref[:, sl].astype(BF16) for sl in lanes]
        qrb = [qr_ref[:, sl].astype(BF16) for sl in lanes]

        def block(kb, carry, diagonal):
            off = pl.multiple_of(kb * ATT_BLK, ATT_BLK)
            krb = kr_ref[pl.ds(off, ATT_BLK), :].astype(BF16)
            both = range(2)
            accs, ms, ls = [carry[0], carry[3]], [carry[1], carry[4]], [carry[2], carry[5]]
            ss = [(_nt(qnb[hh], kn_ref[pl.ds(off, ATT_BLK), lanes[hh]].astype(BF16)) + _nt(qrb[hh], krb)) * scale
                  for hh in both]
            if diagonal:
                causal = (lax.broadcasted_iota(jnp.int32, (ATT_BLK, ATT_BLK), 1)
                          <= lax.broadcasted_iota(jnp.int32, (ATT_BLK, ATT_BLK), 0))
                ss = [jnp.where(causal, s, -jnp.inf) for s in ss]
            m_new = [jnp.maximum(ms[hh], jnp.max(ss[hh], axis=-1, keepdims=True)) for hh in both]
            ps = [jnp.exp(ss[hh] - m_new[hh]) for hh in both]
            alphas = [jnp.exp(ms[hh] - m_new[hh]) for hh in both]
            pvs = [jnp.dot(ps[hh].astype(BF16), v_ref[pl.ds(off, ATT_BLK), lanes[hh]].astype(BF16),
                           preferred_element_type=F32) for hh in both]
            out = []
            for hh in both:
                out += [accs[hh] * alphas[hh] + pvs[hh], m_new[hh],
                        ls[hh] * alphas[hh] + jnp.sum(ps[hh], axis=-1, keepdims=True)]
            return tuple(out)

        init = (jnp.zeros((ATT_BLK, LANES), F32), jnp.full((ATT_BLK, 1), -jnp.inf, F32), jnp.zeros((ATT_BLK, 1), F32))
        carry = block(qi, init + init, True)
        carry = lax.fori_loop(0, qi, lambda kb, cr: block(kb, cr, False), carry)
        for hh in range(2):
            acc, m, l = carry[3 * hh:3 * hh + 3]
            o_ref[:, lanes[hh]] = acc / l
            lse_ref[:, lanes[hh]] = jnp.broadcast_to(m + jnp.log(l), (ATT_BLK, LANES))

    blk = pl.BlockSpec((ATT_BLK, 2 * LANES), lambda p, i: (i, p))
    full = pl.BlockSpec((t, 2 * LANES), lambda p, i: (0, p))
    return pl.pallas_call(
        body, name="mla_attn_fwd", grid=(MLA_HEADS // 2, nq),
        in_specs=[blk, blk, full, pl.BlockSpec((t, LANES), lambda p, i: (0, 0)), full],
        out_specs=[blk, blk],
        out_shape=[jax.ShapeDtypeStruct((t, MLA_HEADS * LANES), F32)] * 2,
        compiler_params=pltpu.CompilerParams(dimension_semantics=("arbitrary", "arbitrary")),
    )(qn, qr, kn, kr, v)


def _mla_bwd(qn, qr, kn, kr, v, o, lse, do):
    t = qn.shape[0]
    nq = t // ATT_BLK
    scale = MLA_QK ** -0.5

    def body(qn_ref, qr_ref, kn_ref, kr_ref, v_ref, o_ref, lse_ref, do_ref,
             dqn_ref, dqr_ref, dkn_ref, dkr_ref, dv_ref):
        pair = pl.program_id(0)
        qi = pl.program_id(1)

        @pl.when(qi == 0)
        def _():
            dkn_ref[...] = jnp.zeros_like(dkn_ref)
            dv_ref[...] = jnp.zeros_like(dv_ref)

        @pl.when((qi == 0) & (pair == 0))
        def _():
            dkr_ref[...] = jnp.zeros_like(dkr_ref)

        lanes = [slice(hh * LANES, (hh + 1) * LANES) for hh in range(2)]
        qnb = [qn_ref[:, sl].astype(BF16) for sl in lanes]
        qrb = [qr_ref[:, sl].astype(BF16) for sl in lanes]
        dob = [do_ref[:, sl].astype(BF16) for sl in lanes]
        delta = [jnp.sum(do_ref[:, sl] * o_ref[:, sl], axis=-1, keepdims=True) for sl in lanes]
        lse_v = [lse_ref[:, hh * LANES:hh * LANES + 1] for hh in range(2)]

        def block(kb, carry, diagonal):
            off = pl.multiple_of(kb * ATT_BLK, ATT_BLK)
            krb = kr_ref[pl.ds(off, ATT_BLK), :].astype(BF16)
            both = range(2)
            knb = [kn_ref[pl.ds(off, ATT_BLK), lanes[hh]].astype(BF16) for hh in both]
            vb = [v_ref[pl.ds(off, ATT_BLK), lanes[hh]].astype(BF16) for hh in both]
            ss = [_nt(qnb[hh], knb[hh]) + _nt(qrb[hh], krb) for hh in both]
            dps = [_nt(dob[hh], vb[hh]) for hh in both]
            ps = [jnp.exp(ss[hh] * scale - lse_v[hh]) for hh in both]
            if diagonal:
                causal = (lax.broadcasted_iota(jnp.int32, (ATT_BLK, ATT_BLK), 1)
                          <= lax.broadcasted_iota(jnp.int32, (ATT_BLK, ATT_BLK), 0))
                ps = [jnp.where(causal, p, 0.0) for p in ps]
            dss = [(ps[hh] * (dps[hh] - delta[hh]) * scale).astype(BF16) for hh in both]
            for hh in both:
                dv_ref[pl.ds(off, ATT_BLK), lanes[hh]] += _tn(ps[hh].astype(BF16), dob[hh])
            for hh in both:
                dkn_ref[pl.ds(off, ATT_BLK), lanes[hh]] += _tn(dss[hh], qnb[hh])
            dkr_ref[pl.ds(off, ATT_BLK), :] += _tn(dss[0], qrb[0]) + _tn(dss[1], qrb[1])
            out = []
            for hh in both:
                out += [carry[2 * hh] + jnp.dot(dss[hh], knb[hh], preferred_element_type=F32),
                        carry[2 * hh + 1] + jnp.dot(dss[hh], krb, preferred_element_type=F32)]
            return tuple(out)

        zero = jnp.zeros((ATT_BLK, LANES), F32)
        carry = lax.fori_loop(0, qi, lambda kb, cr: block(kb, cr, False), (zero, zero, zero, zero))
        carry = block(qi, carry, True)
        for hh in range(2):
            dqn_ref[:, lanes[hh]] = carry[2 * hh]
            dqr_ref[:, lanes[hh]] = carry[2 * hh + 1]

    blk = pl.BlockSpec((ATT_BLK, 2 * LANES), lambda p, i: (i, p))
    full = pl.BlockSpec((t, 2 * LANES), lambda p, i: (0, p))
    shared = pl.BlockSpec((t, LANES), lambda p, i: (0, 0))
    wide = jax.ShapeDtypeStruct((t, MLA_HEADS * LANES), F32)
    return pl.pallas_call(
        body, name="mla_attn_bwd", grid=(MLA_HEADS // 2, nq),
        in_specs=[blk, blk, full, shared, full, blk, blk, blk],
        out_specs=[blk, blk, full, shared, full],
        out_shape=[wide, wide, wide, jax.ShapeDtypeStruct((t, LANES), F32), wide],
        compiler_params=pltpu.CompilerParams(dimension_semantics=("arbitrary", "arbitrary")),
    )(qn, qr, kn, kr, v, o, lse, do)


@jax.custom_vjp
def _mla_attention(qn, qr, kn, kr, v):
    return _mla_fwd(qn, qr, kn, kr, v)[0]


def _mla_attention_fwd(qn, qr, kn, kr, v):
    o, lse = _mla_fwd(qn, qr, kn, kr, v)
    return o, (qn, qr, kn, kr, v, o, lse)


def _mla_attention_bwd(res, do):
    return tuple(_mla_bwd(*res, do))


_mla_attention.defvjp(_mla_attention_fwd, _mla_attention_bwd)


def _split_cols(x, cuts):
    cuts = tuple(cuts)

    @jax.custom_vjp
    def op(x):
        return tuple(x[:, a:b] for a, b in zip((0,) + cuts, cuts + (x.shape[1],)))

    def fwd(x):
        return op(x), None

    def bwd(_, cts):
        return (jnp.concatenate(cts, axis=1),)

    op.defvjp(fwd, bwd)
    return op(x)


def _swap_halves(w):
    half = w.shape[-1] // 2
    return jnp.concatenate([w[..., half:], w[..., :half]], axis=-1)


def _pad_lanes(w):
    return jnp.concatenate([w, jnp.zeros(w.shape[:-1] + (LANES - w.shape[-1],), w.dtype)], axis=-1)


def _join_cols(shards):
    return shards.transpose(1, 0, 2).reshape(shards.shape[1], -1)


def _mod_parts(mod):
    return [mod[:, i * D_MODEL:(i + 1) * D_MODEL] for i in range(N_MOD)]


def _local_loss(x, mod, p, cos, sin, target):
    return _ffn_stage(x, _mixing_stage(x, mod, p, cos, sin), mod, p, target)


def _mixing_stage(x, mod, p, cos, sin):
    shift1, scale1 = _mod_parts(mod)[:2]

    w_in = _join_cols(p["w_in"])
    k_rope_w = w_in[:, 2176:2240]
    w_in_ext = jnp.concatenate([w_in[:, :2176], _pad_lanes(k_rope_w), _pad_lanes(_swap_halves(k_rope_w)),
                                jnp.zeros((D_MODEL, LANES), w_in.dtype)], axis=1)
    (h1,) = _make_rowwise("pre_attn", _f_pre_attn, 1, 3, [D_MODEL], [True])(x, p["norm_attn"], scale1, shift1)
    proj = _make_linear("in_proj", 512, 640)(h1, w_in_ext)
    q_sb, k_sb, v_sb, cq, ckv, kr, kr_sw, _ = _split_cols(proj, (512, 1024, 1536, 1920, 2176, 2304, 2432))

    o_sb = _sb_attention(q_sb, k_sb, v_sb)

    wq = _join_cols(p["w_q_up"]).reshape(MLA_Q_RANK, MLA_HEADS, MLA_QK)
    wq_n, wq_r = wq[:, :, :MLA_NOPE], wq[:, :, MLA_NOPE:]
    w_q_ext = jnp.concatenate([wq_n.reshape(MLA_Q_RANK, -1), _pad_lanes(wq_r).reshape(MLA_Q_RANK, -1),
                               _pad_lanes(_swap_halves(wq_r)).reshape(MLA_Q_RANK, -1)], axis=1)
    wkv = _join_cols(p["w_kv_up"]).reshape(MLA_KV_RANK, MLA_HEADS, MLA_NOPE + MLA_V)
    w_kv_ext = jnp.concatenate([wkv[:, :, :MLA_NOPE].reshape(MLA_KV_RANK, -1),
                                wkv[:, :, MLA_NOPE:].reshape(MLA_KV_RANK, -1)], axis=1)
    cqn, ckvn = _make_rowwise("mla_a", _f_mla_a, 2, 2, [MLA_Q_RANK, MLA_KV_RANK], [True, True])(
        cq, ckv, p["q_a_norm"], p["kv_a_norm"])
    qall = _make_linear("q_up", 384, 768)(cqn, w_q_ext)
    kvall = _make_linear("kv_up", 256, 1024)(ckvn, w_kv_ext)
    kn_all, v_mla = _split_cols(kvall, (512,))
    gq = p["q_norm"]
    gkr = p["k_rope_norm"]
    qn, qr, kn, krr = _make_rowwise("mla_b", _f_mla_b, 6, 6, [512, 512, 512, LANES],
                                    [True, True, True, True, False, False])(
        qall, kn_all, kr, kr_sw, cos, sin,
        gq[:, :MLA_NOPE], _pad_lanes(gq[:, MLA_NOPE:]), _pad_lanes(_swap_halves(gq[:, MLA_NOPE:])),
        p["k_nope_norm"], _pad_lanes(gkr), _pad_lanes(_swap_halves(gkr)))
    o_mla = _mla_attention(qn, qr, kn, krr, v_mla)

    (mixed,) = _make_rowwise("post_attn", _f_post_attn, 2, 2, [D_MODEL], [True, True])(
        o_sb, o_mla, p["out_norm_sb"], p["out_norm_mla"])
    return mixed


def _ffn_stage(x, mixed, mod, p, target):
    _, _, gate1, shift2, scale2, gate2 = _mod_parts(mod)
    attn = _make_linear("out_proj", 512, 512)(mixed, p["w_out"].reshape(D_MODEL, D_MODEL))

    x2, h2 = _make_rowwise("pre_ffn", _f_pre_ffn, 2, 4, [D_MODEL, D_MODEL], [True, True])(
        x, attn, gate1, p["norm_ffn"], scale2, shift2)
    gt = _make_linear_sharded("ffn_gate", 512)(h2, p["w_gate"])
    up = _make_linear_sharded("ffn_up", 512)(h2, p["w_up"])
    (act,) = _make_rowwise("swiglu", _f_swiglu, 2, 0, [N_CHIPS * FF_SHARD_PAD], [True, True])(gt, up)
    ffn = _make_linear("ffn_down", 256, 1024)(act, p["w_down"].reshape(N_CHIPS * FF_SHARD_PAD, D_MODEL))
    (row_loss,) = _make_rowwise("loss", _f_loss, 3, 1, [1], [True, True, False])(x2, ffn, target, gate2)
    return 0.5 * jnp.sum(row_loss)


def _my_place():
    return lax.axis_index("x"), lax.axis_index("y"), lax.axis_index("c")


def _all_gather_small(block, name):
    m_per, n = block.shape

    def body(x_ref, out_ref, send_sems, recv_sems, local_sem):
        x, y, c = _my_place()
        me, sibling = (x, y, c), (x, y, 1 - c)
        chips = [(1 - x, y), (x, 1 - y), (1 - x, 1 - y)]

        def rows(px, py, pc):
            return out_ref.at[pl.ds((4 * px + 2 * py + pc) * m_per, m_per), :]

        def copy(k, blk, to, src=None):
            return pltpu.make_async_remote_copy(
                src_ref=rows(*blk) if src is None else src, dst_ref=rows(*blk),
                send_sem=send_sems.at[k], recv_sem=recv_sems.at[k], device_id=to, device_id_type=MESH)

        mine = pltpu.make_async_copy(x_ref, rows(*me), local_sem)
        mine.start()
        first = [copy(0, me, sibling, src=x_ref)]
        first += [copy(1 + j, me, (*chip, c), src=x_ref) for j, chip in enumerate(chips)]
        for cp in first:
            cp.start()
        passed = [copy(4 + j, (*chip, c), sibling) for j, chip in enumerate(chips)]
        for j, chip in enumerate(chips):
            copy(1 + j, (*chip, c), me).wait_recv()
            passed[j].start()
        copy(0, sibling, me).wait_recv()
        for j, chip in enumerate(chips):
            copy(4 + j, (*chip, 1 - c), me).wait_recv()
        for cp in first + passed:
            cp.wait_send()
        mine.wait()

    return pl.pallas_call(
        body, name=name,
        out_shape=jax.ShapeDtypeStruct((N_DEV * m_per, n), block.dtype),
        in_specs=[pl.BlockSpec(memory_space=pltpu.VMEM)],
        out_specs=pl.BlockSpec(memory_space=pltpu.VMEM),
        scratch_shapes=[pltpu.SemaphoreType.DMA((7,)), pltpu.SemaphoreType.DMA((7,)), pltpu.SemaphoreType.DMA],
    )(block)


EARLY = ("w_in", "w_q_up", "w_kv_up")
LATE = ("w_out", "w_gate", "w_up", "w_down")
BIG = EARLY + LATE
HALF_AXIS = {"w_in": 0, "w_q_up": 0, "w_kv_up": 0, "w_out": 0, "w_gate": 0, "w_up": 0, "w_down": 1}


def _half(ref, h, axis, lead=()):
    trail = ref.shape[len(lead):]
    idx = list(lead) + [slice(None)] * len(trail)
    at = len(trail) - 2 + axis
    n2 = trail[at] // 2
    idx[len(lead) + at] = pl.ds(h * n2, n2)
    return ref.at[tuple(idx)]


def _half_shape(shape, axis):
    shape = list(shape)
    shape[len(shape) - 2 + axis] //= 2
    return tuple(shape)


def _remote(src, dst, send_sems, recv_sems, k, to):
    return pltpu.make_async_remote_copy(src_ref=src, dst_ref=dst, send_sem=send_sems.at[k],
                                        recv_sem=recv_sems.at[k], device_id=to, device_id_type=MESH)


def _gather_weights(names, shards, after):
    n_w = len(shards)
    axes = [HALF_AXIS[n] for n in names]

    def body(*refs):
        w_refs, out_refs, token = refs[:n_w], refs[n_w + 1:2 * n_w + 1], refs[2 * n_w + 1]
        send_sems, recv_sems, local_sems = refs[2 * n_w + 2:]
        token[...] = jnp.zeros_like(token)
        x, y, c = _my_place()
        sibling = (x, y, 1 - c)
        chips = [(1 - x, y), (x, 1 - y), (1 - x, 1 - y)]
        me = 2 * x + y
        mine =[pltpu.make_async_copy(w, o.at[me], local_sems.at[i]) for i, (w, o) in enumerate(zip(w_refs, out_refs))]
        for cp in mine:
            cp.start()
        first = [_remote(_half(w_refs[i], c, axes[i]), _half(out_refs[i], c, axes[i], (me,)),
                         send_sems, recv_sems, 6 * i + j, (*chip, c))
                 for i in range(n_w) for j, chip in enumerate(chips)]
        for cp in first:
            cp.start()
        passed = []
        for j, (cx, cy) in enumerate(chips):
            for i in range(n_w):
                blk = _half(out_refs[i], c, axes[i], (2 * cx + cy,))
                _remote(blk, blk, send_sems, recv_sems, 6 * i + j, (cx, cy, c)).wait_recv()
                cp = _remote(blk, blk, send_sems, recv_sems, 6 * i + 3 + j, sibling)
                cp.start()
                passed.append(cp)
        for j, (cx, cy) in enumerate(chips):
            for i in range(n_w):
                blk = _half(out_refs[i], 1 - c, axes[i], (2 * cx + cy,))
                _remote(blk, blk, send_sems, recv_sems, 6 * i + 3 + j, sibling).wait_recv()
        for cp in first + passed:
            cp.wait_send()
        for cp in mine:
            cp.wait()

    outs = pl.pallas_call(
        body, name="gather_weights",
        out_shape=[jax.ShapeDtypeStruct((N_CHIPS,) + s.shape, s.dtype) for s in shards]
        + [jax.ShapeDtypeStruct((8, LANES), F32)],
        in_specs=[ANY] * (n_w + 1), out_specs=[ANY] * n_w + [pl.BlockSpec(memory_space=pltpu.VMEM)],
        scratch_shapes=[pltpu.SemaphoreType.DMA((6 * n_w,)), pltpu.SemaphoreType.DMA((6 * n_w,)),
                        pltpu.SemaphoreType.DMA((n_w,))],
    )(*shards, after)
    return outs[:n_w], outs[n_w]


def _pair_exchange(names, grads, call_name):
    n_w = len(grads)
    axes = [HALF_AXIS[n] for n in names]

    def body(*refs):
        g_refs, t_refs = refs[:n_w], refs[n_w:2 * n_w]
        send_sems, recv_sems = refs[2 * n_w:]
        x, y, c = _my_place()
        sends = [_remote(_half(g_refs[i], 1 - c, axes[i]), t_refs[i], send_sems, recv_sems, i, (x, y, 1 - c))
                 for i in range(n_w)]
        for cp in sends:
            cp.start()
        for cp in sends:
            cp.wait_recv()
        for cp in sends:
            cp.wait_send()

    return pl.pallas_call(
        body, name=call_name,
        out_shape=[jax.ShapeDtypeStruct(_half_shape(g.shape, a), g.dtype) for g, a in zip(grads, axes)],
        in_specs=[ANY] * n_w, out_specs=[ANY] * n_w,
        scratch_shapes=[pltpu.SemaphoreType.DMA((n_w,)), pltpu.SemaphoreType.DMA((n_w,))],
    )(*grads)


def _chip_scatter(pair_sums):
    n_w = len(pair_sums)

    def body(*refs):
        s_refs, p_refs = refs[:n_w], refs[n_w:2 * n_w]
        send_sems, recv_sems = refs[2 * n_w:]
        x, y, c = _my_place()
        chips = [(1 - x, y), (x, 1 - y), (1 - x, 1 - y)]
        sends = [_remote(s_refs[i].at[2 * cx + cy], p_refs[i].at[j], send_sems, recv_sems, 3 * i + j, (cx, cy, c))
                 for i in range(n_w) for j, (cx, cy) in enumerate(chips)]
        for cp in sends:
            cp.start()
        for cp in sends:
            cp.wait_recv()
        for cp in sends:
            cp.wait_send()

    return pl.pallas_call(
        body, name="grad_chip_scatter",
        out_shape=[jax.ShapeDtypeStruct((N_CHIPS - 1,) + s.shape[1:], s.dtype) for s in pair_sums],
        in_specs=[ANY] * n_w, out_specs=[ANY] * n_w,
        scratch_shapes=[pltpu.SemaphoreType.DMA((3 * n_w,)), pltpu.SemaphoreType.DMA((3 * n_w,))],
    )(*pair_sums)


def _sibling_join(halves):
    n_w = len(halves)

    def body(*refs):
        s_refs, j_refs = refs[:n_w], refs[n_w:2 * n_w]
        send_sems, recv_sems = refs[2 * n_w:]
        x, y, c = _my_place()
        sends = [_remote(s_refs[i], j_refs[i], send_sems, recv_sems, i, (x, y, 1 - c)) for i in range(n_w)]
        for cp in sends:
            cp.start()
        for cp in sends:
            cp.wait_recv()
        for cp in sends:
            cp.wait_send()

    return pl.pallas_call(
        body, name="grad_sibling_join",
        out_shape=[jax.ShapeDtypeStruct(s.shape, s.dtype) for s in halves],
        in_specs=[ANY] * n_w, out_specs=[ANY] * n_w,
        scratch_shapes=[pltpu.SemaphoreType.DMA((n_w,)), pltpu.SemaphoreType.DMA((n_w,))],
    )(*halves)


HBM_SPEC = pl.BlockSpec(memory_space=pltpu.HBM)
SEM_SPEC = pl.BlockSpec(memory_space=pltpu.SEMAPHORE)
DATAFLOW = pltpu.SideEffectType.DATAFLOW_SIDE_EFFECTING


def _in_hbm(a):
    return pltpu.with_memory_space_constraint(a, pltpu.HBM)


def _exchange_start(name, srcs, lands, plan, n_copies):
    n = len(srcs)

    def body(*refs):
        src_refs, land_refs = refs[:n], refs[n:2 * n]
        send_sems, recv_sems = refs[2 * n], refs[2 * n + 1]
        token = refs[-1]
        for k, (src, dst, to) in enumerate(plan(src_refs, land_refs)):
            _remote(src, dst, send_sems, recv_sems, k, to).start()
        token[...] = jnp.zeros_like(token)

    outs = pl.pallas_call(
        body, name=name,
        out_shape=(pltpu.SemaphoreType.DMA((n_copies,)), pltpu.SemaphoreType.DMA((n_copies,)),
                   *[pltpu.HBM(a.shape, a.dtype) for a in srcs], *[pltpu.HBM(a.shape, a.dtype) for a in lands],
                   jax.ShapeDtypeStruct((8, LANES), F32)),
        in_specs=[HBM_SPEC] * (2 * n),
        out_specs=(SEM_SPEC, SEM_SPEC, *[HBM_SPEC] * (2 * n), pl.BlockSpec(memory_space=pltpu.VMEM)),
        input_output_aliases={i: 2 + i for i in range(2 * n)},
        compiler_params=pltpu.CompilerParams(has_side_effects=DATAFLOW),
    )(*[_in_hbm(a) for a in srcs], *[_in_hbm(a) for a in lands])
    return outs[0], outs[1], outs[2:2 + n], outs[2 + n:2 + 2 * n], outs[-1]


def _exchange_wait(name, started, plan, after):
    send_sems, recv_sems, srcs, lands, _ = started
    n = len(srcs)

    def body(*refs):
        src_refs, land_refs = refs[:n], refs[n:2 * n]
        s_sems, r_sems = refs[2 * n], refs[2 * n + 1]
        for k, (src, dst, to) in enumerate(plan(src_refs, land_refs)):
            cp = _remote(src, dst, s_sems, r_sems, k, to)
            cp.wait_send()
            cp.wait_recv()

    outs = pl.pallas_call(
        body, name=name,
        out_shape=tuple(pltpu.HBM(a.shape, a.dtype) for a in list(srcs) + list(lands)),
        in_specs=[HBM_SPEC] * (2 * n) + [SEM_SPEC, SEM_SPEC, ANY],
        out_specs=tuple([HBM_SPEC] * (2 * n)),
        input_output_aliases={i: i for i in range(2 * n)},
        compiler_params=pltpu.CompilerParams(has_side_effects=DATAFLOW),
    )(*srcs, *lands, send_sems, recv_sems, after)
    return outs[:n], outs[n:]


def _late_gather_plan(src_refs, land_refs):
    x, y, c = _my_place()
    chips = [(1 - x, y), (x, 1 - y), (1 - x, 1 - y)]
    return [(src, land.at[2 * x + y], (cx, cy, c)) for src, land in zip(src_refs, land_refs) for cx, cy in chips]


def _late_scatter_plan(src_refs, land_refs):
    x, y, c = _my_place()
    chips = [(1 - x, y), (x, 1 - y), (1 - x, 1 - y)]
    return [(src.at[2 * cx + cy], land.at[j], (cx, cy, c))
            for src, land in zip(src_refs, land_refs) for j, (cx, cy) in enumerate(chips)]


def _row_tile(rows, mult=16):
    return max(d for d in range(mult, ROW_TILE + 1, mult) if rows % d == 0)


def _pair_sum(place, g, theirs, axis, name):
    nj, rr, cc = theirs.shape
    tr = _row_tile(rr)
    nb = rr // tr
    if axis == 0:
        g_map = lambda j, i, pr: (j, pr[0] * nb + i, 0)
    else:
        g_map = lambda j, i, pr: (j, i, pr[0])

    def body(pr, g_ref, t_ref, o_ref):
        o_ref[...] = (g_ref[...].astype(F32) + t_ref[...].astype(F32)).astype(BF16)

    spec = pl.BlockSpec((None, tr, cc), lambda j, i, pr: (j, i, 0))
    return pl.pallas_call(
        body, name=name,
        grid_spec=pltpu.PrefetchScalarGridSpec(
            num_scalar_prefetch=1, grid=(nj, nb),
            in_specs=[pl.BlockSpec((None, tr, cc), g_map), spec], out_specs=spec),
        out_shape=jax.ShapeDtypeStruct(theirs.shape, BF16))(place, g, theirs)


def _chip_sum(place, pair_sums, parts, name):
    _, rr, cc = parts.shape
    tr = _row_tile(rr)

    def body(pr, h_ref, p_ref, o_ref):
        acc = p_ref[0].astype(F32)
        for j in range(1, N_CHIPS - 1):
            acc = acc + p_ref[j].astype(F32)
        o_ref[...] = (acc + h_ref[...].astype(F32)).astype(BF16)

    return pl.pallas_call(
        body, name=name,
        grid_spec=pltpu.PrefetchScalarGridSpec(
            num_scalar_prefetch=1, grid=(rr // tr,),
            in_specs=[pl.BlockSpec((None, tr, cc), lambda i, pr: (pr[1], i, 0)),
                      pl.BlockSpec((N_CHIPS - 1, tr, cc), lambda i, pr: (0, i, 0))],
            out_specs=pl.BlockSpec((tr, cc), lambda i, pr: (i, 0))),
        out_shape=jax.ShapeDtypeStruct((rr, cc), BF16))(place, pair_sums, parts)


def _silu(v):
    return v / (1.0 + jnp.exp(-v))


def _ada_fwd(c_all, w_shard, b_shard):
    def body(c_ref, w_ref, b_ref, o_ref):
        o_ref[...] = jnp.dot(_silu(c_ref[...]), w_ref[...], precision=lax.Precision.HIGHEST,
                             preferred_element_type=F32) + b_ref[...]

    return pl.pallas_call(body, name="ada_fwd", out_shape=jax.ShapeDtypeStruct((c_all.shape[0], w_shard.shape[1]), F32),
                          compiler_params=pltpu.CompilerParams(vmem_limit_bytes=MM_VMEM_LIMIT))(c_all, w_shard, b_shard)


def _ada_bwd(c_all, dmod_cols):
    def body(c_ref, d_ref, o_ref):
        o_ref[...] = lax.dot_general(_silu(c_ref[...]), d_ref[...], (((0,), (0,)), ((), ())),
                                     precision=lax.Precision.HIGHEST, preferred_element_type=F32)

    return pl.pallas_call(body, name="ada_bwd", out_shape=jax.ShapeDtypeStruct((c_all.shape[1], dmod_cols.shape[1]), F32),
                          compiler_params=pltpu.CompilerParams(vmem_limit_bytes=MM_VMEM_LIMIT))(c_all, dmod_cols)


def _adamw_math(w, g, m, v):
    m = ADAM_B1 * m + (1.0 - ADAM_B1) * g
    v = ADAM_B2 * v + (1.0 - ADAM_B2) * (g * g)
    m_hat = m / (1.0 - ADAM_B1 ** ADAM_STEP)
    v_hat = v / (1.0 - ADAM_B2 ** ADAM_STEP)
    delta = -ADAM_LR * (m_hat / (jnp.sqrt(v_hat) + ADAM_EPS) + ADAM_WD * w)
    return delta, m, v


def _adamw(w, g, m, v, name):
    r, ccols = w.shape
    tr = max(d for d in range(8, ROW_TILE + 1, 8) if r % d == 0)
    spec = pl.BlockSpec((tr, ccols), lambda i: (i, 0))

    def body(w_ref, g_ref, m_ref, v_ref, d_ref, nm_ref, nv_ref):
        d_ref[...], nm_ref[...], nv_ref[...] = _adamw_math(w_ref[...], g_ref[...], m_ref[...], v_ref[...])

    return pl.pallas_call(body, name=name, grid=(r // tr,), in_specs=[spec] * 4, out_specs=[spec] * 3,
                          out_shape=[jax.ShapeDtypeStruct(w.shape, F32)] * 3,
                          compiler_params=pltpu.CompilerParams(vmem_limit_bytes=MM_VMEM_LIMIT))(w, g, m, v)


def _adamw_small(w, g_all, m, v):
    def body(w_ref, g_ref, m_ref, v_ref, gs_ref, d_ref, nm_ref, nv_ref):
        g = g_ref[0]
        for d in range(1, N_DEV):
            g = g + g_ref[d]
        gs_ref[...] = g
        d_ref[...], nm_ref[...], nv_ref[...] = _adamw_math(w_ref[...], g, m_ref[...], v_ref[...])

    return pl.pallas_call(body, name="adamw_small", out_shape=[jax.ShapeDtypeStruct(w.shape, F32)] * 4)(w, g_all, m, v)


def _adamw_halves(place, w, own, sib, m, v, axis, name):
    r, cc = w.shape
    if axis == 0:
        rows, gc = own.shape[0], own.shape[1]
        tr = _row_tile(rows)
        nb = rows // tr
        w_spec = pl.BlockSpec((tr, cc), lambda h, i, pr: (h * nb + i, 0))
        g_spec = pl.BlockSpec((tr, gc), lambda h, i, pr: (i, 0))
    else:
        tr = _row_tile(r)
        nb = r // tr
        gc = own.shape[1]
        w_spec = pl.BlockSpec((tr, gc), lambda h, i, pr: (i, h))
        g_spec = pl.BlockSpec((tr, gc), lambda h, i, pr: (i, 0))
    wc = w_spec.block_shape[1]

    def body(pr, w_ref, o_ref, s_ref, m_ref, v_ref, g_ref, d_ref, nm_ref, nv_ref):
        g = jnp.where(pl.program_id(0) == pr[0], o_ref[...], s_ref[...]).astype(F32)[:, :wc]
        g_ref[...] = g
        d_ref[...], nm_ref[...], nv_ref[...] = _adamw_math(w_ref[...], g, m_ref[...], v_ref[...])

    return pl.pallas_call(
        body, name=name,
        grid_spec=pltpu.PrefetchScalarGridSpec(
            num_scalar_prefetch=1, grid=(2, nb),
            in_specs=[w_spec, g_spec, g_spec, w_spec, w_spec], out_specs=[w_spec] * 4),
        out_shape=[jax.ShapeDtypeStruct(w.shape, F32)] * 4,
        compiler_params=pltpu.CompilerParams(vmem_limit_bytes=MM_VMEM_LIMIT))(place, w, own, sib, m, v)


SMALL = ("b_ada", "norm_attn", "norm_ffn", "q_a_norm", "kv_a_norm", "q_norm", "k_nope_norm", "k_rope_norm",
         "out_norm_sb", "out_norm_mla")
WEIGHTS = ("w_ada", "b_ada", "norm_attn", "norm_ffn", "w_in", "q_a_norm", "w_q_up", "kv_a_norm", "w_kv_up",
           "q_norm", "k_nope_norm", "k_rope_norm", "out_norm_sb", "out_norm_mla", "w_out", "w_gate", "w_up",
           "w_down")


def kernel(x, c, positions, w_ada, b_ada, norm_attn, norm_ffn, w_in, q_a_norm, w_q_up, kv_a_norm, w_kv_up, q_norm, k_nope_norm, k_rope_norm, out_norm_sb, out_norm_mla, w_out, w_gate, w_up, w_down, loss_target, m_w_ada, m_b_ada, m_norm_attn, m_norm_ffn, m_w_in, m_q_a_norm, m_w_q_up, m_kv_a_norm, m_w_kv_up, m_q_norm, m_k_nope_norm, m_k_rope_norm, m_out_norm_sb, m_out_norm_mla, m_w_out, m_w_gate, m_w_up, m_w_down, v_w_ada, v_b_ada, v_norm_attn, v_norm_ffn, v_w_in, v_q_a_norm, v_w_q_up, v_kv_a_norm, v_w_kv_up, v_q_norm, v_k_nope_norm, v_k_rope_norm, v_out_norm_sb, v_out_norm_mla, v_w_out, v_w_gate, v_w_up, v_w_down):
    local = dict(locals())
    w = {n: local[n][0] for n in WEIGHTS}
    m = {n: local["m_" + n][0] for n in WEIGHTS}
    v = {n: local["v_" + n][0] for n in WEIGHTS}
    small = {n: w[n].reshape(1, -1) for n in SMALL}
    ix, iy, ic = _my_place()
    chip = 2 * ix + iy
    dev = 2 * chip + ic
    xs, target = x[0], loss_target[0]
    seq = xs.shape[0]

    c_all = _all_gather_small(c.reshape(8, LANES), "gather_c").reshape(N_DEV, D_MODEL)
    ada_cols = w["w_ada"].shape[1]
    b_cols = lax.dynamic_slice_in_dim(small["b_ada"], chip * ada_cols, ada_cols, axis=1)
    mod_cols = _ada_fwd(c_all, w["w_ada"], b_cols)
    mod_all = _all_gather_small(mod_cols, "gather_mod").reshape(N_CHIPS, 2, N_DEV, ada_cols)
    mod = lax.dynamic_index_in_dim(mod_all[:, 0], dev, axis=1, keepdims=False).reshape(1, N_MOD * D_MODEL)

    ff_pad = FF_SHARD_PAD - FF_SHARD
    pads = {"w_gate": ((0, 0), (0, ff_pad)), "w_up": ((0, 0), (0, ff_pad)), "w_down": ((0, ff_pad), (0, 0))}
    shards = {n: jnp.pad(w[n].astype(BF16), pads[n]) if n in pads else w[n].astype(BF16) for n in BIG}
    early, early_done = _gather_weights(EARLY, [shards[n] for n in EARLY], mod)
    gathered = dict(zip(EARLY, early))
    slot = chip + early_done[0, 0].astype(jnp.int32)
    lands = [lax.dynamic_update_index_in_dim(lax.empty((N_CHIPS,) + shards[n].shape, BF16), shards[n], slot, 0)
             for n in LATE]
    late_gather = _exchange_start("gather_late_start", [shards[n] for n in LATE], lands, _late_gather_plan,
                                  3 * len(LATE))

    half = MLA_ROPE // 2
    freqs = 1.0 / (ROPE_THETA ** (np.arange(half, dtype=np.float32) / half))
    zeros = np.zeros(LANES - MLA_ROPE, np.float32)
    freqs_row = jnp.asarray(np.concatenate([freqs, freqs, zeros]).astype(np.float32)[None])
    sign_row = jnp.asarray(np.concatenate([-np.ones(half), np.ones(half), zeros]).astype(np.float32)[None])
    cos, sin = _rope_tables(positions.reshape(seq, 1), freqs_row, sign_row)

    place = jnp.stack([ic, chip]).astype(jnp.int32)
    small_params = {n: small[n] for n in SMALL if n != "b_ada"}
    mod = mod + late_gather[4][0, 0]

    def pair_sums_of(names, grads, call_name):
        theirs = _pair_exchange(names, grads, call_name)
        return [_pair_sum(place, gr, th, HALF_AXIS[n], "grad_pair_sum_" + n) for n, gr, th in zip(names, grads, theirs)]

    p1 = {**{n: gathered[n] for n in EARLY}, **small_params}
    mixed, mixing_vjp = jax.vjp(lambda x_, mod_, p_: _mixing_stage(x_, mod_, p_, cos, sin), xs, mod, p1)
    _, landed = _exchange_wait("gather_late_wait", late_gather, _late_gather_plan, mixed)
    p2 = {**dict(zip(LATE, landed)), **small_params}
    loss_part, ffn_vjp = jax.vjp(lambda x_, mixed_, mod_, p_: _ffn_stage(x_, mixed_, mod_, p_, target), xs, mixed, mod, p2)
    gx2, gmixed, gmod2, gp2 = ffn_vjp(jnp.ones((), F32))
    late_sums = pair_sums_of(LATE, [gp2[n] for n in LATE], "grad_pair_exchange_late")
    late_scatter = _exchange_start(
        "grad_scatter_late_start", late_sums,
        [lax.empty((N_CHIPS - 1,) + s.shape[1:], BF16) for s in late_sums], _late_scatter_plan, 3 * len(LATE))
    gx1, gmod1, gp1 = mixing_vjp(gmixed + late_scatter[4][0, 0])
    gx = gx1 + gx2
    gmod = gmod1 + gmod2
    gp = {n: gp1[n] + gp2[n] for n in small_params}
    loss = lax.psum(loss_part, ("x", "y", "c"))

    early_sums = pair_sums_of(EARLY, [gp1[n] for n in EARLY], "grad_pair_exchange_early")
    early_parts = _chip_scatter(early_sums)
    late_sums, late_parts = _exchange_wait("grad_scatter_late_wait", late_scatter, _late_scatter_plan, gx)
    own = [_chip_sum(place, ps, pt, "grad_chip_sum_" + n)
           for n, ps, pt in zip(BIG, early_sums + list(late_sums), list(early_parts) + list(late_parts))]
    sib = _sibling_join(own)
    axes = [HALF_AXIS[n] for n in BIG]
    g = {}

    small_names = [n for n in SMALL if n != "b_ada"]
    small_vec = jnp.concatenate([gmod] + [gp[n] for n in small_names], axis=1)
    n_small = small_vec.shape[1]
    small_all = _all_gather_small(small_vec.reshape(8, n_small // 8), "gather_small").reshape(N_DEV, 8, n_small // 8)

    def pack_small(d):
        return jnp.concatenate([d[n].reshape(1, -1) for n in SMALL], axis=1).reshape(8, n_small // 8)

    gs, ds, ms, vs = _adamw_small(pack_small(w), small_all, pack_small(m), pack_small(v))
    sizes = [w[n].size for n in SMALL]
    offs = np.concatenate([[0], np.cumsum(sizes)])

    def unpack_small(a):
        flat = a.reshape(-1)
        return {n: flat[offs[i]:offs[i + 1]].reshape(w[n].shape) for i, n in enumerate(SMALL)}

    g.update(unpack_small(gs))
    delta, new_m, new_v = unpack_small(ds), unpack_small(ms), unpack_small(vs)

    dmod_all = small_all.reshape(N_DEV, n_small)[:, :N_MOD * D_MODEL]
    g["w_ada"] = _ada_bwd(c_all, lax.dynamic_slice_in_dim(dmod_all, chip * ada_cols, ada_cols, axis=1))

    delta["w_ada"], new_m["w_ada"], new_v["w_ada"] = _adamw(w["w_ada"], g["w_ada"], m["w_ada"], v["w_ada"], "adamw_w_ada")
    for n, o, s, a in zip(BIG, own, sib, axes):
        g[n], delta[n], new_m[n], new_v[n] = _adamw_halves(place, w[n], o, s, m[n], v[n], a, "adamw_" + n)

    def outs(d):
        return [d[n][None] for n in WEIGHTS]

    return (loss, gx[None], *outs(g), *outs(delta), *outs(new_m), *outs(new_v))
```

```python
import functools
import math

import numpy as np
import jax
import jax.numpy as jnp
from jax import lax
from jax.experimental import pallas as pl
from jax.experimental.pallas import tpu as pltpu

F32 = jnp.float32
BF16 = jnp.bfloat16
MESH = pl.DeviceIdType.MESH
ANY = pl.BlockSpec(memory_space=pl.ANY)

D_MODEL = 1024
SB_HEADS = 8
SB_HEAD_DIM = 64
SB_WIDTH = 512
MLA_HEADS = 4
MLA_NOPE = 128
MLA_ROPE = 64
MLA_QK = 192
MLA_V = 128
MLA_Q_RANK = 384
MLA_KV_RANK = 256
D_FF = 2816
N_MOD = 6
ROPE_THETA = 10000.0
EPS = 1e-6
LANES = 128

ADAM_LR = 0.001
ADAM_B1 = 0.9
ADAM_B2 = 0.999
ADAM_EPS = 1e-08
ADAM_WD = 0.01
ADAM_STEP = 10

N_CHIPS = 4
N_DEV = 8
ROW_TILE = 256
ATT_BLK = 256
MM_VMEM_LIMIT = 48 * 1024 * 1024
FF_SHARD = D_FF // N_CHIPS
FF_SHARD_PAD = 768


def _mm(a, b, mode, name, tm, tn, out_dtype=F32):
    if mode == "nn":
        (m, k), n = a.shape, b.shape[1]
        a_spec = pl.BlockSpec((tm, k), lambda j, i: (i, 0))
        b_spec = pl.BlockSpec((k, tn), lambda j, i: (0, j))
        dims = (((1,), (0,)), ((), ()))
    elif mode == "nt":
        (m, k), n = a.shape, b.shape[0]
        a_spec = pl.BlockSpec((tm, k), lambda j, i: (i, 0))
        b_spec = pl.BlockSpec((tn, k), lambda j, i: (j, 0))
        dims = (((1,), (1,)), ((), ()))
    else:
        (k, m), n = a.shape, b.shape[1]
        a_spec = pl.BlockSpec((k, tm), lambda j, i: (0, i))
        b_spec = pl.BlockSpec((k, tn), lambda j, i: (0, j))
        dims = (((0,), (0,)), ((), ()))
    assert m % tm == 0 and n % tn == 0, (name, m, n, tm, tn)

    def body(a_ref, b_ref, o_ref):
        o_ref[...] = lax.dot_general(a_ref[...].astype(BF16), b_ref[...].astype(BF16), dims,
                                     preferred_element_type=F32).astype(out_dtype)

    return pl.pallas_call(
        body, name=name, grid=(n // tn, m // tm),
        in_specs=[a_spec, b_spec],
        out_specs=pl.BlockSpec((tm, tn), lambda j, i: (i, j)),
        out_shape=jax.ShapeDtypeStruct((m, n), out_dtype),
        compiler_params=pltpu.CompilerParams(dimension_semantics=("arbitrary", "arbitrary"),
                                             vmem_limit_bytes=MM_VMEM_LIMIT),
    )(a, b)


def _make_linear(name, tk_w, tn_w):
    @jax.custom_vjp
    def op(a, w):
        return _mm(a, w, "nn", name + "_fwd", ROW_TILE, w.shape[1])

    def fwd(a, w):
        return op(a, w), (a, w)

    def bwd(res, dy):
        a, w = res
        da = _mm(dy, w, "nt", name + "_dx", ROW_TILE, w.shape[0])
        dw = _mm(a, dy, "tn", name + "_dw", tk_w, tn_w, out_dtype=BF16)
        return da, dw

    op.defvjp(fwd, bwd)
    return op


def _make_linear_sharded(name, tk_w):
    def call_fwd(a, w):
        t, k = a.shape
        n_sh, _, cc = w.shape

        def body(a_ref, w_ref, o_ref):
            o_ref[...] = jnp.dot(a_ref[...].astype(BF16), w_ref[...], preferred_element_type=F32)

        return pl.pallas_call(
            body, name=name + "_fwd", grid=(n_sh, t // ROW_TILE),
            in_specs=[pl.BlockSpec((ROW_TILE, k), lambda j, i: (i, 0)),
                      pl.BlockSpec((None, k, cc), lambda j, i: (j, 0, 0))],
            out_specs=pl.BlockSpec((ROW_TILE, cc), lambda j, i: (i, j)),
            out_shape=jax.ShapeDtypeStruct((t, n_sh * cc), F32),
            compiler_params=pltpu.CompilerParams(dimension_semantics=("arbitrary", "arbitrary"),
                                                 vmem_limit_bytes=MM_VMEM_LIMIT),
        )(a, w)

    def call_dx(dy, w):
        t = dy.shape[0]
        n_sh, k, cc = w.shape

        def body(dy_ref, w_ref, o_ref):
            acc = jnp.zeros((ROW_TILE, k), F32)
            for j in range(n_sh):
                acc = acc + _nt(dy_ref[:, j * cc:(j + 1) * cc].astype(BF16), w_ref[j])
            o_ref[...] = acc

        return pl.pallas_call(
            body, name=name + "_dx", grid=(t // ROW_TILE,),
            in_specs=[pl.BlockSpec((ROW_TILE, n_sh * cc), lambda i: (i, 0)),
                      pl.BlockSpec((n_sh, k, cc), lambda i: (0, 0, 0))],
            out_specs=pl.BlockSpec((ROW_TILE, k), lambda i: (i, 0)),
            out_shape=jax.ShapeDtypeStruct((t, k), F32),
            compiler_params=pltpu.CompilerParams(dimension_semantics=("arbitrary",),
                                                 vmem_limit_bytes=MM_VMEM_LIMIT),
        )(dy, w)

    def call_dw(a, dy, w):
        t, k = a.shape
        n_sh, _, cc = w.shape

        def body(a_ref, dy_ref, o_ref):
            o_ref[...] = _tn(a_ref[...].astype(BF16), dy_ref[...].astype(BF16)).astype(BF16)

        return pl.pallas_call(
            body, name=name + "_dw", grid=(n_sh, k // tk_w),
            in_specs=[pl.BlockSpec((t, tk_w), lambda j, i: (0, i)),
                      pl.BlockSpec((t, cc), lambda j, i: (0, j))],
            out_specs=pl.BlockSpec((None, tk_w, cc), lambda j, i: (j, i, 0)),
            out_shape=jax.ShapeDtypeStruct(w.shape, BF16),
            compiler_params=pltpu.CompilerParams(dimension_semantics=("arbitrary", "arbitrary"),
                                                 vmem_limit_bytes=MM_VMEM_LIMIT),
        )(a, dy)

    @jax.custom_vjp
    def op(a, w):
        return call_fwd(a, w)

    def fwd(a, w):
        return op(a, w), (a, w)

    def bwd(res, dy):
        a, w = res
        return call_dx(dy, w), call_dw(a, dy, w)

    op.defvjp(fwd, bwd)
    return op


def _row_spec(arr, tb):
    return pl.BlockSpec((tb, arr.shape[1]), lambda i: (i, 0))


def _full_spec(arr):
    return pl.BlockSpec(arr.shape, lambda i: (0, 0))


def _make_rowwise(name, f, n_rows, n_params, out_cols, diff_rows):
    n_out = len(out_cols)

    def call_fwd(rows, params):
        t = rows[0].shape[0]

        def body(*refs):
            ins = [r[...] for r in refs[:n_rows + n_params]]
            outs = f(*ins)
            for o_ref, o in zip(refs[n_rows + n_params:], outs):
                o_ref[...] = o

        return pl.pallas_call(
            body, name=name + "_fwd", grid=(t // ROW_TILE,),
            in_specs=[_row_spec(a, ROW_TILE) for a in rows] + [_full_spec(p) for p in params],
            out_specs=[pl.BlockSpec((ROW_TILE, n), lambda i: (i, 0)) for n in out_cols],
            out_shape=[jax.ShapeDtypeStruct((t, n), F32) for n in out_cols],
            compiler_params=pltpu.CompilerParams(dimension_semantics=("arbitrary",),
                                                 vmem_limit_bytes=MM_VMEM_LIMIT),
        )(*rows, *params)

    def call_bwd(rows, params, cts):
        t = rows[0].shape[0]
        d_rows = [a for a, d in zip(rows, diff_rows) if d]
        n_in = n_rows + n_params + n_out

        def body(*refs):
            ins = [r[...] for r in refs[:n_rows + n_params]]
            ct = tuple(r[...] for r in refs[n_rows + n_params:n_in])
            _, vjp = jax.vjp(f, *ins)
            grads = vjp(ct)
            out_refs = refs[n_in:]
            g_rows = [g for g, d in zip(grads[:n_rows], diff_rows) if d]
            for o_ref, g in zip(out_refs[:len(g_rows)], g_rows):
                o_ref[...] = g
            p_refs = out_refs[len(g_rows):]

            if p_refs:
                @pl.when(pl.program_id(0) == 0)
                def _():
                    for p_ref in p_refs:
                        p_ref[...] = jnp.zeros_like(p_ref)

                for p_ref, g in zip(p_refs, grads[n_rows:]):
                    p_ref[...] += g

        return pl.pallas_call(
            body, name=name + "_bwd", grid=(t // ROW_TILE,),
            in_specs=[_row_spec(a, ROW_TILE) for a in rows] + [_full_spec(p) for p in params]
            + [_row_spec(c, ROW_TILE) for c in cts],
            out_specs=[_row_spec(a, ROW_TILE) for a in d_rows] + [_full_spec(p) for p in params],
            out_shape=[jax.ShapeDtypeStruct(a.shape, F32) for a in d_rows]
            + [jax.ShapeDtypeStruct(p.shape, F32) for p in params],
            compiler_params=pltpu.CompilerParams(dimension_semantics=("arbitrary",),
                                                 vmem_limit_bytes=MM_VMEM_LIMIT),
        )(*rows, *params, *cts)

    @jax.custom_vjp
    def op(*args):
        return tuple(call_fwd(args[:n_rows], args[n_rows:]))

    def fwd(*args):
        return op(*args), args

    def bwd(args, cts):
        rows, params = args[:n_rows], args[n_rows:]
        outs = call_bwd(rows, params, cts)
        it = iter(outs)
        g_rows = [next(it) if d else jnp.zeros_like(a) for a, d in zip(rows, diff_rows)]
        return tuple(g_rows) + tuple(it)

    op.defvjp(fwd, bwd)
    return op


def _rms(x, g, n):
    return x * lax.rsqrt(jnp.sum(x * x, axis=-1, keepdims=True) * (1.0 / n) + EPS) * g


def _f_pre_attn(x, g, scale, shift):
    return (_rms(x, g, D_MODEL) * (1.0 + scale) + shift,)


def _f_mla_a(cq, ckv, gq, gkv):
    return _rms(cq, gq, MLA_Q_RANK), _rms(ckv, gkv, MLA_KV_RANK)


@jax.custom_vjp
def _split_lanes(x):
    return tuple(x[:, i * LANES:(i + 1) * LANES] for i in range(x.shape[1] // LANES))


def _split_lanes_fwd(x):
    return _split_lanes(x), None


def _split_lanes_bwd(_, cts):
    return (jnp.concatenate(cts, axis=1),)


_split_lanes.defvjp(_split_lanes_fwd, _split_lanes_bwd)


def _f_mla_b(qall, kn_all, kr, kr_sw, cos, sin, gqn, gqr, gqr_sw, gkn, gkr, gkr_sw):
    q = _split_lanes(qall)
    kn = _split_lanes(kn_all)
    qn_o, qr_o, kn_o = [], [], []
    for h in range(MLA_HEADS):
        qn, qr, qs = q[h], q[MLA_HEADS + h], q[2 * MLA_HEADS + h]
        ss = jnp.sum(qn * qn, axis=-1, keepdims=True) + jnp.sum(qr * qr, axis=-1, keepdims=True)
        rs = lax.rsqrt(ss * (1.0 / MLA_QK) + EPS)
        qn_o.append(qn * rs * gqn)
        qr_o.append((qr * rs * gqr) * cos + (qs * rs * gqr_sw) * sin)
        kn_o.append(_rms(kn[h], gkn, MLA_NOPE))
    rs = lax.rsqrt(jnp.sum(kr * kr, axis=-1, keepdims=True) * (1.0 / MLA_ROPE) + EPS)
    kr_o = (kr * rs * gkr) * cos + (kr_sw * rs * gkr_sw) * sin
    return (jnp.concatenate(qn_o, axis=1), jnp.concatenate(qr_o, axis=1), jnp.concatenate(kn_o, axis=1), kr_o)


def _f_post_attn(o_sb, o_mla, g_sb, g_mla):
    return (jnp.concatenate([_rms(o_sb, g_sb, SB_WIDTH), _rms(o_mla, g_mla, SB_WIDTH)], axis=1),)


def _f_pre_ffn(x, attn, gate, g, scale, shift):
    x2 = x + gate * attn
    return x2, _rms(x2, g, D_MODEL) * (1.0 + scale) + shift


def _f_swiglu(gt, up):
    return (gt / (1.0 + jnp.exp(-gt)) * up,)


def _f_loss(x2, ffn, target, gate):
    err = x2 + gate * ffn - target
    return (jnp.sum(err * err, axis=-1, keepdims=True) * (1.0 / D_MODEL),)


def _rope_tables(pos_col, freqs, sign):
    t = pos_col.shape[0]

    def body(p_ref, f_ref, s_ref, cos_ref, sin_ref):
        ang = p_ref[...].astype(F32) * f_ref[...]
        live = jnp.abs(s_ref[...])
        cos_ref[...] = jnp.cos(ang) * live
        sin_ref[...] = jnp.sin(ang) * s_ref[...]

    return pl.pallas_call(
        body, name="rope_tables", grid=(t // ROW_TILE,),
        in_specs=[pl.BlockSpec((ROW_TILE, 1), lambda i: (i, 0)), _full_spec(freqs), _full_spec(sign)],
        out_specs=[pl.BlockSpec((ROW_TILE, LANES), lambda i: (i, 0))] * 2,
        out_shape=[jax.ShapeDtypeStruct((t, LANES), F32)] * 2,
    )(pos_col, freqs, sign)


def _hi_lo_dot(x, tri):
    hi = x.astype(BF16)
    lo = (x - hi.astype(F32)).astype(BF16)
    return (jnp.dot(hi, tri, preferred_element_type=F32) + jnp.dot(lo, tri, preferred_element_type=F32))


def _tri(cmp):
    r = lax.broadcasted_iota(jnp.int32, (ATT_BLK, ATT_BLK), 0)
    c = lax.broadcasted_iota(jnp.int32, (ATT_BLK, ATT_BLK), 1)
    return cmp(r, c).astype(BF16)


def _nt(a, b):
    return lax.dot_general(a, b, (((1,), (1,)), ((), ())), preferred_element_type=F32)


def _tn(a, b):
    return lax.dot_general(a, b, (((0,), (0,)), ((), ())), preferred_element_type=F32)


def _sb_logs(z):
    lb = jnp.minimum(z, 0.0) - jnp.log(1.0 + jnp.exp(-jnp.abs(z)))
    return lb, lb - z


def _sb_fwd(q, k, v):
    t = q.shape[0]
    nq = t // ATT_BLK
    scale = SB_HEAD_DIM ** -0.5

    def body(q_ref, k_ref, v_ref, o_ref, tot_ref):
        qi = pl.program_id(1)
        lane = lax.broadcasted_iota(jnp.int32, (ATT_BLK, LANES), 1)
        tri = _tri(lambda r, c: r > c)
        qv = q_ref[...] * scale
        heads = [(lane // SB_HEAD_DIM) == hh for hh in range(2)]
        qms = [jnp.where(mine, qv, 0.0).astype(BF16) for mine in heads]

        def block(kb, carry, diagonal):
            acc, runs = carry[0], carry[1:]
            off = pl.multiple_of(kb * ATT_BLK, ATT_BLK)
            kk = k_ref[pl.ds(off, ATT_BLK), :].astype(BF16)
            v_blk = v_ref[pl.ds(off, ATT_BLK), :]
            both = range(2)
            if diagonal:
                valid = (lax.broadcasted_iota(jnp.int32, (ATT_BLK, ATT_BLK), 1)
                         < lax.broadcasted_iota(jnp.int32, (ATT_BLK, ATT_BLK), 0))
            zs = [_nt(qms[hh], kk) for hh in both]
            vvs = [jnp.where(heads[hh], v_blk, 0.0).astype(BF16) for hh in both]
            logs = [_sb_logs(z) for z in zs]
            l1ms = [jnp.where(valid, lg[1], 0.0) for lg in logs] if diagonal else [lg[1] for lg in logs]
            afters = [_hi_lo_dot(l1ms[hh], tri) for hh in both]
            ws = [jnp.exp(logs[hh][0] + (afters[hh] + runs[hh])) for hh in both]
            if diagonal:
                ws = [jnp.where(valid, w, 0.0) for w in ws]
            acc = acc + jnp.dot(ws[0].astype(BF16), vvs[0], preferred_element_type=F32) + jnp.dot(
                ws[1].astype(BF16), vvs[1], preferred_element_type=F32)
            return (acc, *[runs[hh] + jnp.sum(l1ms[hh], axis=-1, keepdims=True) for hh in both])

        zero = jnp.zeros((ATT_BLK, 1), F32)
        carry = block(qi, (jnp.zeros((ATT_BLK, LANES), F32), zero, zero), True)
        carry = lax.fori_loop(0, qi, lambda j, cr: block(qi - 1 - j, cr, False), carry)
        o_ref[...] = carry[0]
        for hh in range(2):
            tot_ref[:, hh * LANES:(hh + 1) * LANES] = jnp.broadcast_to(carry[1 + hh], (ATT_BLK, LANES))

    return pl.pallas_call(
        body, name="sb_attn_fwd", grid=(SB_HEADS // 2, nq),
        in_specs=[pl.BlockSpec((ATT_BLK, LANES), lambda p, i: (i, p)),
                  pl.BlockSpec((t, LANES), lambda p, i: (0, p)),
                  pl.BlockSpec((t, LANES), lambda p, i: (0, p))],
        out_specs=[pl.BlockSpec((ATT_BLK, LANES), lambda p, i: (i, p)),
                   pl.BlockSpec((ATT_BLK, 2 * LANES), lambda p, i: (i, p))],
        out_shape=[jax.ShapeDtypeStruct((t, SB_WIDTH), F32), jax.ShapeDtypeStruct((t, SB_HEADS * LANES), F32)],
        compiler_params=pltpu.CompilerParams(dimension_semantics=("arbitrary", "arbitrary")),
    )(q, k, v)


def _sb_bwd(q, k, v, tot, do):
    t = q.shape[0]
    nq = t // ATT_BLK
    scale = SB_HEAD_DIM ** -0.5

    def body(q_ref, k_ref, v_ref, tot_ref, do_ref, dq_ref, dk_ref, dv_ref):
        qi = pl.program_id(1)

        @pl.when(qi == 0)
        def _():
            dk_ref[...] = jnp.zeros_like(dk_ref)
            dv_ref[...] = jnp.zeros_like(dv_ref)

        lane = lax.broadcasted_iota(jnp.int32, (ATT_BLK, LANES), 1)
        tri_incl = _tri(lambda r, c: r <= c)
        tri_lt = _tri(lambda r, c: r < c)
        qv = q_ref[...] * scale
        dov = do_ref[...]
        heads = [(lane // SB_HEAD_DIM) == hh for hh in range(2)]
        qms = [jnp.where(mine, qv, 0.0).astype(BF16) for mine in heads]
        doms = [jnp.where(mine, dov, 0.0).astype(BF16) for mine in heads]
        tots = [tot_ref[:, hh * LANES:hh * LANES + 1] for hh in range(2)]

        def block(kb, carry, diagonal):
            dq = carry[0]
            off = pl.multiple_of(kb * ATT_BLK, ATT_BLK)
            k_blk = k_ref[pl.ds(off, ATT_BLK), :]
            vv = v_ref[pl.ds(off, ATT_BLK), :].astype(BF16)
            both = range(2)
            pres, c_des = [carry[1], carry[3]], [carry[2], carry[4]]
            if diagonal:
                valid = (lax.broadcasted_iota(jnp.int32, (ATT_BLK, ATT_BLK), 1)
                         < lax.broadcasted_iota(jnp.int32, (ATT_BLK, ATT_BLK), 0))
            kks = [jnp.where(heads[hh], k_blk, 0.0).astype(BF16) for hh in both]
            zs = [_nt(qms[hh], kks[hh]) for hh in both]
            dws = [_nt(doms[hh], vv) for hh in both]
            logs = [_sb_logs(z) for z in zs]
            lbs = [lg[0] for lg in logs]
            l1m_all = [lg[1] for lg in logs]
            l1ms = [jnp.where(valid, a, 0.0) for a in l1m_all] if diagonal else l1m_all
            prefix = [_hi_lo_dot(l1ms[hh], tri_incl) for hh in both]
            ws = [jnp.exp(lbs[hh] + (tots[hh] - (prefix[hh] + pres[hh]))) for hh in both]
            if diagonal:
                ws = [jnp.where(valid, w, 0.0) for w in ws]
            d_es = [ws[hh] * dws[hh] for hh in both]
            dv = _tn(ws[0].astype(BF16), doms[0]) + _tn(ws[1].astype(BF16), doms[1])
            dl1ms = [_hi_lo_dot(d_es[hh], tri_lt) + c_des[hh] for hh in both]
            dzs = [d_es[hh] * jnp.exp(l1m_all[hh]) - dl1ms[hh] * jnp.exp(lbs[hh]) for hh in both]
            if diagonal:
                dzs = [jnp.where(valid, dz, 0.0) for dz in dzs]
            dzs = [dz.astype(BF16) for dz in dzs]
            dq = dq + jnp.dot(dzs[0], kks[0], preferred_element_type=F32) + jnp.dot(dzs[1], kks[1],
                                                                                    preferred_element_type=F32)
            dk_ref[pl.ds(off, ATT_BLK), :] += _tn(dzs[0], qms[0]) + _tn(dzs[1], qms[1])
            dv_ref[pl.ds(off, ATT_BLK), :] += dv
            return (dq,
                    pres[0] + jnp.sum(l1ms[0], axis=-1, keepdims=True), c_des[0] + jnp.sum(d_es[0], axis=-1, keepdims=True),
                    pres[1] + jnp.sum(l1ms[1], axis=-1, keepdims=True), c_des[1] + jnp.sum(d_es[1], axis=-1, keepdims=True))

        zero = jnp.zeros((ATT_BLK, 1), F32)
        carry = lax.fori_loop(0, qi, lambda kb, cr: block(kb, cr, False),
                              (jnp.zeros((ATT_BLK, LANES), F32), zero, zero, zero, zero))
        carry = block(qi, carry, True)
        dq_ref[...] = carry[0] * scale

    return pl.pallas_call(
        body, name="sb_attn_bwd", grid=(SB_HEADS // 2, nq),
        in_specs=[pl.BlockSpec((ATT_BLK, LANES), lambda p, i: (i, p)),
                  pl.BlockSpec((t, LANES), lambda p, i: (0, p)),
                  pl.BlockSpec((t, LANES), lambda p, i: (0, p)),
                  pl.BlockSpec((ATT_BLK, 2 * LANES), lambda p, i: (i, p)),
                  pl.BlockSpec((ATT_BLK, LANES), lambda p, i: (i, p))],
        out_specs=[pl.BlockSpec((ATT_BLK, LANES), lambda p, i: (i, p)),
                   pl.BlockSpec((t, LANES), lambda p, i: (0, p)),
                   pl.BlockSpec((t, LANES), lambda p, i: (0, p))],
        out_shape=[jax.ShapeDtypeStruct((t, SB_WIDTH), F32)] * 3,
        compiler_params=pltpu.CompilerParams(dimension_semantics=("arbitrary", "arbitrary")),
    )(q, k, v, tot, do)


@jax.custom_vjp
def _sb_attention(q, k, v):
    return _sb_fwd(q, k, v)[0]


def _sb_attention_fwd(q, k, v):
    o, tot = _sb_fwd(q, k, v)
    return o, (q, k, v, tot)


def _sb_attention_bwd(res, do):
    return tuple(_sb_bwd(*res, do))


_sb_attention.defvjp(_sb_attention_fwd, _sb_attention_bwd)


def _mla_fwd(qn, qr, kn, kr, v):
    t = qn.shape[0]
    nq = t // ATT_BLK
    scale = MLA_QK ** -0.5

    def body(qn_ref, qr_ref, kn_ref, kr_ref, v_ref, o_ref, lse_ref):
        qi = pl.program_id(1)
        lanes = [slice(hh * LANES, (hh + 1) * LANES) for hh in range(2)]
        qnb = [qn_ref[:, sl].astype(BF16) for sl in lanes]
        qrb = [qr_ref[:, sl].astype(BF16) for sl in lanes]

        def block(kb, carry, diagonal):
            off = pl.multiple_of(kb * ATT_BLK, ATT_BLK)
            krb = kr_ref[pl.ds(off, ATT_BLK), :].astype(BF16)
            both = range(2)
            accs, ms, ls = [carry[0], carry[3]], [carry[1], carry[4]], [carry[2], carry[5]]
            ss = [(_nt(qnb[hh], kn_ref[pl.ds(off, ATT_BLK), lanes[hh]].astype(BF16)) + _nt(qrb[hh], krb)) * scale
                  for hh in both]
            if diagonal:
                causal = (lax.broadcasted_iota(jnp.int32, (ATT_BLK, ATT_BLK), 1)
                          <= lax.broadcasted_iota(jnp.int32, (ATT_BLK, ATT_BLK), 0))
                ss = [jnp.where(causal, s, -jnp.inf) for s in ss]
            m_new = [jnp.maximum(ms[hh], jnp.max(ss[hh], axis=-1, keepdims=True)) for hh in both]
            ps = [jnp.exp(ss[hh] - m_new[hh]) for hh in both]
            alphas = [jnp.exp(ms[hh] - m_new[hh]) for hh in both]
            pvs = [jnp.dot(ps[hh].astype(BF16), v_ref[pl.ds(off, ATT_BLK), lanes[hh]].astype(BF16),
                           preferred_element_type=F32) for hh in both]
            out = []
            for hh in both:
                out += [accs[hh] * alphas[hh] + pvs[hh], m_new[hh],
                        ls[hh] * alphas[hh] + jnp.sum(ps[hh], axis=-1, keepdims=True)]
            return tuple(out)

        init = (jnp.zeros((ATT_BLK, LANES), F32), jnp.full((ATT_BLK, 1), -jnp.inf, F32), jnp.zeros((ATT_BLK, 1), F32))
        carry = block(qi, init + init, True)
        carry = lax.fori_loop(0, qi, lambda kb, cr: block(kb, cr, False), carry)
        for hh in range(2):
            acc, m, l = carry[3 * hh:3 * hh + 3]
            o_ref[:, lanes[hh]] = acc / l
            lse_ref[:, lanes[hh]] = jnp.broadcast_to(m + jnp.log(l), (ATT_BLK, LANES))

    blk = pl.BlockSpec((ATT_BLK, 2 * LANES), lambda p, i: (i, p))
    full = pl.BlockSpec((t, 2 * LANES), lambda p, i: (0, p))
    return pl.pallas_call(
        body, name="mla_attn_fwd", grid=(MLA_HEADS // 2, nq),
        in_specs=[blk, blk, full, pl.BlockSpec((t, LANES), lambda p, i: (0, 0)), full],
        out_specs=[blk, blk],
        out_shape=[jax.ShapeDtypeStruct((t, MLA_HEADS * LANES), F32)] * 2,
        compiler_params=pltpu.CompilerParams(dimension_semantics=("arbitrary", "arbitrary")),
    )(qn, qr, kn, kr, v)


def _mla_bwd(qn, qr, kn, kr, v, o, lse, do):
    t = qn.shape[0]
    nq = t // ATT_BLK
    scale = MLA_QK ** -0.5

    def body(qn_ref, qr_ref, kn_ref, kr_ref, v_ref, o_ref, lse_ref, do_ref,
             dqn_ref, dqr_ref, dkn_ref, dkr_ref, dv_ref):
        pair = pl.program_id(0)
        qi = pl.program_id(1)

        @pl.when(qi == 0)
        def _():
            dkn_ref[...] = jnp.zeros_like(dkn_ref)
            dv_ref[...] = jnp.zeros_like(dv_ref)

        @pl.when((qi == 0) & (pair == 0))
        def _():
            dkr_ref[...] = jnp.zeros_like(dkr_ref)

        lanes = [slice(hh * LANES, (hh + 1) * LANES) for hh in range(2)]
        qnb = [qn_ref[:, sl].astype(BF16) for sl in lanes]
        qrb = [qr_ref[:, sl].astype(BF16) for sl in lanes]
        dob = [do_ref[:, sl].astype(BF16) for sl in lanes]
        delta = [jnp.sum(do_ref[:, sl] * o_ref[:, sl], axis=-1, keepdims=True) for sl in lanes]
        lse_v = [lse_ref[:, hh * LANES:hh * LANES + 1] for hh in range(2)]

        def block(kb, carry, diagonal):
            off = pl.multiple_of(kb * ATT_BLK, ATT_BLK)
            krb = kr_ref[pl.ds(off, ATT_BLK), :].astype(BF16)
            both = range(2)
            knb = [kn_ref[pl.ds(off, ATT_BLK), lanes[hh]].astype(BF16) for hh in both]
            vb = [v_ref[pl.ds(off, ATT_BLK), lanes[hh]].astype(BF16) for hh in both]
            ss = [_nt(qnb[hh], knb[hh]) + _nt(qrb[hh], krb) for hh in both]
            dps = [_nt(dob[hh], vb[hh]) for hh in both]
            ps = [jnp.exp(ss[hh] * scale - lse_v[hh]) for hh in both]
            if diagonal:
                causal = (lax.broadcasted_iota(jnp.int32, (ATT_BLK, ATT_BLK), 1)
                          <= lax.broadcasted_iota(jnp.int32, (ATT_BLK, ATT_BLK), 0))
                ps = [jnp.where(causal, p, 0.0) for p in ps]
            dss = [(ps[hh] * (dps[hh] - delta[hh]) * scale).astype(BF16) for hh in both]
            for hh in both:
                dv_ref[pl.ds(off, ATT_BLK), lanes[hh]] += _tn(ps[hh].astype(BF16), dob[hh])
            for hh in both:
                dkn_ref[pl.ds(off, ATT_BLK), lanes[hh]] += _tn(dss[hh], qnb[hh])
            dkr_ref[pl.ds(off, ATT_BLK), :] += _tn(dss[0], qrb[0]) + _tn(dss[1], qrb[1])
            out = []
            for hh in both:
                out += [carry[2 * hh] + jnp.dot(dss[hh], knb[hh], preferred_element_type=F32),
                        carry[2 * hh + 1] + jnp.dot(dss[hh], krb, preferred_element_type=F32)]
            return tuple(out)

        zero = jnp.zeros((ATT_BLK, LANES), F32)
        carry = lax.fori_loop(0, qi, lambda kb, cr: block(kb, cr, False), (zero, zero, zero, zero))
        carry = block(qi, carry, True)
        for hh in range(2):
            dqn_ref[:, lanes[hh]] = carry[2 * hh]
            dqr_ref[:, lanes[hh]] = carry[2 * hh + 1]

    blk = pl.BlockSpec((ATT_BLK, 2 * LANES), lambda p, i: (i, p))
    full = pl.BlockSpec((t, 2 * LANES), lambda p, i: (0, p))
    shared = pl.BlockSpec((t, LANES), lambda p, i: (0, 0))
    wide = jax.ShapeDtypeStruct((t, MLA_HEADS * LANES), F32)
    return pl.pallas_call(
        body, name="mla_attn_bwd", grid=(MLA_HEADS // 2, nq),
        in_specs=[blk, blk, full, shared, full, blk, blk, blk],
        out_specs=[blk, blk, full, shared, full],
        out_shape=[wide, wide, wide, jax.ShapeDtypeStruct((t, LANES), F32), wide],
        compiler_params=pltpu.CompilerParams(dimension_semantics=("arbitrary", "arbitrary")),
    )(qn, qr, kn, kr, v, o, lse, do)


@jax.custom_vjp
def _mla_attention(qn, qr, kn, kr, v):
    return _mla_fwd(qn, qr, kn, kr, v)[0]


def _mla_attention_fwd(qn, qr, kn, kr, v):
    o, lse = _mla_fwd(qn, qr, kn, kr, v)
    return o, (qn, qr, kn, kr, v, o, lse)


def _mla_attention_bwd(res, do):
    return tuple(_mla_bwd(*res, do))


_mla_attention.defvjp(_mla_attention_fwd, _mla_attention_bwd)


def _ffn_in(h, wg, wu):
    t, k = h.shape
    n_sh, _, cc = wg.shape

    def body(h_ref, wg_ref, wu_ref, g_ref, u_ref, a_ref):
        hb = h_ref[...].astype(BF16)
        g = jnp.dot(hb, wg_ref[...], preferred_element_type=F32)
        u = jnp.dot(hb, wu_ref[...], preferred_element_type=F32)
        g_ref[...] = g
        u_ref[...] = u
        a_ref[...] = _f_swiglu(g, u)[0].astype(BF16)

    w_spec = pl.BlockSpec((None, k, cc), lambda j, i: (j, 0, 0))
    o_spec = pl.BlockSpec((ROW_TILE, cc), lambda j, i: (i, j))
    wide = (t, n_sh * cc)
    return pl.pallas_call(
        body, name="ffn_in_fwd", grid=(n_sh, t // ROW_TILE),
        in_specs=[pl.BlockSpec((ROW_TILE, k), lambda j, i: (i, 0)), w_spec, w_spec],
        out_specs=[o_spec, o_spec, o_spec],
        out_shape=[jax.ShapeDtypeStruct(wide, F32), jax.ShapeDtypeStruct(wide, F32), jax.ShapeDtypeStruct(wide, BF16)],
        compiler_params=pltpu.CompilerParams(dimension_semantics=("arbitrary", "arbitrary"),
                                             vmem_limit_bytes=MM_VMEM_LIMIT),
    )(h, wg, wu)


def _ffn_mid_bwd(dy, wd, g, u):
    t, n = dy.shape
    n_sh, cc, _ = wd.shape

    def body(dy_ref, wd_ref, g_ref, u_ref, dg_ref, du_ref):
        d_act = _nt(dy_ref[...].astype(BF16), wd_ref[...])
        _, vjp = jax.vjp(_f_swiglu, g_ref[...], u_ref[...])
        dg, du = vjp((d_act,))
        dg_ref[...] = dg.astype(BF16)
        du_ref[...] = du.astype(BF16)

    blk = pl.BlockSpec((ROW_TILE, cc), lambda j, i: (i, j))
    wide = jax.ShapeDtypeStruct((t, n_sh * cc), BF16)
    return pl.pallas_call(
        body, name="ffn_mid_bwd", grid=(n_sh, t // ROW_TILE),
        in_specs=[pl.BlockSpec((ROW_TILE, n), lambda j, i: (i, 0)),
                  pl.BlockSpec((None, cc, n), lambda j, i: (j, 0, 0)), blk, blk],
        out_specs=[blk, blk], out_shape=[wide, wide],
        compiler_params=pltpu.CompilerParams(dimension_semantics=("arbitrary", "arbitrary"),
                                             vmem_limit_bytes=MM_VMEM_LIMIT),
    )(dy, wd, g, u)


def _ffn_dh(dg, du, wg, wu):
    t = dg.shape[0]
    n_sh, k, cc = wg.shape

    def body(dg_ref, du_ref, wg_ref, wu_ref, o_ref):
        acc = jnp.zeros((ROW_TILE, k), F32)
        for j in range(n_sh):
            cols = slice(j * cc, (j + 1) * cc)
            acc = acc + _nt(dg_ref[:, cols], wg_ref[j]) + _nt(du_ref[:, cols], wu_ref[j])
        o_ref[...] = acc

    blk = pl.BlockSpec((ROW_TILE, n_sh * cc), lambda i: (i, 0))
    w_spec = pl.BlockSpec((n_sh, k, cc), lambda i: (0, 0, 0))
    return pl.pallas_call(
        body, name="ffn_dh", grid=(t // ROW_TILE,),
        in_specs=[blk, blk, w_spec, w_spec],
        out_specs=pl.BlockSpec((ROW_TILE, k), lambda i: (i, 0)),
        out_shape=jax.ShapeDtypeStruct((t, k), F32),
        compiler_params=pltpu.CompilerParams(dimension_semantics=("arbitrary",), vmem_limit_bytes=MM_VMEM_LIMIT),
    )(dg, du, wg, wu)


def _ffn_dw_in(h, dy, n_sh, name):
    t, k = h.shape
    cc = dy.shape[1] // n_sh
    tk = 512

    def body(h_ref, dy_ref, o_ref):
        o_ref[...] = _tn(h_ref[...].astype(BF16), dy_ref[...]).astype(BF16)

    return pl.pallas_call(
        body, name=name, grid=(n_sh, k // tk),
        in_specs=[pl.BlockSpec((t, tk), lambda j, i: (0, i)), pl.BlockSpec((t, cc), lambda j, i: (0, j))],
        out_specs=pl.BlockSpec((None, tk, cc), lambda j, i: (j, i, 0)),
        out_shape=jax.ShapeDtypeStruct((n_sh, k, cc), BF16),
        compiler_params=pltpu.CompilerParams(dimension_semantics=("arbitrary", "arbitrary"),
                                             vmem_limit_bytes=MM_VMEM_LIMIT),
    )(h, dy)


@jax.custom_vjp
def _ffn_block(h, wg, wu, wd):
    act = _ffn_in(h, wg, wu)[2]
    return _mm(act, wd.reshape(-1, wd.shape[2]), "nn", "ffn_down_fwd", ROW_TILE, wd.shape[2])


def _ffn_block_fwd(h, wg, wu, wd):
    g, u, act = _ffn_in(h, wg, wu)
    y = _mm(act, wd.reshape(-1, wd.shape[2]), "nn", "ffn_down_fwd", ROW_TILE, wd.shape[2])
    return y, (h, wg, wu, wd, g, u, act)


def _ffn_block_bwd(res, dy):
    h, wg, wu, wd, g, u, act = res
    dg, du = _ffn_mid_bwd(dy, wd, g, u)
    dh = _ffn_dh(dg, du, wg, wu)
    n_sh = wg.shape[0]
    dwg = _ffn_dw_in(h, dg, n_sh, "ffn_gate_dw")
    dwu = _ffn_dw_in(h, du, n_sh, "ffn_up_dw")
    dwd = _mm(act, dy, "tn", "ffn_down_dw", 256, wd.shape[2], out_dtype=BF16).reshape(wd.shape)
    return dh, dwg, dwu, dwd


_ffn_block.defvjp(_ffn_block_fwd, _ffn_block_bwd)


def _split_cols(x, cuts):
    cuts = tuple(cuts)

    @jax.custom_vjp
    def op(x):
        return tuple(x[:, a:b] for a, b in zip((0,) + cuts, cuts + (x.shape[1],)))

    def fwd(x):
        return op(x), None

    def bwd(_, cts):
        return (jnp.concatenate(cts, axis=1),)

    op.defvjp(fwd, bwd)
    return op(x)


def _swap_halves(w):
    half = w.shape[-1] // 2
    return jnp.concatenate([w[..., half:], w[..., :half]], axis=-1)


def _pad_lanes(w):
    return jnp.concatenate([w, jnp.zeros(w.shape[:-1] + (LANES - w.shape[-1],), w.dtype)], axis=-1)


def _join_cols(shards):
    return shards.transpose(1, 0, 2).reshape(shards.shape[1], -1)


def _mod_parts(mod):
    return [mod[:, i * D_MODEL:(i + 1) * D_MODEL] for i in range(N_MOD)]


def _local_loss(x, mod, p, cos, sin, target):
    return _ffn_stage(x, _mixing_stage(x, mod, p, cos, sin), mod, p, target)


def _mixing_stage(x, mod, p, cos, sin):
    shift1, scale1 = _mod_parts(mod)[:2]

    w_in = _join_cols(p["w_in"])
    k_rope_w = w_in[:, 2176:2240]
    w_in_ext = jnp.concatenate([w_in[:, :2176], _pad_lanes(k_rope_w), _pad_lanes(_swap_halves(k_rope_w)),
                                jnp.zeros((D_MODEL, LANES), w_in.dtype)], axis=1)
    (h1,) = _make_rowwise("pre_attn", _f_pre_attn, 1, 3, [D_MODEL], [True])(x, p["norm_attn"], scale1, shift1)
    proj = _make_linear("in_proj", 512, 640)(h1, w_in_ext)
    q_sb, k_sb, v_sb, cq, ckv, kr, kr_sw, _ = _split_cols(proj, (512, 1024, 1536, 1920, 2176, 2304, 2432))

    o_sb = _sb_attention(q_sb, k_sb, v_sb)

    wq = _join_cols(p["w_q_up"]).reshape(MLA_Q_RANK, MLA_HEADS, MLA_QK)
    wq_n, wq_r = wq[:, :, :MLA_NOPE], wq[:, :, MLA_NOPE:]
    w_q_ext = jnp.concatenate([wq_n.reshape(MLA_Q_RANK, -1), _pad_lanes(wq_r).reshape(MLA_Q_RANK, -1),
                               _pad_lanes(_swap_halves(wq_r)).reshape(MLA_Q_RANK, -1)], axis=1)
    wkv = _join_cols(p["w_kv_up"]).reshape(MLA_KV_RANK, MLA_HEADS, MLA_NOPE + MLA_V)
    w_kv_ext = jnp.concatenate([wkv[:, :, :MLA_NOPE].reshape(MLA_KV_RANK, -1),
                                wkv[:, :, MLA_NOPE:].reshape(MLA_KV_RANK, -1)], axis=1)
    cqn, ckvn = _make_rowwise("mla_a", _f_mla_a, 2, 2, [MLA_Q_RANK, MLA_KV_RANK], [True, True])(
        cq, ckv, p["q_a_norm"], p["kv_a_norm"])
    qall = _make_linear("q_up", 384, 768)(cqn, w_q_ext)
    kvall = _make_linear("kv_up", 256, 1024)(ckvn, w_kv_ext)
    kn_all, v_mla = _split_cols(kvall, (512,))
    gq = p["q_norm"]
    gkr = p["k_rope_norm"]
    qn, qr, kn, krr = _make_rowwise("mla_b", _f_mla_b, 6, 6, [512, 512, 512, LANES],
                                    [True, True, True, True, False, False])(
        qall, kn_all, kr, kr_sw, cos, sin,
        gq[:, :MLA_NOPE], _pad_lanes(gq[:, MLA_NOPE:]), _pad_lanes(_swap_halves(gq[:, MLA_NOPE:])),
        p["k_nope_norm"], _pad_lanes(gkr), _pad_lanes(_swap_halves(gkr)))
    o_mla = _mla_attention(qn, qr, kn, krr, v_mla)

    (mixed,) = _make_rowwise("post_attn", _f_post_attn, 2, 2, [D_MODEL], [True, True])(
        o_sb, o_mla, p["out_norm_sb"], p["out_norm_mla"])
    return mixed


def _ffn_stage(x, mixed, mod, p, target):
    _, _, gate1, shift2, scale2, gate2 = _mod_parts(mod)
    attn = _make_linear("out_proj", 512, 512)(mixed, p["w_out"].reshape(D_MODEL, D_MODEL))

    x2, h2 = _make_rowwise("pre_ffn", _f_pre_ffn, 2, 4, [D_MODEL, D_MODEL], [True, True])(
        x, attn, gate1, p["norm_ffn"], scale2, shift2)
    ffn = _ffn_block(h2, p["w_gate"], p["w_up"], p["w_down"])
    (row_loss,) = _make_rowwise("loss", _f_loss, 3, 1, [1], [True, True, False])(x2, ffn, target, gate2)
    return 0.5 * jnp.sum(row_loss)


def _my_place():
    return lax.axis_index("x"), lax.axis_index("y"), lax.axis_index("c")


def _all_gather_small(block, name):
    m_per, n = block.shape

    def body(x_ref, out_ref, send_sems, recv_sems, local_sem):
        x, y, c = _my_place()
        me, sibling = (x, y, c), (x, y, 1 - c)
        chips = [(1 - x, y), (x, 1 - y), (1 - x, 1 - y)]

        def rows(px, py, pc):
            return out_ref.at[pl.ds((4 * px + 2 * py + pc) * m_per, m_per), :]

        def copy(k, blk, to, src=None):
            return pltpu.make_async_remote_copy(
                src_ref=rows(*blk) if src is None else src, dst_ref=rows(*blk),
                send_sem=send_sems.at[k], recv_sem=recv_sems.at[k], device_id=to, device_id_type=MESH)

        mine = pltpu.make_async_copy(x_ref, rows(*me), local_sem)
        mine.start()
        first = [copy(0, me, sibling, src=x_ref)]
        first += [copy(1 + j, me, (*chip, c), src=x_ref) for j, chip in enumerate(chips)]
        for cp in first:
            cp.start()
        passed = [copy(4 + j, (*chip, c), sibling) for j, chip in enumerate(chips)]
        for j, chip in enumerate(chips):
            copy(1 + j, (*chip, c), me).wait_recv()
            passed[j].start()
        copy(0, sibling, me).wait_recv()
        for j, chip in enumerate(chips):
            copy(4 + j, (*chip, 1 - c), me).wait_recv()
        for cp in first + passed:
            cp.wait_send()
        mine.wait()

    return pl.pallas_call(
        body, name=name,
        out_shape=jax.ShapeDtypeStruct((N_DEV * m_per, n), block.dtype),
        in_specs=[pl.BlockSpec(memory_space=pltpu.VMEM)],
        out_specs=pl.BlockSpec(memory_space=pltpu.VMEM),
        scratch_shapes=[pltpu.SemaphoreType.DMA((7,)), pltpu.SemaphoreType.DMA((7,)), pltpu.SemaphoreType.DMA],
    )(block)


EARLY = ("w_in", "w_q_up", "w_kv_up")
LATE = ("w_out", "w_gate", "w_up", "w_down")
BIG = EARLY + LATE
HALF_AXIS = {"w_in": 0, "w_q_up": 0, "w_kv_up": 0, "w_out": 0, "w_gate": 0, "w_up": 0, "w_down": 1}


def _half(ref, h, axis, lead=()):
    trail = ref.shape[len(lead):]
    idx = list(lead) + [slice(None)] * len(trail)
    at = len(trail) - 2 + axis
    n2 = trail[at] // 2
    idx[len(lead) + at] = pl.ds(h * n2, n2)
    return ref.at[tuple(idx)]


def _half_shape(shape, axis):
    shape = list(shape)
    shape[len(shape) - 2 + axis] //= 2
    return tuple(shape)


def _remote(src, dst, send_sems, recv_sems, k, to):
    return pltpu.make_async_remote_copy(src_ref=src, dst_ref=dst, send_sem=send_sems.at[k],
                                        recv_sem=recv_sems.at[k], device_id=to, device_id_type=MESH)


def _gather_weights(names, shards, after):
    n_w = len(shards)
    axes = [HALF_AXIS[n] for n in names]

    def body(*refs):
        w_refs, out_refs, token = refs[:n_w], refs[n_w + 1:2 * n_w + 1], refs[2 * n_w + 1]
        send_sems, recv_sems, local_sems = refs[2 * n_w + 2:]
        token[...] = jnp.zeros_like(token)
        x, y, c = _my_place()
        sibling = (x, y, 1 - c)
        chips = [(1 - x, y), (x, 1 - y), (1 - x, 1 - y)]
        me = 2 * x + y
        mine =[pltpu.make_async_copy(w, o.at[me], local_sems.at[i]) for i, (w, o) in enumerate(zip(w_refs, out_refs))]
        for cp in mine:
            cp.start()
        first = [_remote(_half(w_refs[i], c, axes[i]), _half(out_refs[i], c, axes[i], (me,)),
                         send_sems, recv_sems, 6 * i + j, (*chip, c))
                 for i in range(n_w) for j, chip in enumerate(chips)]
        for cp in first:
            cp.start()
        passed = []
        for j, (cx, cy) in enumerate(chips):
            for i in range(n_w):
                blk = _half(out_refs[i], c, axes[i], (2 * cx + cy,))
                _remote(blk, blk, send_sems, recv_sems, 6 * i + j, (cx, cy, c)).wait_recv()
                cp = _remote(blk, blk, send_sems, recv_sems, 6 * i + 3 + j, sibling)
                cp.start()
                passed.append(cp)
        for j, (cx, cy) in enumerate(chips):
            for i in range(n_w):
                blk = _half(out_refs[i], 1 - c, axes[i], (2 * cx + cy,))
                _remote(blk, blk, send_sems, recv_sems, 6 * i + 3 + j, sibling).wait_recv()
        for cp in first + passed:
            cp.wait_send()
        for cp in mine:
            cp.wait()

    outs = pl.pallas_call(
        body, name="gather_weights",
        out_shape=[jax.ShapeDtypeStruct((N_CHIPS,) + s.shape, s.dtype) for s in shards]
        + [jax.ShapeDtypeStruct((8, LANES), F32)],
        in_specs=[ANY] * (n_w + 1), out_specs=[ANY] * n_w + [pl.BlockSpec(memory_space=pltpu.VMEM)],
        scratch_shapes=[pltpu.SemaphoreType.DMA((6 * n_w,)), pltpu.SemaphoreType.DMA((6 * n_w,)),
                        pltpu.SemaphoreType.DMA((n_w,))],
    )(*shards, after)
    return outs[:n_w], outs[n_w]


def _pair_exchange(names, grads, call_name):
    n_w = len(grads)
    axes = [HALF_AXIS[n] for n in names]

    def body(*refs):
        g_refs, t_refs = refs[:n_w], refs[n_w:2 * n_w]
        send_sems, recv_sems = refs[2 * n_w:]
        x, y, c = _my_place()
        sends = [_remote(_half(g_refs[i], 1 - c, axes[i]), t_refs[i], send_sems, recv_sems, i, (x, y, 1 - c))
                 for i in range(n_w)]
        for cp in sends:
            cp.start()
        for cp in sends:
            cp.wait_recv()
        for cp in sends:
            cp.wait_send()

    return pl.pallas_call(
        body, name=call_name,
        out_shape=[jax.ShapeDtypeStruct(_half_shape(g.shape, a), g.dtype) for g, a in zip(grads, axes)],
        in_specs=[ANY] * n_w, out_specs=[ANY] * n_w,
        scratch_shapes=[pltpu.SemaphoreType.DMA((n_w,)), pltpu.SemaphoreType.DMA((n_w,))],
    )(*grads)


def _chip_scatter(pair_sums):
    n_w = len(pair_sums)

    def body(*refs):
        s_refs, p_refs = refs[:n_w], refs[n_w:2 * n_w]
        send_sems, recv_sems = refs[2 * n_w:]
        x, y, c = _my_place()
        chips = [(1 - x, y), (x, 1 - y), (1 - x, 1 - y)]
        sends = [_remote(s_refs[i].at[2 * cx + cy], p_refs[i].at[j], send_sems, recv_sems, 3 * i + j, (cx, cy, c))
                 for i in range(n_w) for j, (cx, cy) in enumerate(chips)]
        for cp in sends:
            cp.start()
        for cp in sends:
            cp.wait_recv()
        for cp in sends:
            cp.wait_send()

    return pl.pallas_call(
        body, name="grad_chip_scatter",
        out_shape=[jax.ShapeDtypeStruct((N_CHIPS - 1,) + s.shape[1:], s.dtype) for s in pair_sums],
        in_specs=[ANY] * n_w, out_specs=[ANY] * n_w,
        scratch_shapes=[pltpu.SemaphoreType.DMA((3 * n_w,)), pltpu.SemaphoreType.DMA((3 * n_w,))],
    )(*pair_sums)


def _sibling_join(halves, name, after):
    n_w = len(halves)

    def body(*refs):
        s_refs, j_refs = refs[:n_w], refs[n_w + 1:2 * n_w + 1]
        send_sems, recv_sems = refs[2 * n_w + 1:]
        x, y, c = _my_place()
        sends = [_remote(s_refs[i], j_refs[i], send_sems, recv_sems, i, (x, y, 1 - c)) for i in range(n_w)]
        for cp in sends:
            cp.start()
        for cp in sends:
            cp.wait_recv()
        for cp in sends:
            cp.wait_send()

    return pl.pallas_call(
        body, name=name,
        out_shape=[jax.ShapeDtypeStruct(s.shape, s.dtype) for s in halves],
        in_specs=[ANY] * (n_w + 1), out_specs=[ANY] * n_w,
        scratch_shapes=[pltpu.SemaphoreType.DMA((n_w,)), pltpu.SemaphoreType.DMA((n_w,))],
    )(*halves, after)


HBM_SPEC = pl.BlockSpec(memory_space=pltpu.HBM)
SEM_SPEC = pl.BlockSpec(memory_space=pltpu.SEMAPHORE)
DATAFLOW = pltpu.SideEffectType.DATAFLOW_SIDE_EFFECTING


def _in_hbm(a):
    return pltpu.with_memory_space_constraint(a, pltpu.HBM)


def _exchange_start(name, srcs, lands, plan, n_copies, after):
    n = len(srcs)

    def body(*refs):
        src_refs, land_refs = refs[:n], refs[n:2 * n]
        send_sems, recv_sems = refs[2 * n + 1], refs[2 * n + 2]
        token = refs[-1]
        for k, (src, dst, to) in enumerate(plan(src_refs, land_refs)):
            _remote(src, dst, send_sems, recv_sems, k, to).start()
        token[...] = jnp.zeros_like(token)

    outs = pl.pallas_call(
        body, name=name,
        out_shape=(pltpu.SemaphoreType.DMA((n_copies,)), pltpu.SemaphoreType.DMA((n_copies,)),
                   *[pltpu.HBM(a.shape, a.dtype) for a in srcs], *[pltpu.HBM(a.shape, a.dtype) for a in lands],
                   jax.ShapeDtypeStruct((8, LANES), F32)),
        in_specs=[HBM_SPEC] * (2 * n) + [ANY],
        out_specs=(SEM_SPEC, SEM_SPEC, *[HBM_SPEC] * (2 * n), pl.BlockSpec(memory_space=pltpu.VMEM)),
        input_output_aliases={i: 2 + i for i in range(2 * n)},
        compiler_params=pltpu.CompilerParams(has_side_effects=DATAFLOW),
    )(*[_in_hbm(a) for a in srcs], *[_in_hbm(a) for a in lands], after)
    return outs[0], outs[1], outs[2:2 + n], outs[2 + n:2 + 2 * n], outs[-1]


def _exchange_wait(name, started, plan, after):
    send_sems, recv_sems, srcs, lands, _ = started
    n = len(srcs)

    def body(*refs):
        src_refs, land_refs = refs[:n], refs[n:2 * n]
        s_sems, r_sems = refs[2 * n], refs[2 * n + 1]
        for k, (src, dst, to) in enumerate(plan(src_refs, land_refs)):
            cp = _remote(src, dst, s_sems, r_sems, k, to)
            cp.wait_send()
            cp.wait_recv()

    outs = pl.pallas_call(
        body, name=name,
        out_shape=tuple(pltpu.HBM(a.shape, a.dtype) for a in list(srcs) + list(lands)),
        in_specs=[HBM_SPEC] * (2 * n) + [SEM_SPEC, SEM_SPEC, ANY],
        out_specs=tuple([HBM_SPEC] * (2 * n)),
        input_output_aliases={i: i for i in range(2 * n)},
        compiler_params=pltpu.CompilerParams(has_side_effects=DATAFLOW),
    )(*srcs, *lands, send_sems, recv_sems, after)
    return outs[:n], outs[n:]


def _late_gather_plan(src_refs, land_refs):
    x, y, c = _my_place()
    chips = [(1 - x, y), (x, 1 - y), (1 - x, 1 - y)]
    return [(src, land.at[2 * x + y], (cx, cy, c)) for src, land in zip(src_refs, land_refs) for cx, cy in chips]


def _late_scatter_plan(src_refs, land_refs):
    x, y, c = _my_place()
    chips = [(1 - x, y), (x, 1 - y), (1 - x, 1 - y)]
    return [(src.at[2 * cx + cy], land.at[j], (cx, cy, c))
            for src, land in zip(src_refs, land_refs) for j, (cx, cy) in enumerate(chips)]


def _row_tile(rows, mult=16):
    return max(d for d in range(mult, ROW_TILE + 1, mult) if rows % d == 0)


def _pair_sum(place, g, theirs, axis, name):
    nj, rr, cc = theirs.shape
    tr = _row_tile(rr)
    nb = rr // tr
    if axis == 0:
        g_map = lambda j, i, pr: (j, pr[0] * nb + i, 0)
    else:
        g_map = lambda j, i, pr: (j, i, pr[0])

    def body(pr, g_ref, t_ref, o_ref):
        o_ref[...] = (g_ref[...].astype(F32) + t_ref[...].astype(F32)).astype(BF16)

    spec = pl.BlockSpec((None, tr, cc), lambda j, i, pr: (j, i, 0))
    return pl.pallas_call(
        body, name=name,
        grid_spec=pltpu.PrefetchScalarGridSpec(
            num_scalar_prefetch=1, grid=(nj, nb),
            in_specs=[pl.BlockSpec((None, tr, cc), g_map), spec], out_specs=spec),
        out_shape=jax.ShapeDtypeStruct(theirs.shape, BF16))(place, g, theirs)


def _chip_sum(place, pair_sums, parts, name):
    _, rr, cc = parts.shape
    tr = _row_tile(rr)

    def body(pr, h_ref, p_ref, o_ref):
        acc = p_ref[0].astype(F32)
        for j in range(1, N_CHIPS - 1):
            acc = acc + p_ref[j].astype(F32)
        o_ref[...] = (acc + h_ref[...].astype(F32)).astype(BF16)

    return pl.pallas_call(
        body, name=name,
        grid_spec=pltpu.PrefetchScalarGridSpec(
            num_scalar_prefetch=1, grid=(rr // tr,),
            in_specs=[pl.BlockSpec((None, tr, cc), lambda i, pr: (pr[1], i, 0)),
                      pl.BlockSpec((N_CHIPS - 1, tr, cc), lambda i, pr: (0, i, 0))],
            out_specs=pl.BlockSpec((tr, cc), lambda i, pr: (i, 0))),
        out_shape=jax.ShapeDtypeStruct((rr, cc), BF16))(place, pair_sums, parts)


def _silu(v):
    return v / (1.0 + jnp.exp(-v))


def _ada_fwd(c_all, w_shard, b_shard):
    def body(c_ref, w_ref, b_ref, o_ref):
        o_ref[...] = jnp.dot(_silu(c_ref[...]), w_ref[...], precision=lax.Precision.HIGHEST,
                             preferred_element_type=F32) + b_ref[...]

    return pl.pallas_call(body, name="ada_fwd", out_shape=jax.ShapeDtypeStruct((c_all.shape[0], w_shard.shape[1]), F32),
                          compiler_params=pltpu.CompilerParams(vmem_limit_bytes=MM_VMEM_LIMIT))(c_all, w_shard, b_shard)


def _ada_bwd(c_all, dmod_cols):
    def body(c_ref, d_ref, o_ref):
        o_ref[...] = lax.dot_general(_silu(c_ref[...]), d_ref[...], (((0,), (0,)), ((), ())),
                                     precision=lax.Precision.HIGHEST, preferred_element_type=F32)

    return pl.pallas_call(body, name="ada_bwd", out_shape=jax.ShapeDtypeStruct((c_all.shape[1], dmod_cols.shape[1]), F32),
                          compiler_params=pltpu.CompilerParams(vmem_limit_bytes=MM_VMEM_LIMIT))(c_all, dmod_cols)


def _adamw_math(w, g, m, v):
    m = ADAM_B1 * m + (1.0 - ADAM_B1) * g
    v = ADAM_B2 * v + (1.0 - ADAM_B2) * (g * g)
    m_hat = m / (1.0 - ADAM_B1 ** ADAM_STEP)
    v_hat = v / (1.0 - ADAM_B2 ** ADAM_STEP)
    delta = -ADAM_LR * (m_hat / (jnp.sqrt(v_hat) + ADAM_EPS) + ADAM_WD * w)
    return delta, m, v


def _adamw(w, g, m, v, name):
    r, ccols = w.shape
    tr = max(d for d in range(8, ROW_TILE + 1, 8) if r % d == 0)
    spec = pl.BlockSpec((tr, ccols), lambda i: (i, 0))

    def body(w_ref, g_ref, m_ref, v_ref, d_ref, nm_ref, nv_ref):
        d_ref[...], nm_ref[...], nv_ref[...] = _adamw_math(w_ref[...], g_ref[...], m_ref[...], v_ref[...])

    return pl.pallas_call(body, name=name, grid=(r // tr,), in_specs=[spec] * 4, out_specs=[spec] * 3,
                          out_shape=[jax.ShapeDtypeStruct(w.shape, F32)] * 3,
                          compiler_params=pltpu.CompilerParams(vmem_limit_bytes=MM_VMEM_LIMIT))(w, g, m, v)


def _adamw_small(w, g_all, m, v):
    def body(w_ref, g_ref, m_ref, v_ref, gs_ref, d_ref, nm_ref, nv_ref):
        g = g_ref[0]
        for d in range(1, N_DEV):
            g = g + g_ref[d]
        gs_ref[...] = g
        d_ref[...], nm_ref[...], nv_ref[...] = _adamw_math(w_ref[...], g, m_ref[...], v_ref[...])

    return pl.pallas_call(body, name="adamw_small", out_shape=[jax.ShapeDtypeStruct(w.shape, F32)] * 4)(w, g_all, m, v)


def _adamw_halves(place, w, own, sib, m, v, axis, name):
    r, cc = w.shape
    if axis == 0:
        rows, gc = own.shape[0], own.shape[1]
        tr = _row_tile(rows)
        nb = rows // tr
        w_spec = pl.BlockSpec((tr, cc), lambda h, i, pr: (h * nb + i, 0))
        g_spec = pl.BlockSpec((tr, gc), lambda h, i, pr: (i, 0))
    else:
        tr = _row_tile(r)
        nb = r // tr
        gc = own.shape[1]
        w_spec = pl.BlockSpec((tr, gc), lambda h, i, pr: (i, h))
        g_spec = pl.BlockSpec((tr, gc), lambda h, i, pr: (i, 0))
    wc = w_spec.block_shape[1]

    def body(pr, w_ref, o_ref, s_ref, m_ref, v_ref, g_ref, d_ref, nm_ref, nv_ref):
        g = jnp.where(pl.program_id(0) == pr[0], o_ref[...], s_ref[...]).astype(F32)[:, :wc]
        g_ref[...] = g
        d_ref[...], nm_ref[...], nv_ref[...] = _adamw_math(w_ref[...], g, m_ref[...], v_ref[...])

    return pl.pallas_call(
        body, name=name,
        grid_spec=pltpu.PrefetchScalarGridSpec(
            num_scalar_prefetch=1, grid=(2, nb),
            in_specs=[w_spec, g_spec, g_spec, w_spec, w_spec], out_specs=[w_spec] * 4),
        out_shape=[jax.ShapeDtypeStruct(w.shape, F32)] * 4,
        compiler_params=pltpu.CompilerParams(vmem_limit_bytes=MM_VMEM_LIMIT))(place, w, own, sib, m, v)


SMALL = ("b_ada", "norm_attn", "norm_ffn", "q_a_norm", "kv_a_norm", "q_norm", "k_nope_norm", "k_rope_norm",
         "out_norm_sb", "out_norm_mla")
WEIGHTS = ("w_ada", "b_ada", "norm_attn", "norm_ffn", "w_in", "q_a_norm", "w_q_up", "kv_a_norm", "w_kv_up",
           "q_norm", "k_nope_norm", "k_rope_norm", "out_norm_sb", "out_norm_mla", "w_out", "w_gate", "w_up",
           "w_down")


def kernel(x, c, positions, w_ada, b_ada, norm_attn, norm_ffn, w_in, q_a_norm, w_q_up, kv_a_norm, w_kv_up, q_norm, k_nope_norm, k_rope_norm, out_norm_sb, out_norm_mla, w_out, w_gate, w_up, w_down, loss_target, m_w_ada, m_b_ada, m_norm_attn, m_norm_ffn, m_w_in, m_q_a_norm, m_w_q_up, m_kv_a_norm, m_w_kv_up, m_q_norm, m_k_nope_norm, m_k_rope_norm, m_out_norm_sb, m_out_norm_mla, m_w_out, m_w_gate, m_w_up, m_w_down, v_w_ada, v_b_ada, v_norm_attn, v_norm_ffn, v_w_in, v_q_a_norm, v_w_q_up, v_kv_a_norm, v_w_kv_up, v_q_norm, v_k_nope_norm, v_k_rope_norm, v_out_norm_sb, v_out_norm_mla, v_w_out, v_w_gate, v_w_up, v_w_down):
    local = dict(locals())
    w = {n: local[n][0] for n in WEIGHTS}
    m = {n: local["m_" + n][0] for n in WEIGHTS}
    v = {n: local["v_" + n][0] for n in WEIGHTS}
    small = {n: w[n].reshape(1, -1) for n in SMALL}
    ix, iy, ic = _my_place()
    chip = 2 * ix + iy
    dev = 2 * chip + ic
    xs, target = x[0], loss_target[0]
    seq = xs.shape[0]

    c_all = _all_gather_small(c.reshape(8, LANES), "gather_c").reshape(N_DEV, D_MODEL)
    ada_cols = w["w_ada"].shape[1]
    b_cols = lax.dynamic_slice_in_dim(small["b_ada"], chip * ada_cols, ada_cols, axis=1)
    mod_cols = _ada_fwd(c_all, w["w_ada"], b_cols)
    mod_all = _all_gather_small(mod_cols, "gather_mod").reshape(N_CHIPS, 2, N_DEV, ada_cols)
    mod = lax.dynamic_index_in_dim(mod_all[:, 0], dev, axis=1, keepdims=False).reshape(1, N_MOD * D_MODEL)

    ff_pad = FF_SHARD_PAD - FF_SHARD
    pads = {"w_gate": ((0, 0), (0, ff_pad)), "w_up": ((0, 0), (0, ff_pad)), "w_down": ((0, ff_pad), (0, 0))}
    shards = {n: jnp.pad(w[n].astype(BF16), pads[n]) if n in pads else w[n].astype(BF16) for n in BIG}
    early, early_done = _gather_weights(EARLY, [shards[n] for n in EARLY], mod)
    gathered = dict(zip(EARLY, early))
    lands = [lax.dynamic_update_index_in_dim(lax.empty((N_CHIPS,) + shards[n].shape, BF16), shards[n], chip, 0)
             for n in LATE]
    late_gather = _exchange_start("gather_late_start", [shards[n] for n in LATE], lands, _late_gather_plan,
                                  3 * len(LATE), early_done)

    half = MLA_ROPE // 2
    freqs = 1.0 / (ROPE_THETA ** (np.arange(half, dtype=np.float32) / half))
    zeros = np.zeros(LANES - MLA_ROPE, np.float32)
    freqs_row = jnp.asarray(np.concatenate([freqs, freqs, zeros]).astype(np.float32)[None])
    sign_row = jnp.asarray(np.concatenate([-np.ones(half), np.ones(half), zeros]).astype(np.float32)[None])
    cos, sin = _rope_tables(positions.reshape(seq, 1), freqs_row, sign_row)

    place = jnp.stack([ic, chip]).astype(jnp.int32)
    small_params = {n: small[n] for n in SMALL if n != "b_ada"}
    mod = mod + late_gather[4][0, 0]

    def pair_sums_of(names, grads, call_name):
        theirs = _pair_exchange(names, grads, call_name)
        return [_pair_sum(place, gr, th, HALF_AXIS[n], "grad_pair_sum_" + n) for n, gr, th in zip(names, grads, theirs)]

    p1 = {**{n: gathered[n] for n in EARLY}, **small_params}
    mixed, mixing_vjp = jax.vjp(lambda x_, mod_, p_: _mixing_stage(x_, mod_, p_, cos, sin), xs, mod, p1)
    _, landed = _exchange_wait("gather_late_wait", late_gather, _late_gather_plan, mixed)
    p2 = {**dict(zip(LATE, landed)), **small_params}
    loss_part, ffn_vjp = jax.vjp(lambda x_, mixed_, mod_, p_: _ffn_stage(x_, mixed_, mod_, p_, target), xs, mixed, mod, p2)
    gx2, gmixed, gmod2, gp2 = ffn_vjp(jnp.ones((), F32))
    late_sums = pair_sums_of(LATE, [gp2[n] for n in LATE], "grad_pair_exchange_late")
    late_scatter = _exchange_start(
        "grad_scatter_late_start", late_sums,
        [lax.empty((N_CHIPS - 1,) + s.shape[1:], BF16) for s in late_sums], _late_scatter_plan, 3 * len(LATE), gx2)
    gx1, gmod1, gp1 = mixing_vjp(gmixed + late_scatter[4][0, 0])
    gx = gx1 + gx2
    gmod = gmod1 + gmod2
    gp = {n: gp1[n] + gp2[n] for n in small_params}
    loss = lax.psum(loss_part, ("x", "y", "c"))

    small_names = [n for n in SMALL if n != "b_ada"]
    small_vec = jnp.concatenate([gmod] + [gp[n] for n in small_names], axis=1)
    n_small = small_vec.shape[1]
    small_all = _all_gather_small(small_vec.reshape(8, n_small // 8), "gather_small").reshape(N_DEV, 8, n_small // 8)

    early_sums = pair_sums_of(EARLY, [gp1[n] for n in EARLY], "grad_pair_exchange_early")
    early_scatter = _exchange_start(
        "grad_scatter_early_start", early_sums,
        [lax.empty((N_CHIPS - 1,) + s.shape[1:], BF16) for s in early_sums], _late_scatter_plan, 3 * len(EARLY),
        small_all)
    late_sums, late_parts = _exchange_wait("grad_scatter_late_wait", late_scatter, _late_scatter_plan, gx)
    own_late = [_chip_sum(place, ps, pt, "grad_chip_sum_" + n) for n, ps, pt in zip(LATE, late_sums, late_parts)]
    sib_late = _sibling_join(own_late, "grad_sibling_join_late", early_scatter[4])
    g, delta, new_m, new_v = {}, {}, {}, {}
    for n, o, s in zip(LATE, own_late, sib_late):
        g[n], delta[n], new_m[n], new_v[n] = _adamw_halves(place, w[n], o, s, m[n], v[n], HALF_AXIS[n], "adamw_" + n)

    def pack_small(d):
        return jnp.concatenate([d[n].reshape(1, -1) for n in SMALL], axis=1).reshape(8, n_small // 8)

    gs, ds, ms, vs = _adamw_small(pack_small(w), small_all, pack_small(m), pack_small(v))
    sizes = [w[n].size for n in SMALL]
    offs = np.concatenate([[0], np.cumsum(sizes)])

    def unpack_small(a):
        flat = a.reshape(-1)
        return {n: flat[offs[i]:offs[i + 1]].reshape(w[n].shape) for i, n in enumerate(SMALL)}

    for d, packed in zip((g, delta, new_m, new_v), (gs, ds, ms, vs)):
        d.update(unpack_small(packed))

    dmod_all = small_all.reshape(N_DEV, n_small)[:, :N_MOD * D_MODEL]
    g["w_ada"] = _ada_bwd(c_all, lax.dynamic_slice_in_dim(dmod_all, chip * ada_cols, ada_cols, axis=1))
    delta["w_ada"], new_m["w_ada"], new_v["w_ada"] = _adamw(w["w_ada"], g["w_ada"], m["w_ada"], v["w_ada"], "adamw_w_ada")

    early_sums, early_parts = _exchange_wait("grad_scatter_early_wait", early_scatter, _late_scatter_plan,
                                             delta["w_ada"])
    own_early = [_chip_sum(place, ps, pt, "grad_chip_sum_" + n) for n, ps, pt in zip(EARLY, early_sums, early_parts)]
    sib_early = _sibling_join(own_early, "grad_sibling_join_early", delta["w_ada"])
    for n, o, s in zip(EARLY, own_early, sib_early):
        g[n], delta[n], new_m[n], new_v[n] = _adamw_halves(place, w[n], o, s, m[n], v[n], HALF_AXIS[n], "adamw_" + n)

    def outs(d):
        return [d[n][None] for n in WEIGHTS]

    return (loss, gx[None], *outs(g), *outs(delta), *outs(new_m), *outs(new_v))
```

```python
import functools
import math

import numpy as np
import jax
import jax.numpy as jnp
from jax import lax
from jax.experimental import pallas as pl
from jax.experimental.pallas import tpu as pltpu

F32 = jnp.float32
BF16 = jnp.bfloat16
MESH = pl.DeviceIdType.MESH
ANY = pl.BlockSpec(memory_space=pl.ANY)

D_MODEL = 1024
SB_HEADS = 8
SB_HEAD_DIM = 64
SB_WIDTH = 512
MLA_HEADS = 4
MLA_NOPE = 128
MLA_ROPE = 64
MLA_QK = 192
MLA_V = 128
MLA_Q_RANK = 384
MLA_KV_RANK = 256
D_FF = 2816
N_MOD = 6
ROPE_THETA = 10000.0
EPS = 1e-6
LANES = 128

ADAM_LR = 0.001
ADAM_B1 = 0.9
ADAM_B2 = 0.999
ADAM_EPS = 1e-08
ADAM_WD = 0.01
ADAM_STEP = 10

N_CHIPS = 4
N_DEV = 8
ROW_TILE = 256
ATT_BLK = 256
MM_VMEM_LIMIT = 48 * 1024 * 1024
FF_SHARD = D_FF // N_CHIPS
FF_SHARD_PAD = 768


def _mm(a, b, mode, name, tm, tn, out_dtype=F32):
    if mode == "nn":
        (m, k), n = a.shape, b.shape[1]
        a_spec = pl.BlockSpec((tm, k), lambda j, i: (i, 0))
        b_spec = pl.BlockSpec((k, tn), lambda j, i: (0, j))
        dims = (((1,), (0,)), ((), ()))
    elif mode == "nt":
        (m, k), n = a.shape, b.shape[0]
        a_spec = pl.BlockSpec((tm, k), lambda j, i: (i, 0))
        b_spec = pl.BlockSpec((tn, k), lambda j, i: (j, 0))
        dims = (((1,), (1,)), ((), ()))
    else:
        (k, m), n = a.shape, b.shape[1]
        a_spec = pl.BlockSpec((k, tm), lambda j, i: (0, i))
        b_spec = pl.BlockSpec((k, tn), lambda j, i: (0, j))
        dims = (((0,), (0,)), ((), ()))
    assert m % tm == 0 and n % tn == 0, (name, m, n, tm, tn)

    def body(a_ref, b_ref, o_ref):
        o_ref[...] = lax.dot_general(a_ref[...].astype(BF16), b_ref[...].astype(BF16), dims,
                                     preferred_element_type=F32).astype(out_dtype)

    return pl.pallas_call(
        body, name=name, grid=(n // tn, m // tm),
        in_specs=[a_spec, b_spec],
        out_specs=pl.BlockSpec((tm, tn), lambda j, i: (i, j)),
        out_shape=jax.ShapeDtypeStruct((m, n), out_dtype),
        compiler_params=pltpu.CompilerParams(dimension_semantics=("arbitrary", "arbitrary"),
                                             vmem_limit_bytes=MM_VMEM_LIMIT),
    )(a, b)


def _make_linear(name, tk_w, tn_w):
    @jax.custom_vjp
    def op(a, w):
        return _mm(a, w, "nn", name + "_fwd", ROW_TILE, w.shape[1])

    def fwd(a, w):
        return op(a, w), (a, w)

    def bwd(res, dy):
        a, w = res
        da = _mm(dy, w, "nt", name + "_dx", ROW_TILE, w.shape[0])
        dw = _mm(a, dy, "tn", name + "_dw", tk_w, tn_w, out_dtype=BF16)
        return da, dw

    op.defvjp(fwd, bwd)
    return op


def _make_linear_sharded(name, tk_w):
    def call_fwd(a, w):
        t, k = a.shape
        n_sh, _, cc = w.shape

        def body(a_ref, w_ref, o_ref):
            o_ref[...] = jnp.dot(a_ref[...].astype(BF16), w_ref[...], preferred_element_type=F32)

        return pl.pallas_call(
            body, name=name + "_fwd", grid=(n_sh, t // ROW_TILE),
            in_specs=[pl.BlockSpec((ROW_TILE, k), lambda j, i: (i, 0)),
                      pl.BlockSpec((None, k, cc), lambda j, i: (j, 0, 0))],
            out_specs=pl.BlockSpec((ROW_TILE, cc), lambda j, i: (i, j)),
            out_shape=jax.ShapeDtypeStruct((t, n_sh * cc), F32),
            compiler_params=pltpu.CompilerParams(dimension_semantics=("arbitrary", "arbitrary"),
                                                 vmem_limit_bytes=MM_VMEM_LIMIT),
        )(a, w)

    def call_dx(dy, w):
        t = dy.shape[0]
        n_sh, k, cc = w.shape

        def body(dy_ref, w_ref, o_ref):
            acc = jnp.zeros((ROW_TILE, k), F32)
            for j in range(n_sh):
                acc = acc + _nt(dy_ref[:, j * cc:(j + 1) * cc].astype(BF16), w_ref[j])
            o_ref[...] = acc

        return pl.pallas_call(
            body, name=name + "_dx", grid=(t // ROW_TILE,),
            in_specs=[pl.BlockSpec((ROW_TILE, n_sh * cc), lambda i: (i, 0)),
                      pl.BlockSpec((n_sh, k, cc), lambda i: (0, 0, 0))],
            out_specs=pl.BlockSpec((ROW_TILE, k), lambda i: (i, 0)),
            out_shape=jax.ShapeDtypeStruct((t, k), F32),
            compiler_params=pltpu.CompilerParams(dimension_semantics=("arbitrary",),
                                                 vmem_limit_bytes=MM_VMEM_LIMIT),
        )(dy, w)

    def call_dw(a, dy, w):
        t, k = a.shape
        n_sh, _, cc = w.shape

        def body(a_ref, dy_ref, o_ref):
            o_ref[...] = _tn(a_ref[...].astype(BF16), dy_ref[...].astype(BF16)).astype(BF16)

        return pl.pallas_call(
            body, name=name + "_dw", grid=(n_sh, k // tk_w),
            in_specs=[pl.BlockSpec((t, tk_w), lambda j, i: (0, i)),
                      pl.BlockSpec((t, cc), lambda j, i: (0, j))],
            out_specs=pl.BlockSpec((None, tk_w, cc), lambda j, i: (j, i, 0)),
            out_shape=jax.ShapeDtypeStruct(w.shape, BF16),
            compiler_params=pltpu.CompilerParams(dimension_semantics=("arbitrary", "arbitrary"),
                                                 vmem_limit_bytes=MM_VMEM_LIMIT),
        )(a, dy)

    @jax.custom_vjp
    def op(a, w):
        return call_fwd(a, w)

    def fwd(a, w):
        return op(a, w), (a, w)

    def bwd(res, dy):
        a, w = res
        return call_dx(dy, w), call_dw(a, dy, w)

    op.defvjp(fwd, bwd)
    return op


def _row_spec(arr, tb):
    return pl.BlockSpec((tb, arr.shape[1]), lambda i: (i, 0))


def _full_spec(arr):
    return pl.BlockSpec(arr.shape, lambda i: (0, 0))


def _make_rowwise(name, f, n_rows, n_params, out_cols, diff_rows):
    n_out = len(out_cols)

    def call_fwd(rows, params):
        t = rows[0].shape[0]

        def body(*refs):
            ins = [r[...] for r in refs[:n_rows + n_params]]
            outs = f(*ins)
            for o_ref, o in zip(refs[n_rows + n_params:], outs):
                o_ref[...] = o

        return pl.pallas_call(
            body, name=name + "_fwd", grid=(t // ROW_TILE,),
            in_specs=[_row_spec(a, ROW_TILE) for a in rows] + [_full_spec(p) for p in params],
            out_specs=[pl.BlockSpec((ROW_TILE, n), lambda i: (i, 0)) for n in out_cols],
            out_shape=[jax.ShapeDtypeStruct((t, n), F32) for n in out_cols],
            compiler_params=pltpu.CompilerParams(dimension_semantics=("arbitrary",),
                                                 vmem_limit_bytes=MM_VMEM_LIMIT),
        )(*rows, *params)

    def call_bwd(rows, params, cts):
        t = rows[0].shape[0]
        d_rows = [a for a, d in zip(rows, diff_rows) if d]
        n_in = n_rows + n_params + n_out

        def body(*refs):
            ins = [r[...] for r in refs[:n_rows + n_params]]
            ct = tuple(r[...] for r in refs[n_rows + n_params:n_in])
            _, vjp = jax.vjp(f, *ins)
            grads = vjp(ct)
            out_refs = refs[n_in:]
            g_rows = [g for g, d in zip(grads[:n_rows], diff_rows) if d]
            for o_ref, g in zip(out_refs[:len(g_rows)], g_rows):
                o_ref[...] = g
            p_refs = out_refs[len(g_rows):]

            if p_refs:
                @pl.when(pl.program_id(0) == 0)
                def _():
                    for p_ref in p_refs:
                        p_ref[...] = jnp.zeros_like(p_ref)

                for p_ref, g in zip(p_refs, grads[n_rows:]):
                    p_ref[...] += g

        return pl.pallas_call(
            body, name=name + "_bwd", grid=(t // ROW_TILE,),
            in_specs=[_row_spec(a, ROW_TILE) for a in rows] + [_full_spec(p) for p in params]
            + [_row_spec(c, ROW_TILE) for c in cts],
            out_specs=[_row_spec(a, ROW_TILE) for a in d_rows] + [_full_spec(p) for p in params],
            out_shape=[jax.ShapeDtypeStruct(a.shape, F32) for a in d_rows]
            + [jax.ShapeDtypeStruct(p.shape, F32) for p in params],
            compiler_params=pltpu.CompilerParams(dimension_semantics=("arbitrary",),
                                                 vmem_limit_bytes=MM_VMEM_LIMIT),
        )(*rows, *params, *cts)

    @jax.custom_vjp
    def op(*args):
        return tuple(call_fwd(args[:n_rows], args[n_rows:]))

    def fwd(*args):
        return op(*args), args

    def bwd(args, cts):
        rows, params = args[:n_rows], args[n_rows:]
        outs = call_bwd(rows, params, cts)
        it = iter(outs)
        g_rows = [next(it) if d else jnp.zeros_like(a) for a, d in zip(rows, diff_rows)]
        return tuple(g_rows) + tuple(it)

    op.defvjp(fwd, bwd)
    return op


def _rms(x, g, n):
    return x * lax.rsqrt(jnp.sum(x * x, axis=-1, keepdims=True) * (1.0 / n) + EPS) * g


def _f_pre_attn(x, g, scale, shift):
    return (_rms(x, g, D_MODEL) * (1.0 + scale) + shift,)


def _f_mla_a(cq, ckv, gq, gkv):
    return _rms(cq, gq, MLA_Q_RANK), _rms(ckv, gkv, MLA_KV_RANK)


@jax.custom_vjp
def _split_lanes(x):
    return tuple(x[:, i * LANES:(i + 1) * LANES] for i in range(x.shape[1] // LANES))


def _split_lanes_fwd(x):
    return _split_lanes(x), None


def _split_lanes_bwd(_, cts):
    return (jnp.concatenate(cts, axis=1),)


_split_lanes.defvjp(_split_lanes_fwd, _split_lanes_bwd)


def _f_mla_b(qall, kn_all, kr, kr_sw, cos, sin, gqn, gqr, gqr_sw, gkn, gkr, gkr_sw):
    q = _split_lanes(qall)
    kn = _split_lanes(kn_all)
    qn_o, qr_o, kn_o = [], [], []
    for h in range(MLA_HEADS):
        qn, qr, qs = q[h], q[MLA_HEADS + h], q[2 * MLA_HEADS + h]
        ss = jnp.sum(qn * qn, axis=-1, keepdims=True) + jnp.sum(qr * qr, axis=-1, keepdims=True)
        rs = lax.rsqrt(ss * (1.0 / MLA_QK) + EPS)
        qn_o.append(qn * rs * gqn)
        qr_o.append((qr * rs * gqr) * cos + (qs * rs * gqr_sw) * sin)
        kn_o.append(_rms(kn[h], gkn, MLA_NOPE))
    rs = lax.rsqrt(jnp.sum(kr * kr, axis=-1, keepdims=True) * (1.0 / MLA_ROPE) + EPS)
    kr_o = (kr * rs * gkr) * cos + (kr_sw * rs * gkr_sw) * sin
    return (jnp.concatenate(qn_o, axis=1), jnp.concatenate(qr_o, axis=1), jnp.concatenate(kn_o, axis=1), kr_o)


def _f_post_attn(o_sb, o_mla, g_sb, g_mla):
    return (jnp.concatenate([_rms(o_sb, g_sb, SB_WIDTH), _rms(o_mla, g_mla, SB_WIDTH)], axis=1),)


def _f_pre_ffn(x, attn, gate, g, scale, shift):
    x2 = x + gate * attn
    return x2, _rms(x2, g, D_MODEL) * (1.0 + scale) + shift


def _f_swiglu(gt, up):
    return (gt / (1.0 + jnp.exp(-gt)) * up,)


def _f_loss(x2, ffn, target, gate):
    err = x2 + gate * ffn - target
    return (jnp.sum(err * err, axis=-1, keepdims=True) * (1.0 / D_MODEL),)


def _rope_tables(pos_col, freqs, sign):
    t = pos_col.shape[0]

    def body(p_ref, f_ref, s_ref, cos_ref, sin_ref):
        ang = p_ref[...].astype(F32) * f_ref[...]
        live = jnp.abs(s_ref[...])
        cos_ref[...] = jnp.cos(ang) * live
        sin_ref[...] = jnp.sin(ang) * s_ref[...]

    return pl.pallas_call(
        body, name="rope_tables", grid=(t // ROW_TILE,),
        in_specs=[pl.BlockSpec((ROW_TILE, 1), lambda i: (i, 0)), _full_spec(freqs), _full_spec(sign)],
        out_specs=[pl.BlockSpec((ROW_TILE, LANES), lambda i: (i, 0))] * 2,
        out_shape=[jax.ShapeDtypeStruct((t, LANES), F32)] * 2,
    )(pos_col, freqs, sign)


def _hi_lo_dot(x, tri):
    hi = x.astype(BF16)
    lo = (x - hi.astype(F32)).astype(BF16)
    return (jnp.dot(hi, tri, preferred_element_type=F32) + jnp.dot(lo, tri, preferred_element_type=F32))


def _tri(cmp):
    r = lax.broadcasted_iota(jnp.int32, (ATT_BLK, ATT_BLK), 0)
    c = lax.broadcasted_iota(jnp.int32, (ATT_BLK, ATT_BLK), 1)
    return cmp(r, c).astype(BF16)


def _nt(a, b):
    return lax.dot_general(a, b, (((1,), (1,)), ((), ())), preferred_element_type=F32)


def _tn(a, b):
    return lax.dot_general(a, b, (((0,), (0,)), ((), ())), preferred_element_type=F32)


def _sb_logs(z):
    lb = jnp.minimum(z, 0.0) - jnp.log(1.0 + jnp.exp(-jnp.abs(z)))
    return lb, lb - z


def _sb_fwd(q, k, v):
    t = q.shape[0]
    nq = t // ATT_BLK
    scale = SB_HEAD_DIM ** -0.5

    def body(q_ref, k_ref, v_ref, o_ref, tot_ref):
        qi = pl.program_id(1)
        lane = lax.broadcasted_iota(jnp.int32, (ATT_BLK, LANES), 1)
        tri = _tri(lambda r, c: r > c)
        qv = q_ref[...] * scale
        heads = [(lane // SB_HEAD_DIM) == hh for hh in range(2)]
        qms = [jnp.where(mine, qv, 0.0).astype(BF16) for mine in heads]

        def blocks(kbs, carry, diagonal):
            acc = carry[0]
            nb = len(kbs)
            chains = [(b, hh) for b in range(nb) for hh in range(2)]
            offs = [pl.multiple_of(kb * ATT_BLK, ATT_BLK) for kb in kbs]
            kks = [k_ref[pl.ds(off, ATT_BLK), :].astype(BF16) for off in offs]
            v_blks = [v_ref[pl.ds(off, ATT_BLK), :] for off in offs]
            if diagonal:
                valid = (lax.broadcasted_iota(jnp.int32, (ATT_BLK, ATT_BLK), 1)
                         < lax.broadcasted_iota(jnp.int32, (ATT_BLK, ATT_BLK), 0))
            zs = {ch: _nt(qms[ch[1]], kks[ch[0]]) for ch in chains}
            vvs = {(b, hh): jnp.where(heads[hh], v_blks[b], 0.0).astype(BF16) for b, hh in chains}
            logs = {ch: _sb_logs(zs[ch]) for ch in chains}
            l1ms = {ch: jnp.where(valid, logs[ch][1], 0.0) if diagonal else logs[ch][1] for ch in chains}
            run = {(0, hh): carry[1 + hh] for hh in range(2)}
            for b, hh in chains:
                run[(b + 1, hh)] = run[(b, hh)] + jnp.sum(l1ms[(b, hh)], axis=-1, keepdims=True)
            afters = {ch: _hi_lo_dot(l1ms[ch], tri) for ch in chains}
            ws = {ch: jnp.exp(logs[ch][0] + (afters[ch] + run[ch])) for ch in chains}
            if diagonal:
                ws = {ch: jnp.where(valid, ws[ch], 0.0) for ch in chains}
            for ch in chains:
                acc = acc + jnp.dot(ws[ch].astype(BF16), vvs[ch], preferred_element_type=F32)
            return (acc, run[(nb, 0)], run[(nb, 1)])

        zero = jnp.zeros((ATT_BLK, 1), F32)
        carry = blocks([qi], (jnp.zeros((ATT_BLK, LANES), F32), zero, zero), True)
        carry = lax.fori_loop(0, qi // 2, lambda pr, cr: blocks([qi - 1 - 2 * pr, qi - 2 - 2 * pr], cr, False), carry)
        carry = lax.cond(qi % 2 == 1, lambda cr: blocks([qi * 0], cr, False), lambda cr: cr, carry)
        o_ref[...] = carry[0]
        for hh in range(2):
            tot_ref[:, hh * LANES:(hh + 1) * LANES] = jnp.broadcast_to(carry[1 + hh], (ATT_BLK, LANES))

    return pl.pallas_call(
        body, name="sb_attn_fwd", grid=(SB_HEADS // 2, nq),
        in_specs=[pl.BlockSpec((ATT_BLK, LANES), lambda p, i: (i, p)),
                  pl.BlockSpec((t, LANES), lambda p, i: (0, p)),
                  pl.BlockSpec((t, LANES), lambda p, i: (0, p))],
        out_specs=[pl.BlockSpec((ATT_BLK, LANES), lambda p, i: (i, p)),
                   pl.BlockSpec((ATT_BLK, 2 * LANES), lambda p, i: (i, p))],
        out_shape=[jax.ShapeDtypeStruct((t, SB_WIDTH), F32), jax.ShapeDtypeStruct((t, SB_HEADS * LANES), F32)],
        compiler_params=pltpu.CompilerParams(dimension_semantics=("arbitrary", "arbitrary")),
    )(q, k, v)


def _sb_bwd(q, k, v, tot, do):
    t = q.shape[0]
    nq = t // ATT_BLK
    scale = SB_HEAD_DIM ** -0.5

    def body(q_ref, k_ref, v_ref, tot_ref, do_ref, dq_ref, dk_ref, dv_ref):
        qi = pl.program_id(1)

        @pl.when(qi == 0)
        def _():
            dk_ref[...] = jnp.zeros_like(dk_ref)
            dv_ref[...] = jnp.zeros_like(dv_ref)

        lane = lax.broadcasted_iota(jnp.int32, (ATT_BLK, LANES), 1)
        tri_incl = _tri(lambda r, c: r <= c)
        tri_lt = _tri(lambda r, c: r < c)
        qv = q_ref[...] * scale
        dov = do_ref[...]
        heads = [(lane // SB_HEAD_DIM) == hh for hh in range(2)]
        qms = [jnp.where(mine, qv, 0.0).astype(BF16) for mine in heads]
        doms = [jnp.where(mine, dov, 0.0).astype(BF16) for mine in heads]
        tots = [tot_ref[:, hh * LANES:hh * LANES + 1] for hh in range(2)]

        def blocks(kbs, carry, diagonal):
            dq = carry[0]
            nb = len(kbs)
            chains = [(b, hh) for b in range(nb) for hh in range(2)]
            offs = [pl.multiple_of(kb * ATT_BLK, ATT_BLK) for kb in kbs]
            k_blks = [k_ref[pl.ds(off, ATT_BLK), :] for off in offs]
            vvs = [v_ref[pl.ds(off, ATT_BLK), :].astype(BF16) for off in offs]
            if diagonal:
                valid = (lax.broadcasted_iota(jnp.int32, (ATT_BLK, ATT_BLK), 1)
                         < lax.broadcasted_iota(jnp.int32, (ATT_BLK, ATT_BLK), 0))
            kks = {(b, hh): jnp.where(heads[hh], k_blks[b], 0.0).astype(BF16) for b, hh in chains}
            zs = {ch: _nt(qms[ch[1]], kks[ch]) for ch in chains}
            dws = {ch: _nt(doms[ch[1]], vvs[ch[0]]) for ch in chains}
            logs = {ch: _sb_logs(zs[ch]) for ch in chains}
            lbs = {ch: logs[ch][0] for ch in chains}
            l1m_all = {ch: logs[ch][1] for ch in chains}
            l1ms = {ch: jnp.where(valid, l1m_all[ch], 0.0) for ch in chains} if diagonal else l1m_all
            pre, c_de = {}, {}
            for hh in range(2):
                pre[(0, hh)], c_de[(0, hh)] = carry[1 + 2 * hh], carry[2 + 2 * hh]
            for b, hh in chains:
                pre[(b + 1, hh)] = pre[(b, hh)] + jnp.sum(l1ms[(b, hh)], axis=-1, keepdims=True)
            prefix = {ch: _hi_lo_dot(l1ms[ch], tri_incl) for ch in chains}
            ws = {ch: jnp.exp(lbs[ch] + (tots[ch[1]] - (prefix[ch] + pre[ch]))) for ch in chains}
            if diagonal:
                ws = {ch: jnp.where(valid, ws[ch], 0.0) for ch in chains}
            d_es = {ch: ws[ch] * dws[ch] for ch in chains}
            for b, hh in chains:
                c_de[(b + 1, hh)] = c_de[(b, hh)] + jnp.sum(d_es[(b, hh)], axis=-1, keepdims=True)
            dvs = [_tn(ws[(b, 0)].astype(BF16), doms[0]) + _tn(ws[(b, 1)].astype(BF16), doms[1]) for b in range(nb)]
            dl1ms = {ch: _hi_lo_dot(d_es[ch], tri_lt) + c_de[ch] for ch in chains}
            dzs = {ch: d_es[ch] * jnp.exp(l1m_all[ch]) - dl1ms[ch] * jnp.exp(lbs[ch]) for ch in chains}
            if diagonal:
                dzs = {ch: jnp.where(valid, dzs[ch], 0.0) for ch in chains}
            dzs = {ch: dzs[ch].astype(BF16) for ch in chains}
            for ch in chains:
                dq = dq + jnp.dot(dzs[ch], kks[ch], preferred_element_type=F32)
            for b in range(nb):
                dk_ref[pl.ds(offs[b], ATT_BLK), :] += _tn(dzs[(b, 0)], qms[0]) + _tn(dzs[(b, 1)], qms[1])
                dv_ref[pl.ds(offs[b], ATT_BLK), :] += dvs[b]
            return (dq, pre[(nb, 0)], c_de[(nb, 0)], pre[(nb, 1)], c_de[(nb, 1)])

        zero = jnp.zeros((ATT_BLK, 1), F32)
        carry = lax.fori_loop(0, qi // 2, lambda pr, cr: blocks([2 * pr, 2 * pr + 1], cr, False),
                              (jnp.zeros((ATT_BLK, LANES), F32), zero, zero, zero, zero))
        carry = lax.cond(qi % 2 == 1, lambda cr: blocks([qi - 1], cr, False), lambda cr: cr, carry)
        carry = blocks([qi], carry, True)
        dq_ref[...] = carry[0] * scale

    return pl.pallas_call(
        body, name="sb_attn_bwd", grid=(SB_HEADS // 2, nq),
        in_specs=[pl.BlockSpec((ATT_BLK, LANES), lambda p, i: (i, p)),
                  pl.BlockSpec((t, LANES), lambda p, i: (0, p)),
                  pl.BlockSpec((t, LANES), lambda p, i: (0, p)),
                  pl.BlockSpec((ATT_BLK, 2 * LANES), lambda p, i: (i, p)),
                  pl.BlockSpec((ATT_BLK, LANES), lambda p, i: (i, p))],
        out_specs=[pl.BlockSpec((ATT_BLK, LANES), lambda p, i: (i, p)),
                   pl.BlockSpec((t, LANES), lambda p, i: (0, p)),
                   pl.BlockSpec((t, LANES), lambda p, i: (0, p))],
        out_shape=[jax.ShapeDtypeStruct((t, SB_WIDTH), F32)] * 3,
        compiler_params=pltpu.CompilerParams(dimension_semantics=("arbitrary", "arbitrary")),
    )(q, k, v, tot, do)


@jax.custom_vjp
def _sb_attention(q, k, v):
    return _sb_fwd(q, k, v)[0]


def _sb_attention_fwd(q, k, v):
    o, tot = _sb_fwd(q, k, v)
    return o, (q, k, v, tot)


def _sb_attention_bwd(res, do):
    return tuple(_sb_bwd(*res, do))


_sb_attention.defvjp(_sb_attention_fwd, _sb_attention_bwd)


def _mla_fwd(qn, qr, kn, kr, v):
    t = qn.shape[0]
    nq = t // ATT_BLK
    scale = MLA_QK ** -0.5

    def body(qn_ref, qr_ref, kn_ref, kr_ref, v_ref, o_ref, lse_ref):
        qi = pl.program_id(1)
        lanes = [slice(hh * LANES, (hh + 1) * LANES) for hh in range(2)]
        qnb = [qn_ref[:, sl].astype(BF16) for sl in lanes]
        qrb = [qr_ref[:, sl].astype(BF16) for sl in lanes]

        def blocks(kbs, carry, diagonal):
            nb = len(kbs)
            chains = [(b, hh) for b in range(nb) for hh in range(2)]
            offs = [pl.multiple_of(kb * ATT_BLK, ATT_BLK) for kb in kbs]
            krbs = [kr_ref[pl.ds(off, ATT_BLK), :].astype(BF16) for off in offs]
            accs, ms, ls = [carry[0], carry[3]], [carry[1], carry[4]], [carry[2], carry[5]]
            ss = {(b, hh): (_nt(qnb[hh], kn_ref[pl.ds(offs[b], ATT_BLK), lanes[hh]].astype(BF16))
                            + _nt(qrb[hh], krbs[b])) * scale for b, hh in chains}
            if diagonal:
                causal = (lax.broadcasted_iota(jnp.int32, (ATT_BLK, ATT_BLK), 1)
                          <= lax.broadcasted_iota(jnp.int32, (ATT_BLK, ATT_BLK), 0))
                ss = {ch: jnp.where(causal, ss[ch], -jnp.inf) for ch in chains}
            m_new = list(ms)
            for b, hh in chains:
                m_new[hh] = jnp.maximum(m_new[hh], jnp.max(ss[(b, hh)], axis=-1, keepdims=True))
            ps = {(b, hh): jnp.exp(ss[(b, hh)] - m_new[hh]) for b, hh in chains}
            alphas = [jnp.exp(ms[hh] - m_new[hh]) for hh in range(2)]
            pvs = {(b, hh): jnp.dot(ps[(b, hh)].astype(BF16), v_ref[pl.ds(offs[b], ATT_BLK), lanes[hh]].astype(BF16),
                                    preferred_element_type=F32) for b, hh in chains}
            out = []
            for hh in range(2):
                acc, l = accs[hh] * alphas[hh], ls[hh] * alphas[hh]
                for b in range(nb):
                    acc, l = acc + pvs[(b, hh)], l + jnp.sum(ps[(b, hh)], axis=-1, keepdims=True)
                out += [acc, m_new[hh], l]
            return tuple(out)

        init = (jnp.zeros((ATT_BLK, LANES), F32), jnp.full((ATT_BLK, 1), -jnp.inf, F32), jnp.zeros((ATT_BLK, 1), F32))
        carry = blocks([qi], init + init, True)
        carry = lax.fori_loop(0, qi // 2, lambda pr, cr: blocks([2 * pr, 2 * pr + 1], cr, False), carry)
        carry = lax.cond(qi % 2 == 1, lambda cr: blocks([qi - 1], cr, False), lambda cr: cr, carry)
        for hh in range(2):
            acc, m, l = carry[3 * hh:3 * hh + 3]
            o_ref[:, lanes[hh]] = acc / l
            lse_ref[:, lanes[hh]] = jnp.broadcast_to(m + jnp.log(l), (ATT_BLK, LANES))

    blk = pl.BlockSpec((ATT_BLK, 2 * LANES), lambda p, i: (i, p))
    full = pl.BlockSpec((t, 2 * LANES), lambda p, i: (0, p))
    return pl.pallas_call(
        body, name="mla_attn_fwd", grid=(MLA_HEADS // 2, nq),
        in_specs=[blk, blk, full, pl.BlockSpec((t, LANES), lambda p, i: (0, 0)), full],
        out_specs=[blk, blk],
        out_shape=[jax.ShapeDtypeStruct((t, MLA_HEADS * LANES), F32)] * 2,
        compiler_params=pltpu.CompilerParams(dimension_semantics=("arbitrary", "arbitrary")),
    )(qn, qr, kn, kr, v)


def _mla_bwd(qn, qr, kn, kr, v, o, lse, do):
    t = qn.shape[0]
    nq = t // ATT_BLK
    scale = MLA_QK ** -0.5

    def body(qn_ref, qr_ref, kn_ref, kr_ref, v_ref, o_ref, lse_ref, do_ref,
             dqn_ref, dqr_ref, dkn_ref, dkr_ref, dv_ref):
        pair = pl.program_id(0)
        qi = pl.program_id(1)

        @pl.when(qi == 0)
        def _():
            dkn_ref[...] = jnp.zeros_like(dkn_ref)
            dv_ref[...] = jnp.zeros_like(dv_ref)

        @pl.when((qi == 0) & (pair == 0))
        def _():
            dkr_ref[...] = jnp.zeros_like(dkr_ref)

        lanes = [slice(hh * LANES, (hh + 1) * LANES) for hh in range(2)]
        qnb = [qn_ref[:, sl].astype(BF16) for sl in lanes]
        qrb = [qr_ref[:, sl].astype(BF16) for sl in lanes]
        dob = [do_ref[:, sl].astype(BF16) for sl in lanes]
        delta = [jnp.sum(do_ref[:, sl] * o_ref[:, sl], axis=-1, keepdims=True) for sl in lanes]
        lse_v = [lse_ref[:, hh * LANES:hh * LANES + 1] for hh in range(2)]

        def blocks(kbs, carry, diagonal):
            nb = len(kbs)
            chains = [(b, hh) for b in range(nb) for hh in range(2)]
            offs = [pl.multiple_of(kb * ATT_BLK, ATT_BLK) for kb in kbs]
            krbs = [kr_ref[pl.ds(off, ATT_BLK), :].astype(BF16) for off in offs]
            knb = {(b, hh): kn_ref[pl.ds(offs[b], ATT_BLK), lanes[hh]].astype(BF16) for b, hh in chains}
            vb = {(b, hh): v_ref[pl.ds(offs[b], ATT_BLK), lanes[hh]].astype(BF16) for b, hh in chains}
            ss = {(b, hh): _nt(qnb[hh], knb[(b, hh)]) + _nt(qrb[hh], krbs[b]) for b, hh in chains}
            dps = {(b, hh): _nt(dob[hh], vb[(b, hh)]) for b, hh in chains}
            ps = {(b, hh): jnp.exp(ss[(b, hh)] * scale - lse_v[hh]) for b, hh in chains}
            if diagonal:
                causal = (lax.broadcasted_iota(jnp.int32, (ATT_BLK, ATT_BLK), 1)
                          <= lax.broadcasted_iota(jnp.int32, (ATT_BLK, ATT_BLK), 0))
                ps = {ch: jnp.where(causal, ps[ch], 0.0) for ch in chains}
            dss = {(b, hh): (ps[(b, hh)] * (dps[(b, hh)] - delta[hh]) * scale).astype(BF16) for b, hh in chains}
            for b, hh in chains:
                dv_ref[pl.ds(offs[b], ATT_BLK), lanes[hh]] += _tn(ps[(b, hh)].astype(BF16), dob[hh])
            for b, hh in chains:
                dkn_ref[pl.ds(offs[b], ATT_BLK), lanes[hh]] += _tn(dss[(b, hh)], qnb[hh])
            for b in range(nb):
                dkr_ref[pl.ds(offs[b], ATT_BLK), :] += _tn(dss[(b, 0)], qrb[0]) + _tn(dss[(b, 1)], qrb[1])
            out = list(carry)
            for b, hh in chains:
                out[2 * hh] = out[2 * hh] + jnp.dot(dss[(b, hh)], knb[(b, hh)], preferred_element_type=F32)
                out[2 * hh + 1] = out[2 * hh + 1] + jnp.dot(dss[(b, hh)], krbs[b], preferred_element_type=F32)
            return tuple(out)

        zero = jnp.zeros((ATT_BLK, LANES), F32)
        carry = lax.fori_loop(0, qi // 2, lambda pr, cr: blocks([2 * pr, 2 * pr + 1], cr, False),
                              (zero, zero, zero, zero))
        carry = lax.cond(qi % 2 == 1, lambda cr: blocks([qi - 1], cr, False), lambda cr: cr, carry)
        carry = blocks([qi], carry, True)
        for hh in range(2):
            dqn_ref[:, lanes[hh]] = carry[2 * hh]
            dqr_ref[:, lanes[hh]] = carry[2 * hh + 1]

    blk = pl.BlockSpec((ATT_BLK, 2 * LANES), lambda p, i: (i, p))
    full = pl.BlockSpec((t, 2 * LANES), lambda p, i: (0, p))
    shared = pl.BlockSpec((t, LANES), lambda p, i: (0, 0))
    wide = jax.ShapeDtypeStruct((t, MLA_HEADS * LANES), F32)
    return pl.pallas_call(
        body, name="mla_attn_bwd", grid=(MLA_HEADS // 2, nq),
        in_specs=[blk, blk, full, shared, full, blk, blk, blk],
        out_specs=[blk, blk, full, shared, full],
        out_shape=[wide, wide, wide, jax.ShapeDtypeStruct((t, LANES), F32), wide],
        compiler_params=pltpu.CompilerParams(dimension_semantics=("arbitrary", "arbitrary")),
    )(qn, qr, kn, kr, v, o, lse, do)


@jax.custom_vjp
def _mla_attention(qn, qr, kn, kr, v):
    return _mla_fwd(qn, qr, kn, kr, v)[0]


def _mla_attention_fwd(qn, qr, kn, kr, v):
    o, lse = _mla_fwd(qn, qr, kn, kr, v)
    return o, (qn, qr, kn, kr, v, o, lse)


def _mla_attention_bwd(res, do):
    return tuple(_mla_bwd(*res, do))


_mla_attention.defvjp(_mla_attention_fwd, _mla_attention_bwd)


def _ffn_in(h, wg, wu):
    t, k = h.shape
    n_sh, _, cc = wg.shape

    def body(h_ref, wg_ref, wu_ref, g_ref, u_ref, a_ref):
        hb = h_ref[...].astype(BF16)
        g = jnp.dot(hb, wg_ref[...], preferred_element_type=F32)
        u = jnp.dot(hb, wu_ref[...], preferred_element_type=F32)
        g_ref[...] = g
        u_ref[...] = u
        a_ref[...] = _f_swiglu(g, u)[0].astype(BF16)

    w_spec = pl.BlockSpec((None, k, cc), lambda j, i: (j, 0, 0))
    o_spec = pl.BlockSpec((ROW_TILE, cc), lambda j, i: (i, j))
    wide = (t, n_sh * cc)
    return pl.pallas_call(
        body, name="ffn_in_fwd", grid=(n_sh, t // ROW_TILE),
        in_specs=[pl.BlockSpec((ROW_TILE, k), lambda j, i: (i, 0)), w_spec, w_spec],
        out_specs=[o_spec, o_spec, o_spec],
        out_shape=[jax.ShapeDtypeStruct(wide, F32), jax.ShapeDtypeStruct(wide, F32), jax.ShapeDtypeStruct(wide, BF16)],
        compiler_params=pltpu.CompilerParams(dimension_semantics=("arbitrary", "arbitrary"),
                                             vmem_limit_bytes=MM_VMEM_LIMIT),
    )(h, wg, wu)


def _ffn_mid_bwd(dy, wd, g, u):
    t, n = dy.shape
    n_sh, cc, _ = wd.shape

    def body(dy_ref, wd_ref, g_ref, u_ref, dg_ref, du_ref):
        d_act = _nt(dy_ref[...].astype(BF16), wd_ref[...])
        _, vjp = jax.vjp(_f_swiglu, g_ref[...], u_ref[...])
        dg, du = vjp((d_act,))
        dg_ref[...] = dg.astype(BF16)
        du_ref[...] = du.astype(BF16)

    blk = pl.BlockSpec((ROW_TILE, cc), lambda j, i: (i, j))
    wide = jax.ShapeDtypeStruct((t, n_sh * cc), BF16)
    return pl.pallas_call(
        body, name="ffn_mid_bwd", grid=(n_sh, t // ROW_TILE),
        in_specs=[pl.BlockSpec((ROW_TILE, n), lambda j, i: (i, 0)),
                  pl.BlockSpec((None, cc, n), lambda j, i: (j, 0, 0)), blk, blk],
        out_specs=[blk, blk], out_shape=[wide, wide],
        compiler_params=pltpu.CompilerParams(dimension_semantics=("arbitrary", "arbitrary"),
                                             vmem_limit_bytes=MM_VMEM_LIMIT),
    )(dy, wd, g, u)


def _ffn_dh(dg, du, wg, wu):
    t = dg.shape[0]
    n_sh, k, cc = wg.shape

    def body(dg_ref, du_ref, wg_ref, wu_ref, o_ref):
        acc = jnp.zeros((ROW_TILE, k), F32)
        for j in range(n_sh):
            cols = slice(j * cc, (j + 1) * cc)
            acc = acc + _nt(dg_ref[:, cols], wg_ref[j]) + _nt(du_ref[:, cols], wu_ref[j])
        o_ref[...] = acc

    blk = pl.BlockSpec((ROW_TILE, n_sh * cc), lambda i: (i, 0))
    w_spec = pl.BlockSpec((n_sh, k, cc), lambda i: (0, 0, 0))
    return pl.pallas_call(
        body, name="ffn_dh", grid=(t // ROW_TILE,),
        in_specs=[blk, blk, w_spec, w_spec],
        out_specs=pl.BlockSpec((ROW_TILE, k), lambda i: (i, 0)),
        out_shape=jax.ShapeDtypeStruct((t, k), F32),
        compiler_params=pltpu.CompilerParams(dimension_semantics=("arbitrary",), vmem_limit_bytes=MM_VMEM_LIMIT),
    )(dg, du, wg, wu)


def _ffn_dw_in(h, dy, n_sh, name):
    t, k = h.shape
    cc = dy.shape[1] // n_sh
    tk = 512

    def body(h_ref, dy_ref, o_ref):
        o_ref[...] = _tn(h_ref[...].astype(BF16), dy_ref[...]).astype(BF16)

    return pl.pallas_call(
        body, name=name, grid=(n_sh, k // tk),
        in_specs=[pl.BlockSpec((t, tk), lambda j, i: (0, i)), pl.BlockSpec((t, cc), lambda j, i: (0, j))],
        out_specs=pl.BlockSpec((None, tk, cc), lambda j, i: (j, i, 0)),
        out_shape=jax.ShapeDtypeStruct((n_sh, k, cc), BF16),
        compiler_params=pltpu.CompilerParams(dimension_semantics=("arbitrary", "arbitrary"),
                                             vmem_limit_bytes=MM_VMEM_LIMIT),
    )(h, dy)


@jax.custom_vjp
def _ffn_block(h, wg, wu, wd):
    act = _ffn_in(h, wg, wu)[2]
    return _mm(act, wd.reshape(-1, wd.shape[2]), "nn", "ffn_down_fwd", ROW_TILE, wd.shape[2])


def _ffn_block_fwd(h, wg, wu, wd):
    g, u, act = _ffn_in(h, wg, wu)
    y = _mm(act, wd.reshape(-1, wd.shape[2]), "nn", "ffn_down_fwd", ROW_TILE, wd.shape[2])
    return y, (h, wg, wu, wd, g, u, act)


def _ffn_block_bwd(res, dy):
    h, wg, wu, wd, g, u, act = res
    dg, du = _ffn_mid_bwd(dy, wd, g, u)
    dh = _ffn_dh(dg, du, wg, wu)
    n_sh = wg.shape[0]
    dwg = _ffn_dw_in(h, dg, n_sh, "ffn_gate_dw")
    dwu = _ffn_dw_in(h, du, n_sh, "ffn_up_dw")
    dwd = _mm(act, dy, "tn", "ffn_down_dw", 256, wd.shape[2], out_dtype=BF16).reshape(wd.shape)
    return dh, dwg, dwu, dwd


_ffn_block.defvjp(_ffn_block_fwd, _ffn_block_bwd)


def _split_cols(x, cuts):
    cuts = tuple(cuts)

    @jax.custom_vjp
    def op(x):
        return tuple(x[:, a:b] for a, b in zip((0,) + cuts, cuts + (x.shape[1],)))

    def fwd(x):
        return op(x), None

    def bwd(_, cts):
        return (jnp.concatenate(cts, axis=1),)

    op.defvjp(fwd, bwd)
    return op(x)


def _swap_halves(w):
    half = w.shape[-1] // 2
    return jnp.concatenate([w[..., half:], w[..., :half]], axis=-1)


def _pad_lanes(w):
    return jnp.concatenate([w, jnp.zeros(w.shape[:-1] + (LANES - w.shape[-1],), w.dtype)], axis=-1)


def _join_cols(shards):
    return shards.transpose(1, 0, 2).reshape(shards.shape[1], -1)


def _mod_parts(mod):
    return [mod[:, i * D_MODEL:(i + 1) * D_MODEL] for i in range(N_MOD)]


def _local_loss(x, mod, p, cos, sin, target):
    return _ffn_stage(x, _mixing_stage(x, mod, p, cos, sin), mod, p, target)


def _mixing_stage(x, mod, p, cos, sin):
    shift1, scale1 = _mod_parts(mod)[:2]

    w_in = _join_cols(p["w_in"])
    k_rope_w = w_in[:, 2176:2240]
    w_in_ext = jnp.concatenate([w_in[:, :2176], _pad_lanes(k_rope_w), _pad_lanes(_swap_halves(k_rope_w)),
                                jnp.zeros((D_MODEL, LANES), w_in.dtype)], axis=1)
    (h1,) = _make_rowwise("pre_attn", _f_pre_attn, 1, 3, [D_MODEL], [True])(x, p["norm_attn"], scale1, shift1)
    proj = _make_linear("in_proj", 512, 640)(h1, w_in_ext)
    q_sb, k_sb, v_sb, cq, ckv, kr, kr_sw, _ = _split_cols(proj, (512, 1024, 1536, 1920, 2176, 2304, 2432))

    o_sb = _sb_attention(q_sb, k_sb, v_sb)

    wq = _join_cols(p["w_q_up"]).reshape(MLA_Q_RANK, MLA_HEADS, MLA_QK)
    wq_n, wq_r = wq[:, :, :MLA_NOPE], wq[:, :, MLA_NOPE:]
    w_q_ext = jnp.concatenate([wq_n.reshape(MLA_Q_RANK, -1), _pad_lanes(wq_r).reshape(MLA_Q_RANK, -1),
                               _pad_lanes(_swap_halves(wq_r)).reshape(MLA_Q_RANK, -1)], axis=1)
    wkv = _join_cols(p["w_kv_up"]).reshape(MLA_KV_RANK, MLA_HEADS, MLA_NOPE + MLA_V)
    w_kv_ext = jnp.concatenate([wkv[:, :, :MLA_NOPE].reshape(MLA_KV_RANK, -1),
                                wkv[:, :, MLA_NOPE:].reshape(MLA_KV_RANK, -1)], axis=1)
    cqn, ckvn = _make_rowwise("mla_a", _f_mla_a, 2, 2, [MLA_Q_RANK, MLA_KV_RANK], [True, True])(
        cq, ckv, p["q_a_norm"], p["kv_a_norm"])
    qall = _make_linear("q_up", 384, 768)(cqn, w_q_ext)
    kvall = _make_linear("kv_up", 256, 1024)(ckvn, w_kv_ext)
    kn_all, v_mla = _split_cols(kvall, (512,))
    gq = p["q_norm"]
    gkr = p["k_rope_norm"]
    qn, qr, kn, krr = _make_rowwise("mla_b", _f_mla_b, 6, 6, [512, 512, 512, LANES],
                                    [True, True, True, True, False, False])(
        qall, kn_all, kr, kr_sw, cos, sin,
        gq[:, :MLA_NOPE], _pad_lanes(gq[:, MLA_NOPE:]), _pad_lanes(_swap_halves(gq[:, MLA_NOPE:])),
        p["k_nope_norm"], _pad_lanes(gkr), _pad_lanes(_swap_halves(gkr)))
    o_mla = _mla_attention(qn, qr, kn, krr, v_mla)

    (mixed,) = _make_rowwise("post_attn", _f_post_attn, 2, 2, [D_MODEL], [True, True])(
        o_sb, o_mla, p["out_norm_sb"], p["out_norm_mla"])
    return mixed


def _ffn_stage(x, mixed, mod, p, target):
    _, _, gate1, shift2, scale2, gate2 = _mod_parts(mod)
    attn = _make_linear("out_proj", 512, 512)(mixed, p["w_out"].reshape(D_MODEL, D_MODEL))

    x2, h2 = _make_rowwise("pre_ffn", _f_pre_ffn, 2, 4, [D_MODEL, D_MODEL], [True, True])(
        x, attn, gate1, p["norm_ffn"], scale2, shift2)
    ffn = _ffn_block(h2, p["w_gate"], p["w_up"], p["w_down"])
    (row_loss,) = _make_rowwise("loss", _f_loss, 3, 1, [1], [True, True, False])(x2, ffn, target, gate2)
    return 0.5 * jnp.sum(row_loss)


def _my_place():
    return lax.axis_index("x"), lax.axis_index("y"), lax.axis_index("c")


def _all_gather_small(block, name):
    m_per, n = block.shape

    def body(x_ref, out_ref, send_sems, recv_sems, local_sem):
        x, y, c = _my_place()
        me, sibling = (x, y, c), (x, y, 1 - c)
        chips = [(1 - x, y), (x, 1 - y), (1 - x, 1 - y)]

        def rows(px, py, pc):
            return out_ref.at[pl.ds((4 * px + 2 * py + pc) * m_per, m_per), :]

        def copy(k, blk, to, src=None):
            return pltpu.make_async_remote_copy(
                src_ref=rows(*blk) if src is None else src, dst_ref=rows(*blk),
                send_sem=send_sems.at[k], recv_sem=recv_sems.at[k], device_id=to, device_id_type=MESH)

        mine = pltpu.make_async_copy(x_ref, rows(*me), local_sem)
        mine.start()
        first = [copy(0, me, sibling, src=x_ref)]
        first += [copy(1 + j, me, (*chip, c), src=x_ref) for j, chip in enumerate(chips)]
        for cp in first:
            cp.start()
        passed = [copy(4 + j, (*chip, c), sibling) for j, chip in enumerate(chips)]
        for j, chip in enumerate(chips):
            copy(1 + j, (*chip, c), me).wait_recv()
            passed[j].start()
        copy(0, sibling, me).wait_recv()
        for j, chip in enumerate(chips):
            copy(4 + j, (*chip, 1 - c), me).wait_recv()
        for cp in first + passed:
            cp.wait_send()
        mine.wait()

    return pl.pallas_call(
        body, name=name,
        out_shape=jax.ShapeDtypeStruct((N_DEV * m_per, n), block.dtype),
        in_specs=[pl.BlockSpec(memory_space=pltpu.VMEM)],
        out_specs=pl.BlockSpec(memory_space=pltpu.VMEM),
        scratch_shapes=[pltpu.SemaphoreType.DMA((7,)), pltpu.SemaphoreType.DMA((7,)), pltpu.SemaphoreType.DMA],
    )(block)


EARLY = ("w_in", "w_q_up", "w_kv_up")
LATE = ("w_out", "w_gate", "w_up", "w_down")
BIG = EARLY + LATE
HALF_AXIS = {"w_in": 0, "w_q_up": 0, "w_kv_up": 0, "w_out": 0, "w_gate": 0, "w_up": 0, "w_down": 1}


def _half(ref, h, axis, lead=()):
    trail = ref.shape[len(lead):]
    idx = list(lead) + [slice(None)] * len(trail)
    at = len(trail) - 2 + axis
    n2 = trail[at] // 2
    idx[len(lead) + at] = pl.ds(h * n2, n2)
    return ref.at[tuple(idx)]


def _half_shape(shape, axis):
    shape = list(shape)
    shape[len(shape) - 2 + axis] //= 2
    return tuple(shape)


def _remote(src, dst, send_sems, recv_sems, k, to):
    return pltpu.make_async_remote_copy(src_ref=src, dst_ref=dst, send_sem=send_sems.at[k],
                                        recv_sem=recv_sems.at[k], device_id=to, device_id_type=MESH)


def _gather_weights(names, shards, after):
    n_w = len(shards)
    axes = [HALF_AXIS[n] for n in names]

    def body(*refs):
        w_refs, out_refs, token = refs[:n_w], refs[n_w + 1:2 * n_w + 1], refs[2 * n_w + 1]
        send_sems, recv_sems, local_sems = refs[2 * n_w + 2:]
        token[...] = jnp.zeros_like(token)
        x, y, c = _my_place()
        sibling = (x, y, 1 - c)
        chips = [(1 - x, y), (x, 1 - y), (1 - x, 1 - y)]
        me = 2 * x + y
        mine =[pltpu.make_async_copy(w, o.at[me], local_sems.at[i]) for i, (w, o) in enumerate(zip(w_refs, out_refs))]
        for cp in mine:
            cp.start()
        first = [_remote(_half(w_refs[i], c, axes[i]), _half(out_refs[i], c, axes[i], (me,)),
                         send_sems, recv_sems, 6 * i + j, (*chip, c))
                 for i in range(n_w) for j, chip in enumerate(chips)]
        for cp in first:
            cp.start()
        passed = []
        for j, (cx, cy) in enumerate(chips):
            for i in range(n_w):
                blk = _half(out_refs[i], c, axes[i], (2 * cx + cy,))
                _remote(blk, blk, send_sems, recv_sems, 6 * i + j, (cx, cy, c)).wait_recv()
                cp = _remote(blk, blk, send_sems, recv_sems, 6 * i + 3 + j, sibling)
                cp.start()
                passed.append(cp)
        for j, (cx, cy) in enumerate(chips):
            for i in range(n_w):
                blk = _half(out_refs[i], 1 - c, axes[i], (2 * cx + cy,))
                _remote(blk, blk, send_sems, recv_sems, 6 * i + 3 + j, sibling).wait_recv()
        for cp in first + passed:
            cp.wait_send()
        for cp in mine:
            cp.wait()

    outs = pl.pallas_call(
        body, name="gather_weights",
        out_shape=[jax.ShapeDtypeStruct((N_CHIPS,) + s.shape, s.dtype) for s in shards]
        + [jax.ShapeDtypeStruct((8, LANES), F32)],
        in_specs=[ANY] * (n_w + 1), out_specs=[ANY] * n_w + [pl.BlockSpec(memory_space=pltpu.VMEM)],
        scratch_shapes=[pltpu.SemaphoreType.DMA((6 * n_w,)), pltpu.SemaphoreType.DMA((6 * n_w,)),
                        pltpu.SemaphoreType.DMA((n_w,))],
    )(*shards, after)
    return outs[:n_w], outs[n_w]


def _pair_exchange(names, grads, call_name):
    n_w = len(grads)
    axes = [HALF_AXIS[n] for n in names]

    def body(*refs):
        g_refs, t_refs = refs[:n_w], refs[n_w:2 * n_w]
        send_sems, recv_sems = refs[2 * n_w:]
        x, y, c = _my_place()
        sends = [_remote(_half(g_refs[i], 1 - c, axes[i]), t_refs[i], send_sems, recv_sems, i, (x, y, 1 - c))
                 for i in range(n_w)]
        for cp in sends:
            cp.start()
        for cp in sends:
            cp.wait_recv()
        for cp in sends:
            cp.wait_send()

    return pl.pallas_call(
        body, name=call_name,
        out_shape=[jax.ShapeDtypeStruct(_half_shape(g.shape, a), g.dtype) for g, a in zip(grads, axes)],
        in_specs=[ANY] * n_w, out_specs=[ANY] * n_w,
        scratch_shapes=[pltpu.SemaphoreType.DMA((n_w,)), pltpu.SemaphoreType.DMA((n_w,))],
    )(*grads)


def _chip_scatter(pair_sums):
    n_w = len(pair_sums)

    def body(*refs):
        s_refs, p_refs = refs[:n_w], refs[n_w:2 * n_w]
        send_sems, recv_sems = refs[2 * n_w:]
        x, y, c = _my_place()
        chips = [(1 - x, y), (x, 1 - y), (1 - x, 1 - y)]
        sends = [_remote(s_refs[i].at[2 * cx + cy], p_refs[i].at[j], send_sems, recv_sems, 3 * i + j, (cx, cy, c))
                 for i in range(n_w) for j, (cx, cy) in enumerate(chips)]
        for cp in sends:
            cp.start()
        for cp in sends:
            cp.wait_recv()
        for cp in sends:
            cp.wait_send()

    return pl.pallas_call(
        body, name="grad_chip_scatter",
        out_shape=[jax.ShapeDtypeStruct((N_CHIPS - 1,) + s.shape[1:], s.dtype) for s in pair_sums],
        in_specs=[ANY] * n_w, out_specs=[ANY] * n_w,
        scratch_shapes=[pltpu.SemaphoreType.DMA((3 * n_w,)), pltpu.SemaphoreType.DMA((3 * n_w,))],
    )(*pair_sums)


def _sibling_join(halves, name, after):
    n_w = len(halves)

    def body(*refs):
        s_refs, j_refs = refs[:n_w], refs[n_w + 1:2 * n_w + 1]
        send_sems, recv_sems = refs[2 * n_w + 1:]
        x, y, c = _my_place()
        sends = [_remote(s_refs[i], j_refs[i], send_sems, recv_sems, i, (x, y, 1 - c)) for i in range(n_w)]
        for cp in sends:
            cp.start()
        for cp in sends:
            cp.wait_recv()
        for cp in sends:
            cp.wait_send()

    return pl.pallas_call(
        body, name=name,
        out_shape=[jax.ShapeDtypeStruct(s.shape, s.dtype) for s in halves],
        in_specs=[ANY] * (n_w + 1), out_specs=[ANY] * n_w,
        scratch_shapes=[pltpu.SemaphoreType.DMA((n_w,)), pltpu.SemaphoreType.DMA((n_w,))],
    )(*halves, after)


HBM_SPEC = pl.BlockSpec(memory_space=pltpu.HBM)
SEM_SPEC = pl.BlockSpec(memory_space=pltpu.SEMAPHORE)
DATAFLOW = pltpu.SideEffectType.DATAFLOW_SIDE_EFFECTING


def _in_hbm(a):
    return pltpu.with_memory_space_constraint(a, pltpu.HBM)


def _exchange_start(name, srcs, lands, plan, n_copies, after):
    n = len(srcs)

    def body(*refs):
        src_refs, land_refs = refs[:n], refs[n:2 * n]
        send_sems, recv_sems = refs[2 * n + 1], refs[2 * n + 2]
        token = refs[-1]
        for k, (src, dst, to) in enumerate(plan(src_refs, land_refs)):
            _remote(src, dst, send_sems, recv_sems, k, to).start()
        token[...] = jnp.zeros_like(token)

    outs = pl.pallas_call(
        body, name=name,
        out_shape=(pltpu.SemaphoreType.DMA((n_copies,)), pltpu.SemaphoreType.DMA((n_copies,)),
                   *[pltpu.HBM(a.shape, a.dtype) for a in srcs], *[pltpu.HBM(a.shape, a.dtype) for a in lands],
                   jax.ShapeDtypeStruct((8, LANES), F32)),
        in_specs=[HBM_SPEC] * (2 * n) + [ANY],
        out_specs=(SEM_SPEC, SEM_SPEC, *[HBM_SPEC] * (2 * n), pl.BlockSpec(memory_space=pltpu.VMEM)),
        input_output_aliases={i: 2 + i for i in range(2 * n)},
        compiler_params=pltpu.CompilerParams(has_side_effects=DATAFLOW),
    )(*[_in_hbm(a) for a in srcs], *[_in_hbm(a) for a in lands], after)
    return outs[0], outs[1], outs[2:2 + n], outs[2 + n:2 + 2 * n], outs[-1]


def _exchange_wait(name, started, plan, after):
    send_sems, recv_sems, srcs, lands, _ = started
    n = len(srcs)

    def body(*refs):
        src_refs, land_refs = refs[:n], refs[n:2 * n]
        s_sems, r_sems = refs[2 * n], refs[2 * n + 1]
        for k, (src, dst, to) in enumerate(plan(src_refs, land_refs)):
            cp = _remote(src, dst, s_sems, r_sems, k, to)
            cp.wait_send()
            cp.wait_recv()

    outs = pl.pallas_call(
        body, name=name,
        out_shape=tuple(pltpu.HBM(a.shape, a.dtype) for a in list(srcs) + list(lands)),
        in_specs=[HBM_SPEC] * (2 * n) + [SEM_SPEC, SEM_SPEC, ANY],
        out_specs=tuple([HBM_SPEC] * (2 * n)),
        input_output_aliases={i: i for i in range(2 * n)},
        compiler_params=pltpu.CompilerParams(has_side_effects=DATAFLOW),
    )(*srcs, *lands, send_sems, recv_sems, after)
    return outs[:n], outs[n:]


def _late_gather_plan(src_refs, land_refs):
    x, y, c = _my_place()
    chips = [(1 - x, y), (x, 1 - y), (1 - x, 1 - y)]
    return [(src, land.at[2 * x + y], (cx, cy, c)) for src, land in zip(src_refs, land_refs) for cx, cy in chips]


def _late_scatter_plan(src_refs, land_refs):
    x, y, c = _my_place()
    chips = [(1 - x, y), (x, 1 - y), (1 - x, 1 - y)]
    return [(src.at[2 * cx + cy], land.at[j], (cx, cy, c))
            for src, land in zip(src_refs, land_refs) for j, (cx, cy) in enumerate(chips)]


def _row_tile(rows, mult=16):
    return max(d for d in range(mult, ROW_TILE + 1, mult) if rows % d == 0)


def _pair_sum(place, g, theirs, axis, name):
    nj, rr, cc = theirs.shape
    tr = _row_tile(rr)
    nb = rr // tr
    if axis == 0:
        g_map = lambda j, i, pr: (j, pr[0] * nb + i, 0)
    else:
        g_map = lambda j, i, pr: (j, i, pr[0])

    def body(pr, g_ref, t_ref, o_ref):
        o_ref[...] = (g_ref[...].astype(F32) + t_ref[...].astype(F32)).astype(BF16)

    spec = pl.BlockSpec((None, tr, cc), lambda j, i, pr: (j, i, 0))
    return pl.pallas_call(
        body, name=name,
        grid_spec=pltpu.PrefetchScalarGridSpec(
            num_scalar_prefetch=1, grid=(nj, nb),
            in_specs=[pl.BlockSpec((None, tr, cc), g_map), spec], out_specs=spec),
        out_shape=jax.ShapeDtypeStruct(theirs.shape, BF16))(place, g, theirs)


def _chip_sum(place, pair_sums, parts, name):
    _, rr, cc = parts.shape
    tr = _row_tile(rr)

    def body(pr, h_ref, p_ref, o_ref):
        acc = p_ref[0].astype(F32)
        for j in range(1, N_CHIPS - 1):
            acc = acc + p_ref[j].astype(F32)
        o_ref[...] = (acc + h_ref[...].astype(F32)).astype(BF16)

    return pl.pallas_call(
        body, name=name,
        grid_spec=pltpu.PrefetchScalarGridSpec(
            num_scalar_prefetch=1, grid=(rr // tr,),
            in_specs=[pl.BlockSpec((None, tr, cc), lambda i, pr: (pr[1], i, 0)),
                      pl.BlockSpec((N_CHIPS - 1, tr, cc), lambda i, pr: (0, i, 0))],
            out_specs=pl.BlockSpec((tr, cc), lambda i, pr: (i, 0))),
        out_shape=jax.ShapeDtypeStruct((rr, cc), BF16))(place, pair_sums, parts)


def _silu(v):
    return v / (1.0 + jnp.exp(-v))


def _ada_fwd(c_all, w_shard, b_shard):
    def body(c_ref, w_ref, b_ref, o_ref):
        o_ref[...] = jnp.dot(_silu(c_ref[...]), w_ref[...], precision=lax.Precision.HIGHEST,
                             preferred_element_type=F32) + b_ref[...]

    return pl.pallas_call(body, name="ada_fwd", out_shape=jax.ShapeDtypeStruct((c_all.shape[0], w_shard.shape[1]), F32),
                          compiler_params=pltpu.CompilerParams(vmem_limit_bytes=MM_VMEM_LIMIT))(c_all, w_shard, b_shard)


def _ada_bwd(c_all, dmod_cols):
    def body(c_ref, d_ref, o_ref):
        o_ref[...] = lax.dot_general(_silu(c_ref[...]), d_ref[...], (((0,), (0,)), ((), ())),
                                     precision=lax.Precision.HIGHEST, preferred_element_type=F32)

    return pl.pallas_call(body, name="ada_bwd", out_shape=jax.ShapeDtypeStruct((c_all.shape[1], dmod_cols.shape[1]), F32),
                          compiler_params=pltpu.CompilerParams(vmem_limit_bytes=MM_VMEM_LIMIT))(c_all, dmod_cols)


def _adamw_math(w, g, m, v):
    m = ADAM_B1 * m + (1.0 - ADAM_B1) * g
    v = ADAM_B2 * v + (1.0 - ADAM_B2) * (g * g)
    m_hat = m / (1.0 - ADAM_B1 ** ADAM_STEP)
    v_hat = v / (1.0 - ADAM_B2 ** ADAM_STEP)
    delta = -ADAM_LR * (m_hat / (jnp.sqrt(v_hat) + ADAM_EPS) + ADAM_WD * w)
    return delta, m, v


def _adamw(w, g, m, v, name):
    r, ccols = w.shape
    tr = max(d for d in range(8, ROW_TILE + 1, 8) if r % d == 0)
    spec = pl.BlockSpec((tr, ccols), lambda i: (i, 0))

    def body(w_ref, g_ref, m_ref, v_ref, d_ref, nm_ref, nv_ref):
        d_ref[...], nm_ref[...], nv_ref[...] = _adamw_math(w_ref[...], g_ref[...], m_ref[...], v_ref[...])

    return pl.pallas_call(body, name=name, grid=(r // tr,), in_specs=[spec] * 4, out_specs=[spec] * 3,
                          out_shape=[jax.ShapeDtypeStruct(w.shape, F32)] * 3,
                          compiler_params=pltpu.CompilerParams(vmem_limit_bytes=MM_VMEM_LIMIT))(w, g, m, v)


def _adamw_small(w, g_all, m, v):
    def body(w_ref, g_ref, m_ref, v_ref, gs_ref, d_ref, nm_ref, nv_ref):
        g = g_ref[0]
        for d in range(1, N_DEV):
            g = g + g_ref[d]
        gs_ref[...] = g
        d_ref[...], nm_ref[...], nv_ref[...] = _adamw_math(w_ref[...], g, m_ref[...], v_ref[...])

    return pl.pallas_call(body, name="adamw_small", out_shape=[jax.ShapeDtypeStruct(w.shape, F32)] * 4)(w, g_all, m, v)


def _adamw_halves(place, w, own, sib, m, v, axis, name):
    r, cc = w.shape
    if axis == 0:
        rows, gc = own.shape[0], own.shape[1]
        tr = _row_tile(rows)
        nb = rows // tr
        w_spec = pl.BlockSpec((tr, cc), lambda h, i, pr: (h * nb + i, 0))
        g_spec = pl.BlockSpec((tr, gc), lambda h, i, pr: (i, 0))
    else:
        tr = _row_tile(r)
        nb = r // tr
        gc = own.shape[1]
        w_spec = pl.BlockSpec((tr, gc), lambda h, i, pr: (i, h))
        g_spec = pl.BlockSpec((tr, gc), lambda h, i, pr: (i, 0))
    wc = w_spec.block_shape[1]

    def body(pr, w_ref, o_ref, s_ref, m_ref, v_ref, g_ref, d_ref, nm_ref, nv_ref):
        g = jnp.where(pl.program_id(0) == pr[0], o_ref[...], s_ref[...]).astype(F32)[:, :wc]
        g_ref[...] = g
        d_ref[...], nm_ref[...], nv_ref[...] = _adamw_math(w_ref[...], g, m_ref[...], v_ref[...])

    return pl.pallas_call(
        body, name=name,
        grid_spec=pltpu.PrefetchScalarGridSpec(
            num_scalar_prefetch=1, grid=(2, nb),
            in_specs=[w_spec, g_spec, g_spec, w_spec, w_spec], out_specs=[w_spec] * 4),
        out_shape=[jax.ShapeDtypeStruct(w.shape, F32)] * 4,
        compiler_params=pltpu.CompilerParams(vmem_limit_bytes=MM_VMEM_LIMIT))(place, w, own, sib, m, v)


SMALL = ("b_ada", "norm_attn", "norm_ffn", "q_a_norm", "kv_a_norm", "q_norm", "k_nope_norm", "k_rope_norm",
         "out_norm_sb", "out_norm_mla")
WEIGHTS = ("w_ada", "b_ada", "norm_attn", "norm_ffn", "w_in", "q_a_norm", "w_q_up", "kv_a_norm", "w_kv_up",
           "q_norm", "k_nope_norm", "k_rope_norm", "out_norm_sb", "out_norm_mla", "w_out", "w_gate", "w_up",
           "w_down")


def kernel(x, c, positions, w_ada, b_ada, norm_attn, norm_ffn, w_in, q_a_norm, w_q_up, kv_a_norm, w_kv_up, q_norm, k_nope_norm, k_rope_norm, out_norm_sb, out_norm_mla, w_out, w_gate, w_up, w_down, loss_target, m_w_ada, m_b_ada, m_norm_attn, m_norm_ffn, m_w_in, m_q_a_norm, m_w_q_up, m_kv_a_norm, m_w_kv_up, m_q_norm, m_k_nope_norm, m_k_rope_norm, m_out_norm_sb, m_out_norm_mla, m_w_out, m_w_gate, m_w_up, m_w_down, v_w_ada, v_b_ada, v_norm_attn, v_norm_ffn, v_w_in, v_q_a_norm, v_w_q_up, v_kv_a_norm, v_w_kv_up, v_q_norm, v_k_nope_norm, v_k_rope_norm, v_out_norm_sb, v_out_norm_mla, v_w_out, v_w_gate, v_w_up, v_w_down):
    local = dict(locals())
    w = {n: local[n][0] for n in WEIGHTS}
    m = {n: local["m_" + n][0] for n in WEIGHTS}
    v = {n: local["v_" + n][0] for n in WEIGHTS}
    small = {n: w[n].reshape(1, -1) for n in SMALL}
    ix, iy, ic = _my_place()
    chip = 2 * ix + iy
    dev = 2 * chip + ic
    xs, target = x[0], loss_target[0]
    seq = xs.shape[0]

    c_all = _all_gather_small(c.reshape(8, LANES), "gather_c").reshape(N_DEV, D_MODEL)
    ada_cols = w["w_ada"].shape[1]
    b_cols = lax.dynamic_slice_in_dim(small["b_ada"], chip * ada_cols, ada_cols, axis=1)
    mod_cols = _ada_fwd(c_all, w["w_ada"], b_cols)
    mod_all = _all_gather_small(mod_cols, "gather_mod").reshape(N_CHIPS, 2, N_DEV, ada_cols)
    mod = lax.dynamic_index_in_dim(mod_all[:, 0], dev, axis=1, keepdims=False).reshape(1, N_MOD * D_MODEL)

    ff_pad = FF_SHARD_PAD - FF_SHARD
    pads = {"w_gate": ((0, 0), (0, ff_pad)), "w_up": ((0, 0), (0, ff_pad)), "w_down": ((0, ff_pad), (0, 0))}
    shards = {n: jnp.pad(w[n].astype(BF16), pads[n]) if n in pads else w[n].astype(BF16) for n in BIG}
    early, early_done = _gather_weights(EARLY, [shards[n] for n in EARLY], mod)
    gathered = dict(zip(EARLY, early))
    lands = [lax.dynamic_update_index_in_dim(lax.empty((N_CHIPS,) + shards[n].shape, BF16), shards[n], chip, 0)
             for n in LATE]
    late_gather = _exchange_start("gather_late_start", [shards[n] for n in LATE], lands, _late_gather_plan,
                                  3 * len(LATE), early_done)

    half = MLA_ROPE // 2
    freqs = 1.0 / (ROPE_THETA ** (np.arange(half, dtype=np.float32) / half))
    zeros = np.zeros(LANES - MLA_ROPE, np.float32)
    freqs_row = jnp.asarray(np.concatenate([freqs, freqs, zeros]).astype(np.float32)[None])
    sign_row = jnp.asarray(np.concatenate([-np.ones(half), np.ones(half), zeros]).astype(np.float32)[None])
    cos, sin = _rope_tables(positions.reshape(seq, 1), freqs_row, sign_row)

    place = jnp.stack([ic, chip]).astype(jnp.int32)
    small_params = {n: small[n] for n in SMALL if n != "b_ada"}
    mod = mod + late_gather[4][0, 0]

    def pair_sums_of(names, grads, call_name):
        theirs = _pair_exchange(names, grads, call_name)
        return [_pair_sum(place, gr, th, HALF_AXIS[n], "grad_pair_sum_" + n) for n, gr, th in zip(names, grads, theirs)]

    p1 = {**{n: gathered[n] for n in EARLY}, **small_params}
    mixed, mixing_vjp = jax.vjp(lambda x_, mod_, p_: _mixing_stage(x_, mod_, p_, cos, sin), xs, mod, p1)
    _, landed = _exchange_wait("gather_late_wait", late_gather, _late_gather_plan, mixed)
    p2 = {**dict(zip(LATE, landed)), **small_params}
    loss_part, ffn_vjp = jax.vjp(lambda x_, mixed_, mod_, p_: _ffn_stage(x_, mixed_, mod_, p_, target), xs, mixed, mod, p2)
    gx2, gmixed, gmod2, gp2 = ffn_vjp(jnp.ones((), F32))
    late_sums = pair_sums_of(LATE, [gp2[n] for n in LATE], "grad_pair_exchange_late")
    late_scatter = _exchange_start(
        "grad_scatter_late_start", late_sums,
        [lax.empty((N_CHIPS - 1,) + s.shape[1:], BF16) for s in late_sums], _late_scatter_plan, 3 * len(LATE), gx2)
    gx1, gmod1, gp1 = mixing_vjp(gmixed + late_scatter[4][0, 0])
    gx = gx1 + gx2
    gmod = gmod1 + gmod2
    gp = {n: gp1[n] + gp2[n] for n in small_params}
    loss = lax.psum(loss_part, ("x", "y", "c"))

    small_names = [n for n in SMALL if n != "b_ada"]
    small_vec = jnp.concatenate([gmod] + [gp[n] for n in small_names], axis=1)
    n_small = small_vec.shape[1]
    small_all = _all_gather_small(small_vec.reshape(8, n_small // 8), "gather_small").reshape(N_DEV, 8, n_small // 8)

    early_sums = pair_sums_of(EARLY, [gp1[n] for n in EARLY], "grad_pair_exchange_early")
    early_scatter = _exchange_start(
        "grad_scatter_early_start", early_sums,
        [lax.empty((N_CHIPS - 1,) + s.shape[1:], BF16) for s in early_sums], _late_scatter_plan, 3 * len(EARLY),
        small_all)
    late_sums, late_parts = _exchange_wait("grad_scatter_late_wait", late_scatter, _late_scatter_plan, gx)
    own_late = [_chip_sum(place, ps, pt, "grad_chip_sum_" + n) for n, ps, pt in zip(LATE, late_sums, late_parts)]
    sib_late = _sibling_join(own_late, "grad_sibling_join_late", early_scatter[4])
    g, delta, new_m, new_v = {}, {}, {}, {}
    for n, o, s in zip(LATE, own_late, sib_late):
        g[n], delta[n], new_m[n], new_v[n] = _adamw_halves(place, w[n], o, s, m[n], v[n], HALF_AXIS[n], "adamw_" + n)

    def pack_small(d):
        return jnp.concatenate([d[n].reshape(1, -1) for n in SMALL], axis=1).reshape(8, n_small // 8)

    gs, ds, ms, vs = _adamw_small(pack_small(w), small_all, pack_small(m), pack_small(v))
    sizes = [w[n].size for n in SMALL]
    offs = np.concatenate([[0], np.cumsum(sizes)])

    def unpack_small(a):
        flat = a.reshape(-1)
        return {n: flat[offs[i]:offs[i + 1]].reshape(w[n].shape) for i, n in enumerate(SMALL)}

    for d, packed in zip((g, delta, new_m, new_v), (gs, ds, ms, vs)):
        d.update(unpack_small(packed))

    dmod_all = small_all.reshape(N_DEV, n_small)[:, :N_MOD * D_MODEL]
    g["w_ada"] = _ada_bwd(c_all, lax.dynamic_slice_in_dim(dmod_all, chip * ada_cols, ada_cols, axis=1))
    delta["w_ada"], new_m["w_ada"], new_v["w_ada"] = _adamw(w["w_ada"], g["w_ada"], m["w_ada"], v["w_ada"], "adamw_w_ada")

    early_sums, early_parts = _exchange_wait("grad_scatter_early_wait", early_scatter, _late_scatter_plan,
                                             delta["w_ada"])
    own_early = [_chip_sum(place, ps, pt, "grad_chip_sum_" + n) for n, ps, pt in zip(EARLY, early_sums, early_parts)]
    sib_early = _sibling_join(own_early, "grad_sibling_join_early", delta["w_ada"])
    for n, o, s in zip(EARLY, own_early, sib_early):
        g[n], delta[n], new_m[n], new_v[n] = _adamw_halves(place, w[n], o, s, m[n], v[n], HALF_AXIS[n], "adamw_" + n)

    def outs(d):
        return [d[n][None] for n in WEIGHTS]

    return (loss, gx[None], *outs(g), *outs(delta), *outs(new_m), *outs(new_v))
```

```python
import functools
import math

import numpy as np
import jax
import jax.numpy as jnp
from jax import lax
from jax.experimental import pallas as pl
from jax.experimental.pallas import tpu as pltpu

F32 = jnp.float32
BF16 = jnp.bfloat16
MESH = pl.DeviceIdType.MESH
ANY = pl.BlockSpec(memory_space=pl.ANY)

D_MODEL = 1024
SB_HEADS = 8
SB_HEAD_DIM = 64
SB_WIDTH = 512
MLA_HEADS = 4
MLA_NOPE = 128
MLA_ROPE = 64
MLA_QK = 192
MLA_V = 128
MLA_Q_RANK = 384
MLA_KV_RANK = 256
D_FF = 2816
N_MOD = 6
ROPE_THETA = 10000.0
EPS = 1e-6
LANES = 128

ADAM_LR = 0.001
ADAM_B1 = 0.9
ADAM_B2 = 0.999
ADAM_EPS = 1e-08
ADAM_WD = 0.01
ADAM_STEP = 10

N_CHIPS = 4
N_DEV = 8
ROW_TILE = 256
ATT_BLK = 256
MM_VMEM_LIMIT = 48 * 1024 * 1024
FF_SHARD = D_FF // N_CHIPS
FF_SHARD_PAD = 768


def _mm(a, b, mode, name, tm, tn, out_dtype=F32):
    if mode == "nn":
        (m, k), n = a.shape, b.shape[1]
        a_spec = pl.BlockSpec((tm, k), lambda j, i: (i, 0))
        b_spec = pl.BlockSpec((k, tn), lambda j, i: (0, j))
        dims = (((1,), (0,)), ((), ()))
    elif mode == "nt":
        (m, k), n = a.shape, b.shape[0]
        a_spec = pl.BlockSpec((tm, k), lambda j, i: (i, 0))
        b_spec = pl.BlockSpec((tn, k), lambda j, i: (j, 0))
        dims = (((1,), (1,)), ((), ()))
    else:
        (k, m), n = a.shape, b.shape[1]
        a_spec = pl.BlockSpec((k, tm), lambda j, i: (0, i))
        b_spec = pl.BlockSpec((k, tn), lambda j, i: (0, j))
        dims = (((0,), (0,)), ((), ()))
    assert m % tm == 0 and n % tn == 0, (name, m, n, tm, tn)

    def body(a_ref, b_ref, o_ref):
        o_ref[...] = lax.dot_general(a_ref[...].astype(BF16), b_ref[...].astype(BF16), dims,
                                     preferred_element_type=F32).astype(out_dtype)

    return pl.pallas_call(
        body, name=name, grid=(n // tn, m // tm),
        in_specs=[a_spec, b_spec],
        out_specs=pl.BlockSpec((tm, tn), lambda j, i: (i, j)),
        out_shape=jax.ShapeDtypeStruct((m, n), out_dtype),
        compiler_params=pltpu.CompilerParams(dimension_semantics=("arbitrary", "arbitrary"),
                                             vmem_limit_bytes=MM_VMEM_LIMIT),
    )(a, b)


def _make_linear(name, tk_w, tn_w):
    @jax.custom_vjp
    def op(a, w):
        return _mm(a, w, "nn", name + "_fwd", ROW_TILE, w.shape[1])

    def fwd(a, w):
        return op(a, w), (a, w)

    def bwd(res, dy):
        a, w = res
        da = _mm(dy, w, "nt", name + "_dx", ROW_TILE, w.shape[0])
        dw = _mm(a, dy, "tn", name + "_dw", tk_w, tn_w, out_dtype=BF16)
        return da, dw

    op.defvjp(fwd, bwd)
    return op


def _make_linear_sharded(name, tk_w):
    def call_fwd(a, w):
        t, k = a.shape
        n_sh, _, cc = w.shape

        def body(a_ref, w_ref, o_ref):
            o_ref[...] = jnp.dot(a_ref[...].astype(BF16), w_ref[...], preferred_element_type=F32)

        return pl.pallas_call(
            body, name=name + "_fwd", grid=(n_sh, t // ROW_TILE),
            in_specs=[pl.BlockSpec((ROW_TILE, k), lambda j, i: (i, 0)),
                      pl.BlockSpec((None, k, cc), lambda j, i: (j, 0, 0))],
            out_specs=pl.BlockSpec((ROW_TILE, cc), lambda j, i: (i, j)),
            out_shape=jax.ShapeDtypeStruct((t, n_sh * cc), F32),
            compiler_params=pltpu.CompilerParams(dimension_semantics=("arbitrary", "arbitrary"),
                                                 vmem_limit_bytes=MM_VMEM_LIMIT),
        )(a, w)

    def call_dx(dy, w):
        t = dy.shape[0]
        n_sh, k, cc = w.shape

        def body(dy_ref, w_ref, o_ref):
            acc = jnp.zeros((ROW_TILE, k), F32)
            for j in range(n_sh):
                acc = acc + _nt(dy_ref[:, j * cc:(j + 1) * cc].astype(BF16), w_ref[j])
            o_ref[...] = acc

        return pl.pallas_call(
            body, name=name + "_dx", grid=(t // ROW_TILE,),
            in_specs=[pl.BlockSpec((ROW_TILE, n_sh * cc), lambda i: (i, 0)),
                      pl.BlockSpec((n_sh, k, cc), lambda i: (0, 0, 0))],
            out_specs=pl.BlockSpec((ROW_TILE, k), lambda i: (i, 0)),
            out_shape=jax.ShapeDtypeStruct((t, k), F32),
            compiler_params=pltpu.CompilerParams(dimension_semantics=("arbitrary",),
                                                 vmem_limit_bytes=MM_VMEM_LIMIT),
        )(dy, w)

    def call_dw(a, dy, w):
        t, k = a.shape
        n_sh, _, cc = w.shape

        def body(a_ref, dy_ref, o_ref):
            o_ref[...] = _tn(a_ref[...].astype(BF16), dy_ref[...].astype(BF16)).astype(BF16)

        return pl.pallas_call(
            body, name=name + "_dw", grid=(n_sh, k // tk_w),
            in_specs=[pl.BlockSpec((t, tk_w), lambda j, i: (0, i)),
                      pl.BlockSpec((t, cc), lambda j, i: (0, j))],
            out_specs=pl.BlockSpec((None, tk_w, cc), lambda j, i: (j, i, 0)),
            out_shape=jax.ShapeDtypeStruct(w.shape, BF16),
            compiler_params=pltpu.CompilerParams(dimension_semantics=("arbitrary", "arbitrary"),
                                                 vmem_limit_bytes=MM_VMEM_LIMIT),
        )(a, dy)

    @jax.custom_vjp
    def op(a, w):
        return call_fwd(a, w)

    def fwd(a, w):
        return op(a, w), (a, w)

    def bwd(res, dy):
        a, w = res
        return call_dx(dy, w), call_dw(a, dy, w)

    op.defvjp(fwd, bwd)
    return op


def _row_spec(arr, tb):
    return pl.BlockSpec((tb, arr.shape[1]), lambda i: (i, 0))


def _full_spec(arr):
    return pl.BlockSpec(arr.shape, lambda i: (0, 0))


def _make_rowwise(name, f, n_rows, n_params, out_cols, diff_rows):
    n_out = len(out_cols)

    def call_fwd(rows, params):
        t = rows[0].shape[0]

        def body(*refs):
            ins = [r[...] for r in refs[:n_rows + n_params]]
            outs = f(*ins)
            for o_ref, o in zip(refs[n_rows + n_params:], outs):
                o_ref[...] = o

        return pl.pallas_call(
            body, name=name + "_fwd", grid=(t // ROW_TILE,),
            in_specs=[_row_spec(a, ROW_TILE) for a in rows] + [_full_spec(p) for p in params],
            out_specs=[pl.BlockSpec((ROW_TILE, n), lambda i: (i, 0)) for n in out_cols],
            out_shape=[jax.ShapeDtypeStruct((t, n), F32) for n in out_cols],
            compiler_params=pltpu.CompilerParams(dimension_semantics=("arbitrary",),
                                                 vmem_limit_bytes=MM_VMEM_LIMIT),
        )(*rows, *params)

    def call_bwd(rows, params, cts):
        t = rows[0].shape[0]
        d_rows = [a for a, d in zip(rows, diff_rows) if d]
        n_in = n_rows + n_params + n_out

        def body(*refs):
            ins = [r[...] for r in refs[:n_rows + n_params]]
            ct = tuple(r[...] for r in refs[n_rows + n_params:n_in])
            _, vjp = jax.vjp(f, *ins)
            grads = vjp(ct)
            out_refs = refs[n_in:]
            g_rows = [g for g, d in zip(grads[:n_rows], diff_rows) if d]
            for o_ref, g in zip(out_refs[:len(g_rows)], g_rows):
                o_ref[...] = g
            p_refs = out_refs[len(g_rows):]

            if p_refs:
                @pl.when(pl.program_id(0) == 0)
                def _():
                    for p_ref in p_refs:
                        p_ref[...] = jnp.zeros_like(p_ref)

                for p_ref, g in zip(p_refs, grads[n_rows:]):
                    p_ref[...] += g

        return pl.pallas_call(
            body, name=name + "_bwd", grid=(t // ROW_TILE,),
            in_specs=[_row_spec(a, ROW_TILE) for a in rows] + [_full_spec(p) for p in params]
            + [_row_spec(c, ROW_TILE) for c in cts],
            out_specs=[_row_spec(a, ROW_TILE) for a in d_rows] + [_full_spec(p) for p in params],
            out_shape=[jax.ShapeDtypeStruct(a.shape, F32) for a in d_rows]
            + [jax.ShapeDtypeStruct(p.shape, F32) for p in params],
            compiler_params=pltpu.CompilerParams(dimension_semantics=("arbitrary",),
                                                 vmem_limit_bytes=MM_VMEM_LIMIT),
        )(*rows, *params, *cts)

    @jax.custom_vjp
    def op(*args):
        return tuple(call_fwd(args[:n_rows], args[n_rows:]))

    def fwd(*args):
        return op(*args), args

    def bwd(args, cts):
        rows, params = args[:n_rows], args[n_rows:]
        outs = call_bwd(rows, params, cts)
        it = iter(outs)
        g_rows = [next(it) if d else jnp.zeros_like(a) for a, d in zip(rows, diff_rows)]
        return tuple(g_rows) + tuple(it)

    op.defvjp(fwd, bwd)
    return op


def _rms(x, g, n):
    return x * lax.rsqrt(jnp.sum(x * x, axis=-1, keepdims=True) * (1.0 / n) + EPS) * g


def _f_pre_attn(x, g, scale, shift):
    return (_rms(x, g, D_MODEL) * (1.0 + scale) + shift,)


def _f_mla_a(cq, ckv, gq, gkv):
    return _rms(cq, gq, MLA_Q_RANK), _rms(ckv, gkv, MLA_KV_RANK)


@jax.custom_vjp
def _split_lanes(x):
    return tuple(x[:, i * LANES:(i + 1) * LANES] for i in range(x.shape[1] // LANES))


def _split_lanes_fwd(x):
    return _split_lanes(x), None


def _split_lanes_bwd(_, cts):
    return (jnp.concatenate(cts, axis=1),)


_split_lanes.defvjp(_split_lanes_fwd, _split_lanes_bwd)


def _f_mla_b(qall, kn_all, kr, kr_sw, cos, sin, gqn, gqr, gqr_sw, gkn, gkr, gkr_sw):
    q = _split_lanes(qall)
    kn = _split_lanes(kn_all)
    qn_o, qr_o, kn_o = [], [], []
    for h in range(MLA_HEADS):
        qn, qr, qs = q[h], q[MLA_HEADS + h], q[2 * MLA_HEADS + h]
        ss = jnp.sum(qn * qn, axis=-1, keepdims=True) + jnp.sum(qr * qr, axis=-1, keepdims=True)
        rs = lax.rsqrt(ss * (1.0 / MLA_QK) + EPS)
        qn_o.append(qn * rs * gqn)
        qr_o.append((qr * rs * gqr) * cos + (qs * rs * gqr_sw) * sin)
        kn_o.append(_rms(kn[h], gkn, MLA_NOPE))
    rs = lax.rsqrt(jnp.sum(kr * kr, axis=-1, keepdims=True) * (1.0 / MLA_ROPE) + EPS)
    kr_o = (kr * rs * gkr) * cos + (kr_sw * rs * gkr_sw) * sin
    return (jnp.concatenate(qn_o, axis=1), jnp.concatenate(qr_o, axis=1), jnp.concatenate(kn_o, axis=1), kr_o)


def _f_post_attn(o_sb, o_mla, g_sb, g_mla):
    return (jnp.concatenate([_rms(o_sb, g_sb, SB_WIDTH), _rms(o_mla, g_mla, SB_WIDTH)], axis=1),)


def _f_pre_ffn(x, attn, gate, g, scale, shift):
    x2 = x + gate * attn
    return x2, _rms(x2, g, D_MODEL) * (1.0 + scale) + shift


def _f_swiglu(gt, up):
    return (gt / (1.0 + jnp.exp(-gt)) * up,)


def _f_loss(x2, ffn, target, gate):
    err = x2 + gate * ffn - target
    return (jnp.sum(err * err, axis=-1, keepdims=True) * (1.0 / D_MODEL),)


def _rope_tables(pos_col, freqs, sign):
    t = pos_col.shape[0]

    def body(p_ref, f_ref, s_ref, cos_ref, sin_ref):
        ang = p_ref[...].astype(F32) * f_ref[...]
        live = jnp.abs(s_ref[...])
        cos_ref[...] = jnp.cos(ang) * live
        sin_ref[...] = jnp.sin(ang) * s_ref[...]

    return pl.pallas_call(
        body, name="rope_tables", grid=(t // ROW_TILE,),
        in_specs=[pl.BlockSpec((ROW_TILE, 1), lambda i: (i, 0)), _full_spec(freqs), _full_spec(sign)],
        out_specs=[pl.BlockSpec((ROW_TILE, LANES), lambda i: (i, 0))] * 2,
        out_shape=[jax.ShapeDtypeStruct((t, LANES), F32)] * 2,
    )(pos_col, freqs, sign)


def _hi_lo_dot(x, tri):
    hi = x.astype(BF16)
    lo = (x - hi.astype(F32)).astype(BF16)
    return (jnp.dot(hi, tri, preferred_element_type=F32) + jnp.dot(lo, tri, preferred_element_type=F32))


def _tri(cmp):
    r = lax.broadcasted_iota(jnp.int32, (ATT_BLK, ATT_BLK), 0)
    c = lax.broadcasted_iota(jnp.int32, (ATT_BLK, ATT_BLK), 1)
    return cmp(r, c).astype(BF16)


def _nt(a, b):
    return lax.dot_general(a, b, (((1,), (1,)), ((), ())), preferred_element_type=F32)


def _tn(a, b):
    return lax.dot_general(a, b, (((0,), (0,)), ((), ())), preferred_element_type=F32)


def _sb_logs(z):
    lb = jnp.minimum(z, 0.0) - jnp.log(1.0 + jnp.exp(-jnp.abs(z)))
    return lb, lb - z


def _sb_fwd(q, k, v):
    t = q.shape[0]
    nq = t // ATT_BLK
    scale = SB_HEAD_DIM ** -0.5

    def body(q_ref, k_ref, v_ref, o_ref, tot_ref):
        qi = pl.program_id(1)
        lane = lax.broadcasted_iota(jnp.int32, (ATT_BLK, LANES), 1)
        tri = _tri(lambda r, c: r > c)
        qv = q_ref[...] * scale
        heads = [(lane // SB_HEAD_DIM) == hh for hh in range(2)]
        qms = [jnp.where(mine, qv, 0.0).astype(BF16) for mine in heads]

        def blocks(kbs, carry, diagonal):
            acc = carry[0]
            nb = len(kbs)
            chains = [(b, hh) for b in range(nb) for hh in range(2)]
            offs = [pl.multiple_of(kb * ATT_BLK, ATT_BLK) for kb in kbs]
            kks = [k_ref[pl.ds(off, ATT_BLK), :].astype(BF16) for off in offs]
            v_blks = [v_ref[pl.ds(off, ATT_BLK), :] for off in offs]
            if diagonal:
                valid = (lax.broadcasted_iota(jnp.int32, (ATT_BLK, ATT_BLK), 1)
                         < lax.broadcasted_iota(jnp.int32, (ATT_BLK, ATT_BLK), 0))
            zs = {ch: _nt(qms[ch[1]], kks[ch[0]]) for ch in chains}
            vvs = {(b, hh): jnp.where(heads[hh], v_blks[b], 0.0).astype(BF16) for b, hh in chains}
            logs = {ch: _sb_logs(zs[ch]) for ch in chains}
            l1ms = {ch: jnp.where(valid, logs[ch][1], 0.0) if diagonal else logs[ch][1] for ch in chains}
            run = {(0, hh): carry[1 + hh] for hh in range(2)}
            for b, hh in chains:
                run[(b + 1, hh)] = run[(b, hh)] + jnp.sum(l1ms[(b, hh)], axis=-1, keepdims=True)
            afters = {ch: _hi_lo_dot(l1ms[ch], tri) for ch in chains}
            ws = {ch: jnp.exp(logs[ch][0] + (afters[ch] + run[ch])) for ch in chains}
            if diagonal:
                ws = {ch: jnp.where(valid, ws[ch], 0.0) for ch in chains}
            for ch in chains:
                acc = acc + jnp.dot(ws[ch].astype(BF16), vvs[ch], preferred_element_type=F32)
            return (acc, run[(nb, 0)], run[(nb, 1)])

        zero = jnp.zeros((ATT_BLK, 1), F32)
        carry = blocks([qi], (jnp.zeros((ATT_BLK, LANES), F32), zero, zero), True)
        carry = lax.fori_loop(0, qi // 2, lambda pr, cr: blocks([qi - 1 - 2 * pr, qi - 2 - 2 * pr], cr, False), carry)
        carry = lax.cond(qi % 2 == 1, lambda cr: blocks([qi * 0], cr, False), lambda cr: cr, carry)
        o_ref[...] = carry[0]
        for hh in range(2):
            tot_ref[:, hh * LANES:(hh + 1) * LANES] = jnp.broadcast_to(carry[1 + hh], (ATT_BLK, LANES))

    return pl.pallas_call(
        body, name="sb_attn_fwd", grid=(SB_HEADS // 2, nq),
        in_specs=[pl.BlockSpec((ATT_BLK, LANES), lambda p, i: (i, p)),
                  pl.BlockSpec((t, LANES), lambda p, i: (0, p)),
                  pl.BlockSpec((t, LANES), lambda p, i: (0, p))],
        out_specs=[pl.BlockSpec((ATT_BLK, LANES), lambda p, i: (i, p)),
                   pl.BlockSpec((ATT_BLK, 2 * LANES), lambda p, i: (i, p))],
        out_shape=[jax.ShapeDtypeStruct((t, SB_WIDTH), F32), jax.ShapeDtypeStruct((t, SB_HEADS * LANES), F32)],
        compiler_params=pltpu.CompilerParams(dimension_semantics=("arbitrary", "arbitrary")),
    )(q, k, v)


def _sb_bwd(q, k, v, tot, do):
    t = q.shape[0]
    nq = t // ATT_BLK
    scale = SB_HEAD_DIM ** -0.5

    def body(q_ref, k_ref, v_ref, tot_ref, do_ref, dq_ref, dk_ref, dv_ref):
        qi = pl.program_id(1)

        @pl.when(qi == 0)
        def _():
            dk_ref[...] = jnp.zeros_like(dk_ref)
            dv_ref[...] = jnp.zeros_like(dv_ref)

        lane = lax.broadcasted_iota(jnp.int32, (ATT_BLK, LANES), 1)
        tri_incl = _tri(lambda r, c: r <= c)
        tri_lt = _tri(lambda r, c: r < c)
        qv = q_ref[...] * scale
        dov = do_ref[...]
        heads = [(lane // SB_HEAD_DIM) == hh for hh in range(2)]
        qms = [jnp.where(mine, qv, 0.0).astype(BF16) for mine in heads]
        doms = [jnp.where(mine, dov, 0.0).astype(BF16) for mine in heads]
        tots = [tot_ref[:, hh * LANES:hh * LANES + 1] for hh in range(2)]

        def blocks(kbs, carry, diagonal):
            dq = carry[0]
            nb = len(kbs)
            chains = [(b, hh) for b in range(nb) for hh in range(2)]
            offs = [pl.multiple_of(kb * ATT_BLK, ATT_BLK) for kb in kbs]
            k_blks = [k_ref[pl.ds(off, ATT_BLK), :] for off in offs]
            vvs = [v_ref[pl.ds(off, ATT_BLK), :].astype(BF16) for off in offs]
            if diagonal:
                valid = (lax.broadcasted_iota(jnp.int32, (ATT_BLK, ATT_BLK), 1)
                         < lax.broadcasted_iota(jnp.int32, (ATT_BLK, ATT_BLK), 0))
            kks = {(b, hh): jnp.where(heads[hh], k_blks[b], 0.0).astype(BF16) for b, hh in chains}
            zs = {ch: _nt(qms[ch[1]], kks[ch]) for ch in chains}
            dws = {ch: _nt(doms[ch[1]], vvs[ch[0]]) for ch in chains}
            logs = {ch: _sb_logs(zs[ch]) for ch in chains}
            lbs = {ch: logs[ch][0] for ch in chains}
            l1m_all = {ch: logs[ch][1] for ch in chains}
            l1ms = {ch: jnp.where(valid, l1m_all[ch], 0.0) for ch in chains} if diagonal else l1m_all
            pre, c_de = {}, {}
            for hh in range(2):
                pre[(0, hh)], c_de[(0, hh)] = carry[1 + 2 * hh], carry[2 + 2 * hh]
            for b, hh in chains:
                pre[(b + 1, hh)] = pre[(b, hh)] + jnp.sum(l1ms[(b, hh)], axis=-1, keepdims=True)
            prefix = {ch: _hi_lo_dot(l1ms[ch], tri_incl) for ch in chains}
            ws = {ch: jnp.exp(lbs[ch] + (tots[ch[1]] - (prefix[ch] + pre[ch]))) for ch in chains}
            if diagonal:
                ws = {ch: jnp.where(valid, ws[ch], 0.0) for ch in chains}
            d_es = {ch: ws[ch] * dws[ch] for ch in chains}
            for b, hh in chains:
                c_de[(b + 1, hh)] = c_de[(b, hh)] + jnp.sum(d_es[(b, hh)], axis=-1, keepdims=True)
            dvs = [_tn(ws[(b, 0)].astype(BF16), doms[0]) + _tn(ws[(b, 1)].astype(BF16), doms[1]) for b in range(nb)]
            dl1ms = {ch: _hi_lo_dot(d_es[ch], tri_lt) + c_de[ch] for ch in chains}
            dzs = {ch: d_es[ch] * jnp.exp(l1m_all[ch]) - dl1ms[ch] * jnp.exp(lbs[ch]) for ch in chains}
            if diagonal:
                dzs = {ch: jnp.where(valid, dzs[ch], 0.0) for ch in chains}
            dzs = {ch: dzs[ch].astype(BF16) for ch in chains}
            for ch in chains:
                dq = dq + jnp.dot(dzs[ch], kks[ch], preferred_element_type=F32)
            for b in range(nb):
                dk_ref[pl.ds(offs[b], ATT_BLK), :] += _tn(dzs[(b, 0)], qms[0]) + _tn(dzs[(b, 1)], qms[1])
                dv_ref[pl.ds(offs[b], ATT_BLK), :] += dvs[b]
            return (dq, pre[(nb, 0)], c_de[(nb, 0)], pre[(nb, 1)], c_de[(nb, 1)])

        zero = jnp.zeros((ATT_BLK, 1), F32)
        carry = lax.fori_loop(0, qi // 2, lambda pr, cr: blocks([2 * pr, 2 * pr + 1], cr, False),
                              (jnp.zeros((ATT_BLK, LANES), F32), zero, zero, zero, zero))
        carry = lax.cond(qi % 2 == 1, lambda cr: blocks([qi - 1], cr, False), lambda cr: cr, carry)
        carry = blocks([qi], carry, True)
        dq_ref[...] = carry[0] * scale

    return pl.pallas_call(
        body, name="sb_attn_bwd", grid=(SB_HEADS // 2, nq),
        in_specs=[pl.BlockSpec((ATT_BLK, LANES), lambda p, i: (i, p)),
                  pl.BlockSpec((t, LANES), lambda p, i: (0, p)),
                  pl.BlockSpec((t, LANES), lambda p, i: (0, p)),
                  pl.BlockSpec((ATT_BLK, 2 * LANES), lambda p, i: (i, p)),
                  pl.BlockSpec((ATT_BLK, LANES), lambda p, i: (i, p))],
        out_specs=[pl.BlockSpec((ATT_BLK, LANES), lambda p, i: (i, p)),
                   pl.BlockSpec((t, LANES), lambda p, i: (0, p)),
                   pl.BlockSpec((t, LANES), lambda p, i: (0, p))],
        out_shape=[jax.ShapeDtypeStruct((t, SB_WIDTH), F32)] * 3,
        compiler_params=pltpu.CompilerParams(dimension_semantics=("arbitrary", "arbitrary")),
    )(q, k, v, tot, do)


@jax.custom_vjp
def _sb_attention(q, k, v):
    return _sb_fwd(q, k, v)[0]


def _sb_attention_fwd(q, k, v):
    o, tot = _sb_fwd(q, k, v)
    return o, (q, k, v, tot)


def _sb_attention_bwd(res, do):
    return tuple(_sb_bwd(*res, do))


_sb_attention.defvjp(_sb_attention_fwd, _sb_attention_bwd)


def _mla_fwd(qn, qr, kn, kr, v):
    t = qn.shape[0]
    nq = t // ATT_BLK
    scale = MLA_QK ** -0.5

    def body(qn_ref, qr_ref, kn_ref, kr_ref, v_ref, o_ref, lse_ref):
        qi = pl.program_id(1)
        lanes = [slice(hh * LANES, (hh + 1) * LANES) for hh in range(2)]
        qnb = [qn_ref[:, sl].astype(BF16) for sl in lanes]
        qrb = [qr_ref[:, sl].astype(BF16) for sl in lanes]

        def blocks(kbs, carry, diagonal):
            nb = len(kbs)
            chains = [(b, hh) for b in range(nb) for hh in range(2)]
            offs = [pl.multiple_of(kb * ATT_BLK, ATT_BLK) for kb in kbs]
            krbs = [kr_ref[pl.ds(off, ATT_BLK), :].astype(BF16) for off in offs]
            accs, ms, ls = [carry[0], carry[3]], [carry[1], carry[4]], [carry[2], carry[5]]
            ss = {(b, hh): (_nt(qnb[hh], kn_ref[pl.ds(offs[b], ATT_BLK), lanes[hh]].astype(BF16))
                            + _nt(qrb[hh], krbs[b])) * scale for b, hh in chains}
            if diagonal:
                causal = (lax.broadcasted_iota(jnp.int32, (ATT_BLK, ATT_BLK), 1)
                          <= lax.broadcasted_iota(jnp.int32, (ATT_BLK, ATT_BLK), 0))
                ss = {ch: jnp.where(causal, ss[ch], -jnp.inf) for ch in chains}
            m_new = list(ms)
            for b, hh in chains:
                m_new[hh] = jnp.maximum(m_new[hh], jnp.max(ss[(b, hh)], axis=-1, keepdims=True))
            ps = {(b, hh): jnp.exp(ss[(b, hh)] - m_new[hh]) for b, hh in chains}
            alphas = [jnp.exp(ms[hh] - m_new[hh]) for hh in range(2)]
            pvs = {(b, hh): jnp.dot(ps[(b, hh)].astype(BF16), v_ref[pl.ds(offs[b], ATT_BLK), lanes[hh]].astype(BF16),
                                    preferred_element_type=F32) for b, hh in chains}
            out = []
            for hh in range(2):
                acc, l = accs[hh] * alphas[hh], ls[hh] * alphas[hh]
                for b in range(nb):
                    acc, l = acc + pvs[(b, hh)], l + jnp.sum(ps[(b, hh)], axis=-1, keepdims=True)
                out += [acc, m_new[hh], l]
            return tuple(out)

        init = (jnp.zeros((ATT_BLK, LANES), F32), jnp.full((ATT_BLK, 1), -jnp.inf, F32), jnp.zeros((ATT_BLK, 1), F32))
        carry = blocks([qi], init + init, True)
        carry = lax.fori_loop(0, qi // 2, lambda pr, cr: blocks([2 * pr, 2 * pr + 1], cr, False), carry)
        carry = lax.cond(qi % 2 == 1, lambda cr: blocks([qi - 1], cr, False), lambda cr: cr, carry)
        for hh in range(2):
            acc, m, l = carry[3 * hh:3 * hh + 3]
            o_ref[:, lanes[hh]] = acc / l
            lse_ref[:, lanes[hh]] = jnp.broadcast_to(m + jnp.log(l), (ATT_BLK, LANES))

    blk = pl.BlockSpec((ATT_BLK, 2 * LANES), lambda p, i: (i, p))
    full = pl.BlockSpec((t, 2 * LANES), lambda p, i: (0, p))
    return pl.pallas_call(
        body, name="mla_attn_fwd", grid=(MLA_HEADS // 2, nq),
        in_specs=[blk, blk, full, pl.BlockSpec((t, LANES), lambda p, i: (0, 0)), full],
        out_specs=[blk, blk],
        out_shape=[jax.ShapeDtypeStruct((t, MLA_HEADS * LANES), F32)] * 2,
        compiler_params=pltpu.CompilerParams(dimension_semantics=("arbitrary", "arbitrary")),
    )(qn, qr, kn, kr, v)


def _mla_bwd(qn, qr, kn, kr, v, o, lse, do):
    t = qn.shape[0]
    nq = t // ATT_BLK
    scale = MLA_QK ** -0.5

    def body(qn_ref, qr_ref, kn_ref, kr_ref, v_ref, o_ref, lse_ref, do_ref,
             dqn_ref, dqr_ref, dkn_ref, dkr_ref, dv_ref):
        pair = pl.program_id(0)
        qi = pl.program_id(1)

        @pl.when(qi == 0)
        def _():
            dkn_ref[...] = jnp.zeros_like(dkn_ref)
            dv_ref[...] = jnp.zeros_like(dv_ref)

        @pl.when((qi == 0) & (pair == 0))
        def _():
            dkr_ref[...] = jnp.zeros_like(dkr_ref)

        lanes = [slice(hh * LANES, (hh + 1) * LANES) for hh in range(2)]
        qnb = [qn_ref[:, sl].astype(BF16) for sl in lanes]
        qrb = [qr_ref[:, sl].astype(BF16) for sl in lanes]
        dob = [do_ref[:, sl].astype(BF16) for sl in lanes]
        delta = [jnp.sum(do_ref[:, sl] * o_ref[:, sl], axis=-1, keepdims=True) for sl in lanes]
        lse_v = [lse_ref[:, hh * LANES:hh * LANES + 1] for hh in range(2)]

        def blocks(kbs, carry, diagonal):
            nb = len(kbs)
            chains = [(b, hh) for b in range(nb) for hh in range(2)]
            offs = [pl.multiple_of(kb * ATT_BLK, ATT_BLK) for kb in kbs]
            krbs = [kr_ref[pl.ds(off, ATT_BLK), :].astype(BF16) for off in offs]
            knb = {(b, hh): kn_ref[pl.ds(offs[b], ATT_BLK), lanes[hh]].astype(BF16) for b, hh in chains}
            vb = {(b, hh): v_ref[pl.ds(offs[b], ATT_BLK), lanes[hh]].astype(BF16) for b, hh in chains}
            ss = {(b, hh): _nt(qnb[hh], knb[(b, hh)]) + _nt(qrb[hh], krbs[b]) for b, hh in chains}
            dps = {(b, hh): _nt(dob[hh], vb[(b, hh)]) for b, hh in chains}
            ps = {(b, hh): jnp.exp(ss[(b, hh)] * scale - lse_v[hh]) for b, hh in chains}
            if diagonal:
                causal = (lax.broadcasted_iota(jnp.int32, (ATT_BLK, ATT_BLK), 1)
                          <= lax.broadcasted_iota(jnp.int32, (ATT_BLK, ATT_BLK), 0))
                ps = {ch: jnp.where(causal, ps[ch], 0.0) for ch in chains}
            dss = {(b, hh): (ps[(b, hh)] * (dps[(b, hh)] - delta[hh]) * scale).astype(BF16) for b, hh in chains}
            for b, hh in chains:
                dv_ref[pl.ds(offs[b], ATT_BLK), lanes[hh]] += _tn(ps[(b, hh)].astype(BF16), dob[hh])
            for b, hh in chains:
                dkn_ref[pl.ds(offs[b], ATT_BLK), lanes[hh]] += _tn(dss[(b, hh)], qnb[hh])
            for b in range(nb):
                dkr_ref[pl.ds(offs[b], ATT_BLK), :] += _tn(dss[(b, 0)], qrb[0]) + _tn(dss[(b, 1)], qrb[1])
            out = list(carry)
            for b, hh in chains:
                out[2 * hh] = out[2 * hh] + jnp.dot(dss[(b, hh)], knb[(b, hh)], preferred_element_type=F32)
                out[2 * hh + 1] = out[2 * hh + 1] + jnp.dot(dss[(b, hh)], krbs[b], preferred_element_type=F32)
            return tuple(out)

        zero = jnp.zeros((ATT_BLK, LANES), F32)
        carry = lax.fori_loop(0, qi // 2, lambda pr, cr: blocks([2 * pr, 2 * pr + 1], cr, False),
                              (zero, zero, zero, zero))
        carry = lax.cond(qi % 2 == 1, lambda cr: blocks([qi - 1], cr, False), lambda cr: cr, carry)
        carry = blocks([qi], carry, True)
        for hh in range(2):
            dqn_ref[:, lanes[hh]] = carry[2 * hh]
            dqr_ref[:, lanes[hh]] = carry[2 * hh + 1]

    blk = pl.BlockSpec((ATT_BLK, 2 * LANES), lambda p, i: (i, p))
    full = pl.BlockSpec((t, 2 * LANES), lambda p, i: (0, p))
    shared = pl.BlockSpec((t, LANES), lambda p, i: (0, 0))
    wide = jax.ShapeDtypeStruct((t, MLA_HEADS * LANES), F32)
    return pl.pallas_call(
        body, name="mla_attn_bwd", grid=(MLA_HEADS // 2, nq),
        in_specs=[blk, blk, full, shared, full, blk, blk, blk],
        out_specs=[blk, blk, full, shared, full],
        out_shape=[wide, wide, wide, jax.ShapeDtypeStruct((t, LANES), F32), wide],
        compiler_params=pltpu.CompilerParams(dimension_semantics=("arbitrary", "arbitrary")),
    )(qn, qr, kn, kr, v, o, lse, do)


@jax.custom_vjp
def _mla_attention(qn, qr, kn, kr, v):
    return _mla_fwd(qn, qr, kn, kr, v)[0]


def _mla_attention_fwd(qn, qr, kn, kr, v):
    o, lse = _mla_fwd(qn, qr, kn, kr, v)
    return o, (qn, qr, kn, kr, v, o, lse)


def _mla_attention_bwd(res, do):
    return tuple(_mla_bwd(*res, do))


_mla_attention.defvjp(_mla_attention_fwd, _mla_attention_bwd)


def _ffn_in(h, wg, wu):
    t, k = h.shape
    n_sh, _, cc = wg.shape

    def body(h_ref, wg_ref, wu_ref, a_ref):
        hb = h_ref[...].astype(BF16)
        for j in range(n_sh):
            g = jnp.dot(hb, wg_ref[j], preferred_element_type=F32)
            u = jnp.dot(hb, wu_ref[j], preferred_element_type=F32)
            a_ref[:, j * cc:(j + 1) * cc] = _f_swiglu(g, u)[0].astype(BF16)

    w_spec = pl.BlockSpec((n_sh, k, cc), lambda i: (0, 0, 0))
    return pl.pallas_call(
        body, name="ffn_in_fwd", grid=(t // ROW_TILE,),
        in_specs=[pl.BlockSpec((ROW_TILE, k), lambda i: (i, 0)), w_spec, w_spec],
        out_specs=pl.BlockSpec((ROW_TILE, n_sh * cc), lambda i: (i, 0)),
        out_shape=jax.ShapeDtypeStruct((t, n_sh * cc), BF16),
        compiler_params=pltpu.CompilerParams(dimension_semantics=("arbitrary",), vmem_limit_bytes=MM_VMEM_LIMIT),
    )(h, wg, wu)


def _ffn_mid_bwd(dy, h, wd, wg, wu):
    t, n = dy.shape
    k = h.shape[1]
    n_sh, cc, _ = wd.shape

    def body(dy_ref, h_ref, wd_ref, wg_ref, wu_ref, dg_ref, du_ref):
        hb = h_ref[...].astype(BF16)
        g = jnp.dot(hb, wg_ref[...], preferred_element_type=F32)
        u = jnp.dot(hb, wu_ref[...], preferred_element_type=F32)
        d_act = _nt(dy_ref[...].astype(BF16), wd_ref[...])
        _, vjp = jax.vjp(_f_swiglu, g, u)
        dg, du = vjp((d_act,))
        dg_ref[...] = dg.astype(BF16)
        du_ref[...] = du.astype(BF16)

    blk = pl.BlockSpec((ROW_TILE, cc), lambda j, i: (i, j))
    w_in_spec = pl.BlockSpec((None, k, cc), lambda j, i: (j, 0, 0))
    wide = jax.ShapeDtypeStruct((t, n_sh * cc), BF16)
    return pl.pallas_call(
        body, name="ffn_mid_bwd", grid=(n_sh, t // ROW_TILE),
        in_specs=[pl.BlockSpec((ROW_TILE, n), lambda j, i: (i, 0)), pl.BlockSpec((ROW_TILE, k), lambda j, i: (i, 0)),
                  pl.BlockSpec((None, cc, n), lambda j, i: (j, 0, 0)), w_in_spec, w_in_spec],
        out_specs=[blk, blk], out_shape=[wide, wide],
        compiler_params=pltpu.CompilerParams(dimension_semantics=("arbitrary", "arbitrary"),
                                             vmem_limit_bytes=MM_VMEM_LIMIT),
    )(dy, h, wd, wg, wu)


def _ffn_dh(dg, du, wg, wu):
    t = dg.shape[0]
    n_sh, k, cc = wg.shape

    def body(dg_ref, du_ref, wg_ref, wu_ref, o_ref):
        acc = jnp.zeros((ROW_TILE, k), F32)
        for j in range(n_sh):
            cols = slice(j * cc, (j + 1) * cc)
            acc = acc + _nt(dg_ref[:, cols], wg_ref[j]) + _nt(du_ref[:, cols], wu_ref[j])
        o_ref[...] = acc

    blk = pl.BlockSpec((ROW_TILE, n_sh * cc), lambda i: (i, 0))
    w_spec = pl.BlockSpec((n_sh, k, cc), lambda i: (0, 0, 0))
    return pl.pallas_call(
        body, name="ffn_dh", grid=(t // ROW_TILE,),
        in_specs=[blk, blk, w_spec, w_spec],
        out_specs=pl.BlockSpec((ROW_TILE, k), lambda i: (i, 0)),
        out_shape=jax.ShapeDtypeStruct((t, k), F32),
        compiler_params=pltpu.CompilerParams(dimension_semantics=("arbitrary",), vmem_limit_bytes=MM_VMEM_LIMIT),
    )(dg, du, wg, wu)


def _ffn_dw_in(h, dy, n_sh, name):
    t, k = h.shape
    cc = dy.shape[1] // n_sh
    tk = 512

    def body(h_ref, dy_ref, o_ref):
        o_ref[...] = _tn(h_ref[...].astype(BF16), dy_ref[...]).astype(BF16)

    return pl.pallas_call(
        body, name=name, grid=(n_sh, k // tk),
        in_specs=[pl.BlockSpec((t, tk), lambda j, i: (0, i)), pl.BlockSpec((t, cc), lambda j, i: (0, j))],
        out_specs=pl.BlockSpec((None, tk, cc), lambda j, i: (j, i, 0)),
        out_shape=jax.ShapeDtypeStruct((n_sh, k, cc), BF16),
        compiler_params=pltpu.CompilerParams(dimension_semantics=("arbitrary", "arbitrary"),
                                             vmem_limit_bytes=MM_VMEM_LIMIT),
    )(h, dy)


@jax.custom_vjp
def _ffn_block(h, wg, wu, wd):
    act = _ffn_in(h, wg, wu)
    return _mm(act, wd.reshape(-1, wd.shape[2]), "nn", "ffn_down_fwd", ROW_TILE, wd.shape[2])


def _ffn_block_fwd(h, wg, wu, wd):
    act = _ffn_in(h, wg, wu)
    y = _mm(act, wd.reshape(-1, wd.shape[2]), "nn", "ffn_down_fwd", ROW_TILE, wd.shape[2])
    return y, (h, wg, wu, wd, act)


def _ffn_block_bwd(res, dy):
    h, wg, wu, wd, act = res
    dg, du = _ffn_mid_bwd(dy, h, wd, wg, wu)
    dh = _ffn_dh(dg, du, wg, wu)
    n_sh = wg.shape[0]
    dwg = _ffn_dw_in(h, dg, n_sh, "ffn_gate_dw")
    dwu = _ffn_dw_in(h, du, n_sh, "ffn_up_dw")
    dwd = _mm(act, dy, "tn", "ffn_down_dw", 256, wd.shape[2], out_dtype=BF16).reshape(wd.shape)
    return dh, dwg, dwu, dwd


_ffn_block.defvjp(_ffn_block_fwd, _ffn_block_bwd)


def _split_cols(x, cuts):
    cuts = tuple(cuts)

    @jax.custom_vjp
    def op(x):
        return tuple(x[:, a:b] for a, b in zip((0,) + cuts, cuts + (x.shape[1],)))

    def fwd(x):
        return op(x), None

    def bwd(_, cts):
        return (jnp.concatenate(cts, axis=1),)

    op.defvjp(fwd, bwd)
    return op(x)


def _swap_halves(w):
    half = w.shape[-1] // 2
    return jnp.concatenate([w[..., half:], w[..., :half]], axis=-1)


def _pad_lanes(w):
    return jnp.concatenate([w, jnp.zeros(w.shape[:-1] + (LANES - w.shape[-1],), w.dtype)], axis=-1)


def _join_cols(shards):
    return shards.transpose(1, 0, 2).reshape(shards.shape[1], -1)


def _mod_parts(mod):
    return [mod[:, i * D_MODEL:(i + 1) * D_MODEL] for i in range(N_MOD)]


def _local_loss(x, mod, p, cos, sin, target):
    return _ffn_stage(x, _mixing_stage(x, mod, p, cos, sin), mod, p, target)


def _mixing_stage(x, mod, p, cos, sin):
    shift1, scale1 = _mod_parts(mod)[:2]

    w_in = _join_cols(p["w_in"])
    k_rope_w = w_in[:, 2176:2240]
    w_in_ext = jnp.concatenate([w_in[:, :2176], _pad_lanes(k_rope_w), _pad_lanes(_swap_halves(k_rope_w)),
                                jnp.zeros((D_MODEL, LANES), w_in.dtype)], axis=1)
    (h1,) = _make_rowwise("pre_attn", _f_pre_attn, 1, 3, [D_MODEL], [True])(x, p["norm_attn"], scale1, shift1)
    proj = _make_linear("in_proj", 512, 640)(h1, w_in_ext)
    q_sb, k_sb, v_sb, cq, ckv, kr, kr_sw, _ = _split_cols(proj, (512, 1024, 1536, 1920, 2176, 2304, 2432))

    o_sb = _sb_attention(q_sb, k_sb, v_sb)

    wq = _join_cols(p["w_q_up"]).reshape(MLA_Q_RANK, MLA_HEADS, MLA_QK)
    wq_n, wq_r = wq[:, :, :MLA_NOPE], wq[:, :, MLA_NOPE:]
    w_q_ext = jnp.concatenate([wq_n.reshape(MLA_Q_RANK, -1), _pad_lanes(wq_r).reshape(MLA_Q_RANK, -1),
                               _pad_lanes(_swap_halves(wq_r)).reshape(MLA_Q_RANK, -1)], axis=1)
    wkv = _join_cols(p["w_kv_up"]).reshape(MLA_KV_RANK, MLA_HEADS, MLA_NOPE + MLA_V)
    w_kv_ext = jnp.concatenate([wkv[:, :, :MLA_NOPE].reshape(MLA_KV_RANK, -1),
                                wkv[:, :, MLA_NOPE:].reshape(MLA_KV_RANK, -1)], axis=1)
    cqn, ckvn = _make_rowwise("mla_a", _f_mla_a, 2, 2, [MLA_Q_RANK, MLA_KV_RANK], [True, True])(
        cq, ckv, p["q_a_norm"], p["kv_a_norm"])
    qall = _make_linear("q_up", 384, 768)(cqn, w_q_ext)
    kvall = _make_linear("kv_up", 256, 1024)(ckvn, w_kv_ext)
    kn_all, v_mla = _split_cols(kvall, (512,))
    gq = p["q_norm"]
    gkr = p["k_rope_norm"]
    qn, qr, kn, krr = _make_rowwise("mla_b", _f_mla_b, 6, 6, [512, 512, 512, LANES],
                                    [True, True, True, True, False, False])(
        qall, kn_all, kr, kr_sw, cos, sin,
        gq[:, :MLA_NOPE], _pad_lanes(gq[:, MLA_NOPE:]), _pad_lanes(_swap_halves(gq[:, MLA_NOPE:])),
        p["k_nope_norm"], _pad_lanes(gkr), _pad_lanes(_swap_halves(gkr)))
    o_mla = _mla_attention(qn, qr, kn, krr, v_mla)

    (mixed,) = _make_rowwise("post_attn", _f_post_attn, 2, 2, [D_MODEL], [True, True])(
        o_sb, o_mla, p["out_norm_sb"], p["out_norm_mla"])
    return mixed


def _ffn_stage(x, mixed, mod, p, target):
    _, _, gate1, shift2, scale2, gate2 = _mod_parts(mod)
    attn = _make_linear("out_proj", 512, 512)(mixed, p["w_out"].reshape(D_MODEL, D_MODEL))

    x2, h2 = _make_rowwise("pre_ffn", _f_pre_ffn, 2, 4, [D_MODEL, D_MODEL], [True, True])(
        x, attn, gate1, p["norm_ffn"], scale2, shift2)
    ffn = _ffn_block(h2, p["w_gate"], p["w_up"], p["w_down"])
    (row_loss,) = _make_rowwise("loss", _f_loss, 3, 1, [1], [True, True, False])(x2, ffn, target, gate2)
    return 0.5 * jnp.sum(row_loss)


def _my_place():
    return lax.axis_index("x"), lax.axis_index("y"), lax.axis_index("c")


def _all_gather_small(block, name):
    m_per, n = block.shape

    def body(x_ref, out_ref, send_sems, recv_sems, local_sem):
        x, y, c = _my_place()
        me, sibling = (x, y, c), (x, y, 1 - c)
        chips = [(1 - x, y), (x, 1 - y), (1 - x, 1 - y)]

        def rows(px, py, pc):
            return out_ref.at[pl.ds((4 * px + 2 * py + pc) * m_per, m_per), :]

        def copy(k, blk, to, src=None):
            return pltpu.make_async_remote_copy(
                src_ref=rows(*blk) if src is None else src, dst_ref=rows(*blk),
                send_sem=send_sems.at[k], recv_sem=recv_sems.at[k], device_id=to, device_id_type=MESH)

        mine = pltpu.make_async_copy(x_ref, rows(*me), local_sem)
        mine.start()
        first = [copy(0, me, sibling, src=x_ref)]
        first += [copy(1 + j, me, (*chip, c), src=x_ref) for j, chip in enumerate(chips)]
        for cp in first:
            cp.start()
        passed = [copy(4 + j, (*chip, c), sibling) for j, chip in enumerate(chips)]
        for j, chip in enumerate(chips):
            copy(1 + j, (*chip, c), me).wait_recv()
            passed[j].start()
        copy(0, sibling, me).wait_recv()
        for j, chip in enumerate(chips):
            copy(4 + j, (*chip, 1 - c), me).wait_recv()
        for cp in first + passed:
            cp.wait_send()
        mine.wait()

    return pl.pallas_call(
        body, name=name,
        out_shape=jax.ShapeDtypeStruct((N_DEV * m_per, n), block.dtype),
        in_specs=[pl.BlockSpec(memory_space=pltpu.VMEM)],
        out_specs=pl.BlockSpec(memory_space=pltpu.VMEM),
        scratch_shapes=[pltpu.SemaphoreType.DMA((7,)), pltpu.SemaphoreType.DMA((7,)), pltpu.SemaphoreType.DMA],
    )(block)


EARLY = ("w_in", "w_q_up", "w_kv_up")
LATE = ("w_out", "w_gate", "w_up", "w_down")
BIG = EARLY + LATE
TRANSPOSED_UPDATE = ("w_in", "w_gate", "w_up")
HALF_AXIS = {"w_in": 0, "w_q_up": 0, "w_kv_up": 0, "w_out": 0, "w_gate": 0, "w_up": 0, "w_down": 1}


def _half(ref, h, axis, lead=()):
    trail = ref.shape[len(lead):]
    idx = list(lead) + [slice(None)] * len(trail)
    at = len(trail) - 2 + axis
    n2 = trail[at] // 2
    idx[len(lead) + at] = pl.ds(h * n2, n2)
    return ref.at[tuple(idx)]


def _half_shape(shape, axis):
    shape = list(shape)
    shape[len(shape) - 2 + axis] //= 2
    return tuple(shape)


def _remote(src, dst, send_sems, recv_sems, k, to):
    return pltpu.make_async_remote_copy(src_ref=src, dst_ref=dst, send_sem=send_sems.at[k],
                                        recv_sem=recv_sems.at[k], device_id=to, device_id_type=MESH)


def _gather_weights(names, shards, after):
    n_w = len(shards)
    axes = [HALF_AXIS[n] for n in names]

    def body(*refs):
        w_refs, out_refs, token = refs[:n_w], refs[n_w + 1:2 * n_w + 1], refs[2 * n_w + 1]
        send_sems, recv_sems, local_sems = refs[2 * n_w + 2:]
        token[...] = jnp.zeros_like(token)
        x, y, c = _my_place()
        sibling = (x, y, 1 - c)
        chips = [(1 - x, y), (x, 1 - y), (1 - x, 1 - y)]
        me = 2 * x + y
        mine =[pltpu.make_async_copy(w, o.at[me], local_sems.at[i]) for i, (w, o) in enumerate(zip(w_refs, out_refs))]
        for cp in mine:
            cp.start()
        first = [_remote(_half(w_refs[i], c, axes[i]), _half(out_refs[i], c, axes[i], (me,)),
                         send_sems, recv_sems, 6 * i + j, (*chip, c))
                 for i in range(n_w) for j, chip in enumerate(chips)]
        for cp in first:
            cp.start()
        passed = []
        for j, (cx, cy) in enumerate(chips):
            for i in range(n_w):
                blk = _half(out_refs[i], c, axes[i], (2 * cx + cy,))
                _remote(blk, blk, send_sems, recv_sems, 6 * i + j, (cx, cy, c)).wait_recv()
                cp = _remote(blk, blk, send_sems, recv_sems, 6 * i + 3 + j, sibling)
                cp.start()
                passed.append(cp)
        for j, (cx, cy) in enumerate(chips):
            for i in range(n_w):
                blk = _half(out_refs[i], 1 - c, axes[i], (2 * cx + cy,))
                _remote(blk, blk, send_sems, recv_sems, 6 * i + 3 + j, sibling).wait_recv()
        for cp in first + passed:
            cp.wait_send()
        for cp in mine:
            cp.wait()

    outs = pl.pallas_call(
        body, name="gather_weights",
        out_shape=[jax.ShapeDtypeStruct((N_CHIPS,) + s.shape, s.dtype) for s in shards]
        + [jax.ShapeDtypeStruct((8, LANES), F32)],
        in_specs=[ANY] * (n_w + 1), out_specs=[ANY] * n_w + [pl.BlockSpec(memory_space=pltpu.VMEM)],
        scratch_shapes=[pltpu.SemaphoreType.DMA((6 * n_w,)), pltpu.SemaphoreType.DMA((6 * n_w,)),
                        pltpu.SemaphoreType.DMA((n_w,))],
    )(*shards, after)
    return outs[:n_w], outs[n_w]


def _pair_exchange(names, grads, call_name):
    n_w = len(grads)
    axes = [HALF_AXIS[n] for n in names]

    def body(*refs):
        g_refs, t_refs = refs[:n_w], refs[n_w:2 * n_w]
        send_sems, recv_sems = refs[2 * n_w:]
        x, y, c = _my_place()
        sends = [_remote(_half(g_refs[i], 1 - c, axes[i]), t_refs[i], send_sems, recv_sems, i, (x, y, 1 - c))
                 for i in range(n_w)]
        for cp in sends:
            cp.start()
        for cp in sends:
            cp.wait_recv()
        for cp in sends:
            cp.wait_send()

    return pl.pallas_call(
        body, name=call_name,
        out_shape=[jax.ShapeDtypeStruct(_half_shape(g.shape, a), g.dtype) for g, a in zip(grads, axes)],
        in_specs=[ANY] * n_w, out_specs=[ANY] * n_w,
        scratch_shapes=[pltpu.SemaphoreType.DMA((n_w,)), pltpu.SemaphoreType.DMA((n_w,))],
    )(*grads)


def _chip_scatter(pair_sums):
    n_w = len(pair_sums)

    def body(*refs):
        s_refs, p_refs = refs[:n_w], refs[n_w:2 * n_w]
        send_sems, recv_sems = refs[2 * n_w:]
        x, y, c = _my_place()
        chips = [(1 - x, y), (x, 1 - y), (1 - x, 1 - y)]
        sends = [_remote(s_refs[i].at[2 * cx + cy], p_refs[i].at[j], send_sems, recv_sems, 3 * i + j, (cx, cy, c))
                 for i in range(n_w) for j, (cx, cy) in enumerate(chips)]
        for cp in sends:
            cp.start()
        for cp in sends:
            cp.wait_recv()
        for cp in sends:
            cp.wait_send()

    return pl.pallas_call(
        body, name="grad_chip_scatter",
        out_shape=[jax.ShapeDtypeStruct((N_CHIPS - 1,) + s.shape[1:], s.dtype) for s in pair_sums],
        in_specs=[ANY] * n_w, out_specs=[ANY] * n_w,
        scratch_shapes=[pltpu.SemaphoreType.DMA((3 * n_w,)), pltpu.SemaphoreType.DMA((3 * n_w,))],
    )(*pair_sums)


def _sibling_join(halves, name, after):
    n_w = len(halves)

    def body(*refs):
        s_refs, j_refs = refs[:n_w], refs[n_w + 1:2 * n_w + 1]
        send_sems, recv_sems = refs[2 * n_w + 1:]
        x, y, c = _my_place()
        sends = [_remote(s_refs[i], j_refs[i], send_sems, recv_sems, i, (x, y, 1 - c)) for i in range(n_w)]
        for cp in sends:
            cp.start()
        for cp in sends:
            cp.wait_recv()
        for cp in sends:
            cp.wait_send()

    return pl.pallas_call(
        body, name=name,
        out_shape=[jax.ShapeDtypeStruct(s.shape, s.dtype) for s in halves],
        in_specs=[ANY] * (n_w + 1), out_specs=[ANY] * n_w,
        scratch_shapes=[pltpu.SemaphoreType.DMA((n_w,)), pltpu.SemaphoreType.DMA((n_w,))],
    )(*halves, after)


HBM_SPEC = pl.BlockSpec(memory_space=pltpu.HBM)
SEM_SPEC = pl.BlockSpec(memory_space=pltpu.SEMAPHORE)
DATAFLOW = pltpu.SideEffectType.DATAFLOW_SIDE_EFFECTING


def _in_hbm(a):
    return pltpu.with_memory_space_constraint(a, pltpu.HBM)


def _exchange_start(name, srcs, lands, plan, n_copies, after):
    n = len(srcs)

    def body(*refs):
        src_refs, land_refs = refs[:n], refs[n:2 * n]
        send_sems, recv_sems = refs[2 * n + 1], refs[2 * n + 2]
        token = refs[-1]
        for k, (src, dst, to) in enumerate(plan(src_refs, land_refs)):
            _remote(src, dst, send_sems, recv_sems, k, to).start()
        token[...] = jnp.zeros_like(token)

    outs = pl.pallas_call(
        body, name=name,
        out_shape=(pltpu.SemaphoreType.DMA((n_copies,)), pltpu.SemaphoreType.DMA((n_copies,)),
                   *[pltpu.HBM(a.shape, a.dtype) for a in srcs], *[pltpu.HBM(a.shape, a.dtype) for a in lands],
                   jax.ShapeDtypeStruct((8, LANES), F32)),
        in_specs=[HBM_SPEC] * (2 * n) + [ANY],
        out_specs=(SEM_SPEC, SEM_SPEC, *[HBM_SPEC] * (2 * n), pl.BlockSpec(memory_space=pltpu.VMEM)),
        input_output_aliases={i: 2 + i for i in range(2 * n)},
        compiler_params=pltpu.CompilerParams(has_side_effects=DATAFLOW),
    )(*[_in_hbm(a) for a in srcs], *[_in_hbm(a) for a in lands], after)
    return outs[0], outs[1], outs[2:2 + n], outs[2 + n:2 + 2 * n], outs[-1]


def _exchange_wait(name, started, plan, after):
    send_sems, recv_sems, srcs, lands, _ = started
    n = len(srcs)

    def body(*refs):
        src_refs, land_refs = refs[:n], refs[n:2 * n]
        s_sems, r_sems = refs[2 * n], refs[2 * n + 1]
        for k, (src, dst, to) in enumerate(plan(src_refs, land_refs)):
            cp = _remote(src, dst, s_sems, r_sems, k, to)
            cp.wait_send()
            cp.wait_recv()

    outs = pl.pallas_call(
        body, name=name,
        out_shape=tuple(pltpu.HBM(a.shape, a.dtype) for a in list(srcs) + list(lands)),
        in_specs=[HBM_SPEC] * (2 * n) + [SEM_SPEC, SEM_SPEC, ANY],
        out_specs=tuple([HBM_SPEC] * (2 * n)),
        input_output_aliases={i: i for i in range(2 * n)},
        compiler_params=pltpu.CompilerParams(has_side_effects=DATAFLOW),
    )(*srcs, *lands, send_sems, recv_sems, after)
    return outs[:n], outs[n:]


def _late_gather_plan(src_refs, land_refs):
    x, y, c = _my_place()
    chips = [(1 - x, y), (x, 1 - y), (1 - x, 1 - y)]
    return [(src, land.at[2 * x + y], (cx, cy, c)) for src, land in zip(src_refs, land_refs) for cx, cy in chips]


def _late_scatter_plan(src_refs, land_refs):
    x, y, c = _my_place()
    chips = [(1 - x, y), (x, 1 - y), (1 - x, 1 - y)]
    return [(src.at[2 * cx + cy], land.at[j], (cx, cy, c))
            for src, land in zip(src_refs, land_refs) for j, (cx, cy) in enumerate(chips)]


def _row_tile(rows, mult=16, limit=ROW_TILE):
    return max(d for d in range(mult, limit + 1, mult) if rows % d == 0)


def _pair_sum(place, g, theirs, axis, name):
    nj, rr, cc = theirs.shape
    tr = _row_tile(rr, limit=1024)
    nb = rr // tr
    if axis == 0:
        g_map = lambda j, i, pr: (j, pr[0] * nb + i, 0)
    else:
        g_map = lambda j, i, pr: (j, i, pr[0])

    def body(pr, g_ref, t_ref, o_ref):
        o_ref[...] = (g_ref[...].astype(F32) + t_ref[...].astype(F32)).astype(BF16)

    spec = pl.BlockSpec((None, tr, cc), lambda j, i, pr: (j, i, 0))
    return pl.pallas_call(
        body, name=name,
        grid_spec=pltpu.PrefetchScalarGridSpec(
            num_scalar_prefetch=1, grid=(nj, nb),
            in_specs=[pl.BlockSpec((None, tr, cc), g_map), spec], out_specs=spec),
        out_shape=jax.ShapeDtypeStruct(theirs.shape, BF16))(place, g, theirs)


def _chip_sum(place, pair_sums, parts, name, transposed):
    _, rr, cc = parts.shape
    tr = _row_tile(rr, LANES) if transposed else _row_tile(rr, limit=1024)

    def body(pr, h_ref, p_ref, o_ref):
        acc = p_ref[0].astype(F32)
        for j in range(1, N_CHIPS - 1):
            acc = acc + p_ref[j].astype(F32)
        acc = acc + h_ref[...].astype(F32)
        o_ref[...] = (acc.T if transposed else acc).astype(BF16)

    out_spec = pl.BlockSpec((cc, tr), lambda i, pr: (0, i)) if transposed else pl.BlockSpec((tr, cc), lambda i, pr: (i, 0))
    return pl.pallas_call(
        body, name=name,
        grid_spec=pltpu.PrefetchScalarGridSpec(
            num_scalar_prefetch=1, grid=(rr // tr,),
            in_specs=[pl.BlockSpec((None, tr, cc), lambda i, pr: (pr[1], i, 0)),
                      pl.BlockSpec((N_CHIPS - 1, tr, cc), lambda i, pr: (0, i, 0))],
            out_specs=out_spec),
        out_shape=jax.ShapeDtypeStruct((cc, rr) if transposed else (rr, cc), BF16))(place, pair_sums, parts)


def _silu(v):
    return v / (1.0 + jnp.exp(-v))


def _ada_fwd(c_all, w_shard, b_shard):
    def body(c_ref, w_ref, b_ref, o_ref):
        o_ref[...] = jnp.dot(_silu(c_ref[...]), w_ref[...], precision=lax.Precision.HIGHEST,
                             preferred_element_type=F32) + b_ref[...]

    return pl.pallas_call(body, name="ada_fwd", out_shape=jax.ShapeDtypeStruct((c_all.shape[0], w_shard.shape[1]), F32),
                          compiler_params=pltpu.CompilerParams(vmem_limit_bytes=MM_VMEM_LIMIT))(c_all, w_shard, b_shard)


def _ada_bwd(c_all, dmod_cols):
    def body(c_ref, d_ref, o_ref):
        o_ref[...] = lax.dot_general(_silu(c_ref[...]), d_ref[...], (((0,), (0,)), ((), ())),
                                     precision=lax.Precision.HIGHEST, preferred_element_type=F32)

    return pl.pallas_call(body, name="ada_bwd", out_shape=jax.ShapeDtypeStruct((c_all.shape[1], dmod_cols.shape[1]), F32),
                          compiler_params=pltpu.CompilerParams(vmem_limit_bytes=MM_VMEM_LIMIT))(c_all, dmod_cols)


def _adamw_math(w, g, m, v):
    m = ADAM_B1 * m + (1.0 - ADAM_B1) * g
    v = ADAM_B2 * v + (1.0 - ADAM_B2) * (g * g)
    m_hat = m / (1.0 - ADAM_B1 ** ADAM_STEP)
    v_hat = v / (1.0 - ADAM_B2 ** ADAM_STEP)
    delta = -ADAM_LR * (m_hat / (jnp.sqrt(v_hat) + ADAM_EPS) + ADAM_WD * w)
    return delta, m, v


def _adamw(w, g, m, v, name):
    r, ccols = w.shape
    tr = max(d for d in range(8, ROW_TILE + 1, 8) if r % d == 0)
    spec = pl.BlockSpec((tr, ccols), lambda i: (i, 0))

    def body(w_ref, g_ref, m_ref, v_ref, d_ref, nm_ref, nv_ref):
        d_ref[...], nm_ref[...], nv_ref[...] = _adamw_math(w_ref[...], g_ref[...], m_ref[...], v_ref[...])

    return pl.pallas_call(body, name=name, grid=(r // tr,), in_specs=[spec] * 4, out_specs=[spec] * 3,
                          out_shape=[jax.ShapeDtypeStruct(w.shape, F32)] * 3,
                          compiler_params=pltpu.CompilerParams(vmem_limit_bytes=MM_VMEM_LIMIT))(w, g, m, v)


def _adamw_small(w, g_all, m, v):
    def body(w_ref, g_ref, m_ref, v_ref, gs_ref, d_ref, nm_ref, nv_ref):
        g = g_ref[0]
        for d in range(1, N_DEV):
            g = g + g_ref[d]
        gs_ref[...] = g
        d_ref[...], nm_ref[...], nv_ref[...] = _adamw_math(w_ref[...], g, m_ref[...], v_ref[...])

    return pl.pallas_call(body, name="adamw_small", out_shape=[jax.ShapeDtypeStruct(w.shape, F32)] * 4)(w, g_all, m, v)


def _adamw_halves(place, w, own, sib, m, v, axis, name):
    r, cc = w.shape
    if axis == 0:
        rows, gc = own.shape[0], own.shape[1]
        tr = _row_tile(rows)
        nb = rows // tr
        w_spec = pl.BlockSpec((tr, cc), lambda h, i, pr: (h * nb + i, 0))
        g_spec = pl.BlockSpec((tr, gc), lambda h, i, pr: (i, 0))
    else:
        tr = _row_tile(r)
        nb = r // tr
        gc = own.shape[1]
        w_spec = pl.BlockSpec((tr, gc), lambda h, i, pr: (i, h))
        g_spec = pl.BlockSpec((tr, gc), lambda h, i, pr: (i, 0))
    wc = w_spec.block_shape[1]

    def body(pr, w_ref, o_ref, s_ref, m_ref, v_ref, g_ref, d_ref, nm_ref, nv_ref):
        g = jnp.where(pl.program_id(0) == pr[0], o_ref[...], s_ref[...]).astype(F32)[:, :wc]
        g_ref[...] = g
        d_ref[...], nm_ref[...], nv_ref[...] = _adamw_math(w_ref[...], g, m_ref[...], v_ref[...])

    return pl.pallas_call(
        body, name=name,
        grid_spec=pltpu.PrefetchScalarGridSpec(
            num_scalar_prefetch=1, grid=(2, nb),
            in_specs=[w_spec, g_spec, g_spec, w_spec, w_spec], out_specs=[w_spec] * 4),
        out_shape=[jax.ShapeDtypeStruct(w.shape, F32)] * 4,
        compiler_params=pltpu.CompilerParams(vmem_limit_bytes=MM_VMEM_LIMIT))(place, w, own, sib, m, v)


SMALL = ("b_ada", "norm_attn", "norm_ffn", "q_a_norm", "kv_a_norm", "q_norm", "k_nope_norm", "k_rope_norm",
         "out_norm_sb", "out_norm_mla")
WEIGHTS = ("w_ada", "b_ada", "norm_attn", "norm_ffn", "w_in", "q_a_norm", "w_q_up", "kv_a_norm", "w_kv_up",
           "q_norm", "k_nope_norm", "k_rope_norm", "out_norm_sb", "out_norm_mla", "w_out", "w_gate", "w_up",
           "w_down")


def kernel(x, c, positions, w_ada, b_ada, norm_attn, norm_ffn, w_in, q_a_norm, w_q_up, kv_a_norm, w_kv_up, q_norm, k_nope_norm, k_rope_norm, out_norm_sb, out_norm_mla, w_out, w_gate, w_up, w_down, loss_target, m_w_ada, m_b_ada, m_norm_attn, m_norm_ffn, m_w_in, m_q_a_norm, m_w_q_up, m_kv_a_norm, m_w_kv_up, m_q_norm, m_k_nope_norm, m_k_rope_norm, m_out_norm_sb, m_out_norm_mla, m_w_out, m_w_gate, m_w_up, m_w_down, v_w_ada, v_b_ada, v_norm_attn, v_norm_ffn, v_w_in, v_q_a_norm, v_w_q_up, v_kv_a_norm, v_w_kv_up, v_q_norm, v_k_nope_norm, v_k_rope_norm, v_out_norm_sb, v_out_norm_mla, v_w_out, v_w_gate, v_w_up, v_w_down):
    local = dict(locals())
    w = {n: local[n][0] for n in WEIGHTS}
    m = {n: local["m_" + n][0] for n in WEIGHTS}
    v = {n: local["v_" + n][0] for n in WEIGHTS}
    small = {n: w[n].reshape(1, -1) for n in SMALL}
    ix, iy, ic = _my_place()
    chip = 2 * ix + iy
    dev = 2 * chip + ic
    xs, target = x[0], loss_target[0]
    seq = xs.shape[0]

    c_all = _all_gather_small(c.reshape(8, LANES), "gather_c").reshape(N_DEV, D_MODEL)
    ada_cols = w["w_ada"].shape[1]
    b_cols = lax.dynamic_slice_in_dim(small["b_ada"], chip * ada_cols, ada_cols, axis=1)
    mod_cols = _ada_fwd(c_all, w["w_ada"], b_cols)
    mod_all = _all_gather_small(mod_cols, "gather_mod").reshape(N_CHIPS, 2, N_DEV, ada_cols)
    mod = lax.dynamic_index_in_dim(mod_all[:, 0], dev, axis=1, keepdims=False).reshape(1, N_MOD * D_MODEL)

    ff_pad = FF_SHARD_PAD - FF_SHARD
    pads = {"w_gate": ((0, 0), (0, ff_pad)), "w_up": ((0, 0), (0, ff_pad)), "w_down": ((0, ff_pad), (0, 0))}
    shards = {n: jnp.pad(w[n].astype(BF16), pads[n]) if n in pads else w[n].astype(BF16) for n in BIG}
    early, early_done = _gather_weights(EARLY, [shards[n] for n in EARLY], mod)
    gathered = dict(zip(EARLY, early))
    lands = [lax.dynamic_update_index_in_dim(lax.empty((N_CHIPS,) + shards[n].shape, BF16), shards[n], chip, 0)
             for n in LATE]
    late_gather = _exchange_start("gather_late_start", [shards[n] for n in LATE], lands, _late_gather_plan,
                                  3 * len(LATE), early_done)

    half = MLA_ROPE // 2
    freqs = 1.0 / (ROPE_THETA ** (np.arange(half, dtype=np.float32) / half))
    zeros = np.zeros(LANES - MLA_ROPE, np.float32)
    freqs_row = jnp.asarray(np.concatenate([freqs, freqs, zeros]).astype(np.float32)[None])
    sign_row = jnp.asarray(np.concatenate([-np.ones(half), np.ones(half), zeros]).astype(np.float32)[None])
    cos, sin = _rope_tables(positions.reshape(seq, 1), freqs_row, sign_row)

    place = jnp.stack([ic, chip]).astype(jnp.int32)
    small_params = {n: small[n] for n in SMALL if n != "b_ada"}
    mod = mod + late_gather[4][0, 0]

    def pair_sums_of(names, grads, call_name):
        theirs = _pair_exchange(names, grads, call_name)
        return [_pair_sum(place, gr, th, HALF_AXIS[n], "grad_pair_sum_" + n) for n, gr, th in zip(names, grads, theirs)]

    p1 = {**{n: gathered[n] for n in EARLY}, **small_params}
    mixed, mixing_vjp = jax.vjp(lambda x_, mod_, p_: _mixing_stage(x_, mod_, p_, cos, sin), xs, mod, p1)
    _, landed = _exchange_wait("gather_late_wait", late_gather, _late_gather_plan, mixed)
    p2 = {**dict(zip(LATE, landed)), **small_params}
    loss_part, ffn_vjp = jax.vjp(lambda x_, mixed_, mod_, p_: _ffn_stage(x_, mixed_, mod_, p_, target), xs, mixed, mod, p2)
    gx2, gmixed, gmod2, gp2 = ffn_vjp(jnp.ones((), F32))
    late_sums = pair_sums_of(LATE, [gp2[n] for n in LATE], "grad_pair_exchange_late")
    late_scatter = _exchange_start(
        "grad_scatter_late_start", late_sums,
        [lax.empty((N_CHIPS - 1,) + s.shape[1:], BF16) for s in late_sums], _late_scatter_plan, 3 * len(LATE), gx2)
    gx1, gmod1, gp1 = mixing_vjp(gmixed + late_scatter[4][0, 0])
    gx = gx1 + gx2
    gmod = gmod1 + gmod2
    gp = {n: gp1[n] + gp2[n] for n in small_params}
    loss = lax.psum(loss_part, ("x", "y", "c"))

    small_names = [n for n in SMALL if n != "b_ada"]
    small_vec = jnp.concatenate([gmod] + [gp[n] for n in small_names], axis=1)
    n_small = small_vec.shape[1]
    small_all = _all_gather_small(small_vec.reshape(8, n_small // 8), "gather_small").reshape(N_DEV, 8, n_small // 8)

    early_sums = pair_sums_of(EARLY, [gp1[n] for n in EARLY], "grad_pair_exchange_early")
    early_scatter = _exchange_start(
        "grad_scatter_early_start", early_sums,
        [lax.empty((N_CHIPS - 1,) + s.shape[1:], BF16) for s in early_sums], _late_scatter_plan, 3 * len(EARLY),
        small_all)
    late_sums, late_parts = _exchange_wait("grad_scatter_late_wait", late_scatter, _late_scatter_plan, gx)
    g, delta, new_m, new_v = {}, {}, {}, {}

    def update(names, sums, parts, join_name, after):
        own = [_chip_sum(place, ps, pt, "grad_chip_sum_" + n, n in TRANSPOSED_UPDATE) for n, ps, pt in zip(names, sums, parts)]
        sib = _sibling_join(own, join_name, after)
        for n, o, s in zip(names, own, sib):
            if n in TRANSPOSED_UPDATE:
                res = _adamw_halves(place, w[n].T, o, s, m[n].T, v[n].T, 1, "adamw_" + n)
                g[n], delta[n], new_m[n], new_v[n] = [r.T for r in res]
            else:
                g[n], delta[n], new_m[n], new_v[n] = _adamw_halves(place, w[n], o, s, m[n], v[n], HALF_AXIS[n], "adamw_" + n)

    update(LATE, late_sums, late_parts, "grad_sibling_join_late", early_scatter[4])

    def pack_small(d):
        return jnp.concatenate([d[n].reshape(1, -1) for n in SMALL], axis=1).reshape(8, n_small // 8)

    gs, ds, ms, vs = _adamw_small(pack_small(w), small_all, pack_small(m), pack_small(v))
    sizes = [w[n].size for n in SMALL]
    offs = np.concatenate([[0], np.cumsum(sizes)])

    def unpack_small(a):
        flat = a.reshape(-1)
        return {n: flat[offs[i]:offs[i + 1]].reshape(w[n].shape) for i, n in enumerate(SMALL)}

    for d, packed in zip((g, delta, new_m, new_v), (gs, ds, ms, vs)):
        d.update(unpack_small(packed))

    dmod_all = small_all.reshape(N_DEV, n_small)[:, :N_MOD * D_MODEL]
    g["w_ada"] = _ada_bwd(c_all, lax.dynamic_slice_in_dim(dmod_all, chip * ada_cols, ada_cols, axis=1))
    delta["w_ada"], new_m["w_ada"], new_v["w_ada"] = _adamw(w["w_ada"], g["w_ada"], m["w_ada"], v["w_ada"], "adamw_w_ada")

    early_sums, early_parts = _exchange_wait("grad_scatter_early_wait", early_scatter, _late_scatter_plan,
                                             delta["w_ada"])
    update(EARLY, early_sums, early_parts, "grad_sibling_join_early", delta["w_ada"])

    def outs(d):
        return [d[n][None] for n in WEIGHTS]

    return (loss, gx[None], *outs(g), *outs(delta), *outs(new_m), *outs(new_v))
```

```python
import functools
import math

import numpy as np
import jax
import jax.numpy as jnp
from jax import lax
from jax.experimental import pallas as pl
from jax.experimental.pallas import tpu as pltpu

F32 = jnp.float32
BF16 = jnp.bfloat16
MESH = pl.DeviceIdType.MESH
ANY = pl.BlockSpec(memory_space=pl.ANY)

D_MODEL = 1024
SB_HEADS = 8
SB_HEAD_DIM = 64
SB_WIDTH = 512
MLA_HEADS = 4
MLA_NOPE = 128
MLA_ROPE = 64
MLA_QK = 192
MLA_V = 128
MLA_Q_RANK = 384
MLA_KV_RANK = 256
D_FF = 2816
N_MOD = 6
ROPE_THETA = 10000.0
EPS = 1e-6
LANES = 128

ADAM_LR = 0.001
ADAM_B1 = 0.9
ADAM_B2 = 0.999
ADAM_EPS = 1e-08
ADAM_WD = 0.01
ADAM_STEP = 10

N_CHIPS = 4
N_DEV = 8
ROW_TILE = 256
ATT_BLK = 256
MM_VMEM_LIMIT = 56 * 1024 * 1024
FF_SHARD = D_FF // N_CHIPS
FF_SHARD_PAD = 768


def _mm(a, b, mode, name, tm, tn, out_dtype=F32):
    if mode == "nn":
        (m, k), n = a.shape, b.shape[1]
        a_spec = pl.BlockSpec((tm, k), lambda j, i: (i, 0))
        b_spec = pl.BlockSpec((k, tn), lambda j, i: (0, j))
        dims = (((1,), (0,)), ((), ()))
    elif mode == "nt":
        (m, k), n = a.shape, b.shape[0]
        a_spec = pl.BlockSpec((tm, k), lambda j, i: (i, 0))
        b_spec = pl.BlockSpec((tn, k), lambda j, i: (j, 0))
        dims = (((1,), (1,)), ((), ()))
    else:
        (k, m), n = a.shape, b.shape[1]
        a_spec = pl.BlockSpec((k, tm), lambda j, i: (0, i))
        b_spec = pl.BlockSpec((k, tn), lambda j, i: (0, j))
        dims = (((0,), (0,)), ((), ()))
    assert m % tm == 0 and n % tn == 0, (name, m, n, tm, tn)

    def body(a_ref, b_ref, o_ref):
        o_ref[...] = lax.dot_general(a_ref[...].astype(BF16), b_ref[...].astype(BF16), dims,
                                     preferred_element_type=F32).astype(out_dtype)

    return pl.pallas_call(
        body, name=name, grid=(n // tn, m // tm),
        in_specs=[a_spec, b_spec],
        out_specs=pl.BlockSpec((tm, tn), lambda j, i: (i, j)),
        out_shape=jax.ShapeDtypeStruct((m, n), out_dtype),
        compiler_params=pltpu.CompilerParams(dimension_semantics=("arbitrary", "arbitrary"),
                                             vmem_limit_bytes=MM_VMEM_LIMIT),
    )(a, b)


def _make_linear(name, tk_w, tn_w):
    @jax.custom_vjp
    def op(a, w):
        return _mm(a, w, "nn", name + "_fwd", ROW_TILE, w.shape[1])

    def fwd(a, w):
        return op(a, w), (a, w)

    def bwd(res, dy):
        a, w = res
        da = _mm(dy, w, "nt", name + "_dx", ROW_TILE, w.shape[0])
        dw = _mm(a, dy, "tn", name + "_dw", tk_w, tn_w, out_dtype=BF16)
        return da, dw

    op.defvjp(fwd, bwd)
    return op


def _make_linear_sharded(name, tk_w):
    def call_fwd(a, w):
        t, k = a.shape
        n_sh, _, cc = w.shape

        def body(a_ref, w_ref, o_ref):
            o_ref[...] = jnp.dot(a_ref[...].astype(BF16), w_ref[...], preferred_element_type=F32)

        return pl.pallas_call(
            body, name=name + "_fwd", grid=(n_sh, t // ROW_TILE),
            in_specs=[pl.BlockSpec((ROW_TILE, k), lambda j, i: (i, 0)),
                      pl.BlockSpec((None, k, cc), lambda j, i: (j, 0, 0))],
            out_specs=pl.BlockSpec((ROW_TILE, cc), lambda j, i: (i, j)),
            out_shape=jax.ShapeDtypeStruct((t, n_sh * cc), F32),
            compiler_params=pltpu.CompilerParams(dimension_semantics=("arbitrary", "arbitrary"),
                                                 vmem_limit_bytes=MM_VMEM_LIMIT),
        )(a, w)

    def call_dx(dy, w):
        t = dy.shape[0]
        n_sh, k, cc = w.shape

        def body(dy_ref, w_ref, o_ref):
            acc = jnp.zeros((ROW_TILE, k), F32)
            for j in range(n_sh):
                acc = acc + _nt(dy_ref[:, j * cc:(j + 1) * cc].astype(BF16), w_ref[j])
            o_ref[...] = acc

        return pl.pallas_call(
            body, name=name + "_dx", grid=(t // ROW_TILE,),
            in_specs=[pl.BlockSpec((ROW_TILE, n_sh * cc), lambda i: (i, 0)),
                      pl.BlockSpec((n_sh, k, cc), lambda i: (0, 0, 0))],
            out_specs=pl.BlockSpec((ROW_TILE, k), lambda i: (i, 0)),
            out_shape=jax.ShapeDtypeStruct((t, k), F32),
            compiler_params=pltpu.CompilerParams(dimension_semantics=("arbitrary",),
                                                 vmem_limit_bytes=MM_VMEM_LIMIT),
        )(dy, w)

    def call_dw(a, dy, w):
        t, k = a.shape
        n_sh, _, cc = w.shape

        def body(a_ref, dy_ref, o_ref):
            o_ref[...] = _tn(a_ref[...].astype(BF16), dy_ref[...].astype(BF16)).astype(BF16)

        return pl.pallas_call(
            body, name=name + "_dw", grid=(n_sh, k // tk_w),
            in_specs=[pl.BlockSpec((t, tk_w), lambda j, i: (0, i)),
                      pl.BlockSpec((t, cc), lambda j, i: (0, j))],
            out_specs=pl.BlockSpec((None, tk_w, cc), lambda j, i: (j, i, 0)),
            out_shape=jax.ShapeDtypeStruct(w.shape, BF16),
            compiler_params=pltpu.CompilerParams(dimension_semantics=("arbitrary", "arbitrary"),
                                                 vmem_limit_bytes=MM_VMEM_LIMIT),
        )(a, dy)

    @jax.custom_vjp
    def op(a, w):
        return call_fwd(a, w)

    def fwd(a, w):
        return op(a, w), (a, w)

    def bwd(res, dy):
        a, w = res
        return call_dx(dy, w), call_dw(a, dy, w)

    op.defvjp(fwd, bwd)
    return op


def _row_spec(arr, tb):
    return pl.BlockSpec((tb, arr.shape[1]), lambda i: (i, 0))


def _full_spec(arr):
    return pl.BlockSpec(arr.shape, lambda i: (0, 0))


def _make_rowwise(name, f, n_rows, n_params, out_cols, diff_rows):
    n_out = len(out_cols)

    def call_fwd(rows, params):
        t = rows[0].shape[0]

        def body(*refs):
            ins = [r[...] for r in refs[:n_rows + n_params]]
            outs = f(*ins)
            for o_ref, o in zip(refs[n_rows + n_params:], outs):
                o_ref[...] = o

        return pl.pallas_call(
            body, name=name + "_fwd", grid=(t // ROW_TILE,),
            in_specs=[_row_spec(a, ROW_TILE) for a in rows] + [_full_spec(p) for p in params],
            out_specs=[pl.BlockSpec((ROW_TILE, n), lambda i: (i, 0)) for n in out_cols],
            out_shape=[jax.ShapeDtypeStruct((t, n), F32) for n in out_cols],
            compiler_params=pltpu.CompilerParams(dimension_semantics=("arbitrary",),
                                                 vmem_limit_bytes=MM_VMEM_LIMIT),
        )(*rows, *params)

    def call_bwd(rows, params, cts):
        t = rows[0].shape[0]
        d_rows = [a for a, d in zip(rows, diff_rows) if d]
        n_in = n_rows + n_params + n_out

        def body(*refs):
            ins = [r[...] for r in refs[:n_rows + n_params]]
            ct = tuple(r[...] for r in refs[n_rows + n_params:n_in])
            _, vjp = jax.vjp(f, *ins)
            grads = vjp(ct)
            out_refs = refs[n_in:]
            g_rows = [g for g, d in zip(grads[:n_rows], diff_rows) if d]
            for o_ref, g in zip(out_refs[:len(g_rows)], g_rows):
                o_ref[...] = g
            p_refs = out_refs[len(g_rows):]

            if p_refs:
                @pl.when(pl.program_id(0) == 0)
                def _():
                    for p_ref in p_refs:
                        p_ref[...] = jnp.zeros_like(p_ref)

                for p_ref, g in zip(p_refs, grads[n_rows:]):
                    p_ref[...] += g

        return pl.pallas_call(
            body, name=name + "_bwd", grid=(t // ROW_TILE,),
            in_specs=[_row_spec(a, ROW_TILE) for a in rows] + [_full_spec(p) for p in params]
            + [_row_spec(c, ROW_TILE) for c in cts],
            out_specs=[_row_spec(a, ROW_TILE) for a in d_rows] + [_full_spec(p) for p in params],
            out_shape=[jax.ShapeDtypeStruct(a.shape, F32) for a in d_rows]
            + [jax.ShapeDtypeStruct(p.shape, F32) for p in params],
            compiler_params=pltpu.CompilerParams(dimension_semantics=("arbitrary",),
                                                 vmem_limit_bytes=MM_VMEM_LIMIT),
        )(*rows, *params, *cts)

    @jax.custom_vjp
    def op(*args):
        return tuple(call_fwd(args[:n_rows], args[n_rows:]))

    def fwd(*args):
        return op(*args), args

    def bwd(args, cts):
        rows, params = args[:n_rows], args[n_rows:]
        outs = call_bwd(rows, params, cts)
        it = iter(outs)
        g_rows = [next(it) if d else jnp.zeros_like(a) for a, d in zip(rows, diff_rows)]
        return tuple(g_rows) + tuple(it)

    op.defvjp(fwd, bwd)
    return op


def _rms(x, g, n):
    return x * lax.rsqrt(jnp.sum(x * x, axis=-1, keepdims=True) * (1.0 / n) + EPS) * g


def _f_pre_attn(x, g, scale, shift):
    return (_rms(x, g, D_MODEL) * (1.0 + scale) + shift,)


def _f_mla_a(cq, ckv, gq, gkv):
    return _rms(cq, gq, MLA_Q_RANK), _rms(ckv, gkv, MLA_KV_RANK)


@jax.custom_vjp
def _split_lanes(x):
    return tuple(x[:, i * LANES:(i + 1) * LANES] for i in range(x.shape[1] // LANES))


def _split_lanes_fwd(x):
    return _split_lanes(x), None


def _split_lanes_bwd(_, cts):
    return (jnp.concatenate(cts, axis=1),)


_split_lanes.defvjp(_split_lanes_fwd, _split_lanes_bwd)


def _f_mla_b(qall, kn_all, kr, kr_sw, cos, sin, gqn, gqr, gqr_sw, gkn, gkr, gkr_sw):
    q = _split_lanes(qall)
    kn = _split_lanes(kn_all)
    qn_o, qr_o, kn_o = [], [], []
    for h in range(MLA_HEADS):
        qn, qr, qs = q[h], q[MLA_HEADS + h], q[2 * MLA_HEADS + h]
        ss = jnp.sum(qn * qn, axis=-1, keepdims=True) + jnp.sum(qr * qr, axis=-1, keepdims=True)
        rs = lax.rsqrt(ss * (1.0 / MLA_QK) + EPS)
        qn_o.append(qn * rs * gqn)
        qr_o.append((qr * rs * gqr) * cos + (qs * rs * gqr_sw) * sin)
        kn_o.append(_rms(kn[h], gkn, MLA_NOPE))
    rs = lax.rsqrt(jnp.sum(kr * kr, axis=-1, keepdims=True) * (1.0 / MLA_ROPE) + EPS)
    kr_o = (kr * rs * gkr) * cos + (kr_sw * rs * gkr_sw) * sin
    return (jnp.concatenate(qn_o, axis=1), jnp.concatenate(qr_o, axis=1), jnp.concatenate(kn_o, axis=1), kr_o)


def _f_post_attn(o_sb, o_mla, g_sb, g_mla):
    return (jnp.concatenate([_rms(o_sb, g_sb, SB_WIDTH), _rms(o_mla, g_mla, SB_WIDTH)], axis=1),)


def _f_pre_ffn(x, attn, gate, g, scale, shift):
    x2 = x + gate * attn
    return x2, _rms(x2, g, D_MODEL) * (1.0 + scale) + shift


def _f_swiglu(gt, up):
    return (gt / (1.0 + jnp.exp(-gt)) * up,)


def _f_loss(x2, ffn, target, gate):
    err = x2 + gate * ffn - target
    return (jnp.sum(err * err, axis=-1, keepdims=True) * (1.0 / D_MODEL),)


def _rope_tables(pos_col, freqs, sign):
    t = pos_col.shape[0]

    def body(p_ref, f_ref, s_ref, cos_ref, sin_ref):
        ang = p_ref[...].astype(F32) * f_ref[...]
        live = jnp.abs(s_ref[...])
        cos_ref[...] = jnp.cos(ang) * live
        sin_ref[...] = jnp.sin(ang) * s_ref[...]

    return pl.pallas_call(
        body, name="rope_tables", grid=(t // ROW_TILE,),
        in_specs=[pl.BlockSpec((ROW_TILE, 1), lambda i: (i, 0)), _full_spec(freqs), _full_spec(sign)],
        out_specs=[pl.BlockSpec((ROW_TILE, LANES), lambda i: (i, 0))] * 2,
        out_shape=[jax.ShapeDtypeStruct((t, LANES), F32)] * 2,
    )(pos_col, freqs, sign)


def _hi_lo_dot(x, tri):
    hi = x.astype(BF16)
    lo = (x - hi.astype(F32)).astype(BF16)
    return (jnp.dot(hi, tri, preferred_element_type=F32) + jnp.dot(lo, tri, preferred_element_type=F32))


def _tri(cmp):
    r = lax.broadcasted_iota(jnp.int32, (ATT_BLK, ATT_BLK), 0)
    c = lax.broadcasted_iota(jnp.int32, (ATT_BLK, ATT_BLK), 1)
    return cmp(r, c).astype(BF16)


def _nt(a, b):
    return lax.dot_general(a, b, (((1,), (1,)), ((), ())), preferred_element_type=F32)


def _tn(a, b):
    return lax.dot_general(a, b, (((0,), (0,)), ((), ())), preferred_element_type=F32)


def _sb_logs(z):
    lb = jnp.minimum(z, 0.0) - jnp.log(1.0 + jnp.exp(-jnp.abs(z)))
    return lb, lb - z


def _sb_fwd(q, k, v):
    t = q.shape[0]
    nq = t // ATT_BLK
    scale = SB_HEAD_DIM ** -0.5

    def body(q_ref, k_ref, v_ref, o_ref, tot_ref):
        qi = pl.program_id(1)
        lane = lax.broadcasted_iota(jnp.int32, (ATT_BLK, LANES), 1)
        tri = _tri(lambda r, c: r > c)
        qv = q_ref[...] * scale
        heads = [(lane // SB_HEAD_DIM) == hh for hh in range(2)]
        qms = [jnp.where(mine, qv, 0.0).astype(BF16) for mine in heads]

        def blocks(kbs, carry, diagonal):
            acc = carry[0]
            nb = len(kbs)
            chains = [(b, hh) for b in range(nb) for hh in range(2)]
            offs = [pl.multiple_of(kb * ATT_BLK, ATT_BLK) for kb in kbs]
            kks = [k_ref[pl.ds(off, ATT_BLK), :].astype(BF16) for off in offs]
            v_blks = [v_ref[pl.ds(off, ATT_BLK), :] for off in offs]
            if any(diagonal):
                valid = (lax.broadcasted_iota(jnp.int32, (ATT_BLK, ATT_BLK), 1)
                         < lax.broadcasted_iota(jnp.int32, (ATT_BLK, ATT_BLK), 0))
            zs = {ch: _nt(qms[ch[1]], kks[ch[0]]) for ch in chains}
            vvs = {(b, hh): jnp.where(heads[hh], v_blks[b], 0.0).astype(BF16) for b, hh in chains}
            logs = {ch: _sb_logs(zs[ch]) for ch in chains}
            l1ms = {ch: jnp.where(valid, logs[ch][1], 0.0) if diagonal[ch[0]] else logs[ch][1] for ch in chains}
            run = {(0, hh): carry[1 + hh] for hh in range(2)}
            for b, hh in chains:
                run[(b + 1, hh)] = run[(b, hh)] + jnp.sum(l1ms[(b, hh)], axis=-1, keepdims=True)
            afters = {ch: _hi_lo_dot(l1ms[ch], tri) for ch in chains}
            ws = {ch: jnp.exp(logs[ch][0] + (afters[ch] + run[ch])) for ch in chains}
            ws = {ch: jnp.where(valid, ws[ch], 0.0) if diagonal[ch[0]] else ws[ch] for ch in chains}
            for ch in chains:
                acc = acc + jnp.dot(ws[ch].astype(BF16), vvs[ch], preferred_element_type=F32)
            return (acc, run[(nb, 0)], run[(nb, 1)])

        zero = jnp.zeros((ATT_BLK, 1), F32)
        init = (jnp.zeros((ATT_BLK, LANES), F32), zero, zero)
        carry = lax.cond(qi % 2 == 1, lambda cr: blocks([qi, qi - 1], cr, (True, False)),
                         lambda cr: blocks([qi], cr, (True,)), init)
        top = qi - 1 - qi % 2
        carry = lax.fori_loop(0, qi // 2, lambda pr, cr: blocks([top - 2 * pr, top - 1 - 2 * pr], cr, (False, False)),
                              carry)
        o_ref[...] = carry[0]
        for hh in range(2):
            tot_ref[:, hh * LANES:(hh + 1) * LANES] = jnp.broadcast_to(carry[1 + hh], (ATT_BLK, LANES))

    return pl.pallas_call(
        body, name="sb_attn_fwd", grid=(SB_HEADS // 2, nq),
        in_specs=[pl.BlockSpec((ATT_BLK, LANES), lambda p, i: (i, p)),
                  pl.BlockSpec((t, LANES), lambda p, i: (0, p)),
                  pl.BlockSpec((t, LANES), lambda p, i: (0, p))],
        out_specs=[pl.BlockSpec((ATT_BLK, LANES), lambda p, i: (i, p)),
                   pl.BlockSpec((ATT_BLK, 2 * LANES), lambda p, i: (i, p))],
        out_shape=[jax.ShapeDtypeStruct((t, SB_WIDTH), F32), jax.ShapeDtypeStruct((t, SB_HEADS * LANES), F32)],
        compiler_params=pltpu.CompilerParams(dimension_semantics=("arbitrary", "arbitrary")),
    )(q, k, v)


def _sb_bwd(q, k, v, tot, do):
    t = q.shape[0]
    nq = t // ATT_BLK
    scale = SB_HEAD_DIM ** -0.5

    def body(q_ref, k_ref, v_ref, tot_ref, do_ref, dq_ref, dk_ref, dv_ref):
        qi = pl.program_id(1)

        @pl.when(qi == 0)
        def _():
            dk_ref[...] = jnp.zeros_like(dk_ref)
            dv_ref[...] = jnp.zeros_like(dv_ref)

        lane = lax.broadcasted_iota(jnp.int32, (ATT_BLK, LANES), 1)
        tri_incl = _tri(lambda r, c: r <= c)
        tri_lt = _tri(lambda r, c: r < c)
        qv = q_ref[...] * scale
        dov = do_ref[...]
        heads = [(lane // SB_HEAD_DIM) == hh for hh in range(2)]
        qms = [jnp.where(mine, qv, 0.0).astype(BF16) for mine in heads]
        doms = [jnp.where(mine, dov, 0.0).astype(BF16) for mine in heads]
        tots = [tot_ref[:, hh * LANES:hh * LANES + 1] for hh in range(2)]

        def blocks(kbs, carry, diagonal):
            dq = carry[0]
            nb = len(kbs)
            chains = [(b, hh) for b in range(nb) for hh in range(2)]
            offs = [pl.multiple_of(kb * ATT_BLK, ATT_BLK) for kb in kbs]
            k_blks = [k_ref[pl.ds(off, ATT_BLK), :] for off in offs]
            vvs = [v_ref[pl.ds(off, ATT_BLK), :].astype(BF16) for off in offs]
            if any(diagonal):
                valid = (lax.broadcasted_iota(jnp.int32, (ATT_BLK, ATT_BLK), 1)
                         < lax.broadcasted_iota(jnp.int32, (ATT_BLK, ATT_BLK), 0))
            kks = {(b, hh): jnp.where(heads[hh], k_blks[b], 0.0).astype(BF16) for b, hh in chains}
            zs = {ch: _nt(qms[ch[1]], kks[ch]) for ch in chains}
            dws = {ch: _nt(doms[ch[1]], vvs[ch[0]]) for ch in chains}
            logs = {ch: _sb_logs(zs[ch]) for ch in chains}
            lbs = {ch: logs[ch][0] for ch in chains}
            l1m_all = {ch: logs[ch][1] for ch in chains}
            l1ms = {ch: jnp.where(valid, l1m_all[ch], 0.0) if diagonal[ch[0]] else l1m_all[ch] for ch in chains}
            pre, c_de = {}, {}
            for hh in range(2):
                pre[(0, hh)], c_de[(0, hh)] = carry[1 + 2 * hh], carry[2 + 2 * hh]
            for b, hh in chains:
                pre[(b + 1, hh)] = pre[(b, hh)] + jnp.sum(l1ms[(b, hh)], axis=-1, keepdims=True)
            prefix = {ch: _hi_lo_dot(l1ms[ch], tri_incl) for ch in chains}
            ws = {ch: jnp.exp(lbs[ch] + (tots[ch[1]] - (prefix[ch] + pre[ch]))) for ch in chains}
            ws = {ch: jnp.where(valid, ws[ch], 0.0) if diagonal[ch[0]] else ws[ch] for ch in chains}
            d_es = {ch: ws[ch] * dws[ch] for ch in chains}
            for b, hh in chains:
                c_de[(b + 1, hh)] = c_de[(b, hh)] + jnp.sum(d_es[(b, hh)], axis=-1, keepdims=True)
            dvs = [_tn(ws[(b, 0)].astype(BF16), doms[0]) + _tn(ws[(b, 1)].astype(BF16), doms[1]) for b in range(nb)]
            dl1ms = {ch: _hi_lo_dot(d_es[ch], tri_lt) + c_de[ch] for ch in chains}
            dzs = {ch: d_es[ch] * jnp.exp(l1m_all[ch]) - dl1ms[ch] * jnp.exp(lbs[ch]) for ch in chains}
            dzs = {ch: jnp.where(valid, dzs[ch], 0.0) if diagonal[ch[0]] else dzs[ch] for ch in chains}
            dzs = {ch: dzs[ch].astype(BF16) for ch in chains}
            for ch in chains:
                dq = dq + jnp.dot(dzs[ch], kks[ch], preferred_element_type=F32)
            for b in range(nb):
                dk_ref[pl.ds(offs[b], ATT_BLK), :] += _tn(dzs[(b, 0)], qms[0]) + _tn(dzs[(b, 1)], qms[1])
                dv_ref[pl.ds(offs[b], ATT_BLK), :] += dvs[b]
            return (dq, pre[(nb, 0)], c_de[(nb, 0)], pre[(nb, 1)], c_de[(nb, 1)])

        zero = jnp.zeros((ATT_BLK, 1), F32)
        carry = lax.fori_loop(0, qi // 2, lambda pr, cr: blocks([2 * pr, 2 * pr + 1], cr, (False, False)),
                              (jnp.zeros((ATT_BLK, LANES), F32), zero, zero, zero, zero))
        carry = lax.cond(qi % 2 == 1, lambda cr: blocks([qi - 1, qi], cr, (False, True)),
                         lambda cr: blocks([qi], cr, (True,)), carry)
        dq_ref[...] = carry[0] * scale

    return pl.pallas_call(
        body, name="sb_attn_bwd", grid=(SB_HEADS // 2, nq),
        in_specs=[pl.BlockSpec((ATT_BLK, LANES), lambda p, i: (i, p)),
                  pl.BlockSpec((t, LANES), lambda p, i: (0, p)),
                  pl.BlockSpec((t, LANES), lambda p, i: (0, p)),
                  pl.BlockSpec((ATT_BLK, 2 * LANES), lambda p, i: (i, p)),
                  pl.BlockSpec((ATT_BLK, LANES), lambda p, i: (i, p))],
        out_specs=[pl.BlockSpec((ATT_BLK, LANES), lambda p, i: (i, p)),
                   pl.BlockSpec((t, LANES), lambda p, i: (0, p)),
                   pl.BlockSpec((t, LANES), lambda p, i: (0, p))],
        out_shape=[jax.ShapeDtypeStruct((t, SB_WIDTH), F32)] * 3,
        compiler_params=pltpu.CompilerParams(dimension_semantics=("arbitrary", "arbitrary")),
    )(q, k, v, tot, do)


@jax.custom_vjp
def _sb_attention(q, k, v):
    return _sb_fwd(q, k, v)[0]


def _sb_attention_fwd(q, k, v):
    o, tot = _sb_fwd(q, k, v)
    return o, (q, k, v, tot)


def _sb_attention_bwd(res, do):
    return tuple(_sb_bwd(*res, do))


_sb_attention.defvjp(_sb_attention_fwd, _sb_attention_bwd)


def _mla_fwd(qn, qr, kn, kr, v):
    t = qn.shape[0]
    nq = t // ATT_BLK
    scale = MLA_QK ** -0.5

    def body(qn_ref, qr_ref, kn_ref, kr_ref, v_ref, o_ref, lse_ref):
        qi = pl.program_id(1)
        lanes = [slice(hh * LANES, (hh + 1) * LANES) for hh in range(2)]
        qnb = [qn_ref[:, sl].astype(BF16) for sl in lanes]
        qrb = [qr_ref[:, sl].astype(BF16) for sl in lanes]

        def blocks(kbs, carry, diagonal):
            nb = len(kbs)
            chains = [(b, hh) for b in range(nb) for hh in range(2)]
            offs = [pl.multiple_of(kb * ATT_BLK, ATT_BLK) for kb in kbs]
            krbs = [kr_ref[pl.ds(off, ATT_BLK), :].astype(BF16) for off in offs]
            accs, ms, ls = [carry[0], carry[3]], [carry[1], carry[4]], [carry[2], carry[5]]
            ss = {(b, hh): (_nt(qnb[hh], kn_ref[pl.ds(offs[b], ATT_BLK), lanes[hh]].astype(BF16))
                            + _nt(qrb[hh], krbs[b])) * scale for b, hh in chains}
            if any(diagonal):
                causal = (lax.broadcasted_iota(jnp.int32, (ATT_BLK, ATT_BLK), 1)
                          <= lax.broadcasted_iota(jnp.int32, (ATT_BLK, ATT_BLK), 0))
                ss = {ch: jnp.where(causal, ss[ch], -jnp.inf) if diagonal[ch[0]] else ss[ch] for ch in chains}
            m_new = list(ms)
            for b, hh in chains:
                m_new[hh] = jnp.maximum(m_new[hh], jnp.max(ss[(b, hh)], axis=-1, keepdims=True))
            ps = {(b, hh): jnp.exp(ss[(b, hh)] - m_new[hh]) for b, hh in chains}
            alphas = [jnp.exp(ms[hh] - m_new[hh]) for hh in range(2)]
            pvs = {(b, hh): jnp.dot(ps[(b, hh)].astype(BF16), v_ref[pl.ds(offs[b], ATT_BLK), lanes[hh]].astype(BF16),
                                    preferred_element_type=F32) for b, hh in chains}
            out = []
            for hh in range(2):
                acc, l = accs[hh] * alphas[hh], ls[hh] * alphas[hh]
                for b in range(nb):
                    acc, l = acc + pvs[(b, hh)], l + jnp.sum(ps[(b, hh)], axis=-1, keepdims=True)
                out += [acc, m_new[hh], l]
            return tuple(out)

        init = (jnp.zeros((ATT_BLK, LANES), F32), jnp.full((ATT_BLK, 1), -jnp.inf, F32), jnp.zeros((ATT_BLK, 1), F32))
        carry = lax.cond(qi % 2 == 1, lambda cr: blocks([qi, qi - 1], cr, (True, False)),
                         lambda cr: blocks([qi], cr, (True,)), init + init)
        carry = lax.fori_loop(0, qi // 2, lambda pr, cr: blocks([2 * pr, 2 * pr + 1], cr, (False, False)), carry)
        for hh in range(2):
            acc, m, l = carry[3 * hh:3 * hh + 3]
            o_ref[:, lanes[hh]] = acc / l
            lse_ref[:, lanes[hh]] = jnp.broadcast_to(m + jnp.log(l), (ATT_BLK, LANES))

    blk = pl.BlockSpec((ATT_BLK, 2 * LANES), lambda p, i: (i, p))
    full = pl.BlockSpec((t, 2 * LANES), lambda p, i: (0, p))
    return pl.pallas_call(
        body, name="mla_attn_fwd", grid=(MLA_HEADS // 2, nq),
        in_specs=[blk, blk, full, pl.BlockSpec((t, LANES), lambda p, i: (0, 0)), full],
        out_specs=[blk, blk],
        out_shape=[jax.ShapeDtypeStruct((t, MLA_HEADS * LANES), F32)] * 2,
        compiler_params=pltpu.CompilerParams(dimension_semantics=("arbitrary", "arbitrary")),
    )(qn, qr, kn, kr, v)


def _mla_bwd(qn, qr, kn, kr, v, o, lse, do):
    t = qn.shape[0]
    nq = t // ATT_BLK
    scale = MLA_QK ** -0.5

    def body(qn_ref, qr_ref, kn_ref, kr_ref, v_ref, o_ref, lse_ref, do_ref,
             dqn_ref, dqr_ref, dkn_ref, dkr_ref, dv_ref):
        pair = pl.program_id(0)
        qi = pl.program_id(1)

        @pl.when(qi == 0)
        def _():
            dkn_ref[...] = jnp.zeros_like(dkn_ref)
            dv_ref[...] = jnp.zeros_like(dv_ref)

        @pl.when((qi == 0) & (pair == 0))
        def _():
            dkr_ref[...] = jnp.zeros_like(dkr_ref)

        lanes = [slice(hh * LANES, (hh + 1) * LANES) for hh in range(2)]
        qnb = [qn_ref[:, sl].astype(BF16) for sl in lanes]
        qrb = [qr_ref[:, sl].astype(BF16) for sl in lanes]
        dob = [do_ref[:, sl].astype(BF16) for sl in lanes]
        delta = [jnp.sum(do_ref[:, sl] * o_ref[:, sl], axis=-1, keepdims=True) for sl in lanes]
        lse_v = [lse_ref[:, hh * LANES:hh * LANES + 1] for hh in range(2)]

        def blocks(kbs, carry, diagonal):
            nb = len(kbs)
            chains = [(b, hh) for b in range(nb) for hh in range(2)]
            offs = [pl.multiple_of(kb * ATT_BLK, ATT_BLK) for kb in kbs]
            krbs = [kr_ref[pl.ds(off, ATT_BLK), :].astype(BF16) for off in offs]
            knb = {(b, hh): kn_ref[pl.ds(offs[b], ATT_BLK), lanes[hh]].astype(BF16) for b, hh in chains}
            vb = {(b, hh): v_ref[pl.ds(offs[b], ATT_BLK), lanes[hh]].astype(BF16) for b, hh in chains}
            ss = {(b, hh): _nt(qnb[hh], knb[(b, hh)]) + _nt(qrb[hh], krbs[b]) for b, hh in chains}
            dps = {(b, hh): _nt(dob[hh], vb[(b, hh)]) for b, hh in chains}
            ps = {(b, hh): jnp.exp(ss[(b, hh)] * scale - lse_v[hh]) for b, hh in chains}
            if any(diagonal):
                causal = (lax.broadcasted_iota(jnp.int32, (ATT_BLK, ATT_BLK), 1)
                          <= lax.broadcasted_iota(jnp.int32, (ATT_BLK, ATT_BLK), 0))
                ps = {ch: jnp.where(causal, ps[ch], 0.0) if diagonal[ch[0]] else ps[ch] for ch in chains}
            dss = {(b, hh): (ps[(b, hh)] * (dps[(b, hh)] - delta[hh]) * scale).astype(BF16) for b, hh in chains}
            for b, hh in chains:
                dv_ref[pl.ds(offs[b], ATT_BLK), lanes[hh]] += _tn(ps[(b, hh)].astype(BF16), dob[hh])
            for b, hh in chains:
                dkn_ref[pl.ds(offs[b], ATT_BLK), lanes[hh]] += _tn(dss[(b, hh)], qnb[hh])
            for b in range(nb):
                dkr_ref[pl.ds(offs[b], ATT_BLK), :] += _tn(dss[(b, 0)], qrb[0]) + _tn(dss[(b, 1)], qrb[1])
            out = list(carry)
            for b, hh in chains:
                out[2 * hh] = out[2 * hh] + jnp.dot(dss[(b, hh)], knb[(b, hh)], preferred_element_type=F32)
                out[2 * hh + 1] = out[2 * hh + 1] + jnp.dot(dss[(b, hh)], krbs[b], preferred_element_type=F32)
            return tuple(out)

        zero = jnp.zeros((ATT_BLK, LANES), F32)
        carry = lax.fori_loop(0, qi // 2, lambda pr, cr: blocks([2 * pr, 2 * pr + 1], cr, (False, False)),
                              (zero, zero, zero, zero))
        carry = lax.cond(qi % 2 == 1, lambda cr: blocks([qi - 1, qi], cr, (False, True)),
                         lambda cr: blocks([qi], cr, (True,)), carry)
        for hh in range(2):
            dqn_ref[:, lanes[hh]] = carry[2 * hh]
            dqr_ref[:, lanes[hh]] = carry[2 * hh + 1]

    blk = pl.BlockSpec((ATT_BLK, 2 * LANES), lambda p, i: (i, p))
    full = pl.BlockSpec((t, 2 * LANES), lambda p, i: (0, p))
    shared = pl.BlockSpec((t, LANES), lambda p, i: (0, 0))
    wide = jax.ShapeDtypeStruct((t, MLA_HEADS * LANES), F32)
    return pl.pallas_call(
        body, name="mla_attn_bwd", grid=(MLA_HEADS // 2, nq),
        in_specs=[blk, blk, full, shared, full, blk, blk, blk],
        out_specs=[blk, blk, full, shared, full],
        out_shape=[wide, wide, wide, jax.ShapeDtypeStruct((t, LANES), F32), wide],
        compiler_params=pltpu.CompilerParams(dimension_semantics=("arbitrary", "arbitrary")),
    )(qn, qr, kn, kr, v, o, lse, do)


@jax.custom_vjp
def _mla_attention(qn, qr, kn, kr, v):
    return _mla_fwd(qn, qr, kn, kr, v)[0]


def _mla_attention_fwd(qn, qr, kn, kr, v):
    o, lse = _mla_fwd(qn, qr, kn, kr, v)
    return o, (qn, qr, kn, kr, v, o, lse)


def _mla_attention_bwd(res, do):
    return tuple(_mla_bwd(*res, do))


_mla_attention.defvjp(_mla_attention_fwd, _mla_attention_bwd)


def _ffn_in(h, wg, wu):
    t, k = h.shape
    n_sh, _, cc = wg.shape

    def body(h_ref, wg_ref, wu_ref, g_ref, u_ref, a_ref):
        hb = h_ref[...].astype(BF16)
        for j in range(n_sh):
            cols = slice(j * cc, (j + 1) * cc)
            g = jnp.dot(hb, wg_ref[j], preferred_element_type=F32)
            u = jnp.dot(hb, wu_ref[j], preferred_element_type=F32)
            g_ref[:, cols] = g
            u_ref[:, cols] = u
            a_ref[:, cols] = _f_swiglu(g, u)[0].astype(BF16)

    w_spec = pl.BlockSpec((n_sh, k, cc), lambda i: (0, 0, 0))
    o_spec = pl.BlockSpec((ROW_TILE, n_sh * cc), lambda i: (i, 0))
    wide = (t, n_sh * cc)
    return pl.pallas_call(
        body, name="ffn_in_fwd", grid=(t // ROW_TILE,),
        in_specs=[pl.BlockSpec((ROW_TILE, k), lambda i: (i, 0)), w_spec, w_spec],
        out_specs=[o_spec, o_spec, o_spec],
        out_shape=[jax.ShapeDtypeStruct(wide, F32), jax.ShapeDtypeStruct(wide, F32), jax.ShapeDtypeStruct(wide, BF16)],
        compiler_params=pltpu.CompilerParams(dimension_semantics=("arbitrary",), vmem_limit_bytes=MM_VMEM_LIMIT),
    )(h, wg, wu)


def _ffn_mid_bwd(dy, wd, g, u):
    t, n = dy.shape
    n_sh, cc, _ = wd.shape

    def body(dy_ref, wd_ref, g_ref, u_ref, dg_ref, du_ref):
        d_act = _nt(dy_ref[...].astype(BF16), wd_ref[...])
        _, vjp = jax.vjp(_f_swiglu, g_ref[...], u_ref[...])
        dg, du = vjp((d_act,))
        dg_ref[...] = dg.astype(BF16)
        du_ref[...] = du.astype(BF16)

    blk = pl.BlockSpec((ROW_TILE, cc), lambda j, i: (i, j))
    wide = jax.ShapeDtypeStruct((t, n_sh * cc), BF16)
    return pl.pallas_call(
        body, name="ffn_mid_bwd", grid=(n_sh, t // ROW_TILE),
        in_specs=[pl.BlockSpec((ROW_TILE, n), lambda j, i: (i, 0)),
                  pl.BlockSpec((None, cc, n), lambda j, i: (j, 0, 0)), blk, blk],
        out_specs=[blk, blk], out_shape=[wide, wide],
        compiler_params=pltpu.CompilerParams(dimension_semantics=("arbitrary", "arbitrary"),
                                             vmem_limit_bytes=MM_VMEM_LIMIT),
    )(dy, wd, g, u)


def _ffn_dh(dg, du, wg, wu):
    t = dg.shape[0]
    n_sh, k, cc = wg.shape

    def body(dg_ref, du_ref, wg_ref, wu_ref, o_ref):
        acc = jnp.zeros((ROW_TILE, k), F32)
        for j in range(n_sh):
            cols = slice(j * cc, (j + 1) * cc)
            acc = acc + _nt(dg_ref[:, cols], wg_ref[j]) + _nt(du_ref[:, cols], wu_ref[j])
        o_ref[...] = acc

    blk = pl.BlockSpec((ROW_TILE, n_sh * cc), lambda i: (i, 0))
    w_spec = pl.BlockSpec((n_sh, k, cc), lambda i: (0, 0, 0))
    return pl.pallas_call(
        body, name="ffn_dh", grid=(t // ROW_TILE,),
        in_specs=[blk, blk, w_spec, w_spec],
        out_specs=pl.BlockSpec((ROW_TILE, k), lambda i: (i, 0)),
        out_shape=jax.ShapeDtypeStruct((t, k), F32),
        compiler_params=pltpu.CompilerParams(dimension_semantics=("arbitrary",), vmem_limit_bytes=MM_VMEM_LIMIT),
    )(dg, du, wg, wu)


def _ffn_dw_in(h, dy, n_sh, name):
    t, k = h.shape
    cc = dy.shape[1] // n_sh
    tk = 512

    def body(h_ref, dy_ref, o_ref):
        o_ref[...] = _tn(h_ref[...].astype(BF16), dy_ref[...]).astype(BF16)

    return pl.pallas_call(
        body, name=name, grid=(n_sh, k // tk),
        in_specs=[pl.BlockSpec((t, tk), lambda j, i: (0, i)), pl.BlockSpec((t, cc), lambda j, i: (0, j))],
        out_specs=pl.BlockSpec((None, tk, cc), lambda j, i: (j, i, 0)),
        out_shape=jax.ShapeDtypeStruct((n_sh, k, cc), BF16),
        compiler_params=pltpu.CompilerParams(dimension_semantics=("arbitrary", "arbitrary"),
                                             vmem_limit_bytes=MM_VMEM_LIMIT),
    )(h, dy)


@jax.custom_vjp
def _ffn_block(h, wg, wu, wd):
    act = _ffn_in(h, wg, wu)[2]
    return _mm(act, wd.reshape(-1, wd.shape[2]), "nn", "ffn_down_fwd", ROW_TILE, wd.shape[2])


def _ffn_block_fwd(h, wg, wu, wd):
    g, u, act = _ffn_in(h, wg, wu)
    y = _mm(act, wd.reshape(-1, wd.shape[2]), "nn", "ffn_down_fwd", ROW_TILE, wd.shape[2])
    return y, (h, wg, wu, wd, g, u, act)


def _ffn_block_bwd(res, dy):
    h, wg, wu, wd, g, u, act = res
    dg, du = _ffn_mid_bwd(dy, wd, g, u)
    dh = _ffn_dh(dg, du, wg, wu)
    n_sh = wg.shape[0]
    dwg = _ffn_dw_in(h, dg, n_sh, "ffn_gate_dw")
    dwu = _ffn_dw_in(h, du, n_sh, "ffn_up_dw")
    dwd = _mm(act, dy, "tn", "ffn_down_dw", 256, wd.shape[2], out_dtype=BF16).reshape(wd.shape)
    return dh, dwg, dwu, dwd


_ffn_block.defvjp(_ffn_block_fwd, _ffn_block_bwd)


def _split_cols(x, cuts):
    cuts = tuple(cuts)

    @jax.custom_vjp
    def op(x):
        return tuple(x[:, a:b] for a, b in zip((0,) + cuts, cuts + (x.shape[1],)))

    def fwd(x):
        return op(x), None

    def bwd(_, cts):
        return (jnp.concatenate(cts, axis=1),)

    op.defvjp(fwd, bwd)
    return op(x)


def _swap_halves(w):
    half = w.shape[-1] // 2
    return jnp.concatenate([w[..., half:], w[..., :half]], axis=-1)


def _pad_lanes(w):
    return jnp.concatenate([w, jnp.zeros(w.shape[:-1] + (LANES - w.shape[-1],), w.dtype)], axis=-1)


def _join_cols(shards):
    return shards.transpose(1, 0, 2).reshape(shards.shape[1], -1)


def _mod_parts(mod):
    return [mod[:, i * D_MODEL:(i + 1) * D_MODEL] for i in range(N_MOD)]


def _local_loss(x, mod, p, cos, sin, target):
    return _ffn_stage(x, _mixing_stage(x, mod, p, cos, sin), mod, p, target)


def _mixing_stage(x, mod, p, cos, sin):
    shift1, scale1 = _mod_parts(mod)[:2]

    w_in = _join_cols(p["w_in"])
    k_rope_w = w_in[:, 2176:2240]
    w_in_ext = jnp.concatenate([w_in[:, :2176], _pad_lanes(k_rope_w), _pad_lanes(_swap_halves(k_rope_w)),
                                jnp.zeros((D_MODEL, LANES), w_in.dtype)], axis=1)
    (h1,) = _make_rowwise("pre_attn", _f_pre_attn, 1, 3, [D_MODEL], [True])(x, p["norm_attn"], scale1, shift1)
    proj = _make_linear("in_proj", 512, 640)(h1, w_in_ext)
    q_sb, k_sb, v_sb, cq, ckv, kr, kr_sw, _ = _split_cols(proj, (512, 1024, 1536, 1920, 2176, 2304, 2432))

    o_sb = _sb_attention(q_sb, k_sb, v_sb)

    wq = _join_cols(p["w_q_up"]).reshape(MLA_Q_RANK, MLA_HEADS, MLA_QK)
    wq_n, wq_r = wq[:, :, :MLA_NOPE], wq[:, :, MLA_NOPE:]
    w_q_ext = jnp.concatenate([wq_n.reshape(MLA_Q_RANK, -1), _pad_lanes(wq_r).reshape(MLA_Q_RANK, -1),
                               _pad_lanes(_swap_halves(wq_r)).reshape(MLA_Q_RANK, -1)], axis=1)
    wkv = _join_cols(p["w_kv_up"]).reshape(MLA_KV_RANK, MLA_HEADS, MLA_NOPE + MLA_V)
    w_kv_ext = jnp.concatenate([wkv[:, :, :MLA_NOPE].reshape(MLA_KV_RANK, -1),
                                wkv[:, :, MLA_NOPE:].reshape(MLA_KV_RANK, -1)], axis=1)
    cqn, ckvn = _make_rowwise("mla_a", _f_mla_a, 2, 2, [MLA_Q_RANK, MLA_KV_RANK], [True, True])(
        cq, ckv, p["q_a_norm"], p["kv_a_norm"])
    qall = _make_linear("q_up", 384, 768)(cqn, w_q_ext)
    kvall = _make_linear("kv_up", 256, 1024)(ckvn, w_kv_ext)
    kn_all, v_mla = _split_cols(kvall, (512,))
    gq = p["q_norm"]
    gkr = p["k_rope_norm"]
    qn, qr, kn, krr = _make_rowwise("mla_b", _f_mla_b, 6, 6, [512, 512, 512, LANES],
                                    [True, True, True, True, False, False])(
        qall, kn_all, kr, kr_sw, cos, sin,
        gq[:, :MLA_NOPE], _pad_lanes(gq[:, MLA_NOPE:]), _pad_lanes(_swap_halves(gq[:, MLA_NOPE:])),
        p["k_nope_norm"], _pad_lanes(gkr), _pad_lanes(_swap_halves(gkr)))
    o_mla = _mla_attention(qn, qr, kn, krr, v_mla)

    (mixed,) = _make_rowwise("post_attn", _f_post_attn, 2, 2, [D_MODEL], [True, True])(
        o_sb, o_mla, p["out_norm_sb"], p["out_norm_mla"])
    return mixed


def _ffn_stage(x, mixed, mod, p, target):
    _, _, gate1, shift2, scale2, gate2 = _mod_parts(mod)
    attn = _make_linear("out_proj", 512, 512)(mixed, p["w_out"].reshape(D_MODEL, D_MODEL))

    x2, h2 = _make_rowwise("pre_ffn", _f_pre_ffn, 2, 4, [D_MODEL, D_MODEL], [True, True])(
        x, attn, gate1, p["norm_ffn"], scale2, shift2)
    ffn = _ffn_block(h2, p["w_gate"], p["w_up"], p["w_down"])
    (row_loss,) = _make_rowwise("loss", _f_loss, 3, 1, [1], [True, True, False])(x2, ffn, target, gate2)
    return 0.5 * jnp.sum(row_loss)


def _my_place():
    return lax.axis_index("x"), lax.axis_index("y"), lax.axis_index("c")


def _all_gather_small(block, name):
    m_per, n = block.shape

    def body(x_ref, out_ref, send_sems, recv_sems, local_sem):
        x, y, c = _my_place()
        me, sibling = (x, y, c), (x, y, 1 - c)
        chips = [(1 - x, y), (x, 1 - y), (1 - x, 1 - y)]

        def rows(px, py, pc):
            return out_ref.at[pl.ds((4 * px + 2 * py + pc) * m_per, m_per), :]

        def copy(k, blk, to, src=None):
            return pltpu.make_async_remote_copy(
                src_ref=rows(*blk) if src is None else src, dst_ref=rows(*blk),
                send_sem=send_sems.at[k], recv_sem=recv_sems.at[k], device_id=to, device_id_type=MESH)

        mine = pltpu.make_async_copy(x_ref, rows(*me), local_sem)
        mine.start()
        first = [copy(0, me, sibling, src=x_ref)]
        first += [copy(1 + j, me, (*chip, c), src=x_ref) for j, chip in enumerate(chips)]
        for cp in first:
            cp.start()
        passed = [copy(4 + j, (*chip, c), sibling) for j, chip in enumerate(chips)]
        for j, chip in enumerate(chips):
            copy(1 + j, (*chip, c), me).wait_recv()
            passed[j].start()
        copy(0, sibling, me).wait_recv()
        for j, chip in enumerate(chips):
            copy(4 + j, (*chip, 1 - c), me).wait_recv()
        for cp in first + passed:
            cp.wait_send()
        mine.wait()

    return pl.pallas_call(
        body, name=name,
        out_shape=jax.ShapeDtypeStruct((N_DEV * m_per, n), block.dtype),
        in_specs=[pl.BlockSpec(memory_space=pltpu.VMEM)],
        out_specs=pl.BlockSpec(memory_space=pltpu.VMEM),
        scratch_shapes=[pltpu.SemaphoreType.DMA((7,)), pltpu.SemaphoreType.DMA((7,)), pltpu.SemaphoreType.DMA],
    )(block)


EARLY = ("w_in", "w_q_up", "w_kv_up")
LATE = ("w_out", "w_gate", "w_up", "w_down")
BIG = EARLY + LATE
TRANSPOSED_UPDATE = ("w_in", "w_gate", "w_up")
HALF_AXIS = {"w_in": 0, "w_q_up": 0, "w_kv_up": 0, "w_out": 0, "w_gate": 0, "w_up": 0, "w_down": 1}


def _half(ref, h, axis, lead=()):
    trail = ref.shape[len(lead):]
    idx = list(lead) + [slice(None)] * len(trail)
    at = len(trail) - 2 + axis
    n2 = trail[at] // 2
    idx[len(lead) + at] = pl.ds(h * n2, n2)
    return ref.at[tuple(idx)]


def _half_shape(shape, axis):
    shape = list(shape)
    shape[len(shape) - 2 + axis] //= 2
    return tuple(shape)


def _remote(src, dst, send_sems, recv_sems, k, to):
    return pltpu.make_async_remote_copy(src_ref=src, dst_ref=dst, send_sem=send_sems.at[k],
                                        recv_sem=recv_sems.at[k], device_id=to, device_id_type=MESH)


def _gather_weights(names, shards, after):
    n_w = len(shards)
    axes = [HALF_AXIS[n] for n in names]

    def body(*refs):
        w_refs, out_refs, token = refs[:n_w], refs[n_w + 1:2 * n_w + 1], refs[2 * n_w + 1]
        send_sems, recv_sems, local_sems = refs[2 * n_w + 2:]
        token[...] = jnp.zeros_like(token)
        x, y, c = _my_place()
        sibling = (x, y, 1 - c)
        chips = [(1 - x, y), (x, 1 - y), (1 - x, 1 - y)]
        me = 2 * x + y
        mine =[pltpu.make_async_copy(w, o.at[me], local_sems.at[i]) for i, (w, o) in enumerate(zip(w_refs, out_refs))]
        for cp in mine:
            cp.start()
        first = [_remote(_half(w_refs[i], c, axes[i]), _half(out_refs[i], c, axes[i], (me,)),
                         send_sems, recv_sems, 6 * i + j, (*chip, c))
                 for i in range(n_w) for j, chip in enumerate(chips)]
        for cp in first:
            cp.start()
        passed = []
        for j, (cx, cy) in enumerate(chips):
            for i in range(n_w):
                blk = _half(out_refs[i], c, axes[i], (2 * cx + cy,))
                _remote(blk, blk, send_sems, recv_sems, 6 * i + j, (cx, cy, c)).wait_recv()
                cp = _remote(blk, blk, send_sems, recv_sems, 6 * i + 3 + j, sibling)
                cp.start()
                passed.append(cp)
        for j, (cx, cy) in enumerate(chips):
            for i in range(n_w):
                blk = _half(out_refs[i], 1 - c, axes[i], (2 * cx + cy,))
                _remote(blk, blk, send_sems, recv_sems, 6 * i + 3 + j, sibling).wait_recv()
        for cp in first + passed:
            cp.wait_send()
        for cp in mine:
            cp.wait()

    outs = pl.pallas_call(
        body, name="gather_weights",
        out_shape=[jax.ShapeDtypeStruct((N_CHIPS,) + s.shape, s.dtype) for s in shards]
        + [jax.ShapeDtypeStruct((8, LANES), F32)],
        in_specs=[ANY] * (n_w + 1), out_specs=[ANY] * n_w + [pl.BlockSpec(memory_space=pltpu.VMEM)],
        scratch_shapes=[pltpu.SemaphoreType.DMA((6 * n_w,)), pltpu.SemaphoreType.DMA((6 * n_w,)),
                        pltpu.SemaphoreType.DMA((n_w,))],
    )(*shards, after)
    return outs[:n_w], outs[n_w]


def _pair_exchange(names, grads, call_name):
    n_w = len(grads)
    axes = [HALF_AXIS[n] for n in names]

    def body(*refs):
        g_refs, t_refs = refs[:n_w], refs[n_w:2 * n_w]
        send_sems, recv_sems = refs[2 * n_w:]
        x, y, c = _my_place()
        sends = [_remote(_half(g_refs[i], 1 - c, axes[i]), t_refs[i], send_sems, recv_sems, i, (x, y, 1 - c))
                 for i in range(n_w)]
        for cp in sends:
            cp.start()
        for cp in sends:
            cp.wait_recv()
        for cp in sends:
            cp.wait_send()

    return pl.pallas_call(
        body, name=call_name,
        out_shape=[jax.ShapeDtypeStruct(_half_shape(g.shape, a), g.dtype) for g, a in zip(grads, axes)],
        in_specs=[ANY] * n_w, out_specs=[ANY] * n_w,
        scratch_shapes=[pltpu.SemaphoreType.DMA((n_w,)), pltpu.SemaphoreType.DMA((n_w,))],
    )(*grads)


def _chip_scatter(pair_sums):
    n_w = len(pair_sums)

    def body(*refs):
        s_refs, p_refs = refs[:n_w], refs[n_w:2 * n_w]
        send_sems, recv_sems = refs[2 * n_w:]
        x, y, c = _my_place()
        chips = [(1 - x, y), (x, 1 - y), (1 - x, 1 - y)]
        sends = [_remote(s_refs[i].at[2 * cx + cy], p_refs[i].at[j], send_sems, recv_sems, 3 * i + j, (cx, cy, c))
                 for i in range(n_w) for j, (cx, cy) in enumerate(chips)]
        for cp in sends:
            cp.start()
        for cp in sends:
            cp.wait_recv()
        for cp in sends:
            cp.wait_send()

    return pl.pallas_call(
        body, name="grad_chip_scatter",
        out_shape=[jax.ShapeDtypeStruct((N_CHIPS - 1,) + s.shape[1:], s.dtype) for s in pair_sums],
        in_specs=[ANY] * n_w, out_specs=[ANY] * n_w,
        scratch_shapes=[pltpu.SemaphoreType.DMA((3 * n_w,)), pltpu.SemaphoreType.DMA((3 * n_w,))],
    )(*pair_sums)


def _sibling_join(halves, name, after):
    n_w = len(halves)

    def body(*refs):
        s_refs, j_refs = refs[:n_w], refs[n_w + 1:2 * n_w + 1]
        send_sems, recv_sems = refs[2 * n_w + 1:]
        x, y, c = _my_place()
        sends = [_remote(s_refs[i], j_refs[i], send_sems, recv_sems, i, (x, y, 1 - c)) for i in range(n_w)]
        for cp in sends:
            cp.start()
        for cp in sends:
            cp.wait_recv()
        for cp in sends:
            cp.wait_send()

    return pl.pallas_call(
        body, name=name,
        out_shape=[jax.ShapeDtypeStruct(s.shape, s.dtype) for s in halves],
        in_specs=[ANY] * (n_w + 1), out_specs=[ANY] * n_w,
        scratch_shapes=[pltpu.SemaphoreType.DMA((n_w,)), pltpu.SemaphoreType.DMA((n_w,))],
    )(*halves, after)


HBM_SPEC = pl.BlockSpec(memory_space=pltpu.HBM)
SEM_SPEC = pl.BlockSpec(memory_space=pltpu.SEMAPHORE)
DATAFLOW = pltpu.SideEffectType.DATAFLOW_SIDE_EFFECTING


def _in_hbm(a):
    return pltpu.with_memory_space_constraint(a, pltpu.HBM)


def _exchange_start(name, srcs, lands, plan, n_copies, after):
    n = len(srcs)

    def body(*refs):
        src_refs, land_refs = refs[:n], refs[n:2 * n]
        send_sems, recv_sems = refs[2 * n + 1], refs[2 * n + 2]
        token = refs[-1]
        for k, (src, dst, to) in enumerate(plan(src_refs, land_refs)):
            _remote(src, dst, send_sems, recv_sems, k, to).start()
        token[...] = jnp.zeros_like(token)

    outs = pl.pallas_call(
        body, name=name,
        out_shape=(pltpu.SemaphoreType.DMA((n_copies,)), pltpu.SemaphoreType.DMA((n_copies,)),
                   *[pltpu.HBM(a.shape, a.dtype) for a in srcs], *[pltpu.HBM(a.shape, a.dtype) for a in lands],
                   jax.ShapeDtypeStruct((8, LANES), F32)),
        in_specs=[HBM_SPEC] * (2 * n) + [ANY],
        out_specs=(SEM_SPEC, SEM_SPEC, *[HBM_SPEC] * (2 * n), pl.BlockSpec(memory_space=pltpu.VMEM)),
        input_output_aliases={i: 2 + i for i in range(2 * n)},
        compiler_params=pltpu.CompilerParams(has_side_effects=DATAFLOW),
    )(*[_in_hbm(a) for a in srcs], *[_in_hbm(a) for a in lands], after)
    return outs[0], outs[1], outs[2:2 + n], outs[2 + n:2 + 2 * n], outs[-1]


def _exchange_wait(name, started, plan, after):
    send_sems, recv_sems, srcs, lands, _ = started
    n = len(srcs)

    def body(*refs):
        src_refs, land_refs = refs[:n], refs[n:2 * n]
        s_sems, r_sems = refs[2 * n], refs[2 * n + 1]
        for k, (src, dst, to) in enumerate(plan(src_refs, land_refs)):
            cp = _remote(src, dst, s_sems, r_sems, k, to)
            cp.wait_send()
            cp.wait_recv()

    outs = pl.pallas_call(
        body, name=name,
        out_shape=tuple(pltpu.HBM(a.shape, a.dtype) for a in list(srcs) + list(lands)),
        in_specs=[HBM_SPEC] * (2 * n) + [SEM_SPEC, SEM_SPEC, ANY],
        out_specs=tuple([HBM_SPEC] * (2 * n)),
        input_output_aliases={i: i for i in range(2 * n)},
        compiler_params=pltpu.CompilerParams(has_side_effects=DATAFLOW),
    )(*srcs, *lands, send_sems, recv_sems, after)
    return outs[:n], outs[n:]


def _late_gather_plan(src_refs, land_refs):
    x, y, c = _my_place()
    chips = [(1 - x, y), (x, 1 - y), (1 - x, 1 - y)]
    return [(src, land.at[2 * x + y], (cx, cy, c)) for src, land in zip(src_refs, land_refs) for cx, cy in chips]


def _late_scatter_plan(src_refs, land_refs):
    x, y, c = _my_place()
    chips = [(1 - x, y), (x, 1 - y), (1 - x, 1 - y)]
    return [(src.at[2 * cx + cy], land.at[j], (cx, cy, c))
            for src, land in zip(src_refs, land_refs) for j, (cx, cy) in enumerate(chips)]


def _row_tile(rows, mult=16, limit=ROW_TILE):
    return max(d for d in range(mult, limit + 1, mult) if rows % d == 0)


def _pair_sum(place, g, theirs, axis, name):
    nj, rr, cc = theirs.shape
    tr = _row_tile(rr, limit=1024)
    nb = rr // tr
    if axis == 0:
        g_map = lambda j, i, pr: (j, pr[0] * nb + i, 0)
    else:
        g_map = lambda j, i, pr: (j, i, pr[0])

    def body(pr, g_ref, t_ref, o_ref):
        o_ref[...] = (g_ref[...].astype(F32) + t_ref[...].astype(F32)).astype(BF16)

    spec = pl.BlockSpec((None, tr, cc), lambda j, i, pr: (j, i, 0))
    return pl.pallas_call(
        body, name=name,
        grid_spec=pltpu.PrefetchScalarGridSpec(
            num_scalar_prefetch=1, grid=(nj, nb),
            in_specs=[pl.BlockSpec((None, tr, cc), g_map), spec], out_specs=spec),
        out_shape=jax.ShapeDtypeStruct(theirs.shape, BF16))(place, g, theirs)


def _chip_sum(place, pair_sums, parts, name, transposed):
    _, rr, cc = parts.shape
    tr = _row_tile(rr, LANES) if transposed else _row_tile(rr, limit=1024)

    def body(pr, h_ref, p_ref, o_ref):
        acc = p_ref[0].astype(F32)
        for j in range(1, N_CHIPS - 1):
            acc = acc + p_ref[j].astype(F32)
        acc = acc + h_ref[...].astype(F32)
        o_ref[...] = (acc.T if transposed else acc).astype(BF16)

    out_spec = pl.BlockSpec((cc, tr), lambda i, pr: (0, i)) if transposed else pl.BlockSpec((tr, cc), lambda i, pr: (i, 0))
    return pl.pallas_call(
        body, name=name,
        grid_spec=pltpu.PrefetchScalarGridSpec(
            num_scalar_prefetch=1, grid=(rr // tr,),
            in_specs=[pl.BlockSpec((None, tr, cc), lambda i, pr: (pr[1], i, 0)),
                      pl.BlockSpec((N_CHIPS - 1, tr, cc), lambda i, pr: (0, i, 0))],
            out_specs=out_spec),
        out_shape=jax.ShapeDtypeStruct((cc, rr) if transposed else (rr, cc), BF16))(place, pair_sums, parts)


def _silu(v):
    return v / (1.0 + jnp.exp(-v))


def _ada_fwd(c_all, w_shard, b_shard):
    def body(c_ref, w_ref, b_ref, o_ref):
        o_ref[...] = jnp.dot(_silu(c_ref[...]), w_ref[...], precision=lax.Precision.HIGHEST,
                             preferred_element_type=F32) + b_ref[...]

    return pl.pallas_call(body, name="ada_fwd", out_shape=jax.ShapeDtypeStruct((c_all.shape[0], w_shard.shape[1]), F32),
                          compiler_params=pltpu.CompilerParams(vmem_limit_bytes=MM_VMEM_LIMIT))(c_all, w_shard, b_shard)


def _ada_bwd(c_all, dmod_cols):
    def body(c_ref, d_ref, o_ref):
        o_ref[...] = lax.dot_general(_silu(c_ref[...]), d_ref[...], (((0,), (0,)), ((), ())),
                                     precision=lax.Precision.HIGHEST, preferred_element_type=F32)

    return pl.pallas_call(body, name="ada_bwd", out_shape=jax.ShapeDtypeStruct((c_all.shape[1], dmod_cols.shape[1]), F32),
                          compiler_params=pltpu.CompilerParams(vmem_limit_bytes=MM_VMEM_LIMIT))(c_all, dmod_cols)


def _adamw_math(w, g, m, v):
    m = ADAM_B1 * m + (1.0 - ADAM_B1) * g
    v = ADAM_B2 * v + (1.0 - ADAM_B2) * (g * g)
    m_hat = m / (1.0 - ADAM_B1 ** ADAM_STEP)
    v_hat = v / (1.0 - ADAM_B2 ** ADAM_STEP)
    delta = -ADAM_LR * (m_hat / (jnp.sqrt(v_hat) + ADAM_EPS) + ADAM_WD * w)
    return delta, m, v


def _adamw(w, g, m, v, name):
    r, ccols = w.shape
    tr = max(d for d in range(8, ROW_TILE + 1, 8) if r % d == 0)
    spec = pl.BlockSpec((tr, ccols), lambda i: (i, 0))

    def body(w_ref, g_ref, m_ref, v_ref, d_ref, nm_ref, nv_ref):
        d_ref[...], nm_ref[...], nv_ref[...] = _adamw_math(w_ref[...], g_ref[...], m_ref[...], v_ref[...])

    return pl.pallas_call(body, name=name, grid=(r // tr,), in_specs=[spec] * 4, out_specs=[spec] * 3,
                          out_shape=[jax.ShapeDtypeStruct(w.shape, F32)] * 3,
                          compiler_params=pltpu.CompilerParams(vmem_limit_bytes=MM_VMEM_LIMIT))(w, g, m, v)


def _adamw_small(w, g_all, m, v):
    def body(w_ref, g_ref, m_ref, v_ref, gs_ref, d_ref, nm_ref, nv_ref):
        g = g_ref[0]
        for d in range(1, N_DEV):
            g = g + g_ref[d]
        gs_ref[...] = g
        d_ref[...], nm_ref[...], nv_ref[...] = _adamw_math(w_ref[...], g, m_ref[...], v_ref[...])

    return pl.pallas_call(body, name="adamw_small", out_shape=[jax.ShapeDtypeStruct(w.shape, F32)] * 4)(w, g_all, m, v)


def _adamw_halves(place, w, own, sib, m, v, axis, name):
    r, cc = w.shape
    if axis == 0:
        rows, gc = own.shape[0], own.shape[1]
        tr = _row_tile(rows)
        nb = rows // tr
        w_spec = pl.BlockSpec((tr, cc), lambda h, i, pr: (h * nb + i, 0))
        g_spec = pl.BlockSpec((tr, gc), lambda h, i, pr: (i, 0))
    else:
        tr = _row_tile(r)
        nb = r // tr
        gc = own.shape[1]
        w_spec = pl.BlockSpec((tr, gc), lambda h, i, pr: (i, h))
        g_spec = pl.BlockSpec((tr, gc), lambda h, i, pr: (i, 0))
    wc = w_spec.block_shape[1]

    def body(pr, w_ref, o_ref, s_ref, m_ref, v_ref, g_ref, d_ref, nm_ref, nv_ref):
        g = jnp.where(pl.program_id(0) == pr[0], o_ref[...], s_ref[...]).astype(F32)[:, :wc]
        g_ref[...] = g
        d_ref[...], nm_ref[...], nv_ref[...] = _adamw_math(w_ref[...], g, m_ref[...], v_ref[...])

    return pl.pallas_call(
        body, name=name,
        grid_spec=pltpu.PrefetchScalarGridSpec(
            num_scalar_prefetch=1, grid=(2, nb),
            in_specs=[w_spec, g_spec, g_spec, w_spec, w_spec], out_specs=[w_spec] * 4),
        out_shape=[jax.ShapeDtypeStruct(w.shape, F32)] * 4,
        compiler_params=pltpu.CompilerParams(vmem_limit_bytes=MM_VMEM_LIMIT))(place, w, own, sib, m, v)


SMALL = ("b_ada", "norm_attn", "norm_ffn", "q_a_norm", "kv_a_norm", "q_norm", "k_nope_norm", "k_rope_norm",
         "out_norm_sb", "out_norm_mla")
WEIGHTS = ("w_ada", "b_ada", "norm_attn", "norm_ffn", "w_in", "q_a_norm", "w_q_up", "kv_a_norm", "w_kv_up",
           "q_norm", "k_nope_norm", "k_rope_norm", "out_norm_sb", "out_norm_mla", "w_out", "w_gate", "w_up",
           "w_down")


def kernel(x, c, positions, w_ada, b_ada, norm_attn, norm_ffn, w_in, q_a_norm, w_q_up, kv_a_norm, w_kv_up, q_norm, k_nope_norm, k_rope_norm, out_norm_sb, out_norm_mla, w_out, w_gate, w_up, w_down, loss_target, m_w_ada, m_b_ada, m_norm_attn, m_norm_ffn, m_w_in, m_q_a_norm, m_w_q_up, m_kv_a_norm, m_w_kv_up, m_q_norm, m_k_nope_norm, m_k_rope_norm, m_out_norm_sb, m_out_norm_mla, m_w_out, m_w_gate, m_w_up, m_w_down, v_w_ada, v_b_ada, v_norm_attn, v_norm_ffn, v_w_in, v_q_a_norm, v_w_q_up, v_kv_a_norm, v_w_kv_up, v_q_norm, v_k_nope_norm, v_k_rope_norm, v_out_norm_sb, v_out_norm_mla, v_w_out, v_w_gate, v_w_up, v_w_down):
    local = dict(locals())
    w = {n: local[n][0] for n in WEIGHTS}
    m = {n: local["m_" + n][0] for n in WEIGHTS}
    v = {n: local["v_" + n][0] for n in WEIGHTS}
    small = {n: w[n].reshape(1, -1) for n in SMALL}
    ix, iy, ic = _my_place()
    chip = 2 * ix + iy
    dev = 2 * chip + ic
    xs, target = x[0], loss_target[0]
    seq = xs.shape[0]

    c_all = _all_gather_small(c.reshape(8, LANES), "gather_c").reshape(N_DEV, D_MODEL)
    ada_cols = w["w_ada"].shape[1]
    b_cols = lax.dynamic_slice_in_dim(small["b_ada"], chip * ada_cols, ada_cols, axis=1)
    mod_cols = _ada_fwd(c_all, w["w_ada"], b_cols)
    mod_all = _all_gather_small(mod_cols, "gather_mod").reshape(N_CHIPS, 2, N_DEV, ada_cols)
    mod = lax.dynamic_index_in_dim(mod_all[:, 0], dev, axis=1, keepdims=False).reshape(1, N_MOD * D_MODEL)

    ff_pad = FF_SHARD_PAD - FF_SHARD
    pads = {"w_gate": ((0, 0), (0, ff_pad)), "w_up": ((0, 0), (0, ff_pad)), "w_down": ((0, ff_pad), (0, 0))}
    shards = {n: jnp.pad(w[n].astype(BF16), pads[n]) if n in pads else w[n].astype(BF16) for n in BIG}
    early, early_done = _gather_weights(EARLY, [shards[n] for n in EARLY], mod)
    gathered = dict(zip(EARLY, early))
    lands = [lax.dynamic_update_index_in_dim(lax.empty((N_CHIPS,) + shards[n].shape, BF16), shards[n], chip, 0)
             for n in LATE]
    late_gather = _exchange_start("gather_late_start", [shards[n] for n in LATE], lands, _late_gather_plan,
                                  3 * len(LATE), early_done)

    half = MLA_ROPE // 2
    freqs = 1.0 / (ROPE_THETA ** (np.arange(half, dtype=np.float32) / half))
    zeros = np.zeros(LANES - MLA_ROPE, np.float32)
    freqs_row = jnp.asarray(np.concatenate([freqs, freqs, zeros]).astype(np.float32)[None])
    sign_row = jnp.asarray(np.concatenate([-np.ones(half), np.ones(half), zeros]).astype(np.float32)[None])
    cos, sin = _rope_tables(positions.reshape(seq, 1), freqs_row, sign_row)

    place = jnp.stack([ic, chip]).astype(jnp.int32)
    small_params = {n: small[n] for n in SMALL if n != "b_ada"}
    mod = mod + late_gather[4][0, 0]

    def pair_sums_of(names, grads, call_name):
        theirs = _pair_exchange(names, grads, call_name)
        return [_pair_sum(place, gr, th, HALF_AXIS[n], "grad_pair_sum_" + n) for n, gr, th in zip(names, grads, theirs)]

    p1 = {**{n: gathered[n] for n in EARLY}, **small_params}
    mixed, mixing_vjp = jax.vjp(lambda x_, mod_, p_: _mixing_stage(x_, mod_, p_, cos, sin), xs, mod, p1)
    _, landed = _exchange_wait("gather_late_wait", late_gather, _late_gather_plan, mixed)
    p2 = {**dict(zip(LATE, landed)), **small_params}
    loss_part, ffn_vjp = jax.vjp(lambda x_, mixed_, mod_, p_: _ffn_stage(x_, mixed_, mod_, p_, target), xs, mixed, mod, p2)
    gx2, gmixed, gmod2, gp2 = ffn_vjp(jnp.ones((), F32))
    late_sums = pair_sums_of(LATE, [gp2[n] for n in LATE], "grad_pair_exchange_late")
    late_scatter = _exchange_start(
        "grad_scatter_late_start", late_sums,
        [lax.empty((N_CHIPS - 1,) + s.shape[1:], BF16) for s in late_sums], _late_scatter_plan, 3 * len(LATE), gx2)
    gx1, gmod1, gp1 = mixing_vjp(gmixed + late_scatter[4][0, 0])
    gx = gx1 + gx2
    gmod = gmod1 + gmod2
    gp = {n: gp1[n] + gp2[n] for n in small_params}
    loss = lax.psum(loss_part, ("x", "y", "c"))

    small_names = [n for n in SMALL if n != "b_ada"]
    small_vec = jnp.concatenate([gmod] + [gp[n] for n in small_names], axis=1)
    n_small = small_vec.shape[1]
    small_all = _all_gather_small(small_vec.reshape(8, n_small // 8), "gather_small").reshape(N_DEV, 8, n_small // 8)

    early_sums = pair_sums_of(EARLY, [gp1[n] for n in EARLY], "grad_pair_exchange_early")
    early_scatter = _exchange_start(
        "grad_scatter_early_start", early_sums,
        [lax.empty((N_CHIPS - 1,) + s.shape[1:], BF16) for s in early_sums], _late_scatter_plan, 3 * len(EARLY),
        small_all)
    late_sums, late_parts = _exchange_wait("grad_scatter_late_wait", late_scatter, _late_scatter_plan, gx)
    g, delta, new_m, new_v = {}, {}, {}, {}

    def update(names, sums, parts, join_name, after):
        own = [_chip_sum(place, ps, pt, "grad_chip_sum_" + n, n in TRANSPOSED_UPDATE) for n, ps, pt in zip(names, sums, parts)]
        sib = _sibling_join(own, join_name, after)
        for n, o, s in zip(names, own, sib):
            if n in TRANSPOSED_UPDATE:
                res = _adamw_halves(place, w[n].T, o, s, m[n].T, v[n].T, 1, "adamw_" + n)
                g[n], delta[n], new_m[n], new_v[n] = [r.T for r in res]
            else:
                g[n], delta[n], new_m[n], new_v[n] = _adamw_halves(place, w[n], o, s, m[n], v[n], HALF_AXIS[n], "adamw_" + n)

    update(LATE, late_sums, late_parts, "grad_sibling_join_late", early_scatter[4])

    def pack_small(d):
        return jnp.concatenate([d[n].reshape(1, -1) for n in SMALL], axis=1).reshape(8, n_small // 8)

    gs, ds, ms, vs = _adamw_small(pack_small(w), small_all, pack_small(m), pack_small(v))
    sizes = [w[n].size for n in SMALL]
    offs = np.concatenate([[0], np.cumsum(sizes)])

    def unpack_small(a):
        flat = a.reshape(-1)
        return {n: flat[offs[i]:offs[i + 1]].reshape(w[n].shape) for i, n in enumerate(SMALL)}

    for d, packed in zip((g, delta, new_m, new_v), (gs, ds, ms, vs)):
        d.update(unpack_small(packed))

    dmod_all = small_all.reshape(N_DEV, n_small)[:, :N_MOD * D_MODEL]
    g["w_ada"] = _ada_bwd(c_all, lax.dynamic_slice_in_dim(dmod_all, chip * ada_cols, ada_cols, axis=1))
    delta["w_ada"], new_m["w_ada"], new_v["w_ada"] = _adamw(w["w_ada"], g["w_ada"], m["w_ada"], v["w_ada"], "adamw_w_ada")

    early_sums, early_parts = _exchange_wait("grad_scatter_early_wait", early_scatter, _late_scatter_plan,
                                             delta["w_ada"])
    update(EARLY, early_sums, early_parts, "grad_sibling_join_early", delta["w_ada"])

    def outs(d):
        return [d[n][None] for n in WEIGHTS]

    return (loss, gx[None], *outs(g), *outs(delta), *outs(new_m), *outs(new_v))
```

```python
import functools
import math

import numpy as np
import jax
import jax.numpy as jnp
from jax import lax
from jax.experimental import pallas as pl
from jax.experimental.pallas import tpu as pltpu

F32 = jnp.float32
BF16 = jnp.bfloat16
MESH = pl.DeviceIdType.MESH
ANY = pl.BlockSpec(memory_space=pl.ANY)

D_MODEL = 1024
SB_HEADS = 8
SB_HEAD_DIM = 64
SB_WIDTH = 512
MLA_HEADS = 4
MLA_NOPE = 128
MLA_ROPE = 64
MLA_QK = 192
MLA_V = 128
MLA_Q_RANK = 384
MLA_KV_RANK = 256
D_FF = 2816
N_MOD = 6
ROPE_THETA = 10000.0
EPS = 1e-6
LANES = 128

ADAM_LR = 0.001
ADAM_B1 = 0.9
ADAM_B2 = 0.999
ADAM_EPS = 1e-08
ADAM_WD = 0.01
ADAM_STEP = 10

N_CHIPS = 4
N_DEV = 8
ROW_TILE = 256
ATT_BLK = 256
MM_VMEM_LIMIT = 56 * 1024 * 1024
FF_SHARD = D_FF // N_CHIPS
FF_SHARD_PAD = 768


def _mm(a, b, mode, name, tm, tn, out_dtype=F32):
    if mode == "nn":
        (m, k), n = a.shape, b.shape[1]
        a_spec = pl.BlockSpec((tm, k), lambda j, i: (i, 0))
        b_spec = pl.BlockSpec((k, tn), lambda j, i: (0, j))
        dims = (((1,), (0,)), ((), ()))
    elif mode == "nt":
        (m, k), n = a.shape, b.shape[0]
        a_spec = pl.BlockSpec((tm, k), lambda j, i: (i, 0))
        b_spec = pl.BlockSpec((tn, k), lambda j, i: (j, 0))
        dims = (((1,), (1,)), ((), ()))
    else:
        (k, m), n = a.shape, b.shape[1]
        a_spec = pl.BlockSpec((k, tm), lambda j, i: (0, i))
        b_spec = pl.BlockSpec((k, tn), lambda j, i: (0, j))
        dims = (((0,), (0,)), ((), ()))
    assert m % tm == 0 and n % tn == 0, (name, m, n, tm, tn)

    def body(a_ref, b_ref, o_ref):
        o_ref[...] = lax.dot_general(a_ref[...].astype(BF16), b_ref[...].astype(BF16), dims,
                                     preferred_element_type=F32).astype(out_dtype)

    return pl.pallas_call(
        body, name=name, grid=(n // tn, m // tm),
        in_specs=[a_spec, b_spec],
        out_specs=pl.BlockSpec((tm, tn), lambda j, i: (i, j)),
        out_shape=jax.ShapeDtypeStruct((m, n), out_dtype),
        compiler_params=pltpu.CompilerParams(dimension_semantics=("arbitrary", "arbitrary"),
                                             vmem_limit_bytes=MM_VMEM_LIMIT),
    )(a, b)


def _make_linear(name, tk_w, tn_w):
    @jax.custom_vjp
    def op(a, w):
        return _mm(a, w, "nn", name + "_fwd", ROW_TILE, w.shape[1])

    def fwd(a, w):
        return op(a, w), (a, w)

    def bwd(res, dy):
        a, w = res
        da = _mm(dy, w, "nt", name + "_dx", ROW_TILE, w.shape[0])
        dw = _mm(a, dy, "tn", name + "_dw", tk_w, tn_w, out_dtype=BF16)
        return da, dw

    op.defvjp(fwd, bwd)
    return op


def _make_linear_sharded(name, tk_w):
    def call_fwd(a, w):
        t, k = a.shape
        n_sh, _, cc = w.shape

        def body(a_ref, w_ref, o_ref):
            o_ref[...] = jnp.dot(a_ref[...].astype(BF16), w_ref[...], preferred_element_type=F32)

        return pl.pallas_call(
            body, name=name + "_fwd", grid=(n_sh, t // ROW_TILE),
            in_specs=[pl.BlockSpec((ROW_TILE, k), lambda j, i: (i, 0)),
                      pl.BlockSpec((None, k, cc), lambda j, i: (j, 0, 0))],
            out_specs=pl.BlockSpec((ROW_TILE, cc), lambda j, i: (i, j)),
            out_shape=jax.ShapeDtypeStruct((t, n_sh * cc), F32),
            compiler_params=pltpu.CompilerParams(dimension_semantics=("arbitrary", "arbitrary"),
                                                 vmem_limit_bytes=MM_VMEM_LIMIT),
        )(a, w)

    def call_dx(dy, w):
        t = dy.shape[0]
        n_sh, k, cc = w.shape

        def body(dy_ref, w_ref, o_ref):
            acc = jnp.zeros((ROW_TILE, k), F32)
            for j in range(n_sh):
                acc = acc + _nt(dy_ref[:, j * cc:(j + 1) * cc].astype(BF16), w_ref[j])
            o_ref[...] = acc

        return pl.pallas_call(
            body, name=name + "_dx", grid=(t // ROW_TILE,),
            in_specs=[pl.BlockSpec((ROW_TILE, n_sh * cc), lambda i: (i, 0)),
                      pl.BlockSpec((n_sh, k, cc), lambda i: (0, 0, 0))],
            out_specs=pl.BlockSpec((ROW_TILE, k), lambda i: (i, 0)),
            out_shape=jax.ShapeDtypeStruct((t, k), F32),
            compiler_params=pltpu.CompilerParams(dimension_semantics=("arbitrary",),
                                                 vmem_limit_bytes=MM_VMEM_LIMIT),
        )(dy, w)

    def call_dw(a, dy, w):
        t, k = a.shape
        n_sh, _, cc = w.shape

        def body(a_ref, dy_ref, o_ref):
            o_ref[...] = _tn(a_ref[...].astype(BF16), dy_ref[...].astype(BF16)).astype(BF16)

        return pl.pallas_call(
            body, name=name + "_dw", grid=(n_sh, k // tk_w),
            in_specs=[pl.BlockSpec((t, tk_w), lambda j, i: (0, i)),
                      pl.BlockSpec((t, cc), lambda j, i: (0, j))],
            out_specs=pl.BlockSpec((None, tk_w, cc), lambda j, i: (j, i, 0)),
            out_shape=jax.ShapeDtypeStruct(w.shape, BF16),
            compiler_params=pltpu.CompilerParams(dimension_semantics=("arbitrary", "arbitrary"),
                                                 vmem_limit_bytes=MM_VMEM_LIMIT),
        )(a, dy)

    @jax.custom_vjp
    def op(a, w):
        return call_fwd(a, w)

    def fwd(a, w):
        return op(a, w), (a, w)

    def bwd(res, dy):
        a, w = res
        return call_dx(dy, w), call_dw(a, dy, w)

    op.defvjp(fwd, bwd)
    return op


def _row_spec(arr, tb):
    return pl.BlockSpec((tb, arr.shape[1]), lambda i: (i, 0))


def _full_spec(arr):
    return pl.BlockSpec(arr.shape, lambda i: (0, 0))


def _make_rowwise(name, f, n_rows, n_params, out_cols, diff_rows, out_dtypes=None, grad_dtypes=None):
    n_out = len(out_cols)
    out_dtypes = out_dtypes or [F32] * n_out
    grad_dtypes = grad_dtypes or [F32] * sum(diff_rows)

    def call_fwd(rows, params):
        t = rows[0].shape[0]

        def body(*refs):
            ins = [r[...] for r in refs[:n_rows + n_params]]
            outs = f(*ins)
            for o_ref, o in zip(refs[n_rows + n_params:], outs):
                o_ref[...] = o.astype(o_ref.dtype)

        return pl.pallas_call(
            body, name=name + "_fwd", grid=(t // ROW_TILE,),
            in_specs=[_row_spec(a, ROW_TILE) for a in rows] + [_full_spec(p) for p in params],
            out_specs=[pl.BlockSpec((ROW_TILE, n), lambda i: (i, 0)) for n in out_cols],
            out_shape=[jax.ShapeDtypeStruct((t, n), dt) for n, dt in zip(out_cols, out_dtypes)],
            compiler_params=pltpu.CompilerParams(dimension_semantics=("arbitrary",),
                                                 vmem_limit_bytes=MM_VMEM_LIMIT),
        )(*rows, *params)

    def call_bwd(rows, params, cts):
        t = rows[0].shape[0]
        d_rows = [a for a, d in zip(rows, diff_rows) if d]
        n_in = n_rows + n_params + n_out

        def body(*refs):
            ins = [r[...] for r in refs[:n_rows + n_params]]
            ct = tuple(r[...].astype(F32) for r in refs[n_rows + n_params:n_in])
            _, vjp = jax.vjp(f, *ins)
            grads = vjp(ct)
            out_refs = refs[n_in:]
            g_rows = [g for g, d in zip(grads[:n_rows], diff_rows) if d]
            for o_ref, g in zip(out_refs[:len(g_rows)], g_rows):
                o_ref[...] = g.astype(o_ref.dtype)
            p_refs = out_refs[len(g_rows):]

            if p_refs:
                @pl.when(pl.program_id(0) == 0)
                def _():
                    for p_ref in p_refs:
                        p_ref[...] = jnp.zeros_like(p_ref)

                for p_ref, g in zip(p_refs, grads[n_rows:]):
                    p_ref[...] += g

        return pl.pallas_call(
            body, name=name + "_bwd", grid=(t // ROW_TILE,),
            in_specs=[_row_spec(a, ROW_TILE) for a in rows] + [_full_spec(p) for p in params]
            + [_row_spec(c, ROW_TILE) for c in cts],
            out_specs=[_row_spec(a, ROW_TILE) for a in d_rows] + [_full_spec(p) for p in params],
            out_shape=[jax.ShapeDtypeStruct(a.shape, dt) for a, dt in zip(d_rows, grad_dtypes)]
            + [jax.ShapeDtypeStruct(p.shape, F32) for p in params],
            compiler_params=pltpu.CompilerParams(dimension_semantics=("arbitrary",),
                                                 vmem_limit_bytes=MM_VMEM_LIMIT),
        )(*rows, *params, *cts)

    @jax.custom_vjp
    def op(*args):
        return tuple(call_fwd(args[:n_rows], args[n_rows:]))

    def fwd(*args):
        return op(*args), args

    def bwd(args, cts):
        rows, params = args[:n_rows], args[n_rows:]
        outs = call_bwd(rows, params, cts)
        it = iter(outs)
        g_rows = [next(it) if d else jnp.zeros_like(a) for a, d in zip(rows, diff_rows)]
        return tuple(g_rows) + tuple(it)

    op.defvjp(fwd, bwd)
    return op


def _rms(x, g, n):
    return x * lax.rsqrt(jnp.sum(x * x, axis=-1, keepdims=True) * (1.0 / n) + EPS) * g


def _f_pre_attn(x, g, scale, shift):
    return (_rms(x, g, D_MODEL) * (1.0 + scale) + shift,)


def _f_mla_a(cq, ckv, gq, gkv):
    return _rms(cq, gq, MLA_Q_RANK), _rms(ckv, gkv, MLA_KV_RANK)


@jax.custom_vjp
def _split_lanes(x):
    return tuple(x[:, i * LANES:(i + 1) * LANES] for i in range(x.shape[1] // LANES))


def _split_lanes_fwd(x):
    return _split_lanes(x), None


def _split_lanes_bwd(_, cts):
    return (jnp.concatenate(cts, axis=1),)


_split_lanes.defvjp(_split_lanes_fwd, _split_lanes_bwd)


def _f_mla_b(qall, kn_all, kr, kr_sw, cos, sin, gqn, gqr, gqr_sw, gkn, gkr, gkr_sw):
    q = _split_lanes(qall)
    kn = _split_lanes(kn_all)
    qn_o, qr_o, kn_o = [], [], []
    for h in range(MLA_HEADS):
        qn, qr, qs = q[h], q[MLA_HEADS + h], q[2 * MLA_HEADS + h]
        ss = jnp.sum(qn * qn, axis=-1, keepdims=True) + jnp.sum(qr * qr, axis=-1, keepdims=True)
        rs = lax.rsqrt(ss * (1.0 / MLA_QK) + EPS)
        qn_o.append(qn * rs * gqn)
        qr_o.append((qr * rs * gqr) * cos + (qs * rs * gqr_sw) * sin)
        kn_o.append(_rms(kn[h], gkn, MLA_NOPE))
    rs = lax.rsqrt(jnp.sum(kr * kr, axis=-1, keepdims=True) * (1.0 / MLA_ROPE) + EPS)
    kr_o = (kr * rs * gkr) * cos + (kr_sw * rs * gkr_sw) * sin
    return (jnp.concatenate(qn_o, axis=1), jnp.concatenate(qr_o, axis=1), jnp.concatenate(kn_o, axis=1), kr_o)


def _f_post_attn(o_sb, o_mla, g_sb, g_mla):
    return (jnp.concatenate([_rms(o_sb, g_sb, SB_WIDTH), _rms(o_mla, g_mla, SB_WIDTH)], axis=1),)


def _f_pre_ffn(x, attn, gate, g, scale, shift):
    x2 = x + gate * attn
    return x2, _rms(x2, g, D_MODEL) * (1.0 + scale) + shift


def _f_swiglu(gt, up):
    return (gt / (1.0 + jnp.exp(-gt)) * up,)


def _f_loss(x2, ffn, target, gate):
    err = x2 + gate * ffn - target
    return (jnp.sum(err * err, axis=-1, keepdims=True) * (1.0 / D_MODEL),)


def _rope_tables(pos_col, freqs, sign):
    t = pos_col.shape[0]

    def body(p_ref, f_ref, s_ref, cos_ref, sin_ref):
        ang = p_ref[...].astype(F32) * f_ref[...]
        live = jnp.abs(s_ref[...])
        cos_ref[...] = jnp.cos(ang) * live
        sin_ref[...] = jnp.sin(ang) * s_ref[...]

    return pl.pallas_call(
        body, name="rope_tables", grid=(t // ROW_TILE,),
        in_specs=[pl.BlockSpec((ROW_TILE, 1), lambda i: (i, 0)), _full_spec(freqs), _full_spec(sign)],
        out_specs=[pl.BlockSpec((ROW_TILE, LANES), lambda i: (i, 0))] * 2,
        out_shape=[jax.ShapeDtypeStruct((t, LANES), F32)] * 2,
    )(pos_col, freqs, sign)


def _hi_lo_dot(x, tri):
    hi = x.astype(BF16)
    lo = (x - hi.astype(F32)).astype(BF16)
    return (jnp.dot(hi, tri, preferred_element_type=F32) + jnp.dot(lo, tri, preferred_element_type=F32))


def _tri(cmp):
    r = lax.broadcasted_iota(jnp.int32, (ATT_BLK, ATT_BLK), 0)
    c = lax.broadcasted_iota(jnp.int32, (ATT_BLK, ATT_BLK), 1)
    return cmp(r, c).astype(BF16)


def _nt(a, b):
    return lax.dot_general(a, b, (((1,), (1,)), ((), ())), preferred_element_type=F32)


def _tn(a, b):
    return lax.dot_general(a, b, (((0,), (0,)), ((), ())), preferred_element_type=F32)


def _sb_logs(z):
    lb = jnp.minimum(z, 0.0) - jnp.log(1.0 + jnp.exp(-jnp.abs(z)))
    return lb, lb - z


def _sb_fwd(q, k, v):
    t = q.shape[0]
    nq = t // ATT_BLK
    scale = SB_HEAD_DIM ** -0.5

    def body(q_ref, k_ref, v_ref, o_ref, tot_ref):
        qi = pl.program_id(1)
        lane = lax.broadcasted_iota(jnp.int32, (ATT_BLK, LANES), 1)
        tri = _tri(lambda r, c: r > c)
        qv = q_ref[...] * scale
        heads = [(lane // SB_HEAD_DIM) == hh for hh in range(2)]
        qms = [jnp.where(mine, qv, 0.0).astype(BF16) for mine in heads]

        def blocks(kbs, carry, diagonal):
            acc = carry[0]
            nb = len(kbs)
            chains = [(b, hh) for b in range(nb) for hh in range(2)]
            offs = [pl.multiple_of(kb * ATT_BLK, ATT_BLK) for kb in kbs]
            kks = [k_ref[pl.ds(off, ATT_BLK), :].astype(BF16) for off in offs]
            v_blks = [v_ref[pl.ds(off, ATT_BLK), :] for off in offs]
            if any(diagonal):
                valid = (lax.broadcasted_iota(jnp.int32, (ATT_BLK, ATT_BLK), 1)
                         < lax.broadcasted_iota(jnp.int32, (ATT_BLK, ATT_BLK), 0))
            zs = {ch: _nt(qms[ch[1]], kks[ch[0]]) for ch in chains}
            vvs = {(b, hh): jnp.where(heads[hh], v_blks[b], 0.0).astype(BF16) for b, hh in chains}
            logs = {ch: _sb_logs(zs[ch]) for ch in chains}
            l1ms = {ch: jnp.where(valid, logs[ch][1], 0.0) if diagonal[ch[0]] else logs[ch][1] for ch in chains}
            run = {(0, hh): carry[1 + hh] for hh in range(2)}
            for b, hh in chains:
                run[(b + 1, hh)] = run[(b, hh)] + jnp.sum(l1ms[(b, hh)], axis=-1, keepdims=True)
            afters = {ch: _hi_lo_dot(l1ms[ch], tri) for ch in chains}
            ws = {ch: jnp.exp(logs[ch][0] + (afters[ch] + run[ch])) for ch in chains}
            ws = {ch: jnp.where(valid, ws[ch], 0.0) if diagonal[ch[0]] else ws[ch] for ch in chains}
            for ch in chains:
                acc = acc + jnp.dot(ws[ch].astype(BF16), vvs[ch], preferred_element_type=F32)
            return (acc, run[(nb, 0)], run[(nb, 1)])

        zero = jnp.zeros((ATT_BLK, 1), F32)
        init = (jnp.zeros((ATT_BLK, LANES), F32), zero, zero)
        carry = lax.cond(qi % 2 == 1, lambda cr: blocks([qi, qi - 1], cr, (True, False)),
                         lambda cr: blocks([qi], cr, (True,)), init)
        top = qi - 1 - qi % 2
        carry = lax.fori_loop(0, qi // 2, lambda pr, cr: blocks([top - 2 * pr, top - 1 - 2 * pr], cr, (False, False)),
                              carry)
        o_ref[...] = carry[0]
        for hh in range(2):
            tot_ref[:, hh * LANES:(hh + 1) * LANES] = jnp.broadcast_to(carry[1 + hh], (ATT_BLK, LANES))

    return pl.pallas_call(
        body, name="sb_attn_fwd", grid=(SB_HEADS // 2, nq),
        in_specs=[pl.BlockSpec((ATT_BLK, LANES), lambda p, i: (i, p)),
                  pl.BlockSpec((t, LANES), lambda p, i: (0, p)),
                  pl.BlockSpec((t, LANES), lambda p, i: (0, p))],
        out_specs=[pl.BlockSpec((ATT_BLK, LANES), lambda p, i: (i, p)),
                   pl.BlockSpec((ATT_BLK, 2 * LANES), lambda p, i: (i, p))],
        out_shape=[jax.ShapeDtypeStruct((t, SB_WIDTH), F32), jax.ShapeDtypeStruct((t, SB_HEADS * LANES), F32)],
        compiler_params=pltpu.CompilerParams(dimension_semantics=("arbitrary", "arbitrary")),
    )(q, k, v)


def _sb_bwd(q, k, v, tot, do):
    t = q.shape[0]
    nq = t // ATT_BLK
    scale = SB_HEAD_DIM ** -0.5

    def body(q_ref, k_ref, v_ref, tot_ref, do_ref, dq_ref, dk_ref, dv_ref):
        qi = pl.program_id(1)

        @pl.when(qi == 0)
        def _():
            dk_ref[...] = jnp.zeros_like(dk_ref)
            dv_ref[...] = jnp.zeros_like(dv_ref)

        lane = lax.broadcasted_iota(jnp.int32, (ATT_BLK, LANES), 1)
        tri_incl = _tri(lambda r, c: r <= c)
        tri_lt = _tri(lambda r, c: r < c)
        qv = q_ref[...] * scale
        dov = do_ref[...]
        heads = [(lane // SB_HEAD_DIM) == hh for hh in range(2)]
        qms = [jnp.where(mine, qv, 0.0).astype(BF16) for mine in heads]
        doms = [jnp.where(mine, dov, 0.0).astype(BF16) for mine in heads]
        tots = [tot_ref[:, hh * LANES:hh * LANES + 1] for hh in range(2)]

        def blocks(kbs, carry, diagonal):
            dq = carry[0]
            nb = len(kbs)
            chains = [(b, hh) for b in range(nb) for hh in range(2)]
            offs = [pl.multiple_of(kb * ATT_BLK, ATT_BLK) for kb in kbs]
            k_blks = [k_ref[pl.ds(off, ATT_BLK), :] for off in offs]
            vvs = [v_ref[pl.ds(off, ATT_BLK), :].astype(BF16) for off in offs]
            if any(diagonal):
                valid = (lax.broadcasted_iota(jnp.int32, (ATT_BLK, ATT_BLK), 1)
                         < lax.broadcasted_iota(jnp.int32, (ATT_BLK, ATT_BLK), 0))
            kks = {(b, hh): jnp.where(heads[hh], k_blks[b], 0.0).astype(BF16) for b, hh in chains}
            zs = {ch: _nt(qms[ch[1]], kks[ch]) for ch in chains}
            dws = {ch: _nt(doms[ch[1]], vvs[ch[0]]) for ch in chains}
            logs = {ch: _sb_logs(zs[ch]) for ch in chains}
            lbs = {ch: logs[ch][0] for ch in chains}
            l1m_all = {ch: logs[ch][1] for ch in chains}
            l1ms = {ch: jnp.where(valid, l1m_all[ch], 0.0) if diagonal[ch[0]] else l1m_all[ch] for ch in chains}
            pre, c_de = {}, {}
            for hh in range(2):
                pre[(0, hh)], c_de[(0, hh)] = carry[1 + 2 * hh], carry[2 + 2 * hh]
            for b, hh in chains:
                pre[(b + 1, hh)] = pre[(b, hh)] + jnp.sum(l1ms[(b, hh)], axis=-1, keepdims=True)
            prefix = {ch: _hi_lo_dot(l1ms[ch], tri_incl) for ch in chains}
            ws = {ch: jnp.exp(lbs[ch] + (tots[ch[1]] - (prefix[ch] + pre[ch]))) for ch in chains}
            ws = {ch: jnp.where(valid, ws[ch], 0.0) if diagonal[ch[0]] else ws[ch] for ch in chains}
            d_es = {ch: ws[ch] * dws[ch] for ch in chains}
            for b, hh in chains:
                c_de[(b + 1, hh)] = c_de[(b, hh)] + jnp.sum(d_es[(b, hh)], axis=-1, keepdims=True)
            dvs = [_tn(ws[(b, 0)].astype(BF16), doms[0]) + _tn(ws[(b, 1)].astype(BF16), doms[1]) for b in range(nb)]
            dl1ms = {ch: _hi_lo_dot(d_es[ch], tri_lt) + c_de[ch] for ch in chains}
            dzs = {ch: d_es[ch] * jnp.exp(l1m_all[ch]) - dl1ms[ch] * jnp.exp(lbs[ch]) for ch in chains}
            dzs = {ch: jnp.where(valid, dzs[ch], 0.0) if diagonal[ch[0]] else dzs[ch] for ch in chains}
            dzs = {ch: dzs[ch].astype(BF16) for ch in chains}
            for ch in chains:
                dq = dq + jnp.dot(dzs[ch], kks[ch], preferred_element_type=F32)
            for b in range(nb):
                dk_ref[pl.ds(offs[b], ATT_BLK), :] += _tn(dzs[(b, 0)], qms[0]) + _tn(dzs[(b, 1)], qms[1])
                dv_ref[pl.ds(offs[b], ATT_BLK), :] += dvs[b]
            return (dq, pre[(nb, 0)], c_de[(nb, 0)], pre[(nb, 1)], c_de[(nb, 1)])

        zero = jnp.zeros((ATT_BLK, 1), F32)
        carry = lax.fori_loop(0, qi // 2, lambda pr, cr: blocks([2 * pr, 2 * pr + 1], cr, (False, False)),
                              (jnp.zeros((ATT_BLK, LANES), F32), zero, zero, zero, zero))
        carry = lax.cond(qi % 2 == 1, lambda cr: blocks([qi - 1, qi], cr, (False, True)),
                         lambda cr: blocks([qi], cr, (True,)), carry)
        dq_ref[...] = carry[0] * scale

    return pl.pallas_call(
        body, name="sb_attn_bwd", grid=(SB_HEADS // 2, nq),
        in_specs=[pl.BlockSpec((ATT_BLK, LANES), lambda p, i: (i, p)),
                  pl.BlockSpec((t, LANES), lambda p, i: (0, p)),
                  pl.BlockSpec((t, LANES), lambda p, i: (0, p)),
                  pl.BlockSpec((ATT_BLK, 2 * LANES), lambda p, i: (i, p)),
                  pl.BlockSpec((ATT_BLK, LANES), lambda p, i: (i, p))],
        out_specs=[pl.BlockSpec((ATT_BLK, LANES), lambda p, i: (i, p)),
                   pl.BlockSpec((t, LANES), lambda p, i: (0, p)),
                   pl.BlockSpec((t, LANES), lambda p, i: (0, p))],
        out_shape=[jax.ShapeDtypeStruct((t, SB_WIDTH), F32)] * 3,
        compiler_params=pltpu.CompilerParams(dimension_semantics=("arbitrary", "arbitrary")),
    )(q, k, v, tot, do)


@jax.custom_vjp
def _sb_attention(q, k, v):
    return _sb_fwd(q, k, v)[0]


def _sb_attention_fwd(q, k, v):
    o, tot = _sb_fwd(q, k, v)
    return o, (q, k, v, tot)


def _sb_attention_bwd(res, do):
    return tuple(_sb_bwd(*res, do))


_sb_attention.defvjp(_sb_attention_fwd, _sb_attention_bwd)


def _mla_fwd(qn, qr, kn, kr, v):
    t = qn.shape[0]
    nq = t // ATT_BLK
    scale = MLA_QK ** -0.5

    def body(qn_ref, qr_ref, kn_ref, kr_ref, v_ref, o_ref, lse_ref):
        qi = pl.program_id(1)
        lanes = [slice(hh * LANES, (hh + 1) * LANES) for hh in range(2)]
        qnb = [qn_ref[:, sl].astype(BF16) for sl in lanes]
        qrb = [qr_ref[:, sl].astype(BF16) for sl in lanes]

        def blocks(kbs, carry, diagonal):
            nb = len(kbs)
            chains = [(b, hh) for b in range(nb) for hh in range(2)]
            offs = [pl.multiple_of(kb * ATT_BLK, ATT_BLK) for kb in kbs]
            krbs = [kr_ref[pl.ds(off, ATT_BLK), :].astype(BF16) for off in offs]
            accs, ms, ls = [carry[0], carry[3]], [carry[1], carry[4]], [carry[2], carry[5]]
            ss = {(b, hh): (_nt(qnb[hh], kn_ref[pl.ds(offs[b], ATT_BLK), lanes[hh]].astype(BF16))
                            + _nt(qrb[hh], krbs[b])) * scale for b, hh in chains}
            if any(diagonal):
                causal = (lax.broadcasted_iota(jnp.int32, (ATT_BLK, ATT_BLK), 1)
                          <= lax.broadcasted_iota(jnp.int32, (ATT_BLK, ATT_BLK), 0))
                ss = {ch: jnp.where(causal, ss[ch], -jnp.inf) if diagonal[ch[0]] else ss[ch] for ch in chains}
            m_new = list(ms)
            for b, hh in chains:
                m_new[hh] = jnp.maximum(m_new[hh], jnp.max(ss[(b, hh)], axis=-1, keepdims=True))
            ps = {(b, hh): jnp.exp(ss[(b, hh)] - m_new[hh]) for b, hh in chains}
            alphas = [jnp.exp(ms[hh] - m_new[hh]) for hh in range(2)]
            pvs = {(b, hh): jnp.dot(ps[(b, hh)].astype(BF16), v_ref[pl.ds(offs[b], ATT_BLK), lanes[hh]].astype(BF16),
                                    preferred_element_type=F32) for b, hh in chains}
            out = []
            for hh in range(2):
                acc, l = accs[hh] * alphas[hh], ls[hh] * alphas[hh]
                for b in range(nb):
                    acc, l = acc + pvs[(b, hh)], l + jnp.sum(ps[(b, hh)], axis=-1, keepdims=True)
                out += [acc, m_new[hh], l]
            return tuple(out)

        init = (jnp.zeros((ATT_BLK, LANES), F32), jnp.full((ATT_BLK, 1), -jnp.inf, F32), jnp.zeros((ATT_BLK, 1), F32))
        carry = lax.cond(qi % 2 == 1, lambda cr: blocks([qi, qi - 1], cr, (True, False)),
                         lambda cr: blocks([qi], cr, (True,)), init + init)
        carry = lax.fori_loop(0, qi // 2, lambda pr, cr: blocks([2 * pr, 2 * pr + 1], cr, (False, False)), carry)
        for hh in range(2):
            acc, m, l = carry[3 * hh:3 * hh + 3]
            o_ref[:, lanes[hh]] = acc / l
            lse_ref[:, lanes[hh]] = jnp.broadcast_to(m + jnp.log(l), (ATT_BLK, LANES))

    blk = pl.BlockSpec((ATT_BLK, 2 * LANES), lambda p, i: (i, p))
    full = pl.BlockSpec((t, 2 * LANES), lambda p, i: (0, p))
    return pl.pallas_call(
        body, name="mla_attn_fwd", grid=(MLA_HEADS // 2, nq),
        in_specs=[blk, blk, full, pl.BlockSpec((t, LANES), lambda p, i: (0, 0)), full],
        out_specs=[blk, blk],
        out_shape=[jax.ShapeDtypeStruct((t, MLA_HEADS * LANES), F32)] * 2,
        compiler_params=pltpu.CompilerParams(dimension_semantics=("arbitrary", "arbitrary")),
    )(qn, qr, kn, kr, v)


def _mla_bwd(qn, qr, kn, kr, v, o, lse, do):
    t = qn.shape[0]
    nq = t // ATT_BLK
    scale = MLA_QK ** -0.5

    def body(qn_ref, qr_ref, kn_ref, kr_ref, v_ref, o_ref, lse_ref, do_ref,
             dqn_ref, dqr_ref, dkn_ref, dkr_ref, dv_ref):
        pair = pl.program_id(0)
        qi = pl.program_id(1)

        @pl.when(qi == 0)
        def _():
            dkn_ref[...] = jnp.zeros_like(dkn_ref)
            dv_ref[...] = jnp.zeros_like(dv_ref)

        @pl.when((qi == 0) & (pair == 0))
        def _():
            dkr_ref[...] = jnp.zeros_like(dkr_ref)

        lanes = [slice(hh * LANES, (hh + 1) * LANES) for hh in range(2)]
        qnb = [qn_ref[:, sl].astype(BF16) for sl in lanes]
        qrb = [qr_ref[:, sl].astype(BF16) for sl in lanes]
        dob = [do_ref[:, sl].astype(BF16) for sl in lanes]
        delta = [jnp.sum(do_ref[:, sl] * o_ref[:, sl], axis=-1, keepdims=True) for sl in lanes]
        lse_v = [lse_ref[:, hh * LANES:hh * LANES + 1] for hh in range(2)]

        def blocks(kbs, carry, diagonal):
            nb = len(kbs)
            chains = [(b, hh) for b in range(nb) for hh in range(2)]
            offs = [pl.multiple_of(kb * ATT_BLK, ATT_BLK) for kb in kbs]
            krbs = [kr_ref[pl.ds(off, ATT_BLK), :].astype(BF16) for off in offs]
            knb = {(b, hh): kn_ref[pl.ds(offs[b], ATT_BLK), lanes[hh]].astype(BF16) for b, hh in chains}
            vb = {(b, hh): v_ref[pl.ds(offs[b], ATT_BLK), lanes[hh]].astype(BF16) for b, hh in chains}
            ss = {(b, hh): _nt(qnb[hh], knb[(b, hh)]) + _nt(qrb[hh], krbs[b]) for b, hh in chains}
            dps = {(b, hh): _nt(dob[hh], vb[(b, hh)]) for b, hh in chains}
            ps = {(b, hh): jnp.exp(ss[(b, hh)] * scale - lse_v[hh]) for b, hh in chains}
            if any(diagonal):
                causal = (lax.broadcasted_iota(jnp.int32, (ATT_BLK, ATT_BLK), 1)
                          <= lax.broadcasted_iota(jnp.int32, (ATT_BLK, ATT_BLK), 0))
                ps = {ch: jnp.where(causal, ps[ch], 0.0) if diagonal[ch[0]] else ps[ch] for ch in chains}
            dss = {(b, hh): (ps[(b, hh)] * (dps[(b, hh)] - delta[hh]) * scale).astype(BF16) for b, hh in chains}
            for b, hh in chains:
                dv_ref[pl.ds(offs[b], ATT_BLK), lanes[hh]] += _tn(ps[(b, hh)].astype(BF16), dob[hh])
            for b, hh in chains:
                dkn_ref[pl.ds(offs[b], ATT_BLK), lanes[hh]] += _tn(dss[(b, hh)], qnb[hh])
            for b in range(nb):
                dkr_ref[pl.ds(offs[b], ATT_BLK), :] += _tn(dss[(b, 0)], qrb[0]) + _tn(dss[(b, 1)], qrb[1])
            out = list(carry)
            for b, hh in chains:
                out[2 * hh] = out[2 * hh] + jnp.dot(dss[(b, hh)], knb[(b, hh)], preferred_element_type=F32)
                out[2 * hh + 1] = out[2 * hh + 1] + jnp.dot(dss[(b, hh)], krbs[b], preferred_element_type=F32)
            return tuple(out)

        zero = jnp.zeros((ATT_BLK, LANES), F32)
        carry = lax.fori_loop(0, qi // 2, lambda pr, cr: blocks([2 * pr, 2 * pr + 1], cr, (False, False)),
                              (zero, zero, zero, zero))
        carry = lax.cond(qi % 2 == 1, lambda cr: blocks([qi - 1, qi], cr, (False, True)),
                         lambda cr: blocks([qi], cr, (True,)), carry)
        for hh in range(2):
            dqn_ref[:, lanes[hh]] = carry[2 * hh]
            dqr_ref[:, lanes[hh]] = carry[2 * hh + 1]

    blk = pl.BlockSpec((ATT_BLK, 2 * LANES), lambda p, i: (i, p))
    full = pl.BlockSpec((t, 2 * LANES), lambda p, i: (0, p))
    shared = pl.BlockSpec((t, LANES), lambda p, i: (0, 0))
    wide = jax.ShapeDtypeStruct((t, MLA_HEADS * LANES), F32)
    return pl.pallas_call(
        body, name="mla_attn_bwd", grid=(MLA_HEADS // 2, nq),
        in_specs=[blk, blk, full, shared, full, blk, blk, blk],
        out_specs=[blk, blk, full, shared, full],
        out_shape=[wide, wide, wide, jax.ShapeDtypeStruct((t, LANES), F32), wide],
        compiler_params=pltpu.CompilerParams(dimension_semantics=("arbitrary", "arbitrary")),
    )(qn, qr, kn, kr, v, o, lse, do)


@jax.custom_vjp
def _mla_attention(qn, qr, kn, kr, v):
    return _mla_fwd(qn, qr, kn, kr, v)[0]


def _mla_attention_fwd(qn, qr, kn, kr, v):
    o, lse = _mla_fwd(qn, qr, kn, kr, v)
    return o, (qn, qr, kn, kr, v, o, lse)


def _mla_attention_bwd(res, do):
    return tuple(_mla_bwd(*res, do))


_mla_attention.defvjp(_mla_attention_fwd, _mla_attention_bwd)


def _ffn_in(h, wg, wu):
    t, k = h.shape
    n_sh, _, cc = wg.shape

    def body(h_ref, wg_ref, wu_ref, g_ref, u_ref, a_ref):
        hb = h_ref[...].astype(BF16)
        for j in range(n_sh):
            cols = slice(j * cc, (j + 1) * cc)
            g = jnp.dot(hb, wg_ref[j], preferred_element_type=F32)
            u = jnp.dot(hb, wu_ref[j], preferred_element_type=F32)
            g_ref[:, cols] = g
            u_ref[:, cols] = u
            a_ref[:, cols] = _f_swiglu(g, u)[0].astype(BF16)

    w_spec = pl.BlockSpec((n_sh, k, cc), lambda i: (0, 0, 0))
    o_spec = pl.BlockSpec((ROW_TILE, n_sh * cc), lambda i: (i, 0))
    wide = (t, n_sh * cc)
    return pl.pallas_call(
        body, name="ffn_in_fwd", grid=(t // ROW_TILE,),
        in_specs=[pl.BlockSpec((ROW_TILE, k), lambda i: (i, 0)), w_spec, w_spec],
        out_specs=[o_spec, o_spec, o_spec],
        out_shape=[jax.ShapeDtypeStruct(wide, F32), jax.ShapeDtypeStruct(wide, F32), jax.ShapeDtypeStruct(wide, BF16)],
        compiler_params=pltpu.CompilerParams(dimension_semantics=("arbitrary",), vmem_limit_bytes=MM_VMEM_LIMIT),
    )(h, wg, wu)


def _ffn_mid_bwd(dy, wd, g, u):
    t, n = dy.shape
    n_sh, cc, _ = wd.shape

    def body(dy_ref, wd_ref, g_ref, u_ref, dg_ref, du_ref):
        d_act = _nt(dy_ref[...].astype(BF16), wd_ref[...])
        _, vjp = jax.vjp(_f_swiglu, g_ref[...], u_ref[...])
        dg, du = vjp((d_act,))
        dg_ref[...] = dg.astype(BF16)
        du_ref[...] = du.astype(BF16)

    blk = pl.BlockSpec((ROW_TILE, cc), lambda j, i: (i, j))
    wide = jax.ShapeDtypeStruct((t, n_sh * cc), BF16)
    return pl.pallas_call(
        body, name="ffn_mid_bwd", grid=(n_sh, t // ROW_TILE),
        in_specs=[pl.BlockSpec((ROW_TILE, n), lambda j, i: (i, 0)),
                  pl.BlockSpec((None, cc, n), lambda j, i: (j, 0, 0)), blk, blk],
        out_specs=[blk, blk], out_shape=[wide, wide],
        compiler_params=pltpu.CompilerParams(dimension_semantics=("arbitrary", "arbitrary"),
                                             vmem_limit_bytes=MM_VMEM_LIMIT),
    )(dy, wd, g, u)


def _ffn_dh(dg, du, wg, wu):
    t = dg.shape[0]
    n_sh, k, cc = wg.shape

    def body(dg_ref, du_ref, wg_ref, wu_ref, o_ref):
        acc = jnp.zeros((ROW_TILE, k), F32)
        for j in range(n_sh):
            cols = slice(j * cc, (j + 1) * cc)
            acc = acc + _nt(dg_ref[:, cols], wg_ref[j]) + _nt(du_ref[:, cols], wu_ref[j])
        o_ref[...] = acc

    blk = pl.BlockSpec((ROW_TILE, n_sh * cc), lambda i: (i, 0))
    w_spec = pl.BlockSpec((n_sh, k, cc), lambda i: (0, 0, 0))
    return pl.pallas_call(
        body, name="ffn_dh", grid=(t // ROW_TILE,),
        in_specs=[blk, blk, w_spec, w_spec],
        out_specs=pl.BlockSpec((ROW_TILE, k), lambda i: (i, 0)),
        out_shape=jax.ShapeDtypeStruct((t, k), F32),
        compiler_params=pltpu.CompilerParams(dimension_semantics=("arbitrary",), vmem_limit_bytes=MM_VMEM_LIMIT),
    )(dg, du, wg, wu)


def _ffn_dw_in(h, dy, n_sh, name):
    t, k = h.shape
    cc = dy.shape[1] // n_sh
    tk = 512

    def body(h_ref, dy_ref, o_ref):
        o_ref[...] = _tn(h_ref[...].astype(BF16), dy_ref[...]).astype(BF16)

    return pl.pallas_call(
        body, name=name, grid=(n_sh, k // tk),
        in_specs=[pl.BlockSpec((t, tk), lambda j, i: (0, i)), pl.BlockSpec((t, cc), lambda j, i: (0, j))],
        out_specs=pl.BlockSpec((None, tk, cc), lambda j, i: (j, i, 0)),
        out_shape=jax.ShapeDtypeStruct((n_sh, k, cc), BF16),
        compiler_params=pltpu.CompilerParams(dimension_semantics=("arbitrary", "arbitrary"),
                                             vmem_limit_bytes=MM_VMEM_LIMIT),
    )(h, dy)


@jax.custom_vjp
def _ffn_block(h, wg, wu, wd):
    act = _ffn_in(h, wg, wu)[2]
    return _mm(act, wd.reshape(-1, wd.shape[2]), "nn", "ffn_down_fwd", ROW_TILE, wd.shape[2])


def _ffn_block_fwd(h, wg, wu, wd):
    g, u, act = _ffn_in(h, wg, wu)
    y = _mm(act, wd.reshape(-1, wd.shape[2]), "nn", "ffn_down_fwd", ROW_TILE, wd.shape[2])
    return y, (h, wg, wu, wd, g, u, act)


def _ffn_block_bwd(res, dy):
    h, wg, wu, wd, g, u, act = res
    dg, du = _ffn_mid_bwd(dy, wd, g, u)
    dh = _ffn_dh(dg, du, wg, wu)
    n_sh = wg.shape[0]
    dwg = _ffn_dw_in(h, dg, n_sh, "ffn_gate_dw")
    dwu = _ffn_dw_in(h, du, n_sh, "ffn_up_dw")
    dwd = _mm(act, dy, "tn", "ffn_down_dw", 256, wd.shape[2], out_dtype=BF16).reshape(wd.shape)
    return dh, dwg, dwu, dwd


_ffn_block.defvjp(_ffn_block_fwd, _ffn_block_bwd)


def _split_cols(x, cuts, ct_dtype):
    cuts = tuple(cuts)

    @jax.custom_vjp
    def op(x):
        return tuple(x[:, a:b] for a, b in zip((0,) + cuts, cuts + (x.shape[1],)))

    def fwd(x):
        return op(x), None

    def bwd(_, cts):
        return (jnp.concatenate([c.astype(ct_dtype) for c in cts], axis=1),)

    op.defvjp(fwd, bwd)
    return op(x)


def _swap_halves(w):
    half = w.shape[-1] // 2
    return jnp.concatenate([w[..., half:], w[..., :half]], axis=-1)


def _pad_lanes(w):
    return jnp.concatenate([w, jnp.zeros(w.shape[:-1] + (LANES - w.shape[-1],), w.dtype)], axis=-1)


def _join_cols(shards):
    return shards.transpose(1, 0, 2).reshape(shards.shape[1], -1)


def _mod_parts(mod):
    return [mod[:, i * D_MODEL:(i + 1) * D_MODEL] for i in range(N_MOD)]


def _local_loss(x, mod, p, cos, sin, target):
    return _ffn_stage(x, _mixing_stage(x, mod, p, cos, sin), mod, p, target)


def _mixing_stage(x, mod, p, cos, sin):
    shift1, scale1 = _mod_parts(mod)[:2]

    w_in = _join_cols(p["w_in"])
    k_rope_w = w_in[:, 2176:2240]
    w_in_ext = jnp.concatenate([w_in[:, :2176], _pad_lanes(k_rope_w), _pad_lanes(_swap_halves(k_rope_w)),
                                jnp.zeros((D_MODEL, LANES), w_in.dtype)], axis=1)
    (h1,) = _make_rowwise("pre_attn", _f_pre_attn, 1, 3, [D_MODEL], [True], out_dtypes=[BF16])(
        x, p["norm_attn"], scale1, shift1)
    proj = _make_linear("in_proj", 512, 640)(h1, w_in_ext)
    q_sb, k_sb, v_sb, cq, ckv, kr, kr_sw, _ = _split_cols(proj, (512, 1024, 1536, 1920, 2176, 2304, 2432), BF16)

    o_sb = _sb_attention(q_sb, k_sb, v_sb)

    wq = _join_cols(p["w_q_up"]).reshape(MLA_Q_RANK, MLA_HEADS, MLA_QK)
    wq_n, wq_r = wq[:, :, :MLA_NOPE], wq[:, :, MLA_NOPE:]
    w_q_ext = jnp.concatenate([wq_n.reshape(MLA_Q_RANK, -1), _pad_lanes(wq_r).reshape(MLA_Q_RANK, -1),
                               _pad_lanes(_swap_halves(wq_r)).reshape(MLA_Q_RANK, -1)], axis=1)
    wkv = _join_cols(p["w_kv_up"]).reshape(MLA_KV_RANK, MLA_HEADS, MLA_NOPE + MLA_V)
    w_kv_ext = jnp.concatenate([wkv[:, :, :MLA_NOPE].reshape(MLA_KV_RANK, -1),
                                wkv[:, :, MLA_NOPE:].reshape(MLA_KV_RANK, -1)], axis=1)
    cqn, ckvn = _make_rowwise("mla_a", _f_mla_a, 2, 2, [MLA_Q_RANK, MLA_KV_RANK], [True, True],
                              out_dtypes=[BF16, BF16], grad_dtypes=[BF16, BF16])(
        cq, ckv, p["q_a_norm"], p["kv_a_norm"])
    qall = _make_linear("q_up", 384, 768)(cqn, w_q_ext)
    kvall = _make_linear("kv_up", 256, 1024)(ckvn, w_kv_ext)
    kn_all, v_mla = _split_cols(kvall, (512,), BF16)
    gq = p["q_norm"]
    gkr = p["k_rope_norm"]
    qn, qr, kn, krr = _make_rowwise("mla_b", _f_mla_b, 6, 6, [512, 512, 512, LANES],
                                    [True, True, True, True, False, False],
                                    out_dtypes=[BF16] * 4, grad_dtypes=[BF16] * 4)(
        qall, kn_all, kr, kr_sw, cos, sin,
        gq[:, :MLA_NOPE], _pad_lanes(gq[:, MLA_NOPE:]), _pad_lanes(_swap_halves(gq[:, MLA_NOPE:])),
        p["k_nope_norm"], _pad_lanes(gkr), _pad_lanes(_swap_halves(gkr)))
    o_mla = _mla_attention(qn, qr, kn, krr, v_mla)

    (mixed,) = _make_rowwise("post_attn", _f_post_attn, 2, 2, [D_MODEL], [True, True])(
        o_sb, o_mla, p["out_norm_sb"], p["out_norm_mla"])
    return mixed


def _ffn_stage(x, mixed, mod, p, target):
    _, _, gate1, shift2, scale2, gate2 = _mod_parts(mod)
    attn = _make_linear("out_proj", 512, 512)(mixed, p["w_out"].reshape(D_MODEL, D_MODEL))

    x2, h2 = _make_rowwise("pre_ffn", _f_pre_ffn, 2, 4, [D_MODEL, D_MODEL], [True, True],
                           out_dtypes=[F32, BF16], grad_dtypes=[F32, BF16])(
        x, attn, gate1, p["norm_ffn"], scale2, shift2)
    ffn = _ffn_block(h2, p["w_gate"], p["w_up"], p["w_down"])
    (row_loss,) = _make_rowwise("loss", _f_loss, 3, 1, [1], [True, True, False], grad_dtypes=[F32, BF16])(
        x2, ffn, target, gate2)
    return 0.5 * jnp.sum(row_loss)


def _my_place():
    return lax.axis_index("x"), lax.axis_index("y"), lax.axis_index("c")


def _all_gather_small(block, name):
    m_per, n = block.shape

    def body(x_ref, out_ref, send_sems, recv_sems, local_sem):
        x, y, c = _my_place()
        me, sibling = (x, y, c), (x, y, 1 - c)
        chips = [(1 - x, y), (x, 1 - y), (1 - x, 1 - y)]

        def rows(px, py, pc):
            return out_ref.at[pl.ds((4 * px + 2 * py + pc) * m_per, m_per), :]

        def copy(k, blk, to, src=None):
            return pltpu.make_async_remote_copy(
                src_ref=rows(*blk) if src is None else src, dst_ref=rows(*blk),
                send_sem=send_sems.at[k], recv_sem=recv_sems.at[k], device_id=to, device_id_type=MESH)

        mine = pltpu.make_async_copy(x_ref, rows(*me), local_sem)
        mine.start()
        first = [copy(0, me, sibling, src=x_ref)]
        first += [copy(1 + j, me, (*chip, c), src=x_ref) for j, chip in enumerate(chips)]
        for cp in first:
            cp.start()
        passed = [copy(4 + j, (*chip, c), sibling) for j, chip in enumerate(chips)]
        for j, chip in enumerate(chips):
            copy(1 + j, (*chip, c), me).wait_recv()
            passed[j].start()
        copy(0, sibling, me).wait_recv()
        for j, chip in enumerate(chips):
            copy(4 + j, (*chip, 1 - c), me).wait_recv()
        for cp in first + passed:
            cp.wait_send()
        mine.wait()

    return pl.pallas_call(
        body, name=name,
        out_shape=jax.ShapeDtypeStruct((N_DEV * m_per, n), block.dtype),
        in_specs=[pl.BlockSpec(memory_space=pltpu.VMEM)],
        out_specs=pl.BlockSpec(memory_space=pltpu.VMEM),
        scratch_shapes=[pltpu.SemaphoreType.DMA((7,)), pltpu.SemaphoreType.DMA((7,)), pltpu.SemaphoreType.DMA],
    )(block)


EARLY = ("w_in", "w_q_up", "w_kv_up")
LATE = ("w_out", "w_gate", "w_up", "w_down")
BIG = EARLY + LATE
TRANSPOSED_UPDATE = ("w_in", "w_gate", "w_up")
HALF_AXIS = {"w_in": 0, "w_q_up": 0, "w_kv_up": 0, "w_out": 0, "w_gate": 0, "w_up": 0, "w_down": 1}


def _half(ref, h, axis, lead=()):
    trail = ref.shape[len(lead):]
    idx = list(lead) + [slice(None)] * len(trail)
    at = len(trail) - 2 + axis
    n2 = trail[at] // 2
    idx[len(lead) + at] = pl.ds(h * n2, n2)
    return ref.at[tuple(idx)]


def _half_shape(shape, axis):
    shape = list(shape)
    shape[len(shape) - 2 + axis] //= 2
    return tuple(shape)


def _remote(src, dst, send_sems, recv_sems, k, to):
    return pltpu.make_async_remote_copy(src_ref=src, dst_ref=dst, send_sem=send_sems.at[k],
                                        recv_sem=recv_sems.at[k], device_id=to, device_id_type=MESH)


def _gather_weights(names, shards, after):
    n_w = len(shards)
    axes = [HALF_AXIS[n] for n in names]

    def body(*refs):
        w_refs, out_refs, token = refs[:n_w], refs[n_w + 1:2 * n_w + 1], refs[2 * n_w + 1]
        send_sems, recv_sems, local_sems = refs[2 * n_w + 2:]
        token[...] = jnp.zeros_like(token)
        x, y, c = _my_place()
        sibling = (x, y, 1 - c)
        chips = [(1 - x, y), (x, 1 - y), (1 - x, 1 - y)]
        me = 2 * x + y
        mine =[pltpu.make_async_copy(w, o.at[me], local_sems.at[i]) for i, (w, o) in enumerate(zip(w_refs, out_refs))]
        for cp in mine:
            cp.start()
        first = [_remote(_half(w_refs[i], c, axes[i]), _half(out_refs[i], c, axes[i], (me,)),
                         send_sems, recv_sems, 6 * i + j, (*chip, c))
                 for i in range(n_w) for j, chip in enumerate(chips)]
        for cp in first:
            cp.start()
        passed = []
        for j, (cx, cy) in enumerate(chips):
            for i in range(n_w):
                blk = _half(out_refs[i], c, axes[i], (2 * cx + cy,))
                _remote(blk, blk, send_sems, recv_sems, 6 * i + j, (cx, cy, c)).wait_recv()
                cp = _remote(blk, blk, send_sems, recv_sems, 6 * i + 3 + j, sibling)
                cp.start()
                passed.append(cp)
        for j, (cx, cy) in enumerate(chips):
            for i in range(n_w):
                blk = _half(out_refs[i], 1 - c, axes[i], (2 * cx + cy,))
                _remote(blk, blk, send_sems, recv_sems, 6 * i + 3 + j, sibling).wait_recv()
        for cp in first + passed:
            cp.wait_send()
        for cp in mine:
            cp.wait()

    outs = pl.pallas_call(
        body, name="gather_weights",
        out_shape=[jax.ShapeDtypeStruct((N_CHIPS,) + s.shape, s.dtype) for s in shards]
        + [jax.ShapeDtypeStruct((8, LANES), F32)],
        in_specs=[ANY] * (n_w + 1), out_specs=[ANY] * n_w + [pl.BlockSpec(memory_space=pltpu.VMEM)],
        scratch_shapes=[pltpu.SemaphoreType.DMA((6 * n_w,)), pltpu.SemaphoreType.DMA((6 * n_w,)),
                        pltpu.SemaphoreType.DMA((n_w,))],
    )(*shards, after)
    return outs[:n_w], outs[n_w]


def _pair_exchange(names, grads, call_name):
    n_w = len(grads)
    axes = [HALF_AXIS[n] for n in names]

    def body(*refs):
        g_refs, t_refs = refs[:n_w], refs[n_w:2 * n_w]
        send_sems, recv_sems = refs[2 * n_w:]
        x, y, c = _my_place()
        sends = [_remote(_half(g_refs[i], 1 - c, axes[i]), t_refs[i], send_sems, recv_sems, i, (x, y, 1 - c))
                 for i in range(n_w)]
        for cp in sends:
            cp.start()
        for cp in sends:
            cp.wait_recv()
        for cp in sends:
            cp.wait_send()

    return pl.pallas_call(
        body, name=call_name,
        out_shape=[jax.ShapeDtypeStruct(_half_shape(g.shape, a), g.dtype) for g, a in zip(grads, axes)],
        in_specs=[ANY] * n_w, out_specs=[ANY] * n_w,
        scratch_shapes=[pltpu.SemaphoreType.DMA((n_w,)), pltpu.SemaphoreType.DMA((n_w,))],
    )(*grads)


def _chip_scatter(pair_sums):
    n_w = len(pair_sums)

    def body(*refs):
        s_refs, p_refs = refs[:n_w], refs[n_w:2 * n_w]
        send_sems, recv_sems = refs[2 * n_w:]
        x, y, c = _my_place()
        chips = [(1 - x, y), (x, 1 - y), (1 - x, 1 - y)]
        sends = [_remote(s_refs[i].at[2 * cx + cy], p_refs[i].at[j], send_sems, recv_sems, 3 * i + j, (cx, cy, c))
                 for i in range(n_w) for j, (cx, cy) in enumerate(chips)]
        for cp in sends:
            cp.start()
        for cp in sends:
            cp.wait_recv()
        for cp in sends:
            cp.wait_send()

    return pl.pallas_call(
        body, name="grad_chip_scatter",
        out_shape=[jax.ShapeDtypeStruct((N_CHIPS - 1,) + s.shape[1:], s.dtype) for s in pair_sums],
        in_specs=[ANY] * n_w, out_specs=[ANY] * n_w,
        scratch_shapes=[pltpu.SemaphoreType.DMA((3 * n_w,)), pltpu.SemaphoreType.DMA((3 * n_w,))],
    )(*pair_sums)


def _sibling_join(halves, name, after):
    n_w = len(halves)

    def body(*refs):
        s_refs, j_refs = refs[:n_w], refs[n_w + 1:2 * n_w + 1]
        send_sems, recv_sems = refs[2 * n_w + 1:]
        x, y, c = _my_place()
        sends = [_remote(s_refs[i], j_refs[i], send_sems, recv_sems, i, (x, y, 1 - c)) for i in range(n_w)]
        for cp in sends:
            cp.start()
        for cp in sends:
            cp.wait_recv()
        for cp in sends:
            cp.wait_send()

    return pl.pallas_call(
        body, name=name,
        out_shape=[jax.ShapeDtypeStruct(s.shape, s.dtype) for s in halves],
        in_specs=[ANY] * (n_w + 1), out_specs=[ANY] * n_w,
        scratch_shapes=[pltpu.SemaphoreType.DMA((n_w,)), pltpu.SemaphoreType.DMA((n_w,))],
    )(*halves, after)


HBM_SPEC = pl.BlockSpec(memory_space=pltpu.HBM)
SEM_SPEC = pl.BlockSpec(memory_space=pltpu.SEMAPHORE)
DATAFLOW = pltpu.SideEffectType.DATAFLOW_SIDE_EFFECTING


def _in_hbm(a):
    return pltpu.with_memory_space_constraint(a, pltpu.HBM)


def _exchange_start(name, srcs, lands, plan, n_copies, after):
    n = len(srcs)

    def body(*refs):
        src_refs, land_refs = refs[:n], refs[n:2 * n]
        send_sems, recv_sems = refs[2 * n + 1], refs[2 * n + 2]
        token = refs[-1]
        for k, (src, dst, to) in enumerate(plan(src_refs, land_refs)):
            _remote(src, dst, send_sems, recv_sems, k, to).start()
        token[...] = jnp.zeros_like(token)

    outs = pl.pallas_call(
        body, name=name,
        out_shape=(pltpu.SemaphoreType.DMA((n_copies,)), pltpu.SemaphoreType.DMA((n_copies,)),
                   *[pltpu.HBM(a.shape, a.dtype) for a in srcs], *[pltpu.HBM(a.shape, a.dtype) for a in lands],
                   jax.ShapeDtypeStruct((8, LANES), F32)),
        in_specs=[HBM_SPEC] * (2 * n) + [ANY],
        out_specs=(SEM_SPEC, SEM_SPEC, *[HBM_SPEC] * (2 * n), pl.BlockSpec(memory_space=pltpu.VMEM)),
        input_output_aliases={i: 2 + i for i in range(2 * n)},
        compiler_params=pltpu.CompilerParams(has_side_effects=DATAFLOW),
    )(*[_in_hbm(a) for a in srcs], *[_in_hbm(a) for a in lands], after)
    return outs[0], outs[1], outs[2:2 + n], outs[2 + n:2 + 2 * n], outs[-1]


def _exchange_wait(name, started, plan, after):
    send_sems, recv_sems, srcs, lands, _ = started
    n = len(srcs)

    def body(*refs):
        src_refs, land_refs = refs[:n], refs[n:2 * n]
        s_sems, r_sems = refs[2 * n], refs[2 * n + 1]
        for k, (src, dst, to) in enumerate(plan(src_refs, land_refs)):
            cp = _remote(src, dst, s_sems, r_sems, k, to)
            cp.wait_send()
            cp.wait_recv()

    outs = pl.pallas_call(
        body, name=name,
        out_shape=tuple(pltpu.HBM(a.shape, a.dtype) for a in list(srcs) + list(lands)),
        in_specs=[HBM_SPEC] * (2 * n) + [SEM_SPEC, SEM_SPEC, ANY],
        out_specs=tuple([HBM_SPEC] * (2 * n)),
        input_output_aliases={i: i for i in range(2 * n)},
        compiler_params=pltpu.CompilerParams(has_side_effects=DATAFLOW),
    )(*srcs, *lands, send_sems, recv_sems, after)
    return outs[:n], outs[n:]


def _late_gather_plan(src_refs, land_refs):
    x, y, c = _my_place()
    chips = [(1 - x, y), (x, 1 - y), (1 - x, 1 - y)]
    return [(src, land.at[2 * x + y], (cx, cy, c)) for src, land in zip(src_refs, land_refs) for cx, cy in chips]


def _late_scatter_plan(src_refs, land_refs):
    x, y, c = _my_place()
    chips = [(1 - x, y), (x, 1 - y), (1 - x, 1 - y)]
    return [(src.at[2 * cx + cy], land.at[j], (cx, cy, c))
            for src, land in zip(src_refs, land_refs) for j, (cx, cy) in enumerate(chips)]


def _row_tile(rows, mult=16, limit=ROW_TILE):
    return max(d for d in range(mult, limit + 1, mult) if rows % d == 0)


def _pair_sum(place, g, theirs, axis, name):
    nj, rr, cc = theirs.shape
    tr = _row_tile(rr, limit=1024)
    nb = rr // tr
    if axis == 0:
        g_map = lambda j, i, pr: (j, pr[0] * nb + i, 0)
    else:
        g_map = lambda j, i, pr: (j, i, pr[0])

    def body(pr, g_ref, t_ref, o_ref):
        o_ref[...] = (g_ref[...].astype(F32) + t_ref[...].astype(F32)).astype(BF16)

    spec = pl.BlockSpec((None, tr, cc), lambda j, i, pr: (j, i, 0))
    return pl.pallas_call(
        body, name=name,
        grid_spec=pltpu.PrefetchScalarGridSpec(
            num_scalar_prefetch=1, grid=(nj, nb),
            in_specs=[pl.BlockSpec((None, tr, cc), g_map), spec], out_specs=spec),
        out_shape=jax.ShapeDtypeStruct(theirs.shape, BF16))(place, g, theirs)


def _chip_sum(place, pair_sums, parts, name, transposed):
    _, rr, cc = parts.shape
    tr = _row_tile(rr, LANES) if transposed else _row_tile(rr, limit=1024)

    def body(pr, h_ref, p_ref, o_ref):
        acc = p_ref[0].astype(F32)
        for j in range(1, N_CHIPS - 1):
            acc = acc + p_ref[j].astype(F32)
        acc = acc + h_ref[...].astype(F32)
        o_ref[...] = (acc.T if transposed else acc).astype(BF16)

    out_spec = pl.BlockSpec((cc, tr), lambda i, pr: (0, i)) if transposed else pl.BlockSpec((tr, cc), lambda i, pr: (i, 0))
    return pl.pallas_call(
        body, name=name,
        grid_spec=pltpu.PrefetchScalarGridSpec(
            num_scalar_prefetch=1, grid=(rr // tr,),
            in_specs=[pl.BlockSpec((None, tr, cc), lambda i, pr: (pr[1], i, 0)),
                      pl.BlockSpec((N_CHIPS - 1, tr, cc), lambda i, pr: (0, i, 0))],
            out_specs=out_spec),
        out_shape=jax.ShapeDtypeStruct((cc, rr) if transposed else (rr, cc), BF16))(place, pair_sums, parts)


def _silu(v):
    return v / (1.0 + jnp.exp(-v))


def _ada_fwd(c_all, w_shard, b_shard):
    def body(c_ref, w_ref, b_ref, o_ref):
        o_ref[...] = jnp.dot(_silu(c_ref[...]), w_ref[...], precision=lax.Precision.HIGHEST,
                             preferred_element_type=F32) + b_ref[...]

    return pl.pallas_call(body, name="ada_fwd", out_shape=jax.ShapeDtypeStruct((c_all.shape[0], w_shard.shape[1]), F32),
                          compiler_params=pltpu.CompilerParams(vmem_limit_bytes=MM_VMEM_LIMIT))(c_all, w_shard, b_shard)


def _ada_bwd(c_all, dmod_cols):
    def body(c_ref, d_ref, o_ref):
        o_ref[...] = lax.dot_general(_silu(c_ref[...]), d_ref[...], (((0,), (0,)), ((), ())),
                                     precision=lax.Precision.HIGHEST, preferred_element_type=F32)

    return pl.pallas_call(body, name="ada_bwd", out_shape=jax.ShapeDtypeStruct((c_all.shape[1], dmod_cols.shape[1]), F32),
                          compiler_params=pltpu.CompilerParams(vmem_limit_bytes=MM_VMEM_LIMIT))(c_all, dmod_cols)


def _adamw_math(w, g, m, v):
    m = ADAM_B1 * m + (1.0 - ADAM_B1) * g
    v = ADAM_B2 * v + (1.0 - ADAM_B2) * (g * g)
    m_hat = m / (1.0 - ADAM_B1 ** ADAM_STEP)
    v_hat = v / (1.0 - ADAM_B2 ** ADAM_STEP)
    delta = -ADAM_LR * (m_hat / (jnp.sqrt(v_hat) + ADAM_EPS) + ADAM_WD * w)
    return delta, m, v


def _adamw(w, g, m, v, name):
    r, ccols = w.shape
    tr = max(d for d in range(8, ROW_TILE + 1, 8) if r % d == 0)
    spec = pl.BlockSpec((tr, ccols), lambda i: (i, 0))

    def body(w_ref, g_ref, m_ref, v_ref, d_ref, nm_ref, nv_ref):
        d_ref[...], nm_ref[...], nv_ref[...] = _adamw_math(w_ref[...], g_ref[...], m_ref[...], v_ref[...])

    return pl.pallas_call(body, name=name, grid=(r // tr,), in_specs=[spec] * 4, out_specs=[spec] * 3,
                          out_shape=[jax.ShapeDtypeStruct(w.shape, F32)] * 3,
                          compiler_params=pltpu.CompilerParams(vmem_limit_bytes=MM_VMEM_LIMIT))(w, g, m, v)


def _adamw_small(w, g_all, m, v):
    def body(w_ref, g_ref, m_ref, v_ref, gs_ref, d_ref, nm_ref, nv_ref):
        g = g_ref[0]
        for d in range(1, N_DEV):
            g = g + g_ref[d]
        gs_ref[...] = g
        d_ref[...], nm_ref[...], nv_ref[...] = _adamw_math(w_ref[...], g, m_ref[...], v_ref[...])

    return pl.pallas_call(body, name="adamw_small", out_shape=[jax.ShapeDtypeStruct(w.shape, F32)] * 4)(w, g_all, m, v)


def _adamw_halves(place, w, own, sib, m, v, axis, name, after):
    r, cc = w.shape
    if axis == 0:
        rows, gc = own.shape[0], own.shape[1]
        tr = _row_tile(rows)
        nb = rows // tr
        w_spec = pl.BlockSpec((tr, cc), lambda h, i, pr: (h * nb + i, 0))
        g_spec = pl.BlockSpec((tr, gc), lambda h, i, pr: (i, 0))
    else:
        tr = _row_tile(r)
        nb = r // tr
        gc = own.shape[1]
        w_spec = pl.BlockSpec((tr, gc), lambda h, i, pr: (i, h))
        g_spec = pl.BlockSpec((tr, gc), lambda h, i, pr: (i, 0))
    wc = w_spec.block_shape[1]

    def body(pr, w_ref, o_ref, s_ref, m_ref, v_ref, after_ref, g_ref, d_ref, nm_ref, nv_ref):
        g = jnp.where(pl.program_id(0) == pr[0], o_ref[...], s_ref[...]).astype(F32)[:, :wc]
        g_ref[...] = g
        d_ref[...], nm_ref[...], nv_ref[...] = _adamw_math(w_ref[...], g, m_ref[...], v_ref[...])

    return pl.pallas_call(
        body, name=name,
        grid_spec=pltpu.PrefetchScalarGridSpec(
            num_scalar_prefetch=1, grid=(2, nb),
            in_specs=[w_spec, g_spec, g_spec, w_spec, w_spec, ANY], out_specs=[w_spec] * 4),
        out_shape=[jax.ShapeDtypeStruct(w.shape, F32)] * 4,
        compiler_params=pltpu.CompilerParams(vmem_limit_bytes=MM_VMEM_LIMIT))(place, w, own, sib, m, v, after)


SMALL = ("b_ada", "norm_attn", "norm_ffn", "q_a_norm", "kv_a_norm", "q_norm", "k_nope_norm", "k_rope_norm",
         "out_norm_sb", "out_norm_mla")
WEIGHTS = ("w_ada", "b_ada", "norm_attn", "norm_ffn", "w_in", "q_a_norm", "w_q_up", "kv_a_norm", "w_kv_up",
           "q_norm", "k_nope_norm", "k_rope_norm", "out_norm_sb", "out_norm_mla", "w_out", "w_gate", "w_up",
           "w_down")


def kernel(x, c, positions, w_ada, b_ada, norm_attn, norm_ffn, w_in, q_a_norm, w_q_up, kv_a_norm, w_kv_up, q_norm, k_nope_norm, k_rope_norm, out_norm_sb, out_norm_mla, w_out, w_gate, w_up, w_down, loss_target, m_w_ada, m_b_ada, m_norm_attn, m_norm_ffn, m_w_in, m_q_a_norm, m_w_q_up, m_kv_a_norm, m_w_kv_up, m_q_norm, m_k_nope_norm, m_k_rope_norm, m_out_norm_sb, m_out_norm_mla, m_w_out, m_w_gate, m_w_up, m_w_down, v_w_ada, v_b_ada, v_norm_attn, v_norm_ffn, v_w_in, v_q_a_norm, v_w_q_up, v_kv_a_norm, v_w_kv_up, v_q_norm, v_k_nope_norm, v_k_rope_norm, v_out_norm_sb, v_out_norm_mla, v_w_out, v_w_gate, v_w_up, v_w_down):
    local = dict(locals())
    w = {n: local[n][0] for n in WEIGHTS}
    m = {n: local["m_" + n][0] for n in WEIGHTS}
    v = {n: local["v_" + n][0] for n in WEIGHTS}
    small = {n: w[n].reshape(1, -1) for n in SMALL}
    ix, iy, ic = _my_place()
    chip = 2 * ix + iy
    dev = 2 * chip + ic
    xs, target = x[0], loss_target[0]
    seq = xs.shape[0]

    c_all = _all_gather_small(c.reshape(8, LANES), "gather_c").reshape(N_DEV, D_MODEL)
    ada_cols = w["w_ada"].shape[1]
    b_cols = lax.dynamic_slice_in_dim(small["b_ada"], chip * ada_cols, ada_cols, axis=1)
    mod_cols = _ada_fwd(c_all, w["w_ada"], b_cols)
    mod_all = _all_gather_small(mod_cols, "gather_mod").reshape(N_CHIPS, 2, N_DEV, ada_cols)
    mod = lax.dynamic_index_in_dim(mod_all[:, 0], dev, axis=1, keepdims=False).reshape(1, N_MOD * D_MODEL)

    ff_pad = FF_SHARD_PAD - FF_SHARD
    pads = {"w_gate": ((0, 0), (0, ff_pad)), "w_up": ((0, 0), (0, ff_pad)), "w_down": ((0, ff_pad), (0, 0))}
    shards = {n: jnp.pad(w[n].astype(BF16), pads[n]) if n in pads else w[n].astype(BF16) for n in BIG}
    early, early_done = _gather_weights(EARLY, [shards[n] for n in EARLY], mod)
    gathered = dict(zip(EARLY, early))
    lands = [lax.dynamic_update_index_in_dim(lax.empty((N_CHIPS,) + shards[n].shape, BF16), shards[n], chip, 0)
             for n in LATE]
    late_gather = _exchange_start("gather_late_start", [shards[n] for n in LATE], lands, _late_gather_plan,
                                  3 * len(LATE), early_done)

    half = MLA_ROPE // 2
    freqs = 1.0 / (ROPE_THETA ** (np.arange(half, dtype=np.float32) / half))
    zeros = np.zeros(LANES - MLA_ROPE, np.float32)
    freqs_row = jnp.asarray(np.concatenate([freqs, freqs, zeros]).astype(np.float32)[None])
    sign_row = jnp.asarray(np.concatenate([-np.ones(half), np.ones(half), zeros]).astype(np.float32)[None])
    cos, sin = _rope_tables(positions.reshape(seq, 1), freqs_row, sign_row)

    place = jnp.stack([ic, chip]).astype(jnp.int32)
    small_params = {n: small[n] for n in SMALL if n != "b_ada"}
    mod = mod + late_gather[4][0, 0]

    def pair_sums_of(names, grads, call_name):
        theirs = _pair_exchange(names, grads, call_name)
        return [_pair_sum(place, gr, th, HALF_AXIS[n], "grad_pair_sum_" + n) for n, gr, th in zip(names, grads, theirs)]

    p1 = {**{n: gathered[n] for n in EARLY}, **small_params}
    mixed, mixing_vjp = jax.vjp(lambda x_, mod_, p_: _mixing_stage(x_, mod_, p_, cos, sin), xs, mod, p1)
    _, landed = _exchange_wait("gather_late_wait", late_gather, _late_gather_plan, mixed)
    p2 = {**dict(zip(LATE, landed)), **small_params}
    loss_part, ffn_vjp = jax.vjp(lambda x_, mixed_, mod_, p_: _ffn_stage(x_, mixed_, mod_, p_, target), xs, mixed, mod, p2)
    gx2, gmixed, gmod2, gp2 = ffn_vjp(jnp.ones((), F32))
    late_sums = pair_sums_of(LATE, [gp2[n] for n in LATE], "grad_pair_exchange_late")
    late_scatter = _exchange_start(
        "grad_scatter_late_start", late_sums,
        [lax.empty((N_CHIPS - 1,) + s.shape[1:], BF16) for s in late_sums], _late_scatter_plan, 3 * len(LATE), gx2)
    gx1, gmod1, gp1 = mixing_vjp(gmixed + late_scatter[4][0, 0])
    gx = gx1 + gx2
    gmod = gmod1 + gmod2
    gp = {n: gp1[n] + gp2[n] for n in small_params}
    loss = lax.psum(loss_part, ("x", "y", "c"))

    small_names = [n for n in SMALL if n != "b_ada"]
    small_vec = jnp.concatenate([gmod] + [gp[n] for n in small_names], axis=1)
    n_small = small_vec.shape[1]
    small_all = _all_gather_small(small_vec.reshape(8, n_small // 8), "gather_small").reshape(N_DEV, 8, n_small // 8)

    g, delta, new_m, new_v = {}, {}, {}, {}

    def reduce_halves(names, sums, parts, join_name, after):
        own = [_chip_sum(place, ps, pt, "grad_chip_sum_" + n, n in TRANSPOSED_UPDATE) for n, ps, pt in zip(names, sums, parts)]
        return own, _sibling_join(own, join_name, after)

    def update(names, own, sib, after):
        for n, o, s in zip(names, own, sib):
            if n in TRANSPOSED_UPDATE:
                res = _adamw_halves(place, w[n].T, o, s, m[n].T, v[n].T, 1, "adamw_" + n, after)
                g[n], delta[n], new_m[n], new_v[n] = [r.T for r in res]
            else:
                g[n], delta[n], new_m[n], new_v[n] = _adamw_halves(place, w[n], o, s, m[n], v[n], HALF_AXIS[n],
                                                                   "adamw_" + n, after)

    late_sums, late_parts = _exchange_wait("grad_scatter_late_wait", late_scatter, _late_scatter_plan, gx)
    own_late, sib_late = reduce_halves(LATE, late_sums, late_parts, "grad_sibling_join_late", small_all)
    early_sums = pair_sums_of(EARLY, [gp1[n] for n in EARLY], "grad_pair_exchange_early")
    early_scatter = _exchange_start(
        "grad_scatter_early_start", early_sums,
        [lax.empty((N_CHIPS - 1,) + s.shape[1:], BF16) for s in early_sums], _late_scatter_plan, 3 * len(EARLY),
        sib_late[0])
    update(LATE, own_late, sib_late, early_scatter[4])

    def pack_small(d):
        return jnp.concatenate([d[n].reshape(1, -1) for n in SMALL], axis=1).reshape(8, n_small // 8)

    gs, ds, ms, vs = _adamw_small(pack_small(w), small_all, pack_small(m), pack_small(v))
    sizes = [w[n].size for n in SMALL]
    offs = np.concatenate([[0], np.cumsum(sizes)])

    def unpack_small(a):
        flat = a.reshape(-1)
        return {n: flat[offs[i]:offs[i + 1]].reshape(w[n].shape) for i, n in enumerate(SMALL)}

    for d, packed in zip((g, delta, new_m, new_v), (gs, ds, ms, vs)):
        d.update(unpack_small(packed))

    dmod_all = small_all.reshape(N_DEV, n_small)[:, :N_MOD * D_MODEL]
    g["w_ada"] = _ada_bwd(c_all, lax.dynamic_slice_in_dim(dmod_all, chip * ada_cols, ada_cols, axis=1))
    delta["w_ada"], new_m["w_ada"], new_v["w_ada"] = _adamw(w["w_ada"], g["w_ada"], m["w_ada"], v["w_ada"], "adamw_w_ada")

    early_sums, early_parts = _exchange_wait("grad_scatter_early_wait", early_scatter, _late_scatter_plan,
                                             delta["w_ada"])
    own_early, sib_early = reduce_halves(EARLY, early_sums, early_parts, "grad_sibling_join_early", delta["w_ada"])
    update(EARLY, own_early, sib_early, sib_early[0])

    def outs(d):
        return [d[n][None] for n in WEIGHTS]

    return (loss, gx[None], *outs(g), *outs(delta), *outs(new_m), *outs(new_v))
```

```python
import functools
import math

import numpy as np
import jax
import jax.numpy as jnp
from jax import lax
from jax.experimental import pallas as pl
from jax.experimental.pallas import tpu as pltpu

F32 = jnp.float32
BF16 = jnp.bfloat16
MESH = pl.DeviceIdType.MESH
ANY = pl.BlockSpec(memory_space=pl.ANY)

D_MODEL = 1024
SB_HEADS = 8
SB_HEAD_DIM = 64
SB_WIDTH = 512
MLA_HEADS = 4
MLA_NOPE = 128
MLA_ROPE = 64
MLA_QK = 192
MLA_V = 128
MLA_Q_RANK = 384
MLA_KV_RANK = 256
D_FF = 2816
N_MOD = 6
ROPE_THETA = 10000.0
EPS = 1e-6
LANES = 128

ADAM_LR = 0.001
ADAM_B1 = 0.9
ADAM_B2 = 0.999
ADAM_EPS = 1e-08
ADAM_WD = 0.01
ADAM_STEP = 10

N_CHIPS = 4
N_DEV = 8
ROW_TILE = 256
ATT_BLK = 256
MM_VMEM_LIMIT = 56 * 1024 * 1024
FF_SHARD = D_FF // N_CHIPS
FF_SHARD_PAD = 768


def _mm(a, b, mode, name, tm, tn, out_dtype=F32):
    if mode == "nn":
        (m, k), n = a.shape, b.shape[1]
        a_spec = pl.BlockSpec((tm, k), lambda j, i: (i, 0))
        b_spec = pl.BlockSpec((k, tn), lambda j, i: (0, j))
        dims = (((1,), (0,)), ((), ()))
    elif mode == "nt":
        (m, k), n = a.shape, b.shape[0]
        a_spec = pl.BlockSpec((tm, k), lambda j, i: (i, 0))
        b_spec = pl.BlockSpec((tn, k), lambda j, i: (j, 0))
        dims = (((1,), (1,)), ((), ()))
    else:
        (k, m), n = a.shape, b.shape[1]
        a_spec = pl.BlockSpec((k, tm), lambda j, i: (0, i))
        b_spec = pl.BlockSpec((k, tn), lambda j, i: (0, j))
        dims = (((0,), (0,)), ((), ()))
    assert m % tm == 0 and n % tn == 0, (name, m, n, tm, tn)

    def body(a_ref, b_ref, o_ref):
        o_ref[...] = lax.dot_general(a_ref[...].astype(BF16), b_ref[...].astype(BF16), dims,
                                     preferred_element_type=F32).astype(out_dtype)

    return pl.pallas_call(
        body, name=name, grid=(n // tn, m // tm),
        in_specs=[a_spec, b_spec],
        out_specs=pl.BlockSpec((tm, tn), lambda j, i: (i, j)),
        out_shape=jax.ShapeDtypeStruct((m, n), out_dtype),
        compiler_params=pltpu.CompilerParams(dimension_semantics=("arbitrary", "arbitrary"),
                                             vmem_limit_bytes=MM_VMEM_LIMIT),
    )(a, b)


def _make_linear(name, tk_w, tn_w):
    @jax.custom_vjp
    def op(a, w):
        return _mm(a, w, "nn", name + "_fwd", ROW_TILE, w.shape[1])

    def fwd(a, w):
        return op(a, w), (a, w)

    def bwd(res, dy):
        a, w = res
        da = _mm(dy, w, "nt", name + "_dx", ROW_TILE, w.shape[0])
        dw = _mm(a, dy, "tn", name + "_dw", tk_w, tn_w, out_dtype=BF16)
        return da, dw

    op.defvjp(fwd, bwd)
    return op


def _make_linear_split(name, widths, tk_w):
    starts = [sum(widths[:g]) for g in range(len(widths))]

    def call_fwd(a, w):
        t, k = a.shape
        n = w.shape[1]

        def body(a_ref, w_ref, *o_refs):
            y = jnp.dot(a_ref[...].astype(BF16), w_ref[...], preferred_element_type=F32)
            for o_ref, s0, wd in zip(o_refs, starts, widths):
                o_ref[...] = y[:, s0:s0 + wd]

        return pl.pallas_call(
            body, name=name + "_fwd", grid=(t // ROW_TILE,),
            in_specs=[pl.BlockSpec((ROW_TILE, k), lambda i: (i, 0)), pl.BlockSpec((k, n), lambda i: (0, 0))],
            out_specs=[pl.BlockSpec((ROW_TILE, wd), lambda i: (i, 0)) for wd in widths],
            out_shape=[jax.ShapeDtypeStruct((t, wd), F32) for wd in widths],
            compiler_params=pltpu.CompilerParams(dimension_semantics=("arbitrary",), vmem_limit_bytes=MM_VMEM_LIMIT),
        )(a, w)

    def call_dx(dys, w):
        t = dys[0].shape[0]
        k, n = w.shape

        def body(*refs):
            dy_refs, w_ref, o_ref = refs[:-2], refs[-2], refs[-1]
            acc = jnp.zeros((ROW_TILE, k), F32)
            for dy_ref, s0, wd in zip(dy_refs, starts, widths):
                acc = acc + _nt(dy_ref[...].astype(BF16), w_ref[:, s0:s0 + wd])
            o_ref[...] = acc

        return pl.pallas_call(
            body, name=name + "_dx", grid=(t // ROW_TILE,),
            in_specs=[pl.BlockSpec((ROW_TILE, wd), lambda i: (i, 0)) for wd in widths]
            + [pl.BlockSpec((k, n), lambda i: (0, 0))],
            out_specs=pl.BlockSpec((ROW_TILE, k), lambda i: (i, 0)),
            out_shape=jax.ShapeDtypeStruct((t, k), F32),
            compiler_params=pltpu.CompilerParams(dimension_semantics=("arbitrary",), vmem_limit_bytes=MM_VMEM_LIMIT),
        )(*dys, w)

    def call_dw(a, dys, w):
        t, k = a.shape
        n = w.shape[1]

        def body(a_ref, *refs):
            dy_refs, o_ref = refs[:-1], refs[-1]
            ab = a_ref[...].astype(BF16)
            for dy_ref, s0, wd in zip(dy_refs, starts, widths):
                o_ref[:, s0:s0 + wd] = _tn(ab, dy_ref[...].astype(BF16)).astype(BF16)
            if starts[-1] + widths[-1] < n:
                o_ref[:, starts[-1] + widths[-1]:] = jnp.zeros((tk_w, n - starts[-1] - widths[-1]), BF16)

        return pl.pallas_call(
            body, name=name + "_dw", grid=(k // tk_w,),
            in_specs=[pl.BlockSpec((t, tk_w), lambda i: (0, i))]
            + [pl.BlockSpec((t, wd), lambda i: (0, 0)) for wd in widths],
            out_specs=pl.BlockSpec((tk_w, n), lambda i: (i, 0)),
            out_shape=jax.ShapeDtypeStruct((k, n), BF16),
            compiler_params=pltpu.CompilerParams(dimension_semantics=("arbitrary",), vmem_limit_bytes=MM_VMEM_LIMIT),
        )(a, *dys)

    @jax.custom_vjp
    def op(a, w):
        return tuple(call_fwd(a, w))

    def fwd(a, w):
        return op(a, w), (a, w)

    def bwd(res, dys):
        a, w = res
        return call_dx(dys, w), call_dw(a, dys, w)

    op.defvjp(fwd, bwd)
    return op


def _make_linear_sharded(name, tk_w):
    def call_fwd(a, w):
        t, k = a.shape
        n_sh, _, cc = w.shape

        def body(a_ref, w_ref, o_ref):
            o_ref[...] = jnp.dot(a_ref[...].astype(BF16), w_ref[...], preferred_element_type=F32)

        return pl.pallas_call(
            body, name=name + "_fwd", grid=(n_sh, t // ROW_TILE),
            in_specs=[pl.BlockSpec((ROW_TILE, k), lambda j, i: (i, 0)),
                      pl.BlockSpec((None, k, cc), lambda j, i: (j, 0, 0))],
            out_specs=pl.BlockSpec((ROW_TILE, cc), lambda j, i: (i, j)),
            out_shape=jax.ShapeDtypeStruct((t, n_sh * cc), F32),
            compiler_params=pltpu.CompilerParams(dimension_semantics=("arbitrary", "arbitrary"),
                                                 vmem_limit_bytes=MM_VMEM_LIMIT),
        )(a, w)

    def call_dx(dy, w):
        t = dy.shape[0]
        n_sh, k, cc = w.shape

        def body(dy_ref, w_ref, o_ref):
            acc = jnp.zeros((ROW_TILE, k), F32)
            for j in range(n_sh):
                acc = acc + _nt(dy_ref[:, j * cc:(j + 1) * cc].astype(BF16), w_ref[j])
            o_ref[...] = acc

        return pl.pallas_call(
            body, name=name + "_dx", grid=(t // ROW_TILE,),
            in_specs=[pl.BlockSpec((ROW_TILE, n_sh * cc), lambda i: (i, 0)),
                      pl.BlockSpec((n_sh, k, cc), lambda i: (0, 0, 0))],
            out_specs=pl.BlockSpec((ROW_TILE, k), lambda i: (i, 0)),
            out_shape=jax.ShapeDtypeStruct((t, k), F32),
            compiler_params=pltpu.CompilerParams(dimension_semantics=("arbitrary",),
                                                 vmem_limit_bytes=MM_VMEM_LIMIT),
        )(dy, w)

    def call_dw(a, dy, w):
        t, k = a.shape
        n_sh, _, cc = w.shape

        def body(a_ref, dy_ref, o_ref):
            o_ref[...] = _tn(a_ref[...].astype(BF16), dy_ref[...].astype(BF16)).astype(BF16)

        return pl.pallas_call(
            body, name=name + "_dw", grid=(n_sh, k // tk_w),
            in_specs=[pl.BlockSpec((t, tk_w), lambda j, i: (0, i)),
                      pl.BlockSpec((t, cc), lambda j, i: (0, j))],
            out_specs=pl.BlockSpec((None, tk_w, cc), lambda j, i: (j, i, 0)),
            out_shape=jax.ShapeDtypeStruct(w.shape, BF16),
            compiler_params=pltpu.CompilerParams(dimension_semantics=("arbitrary", "arbitrary"),
                                                 vmem_limit_bytes=MM_VMEM_LIMIT),
        )(a, dy)

    @jax.custom_vjp
    def op(a, w):
        return call_fwd(a, w)

    def fwd(a, w):
        return op(a, w), (a, w)

    def bwd(res, dy):
        a, w = res
        return call_dx(dy, w), call_dw(a, dy, w)

    op.defvjp(fwd, bwd)
    return op


def _row_spec(arr, tb):
    return pl.BlockSpec((tb, arr.shape[1]), lambda i: (i, 0))


def _full_spec(arr):
    return pl.BlockSpec(arr.shape, lambda i: (0, 0))


def _make_rowwise(name, f, n_rows, n_params, out_cols, diff_rows, out_dtypes=None, grad_dtypes=None):
    n_out = len(out_cols)
    out_dtypes = out_dtypes or [F32] * n_out
    grad_dtypes = grad_dtypes or [F32] * sum(diff_rows)

    def call_fwd(rows, params):
        t = rows[0].shape[0]

        def body(*refs):
            ins = [r[...] for r in refs[:n_rows + n_params]]
            outs = f(*ins)
            for o_ref, o in zip(refs[n_rows + n_params:], outs):
                o_ref[...] = o.astype(o_ref.dtype)

        return pl.pallas_call(
            body, name=name + "_fwd", grid=(t // ROW_TILE,),
            in_specs=[_row_spec(a, ROW_TILE) for a in rows] + [_full_spec(p) for p in params],
            out_specs=[pl.BlockSpec((ROW_TILE, n), lambda i: (i, 0)) for n in out_cols],
            out_shape=[jax.ShapeDtypeStruct((t, n), dt) for n, dt in zip(out_cols, out_dtypes)],
            compiler_params=pltpu.CompilerParams(dimension_semantics=("arbitrary",),
                                                 vmem_limit_bytes=MM_VMEM_LIMIT),
        )(*rows, *params)

    def call_bwd(rows, params, cts):
        t = rows[0].shape[0]
        d_rows = [a for a, d in zip(rows, diff_rows) if d]
        n_in = n_rows + n_params + n_out

        def body(*refs):
            ins = [r[...] for r in refs[:n_rows + n_params]]
            ct = tuple(r[...].astype(F32) for r in refs[n_rows + n_params:n_in])
            _, vjp = jax.vjp(f, *ins)
            grads = vjp(ct)
            out_refs = refs[n_in:]
            g_rows = [g for g, d in zip(grads[:n_rows], diff_rows) if d]
            for o_ref, g in zip(out_refs[:len(g_rows)], g_rows):
                o_ref[...] = g.astype(o_ref.dtype)
            p_refs = out_refs[len(g_rows):]

            if p_refs:
                @pl.when(pl.program_id(0) == 0)
                def _():
                    for p_ref in p_refs:
                        p_ref[...] = jnp.zeros_like(p_ref)

                for p_ref, g in zip(p_refs, grads[n_rows:]):
                    p_ref[...] += g

        return pl.pallas_call(
            body, name=name + "_bwd", grid=(t // ROW_TILE,),
            in_specs=[_row_spec(a, ROW_TILE) for a in rows] + [_full_spec(p) for p in params]
            + [_row_spec(c, ROW_TILE) for c in cts],
            out_specs=[_row_spec(a, ROW_TILE) for a in d_rows] + [_full_spec(p) for p in params],
            out_shape=[jax.ShapeDtypeStruct(a.shape, dt) for a, dt in zip(d_rows, grad_dtypes)]
            + [jax.ShapeDtypeStruct(p.shape, F32) for p in params],
            compiler_params=pltpu.CompilerParams(dimension_semantics=("arbitrary",),
                                                 vmem_limit_bytes=MM_VMEM_LIMIT),
        )(*rows, *params, *cts)

    @jax.custom_vjp
    def op(*args):
        return tuple(call_fwd(args[:n_rows], args[n_rows:]))

    def fwd(*args):
        return op(*args), args

    def bwd(args, cts):
        rows, params = args[:n_rows], args[n_rows:]
        outs = call_bwd(rows, params, cts)
        it = iter(outs)
        g_rows = [next(it) if d else jnp.zeros_like(a) for a, d in zip(rows, diff_rows)]
        return tuple(g_rows) + tuple(it)

    op.defvjp(fwd, bwd)
    return op


def _rms(x, g, n):
    return x * lax.rsqrt(jnp.sum(x * x, axis=-1, keepdims=True) * (1.0 / n) + EPS) * g


def _f_pre_attn(x, g, scale, shift):
    return (_rms(x, g, D_MODEL) * (1.0 + scale) + shift,)


def _f_mla_a(cq, ckv, gq, gkv):
    return _rms(cq, gq, MLA_Q_RANK), _rms(ckv, gkv, MLA_KV_RANK)


@jax.custom_vjp
def _split_lanes(x):
    return tuple(x[:, i * LANES:(i + 1) * LANES] for i in range(x.shape[1] // LANES))


def _split_lanes_fwd(x):
    return _split_lanes(x), None


def _split_lanes_bwd(_, cts):
    return (jnp.concatenate(cts, axis=1),)


_split_lanes.defvjp(_split_lanes_fwd, _split_lanes_bwd)


def _f_mla_b(qall, kn_all, kr, kr_sw, cos, sin, gqn, gqr, gqr_sw, gkn, gkr, gkr_sw):
    q = _split_lanes(qall)
    kn = _split_lanes(kn_all)
    qn_o, qr_o, kn_o = [], [], []
    for h in range(MLA_HEADS):
        qn, qr, qs = q[h], q[MLA_HEADS + h], q[2 * MLA_HEADS + h]
        ss = jnp.sum(qn * qn, axis=-1, keepdims=True) + jnp.sum(qr * qr, axis=-1, keepdims=True)
        rs = lax.rsqrt(ss * (1.0 / MLA_QK) + EPS)
        qn_o.append(qn * rs * gqn)
        qr_o.append((qr * rs * gqr) * cos + (qs * rs * gqr_sw) * sin)
        kn_o.append(_rms(kn[h], gkn, MLA_NOPE))
    rs = lax.rsqrt(jnp.sum(kr * kr, axis=-1, keepdims=True) * (1.0 / MLA_ROPE) + EPS)
    kr_o = (kr * rs * gkr) * cos + (kr_sw * rs * gkr_sw) * sin
    return (jnp.concatenate(qn_o, axis=1), jnp.concatenate(qr_o, axis=1), jnp.concatenate(kn_o, axis=1), kr_o)


def _f_post_attn(o_sb, o_mla, g_sb, g_mla):
    return (jnp.concatenate([_rms(o_sb, g_sb, SB_WIDTH), _rms(o_mla, g_mla, SB_WIDTH)], axis=1),)


def _f_pre_ffn(x, attn, gate, g, scale, shift):
    x2 = x + gate * attn
    return x2, _rms(x2, g, D_MODEL) * (1.0 + scale) + shift


def _f_swiglu(gt, up):
    return (gt / (1.0 + jnp.exp(-gt)) * up,)


def _f_loss(x2, ffn, target, gate):
    err = x2 + gate * ffn - target
    return (jnp.sum(err * err, axis=-1, keepdims=True) * (1.0 / D_MODEL),)


def _rope_tables(pos_col, freqs, sign):
    t = pos_col.shape[0]

    def body(p_ref, f_ref, s_ref, cos_ref, sin_ref):
        ang = p_ref[...].astype(F32) * f_ref[...]
        live = jnp.abs(s_ref[...])
        cos_ref[...] = jnp.cos(ang) * live
        sin_ref[...] = jnp.sin(ang) * s_ref[...]

    return pl.pallas_call(
        body, name="rope_tables", grid=(t // ROW_TILE,),
        in_specs=[pl.BlockSpec((ROW_TILE, 1), lambda i: (i, 0)), _full_spec(freqs), _full_spec(sign)],
        out_specs=[pl.BlockSpec((ROW_TILE, LANES), lambda i: (i, 0))] * 2,
        out_shape=[jax.ShapeDtypeStruct((t, LANES), F32)] * 2,
    )(pos_col, freqs, sign)


def _hi_lo_dot(x, tri):
    hi = x.astype(BF16)
    lo = (x - hi.astype(F32)).astype(BF16)
    return (jnp.dot(hi, tri, preferred_element_type=F32) + jnp.dot(lo, tri, preferred_element_type=F32))


def _tri(cmp):
    r = lax.broadcasted_iota(jnp.int32, (ATT_BLK, ATT_BLK), 0)
    c = lax.broadcasted_iota(jnp.int32, (ATT_BLK, ATT_BLK), 1)
    return cmp(r, c).astype(BF16)


def _nt(a, b):
    return lax.dot_general(a, b, (((1,), (1,)), ((), ())), preferred_element_type=F32)


def _tn(a, b):
    return lax.dot_general(a, b, (((0,), (0,)), ((), ())), preferred_element_type=F32)


def _sb_logs(z):
    lb = jnp.minimum(z, 0.0) - jnp.log(1.0 + jnp.exp(-jnp.abs(z)))
    return lb, lb - z


def _sb_fwd(q, k, v):
    t = q.shape[0]
    nq = t // ATT_BLK
    scale = SB_HEAD_DIM ** -0.5

    def body(q_ref, k_ref, v_ref, o_ref, tot_ref):
        qi = pl.program_id(1)
        lane = lax.broadcasted_iota(jnp.int32, (ATT_BLK, LANES), 1)
        tri = _tri(lambda r, c: r > c)
        qv = q_ref[...] * scale
        heads = [(lane // SB_HEAD_DIM) == hh for hh in range(2)]
        qms = [jnp.where(mine, qv, 0.0).astype(BF16) for mine in heads]

        def blocks(kbs, carry, diagonal):
            acc = carry[0]
            nb = len(kbs)
            chains = [(b, hh) for b in range(nb) for hh in range(2)]
            offs = [pl.multiple_of(kb * ATT_BLK, ATT_BLK) for kb in kbs]
            kks = [k_ref[pl.ds(off, ATT_BLK), :].astype(BF16) for off in offs]
            v_blks = [v_ref[pl.ds(off, ATT_BLK), :] for off in offs]
            if any(diagonal):
                valid = (lax.broadcasted_iota(jnp.int32, (ATT_BLK, ATT_BLK), 1)
                         < lax.broadcasted_iota(jnp.int32, (ATT_BLK, ATT_BLK), 0))
            zs = {ch: _nt(qms[ch[1]], kks[ch[0]]) for ch in chains}
            vvs = {(b, hh): jnp.where(heads[hh], v_blks[b], 0.0).astype(BF16) for b, hh in chains}
            logs = {ch: _sb_logs(zs[ch]) for ch in chains}
            l1ms = {ch: jnp.where(valid, logs[ch][1], 0.0) if diagonal[ch[0]] else logs[ch][1] for ch in chains}
            run = {(0, hh): carry[1 + hh] for hh in range(2)}
            for b, hh in chains:
                run[(b + 1, hh)] = run[(b, hh)] + jnp.sum(l1ms[(b, hh)], axis=-1, keepdims=True)
            afters = {ch: _hi_lo_dot(l1ms[ch], tri) for ch in chains}
            ws = {ch: jnp.exp(logs[ch][0] + (afters[ch] + run[ch])) for ch in chains}
            ws = {ch: jnp.where(valid, ws[ch], 0.0) if diagonal[ch[0]] else ws[ch] for ch in chains}
            for ch in chains:
                acc = acc + jnp.dot(ws[ch].astype(BF16), vvs[ch], preferred_element_type=F32)
            return (acc, run[(nb, 0)], run[(nb, 1)])

        zero = jnp.zeros((ATT_BLK, 1), F32)
        init = (jnp.zeros((ATT_BLK, LANES), F32), zero, zero)
        carry = lax.cond(qi % 2 == 1, lambda cr: blocks([qi, qi - 1], cr, (True, False)),
                         lambda cr: blocks([qi], cr, (True,)), init)
        top = qi - 1 - qi % 2
        carry = lax.fori_loop(0, qi // 2, lambda pr, cr: blocks([top - 2 * pr, top - 1 - 2 * pr], cr, (False, False)),
                              carry)
        o_ref[...] = carry[0]
        for hh in range(2):
            tot_ref[:, hh * LANES:(hh + 1) * LANES] = jnp.broadcast_to(carry[1 + hh], (ATT_BLK, LANES))

    return pl.pallas_call(
        body, name="sb_attn_fwd", grid=(SB_HEADS // 2, nq),
        in_specs=[pl.BlockSpec((ATT_BLK, LANES), lambda p, i: (i, p)),
                  pl.BlockSpec((t, LANES), lambda p, i: (0, p)),
                  pl.BlockSpec((t, LANES), lambda p, i: (0, p))],
        out_specs=[pl.BlockSpec((ATT_BLK, LANES), lambda p, i: (i, p)),
                   pl.BlockSpec((ATT_BLK, 2 * LANES), lambda p, i: (i, p))],
        out_shape=[jax.ShapeDtypeStruct((t, SB_WIDTH), F32), jax.ShapeDtypeStruct((t, SB_HEADS * LANES), F32)],
        compiler_params=pltpu.CompilerParams(dimension_semantics=("arbitrary", "arbitrary")),
    )(q, k, v)


def _sb_bwd(q, k, v, tot, do):
    t = q.shape[0]
    nq = t // ATT_BLK
    scale = SB_HEAD_DIM ** -0.5

    def body(q_ref, k_ref, v_ref, tot_ref, do_ref, dq_ref, dk_ref, dv_ref):
        qi = pl.program_id(1)

        @pl.when(qi == 0)
        def _():
            dk_ref[...] = jnp.zeros_like(dk_ref)
            dv_ref[...] = jnp.zeros_like(dv_ref)

        lane = lax.broadcasted_iota(jnp.int32, (ATT_BLK, LANES), 1)
        tri_incl = _tri(lambda r, c: r <= c)
        tri_lt = _tri(lambda r, c: r < c)
        qv = q_ref[...] * scale
        dov = do_ref[...]
        heads = [(lane // SB_HEAD_DIM) == hh for hh in range(2)]
        qms = [jnp.where(mine, qv, 0.0).astype(BF16) for mine in heads]
        doms = [jnp.where(mine, dov, 0.0).astype(BF16) for mine in heads]
        tots = [tot_ref[:, hh * LANES:hh * LANES + 1] for hh in range(2)]

        def blocks(kbs, carry, diagonal):
            dq = carry[0]
            nb = len(kbs)
            chains = [(b, hh) for b in range(nb) for hh in range(2)]
            offs = [pl.multiple_of(kb * ATT_BLK, ATT_BLK) for kb in kbs]
            k_blks = [k_ref[pl.ds(off, ATT_BLK), :] for off in offs]
            vvs = [v_ref[pl.ds(off, ATT_BLK), :].astype(BF16) for off in offs]
            if any(diagonal):
                valid = (lax.broadcasted_iota(jnp.int32, (ATT_BLK, ATT_BLK), 1)
                         < lax.broadcasted_iota(jnp.int32, (ATT_BLK, ATT_BLK), 0))
            kks = {(b, hh): jnp.where(heads[hh], k_blks[b], 0.0).astype(BF16) for b, hh in chains}
            zs = {ch: _nt(qms[ch[1]], kks[ch]) for ch in chains}
            dws = {ch: _nt(doms[ch[1]], vvs[ch[0]]) for ch in chains}
            logs = {ch: _sb_logs(zs[ch]) for ch in chains}
            lbs = {ch: logs[ch][0] for ch in chains}
            l1m_all = {ch: logs[ch][1] for ch in chains}
            l1ms = {ch: jnp.where(valid, l1m_all[ch], 0.0) if diagonal[ch[0]] else l1m_all[ch] for ch in chains}
            pre, c_de = {}, {}
            for hh in range(2):
                pre[(0, hh)], c_de[(0, hh)] = carry[1 + 2 * hh], carry[2 + 2 * hh]
            for b, hh in chains:
                pre[(b + 1, hh)] = pre[(b, hh)] + jnp.sum(l1ms[(b, hh)], axis=-1, keepdims=True)
            prefix = {ch: _hi_lo_dot(l1ms[ch], tri_incl) for ch in chains}
            ws = {ch: jnp.exp(lbs[ch] + (tots[ch[1]] - (prefix[ch] + pre[ch]))) for ch in chains}
            ws = {ch: jnp.where(valid, ws[ch], 0.0) if diagonal[ch[0]] else ws[ch] for ch in chains}
            d_es = {ch: ws[ch] * dws[ch] for ch in chains}
            for b, hh in chains:
                c_de[(b + 1, hh)] = c_de[(b, hh)] + jnp.sum(d_es[(b, hh)], axis=-1, keepdims=True)
            dvs = [_tn(ws[(b, 0)].astype(BF16), doms[0]) + _tn(ws[(b, 1)].astype(BF16), doms[1]) for b in range(nb)]
            dl1ms = {ch: _hi_lo_dot(d_es[ch], tri_lt) + c_de[ch] for ch in chains}
            dzs = {ch: d_es[ch] * jnp.exp(l1m_all[ch]) - dl1ms[ch] * jnp.exp(lbs[ch]) for ch in chains}
            dzs = {ch: jnp.where(valid, dzs[ch], 0.0) if diagonal[ch[0]] else dzs[ch] for ch in chains}
            dzs = {ch: dzs[ch].astype(BF16) for ch in chains}
            for ch in chains:
                dq = dq + jnp.dot(dzs[ch], kks[ch], preferred_element_type=F32)
            for b in range(nb):
                dk_ref[pl.ds(offs[b], ATT_BLK), :] += _tn(dzs[(b, 0)], qms[0]) + _tn(dzs[(b, 1)], qms[1])
                dv_ref[pl.ds(offs[b], ATT_BLK), :] += dvs[b]
            return (dq, pre[(nb, 0)], c_de[(nb, 0)], pre[(nb, 1)], c_de[(nb, 1)])

        zero = jnp.zeros((ATT_BLK, 1), F32)
        carry = lax.fori_loop(0, qi // 2, lambda pr, cr: blocks([2 * pr, 2 * pr + 1], cr, (False, False)),
                              (jnp.zeros((ATT_BLK, LANES), F32), zero, zero, zero, zero))
        carry = lax.cond(qi % 2 == 1, lambda cr: blocks([qi - 1, qi], cr, (False, True)),
                         lambda cr: blocks([qi], cr, (True,)), carry)
        dq_ref[...] = carry[0] * scale

    return pl.pallas_call(
        body, name="sb_attn_bwd", grid=(SB_HEADS // 2, nq),
        in_specs=[pl.BlockSpec((ATT_BLK, LANES), lambda p, i: (i, p)),
                  pl.BlockSpec((t, LANES), lambda p, i: (0, p)),
                  pl.BlockSpec((t, LANES), lambda p, i: (0, p)),
                  pl.BlockSpec((ATT_BLK, 2 * LANES), lambda p, i: (i, p)),
                  pl.BlockSpec((ATT_BLK, LANES), lambda p, i: (i, p))],
        out_specs=[pl.BlockSpec((ATT_BLK, LANES), lambda p, i: (i, p)),
                   pl.BlockSpec((t, LANES), lambda p, i: (0, p)),
                   pl.BlockSpec((t, LANES), lambda p, i: (0, p))],
        out_shape=[jax.ShapeDtypeStruct((t, SB_WIDTH), F32)] * 3,
        compiler_params=pltpu.CompilerParams(dimension_semantics=("arbitrary", "arbitrary")),
    )(q, k, v, tot, do)


@jax.custom_vjp
def _sb_attention(q, k, v):
    return _sb_fwd(q, k, v)[0]


def _sb_attention_fwd(q, k, v):
    o, tot = _sb_fwd(q, k, v)
    return o, (q, k, v, tot)


def _sb_attention_bwd(res, do):
    return tuple(_sb_bwd(*res, do))


_sb_attention.defvjp(_sb_attention_fwd, _sb_attention_bwd)


def _mla_fwd(qn, qr, kn, kr, v):
    t = qn.shape[0]
    nq = t // ATT_BLK
    scale = MLA_QK ** -0.5

    def body(qn_ref, qr_ref, kn_ref, kr_ref, v_ref, o_ref, lse_ref):
        qi = pl.program_id(1)
        lanes = [slice(hh * LANES, (hh + 1) * LANES) for hh in range(2)]
        qnb = [qn_ref[:, sl].astype(BF16) for sl in lanes]
        qrb = [qr_ref[:, sl].astype(BF16) for sl in lanes]

        def blocks(kbs, carry, diagonal):
            nb = len(kbs)
            chains = [(b, hh) for b in range(nb) for hh in range(2)]
            offs = [pl.multiple_of(kb * ATT_BLK, ATT_BLK) for kb in kbs]
            krbs = [kr_ref[pl.ds(off, ATT_BLK), :].astype(BF16) for off in offs]
            accs, ms, ls = [carry[0], carry[3]], [carry[1], carry[4]], [carry[2], carry[5]]
            ss = {(b, hh): (_nt(qnb[hh], kn_ref[pl.ds(offs[b], ATT_BLK), lanes[hh]].astype(BF16))
                            + _nt(qrb[hh], krbs[b])) * scale for b, hh in chains}
            if any(diagonal):
                causal = (lax.broadcasted_iota(jnp.int32, (ATT_BLK, ATT_BLK), 1)
                          <= lax.broadcasted_iota(jnp.int32, (ATT_BLK, ATT_BLK), 0))
                ss = {ch: jnp.where(causal, ss[ch], -jnp.inf) if diagonal[ch[0]] else ss[ch] for ch in chains}
            m_new = list(ms)
            for b, hh in chains:
                m_new[hh] = jnp.maximum(m_new[hh], jnp.max(ss[(b, hh)], axis=-1, keepdims=True))
            ps = {(b, hh): jnp.exp(ss[(b, hh)] - m_new[hh]) for b, hh in chains}
            alphas = [jnp.exp(ms[hh] - m_new[hh]) for hh in range(2)]
            pvs = {(b, hh): jnp.dot(ps[(b, hh)].astype(BF16), v_ref[pl.ds(offs[b], ATT_BLK), lanes[hh]].astype(BF16),
                                    preferred_element_type=F32) for b, hh in chains}
            out = []
            for hh in range(2):
                acc, l = accs[hh] * alphas[hh], ls[hh] * alphas[hh]
                for b in range(nb):
                    acc, l = acc + pvs[(b, hh)], l + jnp.sum(ps[(b, hh)], axis=-1, keepdims=True)
                out += [acc, m_new[hh], l]
            return tuple(out)

        init = (jnp.zeros((ATT_BLK, LANES), F32), jnp.full((ATT_BLK, 1), -jnp.inf, F32), jnp.zeros((ATT_BLK, 1), F32))
        carry = lax.cond(qi % 2 == 1, lambda cr: blocks([qi, qi - 1], cr, (True, False)),
                         lambda cr: blocks([qi], cr, (True,)), init + init)
        carry = lax.fori_loop(0, qi // 2, lambda pr, cr: blocks([2 * pr, 2 * pr + 1], cr, (False, False)), carry)
        for hh in range(2):
            acc, m, l = carry[3 * hh:3 * hh + 3]
            o_ref[:, lanes[hh]] = acc / l
            lse_ref[:, lanes[hh]] = jnp.broadcast_to(m + jnp.log(l), (ATT_BLK, LANES))

    blk = pl.BlockSpec((ATT_BLK, 2 * LANES), lambda p, i: (i, p))
    full = pl.BlockSpec((t, 2 * LANES), lambda p, i: (0, p))
    return pl.pallas_call(
        body, name="mla_attn_fwd", grid=(MLA_HEADS // 2, nq),
        in_specs=[blk, blk, full, pl.BlockSpec((t, LANES), lambda p, i: (0, 0)), full],
        out_specs=[blk, blk],
        out_shape=[jax.ShapeDtypeStruct((t, MLA_HEADS * LANES), F32)] * 2,
        compiler_params=pltpu.CompilerParams(dimension_semantics=("arbitrary", "arbitrary")),
    )(qn, qr, kn, kr, v)


def _mla_bwd(qn, qr, kn, kr, v, o, lse, do):
    t = qn.shape[0]
    nq = t // ATT_BLK
    scale = MLA_QK ** -0.5

    def body(qn_ref, qr_ref, kn_ref, kr_ref, v_ref, o_ref, lse_ref, do_ref,
             dqn_ref, dqr_ref, dkn_ref, dkr_ref, dv_ref):
        pair = pl.program_id(0)
        qi = pl.program_id(1)

        @pl.when(qi == 0)
        def _():
            dkn_ref[...] = jnp.zeros_like(dkn_ref)
            dv_ref[...] = jnp.zeros_like(dv_ref)

        @pl.when((qi == 0) & (pair == 0))
        def _():
            dkr_ref[...] = jnp.zeros_like(dkr_ref)

        lanes = [slice(hh * LANES, (hh + 1) * LANES) for hh in range(2)]
        qnb = [qn_ref[:, sl].astype(BF16) for sl in lanes]
        qrb = [qr_ref[:, sl].astype(BF16) for sl in lanes]
        dob = [do_ref[:, sl].astype(BF16) for sl in lanes]
        delta = [jnp.sum(do_ref[:, sl] * o_ref[:, sl], axis=-1, keepdims=True) for sl in lanes]
        lse_v = [lse_ref[:, hh * LANES:hh * LANES + 1] for hh in range(2)]

        def blocks(kbs, carry, diagonal):
            nb = len(kbs)
            chains = [(b, hh) for b in range(nb) for hh in range(2)]
            offs = [pl.multiple_of(kb * ATT_BLK, ATT_BLK) for kb in kbs]
            krbs = [kr_ref[pl.ds(off, ATT_BLK), :].astype(BF16) for off in offs]
            knb = {(b, hh): kn_ref[pl.ds(offs[b], ATT_BLK), lanes[hh]].astype(BF16) for b, hh in chains}
            vb = {(b, hh): v_ref[pl.ds(offs[b], ATT_BLK), lanes[hh]].astype(BF16) for b, hh in chains}
            ss = {(b, hh): _nt(qnb[hh], knb[(b, hh)]) + _nt(qrb[hh], krbs[b]) for b, hh in chains}
            dps = {(b, hh): _nt(dob[hh], vb[(b, hh)]) for b, hh in chains}
            ps = {(b, hh): jnp.exp(ss[(b, hh)] * scale - lse_v[hh]) for b, hh in chains}
            if any(diagonal):
                causal = (lax.broadcasted_iota(jnp.int32, (ATT_BLK, ATT_BLK), 1)
                          <= lax.broadcasted_iota(jnp.int32, (ATT_BLK, ATT_BLK), 0))
                ps = {ch: jnp.where(causal, ps[ch], 0.0) if diagonal[ch[0]] else ps[ch] for ch in chains}
            dss = {(b, hh): (ps[(b, hh)] * (dps[(b, hh)] - delta[hh]) * scale).astype(BF16) for b, hh in chains}
            for b, hh in chains:
                dv_ref[pl.ds(offs[b], ATT_BLK), lanes[hh]] += _tn(ps[(b, hh)].astype(BF16), dob[hh])
            for b, hh in chains:
                dkn_ref[pl.ds(offs[b], ATT_BLK), lanes[hh]] += _tn(dss[(b, hh)], qnb[hh])
            for b in range(nb):
                dkr_ref[pl.ds(offs[b], ATT_BLK), :] += _tn(dss[(b, 0)], qrb[0]) + _tn(dss[(b, 1)], qrb[1])
            out = list(carry)
            for b, hh in chains:
                out[2 * hh] = out[2 * hh] + jnp.dot(dss[(b, hh)], knb[(b, hh)], preferred_element_type=F32)
                out[2 * hh + 1] = out[2 * hh + 1] + jnp.dot(dss[(b, hh)], krbs[b], preferred_element_type=F32)
            return tuple(out)

        zero = jnp.zeros((ATT_BLK, LANES), F32)
        carry = lax.fori_loop(0, qi // 2, lambda pr, cr: blocks([2 * pr, 2 * pr + 1], cr, (False, False)),
                              (zero, zero, zero, zero))
        carry = lax.cond(qi % 2 == 1, lambda cr: blocks([qi - 1, qi], cr, (False, True)),
                         lambda cr: blocks([qi], cr, (True,)), carry)
        for hh in range(2):
            dqn_ref[:, lanes[hh]] = carry[2 * hh]
            dqr_ref[:, lanes[hh]] = carry[2 * hh + 1]

    blk = pl.BlockSpec((ATT_BLK, 2 * LANES), lambda p, i: (i, p))
    full = pl.BlockSpec((t, 2 * LANES), lambda p, i: (0, p))
    shared = pl.BlockSpec((t, LANES), lambda p, i: (0, 0))
    wide = jax.ShapeDtypeStruct((t, MLA_HEADS * LANES), F32)
    return pl.pallas_call(
        body, name="mla_attn_bwd", grid=(MLA_HEADS // 2, nq),
        in_specs=[blk, blk, full, shared, full, blk, blk, blk],
        out_specs=[blk, blk, full, shared, full],
        out_shape=[wide, wide, wide, jax.ShapeDtypeStruct((t, LANES), F32), wide],
        compiler_params=pltpu.CompilerParams(dimension_semantics=("arbitrary", "arbitrary")),
    )(qn, qr, kn, kr, v, o, lse, do)


@jax.custom_vjp
def _mla_attention(qn, qr, kn, kr, v):
    return _mla_fwd(qn, qr, kn, kr, v)[0]


def _mla_attention_fwd(qn, qr, kn, kr, v):
    o, lse = _mla_fwd(qn, qr, kn, kr, v)
    return o, (qn, qr, kn, kr, v, o, lse)


def _mla_attention_bwd(res, do):
    return tuple(_mla_bwd(*res, do))


_mla_attention.defvjp(_mla_attention_fwd, _mla_attention_bwd)


def _ffn_in(h, wg, wu):
    t, k = h.shape
    n_sh, _, cc = wg.shape

    def body(h_ref, wg_ref, wu_ref, g_ref, u_ref, a_ref):
        hb = h_ref[...].astype(BF16)
        for j in range(n_sh):
            cols = slice(j * cc, (j + 1) * cc)
            g = jnp.dot(hb, wg_ref[j], preferred_element_type=F32)
            u = jnp.dot(hb, wu_ref[j], preferred_element_type=F32)
            g_ref[:, cols] = g
            u_ref[:, cols] = u
            a_ref[:, cols] = _f_swiglu(g, u)[0].astype(BF16)

    w_spec = pl.BlockSpec((n_sh, k, cc), lambda i: (0, 0, 0))
    o_spec = pl.BlockSpec((ROW_TILE, n_sh * cc), lambda i: (i, 0))
    wide = (t, n_sh * cc)
    return pl.pallas_call(
        body, name="ffn_in_fwd", grid=(t // ROW_TILE,),
        in_specs=[pl.BlockSpec((ROW_TILE, k), lambda i: (i, 0)), w_spec, w_spec],
        out_specs=[o_spec, o_spec, o_spec],
        out_shape=[jax.ShapeDtypeStruct(wide, F32), jax.ShapeDtypeStruct(wide, F32), jax.ShapeDtypeStruct(wide, BF16)],
        compiler_params=pltpu.CompilerParams(dimension_semantics=("arbitrary",), vmem_limit_bytes=MM_VMEM_LIMIT),
    )(h, wg, wu)


def _ffn_mid_bwd(dy, wd, g, u):
    t, n = dy.shape
    n_sh, cc, _ = wd.shape

    def body(dy_ref, wd_ref, g_ref, u_ref, dg_ref, du_ref):
        d_act = _nt(dy_ref[...].astype(BF16), wd_ref[...])
        _, vjp = jax.vjp(_f_swiglu, g_ref[...], u_ref[...])
        dg, du = vjp((d_act,))
        dg_ref[...] = dg.astype(BF16)
        du_ref[...] = du.astype(BF16)

    blk = pl.BlockSpec((ROW_TILE, cc), lambda j, i: (i, j))
    wide = jax.ShapeDtypeStruct((t, n_sh * cc), BF16)
    return pl.pallas_call(
        body, name="ffn_mid_bwd", grid=(n_sh, t // ROW_TILE),
        in_specs=[pl.BlockSpec((ROW_TILE, n), lambda j, i: (i, 0)),
                  pl.BlockSpec((None, cc, n), lambda j, i: (j, 0, 0)), blk, blk],
        out_specs=[blk, blk], out_shape=[wide, wide],
        compiler_params=pltpu.CompilerParams(dimension_semantics=("arbitrary", "arbitrary"),
                                             vmem_limit_bytes=MM_VMEM_LIMIT),
    )(dy, wd, g, u)


def _ffn_dh(dg, du, wg, wu):
    t = dg.shape[0]
    n_sh, k, cc = wg.shape

    def body(dg_ref, du_ref, wg_ref, wu_ref, o_ref):
        acc = jnp.zeros((ROW_TILE, k), F32)
        for j in range(n_sh):
            cols = slice(j * cc, (j + 1) * cc)
            acc = acc + _nt(dg_ref[:, cols], wg_ref[j]) + _nt(du_ref[:, cols], wu_ref[j])
        o_ref[...] = acc

    blk = pl.BlockSpec((ROW_TILE, n_sh * cc), lambda i: (i, 0))
    w_spec = pl.BlockSpec((n_sh, k, cc), lambda i: (0, 0, 0))
    return pl.pallas_call(
        body, name="ffn_dh", grid=(t // ROW_TILE,),
        in_specs=[blk, blk, w_spec, w_spec],
        out_specs=pl.BlockSpec((ROW_TILE, k), lambda i: (i, 0)),
        out_shape=jax.ShapeDtypeStruct((t, k), F32),
        compiler_params=pltpu.CompilerParams(dimension_semantics=("arbitrary",), vmem_limit_bytes=MM_VMEM_LIMIT),
    )(dg, du, wg, wu)


def _ffn_dw_in(h, dy, n_sh, name):
    t, k = h.shape
    cc = dy.shape[1] // n_sh
    tk = 512

    def body(h_ref, dy_ref, o_ref):
        o_ref[...] = _tn(h_ref[...].astype(BF16), dy_ref[...]).astype(BF16)

    return pl.pallas_call(
        body, name=name, grid=(n_sh, k // tk),
        in_specs=[pl.BlockSpec((t, tk), lambda j, i: (0, i)), pl.BlockSpec((t, cc), lambda j, i: (0, j))],
        out_specs=pl.BlockSpec((None, tk, cc), lambda j, i: (j, i, 0)),
        out_shape=jax.ShapeDtypeStruct((n_sh, k, cc), BF16),
        compiler_params=pltpu.CompilerParams(dimension_semantics=("arbitrary", "arbitrary"),
                                             vmem_limit_bytes=MM_VMEM_LIMIT),
    )(h, dy)


@jax.custom_vjp
def _ffn_block(h, wg, wu, wd):
    act = _ffn_in(h, wg, wu)[2]
    return _mm(act, wd.reshape(-1, wd.shape[2]), "nn", "ffn_down_fwd", ROW_TILE, wd.shape[2])


def _ffn_block_fwd(h, wg, wu, wd):
    g, u, act = _ffn_in(h, wg, wu)
    y = _mm(act, wd.reshape(-1, wd.shape[2]), "nn", "ffn_down_fwd", ROW_TILE, wd.shape[2])
    return y, (h, wg, wu, wd, g, u, act)


def _ffn_block_bwd(res, dy):
    h, wg, wu, wd, g, u, act = res
    dg, du = _ffn_mid_bwd(dy, wd, g, u)
    dh = _ffn_dh(dg, du, wg, wu)
    n_sh = wg.shape[0]
    dwg = _ffn_dw_in(h, dg, n_sh, "ffn_gate_dw")
    dwu = _ffn_dw_in(h, du, n_sh, "ffn_up_dw")
    dwd = _mm(act, dy, "tn", "ffn_down_dw", 256, wd.shape[2], out_dtype=BF16).reshape(wd.shape)
    return dh, dwg, dwu, dwd


_ffn_block.defvjp(_ffn_block_fwd, _ffn_block_bwd)


def _split_cols(x, cuts, ct_dtype):
    cuts = tuple(cuts)

    @jax.custom_vjp
    def op(x):
        return tuple(x[:, a:b] for a, b in zip((0,) + cuts, cuts + (x.shape[1],)))

    def fwd(x):
        return op(x), None

    def bwd(_, cts):
        return (jnp.concatenate([c.astype(ct_dtype) for c in cts], axis=1),)

    op.defvjp(fwd, bwd)
    return op(x)


def _swap_halves(w):
    half = w.shape[-1] // 2
    return jnp.concatenate([w[..., half:], w[..., :half]], axis=-1)


def _pad_lanes(w):
    return jnp.concatenate([w, jnp.zeros(w.shape[:-1] + (LANES - w.shape[-1],), w.dtype)], axis=-1)


def _join_cols(shards):
    return shards.transpose(1, 0, 2).reshape(shards.shape[1], -1)


def _mod_parts(mod):
    return [mod[:, i * D_MODEL:(i + 1) * D_MODEL] for i in range(N_MOD)]


def _local_loss(x, mod, p, cos, sin, target):
    return _ffn_stage(x, _mixing_stage(x, mod, p, cos, sin), mod, p, target)


def _mixing_stage(x, mod, p, cos, sin):
    shift1, scale1 = _mod_parts(mod)[:2]

    w_in = _join_cols(p["w_in"])
    k_rope_w = w_in[:, 2176:2240]
    w_in_ext = jnp.concatenate([w_in[:, :2176], _pad_lanes(k_rope_w), _pad_lanes(_swap_halves(k_rope_w)),
                                jnp.zeros((D_MODEL, LANES), w_in.dtype)], axis=1)
    (h1,) = _make_rowwise("pre_attn", _f_pre_attn, 1, 3, [D_MODEL], [True], out_dtypes=[BF16])(
        x, p["norm_attn"], scale1, shift1)
    q_sb, k_sb, v_sb, cq, ckv, kr, kr_sw = _make_linear_split(
        "in_proj", (SB_WIDTH, SB_WIDTH, SB_WIDTH, MLA_Q_RANK, MLA_KV_RANK, LANES, LANES), 512)(h1, w_in_ext)

    o_sb = _sb_attention(q_sb, k_sb, v_sb)

    wq = _join_cols(p["w_q_up"]).reshape(MLA_Q_RANK, MLA_HEADS, MLA_QK)
    wq_n, wq_r = wq[:, :, :MLA_NOPE], wq[:, :, MLA_NOPE:]
    w_q_ext = jnp.concatenate([wq_n.reshape(MLA_Q_RANK, -1), _pad_lanes(wq_r).reshape(MLA_Q_RANK, -1),
                               _pad_lanes(_swap_halves(wq_r)).reshape(MLA_Q_RANK, -1)], axis=1)
    wkv = _join_cols(p["w_kv_up"]).reshape(MLA_KV_RANK, MLA_HEADS, MLA_NOPE + MLA_V)
    w_kv_ext = jnp.concatenate([wkv[:, :, :MLA_NOPE].reshape(MLA_KV_RANK, -1),
                                wkv[:, :, MLA_NOPE:].reshape(MLA_KV_RANK, -1)], axis=1)
    cqn, ckvn = _make_rowwise("mla_a", _f_mla_a, 2, 2, [MLA_Q_RANK, MLA_KV_RANK], [True, True],
                              out_dtypes=[BF16, BF16], grad_dtypes=[BF16, BF16])(
        cq, ckv, p["q_a_norm"], p["kv_a_norm"])
    qall = _make_linear("q_up", 384, 768)(cqn, w_q_ext)
    kn_all, v_mla = _make_linear_split("kv_up", (MLA_HEADS * MLA_NOPE, MLA_HEADS * MLA_V), MLA_KV_RANK)(ckvn, w_kv_ext)
    gq = p["q_norm"]
    gkr = p["k_rope_norm"]
    qn, qr, kn, krr = _make_rowwise("mla_b", _f_mla_b, 6, 6, [512, 512, 512, LANES],
                                    [True, True, True, True, False, False],
                                    out_dtypes=[BF16] * 4, grad_dtypes=[BF16] * 4)(
        qall, kn_all, kr, kr_sw, cos, sin,
        gq[:, :MLA_NOPE], _pad_lanes(gq[:, MLA_NOPE:]), _pad_lanes(_swap_halves(gq[:, MLA_NOPE:])),
        p["k_nope_norm"], _pad_lanes(gkr), _pad_lanes(_swap_halves(gkr)))
    o_mla = _mla_attention(qn, qr, kn, krr, v_mla)

    (mixed,) = _make_rowwise("post_attn", _f_post_attn, 2, 2, [D_MODEL], [True, True])(
        o_sb, o_mla, p["out_norm_sb"], p["out_norm_mla"])
    return mixed


def _ffn_stage(x, mixed, mod, p, target):
    _, _, gate1, shift2, scale2, gate2 = _mod_parts(mod)
    attn = _make_linear("out_proj", 512, 512)(mixed, p["w_out"].reshape(D_MODEL, D_MODEL))

    x2, h2 = _make_rowwise("pre_ffn", _f_pre_ffn, 2, 4, [D_MODEL, D_MODEL], [True, True],
                           out_dtypes=[F32, BF16], grad_dtypes=[F32, BF16])(
        x, attn, gate1, p["norm_ffn"], scale2, shift2)
    ffn = _ffn_block(h2, p["w_gate"], p["w_up"], p["w_down"])
    (row_loss,) = _make_rowwise("loss", _f_loss, 3, 1, [1], [True, True, False], grad_dtypes=[F32, BF16])(
        x2, ffn, target, gate2)
    return 0.5 * jnp.sum(row_loss)


def _my_place():
    return lax.axis_index("x"), lax.axis_index("y"), lax.axis_index("c")


def _all_gather_small(block, name):
    m_per, n = block.shape

    def body(x_ref, out_ref, send_sems, recv_sems, local_sem):
        x, y, c = _my_place()
        me, sibling = (x, y, c), (x, y, 1 - c)
        chips = [(1 - x, y), (x, 1 - y), (1 - x, 1 - y)]

        def rows(px, py, pc):
            return out_ref.at[pl.ds((4 * px + 2 * py + pc) * m_per, m_per), :]

        def copy(k, blk, to, src=None):
            return pltpu.make_async_remote_copy(
                src_ref=rows(*blk) if src is None else src, dst_ref=rows(*blk),
                send_sem=send_sems.at[k], recv_sem=recv_sems.at[k], device_id=to, device_id_type=MESH)

        mine = pltpu.make_async_copy(x_ref, rows(*me), local_sem)
        mine.start()
        first = [copy(0, me, sibling, src=x_ref)]
        first += [copy(1 + j, me, (*chip, c), src=x_ref) for j, chip in enumerate(chips)]
        for cp in first:
            cp.start()
        passed = [copy(4 + j, (*chip, c), sibling) for j, chip in enumerate(chips)]
        for j, chip in enumerate(chips):
            copy(1 + j, (*chip, c), me).wait_recv()
            passed[j].start()
        copy(0, sibling, me).wait_recv()
        for j, chip in enumerate(chips):
            copy(4 + j, (*chip, 1 - c), me).wait_recv()
        for cp in first + passed:
            cp.wait_send()
        mine.wait()

    return pl.pallas_call(
        body, name=name,
        out_shape=jax.ShapeDtypeStruct((N_DEV * m_per, n), block.dtype),
        in_specs=[pl.BlockSpec(memory_space=pltpu.VMEM)],
        out_specs=pl.BlockSpec(memory_space=pltpu.VMEM),
        scratch_shapes=[pltpu.SemaphoreType.DMA((7,)), pltpu.SemaphoreType.DMA((7,)), pltpu.SemaphoreType.DMA],
    )(block)


EARLY = ("w_in", "w_q_up", "w_kv_up")
LATE = ("w_out", "w_gate", "w_up", "w_down")
BIG = EARLY + LATE
TRANSPOSED_UPDATE = ("w_in", "w_gate", "w_up")
HALF_AXIS = {"w_in": 0, "w_q_up": 0, "w_kv_up": 0, "w_out": 0, "w_gate": 0, "w_up": 0, "w_down": 1}


def _half(ref, h, axis, lead=()):
    trail = ref.shape[len(lead):]
    idx = list(lead) + [slice(None)] * len(trail)
    at = len(trail) - 2 + axis
    n2 = trail[at] // 2
    idx[len(lead) + at] = pl.ds(h * n2, n2)
    return ref.at[tuple(idx)]


def _half_shape(shape, axis):
    shape = list(shape)
    shape[len(shape) - 2 + axis] //= 2
    return tuple(shape)


def _remote(src, dst, send_sems, recv_sems, k, to):
    return pltpu.make_async_remote_copy(src_ref=src, dst_ref=dst, send_sem=send_sems.at[k],
                                        recv_sem=recv_sems.at[k], device_id=to, device_id_type=MESH)


def _gather_weights(names, shards, after):
    n_w = len(shards)
    axes = [HALF_AXIS[n] for n in names]

    def body(*refs):
        w_refs, out_refs, token = refs[:n_w], refs[n_w + 1:2 * n_w + 1], refs[2 * n_w + 1]
        send_sems, recv_sems, local_sems = refs[2 * n_w + 2:]
        token[...] = jnp.zeros_like(token)
        x, y, c = _my_place()
        sibling = (x, y, 1 - c)
        chips = [(1 - x, y), (x, 1 - y), (1 - x, 1 - y)]
        me = 2 * x + y
        mine =[pltpu.make_async_copy(w, o.at[me], local_sems.at[i]) for i, (w, o) in enumerate(zip(w_refs, out_refs))]
        for cp in mine:
            cp.start()
        first = [_remote(_half(w_refs[i], c, axes[i]), _half(out_refs[i], c, axes[i], (me,)),
                         send_sems, recv_sems, 6 * i + j, (*chip, c))
                 for i in range(n_w) for j, chip in enumerate(chips)]
        for cp in first:
            cp.start()
        passed = []
        for j, (cx, cy) in enumerate(chips):
            for i in range(n_w):
                blk = _half(out_refs[i], c, axes[i], (2 * cx + cy,))
                _remote(blk, blk, send_sems, recv_sems, 6 * i + j, (cx, cy, c)).wait_recv()
                cp = _remote(blk, blk, send_sems, recv_sems, 6 * i + 3 + j, sibling)
                cp.start()
                passed.append(cp)
        for j, (cx, cy) in enumerate(chips):
            for i in range(n_w):
                blk = _half(out_refs[i], 1 - c, axes[i], (2 * cx + cy,))
                _remote(blk, blk, send_sems, recv_sems, 6 * i + 3 + j, sibling).wait_recv()
        for cp in first + passed:
            cp.wait_send()
        for cp in mine:
            cp.wait()

    outs = pl.pallas_call(
        body, name="gather_weights",
        out_shape=[jax.ShapeDtypeStruct((N_CHIPS,) + s.shape, s.dtype) for s in shards]
        + [jax.ShapeDtypeStruct((8, LANES), F32)],
        in_specs=[ANY] * (n_w + 1), out_specs=[ANY] * n_w + [pl.BlockSpec(memory_space=pltpu.VMEM)],
        scratch_shapes=[pltpu.SemaphoreType.DMA((6 * n_w,)), pltpu.SemaphoreType.DMA((6 * n_w,)),
                        pltpu.SemaphoreType.DMA((n_w,))],
    )(*shards, after)
    return outs[:n_w], outs[n_w]


def _pair_exchange(names, grads, call_name):
    n_w = len(grads)
    axes = [HALF_AXIS[n] for n in names]

    def body(*refs):
        g_refs, t_refs = refs[:n_w], refs[n_w:2 * n_w]
        send_sems, recv_sems = refs[2 * n_w:]
        x, y, c = _my_place()
        sends = [_remote(_half(g_refs[i], 1 - c, axes[i]), t_refs[i], send_sems, recv_sems, i, (x, y, 1 - c))
                 for i in range(n_w)]
        for cp in sends:
            cp.start()
        for cp in sends:
            cp.wait_recv()
        for cp in sends:
            cp.wait_send()

    return pl.pallas_call(
        body, name=call_name,
        out_shape=[jax.ShapeDtypeStruct(_half_shape(g.shape, a), g.dtype) for g, a in zip(grads, axes)],
        in_specs=[ANY] * n_w, out_specs=[ANY] * n_w,
        scratch_shapes=[pltpu.SemaphoreType.DMA((n_w,)), pltpu.SemaphoreType.DMA((n_w,))],
    )(*grads)


def _chip_scatter(pair_sums):
    n_w = len(pair_sums)

    def body(*refs):
        s_refs, p_refs = refs[:n_w], refs[n_w:2 * n_w]
        send_sems, recv_sems = refs[2 * n_w:]
        x, y, c = _my_place()
        chips = [(1 - x, y), (x, 1 - y), (1 - x, 1 - y)]
        sends = [_remote(s_refs[i].at[2 * cx + cy], p_refs[i].at[j], send_sems, recv_sems, 3 * i + j, (cx, cy, c))
                 for i in range(n_w) for j, (cx, cy) in enumerate(chips)]
        for cp in sends:
            cp.start()
        for cp in sends:
            cp.wait_recv()
        for cp in sends:
            cp.wait_send()

    return pl.pallas_call(
        body, name="grad_chip_scatter",
        out_shape=[jax.ShapeDtypeStruct((N_CHIPS - 1,) + s.shape[1:], s.dtype) for s in pair_sums],
        in_specs=[ANY] * n_w, out_specs=[ANY] * n_w,
        scratch_shapes=[pltpu.SemaphoreType.DMA((3 * n_w,)), pltpu.SemaphoreType.DMA((3 * n_w,))],
    )(*pair_sums)


def _sibling_join(halves, name, after):
    n_w = len(halves)

    def body(*refs):
        s_refs, j_refs = refs[:n_w], refs[n_w + 1:2 * n_w + 1]
        send_sems, recv_sems = refs[2 * n_w + 1:]
        x, y, c = _my_place()
        sends = [_remote(s_refs[i], j_refs[i], send_sems, recv_sems, i, (x, y, 1 - c)) for i in range(n_w)]
        for cp in sends:
            cp.start()
        for cp in sends:
            cp.wait_recv()
        for cp in sends:
            cp.wait_send()

    return pl.pallas_call(
        body, name=name,
        out_shape=[jax.ShapeDtypeStruct(s.shape, s.dtype) for s in halves],
        in_specs=[ANY] * (n_w + 1), out_specs=[ANY] * n_w,
        scratch_shapes=[pltpu.SemaphoreType.DMA((n_w,)), pltpu.SemaphoreType.DMA((n_w,))],
    )(*halves, after)


HBM_SPEC = pl.BlockSpec(memory_space=pltpu.HBM)
SEM_SPEC = pl.BlockSpec(memory_space=pltpu.SEMAPHORE)
DATAFLOW = pltpu.SideEffectType.DATAFLOW_SIDE_EFFECTING


def _in_hbm(a):
    return pltpu.with_memory_space_constraint(a, pltpu.HBM)


def _exchange_start(name, srcs, lands, plan, n_copies, after, thru):
    n = len(srcs)

    def body(*refs):
        src_refs, land_refs = refs[:n], refs[n:2 * n]
        send_sems, recv_sems = refs[2 * n + 2], refs[2 * n + 3]
        for k, (src, dst, to) in enumerate(plan(src_refs, land_refs)):
            _remote(src, dst, send_sems, recv_sems, k, to).start()

    outs = pl.pallas_call(
        body, name=name,
        out_shape=(pltpu.SemaphoreType.DMA((n_copies,)), pltpu.SemaphoreType.DMA((n_copies,)),
                   *[pltpu.HBM(a.shape, a.dtype) for a in list(srcs) + list(lands) + [thru]]),
        in_specs=[HBM_SPEC] * (2 * n + 1) + [ANY],
        out_specs=(SEM_SPEC, SEM_SPEC, *[HBM_SPEC] * (2 * n + 1)),
        input_output_aliases={i: 2 + i for i in range(2 * n + 1)},
        compiler_params=pltpu.CompilerParams(has_side_effects=DATAFLOW),
    )(*[_in_hbm(a) for a in list(srcs) + list(lands) + [thru]], after)
    return outs[0], outs[1], outs[2:2 + n], outs[2 + n:2 + 2 * n], outs[2 + 2 * n]


def _exchange_wait(name, started, plan, after):
    send_sems, recv_sems, srcs, lands, _ = started
    n = len(srcs)

    def body(*refs):
        src_refs, land_refs = refs[:n], refs[n:2 * n]
        s_sems, r_sems = refs[2 * n], refs[2 * n + 1]
        for k, (src, dst, to) in enumerate(plan(src_refs, land_refs)):
            cp = _remote(src, dst, s_sems, r_sems, k, to)
            cp.wait_send()
            cp.wait_recv()

    outs = pl.pallas_call(
        body, name=name,
        out_shape=tuple(pltpu.HBM(a.shape, a.dtype) for a in list(srcs) + list(lands)),
        in_specs=[HBM_SPEC] * (2 * n) + [SEM_SPEC, SEM_SPEC, ANY],
        out_specs=tuple([HBM_SPEC] * (2 * n)),
        input_output_aliases={i: i for i in range(2 * n)},
        compiler_params=pltpu.CompilerParams(has_side_effects=DATAFLOW),
    )(*srcs, *lands, send_sems, recv_sems, after)
    return outs[:n], outs[n:]


def _late_gather_plan(src_refs, land_refs):
    x, y, c = _my_place()
    chips = [(1 - x, y), (x, 1 - y), (1 - x, 1 - y)]
    return [(src, land.at[2 * x + y], (cx, cy, c)) for src, land in zip(src_refs, land_refs) for cx, cy in chips]


def _late_scatter_plan(src_refs, land_refs):
    x, y, c = _my_place()
    chips = [(1 - x, y), (x, 1 - y), (1 - x, 1 - y)]
    return [(src.at[2 * cx + cy], land.at[j], (cx, cy, c))
            for src, land in zip(src_refs, land_refs) for j, (cx, cy) in enumerate(chips)]


def _row_tile(rows, mult=16, limit=ROW_TILE):
    return max(d for d in range(mult, limit + 1, mult) if rows % d == 0)


def _pair_sum(place, g, theirs, axis, name):
    nj, rr, cc = theirs.shape
    tr = _row_tile(rr, limit=1024)
    nb = rr // tr
    if axis == 0:
        g_map = lambda j, i, pr: (j, pr[0] * nb + i, 0)
    else:
        g_map = lambda j, i, pr: (j, i, pr[0])

    def body(pr, g_ref, t_ref, o_ref):
        o_ref[...] = (g_ref[...].astype(F32) + t_ref[...].astype(F32)).astype(BF16)

    spec = pl.BlockSpec((None, tr, cc), lambda j, i, pr: (j, i, 0))
    return pl.pallas_call(
        body, name=name,
        grid_spec=pltpu.PrefetchScalarGridSpec(
            num_scalar_prefetch=1, grid=(nj, nb),
            in_specs=[pl.BlockSpec((None, tr, cc), g_map), spec], out_specs=spec),
        out_shape=jax.ShapeDtypeStruct(theirs.shape, BF16))(place, g, theirs)


def _chip_sum(place, pair_sums, parts, name, transposed):
    _, rr, cc = parts.shape
    tr = _row_tile(rr, LANES) if transposed else _row_tile(rr, limit=1024)

    def body(pr, h_ref, p_ref, o_ref):
        acc = p_ref[0].astype(F32)
        for j in range(1, N_CHIPS - 1):
            acc = acc + p_ref[j].astype(F32)
        acc = acc + h_ref[...].astype(F32)
        o_ref[...] = (acc.T if transposed else acc).astype(BF16)

    out_spec = pl.BlockSpec((cc, tr), lambda i, pr: (0, i)) if transposed else pl.BlockSpec((tr, cc), lambda i, pr: (i, 0))
    return pl.pallas_call(
        body, name=name,
        grid_spec=pltpu.PrefetchScalarGridSpec(
            num_scalar_prefetch=1, grid=(rr // tr,),
            in_specs=[pl.BlockSpec((None, tr, cc), lambda i, pr: (pr[1], i, 0)),
                      pl.BlockSpec((N_CHIPS - 1, tr, cc), lambda i, pr: (0, i, 0))],
            out_specs=out_spec),
        out_shape=jax.ShapeDtypeStruct((cc, rr) if transposed else (rr, cc), BF16))(place, pair_sums, parts)


def _silu(v):
    return v / (1.0 + jnp.exp(-v))


def _ada_fwd(c_all, w_shard, b_shard):
    def body(c_ref, w_ref, b_ref, o_ref):
        o_ref[...] = jnp.dot(_silu(c_ref[...]), w_ref[...], precision=lax.Precision.HIGHEST,
                             preferred_element_type=F32) + b_ref[...]

    return pl.pallas_call(body, name="ada_fwd", out_shape=jax.ShapeDtypeStruct((c_all.shape[0], w_shard.shape[1]), F32),
                          compiler_params=pltpu.CompilerParams(vmem_limit_bytes=MM_VMEM_LIMIT))(c_all, w_shard, b_shard)


def _ada_bwd(c_all, dmod_cols):
    def body(c_ref, d_ref, o_ref):
        o_ref[...] = lax.dot_general(_silu(c_ref[...]), d_ref[...], (((0,), (0,)), ((), ())),
                                     precision=lax.Precision.HIGHEST, preferred_element_type=F32)

    return pl.pallas_call(body, name="ada_bwd", out_shape=jax.ShapeDtypeStruct((c_all.shape[1], dmod_cols.shape[1]), F32),
                          compiler_params=pltpu.CompilerParams(vmem_limit_bytes=MM_VMEM_LIMIT))(c_all, dmod_cols)


def _adamw_math(w, g, m, v):
    m = ADAM_B1 * m + (1.0 - ADAM_B1) * g
    v = ADAM_B2 * v + (1.0 - ADAM_B2) * (g * g)
    m_hat = m / (1.0 - ADAM_B1 ** ADAM_STEP)
    v_hat = v / (1.0 - ADAM_B2 ** ADAM_STEP)
    delta = -ADAM_LR * (m_hat / (jnp.sqrt(v_hat) + ADAM_EPS) + ADAM_WD * w)
    return delta, m, v


def _adamw(w, g, m, v, name):
    r, ccols = w.shape
    tr = max(d for d in range(8, ROW_TILE + 1, 8) if r % d == 0)
    spec = pl.BlockSpec((tr, ccols), lambda i: (i, 0))

    def body(w_ref, g_ref, m_ref, v_ref, d_ref, nm_ref, nv_ref):
        d_ref[...], nm_ref[...], nv_ref[...] = _adamw_math(w_ref[...], g_ref[...], m_ref[...], v_ref[...])

    return pl.pallas_call(body, name=name, grid=(r // tr,), in_specs=[spec] * 4, out_specs=[spec] * 3,
                          out_shape=[jax.ShapeDtypeStruct(w.shape, F32)] * 3,
                          compiler_params=pltpu.CompilerParams(vmem_limit_bytes=MM_VMEM_LIMIT))(w, g, m, v)


def _adamw_small(w, g_all, m, v):
    def body(w_ref, g_ref, m_ref, v_ref, gs_ref, d_ref, nm_ref, nv_ref):
        g = g_ref[0]
        for d in range(1, N_DEV):
            g = g + g_ref[d]
        gs_ref[...] = g
        d_ref[...], nm_ref[...], nv_ref[...] = _adamw_math(w_ref[...], g, m_ref[...], v_ref[...])

    return pl.pallas_call(body, name="adamw_small", out_shape=[jax.ShapeDtypeStruct(w.shape, F32)] * 4)(w, g_all, m, v)


def _adamw_halves(place, w, own, sib, m, v, axis, name, after):
    r, cc = w.shape
    if axis == 0:
        rows, gc = own.shape[0], own.shape[1]
        tr = _row_tile(rows)
        nb = rows // tr
        w_spec = pl.BlockSpec((tr, cc), lambda h, i, pr: (h * nb + i, 0))
        g_spec = pl.BlockSpec((tr, gc), lambda h, i, pr: (i, 0))
    else:
        tr = _row_tile(r)
        nb = r // tr
        gc = own.shape[1]
        w_spec = pl.BlockSpec((tr, gc), lambda h, i, pr: (i, h))
        g_spec = pl.BlockSpec((tr, gc), lambda h, i, pr: (i, 0))
    wc = w_spec.block_shape[1]

    def body(pr, w_ref, o_ref, s_ref, m_ref, v_ref, after_ref, g_ref, d_ref, nm_ref, nv_ref):
        g = jnp.where(pl.program_id(0) == pr[0], o_ref[...], s_ref[...]).astype(F32)[:, :wc]
        g_ref[...] = g
        d_ref[...], nm_ref[...], nv_ref[...] = _adamw_math(w_ref[...], g, m_ref[...], v_ref[...])

    return pl.pallas_call(
        body, name=name,
        grid_spec=pltpu.PrefetchScalarGridSpec(
            num_scalar_prefetch=1, grid=(2, nb),
            in_specs=[w_spec, g_spec, g_spec, w_spec, w_spec, ANY], out_specs=[w_spec] * 4),
        out_shape=[jax.ShapeDtypeStruct(w.shape, F32)] * 4,
        compiler_params=pltpu.CompilerParams(vmem_limit_bytes=MM_VMEM_LIMIT))(place, w, own, sib, m, v, after)


SMALL = ("b_ada", "norm_attn", "norm_ffn", "q_a_norm", "kv_a_norm", "q_norm", "k_nope_norm", "k_rope_norm",
         "out_norm_sb", "out_norm_mla")
WEIGHTS = ("w_ada", "b_ada", "norm_attn", "norm_ffn", "w_in", "q_a_norm", "w_q_up", "kv_a_norm", "w_kv_up",
           "q_norm", "k_nope_norm", "k_rope_norm", "out_norm_sb", "out_norm_mla", "w_out", "w_gate", "w_up",
           "w_down")


def kernel(x, c, positions, w_ada, b_ada, norm_attn, norm_ffn, w_in, q_a_norm, w_q_up, kv_a_norm, w_kv_up, q_norm, k_nope_norm, k_rope_norm, out_norm_sb, out_norm_mla, w_out, w_gate, w_up, w_down, loss_target, m_w_ada, m_b_ada, m_norm_attn, m_norm_ffn, m_w_in, m_q_a_norm, m_w_q_up, m_kv_a_norm, m_w_kv_up, m_q_norm, m_k_nope_norm, m_k_rope_norm, m_out_norm_sb, m_out_norm_mla, m_w_out, m_w_gate, m_w_up, m_w_down, v_w_ada, v_b_ada, v_norm_attn, v_norm_ffn, v_w_in, v_q_a_norm, v_w_q_up, v_kv_a_norm, v_w_kv_up, v_q_norm, v_k_nope_norm, v_k_rope_norm, v_out_norm_sb, v_out_norm_mla, v_w_out, v_w_gate, v_w_up, v_w_down):
    local = dict(locals())
    w = {n: local[n][0] for n in WEIGHTS}
    m = {n: local["m_" + n][0] for n in WEIGHTS}
    v = {n: local["v_" + n][0] for n in WEIGHTS}
    small = {n: w[n].reshape(1, -1) for n in SMALL}
    ix, iy, ic = _my_place()
    chip = 2 * ix + iy
    dev = 2 * chip + ic
    xs, target = x[0], loss_target[0]
    seq = xs.shape[0]

    c_all = _all_gather_small(c.reshape(8, LANES), "gather_c").reshape(N_DEV, D_MODEL)
    ada_cols = w["w_ada"].shape[1]
    b_cols = lax.dynamic_slice_in_dim(small["b_ada"], chip * ada_cols, ada_cols, axis=1)
    mod_cols = _ada_fwd(c_all, w["w_ada"], b_cols)
    mod_all = _all_gather_small(mod_cols, "gather_mod").reshape(N_CHIPS, 2, N_DEV, ada_cols)
    mod = lax.dynamic_index_in_dim(mod_all[:, 0], dev, axis=1, keepdims=False).reshape(1, N_MOD * D_MODEL)

    ff_pad = FF_SHARD_PAD - FF_SHARD
    pads = {"w_gate": ((0, 0), (0, ff_pad)), "w_up": ((0, 0), (0, ff_pad)), "w_down": ((0, ff_pad), (0, 0))}
    shards = {n: jnp.pad(w[n].astype(BF16), pads[n]) if n in pads else w[n].astype(BF16) for n in BIG}
    early, early_done = _gather_weights(EARLY, [shards[n] for n in EARLY], mod)
    gathered = dict(zip(EARLY, early))
    lands = [lax.dynamic_update_index_in_dim(lax.empty((N_CHIPS,) + shards[n].shape, BF16), shards[n], chip, 0)
             for n in LATE]
    late_gather = _exchange_start("gather_late_start", [shards[n] for n in LATE], lands, _late_gather_plan,
                                  3 * len(LATE), early_done, mod)
    mod = late_gather[4]

    half = MLA_ROPE // 2
    freqs = 1.0 / (ROPE_THETA ** (np.arange(half, dtype=np.float32) / half))
    zeros = np.zeros(LANES - MLA_ROPE, np.float32)
    freqs_row = jnp.asarray(np.concatenate([freqs, freqs, zeros]).astype(np.float32)[None])
    sign_row = jnp.asarray(np.concatenate([-np.ones(half), np.ones(half), zeros]).astype(np.float32)[None])
    cos, sin = _rope_tables(positions.reshape(seq, 1), freqs_row, sign_row)

    place = jnp.stack([ic, chip]).astype(jnp.int32)
    small_params = {n: small[n] for n in SMALL if n != "b_ada"}

    def pair_sums_of(names, grads, call_name):
        theirs = _pair_exchange(names, grads, call_name)
        return [_pair_sum(place, gr, th, HALF_AXIS[n], "grad_pair_sum_" + n) for n, gr, th in zip(names, grads, theirs)]

    p1 = {**{n: gathered[n] for n in EARLY}, **small_params}
    mixed, mixing_vjp = jax.vjp(lambda x_, mod_, p_: _mixing_stage(x_, mod_, p_, cos, sin), xs, mod, p1)
    _, landed = _exchange_wait("gather_late_wait", late_gather, _late_gather_plan, mixed)
    p2 = {**dict(zip(LATE, landed)), **small_params}
    loss_part, ffn_vjp = jax.vjp(lambda x_, mixed_, mod_, p_: _ffn_stage(x_, mixed_, mod_, p_, target), xs, mixed, mod, p2)
    gx2, gmixed, gmod2, gp2 = ffn_vjp(jnp.ones((), F32))
    late_sums = pair_sums_of(LATE, [gp2[n] for n in LATE], "grad_pair_exchange_late")
    late_scatter = _exchange_start(
        "grad_scatter_late_start", late_sums,
        [lax.empty((N_CHIPS - 1,) + s.shape[1:], BF16) for s in late_sums], _late_scatter_plan, 3 * len(LATE),
        gx2, gmixed)
    gx1, gmod1, gp1 = mixing_vjp(late_scatter[4])
    gx = gx1 + gx2
    gmod = gmod1 + gmod2
    gp = {n: gp1[n] + gp2[n] for n in small_params}
    loss = lax.psum(loss_part, ("x", "y", "c"))

    small_names = [n for n in SMALL if n != "b_ada"]
    small_vec = jnp.concatenate([gmod] + [gp[n] for n in small_names], axis=1)
    n_small = small_vec.shape[1]
    small_all = _all_gather_small(small_vec.reshape(8, n_small // 8), "gather_small").reshape(N_DEV, 8, n_small // 8)

    g, delta, new_m, new_v = {}, {}, {}, {}

    def reduce_halves(names, sums, parts, join_name, after):
        own = [_chip_sum(place, ps, pt, "grad_chip_sum_" + n, n in TRANSPOSED_UPDATE) for n, ps, pt in zip(names, sums, parts)]
        return own, _sibling_join(own, join_name, after)

    def update(names, own, sib, after):
        for n, o, s in zip(names, own, sib):
            if n in TRANSPOSED_UPDATE:
                res = _adamw_halves(place, w[n].T, o, s, m[n].T, v[n].T, 1, "adamw_" + n, after)
                g[n], delta[n], new_m[n], new_v[n] = [r.T for r in res]
            else:
                g[n], delta[n], new_m[n], new_v[n] = _adamw_halves(place, w[n], o, s, m[n], v[n], HALF_AXIS[n],
                                                                   "adamw_" + n, after)

    late_sums, late_parts = _exchange_wait("grad_scatter_late_wait", late_scatter, _late_scatter_plan, gx)
    own_late, sib_late = reduce_halves(LATE, late_sums, late_parts, "grad_sibling_join_late", small_all)
    early_sums = pair_sums_of(EARLY, [gp1[n] for n in EARLY], "grad_pair_exchange_early")
    early_scatter = _exchange_start(
        "grad_scatter_early_start", early_sums,
        [lax.empty((N_CHIPS - 1,) + s.shape[1:], BF16) for s in early_sums], _late_scatter_plan, 3 * len(EARLY),
        sib_late[0], small_all)
    small_all = early_scatter[4]
    update(LATE, own_late, sib_late, small_all)

    def pack_small(d):
        return jnp.concatenate([d[n].reshape(1, -1) for n in SMALL], axis=1).reshape(8, n_small // 8)

    gs, ds, ms, vs = _adamw_small(pack_small(w), small_all, pack_small(m), pack_small(v))
    sizes = [w[n].size for n in SMALL]
    offs = np.concatenate([[0], np.cumsum(sizes)])

    def unpack_small(a):
        flat = a.reshape(-1)
        return {n: flat[offs[i]:offs[i + 1]].reshape(w[n].shape) for i, n in enumerate(SMALL)}

    for d, packed in zip((g, delta, new_m, new_v), (gs, ds, ms, vs)):
        d.update(unpack_small(packed))

    dmod_all = small_all.reshape(N_DEV, n_small)[:, :N_MOD * D_MODEL]
    g["w_ada"] = _ada_bwd(c_all, lax.dynamic_slice_in_dim(dmod_all, chip * ada_cols, ada_cols, axis=1))
    delta["w_ada"], new_m["w_ada"], new_v["w_ada"] = _adamw(w["w_ada"], g["w_ada"], m["w_ada"], v["w_ada"], "adamw_w_ada")

    early_sums, early_parts = _exchange_wait("grad_scatter_early_wait", early_scatter, _late_scatter_plan,
                                             delta["w_ada"])
    own_early, sib_early = reduce_halves(EARLY, early_sums, early_parts, "grad_sibling_join_early", delta["w_ada"])
    update(EARLY, own_early, sib_early, sib_early[0])

    def outs(d):
        return [d[n][None] for n in WEIGHTS]

    return (loss, gx[None], *outs(g), *outs(delta), *outs(new_m), *outs(new_v))
```

```python
import functools
import math

import numpy as np
import jax
import jax.numpy as jnp
from jax import lax
from jax.experimental import pallas as pl
from jax.experimental.pallas import tpu as pltpu

F32 = jnp.float32
BF16 = jnp.bfloat16
MESH = pl.DeviceIdType.MESH
ANY = pl.BlockSpec(memory_space=pl.ANY)

D_MODEL = 1024
SB_HEADS = 8
SB_HEAD_DIM = 64
SB_WIDTH = 512
MLA_HEADS = 4
MLA_NOPE = 128
MLA_ROPE = 64
MLA_QK = 192
MLA_V = 128
MLA_Q_RANK = 384
MLA_KV_RANK = 256
D_FF = 2816
N_MOD = 6
ROPE_THETA = 10000.0
EPS = 1e-6
LANES = 128

ADAM_LR = 0.001
ADAM_B1 = 0.9
ADAM_B2 = 0.999
ADAM_EPS = 1e-08
ADAM_WD = 0.01
ADAM_STEP = 10

N_CHIPS = 4
N_DEV = 8
ROW_TILE = 256
MM_ROW_TILE = 512
ATT_BLK = 256
MM_VMEM_LIMIT = 56 * 1024 * 1024
FF_SHARD = D_FF // N_CHIPS
FF_SHARD_PAD = 768


def _mm(a, b, mode, name, tm, tn, out_dtype=F32):
    if mode == "nn":
        (m, k), n = a.shape, b.shape[1]
        a_spec = pl.BlockSpec((tm, k), lambda j, i: (i, 0))
        b_spec = pl.BlockSpec((k, tn), lambda j, i: (0, j))
        dims = (((1,), (0,)), ((), ()))
    elif mode == "nt":
        (m, k), n = a.shape, b.shape[0]
        a_spec = pl.BlockSpec((tm, k), lambda j, i: (i, 0))
        b_spec = pl.BlockSpec((tn, k), lambda j, i: (j, 0))
        dims = (((1,), (1,)), ((), ()))
    else:
        (k, m), n = a.shape, b.shape[1]
        a_spec = pl.BlockSpec((k, tm), lambda j, i: (0, i))
        b_spec = pl.BlockSpec((k, tn), lambda j, i: (0, j))
        dims = (((0,), (0,)), ((), ()))
    assert m % tm == 0 and n % tn == 0, (name, m, n, tm, tn)

    def body(a_ref, b_ref, o_ref):
        o_ref[...] = lax.dot_general(a_ref[...].astype(BF16), b_ref[...].astype(BF16), dims,
                                     preferred_element_type=F32).astype(out_dtype)

    return pl.pallas_call(
        body, name=name, grid=(n // tn, m // tm),
        in_specs=[a_spec, b_spec],
        out_specs=pl.BlockSpec((tm, tn), lambda j, i: (i, j)),
        out_shape=jax.ShapeDtypeStruct((m, n), out_dtype),
        compiler_params=pltpu.CompilerParams(dimension_semantics=("arbitrary", "arbitrary"),
                                             vmem_limit_bytes=MM_VMEM_LIMIT),
    )(a, b)


def _make_linear(name, tk_w, tn_w):
    @jax.custom_vjp
    def op(a, w):
        return _mm(a, w, "nn", name + "_fwd", MM_ROW_TILE, w.shape[1])

    def fwd(a, w):
        return op(a, w), (a, w)

    def bwd(res, dy):
        a, w = res
        da = _mm(dy, w, "nt", name + "_dx", MM_ROW_TILE, w.shape[0])
        dw = _mm(a, dy, "tn", name + "_dw", tk_w, tn_w, out_dtype=BF16)
        return da, dw

    op.defvjp(fwd, bwd)
    return op


def _make_linear_split(name, widths, tk_w):
    starts = [sum(widths[:g]) for g in range(len(widths))]

    def call_fwd(a, w):
        t, k = a.shape
        n = w.shape[1]

        def body(a_ref, w_ref, *o_refs):
            y = jnp.dot(a_ref[...].astype(BF16), w_ref[...], preferred_element_type=F32)
            for o_ref, s0, wd in zip(o_refs, starts, widths):
                o_ref[...] = y[:, s0:s0 + wd]

        return pl.pallas_call(
            body, name=name + "_fwd", grid=(t // MM_ROW_TILE,),
            in_specs=[pl.BlockSpec((MM_ROW_TILE, k), lambda i: (i, 0)), pl.BlockSpec((k, n), lambda i: (0, 0))],
            out_specs=[pl.BlockSpec((MM_ROW_TILE, wd), lambda i: (i, 0)) for wd in widths],
            out_shape=[jax.ShapeDtypeStruct((t, wd), F32) for wd in widths],
            compiler_params=pltpu.CompilerParams(dimension_semantics=("arbitrary",), vmem_limit_bytes=MM_VMEM_LIMIT),
        )(a, w)

    def call_dx(dys, w):
        t = dys[0].shape[0]
        k, n = w.shape

        def body(*refs):
            dy_refs, w_ref, o_ref = refs[:-2], refs[-2], refs[-1]
            acc = jnp.zeros((MM_ROW_TILE, k), F32)
            for dy_ref, s0, wd in zip(dy_refs, starts, widths):
                acc = acc + _nt(dy_ref[...].astype(BF16), w_ref[:, s0:s0 + wd])
            o_ref[...] = acc

        return pl.pallas_call(
            body, name=name + "_dx", grid=(t // MM_ROW_TILE,),
            in_specs=[pl.BlockSpec((MM_ROW_TILE, wd), lambda i: (i, 0)) for wd in widths]
            + [pl.BlockSpec((k, n), lambda i: (0, 0))],
            out_specs=pl.BlockSpec((MM_ROW_TILE, k), lambda i: (i, 0)),
            out_shape=jax.ShapeDtypeStruct((t, k), F32),
            compiler_params=pltpu.CompilerParams(dimension_semantics=("arbitrary",), vmem_limit_bytes=MM_VMEM_LIMIT),
        )(*dys, w)

    def call_dw(a, dys, w):
        t, k = a.shape
        n = w.shape[1]

        def body(a_ref, *refs):
            dy_refs, o_ref = refs[:-1], refs[-1]
            ab = a_ref[...].astype(BF16)
            for dy_ref, s0, wd in zip(dy_refs, starts, widths):
                o_ref[:, s0:s0 + wd] = _tn(ab, dy_ref[...].astype(BF16)).astype(BF16)
            if starts[-1] + widths[-1] < n:
                o_ref[:, starts[-1] + widths[-1]:] = jnp.zeros((tk_w, n - starts[-1] - widths[-1]), BF16)

        return pl.pallas_call(
            body, name=name + "_dw", grid=(k // tk_w,),
            in_specs=[pl.BlockSpec((t, tk_w), lambda i: (0, i))]
            + [pl.BlockSpec((t, wd), lambda i: (0, 0)) for wd in widths],
            out_specs=pl.BlockSpec((tk_w, n), lambda i: (i, 0)),
            out_shape=jax.ShapeDtypeStruct((k, n), BF16),
            compiler_params=pltpu.CompilerParams(dimension_semantics=("arbitrary",), vmem_limit_bytes=MM_VMEM_LIMIT),
        )(a, *dys)

    @jax.custom_vjp
    def op(a, w):
        return tuple(call_fwd(a, w))

    def fwd(a, w):
        return op(a, w), (a, w)

    def bwd(res, dys):
        a, w = res
        return call_dx(dys, w), call_dw(a, dys, w)

    op.defvjp(fwd, bwd)
    return op


def _make_linear_sharded(name, tk_w):
    def call_fwd(a, w):
        t, k = a.shape
        n_sh, _, cc = w.shape

        def body(a_ref, w_ref, o_ref):
            o_ref[...] = jnp.dot(a_ref[...].astype(BF16), w_ref[...], preferred_element_type=F32)

        return pl.pallas_call(
            body, name=name + "_fwd", grid=(n_sh, t // ROW_TILE),
            in_specs=[pl.BlockSpec((ROW_TILE, k), lambda j, i: (i, 0)),
                      pl.BlockSpec((None, k, cc), lambda j, i: (j, 0, 0))],
            out_specs=pl.BlockSpec((ROW_TILE, cc), lambda j, i: (i, j)),
            out_shape=jax.ShapeDtypeStruct((t, n_sh * cc), F32),
            compiler_params=pltpu.CompilerParams(dimension_semantics=("arbitrary", "arbitrary"),
                                                 vmem_limit_bytes=MM_VMEM_LIMIT),
        )(a, w)

    def call_dx(dy, w):
        t = dy.shape[0]
        n_sh, k, cc = w.shape

        def body(dy_ref, w_ref, o_ref):
            acc = jnp.zeros((ROW_TILE, k), F32)
            for j in range(n_sh):
                acc = acc + _nt(dy_ref[:, j * cc:(j + 1) * cc].astype(BF16), w_ref[j])
            o_ref[...] = acc

        return pl.pallas_call(
            body, name=name + "_dx", grid=(t // ROW_TILE,),
            in_specs=[pl.BlockSpec((ROW_TILE, n_sh * cc), lambda i: (i, 0)),
                      pl.BlockSpec((n_sh, k, cc), lambda i: (0, 0, 0))],
            out_specs=pl.BlockSpec((ROW_TILE, k), lambda i: (i, 0)),
            out_shape=jax.ShapeDtypeStruct((t, k), F32),
            compiler_params=pltpu.CompilerParams(dimension_semantics=("arbitrary",),
                                                 vmem_limit_bytes=MM_VMEM_LIMIT),
        )(dy, w)

    def call_dw(a, dy, w):
        t, k = a.shape
        n_sh, _, cc = w.shape

        def body(a_ref, dy_ref, o_ref):
            o_ref[...] = _tn(a_ref[...].astype(BF16), dy_ref[...].astype(BF16)).astype(BF16)

        return pl.pallas_call(
            body, name=name + "_dw", grid=(n_sh, k // tk_w),
            in_specs=[pl.BlockSpec((t, tk_w), lambda j, i: (0, i)),
                      pl.BlockSpec((t, cc), lambda j, i: (0, j))],
            out_specs=pl.BlockSpec((None, tk_w, cc), lambda j, i: (j, i, 0)),
            out_shape=jax.ShapeDtypeStruct(w.shape, BF16),
            compiler_params=pltpu.CompilerParams(dimension_semantics=("arbitrary", "arbitrary"),
                                                 vmem_limit_bytes=MM_VMEM_LIMIT),
        )(a, dy)

    @jax.custom_vjp
    def op(a, w):
        return call_fwd(a, w)

    def fwd(a, w):
        return op(a, w), (a, w)

    def bwd(res, dy):
        a, w = res
        return call_dx(dy, w), call_dw(a, dy, w)

    op.defvjp(fwd, bwd)
    return op


def _row_spec(arr, tb):
    return pl.BlockSpec((tb, arr.shape[1]), lambda i: (i, 0))


def _full_spec(arr):
    return pl.BlockSpec(arr.shape, lambda i: (0, 0))


def _make_rowwise(name, f, n_rows, n_params, out_cols, diff_rows, out_dtypes=None, grad_dtypes=None):
    n_out = len(out_cols)
    out_dtypes = out_dtypes or [F32] * n_out
    grad_dtypes = grad_dtypes or [F32] * sum(diff_rows)

    def call_fwd(rows, params):
        t = rows[0].shape[0]

        def body(*refs):
            ins = [r[...] for r in refs[:n_rows + n_params]]
            outs = f(*ins)
            for o_ref, o in zip(refs[n_rows + n_params:], outs):
                o_ref[...] = o.astype(o_ref.dtype)

        return pl.pallas_call(
            body, name=name + "_fwd", grid=(t // ROW_TILE,),
            in_specs=[_row_spec(a, ROW_TILE) for a in rows] + [_full_spec(p) for p in params],
            out_specs=[pl.BlockSpec((ROW_TILE, n), lambda i: (i, 0)) for n in out_cols],
            out_shape=[jax.ShapeDtypeStruct((t, n), dt) for n, dt in zip(out_cols, out_dtypes)],
            compiler_params=pltpu.CompilerParams(dimension_semantics=("arbitrary",),
                                                 vmem_limit_bytes=MM_VMEM_LIMIT),
        )(*rows, *params)

    def call_bwd(rows, params, cts):
        t = rows[0].shape[0]
        d_rows = [a for a, d in zip(rows, diff_rows) if d]
        n_in = n_rows + n_params + n_out

        def body(*refs):
            ins = [r[...] for r in refs[:n_rows + n_params]]
            ct = tuple(r[...].astype(F32) for r in refs[n_rows + n_params:n_in])
            _, vjp = jax.vjp(f, *ins)
            grads = vjp(ct)
            out_refs = refs[n_in:]
            g_rows = [g for g, d in zip(grads[:n_rows], diff_rows) if d]
            for o_ref, g in zip(out_refs[:len(g_rows)], g_rows):
                o_ref[...] = g.astype(o_ref.dtype)
            p_refs = out_refs[len(g_rows):]

            if p_refs:
                @pl.when(pl.program_id(0) == 0)
                def _():
                    for p_ref in p_refs:
                        p_ref[...] = jnp.zeros_like(p_ref)

                for p_ref, g in zip(p_refs, grads[n_rows:]):
                    p_ref[...] += g

        return pl.pallas_call(
            body, name=name + "_bwd", grid=(t // ROW_TILE,),
            in_specs=[_row_spec(a, ROW_TILE) for a in rows] + [_full_spec(p) for p in params]
            + [_row_spec(c, ROW_TILE) for c in cts],
            out_specs=[_row_spec(a, ROW_TILE) for a in d_rows] + [_full_spec(p) for p in params],
            out_shape=[jax.ShapeDtypeStruct(a.shape, dt) for a, dt in zip(d_rows, grad_dtypes)]
            + [jax.ShapeDtypeStruct(p.shape, F32) for p in params],
            compiler_params=pltpu.CompilerParams(dimension_semantics=("arbitrary",),
                                                 vmem_limit_bytes=MM_VMEM_LIMIT),
        )(*rows, *params, *cts)

    @jax.custom_vjp
    def op(*args):
        return tuple(call_fwd(args[:n_rows], args[n_rows:]))

    def fwd(*args):
        return op(*args), args

    def bwd(args, cts):
        rows, params = args[:n_rows], args[n_rows:]
        outs = call_bwd(rows, params, cts)
        it = iter(outs)
        g_rows = [next(it) if d else jnp.zeros_like(a) for a, d in zip(rows, diff_rows)]
        return tuple(g_rows) + tuple(it)

    op.defvjp(fwd, bwd)
    return op


def _rms(x, g, n):
    return x * lax.rsqrt(jnp.sum(x * x, axis=-1, keepdims=True) * (1.0 / n) + EPS) * g


def _f_pre_attn(x, g, scale, shift):
    return (_rms(x, g, D_MODEL) * (1.0 + scale) + shift,)


def _f_mla_a(cq, ckv, gq, gkv):
    return _rms(cq, gq, MLA_Q_RANK), _rms(ckv, gkv, MLA_KV_RANK)


@jax.custom_vjp
def _split_lanes(x):
    return tuple(x[:, i * LANES:(i + 1) * LANES] for i in range(x.shape[1] // LANES))


def _split_lanes_fwd(x):
    return _split_lanes(x), None


def _split_lanes_bwd(_, cts):
    return (jnp.concatenate(cts, axis=1),)


_split_lanes.defvjp(_split_lanes_fwd, _split_lanes_bwd)


def _f_mla_b(qall, kn_all, kr, kr_sw, cos, sin, gqn, gqr, gqr_sw, gkn, gkr, gkr_sw):
    q = _split_lanes(qall)
    kn = _split_lanes(kn_all)
    qn_o, qr_o, kn_o = [], [], []
    for h in range(MLA_HEADS):
        qn, qr, qs = q[h], q[MLA_HEADS + h], q[2 * MLA_HEADS + h]
        ss = jnp.sum(qn * qn, axis=-1, keepdims=True) + jnp.sum(qr * qr, axis=-1, keepdims=True)
        rs = lax.rsqrt(ss * (1.0 / MLA_QK) + EPS)
        qn_o.append(qn * rs * gqn)
        qr_o.append((qr * rs * gqr) * cos + (qs * rs * gqr_sw) * sin)
        kn_o.append(_rms(kn[h], gkn, MLA_NOPE))
    rs = lax.rsqrt(jnp.sum(kr * kr, axis=-1, keepdims=True) * (1.0 / MLA_ROPE) + EPS)
    kr_o = (kr * rs * gkr) * cos + (kr_sw * rs * gkr_sw) * sin
    return (jnp.concatenate(qn_o, axis=1), jnp.concatenate(qr_o, axis=1), jnp.concatenate(kn_o, axis=1), kr_o)


def _f_post_attn(o_sb, o_mla, g_sb, g_mla):
    return (jnp.concatenate([_rms(o_sb, g_sb, SB_WIDTH), _rms(o_mla, g_mla, SB_WIDTH)], axis=1),)


def _f_pre_ffn(x, attn, gate, g, scale, shift):
    x2 = x + gate * attn
    return x2, _rms(x2, g, D_MODEL) * (1.0 + scale) + shift


def _f_swiglu(gt, up):
    return (gt / (1.0 + jnp.exp(-gt)) * up,)


def _f_loss(x2, ffn, target, gate):
    err = x2 + gate * ffn - target
    return (jnp.sum(err * err, axis=-1, keepdims=True) * (1.0 / D_MODEL),)


def _rope_tables(pos_col, freqs, sign):
    t = pos_col.shape[0]

    def body(p_ref, f_ref, s_ref, cos_ref, sin_ref):
        ang = p_ref[...].astype(F32) * f_ref[...]
        live = jnp.abs(s_ref[...])
        cos_ref[...] = jnp.cos(ang) * live
        sin_ref[...] = jnp.sin(ang) * s_ref[...]

    return pl.pallas_call(
        body, name="rope_tables", grid=(t // ROW_TILE,),
        in_specs=[pl.BlockSpec((ROW_TILE, 1), lambda i: (i, 0)), _full_spec(freqs), _full_spec(sign)],
        out_specs=[pl.BlockSpec((ROW_TILE, LANES), lambda i: (i, 0))] * 2,
        out_shape=[jax.ShapeDtypeStruct((t, LANES), F32)] * 2,
    )(pos_col, freqs, sign)


def _hi_lo_dot(x, tri):
    hi = x.astype(BF16)
    lo = (x - hi.astype(F32)).astype(BF16)
    return (jnp.dot(hi, tri, preferred_element_type=F32) + jnp.dot(lo, tri, preferred_element_type=F32))


def _tri(cmp):
    r = lax.broadcasted_iota(jnp.int32, (ATT_BLK, ATT_BLK), 0)
    c = lax.broadcasted_iota(jnp.int32, (ATT_BLK, ATT_BLK), 1)
    return cmp(r, c).astype(BF16)


def _nt(a, b):
    return lax.dot_general(a, b, (((1,), (1,)), ((), ())), preferred_element_type=F32)


def _tn(a, b):
    return lax.dot_general(a, b, (((0,), (0,)), ((), ())), preferred_element_type=F32)


def _sb_logs(z):
    lb = jnp.minimum(z, 0.0) - jnp.log(1.0 + jnp.exp(-jnp.abs(z)))
    return lb, lb - z


def _sb_fwd(q, k, v):
    t = q.shape[0]
    nq = t // ATT_BLK
    scale = SB_HEAD_DIM ** -0.5

    def body(q_ref, k_ref, v_ref, o_ref, tot_ref):
        qi = pl.program_id(1)
        lane = lax.broadcasted_iota(jnp.int32, (ATT_BLK, LANES), 1)
        tri = _tri(lambda r, c: r > c)
        qv = q_ref[...] * scale
        heads = [(lane // SB_HEAD_DIM) == hh for hh in range(2)]
        qms = [jnp.where(mine, qv, 0.0).astype(BF16) for mine in heads]

        def blocks(kbs, carry, diagonal):
            acc = carry[0]
            nb = len(kbs)
            chains = [(b, hh) for b in range(nb) for hh in range(2)]
            offs = [pl.multiple_of(kb * ATT_BLK, ATT_BLK) for kb in kbs]
            kks = [k_ref[pl.ds(off, ATT_BLK), :].astype(BF16) for off in offs]
            v_blks = [v_ref[pl.ds(off, ATT_BLK), :] for off in offs]
            if any(diagonal):
                valid = (lax.broadcasted_iota(jnp.int32, (ATT_BLK, ATT_BLK), 1)
                         < lax.broadcasted_iota(jnp.int32, (ATT_BLK, ATT_BLK), 0))
            zs = {ch: _nt(qms[ch[1]], kks[ch[0]]) for ch in chains}
            vvs = {(b, hh): jnp.where(heads[hh], v_blks[b], 0.0).astype(BF16) for b, hh in chains}
            logs = {ch: _sb_logs(zs[ch]) for ch in chains}
            l1ms = {ch: jnp.where(valid, logs[ch][1], 0.0) if diagonal[ch[0]] else logs[ch][1] for ch in chains}
            run = {(0, hh): carry[1 + hh] for hh in range(2)}
            for b, hh in chains:
                run[(b + 1, hh)] = run[(b, hh)] + jnp.sum(l1ms[(b, hh)], axis=-1, keepdims=True)
            afters = {ch: _hi_lo_dot(l1ms[ch], tri) for ch in chains}
            ws = {ch: jnp.exp(logs[ch][0] + (afters[ch] + run[ch])) for ch in chains}
            ws = {ch: jnp.where(valid, ws[ch], 0.0) if diagonal[ch[0]] else ws[ch] for ch in chains}
            for ch in chains:
                acc = acc + jnp.dot(ws[ch].astype(BF16), vvs[ch], preferred_element_type=F32)
            return (acc, run[(nb, 0)], run[(nb, 1)])

        zero = jnp.zeros((ATT_BLK, 1), F32)
        init = (jnp.zeros((ATT_BLK, LANES), F32), zero, zero)
        carry = lax.cond(qi % 2 == 1, lambda cr: blocks([qi, qi - 1], cr, (True, False)),
                         lambda cr: blocks([qi], cr, (True,)), init)
        top = qi - 1 - qi % 2
        carry = lax.fori_loop(0, qi // 2, lambda pr, cr: blocks([top - 2 * pr, top - 1 - 2 * pr], cr, (False, False)),
                              carry)
        o_ref[...] = carry[0]
        for hh in range(2):
            tot_ref[:, hh * LANES:(hh + 1) * LANES] = jnp.broadcast_to(carry[1 + hh], (ATT_BLK, LANES))

    return pl.pallas_call(
        body, name="sb_attn_fwd", grid=(SB_HEADS // 2, nq),
        in_specs=[pl.BlockSpec((ATT_BLK, LANES), lambda p, i: (i, p)),
                  pl.BlockSpec((t, LANES), lambda p, i: (0, p)),
                  pl.BlockSpec((t, LANES), lambda p, i: (0, p))],
        out_specs=[pl.BlockSpec((ATT_BLK, LANES), lambda p, i: (i, p)),
                   pl.BlockSpec((ATT_BLK, 2 * LANES), lambda p, i: (i, p))],
        out_shape=[jax.ShapeDtypeStruct((t, SB_WIDTH), F32), jax.ShapeDtypeStruct((t, SB_HEADS * LANES), F32)],
        compiler_params=pltpu.CompilerParams(dimension_semantics=("arbitrary", "arbitrary")),
    )(q, k, v)


def _sb_bwd(q, k, v, tot, do):
    t = q.shape[0]
    nq = t // ATT_BLK
    scale = SB_HEAD_DIM ** -0.5

    def body(q_ref, k_ref, v_ref, tot_ref, do_ref, dq_ref, dk_ref, dv_ref):
        qi = pl.program_id(1)

        @pl.when(qi == 0)
        def _():
            dk_ref[...] = jnp.zeros_like(dk_ref)
            dv_ref[...] = jnp.zeros_like(dv_ref)

        lane = lax.broadcasted_iota(jnp.int32, (ATT_BLK, LANES), 1)
        tri_incl = _tri(lambda r, c: r <= c)
        tri_lt = _tri(lambda r, c: r < c)
        qv = q_ref[...] * scale
        dov = do_ref[...]
        heads = [(lane // SB_HEAD_DIM) == hh for hh in range(2)]
        qms = [jnp.where(mine, qv, 0.0).astype(BF16) for mine in heads]
        doms = [jnp.where(mine, dov, 0.0).astype(BF16) for mine in heads]
        tots = [tot_ref[:, hh * LANES:hh * LANES + 1] for hh in range(2)]

        def blocks(kbs, carry, diagonal):
            dq = carry[0]
            nb = len(kbs)
            chains = [(b, hh) for b in range(nb) for hh in range(2)]
            offs = [pl.multiple_of(kb * ATT_BLK, ATT_BLK) for kb in kbs]
            k_blks = [k_ref[pl.ds(off, ATT_BLK), :] for off in offs]
            vvs = [v_ref[pl.ds(off, ATT_BLK), :].astype(BF16) for off in offs]
            if any(diagonal):
                valid = (lax.broadcasted_iota(jnp.int32, (ATT_BLK, ATT_BLK), 1)
                         < lax.broadcasted_iota(jnp.int32, (ATT_BLK, ATT_BLK), 0))
            kks = {(b, hh): jnp.where(heads[hh], k_blks[b], 0.0).astype(BF16) for b, hh in chains}
            zs = {ch: _nt(qms[ch[1]], kks[ch]) for ch in chains}
            dws = {ch: _nt(doms[ch[1]], vvs[ch[0]]) for ch in chains}
            logs = {ch: _sb_logs(zs[ch]) for ch in chains}
            lbs = {ch: logs[ch][0] for ch in chains}
            l1m_all = {ch: logs[ch][1] for ch in chains}
            l1ms = {ch: jnp.where(valid, l1m_all[ch], 0.0) if diagonal[ch[0]] else l1m_all[ch] for ch in chains}
            pre, c_de = {}, {}
            for hh in range(2):
                pre[(0, hh)], c_de[(0, hh)] = carry[1 + 2 * hh], carry[2 + 2 * hh]
            for b, hh in chains:
                pre[(b + 1, hh)] = pre[(b, hh)] + jnp.sum(l1ms[(b, hh)], axis=-1, keepdims=True)
            prefix = {ch: _hi_lo_dot(l1ms[ch], tri_incl) for ch in chains}
            ws = {ch: jnp.exp(lbs[ch] + (tots[ch[1]] - (prefix[ch] + pre[ch]))) for ch in chains}
            ws = {ch: jnp.where(valid, ws[ch], 0.0) if diagonal[ch[0]] else ws[ch] for ch in chains}
            d_es = {ch: ws[ch] * dws[ch] for ch in chains}
            for b, hh in chains:
                c_de[(b + 1, hh)] = c_de[(b, hh)] + jnp.sum(d_es[(b, hh)], axis=-1, keepdims=True)
            dvs = [_tn(ws[(b, 0)].astype(BF16), doms[0]) + _tn(ws[(b, 1)].astype(BF16), doms[1]) for b in range(nb)]
            dl1ms = {ch: jnp.dot(d_es[ch].astype(BF16), tri_lt, preferred_element_type=F32) + c_de[ch] for ch in chains}
            dzs = {ch: d_es[ch] * jnp.exp(l1m_all[ch]) - dl1ms[ch] * jnp.exp(lbs[ch]) for ch in chains}
            dzs = {ch: jnp.where(valid, dzs[ch], 0.0) if diagonal[ch[0]] else dzs[ch] for ch in chains}
            dzs = {ch: dzs[ch].astype(BF16) for ch in chains}
            for ch in chains:
                dq = dq + jnp.dot(dzs[ch], kks[ch], preferred_element_type=F32)
            for b in range(nb):
                dk_ref[pl.ds(offs[b], ATT_BLK), :] += _tn(dzs[(b, 0)], qms[0]) + _tn(dzs[(b, 1)], qms[1])
                dv_ref[pl.ds(offs[b], ATT_BLK), :] += dvs[b]
            return (dq, pre[(nb, 0)], c_de[(nb, 0)], pre[(nb, 1)], c_de[(nb, 1)])

        zero = jnp.zeros((ATT_BLK, 1), F32)
        carry = lax.fori_loop(0, qi // 2, lambda pr, cr: blocks([2 * pr, 2 * pr + 1], cr, (False, False)),
                              (jnp.zeros((ATT_BLK, LANES), F32), zero, zero, zero, zero))
        carry = lax.cond(qi % 2 == 1, lambda cr: blocks([qi - 1, qi], cr, (False, True)),
                         lambda cr: blocks([qi], cr, (True,)), carry)
        dq_ref[...] = carry[0] * scale

    return pl.pallas_call(
        body, name="sb_attn_bwd", grid=(SB_HEADS // 2, nq),
        in_specs=[pl.BlockSpec((ATT_BLK, LANES), lambda p, i: (i, p)),
                  pl.BlockSpec((t, LANES), lambda p, i: (0, p)),
                  pl.BlockSpec((t, LANES), lambda p, i: (0, p)),
                  pl.BlockSpec((ATT_BLK, 2 * LANES), lambda p, i: (i, p)),
                  pl.BlockSpec((ATT_BLK, LANES), lambda p, i: (i, p))],
        out_specs=[pl.BlockSpec((ATT_BLK, LANES), lambda p, i: (i, p)),
                   pl.BlockSpec((t, LANES), lambda p, i: (0, p)),
                   pl.BlockSpec((t, LANES), lambda p, i: (0, p))],
        out_shape=[jax.ShapeDtypeStruct((t, SB_WIDTH), F32)] * 3,
        compiler_params=pltpu.CompilerParams(dimension_semantics=("arbitrary", "arbitrary")),
    )(q, k, v, tot, do)


@jax.custom_vjp
def _sb_attention(q, k, v):
    return _sb_fwd(q, k, v)[0]


def _sb_attention_fwd(q, k, v):
    o, tot = _sb_fwd(q, k, v)
    return o, (q, k, v, tot)


def _sb_attention_bwd(res, do):
    return tuple(_sb_bwd(*res, do))


_sb_attention.defvjp(_sb_attention_fwd, _sb_attention_bwd)


def _mla_fwd(qn, qr, kn, kr, v):
    t = qn.shape[0]
    nq = t // ATT_BLK
    scale = MLA_QK ** -0.5

    def body(qn_ref, qr_ref, kn_ref, kr_ref, v_ref, o_ref, lse_ref):
        qi = pl.program_id(1)
        lanes = [slice(hh * LANES, (hh + 1) * LANES) for hh in range(2)]
        qnb = [qn_ref[:, sl].astype(BF16) for sl in lanes]
        qrb = [qr_ref[:, sl].astype(BF16) for sl in lanes]

        def blocks(kbs, carry, diagonal):
            nb = len(kbs)
            chains = [(b, hh) for b in range(nb) for hh in range(2)]
            offs = [pl.multiple_of(kb * ATT_BLK, ATT_BLK) for kb in kbs]
            krbs = [kr_ref[pl.ds(off, ATT_BLK), :].astype(BF16) for off in offs]
            accs, ms, ls = [carry[0], carry[3]], [carry[1], carry[4]], [carry[2], carry[5]]
            ss = {(b, hh): (_nt(qnb[hh], kn_ref[pl.ds(offs[b], ATT_BLK), lanes[hh]].astype(BF16))
                            + _nt(qrb[hh], krbs[b])) * scale for b, hh in chains}
            if any(diagonal):
                causal = (lax.broadcasted_iota(jnp.int32, (ATT_BLK, ATT_BLK), 1)
                          <= lax.broadcasted_iota(jnp.int32, (ATT_BLK, ATT_BLK), 0))
                ss = {ch: jnp.where(causal, ss[ch], -jnp.inf) if diagonal[ch[0]] else ss[ch] for ch in chains}
            m_new = list(ms)
            for b, hh in chains:
                m_new[hh] = jnp.maximum(m_new[hh], jnp.max(ss[(b, hh)], axis=-1, keepdims=True))
            ps = {(b, hh): jnp.exp(ss[(b, hh)] - m_new[hh]) for b, hh in chains}
            alphas = [jnp.exp(ms[hh] - m_new[hh]) for hh in range(2)]
            pvs = {(b, hh): jnp.dot(ps[(b, hh)].astype(BF16), v_ref[pl.ds(offs[b], ATT_BLK), lanes[hh]].astype(BF16),
                                    preferred_element_type=F32) for b, hh in chains}
            out = []
            for hh in range(2):
                acc, l = accs[hh] * alphas[hh], ls[hh] * alphas[hh]
                for b in range(nb):
                    acc, l = acc + pvs[(b, hh)], l + jnp.sum(ps[(b, hh)], axis=-1, keepdims=True)
                out += [acc, m_new[hh], l]
            return tuple(out)

        init = (jnp.zeros((ATT_BLK, LANES), F32), jnp.full((ATT_BLK, 1), -jnp.inf, F32), jnp.zeros((ATT_BLK, 1), F32))
        carry = lax.cond(qi % 2 == 1, lambda cr: blocks([qi, qi - 1], cr, (True, False)),
                         lambda cr: blocks([qi], cr, (True,)), init + init)
        carry = lax.fori_loop(0, qi // 2, lambda pr, cr: blocks([2 * pr, 2 * pr + 1], cr, (False, False)), carry)
        for hh in range(2):
            acc, m, l = carry[3 * hh:3 * hh + 3]
            o_ref[:, lanes[hh]] = acc / l
            lse_ref[:, lanes[hh]] = jnp.broadcast_to(m + jnp.log(l), (ATT_BLK, LANES))

    blk = pl.BlockSpec((ATT_BLK, 2 * LANES), lambda p, i: (i, p))
    full = pl.BlockSpec((t, 2 * LANES), lambda p, i: (0, p))
    return pl.pallas_call(
        body, name="mla_attn_fwd", grid=(MLA_HEADS // 2, nq),
        in_specs=[blk, blk, full, pl.BlockSpec((t, LANES), lambda p, i: (0, 0)), full],
        out_specs=[blk, blk],
        out_shape=[jax.ShapeDtypeStruct((t, MLA_HEADS * LANES), F32)] * 2,
        compiler_params=pltpu.CompilerParams(dimension_semantics=("arbitrary", "arbitrary")),
    )(qn, qr, kn, kr, v)


def _mla_bwd(qn, qr, kn, kr, v, o, lse, do):
    t = qn.shape[0]
    nq = t // ATT_BLK
    scale = MLA_QK ** -0.5

    def body(qn_ref, qr_ref, kn_ref, kr_ref, v_ref, o_ref, lse_ref, do_ref,
             dqn_ref, dqr_ref, dkn_ref, dkr_ref, dv_ref):
        pair = pl.program_id(0)
        qi = pl.program_id(1)

        @pl.when(qi == 0)
        def _():
            dkn_ref[...] = jnp.zeros_like(dkn_ref)
            dv_ref[...] = jnp.zeros_like(dv_ref)

        @pl.when((qi == 0) & (pair == 0))
        def _():
            dkr_ref[...] = jnp.zeros_like(dkr_ref)

        lanes = [slice(hh * LANES, (hh + 1) * LANES) for hh in range(2)]
        qnb = [qn_ref[:, sl].astype(BF16) for sl in lanes]
        qrb = [qr_ref[:, sl].astype(BF16) for sl in lanes]
        dob = [do_ref[:, sl].astype(BF16) for sl in lanes]
        delta = [jnp.sum(do_ref[:, sl] * o_ref[:, sl], axis=-1, keepdims=True) for sl in lanes]
        lse_v = [lse_ref[:, hh * LANES:hh * LANES + 1] for hh in range(2)]

        def blocks(kbs, carry, diagonal):
            nb = len(kbs)
            chains = [(b, hh) for b in range(nb) for hh in range(2)]
            offs = [pl.multiple_of(kb * ATT_BLK, ATT_BLK) for kb in kbs]
            krbs = [kr_ref[pl.ds(off, ATT_BLK), :].astype(BF16) for off in offs]
            knb = {(b, hh): kn_ref[pl.ds(offs[b], ATT_BLK), lanes[hh]].astype(BF16) for b, hh in chains}
            vb = {(b, hh): v_ref[pl.ds(offs[b], ATT_BLK), lanes[hh]].astype(BF16) for b, hh in chains}
            ss = {(b, hh): _nt(qnb[hh], knb[(b, hh)]) + _nt(qrb[hh], krbs[b]) for b, hh in chains}
            dps = {(b, hh): _nt(dob[hh], vb[(b, hh)]) for b, hh in chains}
            ps = {(b, hh): jnp.exp(ss[(b, hh)] * scale - lse_v[hh]) for b, hh in chains}
            if any(diagonal):
                causal = (lax.broadcasted_iota(jnp.int32, (ATT_BLK, ATT_BLK), 1)
                          <= lax.broadcasted_iota(jnp.int32, (ATT_BLK, ATT_BLK), 0))
                ps = {ch: jnp.where(causal, ps[ch], 0.0) if diagonal[ch[0]] else ps[ch] for ch in chains}
            dss = {(b, hh): (ps[(b, hh)] * (dps[(b, hh)] - delta[hh]) * scale).astype(BF16) for b, hh in chains}
            for b, hh in chains:
                dv_ref[pl.ds(offs[b], ATT_BLK), lanes[hh]] += _tn(ps[(b, hh)].astype(BF16), dob[hh])
            for b, hh in chains:
                dkn_ref[pl.ds(offs[b], ATT_BLK), lanes[hh]] += _tn(dss[(b, hh)], qnb[hh])
            for b in range(nb):
                dkr_ref[pl.ds(offs[b], ATT_BLK), :] += _tn(dss[(b, 0)], qrb[0]) + _tn(dss[(b, 1)], qrb[1])
            out = list(carry)
            for b, hh in chains:
                out[2 * hh] = out[2 * hh] + jnp.dot(dss[(b, hh)], knb[(b, hh)], preferred_element_type=F32)
                out[2 * hh + 1] = out[2 * hh + 1] + jnp.dot(dss[(b, hh)], krbs[b], preferred_element_type=F32)
            return tuple(out)

        zero = jnp.zeros((ATT_BLK, LANES), F32)
        carry = lax.fori_loop(0, qi // 2, lambda pr, cr: blocks([2 * pr, 2 * pr + 1], cr, (False, False)),
                              (zero, zero, zero, zero))
        carry = lax.cond(qi % 2 == 1, lambda cr: blocks([qi - 1, qi], cr, (False, True)),
                         lambda cr: blocks([qi], cr, (True,)), carry)
        for hh in range(2):
            dqn_ref[:, lanes[hh]] = carry[2 * hh]
            dqr_ref[:, lanes[hh]] = carry[2 * hh + 1]

    blk = pl.BlockSpec((ATT_BLK, 2 * LANES), lambda p, i: (i, p))
    full = pl.BlockSpec((t, 2 * LANES), lambda p, i: (0, p))
    shared = pl.BlockSpec((t, LANES), lambda p, i: (0, 0))
    wide = jax.ShapeDtypeStruct((t, MLA_HEADS * LANES), F32)
    return pl.pallas_call(
        body, name="mla_attn_bwd", grid=(MLA_HEADS // 2, nq),
        in_specs=[blk, blk, full, shared, full, blk, blk, blk],
        out_specs=[blk, blk, full, shared, full],
        out_shape=[wide, wide, wide, jax.ShapeDtypeStruct((t, LANES), F32), wide],
        compiler_params=pltpu.CompilerParams(dimension_semantics=("arbitrary", "arbitrary")),
    )(qn, qr, kn, kr, v, o, lse, do)


@jax.custom_vjp
def _mla_attention(qn, qr, kn, kr, v):
    return _mla_fwd(qn, qr, kn, kr, v)[0]


def _mla_attention_fwd(qn, qr, kn, kr, v):
    o, lse = _mla_fwd(qn, qr, kn, kr, v)
    return o, (qn, qr, kn, kr, v, o, lse)


def _mla_attention_bwd(res, do):
    return tuple(_mla_bwd(*res, do))


_mla_attention.defvjp(_mla_attention_fwd, _mla_attention_bwd)


def _ffn_in(h, wg, wu):
    t, k = h.shape
    n_sh, _, cc = wg.shape

    def body(h_ref, wg_ref, wu_ref, g_ref, u_ref, a_ref):
        hb = h_ref[...].astype(BF16)
        for j in range(n_sh):
            cols = slice(j * cc, (j + 1) * cc)
            g = jnp.dot(hb, wg_ref[j], preferred_element_type=F32)
            u = jnp.dot(hb, wu_ref[j], preferred_element_type=F32)
            g_ref[:, cols] = g
            u_ref[:, cols] = u
            a_ref[:, cols] = _f_swiglu(g, u)[0].astype(BF16)

    w_spec = pl.BlockSpec((n_sh, k, cc), lambda i: (0, 0, 0))
    o_spec = pl.BlockSpec((ROW_TILE, n_sh * cc), lambda i: (i, 0))
    wide = (t, n_sh * cc)
    return pl.pallas_call(
        body, name="ffn_in_fwd", grid=(t // ROW_TILE,),
        in_specs=[pl.BlockSpec((ROW_TILE, k), lambda i: (i, 0)), w_spec, w_spec],
        out_specs=[o_spec, o_spec, o_spec],
        out_shape=[jax.ShapeDtypeStruct(wide, F32), jax.ShapeDtypeStruct(wide, F32), jax.ShapeDtypeStruct(wide, BF16)],
        compiler_params=pltpu.CompilerParams(dimension_semantics=("arbitrary",), vmem_limit_bytes=MM_VMEM_LIMIT),
    )(h, wg, wu)


def _ffn_mid_bwd(dy, wd, g, u):
    t, n = dy.shape
    n_sh, cc, _ = wd.shape

    def body(dy_ref, wd_ref, g_ref, u_ref, dg_ref, du_ref):
        d_act = _nt(dy_ref[...].astype(BF16), wd_ref[...])
        _, vjp = jax.vjp(_f_swiglu, g_ref[...], u_ref[...])
        dg, du = vjp((d_act,))
        dg_ref[...] = dg.astype(BF16)
        du_ref[...] = du.astype(BF16)

    blk = pl.BlockSpec((MM_ROW_TILE, cc), lambda j, i: (i, j))
    wide = jax.ShapeDtypeStruct((t, n_sh * cc), BF16)
    return pl.pallas_call(
        body, name="ffn_mid_bwd", grid=(n_sh, t // MM_ROW_TILE),
        in_specs=[pl.BlockSpec((MM_ROW_TILE, n), lambda j, i: (i, 0)),
                  pl.BlockSpec((None, cc, n), lambda j, i: (j, 0, 0)), blk, blk],
        out_specs=[blk, blk], out_shape=[wide, wide],
        compiler_params=pltpu.CompilerParams(dimension_semantics=("arbitrary", "arbitrary"),
                                             vmem_limit_bytes=MM_VMEM_LIMIT),
    )(dy, wd, g, u)


def _ffn_dh(dg, du, wg, wu):
    t = dg.shape[0]
    n_sh, k, cc = wg.shape

    def body(dg_ref, du_ref, wg_ref, wu_ref, o_ref):
        acc = jnp.zeros((MM_ROW_TILE, k), F32)
        for j in range(n_sh):
            cols = slice(j * cc, (j + 1) * cc)
            acc = acc + _nt(dg_ref[:, cols], wg_ref[j]) + _nt(du_ref[:, cols], wu_ref[j])
        o_ref[...] = acc

    blk = pl.BlockSpec((MM_ROW_TILE, n_sh * cc), lambda i: (i, 0))
    w_spec = pl.BlockSpec((n_sh, k, cc), lambda i: (0, 0, 0))
    return pl.pallas_call(
        body, name="ffn_dh", grid=(t // MM_ROW_TILE,),
        in_specs=[blk, blk, w_spec, w_spec],
        out_specs=pl.BlockSpec((MM_ROW_TILE, k), lambda i: (i, 0)),
        out_shape=jax.ShapeDtypeStruct((t, k), F32),
        compiler_params=pltpu.CompilerParams(dimension_semantics=("arbitrary",), vmem_limit_bytes=MM_VMEM_LIMIT),
    )(dg, du, wg, wu)


def _ffn_dw_in(h, dy, n_sh, name):
    t, k = h.shape
    cc = dy.shape[1] // n_sh
    tk = 512

    def body(h_ref, dy_ref, o_ref):
        o_ref[...] = _tn(h_ref[...].astype(BF16), dy_ref[...]).astype(BF16)

    return pl.pallas_call(
        body, name=name, grid=(n_sh, k // tk),
        in_specs=[pl.BlockSpec((t, tk), lambda j, i: (0, i)), pl.BlockSpec((t, cc), lambda j, i: (0, j))],
        out_specs=pl.BlockSpec((None, tk, cc), lambda j, i: (j, i, 0)),
        out_shape=jax.ShapeDtypeStruct((n_sh, k, cc), BF16),
        compiler_params=pltpu.CompilerParams(dimension_semantics=("arbitrary", "arbitrary"),
                                             vmem_limit_bytes=MM_VMEM_LIMIT),
    )(h, dy)


@jax.custom_vjp
def _ffn_block(h, wg, wu, wd):
    act = _ffn_in(h, wg, wu)[2]
    return _mm(act, wd.reshape(-1, wd.shape[2]), "nn", "ffn_down_fwd", MM_ROW_TILE, wd.shape[2])


def _ffn_block_fwd(h, wg, wu, wd):
    g, u, act = _ffn_in(h, wg, wu)
    y = _mm(act, wd.reshape(-1, wd.shape[2]), "nn", "ffn_down_fwd", MM_ROW_TILE, wd.shape[2])
    return y, (h, wg, wu, wd, g, u, act)


def _ffn_block_bwd(res, dy):
    h, wg, wu, wd, g, u, act = res
    dg, du = _ffn_mid_bwd(dy, wd, g, u)
    dh = _ffn_dh(dg, du, wg, wu)
    n_sh = wg.shape[0]
    dwg = _ffn_dw_in(h, dg, n_sh, "ffn_gate_dw")
    dwu = _ffn_dw_in(h, du, n_sh, "ffn_up_dw")
    dwd = _mm(act, dy, "tn", "ffn_down_dw", 256, wd.shape[2], out_dtype=BF16).reshape(wd.shape)
    return dh, dwg, dwu, dwd


_ffn_block.defvjp(_ffn_block_fwd, _ffn_block_bwd)


def _split_cols(x, cuts, ct_dtype):
    cuts = tuple(cuts)

    @jax.custom_vjp
    def op(x):
        return tuple(x[:, a:b] for a, b in zip((0,) + cuts, cuts + (x.shape[1],)))

    def fwd(x):
        return op(x), None

    def bwd(_, cts):
        return (jnp.concatenate([c.astype(ct_dtype) for c in cts], axis=1),)

    op.defvjp(fwd, bwd)
    return op(x)


def _swap_halves(w):
    half = w.shape[-1] // 2
    return jnp.concatenate([w[..., half:], w[..., :half]], axis=-1)


def _pad_lanes(w):
    return jnp.concatenate([w, jnp.zeros(w.shape[:-1] + (LANES - w.shape[-1],), w.dtype)], axis=-1)


def _join_cols(shards):
    return shards.transpose(1, 0, 2).reshape(shards.shape[1], -1)


def _mod_parts(mod):
    return [mod[:, i * D_MODEL:(i + 1) * D_MODEL] for i in range(N_MOD)]


def _local_loss(x, mod, p, cos, sin, target):
    return _ffn_stage(x, _mixing_stage(x, mod, p, cos, sin), mod, p, target)


def _mixing_stage(x, mod, p, cos, sin):
    shift1, scale1 = _mod_parts(mod)[:2]

    w_in = _join_cols(p["w_in"])
    k_rope_w = w_in[:, 2176:2240]
    w_in_ext = jnp.concatenate([w_in[:, :2176], _pad_lanes(k_rope_w), _pad_lanes(_swap_halves(k_rope_w)),
                                jnp.zeros((D_MODEL, LANES), w_in.dtype)], axis=1)
    (h1,) = _make_rowwise("pre_attn", _f_pre_attn, 1, 3, [D_MODEL], [True], out_dtypes=[BF16])(
        x, p["norm_attn"], scale1, shift1)
    q_sb, k_sb, v_sb, cq, ckv, kr, kr_sw = _make_linear_split(
        "in_proj", (SB_WIDTH, SB_WIDTH, SB_WIDTH, MLA_Q_RANK, MLA_KV_RANK, LANES, LANES), 512)(h1, w_in_ext)

    o_sb = _sb_attention(q_sb, k_sb, v_sb)

    wq = _join_cols(p["w_q_up"]).reshape(MLA_Q_RANK, MLA_HEADS, MLA_QK)
    wq_n, wq_r = wq[:, :, :MLA_NOPE], wq[:, :, MLA_NOPE:]
    w_q_ext = jnp.concatenate([wq_n.reshape(MLA_Q_RANK, -1), _pad_lanes(wq_r).reshape(MLA_Q_RANK, -1),
                               _pad_lanes(_swap_halves(wq_r)).reshape(MLA_Q_RANK, -1)], axis=1)
    wkv = _join_cols(p["w_kv_up"]).reshape(MLA_KV_RANK, MLA_HEADS, MLA_NOPE + MLA_V)
    w_kv_ext = jnp.concatenate([wkv[:, :, :MLA_NOPE].reshape(MLA_KV_RANK, -1),
                                wkv[:, :, MLA_NOPE:].reshape(MLA_KV_RANK, -1)], axis=1)
    cqn, ckvn = _make_rowwise("mla_a", _f_mla_a, 2, 2, [MLA_Q_RANK, MLA_KV_RANK], [True, True],
                              out_dtypes=[BF16, BF16], grad_dtypes=[BF16, BF16])(
        cq, ckv, p["q_a_norm"], p["kv_a_norm"])
    qall = _make_linear("q_up", 384, 768)(cqn, w_q_ext)
    kn_all, v_mla = _make_linear_split("kv_up", (MLA_HEADS * MLA_NOPE, MLA_HEADS * MLA_V), MLA_KV_RANK)(ckvn, w_kv_ext)
    gq = p["q_norm"]
    gkr = p["k_rope_norm"]
    qn, qr, kn, krr = _make_rowwise("mla_b", _f_mla_b, 6, 6, [512, 512, 512, LANES],
                                    [True, True, True, True, False, False],
                                    out_dtypes=[BF16] * 4, grad_dtypes=[BF16] * 4)(
        qall, kn_all, kr, kr_sw, cos, sin,
        gq[:, :MLA_NOPE], _pad_lanes(gq[:, MLA_NOPE:]), _pad_lanes(_swap_halves(gq[:, MLA_NOPE:])),
        p["k_nope_norm"], _pad_lanes(gkr), _pad_lanes(_swap_halves(gkr)))
    o_mla = _mla_attention(qn, qr, kn, krr, v_mla)

    (mixed,) = _make_rowwise("post_attn", _f_post_attn, 2, 2, [D_MODEL], [True, True])(
        o_sb, o_mla, p["out_norm_sb"], p["out_norm_mla"])
    return mixed


def _ffn_stage(x, mixed, mod, p, target):
    _, _, gate1, shift2, scale2, gate2 = _mod_parts(mod)
    attn = _make_linear("out_proj", 512, 512)(mixed, p["w_out"].reshape(D_MODEL, D_MODEL))

    x2, h2 = _make_rowwise("pre_ffn", _f_pre_ffn, 2, 4, [D_MODEL, D_MODEL], [True, True],
                           out_dtypes=[F32, BF16], grad_dtypes=[F32, BF16])(
        x, attn, gate1, p["norm_ffn"], scale2, shift2)
    ffn = _ffn_block(h2, p["w_gate"], p["w_up"], p["w_down"])
    (row_loss,) = _make_rowwise("loss", _f_loss, 3, 1, [1], [True, True, False], grad_dtypes=[F32, BF16])(
        x2, ffn, target, gate2)
    return 0.5 * jnp.sum(row_loss)


def _my_place():
    return lax.axis_index("x"), lax.axis_index("y"), lax.axis_index("c")


def _all_gather_small(block, name):
    m_per, n = block.shape

    def body(x_ref, out_ref, send_sems, recv_sems, local_sem):
        x, y, c = _my_place()
        me, sibling = (x, y, c), (x, y, 1 - c)
        chips = [(1 - x, y), (x, 1 - y), (1 - x, 1 - y)]

        def rows(px, py, pc):
            return out_ref.at[pl.ds((4 * px + 2 * py + pc) * m_per, m_per), :]

        def copy(k, blk, to, src=None):
            return pltpu.make_async_remote_copy(
                src_ref=rows(*blk) if src is None else src, dst_ref=rows(*blk),
                send_sem=send_sems.at[k], recv_sem=recv_sems.at[k], device_id=to, device_id_type=MESH)

        mine = pltpu.make_async_copy(x_ref, rows(*me), local_sem)
        mine.start()
        first = [copy(0, me, sibling, src=x_ref)]
        first += [copy(1 + j, me, (*chip, c), src=x_ref) for j, chip in enumerate(chips)]
        for cp in first:
            cp.start()
        passed = [copy(4 + j, (*chip, c), sibling) for j, chip in enumerate(chips)]
        for j, chip in enumerate(chips):
            copy(1 + j, (*chip, c), me).wait_recv()
            passed[j].start()
        copy(0, sibling, me).wait_recv()
        for j, chip in enumerate(chips):
            copy(4 + j, (*chip, 1 - c), me).wait_recv()
        for cp in first + passed:
            cp.wait_send()
        mine.wait()

    return pl.pallas_call(
        body, name=name,
        out_shape=jax.ShapeDtypeStruct((N_DEV * m_per, n), block.dtype),
        in_specs=[pl.BlockSpec(memory_space=pltpu.VMEM)],
        out_specs=pl.BlockSpec(memory_space=pltpu.VMEM),
        scratch_shapes=[pltpu.SemaphoreType.DMA((7,)), pltpu.SemaphoreType.DMA((7,)), pltpu.SemaphoreType.DMA],
    )(block)


EARLY = ("w_in", "w_q_up", "w_kv_up")
LATE = ("w_out", "w_gate", "w_up", "w_down")
BIG = EARLY + LATE
TRANSPOSED_UPDATE = ("w_in", "w_gate", "w_up")
HALF_AXIS = {"w_in": 0, "w_q_up": 0, "w_kv_up": 0, "w_out": 0, "w_gate": 0, "w_up": 0, "w_down": 1}


def _half(ref, h, axis, lead=()):
    trail = ref.shape[len(lead):]
    idx = list(lead) + [slice(None)] * len(trail)
    at = len(trail) - 2 + axis
    n2 = trail[at] // 2
    idx[len(lead) + at] = pl.ds(h * n2, n2)
    return ref.at[tuple(idx)]


def _half_shape(shape, axis):
    shape = list(shape)
    shape[len(shape) - 2 + axis] //= 2
    return tuple(shape)


def _remote(src, dst, send_sems, recv_sems, k, to):
    return pltpu.make_async_remote_copy(src_ref=src, dst_ref=dst, send_sem=send_sems.at[k],
                                        recv_sem=recv_sems.at[k], device_id=to, device_id_type=MESH)


def _gather_weights(names, shards, after):
    n_w = len(shards)
    axes = [HALF_AXIS[n] for n in names]

    def body(*refs):
        w_refs, out_refs, token = refs[:n_w], refs[n_w + 1:2 * n_w + 1], refs[2 * n_w + 1]
        send_sems, recv_sems, local_sems = refs[2 * n_w + 2:]
        token[...] = jnp.zeros_like(token)
        x, y, c = _my_place()
        sibling = (x, y, 1 - c)
        chips = [(1 - x, y), (x, 1 - y), (1 - x, 1 - y)]
        me = 2 * x + y
        mine =[pltpu.make_async_copy(w, o.at[me], local_sems.at[i]) for i, (w, o) in enumerate(zip(w_refs, out_refs))]
        for cp in mine:
            cp.start()
        first = [_remote(_half(w_refs[i], c, axes[i]), _half(out_refs[i], c, axes[i], (me,)),
                         send_sems, recv_sems, 6 * i + j, (*chip, c))
                 for i in range(n_w) for j, chip in enumerate(chips)]
        for cp in first:
            cp.start()
        passed = []
        for j, (cx, cy) in enumerate(chips):
            for i in range(n_w):
                blk = _half(out_refs[i], c, axes[i], (2 * cx + cy,))
                _remote(blk, blk, send_sems, recv_sems, 6 * i + j, (cx, cy, c)).wait_recv()
                cp = _remote(blk, blk, send_sems, recv_sems, 6 * i + 3 + j, sibling)
                cp.start()
                passed.append(cp)
        for j, (cx, cy) in enumerate(chips):
            for i in range(n_w):
                blk = _half(out_refs[i], 1 - c, axes[i], (2 * cx + cy,))
                _remote(blk, blk, send_sems, recv_sems, 6 * i + 3 + j, sibling).wait_recv()
        for cp in first + passed:
            cp.wait_send()
        for cp in mine:
            cp.wait()

    outs = pl.pallas_call(
        body, name="gather_weights",
        out_shape=[jax.ShapeDtypeStruct((N_CHIPS,) + s.shape, s.dtype) for s in shards]
        + [jax.ShapeDtypeStruct((8, LANES), F32)],
        in_specs=[ANY] * (n_w + 1), out_specs=[ANY] * n_w + [pl.BlockSpec(memory_space=pltpu.VMEM)],
        scratch_shapes=[pltpu.SemaphoreType.DMA((6 * n_w,)), pltpu.SemaphoreType.DMA((6 * n_w,)),
                        pltpu.SemaphoreType.DMA((n_w,))],
    )(*shards, after)
    return outs[:n_w], outs[n_w]


def _pair_exchange(names, grads, call_name):
    n_w = len(grads)
    axes = [HALF_AXIS[n] for n in names]

    def body(*refs):
        g_refs, t_refs = refs[:n_w], refs[n_w:2 * n_w]
        send_sems, recv_sems = refs[2 * n_w:]
        x, y, c = _my_place()
        sends = [_remote(_half(g_refs[i], 1 - c, axes[i]), t_refs[i], send_sems, recv_sems, i, (x, y, 1 - c))
                 for i in range(n_w)]
        for cp in sends:
            cp.start()
        for cp in sends:
            cp.wait_recv()
        for cp in sends:
            cp.wait_send()

    return pl.pallas_call(
        body, name=call_name,
        out_shape=[jax.ShapeDtypeStruct(_half_shape(g.shape, a), g.dtype) for g, a in zip(grads, axes)],
        in_specs=[ANY] * n_w, out_specs=[ANY] * n_w,
        scratch_shapes=[pltpu.SemaphoreType.DMA((n_w,)), pltpu.SemaphoreType.DMA((n_w,))],
    )(*grads)


def _chip_scatter(pair_sums):
    n_w = len(pair_sums)

    def body(*refs):
        s_refs, p_refs = refs[:n_w], refs[n_w:2 * n_w]
        send_sems, recv_sems = refs[2 * n_w:]
        x, y, c = _my_place()
        chips = [(1 - x, y), (x, 1 - y), (1 - x, 1 - y)]
        sends = [_remote(s_refs[i].at[2 * cx + cy], p_refs[i].at[j], send_sems, recv_sems, 3 * i + j, (cx, cy, c))
                 for i in range(n_w) for j, (cx, cy) in enumerate(chips)]
        for cp in sends:
            cp.start()
        for cp in sends:
            cp.wait_recv()
        for cp in sends:
            cp.wait_send()

    return pl.pallas_call(
        body, name="grad_chip_scatter",
        out_shape=[jax.ShapeDtypeStruct((N_CHIPS - 1,) + s.shape[1:], s.dtype) for s in pair_sums],
        in_specs=[ANY] * n_w, out_specs=[ANY] * n_w,
        scratch_shapes=[pltpu.SemaphoreType.DMA((3 * n_w,)), pltpu.SemaphoreType.DMA((3 * n_w,))],
    )(*pair_sums)


def _sibling_join(halves, name, after):
    n_w = len(halves)

    def body(*refs):
        s_refs, j_refs = refs[:n_w], refs[n_w + 1:2 * n_w + 1]
        send_sems, recv_sems = refs[2 * n_w + 1:]
        x, y, c = _my_place()
        sends = [_remote(s_refs[i], j_refs[i], send_sems, recv_sems, i, (x, y, 1 - c)) for i in range(n_w)]
        for cp in sends:
            cp.start()
        for cp in sends:
            cp.wait_recv()
        for cp in sends:
            cp.wait_send()

    return pl.pallas_call(
        body, name=name,
        out_shape=[jax.ShapeDtypeStruct(s.shape, s.dtype) for s in halves],
        in_specs=[ANY] * (n_w + 1), out_specs=[ANY] * n_w,
        scratch_shapes=[pltpu.SemaphoreType.DMA((n_w,)), pltpu.SemaphoreType.DMA((n_w,))],
    )(*halves, after)


HBM_SPEC = pl.BlockSpec(memory_space=pltpu.HBM)
SEM_SPEC = pl.BlockSpec(memory_space=pltpu.SEMAPHORE)
DATAFLOW = pltpu.SideEffectType.DATAFLOW_SIDE_EFFECTING


def _in_hbm(a):
    return pltpu.with_memory_space_constraint(a, pltpu.HBM)


def _exchange_start(name, srcs, lands, plan, n_copies, after, thru):
    n = len(srcs)

    def body(*refs):
        src_refs, land_refs = refs[:n], refs[n:2 * n]
        send_sems, recv_sems = refs[2 * n + 2], refs[2 * n + 3]
        for k, (src, dst, to) in enumerate(plan(src_refs, land_refs)):
            _remote(src, dst, send_sems, recv_sems, k, to).start()

    outs = pl.pallas_call(
        body, name=name,
        out_shape=(pltpu.SemaphoreType.DMA((n_copies,)), pltpu.SemaphoreType.DMA((n_copies,)),
                   *[pltpu.HBM(a.shape, a.dtype) for a in list(srcs) + list(lands) + [thru]]),
        in_specs=[HBM_SPEC] * (2 * n + 1) + [ANY],
        out_specs=(SEM_SPEC, SEM_SPEC, *[HBM_SPEC] * (2 * n + 1)),
        input_output_aliases={i: 2 + i for i in range(2 * n + 1)},
        compiler_params=pltpu.CompilerParams(has_side_effects=DATAFLOW),
    )(*[_in_hbm(a) for a in list(srcs) + list(lands) + [thru]], after)
    return outs[0], outs[1], outs[2:2 + n], outs[2 + n:2 + 2 * n], outs[2 + 2 * n]


def _exchange_wait(name, started, plan, after):
    send_sems, recv_sems, srcs, lands, _ = started
    n = len(srcs)

    def body(*refs):
        src_refs, land_refs = refs[:n], refs[n:2 * n]
        s_sems, r_sems = refs[2 * n], refs[2 * n + 1]
        for k, (src, dst, to) in enumerate(plan(src_refs, land_refs)):
            cp = _remote(src, dst, s_sems, r_sems, k, to)
            cp.wait_send()
            cp.wait_recv()

    outs = pl.pallas_call(
        body, name=name,
        out_shape=tuple(pltpu.HBM(a.shape, a.dtype) for a in list(srcs) + list(lands)),
        in_specs=[HBM_SPEC] * (2 * n) + [SEM_SPEC, SEM_SPEC, ANY],
        out_specs=tuple([HBM_SPEC] * (2 * n)),
        input_output_aliases={i: i for i in range(2 * n)},
        compiler_params=pltpu.CompilerParams(has_side_effects=DATAFLOW),
    )(*srcs, *lands, send_sems, recv_sems, after)
    return outs[:n], outs[n:]


def _late_gather_plan(src_refs, land_refs):
    x, y, c = _my_place()
    chips = [(1 - x, y), (x, 1 - y), (1 - x, 1 - y)]
    return [(src, land.at[2 * x + y], (cx, cy, c)) for src, land in zip(src_refs, land_refs) for cx, cy in chips]


def _late_scatter_plan(src_refs, land_refs):
    x, y, c = _my_place()
    chips = [(1 - x, y), (x, 1 - y), (1 - x, 1 - y)]
    return [(src.at[2 * cx + cy], land.at[j], (cx, cy, c))
            for src, land in zip(src_refs, land_refs) for j, (cx, cy) in enumerate(chips)]


def _row_tile(rows, mult=16, limit=ROW_TILE):
    return max(d for d in range(mult, limit + 1, mult) if rows % d == 0)


def _pair_sum(place, g, theirs, axis, name):
    nj, rr, cc = theirs.shape
    tr = _row_tile(rr, limit=1024)
    nb = rr // tr
    if axis == 0:
        g_map = lambda j, i, pr: (j, pr[0] * nb + i, 0)
    else:
        g_map = lambda j, i, pr: (j, i, pr[0])

    def body(pr, g_ref, t_ref, o_ref):
        o_ref[...] = (g_ref[...].astype(F32) + t_ref[...].astype(F32)).astype(BF16)

    spec = pl.BlockSpec((None, tr, cc), lambda j, i, pr: (j, i, 0))
    return pl.pallas_call(
        body, name=name,
        grid_spec=pltpu.PrefetchScalarGridSpec(
            num_scalar_prefetch=1, grid=(nj, nb),
            in_specs=[pl.BlockSpec((None, tr, cc), g_map), spec], out_specs=spec),
        out_shape=jax.ShapeDtypeStruct(theirs.shape, BF16))(place, g, theirs)


def _chip_sum(place, pair_sums, parts, name, transposed):
    _, rr, cc = parts.shape
    tr = _row_tile(rr, LANES) if transposed else _row_tile(rr, limit=1024)

    def body(pr, h_ref, p_ref, o_ref):
        acc = p_ref[0].astype(F32)
        for j in range(1, N_CHIPS - 1):
            acc = acc + p_ref[j].astype(F32)
        acc = acc + h_ref[...].astype(F32)
        o_ref[...] = (acc.T if transposed else acc).astype(BF16)

    out_spec = pl.BlockSpec((cc, tr), lambda i, pr: (0, i)) if transposed else pl.BlockSpec((tr, cc), lambda i, pr: (i, 0))
    return pl.pallas_call(
        body, name=name,
        grid_spec=pltpu.PrefetchScalarGridSpec(
            num_scalar_prefetch=1, grid=(rr // tr,),
            in_specs=[pl.BlockSpec((None, tr, cc), lambda i, pr: (pr[1], i, 0)),
                      pl.BlockSpec((N_CHIPS - 1, tr, cc), lambda i, pr: (0, i, 0))],
            out_specs=out_spec),
        out_shape=jax.ShapeDtypeStruct((cc, rr) if transposed else (rr, cc), BF16))(place, pair_sums, parts)


def _silu(v):
    return v / (1.0 + jnp.exp(-v))


def _ada_fwd(c_all, w_shard, b_shard):
    def body(c_ref, w_ref, b_ref, o_ref):
        o_ref[...] = jnp.dot(_silu(c_ref[...]), w_ref[...], precision=lax.Precision.HIGHEST,
                             preferred_element_type=F32) + b_ref[...]

    return pl.pallas_call(body, name="ada_fwd", out_shape=jax.ShapeDtypeStruct((c_all.shape[0], w_shard.shape[1]), F32),
                          compiler_params=pltpu.CompilerParams(vmem_limit_bytes=MM_VMEM_LIMIT))(c_all, w_shard, b_shard)


def _ada_bwd(c_all, dmod_cols):
    def body(c_ref, d_ref, o_ref):
        o_ref[...] = lax.dot_general(_silu(c_ref[...]), d_ref[...], (((0,), (0,)), ((), ())),
                                     precision=lax.Precision.HIGHEST, preferred_element_type=F32)

    return pl.pallas_call(body, name="ada_bwd", out_shape=jax.ShapeDtypeStruct((c_all.shape[1], dmod_cols.shape[1]), F32),
                          compiler_params=pltpu.CompilerParams(vmem_limit_bytes=MM_VMEM_LIMIT))(c_all, dmod_cols)


def _adamw_math(w, g, m, v):
    m = ADAM_B1 * m + (1.0 - ADAM_B1) * g
    v = ADAM_B2 * v + (1.0 - ADAM_B2) * (g * g)
    m_hat = m / (1.0 - ADAM_B1 ** ADAM_STEP)
    v_hat = v / (1.0 - ADAM_B2 ** ADAM_STEP)
    delta = -ADAM_LR * (m_hat / (jnp.sqrt(v_hat) + ADAM_EPS) + ADAM_WD * w)
    return delta, m, v


def _adamw(w, g, m, v, name):
    r, ccols = w.shape
    tr = max(d for d in range(8, ROW_TILE + 1, 8) if r % d == 0)
    spec = pl.BlockSpec((tr, ccols), lambda i: (i, 0))

    def body(w_ref, g_ref, m_ref, v_ref, d_ref, nm_ref, nv_ref):
        d_ref[...], nm_ref[...], nv_ref[...] = _adamw_math(w_ref[...], g_ref[...], m_ref[...], v_ref[...])

    return pl.pallas_call(body, name=name, grid=(r // tr,), in_specs=[spec] * 4, out_specs=[spec] * 3,
                          out_shape=[jax.ShapeDtypeStruct(w.shape, F32)] * 3,
                          compiler_params=pltpu.CompilerParams(vmem_limit_bytes=MM_VMEM_LIMIT))(w, g, m, v)


def _adamw_small(w, g_all, m, v):
    def body(w_ref, g_ref, m_ref, v_ref, gs_ref, d_ref, nm_ref, nv_ref):
        g = g_ref[0]
        for d in range(1, N_DEV):
            g = g + g_ref[d]
        gs_ref[...] = g
        d_ref[...], nm_ref[...], nv_ref[...] = _adamw_math(w_ref[...], g, m_ref[...], v_ref[...])

    return pl.pallas_call(body, name="adamw_small", out_shape=[jax.ShapeDtypeStruct(w.shape, F32)] * 4)(w, g_all, m, v)


def _adamw_halves(place, w, own, sib, m, v, axis, name, after):
    r, cc = w.shape
    if axis == 0:
        rows, gc = own.shape[0], own.shape[1]
        tr = _row_tile(rows)
        nb = rows // tr
        w_spec = pl.BlockSpec((tr, cc), lambda h, i, pr: (h * nb + i, 0))
        g_spec = pl.BlockSpec((tr, gc), lambda h, i, pr: (i, 0))
    else:
        tr = _row_tile(r)
        nb = r // tr
        gc = own.shape[1]
        w_spec = pl.BlockSpec((tr, gc), lambda h, i, pr: (i, h))
        g_spec = pl.BlockSpec((tr, gc), lambda h, i, pr: (i, 0))
    wc = w_spec.block_shape[1]

    def body(pr, w_ref, o_ref, s_ref, m_ref, v_ref, after_ref, g_ref, d_ref, nm_ref, nv_ref):
        g = jnp.where(pl.program_id(0) == pr[0], o_ref[...], s_ref[...]).astype(F32)[:, :wc]
        g_ref[...] = g
        d_ref[...], nm_ref[...], nv_ref[...] = _adamw_math(w_ref[...], g, m_ref[...], v_ref[...])

    return pl.pallas_call(
        body, name=name,
        grid_spec=pltpu.PrefetchScalarGridSpec(
            num_scalar_prefetch=1, grid=(2, nb),
            in_specs=[w_spec, g_spec, g_spec, w_spec, w_spec, ANY], out_specs=[w_spec] * 4),
        out_shape=[jax.ShapeDtypeStruct(w.shape, F32)] * 4,
        compiler_params=pltpu.CompilerParams(vmem_limit_bytes=MM_VMEM_LIMIT))(place, w, own, sib, m, v, after)


SMALL = ("b_ada", "norm_attn", "norm_ffn", "q_a_norm", "kv_a_norm", "q_norm", "k_nope_norm", "k_rope_norm",
         "out_norm_sb", "out_norm_mla")
WEIGHTS = ("w_ada", "b_ada", "norm_attn", "norm_ffn", "w_in", "q_a_norm", "w_q_up", "kv_a_norm", "w_kv_up",
           "q_norm", "k_nope_norm", "k_rope_norm", "out_norm_sb", "out_norm_mla", "w_out", "w_gate", "w_up",
           "w_down")


def kernel(x, c, positions, w_ada, b_ada, norm_attn, norm_ffn, w_in, q_a_norm, w_q_up, kv_a_norm, w_kv_up, q_norm, k_nope_norm, k_rope_norm, out_norm_sb, out_norm_mla, w_out, w_gate, w_up, w_down, loss_target, m_w_ada, m_b_ada, m_norm_attn, m_norm_ffn, m_w_in, m_q_a_norm, m_w_q_up, m_kv_a_norm, m_w_kv_up, m_q_norm, m_k_nope_norm, m_k_rope_norm, m_out_norm_sb, m_out_norm_mla, m_w_out, m_w_gate, m_w_up, m_w_down, v_w_ada, v_b_ada, v_norm_attn, v_norm_ffn, v_w_in, v_q_a_norm, v_w_q_up, v_kv_a_norm, v_w_kv_up, v_q_norm, v_k_nope_norm, v_k_rope_norm, v_out_norm_sb, v_out_norm_mla, v_w_out, v_w_gate, v_w_up, v_w_down):
    local = dict(locals())
    w = {n: local[n][0] for n in WEIGHTS}
    m = {n: local["m_" + n][0] for n in WEIGHTS}
    v = {n: local["v_" + n][0] for n in WEIGHTS}
    small = {n: w[n].reshape(1, -1) for n in SMALL}
    ix, iy, ic = _my_place()
    chip = 2 * ix + iy
    dev = 2 * chip + ic
    xs, target = x[0], loss_target[0]
    seq = xs.shape[0]

    c_all = _all_gather_small(c.reshape(8, LANES), "gather_c").reshape(N_DEV, D_MODEL)
    ada_cols = w["w_ada"].shape[1]
    b_cols = lax.dynamic_slice_in_dim(small["b_ada"], chip * ada_cols, ada_cols, axis=1)
    mod_cols = _ada_fwd(c_all, w["w_ada"], b_cols)
    mod_all = _all_gather_small(mod_cols, "gather_mod").reshape(N_CHIPS, 2, N_DEV, ada_cols)
    mod = lax.dynamic_index_in_dim(mod_all[:, 0], dev, axis=1, keepdims=False).reshape(1, N_MOD * D_MODEL)

    ff_pad = FF_SHARD_PAD - FF_SHARD
    pads = {"w_gate": ((0, 0), (0, ff_pad)), "w_up": ((0, 0), (0, ff_pad)), "w_down": ((0, ff_pad), (0, 0))}
    shards = {n: jnp.pad(w[n].astype(BF16), pads[n]) if n in pads else w[n].astype(BF16) for n in BIG}
    early, early_done = _gather_weights(EARLY, [shards[n] for n in EARLY], mod)
    gathered = dict(zip(EARLY, early))
    lands = [lax.dynamic_update_index_in_dim(lax.empty((N_CHIPS,) + shards[n].shape, BF16), shards[n], chip, 0)
             for n in LATE]
    late_gather = _exchange_start("gather_late_start", [shards[n] for n in LATE], lands, _late_gather_plan,
                                  3 * len(LATE), early_done, mod)
    mod = late_gather[4]

    half = MLA_ROPE // 2
    freqs = 1.0 / (ROPE_THETA ** (np.arange(half, dtype=np.float32) / half))
    zeros = np.zeros(LANES - MLA_ROPE, np.float32)
    freqs_row = jnp.asarray(np.concatenate([freqs, freqs, zeros]).astype(np.float32)[None])
    sign_row = jnp.asarray(np.concatenate([-np.ones(half), np.ones(half), zeros]).astype(np.float32)[None])
    cos, sin = _rope_tables(positions.reshape(seq, 1), freqs_row, sign_row)

    place = jnp.stack([ic, chip]).astype(jnp.int32)
    small_params = {n: small[n] for n in SMALL if n != "b_ada"}

    def pair_sums_of(names, grads, call_name):
        theirs = _pair_exchange(names, grads, call_name)
        return [_pair_sum(place, gr, th, HALF_AXIS[n], "grad_pair_sum_" + n) for n, gr, th in zip(names, grads, theirs)]

    p1 = {**{n: gathered[n] for n in EARLY}, **small_params}
    mixed, mixing_vjp = jax.vjp(lambda x_, mod_, p_: _mixing_stage(x_, mod_, p_, cos, sin), xs, mod, p1)
    _, landed = _exchange_wait("gather_late_wait", late_gather, _late_gather_plan, mixed)
    p2 = {**dict(zip(LATE, landed)), **small_params}
    loss_part, ffn_vjp = jax.vjp(lambda x_, mixed_, mod_, p_: _ffn_stage(x_, mixed_, mod_, p_, target), xs, mixed, mod, p2)
    gx2, gmixed, gmod2, gp2 = ffn_vjp(jnp.ones((), F32))
    late_sums = pair_sums_of(LATE, [gp2[n] for n in LATE], "grad_pair_exchange_late")
    late_scatter = _exchange_start(
        "grad_scatter_late_start", late_sums,
        [lax.empty((N_CHIPS - 1,) + s.shape[1:], BF16) for s in late_sums], _late_scatter_plan, 3 * len(LATE),
        gx2, gmixed)
    gx1, gmod1, gp1 = mixing_vjp(late_scatter[4])
    gx = gx1 + gx2
    gmod = gmod1 + gmod2
    gp = {n: gp1[n] + gp2[n] for n in small_params}
    loss = lax.psum(loss_part, ("x", "y", "c"))

    small_names = [n for n in SMALL if n != "b_ada"]
    small_vec = jnp.concatenate([gmod] + [gp[n] for n in small_names], axis=1)
    n_small = small_vec.shape[1]
    small_all = _all_gather_small(small_vec.reshape(8, n_small // 8), "gather_small").reshape(N_DEV, 8, n_small // 8)

    g, delta, new_m, new_v = {}, {}, {}, {}

    def reduce_halves(names, sums, parts, join_name, after):
        own = [_chip_sum(place, ps, pt, "grad_chip_sum_" + n, n in TRANSPOSED_UPDATE) for n, ps, pt in zip(names, sums, parts)]
        return own, _sibling_join(own, join_name, after)

    def update(names, own, sib, after):
        for n, o, s in zip(names, own, sib):
            if n in TRANSPOSED_UPDATE:
                res = _adamw_halves(place, w[n].T, o, s, m[n].T, v[n].T, 1, "adamw_" + n, after)
                g[n], delta[n], new_m[n], new_v[n] = [r.T for r in res]
            else:
                g[n], delta[n], new_m[n], new_v[n] = _adamw_halves(place, w[n], o, s, m[n], v[n], HALF_AXIS[n],
                                                                   "adamw_" + n, after)

    late_sums, late_parts = _exchange_wait("grad_scatter_late_wait", late_scatter, _late_scatter_plan, gx)
    own_late, sib_late = reduce_halves(LATE, late_sums, late_parts, "grad_sibling_join_late", small_all)
    early_sums = pair_sums_of(EARLY, [gp1[n] for n in EARLY], "grad_pair_exchange_early")
    early_scatter = _exchange_start(
        "grad_scatter_early_start", early_sums,
        [lax.empty((N_CHIPS - 1,) + s.shape[1:], BF16) for s in early_sums], _late_scatter_plan, 3 * len(EARLY),
        sib_late[0], small_all)
    small_all = early_scatter[4]
    update(LATE, own_late, sib_late, small_all)

    def pack_small(d):
        return jnp.concatenate([d[n].reshape(1, -1) for n in SMALL], axis=1).reshape(8, n_small // 8)

    gs, ds, ms, vs = _adamw_small(pack_small(w), small_all, pack_small(m), pack_small(v))
    sizes = [w[n].size for n in SMALL]
    offs = np.concatenate([[0], np.cumsum(sizes)])

    def unpack_small(a):
        flat = a.reshape(-1)
        return {n: flat[offs[i]:offs[i + 1]].reshape(w[n].shape) for i, n in enumerate(SMALL)}

    for d, packed in zip((g, delta, new_m, new_v), (gs, ds, ms, vs)):
        d.update(unpack_small(packed))

    dmod_all = small_all.reshape(N_DEV, n_small)[:, :N_MOD * D_MODEL]
    g["w_ada"] = _ada_bwd(c_all, lax.dynamic_slice_in_dim(dmod_all, chip * ada_cols, ada_cols, axis=1))
    delta["w_ada"], new_m["w_ada"], new_v["w_ada"] = _adamw(w["w_ada"], g["w_ada"], m["w_ada"], v["w_ada"], "adamw_w_ada")

    early_sums, early_parts = _exchange_wait("grad_scatter_early_wait", early_scatter, _late_scatter_plan,
                                             delta["w_ada"])
    own_early, sib_early = reduce_halves(EARLY, early_sums, early_parts, "grad_sibling_join_early", delta["w_ada"])
    update(EARLY, own_early, sib_early, sib_early[0])

    def outs(d):
        return [d[n][None] for n in WEIGHTS]

    return (loss, gx[None], *outs(g), *outs(delta), *outs(new_m), *outs(new_v))
```

```python
import numpy as np
import jax
import jax.numpy as jnp
from jax import lax
from jax.experimental import pallas as pl
from jax.experimental.pallas import tpu as pltpu

F32 = jnp.float32
BF16 = jnp.bfloat16
MESH = pl.DeviceIdType.MESH
ANY = pl.BlockSpec(memory_space=pl.ANY)

D_MODEL = 1024
SB_HEADS = 8
SB_HEAD_DIM = 64
SB_WIDTH = 512
MLA_HEADS = 4
MLA_NOPE = 128
MLA_ROPE = 64
MLA_QK = 192
MLA_V = 128
MLA_Q_RANK = 384
MLA_KV_RANK = 256
D_FF = 2816
N_MOD = 6
ROPE_THETA = 10000.0
EPS = 1e-6
LANES = 128

ADAM_LR = 0.001
ADAM_B1 = 0.9
ADAM_B2 = 0.999
ADAM_EPS = 1e-08
ADAM_WD = 0.01
ADAM_STEP = 10

N_CHIPS = 4
N_DEV = 8
ROW_TILE = 256
MM_ROW_TILE = 512
ATT_BLK = 256
MM_VMEM_LIMIT = 56 * 1024 * 1024
FF_SHARD = D_FF // N_CHIPS
FF_SHARD_PAD = 768


def _mm(a, b, mode, name, tm, tn, out_dtype=F32):
    if mode == "nn":
        (m, k), n = a.shape, b.shape[1]
        a_spec = pl.BlockSpec((tm, k), lambda j, i: (i, 0))
        b_spec = pl.BlockSpec((k, tn), lambda j, i: (0, j))
        dims = (((1,), (0,)), ((), ()))
    elif mode == "nt":
        (m, k), n = a.shape, b.shape[0]
        a_spec = pl.BlockSpec((tm, k), lambda j, i: (i, 0))
        b_spec = pl.BlockSpec((tn, k), lambda j, i: (j, 0))
        dims = (((1,), (1,)), ((), ()))
    else:
        (k, m), n = a.shape, b.shape[1]
        a_spec = pl.BlockSpec((k, tm), lambda j, i: (0, i))
        b_spec = pl.BlockSpec((k, tn), lambda j, i: (0, j))
        dims = (((0,), (0,)), ((), ()))
    assert m % tm == 0 and n % tn == 0, (name, m, n, tm, tn)

    def body(a_ref, b_ref, o_ref):
        o_ref[...] = lax.dot_general(a_ref[...].astype(BF16), b_ref[...].astype(BF16), dims,
                                     preferred_element_type=F32).astype(out_dtype)

    return pl.pallas_call(
        body, name=name, grid=(n // tn, m // tm),
        in_specs=[a_spec, b_spec],
        out_specs=pl.BlockSpec((tm, tn), lambda j, i: (i, j)),
        out_shape=jax.ShapeDtypeStruct((m, n), out_dtype),
        compiler_params=pltpu.CompilerParams(dimension_semantics=("arbitrary", "arbitrary"),
                                             vmem_limit_bytes=MM_VMEM_LIMIT),
    )(a, b)


def _make_linear(name, tk_w, tn_w):
    @jax.custom_vjp
    def op(a, w):
        return _mm(a, w, "nn", name + "_fwd", MM_ROW_TILE, w.shape[1])

    def fwd(a, w):
        return op(a, w), (a, w)

    def bwd(res, dy):
        a, w = res
        da = _mm(dy, w, "nt", name + "_dx", MM_ROW_TILE, w.shape[0])
        dw = _mm(a, dy, "tn", name + "_dw", tk_w, tn_w, out_dtype=BF16)
        return da, dw

    op.defvjp(fwd, bwd)
    return op


def _make_linear_split(name, widths, tk_w):
    starts = [sum(widths[:g]) for g in range(len(widths))]

    def call_fwd(a, w):
        t, k = a.shape
        n = w.shape[1]

        def body(a_ref, w_ref, *o_refs):
            y = jnp.dot(a_ref[...].astype(BF16), w_ref[...], preferred_element_type=F32)
            for o_ref, s0, wd in zip(o_refs, starts, widths):
                o_ref[...] = y[:, s0:s0 + wd]

        return pl.pallas_call(
            body, name=name + "_fwd", grid=(t // MM_ROW_TILE,),
            in_specs=[pl.BlockSpec((MM_ROW_TILE, k), lambda i: (i, 0)), pl.BlockSpec((k, n), lambda i: (0, 0))],
            out_specs=[pl.BlockSpec((MM_ROW_TILE, wd), lambda i: (i, 0)) for wd in widths],
            out_shape=[jax.ShapeDtypeStruct((t, wd), F32) for wd in widths],
            compiler_params=pltpu.CompilerParams(dimension_semantics=("arbitrary",), vmem_limit_bytes=MM_VMEM_LIMIT),
        )(a, w)

    def call_dx(dys, w):
        t = dys[0].shape[0]
        k, n = w.shape

        def body(*refs):
            dy_refs, w_ref, o_ref = refs[:-2], refs[-2], refs[-1]
            acc = jnp.zeros((MM_ROW_TILE, k), F32)
            for dy_ref, s0, wd in zip(dy_refs, starts, widths):
                acc = acc + _nt(dy_ref[...].astype(BF16), w_ref[:, s0:s0 + wd])
            o_ref[...] = acc

        return pl.pallas_call(
            body, name=name + "_dx", grid=(t // MM_ROW_TILE,),
            in_specs=[pl.BlockSpec((MM_ROW_TILE, wd), lambda i: (i, 0)) for wd in widths]
            + [pl.BlockSpec((k, n), lambda i: (0, 0))],
            out_specs=pl.BlockSpec((MM_ROW_TILE, k), lambda i: (i, 0)),
            out_shape=jax.ShapeDtypeStruct((t, k), F32),
            compiler_params=pltpu.CompilerParams(dimension_semantics=("arbitrary",), vmem_limit_bytes=MM_VMEM_LIMIT),
        )(*dys, w)

    def call_dw(a, dys, w):
        t, k = a.shape
        n = w.shape[1]

        def body(a_ref, *refs):
            dy_refs, o_ref = refs[:-1], refs[-1]
            ab = a_ref[...].astype(BF16)
            for dy_ref, s0, wd in zip(dy_refs, starts, widths):
                o_ref[:, s0:s0 + wd] = _tn(ab, dy_ref[...].astype(BF16)).astype(BF16)
            if starts[-1] + widths[-1] < n:
                o_ref[:, starts[-1] + widths[-1]:] = jnp.zeros((tk_w, n - starts[-1] - widths[-1]), BF16)

        return pl.pallas_call(
            body, name=name + "_dw", grid=(k // tk_w,),
            in_specs=[pl.BlockSpec((t, tk_w), lambda i: (0, i))]
            + [pl.BlockSpec((t, wd), lambda i: (0, 0)) for wd in widths],
            out_specs=pl.BlockSpec((tk_w, n), lambda i: (i, 0)),
            out_shape=jax.ShapeDtypeStruct((k, n), BF16),
            compiler_params=pltpu.CompilerParams(dimension_semantics=("arbitrary",), vmem_limit_bytes=MM_VMEM_LIMIT),
        )(a, *dys)

    @jax.custom_vjp
    def op(a, w):
        return tuple(call_fwd(a, w))

    def fwd(a, w):
        return op(a, w), (a, w)

    def bwd(res, dys):
        a, w = res
        return call_dx(dys, w), call_dw(a, dys, w)

    op.defvjp(fwd, bwd)
    return op


def _row_spec(arr, tb):
    return pl.BlockSpec((tb, arr.shape[1]), lambda i: (i, 0))


def _full_spec(arr):
    return pl.BlockSpec(arr.shape, lambda i: (0, 0))


def _make_rowwise(name, f, n_rows, n_params, out_cols, diff_rows, out_dtypes=None, grad_dtypes=None):
    n_out = len(out_cols)
    out_dtypes = out_dtypes or [F32] * n_out
    grad_dtypes = grad_dtypes or [F32] * sum(diff_rows)

    def call_fwd(rows, params):
        t = rows[0].shape[0]

        def body(*refs):
            ins = [r[...] for r in refs[:n_rows + n_params]]
            outs = f(*ins)
            for o_ref, o in zip(refs[n_rows + n_params:], outs):
                o_ref[...] = o.astype(o_ref.dtype)

        return pl.pallas_call(
            body, name=name + "_fwd", grid=(t // ROW_TILE,),
            in_specs=[_row_spec(a, ROW_TILE) for a in rows] + [_full_spec(p) for p in params],
            out_specs=[pl.BlockSpec((ROW_TILE, n), lambda i: (i, 0)) for n in out_cols],
            out_shape=[jax.ShapeDtypeStruct((t, n), dt) for n, dt in zip(out_cols, out_dtypes)],
            compiler_params=pltpu.CompilerParams(dimension_semantics=("arbitrary",),
                                                 vmem_limit_bytes=MM_VMEM_LIMIT),
        )(*rows, *params)

    def call_bwd(rows, params, cts):
        t = rows[0].shape[0]
        d_rows = [a for a, d in zip(rows, diff_rows) if d]
        n_in = n_rows + n_params + n_out

        def body(*refs):
            ins = [r[...] for r in refs[:n_rows + n_params]]
            ct = tuple(r[...].astype(F32) for r in refs[n_rows + n_params:n_in])
            _, vjp = jax.vjp(f, *ins)
            grads = vjp(ct)
            out_refs = refs[n_in:]
            g_rows = [g for g, d in zip(grads[:n_rows], diff_rows) if d]
            for o_ref, g in zip(out_refs[:len(g_rows)], g_rows):
                o_ref[...] = g.astype(o_ref.dtype)
            p_refs = out_refs[len(g_rows):]

            if p_refs:
                @pl.when(pl.program_id(0) == 0)
                def _():
                    for p_ref in p_refs:
                        p_ref[...] = jnp.zeros_like(p_ref)

                for p_ref, g in zip(p_refs, grads[n_rows:]):
                    p_ref[...] += g

        return pl.pallas_call(
            body, name=name + "_bwd", grid=(t // ROW_TILE,),
            in_specs=[_row_spec(a, ROW_TILE) for a in rows] + [_full_spec(p) for p in params]
            + [_row_spec(c, ROW_TILE) for c in cts],
            out_specs=[_row_spec(a, ROW_TILE) for a in d_rows] + [_full_spec(p) for p in params],
            out_shape=[jax.ShapeDtypeStruct(a.shape, dt) for a, dt in zip(d_rows, grad_dtypes)]
            + [jax.ShapeDtypeStruct(p.shape, F32) for p in params],
            compiler_params=pltpu.CompilerParams(dimension_semantics=("arbitrary",),
                                                 vmem_limit_bytes=MM_VMEM_LIMIT),
        )(*rows, *params, *cts)

    @jax.custom_vjp
    def op(*args):
        return tuple(call_fwd(args[:n_rows], args[n_rows:]))

    def fwd(*args):
        return op(*args), args

    def bwd(args, cts):
        rows, params = args[:n_rows], args[n_rows:]
        outs = call_bwd(rows, params, cts)
        it = iter(outs)
        g_rows = [next(it) if d else jnp.zeros_like(a) for a, d in zip(rows, diff_rows)]
        return tuple(g_rows) + tuple(it)

    op.defvjp(fwd, bwd)
    return op


def _rms(x, g, n):
    return x * lax.rsqrt(jnp.sum(x * x, axis=-1, keepdims=True) * (1.0 / n) + EPS) * g


def _f_pre_attn(x, g, scale, shift):
    return (_rms(x, g, D_MODEL) * (1.0 + scale) + shift,)


def _f_mla_a(cq, ckv, gq, gkv):
    return _rms(cq, gq, MLA_Q_RANK), _rms(ckv, gkv, MLA_KV_RANK)


@jax.custom_vjp
def _split_lanes(x):
    return tuple(x[:, i * LANES:(i + 1) * LANES] for i in range(x.shape[1] // LANES))


def _split_lanes_fwd(x):
    return _split_lanes(x), None


def _split_lanes_bwd(_, cts):
    return (jnp.concatenate(cts, axis=1),)


_split_lanes.defvjp(_split_lanes_fwd, _split_lanes_bwd)


def _f_mla_b(qall, kn_all, kr, kr_sw, cos, sin, gqn, gqr, gqr_sw, gkn, gkr, gkr_sw):
    q = _split_lanes(qall)
    kn = _split_lanes(kn_all)
    qn_o, qr_o, kn_o = [], [], []
    for h in range(MLA_HEADS):
        qn, qr, qs = q[h], q[MLA_HEADS + h], q[2 * MLA_HEADS + h]
        ss = jnp.sum(qn * qn, axis=-1, keepdims=True) + jnp.sum(qr * qr, axis=-1, keepdims=True)
        rs = lax.rsqrt(ss * (1.0 / MLA_QK) + EPS)
        qn_o.append(qn * rs * gqn)
        qr_o.append((qr * rs * gqr) * cos + (qs * rs * gqr_sw) * sin)
        kn_o.append(_rms(kn[h], gkn, MLA_NOPE))
    rs = lax.rsqrt(jnp.sum(kr * kr, axis=-1, keepdims=True) * (1.0 / MLA_ROPE) + EPS)
    kr_o = (kr * rs * gkr) * cos + (kr_sw * rs * gkr_sw) * sin
    return (jnp.concatenate(qn_o, axis=1), jnp.concatenate(qr_o, axis=1), jnp.concatenate(kn_o, axis=1), kr_o)


def _f_post_attn(o_sb, o_mla, g_sb, g_mla):
    return (jnp.concatenate([_rms(o_sb, g_sb, SB_WIDTH), _rms(o_mla, g_mla, SB_WIDTH)], axis=1),)


def _f_pre_ffn(x, attn, gate, g, scale, shift):
    x2 = x + gate * attn
    return x2, _rms(x2, g, D_MODEL) * (1.0 + scale) + shift


def _f_swiglu(gt, up):
    return (gt / (1.0 + jnp.exp(-gt)) * up,)


def _f_loss(x2, ffn, target, gate):
    err = x2 + gate * ffn - target
    return (jnp.sum(err * err, axis=-1, keepdims=True) * (1.0 / D_MODEL),)


def _rope_tables(pos_col, freqs, sign):
    t = pos_col.shape[0]

    def body(p_ref, f_ref, s_ref, cos_ref, sin_ref):
        ang = p_ref[...].astype(F32) * f_ref[...]
        live = jnp.abs(s_ref[...])
        cos_ref[...] = jnp.cos(ang) * live
        sin_ref[...] = jnp.sin(ang) * s_ref[...]

    return pl.pallas_call(
        body, name="rope_tables", grid=(t // ROW_TILE,),
        in_specs=[pl.BlockSpec((ROW_TILE, 1), lambda i: (i, 0)), _full_spec(freqs), _full_spec(sign)],
        out_specs=[pl.BlockSpec((ROW_TILE, LANES), lambda i: (i, 0))] * 2,
        out_shape=[jax.ShapeDtypeStruct((t, LANES), F32)] * 2,
    )(pos_col, freqs, sign)


def _hi_lo_dot(x, tri):
    hi = x.astype(BF16)
    lo = (x - hi.astype(F32)).astype(BF16)
    return (jnp.dot(hi, tri, preferred_element_type=F32) + jnp.dot(lo, tri, preferred_element_type=F32))


def _tri(cmp):
    r = lax.broadcasted_iota(jnp.int32, (ATT_BLK, ATT_BLK), 0)
    c = lax.broadcasted_iota(jnp.int32, (ATT_BLK, ATT_BLK), 1)
    return cmp(r, c).astype(BF16)


def _nt(a, b):
    return lax.dot_general(a, b, (((1,), (1,)), ((), ())), preferred_element_type=F32)


def _tn(a, b):
    return lax.dot_general(a, b, (((0,), (0,)), ((), ())), preferred_element_type=F32)


def _sb_logs(z):
    lb = jnp.minimum(z, 0.0) - jnp.log(1.0 + jnp.exp(-jnp.abs(z)))
    return lb, lb - z


def _sb_fwd(q, k, v):
    t = q.shape[0]
    nq = t // ATT_BLK
    scale = SB_HEAD_DIM ** -0.5

    def body(q_ref, k_ref, v_ref, o_ref, tot_ref):
        qi = pl.program_id(1)
        lane = lax.broadcasted_iota(jnp.int32, (ATT_BLK, LANES), 1)
        tri = _tri(lambda r, c: r > c)
        qv = q_ref[...] * scale
        heads = [(lane // SB_HEAD_DIM) == hh for hh in range(2)]
        qms = [jnp.where(mine, qv, 0.0).astype(BF16) for mine in heads]

        def blocks(kbs, carry, diagonal):
            acc = carry[0]
            nb = len(kbs)
            chains = [(b, hh) for b in range(nb) for hh in range(2)]
            offs = [pl.multiple_of(kb * ATT_BLK, ATT_BLK) for kb in kbs]
            kks = [k_ref[pl.ds(off, ATT_BLK), :].astype(BF16) for off in offs]
            v_blks = [v_ref[pl.ds(off, ATT_BLK), :] for off in offs]
            if any(diagonal):
                valid = (lax.broadcasted_iota(jnp.int32, (ATT_BLK, ATT_BLK), 1)
                         < lax.broadcasted_iota(jnp.int32, (ATT_BLK, ATT_BLK), 0))
            zs = {ch: _nt(qms[ch[1]], kks[ch[0]]) for ch in chains}
            vvs = {(b, hh): jnp.where(heads[hh], v_blks[b], 0.0).astype(BF16) for b, hh in chains}
            logs = {ch: _sb_logs(zs[ch]) for ch in chains}
            l1ms = {ch: jnp.where(valid, logs[ch][1], 0.0) if diagonal[ch[0]] else logs[ch][1] for ch in chains}
            run = {(0, hh): carry[1 + hh] for hh in range(2)}
            for b, hh in chains:
                run[(b + 1, hh)] = run[(b, hh)] + jnp.sum(l1ms[(b, hh)], axis=-1, keepdims=True)
            afters = {ch: _hi_lo_dot(l1ms[ch], tri) for ch in chains}
            ws = {ch: jnp.exp(logs[ch][0] + (afters[ch] + run[ch])) for ch in chains}
            ws = {ch: jnp.where(valid, ws[ch], 0.0) if diagonal[ch[0]] else ws[ch] for ch in chains}
            for ch in chains:
                acc = acc + jnp.dot(ws[ch].astype(BF16), vvs[ch], preferred_element_type=F32)
            return (acc, run[(nb, 0)], run[(nb, 1)])

        zero = jnp.zeros((ATT_BLK, 1), F32)
        init = (jnp.zeros((ATT_BLK, LANES), F32), zero, zero)
        carry = lax.cond(qi % 2 == 1, lambda cr: blocks([qi, qi - 1], cr, (True, False)),
                         lambda cr: blocks([qi], cr, (True,)), init)
        top = qi - 1 - qi % 2
        carry = lax.fori_loop(0, qi // 2, lambda pr, cr: blocks([top - 2 * pr, top - 1 - 2 * pr], cr, (False, False)),
                              carry)
        o_ref[...] = carry[0]
        for hh in range(2):
            tot_ref[:, hh * LANES:(hh + 1) * LANES] = jnp.broadcast_to(carry[1 + hh], (ATT_BLK, LANES))

    return pl.pallas_call(
        body, name="sb_attn_fwd", grid=(SB_HEADS // 2, nq),
        in_specs=[pl.BlockSpec((ATT_BLK, LANES), lambda p, i: (i, p)),
                  pl.BlockSpec((t, LANES), lambda p, i: (0, p)),
                  pl.BlockSpec((t, LANES), lambda p, i: (0, p))],
        out_specs=[pl.BlockSpec((ATT_BLK, LANES), lambda p, i: (i, p)),
                   pl.BlockSpec((ATT_BLK, 2 * LANES), lambda p, i: (i, p))],
        out_shape=[jax.ShapeDtypeStruct((t, SB_WIDTH), F32), jax.ShapeDtypeStruct((t, SB_HEADS * LANES), F32)],
        compiler_params=pltpu.CompilerParams(dimension_semantics=("arbitrary", "arbitrary")),
    )(q, k, v)


def _sb_bwd(q, k, v, tot, do):
    t = q.shape[0]
    nq = t // ATT_BLK
    scale = SB_HEAD_DIM ** -0.5

    def body(q_ref, k_ref, v_ref, tot_ref, do_ref, dq_ref, dk_ref, dv_ref):
        qi = pl.program_id(1)

        @pl.when(qi == 0)
        def _():
            dk_ref[...] = jnp.zeros_like(dk_ref)
            dv_ref[...] = jnp.zeros_like(dv_ref)

        lane = lax.broadcasted_iota(jnp.int32, (ATT_BLK, LANES), 1)
        tri_incl = _tri(lambda r, c: r <= c)
        tri_lt = _tri(lambda r, c: r < c)
        qv = q_ref[...] * scale
        dov = do_ref[...]
        heads = [(lane // SB_HEAD_DIM) == hh for hh in range(2)]
        qms = [jnp.where(mine, qv, 0.0).astype(BF16) for mine in heads]
        doms = [jnp.where(mine, dov, 0.0).astype(BF16) for mine in heads]
        tots = [tot_ref[:, hh * LANES:hh * LANES + 1] for hh in range(2)]

        def blocks(kbs, carry, diagonal):
            dq = carry[0]
            nb = len(kbs)
            chains = [(b, hh) for b in range(nb) for hh in range(2)]
            offs = [pl.multiple_of(kb * ATT_BLK, ATT_BLK) for kb in kbs]
            k_blks = [k_ref[pl.ds(off, ATT_BLK), :] for off in offs]
            vvs = [v_ref[pl.ds(off, ATT_BLK), :].astype(BF16) for off in offs]
            if any(diagonal):
                valid = (lax.broadcasted_iota(jnp.int32, (ATT_BLK, ATT_BLK), 1)
                         < lax.broadcasted_iota(jnp.int32, (ATT_BLK, ATT_BLK), 0))
            kks = {(b, hh): jnp.where(heads[hh], k_blks[b], 0.0).astype(BF16) for b, hh in chains}
            zs = {ch: _nt(qms[ch[1]], kks[ch]) for ch in chains}
            dws = {ch: _nt(doms[ch[1]], vvs[ch[0]]) for ch in chains}
            logs = {ch: _sb_logs(zs[ch]) for ch in chains}
            lbs = {ch: logs[ch][0] for ch in chains}
            l1m_all = {ch: logs[ch][1] for ch in chains}
            l1ms = {ch: jnp.where(valid, l1m_all[ch], 0.0) if diagonal[ch[0]] else l1m_all[ch] for ch in chains}
            pre, c_de = {}, {}
            for hh in range(2):
                pre[(0, hh)], c_de[(0, hh)] = carry[1 + 2 * hh], carry[2 + 2 * hh]
            for b, hh in chains:
                pre[(b + 1, hh)] = pre[(b, hh)] + jnp.sum(l1ms[(b, hh)], axis=-1, keepdims=True)
            prefix = {ch: _hi_lo_dot(l1ms[ch], tri_incl) for ch in chains}
            ws = {ch: jnp.exp(lbs[ch] + (tots[ch[1]] - (prefix[ch] + pre[ch]))) for ch in chains}
            ws = {ch: jnp.where(valid, ws[ch], 0.0) if diagonal[ch[0]] else ws[ch] for ch in chains}
            d_es = {ch: ws[ch] * dws[ch] for ch in chains}
            for b, hh in chains:
                c_de[(b + 1, hh)] = c_de[(b, hh)] + jnp.sum(d_es[(b, hh)], axis=-1, keepdims=True)
            dvs = [_tn(ws[(b, 0)].astype(BF16), doms[0]) + _tn(ws[(b, 1)].astype(BF16), doms[1]) for b in range(nb)]
            dl1ms = {ch: jnp.dot(d_es[ch].astype(BF16), tri_lt, preferred_element_type=F32) + c_de[ch] for ch in chains}
            dzs = {ch: d_es[ch] * jnp.exp(l1m_all[ch]) - dl1ms[ch] * jnp.exp(lbs[ch]) for ch in chains}
            dzs = {ch: jnp.where(valid, dzs[ch], 0.0) if diagonal[ch[0]] else dzs[ch] for ch in chains}
            dzs = {ch: dzs[ch].astype(BF16) for ch in chains}
            for ch in chains:
                dq = dq + jnp.dot(dzs[ch], kks[ch], preferred_element_type=F32)
            for b in range(nb):
                dk_ref[pl.ds(offs[b], ATT_BLK), :] += _tn(dzs[(b, 0)], qms[0]) + _tn(dzs[(b, 1)], qms[1])
                dv_ref[pl.ds(offs[b], ATT_BLK), :] += dvs[b]
            return (dq, pre[(nb, 0)], c_de[(nb, 0)], pre[(nb, 1)], c_de[(nb, 1)])

        zero = jnp.zeros((ATT_BLK, 1), F32)
        carry = lax.fori_loop(0, qi // 2, lambda pr, cr: blocks([2 * pr, 2 * pr + 1], cr, (False, False)),
                              (jnp.zeros((ATT_BLK, LANES), F32), zero, zero, zero, zero))
        carry = lax.cond(qi % 2 == 1, lambda cr: blocks([qi - 1, qi], cr, (False, True)),
                         lambda cr: blocks([qi], cr, (True,)), carry)
        dq_ref[...] = carry[0] * scale

    return pl.pallas_call(
        body, name="sb_attn_bwd", grid=(SB_HEADS // 2, nq),
        in_specs=[pl.BlockSpec((ATT_BLK, LANES), lambda p, i: (i, p)),
                  pl.BlockSpec((t, LANES), lambda p, i: (0, p)),
                  pl.BlockSpec((t, LANES), lambda p, i: (0, p)),
                  pl.BlockSpec((ATT_BLK, 2 * LANES), lambda p, i: (i, p)),
                  pl.BlockSpec((ATT_BLK, LANES), lambda p, i: (i, p))],
        out_specs=[pl.BlockSpec((ATT_BLK, LANES), lambda p, i: (i, p)),
                   pl.BlockSpec((t, LANES), lambda p, i: (0, p)),
                   pl.BlockSpec((t, LANES), lambda p, i: (0, p))],
        out_shape=[jax.ShapeDtypeStruct((t, SB_WIDTH), F32)] * 3,
        compiler_params=pltpu.CompilerParams(dimension_semantics=("arbitrary", "arbitrary")),
    )(q, k, v, tot, do)


@jax.custom_vjp
def _sb_attention(q, k, v):
    return _sb_fwd(q, k, v)[0]


def _sb_attention_fwd(q, k, v):
    o, tot = _sb_fwd(q, k, v)
    return o, (q, k, v, tot)


def _sb_attention_bwd(res, do):
    return tuple(_sb_bwd(*res, do))


_sb_attention.defvjp(_sb_attention_fwd, _sb_attention_bwd)


def _mla_fwd(qn, qr, kn, kr, v):
    t = qn.shape[0]
    nq = t // ATT_BLK
    scale = MLA_QK ** -0.5

    def body(qn_ref, qr_ref, kn_ref, kr_ref, v_ref, o_ref, lse_ref):
        qi = pl.program_id(1)
        lanes = [slice(hh * LANES, (hh + 1) * LANES) for hh in range(2)]
        qnb = [qn_ref[:, sl].astype(BF16) for sl in lanes]
        qrb = [qr_ref[:, sl].astype(BF16) for sl in lanes]

        def blocks(kbs, carry, diagonal):
            nb = len(kbs)
            chains = [(b, hh) for b in range(nb) for hh in range(2)]
            offs = [pl.multiple_of(kb * ATT_BLK, ATT_BLK) for kb in kbs]
            krbs = [kr_ref[pl.ds(off, ATT_BLK), :].astype(BF16) for off in offs]
            accs, ms, ls = [carry[0], carry[3]], [carry[1], carry[4]], [carry[2], carry[5]]
            ss = {(b, hh): (_nt(qnb[hh], kn_ref[pl.ds(offs[b], ATT_BLK), lanes[hh]].astype(BF16))
                            + _nt(qrb[hh], krbs[b])) * scale for b, hh in chains}
            if any(diagonal):
                causal = (lax.broadcasted_iota(jnp.int32, (ATT_BLK, ATT_BLK), 1)
                          <= lax.broadcasted_iota(jnp.int32, (ATT_BLK, ATT_BLK), 0))
                ss = {ch: jnp.where(causal, ss[ch], -jnp.inf) if diagonal[ch[0]] else ss[ch] for ch in chains}
            m_new = list(ms)
            for b, hh in chains:
                m_new[hh] = jnp.maximum(m_new[hh], jnp.max(ss[(b, hh)], axis=-1, keepdims=True))
            ps = {(b, hh): jnp.exp(ss[(b, hh)] - m_new[hh]) for b, hh in chains}
            alphas = [jnp.exp(ms[hh] - m_new[hh]) for hh in range(2)]
            pvs = {(b, hh): jnp.dot(ps[(b, hh)].astype(BF16), v_ref[pl.ds(offs[b], ATT_BLK), lanes[hh]].astype(BF16),
                                    preferred_element_type=F32) for b, hh in chains}
            out = []
            for hh in range(2):
                acc, l = accs[hh] * alphas[hh], ls[hh] * alphas[hh]
                for b in range(nb):
                    acc, l = acc + pvs[(b, hh)], l + jnp.sum(ps[(b, hh)], axis=-1, keepdims=True)
                out += [acc, m_new[hh], l]
            return tuple(out)

        init = (jnp.zeros((ATT_BLK, LANES), F32), jnp.full((ATT_BLK, 1), -jnp.inf, F32), jnp.zeros((ATT_BLK, 1), F32))
        carry = lax.cond(qi % 2 == 1, lambda cr: blocks([qi, qi - 1], cr, (True, False)),
                         lambda cr: blocks([qi], cr, (True,)), init + init)
        carry = lax.fori_loop(0, qi // 2, lambda pr, cr: blocks([2 * pr, 2 * pr + 1], cr, (False, False)), carry)
        for hh in range(2):
            acc, m, l = carry[3 * hh:3 * hh + 3]
            o_ref[:, lanes[hh]] = acc / l
            lse_ref[:, lanes[hh]] = jnp.broadcast_to(m + jnp.log(l), (ATT_BLK, LANES))

    blk = pl.BlockSpec((ATT_BLK, 2 * LANES), lambda p, i: (i, p))
    full = pl.BlockSpec((t, 2 * LANES), lambda p, i: (0, p))
    return pl.pallas_call(
        body, name="mla_attn_fwd", grid=(MLA_HEADS // 2, nq),
        in_specs=[blk, blk, full, pl.BlockSpec((t, LANES), lambda p, i: (0, 0)), full],
        out_specs=[blk, blk],
        out_shape=[jax.ShapeDtypeStruct((t, MLA_HEADS * LANES), F32)] * 2,
        compiler_params=pltpu.CompilerParams(dimension_semantics=("arbitrary", "arbitrary")),
    )(qn, qr, kn, kr, v)


def _mla_bwd(qn, qr, kn, kr, v, o, lse, do):
    t = qn.shape[0]
    nq = t // ATT_BLK
    scale = MLA_QK ** -0.5

    def body(qn_ref, qr_ref, kn_ref, kr_ref, v_ref, o_ref, lse_ref, do_ref,
             dqn_ref, dqr_ref, dkn_ref, dkr_ref, dv_ref):
        pair = pl.program_id(0)
        qi = pl.program_id(1)

        @pl.when(qi == 0)
        def _():
            dkn_ref[...] = jnp.zeros_like(dkn_ref)
            dv_ref[...] = jnp.zeros_like(dv_ref)

        @pl.when((qi == 0) & (pair == 0))
        def _():
            dkr_ref[...] = jnp.zeros_like(dkr_ref)

        lanes = [slice(hh * LANES, (hh + 1) * LANES) for hh in range(2)]
        qnb = [qn_ref[:, sl].astype(BF16) for sl in lanes]
        qrb = [qr_ref[:, sl].astype(BF16) for sl in lanes]
        dob = [do_ref[:, sl].astype(BF16) for sl in lanes]
        delta = [jnp.sum(do_ref[:, sl] * o_ref[:, sl], axis=-1, keepdims=True) for sl in lanes]
        lse_v = [lse_ref[:, hh * LANES:hh * LANES + 1] for hh in range(2)]

        def blocks(kbs, carry, diagonal):
            nb = len(kbs)
            chains = [(b, hh) for b in range(nb) for hh in range(2)]
            offs = [pl.multiple_of(kb * ATT_BLK, ATT_BLK) for kb in kbs]
            krbs = [kr_ref[pl.ds(off, ATT_BLK), :].astype(BF16) for off in offs]
            knb = {(b, hh): kn_ref[pl.ds(offs[b], ATT_BLK), lanes[hh]].astype(BF16) for b, hh in chains}
            vb = {(b, hh): v_ref[pl.ds(offs[b], ATT_BLK), lanes[hh]].astype(BF16) for b, hh in chains}
            ss = {(b, hh): _nt(qnb[hh], knb[(b, hh)]) + _nt(qrb[hh], krbs[b]) for b, hh in chains}
            dps = {(b, hh): _nt(dob[hh], vb[(b, hh)]) for b, hh in chains}
            ps = {(b, hh): jnp.exp(ss[(b, hh)] * scale - lse_v[hh]) for b, hh in chains}
            if any(diagonal):
                causal = (lax.broadcasted_iota(jnp.int32, (ATT_BLK, ATT_BLK), 1)
                          <= lax.broadcasted_iota(jnp.int32, (ATT_BLK, ATT_BLK), 0))
                ps = {ch: jnp.where(causal, ps[ch], 0.0) if diagonal[ch[0]] else ps[ch] for ch in chains}
            dss = {(b, hh): (ps[(b, hh)] * (dps[(b, hh)] - delta[hh]) * scale).astype(BF16) for b, hh in chains}
            for b, hh in chains:
                dv_ref[pl.ds(offs[b], ATT_BLK), lanes[hh]] += _tn(ps[(b, hh)].astype(BF16), dob[hh])
            for b, hh in chains:
                dkn_ref[pl.ds(offs[b], ATT_BLK), lanes[hh]] += _tn(dss[(b, hh)], qnb[hh])
            for b in range(nb):
                dkr_ref[pl.ds(offs[b], ATT_BLK), :] += _tn(dss[(b, 0)], qrb[0]) + _tn(dss[(b, 1)], qrb[1])
            out = list(carry)
            for b, hh in chains:
                out[2 * hh] = out[2 * hh] + jnp.dot(dss[(b, hh)], knb[(b, hh)], preferred_element_type=F32)
                out[2 * hh + 1] = out[2 * hh + 1] + jnp.dot(dss[(b, hh)], krbs[b], preferred_element_type=F32)
            return tuple(out)

        zero = jnp.zeros((ATT_BLK, LANES), F32)
        carry = lax.fori_loop(0, qi // 2, lambda pr, cr: blocks([2 * pr, 2 * pr + 1], cr, (False, False)),
                              (zero, zero, zero, zero))
        carry = lax.cond(qi % 2 == 1, lambda cr: blocks([qi - 1, qi], cr, (False, True)),
                         lambda cr: blocks([qi], cr, (True,)), carry)
        for hh in range(2):
            dqn_ref[:, lanes[hh]] = carry[2 * hh]
            dqr_ref[:, lanes[hh]] = carry[2 * hh + 1]

    blk = pl.BlockSpec((ATT_BLK, 2 * LANES), lambda p, i: (i, p))
    full = pl.BlockSpec((t, 2 * LANES), lambda p, i: (0, p))
    shared = pl.BlockSpec((t, LANES), lambda p, i: (0, 0))
    wide = jax.ShapeDtypeStruct((t, MLA_HEADS * LANES), F32)
    return pl.pallas_call(
        body, name="mla_attn_bwd", grid=(MLA_HEADS // 2, nq),
        in_specs=[blk, blk, full, shared, full, blk, blk, blk],
        out_specs=[blk, blk, full, shared, full],
        out_shape=[wide, wide, wide, jax.ShapeDtypeStruct((t, LANES), F32), wide],
        compiler_params=pltpu.CompilerParams(dimension_semantics=("arbitrary", "arbitrary")),
    )(qn, qr, kn, kr, v, o, lse, do)


@jax.custom_vjp
def _mla_attention(qn, qr, kn, kr, v):
    return _mla_fwd(qn, qr, kn, kr, v)[0]


def _mla_attention_fwd(qn, qr, kn, kr, v):
    o, lse = _mla_fwd(qn, qr, kn, kr, v)
    return o, (qn, qr, kn, kr, v, o, lse)


def _mla_attention_bwd(res, do):
    return tuple(_mla_bwd(*res, do))


_mla_attention.defvjp(_mla_attention_fwd, _mla_attention_bwd)


def _ffn_in(h, wg, wu):
    t, k = h.shape
    n_sh, _, cc = wg.shape

    def body(h_ref, wg_ref, wu_ref, g_ref, u_ref, a_ref):
        hb = h_ref[...].astype(BF16)
        for j in range(n_sh):
            cols = slice(j * cc, (j + 1) * cc)
            g = jnp.dot(hb, wg_ref[j], preferred_element_type=F32)
            u = jnp.dot(hb, wu_ref[j], preferred_element_type=F32)
            g_ref[:, cols] = g
            u_ref[:, cols] = u
            a_ref[:, cols] = _f_swiglu(g, u)[0].astype(BF16)

    w_spec = pl.BlockSpec((n_sh, k, cc), lambda i: (0, 0, 0))
    o_spec = pl.BlockSpec((ROW_TILE, n_sh * cc), lambda i: (i, 0))
    wide = (t, n_sh * cc)
    return pl.pallas_call(
        body, name="ffn_in_fwd", grid=(t // ROW_TILE,),
        in_specs=[pl.BlockSpec((ROW_TILE, k), lambda i: (i, 0)), w_spec, w_spec],
        out_specs=[o_spec, o_spec, o_spec],
        out_shape=[jax.ShapeDtypeStruct(wide, F32), jax.ShapeDtypeStruct(wide, F32), jax.ShapeDtypeStruct(wide, BF16)],
        compiler_params=pltpu.CompilerParams(dimension_semantics=("arbitrary",), vmem_limit_bytes=MM_VMEM_LIMIT),
    )(h, wg, wu)


def _ffn_mid_bwd(dy, wd, g, u):
    t, n = dy.shape
    n_sh, cc, _ = wd.shape

    def body(dy_ref, wd_ref, g_ref, u_ref, dg_ref, du_ref):
        d_act = _nt(dy_ref[...].astype(BF16), wd_ref[...])
        _, vjp = jax.vjp(_f_swiglu, g_ref[...], u_ref[...])
        dg, du = vjp((d_act,))
        dg_ref[...] = dg.astype(BF16)
        du_ref[...] = du.astype(BF16)

    blk = pl.BlockSpec((MM_ROW_TILE, cc), lambda j, i: (i, j))
    wide = jax.ShapeDtypeStruct((t, n_sh * cc), BF16)
    return pl.pallas_call(
        body, name="ffn_mid_bwd", grid=(n_sh, t // MM_ROW_TILE),
        in_specs=[pl.BlockSpec((MM_ROW_TILE, n), lambda j, i: (i, 0)),
                  pl.BlockSpec((None, cc, n), lambda j, i: (j, 0, 0)), blk, blk],
        out_specs=[blk, blk], out_shape=[wide, wide],
        compiler_params=pltpu.CompilerParams(dimension_semantics=("arbitrary", "arbitrary"),
                                             vmem_limit_bytes=MM_VMEM_LIMIT),
    )(dy, wd, g, u)


def _ffn_dh(dg, du, wg, wu):
    t = dg.shape[0]
    n_sh, k, cc = wg.shape

    def body(dg_ref, du_ref, wg_ref, wu_ref, o_ref):
        acc = jnp.zeros((MM_ROW_TILE, k), F32)
        for j in range(n_sh):
            cols = slice(j * cc, (j + 1) * cc)
            acc = acc + _nt(dg_ref[:, cols], wg_ref[j]) + _nt(du_ref[:, cols], wu_ref[j])
        o_ref[...] = acc

    blk = pl.BlockSpec((MM_ROW_TILE, n_sh * cc), lambda i: (i, 0))
    w_spec = pl.BlockSpec((n_sh, k, cc), lambda i: (0, 0, 0))
    return pl.pallas_call(
        body, name="ffn_dh", grid=(t // MM_ROW_TILE,),
        in_specs=[blk, blk, w_spec, w_spec],
        out_specs=pl.BlockSpec((MM_ROW_TILE, k), lambda i: (i, 0)),
        out_shape=jax.ShapeDtypeStruct((t, k), F32),
        compiler_params=pltpu.CompilerParams(dimension_semantics=("arbitrary",), vmem_limit_bytes=MM_VMEM_LIMIT),
    )(dg, du, wg, wu)


def _ffn_dw_in(h, dy, n_sh, name):
    t, k = h.shape
    cc = dy.shape[1] // n_sh
    tk = 512

    def body(h_ref, dy_ref, o_ref):
        o_ref[...] = _tn(h_ref[...].astype(BF16), dy_ref[...]).astype(BF16)

    return pl.pallas_call(
        body, name=name, grid=(n_sh, k // tk),
        in_specs=[pl.BlockSpec((t, tk), lambda j, i: (0, i)), pl.BlockSpec((t, cc), lambda j, i: (0, j))],
        out_specs=pl.BlockSpec((None, tk, cc), lambda j, i: (j, i, 0)),
        out_shape=jax.ShapeDtypeStruct((n_sh, k, cc), BF16),
        compiler_params=pltpu.CompilerParams(dimension_semantics=("arbitrary", "arbitrary"),
                                             vmem_limit_bytes=MM_VMEM_LIMIT),
    )(h, dy)


@jax.custom_vjp
def _ffn_block(h, wg, wu, wd):
    act = _ffn_in(h, wg, wu)[2]
    return _mm(act, wd.reshape(-1, wd.shape[2]), "nn", "ffn_down_fwd", MM_ROW_TILE, wd.shape[2])


def _ffn_block_fwd(h, wg, wu, wd):
    g, u, act = _ffn_in(h, wg, wu)
    y = _mm(act, wd.reshape(-1, wd.shape[2]), "nn", "ffn_down_fwd", MM_ROW_TILE, wd.shape[2])
    return y, (h, wg, wu, wd, g, u, act)


def _ffn_block_bwd(res, dy):
    h, wg, wu, wd, g, u, act = res
    dg, du = _ffn_mid_bwd(dy, wd, g, u)
    dh = _ffn_dh(dg, du, wg, wu)
    n_sh = wg.shape[0]
    dwg = _ffn_dw_in(h, dg, n_sh, "ffn_gate_dw")
    dwu = _ffn_dw_in(h, du, n_sh, "ffn_up_dw")
    dwd = _mm(act, dy, "tn", "ffn_down_dw", 256, wd.shape[2], out_dtype=BF16).reshape(wd.shape)
    return dh, dwg, dwu, dwd


_ffn_block.defvjp(_ffn_block_fwd, _ffn_block_bwd)


def _swap_halves(w):
    half = w.shape[-1] // 2
    return jnp.concatenate([w[..., half:], w[..., :half]], axis=-1)


def _pad_lanes(w):
    return jnp.concatenate([w, jnp.zeros(w.shape[:-1] + (LANES - w.shape[-1],), w.dtype)], axis=-1)


def _join_cols(shards):
    return shards.transpose(1, 0, 2).reshape(shards.shape[1], -1)


def _mod_parts(mod):
    return [mod[:, i * D_MODEL:(i + 1) * D_MODEL] for i in range(N_MOD)]


def _local_loss(x, mod, p, cos, sin, target):
    return _ffn_stage(x, _mixing_stage(x, mod, p, cos, sin), mod, p, target)


def _mixing_stage(x, mod, p, cos, sin):
    shift1, scale1 = _mod_parts(mod)[:2]

    w_in = _join_cols(p["w_in"])
    k_rope_w = w_in[:, 2176:2240]
    w_in_ext = jnp.concatenate([w_in[:, :2176], _pad_lanes(k_rope_w), _pad_lanes(_swap_halves(k_rope_w)),
                                jnp.zeros((D_MODEL, LANES), w_in.dtype)], axis=1)
    (h1,) = _make_rowwise("pre_attn", _f_pre_attn, 1, 3, [D_MODEL], [True], out_dtypes=[BF16])(
        x, p["norm_attn"], scale1, shift1)
    q_sb, k_sb, v_sb, cq, ckv, kr, kr_sw = _make_linear_split(
        "in_proj", (SB_WIDTH, SB_WIDTH, SB_WIDTH, MLA_Q_RANK, MLA_KV_RANK, LANES, LANES), 512)(h1, w_in_ext)

    o_sb = _sb_attention(q_sb, k_sb, v_sb)

    wq = _join_cols(p["w_q_up"]).reshape(MLA_Q_RANK, MLA_HEADS, MLA_QK)
    wq_n, wq_r = wq[:, :, :MLA_NOPE], wq[:, :, MLA_NOPE:]
    w_q_ext = jnp.concatenate([wq_n.reshape(MLA_Q_RANK, -1), _pad_lanes(wq_r).reshape(MLA_Q_RANK, -1),
                               _pad_lanes(_swap_halves(wq_r)).reshape(MLA_Q_RANK, -1)], axis=1)
    wkv = _join_cols(p["w_kv_up"]).reshape(MLA_KV_RANK, MLA_HEADS, MLA_NOPE + MLA_V)
    w_kv_ext = jnp.concatenate([wkv[:, :, :MLA_NOPE].reshape(MLA_KV_RANK, -1),
                                wkv[:, :, MLA_NOPE:].reshape(MLA_KV_RANK, -1)], axis=1)
    cqn, ckvn = _make_rowwise("mla_a", _f_mla_a, 2, 2, [MLA_Q_RANK, MLA_KV_RANK], [True, True],
                              out_dtypes=[BF16, BF16], grad_dtypes=[BF16, BF16])(
        cq, ckv, p["q_a_norm"], p["kv_a_norm"])
    qall = _make_linear("q_up", 384, 768)(cqn, w_q_ext)
    kn_all, v_mla = _make_linear_split("kv_up", (MLA_HEADS * MLA_NOPE, MLA_HEADS * MLA_V), MLA_KV_RANK)(ckvn, w_kv_ext)
    gq = p["q_norm"]
    gkr = p["k_rope_norm"]
    qn, qr, kn, krr = _make_rowwise("mla_b", _f_mla_b, 6, 6, [512, 512, 512, LANES],
                                    [True, True, True, True, False, False],
                                    out_dtypes=[BF16] * 4, grad_dtypes=[BF16] * 4)(
        qall, kn_all, kr, kr_sw, cos, sin,
        gq[:, :MLA_NOPE], _pad_lanes(gq[:, MLA_NOPE:]), _pad_lanes(_swap_halves(gq[:, MLA_NOPE:])),
        p["k_nope_norm"], _pad_lanes(gkr), _pad_lanes(_swap_halves(gkr)))
    o_mla = _mla_attention(qn, qr, kn, krr, v_mla)

    (mixed,) = _make_rowwise("post_attn", _f_post_attn, 2, 2, [D_MODEL], [True, True])(
        o_sb, o_mla, p["out_norm_sb"], p["out_norm_mla"])
    return mixed


def _ffn_stage(x, mixed, mod, p, target):
    _, _, gate1, shift2, scale2, gate2 = _mod_parts(mod)
    attn = _make_linear("out_proj", 512, 512)(mixed, p["w_out"].reshape(D_MODEL, D_MODEL))

    x2, h2 = _make_rowwise("pre_ffn", _f_pre_ffn, 2, 4, [D_MODEL, D_MODEL], [True, True],
                           out_dtypes=[F32, BF16], grad_dtypes=[F32, BF16])(
        x, attn, gate1, p["norm_ffn"], scale2, shift2)
    ffn = _ffn_block(h2, p["w_gate"], p["w_up"], p["w_down"])
    (row_loss,) = _make_rowwise("loss", _f_loss, 3, 1, [1], [True, True, False], grad_dtypes=[F32, BF16])(
        x2, ffn, target, gate2)
    return 0.5 * jnp.sum(row_loss)


def _my_place():
    return lax.axis_index("x"), lax.axis_index("y"), lax.axis_index("c")


def _all_gather_small(block, name):
    m_per, n = block.shape

    def body(x_ref, out_ref, send_sems, recv_sems, local_sem):
        x, y, c = _my_place()
        me, sibling = (x, y, c), (x, y, 1 - c)
        chips = [(1 - x, y), (x, 1 - y), (1 - x, 1 - y)]

        def rows(px, py, pc):
            return out_ref.at[pl.ds((4 * px + 2 * py + pc) * m_per, m_per), :]

        def copy(k, blk, to, src=None):
            return pltpu.make_async_remote_copy(
                src_ref=rows(*blk) if src is None else src, dst_ref=rows(*blk),
                send_sem=send_sems.at[k], recv_sem=recv_sems.at[k], device_id=to, device_id_type=MESH)

        mine = pltpu.make_async_copy(x_ref, rows(*me), local_sem)
        mine.start()
        first = [copy(0, me, sibling, src=x_ref)]
        first += [copy(1 + j, me, (*chip, c), src=x_ref) for j, chip in enumerate(chips)]
        for cp in first:
            cp.start()
        passed = [copy(4 + j, (*chip, c), sibling) for j, chip in enumerate(chips)]
        for j, chip in enumerate(chips):
            copy(1 + j, (*chip, c), me).wait_recv()
            passed[j].start()
        copy(0, sibling, me).wait_recv()
        for j, chip in enumerate(chips):
            copy(4 + j, (*chip, 1 - c), me).wait_recv()
        for cp in first + passed:
            cp.wait_send()
        mine.wait()

    return pl.pallas_call(
        body, name=name,
        out_shape=jax.ShapeDtypeStruct((N_DEV * m_per, n), block.dtype),
        in_specs=[pl.BlockSpec(memory_space=pltpu.VMEM)],
        out_specs=pl.BlockSpec(memory_space=pltpu.VMEM),
        scratch_shapes=[pltpu.SemaphoreType.DMA((7,)), pltpu.SemaphoreType.DMA((7,)), pltpu.SemaphoreType.DMA],
    )(block)


EARLY = ("w_in", "w_q_up", "w_kv_up")
LATE = ("w_out", "w_gate", "w_up", "w_down")
BIG = EARLY + LATE
TRANSPOSED_UPDATE = ("w_in", "w_gate", "w_up")
HALF_AXIS = {"w_in": 0, "w_q_up": 0, "w_kv_up": 0, "w_out": 0, "w_gate": 0, "w_up": 0, "w_down": 1}


def _half(ref, h, axis, lead=()):
    trail = ref.shape[len(lead):]
    idx = list(lead) + [slice(None)] * len(trail)
    at = len(trail) - 2 + axis
    n2 = trail[at] // 2
    idx[len(lead) + at] = pl.ds(h * n2, n2)
    return ref.at[tuple(idx)]


def _half_shape(shape, axis):
    shape = list(shape)
    shape[len(shape) - 2 + axis] //= 2
    return tuple(shape)


def _remote(src, dst, send_sems, recv_sems, k, to):
    return pltpu.make_async_remote_copy(src_ref=src, dst_ref=dst, send_sem=send_sems.at[k],
                                        recv_sem=recv_sems.at[k], device_id=to, device_id_type=MESH)


def _gather_weights(names, shards, after):
    n_w = len(shards)
    axes = [HALF_AXIS[n] for n in names]

    def body(*refs):
        w_refs, out_refs, token = refs[:n_w], refs[n_w + 1:2 * n_w + 1], refs[2 * n_w + 1]
        send_sems, recv_sems, local_sems = refs[2 * n_w + 2:]
        token[...] = jnp.zeros_like(token)
        x, y, c = _my_place()
        sibling = (x, y, 1 - c)
        chips = [(1 - x, y), (x, 1 - y), (1 - x, 1 - y)]
        me = 2 * x + y
        mine =[pltpu.make_async_copy(w, o.at[me], local_sems.at[i]) for i, (w, o) in enumerate(zip(w_refs, out_refs))]
        for cp in mine:
            cp.start()
        first = [_remote(_half(w_refs[i], c, axes[i]), _half(out_refs[i], c, axes[i], (me,)),
                         send_sems, recv_sems, 6 * i + j, (*chip, c))
                 for i in range(n_w) for j, chip in enumerate(chips)]
        for cp in first:
            cp.start()
        passed = []
        for j, (cx, cy) in enumerate(chips):
            for i in range(n_w):
                blk = _half(out_refs[i], c, axes[i], (2 * cx + cy,))
                _remote(blk, blk, send_sems, recv_sems, 6 * i + j, (cx, cy, c)).wait_recv()
                cp = _remote(blk, blk, send_sems, recv_sems, 6 * i + 3 + j, sibling)
                cp.start()
                passed.append(cp)
        for j, (cx, cy) in enumerate(chips):
            for i in range(n_w):
                blk = _half(out_refs[i], 1 - c, axes[i], (2 * cx + cy,))
                _remote(blk, blk, send_sems, recv_sems, 6 * i + 3 + j, sibling).wait_recv()
        for cp in first + passed:
            cp.wait_send()
        for cp in mine:
            cp.wait()

    outs = pl.pallas_call(
        body, name="gather_weights",
        out_shape=[jax.ShapeDtypeStruct((N_CHIPS,) + s.shape, s.dtype) for s in shards]
        + [jax.ShapeDtypeStruct((8, LANES), F32)],
        in_specs=[ANY] * (n_w + 1), out_specs=[ANY] * n_w + [pl.BlockSpec(memory_space=pltpu.VMEM)],
        scratch_shapes=[pltpu.SemaphoreType.DMA((6 * n_w,)), pltpu.SemaphoreType.DMA((6 * n_w,)),
                        pltpu.SemaphoreType.DMA((n_w,))],
    )(*shards, after)
    return outs[:n_w], outs[n_w]


def _pair_exchange(names, grads, call_name):
    n_w = len(grads)
    axes = [HALF_AXIS[n] for n in names]

    def body(*refs):
        g_refs, t_refs = refs[:n_w], refs[n_w:2 * n_w]
        send_sems, recv_sems = refs[2 * n_w:]
        x, y, c = _my_place()
        sends = [_remote(_half(g_refs[i], 1 - c, axes[i]), t_refs[i], send_sems, recv_sems, i, (x, y, 1 - c))
                 for i in range(n_w)]
        for cp in sends:
            cp.start()
        for cp in sends:
            cp.wait_recv()
        for cp in sends:
            cp.wait_send()

    return pl.pallas_call(
        body, name=call_name,
        out_shape=[jax.ShapeDtypeStruct(_half_shape(g.shape, a), g.dtype) for g, a in zip(grads, axes)],
        in_specs=[ANY] * n_w, out_specs=[ANY] * n_w,
        scratch_shapes=[pltpu.SemaphoreType.DMA((n_w,)), pltpu.SemaphoreType.DMA((n_w,))],
    )(*grads)


def _sibling_join(halves, name, after):
    n_w = len(halves)

    def body(*refs):
        s_refs, j_refs = refs[:n_w], refs[n_w + 1:2 * n_w + 1]
        send_sems, recv_sems = refs[2 * n_w + 1:]
        x, y, c = _my_place()
        sends = [_remote(s_refs[i], j_refs[i], send_sems, recv_sems, i, (x, y, 1 - c)) for i in range(n_w)]
        for cp in sends:
            cp.start()
        for cp in sends:
            cp.wait_recv()
        for cp in sends:
            cp.wait_send()

    return pl.pallas_call(
        body, name=name,
        out_shape=[jax.ShapeDtypeStruct(s.shape, s.dtype) for s in halves],
        in_specs=[ANY] * (n_w + 1), out_specs=[ANY] * n_w,
        scratch_shapes=[pltpu.SemaphoreType.DMA((n_w,)), pltpu.SemaphoreType.DMA((n_w,))],
    )(*halves, after)


HBM_SPEC = pl.BlockSpec(memory_space=pltpu.HBM)
SEM_SPEC = pl.BlockSpec(memory_space=pltpu.SEMAPHORE)
DATAFLOW = pltpu.SideEffectType.DATAFLOW_SIDE_EFFECTING


def _in_hbm(a):
    return pltpu.with_memory_space_constraint(a, pltpu.HBM)


def _exchange_start(name, srcs, lands, plan, n_copies, after, thru):
    n = len(srcs)

    def body(*refs):
        src_refs, land_refs = refs[:n], refs[n:2 * n]
        send_sems, recv_sems = refs[2 * n + 2], refs[2 * n + 3]
        for k, (src, dst, to) in enumerate(plan(src_refs, land_refs)):
            _remote(src, dst, send_sems, recv_sems, k, to).start()

    outs = pl.pallas_call(
        body, name=name,
        out_shape=(pltpu.SemaphoreType.DMA((n_copies,)), pltpu.SemaphoreType.DMA((n_copies,)),
                   *[pltpu.HBM(a.shape, a.dtype) for a in list(srcs) + list(lands) + [thru]]),
        in_specs=[HBM_SPEC] * (2 * n + 1) + [ANY],
        out_specs=(SEM_SPEC, SEM_SPEC, *[HBM_SPEC] * (2 * n + 1)),
        input_output_aliases={i: 2 + i for i in range(2 * n + 1)},
        compiler_params=pltpu.CompilerParams(has_side_effects=DATAFLOW),
    )(*[_in_hbm(a) for a in list(srcs) + list(lands) + [thru]], after)
    return outs[0], outs[1], outs[2:2 + n], outs[2 + n:2 + 2 * n], outs[2 + 2 * n]


def _exchange_wait(name, started, plan, after):
    send_sems, recv_sems, srcs, lands, _ = started
    n = len(srcs)

    def body(*refs):
        src_refs, land_refs = refs[:n], refs[n:2 * n]
        s_sems, r_sems = refs[2 * n], refs[2 * n + 1]
        for k, (src, dst, to) in enumerate(plan(src_refs, land_refs)):
            cp = _remote(src, dst, s_sems, r_sems, k, to)
            cp.wait_send()
            cp.wait_recv()

    outs = pl.pallas_call(
        body, name=name,
        out_shape=tuple(pltpu.HBM(a.shape, a.dtype) for a in list(srcs) + list(lands)),
        in_specs=[HBM_SPEC] * (2 * n) + [SEM_SPEC, SEM_SPEC, ANY],
        out_specs=tuple([HBM_SPEC] * (2 * n)),
        input_output_aliases={i: i for i in range(2 * n)},
        compiler_params=pltpu.CompilerParams(has_side_effects=DATAFLOW),
    )(*srcs, *lands, send_sems, recv_sems, after)
    return outs[:n], outs[n:]


def _late_gather_plan(src_refs, land_refs):
    x, y, c = _my_place()
    chips = [(1 - x, y), (x, 1 - y), (1 - x, 1 - y)]
    return [(src, land.at[2 * x + y], (cx, cy, c)) for src, land in zip(src_refs, land_refs) for cx, cy in chips]


def _late_scatter_plan(src_refs, land_refs):
    x, y, c = _my_place()
    chips = [(1 - x, y), (x, 1 - y), (1 - x, 1 - y)]
    return [(src.at[2 * cx + cy], land.at[j], (cx, cy, c))
            for src, land in zip(src_refs, land_refs) for j, (cx, cy) in enumerate(chips)]


def _row_tile(rows, mult=16, limit=ROW_TILE):
    return max(d for d in range(mult, limit + 1, mult) if rows % d == 0)


def _pair_sum(place, g, theirs, axis, name):
    nj, rr, cc = theirs.shape
    tr = _row_tile(rr, limit=1024)
    nb = rr // tr
    if axis == 0:
        g_map = lambda j, i, pr: (j, pr[0] * nb + i, 0)
    else:
        g_map = lambda j, i, pr: (j, i, pr[0])

    def body(pr, g_ref, t_ref, o_ref):
        o_ref[...] = (g_ref[...].astype(F32) + t_ref[...].astype(F32)).astype(BF16)

    spec = pl.BlockSpec((None, tr, cc), lambda j, i, pr: (j, i, 0))
    return pl.pallas_call(
        body, name=name,
        grid_spec=pltpu.PrefetchScalarGridSpec(
            num_scalar_prefetch=1, grid=(nj, nb),
            in_specs=[pl.BlockSpec((None, tr, cc), g_map), spec], out_specs=spec),
        out_shape=jax.ShapeDtypeStruct(theirs.shape, BF16))(place, g, theirs)


def _chip_sum(place, pair_sums, parts, name, transposed):
    _, rr, cc = parts.shape
    tr = _row_tile(rr, LANES) if transposed else _row_tile(rr, limit=1024)

    def body(pr, h_ref, p_ref, o_ref):
        acc = p_ref[0].astype(F32)
        for j in range(1, N_CHIPS - 1):
            acc = acc + p_ref[j].astype(F32)
        acc = acc + h_ref[...].astype(F32)
        o_ref[...] = (acc.T if transposed else acc).astype(BF16)

    out_spec = pl.BlockSpec((cc, tr), lambda i, pr: (0, i)) if transposed else pl.BlockSpec((tr, cc), lambda i, pr: (i, 0))
    return pl.pallas_call(
        body, name=name,
        grid_spec=pltpu.PrefetchScalarGridSpec(
            num_scalar_prefetch=1, grid=(rr // tr,),
            in_specs=[pl.BlockSpec((None, tr, cc), lambda i, pr: (pr[1], i, 0)),
                      pl.BlockSpec((N_CHIPS - 1, tr, cc), lambda i, pr: (0, i, 0))],
            out_specs=out_spec),
        out_shape=jax.ShapeDtypeStruct((cc, rr) if transposed else (rr, cc), BF16))(place, pair_sums, parts)


def _silu(v):
    return v / (1.0 + jnp.exp(-v))


def _ada_fwd(c_all, w_shard, b_shard):
    def body(c_ref, w_ref, b_ref, o_ref):
        o_ref[...] = jnp.dot(_silu(c_ref[...]), w_ref[...], precision=lax.Precision.HIGHEST,
                             preferred_element_type=F32) + b_ref[...]

    return pl.pallas_call(body, name="ada_fwd", out_shape=jax.ShapeDtypeStruct((c_all.shape[0], w_shard.shape[1]), F32),
                          compiler_params=pltpu.CompilerParams(vmem_limit_bytes=MM_VMEM_LIMIT))(c_all, w_shard, b_shard)


def _ada_bwd(c_all, dmod_cols):
    def body(c_ref, d_ref, o_ref):
        o_ref[...] = lax.dot_general(_silu(c_ref[...]), d_ref[...], (((0,), (0,)), ((), ())),
                                     precision=lax.Precision.HIGHEST, preferred_element_type=F32)

    return pl.pallas_call(body, name="ada_bwd", out_shape=jax.ShapeDtypeStruct((c_all.shape[1], dmod_cols.shape[1]), F32),
                          compiler_params=pltpu.CompilerParams(vmem_limit_bytes=MM_VMEM_LIMIT))(c_all, dmod_cols)


def _adamw_math(w, g, m, v):
    m = ADAM_B1 * m + (1.0 - ADAM_B1) * g
    v = ADAM_B2 * v + (1.0 - ADAM_B2) * (g * g)
    m_hat = m / (1.0 - ADAM_B1 ** ADAM_STEP)
    v_hat = v / (1.0 - ADAM_B2 ** ADAM_STEP)
    delta = -ADAM_LR * (m_hat / (jnp.sqrt(v_hat) + ADAM_EPS) + ADAM_WD * w)
    return delta, m, v


def _adamw(w, g, m, v, name):
    r, ccols = w.shape
    tr = max(d for d in range(8, ROW_TILE + 1, 8) if r % d == 0)
    spec = pl.BlockSpec((tr, ccols), lambda i: (i, 0))

    def body(w_ref, g_ref, m_ref, v_ref, d_ref, nm_ref, nv_ref):
        d_ref[...], nm_ref[...], nv_ref[...] = _adamw_math(w_ref[...], g_ref[...], m_ref[...], v_ref[...])

    return pl.pallas_call(body, name=name, grid=(r // tr,), in_specs=[spec] * 4, out_specs=[spec] * 3,
                          out_shape=[jax.ShapeDtypeStruct(w.shape, F32)] * 3,
                          compiler_params=pltpu.CompilerParams(vmem_limit_bytes=MM_VMEM_LIMIT))(w, g, m, v)


def _small_layout(sizes):
    offs, off = [], 0
    for n in sizes:
        offs.append(off)
        off += -(-n // LANES) * LANES
    total = -(-(off + LANES) // (8 * LANES)) * (8 * LANES)
    return offs, off, total


def _adamw_small(ws, g_all, ms, vs, offs, loss_off):
    n_p = len(ws)

    def device_sum(g_ref, off, width):
        blk = g_ref[:, off:off + width]
        acc = blk[0:1]
        for d in range(1, N_DEV):
            acc = acc + blk[d:d + 1]
        return acc

    def body(*refs):
        w_refs, m_refs, v_refs = refs[:n_p], refs[n_p:2 * n_p], refs[2 * n_p:3 * n_p]
        g_ref = refs[3 * n_p]
        outs = refs[3 * n_p + 1:]
        for i in range(n_p):
            n = w_refs[i].shape[1]
            g = device_sum(g_ref, offs[i], -(-n // LANES) * LANES)[:, :n]
            outs[i][...] = g
            outs[n_p + i][...], outs[2 * n_p + i][...], outs[3 * n_p + i][...] = _adamw_math(
                w_refs[i][...], g, m_refs[i][...], v_refs[i][...])
        outs[4 * n_p][...] = device_sum(g_ref, loss_off, LANES)

    res = pl.pallas_call(
        body, name="adamw_small",
        out_shape=[jax.ShapeDtypeStruct(a.shape, F32) for a in list(ws) * 4] + [jax.ShapeDtypeStruct((1, LANES), F32)],
    )(*ws, *ms, *vs, g_all)
    return res[:n_p], res[n_p:2 * n_p], res[2 * n_p:3 * n_p], res[3 * n_p:4 * n_p], res[4 * n_p]


def _adamw_halves(place, w, own, sib, m, v, axis, name, after):
    r, cc = w.shape
    if axis == 0:
        rows, gc = own.shape[0], own.shape[1]
        tr = _row_tile(rows)
        nb = rows // tr
        w_spec = pl.BlockSpec((tr, cc), lambda h, i, pr: (h * nb + i, 0))
        g_spec = pl.BlockSpec((tr, gc), lambda h, i, pr: (i, 0))
    else:
        tr = _row_tile(r)
        nb = r // tr
        gc = own.shape[1]
        w_spec = pl.BlockSpec((tr, gc), lambda h, i, pr: (i, h))
        g_spec = pl.BlockSpec((tr, gc), lambda h, i, pr: (i, 0))
    wc = w_spec.block_shape[1]

    def body(pr, w_ref, o_ref, s_ref, m_ref, v_ref, after_ref, g_ref, d_ref, nm_ref, nv_ref):
        g = jnp.where(pl.program_id(0) == pr[0], o_ref[...], s_ref[...]).astype(F32)[:, :wc]
        g_ref[...] = g
        d_ref[...], nm_ref[...], nv_ref[...] = _adamw_math(w_ref[...], g, m_ref[...], v_ref[...])

    return pl.pallas_call(
        body, name=name,
        grid_spec=pltpu.PrefetchScalarGridSpec(
            num_scalar_prefetch=1, grid=(2, nb),
            in_specs=[w_spec, g_spec, g_spec, w_spec, w_spec, ANY], out_specs=[w_spec] * 4),
        out_shape=[jax.ShapeDtypeStruct(w.shape, F32)] * 4,
        compiler_params=pltpu.CompilerParams(vmem_limit_bytes=MM_VMEM_LIMIT))(place, w, own, sib, m, v, after)


SMALL = ("b_ada", "norm_attn", "norm_ffn", "q_a_norm", "kv_a_norm", "q_norm", "k_nope_norm", "k_rope_norm",
         "out_norm_sb", "out_norm_mla")
WEIGHTS = ("w_ada", "b_ada", "norm_attn", "norm_ffn", "w_in", "q_a_norm", "w_q_up", "kv_a_norm", "w_kv_up",
           "q_norm", "k_nope_norm", "k_rope_norm", "out_norm_sb", "out_norm_mla", "w_out", "w_gate", "w_up",
           "w_down")


def kernel(x, c, positions, w_ada, b_ada, norm_attn, norm_ffn, w_in, q_a_norm, w_q_up, kv_a_norm, w_kv_up, q_norm, k_nope_norm, k_rope_norm, out_norm_sb, out_norm_mla, w_out, w_gate, w_up, w_down, loss_target, m_w_ada, m_b_ada, m_norm_attn, m_norm_ffn, m_w_in, m_q_a_norm, m_w_q_up, m_kv_a_norm, m_w_kv_up, m_q_norm, m_k_nope_norm, m_k_rope_norm, m_out_norm_sb, m_out_norm_mla, m_w_out, m_w_gate, m_w_up, m_w_down, v_w_ada, v_b_ada, v_norm_attn, v_norm_ffn, v_w_in, v_q_a_norm, v_w_q_up, v_kv_a_norm, v_w_kv_up, v_q_norm, v_k_nope_norm, v_k_rope_norm, v_out_norm_sb, v_out_norm_mla, v_w_out, v_w_gate, v_w_up, v_w_down):
    local = dict(locals())
    w = {n: local[n][0] for n in WEIGHTS}
    m = {n: local["m_" + n][0] for n in WEIGHTS}
    v = {n: local["v_" + n][0] for n in WEIGHTS}
    small = {n: w[n].reshape(1, -1) for n in SMALL}
    ix, iy, ic = _my_place()
    chip = 2 * ix + iy
    dev = 2 * chip + ic
    xs, target = x[0], loss_target[0]
    seq = xs.shape[0]

    c_all = _all_gather_small(c.reshape(8, LANES), "gather_c").reshape(N_DEV, D_MODEL)
    ada_cols = w["w_ada"].shape[1]
    b_cols = lax.dynamic_slice_in_dim(small["b_ada"], chip * ada_cols, ada_cols, axis=1)
    mod_cols = _ada_fwd(c_all, w["w_ada"], b_cols)
    mod_all = _all_gather_small(mod_cols, "gather_mod").reshape(N_CHIPS, 2, N_DEV, ada_cols)
    mod = lax.dynamic_index_in_dim(mod_all[:, 0], dev, axis=1, keepdims=False).reshape(1, N_MOD * D_MODEL)

    ff_pad = FF_SHARD_PAD - FF_SHARD
    pads = {"w_gate": ((0, 0), (0, ff_pad)), "w_up": ((0, 0), (0, ff_pad)), "w_down": ((0, ff_pad), (0, 0))}
    shards = {n: jnp.pad(w[n].astype(BF16), pads[n]) if n in pads else w[n].astype(BF16) for n in BIG}
    early, early_done = _gather_weights(EARLY, [shards[n] for n in EARLY], mod)
    gathered = dict(zip(EARLY, early))
    lands = [lax.dynamic_update_index_in_dim(lax.empty((N_CHIPS,) + shards[n].shape, BF16), shards[n], chip, 0)
             for n in LATE]
    late_gather = _exchange_start("gather_late_start", [shards[n] for n in LATE], lands, _late_gather_plan,
                                  3 * len(LATE), early_done, mod)
    mod = late_gather[4]

    half = MLA_ROPE // 2
    freqs = 1.0 / (ROPE_THETA ** (np.arange(half, dtype=np.float32) / half))
    zeros = np.zeros(LANES - MLA_ROPE, np.float32)
    freqs_row = jnp.asarray(np.concatenate([freqs, freqs, zeros]).astype(np.float32)[None])
    sign_row = jnp.asarray(np.concatenate([-np.ones(half), np.ones(half), zeros]).astype(np.float32)[None])
    cos, sin = _rope_tables(positions.reshape(seq, 1), freqs_row, sign_row)

    place = jnp.stack([ic, chip]).astype(jnp.int32)
    small_params = {n: small[n] for n in SMALL if n != "b_ada"}

    def pair_sums_of(names, grads, call_name):
        theirs = _pair_exchange(names, grads, call_name)
        return [_pair_sum(place, gr, th, HALF_AXIS[n], "grad_pair_sum_" + n) for n, gr, th in zip(names, grads, theirs)]

    p1 = {**{n: gathered[n] for n in EARLY}, **small_params}
    mixed, mixing_vjp = jax.vjp(lambda x_, mod_, p_: _mixing_stage(x_, mod_, p_, cos, sin), xs, mod, p1)
    _, landed = _exchange_wait("gather_late_wait", late_gather, _late_gather_plan, mixed)
    p2 = {**dict(zip(LATE, landed)), **small_params}
    loss_part, ffn_vjp = jax.vjp(lambda x_, mixed_, mod_, p_: _ffn_stage(x_, mixed_, mod_, p_, target), xs, mixed, mod, p2)
    gx2, gmixed, gmod2, gp2 = ffn_vjp(jnp.ones((), F32))
    late_sums = pair_sums_of(LATE, [gp2[n] for n in LATE], "grad_pair_exchange_late")
    late_scatter = _exchange_start(
        "grad_scatter_late_start", late_sums,
        [lax.empty((N_CHIPS - 1,) + s.shape[1:], BF16) for s in late_sums], _late_scatter_plan, 3 * len(LATE),
        gx2, gmixed)
    gx1, gmod1, gp1 = mixing_vjp(late_scatter[4])
    gx = gx1 + gx2
    gmod = gmod1 + gmod2
    gp = {n: gp1[n] + gp2[n] for n in small_params}

    sizes = [w[n].size for n in SMALL]
    offs, loss_off, n_small = _small_layout(sizes)
    pieces = []
    for n, size in zip(SMALL, sizes):
        pieces.append(gmod if n == "b_ada" else gp[n])
        if size % LANES:
            pieces.append(jnp.zeros((1, LANES - size % LANES), F32))
    pieces += [jnp.full((1, LANES), loss_part), jnp.zeros((1, n_small - loss_off - LANES), F32)]
    small_vec = jnp.concatenate(pieces, axis=1)
    small_all = _all_gather_small(small_vec.reshape(8, n_small // 8), "gather_small").reshape(N_DEV, n_small)

    g, delta, new_m, new_v = {}, {}, {}, {}

    def reduce_halves(names, sums, parts, join_name, after):
        own = [_chip_sum(place, ps, pt, "grad_chip_sum_" + n, n in TRANSPOSED_UPDATE) for n, ps, pt in zip(names, sums, parts)]
        return own, _sibling_join(own, join_name, after)

    def update(names, own, sib, after):
        for n, o, s in zip(names, own, sib):
            if n in TRANSPOSED_UPDATE:
                res = _adamw_halves(place, w[n].T, o, s, m[n].T, v[n].T, 1, "adamw_" + n, after)
                g[n], delta[n], new_m[n], new_v[n] = [r.T for r in res]
            else:
                g[n], delta[n], new_m[n], new_v[n] = _adamw_halves(place, w[n], o, s, m[n], v[n], HALF_AXIS[n],
                                                                   "adamw_" + n, after)

    late_sums, late_parts = _exchange_wait("grad_scatter_late_wait", late_scatter, _late_scatter_plan, gx)
    own_late, sib_late = reduce_halves(LATE, late_sums, late_parts, "grad_sibling_join_late", small_all)
    early_sums = pair_sums_of(EARLY, [gp1[n] for n in EARLY], "grad_pair_exchange_early")
    early_scatter = _exchange_start(
        "grad_scatter_early_start", early_sums,
        [lax.empty((N_CHIPS - 1,) + s.shape[1:], BF16) for s in early_sums], _late_scatter_plan, 3 * len(EARLY),
        sib_late[0], small_all)
    small_all = early_scatter[4]
    update(LATE, own_late, sib_late, small_all)

    *small_out, loss_row = _adamw_small([small[n] for n in SMALL], small_all, [m[n].reshape(1, -1) for n in SMALL],
                                        [v[n].reshape(1, -1) for n in SMALL], offs, loss_off)
    loss = loss_row[0, 0]
    for d, outs_d in zip((g, delta, new_m, new_v), small_out):
        d.update({n: o.reshape(w[n].shape) for n, o in zip(SMALL, outs_d)})

    dmod_all = small_all[:, :N_MOD * D_MODEL]
    g["w_ada"] = _ada_bwd(c_all, lax.dynamic_slice_in_dim(dmod_all, chip * ada_cols, ada_cols, axis=1))
    delta["w_ada"], new_m["w_ada"], new_v["w_ada"] = _adamw(w["w_ada"], g["w_ada"], m["w_ada"], v["w_ada"], "adamw_w_ada")

    early_sums, early_parts = _exchange_wait("grad_scatter_early_wait", early_scatter, _late_scatter_plan,
                                             delta["w_ada"])
    own_early, sib_early = reduce_halves(EARLY, early_sums, early_parts, "grad_sibling_join_early", delta["w_ada"])
    update(EARLY, own_early, sib_early, sib_early[0])

    def outs(d):
        return [d[n][None] for n in WEIGHTS]

    return (loss, gx[None], *outs(g), *outs(delta), *outs(new_m), *outs(new_v))
```

```python
import numpy as np
import jax
import jax.numpy as jnp
from jax import lax
from jax.experimental import pallas as pl
from jax.experimental.pallas import tpu as pltpu

F32 = jnp.float32
BF16 = jnp.bfloat16
MESH = pl.DeviceIdType.MESH
ANY = pl.BlockSpec(memory_space=pl.ANY)

D_MODEL = 1024
SB_HEADS = 8
SB_HEAD_DIM = 64
SB_WIDTH = 512
MLA_HEADS = 4
MLA_NOPE = 128
MLA_ROPE = 64
MLA_QK = 192
MLA_V = 128
MLA_Q_RANK = 384
MLA_KV_RANK = 256
D_FF = 2816
N_MOD = 6
ROPE_THETA = 10000.0
EPS = 1e-6
LANES = 128

ADAM_LR = 0.001
ADAM_B1 = 0.9
ADAM_B2 = 0.999
ADAM_EPS = 1e-08
ADAM_WD = 0.01
ADAM_STEP = 10

N_CHIPS = 4
N_DEV = 8
ROW_TILE = 256
MM_ROW_TILE = 512
ATT_BLK = 256
MM_VMEM_LIMIT = 56 * 1024 * 1024
FF_SHARD = D_FF // N_CHIPS
FF_SHARD_PAD = 768


def _mm(a, b, mode, name, tm, tn, out_dtype=F32):
    if mode == "nn":
        (m, k), n = a.shape, b.shape[1]
        a_spec = pl.BlockSpec((tm, k), lambda j, i: (i, 0))
        b_spec = pl.BlockSpec((k, tn), lambda j, i: (0, j))
        dims = (((1,), (0,)), ((), ()))
    elif mode == "nt":
        (m, k), n = a.shape, b.shape[0]
        a_spec = pl.BlockSpec((tm, k), lambda j, i: (i, 0))
        b_spec = pl.BlockSpec((tn, k), lambda j, i: (j, 0))
        dims = (((1,), (1,)), ((), ()))
    else:
        (k, m), n = a.shape, b.shape[1]
        a_spec = pl.BlockSpec((k, tm), lambda j, i: (0, i))
        b_spec = pl.BlockSpec((k, tn), lambda j, i: (0, j))
        dims = (((0,), (0,)), ((), ()))
    assert m % tm == 0 and n % tn == 0, (name, m, n, tm, tn)

    def body(a_ref, b_ref, o_ref):
        o_ref[...] = lax.dot_general(a_ref[...].astype(BF16), b_ref[...].astype(BF16), dims,
                                     preferred_element_type=F32).astype(out_dtype)

    return pl.pallas_call(
        body, name=name, grid=(n // tn, m // tm),
        in_specs=[a_spec, b_spec],
        out_specs=pl.BlockSpec((tm, tn), lambda j, i: (i, j)),
        out_shape=jax.ShapeDtypeStruct((m, n), out_dtype),
        compiler_params=pltpu.CompilerParams(dimension_semantics=("arbitrary", "arbitrary"),
                                             vmem_limit_bytes=MM_VMEM_LIMIT),
    )(a, b)


def _make_linear(name, tk_w, tn_w):
    @jax.custom_vjp
    def op(a, w):
        return _mm(a, w, "nn", name + "_fwd", MM_ROW_TILE, w.shape[1])

    def fwd(a, w):
        return op(a, w), (a, w)

    def bwd(res, dy):
        a, w = res
        da = _mm(dy, w, "nt", name + "_dx", MM_ROW_TILE, w.shape[0])
        dw = _mm(a, dy, "tn", name + "_dw", tk_w, tn_w, out_dtype=BF16)
        return da, dw

    op.defvjp(fwd, bwd)
    return op


def _make_linear_split(name, widths, tk_w):
    starts = [sum(widths[:g]) for g in range(len(widths))]

    def call_fwd(a, w):
        t, k = a.shape
        n = w.shape[1]

        def body(a_ref, w_ref, *o_refs):
            y = jnp.dot(a_ref[...].astype(BF16), w_ref[...], preferred_element_type=F32)
            for o_ref, s0, wd in zip(o_refs, starts, widths):
                o_ref[...] = y[:, s0:s0 + wd]

        return pl.pallas_call(
            body, name=name + "_fwd", grid=(t // MM_ROW_TILE,),
            in_specs=[pl.BlockSpec((MM_ROW_TILE, k), lambda i: (i, 0)), pl.BlockSpec((k, n), lambda i: (0, 0))],
            out_specs=[pl.BlockSpec((MM_ROW_TILE, wd), lambda i: (i, 0)) for wd in widths],
            out_shape=[jax.ShapeDtypeStruct((t, wd), F32) for wd in widths],
            compiler_params=pltpu.CompilerParams(dimension_semantics=("arbitrary",), vmem_limit_bytes=MM_VMEM_LIMIT),
        )(a, w)

    def call_dx(dys, w):
        t = dys[0].shape[0]
        k, n = w.shape

        def body(*refs):
            dy_refs, w_ref, o_ref = refs[:-2], refs[-2], refs[-1]
            acc = jnp.zeros((MM_ROW_TILE, k), F32)
            for dy_ref, s0, wd in zip(dy_refs, starts, widths):
                acc = acc + _nt(dy_ref[...].astype(BF16), w_ref[:, s0:s0 + wd])
            o_ref[...] = acc

        return pl.pallas_call(
            body, name=name + "_dx", grid=(t // MM_ROW_TILE,),
            in_specs=[pl.BlockSpec((MM_ROW_TILE, wd), lambda i: (i, 0)) for wd in widths]
            + [pl.BlockSpec((k, n), lambda i: (0, 0))],
            out_specs=pl.BlockSpec((MM_ROW_TILE, k), lambda i: (i, 0)),
            out_shape=jax.ShapeDtypeStruct((t, k), F32),
            compiler_params=pltpu.CompilerParams(dimension_semantics=("arbitrary",), vmem_limit_bytes=MM_VMEM_LIMIT),
        )(*dys, w)

    def call_dw(a, dys, w):
        t, k = a.shape
        n = w.shape[1]

        def body(a_ref, *refs):
            dy_refs, o_ref = refs[:-1], refs[-1]
            ab = a_ref[...].astype(BF16)
            for dy_ref, s0, wd in zip(dy_refs, starts, widths):
                o_ref[:, s0:s0 + wd] = _tn(ab, dy_ref[...].astype(BF16)).astype(BF16)
            if starts[-1] + widths[-1] < n:
                o_ref[:, starts[-1] + widths[-1]:] = jnp.zeros((tk_w, n - starts[-1] - widths[-1]), BF16)

        return pl.pallas_call(
            body, name=name + "_dw", grid=(k // tk_w,),
            in_specs=[pl.BlockSpec((t, tk_w), lambda i: (0, i))]
            + [pl.BlockSpec((t, wd), lambda i: (0, 0)) for wd in widths],
            out_specs=pl.BlockSpec((tk_w, n), lambda i: (i, 0)),
            out_shape=jax.ShapeDtypeStruct((k, n), BF16),
            compiler_params=pltpu.CompilerParams(dimension_semantics=("arbitrary",), vmem_limit_bytes=MM_VMEM_LIMIT),
        )(a, *dys)

    @jax.custom_vjp
    def op(a, w):
        return tuple(call_fwd(a, w))

    def fwd(a, w):
        return op(a, w), (a, w)

    def bwd(res, dys):
        a, w = res
        return call_dx(dys, w), call_dw(a, dys, w)

    op.defvjp(fwd, bwd)
    return op


def _row_spec(arr, tb):
    return pl.BlockSpec((tb, arr.shape[1]), lambda i: (i, 0))


def _full_spec(arr):
    return pl.BlockSpec(arr.shape, lambda i: (0, 0))


def _make_rowwise(name, f, n_rows, n_params, out_cols, diff_rows, out_dtypes=None, grad_dtypes=None):
    n_out = len(out_cols)
    out_dtypes = out_dtypes or [F32] * n_out
    grad_dtypes = grad_dtypes or [F32] * sum(diff_rows)

    def call_fwd(rows, params):
        t = rows[0].shape[0]

        def body(*refs):
            ins = [r[...] for r in refs[:n_rows + n_params]]
            outs = f(*ins)
            for o_ref, o in zip(refs[n_rows + n_params:], outs):
                o_ref[...] = o.astype(o_ref.dtype)

        return pl.pallas_call(
            body, name=name + "_fwd", grid=(t // ROW_TILE,),
            in_specs=[_row_spec(a, ROW_TILE) for a in rows] + [_full_spec(p) for p in params],
            out_specs=[pl.BlockSpec((ROW_TILE, n), lambda i: (i, 0)) for n in out_cols],
            out_shape=[jax.ShapeDtypeStruct((t, n), dt) for n, dt in zip(out_cols, out_dtypes)],
            compiler_params=pltpu.CompilerParams(dimension_semantics=("arbitrary",),
                                                 vmem_limit_bytes=MM_VMEM_LIMIT),
        )(*rows, *params)

    def call_bwd(rows, params, cts):
        t = rows[0].shape[0]
        d_rows = [a for a, d in zip(rows, diff_rows) if d]
        n_in = n_rows + n_params + n_out

        def body(*refs):
            ins = [r[...] for r in refs[:n_rows + n_params]]
            ct = tuple(r[...].astype(F32) for r in refs[n_rows + n_params:n_in])
            _, vjp = jax.vjp(f, *ins)
            grads = vjp(ct)
            out_refs = refs[n_in:]
            g_rows = [g for g, d in zip(grads[:n_rows], diff_rows) if d]
            for o_ref, g in zip(out_refs[:len(g_rows)], g_rows):
                o_ref[...] = g.astype(o_ref.dtype)
            p_refs = out_refs[len(g_rows):]

            if p_refs:
                @pl.when(pl.program_id(0) == 0)
                def _():
                    for p_ref in p_refs:
                        p_ref[...] = jnp.zeros_like(p_ref)

                for p_ref, g in zip(p_refs, grads[n_rows:]):
                    p_ref[...] += g

        return pl.pallas_call(
            body, name=name + "_bwd", grid=(t // ROW_TILE,),
            in_specs=[_row_spec(a, ROW_TILE) for a in rows] + [_full_spec(p) for p in params]
            + [_row_spec(c, ROW_TILE) for c in cts],
            out_specs=[_row_spec(a, ROW_TILE) for a in d_rows] + [_full_spec(p) for p in params],
            out_shape=[jax.ShapeDtypeStruct(a.shape, dt) for a, dt in zip(d_rows, grad_dtypes)]
            + [jax.ShapeDtypeStruct(p.shape, F32) for p in params],
            compiler_params=pltpu.CompilerParams(dimension_semantics=("arbitrary",),
                                                 vmem_limit_bytes=MM_VMEM_LIMIT),
        )(*rows, *params, *cts)

    @jax.custom_vjp
    def op(*args):
        return tuple(call_fwd(args[:n_rows], args[n_rows:]))

    def fwd(*args):
        return op(*args), args

    def bwd(args, cts):
        rows, params = args[:n_rows], args[n_rows:]
        outs = call_bwd(rows, params, cts)
        it = iter(outs)
        g_rows = [next(it) if d else jnp.zeros_like(a) for a, d in zip(rows, diff_rows)]
        return tuple(g_rows) + tuple(it)

    op.defvjp(fwd, bwd)
    return op


def _rms(x, g, n):
    return x * lax.rsqrt(jnp.sum(x * x, axis=-1, keepdims=True) * (1.0 / n) + EPS) * g


def _f_pre_attn(x, g, scale, shift):
    return (_rms(x, g, D_MODEL) * (1.0 + scale) + shift,)


def _f_mla_a(cq, ckv, gq, gkv):
    return _rms(cq, gq, MLA_Q_RANK), _rms(ckv, gkv, MLA_KV_RANK)


@jax.custom_vjp
def _split_lanes(x):
    return tuple(x[:, i * LANES:(i + 1) * LANES] for i in range(x.shape[1] // LANES))


def _split_lanes_fwd(x):
    return _split_lanes(x), None


def _split_lanes_bwd(_, cts):
    return (jnp.concatenate(cts, axis=1),)


_split_lanes.defvjp(_split_lanes_fwd, _split_lanes_bwd)


def _f_mla_b(qall, kn_all, kr, kr_sw, cos, sin, gqn, gqr, gqr_sw, gkn, gkr, gkr_sw):
    q = _split_lanes(qall)
    kn = _split_lanes(kn_all)
    qn_o, qr_o, kn_o = [], [], []
    for h in range(MLA_HEADS):
        qn, qr, qs = q[h], q[MLA_HEADS + h], q[2 * MLA_HEADS + h]
        ss = jnp.sum(qn * qn, axis=-1, keepdims=True) + jnp.sum(qr * qr, axis=-1, keepdims=True)
        rs = lax.rsqrt(ss * (1.0 / MLA_QK) + EPS)
        qn_o.append(qn * rs * gqn)
        qr_o.append((qr * rs * gqr) * cos + (qs * rs * gqr_sw) * sin)
        kn_o.append(_rms(kn[h], gkn, MLA_NOPE))
    rs = lax.rsqrt(jnp.sum(kr * kr, axis=-1, keepdims=True) * (1.0 / MLA_ROPE) + EPS)
    kr_o = (kr * rs * gkr) * cos + (kr_sw * rs * gkr_sw) * sin
    return (jnp.concatenate(qn_o, axis=1), jnp.concatenate(qr_o, axis=1), jnp.concatenate(kn_o, axis=1), kr_o)


def _f_post_attn(o_sb, o_mla, g_sb, g_mla):
    return (jnp.concatenate([_rms(o_sb, g_sb, SB_WIDTH), _rms(o_mla, g_mla, SB_WIDTH)], axis=1),)


def _f_pre_ffn(x, attn, gate, g, scale, shift):
    x2 = x + gate * attn
    return x2, _rms(x2, g, D_MODEL) * (1.0 + scale) + shift


def _f_swiglu(gt, up):
    return (gt / (1.0 + jnp.exp(-gt)) * up,)


def _f_loss(x2, ffn, target, gate):
    err = x2 + gate * ffn - target
    return (jnp.sum(err * err, axis=-1, keepdims=True) * (1.0 / D_MODEL),)


def _rope_tables(pos_col, freqs, sign):
    t = pos_col.shape[0]

    def body(p_ref, f_ref, s_ref, cos_ref, sin_ref):
        ang = p_ref[...].astype(F32) * f_ref[...]
        live = jnp.abs(s_ref[...])
        cos_ref[...] = jnp.cos(ang) * live
        sin_ref[...] = jnp.sin(ang) * s_ref[...]

    return pl.pallas_call(
        body, name="rope_tables", grid=(t // ROW_TILE,),
        in_specs=[pl.BlockSpec((ROW_TILE, 1), lambda i: (i, 0)), _full_spec(freqs), _full_spec(sign)],
        out_specs=[pl.BlockSpec((ROW_TILE, LANES), lambda i: (i, 0))] * 2,
        out_shape=[jax.ShapeDtypeStruct((t, LANES), F32)] * 2,
    )(pos_col, freqs, sign)


def _hi_lo_dot(x, tri):
    hi = x.astype(BF16)
    lo = (x - hi.astype(F32)).astype(BF16)
    return (jnp.dot(hi, tri, preferred_element_type=F32) + jnp.dot(lo, tri, preferred_element_type=F32))


def _tri(cmp):
    r = lax.broadcasted_iota(jnp.int32, (ATT_BLK, ATT_BLK), 0)
    c = lax.broadcasted_iota(jnp.int32, (ATT_BLK, ATT_BLK), 1)
    return cmp(r, c).astype(BF16)


def _nt(a, b):
    return lax.dot_general(a, b, (((1,), (1,)), ((), ())), preferred_element_type=F32)


def _tn(a, b):
    return lax.dot_general(a, b, (((0,), (0,)), ((), ())), preferred_element_type=F32)


def _sb_logs(z):
    lb = jnp.minimum(z, 0.0) - jnp.log(1.0 + jnp.exp(-jnp.abs(z)))
    return lb, lb - z


def _sb_fwd(q, k, v):
    t = q.shape[0]
    nq = t // ATT_BLK
    scale = SB_HEAD_DIM ** -0.5

    def body(q_ref, k_ref, v_ref, o_ref, tot_ref):
        qi = pl.program_id(1)
        lane = lax.broadcasted_iota(jnp.int32, (ATT_BLK, LANES), 1)
        tri = _tri(lambda r, c: r > c)
        qv = q_ref[...] * scale
        heads = [(lane // SB_HEAD_DIM) == hh for hh in range(2)]
        qms = [jnp.where(mine, qv, 0.0).astype(BF16) for mine in heads]

        def blocks(kbs, carry, diagonal):
            acc = carry[0]
            nb = len(kbs)
            chains = [(b, hh) for b in range(nb) for hh in range(2)]
            offs = [pl.multiple_of(kb * ATT_BLK, ATT_BLK) for kb in kbs]
            kks = [k_ref[pl.ds(off, ATT_BLK), :].astype(BF16) for off in offs]
            v_blks = [v_ref[pl.ds(off, ATT_BLK), :] for off in offs]
            if any(diagonal):
                valid = (lax.broadcasted_iota(jnp.int32, (ATT_BLK, ATT_BLK), 1)
                         < lax.broadcasted_iota(jnp.int32, (ATT_BLK, ATT_BLK), 0))
            zs = {ch: _nt(qms[ch[1]], kks[ch[0]]) for ch in chains}
            vvs = {(b, hh): jnp.where(heads[hh], v_blks[b], 0.0).astype(BF16) for b, hh in chains}
            logs = {ch: _sb_logs(zs[ch]) for ch in chains}
            l1ms = {ch: jnp.where(valid, logs[ch][1], 0.0) if diagonal[ch[0]] else logs[ch][1] for ch in chains}
            run = {(0, hh): carry[1 + hh] for hh in range(2)}
            for b, hh in chains:
                run[(b + 1, hh)] = run[(b, hh)] + jnp.sum(l1ms[(b, hh)], axis=-1, keepdims=True)
            afters = {ch: _hi_lo_dot(l1ms[ch], tri) for ch in chains}
            ws = {ch: jnp.exp(logs[ch][0] + (afters[ch] + run[ch])) for ch in chains}
            ws = {ch: jnp.where(valid, ws[ch], 0.0) if diagonal[ch[0]] else ws[ch] for ch in chains}
            for ch in chains:
                acc = acc + jnp.dot(ws[ch].astype(BF16), vvs[ch], preferred_element_type=F32)
            return (acc, run[(nb, 0)], run[(nb, 1)])

        zero = jnp.zeros((ATT_BLK, 1), F32)
        init = (jnp.zeros((ATT_BLK, LANES), F32), zero, zero)
        carry = lax.cond(qi % 2 == 1, lambda cr: blocks([qi, qi - 1], cr, (True, False)),
                         lambda cr: blocks([qi], cr, (True,)), init)
        top = qi - 1 - qi % 2
        carry = lax.fori_loop(0, qi // 2, lambda pr, cr: blocks([top - 2 * pr, top - 1 - 2 * pr], cr, (False, False)),
                              carry)
        o_ref[...] = carry[0]
        for hh in range(2):
            tot_ref[:, hh * LANES:(hh + 1) * LANES] = jnp.broadcast_to(carry[1 + hh], (ATT_BLK, LANES))

    return pl.pallas_call(
        body, name="sb_attn_fwd", grid=(SB_HEADS // 2, nq),
        in_specs=[pl.BlockSpec((ATT_BLK, LANES), lambda p, i: (i, p)),
                  pl.BlockSpec((t, LANES), lambda p, i: (0, p)),
                  pl.BlockSpec((t, LANES), lambda p, i: (0, p))],
        out_specs=[pl.BlockSpec((ATT_BLK, LANES), lambda p, i: (i, p)),
                   pl.BlockSpec((ATT_BLK, 2 * LANES), lambda p, i: (i, p))],
        out_shape=[jax.ShapeDtypeStruct((t, SB_WIDTH), F32), jax.ShapeDtypeStruct((t, SB_HEADS * LANES), F32)],
        compiler_params=pltpu.CompilerParams(dimension_semantics=("arbitrary", "arbitrary")),
    )(q, k, v)


def _sb_bwd(q, k, v, tot, do):
    t = q.shape[0]
    nq = t // ATT_BLK
    scale = SB_HEAD_DIM ** -0.5

    def body(q_ref, k_ref, v_ref, tot_ref, do_ref, dq_ref, dk_ref, dv_ref):
        qi = pl.program_id(1)

        @pl.when(qi == 0)
        def _():
            dk_ref[...] = jnp.zeros_like(dk_ref)
            dv_ref[...] = jnp.zeros_like(dv_ref)

        lane = lax.broadcasted_iota(jnp.int32, (ATT_BLK, LANES), 1)
        tri_incl = _tri(lambda r, c: r <= c)
        tri_lt = _tri(lambda r, c: r < c)
        qv = q_ref[...] * scale
        dov = do_ref[...]
        heads = [(lane // SB_HEAD_DIM) == hh for hh in range(2)]
        qms = [jnp.where(mine, qv, 0.0).astype(BF16) for mine in heads]
        doms = [jnp.where(mine, dov, 0.0).astype(BF16) for mine in heads]
        tots = [tot_ref[:, hh * LANES:hh * LANES + 1] for hh in range(2)]

        def blocks(kbs, carry, diagonal):
            dq = carry[0]
            nb = len(kbs)
            chains = [(b, hh) for b in range(nb) for hh in range(2)]
            offs = [pl.multiple_of(kb * ATT_BLK, ATT_BLK) for kb in kbs]
            k_blks = [k_ref[pl.ds(off, ATT_BLK), :] for off in offs]
            vvs = [v_ref[pl.ds(off, ATT_BLK), :].astype(BF16) for off in offs]
            if any(diagonal):
                valid = (lax.broadcasted_iota(jnp.int32, (ATT_BLK, ATT_BLK), 1)
                         < lax.broadcasted_iota(jnp.int32, (ATT_BLK, ATT_BLK), 0))
            kks = {(b, hh): jnp.where(heads[hh], k_blks[b], 0.0).astype(BF16) for b, hh in chains}
            zs = {ch: _nt(qms[ch[1]], kks[ch]) for ch in chains}
            dws = {ch: _nt(doms[ch[1]], vvs[ch[0]]) for ch in chains}
            logs = {ch: _sb_logs(zs[ch]) for ch in chains}
            lbs = {ch: logs[ch][0] for ch in chains}
            l1m_all = {ch: logs[ch][1] for ch in chains}
            l1ms = {ch: jnp.where(valid, l1m_all[ch], 0.0) if diagonal[ch[0]] else l1m_all[ch] for ch in chains}
            pre, c_de = {}, {}
            for hh in range(2):
                pre[(0, hh)], c_de[(0, hh)] = carry[1 + 2 * hh], carry[2 + 2 * hh]
            for b, hh in chains:
                pre[(b + 1, hh)] = pre[(b, hh)] + jnp.sum(l1ms[(b, hh)], axis=-1, keepdims=True)
            prefix = {ch: _hi_lo_dot(l1ms[ch], tri_incl) for ch in chains}
            ws = {ch: jnp.exp(lbs[ch] + (tots[ch[1]] - (prefix[ch] + pre[ch]))) for ch in chains}
            ws = {ch: jnp.where(valid, ws[ch], 0.0) if diagonal[ch[0]] else ws[ch] for ch in chains}
            d_es = {ch: ws[ch] * dws[ch] for ch in chains}
            for b, hh in chains:
                c_de[(b + 1, hh)] = c_de[(b, hh)] + jnp.sum(d_es[(b, hh)], axis=-1, keepdims=True)
            dvs = [_tn(ws[(b, 0)].astype(BF16), doms[0]) + _tn(ws[(b, 1)].astype(BF16), doms[1]) for b in range(nb)]
            dl1ms = {ch: jnp.dot(d_es[ch].astype(BF16), tri_lt, preferred_element_type=F32) + c_de[ch] for ch in chains}
            dzs = {ch: d_es[ch] * jnp.exp(l1m_all[ch]) - dl1ms[ch] * jnp.exp(lbs[ch]) for ch in chains}
            dzs = {ch: jnp.where(valid, dzs[ch], 0.0) if diagonal[ch[0]] else dzs[ch] for ch in chains}
            dzs = {ch: dzs[ch].astype(BF16) for ch in chains}
            for ch in chains:
                dq = dq + jnp.dot(dzs[ch], kks[ch], preferred_element_type=F32)
            for b in range(nb):
                dk_ref[pl.ds(offs[b], ATT_BLK), :] += _tn(dzs[(b, 0)], qms[0]) + _tn(dzs[(b, 1)], qms[1])
                dv_ref[pl.ds(offs[b], ATT_BLK), :] += dvs[b]
            return (dq, pre[(nb, 0)], c_de[(nb, 0)], pre[(nb, 1)], c_de[(nb, 1)])

        zero = jnp.zeros((ATT_BLK, 1), F32)
        carry = lax.fori_loop(0, qi // 2, lambda pr, cr: blocks([2 * pr, 2 * pr + 1], cr, (False, False)),
                              (jnp.zeros((ATT_BLK, LANES), F32), zero, zero, zero, zero))
        carry = lax.cond(qi % 2 == 1, lambda cr: blocks([qi - 1, qi], cr, (False, True)),
                         lambda cr: blocks([qi], cr, (True,)), carry)
        dq_ref[...] = carry[0] * scale

    return pl.pallas_call(
        body, name="sb_attn_bwd", grid=(SB_HEADS // 2, nq),
        in_specs=[pl.BlockSpec((ATT_BLK, LANES), lambda p, i: (i, p)),
                  pl.BlockSpec((t, LANES), lambda p, i: (0, p)),
                  pl.BlockSpec((t, LANES), lambda p, i: (0, p)),
                  pl.BlockSpec((ATT_BLK, 2 * LANES), lambda p, i: (i, p)),
                  pl.BlockSpec((ATT_BLK, LANES), lambda p, i: (i, p))],
        out_specs=[pl.BlockSpec((ATT_BLK, LANES), lambda p, i: (i, p)),
                   pl.BlockSpec((t, LANES), lambda p, i: (0, p)),
                   pl.BlockSpec((t, LANES), lambda p, i: (0, p))],
        out_shape=[jax.ShapeDtypeStruct((t, SB_WIDTH), F32)] * 3,
        compiler_params=pltpu.CompilerParams(dimension_semantics=("arbitrary", "arbitrary")),
    )(q, k, v, tot, do)


@jax.custom_vjp
def _sb_attention(q, k, v):
    return _sb_fwd(q, k, v)[0]


def _sb_attention_fwd(q, k, v):
    o, tot = _sb_fwd(q, k, v)
    return o, (q, k, v, tot)


def _sb_attention_bwd(res, do):
    return tuple(_sb_bwd(*res, do))


_sb_attention.defvjp(_sb_attention_fwd, _sb_attention_bwd)


def _mla_fwd(qn, qr, kn, kr, v):
    t = qn.shape[0]
    nq = t // ATT_BLK
    scale = MLA_QK ** -0.5

    def body(qn_ref, qr_ref, kn_ref, kr_ref, v_ref, o_ref, lse_ref):
        qi = pl.program_id(1)
        lanes = [slice(hh * LANES, (hh + 1) * LANES) for hh in range(2)]
        qnb = [qn_ref[:, sl].astype(BF16) for sl in lanes]
        qrb = [qr_ref[:, sl].astype(BF16) for sl in lanes]

        def blocks(kbs, carry, diagonal):
            nb = len(kbs)
            chains = [(b, hh) for b in range(nb) for hh in range(2)]
            offs = [pl.multiple_of(kb * ATT_BLK, ATT_BLK) for kb in kbs]
            krbs = [kr_ref[pl.ds(off, ATT_BLK), :].astype(BF16) for off in offs]
            accs, ms, ls = [carry[0], carry[3]], [carry[1], carry[4]], [carry[2], carry[5]]
            ss = {(b, hh): (_nt(qnb[hh], kn_ref[pl.ds(offs[b], ATT_BLK), lanes[hh]].astype(BF16))
                            + _nt(qrb[hh], krbs[b])) * scale for b, hh in chains}
            if any(diagonal):
                causal = (lax.broadcasted_iota(jnp.int32, (ATT_BLK, ATT_BLK), 1)
                          <= lax.broadcasted_iota(jnp.int32, (ATT_BLK, ATT_BLK), 0))
                ss = {ch: jnp.where(causal, ss[ch], -jnp.inf) if diagonal[ch[0]] else ss[ch] for ch in chains}
            m_new = list(ms)
            for b, hh in chains:
                m_new[hh] = jnp.maximum(m_new[hh], jnp.max(ss[(b, hh)], axis=-1, keepdims=True))
            ps = {(b, hh): jnp.exp(ss[(b, hh)] - m_new[hh]) for b, hh in chains}
            alphas = [jnp.exp(ms[hh] - m_new[hh]) for hh in range(2)]
            pvs = {(b, hh): jnp.dot(ps[(b, hh)].astype(BF16), v_ref[pl.ds(offs[b], ATT_BLK), lanes[hh]].astype(BF16),
                                    preferred_element_type=F32) for b, hh in chains}
            out = []
            for hh in range(2):
                acc, l = accs[hh] * alphas[hh], ls[hh] * alphas[hh]
                for b in range(nb):
                    acc, l = acc + pvs[(b, hh)], l + jnp.sum(ps[(b, hh)], axis=-1, keepdims=True)
                out += [acc, m_new[hh], l]
            return tuple(out)

        init = (jnp.zeros((ATT_BLK, LANES), F32), jnp.full((ATT_BLK, 1), -jnp.inf, F32), jnp.zeros((ATT_BLK, 1), F32))
        carry = lax.cond(qi % 2 == 1, lambda cr: blocks([qi, qi - 1], cr, (True, False)),
                         lambda cr: blocks([qi], cr, (True,)), init + init)
        carry = lax.fori_loop(0, qi // 2, lambda pr, cr: blocks([2 * pr, 2 * pr + 1], cr, (False, False)), carry)
        for hh in range(2):
            acc, m, l = carry[3 * hh:3 * hh + 3]
            o_ref[:, lanes[hh]] = acc / l
            lse_ref[:, lanes[hh]] = jnp.broadcast_to(m + jnp.log(l), (ATT_BLK, LANES))

    blk = pl.BlockSpec((ATT_BLK, 2 * LANES), lambda p, i: (i, p))
    full = pl.BlockSpec((t, 2 * LANES), lambda p, i: (0, p))
    return pl.pallas_call(
        body, name="mla_attn_fwd", grid=(MLA_HEADS // 2, nq),
        in_specs=[blk, blk, full, pl.BlockSpec((t, LANES), lambda p, i: (0, 0)), full],
        out_specs=[blk, blk],
        out_shape=[jax.ShapeDtypeStruct((t, MLA_HEADS * LANES), F32)] * 2,
        compiler_params=pltpu.CompilerParams(dimension_semantics=("arbitrary", "arbitrary")),
    )(qn, qr, kn, kr, v)


def _mla_bwd(qn, qr, kn, kr, v, o, lse, do):
    t = qn.shape[0]
    nq = t // ATT_BLK
    scale = MLA_QK ** -0.5

    def body(qn_ref, qr_ref, kn_ref, kr_ref, v_ref, o_ref, lse_ref, do_ref,
             dqn_ref, dqr_ref, dkn_ref, dkr_ref, dv_ref):
        pair = pl.program_id(0)
        qi = pl.program_id(1)

        @pl.when(qi == 0)
        def _():
            dkn_ref[...] = jnp.zeros_like(dkn_ref)
            dv_ref[...] = jnp.zeros_like(dv_ref)

        @pl.when((qi == 0) & (pair == 0))
        def _():
            dkr_ref[...] = jnp.zeros_like(dkr_ref)

        lanes = [slice(hh * LANES, (hh + 1) * LANES) for hh in range(2)]
        qnb = [qn_ref[:, sl].astype(BF16) for sl in lanes]
        qrb = [qr_ref[:, sl].astype(BF16) for sl in lanes]
        dob = [do_ref[:, sl].astype(BF16) for sl in lanes]
        delta = [jnp.sum(do_ref[:, sl] * o_ref[:, sl], axis=-1, keepdims=True) for sl in lanes]
        lse_v = [lse_ref[:, hh * LANES:hh * LANES + 1] for hh in range(2)]

        def blocks(kbs, carry, diagonal):
            nb = len(kbs)
            chains = [(b, hh) for b in range(nb) for hh in range(2)]
            offs = [pl.multiple_of(kb * ATT_BLK, ATT_BLK) for kb in kbs]
            krbs = [kr_ref[pl.ds(off, ATT_BLK), :].astype(BF16) for off in offs]
            knb = {(b, hh): kn_ref[pl.ds(offs[b], ATT_BLK), lanes[hh]].astype(BF16) for b, hh in chains}
            vb = {(b, hh): v_ref[pl.ds(offs[b], ATT_BLK), lanes[hh]].astype(BF16) for b, hh in chains}
            ss = {(b, hh): _nt(qnb[hh], knb[(b, hh)]) + _nt(qrb[hh], krbs[b]) for b, hh in chains}
            dps = {(b, hh): _nt(dob[hh], vb[(b, hh)]) for b, hh in chains}
            ps = {(b, hh): jnp.exp(ss[(b, hh)] * scale - lse_v[hh]) for b, hh in chains}
            if any(diagonal):
                causal = (lax.broadcasted_iota(jnp.int32, (ATT_BLK, ATT_BLK), 1)
                          <= lax.broadcasted_iota(jnp.int32, (ATT_BLK, ATT_BLK), 0))
                ps = {ch: jnp.where(causal, ps[ch], 0.0) if diagonal[ch[0]] else ps[ch] for ch in chains}
            dss = {(b, hh): (ps[(b, hh)] * (dps[(b, hh)] - delta[hh]) * scale).astype(BF16) for b, hh in chains}
            for b, hh in chains:
                dv_ref[pl.ds(offs[b], ATT_BLK), lanes[hh]] += _tn(ps[(b, hh)].astype(BF16), dob[hh])
            for b, hh in chains:
                dkn_ref[pl.ds(offs[b], ATT_BLK), lanes[hh]] += _tn(dss[(b, hh)], qnb[hh])
            for b in range(nb):
                dkr_ref[pl.ds(offs[b], ATT_BLK), :] += _tn(dss[(b, 0)], qrb[0]) + _tn(dss[(b, 1)], qrb[1])
            out = list(carry)
            for b, hh in chains:
                out[2 * hh] = out[2 * hh] + jnp.dot(dss[(b, hh)], knb[(b, hh)], preferred_element_type=F32)
                out[2 * hh + 1] = out[2 * hh + 1] + jnp.dot(dss[(b, hh)], krbs[b], preferred_element_type=F32)
            return tuple(out)

        zero = jnp.zeros((ATT_BLK, LANES), F32)
        carry = lax.fori_loop(0, qi // 2, lambda pr, cr: blocks([2 * pr, 2 * pr + 1], cr, (False, False)),
                              (zero, zero, zero, zero))
        carry = lax.cond(qi % 2 == 1, lambda cr: blocks([qi - 1, qi], cr, (False, True)),
                         lambda cr: blocks([qi], cr, (True,)), carry)
        for hh in range(2):
            dqn_ref[:, lanes[hh]] = carry[2 * hh]
            dqr_ref[:, lanes[hh]] = carry[2 * hh + 1]

    blk = pl.BlockSpec((ATT_BLK, 2 * LANES), lambda p, i: (i, p))
    full = pl.BlockSpec((t, 2 * LANES), lambda p, i: (0, p))
    shared = pl.BlockSpec((t, LANES), lambda p, i: (0, 0))
    wide = jax.ShapeDtypeStruct((t, MLA_HEADS * LANES), F32)
    return pl.pallas_call(
        body, name="mla_attn_bwd", grid=(MLA_HEADS // 2, nq),
        in_specs=[blk, blk, full, shared, full, blk, blk, blk],
        out_specs=[blk, blk, full, shared, full],
        out_shape=[wide, wide, wide, jax.ShapeDtypeStruct((t, LANES), F32), wide],
        compiler_params=pltpu.CompilerParams(dimension_semantics=("arbitrary", "arbitrary")),
    )(qn, qr, kn, kr, v, o, lse, do)


@jax.custom_vjp
def _mla_attention(qn, qr, kn, kr, v):
    return _mla_fwd(qn, qr, kn, kr, v)[0]


def _mla_attention_fwd(qn, qr, kn, kr, v):
    o, lse = _mla_fwd(qn, qr, kn, kr, v)
    return o, (qn, qr, kn, kr, v, o, lse)


def _mla_attention_bwd(res, do):
    return tuple(_mla_bwd(*res, do))


_mla_attention.defvjp(_mla_attention_fwd, _mla_attention_bwd)


def _ffn_in(h, wg, wu):
    t, k = h.shape
    n_sh, _, cc = wg.shape

    def body(h_ref, wg_ref, wu_ref, g_ref, u_ref, a_ref):
        hb = h_ref[...].astype(BF16)
        for j in range(n_sh):
            cols = slice(j * cc, (j + 1) * cc)
            g = jnp.dot(hb, wg_ref[j], preferred_element_type=F32)
            u = jnp.dot(hb, wu_ref[j], preferred_element_type=F32)
            g_ref[:, cols] = g
            u_ref[:, cols] = u
            a_ref[:, cols] = _f_swiglu(g, u)[0].astype(BF16)

    w_spec = pl.BlockSpec((n_sh, k, cc), lambda i: (0, 0, 0))
    o_spec = pl.BlockSpec((ROW_TILE, n_sh * cc), lambda i: (i, 0))
    wide = (t, n_sh * cc)
    return pl.pallas_call(
        body, name="ffn_in_fwd", grid=(t // ROW_TILE,),
        in_specs=[pl.BlockSpec((ROW_TILE, k), lambda i: (i, 0)), w_spec, w_spec],
        out_specs=[o_spec, o_spec, o_spec],
        out_shape=[jax.ShapeDtypeStruct(wide, F32), jax.ShapeDtypeStruct(wide, F32), jax.ShapeDtypeStruct(wide, BF16)],
        compiler_params=pltpu.CompilerParams(dimension_semantics=("arbitrary",), vmem_limit_bytes=MM_VMEM_LIMIT),
    )(h, wg, wu)


def _ffn_mid_bwd(dy, wd, g, u):
    t, n = dy.shape
    n_sh, cc, _ = wd.shape

    def body(dy_ref, wd_ref, g_ref, u_ref, dg_ref, du_ref):
        d_act = _nt(dy_ref[...].astype(BF16), wd_ref[...])
        _, vjp = jax.vjp(_f_swiglu, g_ref[...], u_ref[...])
        dg, du = vjp((d_act,))
        dg_ref[...] = dg.astype(BF16)
        du_ref[...] = du.astype(BF16)

    blk = pl.BlockSpec((MM_ROW_TILE, cc), lambda j, i: (i, j))
    wide = jax.ShapeDtypeStruct((t, n_sh * cc), BF16)
    return pl.pallas_call(
        body, name="ffn_mid_bwd", grid=(n_sh, t // MM_ROW_TILE),
        in_specs=[pl.BlockSpec((MM_ROW_TILE, n), lambda j, i: (i, 0)),
                  pl.BlockSpec((None, cc, n), lambda j, i: (j, 0, 0)), blk, blk],
        out_specs=[blk, blk], out_shape=[wide, wide],
        compiler_params=pltpu.CompilerParams(dimension_semantics=("arbitrary", "arbitrary"),
                                             vmem_limit_bytes=MM_VMEM_LIMIT),
    )(dy, wd, g, u)


def _ffn_dh(dg, du, wg, wu):
    t = dg.shape[0]
    n_sh, k, cc = wg.shape

    def body(dg_ref, du_ref, wg_ref, wu_ref, o_ref):
        acc = jnp.zeros((MM_ROW_TILE, k), F32)
        for j in range(n_sh):
            cols = slice(j * cc, (j + 1) * cc)
            acc = acc + _nt(dg_ref[:, cols], wg_ref[j]) + _nt(du_ref[:, cols], wu_ref[j])
        o_ref[...] = acc

    blk = pl.BlockSpec((MM_ROW_TILE, n_sh * cc), lambda i: (i, 0))
    w_spec = pl.BlockSpec((n_sh, k, cc), lambda i: (0, 0, 0))
    return pl.pallas_call(
        body, name="ffn_dh", grid=(t // MM_ROW_TILE,),
        in_specs=[blk, blk, w_spec, w_spec],
        out_specs=pl.BlockSpec((MM_ROW_TILE, k), lambda i: (i, 0)),
        out_shape=jax.ShapeDtypeStruct((t, k), F32),
        compiler_params=pltpu.CompilerParams(dimension_semantics=("arbitrary",), vmem_limit_bytes=MM_VMEM_LIMIT),
    )(dg, du, wg, wu)


def _ffn_dw_in(h, dy, n_sh, name):
    t, k = h.shape
    cc = dy.shape[1] // n_sh
    tk = 512

    def body(h_ref, dy_ref, o_ref):
        o_ref[...] = _tn(h_ref[...].astype(BF16), dy_ref[...]).astype(BF16)

    return pl.pallas_call(
        body, name=name, grid=(n_sh, k // tk),
        in_specs=[pl.BlockSpec((t, tk), lambda j, i: (0, i)), pl.BlockSpec((t, cc), lambda j, i: (0, j))],
        out_specs=pl.BlockSpec((None, tk, cc), lambda j, i: (j, i, 0)),
        out_shape=jax.ShapeDtypeStruct((n_sh, k, cc), BF16),
        compiler_params=pltpu.CompilerParams(dimension_semantics=("arbitrary", "arbitrary"),
                                             vmem_limit_bytes=MM_VMEM_LIMIT),
    )(h, dy)


@jax.custom_vjp
def _ffn_block(h, wg, wu, wd):
    act = _ffn_in(h, wg, wu)[2]
    return _mm(act, wd.reshape(-1, wd.shape[2]), "nn", "ffn_down_fwd", MM_ROW_TILE, wd.shape[2])


def _ffn_block_fwd(h, wg, wu, wd):
    g, u, act = _ffn_in(h, wg, wu)
    y = _mm(act, wd.reshape(-1, wd.shape[2]), "nn", "ffn_down_fwd", MM_ROW_TILE, wd.shape[2])
    return y, (h, wg, wu, wd, g, u, act)


def _ffn_block_bwd(res, dy):
    h, wg, wu, wd, g, u, act = res
    dg, du = _ffn_mid_bwd(dy, wd, g, u)
    dh = _ffn_dh(dg, du, wg, wu)
    n_sh = wg.shape[0]
    dwg = _ffn_dw_in(h, dg, n_sh, "ffn_gate_dw")
    dwu = _ffn_dw_in(h, du, n_sh, "ffn_up_dw")
    dwd = _mm(act, dy, "tn", "ffn_down_dw", 256, wd.shape[2], out_dtype=BF16).reshape(wd.shape)
    return dh, dwg, dwu, dwd


_ffn_block.defvjp(_ffn_block_fwd, _ffn_block_bwd)


def _swap_halves(w):
    half = w.shape[-1] // 2
    return jnp.concatenate([w[..., half:], w[..., :half]], axis=-1)


def _pad_lanes(w):
    return jnp.concatenate([w, jnp.zeros(w.shape[:-1] + (LANES - w.shape[-1],), w.dtype)], axis=-1)


def _join_cols(shards):
    return shards.transpose(1, 0, 2).reshape(shards.shape[1], -1)


def _mod_parts(mod):
    return [mod[:, i * D_MODEL:(i + 1) * D_MODEL] for i in range(N_MOD)]


def _local_loss(x, mod, p, cos, sin, target):
    return _ffn_stage(x, _mixing_stage(x, mod, p, cos, sin), mod, p, target)


def _mixing_stage(x, mod, p, cos, sin):
    shift1, scale1 = _mod_parts(mod)[:2]

    w_in = _join_cols(p["w_in"])
    k_rope_w = w_in[:, 2176:2240]
    w_in_ext = jnp.concatenate([w_in[:, :2176], _pad_lanes(k_rope_w), _pad_lanes(_swap_halves(k_rope_w)),
                                jnp.zeros((D_MODEL, LANES), w_in.dtype)], axis=1)
    (h1,) = _make_rowwise("pre_attn", _f_pre_attn, 1, 3, [D_MODEL], [True], out_dtypes=[BF16])(
        x, p["norm_attn"], scale1, shift1)
    q_sb, k_sb, v_sb, cq, ckv, kr, kr_sw = _make_linear_split(
        "in_proj", (SB_WIDTH, SB_WIDTH, SB_WIDTH, MLA_Q_RANK, MLA_KV_RANK, LANES, LANES), 512)(h1, w_in_ext)

    o_sb = _sb_attention(q_sb, k_sb, v_sb)

    wq = _join_cols(p["w_q_up"]).reshape(MLA_Q_RANK, MLA_HEADS, MLA_QK)
    wq_n, wq_r = wq[:, :, :MLA_NOPE], wq[:, :, MLA_NOPE:]
    w_q_ext = jnp.concatenate([wq_n.reshape(MLA_Q_RANK, -1), _pad_lanes(wq_r).reshape(MLA_Q_RANK, -1),
                               _pad_lanes(_swap_halves(wq_r)).reshape(MLA_Q_RANK, -1)], axis=1)
    wkv = _join_cols(p["w_kv_up"]).reshape(MLA_KV_RANK, MLA_HEADS, MLA_NOPE + MLA_V)
    w_kv_ext = jnp.concatenate([wkv[:, :, :MLA_NOPE].reshape(MLA_KV_RANK, -1),
                                wkv[:, :, MLA_NOPE:].reshape(MLA_KV_RANK, -1)], axis=1)
    cqn, ckvn = _make_rowwise("mla_a", _f_mla_a, 2, 2, [MLA_Q_RANK, MLA_KV_RANK], [True, True],
                              out_dtypes=[BF16, BF16], grad_dtypes=[BF16, BF16])(
        cq, ckv, p["q_a_norm"], p["kv_a_norm"])
    qall = _make_linear("q_up", 384, 768)(cqn, w_q_ext)
    kn_all, v_mla = _make_linear_split("kv_up", (MLA_HEADS * MLA_NOPE, MLA_HEADS * MLA_V), MLA_KV_RANK)(ckvn, w_kv_ext)
    gq = p["q_norm"]
    gkr = p["k_rope_norm"]
    qn, qr, kn, krr = _make_rowwise("mla_b", _f_mla_b, 6, 6, [512, 512, 512, LANES],
                                    [True, True, True, True, False, False],
                                    out_dtypes=[BF16] * 4, grad_dtypes=[BF16] * 4)(
        qall, kn_all, kr, kr_sw, cos, sin,
        gq[:, :MLA_NOPE], _pad_lanes(gq[:, MLA_NOPE:]), _pad_lanes(_swap_halves(gq[:, MLA_NOPE:])),
        p["k_nope_norm"], _pad_lanes(gkr), _pad_lanes(_swap_halves(gkr)))
    o_mla = _mla_attention(qn, qr, kn, krr, v_mla)

    (mixed,) = _make_rowwise("post_attn", _f_post_attn, 2, 2, [D_MODEL], [True, True])(
        o_sb, o_mla, p["out_norm_sb"], p["out_norm_mla"])
    return mixed


def _ffn_stage(x, mixed, mod, p, target):
    _, _, gate1, shift2, scale2, gate2 = _mod_parts(mod)
    attn = _make_linear("out_proj", 512, 512)(mixed, p["w_out"].reshape(D_MODEL, D_MODEL))

    x2, h2 = _make_rowwise("pre_ffn", _f_pre_ffn, 2, 4, [D_MODEL, D_MODEL], [True, True],
                           out_dtypes=[F32, BF16], grad_dtypes=[F32, BF16])(
        x, attn, gate1, p["norm_ffn"], scale2, shift2)
    ffn = _ffn_block(h2, p["w_gate"], p["w_up"], p["w_down"])
    (row_loss,) = _make_rowwise("loss", _f_loss, 3, 1, [1], [True, True, False], grad_dtypes=[F32, BF16])(
        x2, ffn, target, gate2)
    return 0.5 * jnp.sum(row_loss)


def _my_place():
    return lax.axis_index("x"), lax.axis_index("y"), lax.axis_index("c")


def _all_gather_small(block, name):
    m_per, n = block.shape

    def body(x_ref, out_ref, send_sems, recv_sems, local_sem):
        x, y, c = _my_place()
        me, sibling = (x, y, c), (x, y, 1 - c)
        chips = [(1 - x, y), (x, 1 - y), (1 - x, 1 - y)]

        def rows(px, py, pc):
            return out_ref.at[pl.ds((4 * px + 2 * py + pc) * m_per, m_per), :]

        def copy(k, blk, to, src=None):
            return pltpu.make_async_remote_copy(
                src_ref=rows(*blk) if src is None else src, dst_ref=rows(*blk),
                send_sem=send_sems.at[k], recv_sem=recv_sems.at[k], device_id=to, device_id_type=MESH)

        mine = pltpu.make_async_copy(x_ref, rows(*me), local_sem)
        mine.start()
        first = [copy(0, me, sibling, src=x_ref)]
        first += [copy(1 + j, me, (*chip, c), src=x_ref) for j, chip in enumerate(chips)]
        for cp in first:
            cp.start()
        passed = [copy(4 + j, (*chip, c), sibling) for j, chip in enumerate(chips)]
        for j, chip in enumerate(chips):
            copy(1 + j, (*chip, c), me).wait_recv()
            passed[j].start()
        copy(0, sibling, me).wait_recv()
        for j, chip in enumerate(chips):
            copy(4 + j, (*chip, 1 - c), me).wait_recv()
        for cp in first + passed:
            cp.wait_send()
        mine.wait()

    return pl.pallas_call(
        body, name=name,
        out_shape=jax.ShapeDtypeStruct((N_DEV * m_per, n), block.dtype),
        in_specs=[pl.BlockSpec(memory_space=pltpu.VMEM)],
        out_specs=pl.BlockSpec(memory_space=pltpu.VMEM),
        scratch_shapes=[pltpu.SemaphoreType.DMA((7,)), pltpu.SemaphoreType.DMA((7,)), pltpu.SemaphoreType.DMA],
    )(block)


EARLY = ("w_in", "w_q_up", "w_kv_up")
LATE = ("w_out", "w_gate", "w_up", "w_down")
BIG = EARLY + LATE
TRANSPOSED_UPDATE = ("w_in", "w_gate", "w_up")
HALF_AXIS = {"w_in": 0, "w_q_up": 0, "w_kv_up": 0, "w_out": 0, "w_gate": 0, "w_up": 0, "w_down": 1}


def _half(ref, h, axis, lead=()):
    trail = ref.shape[len(lead):]
    idx = list(lead) + [slice(None)] * len(trail)
    at = len(trail) - 2 + axis
    n2 = trail[at] // 2
    idx[len(lead) + at] = pl.ds(h * n2, n2)
    return ref.at[tuple(idx)]


def _half_shape(shape, axis):
    shape = list(shape)
    shape[len(shape) - 2 + axis] //= 2
    return tuple(shape)


def _remote(src, dst, send_sems, recv_sems, k, to):
    return pltpu.make_async_remote_copy(src_ref=src, dst_ref=dst, send_sem=send_sems.at[k],
                                        recv_sem=recv_sems.at[k], device_id=to, device_id_type=MESH)


def _gather_weights(names, shards, after):
    n_w = len(shards)
    axes = [HALF_AXIS[n] for n in names]

    def body(*refs):
        w_refs, out_refs, token = refs[:n_w], refs[n_w + 1:2 * n_w + 1], refs[2 * n_w + 1]
        send_sems, recv_sems, local_sems = refs[2 * n_w + 2:]
        token[...] = jnp.zeros_like(token)
        x, y, c = _my_place()
        sibling = (x, y, 1 - c)
        chips = [(1 - x, y), (x, 1 - y), (1 - x, 1 - y)]
        me = 2 * x + y
        mine =[pltpu.make_async_copy(w, o.at[me], local_sems.at[i]) for i, (w, o) in enumerate(zip(w_refs, out_refs))]
        for cp in mine:
            cp.start()
        first = [_remote(_half(w_refs[i], c, axes[i]), _half(out_refs[i], c, axes[i], (me,)),
                         send_sems, recv_sems, 6 * i + j, (*chip, c))
                 for i in range(n_w) for j, chip in enumerate(chips)]
        for cp in first:
            cp.start()
        passed = []
        for j, (cx, cy) in enumerate(chips):
            for i in range(n_w):
                blk = _half(out_refs[i], c, axes[i], (2 * cx + cy,))
                _remote(blk, blk, send_sems, recv_sems, 6 * i + j, (cx, cy, c)).wait_recv()
                cp = _remote(blk, blk, send_sems, recv_sems, 6 * i + 3 + j, sibling)
                cp.start()
                passed.append(cp)
        for j, (cx, cy) in enumerate(chips):
            for i in range(n_w):
                blk = _half(out_refs[i], 1 - c, axes[i], (2 * cx + cy,))
                _remote(blk, blk, send_sems, recv_sems, 6 * i + 3 + j, sibling).wait_recv()
        for cp in first + passed:
            cp.wait_send()
        for cp in mine:
            cp.wait()

    outs = pl.pallas_call(
        body, name="gather_weights",
        out_shape=[jax.ShapeDtypeStruct((N_CHIPS,) + s.shape, s.dtype) for s in shards]
        + [jax.ShapeDtypeStruct((8, LANES), F32)],
        in_specs=[ANY] * (n_w + 1), out_specs=[ANY] * n_w + [pl.BlockSpec(memory_space=pltpu.VMEM)],
        scratch_shapes=[pltpu.SemaphoreType.DMA((6 * n_w,)), pltpu.SemaphoreType.DMA((6 * n_w,)),
                        pltpu.SemaphoreType.DMA((n_w,))],
    )(*shards, after)
    return outs[:n_w], outs[n_w]


def _pair_exchange(names, grads, call_name):
    n_w = len(grads)
    axes = [HALF_AXIS[n] for n in names]

    def body(*refs):
        g_refs, t_refs = refs[:n_w], refs[n_w:2 * n_w]
        send_sems, recv_sems = refs[2 * n_w:]
        x, y, c = _my_place()
        sends = [_remote(_half(g_refs[i], 1 - c, axes[i]), t_refs[i], send_sems, recv_sems, i, (x, y, 1 - c))
                 for i in range(n_w)]
        for cp in sends:
            cp.start()
        for cp in sends:
            cp.wait_recv()
        for cp in sends:
            cp.wait_send()

    return pl.pallas_call(
        body, name=call_name,
        out_shape=[jax.ShapeDtypeStruct(_half_shape(g.shape, a), g.dtype) for g, a in zip(grads, axes)],
        in_specs=[ANY] * n_w, out_specs=[ANY] * n_w,
        scratch_shapes=[pltpu.SemaphoreType.DMA((n_w,)), pltpu.SemaphoreType.DMA((n_w,))],
    )(*grads)


def _sibling_join(halves, name, after):
    n_w = len(halves)

    def body(*refs):
        s_refs, j_refs = refs[:n_w], refs[n_w + 1:2 * n_w + 1]
        send_sems, recv_sems = refs[2 * n_w + 1:]
        x, y, c = _my_place()
        sends = [_remote(s_refs[i], j_refs[i], send_sems, recv_sems, i, (x, y, 1 - c)) for i in range(n_w)]
        for cp in sends:
            cp.start()
        for cp in sends:
            cp.wait_recv()
        for cp in sends:
            cp.wait_send()

    return pl.pallas_call(
        body, name=name,
        out_shape=[jax.ShapeDtypeStruct(s.shape, s.dtype) for s in halves],
        in_specs=[ANY] * (n_w + 1), out_specs=[ANY] * n_w,
        scratch_shapes=[pltpu.SemaphoreType.DMA((n_w,)), pltpu.SemaphoreType.DMA((n_w,))],
    )(*halves, after)


HBM_SPEC = pl.BlockSpec(memory_space=pltpu.HBM)
SEM_SPEC = pl.BlockSpec(memory_space=pltpu.SEMAPHORE)
DATAFLOW = pltpu.SideEffectType.DATAFLOW_SIDE_EFFECTING


def _in_hbm(a):
    return pltpu.with_memory_space_constraint(a, pltpu.HBM)


def _exchange_start(name, srcs, lands, plan, n_copies, after, thru):
    n = len(srcs)

    def body(*refs):
        src_refs, land_refs = refs[:n], refs[n:2 * n]
        send_sems, recv_sems = refs[2 * n + 2], refs[2 * n + 3]
        for k, (src, dst, to, k_recv) in enumerate(plan(src_refs, land_refs)):
            pltpu.make_async_remote_copy(src_ref=src, dst_ref=dst, send_sem=send_sems.at[k],
                                         recv_sem=recv_sems.at[k_recv], device_id=to, device_id_type=MESH).start()

    outs = pl.pallas_call(
        body, name=name,
        out_shape=(pltpu.SemaphoreType.DMA((n_copies,)), pltpu.SemaphoreType.DMA((n_copies,)),
                   *[pltpu.HBM(a.shape, a.dtype) for a in list(srcs) + list(lands) + [thru]]),
        in_specs=[HBM_SPEC] * (2 * n + 1) + [ANY],
        out_specs=(SEM_SPEC, SEM_SPEC, *[HBM_SPEC] * (2 * n + 1)),
        input_output_aliases={i: 2 + i for i in range(2 * n + 1)},
        compiler_params=pltpu.CompilerParams(has_side_effects=DATAFLOW),
    )(*[_in_hbm(a) for a in list(srcs) + list(lands) + [thru]], after)
    return outs[0], outs[1], outs[2:2 + n], outs[2 + n:2 + 2 * n], outs[2 + 2 * n]


def _exchange_wait(name, started, plan, after):
    send_sems, recv_sems, srcs, lands, _ = started
    n = len(srcs)

    def body(*refs):
        src_refs, land_refs = refs[:n], refs[n:2 * n]
        s_sems, r_sems = refs[2 * n], refs[2 * n + 1]
        for k, (src, dst, to, _) in enumerate(plan(src_refs, land_refs)):
            cp = _remote(src, dst, s_sems, r_sems, k, to)
            cp.wait_send()
            cp.wait_recv()

    outs = pl.pallas_call(
        body, name=name,
        out_shape=tuple(pltpu.HBM(a.shape, a.dtype) for a in list(srcs) + list(lands)),
        in_specs=[HBM_SPEC] * (2 * n) + [SEM_SPEC, SEM_SPEC, ANY],
        out_specs=tuple([HBM_SPEC] * (2 * n)),
        input_output_aliases={i: i for i in range(2 * n)},
        compiler_params=pltpu.CompilerParams(has_side_effects=DATAFLOW),
    )(*srcs, *lands, send_sems, recv_sems, after)
    return outs[:n], outs[n:]


def _late_gather_plan(src_refs, land_refs):
    x, y, c = _my_place()
    chips = [(1 - x, y), (x, 1 - y), (1 - x, 1 - y)]
    plan = [(src, land.at[2 * x + y], (cx, cy, c)) for src, land in zip(src_refs, land_refs) for cx, cy in chips]
    return [entry + (k,) for k, entry in enumerate(plan)]


def _late_scatter_plan(src_refs, land_refs):
    x, y, c = _my_place()
    chips = [(1 - x, y), (x, 1 - y), (1 - x, 1 - y)]
    plan = [(src.at[2 * cx + cy], land.at[j], (cx, cy, c))
            for src, land in zip(src_refs, land_refs) for j, (cx, cy) in enumerate(chips)]
    return [entry + (k,) for k, entry in enumerate(plan)]


def _direct_scatter_plan(names):
    axes = [HALF_AXIS[n] for n in names]

    def plan(src_refs, land_refs):
        x, y, c = _my_place()
        chips = [(1 - x, y), (x, 1 - y), (1 - x, 1 - y)]
        out = []
        for i, (src, land) in enumerate(zip(src_refs, land_refs)):
            for f, (cx, cy) in enumerate(chips):
                for core in range(2):
                    out.append((_half(src, core, axes[i], (2 * cx + cy,)), land.at[2 * f + c], (cx, cy, core),
                                7 * i + 2 * f + c))
            out.append((_half(src, 1 - c, axes[i], (2 * x + y,)), land.at[6], (x, y, 1 - c), 7 * i + 6))
        return out

    return plan


def _row_tile(rows, mult=16, limit=ROW_TILE):
    return max(d for d in range(mult, limit + 1, mult) if rows % d == 0)


def _pair_sum(place, g, theirs, axis, name):
    nj, rr, cc = theirs.shape
    tr = _row_tile(rr, limit=1024)
    nb = rr // tr
    if axis == 0:
        g_map = lambda j, i, pr: (j, pr[0] * nb + i, 0)
    else:
        g_map = lambda j, i, pr: (j, i, pr[0])

    def body(pr, g_ref, t_ref, o_ref):
        o_ref[...] = (g_ref[...].astype(F32) + t_ref[...].astype(F32)).astype(BF16)

    spec = pl.BlockSpec((None, tr, cc), lambda j, i, pr: (j, i, 0))
    return pl.pallas_call(
        body, name=name,
        grid_spec=pltpu.PrefetchScalarGridSpec(
            num_scalar_prefetch=1, grid=(nj, nb),
            in_specs=[pl.BlockSpec((None, tr, cc), g_map), spec], out_specs=spec),
        out_shape=jax.ShapeDtypeStruct(theirs.shape, BF16))(place, g, theirs)


def _chip_sum(place, pair_sums, parts, name, transposed):
    _, rr, cc = parts.shape
    tr = _row_tile(rr, LANES) if transposed else _row_tile(rr, limit=1024)

    def body(pr, h_ref, p_ref, o_ref):
        acc = p_ref[0].astype(F32)
        for j in range(1, N_CHIPS - 1):
            acc = acc + p_ref[j].astype(F32)
        acc = acc + h_ref[...].astype(F32)
        o_ref[...] = (acc.T if transposed else acc).astype(BF16)

    out_spec = pl.BlockSpec((cc, tr), lambda i, pr: (0, i)) if transposed else pl.BlockSpec((tr, cc), lambda i, pr: (i, 0))
    return pl.pallas_call(
        body, name=name,
        grid_spec=pltpu.PrefetchScalarGridSpec(
            num_scalar_prefetch=1, grid=(rr // tr,),
            in_specs=[pl.BlockSpec((None, tr, cc), lambda i, pr: (pr[1], i, 0)),
                      pl.BlockSpec((N_CHIPS - 1, tr, cc), lambda i, pr: (0, i, 0))],
            out_specs=out_spec),
        out_shape=jax.ShapeDtypeStruct((cc, rr) if transposed else (rr, cc), BF16))(place, pair_sums, parts)


def _chip_sum_direct(place, g, parts, axis, name, transposed):
    n_parts, rr, cc = parts.shape
    tr = _row_tile(rr, LANES) if transposed else _row_tile(rr, limit=1024)
    nb = rr // tr
    if axis == 0:
        g_map = lambda i, pr: (pr[1], pr[0] * nb + i, 0)
    else:
        g_map = lambda i, pr: (pr[1], i, pr[0])

    def body(pr, g_ref, p_ref, o_ref):
        acc = p_ref[0].astype(F32)
        for j in range(1, n_parts):
            acc = acc + p_ref[j].astype(F32)
        acc = acc + g_ref[...].astype(F32)
        o_ref[...] = (acc.T if transposed else acc).astype(BF16)

    out_spec = pl.BlockSpec((cc, tr), lambda i, pr: (0, i)) if transposed else pl.BlockSpec((tr, cc), lambda i, pr: (i, 0))
    return pl.pallas_call(
        body, name=name,
        grid_spec=pltpu.PrefetchScalarGridSpec(
            num_scalar_prefetch=1, grid=(nb,),
            in_specs=[pl.BlockSpec((None, tr, cc), g_map), pl.BlockSpec((n_parts, tr, cc), lambda i, pr: (0, i, 0))],
            out_specs=out_spec),
        out_shape=jax.ShapeDtypeStruct((cc, rr) if transposed else (rr, cc), BF16))(place, g, parts)


def _silu(v):
    return v / (1.0 + jnp.exp(-v))


def _ada_fwd(c_all, w_shard, b_shard):
    def body(c_ref, w_ref, b_ref, o_ref):
        o_ref[...] = jnp.dot(_silu(c_ref[...]), w_ref[...], precision=lax.Precision.HIGHEST,
                             preferred_element_type=F32) + b_ref[...]

    return pl.pallas_call(body, name="ada_fwd", out_shape=jax.ShapeDtypeStruct((c_all.shape[0], w_shard.shape[1]), F32),
                          compiler_params=pltpu.CompilerParams(vmem_limit_bytes=MM_VMEM_LIMIT))(c_all, w_shard, b_shard)


def _ada_bwd(c_all, dmod_cols):
    def body(c_ref, d_ref, o_ref):
        o_ref[...] = lax.dot_general(_silu(c_ref[...]), d_ref[...], (((0,), (0,)), ((), ())),
                                     precision=lax.Precision.HIGHEST, preferred_element_type=F32)

    return pl.pallas_call(body, name="ada_bwd", out_shape=jax.ShapeDtypeStruct((c_all.shape[1], dmod_cols.shape[1]), F32),
                          compiler_params=pltpu.CompilerParams(vmem_limit_bytes=MM_VMEM_LIMIT))(c_all, dmod_cols)


def _adamw_math(w, g, m, v):
    m = ADAM_B1 * m + (1.0 - ADAM_B1) * g
    v = ADAM_B2 * v + (1.0 - ADAM_B2) * (g * g)
    m_hat = m / (1.0 - ADAM_B1 ** ADAM_STEP)
    v_hat = v / (1.0 - ADAM_B2 ** ADAM_STEP)
    delta = -ADAM_LR * (m_hat / (jnp.sqrt(v_hat) + ADAM_EPS) + ADAM_WD * w)
    return delta, m, v


def _adamw(w, g, m, v, name):
    r, ccols = w.shape
    tr = max(d for d in range(8, ROW_TILE + 1, 8) if r % d == 0)
    spec = pl.BlockSpec((tr, ccols), lambda i: (i, 0))

    def body(w_ref, g_ref, m_ref, v_ref, d_ref, nm_ref, nv_ref):
        d_ref[...], nm_ref[...], nv_ref[...] = _adamw_math(w_ref[...], g_ref[...], m_ref[...], v_ref[...])

    return pl.pallas_call(body, name=name, grid=(r // tr,), in_specs=[spec] * 4, out_specs=[spec] * 3,
                          out_shape=[jax.ShapeDtypeStruct(w.shape, F32)] * 3,
                          compiler_params=pltpu.CompilerParams(vmem_limit_bytes=MM_VMEM_LIMIT))(w, g, m, v)


def _small_layout(sizes):
    offs, off = [], 0
    for n in sizes:
        offs.append(off)
        off += -(-n // LANES) * LANES
    total = -(-(off + LANES) // (8 * LANES)) * (8 * LANES)
    return offs, off, total


def _adamw_small(ws, g_all, ms, vs, offs, loss_off):
    n_p = len(ws)

    def device_sum(g_ref, off, width):
        blk = g_ref[:, off:off + width]
        acc = blk[0:1]
        for d in range(1, N_DEV):
            acc = acc + blk[d:d + 1]
        return acc

    def body(*refs):
        w_refs, m_refs, v_refs = refs[:n_p], refs[n_p:2 * n_p], refs[2 * n_p:3 * n_p]
        g_ref = refs[3 * n_p]
        outs = refs[3 * n_p + 1:]
        for i in range(n_p):
            n = w_refs[i].shape[1]
            g = device_sum(g_ref, offs[i], -(-n // LANES) * LANES)[:, :n]
            outs[i][...] = g
            outs[n_p + i][...], outs[2 * n_p + i][...], outs[3 * n_p + i][...] = _adamw_math(
                w_refs[i][...], g, m_refs[i][...], v_refs[i][...])
        outs[4 * n_p][...] = device_sum(g_ref, loss_off, LANES)

    res = pl.pallas_call(
        body, name="adamw_small",
        out_shape=[jax.ShapeDtypeStruct(a.shape, F32) for a in list(ws) * 4] + [jax.ShapeDtypeStruct((1, LANES), F32)],
    )(*ws, *ms, *vs, g_all)
    return res[:n_p], res[n_p:2 * n_p], res[2 * n_p:3 * n_p], res[3 * n_p:4 * n_p], res[4 * n_p]


def _adamw_halves(place, w, own, sib, m, v, axis, name, after):
    r, cc = w.shape
    if axis == 0:
        rows, gc = own.shape[0], own.shape[1]
        tr = _row_tile(rows)
        nb = rows // tr
        w_spec = pl.BlockSpec((tr, cc), lambda h, i, pr: (h * nb + i, 0))
        g_spec = pl.BlockSpec((tr, gc), lambda h, i, pr: (i, 0))
    else:
        tr = _row_tile(r)
        nb = r // tr
        gc = own.shape[1]
        w_spec = pl.BlockSpec((tr, gc), lambda h, i, pr: (i, h))
        g_spec = pl.BlockSpec((tr, gc), lambda h, i, pr: (i, 0))
    wc = w_spec.block_shape[1]

    def body(pr, w_ref, o_ref, s_ref, m_ref, v_ref, after_ref, g_ref, d_ref, nm_ref, nv_ref):
        g = jnp.where(pl.program_id(0) == pr[0], o_ref[...], s_ref[...]).astype(F32)[:, :wc]
        g_ref[...] = g
        d_ref[...], nm_ref[...], nv_ref[...] = _adamw_math(w_ref[...], g, m_ref[...], v_ref[...])

    return pl.pallas_call(
        body, name=name,
        grid_spec=pltpu.PrefetchScalarGridSpec(
            num_scalar_prefetch=1, grid=(2, nb),
            in_specs=[w_spec, g_spec, g_spec, w_spec, w_spec, ANY], out_specs=[w_spec] * 4),
        out_shape=[jax.ShapeDtypeStruct(w.shape, F32)] * 4,
        compiler_params=pltpu.CompilerParams(vmem_limit_bytes=MM_VMEM_LIMIT))(place, w, own, sib, m, v, after)


SMALL = ("b_ada", "norm_attn", "norm_ffn", "q_a_norm", "kv_a_norm", "q_norm", "k_nope_norm", "k_rope_norm",
         "out_norm_sb", "out_norm_mla")
WEIGHTS = ("w_ada", "b_ada", "norm_attn", "norm_ffn", "w_in", "q_a_norm", "w_q_up", "kv_a_norm", "w_kv_up",
           "q_norm", "k_nope_norm", "k_rope_norm", "out_norm_sb", "out_norm_mla", "w_out", "w_gate", "w_up",
           "w_down")


def kernel(x, c, positions, w_ada, b_ada, norm_attn, norm_ffn, w_in, q_a_norm, w_q_up, kv_a_norm, w_kv_up, q_norm, k_nope_norm, k_rope_norm, out_norm_sb, out_norm_mla, w_out, w_gate, w_up, w_down, loss_target, m_w_ada, m_b_ada, m_norm_attn, m_norm_ffn, m_w_in, m_q_a_norm, m_w_q_up, m_kv_a_norm, m_w_kv_up, m_q_norm, m_k_nope_norm, m_k_rope_norm, m_out_norm_sb, m_out_norm_mla, m_w_out, m_w_gate, m_w_up, m_w_down, v_w_ada, v_b_ada, v_norm_attn, v_norm_ffn, v_w_in, v_q_a_norm, v_w_q_up, v_kv_a_norm, v_w_kv_up, v_q_norm, v_k_nope_norm, v_k_rope_norm, v_out_norm_sb, v_out_norm_mla, v_w_out, v_w_gate, v_w_up, v_w_down):
    local = dict(locals())
    w = {n: local[n][0] for n in WEIGHTS}
    m = {n: local["m_" + n][0] for n in WEIGHTS}
    v = {n: local["v_" + n][0] for n in WEIGHTS}
    small = {n: w[n].reshape(1, -1) for n in SMALL}
    ix, iy, ic = _my_place()
    chip = 2 * ix + iy
    dev = 2 * chip + ic
    xs, target = x[0], loss_target[0]
    seq = xs.shape[0]

    c_all = _all_gather_small(c.reshape(8, LANES), "gather_c").reshape(N_DEV, D_MODEL)
    ada_cols = w["w_ada"].shape[1]
    b_cols = lax.dynamic_slice_in_dim(small["b_ada"], chip * ada_cols, ada_cols, axis=1)
    mod_cols = _ada_fwd(c_all, w["w_ada"], b_cols)
    mod_all = _all_gather_small(mod_cols, "gather_mod").reshape(N_CHIPS, 2, N_DEV, ada_cols)
    mod = lax.dynamic_index_in_dim(mod_all[:, 0], dev, axis=1, keepdims=False).reshape(1, N_MOD * D_MODEL)

    ff_pad = FF_SHARD_PAD - FF_SHARD
    pads = {"w_gate": ((0, 0), (0, ff_pad)), "w_up": ((0, 0), (0, ff_pad)), "w_down": ((0, ff_pad), (0, 0))}
    shards = {n: jnp.pad(w[n].astype(BF16), pads[n]) if n in pads else w[n].astype(BF16) for n in BIG}
    early, early_done = _gather_weights(EARLY, [shards[n] for n in EARLY], mod)
    gathered = dict(zip(EARLY, early))
    lands = [lax.dynamic_update_index_in_dim(lax.empty((N_CHIPS,) + shards[n].shape, BF16), shards[n], chip, 0)
             for n in LATE]
    late_gather = _exchange_start("gather_late_start", [shards[n] for n in LATE], lands, _late_gather_plan,
                                  3 * len(LATE), early_done, mod)
    mod = late_gather[4]

    half = MLA_ROPE // 2
    freqs = 1.0 / (ROPE_THETA ** (np.arange(half, dtype=np.float32) / half))
    zeros = np.zeros(LANES - MLA_ROPE, np.float32)
    freqs_row = jnp.asarray(np.concatenate([freqs, freqs, zeros]).astype(np.float32)[None])
    sign_row = jnp.asarray(np.concatenate([-np.ones(half), np.ones(half), zeros]).astype(np.float32)[None])
    cos, sin = _rope_tables(positions.reshape(seq, 1), freqs_row, sign_row)

    place = jnp.stack([ic, chip]).astype(jnp.int32)
    small_params = {n: small[n] for n in SMALL if n != "b_ada"}

    def pair_sums_of(names, grads, call_name):
        theirs = _pair_exchange(names, grads, call_name)
        return [_pair_sum(place, gr, th, HALF_AXIS[n], "grad_pair_sum_" + n) for n, gr, th in zip(names, grads, theirs)]

    p1 = {**{n: gathered[n] for n in EARLY}, **small_params}
    mixed, mixing_vjp = jax.vjp(lambda x_, mod_, p_: _mixing_stage(x_, mod_, p_, cos, sin), xs, mod, p1)
    _, landed = _exchange_wait("gather_late_wait", late_gather, _late_gather_plan, mixed)
    p2 = {**dict(zip(LATE, landed)), **small_params}
    loss_part, ffn_vjp = jax.vjp(lambda x_, mixed_, mod_, p_: _ffn_stage(x_, mixed_, mod_, p_, target), xs, mixed, mod, p2)
    gx2, gmixed, gmod2, gp2 = ffn_vjp(jnp.ones((), F32))
    late_grads = [gp2[n] for n in LATE]
    late_plan = _direct_scatter_plan(LATE)
    late_scatter = _exchange_start(
        "grad_scatter_late_start", late_grads,
        [lax.empty((7,) + _half_shape(gr.shape[1:], HALF_AXIS[n]), BF16) for n, gr in zip(LATE, late_grads)],
        late_plan, 7 * len(LATE), gx2, gmixed)
    gx1, gmod1, gp1 = mixing_vjp(late_scatter[4])
    gx = gx1 + gx2
    gmod = gmod1 + gmod2
    gp = {n: gp1[n] + gp2[n] for n in small_params}

    sizes = [w[n].size for n in SMALL]
    offs, loss_off, n_small = _small_layout(sizes)
    pieces = []
    for n, size in zip(SMALL, sizes):
        pieces.append(gmod if n == "b_ada" else gp[n])
        if size % LANES:
            pieces.append(jnp.zeros((1, LANES - size % LANES), F32))
    pieces += [jnp.full((1, LANES), loss_part), jnp.zeros((1, n_small - loss_off - LANES), F32)]
    small_vec = jnp.concatenate(pieces, axis=1)
    small_all = _all_gather_small(small_vec.reshape(8, n_small // 8), "gather_small").reshape(N_DEV, n_small)

    g, delta, new_m, new_v = {}, {}, {}, {}

    def reduce_halves(names, sums, parts, join_name, after):
        own = [_chip_sum(place, ps, pt, "grad_chip_sum_" + n, n in TRANSPOSED_UPDATE) for n, ps, pt in zip(names, sums, parts)]
        return own, _sibling_join(own, join_name, after)

    def update(names, own, sib, after):
        for n, o, s in zip(names, own, sib):
            if n in TRANSPOSED_UPDATE:
                res = _adamw_halves(place, w[n].T, o, s, m[n].T, v[n].T, 1, "adamw_" + n, after)
                g[n], delta[n], new_m[n], new_v[n] = [r.T for r in res]
            else:
                g[n], delta[n], new_m[n], new_v[n] = _adamw_halves(place, w[n], o, s, m[n], v[n], HALF_AXIS[n],
                                                                   "adamw_" + n, after)

    late_grads, late_parts = _exchange_wait("grad_scatter_late_wait", late_scatter, late_plan, gx)
    own_late = [_chip_sum_direct(place, gr, pt, HALF_AXIS[n], "grad_chip_sum_" + n, n in TRANSPOSED_UPDATE)
                for n, gr, pt in zip(LATE, late_grads, late_parts)]
    sib_late = _sibling_join(own_late, "grad_sibling_join_late", small_all)
    early_sums = pair_sums_of(EARLY, [gp1[n] for n in EARLY], "grad_pair_exchange_early")
    early_scatter = _exchange_start(
        "grad_scatter_early_start", early_sums,
        [lax.empty((N_CHIPS - 1,) + s.shape[1:], BF16) for s in early_sums], _late_scatter_plan, 3 * len(EARLY),
        sib_late[0], small_all)
    small_all = early_scatter[4]
    update(LATE, own_late, sib_late, small_all)

    *small_out, loss_row = _adamw_small([small[n] for n in SMALL], small_all, [m[n].reshape(1, -1) for n in SMALL],
                                        [v[n].reshape(1, -1) for n in SMALL], offs, loss_off)
    loss = loss_row[0, 0]
    for d, outs_d in zip((g, delta, new_m, new_v), small_out):
        d.update({n: o.reshape(w[n].shape) for n, o in zip(SMALL, outs_d)})

    dmod_all = small_all[:, :N_MOD * D_MODEL]
    g["w_ada"] = _ada_bwd(c_all, lax.dynamic_slice_in_dim(dmod_all, chip * ada_cols, ada_cols, axis=1))
    delta["w_ada"], new_m["w_ada"], new_v["w_ada"] = _adamw(w["w_ada"], g["w_ada"], m["w_ada"], v["w_ada"], "adamw_w_ada")

    early_sums, early_parts = _exchange_wait("grad_scatter_early_wait", early_scatter, _late_scatter_plan,
                                             delta["w_ada"])
    own_early, sib_early = reduce_halves(EARLY, early_sums, early_parts, "grad_sibling_join_early", delta["w_ada"])
    update(EARLY, own_early, sib_early, sib_early[0])

    def outs(d):
        return [d[n][None] for n in WEIGHTS]

    return (loss, gx[None], *outs(g), *outs(delta), *outs(new_m), *outs(new_v))
```

```python
import numpy as np
import jax
import jax.numpy as jnp
from jax import lax
from jax.experimental import pallas as pl
from jax.experimental.pallas import tpu as pltpu

F32 = jnp.float32
BF16 = jnp.bfloat16
MESH = pl.DeviceIdType.MESH
ANY = pl.BlockSpec(memory_space=pl.ANY)

D_MODEL = 1024
SB_HEADS = 8
SB_HEAD_DIM = 64
SB_WIDTH = 512
MLA_HEADS = 4
MLA_NOPE = 128
MLA_ROPE = 64
MLA_QK = 192
MLA_V = 128
MLA_Q_RANK = 384
MLA_KV_RANK = 256
D_FF = 2816
N_MOD = 6
ROPE_THETA = 10000.0
EPS = 1e-6
LANES = 128

ADAM_LR = 0.001
ADAM_B1 = 0.9
ADAM_B2 = 0.999
ADAM_EPS = 1e-08
ADAM_WD = 0.01
ADAM_STEP = 10

N_CHIPS = 4
N_DEV = 8
ROW_TILE = 256
MM_ROW_TILE = 512
ATT_BLK = 256
MM_VMEM_LIMIT = 56 * 1024 * 1024
FF_SHARD = D_FF // N_CHIPS
FF_SHARD_PAD = 768


def _mm(a, b, mode, name, tm, tn, out_dtype=F32):
    if mode == "nn":
        (m, k), n = a.shape, b.shape[1]
        a_spec = pl.BlockSpec((tm, k), lambda j, i: (i, 0))
        b_spec = pl.BlockSpec((k, tn), lambda j, i: (0, j))
        dims = (((1,), (0,)), ((), ()))
    elif mode == "nt":
        (m, k), n = a.shape, b.shape[0]
        a_spec = pl.BlockSpec((tm, k), lambda j, i: (i, 0))
        b_spec = pl.BlockSpec((tn, k), lambda j, i: (j, 0))
        dims = (((1,), (1,)), ((), ()))
    else:
        (k, m), n = a.shape, b.shape[1]
        a_spec = pl.BlockSpec((k, tm), lambda j, i: (0, i))
        b_spec = pl.BlockSpec((k, tn), lambda j, i: (0, j))
        dims = (((0,), (0,)), ((), ()))
    assert m % tm == 0 and n % tn == 0, (name, m, n, tm, tn)

    def body(a_ref, b_ref, o_ref):
        o_ref[...] = lax.dot_general(a_ref[...].astype(BF16), b_ref[...].astype(BF16), dims,
                                     preferred_element_type=F32).astype(out_dtype)

    return pl.pallas_call(
        body, name=name, grid=(n // tn, m // tm),
        in_specs=[a_spec, b_spec],
        out_specs=pl.BlockSpec((tm, tn), lambda j, i: (i, j)),
        out_shape=jax.ShapeDtypeStruct((m, n), out_dtype),
        compiler_params=pltpu.CompilerParams(dimension_semantics=("arbitrary", "arbitrary"),
                                             vmem_limit_bytes=MM_VMEM_LIMIT),
    )(a, b)


def _make_linear(name, tk_w, tn_w):
    @jax.custom_vjp
    def op(a, w):
        return _mm(a, w, "nn", name + "_fwd", MM_ROW_TILE, w.shape[1])

    def fwd(a, w):
        return op(a, w), (a, w)

    def bwd(res, dy):
        a, w = res
        da = _mm(dy, w, "nt", name + "_dx", MM_ROW_TILE, w.shape[0])
        dw = _mm(a, dy, "tn", name + "_dw", tk_w, tn_w, out_dtype=BF16)
        return da, dw

    op.defvjp(fwd, bwd)
    return op


def _make_linear_split(name, widths, tk_w):
    starts = [sum(widths[:g]) for g in range(len(widths))]

    def call_fwd(a, w):
        t, k = a.shape
        n = w.shape[1]

        def body(a_ref, w_ref, *o_refs):
            y = jnp.dot(a_ref[...].astype(BF16), w_ref[...], preferred_element_type=F32)
            for o_ref, s0, wd in zip(o_refs, starts, widths):
                o_ref[...] = y[:, s0:s0 + wd]

        return pl.pallas_call(
            body, name=name + "_fwd", grid=(t // MM_ROW_TILE,),
            in_specs=[pl.BlockSpec((MM_ROW_TILE, k), lambda i: (i, 0)), pl.BlockSpec((k, n), lambda i: (0, 0))],
            out_specs=[pl.BlockSpec((MM_ROW_TILE, wd), lambda i: (i, 0)) for wd in widths],
            out_shape=[jax.ShapeDtypeStruct((t, wd), F32) for wd in widths],
            compiler_params=pltpu.CompilerParams(dimension_semantics=("arbitrary",), vmem_limit_bytes=MM_VMEM_LIMIT),
        )(a, w)

    def call_dx(dys, w):
        t = dys[0].shape[0]
        k, n = w.shape

        def body(*refs):
            dy_refs, w_ref, o_ref = refs[:-2], refs[-2], refs[-1]
            acc = jnp.zeros((MM_ROW_TILE, k), F32)
            for dy_ref, s0, wd in zip(dy_refs, starts, widths):
                acc = acc + _nt(dy_ref[...].astype(BF16), w_ref[:, s0:s0 + wd])
            o_ref[...] = acc

        return pl.pallas_call(
            body, name=name + "_dx", grid=(t // MM_ROW_TILE,),
            in_specs=[pl.BlockSpec((MM_ROW_TILE, wd), lambda i: (i, 0)) for wd in widths]
            + [pl.BlockSpec((k, n), lambda i: (0, 0))],
            out_specs=pl.BlockSpec((MM_ROW_TILE, k), lambda i: (i, 0)),
            out_shape=jax.ShapeDtypeStruct((t, k), F32),
            compiler_params=pltpu.CompilerParams(dimension_semantics=("arbitrary",), vmem_limit_bytes=MM_VMEM_LIMIT),
        )(*dys, w)

    def call_dw(a, dys, w):
        t, k = a.shape
        n = w.shape[1]

        def body(a_ref, *refs):
            dy_refs, o_ref = refs[:-1], refs[-1]
            ab = a_ref[...].astype(BF16)
            for dy_ref, s0, wd in zip(dy_refs, starts, widths):
                o_ref[:, s0:s0 + wd] = _tn(ab, dy_ref[...].astype(BF16)).astype(BF16)
            if starts[-1] + widths[-1] < n:
                o_ref[:, starts[-1] + widths[-1]:] = jnp.zeros((tk_w, n - starts[-1] - widths[-1]), BF16)

        return pl.pallas_call(
            body, name=name + "_dw", grid=(k // tk_w,),
            in_specs=[pl.BlockSpec((t, tk_w), lambda i: (0, i))]
            + [pl.BlockSpec((t, wd), lambda i: (0, 0)) for wd in widths],
            out_specs=pl.BlockSpec((tk_w, n), lambda i: (i, 0)),
            out_shape=jax.ShapeDtypeStruct((k, n), BF16),
            compiler_params=pltpu.CompilerParams(dimension_semantics=("arbitrary",), vmem_limit_bytes=MM_VMEM_LIMIT),
        )(a, *dys)

    @jax.custom_vjp
    def op(a, w):
        return tuple(call_fwd(a, w))

    def fwd(a, w):
        return op(a, w), (a, w)

    def bwd(res, dys):
        a, w = res
        return call_dx(dys, w), call_dw(a, dys, w)

    op.defvjp(fwd, bwd)
    return op


def _row_spec(arr, tb):
    return pl.BlockSpec((tb, arr.shape[1]), lambda i: (i, 0))


def _full_spec(arr):
    return pl.BlockSpec(arr.shape, lambda i: (0, 0))


def _make_rowwise(name, f, n_rows, n_params, out_cols, diff_rows, out_dtypes=None, grad_dtypes=None):
    n_out = len(out_cols)
    out_dtypes = out_dtypes or [F32] * n_out
    grad_dtypes = grad_dtypes or [F32] * sum(diff_rows)

    def call_fwd(rows, params):
        t = rows[0].shape[0]

        def body(*refs):
            ins = [r[...] for r in refs[:n_rows + n_params]]
            outs = f(*ins)
            for o_ref, o in zip(refs[n_rows + n_params:], outs):
                o_ref[...] = o.astype(o_ref.dtype)

        return pl.pallas_call(
            body, name=name + "_fwd", grid=(t // ROW_TILE,),
            in_specs=[_row_spec(a, ROW_TILE) for a in rows] + [_full_spec(p) for p in params],
            out_specs=[pl.BlockSpec((ROW_TILE, n), lambda i: (i, 0)) for n in out_cols],
            out_shape=[jax.ShapeDtypeStruct((t, n), dt) for n, dt in zip(out_cols, out_dtypes)],
            compiler_params=pltpu.CompilerParams(dimension_semantics=("arbitrary",),
                                                 vmem_limit_bytes=MM_VMEM_LIMIT),
        )(*rows, *params)

    def call_bwd(rows, params, cts):
        t = rows[0].shape[0]
        d_rows = [a for a, d in zip(rows, diff_rows) if d]
        n_in = n_rows + n_params + n_out

        def body(*refs):
            ins = [r[...] for r in refs[:n_rows + n_params]]
            ct = tuple(r[...].astype(F32) for r in refs[n_rows + n_params:n_in])
            _, vjp = jax.vjp(f, *ins)
            grads = vjp(ct)
            out_refs = refs[n_in:]
            g_rows = [g for g, d in zip(grads[:n_rows], diff_rows) if d]
            for o_ref, g in zip(out_refs[:len(g_rows)], g_rows):
                o_ref[...] = g.astype(o_ref.dtype)
            p_refs = out_refs[len(g_rows):]

            if p_refs:
                @pl.when(pl.program_id(0) == 0)
                def _():
                    for p_ref in p_refs:
                        p_ref[...] = jnp.zeros_like(p_ref)

                for p_ref, g in zip(p_refs, grads[n_rows:]):
                    p_ref[...] += g

        return pl.pallas_call(
            body, name=name + "_bwd", grid=(t // ROW_TILE,),
            in_specs=[_row_spec(a, ROW_TILE) for a in rows] + [_full_spec(p) for p in params]
            + [_row_spec(c, ROW_TILE) for c in cts],
            out_specs=[_row_spec(a, ROW_TILE) for a in d_rows] + [_full_spec(p) for p in params],
            out_shape=[jax.ShapeDtypeStruct(a.shape, dt) for a, dt in zip(d_rows, grad_dtypes)]
            + [jax.ShapeDtypeStruct(p.shape, F32) for p in params],
            compiler_params=pltpu.CompilerParams(dimension_semantics=("arbitrary",),
                                                 vmem_limit_bytes=MM_VMEM_LIMIT),
        )(*rows, *params, *cts)

    @jax.custom_vjp
    def op(*args):
        return tuple(call_fwd(args[:n_rows], args[n_rows:]))

    def fwd(*args):
        return op(*args), args

    def bwd(args, cts):
        rows, params = args[:n_rows], args[n_rows:]
        outs = call_bwd(rows, params, cts)
        it = iter(outs)
        g_rows = [next(it) if d else jnp.zeros_like(a) for a, d in zip(rows, diff_rows)]
        return tuple(g_rows) + tuple(it)

    op.defvjp(fwd, bwd)
    return op


def _rms(x, g, n):
    return x * lax.rsqrt(jnp.sum(x * x, axis=-1, keepdims=True) * (1.0 / n) + EPS) * g


def _f_pre_attn(x, g, scale, shift):
    return (_rms(x, g, D_MODEL) * (1.0 + scale) + shift,)


def _f_mla_a(cq, ckv, gq, gkv):
    return _rms(cq, gq, MLA_Q_RANK), _rms(ckv, gkv, MLA_KV_RANK)


@jax.custom_vjp
def _split_lanes(x):
    return tuple(x[:, i * LANES:(i + 1) * LANES] for i in range(x.shape[1] // LANES))


def _split_lanes_fwd(x):
    return _split_lanes(x), None


def _split_lanes_bwd(_, cts):
    return (jnp.concatenate(cts, axis=1),)


_split_lanes.defvjp(_split_lanes_fwd, _split_lanes_bwd)


def _f_mla_b(qall, kn_all, kr, kr_sw, cos, sin, gqn, gqr, gqr_sw, gkn, gkr, gkr_sw):
    q = _split_lanes(qall)
    kn = _split_lanes(kn_all)
    qn_o, qr_o, kn_o = [], [], []
    for h in range(MLA_HEADS):
        qn, qr, qs = q[h], q[MLA_HEADS + h], q[2 * MLA_HEADS + h]
        ss = jnp.sum(qn * qn, axis=-1, keepdims=True) + jnp.sum(qr * qr, axis=-1, keepdims=True)
        rs = lax.rsqrt(ss * (1.0 / MLA_QK) + EPS)
        qn_o.append(qn * rs * gqn)
        qr_o.append((qr * rs * gqr) * cos + (qs * rs * gqr_sw) * sin)
        kn_o.append(_rms(kn[h], gkn, MLA_NOPE))
    rs = lax.rsqrt(jnp.sum(kr * kr, axis=-1, keepdims=True) * (1.0 / MLA_ROPE) + EPS)
    kr_o = (kr * rs * gkr) * cos + (kr_sw * rs * gkr_sw) * sin
    return (jnp.concatenate(qn_o, axis=1), jnp.concatenate(qr_o, axis=1), jnp.concatenate(kn_o, axis=1), kr_o)


def _f_post_attn(o_sb, o_mla, g_sb, g_mla):
    return (jnp.concatenate([_rms(o_sb, g_sb, SB_WIDTH), _rms(o_mla, g_mla, SB_WIDTH)], axis=1),)


def _f_pre_ffn(x, attn, gate, g, scale, shift):
    x2 = x + gate * attn
    return x2, _rms(x2, g, D_MODEL) * (1.0 + scale) + shift


def _f_swiglu(gt, up):
    return (gt / (1.0 + jnp.exp(-gt)) * up,)


def _f_loss(x2, ffn, target, gate):
    err = x2 + gate * ffn - target
    return (jnp.sum(err * err, axis=-1, keepdims=True) * (1.0 / D_MODEL),)


def _rope_tables(pos_col, freqs, sign):
    t = pos_col.shape[0]

    def body(p_ref, f_ref, s_ref, cos_ref, sin_ref):
        ang = p_ref[...].astype(F32) * f_ref[...]
        live = jnp.abs(s_ref[...])
        cos_ref[...] = jnp.cos(ang) * live
        sin_ref[...] = jnp.sin(ang) * s_ref[...]

    return pl.pallas_call(
        body, name="rope_tables", grid=(t // ROW_TILE,),
        in_specs=[pl.BlockSpec((ROW_TILE, 1), lambda i: (i, 0)), _full_spec(freqs), _full_spec(sign)],
        out_specs=[pl.BlockSpec((ROW_TILE, LANES), lambda i: (i, 0))] * 2,
        out_shape=[jax.ShapeDtypeStruct((t, LANES), F32)] * 2,
    )(pos_col, freqs, sign)


def _hi_lo_dot(x, tri):
    hi = x.astype(BF16)
    lo = (x - hi.astype(F32)).astype(BF16)
    return (jnp.dot(hi, tri, preferred_element_type=F32) + jnp.dot(lo, tri, preferred_element_type=F32))


def _tri(cmp):
    r = lax.broadcasted_iota(jnp.int32, (ATT_BLK, ATT_BLK), 0)
    c = lax.broadcasted_iota(jnp.int32, (ATT_BLK, ATT_BLK), 1)
    return cmp(r, c).astype(BF16)


def _nt(a, b):
    return lax.dot_general(a, b, (((1,), (1,)), ((), ())), preferred_element_type=F32)


def _tn(a, b):
    return lax.dot_general(a, b, (((0,), (0,)), ((), ())), preferred_element_type=F32)


def _sb_logs(z):
    lb = jnp.minimum(z, 0.0) - jnp.log(1.0 + jnp.exp(-jnp.abs(z)))
    return lb, lb - z


def _sb_fwd(q, k, v):
    t = q.shape[0]
    nq = t // ATT_BLK
    scale = SB_HEAD_DIM ** -0.5

    def body(q_ref, k_ref, v_ref, o_ref, tot_ref):
        qi = pl.program_id(1)
        lane = lax.broadcasted_iota(jnp.int32, (ATT_BLK, LANES), 1)
        tri = _tri(lambda r, c: r > c)
        qv = q_ref[...] * scale
        heads = [(lane // SB_HEAD_DIM) == hh for hh in range(2)]
        qms = [jnp.where(mine, qv, 0.0).astype(BF16) for mine in heads]

        def blocks(kbs, carry, diagonal):
            acc = carry[0]
            nb = len(kbs)
            chains = [(b, hh) for b in range(nb) for hh in range(2)]
            offs = [pl.multiple_of(kb * ATT_BLK, ATT_BLK) for kb in kbs]
            kks = [k_ref[pl.ds(off, ATT_BLK), :].astype(BF16) for off in offs]
            v_blks = [v_ref[pl.ds(off, ATT_BLK), :] for off in offs]
            if any(diagonal):
                valid = (lax.broadcasted_iota(jnp.int32, (ATT_BLK, ATT_BLK), 1)
                         < lax.broadcasted_iota(jnp.int32, (ATT_BLK, ATT_BLK), 0))
            zs = {ch: _nt(qms[ch[1]], kks[ch[0]]) for ch in chains}
            vvs = {(b, hh): jnp.where(heads[hh], v_blks[b], 0.0).astype(BF16) for b, hh in chains}
            logs = {ch: _sb_logs(zs[ch]) for ch in chains}
            l1ms = {ch: jnp.where(valid, logs[ch][1], 0.0) if diagonal[ch[0]] else logs[ch][1] for ch in chains}
            run = {(0, hh): carry[1 + hh] for hh in range(2)}
            for b, hh in chains:
                run[(b + 1, hh)] = run[(b, hh)] + jnp.sum(l1ms[(b, hh)], axis=-1, keepdims=True)
            afters = {ch: _hi_lo_dot(l1ms[ch], tri) for ch in chains}
            ws = {ch: jnp.exp(logs[ch][0] + (afters[ch] + run[ch])) for ch in chains}
            ws = {ch: jnp.where(valid, ws[ch], 0.0) if diagonal[ch[0]] else ws[ch] for ch in chains}
            for ch in chains:
                acc = acc + jnp.dot(ws[ch].astype(BF16), vvs[ch], preferred_element_type=F32)
            return (acc, run[(nb, 0)], run[(nb, 1)])

        zero = jnp.zeros((ATT_BLK, 1), F32)
        init = (jnp.zeros((ATT_BLK, LANES), F32), zero, zero)
        carry = lax.cond(qi % 2 == 1, lambda cr: blocks([qi, qi - 1], cr, (True, False)),
                         lambda cr: blocks([qi], cr, (True,)), init)
        top = qi - 1 - qi % 2
        carry = lax.fori_loop(0, qi // 2, lambda pr, cr: blocks([top - 2 * pr, top - 1 - 2 * pr], cr, (False, False)),
                              carry)
        o_ref[...] = carry[0]
        for hh in range(2):
            tot_ref[:, hh * LANES:(hh + 1) * LANES] = jnp.broadcast_to(carry[1 + hh], (ATT_BLK, LANES))

    return pl.pallas_call(
        body, name="sb_attn_fwd", grid=(SB_HEADS // 2, nq),
        in_specs=[pl.BlockSpec((ATT_BLK, LANES), lambda p, i: (i, p)),
                  pl.BlockSpec((t, LANES), lambda p, i: (0, p)),
                  pl.BlockSpec((t, LANES), lambda p, i: (0, p))],
        out_specs=[pl.BlockSpec((ATT_BLK, LANES), lambda p, i: (i, p)),
                   pl.BlockSpec((ATT_BLK, 2 * LANES), lambda p, i: (i, p))],
        out_shape=[jax.ShapeDtypeStruct((t, SB_WIDTH), F32), jax.ShapeDtypeStruct((t, SB_HEADS * LANES), F32)],
        compiler_params=pltpu.CompilerParams(dimension_semantics=("arbitrary", "arbitrary")),
    )(q, k, v)


def _sb_bwd(q, k, v, tot, do):
    t = q.shape[0]
    nq = t // ATT_BLK
    scale = SB_HEAD_DIM ** -0.5

    def body(q_ref, k_ref, v_ref, tot_ref, do_ref, dq_ref, dk_ref, dv_ref):
        qi = pl.program_id(1)

        @pl.when(qi == 0)
        def _():
            dk_ref[...] = jnp.zeros_like(dk_ref)
            dv_ref[...] = jnp.zeros_like(dv_ref)

        lane = lax.broadcasted_iota(jnp.int32, (ATT_BLK, LANES), 1)
        tri_incl = _tri(lambda r, c: r <= c)
        tri_lt = _tri(lambda r, c: r < c)
        qv = q_ref[...] * scale
        dov = do_ref[...]
        heads = [(lane // SB_HEAD_DIM) == hh for hh in range(2)]
        qms = [jnp.where(mine, qv, 0.0).astype(BF16) for mine in heads]
        doms = [jnp.where(mine, dov, 0.0).astype(BF16) for mine in heads]
        tots = [tot_ref[:, hh * LANES:hh * LANES + 1] for hh in range(2)]

        def blocks(kbs, carry, diagonal):
            dq = carry[0]
            nb = len(kbs)
            chains = [(b, hh) for b in range(nb) for hh in range(2)]
            offs = [pl.multiple_of(kb * ATT_BLK, ATT_BLK) for kb in kbs]
            k_blks = [k_ref[pl.ds(off, ATT_BLK), :] for off in offs]
            vvs = [v_ref[pl.ds(off, ATT_BLK), :].astype(BF16) for off in offs]
            if any(diagonal):
                valid = (lax.broadcasted_iota(jnp.int32, (ATT_BLK, ATT_BLK), 1)
                         < lax.broadcasted_iota(jnp.int32, (ATT_BLK, ATT_BLK), 0))
            kks = {(b, hh): jnp.where(heads[hh], k_blks[b], 0.0).astype(BF16) for b, hh in chains}
            zs = {ch: _nt(qms[ch[1]], kks[ch]) for ch in chains}
            dws = {ch: _nt(doms[ch[1]], vvs[ch[0]]) for ch in chains}
            logs = {ch: _sb_logs(zs[ch]) for ch in chains}
            lbs = {ch: logs[ch][0] for ch in chains}
            l1m_all = {ch: logs[ch][1] for ch in chains}
            l1ms = {ch: jnp.where(valid, l1m_all[ch], 0.0) if diagonal[ch[0]] else l1m_all[ch] for ch in chains}
            pre, c_de = {}, {}
            for hh in range(2):
                pre[(0, hh)], c_de[(0, hh)] = carry[1 + 2 * hh], carry[2 + 2 * hh]
            for b, hh in chains:
                pre[(b + 1, hh)] = pre[(b, hh)] + jnp.sum(l1ms[(b, hh)], axis=-1, keepdims=True)
            prefix = {ch: _hi_lo_dot(l1ms[ch], tri_incl) for ch in chains}
            ws = {ch: jnp.exp(lbs[ch] + (tots[ch[1]] - (prefix[ch] + pre[ch]))) for ch in chains}
            ws = {ch: jnp.where(valid, ws[ch], 0.0) if diagonal[ch[0]] else ws[ch] for ch in chains}
            d_es = {ch: ws[ch] * dws[ch] for ch in chains}
            for b, hh in chains:
                c_de[(b + 1, hh)] = c_de[(b, hh)] + jnp.sum(d_es[(b, hh)], axis=-1, keepdims=True)
            dvs = [_tn(ws[(b, 0)].astype(BF16), doms[0]) + _tn(ws[(b, 1)].astype(BF16), doms[1]) for b in range(nb)]
            dl1ms = {ch: jnp.dot(d_es[ch].astype(BF16), tri_lt, preferred_element_type=F32) + c_de[ch] for ch in chains}
            dzs = {ch: d_es[ch] * jnp.exp(l1m_all[ch]) - dl1ms[ch] * jnp.exp(lbs[ch]) for ch in chains}
            dzs = {ch: jnp.where(valid, dzs[ch], 0.0) if diagonal[ch[0]] else dzs[ch] for ch in chains}
            dzs = {ch: dzs[ch].astype(BF16) for ch in chains}
            for ch in chains:
                dq = dq + jnp.dot(dzs[ch], kks[ch], preferred_element_type=F32)
            for b in range(nb):
                dk_ref[pl.ds(offs[b], ATT_BLK), :] += _tn(dzs[(b, 0)], qms[0]) + _tn(dzs[(b, 1)], qms[1])
                dv_ref[pl.ds(offs[b], ATT_BLK), :] += dvs[b]
            return (dq, pre[(nb, 0)], c_de[(nb, 0)], pre[(nb, 1)], c_de[(nb, 1)])

        zero = jnp.zeros((ATT_BLK, 1), F32)
        carry = lax.fori_loop(0, qi // 2, lambda pr, cr: blocks([2 * pr, 2 * pr + 1], cr, (False, False)),
                              (jnp.zeros((ATT_BLK, LANES), F32), zero, zero, zero, zero))
        carry = lax.cond(qi % 2 == 1, lambda cr: blocks([qi - 1, qi], cr, (False, True)),
                         lambda cr: blocks([qi], cr, (True,)), carry)
        dq_ref[...] = carry[0] * scale

    return pl.pallas_call(
        body, name="sb_attn_bwd", grid=(SB_HEADS // 2, nq),
        in_specs=[pl.BlockSpec((ATT_BLK, LANES), lambda p, i: (i, p)),
                  pl.BlockSpec((t, LANES), lambda p, i: (0, p)),
                  pl.BlockSpec((t, LANES), lambda p, i: (0, p)),
                  pl.BlockSpec((ATT_BLK, 2 * LANES), lambda p, i: (i, p)),
                  pl.BlockSpec((ATT_BLK, LANES), lambda p, i: (i, p))],
        out_specs=[pl.BlockSpec((ATT_BLK, LANES), lambda p, i: (i, p)),
                   pl.BlockSpec((t, LANES), lambda p, i: (0, p)),
                   pl.BlockSpec((t, LANES), lambda p, i: (0, p))],
        out_shape=[jax.ShapeDtypeStruct((t, SB_WIDTH), F32)] * 3,
        compiler_params=pltpu.CompilerParams(dimension_semantics=("arbitrary", "arbitrary")),
    )(q, k, v, tot, do)


@jax.custom_vjp
def _sb_attention(q, k, v):
    return _sb_fwd(q, k, v)[0]


def _sb_attention_fwd(q, k, v):
    o, tot = _sb_fwd(q, k, v)
    return o, (q, k, v, tot)


def _sb_attention_bwd(res, do):
    return tuple(_sb_bwd(*res, do))


_sb_attention.defvjp(_sb_attention_fwd, _sb_attention_bwd)


def _mla_fwd(qn, qr, kn, kr, v):
    t = qn.shape[0]
    nq = t // ATT_BLK
    scale = MLA_QK ** -0.5

    def body(qn_ref, qr_ref, kn_ref, kr_ref, v_ref, o_ref, lse_ref):
        qi = pl.program_id(1)
        lanes = [slice(hh * LANES, (hh + 1) * LANES) for hh in range(2)]
        qnb = [qn_ref[:, sl].astype(BF16) for sl in lanes]
        qrb = [qr_ref[:, sl].astype(BF16) for sl in lanes]

        def blocks(kbs, carry, diagonal):
            nb = len(kbs)
            chains = [(b, hh) for b in range(nb) for hh in range(2)]
            offs = [pl.multiple_of(kb * ATT_BLK, ATT_BLK) for kb in kbs]
            krbs = [kr_ref[pl.ds(off, ATT_BLK), :].astype(BF16) for off in offs]
            accs, ms, ls = [carry[0], carry[3]], [carry[1], carry[4]], [carry[2], carry[5]]
            ss = {(b, hh): (_nt(qnb[hh], kn_ref[pl.ds(offs[b], ATT_BLK), lanes[hh]].astype(BF16))
                            + _nt(qrb[hh], krbs[b])) * scale for b, hh in chains}
            if any(diagonal):
                causal = (lax.broadcasted_iota(jnp.int32, (ATT_BLK, ATT_BLK), 1)
                          <= lax.broadcasted_iota(jnp.int32, (ATT_BLK, ATT_BLK), 0))
                ss = {ch: jnp.where(causal, ss[ch], -jnp.inf) if diagonal[ch[0]] else ss[ch] for ch in chains}
            m_new = list(ms)
            for b, hh in chains:
                m_new[hh] = jnp.maximum(m_new[hh], jnp.max(ss[(b, hh)], axis=-1, keepdims=True))
            ps = {(b, hh): jnp.exp(ss[(b, hh)] - m_new[hh]) for b, hh in chains}
            alphas = [jnp.exp(ms[hh] - m_new[hh]) for hh in range(2)]
            pvs = {(b, hh): jnp.dot(ps[(b, hh)].astype(BF16), v_ref[pl.ds(offs[b], ATT_BLK), lanes[hh]].astype(BF16),
                                    preferred_element_type=F32) for b, hh in chains}
            out = []
            for hh in range(2):
                acc, l = accs[hh] * alphas[hh], ls[hh] * alphas[hh]
                for b in range(nb):
                    acc, l = acc + pvs[(b, hh)], l + jnp.sum(ps[(b, hh)], axis=-1, keepdims=True)
                out += [acc, m_new[hh], l]
            return tuple(out)

        init = (jnp.zeros((ATT_BLK, LANES), F32), jnp.full((ATT_BLK, 1), -jnp.inf, F32), jnp.zeros((ATT_BLK, 1), F32))
        carry = lax.cond(qi % 2 == 1, lambda cr: blocks([qi, qi - 1], cr, (True, False)),
                         lambda cr: blocks([qi], cr, (True,)), init + init)
        carry = lax.fori_loop(0, qi // 2, lambda pr, cr: blocks([2 * pr, 2 * pr + 1], cr, (False, False)), carry)
        for hh in range(2):
            acc, m, l = carry[3 * hh:3 * hh + 3]
            o_ref[:, lanes[hh]] = acc / l
            lse_ref[:, lanes[hh]] = jnp.broadcast_to(m + jnp.log(l), (ATT_BLK, LANES))

    blk = pl.BlockSpec((ATT_BLK, 2 * LANES), lambda p, i: (i, p))
    full = pl.BlockSpec((t, 2 * LANES), lambda p, i: (0, p))
    return pl.pallas_call(
        body, name="mla_attn_fwd", grid=(MLA_HEADS // 2, nq),
        in_specs=[blk, blk, full, pl.BlockSpec((t, LANES), lambda p, i: (0, 0)), full],
        out_specs=[blk, blk],
        out_shape=[jax.ShapeDtypeStruct((t, MLA_HEADS * LANES), F32)] * 2,
        compiler_params=pltpu.CompilerParams(dimension_semantics=("arbitrary", "arbitrary")),
    )(qn, qr, kn, kr, v)


def _mla_bwd(qn, qr, kn, kr, v, o, lse, do):
    t = qn.shape[0]
    nq = t // ATT_BLK
    scale = MLA_QK ** -0.5

    def body(qn_ref, qr_ref, kn_ref, kr_ref, v_ref, o_ref, lse_ref, do_ref,
             dqn_ref, dqr_ref, dkn_ref, dkr_ref, dv_ref):
        pair = pl.program_id(0)
        qi = pl.program_id(1)

        @pl.when(qi == 0)
        def _():
            dkn_ref[...] = jnp.zeros_like(dkn_ref)
            dv_ref[...] = jnp.zeros_like(dv_ref)

        @pl.when((qi == 0) & (pair == 0))
        def _():
            dkr_ref[...] = jnp.zeros_like(dkr_ref)

        lanes = [slice(hh * LANES, (hh + 1) * LANES) for hh in range(2)]
        qnb = [qn_ref[:, sl].astype(BF16) for sl in lanes]
        qrb = [qr_ref[:, sl].astype(BF16) for sl in lanes]
        dob = [do_ref[:, sl].astype(BF16) for sl in lanes]
        delta = [jnp.sum(do_ref[:, sl] * o_ref[:, sl], axis=-1, keepdims=True) for sl in lanes]
        lse_v = [lse_ref[:, hh * LANES:hh * LANES + 1] for hh in range(2)]

        def blocks(kbs, carry, diagonal):
            nb = len(kbs)
            chains = [(b, hh) for b in range(nb) for hh in range(2)]
            offs = [pl.multiple_of(kb * ATT_BLK, ATT_BLK) for kb in kbs]
            krbs = [kr_ref[pl.ds(off, ATT_BLK), :].astype(BF16) for off in offs]
            knb = {(b, hh): kn_ref[pl.ds(offs[b], ATT_BLK), lanes[hh]].astype(BF16) for b, hh in chains}
            vb = {(b, hh): v_ref[pl.ds(offs[b], ATT_BLK), lanes[hh]].astype(BF16) for b, hh in chains}
            ss = {(b, hh): _nt(qnb[hh], knb[(b, hh)]) + _nt(qrb[hh], krbs[b]) for b, hh in chains}
            dps = {(b, hh): _nt(dob[hh], vb[(b, hh)]) for b, hh in chains}
            ps = {(b, hh): jnp.exp(ss[(b, hh)] * scale - lse_v[hh]) for b, hh in chains}
            if any(diagonal):
                causal = (lax.broadcasted_iota(jnp.int32, (ATT_BLK, ATT_BLK), 1)
                          <= lax.broadcasted_iota(jnp.int32, (ATT_BLK, ATT_BLK), 0))
                ps = {ch: jnp.where(causal, ps[ch], 0.0) if diagonal[ch[0]] else ps[ch] for ch in chains}
            dss = {(b, hh): (ps[(b, hh)] * (dps[(b, hh)] - delta[hh]) * scale).astype(BF16) for b, hh in chains}
            for b, hh in chains:
                dv_ref[pl.ds(offs[b], ATT_BLK), lanes[hh]] += _tn(ps[(b, hh)].astype(BF16), dob[hh])
            for b, hh in chains:
                dkn_ref[pl.ds(offs[b], ATT_BLK), lanes[hh]] += _tn(dss[(b, hh)], qnb[hh])
            for b in range(nb):
                dkr_ref[pl.ds(offs[b], ATT_BLK), :] += _tn(dss[(b, 0)], qrb[0]) + _tn(dss[(b, 1)], qrb[1])
            out = list(carry)
            for b, hh in chains:
                out[2 * hh] = out[2 * hh] + jnp.dot(dss[(b, hh)], knb[(b, hh)], preferred_element_type=F32)
                out[2 * hh + 1] = out[2 * hh + 1] + jnp.dot(dss[(b, hh)], krbs[b], preferred_element_type=F32)
            return tuple(out)

        zero = jnp.zeros((ATT_BLK, LANES), F32)
        carry = lax.fori_loop(0, qi // 2, lambda pr, cr: blocks([2 * pr, 2 * pr + 1], cr, (False, False)),
                              (zero, zero, zero, zero))
        carry = lax.cond(qi % 2 == 1, lambda cr: blocks([qi - 1, qi], cr, (False, True)),
                         lambda cr: blocks([qi], cr, (True,)), carry)
        for hh in range(2):
            dqn_ref[:, lanes[hh]] = carry[2 * hh]
            dqr_ref[:, lanes[hh]] = carry[2 * hh + 1]

    blk = pl.BlockSpec((ATT_BLK, 2 * LANES), lambda p, i: (i, p))
    full = pl.BlockSpec((t, 2 * LANES), lambda p, i: (0, p))
    shared = pl.BlockSpec((t, LANES), lambda p, i: (0, 0))
    wide = jax.ShapeDtypeStruct((t, MLA_HEADS * LANES), F32)
    return pl.pallas_call(
        body, name="mla_attn_bwd", grid=(MLA_HEADS // 2, nq),
        in_specs=[blk, blk, full, shared, full, blk, blk, blk],
        out_specs=[blk, blk, full, shared, full],
        out_shape=[wide, wide, wide, jax.ShapeDtypeStruct((t, LANES), F32), wide],
        compiler_params=pltpu.CompilerParams(dimension_semantics=("arbitrary", "arbitrary")),
    )(qn, qr, kn, kr, v, o, lse, do)


@jax.custom_vjp
def _mla_attention(qn, qr, kn, kr, v):
    return _mla_fwd(qn, qr, kn, kr, v)[0]


def _mla_attention_fwd(qn, qr, kn, kr, v):
    o, lse = _mla_fwd(qn, qr, kn, kr, v)
    return o, (qn, qr, kn, kr, v, o, lse)


def _mla_attention_bwd(res, do):
    return tuple(_mla_bwd(*res, do))


_mla_attention.defvjp(_mla_attention_fwd, _mla_attention_bwd)


def _ffn_in(h, wg, wu):
    t, k = h.shape
    n_sh, _, cc = wg.shape

    def body(h_ref, wg_ref, wu_ref, g_ref, u_ref, a_ref):
        hb = h_ref[...].astype(BF16)
        for j in range(n_sh):
            cols = slice(j * cc, (j + 1) * cc)
            g = jnp.dot(hb, wg_ref[j], preferred_element_type=F32)
            u = jnp.dot(hb, wu_ref[j], preferred_element_type=F32)
            g_ref[:, cols] = g
            u_ref[:, cols] = u
            a_ref[:, cols] = _f_swiglu(g, u)[0].astype(BF16)

    w_spec = pl.BlockSpec((n_sh, k, cc), lambda i: (0, 0, 0))
    o_spec = pl.BlockSpec((ROW_TILE, n_sh * cc), lambda i: (i, 0))
    wide = (t, n_sh * cc)
    return pl.pallas_call(
        body, name="ffn_in_fwd", grid=(t // ROW_TILE,),
        in_specs=[pl.BlockSpec((ROW_TILE, k), lambda i: (i, 0)), w_spec, w_spec],
        out_specs=[o_spec, o_spec, o_spec],
        out_shape=[jax.ShapeDtypeStruct(wide, F32), jax.ShapeDtypeStruct(wide, F32), jax.ShapeDtypeStruct(wide, BF16)],
        compiler_params=pltpu.CompilerParams(dimension_semantics=("arbitrary",), vmem_limit_bytes=MM_VMEM_LIMIT),
    )(h, wg, wu)


def _ffn_mid_bwd(dy, wd, g, u):
    t, n = dy.shape
    n_sh, cc, _ = wd.shape

    def body(dy_ref, wd_ref, g_ref, u_ref, dg_ref, du_ref):
        d_act = _nt(dy_ref[...].astype(BF16), wd_ref[...])
        _, vjp = jax.vjp(_f_swiglu, g_ref[...], u_ref[...])
        dg, du = vjp((d_act,))
        dg_ref[...] = dg.astype(BF16)
        du_ref[...] = du.astype(BF16)

    blk = pl.BlockSpec((MM_ROW_TILE, cc), lambda j, i: (i, j))
    wide = jax.ShapeDtypeStruct((t, n_sh * cc), BF16)
    return pl.pallas_call(
        body, name="ffn_mid_bwd", grid=(n_sh, t // MM_ROW_TILE),
        in_specs=[pl.BlockSpec((MM_ROW_TILE, n), lambda j, i: (i, 0)),
                  pl.BlockSpec((None, cc, n), lambda j, i: (j, 0, 0)), blk, blk],
        out_specs=[blk, blk], out_shape=[wide, wide],
        compiler_params=pltpu.CompilerParams(dimension_semantics=("arbitrary", "arbitrary"),
                                             vmem_limit_bytes=MM_VMEM_LIMIT),
    )(dy, wd, g, u)


def _ffn_dh(dg, du, wg, wu):
    t = dg.shape[0]
    n_sh, k, cc = wg.shape

    def body(dg_ref, du_ref, wg_ref, wu_ref, o_ref):
        acc = jnp.zeros((MM_ROW_TILE, k), F32)
        for j in range(n_sh):
            cols = slice(j * cc, (j + 1) * cc)
            acc = acc + _nt(dg_ref[:, cols], wg_ref[j]) + _nt(du_ref[:, cols], wu_ref[j])
        o_ref[...] = acc

    blk = pl.BlockSpec((MM_ROW_TILE, n_sh * cc), lambda i: (i, 0))
    w_spec = pl.BlockSpec((n_sh, k, cc), lambda i: (0, 0, 0))
    return pl.pallas_call(
        body, name="ffn_dh", grid=(t // MM_ROW_TILE,),
        in_specs=[blk, blk, w_spec, w_spec],
        out_specs=pl.BlockSpec((MM_ROW_TILE, k), lambda i: (i, 0)),
        out_shape=jax.ShapeDtypeStruct((t, k), F32),
        compiler_params=pltpu.CompilerParams(dimension_semantics=("arbitrary",), vmem_limit_bytes=MM_VMEM_LIMIT),
    )(dg, du, wg, wu)


def _ffn_dw_in(h, dy, n_sh, name):
    t, k = h.shape
    cc = dy.shape[1] // n_sh
    tk = 512

    def body(h_ref, dy_ref, o_ref):
        o_ref[...] = _tn(h_ref[...].astype(BF16), dy_ref[...]).astype(BF16)

    return pl.pallas_call(
        body, name=name, grid=(n_sh, k // tk),
        in_specs=[pl.BlockSpec((t, tk), lambda j, i: (0, i)), pl.BlockSpec((t, cc), lambda j, i: (0, j))],
        out_specs=pl.BlockSpec((None, tk, cc), lambda j, i: (j, i, 0)),
        out_shape=jax.ShapeDtypeStruct((n_sh, k, cc), BF16),
        compiler_params=pltpu.CompilerParams(dimension_semantics=("arbitrary", "arbitrary"),
                                             vmem_limit_bytes=MM_VMEM_LIMIT),
    )(h, dy)


@jax.custom_vjp
def _ffn_block(h, wg, wu, wd):
    act = _ffn_in(h, wg, wu)[2]
    return _mm(act, wd.reshape(-1, wd.shape[2]), "nn", "ffn_down_fwd", MM_ROW_TILE, wd.shape[2])


def _ffn_block_fwd(h, wg, wu, wd):
    g, u, act = _ffn_in(h, wg, wu)
    y = _mm(act, wd.reshape(-1, wd.shape[2]), "nn", "ffn_down_fwd", MM_ROW_TILE, wd.shape[2])
    return y, (h, wg, wu, wd, g, u, act)


def _ffn_block_bwd(res, dy):
    h, wg, wu, wd, g, u, act = res
    dg, du = _ffn_mid_bwd(dy, wd, g, u)
    dh = _ffn_dh(dg, du, wg, wu)
    n_sh = wg.shape[0]
    dwg = _ffn_dw_in(h, dg, n_sh, "ffn_gate_dw")
    dwu = _ffn_dw_in(h, du, n_sh, "ffn_up_dw")
    dwd = _mm(act, dy, "tn", "ffn_down_dw", 256, wd.shape[2], out_dtype=BF16).reshape(wd.shape)
    return dh, dwg, dwu, dwd


_ffn_block.defvjp(_ffn_block_fwd, _ffn_block_bwd)


def _swap_halves(w):
    half = w.shape[-1] // 2
    return jnp.concatenate([w[..., half:], w[..., :half]], axis=-1)


def _pad_lanes(w):
    return jnp.concatenate([w, jnp.zeros(w.shape[:-1] + (LANES - w.shape[-1],), w.dtype)], axis=-1)


def _join_cols(shards):
    return shards.transpose(1, 0, 2).reshape(shards.shape[1], -1)


def _mod_parts(mod):
    return [mod[:, i * D_MODEL:(i + 1) * D_MODEL] for i in range(N_MOD)]


def _local_loss(x, mod, p, cos, sin, target):
    return _ffn_stage(x, _mixing_stage(x, mod, p, cos, sin), mod, p, target)


def _mixing_stage(x, mod, p, cos, sin):
    shift1, scale1 = _mod_parts(mod)[:2]

    w_in = _join_cols(p["w_in"])
    k_rope_w = w_in[:, 2176:2240]
    w_in_ext = jnp.concatenate([w_in[:, :2176], _pad_lanes(k_rope_w), _pad_lanes(_swap_halves(k_rope_w)),
                                jnp.zeros((D_MODEL, LANES), w_in.dtype)], axis=1)
    (h1,) = _make_rowwise("pre_attn", _f_pre_attn, 1, 3, [D_MODEL], [True], out_dtypes=[BF16])(
        x, p["norm_attn"], scale1, shift1)
    q_sb, k_sb, v_sb, cq, ckv, kr, kr_sw = _make_linear_split(
        "in_proj", (SB_WIDTH, SB_WIDTH, SB_WIDTH, MLA_Q_RANK, MLA_KV_RANK, LANES, LANES), 512)(h1, w_in_ext)

    o_sb = _sb_attention(q_sb, k_sb, v_sb)

    wq = _join_cols(p["w_q_up"]).reshape(MLA_Q_RANK, MLA_HEADS, MLA_QK)
    wq_n, wq_r = wq[:, :, :MLA_NOPE], wq[:, :, MLA_NOPE:]
    w_q_ext = jnp.concatenate([wq_n.reshape(MLA_Q_RANK, -1), _pad_lanes(wq_r).reshape(MLA_Q_RANK, -1),
                               _pad_lanes(_swap_halves(wq_r)).reshape(MLA_Q_RANK, -1)], axis=1)
    wkv = _join_cols(p["w_kv_up"]).reshape(MLA_KV_RANK, MLA_HEADS, MLA_NOPE + MLA_V)
    w_kv_ext = jnp.concatenate([wkv[:, :, :MLA_NOPE].reshape(MLA_KV_RANK, -1),
                                wkv[:, :, MLA_NOPE:].reshape(MLA_KV_RANK, -1)], axis=1)
    cqn, ckvn = _make_rowwise("mla_a", _f_mla_a, 2, 2, [MLA_Q_RANK, MLA_KV_RANK], [True, True],
                              out_dtypes=[BF16, BF16], grad_dtypes=[BF16, BF16])(
        cq, ckv, p["q_a_norm"], p["kv_a_norm"])
    qall = _make_linear("q_up", 384, 768)(cqn, w_q_ext)
    kn_all, v_mla = _make_linear_split("kv_up", (MLA_HEADS * MLA_NOPE, MLA_HEADS * MLA_V), MLA_KV_RANK)(ckvn, w_kv_ext)
    gq = p["q_norm"]
    gkr = p["k_rope_norm"]
    qn, qr, kn, krr = _make_rowwise("mla_b", _f_mla_b, 6, 6, [512, 512, 512, LANES],
                                    [True, True, True, True, False, False],
                                    out_dtypes=[BF16] * 4, grad_dtypes=[BF16] * 4)(
        qall, kn_all, kr, kr_sw, cos, sin,
        gq[:, :MLA_NOPE], _pad_lanes(gq[:, MLA_NOPE:]), _pad_lanes(_swap_halves(gq[:, MLA_NOPE:])),
        p["k_nope_norm"], _pad_lanes(gkr), _pad_lanes(_swap_halves(gkr)))
    o_mla = _mla_attention(qn, qr, kn, krr, v_mla)

    (mixed,) = _make_rowwise("post_attn", _f_post_attn, 2, 2, [D_MODEL], [True, True])(
        o_sb, o_mla, p["out_norm_sb"], p["out_norm_mla"])
    return mixed


def _ffn_stage(x, mixed, mod, p, target):
    _, _, gate1, shift2, scale2, gate2 = _mod_parts(mod)
    attn = _make_linear("out_proj", 512, 512)(mixed, p["w_out"].reshape(D_MODEL, D_MODEL))

    x2, h2 = _make_rowwise("pre_ffn", _f_pre_ffn, 2, 4, [D_MODEL, D_MODEL], [True, True],
                           out_dtypes=[F32, BF16], grad_dtypes=[F32, BF16])(
        x, attn, gate1, p["norm_ffn"], scale2, shift2)
    ffn = _ffn_block(h2, p["w_gate"], p["w_up"], p["w_down"])
    (row_loss,) = _make_rowwise("loss", _f_loss, 3, 1, [1], [True, True, False], grad_dtypes=[F32, BF16])(
        x2, ffn, target, gate2)
    return 0.5 * jnp.sum(row_loss)


def _my_place():
    return lax.axis_index("x"), lax.axis_index("y"), lax.axis_index("c")


def _all_gather_small(block, name):
    m_per, n = block.shape

    def body(x_ref, out_ref, send_sems, recv_sems, local_sem):
        x, y, c = _my_place()
        me, sibling = (x, y, c), (x, y, 1 - c)
        chips = [(1 - x, y), (x, 1 - y), (1 - x, 1 - y)]

        def rows(px, py, pc):
            return out_ref.at[pl.ds((4 * px + 2 * py + pc) * m_per, m_per), :]

        def copy(k, blk, to, src=None):
            return pltpu.make_async_remote_copy(
                src_ref=rows(*blk) if src is None else src, dst_ref=rows(*blk),
                send_sem=send_sems.at[k], recv_sem=recv_sems.at[k], device_id=to, device_id_type=MESH)

        mine = pltpu.make_async_copy(x_ref, rows(*me), local_sem)
        mine.start()
        first = [copy(0, me, sibling, src=x_ref)]
        first += [copy(1 + j, me, (*chip, c), src=x_ref) for j, chip in enumerate(chips)]
        for cp in first:
            cp.start()
        passed = [copy(4 + j, (*chip, c), sibling) for j, chip in enumerate(chips)]
        for j, chip in enumerate(chips):
            copy(1 + j, (*chip, c), me).wait_recv()
            passed[j].start()
        copy(0, sibling, me).wait_recv()
        for j, chip in enumerate(chips):
            copy(4 + j, (*chip, 1 - c), me).wait_recv()
        for cp in first + passed:
            cp.wait_send()
        mine.wait()

    return pl.pallas_call(
        body, name=name,
        out_shape=jax.ShapeDtypeStruct((N_DEV * m_per, n), block.dtype),
        in_specs=[pl.BlockSpec(memory_space=pltpu.VMEM)],
        out_specs=pl.BlockSpec(memory_space=pltpu.VMEM),
        scratch_shapes=[pltpu.SemaphoreType.DMA((7,)), pltpu.SemaphoreType.DMA((7,)), pltpu.SemaphoreType.DMA],
    )(block)


EARLY = ("w_in", "w_q_up", "w_kv_up")
LATE = ("w_out", "w_gate", "w_up", "w_down")
BIG = EARLY + LATE
TRANSPOSED_UPDATE = ("w_in", "w_gate", "w_up")
HALF_AXIS = {"w_in": 0, "w_q_up": 0, "w_kv_up": 0, "w_out": 0, "w_gate": 0, "w_up": 0, "w_down": 1}


def _half(ref, h, axis, lead=()):
    trail = ref.shape[len(lead):]
    idx = list(lead) + [slice(None)] * len(trail)
    at = len(trail) - 2 + axis
    n2 = trail[at] // 2
    idx[len(lead) + at] = pl.ds(h * n2, n2)
    return ref.at[tuple(idx)]


def _half_shape(shape, axis):
    shape = list(shape)
    shape[len(shape) - 2 + axis] //= 2
    return tuple(shape)


def _remote(src, dst, send_sems, recv_sems, k, to):
    return pltpu.make_async_remote_copy(src_ref=src, dst_ref=dst, send_sem=send_sems.at[k],
                                        recv_sem=recv_sems.at[k], device_id=to, device_id_type=MESH)


def _gather_weights(names, shards, after):
    n_w = len(shards)
    axes = [HALF_AXIS[n] for n in names]

    def body(*refs):
        w_refs, out_refs, token = refs[:n_w], refs[n_w + 1:2 * n_w + 1], refs[2 * n_w + 1]
        send_sems, recv_sems, local_sems = refs[2 * n_w + 2:]
        token[...] = jnp.zeros_like(token)
        x, y, c = _my_place()
        sibling = (x, y, 1 - c)
        chips = [(1 - x, y), (x, 1 - y), (1 - x, 1 - y)]
        me = 2 * x + y
        mine =[pltpu.make_async_copy(w, o.at[me], local_sems.at[i]) for i, (w, o) in enumerate(zip(w_refs, out_refs))]
        for cp in mine:
            cp.start()
        first = [_remote(_half(w_refs[i], c, axes[i]), _half(out_refs[i], c, axes[i], (me,)),
                         send_sems, recv_sems, 6 * i + j, (*chip, c))
                 for i in range(n_w) for j, chip in enumerate(chips)]
        for cp in first:
            cp.start()
        passed = []
        for j, (cx, cy) in enumerate(chips):
            for i in range(n_w):
                blk = _half(out_refs[i], c, axes[i], (2 * cx + cy,))
                _remote(blk, blk, send_sems, recv_sems, 6 * i + j, (cx, cy, c)).wait_recv()
                cp = _remote(blk, blk, send_sems, recv_sems, 6 * i + 3 + j, sibling)
                cp.start()
                passed.append(cp)
        for j, (cx, cy) in enumerate(chips):
            for i in range(n_w):
                blk = _half(out_refs[i], 1 - c, axes[i], (2 * cx + cy,))
                _remote(blk, blk, send_sems, recv_sems, 6 * i + 3 + j, sibling).wait_recv()
        for cp in first + passed:
            cp.wait_send()
        for cp in mine:
            cp.wait()

    outs = pl.pallas_call(
        body, name="gather_weights",
        out_shape=[jax.ShapeDtypeStruct((N_CHIPS,) + s.shape, s.dtype) for s in shards]
        + [jax.ShapeDtypeStruct((8, LANES), F32)],
        in_specs=[ANY] * (n_w + 1), out_specs=[ANY] * n_w + [pl.BlockSpec(memory_space=pltpu.VMEM)],
        scratch_shapes=[pltpu.SemaphoreType.DMA((6 * n_w,)), pltpu.SemaphoreType.DMA((6 * n_w,)),
                        pltpu.SemaphoreType.DMA((n_w,))],
    )(*shards, after)
    return outs[:n_w], outs[n_w]


def _sibling_join(halves, name, after):
    n_w = len(halves)

    def body(*refs):
        s_refs, j_refs = refs[:n_w], refs[n_w + 1:2 * n_w + 1]
        send_sems, recv_sems = refs[2 * n_w + 1:]
        x, y, c = _my_place()
        sends = [_remote(s_refs[i], j_refs[i], send_sems, recv_sems, i, (x, y, 1 - c)) for i in range(n_w)]
        for cp in sends:
            cp.start()
        for cp in sends:
            cp.wait_recv()
        for cp in sends:
            cp.wait_send()

    return pl.pallas_call(
        body, name=name,
        out_shape=[jax.ShapeDtypeStruct(s.shape, s.dtype) for s in halves],
        in_specs=[ANY] * (n_w + 1), out_specs=[ANY] * n_w,
        scratch_shapes=[pltpu.SemaphoreType.DMA((n_w,)), pltpu.SemaphoreType.DMA((n_w,))],
    )(*halves, after)


HBM_SPEC = pl.BlockSpec(memory_space=pltpu.HBM)
SEM_SPEC = pl.BlockSpec(memory_space=pltpu.SEMAPHORE)
DATAFLOW = pltpu.SideEffectType.DATAFLOW_SIDE_EFFECTING


def _in_hbm(a):
    return pltpu.with_memory_space_constraint(a, pltpu.HBM)


def _exchange_start(name, srcs, lands, plan, n_copies, after, thru):
    n = len(srcs)

    def body(*refs):
        src_refs, land_refs = refs[:n], refs[n:2 * n]
        send_sems, recv_sems = refs[2 * n + 2], refs[2 * n + 3]
        for k, (src, dst, to, k_recv) in enumerate(plan(src_refs, land_refs)):
            pltpu.make_async_remote_copy(src_ref=src, dst_ref=dst, send_sem=send_sems.at[k],
                                         recv_sem=recv_sems.at[k_recv], device_id=to, device_id_type=MESH).start()

    outs = pl.pallas_call(
        body, name=name,
        out_shape=(pltpu.SemaphoreType.DMA((n_copies,)), pltpu.SemaphoreType.DMA((n_copies,)),
                   *[pltpu.HBM(a.shape, a.dtype) for a in list(srcs) + list(lands) + [thru]]),
        in_specs=[HBM_SPEC] * (2 * n + 1) + [ANY],
        out_specs=(SEM_SPEC, SEM_SPEC, *[HBM_SPEC] * (2 * n + 1)),
        input_output_aliases={i: 2 + i for i in range(2 * n + 1)},
        compiler_params=pltpu.CompilerParams(has_side_effects=DATAFLOW),
    )(*[_in_hbm(a) for a in list(srcs) + list(lands) + [thru]], after)
    return outs[0], outs[1], outs[2:2 + n], outs[2 + n:2 + 2 * n], outs[2 + 2 * n]


def _exchange_wait(name, started, plan, after):
    send_sems, recv_sems, srcs, lands, _ = started
    n = len(srcs)

    def body(*refs):
        src_refs, land_refs = refs[:n], refs[n:2 * n]
        s_sems, r_sems = refs[2 * n], refs[2 * n + 1]
        for k, (src, dst, to, _) in enumerate(plan(src_refs, land_refs)):
            cp = _remote(src, dst, s_sems, r_sems, k, to)
            cp.wait_send()
            cp.wait_recv()

    outs = pl.pallas_call(
        body, name=name,
        out_shape=tuple(pltpu.HBM(a.shape, a.dtype) for a in list(srcs) + list(lands)),
        in_specs=[HBM_SPEC] * (2 * n) + [SEM_SPEC, SEM_SPEC, ANY],
        out_specs=tuple([HBM_SPEC] * (2 * n)),
        input_output_aliases={i: i for i in range(2 * n)},
        compiler_params=pltpu.CompilerParams(has_side_effects=DATAFLOW),
    )(*srcs, *lands, send_sems, recv_sems, after)
    return outs[:n], outs[n:]


def _late_gather_plan(src_refs, land_refs):
    x, y, c = _my_place()
    chips = [(1 - x, y), (x, 1 - y), (1 - x, 1 - y)]
    plan = [(src, land.at[2 * x + y], (cx, cy, c)) for src, land in zip(src_refs, land_refs) for cx, cy in chips]
    return [entry + (k,) for k, entry in enumerate(plan)]


def _direct_scatter_plan(names):
    axes = [HALF_AXIS[n] for n in names]

    def plan(src_refs, land_refs):
        x, y, c = _my_place()
        chips = [(1 - x, y), (x, 1 - y), (1 - x, 1 - y)]
        out = []
        for i, (src, land) in enumerate(zip(src_refs, land_refs)):
            for f, (cx, cy) in enumerate(chips):
                for core in range(2):
                    out.append((_half(src, core, axes[i], (2 * cx + cy,)), land.at[2 * f + c], (cx, cy, core),
                                7 * i + 2 * f + c))
            out.append((_half(src, 1 - c, axes[i], (2 * x + y,)), land.at[6], (x, y, 1 - c), 7 * i + 6))
        return out

    return plan


def _row_tile(rows, mult=16, limit=ROW_TILE):
    return max(d for d in range(mult, limit + 1, mult) if rows % d == 0)


def _chip_sum_direct(place, g, parts, axis, name, transposed):
    n_parts, rr, cc = parts.shape
    tr = _row_tile(rr, LANES) if transposed else _row_tile(rr, limit=1024)
    nb = rr // tr
    if axis == 0:
        g_map = lambda i, pr: (pr[1], pr[0] * nb + i, 0)
    else:
        g_map = lambda i, pr: (pr[1], i, pr[0])

    def body(pr, g_ref, p_ref, o_ref):
        acc = p_ref[0].astype(F32)
        for j in range(1, n_parts):
            acc = acc + p_ref[j].astype(F32)
        acc = acc + g_ref[...].astype(F32)
        o_ref[...] = (acc.T if transposed else acc).astype(BF16)

    out_spec = pl.BlockSpec((cc, tr), lambda i, pr: (0, i)) if transposed else pl.BlockSpec((tr, cc), lambda i, pr: (i, 0))
    return pl.pallas_call(
        body, name=name,
        grid_spec=pltpu.PrefetchScalarGridSpec(
            num_scalar_prefetch=1, grid=(nb,),
            in_specs=[pl.BlockSpec((None, tr, cc), g_map), pl.BlockSpec((n_parts, tr, cc), lambda i, pr: (0, i, 0))],
            out_specs=out_spec),
        out_shape=jax.ShapeDtypeStruct((cc, rr) if transposed else (rr, cc), BF16))(place, g, parts)


def _silu(v):
    return v / (1.0 + jnp.exp(-v))


def _ada_fwd(c_all, w_shard, b_shard):
    def body(c_ref, w_ref, b_ref, o_ref):
        o_ref[...] = jnp.dot(_silu(c_ref[...]), w_ref[...], precision=lax.Precision.HIGHEST,
                             preferred_element_type=F32) + b_ref[...]

    return pl.pallas_call(body, name="ada_fwd", out_shape=jax.ShapeDtypeStruct((c_all.shape[0], w_shard.shape[1]), F32),
                          compiler_params=pltpu.CompilerParams(vmem_limit_bytes=MM_VMEM_LIMIT))(c_all, w_shard, b_shard)


def _ada_bwd(c_all, dmod_cols):
    def body(c_ref, d_ref, o_ref):
        o_ref[...] = lax.dot_general(_silu(c_ref[...]), d_ref[...], (((0,), (0,)), ((), ())),
                                     precision=lax.Precision.HIGHEST, preferred_element_type=F32)

    return pl.pallas_call(body, name="ada_bwd", out_shape=jax.ShapeDtypeStruct((c_all.shape[1], dmod_cols.shape[1]), F32),
                          compiler_params=pltpu.CompilerParams(vmem_limit_bytes=MM_VMEM_LIMIT))(c_all, dmod_cols)


def _adamw_math(w, g, m, v):
    m = ADAM_B1 * m + (1.0 - ADAM_B1) * g
    v = ADAM_B2 * v + (1.0 - ADAM_B2) * (g * g)
    m_hat = m / (1.0 - ADAM_B1 ** ADAM_STEP)
    v_hat = v / (1.0 - ADAM_B2 ** ADAM_STEP)
    delta = -ADAM_LR * (m_hat / (jnp.sqrt(v_hat) + ADAM_EPS) + ADAM_WD * w)
    return delta, m, v


def _adamw(w, g, m, v, name):
    r, ccols = w.shape
    tr = max(d for d in range(8, ROW_TILE + 1, 8) if r % d == 0)
    spec = pl.BlockSpec((tr, ccols), lambda i: (i, 0))

    def body(w_ref, g_ref, m_ref, v_ref, d_ref, nm_ref, nv_ref):
        d_ref[...], nm_ref[...], nv_ref[...] = _adamw_math(w_ref[...], g_ref[...], m_ref[...], v_ref[...])

    return pl.pallas_call(body, name=name, grid=(r // tr,), in_specs=[spec] * 4, out_specs=[spec] * 3,
                          out_shape=[jax.ShapeDtypeStruct(w.shape, F32)] * 3,
                          compiler_params=pltpu.CompilerParams(vmem_limit_bytes=MM_VMEM_LIMIT))(w, g, m, v)


def _small_layout(sizes):
    offs, off = [], 0
    for n in sizes:
        offs.append(off)
        off += -(-n // LANES) * LANES
    total = -(-(off + LANES) // (8 * LANES)) * (8 * LANES)
    return offs, off, total


def _adamw_small(ws, g_all, ms, vs, offs, loss_off):
    n_p = len(ws)

    def device_sum(g_ref, off, width):
        blk = g_ref[:, off:off + width]
        acc = blk[0:1]
        for d in range(1, N_DEV):
            acc = acc + blk[d:d + 1]
        return acc

    def body(*refs):
        w_refs, m_refs, v_refs = refs[:n_p], refs[n_p:2 * n_p], refs[2 * n_p:3 * n_p]
        g_ref = refs[3 * n_p]
        outs = refs[3 * n_p + 1:]
        for i in range(n_p):
            n = w_refs[i].shape[1]
            g = device_sum(g_ref, offs[i], -(-n // LANES) * LANES)[:, :n]
            outs[i][...] = g
            outs[n_p + i][...], outs[2 * n_p + i][...], outs[3 * n_p + i][...] = _adamw_math(
                w_refs[i][...], g, m_refs[i][...], v_refs[i][...])
        outs[4 * n_p][...] = device_sum(g_ref, loss_off, LANES)

    res = pl.pallas_call(
        body, name="adamw_small",
        out_shape=[jax.ShapeDtypeStruct(a.shape, F32) for a in list(ws) * 4] + [jax.ShapeDtypeStruct((1, LANES), F32)],
    )(*ws, *ms, *vs, g_all)
    return res[:n_p], res[n_p:2 * n_p], res[2 * n_p:3 * n_p], res[3 * n_p:4 * n_p], res[4 * n_p]


def _adamw_halves(place, w, own, sib, m, v, axis, name, after):
    r, cc = w.shape
    if axis == 0:
        rows, gc = own.shape[0], own.shape[1]
        tr = _row_tile(rows)
        nb = rows // tr
        w_spec = pl.BlockSpec((tr, cc), lambda h, i, pr: (h * nb + i, 0))
        g_spec = pl.BlockSpec((tr, gc), lambda h, i, pr: (i, 0))
    else:
        tr = _row_tile(r)
        nb = r // tr
        gc = own.shape[1]
        w_spec = pl.BlockSpec((tr, gc), lambda h, i, pr: (i, h))
        g_spec = pl.BlockSpec((tr, gc), lambda h, i, pr: (i, 0))
    wc = w_spec.block_shape[1]

    def body(pr, w_ref, o_ref, s_ref, m_ref, v_ref, after_ref, g_ref, d_ref, nm_ref, nv_ref):
        g = jnp.where(pl.program_id(0) == pr[0], o_ref[...], s_ref[...]).astype(F32)[:, :wc]
        g_ref[...] = g
        d_ref[...], nm_ref[...], nv_ref[...] = _adamw_math(w_ref[...], g, m_ref[...], v_ref[...])

    return pl.pallas_call(
        body, name=name,
        grid_spec=pltpu.PrefetchScalarGridSpec(
            num_scalar_prefetch=1, grid=(2, nb),
            in_specs=[w_spec, g_spec, g_spec, w_spec, w_spec, ANY], out_specs=[w_spec] * 4),
        out_shape=[jax.ShapeDtypeStruct(w.shape, F32)] * 4,
        compiler_params=pltpu.CompilerParams(vmem_limit_bytes=MM_VMEM_LIMIT))(place, w, own, sib, m, v, after)


SMALL = ("b_ada", "norm_attn", "norm_ffn", "q_a_norm", "kv_a_norm", "q_norm", "k_nope_norm", "k_rope_norm",
         "out_norm_sb", "out_norm_mla")
WEIGHTS = ("w_ada", "b_ada", "norm_attn", "norm_ffn", "w_in", "q_a_norm", "w_q_up", "kv_a_norm", "w_kv_up",
           "q_norm", "k_nope_norm", "k_rope_norm", "out_norm_sb", "out_norm_mla", "w_out", "w_gate", "w_up",
           "w_down")


def kernel(x, c, positions, w_ada, b_ada, norm_attn, norm_ffn, w_in, q_a_norm, w_q_up, kv_a_norm, w_kv_up, q_norm, k_nope_norm, k_rope_norm, out_norm_sb, out_norm_mla, w_out, w_gate, w_up, w_down, loss_target, m_w_ada, m_b_ada, m_norm_attn, m_norm_ffn, m_w_in, m_q_a_norm, m_w_q_up, m_kv_a_norm, m_w_kv_up, m_q_norm, m_k_nope_norm, m_k_rope_norm, m_out_norm_sb, m_out_norm_mla, m_w_out, m_w_gate, m_w_up, m_w_down, v_w_ada, v_b_ada, v_norm_attn, v_norm_ffn, v_w_in, v_q_a_norm, v_w_q_up, v_kv_a_norm, v_w_kv_up, v_q_norm, v_k_nope_norm, v_k_rope_norm, v_out_norm_sb, v_out_norm_mla, v_w_out, v_w_gate, v_w_up, v_w_down):
    local = dict(locals())
    w = {n: local[n][0] for n in WEIGHTS}
    m = {n: local["m_" + n][0] for n in WEIGHTS}
    v = {n: local["v_" + n][0] for n in WEIGHTS}
    small = {n: w[n].reshape(1, -1) for n in SMALL}
    ix, iy, ic = _my_place()
    chip = 2 * ix + iy
    dev = 2 * chip + ic
    xs, target = x[0], loss_target[0]
    seq = xs.shape[0]

    c_all = _all_gather_small(c.reshape(8, LANES), "gather_c").reshape(N_DEV, D_MODEL)
    ada_cols = w["w_ada"].shape[1]
    b_cols = lax.dynamic_slice_in_dim(small["b_ada"], chip * ada_cols, ada_cols, axis=1)
    mod_cols = _ada_fwd(c_all, w["w_ada"], b_cols)
    mod_all = _all_gather_small(mod_cols, "gather_mod").reshape(N_CHIPS, 2, N_DEV, ada_cols)
    mod = lax.dynamic_index_in_dim(mod_all[:, 0], dev, axis=1, keepdims=False).reshape(1, N_MOD * D_MODEL)

    ff_pad = FF_SHARD_PAD - FF_SHARD
    pads = {"w_gate": ((0, 0), (0, ff_pad)), "w_up": ((0, 0), (0, ff_pad)), "w_down": ((0, ff_pad), (0, 0))}
    shards = {n: jnp.pad(w[n].astype(BF16), pads[n]) if n in pads else w[n].astype(BF16) for n in BIG}
    early, early_done = _gather_weights(EARLY, [shards[n] for n in EARLY], mod)
    gathered = dict(zip(EARLY, early))
    lands = [lax.dynamic_update_index_in_dim(lax.empty((N_CHIPS,) + shards[n].shape, BF16), shards[n], chip, 0)
             for n in LATE]
    late_gather = _exchange_start("gather_late_start", [shards[n] for n in LATE], lands, _late_gather_plan,
                                  3 * len(LATE), early_done, mod)
    mod = late_gather[4]

    half = MLA_ROPE // 2
    freqs = 1.0 / (ROPE_THETA ** (np.arange(half, dtype=np.float32) / half))
    zeros = np.zeros(LANES - MLA_ROPE, np.float32)
    freqs_row = jnp.asarray(np.concatenate([freqs, freqs, zeros]).astype(np.float32)[None])
    sign_row = jnp.asarray(np.concatenate([-np.ones(half), np.ones(half), zeros]).astype(np.float32)[None])
    cos, sin = _rope_tables(positions.reshape(seq, 1), freqs_row, sign_row)

    place = jnp.stack([ic, chip]).astype(jnp.int32)
    small_params = {n: small[n] for n in SMALL if n != "b_ada"}

    p1 = {**{n: gathered[n] for n in EARLY}, **small_params}
    mixed, mixing_vjp = jax.vjp(lambda x_, mod_, p_: _mixing_stage(x_, mod_, p_, cos, sin), xs, mod, p1)
    _, landed = _exchange_wait("gather_late_wait", late_gather, _late_gather_plan, mixed)
    p2 = {**dict(zip(LATE, landed)), **small_params}
    loss_part, ffn_vjp = jax.vjp(lambda x_, mixed_, mod_, p_: _ffn_stage(x_, mixed_, mod_, p_, target), xs, mixed, mod, p2)
    gx2, gmixed, gmod2, gp2 = ffn_vjp(jnp.ones((), F32))
    late_grads = [gp2[n] for n in LATE]
    late_plan = _direct_scatter_plan(LATE)
    late_scatter = _exchange_start(
        "grad_scatter_late_start", late_grads,
        [lax.empty((7,) + _half_shape(gr.shape[1:], HALF_AXIS[n]), BF16) for n, gr in zip(LATE, late_grads)],
        late_plan, 7 * len(LATE), gx2, gmixed)
    gx1, gmod1, gp1 = mixing_vjp(late_scatter[4])
    gx = gx1 + gx2
    gmod = gmod1 + gmod2
    gp = {n: gp1[n] + gp2[n] for n in small_params}

    sizes = [w[n].size for n in SMALL]
    offs, loss_off, n_small = _small_layout(sizes)
    pieces = []
    for n, size in zip(SMALL, sizes):
        pieces.append(gmod if n == "b_ada" else gp[n])
        if size % LANES:
            pieces.append(jnp.zeros((1, LANES - size % LANES), F32))
    pieces += [jnp.full((1, LANES), loss_part), jnp.zeros((1, n_small - loss_off - LANES), F32)]
    small_vec = jnp.concatenate(pieces, axis=1)
    small_all = _all_gather_small(small_vec.reshape(8, n_small // 8), "gather_small").reshape(N_DEV, n_small)

    g, delta, new_m, new_v = {}, {}, {}, {}

    def update(names, own, sib, after):
        for n, o, s in zip(names, own, sib):
            if n in TRANSPOSED_UPDATE:
                res = _adamw_halves(place, w[n].T, o, s, m[n].T, v[n].T, 1, "adamw_" + n, after)
                g[n], delta[n], new_m[n], new_v[n] = [r.T for r in res]
            else:
                g[n], delta[n], new_m[n], new_v[n] = _adamw_halves(place, w[n], o, s, m[n], v[n], HALF_AXIS[n],
                                                                   "adamw_" + n, after)

    late_grads, late_parts = _exchange_wait("grad_scatter_late_wait", late_scatter, late_plan, gx)
    own_late = [_chip_sum_direct(place, gr, pt, HALF_AXIS[n], "grad_chip_sum_" + n, n in TRANSPOSED_UPDATE)
                for n, gr, pt in zip(LATE, late_grads, late_parts)]
    sib_late = _sibling_join(own_late, "grad_sibling_join_late", small_all)
    early_grads = [gp1[n] for n in EARLY]
    early_plan = _direct_scatter_plan(EARLY)
    early_scatter = _exchange_start(
        "grad_scatter_early_start", early_grads,
        [lax.empty((7,) + _half_shape(gr.shape[1:], HALF_AXIS[n]), BF16) for n, gr in zip(EARLY, early_grads)],
        early_plan, 7 * len(EARLY), sib_late[0], small_all)
    small_all = early_scatter[4]
    update(LATE, own_late, sib_late, small_all)

    *small_out, loss_row = _adamw_small([small[n] for n in SMALL], small_all, [m[n].reshape(1, -1) for n in SMALL],
                                        [v[n].reshape(1, -1) for n in SMALL], offs, loss_off)
    loss = loss_row[0, 0]
    for d, outs_d in zip((g, delta, new_m, new_v), small_out):
        d.update({n: o.reshape(w[n].shape) for n, o in zip(SMALL, outs_d)})

    dmod_all = small_all[:, :N_MOD * D_MODEL]
    g["w_ada"] = _ada_bwd(c_all, lax.dynamic_slice_in_dim(dmod_all, chip * ada_cols, ada_cols, axis=1))
    delta["w_ada"], new_m["w_ada"], new_v["w_ada"] = _adamw(w["w_ada"], g["w_ada"], m["w_ada"], v["w_ada"], "adamw_w_ada")

    early_grads, early_parts = _exchange_wait("grad_scatter_early_wait", early_scatter, early_plan, delta["w_ada"])
    own_early = [_chip_sum_direct(place, gr, pt, HALF_AXIS[n], "grad_chip_sum_" + n, n in TRANSPOSED_UPDATE)
                 for n, gr, pt in zip(EARLY, early_grads, early_parts)]
    sib_early = _sibling_join(own_early, "grad_sibling_join_early", delta["w_ada"])
    update(EARLY, own_early, sib_early, sib_early[0])

    def outs(d):
        return [d[n][None] for n in WEIGHTS]

    return (loss, gx[None], *outs(g), *outs(delta), *outs(new_m), *outs(new_v))
```

```python
import numpy as np
import jax
import jax.numpy as jnp
from jax import lax
from jax.experimental import pallas as pl
from jax.experimental.pallas import tpu as pltpu

F32 = jnp.float32
BF16 = jnp.bfloat16
MESH = pl.DeviceIdType.MESH
ANY = pl.BlockSpec(memory_space=pl.ANY)

D_MODEL = 1024
SB_HEADS = 8
SB_HEAD_DIM = 64
SB_WIDTH = 512
MLA_HEADS = 4
MLA_NOPE = 128
MLA_ROPE = 64
MLA_QK = 192
MLA_V = 128
MLA_Q_RANK = 384
MLA_KV_RANK = 256
D_FF = 2816
N_MOD = 6
ROPE_THETA = 10000.0
EPS = 1e-6
LANES = 128

ADAM_LR = 0.001
ADAM_B1 = 0.9
ADAM_B2 = 0.999
ADAM_EPS = 1e-08
ADAM_WD = 0.01
ADAM_STEP = 10

N_CHIPS = 4
N_DEV = 8
ROW_TILE = 256
MM_ROW_TILE = 512
ATT_BLK = 256
MM_VMEM_LIMIT = 56 * 1024 * 1024
FF_SHARD = D_FF // N_CHIPS
FF_SHARD_PAD = 768


def _mm(a, b, mode, name, tm, tn, out_dtype=F32):
    if mode == "nn":
        (m, k), n = a.shape, b.shape[1]
        a_spec = pl.BlockSpec((tm, k), lambda j, i: (i, 0))
        b_spec = pl.BlockSpec((k, tn), lambda j, i: (0, j))
        dims = (((1,), (0,)), ((), ()))
    elif mode == "nt":
        (m, k), n = a.shape, b.shape[0]
        a_spec = pl.BlockSpec((tm, k), lambda j, i: (i, 0))
        b_spec = pl.BlockSpec((tn, k), lambda j, i: (j, 0))
        dims = (((1,), (1,)), ((), ()))
    else:
        (k, m), n = a.shape, b.shape[1]
        a_spec = pl.BlockSpec((k, tm), lambda j, i: (0, i))
        b_spec = pl.BlockSpec((k, tn), lambda j, i: (0, j))
        dims = (((0,), (0,)), ((), ()))
    assert m % tm == 0 and n % tn == 0, (name, m, n, tm, tn)

    def body(a_ref, b_ref, o_ref):
        o_ref[...] = lax.dot_general(a_ref[...].astype(BF16), b_ref[...].astype(BF16), dims,
                                     preferred_element_type=F32).astype(out_dtype)

    return pl.pallas_call(
        body, name=name, grid=(n // tn, m // tm),
        in_specs=[a_spec, b_spec],
        out_specs=pl.BlockSpec((tm, tn), lambda j, i: (i, j)),
        out_shape=jax.ShapeDtypeStruct((m, n), out_dtype),
        compiler_params=pltpu.CompilerParams(dimension_semantics=("arbitrary", "arbitrary"),
                                             vmem_limit_bytes=MM_VMEM_LIMIT),
    )(a, b)


def _make_linear(name, tk_w, tn_w):
    @jax.custom_vjp
    def op(a, w):
        return _mm(a, w, "nn", name + "_fwd", MM_ROW_TILE, w.shape[1])

    def fwd(a, w):
        return op(a, w), (a, w)

    def bwd(res, dy):
        a, w = res
        da = _mm(dy, w, "nt", name + "_dx", MM_ROW_TILE, w.shape[0])
        dw = _mm(a, dy, "tn", name + "_dw", tk_w, tn_w, out_dtype=BF16)
        return da, dw

    op.defvjp(fwd, bwd)
    return op


def _make_linear_split(name, widths, tk_w):
    starts = [sum(widths[:g]) for g in range(len(widths))]

    def call_fwd(a, w):
        t, k = a.shape
        n = w.shape[1]

        def body(a_ref, w_ref, *o_refs):
            y = jnp.dot(a_ref[...].astype(BF16), w_ref[...], preferred_element_type=F32)
            for o_ref, s0, wd in zip(o_refs, starts, widths):
                o_ref[...] = y[:, s0:s0 + wd]

        return pl.pallas_call(
            body, name=name + "_fwd", grid=(t // MM_ROW_TILE,),
            in_specs=[pl.BlockSpec((MM_ROW_TILE, k), lambda i: (i, 0)), pl.BlockSpec((k, n), lambda i: (0, 0))],
            out_specs=[pl.BlockSpec((MM_ROW_TILE, wd), lambda i: (i, 0)) for wd in widths],
            out_shape=[jax.ShapeDtypeStruct((t, wd), F32) for wd in widths],
            compiler_params=pltpu.CompilerParams(dimension_semantics=("arbitrary",), vmem_limit_bytes=MM_VMEM_LIMIT),
        )(a, w)

    def call_dx(dys, w):
        t = dys[0].shape[0]
        k, n = w.shape

        def body(*refs):
            dy_refs, w_ref, o_ref = refs[:-2], refs[-2], refs[-1]
            acc = jnp.zeros((MM_ROW_TILE, k), F32)
            for dy_ref, s0, wd in zip(dy_refs, starts, widths):
                acc = acc + _nt(dy_ref[...].astype(BF16), w_ref[:, s0:s0 + wd])
            o_ref[...] = acc

        return pl.pallas_call(
            body, name=name + "_dx", grid=(t // MM_ROW_TILE,),
            in_specs=[pl.BlockSpec((MM_ROW_TILE, wd), lambda i: (i, 0)) for wd in widths]
            + [pl.BlockSpec((k, n), lambda i: (0, 0))],
            out_specs=pl.BlockSpec((MM_ROW_TILE, k), lambda i: (i, 0)),
            out_shape=jax.ShapeDtypeStruct((t, k), F32),
            compiler_params=pltpu.CompilerParams(dimension_semantics=("arbitrary",), vmem_limit_bytes=MM_VMEM_LIMIT),
        )(*dys, w)

    def call_dw(a, dys, w):
        t, k = a.shape
        n = w.shape[1]

        def body(a_ref, *refs):
            dy_refs, o_ref = refs[:-1], refs[-1]
            ab = a_ref[...].astype(BF16)
            for dy_ref, s0, wd in zip(dy_refs, starts, widths):
                o_ref[:, s0:s0 + wd] = _tn(ab, dy_ref[...].astype(BF16)).astype(BF16)
            if starts[-1] + widths[-1] < n:
                o_ref[:, starts[-1] + widths[-1]:] = jnp.zeros((tk_w, n - starts[-1] - widths[-1]), BF16)

        return pl.pallas_call(
            body, name=name + "_dw", grid=(k // tk_w,),
            in_specs=[pl.BlockSpec((t, tk_w), lambda i: (0, i))]
            + [pl.BlockSpec((t, wd), lambda i: (0, 0)) for wd in widths],
            out_specs=pl.BlockSpec((tk_w, n), lambda i: (i, 0)),
            out_shape=jax.ShapeDtypeStruct((k, n), BF16),
            compiler_params=pltpu.CompilerParams(dimension_semantics=("arbitrary",), vmem_limit_bytes=MM_VMEM_LIMIT),
        )(a, *dys)

    @jax.custom_vjp
    def op(a, w):
        return tuple(call_fwd(a, w))

    def fwd(a, w):
        return op(a, w), (a, w)

    def bwd(res, dys):
        a, w = res
        return call_dx(dys, w), call_dw(a, dys, w)

    op.defvjp(fwd, bwd)
    return op


def _row_spec(arr, tb):
    return pl.BlockSpec((tb, arr.shape[1]), lambda i: (i, 0))


def _full_spec(arr):
    return pl.BlockSpec(arr.shape, lambda i: (0, 0))


def _make_rowwise(name, f, n_rows, n_params, out_cols, diff_rows, out_dtypes=None, grad_dtypes=None):
    n_out = len(out_cols)
    out_dtypes = out_dtypes or [F32] * n_out
    grad_dtypes = grad_dtypes or [F32] * sum(diff_rows)

    def call_fwd(rows, params):
        t = rows[0].shape[0]

        def body(*refs):
            ins = [r[...] for r in refs[:n_rows + n_params]]
            outs = f(*ins)
            for o_ref, o in zip(refs[n_rows + n_params:], outs):
                o_ref[...] = o.astype(o_ref.dtype)

        return pl.pallas_call(
            body, name=name + "_fwd", grid=(t // ROW_TILE,),
            in_specs=[_row_spec(a, ROW_TILE) for a in rows] + [_full_spec(p) for p in params],
            out_specs=[pl.BlockSpec((ROW_TILE, n), lambda i: (i, 0)) for n in out_cols],
            out_shape=[jax.ShapeDtypeStruct((t, n), dt) for n, dt in zip(out_cols, out_dtypes)],
            compiler_params=pltpu.CompilerParams(dimension_semantics=("arbitrary",),
                                                 vmem_limit_bytes=MM_VMEM_LIMIT),
        )(*rows, *params)

    def call_bwd(rows, params, cts):
        t = rows[0].shape[0]
        d_rows = [a for a, d in zip(rows, diff_rows) if d]
        n_in = n_rows + n_params + n_out

        def body(*refs):
            ins = [r[...] for r in refs[:n_rows + n_params]]
            ct = tuple(r[...].astype(F32) for r in refs[n_rows + n_params:n_in])
            _, vjp = jax.vjp(f, *ins)
            grads = vjp(ct)
            out_refs = refs[n_in:]
            g_rows = [g for g, d in zip(grads[:n_rows], diff_rows) if d]
            for o_ref, g in zip(out_refs[:len(g_rows)], g_rows):
                o_ref[...] = g.astype(o_ref.dtype)
            p_refs = out_refs[len(g_rows):]

            if p_refs:
                @pl.when(pl.program_id(0) == 0)
                def _():
                    for p_ref in p_refs:
                        p_ref[...] = jnp.zeros_like(p_ref)

                for p_ref, g in zip(p_refs, grads[n_rows:]):
                    p_ref[...] += g

        return pl.pallas_call(
            body, name=name + "_bwd", grid=(t // ROW_TILE,),
            in_specs=[_row_spec(a, ROW_TILE) for a in rows] + [_full_spec(p) for p in params]
            + [_row_spec(c, ROW_TILE) for c in cts],
            out_specs=[_row_spec(a, ROW_TILE) for a in d_rows] + [_full_spec(p) for p in params],
            out_shape=[jax.ShapeDtypeStruct(a.shape, dt) for a, dt in zip(d_rows, grad_dtypes)]
            + [jax.ShapeDtypeStruct(p.shape, F32) for p in params],
            compiler_params=pltpu.CompilerParams(dimension_semantics=("arbitrary",),
                                                 vmem_limit_bytes=MM_VMEM_LIMIT),
        )(*rows, *params, *cts)

    @jax.custom_vjp
    def op(*args):
        return tuple(call_fwd(args[:n_rows], args[n_rows:]))

    def fwd(*args):
        return op(*args), args

    def bwd(args, cts):
        rows, params = args[:n_rows], args[n_rows:]
        outs = call_bwd(rows, params, cts)
        it = iter(outs)
        g_rows = [next(it) if d else jnp.zeros_like(a) for a, d in zip(rows, diff_rows)]
        return tuple(g_rows) + tuple(it)

    op.defvjp(fwd, bwd)
    return op


def _rms(x, g, n):
    return x * lax.rsqrt(jnp.sum(x * x, axis=-1, keepdims=True) * (1.0 / n) + EPS) * g


def _f_pre_attn(x, g, scale, shift):
    return (_rms(x, g, D_MODEL) * (1.0 + scale) + shift,)


def _f_mla_a(cq, ckv, gq, gkv):
    return _rms(cq, gq, MLA_Q_RANK), _rms(ckv, gkv, MLA_KV_RANK)


@jax.custom_vjp
def _split_lanes(x):
    return tuple(x[:, i * LANES:(i + 1) * LANES] for i in range(x.shape[1] // LANES))


def _split_lanes_fwd(x):
    return _split_lanes(x), None


def _split_lanes_bwd(_, cts):
    return (jnp.concatenate(cts, axis=1),)


_split_lanes.defvjp(_split_lanes_fwd, _split_lanes_bwd)


def _f_mla_b(qall, kn_all, kr, kr_sw, cos, sin, gqn, gqr, gqr_sw, gkn, gkr, gkr_sw):
    q = _split_lanes(qall)
    kn = _split_lanes(kn_all)
    qn_o, qr_o, kn_o = [], [], []
    for h in range(MLA_HEADS):
        qn, qr, qs = q[h], q[MLA_HEADS + h], q[2 * MLA_HEADS + h]
        ss = jnp.sum(qn * qn, axis=-1, keepdims=True) + jnp.sum(qr * qr, axis=-1, keepdims=True)
        rs = lax.rsqrt(ss * (1.0 / MLA_QK) + EPS)
        qn_o.append(qn * rs * gqn)
        qr_o.append((qr * rs * gqr) * cos + (qs * rs * gqr_sw) * sin)
        kn_o.append(_rms(kn[h], gkn, MLA_NOPE))
    rs = lax.rsqrt(jnp.sum(kr * kr, axis=-1, keepdims=True) * (1.0 / MLA_ROPE) + EPS)
    kr_o = (kr * rs * gkr) * cos + (kr_sw * rs * gkr_sw) * sin
    return (jnp.concatenate(qn_o, axis=1), jnp.concatenate(qr_o, axis=1), jnp.concatenate(kn_o, axis=1), kr_o)


def _f_post_attn(o_sb, o_mla, g_sb, g_mla):
    return (jnp.concatenate([_rms(o_sb, g_sb, SB_WIDTH), _rms(o_mla, g_mla, SB_WIDTH)], axis=1),)


def _f_pre_ffn(x, attn, gate, g, scale, shift):
    x2 = x + gate * attn
    return x2, _rms(x2, g, D_MODEL) * (1.0 + scale) + shift


def _f_swiglu(gt, up):
    return (gt / (1.0 + jnp.exp(-gt)) * up,)


def _f_loss(x2, ffn, target, gate):
    err = x2 + gate * ffn - target
    return (jnp.sum(err * err, axis=-1, keepdims=True) * (1.0 / D_MODEL),)


def _rope_tables(pos_col, freqs, sign):
    t = pos_col.shape[0]

    def body(p_ref, f_ref, s_ref, cos_ref, sin_ref):
        ang = p_ref[...].astype(F32) * f_ref[...]
        live = jnp.abs(s_ref[...])
        cos_ref[...] = jnp.cos(ang) * live
        sin_ref[...] = jnp.sin(ang) * s_ref[...]

    return pl.pallas_call(
        body, name="rope_tables", grid=(t // ROW_TILE,),
        in_specs=[pl.BlockSpec((ROW_TILE, 1), lambda i: (i, 0)), _full_spec(freqs), _full_spec(sign)],
        out_specs=[pl.BlockSpec((ROW_TILE, LANES), lambda i: (i, 0))] * 2,
        out_shape=[jax.ShapeDtypeStruct((t, LANES), F32)] * 2,
    )(pos_col, freqs, sign)


def _hi_lo_dot(x, tri):
    hi = x.astype(BF16)
    lo = (x - hi.astype(F32)).astype(BF16)
    return (jnp.dot(hi, tri, preferred_element_type=F32) + jnp.dot(lo, tri, preferred_element_type=F32))


def _tri(cmp):
    r = lax.broadcasted_iota(jnp.int32, (ATT_BLK, ATT_BLK), 0)
    c = lax.broadcasted_iota(jnp.int32, (ATT_BLK, ATT_BLK), 1)
    return cmp(r, c).astype(BF16)


def _nt(a, b):
    return lax.dot_general(a, b, (((1,), (1,)), ((), ())), preferred_element_type=F32)


def _tn(a, b):
    return lax.dot_general(a, b, (((0,), (0,)), ((), ())), preferred_element_type=F32)


def _sb_logs(z):
    lb = jnp.minimum(z, 0.0) - jnp.log(1.0 + jnp.exp(-jnp.abs(z)))
    return lb, lb - z


def _sb_fwd(q, k, v):
    t = q.shape[0]
    nq = t // ATT_BLK
    scale = SB_HEAD_DIM ** -0.5

    def body(q_ref, k_ref, v_ref, o_ref, tot_ref):
        qi = pl.program_id(1)
        lane = lax.broadcasted_iota(jnp.int32, (ATT_BLK, LANES), 1)
        tri = _tri(lambda r, c: r > c)
        qv = q_ref[...] * scale
        heads = [(lane // SB_HEAD_DIM) == hh for hh in range(2)]
        qms = [jnp.where(mine, qv, 0.0).astype(BF16) for mine in heads]

        def blocks(kbs, carry, diagonal):
            acc = carry[0]
            nb = len(kbs)
            chains = [(b, hh) for b in range(nb) for hh in range(2)]
            offs = [pl.multiple_of(kb * ATT_BLK, ATT_BLK) for kb in kbs]
            kks = [k_ref[pl.ds(off, ATT_BLK), :].astype(BF16) for off in offs]
            v_blks = [v_ref[pl.ds(off, ATT_BLK), :] for off in offs]
            if any(diagonal):
                valid = (lax.broadcasted_iota(jnp.int32, (ATT_BLK, ATT_BLK), 1)
                         < lax.broadcasted_iota(jnp.int32, (ATT_BLK, ATT_BLK), 0))
            zs = {ch: _nt(qms[ch[1]], kks[ch[0]]) for ch in chains}
            vvs = {(b, hh): jnp.where(heads[hh], v_blks[b], 0.0).astype(BF16) for b, hh in chains}
            logs = {ch: _sb_logs(zs[ch]) for ch in chains}
            l1ms = {ch: jnp.where(valid, logs[ch][1], 0.0) if diagonal[ch[0]] else logs[ch][1] for ch in chains}
            run = {(0, hh): carry[1 + hh] for hh in range(2)}
            for b, hh in chains:
                run[(b + 1, hh)] = run[(b, hh)] + jnp.sum(l1ms[(b, hh)], axis=-1, keepdims=True)
            afters = {ch: _hi_lo_dot(l1ms[ch], tri) for ch in chains}
            ws = {ch: jnp.exp(logs[ch][0] + (afters[ch] + run[ch])) for ch in chains}
            ws = {ch: jnp.where(valid, ws[ch], 0.0) if diagonal[ch[0]] else ws[ch] for ch in chains}
            for ch in chains:
                acc = acc + jnp.dot(ws[ch].astype(BF16), vvs[ch], preferred_element_type=F32)
            return (acc, run[(nb, 0)], run[(nb, 1)])

        zero = jnp.zeros((ATT_BLK, 1), F32)
        init = (jnp.zeros((ATT_BLK, LANES), F32), zero, zero)
        carry = lax.cond(qi % 2 == 1, lambda cr: blocks([qi, qi - 1], cr, (True, False)),
                         lambda cr: blocks([qi], cr, (True,)), init)
        top = qi - 1 - qi % 2
        carry = lax.fori_loop(0, qi // 2, lambda pr, cr: blocks([top - 2 * pr, top - 1 - 2 * pr], cr, (False, False)),
                              carry)
        o_ref[...] = carry[0]
        for hh in range(2):
            tot_ref[:, hh * LANES:(hh + 1) * LANES] = jnp.broadcast_to(carry[1 + hh], (ATT_BLK, LANES))

    return pl.pallas_call(
        body, name="sb_attn_fwd", grid=(SB_HEADS // 2, nq),
        in_specs=[pl.BlockSpec((ATT_BLK, LANES), lambda p, i: (i, p)),
                  pl.BlockSpec((t, LANES), lambda p, i: (0, p)),
                  pl.BlockSpec((t, LANES), lambda p, i: (0, p))],
        out_specs=[pl.BlockSpec((ATT_BLK, LANES), lambda p, i: (i, p)),
                   pl.BlockSpec((ATT_BLK, 2 * LANES), lambda p, i: (i, p))],
        out_shape=[jax.ShapeDtypeStruct((t, SB_WIDTH), F32), jax.ShapeDtypeStruct((t, SB_HEADS * LANES), F32)],
        compiler_params=pltpu.CompilerParams(dimension_semantics=("arbitrary", "arbitrary")),
    )(q, k, v)


def _sb_bwd(q, k, v, tot, do):
    t = q.shape[0]
    nq = t // ATT_BLK
    scale = SB_HEAD_DIM ** -0.5

    def body(q_ref, k_ref, v_ref, tot_ref, do_ref, dq_ref, dk_ref, dv_ref):
        qi = pl.program_id(1)

        @pl.when(qi == 0)
        def _():
            dk_ref[...] = jnp.zeros_like(dk_ref)
            dv_ref[...] = jnp.zeros_like(dv_ref)

        lane = lax.broadcasted_iota(jnp.int32, (ATT_BLK, LANES), 1)
        tri_incl = _tri(lambda r, c: r <= c)
        tri_lt = _tri(lambda r, c: r < c)
        qv = q_ref[...] * scale
        dov = do_ref[...]
        heads = [(lane // SB_HEAD_DIM) == hh for hh in range(2)]
        qms = [jnp.where(mine, qv, 0.0).astype(BF16) for mine in heads]
        doms = [jnp.where(mine, dov, 0.0).astype(BF16) for mine in heads]
        tots = [tot_ref[:, hh * LANES:hh * LANES + 1] for hh in range(2)]

        def blocks(kbs, carry, diagonal):
            dq = carry[0]
            nb = len(kbs)
            chains = [(b, hh) for b in range(nb) for hh in range(2)]
            offs = [pl.multiple_of(kb * ATT_BLK, ATT_BLK) for kb in kbs]
            k_blks = [k_ref[pl.ds(off, ATT_BLK), :] for off in offs]
            vvs = [v_ref[pl.ds(off, ATT_BLK), :].astype(BF16) for off in offs]
            if any(diagonal):
                valid = (lax.broadcasted_iota(jnp.int32, (ATT_BLK, ATT_BLK), 1)
                         < lax.broadcasted_iota(jnp.int32, (ATT_BLK, ATT_BLK), 0))
            kks = {(b, hh): jnp.where(heads[hh], k_blks[b], 0.0).astype(BF16) for b, hh in chains}
            zs = {ch: _nt(qms[ch[1]], kks[ch]) for ch in chains}
            dws = {ch: _nt(doms[ch[1]], vvs[ch[0]]) for ch in chains}
            logs = {ch: _sb_logs(zs[ch]) for ch in chains}
            lbs = {ch: logs[ch][0] for ch in chains}
            l1m_all = {ch: logs[ch][1] for ch in chains}
            l1ms = {ch: jnp.where(valid, l1m_all[ch], 0.0) if diagonal[ch[0]] else l1m_all[ch] for ch in chains}
            pre, c_de = {}, {}
            for hh in range(2):
                pre[(0, hh)], c_de[(0, hh)] = carry[1 + 2 * hh], carry[2 + 2 * hh]
            for b, hh in chains:
                pre[(b + 1, hh)] = pre[(b, hh)] + jnp.sum(l1ms[(b, hh)], axis=-1, keepdims=True)
            prefix = {ch: _hi_lo_dot(l1ms[ch], tri_incl) for ch in chains}
            ws = {ch: jnp.exp(lbs[ch] + (tots[ch[1]] - (prefix[ch] + pre[ch]))) for ch in chains}
            ws = {ch: jnp.where(valid, ws[ch], 0.0) if diagonal[ch[0]] else ws[ch] for ch in chains}
            d_es = {ch: ws[ch] * dws[ch] for ch in chains}
            for b, hh in chains:
                c_de[(b + 1, hh)] = c_de[(b, hh)] + jnp.sum(d_es[(b, hh)], axis=-1, keepdims=True)
            dvs = [_tn(ws[(b, 0)].astype(BF16), doms[0]) + _tn(ws[(b, 1)].astype(BF16), doms[1]) for b in range(nb)]
            dl1ms = {ch: jnp.dot(d_es[ch].astype(BF16), tri_lt, preferred_element_type=F32) + c_de[ch] for ch in chains}
            dzs = {ch: d_es[ch] * jnp.exp(l1m_all[ch]) - dl1ms[ch] * jnp.exp(lbs[ch]) for ch in chains}
            dzs = {ch: jnp.where(valid, dzs[ch], 0.0) if diagonal[ch[0]] else dzs[ch] for ch in chains}
            dzs = {ch: dzs[ch].astype(BF16) for ch in chains}
            for ch in chains:
                dq = dq + jnp.dot(dzs[ch], kks[ch], preferred_element_type=F32)
            for b in range(nb):
                dk_ref[pl.ds(offs[b], ATT_BLK), :] += _tn(dzs[(b, 0)], qms[0]) + _tn(dzs[(b, 1)], qms[1])
                dv_ref[pl.ds(offs[b], ATT_BLK), :] += dvs[b]
            return (dq, pre[(nb, 0)], c_de[(nb, 0)], pre[(nb, 1)], c_de[(nb, 1)])

        zero = jnp.zeros((ATT_BLK, 1), F32)
        carry = lax.fori_loop(0, qi // 2, lambda pr, cr: blocks([2 * pr, 2 * pr + 1], cr, (False, False)),
                              (jnp.zeros((ATT_BLK, LANES), F32), zero, zero, zero, zero))
        carry = lax.cond(qi % 2 == 1, lambda cr: blocks([qi - 1, qi], cr, (False, True)),
                         lambda cr: blocks([qi], cr, (True,)), carry)
        dq_ref[...] = carry[0] * scale

    return pl.pallas_call(
        body, name="sb_attn_bwd", grid=(SB_HEADS // 2, nq),
        in_specs=[pl.BlockSpec((ATT_BLK, LANES), lambda p, i: (i, p)),
                  pl.BlockSpec((t, LANES), lambda p, i: (0, p)),
                  pl.BlockSpec((t, LANES), lambda p, i: (0, p)),
                  pl.BlockSpec((ATT_BLK, 2 * LANES), lambda p, i: (i, p)),
                  pl.BlockSpec((ATT_BLK, LANES), lambda p, i: (i, p))],
        out_specs=[pl.BlockSpec((ATT_BLK, LANES), lambda p, i: (i, p)),
                   pl.BlockSpec((t, LANES), lambda p, i: (0, p)),
                   pl.BlockSpec((t, LANES), lambda p, i: (0, p))],
        out_shape=[jax.ShapeDtypeStruct((t, SB_WIDTH), F32)] * 3,
        compiler_params=pltpu.CompilerParams(dimension_semantics=("arbitrary", "arbitrary")),
    )(q, k, v, tot, do)


@jax.custom_vjp
def _sb_attention(q, k, v):
    return _sb_fwd(q, k, v)[0]


def _sb_attention_fwd(q, k, v):
    o, tot = _sb_fwd(q, k, v)
    return o, (q, k, v, tot)


def _sb_attention_bwd(res, do):
    return tuple(_sb_bwd(*res, do))


_sb_attention.defvjp(_sb_attention_fwd, _sb_attention_bwd)


def _mla_fwd(qn, qr, kn, kr, v):
    t = qn.shape[0]
    nq = t // ATT_BLK
    scale = MLA_QK ** -0.5

    def body(qn_ref, qr_ref, kn_ref, kr_ref, v_ref, o_ref, lse_ref):
        qi = pl.program_id(1)
        lanes = [slice(hh * LANES, (hh + 1) * LANES) for hh in range(2)]
        qnb = [qn_ref[:, sl].astype(BF16) for sl in lanes]
        qrb = [qr_ref[:, sl].astype(BF16) for sl in lanes]

        def blocks(kbs, carry, diagonal):
            nb = len(kbs)
            chains = [(b, hh) for b in range(nb) for hh in range(2)]
            offs = [pl.multiple_of(kb * ATT_BLK, ATT_BLK) for kb in kbs]
            krbs = [kr_ref[pl.ds(off, ATT_BLK), :].astype(BF16) for off in offs]
            accs, ms, ls = [carry[0], carry[3]], [carry[1], carry[4]], [carry[2], carry[5]]
            ss = {(b, hh): (_nt(qnb[hh], kn_ref[pl.ds(offs[b], ATT_BLK), lanes[hh]].astype(BF16))
                            + _nt(qrb[hh], krbs[b])) * scale for b, hh in chains}
            if any(diagonal):
                causal = (lax.broadcasted_iota(jnp.int32, (ATT_BLK, ATT_BLK), 1)
                          <= lax.broadcasted_iota(jnp.int32, (ATT_BLK, ATT_BLK), 0))
                ss = {ch: jnp.where(causal, ss[ch], -jnp.inf) if diagonal[ch[0]] else ss[ch] for ch in chains}
            m_new = list(ms)
            for b, hh in chains:
                m_new[hh] = jnp.maximum(m_new[hh], jnp.max(ss[(b, hh)], axis=-1, keepdims=True))
            ps = {(b, hh): jnp.exp(ss[(b, hh)] - m_new[hh]) for b, hh in chains}
            alphas = [jnp.exp(ms[hh] - m_new[hh]) for hh in range(2)]
            pvs = {(b, hh): jnp.dot(ps[(b, hh)].astype(BF16), v_ref[pl.ds(offs[b], ATT_BLK), lanes[hh]].astype(BF16),
                                    preferred_element_type=F32) for b, hh in chains}
            out = []
            for hh in range(2):
                acc, l = accs[hh] * alphas[hh], ls[hh] * alphas[hh]
                for b in range(nb):
                    acc, l = acc + pvs[(b, hh)], l + jnp.sum(ps[(b, hh)], axis=-1, keepdims=True)
                out += [acc, m_new[hh], l]
            return tuple(out)

        init = (jnp.zeros((ATT_BLK, LANES), F32), jnp.full((ATT_BLK, 1), -jnp.inf, F32), jnp.zeros((ATT_BLK, 1), F32))
        carry = lax.cond(qi % 2 == 1, lambda cr: blocks([qi, qi - 1], cr, (True, False)),
                         lambda cr: blocks([qi], cr, (True,)), init + init)
        carry = lax.fori_loop(0, qi // 2, lambda pr, cr: blocks([2 * pr, 2 * pr + 1], cr, (False, False)), carry)
        for hh in range(2):
            acc, m, l = carry[3 * hh:3 * hh + 3]
            o_ref[:, lanes[hh]] = acc / l
            lse_ref[:, lanes[hh]] = jnp.broadcast_to(m + jnp.log(l), (ATT_BLK, LANES))

    blk = pl.BlockSpec((ATT_BLK, 2 * LANES), lambda p, i: (i, p))
    full = pl.BlockSpec((t, 2 * LANES), lambda p, i: (0, p))
    return pl.pallas_call(
        body, name="mla_attn_fwd", grid=(MLA_HEADS // 2, nq),
        in_specs=[blk, blk, full, pl.BlockSpec((t, LANES), lambda p, i: (0, 0)), full],
        out_specs=[blk, blk],
        out_shape=[jax.ShapeDtypeStruct((t, MLA_HEADS * LANES), F32)] * 2,
        compiler_params=pltpu.CompilerParams(dimension_semantics=("arbitrary", "arbitrary")),
    )(qn, qr, kn, kr, v)


def _mla_bwd(qn, qr, kn, kr, v, o, lse, do):
    t = qn.shape[0]
    nq = t // ATT_BLK
    scale = MLA_QK ** -0.5

    def body(qn_ref, qr_ref, kn_ref, kr_ref, v_ref, o_ref, lse_ref, do_ref,
             dqn_ref, dqr_ref, dkn_ref, dkr_ref, dv_ref):
        pair = pl.program_id(0)
        qi = pl.program_id(1)

        @pl.when(qi == 0)
        def _():
            dkn_ref[...] = jnp.zeros_like(dkn_ref)
            dv_ref[...] = jnp.zeros_like(dv_ref)

        @pl.when((qi == 0) & (pair == 0))
        def _():
            dkr_ref[...] = jnp.zeros_like(dkr_ref)

        lanes = [slice(hh * LANES, (hh + 1) * LANES) for hh in range(2)]
        qnb = [qn_ref[:, sl].astype(BF16) for sl in lanes]
        qrb = [qr_ref[:, sl].astype(BF16) for sl in lanes]
        dob = [do_ref[:, sl].astype(BF16) for sl in lanes]
        delta = [jnp.sum(do_ref[:, sl] * o_ref[:, sl], axis=-1, keepdims=True) for sl in lanes]
        lse_v = [lse_ref[:, hh * LANES:hh * LANES + 1] for hh in range(2)]

        def blocks(kbs, carry, diagonal):
            nb = len(kbs)
            chains = [(b, hh) for b in range(nb) for hh in range(2)]
            offs = [pl.multiple_of(kb * ATT_BLK, ATT_BLK) for kb in kbs]
            krbs = [kr_ref[pl.ds(off, ATT_BLK), :].astype(BF16) for off in offs]
            knb = {(b, hh): kn_ref[pl.ds(offs[b], ATT_BLK), lanes[hh]].astype(BF16) for b, hh in chains}
            vb = {(b, hh): v_ref[pl.ds(offs[b], ATT_BLK), lanes[hh]].astype(BF16) for b, hh in chains}
            ss = {(b, hh): _nt(qnb[hh], knb[(b, hh)]) + _nt(qrb[hh], krbs[b]) for b, hh in chains}
            dps = {(b, hh): _nt(dob[hh], vb[(b, hh)]) for b, hh in chains}
            ps = {(b, hh): jnp.exp(ss[(b, hh)] * scale - lse_v[hh]) for b, hh in chains}
            if any(diagonal):
                causal = (lax.broadcasted_iota(jnp.int32, (ATT_BLK, ATT_BLK), 1)
                          <= lax.broadcasted_iota(jnp.int32, (ATT_BLK, ATT_BLK), 0))
                ps = {ch: jnp.where(causal, ps[ch], 0.0) if diagonal[ch[0]] else ps[ch] for ch in chains}
            dss = {(b, hh): (ps[(b, hh)] * (dps[(b, hh)] - delta[hh]) * scale).astype(BF16) for b, hh in chains}
            for b, hh in chains:
                dv_ref[pl.ds(offs[b], ATT_BLK), lanes[hh]] += _tn(ps[(b, hh)].astype(BF16), dob[hh])
            for b, hh in chains:
                dkn_ref[pl.ds(offs[b], ATT_BLK), lanes[hh]] += _tn(dss[(b, hh)], qnb[hh])
            for b in range(nb):
                dkr_ref[pl.ds(offs[b], ATT_BLK), :] += _tn(dss[(b, 0)], qrb[0]) + _tn(dss[(b, 1)], qrb[1])
            out = list(carry)
            for b, hh in chains:
                out[2 * hh] = out[2 * hh] + jnp.dot(dss[(b, hh)], knb[(b, hh)], preferred_element_type=F32)
                out[2 * hh + 1] = out[2 * hh + 1] + jnp.dot(dss[(b, hh)], krbs[b], preferred_element_type=F32)
            return tuple(out)

        zero = jnp.zeros((ATT_BLK, LANES), F32)
        carry = lax.fori_loop(0, qi // 2, lambda pr, cr: blocks([2 * pr, 2 * pr + 1], cr, (False, False)),
                              (zero, zero, zero, zero))
        carry = lax.cond(qi % 2 == 1, lambda cr: blocks([qi - 1, qi], cr, (False, True)),
                         lambda cr: blocks([qi], cr, (True,)), carry)
        for hh in range(2):
            dqn_ref[:, lanes[hh]] = carry[2 * hh]
            dqr_ref[:, lanes[hh]] = carry[2 * hh + 1]

    blk = pl.BlockSpec((ATT_BLK, 2 * LANES), lambda p, i: (i, p))
    full = pl.BlockSpec((t, 2 * LANES), lambda p, i: (0, p))
    shared = pl.BlockSpec((t, LANES), lambda p, i: (0, 0))
    wide = jax.ShapeDtypeStruct((t, MLA_HEADS * LANES), F32)
    return pl.pallas_call(
        body, name="mla_attn_bwd", grid=(MLA_HEADS // 2, nq),
        in_specs=[blk, blk, full, shared, full, blk, blk, blk],
        out_specs=[blk, blk, full, shared, full],
        out_shape=[wide, wide, wide, jax.ShapeDtypeStruct((t, LANES), F32), wide],
        compiler_params=pltpu.CompilerParams(dimension_semantics=("arbitrary", "arbitrary")),
    )(qn, qr, kn, kr, v, o, lse, do)


@jax.custom_vjp
def _mla_attention(qn, qr, kn, kr, v):
    return _mla_fwd(qn, qr, kn, kr, v)[0]


def _mla_attention_fwd(qn, qr, kn, kr, v):
    o, lse = _mla_fwd(qn, qr, kn, kr, v)
    return o, (qn, qr, kn, kr, v, o, lse)


def _mla_attention_bwd(res, do):
    return tuple(_mla_bwd(*res, do))


_mla_attention.defvjp(_mla_attention_fwd, _mla_attention_bwd)


def _ffn_in(h, wg, wu):
    t, k = h.shape
    n_sh, _, cc = wg.shape

    def body(h_ref, wg_ref, wu_ref, g_ref, u_ref, a_ref):
        hb = h_ref[...].astype(BF16)
        for j in range(n_sh):
            cols = slice(j * cc, (j + 1) * cc)
            g = jnp.dot(hb, wg_ref[j], preferred_element_type=F32)
            u = jnp.dot(hb, wu_ref[j], preferred_element_type=F32)
            g_ref[:, cols] = g
            u_ref[:, cols] = u
            a_ref[:, cols] = _f_swiglu(g, u)[0].astype(BF16)

    w_spec = pl.BlockSpec((n_sh, k, cc), lambda i: (0, 0, 0))
    o_spec = pl.BlockSpec((ROW_TILE, n_sh * cc), lambda i: (i, 0))
    wide = (t, n_sh * cc)
    return pl.pallas_call(
        body, name="ffn_in_fwd", grid=(t // ROW_TILE,),
        in_specs=[pl.BlockSpec((ROW_TILE, k), lambda i: (i, 0)), w_spec, w_spec],
        out_specs=[o_spec, o_spec, o_spec],
        out_shape=[jax.ShapeDtypeStruct(wide, F32), jax.ShapeDtypeStruct(wide, F32), jax.ShapeDtypeStruct(wide, BF16)],
        compiler_params=pltpu.CompilerParams(dimension_semantics=("arbitrary",), vmem_limit_bytes=MM_VMEM_LIMIT),
    )(h, wg, wu)


def _ffn_mid_bwd(dy, wd, g, u):
    t, n = dy.shape
    n_sh, cc, _ = wd.shape

    def body(dy_ref, wd_ref, g_ref, u_ref, dg_ref, du_ref):
        d_act = _nt(dy_ref[...].astype(BF16), wd_ref[...])
        _, vjp = jax.vjp(_f_swiglu, g_ref[...], u_ref[...])
        dg, du = vjp((d_act,))
        dg_ref[...] = dg.astype(BF16)
        du_ref[...] = du.astype(BF16)

    blk = pl.BlockSpec((MM_ROW_TILE, cc), lambda j, i: (i, j))
    wide = jax.ShapeDtypeStruct((t, n_sh * cc), BF16)
    return pl.pallas_call(
        body, name="ffn_mid_bwd", grid=(n_sh, t // MM_ROW_TILE),
        in_specs=[pl.BlockSpec((MM_ROW_TILE, n), lambda j, i: (i, 0)),
                  pl.BlockSpec((None, cc, n), lambda j, i: (j, 0, 0)), blk, blk],
        out_specs=[blk, blk], out_shape=[wide, wide],
        compiler_params=pltpu.CompilerParams(dimension_semantics=("arbitrary", "arbitrary"),
                                             vmem_limit_bytes=MM_VMEM_LIMIT),
    )(dy, wd, g, u)


def _ffn_dh(dg, du, wg, wu):
    t = dg.shape[0]
    n_sh, k, cc = wg.shape

    def body(dg_ref, du_ref, wg_ref, wu_ref, o_ref):
        acc = jnp.zeros((MM_ROW_TILE, k), F32)
        for j in range(n_sh):
            cols = slice(j * cc, (j + 1) * cc)
            acc = acc + _nt(dg_ref[:, cols], wg_ref[j]) + _nt(du_ref[:, cols], wu_ref[j])
        o_ref[...] = acc

    blk = pl.BlockSpec((MM_ROW_TILE, n_sh * cc), lambda i: (i, 0))
    w_spec = pl.BlockSpec((n_sh, k, cc), lambda i: (0, 0, 0))
    return pl.pallas_call(
        body, name="ffn_dh", grid=(t // MM_ROW_TILE,),
        in_specs=[blk, blk, w_spec, w_spec],
        out_specs=pl.BlockSpec((MM_ROW_TILE, k), lambda i: (i, 0)),
        out_shape=jax.ShapeDtypeStruct((t, k), F32),
        compiler_params=pltpu.CompilerParams(dimension_semantics=("arbitrary",), vmem_limit_bytes=MM_VMEM_LIMIT),
    )(dg, du, wg, wu)


def _ffn_dw_in(h, dy, n_sh, name):
    t, k = h.shape
    cc = dy.shape[1] // n_sh
    tk = 512

    def body(h_ref, dy_ref, o_ref):
        o_ref[...] = _tn(h_ref[...].astype(BF16), dy_ref[...]).astype(BF16)

    return pl.pallas_call(
        body, name=name, grid=(n_sh, k // tk),
        in_specs=[pl.BlockSpec((t, tk), lambda j, i: (0, i)), pl.BlockSpec((t, cc), lambda j, i: (0, j))],
        out_specs=pl.BlockSpec((None, tk, cc), lambda j, i: (j, i, 0)),
        out_shape=jax.ShapeDtypeStruct((n_sh, k, cc), BF16),
        compiler_params=pltpu.CompilerParams(dimension_semantics=("arbitrary", "arbitrary"),
                                             vmem_limit_bytes=MM_VMEM_LIMIT),
    )(h, dy)


@jax.custom_vjp
def _ffn_block(h, wg, wu, wd):
    act = _ffn_in(h, wg, wu)[2]
    return _mm(act, wd.reshape(-1, wd.shape[2]), "nn", "ffn_down_fwd", MM_ROW_TILE, wd.shape[2])


def _ffn_block_fwd(h, wg, wu, wd):
    g, u, act = _ffn_in(h, wg, wu)
    y = _mm(act, wd.reshape(-1, wd.shape[2]), "nn", "ffn_down_fwd", MM_ROW_TILE, wd.shape[2])
    return y, (h, wg, wu, wd, g, u, act)


def _ffn_block_bwd(res, dy):
    h, wg, wu, wd, g, u, act = res
    dg, du = _ffn_mid_bwd(dy, wd, g, u)
    dh = _ffn_dh(dg, du, wg, wu)
    n_sh = wg.shape[0]
    dwg = _ffn_dw_in(h, dg, n_sh, "ffn_gate_dw")
    dwu = _ffn_dw_in(h, du, n_sh, "ffn_up_dw")
    dwd = _mm(act, dy, "tn", "ffn_down_dw", 256, wd.shape[2], out_dtype=BF16).reshape(wd.shape)
    return dh, dwg, dwu, dwd


_ffn_block.defvjp(_ffn_block_fwd, _ffn_block_bwd)


def _swap_halves(w):
    half = w.shape[-1] // 2
    return jnp.concatenate([w[..., half:], w[..., :half]], axis=-1)


def _pad_lanes(w):
    return jnp.concatenate([w, jnp.zeros(w.shape[:-1] + (LANES - w.shape[-1],), w.dtype)], axis=-1)


def _join_cols(shards):
    return shards.transpose(1, 0, 2).reshape(shards.shape[1], -1)


def _mod_parts(mod):
    return [mod[:, i * D_MODEL:(i + 1) * D_MODEL] for i in range(N_MOD)]


def _local_loss(x, mod, p, cos, sin, target):
    return _ffn_stage(x, _mixing_stage(x, mod, p, cos, sin), mod, p, target)


def _mixing_stage(x, mod, p, cos, sin):
    shift1, scale1 = _mod_parts(mod)[:2]

    w_in = _join_cols(p["w_in"])
    k_rope_w = w_in[:, 2176:2240]
    w_in_ext = jnp.concatenate([w_in[:, :2176], _pad_lanes(k_rope_w), _pad_lanes(_swap_halves(k_rope_w)),
                                jnp.zeros((D_MODEL, LANES), w_in.dtype)], axis=1)
    (h1,) = _make_rowwise("pre_attn", _f_pre_attn, 1, 3, [D_MODEL], [True], out_dtypes=[BF16])(
        x, p["norm_attn"], scale1, shift1)
    q_sb, k_sb, v_sb, cq, ckv, kr, kr_sw = _make_linear_split(
        "in_proj", (SB_WIDTH, SB_WIDTH, SB_WIDTH, MLA_Q_RANK, MLA_KV_RANK, LANES, LANES), 512)(h1, w_in_ext)

    o_sb = _sb_attention(q_sb, k_sb, v_sb)

    wq = _join_cols(p["w_q_up"]).reshape(MLA_Q_RANK, MLA_HEADS, MLA_QK)
    wq_n, wq_r = wq[:, :, :MLA_NOPE], wq[:, :, MLA_NOPE:]
    w_q_ext = jnp.concatenate([wq_n.reshape(MLA_Q_RANK, -1), _pad_lanes(wq_r).reshape(MLA_Q_RANK, -1),
                               _pad_lanes(_swap_halves(wq_r)).reshape(MLA_Q_RANK, -1)], axis=1)
    wkv = _join_cols(p["w_kv_up"]).reshape(MLA_KV_RANK, MLA_HEADS, MLA_NOPE + MLA_V)
    w_kv_ext = jnp.concatenate([wkv[:, :, :MLA_NOPE].reshape(MLA_KV_RANK, -1),
                                wkv[:, :, MLA_NOPE:].reshape(MLA_KV_RANK, -1)], axis=1)
    cqn, ckvn = _make_rowwise("mla_a", _f_mla_a, 2, 2, [MLA_Q_RANK, MLA_KV_RANK], [True, True],
                              out_dtypes=[BF16, BF16], grad_dtypes=[BF16, BF16])(
        cq, ckv, p["q_a_norm"], p["kv_a_norm"])
    qall = _make_linear("q_up", 384, 768)(cqn, w_q_ext)
    kn_all, v_mla = _make_linear_split("kv_up", (MLA_HEADS * MLA_NOPE, MLA_HEADS * MLA_V), MLA_KV_RANK)(ckvn, w_kv_ext)
    gq = p["q_norm"]
    gkr = p["k_rope_norm"]
    qn, qr, kn, krr = _make_rowwise("mla_b", _f_mla_b, 6, 6, [512, 512, 512, LANES],
                                    [True, True, True, True, False, False],
                                    out_dtypes=[BF16] * 4, grad_dtypes=[BF16] * 4)(
        qall, kn_all, kr, kr_sw, cos, sin,
        gq[:, :MLA_NOPE], _pad_lanes(gq[:, MLA_NOPE:]), _pad_lanes(_swap_halves(gq[:, MLA_NOPE:])),
        p["k_nope_norm"], _pad_lanes(gkr), _pad_lanes(_swap_halves(gkr)))
    o_mla = _mla_attention(qn, qr, kn, krr, v_mla)

    (mixed,) = _make_rowwise("post_attn", _f_post_attn, 2, 2, [D_MODEL], [True, True])(
        o_sb, o_mla, p["out_norm_sb"], p["out_norm_mla"])
    return mixed


def _ffn_stage(x, mixed, mod, p, target):
    _, _, gate1, shift2, scale2, gate2 = _mod_parts(mod)
    attn = _make_linear("out_proj", 512, 512)(mixed, p["w_out"].reshape(D_MODEL, D_MODEL))

    x2, h2 = _make_rowwise("pre_ffn", _f_pre_ffn, 2, 4, [D_MODEL, D_MODEL], [True, True],
                           out_dtypes=[F32, BF16], grad_dtypes=[F32, BF16])(
        x, attn, gate1, p["norm_ffn"], scale2, shift2)
    ffn = _ffn_block(h2, p["w_gate"], p["w_up"], p["w_down"])
    (row_loss,) = _make_rowwise("loss", _f_loss, 3, 1, [1], [True, True, False], grad_dtypes=[F32, BF16])(
        x2, ffn, target, gate2)
    return 0.5 * jnp.sum(row_loss)


def _my_place():
    return lax.axis_index("x"), lax.axis_index("y"), lax.axis_index("c")


def _all_gather_small(block, name):
    m_per, n = block.shape

    def body(x_ref, out_ref, send_sems, recv_sems, local_sem):
        x, y, c = _my_place()
        me, sibling = (x, y, c), (x, y, 1 - c)
        chips = [(1 - x, y), (x, 1 - y), (1 - x, 1 - y)]

        def rows(px, py, pc):
            return out_ref.at[pl.ds((4 * px + 2 * py + pc) * m_per, m_per), :]

        def copy(k, blk, to, src=None):
            return pltpu.make_async_remote_copy(
                src_ref=rows(*blk) if src is None else src, dst_ref=rows(*blk),
                send_sem=send_sems.at[k], recv_sem=recv_sems.at[k], device_id=to, device_id_type=MESH)

        mine = pltpu.make_async_copy(x_ref, rows(*me), local_sem)
        mine.start()
        first = [copy(0, me, sibling, src=x_ref)]
        first += [copy(1 + j, me, (*chip, c), src=x_ref) for j, chip in enumerate(chips)]
        for cp in first:
            cp.start()
        passed = [copy(4 + j, (*chip, c), sibling) for j, chip in enumerate(chips)]
        for j, chip in enumerate(chips):
            copy(1 + j, (*chip, c), me).wait_recv()
            passed[j].start()
        copy(0, sibling, me).wait_recv()
        for j, chip in enumerate(chips):
            copy(4 + j, (*chip, 1 - c), me).wait_recv()
        for cp in first + passed:
            cp.wait_send()
        mine.wait()

    return pl.pallas_call(
        body, name=name,
        out_shape=jax.ShapeDtypeStruct((N_DEV * m_per, n), block.dtype),
        in_specs=[pl.BlockSpec(memory_space=pltpu.VMEM)],
        out_specs=pl.BlockSpec(memory_space=pltpu.VMEM),
        scratch_shapes=[pltpu.SemaphoreType.DMA((7,)), pltpu.SemaphoreType.DMA((7,)), pltpu.SemaphoreType.DMA],
    )(block)


EARLY = ("w_in", "w_q_up", "w_kv_up")
LATE = ("w_out", "w_gate", "w_up", "w_down")
BIG = EARLY + LATE
TRANSPOSED_UPDATE = ("w_in", "w_gate", "w_up")
HALF_AXIS = {"w_in": 0, "w_q_up": 0, "w_kv_up": 0, "w_out": 0, "w_gate": 0, "w_up": 0, "w_down": 1}


def _half(ref, h, axis, lead=()):
    trail = ref.shape[len(lead):]
    idx = list(lead) + [slice(None)] * len(trail)
    at = len(trail) - 2 + axis
    n2 = trail[at] // 2
    idx[len(lead) + at] = pl.ds(h * n2, n2)
    return ref.at[tuple(idx)]


def _half_shape(shape, axis):
    shape = list(shape)
    shape[len(shape) - 2 + axis] //= 2
    return tuple(shape)


def _remote(src, dst, send_sems, recv_sems, k, to):
    return pltpu.make_async_remote_copy(src_ref=src, dst_ref=dst, send_sem=send_sems.at[k],
                                        recv_sem=recv_sems.at[k], device_id=to, device_id_type=MESH)


def _gather_weights(names, shards, after):
    n_w = len(shards)
    axes = [HALF_AXIS[n] for n in names]

    def body(*refs):
        w_refs, out_refs, token = refs[:n_w], refs[n_w + 1:2 * n_w + 1], refs[2 * n_w + 1]
        send_sems, recv_sems, local_sems = refs[2 * n_w + 2:]
        token[...] = jnp.zeros_like(token)
        x, y, c = _my_place()
        sibling = (x, y, 1 - c)
        chips = [(1 - x, y), (x, 1 - y), (1 - x, 1 - y)]
        me = 2 * x + y
        mine =[pltpu.make_async_copy(w, o.at[me], local_sems.at[i]) for i, (w, o) in enumerate(zip(w_refs, out_refs))]
        for cp in mine:
            cp.start()
        first = [_remote(_half(w_refs[i], c, axes[i]), _half(out_refs[i], c, axes[i], (me,)),
                         send_sems, recv_sems, 6 * i + j, (*chip, c))
                 for i in range(n_w) for j, chip in enumerate(chips)]
        for cp in first:
            cp.start()
        passed = []
        for j, (cx, cy) in enumerate(chips):
            for i in range(n_w):
                blk = _half(out_refs[i], c, axes[i], (2 * cx + cy,))
                _remote(blk, blk, send_sems, recv_sems, 6 * i + j, (cx, cy, c)).wait_recv()
                cp = _remote(blk, blk, send_sems, recv_sems, 6 * i + 3 + j, sibling)
                cp.start()
                passed.append(cp)
        for j, (cx, cy) in enumerate(chips):
            for i in range(n_w):
                blk = _half(out_refs[i], 1 - c, axes[i], (2 * cx + cy,))
                _remote(blk, blk, send_sems, recv_sems, 6 * i + 3 + j, sibling).wait_recv()
        for cp in first + passed:
            cp.wait_send()
        for cp in mine:
            cp.wait()

    outs = pl.pallas_call(
        body, name="gather_weights",
        out_shape=[jax.ShapeDtypeStruct((N_CHIPS,) + s.shape, s.dtype) for s in shards]
        + [jax.ShapeDtypeStruct((8, LANES), F32)],
        in_specs=[ANY] * (n_w + 1), out_specs=[ANY] * n_w + [pl.BlockSpec(memory_space=pltpu.VMEM)],
        scratch_shapes=[pltpu.SemaphoreType.DMA((6 * n_w,)), pltpu.SemaphoreType.DMA((6 * n_w,)),
                        pltpu.SemaphoreType.DMA((n_w,))],
    )(*shards, after)
    return outs[:n_w], outs[n_w]


def _sibling_join(halves, name, after):
    n_w = len(halves)

    def body(*refs):
        s_refs, j_refs = refs[:n_w], refs[n_w + 1:2 * n_w + 1]
        send_sems, recv_sems = refs[2 * n_w + 1:]
        x, y, c = _my_place()
        sends = [_remote(s_refs[i], j_refs[i], send_sems, recv_sems, i, (x, y, 1 - c)) for i in range(n_w)]
        for cp in sends:
            cp.start()
        for cp in sends:
            cp.wait_recv()
        for cp in sends:
            cp.wait_send()

    return pl.pallas_call(
        body, name=name,
        out_shape=[jax.ShapeDtypeStruct(s.shape, s.dtype) for s in halves],
        in_specs=[ANY] * (n_w + 1), out_specs=[ANY] * n_w,
        scratch_shapes=[pltpu.SemaphoreType.DMA((n_w,)), pltpu.SemaphoreType.DMA((n_w,))],
    )(*halves, after)


HBM_SPEC = pl.BlockSpec(memory_space=pltpu.HBM)
SEM_SPEC = pl.BlockSpec(memory_space=pltpu.SEMAPHORE)
DATAFLOW = pltpu.SideEffectType.DATAFLOW_SIDE_EFFECTING


def _in_hbm(a):
    return pltpu.with_memory_space_constraint(a, pltpu.HBM)


def _exchange_start(name, srcs, lands, plan, n_copies, after, thru):
    n = len(srcs)

    def body(*refs):
        src_refs, land_refs = refs[:n], refs[n:2 * n]
        send_sems, recv_sems = refs[2 * n + 2], refs[2 * n + 3]
        for k, (src, dst, to, k_recv) in enumerate(plan(src_refs, land_refs)):
            pltpu.make_async_remote_copy(src_ref=src, dst_ref=dst, send_sem=send_sems.at[k],
                                         recv_sem=recv_sems.at[k_recv], device_id=to, device_id_type=MESH).start()

    outs = pl.pallas_call(
        body, name=name,
        out_shape=(pltpu.SemaphoreType.DMA((n_copies,)), pltpu.SemaphoreType.DMA((n_copies,)),
                   *[pltpu.HBM(a.shape, a.dtype) for a in list(srcs) + list(lands) + [thru]]),
        in_specs=[HBM_SPEC] * (2 * n + 1) + [ANY],
        out_specs=(SEM_SPEC, SEM_SPEC, *[HBM_SPEC] * (2 * n + 1)),
        input_output_aliases={i: 2 + i for i in range(2 * n + 1)},
        compiler_params=pltpu.CompilerParams(has_side_effects=DATAFLOW),
    )(*[_in_hbm(a) for a in list(srcs) + list(lands) + [thru]], after)
    return outs[0], outs[1], outs[2:2 + n], outs[2 + n:2 + 2 * n], outs[2 + 2 * n]


def _exchange_wait(name, started, plan, after):
    send_sems, recv_sems, srcs, lands, _ = started
    n = len(srcs)

    def body(*refs):
        src_refs, land_refs = refs[:n], refs[n:2 * n]
        s_sems, r_sems = refs[2 * n], refs[2 * n + 1]
        for k, (src, dst, to, _) in enumerate(plan(src_refs, land_refs)):
            cp = _remote(src, dst, s_sems, r_sems, k, to)
            cp.wait_send()
            cp.wait_recv()

    outs = pl.pallas_call(
        body, name=name,
        out_shape=tuple(pltpu.HBM(a.shape, a.dtype) for a in list(srcs) + list(lands)),
        in_specs=[HBM_SPEC] * (2 * n) + [SEM_SPEC, SEM_SPEC, ANY],
        out_specs=tuple([HBM_SPEC] * (2 * n)),
        input_output_aliases={i: i for i in range(2 * n)},
        compiler_params=pltpu.CompilerParams(has_side_effects=DATAFLOW),
    )(*srcs, *lands, send_sems, recv_sems, after)
    return outs[:n], outs[n:]


def _late_gather_plan(src_refs, land_refs):
    x, y, c = _my_place()
    chips = [(1 - x, y), (x, 1 - y), (1 - x, 1 - y)]
    plan = [(src, land.at[2 * x + y], (cx, cy, c)) for src, land in zip(src_refs, land_refs) for cx, cy in chips]
    return [entry + (k,) for k, entry in enumerate(plan)]


def _direct_scatter_plan(names):
    axes = [HALF_AXIS[n] for n in names]

    def plan(src_refs, land_refs):
        x, y, c = _my_place()
        chips = [(1 - x, y), (x, 1 - y), (1 - x, 1 - y)]
        out = []
        for i, (src, land) in enumerate(zip(src_refs, land_refs)):
            for f, (cx, cy) in enumerate(chips):
                for core in range(2):
                    out.append((_half(src, core, axes[i], (2 * cx + cy,)), land.at[2 * f + c], (cx, cy, core),
                                7 * i + 2 * f + c))
            out.append((_half(src, 1 - c, axes[i], (2 * x + y,)), land.at[6], (x, y, 1 - c), 7 * i + 6))
        return out

    return plan


def _row_tile(rows, mult=16, limit=ROW_TILE):
    return max(d for d in range(mult, limit + 1, mult) if rows % d == 0)


def _chip_sum_direct(place, g, parts, axis, name, transposed):
    n_parts, rr, cc = parts.shape
    tr = _row_tile(rr, LANES) if transposed else _row_tile(rr, limit=1024)
    nb = rr // tr
    if axis == 0:
        g_map = lambda i, pr: (pr[1], pr[0] * nb + i, 0)
    else:
        g_map = lambda i, pr: (pr[1], i, pr[0])

    def body(pr, g_ref, p_ref, o_ref):
        acc = p_ref[0].astype(F32)
        for j in range(1, n_parts):
            acc = acc + p_ref[j].astype(F32)
        acc = acc + g_ref[...].astype(F32)
        o_ref[...] = (acc.T if transposed else acc).astype(BF16)

    out_spec = pl.BlockSpec((cc, tr), lambda i, pr: (0, i)) if transposed else pl.BlockSpec((tr, cc), lambda i, pr: (i, 0))
    return pl.pallas_call(
        body, name=name,
        grid_spec=pltpu.PrefetchScalarGridSpec(
            num_scalar_prefetch=1, grid=(nb,),
            in_specs=[pl.BlockSpec((None, tr, cc), g_map), pl.BlockSpec((n_parts, tr, cc), lambda i, pr: (0, i, 0))],
            out_specs=out_spec),
        out_shape=jax.ShapeDtypeStruct((cc, rr) if transposed else (rr, cc), BF16))(place, g, parts)


def _silu(v):
    return v / (1.0 + jnp.exp(-v))


def _ada_fwd(c_all, w_shard, b_shard):
    def body(c_ref, w_ref, b_ref, o_ref):
        o_ref[...] = jnp.dot(_silu(c_ref[...]), w_ref[...], precision=lax.Precision.HIGHEST,
                             preferred_element_type=F32) + b_ref[...]

    return pl.pallas_call(body, name="ada_fwd", out_shape=jax.ShapeDtypeStruct((c_all.shape[0], w_shard.shape[1]), F32),
                          compiler_params=pltpu.CompilerParams(vmem_limit_bytes=MM_VMEM_LIMIT))(c_all, w_shard, b_shard)


def _ada_bwd(c_all, dmod_cols):
    def body(c_ref, d_ref, o_ref):
        o_ref[...] = lax.dot_general(_silu(c_ref[...]), d_ref[...], (((0,), (0,)), ((), ())),
                                     precision=lax.Precision.HIGHEST, preferred_element_type=F32)

    return pl.pallas_call(body, name="ada_bwd", out_shape=jax.ShapeDtypeStruct((c_all.shape[1], dmod_cols.shape[1]), F32),
                          compiler_params=pltpu.CompilerParams(vmem_limit_bytes=MM_VMEM_LIMIT))(c_all, dmod_cols)


def _adamw_math(w, g, m, v):
    m = ADAM_B1 * m + (1.0 - ADAM_B1) * g
    v = ADAM_B2 * v + (1.0 - ADAM_B2) * (g * g)
    m_hat = m / (1.0 - ADAM_B1 ** ADAM_STEP)
    v_hat = v / (1.0 - ADAM_B2 ** ADAM_STEP)
    delta = -ADAM_LR * (m_hat / (jnp.sqrt(v_hat) + ADAM_EPS) + ADAM_WD * w)
    return delta, m, v


def _adamw(w, g, m, v, name):
    r, ccols = w.shape
    tr = max(d for d in range(8, ROW_TILE + 1, 8) if r % d == 0)
    spec = pl.BlockSpec((tr, ccols), lambda i: (i, 0))

    def body(w_ref, g_ref, m_ref, v_ref, d_ref, nm_ref, nv_ref):
        d_ref[...], nm_ref[...], nv_ref[...] = _adamw_math(w_ref[...], g_ref[...], m_ref[...], v_ref[...])

    return pl.pallas_call(body, name=name, grid=(r // tr,), in_specs=[spec] * 4, out_specs=[spec] * 3,
                          out_shape=[jax.ShapeDtypeStruct(w.shape, F32)] * 3,
                          compiler_params=pltpu.CompilerParams(vmem_limit_bytes=MM_VMEM_LIMIT))(w, g, m, v)


def _small_layout(sizes):
    offs, off = [], 0
    for n in sizes:
        offs.append(off)
        off += -(-n // LANES) * LANES
    total = -(-(off + LANES) // (8 * LANES)) * (8 * LANES)
    return offs, off, total


def _adamw_small(ws, g_all, ms, vs, offs, loss_off):
    n_p = len(ws)

    def device_sum(g_ref, off, width):
        blk = g_ref[:, off:off + width]
        acc = blk[0:1]
        for d in range(1, N_DEV):
            acc = acc + blk[d:d + 1]
        return acc

    def body(*refs):
        w_refs, m_refs, v_refs = refs[:n_p], refs[n_p:2 * n_p], refs[2 * n_p:3 * n_p]
        g_ref = refs[3 * n_p]
        outs = refs[3 * n_p + 1:]
        for i in range(n_p):
            n = w_refs[i].shape[1]
            g = device_sum(g_ref, offs[i], -(-n // LANES) * LANES)[:, :n]
            outs[i][...] = g
            outs[n_p + i][...], outs[2 * n_p + i][...], outs[3 * n_p + i][...] = _adamw_math(
                w_refs[i][...], g, m_refs[i][...], v_refs[i][...])
        outs[4 * n_p][...] = device_sum(g_ref, loss_off, LANES)

    res = pl.pallas_call(
        body, name="adamw_small",
        out_shape=[jax.ShapeDtypeStruct(a.shape, F32) for a in list(ws) * 4] + [jax.ShapeDtypeStruct((1, LANES), F32)],
    )(*ws, *ms, *vs, g_all)
    return res[:n_p], res[n_p:2 * n_p], res[2 * n_p:3 * n_p], res[3 * n_p:4 * n_p], res[4 * n_p]


def _adamw_halves(place, w, own, sib, m, v, axis, name, after):
    r, cc = w.shape
    if axis == 0:
        rows, gc = own.shape[0], own.shape[1]
        tr = _row_tile(rows)
        nb = rows // tr
        w_spec = pl.BlockSpec((tr, cc), lambda h, i, pr: (h * nb + i, 0))
        g_spec = pl.BlockSpec((tr, gc), lambda h, i, pr: (i, 0))
    else:
        tr = _row_tile(r)
        nb = r // tr
        gc = own.shape[1]
        w_spec = pl.BlockSpec((tr, gc), lambda h, i, pr: (i, h))
        g_spec = pl.BlockSpec((tr, gc), lambda h, i, pr: (i, 0))
    wc = w_spec.block_shape[1]

    def body(pr, w_ref, o_ref, s_ref, m_ref, v_ref, after_ref, g_ref, d_ref, nm_ref, nv_ref):
        g = jnp.where(pl.program_id(0) == pr[0], o_ref[...], s_ref[...]).astype(F32)[:, :wc]
        g_ref[...] = g
        d_ref[...], nm_ref[...], nv_ref[...] = _adamw_math(w_ref[...], g, m_ref[...], v_ref[...])

    return pl.pallas_call(
        body, name=name,
        grid_spec=pltpu.PrefetchScalarGridSpec(
            num_scalar_prefetch=1, grid=(2, nb),
            in_specs=[w_spec, g_spec, g_spec, w_spec, w_spec, ANY], out_specs=[w_spec] * 4),
        out_shape=[jax.ShapeDtypeStruct(w.shape, F32)] * 4,
        compiler_params=pltpu.CompilerParams(vmem_limit_bytes=MM_VMEM_LIMIT))(place, w, own, sib, m, v, after)


SMALL = ("b_ada", "norm_attn", "norm_ffn", "q_a_norm", "kv_a_norm", "q_norm", "k_nope_norm", "k_rope_norm",
         "out_norm_sb", "out_norm_mla")
WEIGHTS = ("w_ada", "b_ada", "norm_attn", "norm_ffn", "w_in", "q_a_norm", "w_q_up", "kv_a_norm", "w_kv_up",
           "q_norm", "k_nope_norm", "k_rope_norm", "out_norm_sb", "out_norm_mla", "w_out", "w_gate", "w_up",
           "w_down")


def kernel(x, c, positions, w_ada, b_ada, norm_attn, norm_ffn, w_in, q_a_norm, w_q_up, kv_a_norm, w_kv_up, q_norm, k_nope_norm, k_rope_norm, out_norm_sb, out_norm_mla, w_out, w_gate, w_up, w_down, loss_target, m_w_ada, m_b_ada, m_norm_attn, m_norm_ffn, m_w_in, m_q_a_norm, m_w_q_up, m_kv_a_norm, m_w_kv_up, m_q_norm, m_k_nope_norm, m_k_rope_norm, m_out_norm_sb, m_out_norm_mla, m_w_out, m_w_gate, m_w_up, m_w_down, v_w_ada, v_b_ada, v_norm_attn, v_norm_ffn, v_w_in, v_q_a_norm, v_w_q_up, v_kv_a_norm, v_w_kv_up, v_q_norm, v_k_nope_norm, v_k_rope_norm, v_out_norm_sb, v_out_norm_mla, v_w_out, v_w_gate, v_w_up, v_w_down):
    local = dict(locals())
    w = {n: local[n][0] for n in WEIGHTS}
    m = {n: local["m_" + n][0] for n in WEIGHTS}
    v = {n: local["v_" + n][0] for n in WEIGHTS}
    small = {n: w[n].reshape(1, -1) for n in SMALL}
    ix, iy, ic = _my_place()
    chip = 2 * ix + iy
    dev = 2 * chip + ic
    xs, target = x[0], loss_target[0]
    seq = xs.shape[0]

    c_all = _all_gather_small(c.reshape(8, LANES), "gather_c").reshape(N_DEV, D_MODEL)
    ada_cols = w["w_ada"].shape[1]
    b_cols = lax.dynamic_slice_in_dim(small["b_ada"], chip * ada_cols, ada_cols, axis=1)
    mod_cols = _ada_fwd(c_all, w["w_ada"], b_cols)
    mod_all = _all_gather_small(mod_cols, "gather_mod").reshape(N_CHIPS, 2, N_DEV, ada_cols)
    mod = lax.dynamic_index_in_dim(mod_all[:, 0], dev, axis=1, keepdims=False).reshape(1, N_MOD * D_MODEL)

    ff_pad = FF_SHARD_PAD - FF_SHARD
    pads = {"w_gate": ((0, 0), (0, ff_pad)), "w_up": ((0, 0), (0, ff_pad)), "w_down": ((0, ff_pad), (0, 0))}
    shards = {n: jnp.pad(w[n].astype(BF16), pads[n]) if n in pads else w[n].astype(BF16) for n in BIG}
    early, early_done = _gather_weights(EARLY, [shards[n] for n in EARLY], mod)
    gathered = dict(zip(EARLY, early))
    lands = [lax.dynamic_update_index_in_dim(lax.empty((N_CHIPS,) + shards[n].shape, BF16), shards[n], chip, 0)
             for n in LATE]
    late_gather = _exchange_start("gather_late_start", [shards[n] for n in LATE], lands, _late_gather_plan,
                                  3 * len(LATE), early_done, mod)
    mod = late_gather[4]

    half = MLA_ROPE // 2
    freqs = 1.0 / (ROPE_THETA ** (np.arange(half, dtype=np.float32) / half))
    zeros = np.zeros(LANES - MLA_ROPE, np.float32)
    freqs_row = jnp.asarray(np.concatenate([freqs, freqs, zeros]).astype(np.float32)[None])
    sign_row = jnp.asarray(np.concatenate([-np.ones(half), np.ones(half), zeros]).astype(np.float32)[None])
    cos, sin = _rope_tables(positions.reshape(seq, 1), freqs_row, sign_row)

    place = jnp.stack([ic, chip]).astype(jnp.int32)
    small_params = {n: small[n] for n in SMALL if n != "b_ada"}

    p1 = {**{n: gathered[n] for n in EARLY}, **small_params}
    mixed, mixing_vjp = jax.vjp(lambda x_, mod_, p_: _mixing_stage(x_, mod_, p_, cos, sin), xs, mod, p1)
    _, landed = _exchange_wait("gather_late_wait", late_gather, _late_gather_plan, mixed)
    p2 = {**dict(zip(LATE, landed)), **small_params}
    loss_part, ffn_vjp = jax.vjp(lambda x_, mixed_, mod_, p_: _ffn_stage(x_, mixed_, mod_, p_, target), xs, mixed, mod, p2)
    gx2, gmixed, gmod2, gp2 = ffn_vjp(jnp.ones((), F32))
    late_grads = [gp2[n] for n in LATE]
    late_plan = _direct_scatter_plan(LATE)
    late_scatter = _exchange_start(
        "grad_scatter_late_start", late_grads,
        [lax.empty((7,) + _half_shape(gr.shape[1:], HALF_AXIS[n]), BF16) for n, gr in zip(LATE, late_grads)],
        late_plan, 7 * len(LATE), gx2, gmixed)
    gx1, gmod1, gp1 = mixing_vjp(late_scatter[4])
    gx = gx1 + gx2
    gmod = gmod1 + gmod2
    gp = {n: gp1[n] + gp2[n] for n in small_params}

    sizes = [w[n].size for n in SMALL]
    offs, loss_off, n_small = _small_layout(sizes)
    pieces = []
    for n, size in zip(SMALL, sizes):
        pieces.append(gmod if n == "b_ada" else gp[n])
        if size % LANES:
            pieces.append(jnp.zeros((1, LANES - size % LANES), F32))
    pieces += [jnp.full((1, LANES), loss_part), jnp.zeros((1, n_small - loss_off - LANES), F32)]
    small_vec = jnp.concatenate(pieces, axis=1)
    small_all = _all_gather_small(small_vec.reshape(8, n_small // 8), "gather_small").reshape(N_DEV, n_small)

    g, delta, new_m, new_v = {}, {}, {}, {}

    def update(names, own, sib, after):
        for n, o, s in zip(names, own, sib):
            if n in TRANSPOSED_UPDATE:
                res = _adamw_halves(place, w[n].T, o, s, m[n].T, v[n].T, 1, "adamw_" + n, after)
                g[n], delta[n], new_m[n], new_v[n] = [r.T for r in res]
            else:
                g[n], delta[n], new_m[n], new_v[n] = _adamw_halves(place, w[n], o, s, m[n], v[n], HALF_AXIS[n],
                                                                   "adamw_" + n, after)

    early_grads = [gp1[n] for n in EARLY]
    early_plan = _direct_scatter_plan(EARLY)
    early_scatter = _exchange_start(
        "grad_scatter_early_start", early_grads,
        [lax.empty((7,) + _half_shape(gr.shape[1:], HALF_AXIS[n]), BF16) for n, gr in zip(EARLY, early_grads)],
        early_plan, 7 * len(EARLY), gx, small_all)
    small_all = early_scatter[4]
    late_grads, late_parts = _exchange_wait("grad_scatter_late_wait", late_scatter, late_plan, small_all)
    own_late = [_chip_sum_direct(place, gr, pt, HALF_AXIS[n], "grad_chip_sum_" + n, n in TRANSPOSED_UPDATE)
                for n, gr, pt in zip(LATE, late_grads, late_parts)]
    sib_late = _sibling_join(own_late, "grad_sibling_join_late", small_all)
    update(LATE, own_late, sib_late, small_all)

    *small_out, loss_row = _adamw_small([small[n] for n in SMALL], small_all, [m[n].reshape(1, -1) for n in SMALL],
                                        [v[n].reshape(1, -1) for n in SMALL], offs, loss_off)
    loss = loss_row[0, 0]
    for d, outs_d in zip((g, delta, new_m, new_v), small_out):
        d.update({n: o.reshape(w[n].shape) for n, o in zip(SMALL, outs_d)})

    dmod_all = small_all[:, :N_MOD * D_MODEL]
    g["w_ada"] = _ada_bwd(c_all, lax.dynamic_slice_in_dim(dmod_all, chip * ada_cols, ada_cols, axis=1))
    delta["w_ada"], new_m["w_ada"], new_v["w_ada"] = _adamw(w["w_ada"], g["w_ada"], m["w_ada"], v["w_ada"], "adamw_w_ada")

    early_grads, early_parts = _exchange_wait("grad_scatter_early_wait", early_scatter, early_plan, delta["w_ada"])
    own_early = [_chip_sum_direct(place, gr, pt, HALF_AXIS[n], "grad_chip_sum_" + n, n in TRANSPOSED_UPDATE)
                 for n, gr, pt in zip(EARLY, early_grads, early_parts)]
    sib_early = _sibling_join(own_early, "grad_sibling_join_early", delta["w_ada"])
    update(EARLY, own_early, sib_early, sib_early[0])

    def outs(d):
        return [d[n][None] for n in WEIGHTS]

    return (loss, gx[None], *outs(g), *outs(delta), *outs(new_m), *outs(new_v))
```

```python
import numpy as np
import jax
import jax.numpy as jnp
from jax import lax
from jax.experimental import pallas as pl
from jax.experimental.pallas import tpu as pltpu

F32 = jnp.float32
BF16 = jnp.bfloat16
MESH = pl.DeviceIdType.MESH
ANY = pl.BlockSpec(memory_space=pl.ANY)

D_MODEL = 1024
SB_HEADS = 8
SB_HEAD_DIM = 64
SB_WIDTH = 512
MLA_HEADS = 4
MLA_NOPE = 128
MLA_ROPE = 64
MLA_QK = 192
MLA_V = 128
MLA_Q_RANK = 384
MLA_KV_RANK = 256
D_FF = 2816
N_MOD = 6
ROPE_THETA = 10000.0
EPS = 1e-6
LANES = 128

ADAM_LR = 0.001
ADAM_B1 = 0.9
ADAM_B2 = 0.999
ADAM_EPS = 1e-08
ADAM_WD = 0.01
ADAM_STEP = 10

N_CHIPS = 4
N_DEV = 8
ROW_TILE = 256
MM_ROW_TILE = 512
ATT_BLK = 256
MM_VMEM_LIMIT = 56 * 1024 * 1024
FF_SHARD = D_FF // N_CHIPS
FF_SHARD_PAD = 768


def _mm(a, b, mode, name, tm, tn, out_dtype=F32):
    if mode == "nn":
        (m, k), n = a.shape, b.shape[1]
        a_spec = pl.BlockSpec((tm, k), lambda j, i: (i, 0))
        b_spec = pl.BlockSpec((k, tn), lambda j, i: (0, j))
        dims = (((1,), (0,)), ((), ()))
    elif mode == "nt":
        (m, k), n = a.shape, b.shape[0]
        a_spec = pl.BlockSpec((tm, k), lambda j, i: (i, 0))
        b_spec = pl.BlockSpec((tn, k), lambda j, i: (j, 0))
        dims = (((1,), (1,)), ((), ()))
    else:
        (k, m), n = a.shape, b.shape[1]
        a_spec = pl.BlockSpec((k, tm), lambda j, i: (0, i))
        b_spec = pl.BlockSpec((k, tn), lambda j, i: (0, j))
        dims = (((0,), (0,)), ((), ()))
    assert m % tm == 0 and n % tn == 0, (name, m, n, tm, tn)

    def body(a_ref, b_ref, o_ref):
        o_ref[...] = lax.dot_general(a_ref[...].astype(BF16), b_ref[...].astype(BF16), dims,
                                     preferred_element_type=F32).astype(out_dtype)

    return pl.pallas_call(
        body, name=name, grid=(n // tn, m // tm),
        in_specs=[a_spec, b_spec],
        out_specs=pl.BlockSpec((tm, tn), lambda j, i: (i, j)),
        out_shape=jax.ShapeDtypeStruct((m, n), out_dtype),
        compiler_params=pltpu.CompilerParams(dimension_semantics=("arbitrary", "arbitrary"),
                                             vmem_limit_bytes=MM_VMEM_LIMIT),
    )(a, b)


def _make_linear(name, tk_w, tn_w):
    @jax.custom_vjp
    def op(a, w):
        return _mm(a, w, "nn", name + "_fwd", MM_ROW_TILE, w.shape[1])

    def fwd(a, w):
        return op(a, w), (a, w)

    def bwd(res, dy):
        a, w = res
        da = _mm(dy, w, "nt", name + "_dx", MM_ROW_TILE, w.shape[0])
        dw = _mm(a, dy, "tn", name + "_dw", tk_w, tn_w, out_dtype=BF16)
        return da, dw

    op.defvjp(fwd, bwd)
    return op


def _make_linear_split(name, widths, tk_w):
    starts = [sum(widths[:g]) for g in range(len(widths))]

    def call_fwd(a, w):
        t, k = a.shape
        n = w.shape[1]

        def body(a_ref, w_ref, *o_refs):
            y = jnp.dot(a_ref[...].astype(BF16), w_ref[...], preferred_element_type=F32)
            for o_ref, s0, wd in zip(o_refs, starts, widths):
                o_ref[...] = y[:, s0:s0 + wd]

        return pl.pallas_call(
            body, name=name + "_fwd", grid=(t // MM_ROW_TILE,),
            in_specs=[pl.BlockSpec((MM_ROW_TILE, k), lambda i: (i, 0)), pl.BlockSpec((k, n), lambda i: (0, 0))],
            out_specs=[pl.BlockSpec((MM_ROW_TILE, wd), lambda i: (i, 0)) for wd in widths],
            out_shape=[jax.ShapeDtypeStruct((t, wd), F32) for wd in widths],
            compiler_params=pltpu.CompilerParams(dimension_semantics=("arbitrary",), vmem_limit_bytes=MM_VMEM_LIMIT),
        )(a, w)

    def call_dx(dys, w):
        t = dys[0].shape[0]
        k, n = w.shape

        def body(*refs):
            dy_refs, w_ref, o_ref = refs[:-2], refs[-2], refs[-1]
            acc = jnp.zeros((MM_ROW_TILE, k), F32)
            for dy_ref, s0, wd in zip(dy_refs, starts, widths):
                acc = acc + _nt(dy_ref[...].astype(BF16), w_ref[:, s0:s0 + wd])
            o_ref[...] = acc

        return pl.pallas_call(
            body, name=name + "_dx", grid=(t // MM_ROW_TILE,),
            in_specs=[pl.BlockSpec((MM_ROW_TILE, wd), lambda i: (i, 0)) for wd in widths]
            + [pl.BlockSpec((k, n), lambda i: (0, 0))],
            out_specs=pl.BlockSpec((MM_ROW_TILE, k), lambda i: (i, 0)),
            out_shape=jax.ShapeDtypeStruct((t, k), F32),
            compiler_params=pltpu.CompilerParams(dimension_semantics=("arbitrary",), vmem_limit_bytes=MM_VMEM_LIMIT),
        )(*dys, w)

    def call_dw(a, dys, w):
        t, k = a.shape
        n = w.shape[1]

        def body(a_ref, *refs):
            dy_refs, o_ref = refs[:-1], refs[-1]
            ab = a_ref[...].astype(BF16)
            for dy_ref, s0, wd in zip(dy_refs, starts, widths):
                o_ref[:, s0:s0 + wd] = _tn(ab, dy_ref[...].astype(BF16)).astype(BF16)
            if starts[-1] + widths[-1] < n:
                o_ref[:, starts[-1] + widths[-1]:] = jnp.zeros((tk_w, n - starts[-1] - widths[-1]), BF16)

        return pl.pallas_call(
            body, name=name + "_dw", grid=(k // tk_w,),
            in_specs=[pl.BlockSpec((t, tk_w), lambda i: (0, i))]
            + [pl.BlockSpec((t, wd), lambda i: (0, 0)) for wd in widths],
            out_specs=pl.BlockSpec((tk_w, n), lambda i: (i, 0)),
            out_shape=jax.ShapeDtypeStruct((k, n), BF16),
            compiler_params=pltpu.CompilerParams(dimension_semantics=("arbitrary",), vmem_limit_bytes=MM_VMEM_LIMIT),
        )(a, *dys)

    @jax.custom_vjp
    def op(a, w):
        return tuple(call_fwd(a, w))

    def fwd(a, w):
        return op(a, w), (a, w)

    def bwd(res, dys):
        a, w = res
        return call_dx(dys, w), call_dw(a, dys, w)

    op.defvjp(fwd, bwd)
    return op


def _row_spec(arr, tb):
    return pl.BlockSpec((tb, arr.shape[1]), lambda i: (i, 0))


def _full_spec(arr):
    return pl.BlockSpec(arr.shape, lambda i: (0, 0))


def _make_rowwise(name, f, n_rows, n_params, out_cols, diff_rows, out_dtypes=None, grad_dtypes=None):
    n_out = len(out_cols)
    out_dtypes = out_dtypes or [F32] * n_out
    grad_dtypes = grad_dtypes or [F32] * sum(diff_rows)

    def call_fwd(rows, params):
        t = rows[0].shape[0]

        def body(*refs):
            ins = [r[...] for r in refs[:n_rows + n_params]]
            outs = f(*ins)
            for o_ref, o in zip(refs[n_rows + n_params:], outs):
                o_ref[...] = o.astype(o_ref.dtype)

        return pl.pallas_call(
            body, name=name + "_fwd", grid=(t // ROW_TILE,),
            in_specs=[_row_spec(a, ROW_TILE) for a in rows] + [_full_spec(p) for p in params],
            out_specs=[pl.BlockSpec((ROW_TILE, n), lambda i: (i, 0)) for n in out_cols],
            out_shape=[jax.ShapeDtypeStruct((t, n), dt) for n, dt in zip(out_cols, out_dtypes)],
            compiler_params=pltpu.CompilerParams(dimension_semantics=("arbitrary",),
                                                 vmem_limit_bytes=MM_VMEM_LIMIT),
        )(*rows, *params)

    def call_bwd(rows, params, cts):
        t = rows[0].shape[0]
        d_rows = [a for a, d in zip(rows, diff_rows) if d]
        n_in = n_rows + n_params + n_out

        def body(*refs):
            ins = [r[...] for r in refs[:n_rows + n_params]]
            ct = tuple(r[...].astype(F32) for r in refs[n_rows + n_params:n_in])
            _, vjp = jax.vjp(f, *ins)
            grads = vjp(ct)
            out_refs = refs[n_in:]
            g_rows = [g for g, d in zip(grads[:n_rows], diff_rows) if d]
            for o_ref, g in zip(out_refs[:len(g_rows)], g_rows):
                o_ref[...] = g.astype(o_ref.dtype)
            p_refs = out_refs[len(g_rows):]

            if p_refs:
                @pl.when(pl.program_id(0) == 0)
                def _():
                    for p_ref in p_refs:
                        p_ref[...] = jnp.zeros_like(p_ref)

                for p_ref, g in zip(p_refs, grads[n_rows:]):
                    p_ref[...] += g

        return pl.pallas_call(
            body, name=name + "_bwd", grid=(t // ROW_TILE,),
            in_specs=[_row_spec(a, ROW_TILE) for a in rows] + [_full_spec(p) for p in params]
            + [_row_spec(c, ROW_TILE) for c in cts],
            out_specs=[_row_spec(a, ROW_TILE) for a in d_rows] + [_full_spec(p) for p in params],
            out_shape=[jax.ShapeDtypeStruct(a.shape, dt) for a, dt in zip(d_rows, grad_dtypes)]
            + [jax.ShapeDtypeStruct(p.shape, F32) for p in params],
            compiler_params=pltpu.CompilerParams(dimension_semantics=("arbitrary",),
                                                 vmem_limit_bytes=MM_VMEM_LIMIT),
        )(*rows, *params, *cts)

    @jax.custom_vjp
    def op(*args):
        return tuple(call_fwd(args[:n_rows], args[n_rows:]))

    def fwd(*args):
        return op(*args), args

    def bwd(args, cts):
        rows, params = args[:n_rows], args[n_rows:]
        outs = call_bwd(rows, params, cts)
        it = iter(outs)
        g_rows = [next(it) if d else jnp.zeros_like(a) for a, d in zip(rows, diff_rows)]
        return tuple(g_rows) + tuple(it)

    op.defvjp(fwd, bwd)
    return op


def _rms(x, g, n):
    return x * lax.rsqrt(jnp.sum(x * x, axis=-1, keepdims=True) * (1.0 / n) + EPS) * g


def _f_pre_attn(x, g, scale, shift):
    return (_rms(x, g, D_MODEL) * (1.0 + scale) + shift,)


def _f_mla_a(cq, ckv, gq, gkv):
    return _rms(cq, gq, MLA_Q_RANK), _rms(ckv, gkv, MLA_KV_RANK)


@jax.custom_vjp
def _split_lanes(x):
    return tuple(x[:, i * LANES:(i + 1) * LANES] for i in range(x.shape[1] // LANES))


def _split_lanes_fwd(x):
    return _split_lanes(x), None


def _split_lanes_bwd(_, cts):
    return (jnp.concatenate(cts, axis=1),)


_split_lanes.defvjp(_split_lanes_fwd, _split_lanes_bwd)


def _f_mla_b(qall, kn_all, kr, kr_sw, cos, sin, gqn, gqr, gqr_sw, gkn, gkr, gkr_sw):
    q = _split_lanes(qall)
    kn = _split_lanes(kn_all)
    qn_o, qr_o, kn_o = [], [], []
    for h in range(MLA_HEADS):
        qn, qr, qs = q[h], q[MLA_HEADS + h], q[2 * MLA_HEADS + h]
        ss = jnp.sum(qn * qn, axis=-1, keepdims=True) + jnp.sum(qr * qr, axis=-1, keepdims=True)
        rs = lax.rsqrt(ss * (1.0 / MLA_QK) + EPS)
        qn_o.append(qn * rs * gqn)
        qr_o.append((qr * rs * gqr) * cos + (qs * rs * gqr_sw) * sin)
        kn_o.append(_rms(kn[h], gkn, MLA_NOPE))
    rs = lax.rsqrt(jnp.sum(kr * kr, axis=-1, keepdims=True) * (1.0 / MLA_ROPE) + EPS)
    kr_o = (kr * rs * gkr) * cos + (kr_sw * rs * gkr_sw) * sin
    return (jnp.concatenate(qn_o, axis=1), jnp.concatenate(qr_o, axis=1), jnp.concatenate(kn_o, axis=1), kr_o)


def _f_post_attn(o_sb, o_mla, g_sb, g_mla):
    return (jnp.concatenate([_rms(o_sb, g_sb, SB_WIDTH), _rms(o_mla, g_mla, SB_WIDTH)], axis=1),)


def _f_pre_ffn(x, attn, gate, g, scale, shift):
    x2 = x + gate * attn
    return x2, _rms(x2, g, D_MODEL) * (1.0 + scale) + shift


def _f_swiglu(gt, up):
    return (gt / (1.0 + jnp.exp(-gt)) * up,)


def _f_loss(x2, ffn, target, gate):
    err = x2 + gate * ffn - target
    return (jnp.sum(err * err, axis=-1, keepdims=True) * (1.0 / D_MODEL),)


def _rope_tables(pos_col, freqs, sign):
    t = pos_col.shape[0]

    def body(p_ref, f_ref, s_ref, cos_ref, sin_ref):
        ang = p_ref[...].astype(F32) * f_ref[...]
        live = jnp.abs(s_ref[...])
        cos_ref[...] = jnp.cos(ang) * live
        sin_ref[...] = jnp.sin(ang) * s_ref[...]

    return pl.pallas_call(
        body, name="rope_tables", grid=(t // ROW_TILE,),
        in_specs=[pl.BlockSpec((ROW_TILE, 1), lambda i: (i, 0)), _full_spec(freqs), _full_spec(sign)],
        out_specs=[pl.BlockSpec((ROW_TILE, LANES), lambda i: (i, 0))] * 2,
        out_shape=[jax.ShapeDtypeStruct((t, LANES), F32)] * 2,
    )(pos_col, freqs, sign)


def _hi_lo_dot(x, tri):
    hi = x.astype(BF16)
    lo = (x - hi.astype(F32)).astype(BF16)
    return (jnp.dot(hi, tri, preferred_element_type=F32) + jnp.dot(lo, tri, preferred_element_type=F32))


def _tri(cmp):
    r = lax.broadcasted_iota(jnp.int32, (ATT_BLK, ATT_BLK), 0)
    c = lax.broadcasted_iota(jnp.int32, (ATT_BLK, ATT_BLK), 1)
    return cmp(r, c).astype(BF16)


def _nt(a, b):
    return lax.dot_general(a, b, (((1,), (1,)), ((), ())), preferred_element_type=F32)


def _tn(a, b):
    return lax.dot_general(a, b, (((0,), (0,)), ((), ())), preferred_element_type=F32)


def _sb_logs(z):
    lb = jnp.minimum(z, 0.0) - jnp.log(1.0 + jnp.exp(-jnp.abs(z)))
    return lb, lb - z


def _sb_fwd(q, k, v):
    t = q.shape[0]
    nq = t // ATT_BLK
    scale = SB_HEAD_DIM ** -0.5

    def body(q_ref, k_ref, v_ref, o_ref, tot_ref):
        qi = pl.program_id(1)
        lane = lax.broadcasted_iota(jnp.int32, (ATT_BLK, LANES), 1)
        tri = _tri(lambda r, c: r > c)
        qv = q_ref[...] * scale
        heads = [(lane // SB_HEAD_DIM) == hh for hh in range(2)]
        qms = [jnp.where(mine, qv, 0.0).astype(BF16) for mine in heads]

        def blocks(kbs, carry, diagonal):
            acc = carry[0]
            nb = len(kbs)
            chains = [(b, hh) for b in range(nb) for hh in range(2)]
            offs = [pl.multiple_of(kb * ATT_BLK, ATT_BLK) for kb in kbs]
            kks = [k_ref[pl.ds(off, ATT_BLK), :].astype(BF16) for off in offs]
            v_blks = [v_ref[pl.ds(off, ATT_BLK), :] for off in offs]
            if any(diagonal):
                valid = (lax.broadcasted_iota(jnp.int32, (ATT_BLK, ATT_BLK), 1)
                         < lax.broadcasted_iota(jnp.int32, (ATT_BLK, ATT_BLK), 0))
            zs = {ch: _nt(qms[ch[1]], kks[ch[0]]) for ch in chains}
            vvs = {(b, hh): jnp.where(heads[hh], v_blks[b], 0.0).astype(BF16) for b, hh in chains}
            logs = {ch: _sb_logs(zs[ch]) for ch in chains}
            l1ms = {ch: jnp.where(valid, logs[ch][1], 0.0) if diagonal[ch[0]] else logs[ch][1] for ch in chains}
            run = {(0, hh): carry[1 + hh] for hh in range(2)}
            for b, hh in chains:
                run[(b + 1, hh)] = run[(b, hh)] + jnp.sum(l1ms[(b, hh)], axis=-1, keepdims=True)
            afters = {ch: _hi_lo_dot(l1ms[ch], tri) for ch in chains}
            ws = {ch: jnp.exp(logs[ch][0] + (afters[ch] + run[ch])) for ch in chains}
            ws = {ch: jnp.where(valid, ws[ch], 0.0) if diagonal[ch[0]] else ws[ch] for ch in chains}
            for ch in chains:
                acc = acc + jnp.dot(ws[ch].astype(BF16), vvs[ch], preferred_element_type=F32)
            return (acc, run[(nb, 0)], run[(nb, 1)])

        zero = jnp.zeros((ATT_BLK, 1), F32)
        init = (jnp.zeros((ATT_BLK, LANES), F32), zero, zero)
        carry = lax.cond(qi % 2 == 1, lambda cr: blocks([qi, qi - 1], cr, (True, False)),
                         lambda cr: blocks([qi], cr, (True,)), init)
        top = qi - 1 - qi % 2
        carry = lax.fori_loop(0, qi // 2, lambda pr, cr: blocks([top - 2 * pr, top - 1 - 2 * pr], cr, (False, False)),
                              carry)
        o_ref[...] = carry[0]
        for hh in range(2):
            tot_ref[:, hh * LANES:(hh + 1) * LANES] = jnp.broadcast_to(carry[1 + hh], (ATT_BLK, LANES))

    return pl.pallas_call(
        body, name="sb_attn_fwd", grid=(SB_HEADS // 2, nq),
        in_specs=[pl.BlockSpec((ATT_BLK, LANES), lambda p, i: (i, p)),
                  pl.BlockSpec((t, LANES), lambda p, i: (0, p)),
                  pl.BlockSpec((t, LANES), lambda p, i: (0, p))],
        out_specs=[pl.BlockSpec((ATT_BLK, LANES), lambda p, i: (i, p)),
                   pl.BlockSpec((ATT_BLK, 2 * LANES), lambda p, i: (i, p))],
        out_shape=[jax.ShapeDtypeStruct((t, SB_WIDTH), F32), jax.ShapeDtypeStruct((t, SB_HEADS * LANES), F32)],
        compiler_params=pltpu.CompilerParams(dimension_semantics=("arbitrary", "arbitrary")),
    )(q, k, v)


def _sb_bwd(q, k, v, tot, do):
    t = q.shape[0]
    nq = t // ATT_BLK
    scale = SB_HEAD_DIM ** -0.5

    def body(q_ref, k_ref, v_ref, tot_ref, do_ref, dq_ref, dk_ref, dv_ref):
        qi = pl.program_id(1)

        @pl.when(qi == 0)
        def _():
            dk_ref[...] = jnp.zeros_like(dk_ref)
            dv_ref[...] = jnp.zeros_like(dv_ref)

        lane = lax.broadcasted_iota(jnp.int32, (ATT_BLK, LANES), 1)
        tri_incl = _tri(lambda r, c: r <= c)
        tri_lt = _tri(lambda r, c: r < c)
        qv = q_ref[...] * scale
        dov = do_ref[...]
        heads = [(lane // SB_HEAD_DIM) == hh for hh in range(2)]
        qms = [jnp.where(mine, qv, 0.0).astype(BF16) for mine in heads]
        doms = [jnp.where(mine, dov, 0.0).astype(BF16) for mine in heads]
        tots = [tot_ref[:, hh * LANES:hh * LANES + 1] for hh in range(2)]

        def blocks(kbs, carry, diagonal):
            dq = carry[0]
            nb = len(kbs)
            chains = [(b, hh) for b in range(nb) for hh in range(2)]
            offs = [pl.multiple_of(kb * ATT_BLK, ATT_BLK) for kb in kbs]
            k_blks = [k_ref[pl.ds(off, ATT_BLK), :] for off in offs]
            vvs = [v_ref[pl.ds(off, ATT_BLK), :].astype(BF16) for off in offs]
            if any(diagonal):
                valid = (lax.broadcasted_iota(jnp.int32, (ATT_BLK, ATT_BLK), 1)
                         < lax.broadcasted_iota(jnp.int32, (ATT_BLK, ATT_BLK), 0))
            kks = {(b, hh): jnp.where(heads[hh], k_blks[b], 0.0).astype(BF16) for b, hh in chains}
            zs = {ch: _nt(qms[ch[1]], kks[ch]) for ch in chains}
            dws = {ch: _nt(doms[ch[1]], vvs[ch[0]]) for ch in chains}
            logs = {ch: _sb_logs(zs[ch]) for ch in chains}
            lbs = {ch: logs[ch][0] for ch in chains}
            l1m_all = {ch: logs[ch][1] for ch in chains}
            l1ms = {ch: jnp.where(valid, l1m_all[ch], 0.0) if diagonal[ch[0]] else l1m_all[ch] for ch in chains}
            pre, c_de = {}, {}
            for hh in range(2):
                pre[(0, hh)], c_de[(0, hh)] = carry[1 + 2 * hh], carry[2 + 2 * hh]
            for b, hh in chains:
                pre[(b + 1, hh)] = pre[(b, hh)] + jnp.sum(l1ms[(b, hh)], axis=-1, keepdims=True)
            prefix = {ch: _hi_lo_dot(l1ms[ch], tri_incl) for ch in chains}
            ws = {ch: jnp.exp(lbs[ch] + (tots[ch[1]] - (prefix[ch] + pre[ch]))) for ch in chains}
            ws = {ch: jnp.where(valid, ws[ch], 0.0) if diagonal[ch[0]] else ws[ch] for ch in chains}
            d_es = {ch: ws[ch] * dws[ch] for ch in chains}
            for b, hh in chains:
                c_de[(b + 1, hh)] = c_de[(b, hh)] + jnp.sum(d_es[(b, hh)], axis=-1, keepdims=True)
            dvs = [_tn(ws[(b, 0)].astype(BF16), doms[0]) + _tn(ws[(b, 1)].astype(BF16), doms[1]) for b in range(nb)]
            dl1ms = {ch: jnp.dot(d_es[ch].astype(BF16), tri_lt, preferred_element_type=F32) + c_de[ch] for ch in chains}
            dzs = {ch: d_es[ch] * jnp.exp(l1m_all[ch]) - dl1ms[ch] * jnp.exp(lbs[ch]) for ch in chains}
            dzs = {ch: jnp.where(valid, dzs[ch], 0.0) if diagonal[ch[0]] else dzs[ch] for ch in chains}
            dzs = {ch: dzs[ch].astype(BF16) for ch in chains}
            for ch in chains:
                dq = dq + jnp.dot(dzs[ch], kks[ch], preferred_element_type=F32)
            for b in range(nb):
                dk_ref[pl.ds(offs[b], ATT_BLK), :] += _tn(dzs[(b, 0)], qms[0]) + _tn(dzs[(b, 1)], qms[1])
                dv_ref[pl.ds(offs[b], ATT_BLK), :] += dvs[b]
            return (dq, pre[(nb, 0)], c_de[(nb, 0)], pre[(nb, 1)], c_de[(nb, 1)])

        zero = jnp.zeros((ATT_BLK, 1), F32)
        carry = lax.fori_loop(0, qi // 2, lambda pr, cr: blocks([2 * pr, 2 * pr + 1], cr, (False, False)),
                              (jnp.zeros((ATT_BLK, LANES), F32), zero, zero, zero, zero))
        carry = lax.cond(qi % 2 == 1, lambda cr: blocks([qi - 1, qi], cr, (False, True)),
                         lambda cr: blocks([qi], cr, (True,)), carry)
        dq_ref[...] = carry[0] * scale

    return pl.pallas_call(
        body, name="sb_attn_bwd", grid=(SB_HEADS // 2, nq),
        in_specs=[pl.BlockSpec((ATT_BLK, LANES), lambda p, i: (i, p)),
                  pl.BlockSpec((t, LANES), lambda p, i: (0, p)),
                  pl.BlockSpec((t, LANES), lambda p, i: (0, p)),
                  pl.BlockSpec((ATT_BLK, 2 * LANES), lambda p, i: (i, p)),
                  pl.BlockSpec((ATT_BLK, LANES), lambda p, i: (i, p))],
        out_specs=[pl.BlockSpec((ATT_BLK, LANES), lambda p, i: (i, p)),
                   pl.BlockSpec((t, LANES), lambda p, i: (0, p)),
                   pl.BlockSpec((t, LANES), lambda p, i: (0, p))],
        out_shape=[jax.ShapeDtypeStruct((t, SB_WIDTH), F32)] * 3,
        compiler_params=pltpu.CompilerParams(dimension_semantics=("arbitrary", "arbitrary")),
    )(q, k, v, tot, do)


@jax.custom_vjp
def _sb_attention(q, k, v):
    return _sb_fwd(q, k, v)[0]


def _sb_attention_fwd(q, k, v):
    o, tot = _sb_fwd(q, k, v)
    return o, (q, k, v, tot)


def _sb_attention_bwd(res, do):
    return tuple(_sb_bwd(*res, do))


_sb_attention.defvjp(_sb_attention_fwd, _sb_attention_bwd)


def _mla_fwd(qn, qr, kn, kr, v):
    t = qn.shape[0]
    nq = t // ATT_BLK
    scale = MLA_QK ** -0.5

    def body(qn_ref, qr_ref, kn_ref, kr_ref, v_ref, o_ref, lse_ref):
        qi = pl.program_id(1)
        lanes = [slice(hh * LANES, (hh + 1) * LANES) for hh in range(2)]
        qnb = [qn_ref[:, sl].astype(BF16) for sl in lanes]
        qrb = [qr_ref[:, sl].astype(BF16) for sl in lanes]

        def blocks(kbs, carry, diagonal):
            nb = len(kbs)
            chains = [(b, hh) for b in range(nb) for hh in range(2)]
            offs = [pl.multiple_of(kb * ATT_BLK, ATT_BLK) for kb in kbs]
            krbs = [kr_ref[pl.ds(off, ATT_BLK), :].astype(BF16) for off in offs]
            accs, ms, ls = [carry[0], carry[3]], [carry[1], carry[4]], [carry[2], carry[5]]
            ss = {(b, hh): (_nt(qnb[hh], kn_ref[pl.ds(offs[b], ATT_BLK), lanes[hh]].astype(BF16))
                            + _nt(qrb[hh], krbs[b])) * scale for b, hh in chains}
            if any(diagonal):
                causal = (lax.broadcasted_iota(jnp.int32, (ATT_BLK, ATT_BLK), 1)
                          <= lax.broadcasted_iota(jnp.int32, (ATT_BLK, ATT_BLK), 0))
                ss = {ch: jnp.where(causal, ss[ch], -jnp.inf) if diagonal[ch[0]] else ss[ch] for ch in chains}
            m_new = list(ms)
            for b, hh in chains:
                m_new[hh] = jnp.maximum(m_new[hh], jnp.max(ss[(b, hh)], axis=-1, keepdims=True))
            ps = {(b, hh): jnp.exp(ss[(b, hh)] - m_new[hh]) for b, hh in chains}
            alphas = [jnp.exp(ms[hh] - m_new[hh]) for hh in range(2)]
            pvs = {(b, hh): jnp.dot(ps[(b, hh)].astype(BF16), v_ref[pl.ds(offs[b], ATT_BLK), lanes[hh]].astype(BF16),
                                    preferred_element_type=F32) for b, hh in chains}
            out = []
            for hh in range(2):
                acc, l = accs[hh] * alphas[hh], ls[hh] * alphas[hh]
                for b in range(nb):
                    acc, l = acc + pvs[(b, hh)], l + jnp.sum(ps[(b, hh)], axis=-1, keepdims=True)
                out += [acc, m_new[hh], l]
            return tuple(out)

        init = (jnp.zeros((ATT_BLK, LANES), F32), jnp.full((ATT_BLK, 1), -jnp.inf, F32), jnp.zeros((ATT_BLK, 1), F32))
        carry = lax.cond(qi % 2 == 1, lambda cr: blocks([qi, qi - 1], cr, (True, False)),
                         lambda cr: blocks([qi], cr, (True,)), init + init)
        carry = lax.fori_loop(0, qi // 2, lambda pr, cr: blocks([2 * pr, 2 * pr + 1], cr, (False, False)), carry)
        for hh in range(2):
            acc, m, l = carry[3 * hh:3 * hh + 3]
            o_ref[:, lanes[hh]] = acc / l
            lse_ref[:, lanes[hh]] = jnp.broadcast_to(m + jnp.log(l), (ATT_BLK, LANES))

    blk = pl.BlockSpec((ATT_BLK, 2 * LANES), lambda p, i: (i, p))
    full = pl.BlockSpec((t, 2 * LANES), lambda p, i: (0, p))
    return pl.pallas_call(
        body, name="mla_attn_fwd", grid=(MLA_HEADS // 2, nq),
        in_specs=[blk, blk, full, pl.BlockSpec((t, LANES), lambda p, i: (0, 0)), full],
        out_specs=[blk, blk],
        out_shape=[jax.ShapeDtypeStruct((t, MLA_HEADS * LANES), F32)] * 2,
        compiler_params=pltpu.CompilerParams(dimension_semantics=("arbitrary", "arbitrary")),
    )(qn, qr, kn, kr, v)


def _mla_bwd(qn, qr, kn, kr, v, o, lse, do):
    t = qn.shape[0]
    nq = t // ATT_BLK
    scale = MLA_QK ** -0.5

    def body(qn_ref, qr_ref, kn_ref, kr_ref, v_ref, o_ref, lse_ref, do_ref,
             dqn_ref, dqr_ref, dkn_ref, dkr_ref, dv_ref):
        pair = pl.program_id(0)
        qi = pl.program_id(1)

        @pl.when(qi == 0)
        def _():
            dkn_ref[...] = jnp.zeros_like(dkn_ref)
            dv_ref[...] = jnp.zeros_like(dv_ref)

        @pl.when((qi == 0) & (pair == 0))
        def _():
            dkr_ref[...] = jnp.zeros_like(dkr_ref)

        lanes = [slice(hh * LANES, (hh + 1) * LANES) for hh in range(2)]
        qnb = [qn_ref[:, sl].astype(BF16) for sl in lanes]
        qrb = [qr_ref[:, sl].astype(BF16) for sl in lanes]
        dob = [do_ref[:, sl].astype(BF16) for sl in lanes]
        delta = [jnp.sum(do_ref[:, sl] * o_ref[:, sl], axis=-1, keepdims=True) for sl in lanes]
        lse_v = [lse_ref[:, hh * LANES:hh * LANES + 1] for hh in range(2)]

        def blocks(kbs, carry, diagonal):
            nb = len(kbs)
            chains = [(b, hh) for b in range(nb) for hh in range(2)]
            offs = [pl.multiple_of(kb * ATT_BLK, ATT_BLK) for kb in kbs]
            krbs = [kr_ref[pl.ds(off, ATT_BLK), :].astype(BF16) for off in offs]
            knb = {(b, hh): kn_ref[pl.ds(offs[b], ATT_BLK), lanes[hh]].astype(BF16) for b, hh in chains}
            vb = {(b, hh): v_ref[pl.ds(offs[b], ATT_BLK), lanes[hh]].astype(BF16) for b, hh in chains}
            ss = {(b, hh): _nt(qnb[hh], knb[(b, hh)]) + _nt(qrb[hh], krbs[b]) for b, hh in chains}
            dps = {(b, hh): _nt(dob[hh], vb[(b, hh)]) for b, hh in chains}
            ps = {(b, hh): jnp.exp(ss[(b, hh)] * scale - lse_v[hh]) for b, hh in chains}
            if any(diagonal):
                causal = (lax.broadcasted_iota(jnp.int32, (ATT_BLK, ATT_BLK), 1)
                          <= lax.broadcasted_iota(jnp.int32, (ATT_BLK, ATT_BLK), 0))
                ps = {ch: jnp.where(causal, ps[ch], 0.0) if diagonal[ch[0]] else ps[ch] for ch in chains}
            dss = {(b, hh): (ps[(b, hh)] * (dps[(b, hh)] - delta[hh]) * scale).astype(BF16) for b, hh in chains}
            for b, hh in chains:
                dv_ref[pl.ds(offs[b], ATT_BLK), lanes[hh]] += _tn(ps[(b, hh)].astype(BF16), dob[hh])
            for b, hh in chains:
                dkn_ref[pl.ds(offs[b], ATT_BLK), lanes[hh]] += _tn(dss[(b, hh)], qnb[hh])
            for b in range(nb):
                dkr_ref[pl.ds(offs[b], ATT_BLK), :] += _tn(dss[(b, 0)], qrb[0]) + _tn(dss[(b, 1)], qrb[1])
            out = list(carry)
            for b, hh in chains:
                out[2 * hh] = out[2 * hh] + jnp.dot(dss[(b, hh)], knb[(b, hh)], preferred_element_type=F32)
                out[2 * hh + 1] = out[2 * hh + 1] + jnp.dot(dss[(b, hh)], krbs[b], preferred_element_type=F32)
            return tuple(out)

        zero = jnp.zeros((ATT_BLK, LANES), F32)
        carry = lax.fori_loop(0, qi // 2, lambda pr, cr: blocks([2 * pr, 2 * pr + 1], cr, (False, False)),
                              (zero, zero, zero, zero))
        carry = lax.cond(qi % 2 == 1, lambda cr: blocks([qi - 1, qi], cr, (False, True)),
                         lambda cr: blocks([qi], cr, (True,)), carry)
        for hh in range(2):
            dqn_ref[:, lanes[hh]] = carry[2 * hh]
            dqr_ref[:, lanes[hh]] = carry[2 * hh + 1]

    blk = pl.BlockSpec((ATT_BLK, 2 * LANES), lambda p, i: (i, p))
    full = pl.BlockSpec((t, 2 * LANES), lambda p, i: (0, p))
    shared = pl.BlockSpec((t, LANES), lambda p, i: (0, 0))
    wide = jax.ShapeDtypeStruct((t, MLA_HEADS * LANES), F32)
    return pl.pallas_call(
        body, name="mla_attn_bwd", grid=(MLA_HEADS // 2, nq),
        in_specs=[blk, blk, full, shared, full, blk, blk, blk],
        out_specs=[blk, blk, full, shared, full],
        out_shape=[wide, wide, wide, jax.ShapeDtypeStruct((t, LANES), F32), wide],
        compiler_params=pltpu.CompilerParams(dimension_semantics=("arbitrary", "arbitrary")),
    )(qn, qr, kn, kr, v, o, lse, do)


@jax.custom_vjp
def _mla_attention(qn, qr, kn, kr, v):
    return _mla_fwd(qn, qr, kn, kr, v)[0]


def _mla_attention_fwd(qn, qr, kn, kr, v):
    o, lse = _mla_fwd(qn, qr, kn, kr, v)
    return o, (qn, qr, kn, kr, v, o, lse)


def _mla_attention_bwd(res, do):
    return tuple(_mla_bwd(*res, do))


_mla_attention.defvjp(_mla_attention_fwd, _mla_attention_bwd)


def _ffn_in(h, wg, wu):
    t, k = h.shape
    n_sh, _, cc = wg.shape

    def body(h_ref, wg_ref, wu_ref, g_ref, u_ref, a_ref):
        hb = h_ref[...].astype(BF16)
        for j in range(n_sh):
            cols = slice(j * cc, (j + 1) * cc)
            g = jnp.dot(hb, wg_ref[j], preferred_element_type=F32)
            u = jnp.dot(hb, wu_ref[j], preferred_element_type=F32)
            g_ref[:, cols] = g.astype(BF16)
            u_ref[:, cols] = u.astype(BF16)
            a_ref[:, cols] = _f_swiglu(g, u)[0].astype(BF16)

    w_spec = pl.BlockSpec((n_sh, k, cc), lambda i: (0, 0, 0))
    o_spec = pl.BlockSpec((MM_ROW_TILE, n_sh * cc), lambda i: (i, 0))
    wide = jax.ShapeDtypeStruct((t, n_sh * cc), BF16)
    return pl.pallas_call(
        body, name="ffn_in_fwd", grid=(t // MM_ROW_TILE,),
        in_specs=[pl.BlockSpec((MM_ROW_TILE, k), lambda i: (i, 0)), w_spec, w_spec],
        out_specs=[o_spec, o_spec, o_spec],
        out_shape=[wide, wide, wide],
        compiler_params=pltpu.CompilerParams(dimension_semantics=("arbitrary",), vmem_limit_bytes=MM_VMEM_LIMIT),
    )(h, wg, wu)


def _ffn_mid_bwd(dy, wd, g, u):
    t, n = dy.shape
    n_sh, cc, _ = wd.shape

    def body(dy_ref, wd_ref, g_ref, u_ref, dg_ref, du_ref):
        d_act = _nt(dy_ref[...].astype(BF16), wd_ref[...])
        _, vjp = jax.vjp(_f_swiglu, g_ref[...].astype(F32), u_ref[...].astype(F32))
        dg, du = vjp((d_act,))
        dg_ref[...] = dg.astype(BF16)
        du_ref[...] = du.astype(BF16)

    blk = pl.BlockSpec((MM_ROW_TILE, cc), lambda j, i: (i, j))
    wide = jax.ShapeDtypeStruct((t, n_sh * cc), BF16)
    return pl.pallas_call(
        body, name="ffn_mid_bwd", grid=(n_sh, t // MM_ROW_TILE),
        in_specs=[pl.BlockSpec((MM_ROW_TILE, n), lambda j, i: (i, 0)),
                  pl.BlockSpec((None, cc, n), lambda j, i: (j, 0, 0)), blk, blk],
        out_specs=[blk, blk], out_shape=[wide, wide],
        compiler_params=pltpu.CompilerParams(dimension_semantics=("arbitrary", "arbitrary"),
                                             vmem_limit_bytes=MM_VMEM_LIMIT),
    )(dy, wd, g, u)


def _ffn_dh(dg, du, wg, wu):
    t = dg.shape[0]
    n_sh, k, cc = wg.shape

    def body(dg_ref, du_ref, wg_ref, wu_ref, o_ref):
        acc = jnp.zeros((MM_ROW_TILE, k), F32)
        for j in range(n_sh):
            cols = slice(j * cc, (j + 1) * cc)
            acc = acc + _nt(dg_ref[:, cols], wg_ref[j]) + _nt(du_ref[:, cols], wu_ref[j])
        o_ref[...] = acc

    blk = pl.BlockSpec((MM_ROW_TILE, n_sh * cc), lambda i: (i, 0))
    w_spec = pl.BlockSpec((n_sh, k, cc), lambda i: (0, 0, 0))
    return pl.pallas_call(
        body, name="ffn_dh", grid=(t // MM_ROW_TILE,),
        in_specs=[blk, blk, w_spec, w_spec],
        out_specs=pl.BlockSpec((MM_ROW_TILE, k), lambda i: (i, 0)),
        out_shape=jax.ShapeDtypeStruct((t, k), F32),
        compiler_params=pltpu.CompilerParams(dimension_semantics=("arbitrary",), vmem_limit_bytes=MM_VMEM_LIMIT),
    )(dg, du, wg, wu)


def _ffn_dw_in(h, dy, n_sh, name):
    t, k = h.shape
    cc = dy.shape[1] // n_sh
    tk = 512

    def body(h_ref, dy_ref, o_ref):
        o_ref[...] = _tn(h_ref[...].astype(BF16), dy_ref[...]).astype(BF16)

    return pl.pallas_call(
        body, name=name, grid=(n_sh, k // tk),
        in_specs=[pl.BlockSpec((t, tk), lambda j, i: (0, i)), pl.BlockSpec((t, cc), lambda j, i: (0, j))],
        out_specs=pl.BlockSpec((None, tk, cc), lambda j, i: (j, i, 0)),
        out_shape=jax.ShapeDtypeStruct((n_sh, k, cc), BF16),
        compiler_params=pltpu.CompilerParams(dimension_semantics=("arbitrary", "arbitrary"),
                                             vmem_limit_bytes=MM_VMEM_LIMIT),
    )(h, dy)


@jax.custom_vjp
def _ffn_block(h, wg, wu, wd):
    act = _ffn_in(h, wg, wu)[2]
    return _mm(act, wd.reshape(-1, wd.shape[2]), "nn", "ffn_down_fwd", MM_ROW_TILE, wd.shape[2])


def _ffn_block_fwd(h, wg, wu, wd):
    g, u, act = _ffn_in(h, wg, wu)
    y = _mm(act, wd.reshape(-1, wd.shape[2]), "nn", "ffn_down_fwd", MM_ROW_TILE, wd.shape[2])
    return y, (h, wg, wu, wd, g, u, act)


def _ffn_block_bwd(res, dy):
    h, wg, wu, wd, g, u, act = res
    dg, du = _ffn_mid_bwd(dy, wd, g, u)
    dh = _ffn_dh(dg, du, wg, wu)
    n_sh = wg.shape[0]
    dwg = _ffn_dw_in(h, dg, n_sh, "ffn_gate_dw")
    dwu = _ffn_dw_in(h, du, n_sh, "ffn_up_dw")
    dwd = _mm(act, dy, "tn", "ffn_down_dw", 256, wd.shape[2], out_dtype=BF16).reshape(wd.shape)
    return dh, dwg, dwu, dwd


_ffn_block.defvjp(_ffn_block_fwd, _ffn_block_bwd)


def _swap_halves(w):
    half = w.shape[-1] // 2
    return jnp.concatenate([w[..., half:], w[..., :half]], axis=-1)


def _pad_lanes(w):
    return jnp.concatenate([w, jnp.zeros(w.shape[:-1] + (LANES - w.shape[-1],), w.dtype)], axis=-1)


def _join_cols(shards):
    return shards.transpose(1, 0, 2).reshape(shards.shape[1], -1)


def _mod_parts(mod):
    return [mod[:, i * D_MODEL:(i + 1) * D_MODEL] for i in range(N_MOD)]


def _local_loss(x, mod, p, cos, sin, target):
    return _ffn_stage(x, _mixing_stage(x, mod, p, cos, sin), mod, p, target)


def _mixing_stage(x, mod, p, cos, sin):
    shift1, scale1 = _mod_parts(mod)[:2]

    w_in = _join_cols(p["w_in"])
    k_rope_w = w_in[:, 2176:2240]
    w_in_ext = jnp.concatenate([w_in[:, :2176], _pad_lanes(k_rope_w), _pad_lanes(_swap_halves(k_rope_w)),
                                jnp.zeros((D_MODEL, LANES), w_in.dtype)], axis=1)
    (h1,) = _make_rowwise("pre_attn", _f_pre_attn, 1, 3, [D_MODEL], [True], out_dtypes=[BF16])(
        x, p["norm_attn"], scale1, shift1)
    q_sb, k_sb, v_sb, cq, ckv, kr, kr_sw = _make_linear_split(
        "in_proj", (SB_WIDTH, SB_WIDTH, SB_WIDTH, MLA_Q_RANK, MLA_KV_RANK, LANES, LANES), 512)(h1, w_in_ext)

    o_sb = _sb_attention(q_sb, k_sb, v_sb)

    wq = _join_cols(p["w_q_up"]).reshape(MLA_Q_RANK, MLA_HEADS, MLA_QK)
    wq_n, wq_r = wq[:, :, :MLA_NOPE], wq[:, :, MLA_NOPE:]
    w_q_ext = jnp.concatenate([wq_n.reshape(MLA_Q_RANK, -1), _pad_lanes(wq_r).reshape(MLA_Q_RANK, -1),
                               _pad_lanes(_swap_halves(wq_r)).reshape(MLA_Q_RANK, -1)], axis=1)
    wkv = _join_cols(p["w_kv_up"]).reshape(MLA_KV_RANK, MLA_HEADS, MLA_NOPE + MLA_V)
    w_kv_ext = jnp.concatenate([wkv[:, :, :MLA_NOPE].reshape(MLA_KV_RANK, -1),
                                wkv[:, :, MLA_NOPE:].reshape(MLA_KV_RANK, -1)], axis=1)
    cqn, ckvn = _make_rowwise("mla_a", _f_mla_a, 2, 2, [MLA_Q_RANK, MLA_KV_RANK], [True, True],
                              out_dtypes=[BF16, BF16], grad_dtypes=[BF16, BF16])(
        cq, ckv, p["q_a_norm"], p["kv_a_norm"])
    qall = _make_linear("q_up", 384, 768)(cqn, w_q_ext)
    kn_all, v_mla = _make_linear_split("kv_up", (MLA_HEADS * MLA_NOPE, MLA_HEADS * MLA_V), MLA_KV_RANK)(ckvn, w_kv_ext)
    gq = p["q_norm"]
    gkr = p["k_rope_norm"]
    qn, qr, kn, krr = _make_rowwise("mla_b", _f_mla_b, 6, 6, [512, 512, 512, LANES],
                                    [True, True, True, True, False, False],
                                    out_dtypes=[BF16] * 4, grad_dtypes=[BF16] * 4)(
        qall, kn_all, kr, kr_sw, cos, sin,
        gq[:, :MLA_NOPE], _pad_lanes(gq[:, MLA_NOPE:]), _pad_lanes(_swap_halves(gq[:, MLA_NOPE:])),
        p["k_nope_norm"], _pad_lanes(gkr), _pad_lanes(_swap_halves(gkr)))
    o_mla = _mla_attention(qn, qr, kn, krr, v_mla)

    (mixed,) = _make_rowwise("post_attn", _f_post_attn, 2, 2, [D_MODEL], [True, True])(
        o_sb, o_mla, p["out_norm_sb"], p["out_norm_mla"])
    return mixed


def _ffn_stage(x, mixed, mod, p, target):
    _, _, gate1, shift2, scale2, gate2 = _mod_parts(mod)
    attn = _make_linear("out_proj", 512, 512)(mixed, p["w_out"].reshape(D_MODEL, D_MODEL))

    x2, h2 = _make_rowwise("pre_ffn", _f_pre_ffn, 2, 4, [D_MODEL, D_MODEL], [True, True],
                           out_dtypes=[F32, BF16], grad_dtypes=[F32, BF16])(
        x, attn, gate1, p["norm_ffn"], scale2, shift2)
    ffn = _ffn_block(h2, p["w_gate"], p["w_up"], p["w_down"])
    (row_loss,) = _make_rowwise("loss", _f_loss, 3, 1, [1], [True, True, False], grad_dtypes=[F32, BF16])(
        x2, ffn, target, gate2)
    return 0.5 * jnp.sum(row_loss)


def _my_place():
    return lax.axis_index("x"), lax.axis_index("y"), lax.axis_index("c")


def _all_gather_small(block, name):
    m_per, n = block.shape

    def body(x_ref, out_ref, send_sems, recv_sems, local_sem):
        x, y, c = _my_place()
        me, sibling = (x, y, c), (x, y, 1 - c)
        chips = [(1 - x, y), (x, 1 - y), (1 - x, 1 - y)]

        def rows(px, py, pc):
            return out_ref.at[pl.ds((4 * px + 2 * py + pc) * m_per, m_per), :]

        def copy(k, blk, to, src=None):
            return pltpu.make_async_remote_copy(
                src_ref=rows(*blk) if src is None else src, dst_ref=rows(*blk),
                send_sem=send_sems.at[k], recv_sem=recv_sems.at[k], device_id=to, device_id_type=MESH)

        mine = pltpu.make_async_copy(x_ref, rows(*me), local_sem)
        mine.start()
        first = [copy(0, me, sibling, src=x_ref)]
        first += [copy(1 + j, me, (*chip, c), src=x_ref) for j, chip in enumerate(chips)]
        for cp in first:
            cp.start()
        passed = [copy(4 + j, (*chip, c), sibling) for j, chip in enumerate(chips)]
        for j, chip in enumerate(chips):
            copy(1 + j, (*chip, c), me).wait_recv()
            passed[j].start()
        copy(0, sibling, me).wait_recv()
        for j, chip in enumerate(chips):
            copy(4 + j, (*chip, 1 - c), me).wait_recv()
        for cp in first + passed:
            cp.wait_send()
        mine.wait()

    return pl.pallas_call(
        body, name=name,
        out_shape=jax.ShapeDtypeStruct((N_DEV * m_per, n), block.dtype),
        in_specs=[pl.BlockSpec(memory_space=pltpu.VMEM)],
        out_specs=pl.BlockSpec(memory_space=pltpu.VMEM),
        scratch_shapes=[pltpu.SemaphoreType.DMA((7,)), pltpu.SemaphoreType.DMA((7,)), pltpu.SemaphoreType.DMA],
    )(block)


EARLY = ("w_in", "w_q_up", "w_kv_up")
LATE = ("w_out", "w_gate", "w_up", "w_down")
BIG = EARLY + LATE
TRANSPOSED_UPDATE = ("w_in", "w_gate", "w_up")
HALF_AXIS = {"w_in": 0, "w_q_up": 0, "w_kv_up": 0, "w_out": 0, "w_gate": 0, "w_up": 0, "w_down": 1}


def _half(ref, h, axis, lead=()):
    trail = ref.shape[len(lead):]
    idx = list(lead) + [slice(None)] * len(trail)
    at = len(trail) - 2 + axis
    n2 = trail[at] // 2
    idx[len(lead) + at] = pl.ds(h * n2, n2)
    return ref.at[tuple(idx)]


def _half_shape(shape, axis):
    shape = list(shape)
    shape[len(shape) - 2 + axis] //= 2
    return tuple(shape)


def _remote(src, dst, send_sems, recv_sems, k, to):
    return pltpu.make_async_remote_copy(src_ref=src, dst_ref=dst, send_sem=send_sems.at[k],
                                        recv_sem=recv_sems.at[k], device_id=to, device_id_type=MESH)


def _gather_weights(names, shards, after):
    n_w = len(shards)
    axes = [HALF_AXIS[n] for n in names]

    def body(*refs):
        w_refs, out_refs, token = refs[:n_w], refs[n_w + 1:2 * n_w + 1], refs[2 * n_w + 1]
        send_sems, recv_sems, local_sems = refs[2 * n_w + 2:]
        token[...] = jnp.zeros_like(token)
        x, y, c = _my_place()
        sibling = (x, y, 1 - c)
        chips = [(1 - x, y), (x, 1 - y), (1 - x, 1 - y)]
        me = 2 * x + y
        mine =[pltpu.make_async_copy(w, o.at[me], local_sems.at[i]) for i, (w, o) in enumerate(zip(w_refs, out_refs))]
        for cp in mine:
            cp.start()
        first = [_remote(_half(w_refs[i], c, axes[i]), _half(out_refs[i], c, axes[i], (me,)),
                         send_sems, recv_sems, 6 * i + j, (*chip, c))
                 for i in range(n_w) for j, chip in enumerate(chips)]
        for cp in first:
            cp.start()
        passed = []
        for j, (cx, cy) in enumerate(chips):
            for i in range(n_w):
                blk = _half(out_refs[i], c, axes[i], (2 * cx + cy,))
                _remote(blk, blk, send_sems, recv_sems, 6 * i + j, (cx, cy, c)).wait_recv()
                cp = _remote(blk, blk, send_sems, recv_sems, 6 * i + 3 + j, sibling)
                cp.start()
                passed.append(cp)
        for j, (cx, cy) in enumerate(chips):
            for i in range(n_w):
                blk = _half(out_refs[i], 1 - c, axes[i], (2 * cx + cy,))
                _remote(blk, blk, send_sems, recv_sems, 6 * i + 3 + j, sibling).wait_recv()
        for cp in first + passed:
            cp.wait_send()
        for cp in mine:
            cp.wait()

    outs = pl.pallas_call(
        body, name="gather_weights",
        out_shape=[jax.ShapeDtypeStruct((N_CHIPS,) + s.shape, s.dtype) for s in shards]
        + [jax.ShapeDtypeStruct((8, LANES), F32)],
        in_specs=[ANY] * (n_w + 1), out_specs=[ANY] * n_w + [pl.BlockSpec(memory_space=pltpu.VMEM)],
        scratch_shapes=[pltpu.SemaphoreType.DMA((6 * n_w,)), pltpu.SemaphoreType.DMA((6 * n_w,)),
                        pltpu.SemaphoreType.DMA((n_w,))],
    )(*shards, after)
    return outs[:n_w], outs[n_w]


def _pair_exchange(names, grads, call_name):
    n_w = len(grads)
    axes = [HALF_AXIS[n] for n in names]

    def body(*refs):
        g_refs, t_refs = refs[:n_w], refs[n_w:2 * n_w]
        send_sems, recv_sems = refs[2 * n_w:]
        x, y, c = _my_place()
        sends = [_remote(_half(g_refs[i], 1 - c, axes[i]), t_refs[i], send_sems, recv_sems, i, (x, y, 1 - c))
                 for i in range(n_w)]
        for cp in sends:
            cp.start()
        for cp in sends:
            cp.wait_recv()
        for cp in sends:
            cp.wait_send()

    return pl.pallas_call(
        body, name=call_name,
        out_shape=[jax.ShapeDtypeStruct(_half_shape(g.shape, a), g.dtype) for g, a in zip(grads, axes)],
        in_specs=[ANY] * n_w, out_specs=[ANY] * n_w,
        scratch_shapes=[pltpu.SemaphoreType.DMA((n_w,)), pltpu.SemaphoreType.DMA((n_w,))],
    )(*grads)


def _sibling_join(halves, name, after):
    n_w = len(halves)

    def body(*refs):
        s_refs, j_refs = refs[:n_w], refs[n_w + 1:2 * n_w + 1]
        send_sems, recv_sems = refs[2 * n_w + 1:]
        x, y, c = _my_place()
        sends = [_remote(s_refs[i], j_refs[i], send_sems, recv_sems, i, (x, y, 1 - c)) for i in range(n_w)]
        for cp in sends:
            cp.start()
        for cp in sends:
            cp.wait_recv()
        for cp in sends:
            cp.wait_send()

    return pl.pallas_call(
        body, name=name,
        out_shape=[jax.ShapeDtypeStruct(s.shape, s.dtype) for s in halves],
        in_specs=[ANY] * (n_w + 1), out_specs=[ANY] * n_w,
        scratch_shapes=[pltpu.SemaphoreType.DMA((n_w,)), pltpu.SemaphoreType.DMA((n_w,))],
    )(*halves, after)


HBM_SPEC = pl.BlockSpec(memory_space=pltpu.HBM)
SEM_SPEC = pl.BlockSpec(memory_space=pltpu.SEMAPHORE)
DATAFLOW = pltpu.SideEffectType.DATAFLOW_SIDE_EFFECTING


def _in_hbm(a):
    return pltpu.with_memory_space_constraint(a, pltpu.HBM)


def _exchange_start(name, srcs, lands, plan, n_copies, after, thru):
    n = len(srcs)

    def body(*refs):
        src_refs, land_refs = refs[:n], refs[n:2 * n]
        send_sems, recv_sems = refs[2 * n + 2], refs[2 * n + 3]
        for k, (src, dst, to, k_recv) in enumerate(plan(src_refs, land_refs)):
            pltpu.make_async_remote_copy(src_ref=src, dst_ref=dst, send_sem=send_sems.at[k],
                                         recv_sem=recv_sems.at[k_recv], device_id=to, device_id_type=MESH).start()

    outs = pl.pallas_call(
        body, name=name,
        out_shape=(pltpu.SemaphoreType.DMA((n_copies,)), pltpu.SemaphoreType.DMA((n_copies,)),
                   *[pltpu.HBM(a.shape, a.dtype) for a in list(srcs) + list(lands) + [thru]]),
        in_specs=[HBM_SPEC] * (2 * n + 1) + [ANY],
        out_specs=(SEM_SPEC, SEM_SPEC, *[HBM_SPEC] * (2 * n + 1)),
        input_output_aliases={i: 2 + i for i in range(2 * n + 1)},
        compiler_params=pltpu.CompilerParams(has_side_effects=DATAFLOW),
    )(*[_in_hbm(a) for a in list(srcs) + list(lands) + [thru]], after)
    return outs[0], outs[1], outs[2:2 + n], outs[2 + n:2 + 2 * n], outs[2 + 2 * n]


def _exchange_wait(name, started, plan, after):
    send_sems, recv_sems, srcs, lands, _ = started
    n = len(srcs)

    def body(*refs):
        src_refs, land_refs = refs[:n], refs[n:2 * n]
        s_sems, r_sems = refs[2 * n], refs[2 * n + 1]
        for k, (src, dst, to, _) in enumerate(plan(src_refs, land_refs)):
            cp = _remote(src, dst, s_sems, r_sems, k, to)
            cp.wait_send()
            cp.wait_recv()

    outs = pl.pallas_call(
        body, name=name,
        out_shape=tuple(pltpu.HBM(a.shape, a.dtype) for a in list(srcs) + list(lands)),
        in_specs=[HBM_SPEC] * (2 * n) + [SEM_SPEC, SEM_SPEC, ANY],
        out_specs=tuple([HBM_SPEC] * (2 * n)),
        input_output_aliases={i: i for i in range(2 * n)},
        compiler_params=pltpu.CompilerParams(has_side_effects=DATAFLOW),
    )(*srcs, *lands, send_sems, recv_sems, after)
    return outs[:n], outs[n:]


def _late_gather_plan(src_refs, land_refs):
    x, y, c = _my_place()
    chips = [(1 - x, y), (x, 1 - y), (1 - x, 1 - y)]
    plan = [(src, land.at[2 * x + y], (cx, cy, c)) for src, land in zip(src_refs, land_refs) for cx, cy in chips]
    return [entry + (k,) for k, entry in enumerate(plan)]


def _late_scatter_plan(src_refs, land_refs):
    x, y, c = _my_place()
    chips = [(1 - x, y), (x, 1 - y), (1 - x, 1 - y)]
    plan = [(src.at[2 * cx + cy], land.at[j], (cx, cy, c))
            for src, land in zip(src_refs, land_refs) for j, (cx, cy) in enumerate(chips)]
    return [entry + (k,) for k, entry in enumerate(plan)]


def _direct_scatter_plan(names):
    axes = [HALF_AXIS[n] for n in names]

    def plan(src_refs, land_refs):
        x, y, c = _my_place()
        chips = [(1 - x, y), (x, 1 - y), (1 - x, 1 - y)]
        out = []
        for i, (src, land) in enumerate(zip(src_refs, land_refs)):
            for f, (cx, cy) in enumerate(chips):
                for core in range(2):
                    out.append((_half(src, core, axes[i], (2 * cx + cy,)), land.at[2 * f + c], (cx, cy, core),
                                7 * i + 2 * f + c))
            out.append((_half(src, 1 - c, axes[i], (2 * x + y,)), land.at[6], (x, y, 1 - c), 7 * i + 6))
        return out

    return plan


def _row_tile(rows, mult=16, limit=ROW_TILE):
    return max(d for d in range(mult, limit + 1, mult) if rows % d == 0)


def _pair_sum(place, g, theirs, axis, name):
    nj, rr, cc = theirs.shape
    tr = _row_tile(rr, limit=1024)
    nb = rr // tr
    if axis == 0:
        g_map = lambda j, i, pr: (j, pr[0] * nb + i, 0)
    else:
        g_map = lambda j, i, pr: (j, i, pr[0])

    def body(pr, g_ref, t_ref, o_ref):
        o_ref[...] = (g_ref[...].astype(F32) + t_ref[...].astype(F32)).astype(BF16)

    spec = pl.BlockSpec((None, tr, cc), lambda j, i, pr: (j, i, 0))
    return pl.pallas_call(
        body, name=name,
        grid_spec=pltpu.PrefetchScalarGridSpec(
            num_scalar_prefetch=1, grid=(nj, nb),
            in_specs=[pl.BlockSpec((None, tr, cc), g_map), spec], out_specs=spec),
        out_shape=jax.ShapeDtypeStruct(theirs.shape, BF16))(place, g, theirs)


def _chip_sum(place, pair_sums, parts, name, transposed):
    _, rr, cc = parts.shape
    tr = _row_tile(rr, LANES) if transposed else _row_tile(rr, limit=1024)

    def body(pr, h_ref, p_ref, o_ref):
        acc = p_ref[0].astype(F32)
        for j in range(1, N_CHIPS - 1):
            acc = acc + p_ref[j].astype(F32)
        acc = acc + h_ref[...].astype(F32)
        o_ref[...] = (acc.T if transposed else acc).astype(BF16)

    out_spec = pl.BlockSpec((cc, tr), lambda i, pr: (0, i)) if transposed else pl.BlockSpec((tr, cc), lambda i, pr: (i, 0))
    return pl.pallas_call(
        body, name=name,
        grid_spec=pltpu.PrefetchScalarGridSpec(
            num_scalar_prefetch=1, grid=(rr // tr,),
            in_specs=[pl.BlockSpec((None, tr, cc), lambda i, pr: (pr[1], i, 0)),
                      pl.BlockSpec((N_CHIPS - 1, tr, cc), lambda i, pr: (0, i, 0))],
            out_specs=out_spec),
        out_shape=jax.ShapeDtypeStruct((cc, rr) if transposed else (rr, cc), BF16))(place, pair_sums, parts)


def _chip_sum_direct(place, g, parts, axis, name, transposed):
    n_parts, rr, cc = parts.shape
    tr = _row_tile(rr, LANES) if transposed else _row_tile(rr, limit=1024)
    nb = rr // tr
    if axis == 0:
        g_map = lambda i, pr: (pr[1], pr[0] * nb + i, 0)
    else:
        g_map = lambda i, pr: (pr[1], i, pr[0])

    def body(pr, g_ref, p_ref, o_ref):
        acc = p_ref[0].astype(F32)
        for j in range(1, n_parts):
            acc = acc + p_ref[j].astype(F32)
        acc = acc + g_ref[...].astype(F32)
        o_ref[...] = (acc.T if transposed else acc).astype(BF16)

    out_spec = pl.BlockSpec((cc, tr), lambda i, pr: (0, i)) if transposed else pl.BlockSpec((tr, cc), lambda i, pr: (i, 0))
    return pl.pallas_call(
        body, name=name,
        grid_spec=pltpu.PrefetchScalarGridSpec(
            num_scalar_prefetch=1, grid=(nb,),
            in_specs=[pl.BlockSpec((None, tr, cc), g_map), pl.BlockSpec((n_parts, tr, cc), lambda i, pr: (0, i, 0))],
            out_specs=out_spec),
        out_shape=jax.ShapeDtypeStruct((cc, rr) if transposed else (rr, cc), BF16))(place, g, parts)


def _silu(v):
    return v / (1.0 + jnp.exp(-v))


def _ada_fwd(c_all, w_shard, b_shard):
    def body(c_ref, w_ref, b_ref, o_ref):
        o_ref[...] = jnp.dot(_silu(c_ref[...]), w_ref[...], precision=lax.Precision.HIGHEST,
                             preferred_element_type=F32) + b_ref[...]

    return pl.pallas_call(body, name="ada_fwd", out_shape=jax.ShapeDtypeStruct((c_all.shape[0], w_shard.shape[1]), F32),
                          compiler_params=pltpu.CompilerParams(vmem_limit_bytes=MM_VMEM_LIMIT))(c_all, w_shard, b_shard)


def _ada_bwd(c_all, dmod_cols):
    def body(c_ref, d_ref, o_ref):
        o_ref[...] = lax.dot_general(_silu(c_ref[...]), d_ref[...], (((0,), (0,)), ((), ())),
                                     precision=lax.Precision.HIGHEST, preferred_element_type=F32)

    return pl.pallas_call(body, name="ada_bwd", out_shape=jax.ShapeDtypeStruct((c_all.shape[1], dmod_cols.shape[1]), F32),
                          compiler_params=pltpu.CompilerParams(vmem_limit_bytes=MM_VMEM_LIMIT))(c_all, dmod_cols)


def _adamw_math(w, g, m, v):
    m = ADAM_B1 * m + (1.0 - ADAM_B1) * g
    v = ADAM_B2 * v + (1.0 - ADAM_B2) * (g * g)
    m_hat = m / (1.0 - ADAM_B1 ** ADAM_STEP)
    v_hat = v / (1.0 - ADAM_B2 ** ADAM_STEP)
    delta = -ADAM_LR * (m_hat / (jnp.sqrt(v_hat) + ADAM_EPS) + ADAM_WD * w)
    return delta, m, v


def _adamw(w, g, m, v, name):
    r, ccols = w.shape
    tr = max(d for d in range(8, ROW_TILE + 1, 8) if r % d == 0)
    spec = pl.BlockSpec((tr, ccols), lambda i: (i, 0))

    def body(w_ref, g_ref, m_ref, v_ref, d_ref, nm_ref, nv_ref):
        d_ref[...], nm_ref[...], nv_ref[...] = _adamw_math(w_ref[...], g_ref[...], m_ref[...], v_ref[...])

    return pl.pallas_call(body, name=name, grid=(r // tr,), in_specs=[spec] * 4, out_specs=[spec] * 3,
                          out_shape=[jax.ShapeDtypeStruct(w.shape, F32)] * 3,
                          compiler_params=pltpu.CompilerParams(vmem_limit_bytes=MM_VMEM_LIMIT))(w, g, m, v)


def _small_layout(sizes):
    offs, off = [], 0
    for n in sizes:
        offs.append(off)
        off += -(-n // LANES) * LANES
    total = -(-(off + LANES) // (8 * LANES)) * (8 * LANES)
    return offs, off, total


def _adamw_small(ws, g_all, ms, vs, offs, loss_off):
    n_p = len(ws)

    def device_sum(g_ref, off, width):
        blk = g_ref[:, off:off + width]
        acc = blk[0:1]
        for d in range(1, N_DEV):
            acc = acc + blk[d:d + 1]
        return acc

    def body(*refs):
        w_refs, m_refs, v_refs = refs[:n_p], refs[n_p:2 * n_p], refs[2 * n_p:3 * n_p]
        g_ref = refs[3 * n_p]
        outs = refs[3 * n_p + 1:]
        for i in range(n_p):
            n = w_refs[i].shape[1]
            g = device_sum(g_ref, offs[i], -(-n // LANES) * LANES)[:, :n]
            outs[i][...] = g
            outs[n_p + i][...], outs[2 * n_p + i][...], outs[3 * n_p + i][...] = _adamw_math(
                w_refs[i][...], g, m_refs[i][...], v_refs[i][...])
        outs[4 * n_p][...] = device_sum(g_ref, loss_off, LANES)

    res = pl.pallas_call(
        body, name="adamw_small",
        out_shape=[jax.ShapeDtypeStruct(a.shape, F32) for a in list(ws) * 4] + [jax.ShapeDtypeStruct((1, LANES), F32)],
    )(*ws, *ms, *vs, g_all)
    return res[:n_p], res[n_p:2 * n_p], res[2 * n_p:3 * n_p], res[3 * n_p:4 * n_p], res[4 * n_p]


def _adamw_halves(place, w, own, sib, m, v, axis, name, after):
    r, cc = w.shape
    if axis == 0:
        rows, gc = own.shape[0], own.shape[1]
        tr = _row_tile(rows)
        nb = rows // tr
        w_spec = pl.BlockSpec((tr, cc), lambda h, i, pr: (h * nb + i, 0))
        g_spec = pl.BlockSpec((tr, gc), lambda h, i, pr: (i, 0))
    else:
        tr = _row_tile(r)
        nb = r // tr
        gc = own.shape[1]
        w_spec = pl.BlockSpec((tr, gc), lambda h, i, pr: (i, h))
        g_spec = pl.BlockSpec((tr, gc), lambda h, i, pr: (i, 0))
    wc = w_spec.block_shape[1]

    def body(pr, w_ref, o_ref, s_ref, m_ref, v_ref, after_ref, g_ref, d_ref, nm_ref, nv_ref):
        g = jnp.where(pl.program_id(0) == pr[0], o_ref[...], s_ref[...]).astype(F32)[:, :wc]
        g_ref[...] = g
        d_ref[...], nm_ref[...], nv_ref[...] = _adamw_math(w_ref[...], g, m_ref[...], v_ref[...])

    return pl.pallas_call(
        body, name=name,
        grid_spec=pltpu.PrefetchScalarGridSpec(
            num_scalar_prefetch=1, grid=(2, nb),
            in_specs=[w_spec, g_spec, g_spec, w_spec, w_spec, ANY], out_specs=[w_spec] * 4),
        out_shape=[jax.ShapeDtypeStruct(w.shape, F32)] * 4,
        compiler_params=pltpu.CompilerParams(vmem_limit_bytes=MM_VMEM_LIMIT))(place, w, own, sib, m, v, after)


SMALL = ("b_ada", "norm_attn", "norm_ffn", "q_a_norm", "kv_a_norm", "q_norm", "k_nope_norm", "k_rope_norm",
         "out_norm_sb", "out_norm_mla")
WEIGHTS = ("w_ada", "b_ada", "norm_attn", "norm_ffn", "w_in", "q_a_norm", "w_q_up", "kv_a_norm", "w_kv_up",
           "q_norm", "k_nope_norm", "k_rope_norm", "out_norm_sb", "out_norm_mla", "w_out", "w_gate", "w_up",
           "w_down")


def kernel(x, c, positions, w_ada, b_ada, norm_attn, norm_ffn, w_in, q_a_norm, w_q_up, kv_a_norm, w_kv_up, q_norm, k_nope_norm, k_rope_norm, out_norm_sb, out_norm_mla, w_out, w_gate, w_up, w_down, loss_target, m_w_ada, m_b_ada, m_norm_attn, m_norm_ffn, m_w_in, m_q_a_norm, m_w_q_up, m_kv_a_norm, m_w_kv_up, m_q_norm, m_k_nope_norm, m_k_rope_norm, m_out_norm_sb, m_out_norm_mla, m_w_out, m_w_gate, m_w_up, m_w_down, v_w_ada, v_b_ada, v_norm_attn, v_norm_ffn, v_w_in, v_q_a_norm, v_w_q_up, v_kv_a_norm, v_w_kv_up, v_q_norm, v_k_nope_norm, v_k_rope_norm, v_out_norm_sb, v_out_norm_mla, v_w_out, v_w_gate, v_w_up, v_w_down):
    local = dict(locals())
    w = {n: local[n][0] for n in WEIGHTS}
    m = {n: local["m_" + n][0] for n in WEIGHTS}
    v = {n: local["v_" + n][0] for n in WEIGHTS}
    small = {n: w[n].reshape(1, -1) for n in SMALL}
    ix, iy, ic = _my_place()
    chip = 2 * ix + iy
    dev = 2 * chip + ic
    xs, target = x[0], loss_target[0]
    seq = xs.shape[0]

    c_all = _all_gather_small(c.reshape(8, LANES), "gather_c").reshape(N_DEV, D_MODEL)
    ada_cols = w["w_ada"].shape[1]
    b_cols = lax.dynamic_slice_in_dim(small["b_ada"], chip * ada_cols, ada_cols, axis=1)
    mod_cols = _ada_fwd(c_all, w["w_ada"], b_cols)
    mod_all = _all_gather_small(mod_cols, "gather_mod").reshape(N_CHIPS, 2, N_DEV, ada_cols)
    mod = lax.dynamic_index_in_dim(mod_all[:, 0], dev, axis=1, keepdims=False).reshape(1, N_MOD * D_MODEL)

    ff_pad = FF_SHARD_PAD - FF_SHARD
    pads = {"w_gate": ((0, 0), (0, ff_pad)), "w_up": ((0, 0), (0, ff_pad)), "w_down": ((0, ff_pad), (0, 0))}
    shards = {n: jnp.pad(w[n].astype(BF16), pads[n]) if n in pads else w[n].astype(BF16) for n in BIG}
    early, early_done = _gather_weights(EARLY, [shards[n] for n in EARLY], mod)
    gathered = dict(zip(EARLY, early))
    lands = [lax.dynamic_update_index_in_dim(lax.empty((N_CHIPS,) + shards[n].shape, BF16), shards[n], chip, 0)
             for n in LATE]
    late_gather = _exchange_start("gather_late_start", [shards[n] for n in LATE], lands, _late_gather_plan,
                                  3 * len(LATE), early_done, mod)
    mod = late_gather[4]

    half = MLA_ROPE // 2
    freqs = 1.0 / (ROPE_THETA ** (np.arange(half, dtype=np.float32) / half))
    zeros = np.zeros(LANES - MLA_ROPE, np.float32)
    freqs_row = jnp.asarray(np.concatenate([freqs, freqs, zeros]).astype(np.float32)[None])
    sign_row = jnp.asarray(np.concatenate([-np.ones(half), np.ones(half), zeros]).astype(np.float32)[None])
    cos, sin = _rope_tables(positions.reshape(seq, 1), freqs_row, sign_row)

    place = jnp.stack([ic, chip]).astype(jnp.int32)
    small_params = {n: small[n] for n in SMALL if n != "b_ada"}

    p1 = {**{n: gathered[n] for n in EARLY}, **small_params}
    mixed, mixing_vjp = jax.vjp(lambda x_, mod_, p_: _mixing_stage(x_, mod_, p_, cos, sin), xs, mod, p1)
    _, landed = _exchange_wait("gather_late_wait", late_gather, _late_gather_plan, mixed)
    p2 = {**dict(zip(LATE, landed)), **small_params}
    loss_part, ffn_vjp = jax.vjp(lambda x_, mixed_, mod_, p_: _ffn_stage(x_, mixed_, mod_, p_, target), xs, mixed, mod, p2)
    gx2, gmixed, gmod2, gp2 = ffn_vjp(jnp.ones((), F32))
    late_grads = [gp2[n] for n in LATE]
    late_plan = _direct_scatter_plan(LATE)
    late_scatter = _exchange_start(
        "grad_scatter_late_start", late_grads,
        [lax.empty((7,) + _half_shape(gr.shape[1:], HALF_AXIS[n]), BF16) for n, gr in zip(LATE, late_grads)],
        late_plan, 7 * len(LATE), gx2, gmixed)
    gx1, gmod1, gp1 = mixing_vjp(late_scatter[4])
    gx = gx1 + gx2
    gmod = gmod1 + gmod2
    gp = {n: gp1[n] + gp2[n] for n in small_params}

    sizes = [w[n].size for n in SMALL]
    offs, loss_off, n_small = _small_layout(sizes)
    pieces = []
    for n, size in zip(SMALL, sizes):
        pieces.append(gmod if n == "b_ada" else gp[n])
        if size % LANES:
            pieces.append(jnp.zeros((1, LANES - size % LANES), F32))
    pieces += [jnp.full((1, LANES), loss_part), jnp.zeros((1, n_small - loss_off - LANES), F32)]
    small_vec = jnp.concatenate(pieces, axis=1)
    small_all = _all_gather_small(small_vec.reshape(8, n_small // 8), "gather_small").reshape(N_DEV, n_small)

    g, delta, new_m, new_v = {}, {}, {}, {}

    def update(names, own, sib, after):
        for n, o, s in zip(names, own, sib):
            if n in TRANSPOSED_UPDATE:
                res = _adamw_halves(place, w[n].T, o, s, m[n].T, v[n].T, 1, "adamw_" + n, after)
                g[n], delta[n], new_m[n], new_v[n] = [r.T for r in res]
            else:
                g[n], delta[n], new_m[n], new_v[n] = _adamw_halves(place, w[n], o, s, m[n], v[n], HALF_AXIS[n],
                                                                   "adamw_" + n, after)

    late_grads, late_parts = _exchange_wait("grad_scatter_late_wait", late_scatter, late_plan, gx)
    own_late = [_chip_sum_direct(place, gr, pt, HALF_AXIS[n], "grad_chip_sum_" + n, n in TRANSPOSED_UPDATE)
                for n, gr, pt in zip(LATE, late_grads, late_parts)]
    sib_late = _sibling_join(own_late, "grad_sibling_join_late", small_all)
    early_grads = [gp1[n] for n in EARLY]
    theirs = _pair_exchange(EARLY, early_grads, "grad_pair_exchange_early")
    early_sums = [_pair_sum(place, gr, th, HALF_AXIS[n], "grad_pair_sum_" + n)
                  for n, gr, th in zip(EARLY, early_grads, theirs)]
    early_scatter = _exchange_start(
        "grad_scatter_early_start", early_sums,
        [lax.empty((N_CHIPS - 1,) + s.shape[1:], BF16) for s in early_sums], _late_scatter_plan, 3 * len(EARLY),
        sib_late[0], small_all)
    small_all = early_scatter[4]
    update(LATE, own_late, sib_late, small_all)

    *small_out, loss_row = _adamw_small([small[n] for n in SMALL], small_all, [m[n].reshape(1, -1) for n in SMALL],
                                        [v[n].reshape(1, -1) for n in SMALL], offs, loss_off)
    loss = loss_row[0, 0]
    for d, outs_d in zip((g, delta, new_m, new_v), small_out):
        d.update({n: o.reshape(w[n].shape) for n, o in zip(SMALL, outs_d)})

    dmod_all = small_all[:, :N_MOD * D_MODEL]
    g["w_ada"] = _ada_bwd(c_all, lax.dynamic_slice_in_dim(dmod_all, chip * ada_cols, ada_cols, axis=1))
    delta["w_ada"], new_m["w_ada"], new_v["w_ada"] = _adamw(w["w_ada"], g["w_ada"], m["w_ada"], v["w_ada"], "adamw_w_ada")

    early_sums, early_parts = _exchange_wait("grad_scatter_early_wait", early_scatter, _late_scatter_plan,
                                             delta["w_ada"])
    own_early = [_chip_sum(place, ps, pt, "grad_chip_sum_" + n, n in TRANSPOSED_UPDATE)
                 for n, ps, pt in zip(EARLY, early_sums, early_parts)]
    sib_early = _sibling_join(own_early, "grad_sibling_join_early", delta["w_ada"])
    update(EARLY, own_early, sib_early, sib_early[0])

    def outs(d):
        return [d[n][None] for n in WEIGHTS]

    return (loss, gx[None], *outs(g), *outs(delta), *outs(new_m), *outs(new_v))
```

```python
import numpy as np
import jax
import jax.numpy as jnp
from jax import lax
from jax.experimental import pallas as pl
from jax.experimental.pallas import tpu as pltpu

F32 = jnp.float32
BF16 = jnp.bfloat16
MESH = pl.DeviceIdType.MESH
ANY = pl.BlockSpec(memory_space=pl.ANY)

D_MODEL = 1024
SB_HEADS = 8
SB_HEAD_DIM = 64
SB_WIDTH = 512
MLA_HEADS = 4
MLA_NOPE = 128
MLA_ROPE = 64
MLA_QK = 192
MLA_V = 128
MLA_Q_RANK = 384
MLA_KV_RANK = 256
D_FF = 2816
N_MOD = 6
ROPE_THETA = 10000.0
EPS = 1e-6
LANES = 128

ADAM_LR = 0.001
ADAM_B1 = 0.9
ADAM_B2 = 0.999
ADAM_EPS = 1e-08
ADAM_WD = 0.01
ADAM_STEP = 10

N_CHIPS = 4
N_DEV = 8
ROW_TILE = 256
MM_ROW_TILE = 512
ATT_BLK = 256
MM_VMEM_LIMIT = 56 * 1024 * 1024
FF_SHARD = D_FF // N_CHIPS
FF_SHARD_PAD = 768


def _mm(a, b, mode, name, tm, tn, out_dtype=F32):
    if mode == "nn":
        (m, k), n = a.shape, b.shape[1]
        a_spec = pl.BlockSpec((tm, k), lambda j, i: (i, 0))
        b_spec = pl.BlockSpec((k, tn), lambda j, i: (0, j))
        dims = (((1,), (0,)), ((), ()))
    elif mode == "nt":
        (m, k), n = a.shape, b.shape[0]
        a_spec = pl.BlockSpec((tm, k), lambda j, i: (i, 0))
        b_spec = pl.BlockSpec((tn, k), lambda j, i: (j, 0))
        dims = (((1,), (1,)), ((), ()))
    else:
        (k, m), n = a.shape, b.shape[1]
        a_spec = pl.BlockSpec((k, tm), lambda j, i: (0, i))
        b_spec = pl.BlockSpec((k, tn), lambda j, i: (0, j))
        dims = (((0,), (0,)), ((), ()))
    assert m % tm == 0 and n % tn == 0, (name, m, n, tm, tn)

    def body(a_ref, b_ref, o_ref):
        o_ref[...] = lax.dot_general(a_ref[...].astype(BF16), b_ref[...].astype(BF16), dims,
                                     preferred_element_type=F32).astype(out_dtype)

    return pl.pallas_call(
        body, name=name, grid=(n // tn, m // tm),
        in_specs=[a_spec, b_spec],
        out_specs=pl.BlockSpec((tm, tn), lambda j, i: (i, j)),
        out_shape=jax.ShapeDtypeStruct((m, n), out_dtype),
        compiler_params=pltpu.CompilerParams(dimension_semantics=("arbitrary", "arbitrary"),
                                             vmem_limit_bytes=MM_VMEM_LIMIT),
    )(a, b)


def _make_linear(name, tk_w, tn_w):
    @jax.custom_vjp
    def op(a, w):
        return _mm(a, w, "nn", name + "_fwd", MM_ROW_TILE, w.shape[1])

    def fwd(a, w):
        return op(a, w), (a, w)

    def bwd(res, dy):
        a, w = res
        da = _mm(dy, w, "nt", name + "_dx", MM_ROW_TILE, w.shape[0])
        dw = _mm(a, dy, "tn", name + "_dw", tk_w, tn_w, out_dtype=BF16)
        return da, dw

    op.defvjp(fwd, bwd)
    return op


def _make_linear_split(name, widths, tk_w):
    starts = [sum(widths[:g]) for g in range(len(widths))]

    def call_fwd(a, w):
        t, k = a.shape
        n = w.shape[1]

        def body(a_ref, w_ref, *o_refs):
            y = jnp.dot(a_ref[...].astype(BF16), w_ref[...], preferred_element_type=F32)
            for o_ref, s0, wd in zip(o_refs, starts, widths):
                o_ref[...] = y[:, s0:s0 + wd]

        return pl.pallas_call(
            body, name=name + "_fwd", grid=(t // MM_ROW_TILE,),
            in_specs=[pl.BlockSpec((MM_ROW_TILE, k), lambda i: (i, 0)), pl.BlockSpec((k, n), lambda i: (0, 0))],
            out_specs=[pl.BlockSpec((MM_ROW_TILE, wd), lambda i: (i, 0)) for wd in widths],
            out_shape=[jax.ShapeDtypeStruct((t, wd), F32) for wd in widths],
            compiler_params=pltpu.CompilerParams(dimension_semantics=("arbitrary",), vmem_limit_bytes=MM_VMEM_LIMIT),
        )(a, w)

    def call_dx(dys, w):
        t = dys[0].shape[0]
        k, n = w.shape

        def body(*refs):
            dy_refs, w_ref, o_ref = refs[:-2], refs[-2], refs[-1]
            acc = jnp.zeros((MM_ROW_TILE, k), F32)
            for dy_ref, s0, wd in zip(dy_refs, starts, widths):
                acc = acc + _nt(dy_ref[...].astype(BF16), w_ref[:, s0:s0 + wd])
            o_ref[...] = acc

        return pl.pallas_call(
            body, name=name + "_dx", grid=(t // MM_ROW_TILE,),
            in_specs=[pl.BlockSpec((MM_ROW_TILE, wd), lambda i: (i, 0)) for wd in widths]
            + [pl.BlockSpec((k, n), lambda i: (0, 0))],
            out_specs=pl.BlockSpec((MM_ROW_TILE, k), lambda i: (i, 0)),
            out_shape=jax.ShapeDtypeStruct((t, k), F32),
            compiler_params=pltpu.CompilerParams(dimension_semantics=("arbitrary",), vmem_limit_bytes=MM_VMEM_LIMIT),
        )(*dys, w)

    def call_dw(a, dys, w):
        t, k = a.shape
        n = w.shape[1]

        def body(a_ref, *refs):
            dy_refs, o_ref = refs[:-1], refs[-1]
            ab = a_ref[...].astype(BF16)
            for dy_ref, s0, wd in zip(dy_refs, starts, widths):
                o_ref[:, s0:s0 + wd] = _tn(ab, dy_ref[...].astype(BF16)).astype(BF16)
            if starts[-1] + widths[-1] < n:
                o_ref[:, starts[-1] + widths[-1]:] = jnp.zeros((tk_w, n - starts[-1] - widths[-1]), BF16)

        return pl.pallas_call(
            body, name=name + "_dw", grid=(k // tk_w,),
            in_specs=[pl.BlockSpec((t, tk_w), lambda i: (0, i))]
            + [pl.BlockSpec((t, wd), lambda i: (0, 0)) for wd in widths],
            out_specs=pl.BlockSpec((tk_w, n), lambda i: (i, 0)),
            out_shape=jax.ShapeDtypeStruct((k, n), BF16),
            compiler_params=pltpu.CompilerParams(dimension_semantics=("arbitrary",), vmem_limit_bytes=MM_VMEM_LIMIT),
        )(a, *dys)

    @jax.custom_vjp
    def op(a, w):
        return tuple(call_fwd(a, w))

    def fwd(a, w):
        return op(a, w), (a, w)

    def bwd(res, dys):
        a, w = res
        return call_dx(dys, w), call_dw(a, dys, w)

    op.defvjp(fwd, bwd)
    return op


def _row_spec(arr, tb):
    return pl.BlockSpec((tb, arr.shape[1]), lambda i: (i, 0))


def _full_spec(arr):
    return pl.BlockSpec(arr.shape, lambda i: (0, 0))


def _make_rowwise(name, f, n_rows, n_params, out_cols, diff_rows, out_dtypes=None, grad_dtypes=None):
    n_out = len(out_cols)
    out_dtypes = out_dtypes or [F32] * n_out
    grad_dtypes = grad_dtypes or [F32] * sum(diff_rows)

    def call_fwd(rows, params):
        t = rows[0].shape[0]

        def body(*refs):
            ins = [r[...] for r in refs[:n_rows + n_params]]
            outs = f(*ins)
            for o_ref, o in zip(refs[n_rows + n_params:], outs):
                o_ref[...] = o.astype(o_ref.dtype)

        return pl.pallas_call(
            body, name=name + "_fwd", grid=(t // ROW_TILE,),
            in_specs=[_row_spec(a, ROW_TILE) for a in rows] + [_full_spec(p) for p in params],
            out_specs=[pl.BlockSpec((ROW_TILE, n), lambda i: (i, 0)) for n in out_cols],
            out_shape=[jax.ShapeDtypeStruct((t, n), dt) for n, dt in zip(out_cols, out_dtypes)],
            compiler_params=pltpu.CompilerParams(dimension_semantics=("arbitrary",),
                                                 vmem_limit_bytes=MM_VMEM_LIMIT),
        )(*rows, *params)

    def call_bwd(rows, params, cts):
        t = rows[0].shape[0]
        d_rows = [a for a, d in zip(rows, diff_rows) if d]
        n_in = n_rows + n_params + n_out

        def body(*refs):
            ins = [r[...] for r in refs[:n_rows + n_params]]
            ct = tuple(r[...].astype(F32) for r in refs[n_rows + n_params:n_in])
            _, vjp = jax.vjp(f, *ins)
            grads = vjp(ct)
            out_refs = refs[n_in:]
            g_rows = [g for g, d in zip(grads[:n_rows], diff_rows) if d]
            for o_ref, g in zip(out_refs[:len(g_rows)], g_rows):
                o_ref[...] = g.astype(o_ref.dtype)
            p_refs = out_refs[len(g_rows):]

            if p_refs:
                @pl.when(pl.program_id(0) == 0)
                def _():
                    for p_ref in p_refs:
                        p_ref[...] = jnp.zeros_like(p_ref)

                for p_ref, g in zip(p_refs, grads[n_rows:]):
                    p_ref[...] += g

        return pl.pallas_call(
            body, name=name + "_bwd", grid=(t // ROW_TILE,),
            in_specs=[_row_spec(a, ROW_TILE) for a in rows] + [_full_spec(p) for p in params]
            + [_row_spec(c, ROW_TILE) for c in cts],
            out_specs=[_row_spec(a, ROW_TILE) for a in d_rows] + [_full_spec(p) for p in params],
            out_shape=[jax.ShapeDtypeStruct(a.shape, dt) for a, dt in zip(d_rows, grad_dtypes)]
            + [jax.ShapeDtypeStruct(p.shape, F32) for p in params],
            compiler_params=pltpu.CompilerParams(dimension_semantics=("arbitrary",),
                                                 vmem_limit_bytes=MM_VMEM_LIMIT),
        )(*rows, *params, *cts)

    @jax.custom_vjp
    def op(*args):
        return tuple(call_fwd(args[:n_rows], args[n_rows:]))

    def fwd(*args):
        return op(*args), args

    def bwd(args, cts):
        rows, params = args[:n_rows], args[n_rows:]
        outs = call_bwd(rows, params, cts)
        it = iter(outs)
        g_rows = [next(it) if d else jnp.zeros_like(a) for a, d in zip(rows, diff_rows)]
        return tuple(g_rows) + tuple(it)

    op.defvjp(fwd, bwd)
    return op


def _rms(x, g, n):
    return x * lax.rsqrt(jnp.sum(x * x, axis=-1, keepdims=True) * (1.0 / n) + EPS) * g


def _f_pre_attn(x, g, scale, shift):
    return (_rms(x, g, D_MODEL) * (1.0 + scale) + shift,)


def _f_mla_a(cq, ckv, gq, gkv):
    return _rms(cq, gq, MLA_Q_RANK), _rms(ckv, gkv, MLA_KV_RANK)


@jax.custom_vjp
def _split_lanes(x):
    return tuple(x[:, i * LANES:(i + 1) * LANES] for i in range(x.shape[1] // LANES))


def _split_lanes_fwd(x):
    return _split_lanes(x), None


def _split_lanes_bwd(_, cts):
    return (jnp.concatenate(cts, axis=1),)


_split_lanes.defvjp(_split_lanes_fwd, _split_lanes_bwd)


def _f_mla_b(qall, kn_all, kr, kr_sw, cos, sin, gqn, gqr, gqr_sw, gkn, gkr, gkr_sw):
    q = _split_lanes(qall)
    kn = _split_lanes(kn_all)
    qn_o, qr_o, kn_o = [], [], []
    for h in range(MLA_HEADS):
        qn, qr, qs = q[h], q[MLA_HEADS + h], q[2 * MLA_HEADS + h]
        ss = jnp.sum(qn * qn, axis=-1, keepdims=True) + jnp.sum(qr * qr, axis=-1, keepdims=True)
        rs = lax.rsqrt(ss * (1.0 / MLA_QK) + EPS)
        qn_o.append(qn * rs * gqn)
        qr_o.append((qr * rs * gqr) * cos + (qs * rs * gqr_sw) * sin)
        kn_o.append(_rms(kn[h], gkn, MLA_NOPE))
    rs = lax.rsqrt(jnp.sum(kr * kr, axis=-1, keepdims=True) * (1.0 / MLA_ROPE) + EPS)
    kr_o = (kr * rs * gkr) * cos + (kr_sw * rs * gkr_sw) * sin
    return (jnp.concatenate(qn_o, axis=1), jnp.concatenate(qr_o, axis=1), jnp.concatenate(kn_o, axis=1), kr_o)


def _f_post_attn(o_sb, o_mla, g_sb, g_mla):
    return (jnp.concatenate([_rms(o_sb, g_sb, SB_WIDTH), _rms(o_mla, g_mla, SB_WIDTH)], axis=1),)


def _f_pre_ffn(x, attn, gate, g, scale, shift):
    x2 = x + gate * attn
    return x2, _rms(x2, g, D_MODEL) * (1.0 + scale) + shift


def _f_swiglu(gt, up):
    return (gt / (1.0 + jnp.exp(-gt)) * up,)


def _f_loss(x2, ffn, target, gate):
    err = x2 + gate * ffn - target
    return (jnp.sum(err * err, axis=-1, keepdims=True) * (1.0 / D_MODEL),)


def _rope_tables(pos_col, freqs, sign):
    t = pos_col.shape[0]

    def body(p_ref, f_ref, s_ref, cos_ref, sin_ref):
        ang = p_ref[...].astype(F32) * f_ref[...]
        live = jnp.abs(s_ref[...])
        cos_ref[...] = jnp.cos(ang) * live
        sin_ref[...] = jnp.sin(ang) * s_ref[...]

    return pl.pallas_call(
        body, name="rope_tables", grid=(t // ROW_TILE,),
        in_specs=[pl.BlockSpec((ROW_TILE, 1), lambda i: (i, 0)), _full_spec(freqs), _full_spec(sign)],
        out_specs=[pl.BlockSpec((ROW_TILE, LANES), lambda i: (i, 0))] * 2,
        out_shape=[jax.ShapeDtypeStruct((t, LANES), F32)] * 2,
    )(pos_col, freqs, sign)


def _hi_lo_dot(x, tri):
    hi = x.astype(BF16)
    lo = (x - hi.astype(F32)).astype(BF16)
    return (jnp.dot(hi, tri, preferred_element_type=F32) + jnp.dot(lo, tri, preferred_element_type=F32))


def _tri(cmp):
    r = lax.broadcasted_iota(jnp.int32, (ATT_BLK, ATT_BLK), 0)
    c = lax.broadcasted_iota(jnp.int32, (ATT_BLK, ATT_BLK), 1)
    return cmp(r, c).astype(BF16)


def _nt(a, b):
    return lax.dot_general(a, b, (((1,), (1,)), ((), ())), preferred_element_type=F32)


def _tn(a, b):
    return lax.dot_general(a, b, (((0,), (0,)), ((), ())), preferred_element_type=F32)


def _sb_logs(z):
    lb = jnp.minimum(z, 0.0) - jnp.log(1.0 + jnp.exp(-jnp.abs(z)))
    return lb, lb - z


def _sb_fwd(q, k, v):
    t = q.shape[0]
    nq = t // ATT_BLK
    scale = SB_HEAD_DIM ** -0.5

    def body(q_ref, k_ref, v_ref, o_ref, tot_ref):
        qi = pl.program_id(1)
        lane = lax.broadcasted_iota(jnp.int32, (ATT_BLK, LANES), 1)
        tri = _tri(lambda r, c: r > c)
        qv = q_ref[...] * scale
        heads = [(lane // SB_HEAD_DIM) == hh for hh in range(2)]
        qms = [jnp.where(mine, qv, 0.0).astype(BF16) for mine in heads]

        def blocks(kbs, carry, diagonal):
            acc = carry[0]
            nb = len(kbs)
            chains = [(b, hh) for b in range(nb) for hh in range(2)]
            offs = [pl.multiple_of(kb * ATT_BLK, ATT_BLK) for kb in kbs]
            kks = [k_ref[pl.ds(off, ATT_BLK), :].astype(BF16) for off in offs]
            v_blks = [v_ref[pl.ds(off, ATT_BLK), :] for off in offs]
            if any(diagonal):
                valid = (lax.broadcasted_iota(jnp.int32, (ATT_BLK, ATT_BLK), 1)
                         < lax.broadcasted_iota(jnp.int32, (ATT_BLK, ATT_BLK), 0))
            zs = {ch: _nt(qms[ch[1]], kks[ch[0]]) for ch in chains}
            vvs = {(b, hh): jnp.where(heads[hh], v_blks[b], 0.0).astype(BF16) for b, hh in chains}
            logs = {ch: _sb_logs(zs[ch]) for ch in chains}
            l1ms = {ch: jnp.where(valid, logs[ch][1], 0.0) if diagonal[ch[0]] else logs[ch][1] for ch in chains}
            run = {(0, hh): carry[1 + hh] for hh in range(2)}
            for b, hh in chains:
                run[(b + 1, hh)] = run[(b, hh)] + jnp.sum(l1ms[(b, hh)], axis=-1, keepdims=True)
            afters = {ch: _hi_lo_dot(l1ms[ch], tri) for ch in chains}
            ws = {ch: jnp.exp(logs[ch][0] + (afters[ch] + run[ch])) for ch in chains}
            ws = {ch: jnp.where(valid, ws[ch], 0.0) if diagonal[ch[0]] else ws[ch] for ch in chains}
            for ch in chains:
                acc = acc + jnp.dot(ws[ch].astype(BF16), vvs[ch], preferred_element_type=F32)
            return (acc, run[(nb, 0)], run[(nb, 1)])

        zero = jnp.zeros((ATT_BLK, 1), F32)
        init = (jnp.zeros((ATT_BLK, LANES), F32), zero, zero)
        carry = lax.cond(qi % 2 == 1, lambda cr: blocks([qi, qi - 1], cr, (True, False)),
                         lambda cr: blocks([qi], cr, (True,)), init)
        top = qi - 1 - qi % 2
        carry = lax.fori_loop(0, qi // 2, lambda pr, cr: blocks([top - 2 * pr, top - 1 - 2 * pr], cr, (False, False)),
                              carry)
        o_ref[...] = carry[0]
        for hh in range(2):
            tot_ref[:, hh * LANES:(hh + 1) * LANES] = jnp.broadcast_to(carry[1 + hh], (ATT_BLK, LANES))

    return pl.pallas_call(
        body, name="sb_attn_fwd", grid=(SB_HEADS // 2, nq),
        in_specs=[pl.BlockSpec((ATT_BLK, LANES), lambda p, i: (i, p)),
                  pl.BlockSpec((t, LANES), lambda p, i: (0, p)),
                  pl.BlockSpec((t, LANES), lambda p, i: (0, p))],
        out_specs=[pl.BlockSpec((ATT_BLK, LANES), lambda p, i: (i, p)),
                   pl.BlockSpec((ATT_BLK, 2 * LANES), lambda p, i: (i, p))],
        out_shape=[jax.ShapeDtypeStruct((t, SB_WIDTH), F32), jax.ShapeDtypeStruct((t, SB_HEADS * LANES), F32)],
        compiler_params=pltpu.CompilerParams(dimension_semantics=("arbitrary", "arbitrary")),
    )(q, k, v)


def _sb_bwd(q, k, v, tot, do):
    t = q.shape[0]
    nq = t // ATT_BLK
    scale = SB_HEAD_DIM ** -0.5

    def body(q_ref, k_ref, v_ref, tot_ref, do_ref, dq_ref, dk_ref, dv_ref):
        qi = pl.program_id(1)

        @pl.when(qi == 0)
        def _():
            dk_ref[...] = jnp.zeros_like(dk_ref)
            dv_ref[...] = jnp.zeros_like(dv_ref)

        lane = lax.broadcasted_iota(jnp.int32, (ATT_BLK, LANES), 1)
        tri_incl = _tri(lambda r, c: r <= c)
        tri_lt = _tri(lambda r, c: r < c)
        qv = q_ref[...] * scale
        dov = do_ref[...]
        heads = [(lane // SB_HEAD_DIM) == hh for hh in range(2)]
        qms = [jnp.where(mine, qv, 0.0).astype(BF16) for mine in heads]
        doms = [jnp.where(mine, dov, 0.0).astype(BF16) for mine in heads]
        tots = [tot_ref[:, hh * LANES:hh * LANES + 1] for hh in range(2)]

        def blocks(kbs, carry, diagonal):
            dq = carry[0]
            nb = len(kbs)
            chains = [(b, hh) for b in range(nb) for hh in range(2)]
            offs = [pl.multiple_of(kb * ATT_BLK, ATT_BLK) for kb in kbs]
            k_blks = [k_ref[pl.ds(off, ATT_BLK), :] for off in offs]
            vvs = [v_ref[pl.ds(off, ATT_BLK), :].astype(BF16) for off in offs]
            if any(diagonal):
                valid = (lax.broadcasted_iota(jnp.int32, (ATT_BLK, ATT_BLK), 1)
                         < lax.broadcasted_iota(jnp.int32, (ATT_BLK, ATT_BLK), 0))
            kks = {(b, hh): jnp.where(heads[hh], k_blks[b], 0.0).astype(BF16) for b, hh in chains}
            zs = {ch: _nt(qms[ch[1]], kks[ch]) for ch in chains}
            dws = {ch: _nt(doms[ch[1]], vvs[ch[0]]) for ch in chains}
            logs = {ch: _sb_logs(zs[ch]) for ch in chains}
            lbs = {ch: logs[ch][0] for ch in chains}
            l1m_all = {ch: logs[ch][1] for ch in chains}
            l1ms = {ch: jnp.where(valid, l1m_all[ch], 0.0) if diagonal[ch[0]] else l1m_all[ch] for ch in chains}
            pre, c_de = {}, {}
            for hh in range(2):
                pre[(0, hh)], c_de[(0, hh)] = carry[1 + 2 * hh], carry[2 + 2 * hh]
            for b, hh in chains:
                pre[(b + 1, hh)] = pre[(b, hh)] + jnp.sum(l1ms[(b, hh)], axis=-1, keepdims=True)
            prefix = {ch: _hi_lo_dot(l1ms[ch], tri_incl) for ch in chains}
            ws = {ch: jnp.exp(lbs[ch] + (tots[ch[1]] - (prefix[ch] + pre[ch]))) for ch in chains}
            ws = {ch: jnp.where(valid, ws[ch], 0.0) if diagonal[ch[0]] else ws[ch] for ch in chains}
            d_es = {ch: ws[ch] * dws[ch] for ch in chains}
            for b, hh in chains:
                c_de[(b + 1, hh)] = c_de[(b, hh)] + jnp.sum(d_es[(b, hh)], axis=-1, keepdims=True)
            dvs = [_tn(ws[(b, 0)].astype(BF16), doms[0]) + _tn(ws[(b, 1)].astype(BF16), doms[1]) for b in range(nb)]
            dl1ms = {ch: jnp.dot(d_es[ch].astype(BF16), tri_lt, preferred_element_type=F32) + c_de[ch] for ch in chains}
            dzs = {ch: d_es[ch] * jnp.exp(l1m_all[ch]) - dl1ms[ch] * jnp.exp(lbs[ch]) for ch in chains}
            dzs = {ch: jnp.where(valid, dzs[ch], 0.0) if diagonal[ch[0]] else dzs[ch] for ch in chains}
            dzs = {ch: dzs[ch].astype(BF16) for ch in chains}
            for ch in chains:
                dq = dq + jnp.dot(dzs[ch], kks[ch], preferred_element_type=F32)
            for b in range(nb):
                dk_ref[pl.ds(offs[b], ATT_BLK), :] += _tn(dzs[(b, 0)], qms[0]) + _tn(dzs[(b, 1)], qms[1])
                dv_ref[pl.ds(offs[b], ATT_BLK), :] += dvs[b]
            return (dq, pre[(nb, 0)], c_de[(nb, 0)], pre[(nb, 1)], c_de[(nb, 1)])

        zero = jnp.zeros((ATT_BLK, 1), F32)
        carry = lax.fori_loop(0, qi // 2, lambda pr, cr: blocks([2 * pr, 2 * pr + 1], cr, (False, False)),
                              (jnp.zeros((ATT_BLK, LANES), F32), zero, zero, zero, zero))
        carry = lax.cond(qi % 2 == 1, lambda cr: blocks([qi - 1, qi], cr, (False, True)),
                         lambda cr: blocks([qi], cr, (True,)), carry)
        dq_ref[...] = carry[0] * scale

    return pl.pallas_call(
        body, name="sb_attn_bwd", grid=(SB_HEADS // 2, nq),
        in_specs=[pl.BlockSpec((ATT_BLK, LANES), lambda p, i: (i, p)),
                  pl.BlockSpec((t, LANES), lambda p, i: (0, p)),
                  pl.BlockSpec((t, LANES), lambda p, i: (0, p)),
                  pl.BlockSpec((ATT_BLK, 2 * LANES), lambda p, i: (i, p)),
                  pl.BlockSpec((ATT_BLK, LANES), lambda p, i: (i, p))],
        out_specs=[pl.BlockSpec((ATT_BLK, LANES), lambda p, i: (i, p)),
                   pl.BlockSpec((t, LANES), lambda p, i: (0, p)),
                   pl.BlockSpec((t, LANES), lambda p, i: (0, p))],
        out_shape=[jax.ShapeDtypeStruct((t, SB_WIDTH), F32)] * 3,
        compiler_params=pltpu.CompilerParams(dimension_semantics=("arbitrary", "arbitrary")),
    )(q, k, v, tot, do)


@jax.custom_vjp
def _sb_attention(q, k, v):
    return _sb_fwd(q, k, v)[0]


def _sb_attention_fwd(q, k, v):
    o, tot = _sb_fwd(q, k, v)
    return o, (q, k, v, tot)


def _sb_attention_bwd(res, do):
    return tuple(_sb_bwd(*res, do))


_sb_attention.defvjp(_sb_attention_fwd, _sb_attention_bwd)


def _mla_fwd(qn, qr, kn, kr, v):
    t = qn.shape[0]
    nq = t // ATT_BLK
    scale = MLA_QK ** -0.5

    def body(qn_ref, qr_ref, kn_ref, kr_ref, v_ref, o_ref, lse_ref):
        qi = pl.program_id(1)
        lanes = [slice(hh * LANES, (hh + 1) * LANES) for hh in range(2)]
        qnb = [qn_ref[:, sl].astype(BF16) for sl in lanes]
        qrb = [qr_ref[:, sl].astype(BF16) for sl in lanes]

        def blocks(kbs, carry, diagonal):
            nb = len(kbs)
            chains = [(b, hh) for b in range(nb) for hh in range(2)]
            offs = [pl.multiple_of(kb * ATT_BLK, ATT_BLK) for kb in kbs]
            krbs = [kr_ref[pl.ds(off, ATT_BLK), :].astype(BF16) for off in offs]
            accs, ms, ls = [carry[0], carry[3]], [carry[1], carry[4]], [carry[2], carry[5]]
            ss = {(b, hh): (_nt(qnb[hh], kn_ref[pl.ds(offs[b], ATT_BLK), lanes[hh]].astype(BF16))
                            + _nt(qrb[hh], krbs[b])) * scale for b, hh in chains}
            if any(diagonal):
                causal = (lax.broadcasted_iota(jnp.int32, (ATT_BLK, ATT_BLK), 1)
                          <= lax.broadcasted_iota(jnp.int32, (ATT_BLK, ATT_BLK), 0))
                ss = {ch: jnp.where(causal, ss[ch], -jnp.inf) if diagonal[ch[0]] else ss[ch] for ch in chains}
            m_new = list(ms)
            for b, hh in chains:
                m_new[hh] = jnp.maximum(m_new[hh], jnp.max(ss[(b, hh)], axis=-1, keepdims=True))
            ps = {(b, hh): jnp.exp(ss[(b, hh)] - m_new[hh]) for b, hh in chains}
            alphas = [jnp.exp(ms[hh] - m_new[hh]) for hh in range(2)]
            pvs = {(b, hh): jnp.dot(ps[(b, hh)].astype(BF16), v_ref[pl.ds(offs[b], ATT_BLK), lanes[hh]].astype(BF16),
                                    preferred_element_type=F32) for b, hh in chains}
            out = []
            for hh in range(2):
                acc, l = accs[hh] * alphas[hh], ls[hh] * alphas[hh]
                for b in range(nb):
                    acc, l = acc + pvs[(b, hh)], l + jnp.sum(ps[(b, hh)], axis=-1, keepdims=True)
                out += [acc, m_new[hh], l]
            return tuple(out)

        init = (jnp.zeros((ATT_BLK, LANES), F32), jnp.full((ATT_BLK, 1), -jnp.inf, F32), jnp.zeros((ATT_BLK, 1), F32))
        carry = lax.cond(qi % 2 == 1, lambda cr: blocks([qi, qi - 1], cr, (True, False)),
                         lambda cr: blocks([qi], cr, (True,)), init + init)
        carry = lax.fori_loop(0, qi // 2, lambda pr, cr: blocks([2 * pr, 2 * pr + 1], cr, (False, False)), carry)
        for hh in range(2):
            acc, m, l = carry[3 * hh:3 * hh + 3]
            o_ref[:, lanes[hh]] = acc / l
            lse_ref[:, lanes[hh]] = jnp.broadcast_to(m + jnp.log(l), (ATT_BLK, LANES))

    blk = pl.BlockSpec((ATT_BLK, 2 * LANES), lambda p, i: (i, p))
    full = pl.BlockSpec((t, 2 * LANES), lambda p, i: (0, p))
    return pl.pallas_call(
        body, name="mla_attn_fwd", grid=(MLA_HEADS // 2, nq),
        in_specs=[blk, blk, full, pl.BlockSpec((t, LANES), lambda p, i: (0, 0)), full],
        out_specs=[blk, blk],
        out_shape=[jax.ShapeDtypeStruct((t, MLA_HEADS * LANES), F32)] * 2,
        compiler_params=pltpu.CompilerParams(dimension_semantics=("arbitrary", "arbitrary")),
    )(qn, qr, kn, kr, v)


def _mla_bwd(qn, qr, kn, kr, v, o, lse, do):
    t = qn.shape[0]
    nq = t // ATT_BLK
    scale = MLA_QK ** -0.5

    def body(qn_ref, qr_ref, kn_ref, kr_ref, v_ref, o_ref, lse_ref, do_ref,
             dqn_ref, dqr_ref, dkn_ref, dkr_ref, dv_ref):
        pair = pl.program_id(0)
        qi = pl.program_id(1)

        @pl.when(qi == 0)
        def _():
            dkn_ref[...] = jnp.zeros_like(dkn_ref)
            dv_ref[...] = jnp.zeros_like(dv_ref)

        @pl.when((qi == 0) & (pair == 0))
        def _():
            dkr_ref[...] = jnp.zeros_like(dkr_ref)

        lanes = [slice(hh * LANES, (hh + 1) * LANES) for hh in range(2)]
        qnb = [qn_ref[:, sl].astype(BF16) for sl in lanes]
        qrb = [qr_ref[:, sl].astype(BF16) for sl in lanes]
        dob = [do_ref[:, sl].astype(BF16) for sl in lanes]
        delta = [jnp.sum(do_ref[:, sl] * o_ref[:, sl], axis=-1, keepdims=True) for sl in lanes]
        lse_v = [lse_ref[:, hh * LANES:hh * LANES + 1] for hh in range(2)]

        def blocks(kbs, carry, diagonal):
            nb = len(kbs)
            chains = [(b, hh) for b in range(nb) for hh in range(2)]
            offs = [pl.multiple_of(kb * ATT_BLK, ATT_BLK) for kb in kbs]
            krbs = [kr_ref[pl.ds(off, ATT_BLK), :].astype(BF16) for off in offs]
            knb = {(b, hh): kn_ref[pl.ds(offs[b], ATT_BLK), lanes[hh]].astype(BF16) for b, hh in chains}
            vb = {(b, hh): v_ref[pl.ds(offs[b], ATT_BLK), lanes[hh]].astype(BF16) for b, hh in chains}
            ss = {(b, hh): _nt(qnb[hh], knb[(b, hh)]) + _nt(qrb[hh], krbs[b]) for b, hh in chains}
            dps = {(b, hh): _nt(dob[hh], vb[(b, hh)]) for b, hh in chains}
            ps = {(b, hh): jnp.exp(ss[(b, hh)] * scale - lse_v[hh]) for b, hh in chains}
            if any(diagonal):
                causal = (lax.broadcasted_iota(jnp.int32, (ATT_BLK, ATT_BLK), 1)
                          <= lax.broadcasted_iota(jnp.int32, (ATT_BLK, ATT_BLK), 0))
                ps = {ch: jnp.where(causal, ps[ch], 0.0) if diagonal[ch[0]] else ps[ch] for ch in chains}
            dss = {(b, hh): (ps[(b, hh)] * (dps[(b, hh)] - delta[hh]) * scale).astype(BF16) for b, hh in chains}
            for b, hh in chains:
                dv_ref[pl.ds(offs[b], ATT_BLK), lanes[hh]] += _tn(ps[(b, hh)].astype(BF16), dob[hh])
            for b, hh in chains:
                dkn_ref[pl.ds(offs[b], ATT_BLK), lanes[hh]] += _tn(dss[(b, hh)], qnb[hh])
            for b in range(nb):
                dkr_ref[pl.ds(offs[b], ATT_BLK), :] += _tn(dss[(b, 0)], qrb[0]) + _tn(dss[(b, 1)], qrb[1])
            out = list(carry)
            for b, hh in chains:
                out[2 * hh] = out[2 * hh] + jnp.dot(dss[(b, hh)], knb[(b, hh)], preferred_element_type=F32)
                out[2 * hh + 1] = out[2 * hh + 1] + jnp.dot(dss[(b, hh)], krbs[b], preferred_element_type=F32)
            return tuple(out)

        zero = jnp.zeros((ATT_BLK, LANES), F32)
        carry = lax.fori_loop(0, qi // 2, lambda pr, cr: blocks([2 * pr, 2 * pr + 1], cr, (False, False)),
                              (zero, zero, zero, zero))
        carry = lax.cond(qi % 2 == 1, lambda cr: blocks([qi - 1, qi], cr, (False, True)),
                         lambda cr: blocks([qi], cr, (True,)), carry)
        for hh in range(2):
            dqn_ref[:, lanes[hh]] = carry[2 * hh]
            dqr_ref[:, lanes[hh]] = carry[2 * hh + 1]

    blk = pl.BlockSpec((ATT_BLK, 2 * LANES), lambda p, i: (i, p))
    full = pl.BlockSpec((t, 2 * LANES), lambda p, i: (0, p))
    shared = pl.BlockSpec((t, LANES), lambda p, i: (0, 0))
    wide = jax.ShapeDtypeStruct((t, MLA_HEADS * LANES), F32)
    return pl.pallas_call(
        body, name="mla_attn_bwd", grid=(MLA_HEADS // 2, nq),
        in_specs=[blk, blk, full, shared, full, blk, blk, blk],
        out_specs=[blk, blk, full, shared, full],
        out_shape=[wide, wide, wide, jax.ShapeDtypeStruct((t, LANES), F32), wide],
        compiler_params=pltpu.CompilerParams(dimension_semantics=("arbitrary", "arbitrary")),
    )(qn, qr, kn, kr, v, o, lse, do)


@jax.custom_vjp
def _mla_attention(qn, qr, kn, kr, v):
    return _mla_fwd(qn, qr, kn, kr, v)[0]


def _mla_attention_fwd(qn, qr, kn, kr, v):
    o, lse = _mla_fwd(qn, qr, kn, kr, v)
    return o, (qn, qr, kn, kr, v, o, lse)


def _mla_attention_bwd(res, do):
    return tuple(_mla_bwd(*res, do))


_mla_attention.defvjp(_mla_attention_fwd, _mla_attention_bwd)


def _ffn_in(h, wg, wu):
    t, k = h.shape
    n_sh, _, cc = wg.shape

    def body(h_ref, wg_ref, wu_ref, g_ref, u_ref, a_ref):
        hb = h_ref[...].astype(BF16)
        for j in range(n_sh):
            cols = slice(j * cc, (j + 1) * cc)
            g = jnp.dot(hb, wg_ref[j], preferred_element_type=F32)
            u = jnp.dot(hb, wu_ref[j], preferred_element_type=F32)
            g_ref[:, cols] = g.astype(BF16)
            u_ref[:, cols] = u.astype(BF16)
            a_ref[:, cols] = _f_swiglu(g, u)[0].astype(BF16)

    w_spec = pl.BlockSpec((n_sh, k, cc), lambda i: (0, 0, 0))
    o_spec = pl.BlockSpec((MM_ROW_TILE, n_sh * cc), lambda i: (i, 0))
    wide = jax.ShapeDtypeStruct((t, n_sh * cc), BF16)
    return pl.pallas_call(
        body, name="ffn_in_fwd", grid=(t // MM_ROW_TILE,),
        in_specs=[pl.BlockSpec((MM_ROW_TILE, k), lambda i: (i, 0)), w_spec, w_spec],
        out_specs=[o_spec, o_spec, o_spec],
        out_shape=[wide, wide, wide],
        compiler_params=pltpu.CompilerParams(dimension_semantics=("arbitrary",), vmem_limit_bytes=MM_VMEM_LIMIT),
    )(h, wg, wu)


def _ffn_mid_bwd(dy, wd, g, u):
    t, n = dy.shape
    n_sh, cc, _ = wd.shape

    def body(dy_ref, wd_ref, g_ref, u_ref, dg_ref, du_ref):
        d_act = _nt(dy_ref[...].astype(BF16), wd_ref[...])
        _, vjp = jax.vjp(_f_swiglu, g_ref[...].astype(F32), u_ref[...].astype(F32))
        dg, du = vjp((d_act,))
        dg_ref[...] = dg.astype(BF16)
        du_ref[...] = du.astype(BF16)

    blk = pl.BlockSpec((MM_ROW_TILE, cc), lambda j, i: (i, j))
    wide = jax.ShapeDtypeStruct((t, n_sh * cc), BF16)
    return pl.pallas_call(
        body, name="ffn_mid_bwd", grid=(n_sh, t // MM_ROW_TILE),
        in_specs=[pl.BlockSpec((MM_ROW_TILE, n), lambda j, i: (i, 0)),
                  pl.BlockSpec((None, cc, n), lambda j, i: (j, 0, 0)), blk, blk],
        out_specs=[blk, blk], out_shape=[wide, wide],
        compiler_params=pltpu.CompilerParams(dimension_semantics=("arbitrary", "arbitrary"),
                                             vmem_limit_bytes=MM_VMEM_LIMIT),
    )(dy, wd, g, u)


def _ffn_dh(dg, du, wg, wu):
    t = dg.shape[0]
    n_sh, k, cc = wg.shape

    def body(dg_ref, du_ref, wg_ref, wu_ref, o_ref):
        acc = jnp.zeros((MM_ROW_TILE, k), F32)
        for j in range(n_sh):
            cols = slice(j * cc, (j + 1) * cc)
            acc = acc + _nt(dg_ref[:, cols], wg_ref[j]) + _nt(du_ref[:, cols], wu_ref[j])
        o_ref[...] = acc

    blk = pl.BlockSpec((MM_ROW_TILE, n_sh * cc), lambda i: (i, 0))
    w_spec = pl.BlockSpec((n_sh, k, cc), lambda i: (0, 0, 0))
    return pl.pallas_call(
        body, name="ffn_dh", grid=(t // MM_ROW_TILE,),
        in_specs=[blk, blk, w_spec, w_spec],
        out_specs=pl.BlockSpec((MM_ROW_TILE, k), lambda i: (i, 0)),
        out_shape=jax.ShapeDtypeStruct((t, k), F32),
        compiler_params=pltpu.CompilerParams(dimension_semantics=("arbitrary",), vmem_limit_bytes=MM_VMEM_LIMIT),
    )(dg, du, wg, wu)


def _ffn_dw_in(h, dy, n_sh, name):
    t, k = h.shape
    cc = dy.shape[1] // n_sh
    tk = 512

    def body(h_ref, dy_ref, o_ref):
        o_ref[...] = _tn(h_ref[...].astype(BF16), dy_ref[...]).astype(BF16)

    return pl.pallas_call(
        body, name=name, grid=(n_sh, k // tk),
        in_specs=[pl.BlockSpec((t, tk), lambda j, i: (0, i)), pl.BlockSpec((t, cc), lambda j, i: (0, j))],
        out_specs=pl.BlockSpec((None, tk, cc), lambda j, i: (j, i, 0)),
        out_shape=jax.ShapeDtypeStruct((n_sh, k, cc), BF16),
        compiler_params=pltpu.CompilerParams(dimension_semantics=("arbitrary", "arbitrary"),
                                             vmem_limit_bytes=MM_VMEM_LIMIT),
    )(h, dy)


@jax.custom_vjp
def _ffn_block(h, wg, wu, wd):
    act = _ffn_in(h, wg, wu)[2]
    return _mm(act, wd.reshape(-1, wd.shape[2]), "nn", "ffn_down_fwd", MM_ROW_TILE, wd.shape[2])


def _ffn_block_fwd(h, wg, wu, wd):
    g, u, act = _ffn_in(h, wg, wu)
    y = _mm(act, wd.reshape(-1, wd.shape[2]), "nn", "ffn_down_fwd", MM_ROW_TILE, wd.shape[2])
    return y, (h, wg, wu, wd, g, u, act)


def _ffn_block_bwd(res, dy):
    h, wg, wu, wd, g, u, act = res
    dg, du = _ffn_mid_bwd(dy, wd, g, u)
    dh = _ffn_dh(dg, du, wg, wu)
    n_sh = wg.shape[0]
    dwg = _ffn_dw_in(h, dg, n_sh, "ffn_gate_dw")
    dwu = _ffn_dw_in(h, du, n_sh, "ffn_up_dw")
    dwd = _mm(act, dy, "tn", "ffn_down_dw", 256, wd.shape[2], out_dtype=BF16).reshape(wd.shape)
    return dh, dwg, dwu, dwd


_ffn_block.defvjp(_ffn_block_fwd, _ffn_block_bwd)


def _swap_halves(w):
    half = w.shape[-1] // 2
    return jnp.concatenate([w[..., half:], w[..., :half]], axis=-1)


def _pad_lanes(w):
    return jnp.concatenate([w, jnp.zeros(w.shape[:-1] + (LANES - w.shape[-1],), w.dtype)], axis=-1)


def _join_cols(shards):
    return shards.transpose(1, 0, 2).reshape(shards.shape[1], -1)


def _mod_parts(mod):
    return [mod[:, i * D_MODEL:(i + 1) * D_MODEL] for i in range(N_MOD)]


def _local_loss(x, mod, p, cos, sin, target):
    return _ffn_stage(x, _mixing_stage(x, mod, p, cos, sin), mod, p, target)


def _mixing_stage(x, mod, p, cos, sin):
    shift1, scale1 = _mod_parts(mod)[:2]

    w_in = _join_cols(p["w_in"])
    k_rope_w = w_in[:, 2176:2240]
    w_in_ext = jnp.concatenate([w_in[:, :2176], _pad_lanes(k_rope_w), _pad_lanes(_swap_halves(k_rope_w)),
                                jnp.zeros((D_MODEL, LANES), w_in.dtype)], axis=1)
    (h1,) = _make_rowwise("pre_attn", _f_pre_attn, 1, 3, [D_MODEL], [True], out_dtypes=[BF16])(
        x, p["norm_attn"], scale1, shift1)
    q_sb, k_sb, v_sb, cq, ckv, kr, kr_sw = _make_linear_split(
        "in_proj", (SB_WIDTH, SB_WIDTH, SB_WIDTH, MLA_Q_RANK, MLA_KV_RANK, LANES, LANES), 512)(h1, w_in_ext)

    o_sb = _sb_attention(q_sb, k_sb, v_sb)

    wq = _join_cols(p["w_q_up"]).reshape(MLA_Q_RANK, MLA_HEADS, MLA_QK)
    wq_n, wq_r = wq[:, :, :MLA_NOPE], wq[:, :, MLA_NOPE:]
    w_q_ext = jnp.concatenate([wq_n.reshape(MLA_Q_RANK, -1), _pad_lanes(wq_r).reshape(MLA_Q_RANK, -1),
                               _pad_lanes(_swap_halves(wq_r)).reshape(MLA_Q_RANK, -1)], axis=1)
    wkv = _join_cols(p["w_kv_up"]).reshape(MLA_KV_RANK, MLA_HEADS, MLA_NOPE + MLA_V)
    w_kv_ext = jnp.concatenate([wkv[:, :, :MLA_NOPE].reshape(MLA_KV_RANK, -1),
                                wkv[:, :, MLA_NOPE:].reshape(MLA_KV_RANK, -1)], axis=1)
    cqn, ckvn = _make_rowwise("mla_a", _f_mla_a, 2, 2, [MLA_Q_RANK, MLA_KV_RANK], [True, True],
                              out_dtypes=[BF16, BF16], grad_dtypes=[BF16, BF16])(
        cq, ckv, p["q_a_norm"], p["kv_a_norm"])
    qall = _make_linear("q_up", 384, 768)(cqn, w_q_ext)
    kn_all, v_mla = _make_linear_split("kv_up", (MLA_HEADS * MLA_NOPE, MLA_HEADS * MLA_V), MLA_KV_RANK)(ckvn, w_kv_ext)
    gq = p["q_norm"]
    gkr = p["k_rope_norm"]
    qn, qr, kn, krr = _make_rowwise("mla_b", _f_mla_b, 6, 6, [512, 512, 512, LANES],
                                    [True, True, True, True, False, False],
                                    out_dtypes=[BF16] * 4, grad_dtypes=[BF16] * 4)(
        qall, kn_all, kr, kr_sw, cos, sin,
        gq[:, :MLA_NOPE], _pad_lanes(gq[:, MLA_NOPE:]), _pad_lanes(_swap_halves(gq[:, MLA_NOPE:])),
        p["k_nope_norm"], _pad_lanes(gkr), _pad_lanes(_swap_halves(gkr)))
    o_mla = _mla_attention(qn, qr, kn, krr, v_mla)

    (mixed,) = _make_rowwise("post_attn", _f_post_attn, 2, 2, [D_MODEL], [True, True])(
        o_sb, o_mla, p["out_norm_sb"], p["out_norm_mla"])
    return mixed


def _ffn_stage(x, mixed, mod, p, target):
    _, _, gate1, shift2, scale2, gate2 = _mod_parts(mod)
    attn = _make_linear("out_proj", 512, 512)(mixed, p["w_out"].reshape(D_MODEL, D_MODEL))

    x2, h2 = _make_rowwise("pre_ffn", _f_pre_ffn, 2, 4, [D_MODEL, D_MODEL], [True, True],
                           out_dtypes=[F32, BF16], grad_dtypes=[F32, BF16])(
        x, attn, gate1, p["norm_ffn"], scale2, shift2)
    ffn = _ffn_block(h2, p["w_gate"], p["w_up"], p["w_down"])
    (row_loss,) = _make_rowwise("loss", _f_loss, 3, 1, [1], [True, True, False], grad_dtypes=[F32, BF16])(
        x2, ffn, target, gate2)
    return 0.5 * jnp.sum(row_loss)


def _my_place():
    return lax.axis_index("x"), lax.axis_index("y"), lax.axis_index("c")


def _small_gather(x_ref, out_ref, send_sems, recv_sems, base, local_sem):
    m_per = x_ref.shape[0]
    x, y, c = _my_place()
    me, sibling = (x, y, c), (x, y, 1 - c)
    chips = [(1 - x, y), (x, 1 - y), (1 - x, 1 - y)]

    def rows(px, py, pc):
        return out_ref.at[pl.ds((4 * px + 2 * py + pc) * m_per, m_per), :]

    def copy(k, blk, to, src=None):
        return _remote(rows(*blk) if src is None else src, rows(*blk), send_sems, recv_sems, base + k, to)

    mine = pltpu.make_async_copy(x_ref, rows(*me), local_sem)
    first = [copy(0, me, sibling, src=x_ref)] + [copy(1 + j, me, (*chip, c), src=x_ref) for j, chip in enumerate(chips)]
    passed = [copy(4 + j, (*chip, c), sibling) for j, chip in enumerate(chips)]

    def start():
        mine.start()
        for cp in first:
            cp.start()

    def finish():
        for j, chip in enumerate(chips):
            copy(1 + j, (*chip, c), me).wait_recv()
            passed[j].start()
        copy(0, sibling, me).wait_recv()
        for j, chip in enumerate(chips):
            copy(4 + j, (*chip, 1 - c), me).wait_recv()
        for cp in first + passed:
            cp.wait_send()
        mine.wait()

    return start, finish


def _all_gather_small(block, name):
    m_per, n = block.shape

    def body(x_ref, out_ref, send_sems, recv_sems, local_sem):
        start, finish = _small_gather(x_ref, out_ref, send_sems, recv_sems, 0, local_sem)
        start()
        finish()

    return pl.pallas_call(
        body, name=name,
        out_shape=jax.ShapeDtypeStruct((N_DEV * m_per, n), block.dtype),
        in_specs=[pl.BlockSpec(memory_space=pltpu.VMEM)],
        out_specs=pl.BlockSpec(memory_space=pltpu.VMEM),
        scratch_shapes=[pltpu.SemaphoreType.DMA((7,)), pltpu.SemaphoreType.DMA((7,)), pltpu.SemaphoreType.DMA],
    )(block)


EARLY = ("w_in", "w_q_up", "w_kv_up")
LATE = ("w_out", "w_gate", "w_up", "w_down")
BIG = EARLY + LATE
TRANSPOSED_UPDATE = ("w_in", "w_gate", "w_up")
HALF_AXIS = {"w_in": 0, "w_q_up": 0, "w_kv_up": 0, "w_out": 0, "w_gate": 0, "w_up": 0, "w_down": 1}


def _half(ref, h, axis, lead=()):
    trail = ref.shape[len(lead):]
    idx = list(lead) + [slice(None)] * len(trail)
    at = len(trail) - 2 + axis
    n2 = trail[at] // 2
    idx[len(lead) + at] = pl.ds(h * n2, n2)
    return ref.at[tuple(idx)]


def _half_shape(shape, axis):
    shape = list(shape)
    shape[len(shape) - 2 + axis] //= 2
    return tuple(shape)


def _remote(src, dst, send_sems, recv_sems, k, to):
    return pltpu.make_async_remote_copy(src_ref=src, dst_ref=dst, send_sem=send_sems.at[k],
                                        recv_sem=recv_sems.at[k], device_id=to, device_id_type=MESH)


def _gather_weights(names, shards, small_block):
    n_w = len(shards)
    axes = [HALF_AXIS[n] for n in names]

    def body(*refs):
        w_refs, small_ref = refs[:n_w], refs[n_w]
        out_refs, token, small_out = refs[n_w + 1:2 * n_w + 1], refs[2 * n_w + 1], refs[2 * n_w + 2]
        send_sems, recv_sems, local_sems = refs[2 * n_w + 3:]
        token[...] = jnp.zeros_like(token)
        x, y, c = _my_place()
        sibling = (x, y, 1 - c)
        chips = [(1 - x, y), (x, 1 - y), (1 - x, 1 - y)]
        me = 2 * x + y
        small_start, small_finish = _small_gather(small_ref, small_out, send_sems, recv_sems, 6 * n_w,
                                                  local_sems.at[n_w])
        small_start()
        mine = [pltpu.make_async_copy(w, o.at[me], local_sems.at[i]) for i, (w, o) in enumerate(zip(w_refs, out_refs))]
        for cp in mine:
            cp.start()
        first = [_remote(_half(w_refs[i], c, axes[i]), _half(out_refs[i], c, axes[i], (me,)),
                         send_sems, recv_sems, 6 * i + j, (*chip, c))
                 for i in range(n_w) for j, chip in enumerate(chips)]
        for cp in first:
            cp.start()
        small_finish()
        passed = []
        for j, (cx, cy) in enumerate(chips):
            for i in range(n_w):
                blk = _half(out_refs[i], c, axes[i], (2 * cx + cy,))
                _remote(blk, blk, send_sems, recv_sems, 6 * i + j, (cx, cy, c)).wait_recv()
                cp = _remote(blk, blk, send_sems, recv_sems, 6 * i + 3 + j, sibling)
                cp.start()
                passed.append(cp)
        for j, (cx, cy) in enumerate(chips):
            for i in range(n_w):
                blk = _half(out_refs[i], 1 - c, axes[i], (2 * cx + cy,))
                _remote(blk, blk, send_sems, recv_sems, 6 * i + 3 + j, sibling).wait_recv()
        for cp in first + passed:
            cp.wait_send()
        for cp in mine:
            cp.wait()

    outs = pl.pallas_call(
        body, name="gather_weights",
        out_shape=[jax.ShapeDtypeStruct((N_CHIPS,) + s.shape, s.dtype) for s in shards]
        + [jax.ShapeDtypeStruct((8, LANES), F32),
           jax.ShapeDtypeStruct((N_DEV * small_block.shape[0], small_block.shape[1]), small_block.dtype)],
        in_specs=[ANY] * (n_w + 1), out_specs=[ANY] * n_w + [pl.BlockSpec(memory_space=pltpu.VMEM), ANY],
        scratch_shapes=[pltpu.SemaphoreType.DMA((6 * n_w + 7,)), pltpu.SemaphoreType.DMA((6 * n_w + 7,)),
                        pltpu.SemaphoreType.DMA((n_w + 1,))],
    )(*shards, small_block)
    return outs[:n_w], outs[n_w], outs[n_w + 1]


def _pair_exchange(names, grads, call_name, small_block):
    n_w = len(grads)
    axes = [HALF_AXIS[n] for n in names]

    def body(*refs):
        g_refs, small_ref = refs[:n_w], refs[n_w]
        t_refs, small_out = refs[n_w + 1:2 * n_w + 1], refs[2 * n_w + 1]
        send_sems, recv_sems, local_sem = refs[2 * n_w + 2:]
        x, y, c = _my_place()
        small_start, small_finish = _small_gather(small_ref, small_out, send_sems, recv_sems, n_w, local_sem)
        small_start()
        sends = [_remote(_half(g_refs[i], 1 - c, axes[i]), t_refs[i], send_sems, recv_sems, i, (x, y, 1 - c))
                 for i in range(n_w)]
        for cp in sends:
            cp.start()
        small_finish()
        for cp in sends:
            cp.wait_recv()
        for cp in sends:
            cp.wait_send()

    outs = pl.pallas_call(
        body, name=call_name,
        out_shape=[jax.ShapeDtypeStruct(_half_shape(g.shape, a), g.dtype) for g, a in zip(grads, axes)]
        + [jax.ShapeDtypeStruct((N_DEV * small_block.shape[0], small_block.shape[1]), small_block.dtype)],
        in_specs=[ANY] * (n_w + 1), out_specs=[ANY] * (n_w + 1),
        scratch_shapes=[pltpu.SemaphoreType.DMA((n_w + 7,)), pltpu.SemaphoreType.DMA((n_w + 7,)),
                        pltpu.SemaphoreType.DMA],
    )(*grads, small_block)
    return outs[:n_w], outs[n_w]


def _sibling_join(halves, name, after):
    n_w = len(halves)

    def body(*refs):
        s_refs, j_refs = refs[:n_w], refs[n_w + 1:2 * n_w + 1]
        send_sems, recv_sems = refs[2 * n_w + 1:]
        x, y, c = _my_place()
        sends = [_remote(s_refs[i], j_refs[i], send_sems, recv_sems, i, (x, y, 1 - c)) for i in range(n_w)]
        for cp in sends:
            cp.start()
        for cp in sends:
            cp.wait_recv()
        for cp in sends:
            cp.wait_send()

    return pl.pallas_call(
        body, name=name,
        out_shape=[jax.ShapeDtypeStruct(s.shape, s.dtype) for s in halves],
        in_specs=[ANY] * (n_w + 1), out_specs=[ANY] * n_w,
        scratch_shapes=[pltpu.SemaphoreType.DMA((n_w,)), pltpu.SemaphoreType.DMA((n_w,))],
    )(*halves, after)


HBM_SPEC = pl.BlockSpec(memory_space=pltpu.HBM)
SEM_SPEC = pl.BlockSpec(memory_space=pltpu.SEMAPHORE)
DATAFLOW = pltpu.SideEffectType.DATAFLOW_SIDE_EFFECTING


def _in_hbm(a):
    return pltpu.with_memory_space_constraint(a, pltpu.HBM)


def _exchange_start(name, srcs, lands, plan, n_copies, after, thru):
    n = len(srcs)

    def body(*refs):
        src_refs, land_refs = refs[:n], refs[n:2 * n]
        send_sems, recv_sems = refs[2 * n + 2], refs[2 * n + 3]
        for k, (src, dst, to, k_recv) in enumerate(plan(src_refs, land_refs)):
            pltpu.make_async_remote_copy(src_ref=src, dst_ref=dst, send_sem=send_sems.at[k],
                                         recv_sem=recv_sems.at[k_recv], device_id=to, device_id_type=MESH).start()

    outs = pl.pallas_call(
        body, name=name,
        out_shape=(pltpu.SemaphoreType.DMA((n_copies,)), pltpu.SemaphoreType.DMA((n_copies,)),
                   *[pltpu.HBM(a.shape, a.dtype) for a in list(srcs) + list(lands) + [thru]]),
        in_specs=[HBM_SPEC] * (2 * n + 1) + [ANY],
        out_specs=(SEM_SPEC, SEM_SPEC, *[HBM_SPEC] * (2 * n + 1)),
        input_output_aliases={i: 2 + i for i in range(2 * n + 1)},
        compiler_params=pltpu.CompilerParams(has_side_effects=DATAFLOW),
    )(*[_in_hbm(a) for a in list(srcs) + list(lands) + [thru]], after)
    return outs[0], outs[1], outs[2:2 + n], outs[2 + n:2 + 2 * n], outs[2 + 2 * n]


def _exchange_wait(name, started, plan, after):
    send_sems, recv_sems, srcs, lands, _ = started
    n = len(srcs)

    def body(*refs):
        src_refs, land_refs = refs[:n], refs[n:2 * n]
        s_sems, r_sems = refs[2 * n], refs[2 * n + 1]
        for k, (src, dst, to, _) in enumerate(plan(src_refs, land_refs)):
            cp = _remote(src, dst, s_sems, r_sems, k, to)
            cp.wait_send()
            cp.wait_recv()

    outs = pl.pallas_call(
        body, name=name,
        out_shape=tuple(pltpu.HBM(a.shape, a.dtype) for a in list(srcs) + list(lands)),
        in_specs=[HBM_SPEC] * (2 * n) + [SEM_SPEC, SEM_SPEC, ANY],
        out_specs=tuple([HBM_SPEC] * (2 * n)),
        input_output_aliases={i: i for i in range(2 * n)},
        compiler_params=pltpu.CompilerParams(has_side_effects=DATAFLOW),
    )(*srcs, *lands, send_sems, recv_sems, after)
    return outs[:n], outs[n:]


def _late_gather_plan(src_refs, land_refs):
    x, y, c = _my_place()
    chips = [(1 - x, y), (x, 1 - y), (1 - x, 1 - y)]
    plan = [(src, land.at[2 * x + y], (cx, cy, c)) for src, land in zip(src_refs, land_refs) for cx, cy in chips]
    return [entry + (k,) for k, entry in enumerate(plan)]


def _late_scatter_plan(src_refs, land_refs):
    x, y, c = _my_place()
    chips = [(1 - x, y), (x, 1 - y), (1 - x, 1 - y)]
    plan = [(src.at[2 * cx + cy], land.at[j], (cx, cy, c))
            for src, land in zip(src_refs, land_refs) for j, (cx, cy) in enumerate(chips)]
    return [entry + (k,) for k, entry in enumerate(plan)]


def _direct_scatter_plan(names):
    axes = [HALF_AXIS[n] for n in names]

    def plan(src_refs, land_refs):
        x, y, c = _my_place()
        chips = [(1 - x, y), (x, 1 - y), (1 - x, 1 - y)]
        out = []
        for i, (src, land) in enumerate(zip(src_refs, land_refs)):
            for f, (cx, cy) in enumerate(chips):
                for core in range(2):
                    out.append((_half(src, core, axes[i], (2 * cx + cy,)), land.at[2 * f + c], (cx, cy, core),
                                7 * i + 2 * f + c))
            out.append((_half(src, 1 - c, axes[i], (2 * x + y,)), land.at[6], (x, y, 1 - c), 7 * i + 6))
        return out

    return plan


def _row_tile(rows, mult=16, limit=ROW_TILE):
    return max(d for d in range(mult, limit + 1, mult) if rows % d == 0)


def _pair_sum(place, g, theirs, axis, name):
    nj, rr, cc = theirs.shape
    tr = _row_tile(rr, limit=1024)
    nb = rr // tr
    if axis == 0:
        g_map = lambda j, i, pr: (j, pr[0] * nb + i, 0)
    else:
        g_map = lambda j, i, pr: (j, i, pr[0])

    def body(pr, g_ref, t_ref, o_ref):
        o_ref[...] = (g_ref[...].astype(F32) + t_ref[...].astype(F32)).astype(BF16)

    spec = pl.BlockSpec((None, tr, cc), lambda j, i, pr: (j, i, 0))
    return pl.pallas_call(
        body, name=name,
        grid_spec=pltpu.PrefetchScalarGridSpec(
            num_scalar_prefetch=1, grid=(nj, nb),
            in_specs=[pl.BlockSpec((None, tr, cc), g_map), spec], out_specs=spec),
        out_shape=jax.ShapeDtypeStruct(theirs.shape, BF16))(place, g, theirs)


def _chip_sum(place, pair_sums, parts, name, transposed):
    _, rr, cc = parts.shape
    tr = _row_tile(rr, LANES) if transposed else _row_tile(rr, limit=1024)

    def body(pr, h_ref, p_ref, o_ref):
        acc = p_ref[0].astype(F32)
        for j in range(1, N_CHIPS - 1):
            acc = acc + p_ref[j].astype(F32)
        acc = acc + h_ref[...].astype(F32)
        o_ref[...] = (acc.T if transposed else acc).astype(BF16)

    out_spec = pl.BlockSpec((cc, tr), lambda i, pr: (0, i)) if transposed else pl.BlockSpec((tr, cc), lambda i, pr: (i, 0))
    return pl.pallas_call(
        body, name=name,
        grid_spec=pltpu.PrefetchScalarGridSpec(
            num_scalar_prefetch=1, grid=(rr // tr,),
            in_specs=[pl.BlockSpec((None, tr, cc), lambda i, pr: (pr[1], i, 0)),
                      pl.BlockSpec((N_CHIPS - 1, tr, cc), lambda i, pr: (0, i, 0))],
            out_specs=out_spec),
        out_shape=jax.ShapeDtypeStruct((cc, rr) if transposed else (rr, cc), BF16))(place, pair_sums, parts)


def _chip_sum_direct(place, g, parts, axis, name, transposed):
    n_parts, rr, cc = parts.shape
    tr = _row_tile(rr, LANES) if transposed else _row_tile(rr, limit=1024)
    nb = rr // tr
    if axis == 0:
        g_map = lambda i, pr: (pr[1], pr[0] * nb + i, 0)
    else:
        g_map = lambda i, pr: (pr[1], i, pr[0])

    def body(pr, g_ref, p_ref, o_ref):
        acc = p_ref[0].astype(F32)
        for j in range(1, n_parts):
            acc = acc + p_ref[j].astype(F32)
        acc = acc + g_ref[...].astype(F32)
        o_ref[...] = (acc.T if transposed else acc).astype(BF16)

    out_spec = pl.BlockSpec((cc, tr), lambda i, pr: (0, i)) if transposed else pl.BlockSpec((tr, cc), lambda i, pr: (i, 0))
    return pl.pallas_call(
        body, name=name,
        grid_spec=pltpu.PrefetchScalarGridSpec(
            num_scalar_prefetch=1, grid=(nb,),
            in_specs=[pl.BlockSpec((None, tr, cc), g_map), pl.BlockSpec((n_parts, tr, cc), lambda i, pr: (0, i, 0))],
            out_specs=out_spec),
        out_shape=jax.ShapeDtypeStruct((cc, rr) if transposed else (rr, cc), BF16))(place, g, parts)


def _silu(v):
    return v / (1.0 + jnp.exp(-v))


def _ada_fwd(c_all, w_shard, b_shard):
    def body(c_ref, w_ref, b_ref, o_ref):
        o_ref[...] = jnp.dot(_silu(c_ref[...]), w_ref[...], precision=lax.Precision.HIGHEST,
                             preferred_element_type=F32) + b_ref[...]

    return pl.pallas_call(body, name="ada_fwd", out_shape=jax.ShapeDtypeStruct((c_all.shape[0], w_shard.shape[1]), F32),
                          compiler_params=pltpu.CompilerParams(vmem_limit_bytes=MM_VMEM_LIMIT))(c_all, w_shard, b_shard)


def _ada_bwd(c_all, dmod_cols):
    def body(c_ref, d_ref, o_ref):
        o_ref[...] = lax.dot_general(_silu(c_ref[...]), d_ref[...], (((0,), (0,)), ((), ())),
                                     precision=lax.Precision.HIGHEST, preferred_element_type=F32)

    return pl.pallas_call(body, name="ada_bwd", out_shape=jax.ShapeDtypeStruct((c_all.shape[1], dmod_cols.shape[1]), F32),
                          compiler_params=pltpu.CompilerParams(vmem_limit_bytes=MM_VMEM_LIMIT))(c_all, dmod_cols)


def _adamw_math(w, g, m, v):
    m = ADAM_B1 * m + (1.0 - ADAM_B1) * g
    v = ADAM_B2 * v + (1.0 - ADAM_B2) * (g * g)
    m_hat = m / (1.0 - ADAM_B1 ** ADAM_STEP)
    v_hat = v / (1.0 - ADAM_B2 ** ADAM_STEP)
    delta = -ADAM_LR * (m_hat / (jnp.sqrt(v_hat) + ADAM_EPS) + ADAM_WD * w)
    return delta, m, v


def _adamw(w, g, m, v, name):
    r, ccols = w.shape
    tr = max(d for d in range(8, ROW_TILE + 1, 8) if r % d == 0)
    spec = pl.BlockSpec((tr, ccols), lambda i: (i, 0))

    def body(w_ref, g_ref, m_ref, v_ref, d_ref, nm_ref, nv_ref):
        d_ref[...], nm_ref[...], nv_ref[...] = _adamw_math(w_ref[...], g_ref[...], m_ref[...], v_ref[...])

    return pl.pallas_call(body, name=name, grid=(r // tr,), in_specs=[spec] * 4, out_specs=[spec] * 3,
                          out_shape=[jax.ShapeDtypeStruct(w.shape, F32)] * 3,
                          compiler_params=pltpu.CompilerParams(vmem_limit_bytes=MM_VMEM_LIMIT))(w, g, m, v)


def _small_layout(sizes):
    offs, off = [], 0
    for n in sizes:
        offs.append(off)
        off += -(-n // LANES) * LANES
    total = -(-(off + LANES) // (8 * LANES)) * (8 * LANES)
    return offs, off, total


def _adamw_small(ws, g_all, ms, vs, offs, loss_off):
    n_p = len(ws)

    def device_sum(g_ref, off, width):
        blk = g_ref[:, off:off + width]
        acc = blk[0:1]
        for d in range(1, N_DEV):
            acc = acc + blk[d:d + 1]
        return acc

    def body(*refs):
        w_refs, m_refs, v_refs = refs[:n_p], refs[n_p:2 * n_p], refs[2 * n_p:3 * n_p]
        g_ref = refs[3 * n_p]
        outs = refs[3 * n_p + 1:]
        for i in range(n_p):
            n = w_refs[i].shape[1]
            g = device_sum(g_ref, offs[i], -(-n // LANES) * LANES)[:, :n]
            outs[i][...] = g
            outs[n_p + i][...], outs[2 * n_p + i][...], outs[3 * n_p + i][...] = _adamw_math(
                w_refs[i][...], g, m_refs[i][...], v_refs[i][...])
        outs[4 * n_p][...] = device_sum(g_ref, loss_off, LANES)

    res = pl.pallas_call(
        body, name="adamw_small",
        out_shape=[jax.ShapeDtypeStruct(a.shape, F32) for a in list(ws) * 4] + [jax.ShapeDtypeStruct((1, LANES), F32)],
    )(*ws, *ms, *vs, g_all)
    return res[:n_p], res[n_p:2 * n_p], res[2 * n_p:3 * n_p], res[3 * n_p:4 * n_p], res[4 * n_p]


def _adamw_halves(place, w, own, sib, m, v, axis, name, after):
    r, cc = w.shape
    if axis == 0:
        rows, gc = own.shape[0], own.shape[1]
        tr = _row_tile(rows)
        nb = rows // tr
        w_spec = pl.BlockSpec((tr, cc), lambda h, i, pr: (h * nb + i, 0))
        g_spec = pl.BlockSpec((tr, gc), lambda h, i, pr: (i, 0))
    else:
        tr = _row_tile(r)
        nb = r // tr
        gc = own.shape[1]
        w_spec = pl.BlockSpec((tr, gc), lambda h, i, pr: (i, h))
        g_spec = pl.BlockSpec((tr, gc), lambda h, i, pr: (i, 0))
    wc = w_spec.block_shape[1]

    def body(pr, w_ref, o_ref, s_ref, m_ref, v_ref, after_ref, g_ref, d_ref, nm_ref, nv_ref):
        g = jnp.where(pl.program_id(0) == pr[0], o_ref[...], s_ref[...]).astype(F32)[:, :wc]
        g_ref[...] = g
        d_ref[...], nm_ref[...], nv_ref[...] = _adamw_math(w_ref[...], g, m_ref[...], v_ref[...])

    return pl.pallas_call(
        body, name=name,
        grid_spec=pltpu.PrefetchScalarGridSpec(
            num_scalar_prefetch=1, grid=(2, nb),
            in_specs=[w_spec, g_spec, g_spec, w_spec, w_spec, ANY], out_specs=[w_spec] * 4),
        out_shape=[jax.ShapeDtypeStruct(w.shape, F32)] * 4,
        compiler_params=pltpu.CompilerParams(vmem_limit_bytes=MM_VMEM_LIMIT))(place, w, own, sib, m, v, after)


SMALL = ("b_ada", "norm_attn", "norm_ffn", "q_a_norm", "kv_a_norm", "q_norm", "k_nope_norm", "k_rope_norm",
         "out_norm_sb", "out_norm_mla")
WEIGHTS = ("w_ada", "b_ada", "norm_attn", "norm_ffn", "w_in", "q_a_norm", "w_q_up", "kv_a_norm", "w_kv_up",
           "q_norm", "k_nope_norm", "k_rope_norm", "out_norm_sb", "out_norm_mla", "w_out", "w_gate", "w_up",
           "w_down")


def kernel(x, c, positions, w_ada, b_ada, norm_attn, norm_ffn, w_in, q_a_norm, w_q_up, kv_a_norm, w_kv_up, q_norm, k_nope_norm, k_rope_norm, out_norm_sb, out_norm_mla, w_out, w_gate, w_up, w_down, loss_target, m_w_ada, m_b_ada, m_norm_attn, m_norm_ffn, m_w_in, m_q_a_norm, m_w_q_up, m_kv_a_norm, m_w_kv_up, m_q_norm, m_k_nope_norm, m_k_rope_norm, m_out_norm_sb, m_out_norm_mla, m_w_out, m_w_gate, m_w_up, m_w_down, v_w_ada, v_b_ada, v_norm_attn, v_norm_ffn, v_w_in, v_q_a_norm, v_w_q_up, v_kv_a_norm, v_w_kv_up, v_q_norm, v_k_nope_norm, v_k_rope_norm, v_out_norm_sb, v_out_norm_mla, v_w_out, v_w_gate, v_w_up, v_w_down):
    local = dict(locals())
    w = {n: local[n][0] for n in WEIGHTS}
    m = {n: local["m_" + n][0] for n in WEIGHTS}
    v = {n: local["v_" + n][0] for n in WEIGHTS}
    small = {n: w[n].reshape(1, -1) for n in SMALL}
    ix, iy, ic = _my_place()
    chip = 2 * ix + iy
    dev = 2 * chip + ic
    xs, target = x[0], loss_target[0]
    seq = xs.shape[0]

    ff_pad = FF_SHARD_PAD - FF_SHARD
    pads = {"w_gate": ((0, 0), (0, ff_pad)), "w_up": ((0, 0), (0, ff_pad)), "w_down": ((0, ff_pad), (0, 0))}
    shards = {n: jnp.pad(w[n].astype(BF16), pads[n]) if n in pads else w[n].astype(BF16) for n in BIG}
    early, early_done, c_gathered = _gather_weights(EARLY, [shards[n] for n in EARLY], c.reshape(8, LANES))
    gathered = dict(zip(EARLY, early))

    c_all = c_gathered.reshape(N_DEV, D_MODEL)
    ada_cols = w["w_ada"].shape[1]
    b_cols = lax.dynamic_slice_in_dim(small["b_ada"], chip * ada_cols, ada_cols, axis=1)
    mod_cols = _ada_fwd(c_all, w["w_ada"], b_cols)
    mod_all = _all_gather_small(mod_cols, "gather_mod").reshape(N_CHIPS, 2, N_DEV, ada_cols)
    mod = lax.dynamic_index_in_dim(mod_all[:, 0], dev, axis=1, keepdims=False).reshape(1, N_MOD * D_MODEL)

    lands = [lax.dynamic_update_index_in_dim(lax.empty((N_CHIPS,) + shards[n].shape, BF16), shards[n], chip, 0)
             for n in LATE]
    late_gather = _exchange_start("gather_late_start", [shards[n] for n in LATE], lands, _late_gather_plan,
                                  3 * len(LATE), early_done, mod)
    mod = late_gather[4]

    half = MLA_ROPE // 2
    freqs = 1.0 / (ROPE_THETA ** (np.arange(half, dtype=np.float32) / half))
    zeros = np.zeros(LANES - MLA_ROPE, np.float32)
    freqs_row = jnp.asarray(np.concatenate([freqs, freqs, zeros]).astype(np.float32)[None])
    sign_row = jnp.asarray(np.concatenate([-np.ones(half), np.ones(half), zeros]).astype(np.float32)[None])
    cos, sin = _rope_tables(positions.reshape(seq, 1), freqs_row, sign_row)

    place = jnp.stack([ic, chip]).astype(jnp.int32)
    small_params = {n: small[n] for n in SMALL if n != "b_ada"}

    p1 = {**{n: gathered[n] for n in EARLY}, **small_params}
    mixed, mixing_vjp = jax.vjp(lambda x_, mod_, p_: _mixing_stage(x_, mod_, p_, cos, sin), xs, mod, p1)
    _, landed = _exchange_wait("gather_late_wait", late_gather, _late_gather_plan, mixed)
    p2 = {**dict(zip(LATE, landed)), **small_params}
    loss_part, ffn_vjp = jax.vjp(lambda x_, mixed_, mod_, p_: _ffn_stage(x_, mixed_, mod_, p_, target), xs, mixed, mod, p2)
    gx2, gmixed, gmod2, gp2 = ffn_vjp(jnp.ones((), F32))
    late_grads = [gp2[n] for n in LATE]
    late_plan = _direct_scatter_plan(LATE)
    late_scatter = _exchange_start(
        "grad_scatter_late_start", late_grads,
        [lax.empty((7,) + _half_shape(gr.shape[1:], HALF_AXIS[n]), BF16) for n, gr in zip(LATE, late_grads)],
        late_plan, 7 * len(LATE), gx2, gmixed)
    gx1, gmod1, gp1 = mixing_vjp(late_scatter[4])
    gx = gx1 + gx2
    gmod = gmod1 + gmod2
    gp = {n: gp1[n] + gp2[n] for n in small_params}

    sizes = [w[n].size for n in SMALL]
    offs, loss_off, n_small = _small_layout(sizes)
    pieces = []
    for n, size in zip(SMALL, sizes):
        pieces.append(gmod if n == "b_ada" else gp[n])
        if size % LANES:
            pieces.append(jnp.zeros((1, LANES - size % LANES), F32))
    pieces += [jnp.full((1, LANES), loss_part), jnp.zeros((1, n_small - loss_off - LANES), F32)]
    small_vec = jnp.concatenate(pieces, axis=1)

    g, delta, new_m, new_v = {}, {}, {}, {}

    def update(names, own, sib, after):
        for n, o, s in zip(names, own, sib):
            if n in TRANSPOSED_UPDATE:
                res = _adamw_halves(place, w[n].T, o, s, m[n].T, v[n].T, 1, "adamw_" + n, after)
                g[n], delta[n], new_m[n], new_v[n] = [r.T for r in res]
            else:
                g[n], delta[n], new_m[n], new_v[n] = _adamw_halves(place, w[n], o, s, m[n], v[n], HALF_AXIS[n],
                                                                   "adamw_" + n, after)

    late_grads, late_parts = _exchange_wait("grad_scatter_late_wait", late_scatter, late_plan, gx)
    own_late = [_chip_sum_direct(place, gr, pt, HALF_AXIS[n], "grad_chip_sum_" + n, n in TRANSPOSED_UPDATE)
                for n, gr, pt in zip(LATE, late_grads, late_parts)]
    early_grads = [gp1[n] for n in EARLY]
    theirs, small_gathered = _pair_exchange(EARLY, early_grads, "grad_pair_exchange_early",
                                            small_vec.reshape(8, n_small // 8))
    small_all = small_gathered.reshape(N_DEV, n_small)
    sib_late = _sibling_join(own_late, "grad_sibling_join_late", small_all)
    early_sums = [_pair_sum(place, gr, th, HALF_AXIS[n], "grad_pair_sum_" + n)
                  for n, gr, th in zip(EARLY, early_grads, theirs)]
    early_scatter = _exchange_start(
        "grad_scatter_early_start", early_sums,
        [lax.empty((N_CHIPS - 1,) + s.shape[1:], BF16) for s in early_sums], _late_scatter_plan, 3 * len(EARLY),
        sib_late[0], small_all)
    small_all = early_scatter[4]
    update(LATE, own_late, sib_late, small_all)

    *small_out, loss_row = _adamw_small([small[n] for n in SMALL], small_all, [m[n].reshape(1, -1) for n in SMALL],
                                        [v[n].reshape(1, -1) for n in SMALL], offs, loss_off)
    loss = loss_row[0, 0]
    for d, outs_d in zip((g, delta, new_m, new_v), small_out):
        d.update({n: o.reshape(w[n].shape) for n, o in zip(SMALL, outs_d)})

    dmod_all = small_all[:, :N_MOD * D_MODEL]
    g["w_ada"] = _ada_bwd(c_all, lax.dynamic_slice_in_dim(dmod_all, chip * ada_cols, ada_cols, axis=1))
    delta["w_ada"], new_m["w_ada"], new_v["w_ada"] = _adamw(w["w_ada"], g["w_ada"], m["w_ada"], v["w_ada"], "adamw_w_ada")

    early_sums, early_parts = _exchange_wait("grad_scatter_early_wait", early_scatter, _late_scatter_plan,
                                             delta["w_ada"])
    own_early = [_chip_sum(place, ps, pt, "grad_chip_sum_" + n, n in TRANSPOSED_UPDATE)
                 for n, ps, pt in zip(EARLY, early_sums, early_parts)]
    sib_early = _sibling_join(own_early, "grad_sibling_join_early", delta["w_ada"])
    update(EARLY, own_early, sib_early, sib_early[0])

    def outs(d):
        return [d[n][None] for n in WEIGHTS]

    return (loss, gx[None], *outs(g), *outs(delta), *outs(new_m), *outs(new_v))
```

```python
import numpy as np
import jax
import jax.numpy as jnp
from jax import lax
from jax.experimental import pallas as pl
from jax.experimental.pallas import tpu as pltpu

F32 = jnp.float32
BF16 = jnp.bfloat16
MESH = pl.DeviceIdType.MESH
ANY = pl.BlockSpec(memory_space=pl.ANY)

D_MODEL = 1024
SB_HEADS = 8
SB_HEAD_DIM = 64
SB_WIDTH = 512
MLA_HEADS = 4
MLA_NOPE = 128
MLA_ROPE = 64
MLA_QK = 192
MLA_V = 128
MLA_Q_RANK = 384
MLA_KV_RANK = 256
D_FF = 2816
N_MOD = 6
ROPE_THETA = 10000.0
EPS = 1e-6
LANES = 128

ADAM_LR = 0.001
ADAM_B1 = 0.9
ADAM_B2 = 0.999
ADAM_EPS = 1e-08
ADAM_WD = 0.01
ADAM_STEP = 10

N_CHIPS = 4
N_DEV = 8
ROW_TILE = 256
MM_ROW_TILE = 512
ATT_BLK = 256
MM_VMEM_LIMIT = 56 * 1024 * 1024
FF_SHARD = D_FF // N_CHIPS
FF_SHARD_PAD = 768


def _mm(a, b, mode, name, tm, tn, out_dtype=F32):
    if mode == "nn":
        (m, k), n = a.shape, b.shape[1]
        a_spec = pl.BlockSpec((tm, k), lambda j, i: (i, 0))
        b_spec = pl.BlockSpec((k, tn), lambda j, i: (0, j))
        dims = (((1,), (0,)), ((), ()))
    elif mode == "nt":
        (m, k), n = a.shape, b.shape[0]
        a_spec = pl.BlockSpec((tm, k), lambda j, i: (i, 0))
        b_spec = pl.BlockSpec((tn, k), lambda j, i: (j, 0))
        dims = (((1,), (1,)), ((), ()))
    else:
        (k, m), n = a.shape, b.shape[1]
        a_spec = pl.BlockSpec((k, tm), lambda j, i: (0, i))
        b_spec = pl.BlockSpec((k, tn), lambda j, i: (0, j))
        dims = (((0,), (0,)), ((), ()))
    assert m % tm == 0 and n % tn == 0, (name, m, n, tm, tn)

    def body(a_ref, b_ref, o_ref):
        o_ref[...] = lax.dot_general(a_ref[...].astype(BF16), b_ref[...].astype(BF16), dims,
                                     preferred_element_type=F32).astype(out_dtype)

    return pl.pallas_call(
        body, name=name, grid=(n // tn, m // tm),
        in_specs=[a_spec, b_spec],
        out_specs=pl.BlockSpec((tm, tn), lambda j, i: (i, j)),
        out_shape=jax.ShapeDtypeStruct((m, n), out_dtype),
        compiler_params=pltpu.CompilerParams(dimension_semantics=("arbitrary", "arbitrary"),
                                             vmem_limit_bytes=MM_VMEM_LIMIT),
    )(a, b)


def _make_linear(name, tk_w, tn_w):
    @jax.custom_vjp
    def op(a, w):
        return _mm(a, w, "nn", name + "_fwd", MM_ROW_TILE, w.shape[1])

    def fwd(a, w):
        return op(a, w), (a, w)

    def bwd(res, dy):
        a, w = res
        da = _mm(dy, w, "nt", name + "_dx", MM_ROW_TILE, w.shape[0])
        dw = _mm(a, dy, "tn", name + "_dw", tk_w, tn_w, out_dtype=BF16)
        return da, dw

    op.defvjp(fwd, bwd)
    return op


def _make_linear_split(name, widths, tk_w):
    starts = [sum(widths[:g]) for g in range(len(widths))]

    def call_fwd(a, w):
        t, k = a.shape
        n = w.shape[1]

        def body(a_ref, w_ref, *o_refs):
            y = jnp.dot(a_ref[...].astype(BF16), w_ref[...], preferred_element_type=F32)
            for o_ref, s0, wd in zip(o_refs, starts, widths):
                o_ref[...] = y[:, s0:s0 + wd]

        return pl.pallas_call(
            body, name=name + "_fwd", grid=(t // MM_ROW_TILE,),
            in_specs=[pl.BlockSpec((MM_ROW_TILE, k), lambda i: (i, 0)), pl.BlockSpec((k, n), lambda i: (0, 0))],
            out_specs=[pl.BlockSpec((MM_ROW_TILE, wd), lambda i: (i, 0)) for wd in widths],
            out_shape=[jax.ShapeDtypeStruct((t, wd), F32) for wd in widths],
            compiler_params=pltpu.CompilerParams(dimension_semantics=("arbitrary",), vmem_limit_bytes=MM_VMEM_LIMIT),
        )(a, w)

    def call_dx(dys, w):
        t = dys[0].shape[0]
        k, n = w.shape

        def body(*refs):
            dy_refs, w_ref, o_ref = refs[:-2], refs[-2], refs[-1]
            acc = jnp.zeros((MM_ROW_TILE, k), F32)
            for dy_ref, s0, wd in zip(dy_refs, starts, widths):
                acc = acc + _nt(dy_ref[...].astype(BF16), w_ref[:, s0:s0 + wd])
            o_ref[...] = acc

        return pl.pallas_call(
            body, name=name + "_dx", grid=(t // MM_ROW_TILE,),
            in_specs=[pl.BlockSpec((MM_ROW_TILE, wd), lambda i: (i, 0)) for wd in widths]
            + [pl.BlockSpec((k, n), lambda i: (0, 0))],
            out_specs=pl.BlockSpec((MM_ROW_TILE, k), lambda i: (i, 0)),
            out_shape=jax.ShapeDtypeStruct((t, k), F32),
            compiler_params=pltpu.CompilerParams(dimension_semantics=("arbitrary",), vmem_limit_bytes=MM_VMEM_LIMIT),
        )(*dys, w)

    def call_dw(a, dys, w):
        t, k = a.shape
        n = w.shape[1]

        def body(a_ref, *refs):
            dy_refs, o_ref = refs[:-1], refs[-1]
            ab = a_ref[...].astype(BF16)
            for dy_ref, s0, wd in zip(dy_refs, starts, widths):
                o_ref[:, s0:s0 + wd] = _tn(ab, dy_ref[...].astype(BF16)).astype(BF16)
            if starts[-1] + widths[-1] < n:
                o_ref[:, starts[-1] + widths[-1]:] = jnp.zeros((tk_w, n - starts[-1] - widths[-1]), BF16)

        return pl.pallas_call(
            body, name=name + "_dw", grid=(k // tk_w,),
            in_specs=[pl.BlockSpec((t, tk_w), lambda i: (0, i))]
            + [pl.BlockSpec((t, wd), lambda i: (0, 0)) for wd in widths],
            out_specs=pl.BlockSpec((tk_w, n), lambda i: (i, 0)),
            out_shape=jax.ShapeDtypeStruct((k, n), BF16),
            compiler_params=pltpu.CompilerParams(dimension_semantics=("arbitrary",), vmem_limit_bytes=MM_VMEM_LIMIT),
        )(a, *dys)

    @jax.custom_vjp
    def op(a, w):
        return tuple(call_fwd(a, w))

    def fwd(a, w):
        return op(a, w), (a, w)

    def bwd(res, dys):
        a, w = res
        return call_dx(dys, w), call_dw(a, dys, w)

    op.defvjp(fwd, bwd)
    return op


def _row_spec(arr, tb):
    return pl.BlockSpec((tb, arr.shape[1]), lambda i: (i, 0))


def _full_spec(arr):
    return pl.BlockSpec(arr.shape, lambda i: (0, 0))


def _make_rowwise(name, f, n_rows, n_params, out_cols, diff_rows, out_dtypes=None, grad_dtypes=None):
    n_out = len(out_cols)
    out_dtypes = out_dtypes or [F32] * n_out
    grad_dtypes = grad_dtypes or [F32] * sum(diff_rows)

    def call_fwd(rows, params):
        t = rows[0].shape[0]

        def body(*refs):
            ins = [r[...] for r in refs[:n_rows + n_params]]
            outs = f(*ins)
            for o_ref, o in zip(refs[n_rows + n_params:], outs):
                o_ref[...] = o.astype(o_ref.dtype)

        return pl.pallas_call(
            body, name=name + "_fwd", grid=(t // ROW_TILE,),
            in_specs=[_row_spec(a, ROW_TILE) for a in rows] + [_full_spec(p) for p in params],
            out_specs=[pl.BlockSpec((ROW_TILE, n), lambda i: (i, 0)) for n in out_cols],
            out_shape=[jax.ShapeDtypeStruct((t, n), dt) for n, dt in zip(out_cols, out_dtypes)],
            compiler_params=pltpu.CompilerParams(dimension_semantics=("arbitrary",),
                                                 vmem_limit_bytes=MM_VMEM_LIMIT),
        )(*rows, *params)

    def call_bwd(rows, params, cts):
        t = rows[0].shape[0]
        d_rows = [a for a, d in zip(rows, diff_rows) if d]
        n_in = n_rows + n_params + n_out

        def body(*refs):
            ins = [r[...] for r in refs[:n_rows + n_params]]
            ct = tuple(r[...].astype(F32) for r in refs[n_rows + n_params:n_in])
            _, vjp = jax.vjp(f, *ins)
            grads = vjp(ct)
            out_refs = refs[n_in:]
            g_rows = [g for g, d in zip(grads[:n_rows], diff_rows) if d]
            for o_ref, g in zip(out_refs[:len(g_rows)], g_rows):
                o_ref[...] = g.astype(o_ref.dtype)
            p_refs = out_refs[len(g_rows):]

            if p_refs:
                @pl.when(pl.program_id(0) == 0)
                def _():
                    for p_ref in p_refs:
                        p_ref[...] = jnp.zeros_like(p_ref)

                for p_ref, g in zip(p_refs, grads[n_rows:]):
                    p_ref[...] += g

        return pl.pallas_call(
            body, name=name + "_bwd", grid=(t // ROW_TILE,),
            in_specs=[_row_spec(a, ROW_TILE) for a in rows] + [_full_spec(p) for p in params]
            + [_row_spec(c, ROW_TILE) for c in cts],
            out_specs=[_row_spec(a, ROW_TILE) for a in d_rows] + [_full_spec(p) for p in params],
            out_shape=[jax.ShapeDtypeStruct(a.shape, dt) for a, dt in zip(d_rows, grad_dtypes)]
            + [jax.ShapeDtypeStruct(p.shape, F32) for p in params],
            compiler_params=pltpu.CompilerParams(dimension_semantics=("arbitrary",),
                                                 vmem_limit_bytes=MM_VMEM_LIMIT),
        )(*rows, *params, *cts)

    @jax.custom_vjp
    def op(*args):
        return tuple(call_fwd(args[:n_rows], args[n_rows:]))

    def fwd(*args):
        return op(*args), args

    def bwd(args, cts):
        rows, params = args[:n_rows], args[n_rows:]
        outs = call_bwd(rows, params, cts)
        it = iter(outs)
        g_rows = [next(it) if d else jnp.zeros_like(a) for a, d in zip(rows, diff_rows)]
        return tuple(g_rows) + tuple(it)

    op.defvjp(fwd, bwd)
    return op


def _rms(x, g, n):
    return x * lax.rsqrt(jnp.sum(x * x, axis=-1, keepdims=True) * (1.0 / n) + EPS) * g


def _f_pre_attn(x, g, scale, shift):
    return (_rms(x, g, D_MODEL) * (1.0 + scale) + shift,)


def _f_mla_a(cq, ckv, gq, gkv):
    return _rms(cq, gq, MLA_Q_RANK), _rms(ckv, gkv, MLA_KV_RANK)


@jax.custom_vjp
def _split_lanes(x):
    return tuple(x[:, i * LANES:(i + 1) * LANES] for i in range(x.shape[1] // LANES))


def _split_lanes_fwd(x):
    return _split_lanes(x), None


def _split_lanes_bwd(_, cts):
    return (jnp.concatenate(cts, axis=1),)


_split_lanes.defvjp(_split_lanes_fwd, _split_lanes_bwd)


def _f_mla_b(qall, kn_all, kr, kr_sw, cos, sin, gqn, gqr, gqr_sw, gkn, gkr, gkr_sw):
    q = _split_lanes(qall)
    kn = _split_lanes(kn_all)
    qn_o, qr_o, kn_o = [], [], []
    for h in range(MLA_HEADS):
        qn, qr, qs = q[h], q[MLA_HEADS + h], q[2 * MLA_HEADS + h]
        ss = jnp.sum(qn * qn, axis=-1, keepdims=True) + jnp.sum(qr * qr, axis=-1, keepdims=True)
        rs = lax.rsqrt(ss * (1.0 / MLA_QK) + EPS)
        qn_o.append(qn * rs * gqn)
        qr_o.append((qr * rs * gqr) * cos + (qs * rs * gqr_sw) * sin)
        kn_o.append(_rms(kn[h], gkn, MLA_NOPE))
    rs = lax.rsqrt(jnp.sum(kr * kr, axis=-1, keepdims=True) * (1.0 / MLA_ROPE) + EPS)
    kr_o = (kr * rs * gkr) * cos + (kr_sw * rs * gkr_sw) * sin
    return (jnp.concatenate(qn_o, axis=1), jnp.concatenate(qr_o, axis=1), jnp.concatenate(kn_o, axis=1), kr_o)


def _f_post_attn(o_sb, o_mla, g_sb, g_mla):
    return (jnp.concatenate([_rms(o_sb, g_sb, SB_WIDTH), _rms(o_mla, g_mla, SB_WIDTH)], axis=1),)


def _f_pre_ffn(x, attn, gate, g, scale, shift):
    x2 = x + gate * attn
    return x2, _rms(x2, g, D_MODEL) * (1.0 + scale) + shift


def _f_swiglu(gt, up):
    return (gt / (1.0 + jnp.exp(-gt)) * up,)


def _f_loss(x2, ffn, target, gate):
    err = x2 + gate * ffn - target
    return (jnp.sum(err * err, axis=-1, keepdims=True) * (1.0 / D_MODEL),)


def _rope_tables(pos_col, freqs, sign):
    t = pos_col.shape[0]

    def body(p_ref, f_ref, s_ref, cos_ref, sin_ref):
        ang = p_ref[...].astype(F32) * f_ref[...]
        live = jnp.abs(s_ref[...])
        cos_ref[...] = jnp.cos(ang) * live
        sin_ref[...] = jnp.sin(ang) * s_ref[...]

    return pl.pallas_call(
        body, name="rope_tables", grid=(t // ROW_TILE,),
        in_specs=[pl.BlockSpec((ROW_TILE, 1), lambda i: (i, 0)), _full_spec(freqs), _full_spec(sign)],
        out_specs=[pl.BlockSpec((ROW_TILE, LANES), lambda i: (i, 0))] * 2,
        out_shape=[jax.ShapeDtypeStruct((t, LANES), F32)] * 2,
    )(pos_col, freqs, sign)


def _hi_lo_dot(x, tri):
    hi = x.astype(BF16)
    lo = (x - hi.astype(F32)).astype(BF16)
    return (jnp.dot(hi, tri, preferred_element_type=F32) + jnp.dot(lo, tri, preferred_element_type=F32))


def _tri(cmp):
    r = lax.broadcasted_iota(jnp.int32, (ATT_BLK, ATT_BLK), 0)
    c = lax.broadcasted_iota(jnp.int32, (ATT_BLK, ATT_BLK), 1)
    return cmp(r, c).astype(BF16)


def _nt(a, b):
    return lax.dot_general(a, b, (((1,), (1,)), ((), ())), preferred_element_type=F32)


def _tn(a, b):
    return lax.dot_general(a, b, (((0,), (0,)), ((), ())), preferred_element_type=F32)


def _sb_logs(z):
    lb = jnp.minimum(z, 0.0) - jnp.log(1.0 + jnp.exp(-jnp.abs(z)))
    return lb, lb - z


def _sb_fwd(q, k, v):
    t = q.shape[0]
    nq = t // ATT_BLK
    scale = SB_HEAD_DIM ** -0.5

    def body(q_ref, k_ref, v_ref, o_ref, tot_ref):
        qi = pl.program_id(1)
        lane = lax.broadcasted_iota(jnp.int32, (ATT_BLK, LANES), 1)
        tri = _tri(lambda r, c: r > c)
        qv = q_ref[...] * scale
        heads = [(lane // SB_HEAD_DIM) == hh for hh in range(2)]
        qms = [jnp.where(mine, qv, 0.0).astype(BF16) for mine in heads]

        def blocks(kbs, carry, diagonal):
            acc = carry[0]
            nb = len(kbs)
            chains = [(b, hh) for b in range(nb) for hh in range(2)]
            offs = [pl.multiple_of(kb * ATT_BLK, ATT_BLK) for kb in kbs]
            kks = [k_ref[pl.ds(off, ATT_BLK), :].astype(BF16) for off in offs]
            v_blks = [v_ref[pl.ds(off, ATT_BLK), :] for off in offs]
            if any(diagonal):
                valid = (lax.broadcasted_iota(jnp.int32, (ATT_BLK, ATT_BLK), 1)
                         < lax.broadcasted_iota(jnp.int32, (ATT_BLK, ATT_BLK), 0))
            zs = {ch: _nt(qms[ch[1]], kks[ch[0]]) for ch in chains}
            vvs = {(b, hh): jnp.where(heads[hh], v_blks[b], 0.0).astype(BF16) for b, hh in chains}
            logs = {ch: _sb_logs(zs[ch]) for ch in chains}
            l1ms = {ch: jnp.where(valid, logs[ch][1], 0.0) if diagonal[ch[0]] else logs[ch][1] for ch in chains}
            run = {(0, hh): carry[1 + hh] for hh in range(2)}
            for b, hh in chains:
                run[(b + 1, hh)] = run[(b, hh)] + jnp.sum(l1ms[(b, hh)], axis=-1, keepdims=True)
            afters = {ch: _hi_lo_dot(l1ms[ch], tri) for ch in chains}
            ws = {ch: jnp.exp(logs[ch][0] + (afters[ch] + run[ch])) for ch in chains}
            ws = {ch: jnp.where(valid, ws[ch], 0.0) if diagonal[ch[0]] else ws[ch] for ch in chains}
            for ch in chains:
                acc = acc + jnp.dot(ws[ch].astype(BF16), vvs[ch], preferred_element_type=F32)
            return (acc, run[(nb, 0)], run[(nb, 1)])

        zero = jnp.zeros((ATT_BLK, 1), F32)
        init = (jnp.zeros((ATT_BLK, LANES), F32), zero, zero)
        carry = lax.cond(qi % 2 == 1, lambda cr: blocks([qi, qi - 1], cr, (True, False)),
                         lambda cr: blocks([qi], cr, (True,)), init)
        top = qi - 1 - qi % 2
        carry = lax.fori_loop(0, qi // 2, lambda pr, cr: blocks([top - 2 * pr, top - 1 - 2 * pr], cr, (False, False)),
                              carry)
        o_ref[...] = carry[0]
        for hh in range(2):
            tot_ref[:, hh * LANES:(hh + 1) * LANES] = jnp.broadcast_to(carry[1 + hh], (ATT_BLK, LANES))

    return pl.pallas_call(
        body, name="sb_attn_fwd", grid=(SB_HEADS // 2, nq),
        in_specs=[pl.BlockSpec((ATT_BLK, LANES), lambda p, i: (i, p)),
                  pl.BlockSpec((t, LANES), lambda p, i: (0, p)),
                  pl.BlockSpec((t, LANES), lambda p, i: (0, p))],
        out_specs=[pl.BlockSpec((ATT_BLK, LANES), lambda p, i: (i, p)),
                   pl.BlockSpec((ATT_BLK, 2 * LANES), lambda p, i: (i, p))],
        out_shape=[jax.ShapeDtypeStruct((t, SB_WIDTH), F32), jax.ShapeDtypeStruct((t, SB_HEADS * LANES), F32)],
        compiler_params=pltpu.CompilerParams(dimension_semantics=("arbitrary", "arbitrary")),
    )(q, k, v)


def _sb_bwd(q, k, v, tot, do):
    t = q.shape[0]
    nq = t // ATT_BLK
    scale = SB_HEAD_DIM ** -0.5

    def body(q_ref, k_ref, v_ref, tot_ref, do_ref, dq_ref, dk_ref, dv_ref):
        qi = pl.program_id(1)

        @pl.when(qi == 0)
        def _():
            dk_ref[...] = jnp.zeros_like(dk_ref)
            dv_ref[...] = jnp.zeros_like(dv_ref)

        lane = lax.broadcasted_iota(jnp.int32, (ATT_BLK, LANES), 1)
        tri_incl = _tri(lambda r, c: r <= c)
        tri_lt = _tri(lambda r, c: r < c)
        qv = q_ref[...] * scale
        dov = do_ref[...]
        heads = [(lane // SB_HEAD_DIM) == hh for hh in range(2)]
        qms = [jnp.where(mine, qv, 0.0).astype(BF16) for mine in heads]
        doms = [jnp.where(mine, dov, 0.0).astype(BF16) for mine in heads]
        tots = [tot_ref[:, hh * LANES:hh * LANES + 1] for hh in range(2)]

        def blocks(kbs, carry, diagonal):
            dq = carry[0]
            nb = len(kbs)
            chains = [(b, hh) for b in range(nb) for hh in range(2)]
            offs = [pl.multiple_of(kb * ATT_BLK, ATT_BLK) for kb in kbs]
            k_blks = [k_ref[pl.ds(off, ATT_BLK), :] for off in offs]
            vvs = [v_ref[pl.ds(off, ATT_BLK), :].astype(BF16) for off in offs]
            if any(diagonal):
                valid = (lax.broadcasted_iota(jnp.int32, (ATT_BLK, ATT_BLK), 1)
                         < lax.broadcasted_iota(jnp.int32, (ATT_BLK, ATT_BLK), 0))
            kks = {(b, hh): jnp.where(heads[hh], k_blks[b], 0.0).astype(BF16) for b, hh in chains}
            zs = {ch: _nt(qms[ch[1]], kks[ch]) for ch in chains}
            dws = {ch: _nt(doms[ch[1]], vvs[ch[0]]) for ch in chains}
            logs = {ch: _sb_logs(zs[ch]) for ch in chains}
            lbs = {ch: logs[ch][0] for ch in chains}
            l1m_all = {ch: logs[ch][1] for ch in chains}
            l1ms = {ch: jnp.where(valid, l1m_all[ch], 0.0) if diagonal[ch[0]] else l1m_all[ch] for ch in chains}
            pre, c_de = {}, {}
            for hh in range(2):
                pre[(0, hh)], c_de[(0, hh)] = carry[1 + 2 * hh], carry[2 + 2 * hh]
            for b, hh in chains:
                pre[(b + 1, hh)] = pre[(b, hh)] + jnp.sum(l1ms[(b, hh)], axis=-1, keepdims=True)
            prefix = {ch: _hi_lo_dot(l1ms[ch], tri_incl) for ch in chains}
            ws = {ch: jnp.exp(lbs[ch] + (tots[ch[1]] - (prefix[ch] + pre[ch]))) for ch in chains}
            ws = {ch: jnp.where(valid, ws[ch], 0.0) if diagonal[ch[0]] else ws[ch] for ch in chains}
            d_es = {ch: ws[ch] * dws[ch] for ch in chains}
            for b, hh in chains:
                c_de[(b + 1, hh)] = c_de[(b, hh)] + jnp.sum(d_es[(b, hh)], axis=-1, keepdims=True)
            dvs = [_tn(ws[(b, 0)].astype(BF16), doms[0]) + _tn(ws[(b, 1)].astype(BF16), doms[1]) for b in range(nb)]
            dl1ms = {ch: jnp.dot(d_es[ch].astype(BF16), tri_lt, preferred_element_type=F32) + c_de[ch] for ch in chains}
            dzs = {ch: d_es[ch] * jnp.exp(l1m_all[ch]) - dl1ms[ch] * jnp.exp(lbs[ch]) for ch in chains}
            dzs = {ch: jnp.where(valid, dzs[ch], 0.0) if diagonal[ch[0]] else dzs[ch] for ch in chains}
            dzs = {ch: dzs[ch].astype(BF16) for ch in chains}
            for ch in chains:
                dq = dq + jnp.dot(dzs[ch], kks[ch], preferred_element_type=F32)
            for b in range(nb):
                dk_ref[pl.ds(offs[b], ATT_BLK), :] += _tn(dzs[(b, 0)], qms[0]) + _tn(dzs[(b, 1)], qms[1])
                dv_ref[pl.ds(offs[b], ATT_BLK), :] += dvs[b]
            return (dq, pre[(nb, 0)], c_de[(nb, 0)], pre[(nb, 1)], c_de[(nb, 1)])

        zero = jnp.zeros((ATT_BLK, 1), F32)
        carry = lax.fori_loop(0, qi // 2, lambda pr, cr: blocks([2 * pr, 2 * pr + 1], cr, (False, False)),
                              (jnp.zeros((ATT_BLK, LANES), F32), zero, zero, zero, zero))
        carry = lax.cond(qi % 2 == 1, lambda cr: blocks([qi - 1, qi], cr, (False, True)),
                         lambda cr: blocks([qi], cr, (True,)), carry)
        dq_ref[...] = carry[0] * scale

    return pl.pallas_call(
        body, name="sb_attn_bwd", grid=(SB_HEADS // 2, nq),
        in_specs=[pl.BlockSpec((ATT_BLK, LANES), lambda p, i: (i, p)),
                  pl.BlockSpec((t, LANES), lambda p, i: (0, p)),
                  pl.BlockSpec((t, LANES), lambda p, i: (0, p)),
                  pl.BlockSpec((ATT_BLK, 2 * LANES), lambda p, i: (i, p)),
                  pl.BlockSpec((ATT_BLK, LANES), lambda p, i: (i, p))],
        out_specs=[pl.BlockSpec((ATT_BLK, LANES), lambda p, i: (i, p)),
                   pl.BlockSpec((t, LANES), lambda p, i: (0, p)),
                   pl.BlockSpec((t, LANES), lambda p, i: (0, p))],
        out_shape=[jax.ShapeDtypeStruct((t, SB_WIDTH), F32)] * 3,
        compiler_params=pltpu.CompilerParams(dimension_semantics=("arbitrary", "arbitrary")),
    )(q, k, v, tot, do)


@jax.custom_vjp
def _sb_attention(q, k, v):
    return _sb_fwd(q, k, v)[0]


def _sb_attention_fwd(q, k, v):
    o, tot = _sb_fwd(q, k, v)
    return o, (q, k, v, tot)


def _sb_attention_bwd(res, do):
    return tuple(_sb_bwd(*res, do))


_sb_attention.defvjp(_sb_attention_fwd, _sb_attention_bwd)


def _mla_fwd(qn, qr, kn, kr, v):
    t = qn.shape[0]
    nq = t // ATT_BLK
    scale = MLA_QK ** -0.5

    def body(qn_ref, qr_ref, kn_ref, kr_ref, v_ref, o_ref, lse_ref):
        qi = pl.program_id(1)
        lanes = [slice(hh * LANES, (hh + 1) * LANES) for hh in range(2)]
        qnb = [qn_ref[:, sl].astype(BF16) for sl in lanes]
        qrb = [qr_ref[:, sl].astype(BF16) for sl in lanes]

        def blocks(kbs, carry, diagonal):
            nb = len(kbs)
            chains = [(b, hh) for b in range(nb) for hh in range(2)]
            offs = [pl.multiple_of(kb * ATT_BLK, ATT_BLK) for kb in kbs]
            krbs = [kr_ref[pl.ds(off, ATT_BLK), :].astype(BF16) for off in offs]
            accs, ms, ls = [carry[0], carry[3]], [carry[1], carry[4]], [carry[2], carry[5]]
            ss = {(b, hh): (_nt(qnb[hh], kn_ref[pl.ds(offs[b], ATT_BLK), lanes[hh]].astype(BF16))
                            + _nt(qrb[hh], krbs[b])) * scale for b, hh in chains}
            if any(diagonal):
                causal = (lax.broadcasted_iota(jnp.int32, (ATT_BLK, ATT_BLK), 1)
                          <= lax.broadcasted_iota(jnp.int32, (ATT_BLK, ATT_BLK), 0))
                ss = {ch: jnp.where(causal, ss[ch], -jnp.inf) if diagonal[ch[0]] else ss[ch] for ch in chains}
            m_new = list(ms)
            for b, hh in chains:
                m_new[hh] = jnp.maximum(m_new[hh], jnp.max(ss[(b, hh)], axis=-1, keepdims=True))
            ps = {(b, hh): jnp.exp(ss[(b, hh)] - m_new[hh]) for b, hh in chains}
            alphas = [jnp.exp(ms[hh] - m_new[hh]) for hh in range(2)]
            pvs = {(b, hh): jnp.dot(ps[(b, hh)].astype(BF16), v_ref[pl.ds(offs[b], ATT_BLK), lanes[hh]].astype(BF16),
                                    preferred_element_type=F32) for b, hh in chains}
            out = []
            for hh in range(2):
                acc, l = accs[hh] * alphas[hh], ls[hh] * alphas[hh]
                for b in range(nb):
                    acc, l = acc + pvs[(b, hh)], l + jnp.sum(ps[(b, hh)], axis=-1, keepdims=True)
                out += [acc, m_new[hh], l]
            return tuple(out)

        init = (jnp.zeros((ATT_BLK, LANES), F32), jnp.full((ATT_BLK, 1), -jnp.inf, F32), jnp.zeros((ATT_BLK, 1), F32))
        carry = lax.cond(qi % 2 == 1, lambda cr: blocks([qi, qi - 1], cr, (True, False)),
                         lambda cr: blocks([qi], cr, (True,)), init + init)
        carry = lax.fori_loop(0, qi // 2, lambda pr, cr: blocks([2 * pr, 2 * pr + 1], cr, (False, False)), carry)
        for hh in range(2):
            acc, m, l = carry[3 * hh:3 * hh + 3]
            o_ref[:, lanes[hh]] = acc / l
            lse_ref[:, lanes[hh]] = jnp.broadcast_to(m + jnp.log(l), (ATT_BLK, LANES))

    blk = pl.BlockSpec((ATT_BLK, 2 * LANES), lambda p, i: (i, p))
    full = pl.BlockSpec((t, 2 * LANES), lambda p, i: (0, p))
    return pl.pallas_call(
        body, name="mla_attn_fwd", grid=(MLA_HEADS // 2, nq),
        in_specs=[blk, blk, full, pl.BlockSpec((t, LANES), lambda p, i: (0, 0)), full],
        out_specs=[blk, blk],
        out_shape=[jax.ShapeDtypeStruct((t, MLA_HEADS * LANES), F32)] * 2,
        compiler_params=pltpu.CompilerParams(dimension_semantics=("arbitrary", "arbitrary")),
    )(qn, qr, kn, kr, v)


def _mla_bwd(qn, qr, kn, kr, v, o, lse, do):
    t = qn.shape[0]
    nq = t // ATT_BLK
    scale = MLA_QK ** -0.5

    def body(qn_ref, qr_ref, kn_ref, kr_ref, v_ref, o_ref, lse_ref, do_ref,
             dqn_ref, dqr_ref, dkn_ref, dkr_ref, dv_ref):
        pair = pl.program_id(0)
        qi = pl.program_id(1)

        @pl.when(qi == 0)
        def _():
            dkn_ref[...] = jnp.zeros_like(dkn_ref)
            dv_ref[...] = jnp.zeros_like(dv_ref)

        @pl.when((qi == 0) & (pair == 0))
        def _():
            dkr_ref[...] = jnp.zeros_like(dkr_ref)

        lanes = [slice(hh * LANES, (hh + 1) * LANES) for hh in range(2)]
        qnb = [qn_ref[:, sl].astype(BF16) for sl in lanes]
        qrb = [qr_ref[:, sl].astype(BF16) for sl in lanes]
        dob = [do_ref[:, sl].astype(BF16) for sl in lanes]
        delta = [jnp.sum(do_ref[:, sl] * o_ref[:, sl], axis=-1, keepdims=True) for sl in lanes]
        lse_v = [lse_ref[:, hh * LANES:hh * LANES + 1] for hh in range(2)]

        def blocks(kbs, carry, diagonal):
            nb = len(kbs)
            chains = [(b, hh) for b in range(nb) for hh in range(2)]
            offs = [pl.multiple_of(kb * ATT_BLK, ATT_BLK) for kb in kbs]
            krbs = [kr_ref[pl.ds(off, ATT_BLK), :].astype(BF16) for off in offs]
            knb = {(b, hh): kn_ref[pl.ds(offs[b], ATT_BLK), lanes[hh]].astype(BF16) for b, hh in chains}
            vb = {(b, hh): v_ref[pl.ds(offs[b], ATT_BLK), lanes[hh]].astype(BF16) for b, hh in chains}
            ss = {(b, hh): _nt(qnb[hh], knb[(b, hh)]) + _nt(qrb[hh], krbs[b]) for b, hh in chains}
            dps = {(b, hh): _nt(dob[hh], vb[(b, hh)]) for b, hh in chains}
            ps = {(b, hh): jnp.exp(ss[(b, hh)] * scale - lse_v[hh]) for b, hh in chains}
            if any(diagonal):
                causal = (lax.broadcasted_iota(jnp.int32, (ATT_BLK, ATT_BLK), 1)
                          <= lax.broadcasted_iota(jnp.int32, (ATT_BLK, ATT_BLK), 0))
                ps = {ch: jnp.where(causal, ps[ch], 0.0) if diagonal[ch[0]] else ps[ch] for ch in chains}
            dss = {(b, hh): (ps[(b, hh)] * (dps[(b, hh)] - delta[hh]) * scale).astype(BF16) for b, hh in chains}
            for b, hh in chains:
                dv_ref[pl.ds(offs[b], ATT_BLK), lanes[hh]] += _tn(ps[(b, hh)].astype(BF16), dob[hh])
            for b, hh in chains:
                dkn_ref[pl.ds(offs[b], ATT_BLK), lanes[hh]] += _tn(dss[(b, hh)], qnb[hh])
            for b in range(nb):
                dkr_ref[pl.ds(offs[b], ATT_BLK), :] += _tn(dss[(b, 0)], qrb[0]) + _tn(dss[(b, 1)], qrb[1])
            out = list(carry)
            for b, hh in chains:
                out[2 * hh] = out[2 * hh] + jnp.dot(dss[(b, hh)], knb[(b, hh)], preferred_element_type=F32)
                out[2 * hh + 1] = out[2 * hh + 1] + jnp.dot(dss[(b, hh)], krbs[b], preferred_element_type=F32)
            return tuple(out)

        zero = jnp.zeros((ATT_BLK, LANES), F32)
        carry = lax.fori_loop(0, qi // 2, lambda pr, cr: blocks([2 * pr, 2 * pr + 1], cr, (False, False)),
                              (zero, zero, zero, zero))
        carry = lax.cond(qi % 2 == 1, lambda cr: blocks([qi - 1, qi], cr, (False, True)),
                         lambda cr: blocks([qi], cr, (True,)), carry)
        for hh in range(2):
            dqn_ref[:, lanes[hh]] = carry[2 * hh]
            dqr_ref[:, lanes[hh]] = carry[2 * hh + 1]

    blk = pl.BlockSpec((ATT_BLK, 2 * LANES), lambda p, i: (i, p))
    full = pl.BlockSpec((t, 2 * LANES), lambda p, i: (0, p))
    shared = pl.BlockSpec((t, LANES), lambda p, i: (0, 0))
    wide = jax.ShapeDtypeStruct((t, MLA_HEADS * LANES), F32)
    return pl.pallas_call(
        body, name="mla_attn_bwd", grid=(MLA_HEADS // 2, nq),
        in_specs=[blk, blk, full, shared, full, blk, blk, blk],
        out_specs=[blk, blk, full, shared, full],
        out_shape=[wide, wide, wide, jax.ShapeDtypeStruct((t, LANES), F32), wide],
        compiler_params=pltpu.CompilerParams(dimension_semantics=("arbitrary", "arbitrary")),
    )(qn, qr, kn, kr, v, o, lse, do)


@jax.custom_vjp
def _mla_attention(qn, qr, kn, kr, v):
    return _mla_fwd(qn, qr, kn, kr, v)[0]


def _mla_attention_fwd(qn, qr, kn, kr, v):
    o, lse = _mla_fwd(qn, qr, kn, kr, v)
    return o, (qn, qr, kn, kr, v, o, lse)


def _mla_attention_bwd(res, do):
    return tuple(_mla_bwd(*res, do))


_mla_attention.defvjp(_mla_attention_fwd, _mla_attention_bwd)


def _ffn_in(h, wg, wu):
    t, k = h.shape
    n_sh, _, cc = wg.shape

    def body(h_ref, wg_ref, wu_ref, g_ref, u_ref, a_ref):
        hb = h_ref[...].astype(BF16)
        for j in range(n_sh):
            cols = slice(j * cc, (j + 1) * cc)
            g = jnp.dot(hb, wg_ref[j], preferred_element_type=F32)
            u = jnp.dot(hb, wu_ref[j], preferred_element_type=F32)
            g_ref[:, cols] = g.astype(BF16)
            u_ref[:, cols] = u.astype(BF16)
            a_ref[:, cols] = _f_swiglu(g, u)[0].astype(BF16)

    w_spec = pl.BlockSpec((n_sh, k, cc), lambda i: (0, 0, 0))
    o_spec = pl.BlockSpec((MM_ROW_TILE, n_sh * cc), lambda i: (i, 0))
    wide = jax.ShapeDtypeStruct((t, n_sh * cc), BF16)
    return pl.pallas_call(
        body, name="ffn_in_fwd", grid=(t // MM_ROW_TILE,),
        in_specs=[pl.BlockSpec((MM_ROW_TILE, k), lambda i: (i, 0)), w_spec, w_spec],
        out_specs=[o_spec, o_spec, o_spec],
        out_shape=[wide, wide, wide],
        compiler_params=pltpu.CompilerParams(dimension_semantics=("arbitrary",), vmem_limit_bytes=MM_VMEM_LIMIT),
    )(h, wg, wu)


def _ffn_mid_bwd(dy, wd, g, u):
    t, n = dy.shape
    n_sh, cc, _ = wd.shape

    def body(dy_ref, wd_ref, g_ref, u_ref, dg_ref, du_ref):
        d_act = _nt(dy_ref[...].astype(BF16), wd_ref[...])
        _, vjp = jax.vjp(_f_swiglu, g_ref[...].astype(F32), u_ref[...].astype(F32))
        dg, du = vjp((d_act,))
        dg_ref[...] = dg.astype(BF16)
        du_ref[...] = du.astype(BF16)

    blk = pl.BlockSpec((MM_ROW_TILE, cc), lambda j, i: (i, j))
    wide = jax.ShapeDtypeStruct((t, n_sh * cc), BF16)
    return pl.pallas_call(
        body, name="ffn_mid_bwd", grid=(n_sh, t // MM_ROW_TILE),
        in_specs=[pl.BlockSpec((MM_ROW_TILE, n), lambda j, i: (i, 0)),
                  pl.BlockSpec((None, cc, n), lambda j, i: (j, 0, 0)), blk, blk],
        out_specs=[blk, blk], out_shape=[wide, wide],
        compiler_params=pltpu.CompilerParams(dimension_semantics=("arbitrary", "arbitrary"),
                                             vmem_limit_bytes=MM_VMEM_LIMIT),
    )(dy, wd, g, u)


def _ffn_dh(dg, du, wg, wu):
    t = dg.shape[0]
    n_sh, k, cc = wg.shape

    def body(dg_ref, du_ref, wg_ref, wu_ref, o_ref):
        acc = jnp.zeros((MM_ROW_TILE, k), F32)
        for j in range(n_sh):
            cols = slice(j * cc, (j + 1) * cc)
            acc = acc + _nt(dg_ref[:, cols], wg_ref[j]) + _nt(du_ref[:, cols], wu_ref[j])
        o_ref[...] = acc

    blk = pl.BlockSpec((MM_ROW_TILE, n_sh * cc), lambda i: (i, 0))
    w_spec = pl.BlockSpec((n_sh, k, cc), lambda i: (0, 0, 0))
    return pl.pallas_call(
        body, name="ffn_dh", grid=(t // MM_ROW_TILE,),
        in_specs=[blk, blk, w_spec, w_spec],
        out_specs=pl.BlockSpec((MM_ROW_TILE, k), lambda i: (i, 0)),
        out_shape=jax.ShapeDtypeStruct((t, k), F32),
        compiler_params=pltpu.CompilerParams(dimension_semantics=("arbitrary",), vmem_limit_bytes=MM_VMEM_LIMIT),
    )(dg, du, wg, wu)


def _ffn_dw_in(h, dy, n_sh, name):
    t, k = h.shape
    cc = dy.shape[1] // n_sh
    tk = 512

    def body(h_ref, dy_ref, o_ref):
        o_ref[...] = _tn(h_ref[...].astype(BF16), dy_ref[...]).astype(BF16)

    return pl.pallas_call(
        body, name=name, grid=(n_sh, k // tk),
        in_specs=[pl.BlockSpec((t, tk), lambda j, i: (0, i)), pl.BlockSpec((t, cc), lambda j, i: (0, j))],
        out_specs=pl.BlockSpec((None, tk, cc), lambda j, i: (j, i, 0)),
        out_shape=jax.ShapeDtypeStruct((n_sh, k, cc), BF16),
        compiler_params=pltpu.CompilerParams(dimension_semantics=("arbitrary", "arbitrary"),
                                             vmem_limit_bytes=MM_VMEM_LIMIT),
    )(h, dy)


@jax.custom_vjp
def _ffn_block(h, wg, wu, wd):
    act = _ffn_in(h, wg, wu)[2]
    return _mm(act, wd.reshape(-1, wd.shape[2]), "nn", "ffn_down_fwd", MM_ROW_TILE, wd.shape[2])


def _ffn_block_fwd(h, wg, wu, wd):
    g, u, act = _ffn_in(h, wg, wu)
    y = _mm(act, wd.reshape(-1, wd.shape[2]), "nn", "ffn_down_fwd", MM_ROW_TILE, wd.shape[2])
    return y, (h, wg, wu, wd, g, u, act)


def _ffn_block_bwd(res, dy):
    h, wg, wu, wd, g, u, act = res
    dg, du = _ffn_mid_bwd(dy, wd, g, u)
    dh = _ffn_dh(dg, du, wg, wu)
    n_sh = wg.shape[0]
    dwg = _ffn_dw_in(h, dg, n_sh, "ffn_gate_dw")
    dwu = _ffn_dw_in(h, du, n_sh, "ffn_up_dw")
    dwd = _mm(act, dy, "tn", "ffn_down_dw", 256, wd.shape[2], out_dtype=BF16).reshape(wd.shape)
    return dh, dwg, dwu, dwd


_ffn_block.defvjp(_ffn_block_fwd, _ffn_block_bwd)


def _swap_halves(w):
    half = w.shape[-1] // 2
    return jnp.concatenate([w[..., half:], w[..., :half]], axis=-1)


def _pad_lanes(w):
    return jnp.concatenate([w, jnp.zeros(w.shape[:-1] + (LANES - w.shape[-1],), w.dtype)], axis=-1)


def _join_cols(shards):
    return shards.transpose(1, 0, 2).reshape(shards.shape[1], -1)


def _mod_parts(mod):
    return [mod[:, i * D_MODEL:(i + 1) * D_MODEL] for i in range(N_MOD)]


def _mixing_stage(x, mod, p, cos, sin):
    shift1, scale1 = _mod_parts(mod)[:2]

    w_in = _join_cols(p["w_in"])
    k_rope_w = w_in[:, 2176:2240]
    w_in_ext = jnp.concatenate([w_in[:, :2176], _pad_lanes(k_rope_w), _pad_lanes(_swap_halves(k_rope_w)),
                                jnp.zeros((D_MODEL, LANES), w_in.dtype)], axis=1)
    (h1,) = _make_rowwise("pre_attn", _f_pre_attn, 1, 3, [D_MODEL], [True], out_dtypes=[BF16])(
        x, p["norm_attn"], scale1, shift1)
    q_sb, k_sb, v_sb, cq, ckv, kr, kr_sw = _make_linear_split(
        "in_proj", (SB_WIDTH, SB_WIDTH, SB_WIDTH, MLA_Q_RANK, MLA_KV_RANK, LANES, LANES), 512)(h1, w_in_ext)

    o_sb = _sb_attention(q_sb, k_sb, v_sb)

    wq = _join_cols(p["w_q_up"]).reshape(MLA_Q_RANK, MLA_HEADS, MLA_QK)
    wq_n, wq_r = wq[:, :, :MLA_NOPE], wq[:, :, MLA_NOPE:]
    w_q_ext = jnp.concatenate([wq_n.reshape(MLA_Q_RANK, -1), _pad_lanes(wq_r).reshape(MLA_Q_RANK, -1),
                               _pad_lanes(_swap_halves(wq_r)).reshape(MLA_Q_RANK, -1)], axis=1)
    wkv = _join_cols(p["w_kv_up"]).reshape(MLA_KV_RANK, MLA_HEADS, MLA_NOPE + MLA_V)
    w_kv_ext = jnp.concatenate([wkv[:, :, :MLA_NOPE].reshape(MLA_KV_RANK, -1),
                                wkv[:, :, MLA_NOPE:].reshape(MLA_KV_RANK, -1)], axis=1)
    cqn, ckvn = _make_rowwise("mla_a", _f_mla_a, 2, 2, [MLA_Q_RANK, MLA_KV_RANK], [True, True],
                              out_dtypes=[BF16, BF16], grad_dtypes=[BF16, BF16])(
        cq, ckv, p["q_a_norm"], p["kv_a_norm"])
    qall = _make_linear("q_up", 384, 768)(cqn, w_q_ext)
    kn_all, v_mla = _make_linear_split("kv_up", (MLA_HEADS * MLA_NOPE, MLA_HEADS * MLA_V), MLA_KV_RANK)(ckvn, w_kv_ext)
    gq = p["q_norm"]
    gkr = p["k_rope_norm"]
    qn, qr, kn, krr = _make_rowwise("mla_b", _f_mla_b, 6, 6, [512, 512, 512, LANES],
                                    [True, True, True, True, False, False],
                                    out_dtypes=[BF16] * 4, grad_dtypes=[BF16] * 4)(
        qall, kn_all, kr, kr_sw, cos, sin,
        gq[:, :MLA_NOPE], _pad_lanes(gq[:, MLA_NOPE:]), _pad_lanes(_swap_halves(gq[:, MLA_NOPE:])),
        p["k_nope_norm"], _pad_lanes(gkr), _pad_lanes(_swap_halves(gkr)))
    o_mla = _mla_attention(qn, qr, kn, krr, v_mla)

    (mixed,) = _make_rowwise("post_attn", _f_post_attn, 2, 2, [D_MODEL], [True, True])(
        o_sb, o_mla, p["out_norm_sb"], p["out_norm_mla"])
    return mixed


def _ffn_stage(x, mixed, mod, p):
    _, _, gate1, shift2, scale2, _ = _mod_parts(mod)
    attn = _make_linear("out_proj", 512, 512)(mixed, p["w_out"].reshape(D_MODEL, D_MODEL))

    x2, h2 = _make_rowwise("pre_ffn", _f_pre_ffn, 2, 4, [D_MODEL, D_MODEL], [True, True],
                           out_dtypes=[F32, BF16], grad_dtypes=[F32, BF16])(
        x, attn, gate1, p["norm_ffn"], scale2, shift2)
    return x2, _ffn_block(h2, p["w_gate"], p["w_up"], p["w_down"])


def _my_place():
    return lax.axis_index("x"), lax.axis_index("y"), lax.axis_index("c")


def _small_gather(x_ref, out_ref, send_sems, recv_sems, base, local_sem):
    m_per = x_ref.shape[0]
    x, y, c = _my_place()
    me, sibling = (x, y, c), (x, y, 1 - c)
    chips = [(1 - x, y), (x, 1 - y), (1 - x, 1 - y)]

    def rows(px, py, pc):
        return out_ref.at[pl.ds((4 * px + 2 * py + pc) * m_per, m_per), :]

    def copy(k, blk, to, src=None):
        return _remote(rows(*blk) if src is None else src, rows(*blk), send_sems, recv_sems, base + k, to)

    mine = pltpu.make_async_copy(x_ref, rows(*me), local_sem)
    first = [copy(0, me, sibling, src=x_ref)] + [copy(1 + j, me, (*chip, c), src=x_ref) for j, chip in enumerate(chips)]
    passed = [copy(4 + j, (*chip, c), sibling) for j, chip in enumerate(chips)]

    def start():
        mine.start()
        for cp in first:
            cp.start()

    def finish():
        for j, chip in enumerate(chips):
            copy(1 + j, (*chip, c), me).wait_recv()
            passed[j].start()
        copy(0, sibling, me).wait_recv()
        for j, chip in enumerate(chips):
            copy(4 + j, (*chip, 1 - c), me).wait_recv()
        for cp in first + passed:
            cp.wait_send()
        mine.wait()

    return start, finish


def _all_gather_small(block, name):
    m_per, n = block.shape

    def body(x_ref, out_ref, send_sems, recv_sems, local_sem):
        start, finish = _small_gather(x_ref, out_ref, send_sems, recv_sems, 0, local_sem)
        start()
        finish()

    return pl.pallas_call(
        body, name=name,
        out_shape=jax.ShapeDtypeStruct((N_DEV * m_per, n), block.dtype),
        in_specs=[pl.BlockSpec(memory_space=pltpu.VMEM)],
        out_specs=pl.BlockSpec(memory_space=pltpu.VMEM),
        scratch_shapes=[pltpu.SemaphoreType.DMA((7,)), pltpu.SemaphoreType.DMA((7,)), pltpu.SemaphoreType.DMA],
    )(block)


EARLY = ("w_in", "w_q_up", "w_kv_up")
LATE = ("w_out", "w_gate", "w_up", "w_down")
BIG = EARLY + LATE
TRANSPOSED_UPDATE = ("w_in", "w_gate", "w_up")
HALF_AXIS = {"w_in": 0, "w_q_up": 0, "w_kv_up": 0, "w_out": 0, "w_gate": 0, "w_up": 0, "w_down": 1}


def _half(ref, h, axis, lead=()):
    trail = ref.shape[len(lead):]
    idx = list(lead) + [slice(None)] * len(trail)
    at = len(trail) - 2 + axis
    n2 = trail[at] // 2
    idx[len(lead) + at] = pl.ds(h * n2, n2)
    return ref.at[tuple(idx)]


def _half_shape(shape, axis):
    shape = list(shape)
    shape[len(shape) - 2 + axis] //= 2
    return tuple(shape)


def _remote(src, dst, send_sems, recv_sems, k, to):
    return pltpu.make_async_remote_copy(src_ref=src, dst_ref=dst, send_sem=send_sems.at[k],
                                        recv_sem=recv_sems.at[k], device_id=to, device_id_type=MESH)


def _gather_weights(names, shards, small_block):
    n_w = len(shards)
    axes = [HALF_AXIS[n] for n in names]

    def body(*refs):
        w_refs, small_ref = refs[:n_w], refs[n_w]
        out_refs, token, small_out = refs[n_w + 1:2 * n_w + 1], refs[2 * n_w + 1], refs[2 * n_w + 2]
        send_sems, recv_sems, local_sems = refs[2 * n_w + 3:]
        token[...] = jnp.zeros_like(token)
        x, y, c = _my_place()
        sibling = (x, y, 1 - c)
        chips = [(1 - x, y), (x, 1 - y), (1 - x, 1 - y)]
        me = 2 * x + y
        small_start, small_finish = _small_gather(small_ref, small_out, send_sems, recv_sems, 6 * n_w,
                                                  local_sems.at[n_w])
        small_start()
        mine = [pltpu.make_async_copy(w, o.at[me], local_sems.at[i]) for i, (w, o) in enumerate(zip(w_refs, out_refs))]
        for cp in mine:
            cp.start()
        first = [_remote(_half(w_refs[i], c, axes[i]), _half(out_refs[i], c, axes[i], (me,)),
                         send_sems, recv_sems, 6 * i + j, (*chip, c))
                 for i in range(n_w) for j, chip in enumerate(chips)]
        for cp in first:
            cp.start()
        small_finish()
        passed = []
        for j, (cx, cy) in enumerate(chips):
            for i in range(n_w):
                blk = _half(out_refs[i], c, axes[i], (2 * cx + cy,))
                _remote(blk, blk, send_sems, recv_sems, 6 * i + j, (cx, cy, c)).wait_recv()
                cp = _remote(blk, blk, send_sems, recv_sems, 6 * i + 3 + j, sibling)
                cp.start()
                passed.append(cp)
        for j, (cx, cy) in enumerate(chips):
            for i in range(n_w):
                blk = _half(out_refs[i], 1 - c, axes[i], (2 * cx + cy,))
                _remote(blk, blk, send_sems, recv_sems, 6 * i + 3 + j, sibling).wait_recv()
        for cp in first + passed:
            cp.wait_send()
        for cp in mine:
            cp.wait()

    outs = pl.pallas_call(
        body, name="gather_weights",
        out_shape=[jax.ShapeDtypeStruct((N_CHIPS,) + s.shape, s.dtype) for s in shards]
        + [jax.ShapeDtypeStruct((8, LANES), F32),
           jax.ShapeDtypeStruct((N_DEV * small_block.shape[0], small_block.shape[1]), small_block.dtype)],
        in_specs=[ANY] * (n_w + 1), out_specs=[ANY] * n_w + [pl.BlockSpec(memory_space=pltpu.VMEM), ANY],
        scratch_shapes=[pltpu.SemaphoreType.DMA((6 * n_w + 7,)), pltpu.SemaphoreType.DMA((6 * n_w + 7,)),
                        pltpu.SemaphoreType.DMA((n_w + 1,))],
    )(*shards, small_block)
    return outs[:n_w], outs[n_w], outs[n_w + 1]


def _pair_exchange(names, grads, call_name, small_block):
    n_w = len(grads)
    axes = [HALF_AXIS[n] for n in names]

    def body(*refs):
        g_refs, small_ref = refs[:n_w], refs[n_w]
        t_refs, small_out = refs[n_w + 1:2 * n_w + 1], refs[2 * n_w + 1]
        send_sems, recv_sems, local_sem = refs[2 * n_w + 2:]
        x, y, c = _my_place()
        small_start, small_finish = _small_gather(small_ref, small_out, send_sems, recv_sems, n_w, local_sem)
        small_start()
        sends = [_remote(_half(g_refs[i], 1 - c, axes[i]), t_refs[i], send_sems, recv_sems, i, (x, y, 1 - c))
                 for i in range(n_w)]
        for cp in sends:
            cp.start()
        small_finish()
        for cp in sends:
            cp.wait_recv()
        for cp in sends:
            cp.wait_send()

    outs = pl.pallas_call(
        body, name=call_name,
        out_shape=[jax.ShapeDtypeStruct(_half_shape(g.shape, a), g.dtype) for g, a in zip(grads, axes)]
        + [jax.ShapeDtypeStruct((N_DEV * small_block.shape[0], small_block.shape[1]), small_block.dtype)],
        in_specs=[ANY] * (n_w + 1), out_specs=[ANY] * (n_w + 1),
        scratch_shapes=[pltpu.SemaphoreType.DMA((n_w + 7,)), pltpu.SemaphoreType.DMA((n_w + 7,)),
                        pltpu.SemaphoreType.DMA],
    )(*grads, small_block)
    return outs[:n_w], outs[n_w]


def _sibling_join(halves, name, after):
    n_w = len(halves)

    def body(*refs):
        s_refs, j_refs = refs[:n_w], refs[n_w + 1:2 * n_w + 1]
        send_sems, recv_sems = refs[2 * n_w + 1:]
        x, y, c = _my_place()
        sends = [_remote(s_refs[i], j_refs[i], send_sems, recv_sems, i, (x, y, 1 - c)) for i in range(n_w)]
        for cp in sends:
            cp.start()
        for cp in sends:
            cp.wait_recv()
        for cp in sends:
            cp.wait_send()

    return pl.pallas_call(
        body, name=name,
        out_shape=[jax.ShapeDtypeStruct(s.shape, s.dtype) for s in halves],
        in_specs=[ANY] * (n_w + 1), out_specs=[ANY] * n_w,
        scratch_shapes=[pltpu.SemaphoreType.DMA((n_w,)), pltpu.SemaphoreType.DMA((n_w,))],
    )(*halves, after)


HBM_SPEC = pl.BlockSpec(memory_space=pltpu.HBM)
SEM_SPEC = pl.BlockSpec(memory_space=pltpu.SEMAPHORE)
DATAFLOW = pltpu.SideEffectType.DATAFLOW_SIDE_EFFECTING


def _in_hbm(a):
    return pltpu.with_memory_space_constraint(a, pltpu.HBM)


def _exchange_start(name, srcs, lands, plan, n_copies, after, thru):
    n = len(srcs)

    def body(*refs):
        src_refs, land_refs = refs[:n], refs[n:2 * n]
        send_sems, recv_sems = refs[2 * n + 2], refs[2 * n + 3]
        for k, (src, dst, to, k_recv) in enumerate(plan(src_refs, land_refs)):
            pltpu.make_async_remote_copy(src_ref=src, dst_ref=dst, send_sem=send_sems.at[k],
                                         recv_sem=recv_sems.at[k_recv], device_id=to, device_id_type=MESH).start()

    outs = pl.pallas_call(
        body, name=name,
        out_shape=(pltpu.SemaphoreType.DMA((n_copies,)), pltpu.SemaphoreType.DMA((n_copies,)),
                   *[pltpu.HBM(a.shape, a.dtype) for a in list(srcs) + list(lands) + [thru]]),
        in_specs=[HBM_SPEC] * (2 * n + 1) + [ANY],
        out_specs=(SEM_SPEC, SEM_SPEC, *[HBM_SPEC] * (2 * n + 1)),
        input_output_aliases={i: 2 + i for i in range(2 * n + 1)},
        compiler_params=pltpu.CompilerParams(has_side_effects=DATAFLOW),
    )(*[_in_hbm(a) for a in list(srcs) + list(lands) + [thru]], after)
    return outs[0], outs[1], outs[2:2 + n], outs[2 + n:2 + 2 * n], outs[2 + 2 * n]


def _exchange_wait(name, started, plan, after):
    send_sems, recv_sems, srcs, lands, _ = started
    n = len(srcs)

    def body(*refs):
        src_refs, land_refs = refs[:n], refs[n:2 * n]
        s_sems, r_sems = refs[2 * n], refs[2 * n + 1]
        for k, (src, dst, to, _) in enumerate(plan(src_refs, land_refs)):
            cp = _remote(src, dst, s_sems, r_sems, k, to)
            cp.wait_send()
            cp.wait_recv()

    outs = pl.pallas_call(
        body, name=name,
        out_shape=tuple(pltpu.HBM(a.shape, a.dtype) for a in list(srcs) + list(lands)),
        in_specs=[HBM_SPEC] * (2 * n) + [SEM_SPEC, SEM_SPEC, ANY],
        out_specs=tuple([HBM_SPEC] * (2 * n)),
        input_output_aliases={i: i for i in range(2 * n)},
        compiler_params=pltpu.CompilerParams(has_side_effects=DATAFLOW),
    )(*srcs, *lands, send_sems, recv_sems, after)
    return outs[:n], outs[n:]


def _late_gather_plan(src_refs, land_refs):
    x, y, c = _my_place()
    chips = [(1 - x, y), (x, 1 - y), (1 - x, 1 - y)]
    plan = [(src, land.at[2 * x + y], (cx, cy, c)) for src, land in zip(src_refs, land_refs) for cx, cy in chips]
    return [entry + (k,) for k, entry in enumerate(plan)]


def _late_scatter_plan(src_refs, land_refs):
    x, y, c = _my_place()
    chips = [(1 - x, y), (x, 1 - y), (1 - x, 1 - y)]
    plan = [(src.at[2 * cx + cy], land.at[j], (cx, cy, c))
            for src, land in zip(src_refs, land_refs) for j, (cx, cy) in enumerate(chips)]
    return [entry + (k,) for k, entry in enumerate(plan)]


def _direct_scatter_plan(names):
    axes = [HALF_AXIS[n] for n in names]

    def plan(src_refs, land_refs):
        x, y, c = _my_place()
        chips = [(1 - x, y), (x, 1 - y), (1 - x, 1 - y)]
        out = []
        for i, (src, land) in enumerate(zip(src_refs, land_refs)):
            for f, (cx, cy) in enumerate(chips):
                for core in range(2):
                    out.append((_half(src, core, axes[i], (2 * cx + cy,)), land.at[2 * f + c], (cx, cy, core),
                                7 * i + 2 * f + c))
            out.append((_half(src, 1 - c, axes[i], (2 * x + y,)), land.at[6], (x, y, 1 - c), 7 * i + 6))
        return out

    return plan


def _row_tile(rows, mult=16, limit=ROW_TILE):
    return max(d for d in range(mult, limit + 1, mult) if rows % d == 0)


def _pair_sum(place, g, theirs, axis, name):
    nj, rr, cc = theirs.shape
    tr = _row_tile(rr, limit=1024)
    nb = rr // tr
    if axis == 0:
        g_map = lambda j, i, pr: (j, pr[0] * nb + i, 0)
    else:
        g_map = lambda j, i, pr: (j, i, pr[0])

    def body(pr, g_ref, t_ref, o_ref):
        o_ref[...] = (g_ref[...].astype(F32) + t_ref[...].astype(F32)).astype(BF16)

    spec = pl.BlockSpec((None, tr, cc), lambda j, i, pr: (j, i, 0))
    return pl.pallas_call(
        body, name=name,
        grid_spec=pltpu.PrefetchScalarGridSpec(
            num_scalar_prefetch=1, grid=(nj, nb),
            in_specs=[pl.BlockSpec((None, tr, cc), g_map), spec], out_specs=spec),
        out_shape=jax.ShapeDtypeStruct(theirs.shape, BF16))(place, g, theirs)


def _chip_sum(place, pair_sums, parts, name, transposed):
    _, rr, cc = parts.shape
    tr = _row_tile(rr, LANES) if transposed else _row_tile(rr, limit=1024)

    def body(pr, h_ref, p_ref, o_ref):
        acc = p_ref[0].astype(F32)
        for j in range(1, N_CHIPS - 1):
            acc = acc + p_ref[j].astype(F32)
        acc = acc + h_ref[...].astype(F32)
        o_ref[...] = (acc.T if transposed else acc).astype(BF16)

    out_spec = pl.BlockSpec((cc, tr), lambda i, pr: (0, i)) if transposed else pl.BlockSpec((tr, cc), lambda i, pr: (i, 0))
    return pl.pallas_call(
        body, name=name,
        grid_spec=pltpu.PrefetchScalarGridSpec(
            num_scalar_prefetch=1, grid=(rr // tr,),
            in_specs=[pl.BlockSpec((None, tr, cc), lambda i, pr: (pr[1], i, 0)),
                      pl.BlockSpec((N_CHIPS - 1, tr, cc), lambda i, pr: (0, i, 0))],
            out_specs=out_spec),
        out_shape=jax.ShapeDtypeStruct((cc, rr) if transposed else (rr, cc), BF16))(place, pair_sums, parts)


def _chip_sum_direct(place, g, parts, axis, name, transposed):
    n_parts, rr, cc = parts.shape
    tr = _row_tile(rr, LANES) if transposed else _row_tile(rr, limit=1024)
    nb = rr // tr
    if axis == 0:
        g_map = lambda i, pr: (pr[1], pr[0] * nb + i, 0)
    else:
        g_map = lambda i, pr: (pr[1], i, pr[0])

    def body(pr, g_ref, p_ref, o_ref):
        acc = p_ref[0].astype(F32)
        for j in range(1, n_parts):
            acc = acc + p_ref[j].astype(F32)
        acc = acc + g_ref[...].astype(F32)
        o_ref[...] = (acc.T if transposed else acc).astype(BF16)

    out_spec = pl.BlockSpec((cc, tr), lambda i, pr: (0, i)) if transposed else pl.BlockSpec((tr, cc), lambda i, pr: (i, 0))
    return pl.pallas_call(
        body, name=name,
        grid_spec=pltpu.PrefetchScalarGridSpec(
            num_scalar_prefetch=1, grid=(nb,),
            in_specs=[pl.BlockSpec((None, tr, cc), g_map), pl.BlockSpec((n_parts, tr, cc), lambda i, pr: (0, i, 0))],
            out_specs=out_spec),
        out_shape=jax.ShapeDtypeStruct((cc, rr) if transposed else (rr, cc), BF16))(place, g, parts)


def _silu(v):
    return v / (1.0 + jnp.exp(-v))


def _ada_fwd(c_all, w_shard, b_shard):
    n_seq, n_cols = c_all.shape[0], w_shard.shape[1]

    def body(c_ref, w_ref, b_ref, o_ref, mine_ref, send_sems, recv_sems, local_sem):
        mine_ref[...] = jnp.dot(_silu(c_ref[...]), w_ref[...], precision=lax.Precision.HIGHEST,
                                preferred_element_type=F32) + b_ref[...]
        start, finish = _small_gather(mine_ref, o_ref, send_sems, recv_sems, 0, local_sem)
        start()
        finish()

    return pl.pallas_call(
        body, name="ada_fwd", out_shape=jax.ShapeDtypeStruct((N_DEV * n_seq, n_cols), F32),
        scratch_shapes=[pltpu.VMEM((n_seq, n_cols), F32), pltpu.SemaphoreType.DMA((7,)), pltpu.SemaphoreType.DMA((7,)),
                        pltpu.SemaphoreType.DMA],
        compiler_params=pltpu.CompilerParams(vmem_limit_bytes=MM_VMEM_LIMIT))(c_all, w_shard, b_shard)


def _loss_and_grads(x2, ffn, target, gate):
    t, d = x2.shape

    def half_loss(x2_blk, ffn_blk, gate_row, target_blk):
        return 0.5 * _f_loss(x2_blk, ffn_blk, target_blk, gate_row)[0]

    def body(x2_ref, ffn_ref, tgt_ref, gate_ref, loss_ref, dx2_ref, dffn_ref, dgate_ref):
        rows, vjp = jax.vjp(lambda a, b, g: half_loss(a, b, g, tgt_ref[...]), x2_ref[...], ffn_ref[...], gate_ref[...])
        loss_ref[...] = rows
        dx2_ref[...], dffn_ref[...], dgate = vjp(jnp.ones_like(rows))

        @pl.when(pl.program_id(0) == 0)
        def _():
            dgate_ref[...] = jnp.zeros_like(dgate_ref)

        dgate_ref[...] += dgate

    blk = pl.BlockSpec((ROW_TILE, d), lambda i: (i, 0))
    row = pl.BlockSpec((1, d), lambda i: (0, 0))
    return pl.pallas_call(
        body, name="loss_and_grads", grid=(t // ROW_TILE,),
        in_specs=[blk, blk, blk, row],
        out_specs=[pl.BlockSpec((ROW_TILE, 1), lambda i: (i, 0)), blk, blk, row],
        out_shape=[jax.ShapeDtypeStruct((t, 1), F32), jax.ShapeDtypeStruct((t, d), F32), jax.ShapeDtypeStruct((t, d), F32),
                   jax.ShapeDtypeStruct((1, d), F32)],
        compiler_params=pltpu.CompilerParams(dimension_semantics=("arbitrary",), vmem_limit_bytes=MM_VMEM_LIMIT),
    )(x2, ffn, target, gate)


def _ada_bwd(c_all, dmod_cols):
    def body(c_ref, d_ref, o_ref):
        o_ref[...] = lax.dot_general(_silu(c_ref[...]), d_ref[...], (((0,), (0,)), ((), ())),
                                     precision=lax.Precision.HIGHEST, preferred_element_type=F32)

    return pl.pallas_call(body, name="ada_bwd", out_shape=jax.ShapeDtypeStruct((c_all.shape[1], dmod_cols.shape[1]), F32),
                          compiler_params=pltpu.CompilerParams(vmem_limit_bytes=MM_VMEM_LIMIT))(c_all, dmod_cols)


def _adamw_math(w, g, m, v):
    m = ADAM_B1 * m + (1.0 - ADAM_B1) * g
    v = ADAM_B2 * v + (1.0 - ADAM_B2) * (g * g)
    m_hat = m / (1.0 - ADAM_B1 ** ADAM_STEP)
    v_hat = v / (1.0 - ADAM_B2 ** ADAM_STEP)
    delta = -ADAM_LR * (m_hat / (jnp.sqrt(v_hat) + ADAM_EPS) + ADAM_WD * w)
    return delta, m, v


def _adamw(w, g, m, v, name):
    r, ccols = w.shape
    tr = max(d for d in range(8, ROW_TILE + 1, 8) if r % d == 0)
    spec = pl.BlockSpec((tr, ccols), lambda i: (i, 0))

    def body(w_ref, g_ref, m_ref, v_ref, d_ref, nm_ref, nv_ref):
        d_ref[...], nm_ref[...], nv_ref[...] = _adamw_math(w_ref[...], g_ref[...], m_ref[...], v_ref[...])

    return pl.pallas_call(body, name=name, grid=(r // tr,), in_specs=[spec] * 4, out_specs=[spec] * 3,
                          out_shape=[jax.ShapeDtypeStruct(w.shape, F32)] * 3,
                          compiler_params=pltpu.CompilerParams(vmem_limit_bytes=MM_VMEM_LIMIT))(w, g, m, v)


def _small_layout(sizes):
    offs, off = [], 0
    for n in sizes:
        offs.append(off)
        off += -(-n // LANES) * LANES
    total = -(-(off + LANES) // (8 * LANES)) * (8 * LANES)
    return offs, off, total


def _adamw_small(ws, g_all, ms, vs, offs, loss_off):
    n_p = len(ws)

    def device_sum(g_ref, off, width):
        blk = g_ref[:, off:off + width]
        acc = blk[0:1]
        for d in range(1, N_DEV):
            acc = acc + blk[d:d + 1]
        return acc

    def body(*refs):
        w_refs, m_refs, v_refs = refs[:n_p], refs[n_p:2 * n_p], refs[2 * n_p:3 * n_p]
        g_ref = refs[3 * n_p]
        outs = refs[3 * n_p + 1:]
        for i in range(n_p):
            n = w_refs[i].shape[1]
            g = device_sum(g_ref, offs[i], -(-n // LANES) * LANES)[:, :n]
            outs[i][...] = g
            outs[n_p + i][...], outs[2 * n_p + i][...], outs[3 * n_p + i][...] = _adamw_math(
                w_refs[i][...], g, m_refs[i][...], v_refs[i][...])
        outs[4 * n_p][...] = device_sum(g_ref, loss_off, LANES)

    res = pl.pallas_call(
        body, name="adamw_small",
        out_shape=[jax.ShapeDtypeStruct(a.shape, F32) for a in list(ws) * 4] + [jax.ShapeDtypeStruct((1, LANES), F32)],
    )(*ws, *ms, *vs, g_all)
    return res[:n_p], res[n_p:2 * n_p], res[2 * n_p:3 * n_p], res[3 * n_p:4 * n_p], res[4 * n_p]


def _adamw_halves(place, w, own, sib, m, v, axis, name, after):
    r, cc = w.shape
    if axis == 0:
        rows, gc = own.shape[0], own.shape[1]
        tr = _row_tile(rows)
        nb = rows // tr
        w_spec = pl.BlockSpec((tr, cc), lambda h, i, pr: (h * nb + i, 0))
        g_spec = pl.BlockSpec((tr, gc), lambda h, i, pr: (i, 0))
    else:
        tr = _row_tile(r)
        nb = r // tr
        gc = own.shape[1]
        w_spec = pl.BlockSpec((tr, gc), lambda h, i, pr: (i, h))
        g_spec = pl.BlockSpec((tr, gc), lambda h, i, pr: (i, 0))
    wc = w_spec.block_shape[1]

    def body(pr, w_ref, o_ref, s_ref, m_ref, v_ref, after_ref, g_ref, d_ref, nm_ref, nv_ref):
        g = jnp.where(pl.program_id(0) == pr[0], o_ref[...], s_ref[...]).astype(F32)[:, :wc]
        g_ref[...] = g
        d_ref[...], nm_ref[...], nv_ref[...] = _adamw_math(w_ref[...], g, m_ref[...], v_ref[...])

    return pl.pallas_call(
        body, name=name,
        grid_spec=pltpu.PrefetchScalarGridSpec(
            num_scalar_prefetch=1, grid=(2, nb),
            in_specs=[w_spec, g_spec, g_spec, w_spec, w_spec, ANY], out_specs=[w_spec] * 4),
        out_shape=[jax.ShapeDtypeStruct(w.shape, F32)] * 4,
        compiler_params=pltpu.CompilerParams(vmem_limit_bytes=MM_VMEM_LIMIT))(place, w, own, sib, m, v, after)


SMALL = ("b_ada", "norm_attn", "norm_ffn", "q_a_norm", "kv_a_norm", "q_norm", "k_nope_norm", "k_rope_norm",
         "out_norm_sb", "out_norm_mla")
WEIGHTS = ("w_ada", "b_ada", "norm_attn", "norm_ffn", "w_in", "q_a_norm", "w_q_up", "kv_a_norm", "w_kv_up",
           "q_norm", "k_nope_norm", "k_rope_norm", "out_norm_sb", "out_norm_mla", "w_out", "w_gate", "w_up",
           "w_down")


def kernel(x, c, positions, w_ada, b_ada, norm_attn, norm_ffn, w_in, q_a_norm, w_q_up, kv_a_norm, w_kv_up, q_norm, k_nope_norm, k_rope_norm, out_norm_sb, out_norm_mla, w_out, w_gate, w_up, w_down, loss_target, m_w_ada, m_b_ada, m_norm_attn, m_norm_ffn, m_w_in, m_q_a_norm, m_w_q_up, m_kv_a_norm, m_w_kv_up, m_q_norm, m_k_nope_norm, m_k_rope_norm, m_out_norm_sb, m_out_norm_mla, m_w_out, m_w_gate, m_w_up, m_w_down, v_w_ada, v_b_ada, v_norm_attn, v_norm_ffn, v_w_in, v_q_a_norm, v_w_q_up, v_kv_a_norm, v_w_kv_up, v_q_norm, v_k_nope_norm, v_k_rope_norm, v_out_norm_sb, v_out_norm_mla, v_w_out, v_w_gate, v_w_up, v_w_down):
    local = dict(locals())
    w = {n: local[n][0] for n in WEIGHTS}
    m = {n: local["m_" + n][0] for n in WEIGHTS}
    v = {n: local["v_" + n][0] for n in WEIGHTS}
    small = {n: w[n].reshape(1, -1) for n in SMALL}
    ix, iy, ic = _my_place()
    chip = 2 * ix + iy
    dev = 2 * chip + ic
    xs, target = x[0], loss_target[0]
    seq = xs.shape[0]

    ff_pad = FF_SHARD_PAD - FF_SHARD
    pads = {"w_gate": ((0, 0), (0, ff_pad)), "w_up": ((0, 0), (0, ff_pad)), "w_down": ((0, ff_pad), (0, 0))}
    shards = {n: jnp.pad(w[n].astype(BF16), pads[n]) if n in pads else w[n].astype(BF16) for n in BIG}
    early, early_done, c_gathered = _gather_weights(EARLY, [shards[n] for n in EARLY], c.reshape(8, LANES))
    gathered = dict(zip(EARLY, early))

    c_all = c_gathered.reshape(N_DEV, D_MODEL)
    ada_cols = w["w_ada"].shape[1]
    b_cols = lax.dynamic_slice_in_dim(small["b_ada"], chip * ada_cols, ada_cols, axis=1)
    mod_all = _ada_fwd(c_all, w["w_ada"], b_cols).reshape(N_CHIPS, 2, N_DEV, ada_cols)
    mod = lax.dynamic_index_in_dim(mod_all[:, 0], dev, axis=1, keepdims=False).reshape(1, N_MOD * D_MODEL)

    lands = [lax.dynamic_update_index_in_dim(lax.empty((N_CHIPS,) + shards[n].shape, BF16), shards[n], chip, 0)
             for n in LATE]
    late_gather = _exchange_start("gather_late_start", [shards[n] for n in LATE], lands, _late_gather_plan,
                                  3 * len(LATE), early_done, mod)
    mod = late_gather[4]

    half = MLA_ROPE // 2
    freqs = 1.0 / (ROPE_THETA ** (np.arange(half, dtype=np.float32) / half))
    zeros = np.zeros(LANES - MLA_ROPE, np.float32)
    freqs_row = jnp.asarray(np.concatenate([freqs, freqs, zeros]).astype(np.float32)[None])
    sign_row = jnp.asarray(np.concatenate([-np.ones(half), np.ones(half), zeros]).astype(np.float32)[None])
    cos, sin = _rope_tables(positions.reshape(seq, 1), freqs_row, sign_row)

    place = jnp.stack([ic, chip]).astype(jnp.int32)
    small_params = {n: small[n] for n in SMALL if n != "b_ada"}

    p1 = {**{n: gathered[n] for n in EARLY}, **small_params}
    mixed, mixing_vjp = jax.vjp(lambda x_, mod_, p_: _mixing_stage(x_, mod_, p_, cos, sin), xs, mod, p1)
    _, landed = _exchange_wait("gather_late_wait", late_gather, _late_gather_plan, mixed)
    p2 = {**dict(zip(LATE, landed)), **small_params}
    (x2, ffn), ffn_vjp = jax.vjp(_ffn_stage, xs, mixed, mod, p2)
    loss_rows, g_x2, g_ffn, g_gate2 = _loss_and_grads(x2, ffn, target, _mod_parts(mod)[5])
    loss_part = jnp.sum(loss_rows)
    gx2, gmixed, gmod2, gp2 = ffn_vjp((g_x2, g_ffn))
    gmod2 = gmod2 + jnp.concatenate([jnp.zeros((1, (N_MOD - 1) * D_MODEL), F32), g_gate2], axis=1)
    late_grads = [gp2[n] for n in LATE]
    late_plan = _direct_scatter_plan(LATE)
    late_scatter = _exchange_start(
        "grad_scatter_late_start", late_grads,
        [lax.empty((7,) + _half_shape(gr.shape[1:], HALF_AXIS[n]), BF16) for n, gr in zip(LATE, late_grads)],
        late_plan, 7 * len(LATE), gx2, gmixed)
    gx1, gmod1, gp1 = mixing_vjp(late_scatter[4])
    gx = gx1 + gx2
    gmod = gmod1 + gmod2
    gp = {n: gp1[n] + gp2[n] for n in small_params}

    sizes = [w[n].size for n in SMALL]
    offs, loss_off, n_small = _small_layout(sizes)
    pieces = []
    for n, size in zip(SMALL, sizes):
        pieces.append(gmod if n == "b_ada" else gp[n])
        if size % LANES:
            pieces.append(jnp.zeros((1, LANES - size % LANES), F32))
    pieces += [jnp.full((1, LANES), loss_part), jnp.zeros((1, n_small - loss_off - LANES), F32)]
    small_vec = jnp.concatenate(pieces, axis=1)

    g, delta, new_m, new_v = {}, {}, {}, {}

    def update(names, own, sib, after):
        for n, o, s in zip(names, own, sib):
            if n in TRANSPOSED_UPDATE:
                res = _adamw_halves(place, w[n].T, o, s, m[n].T, v[n].T, 1, "adamw_" + n, after)
                g[n], delta[n], new_m[n], new_v[n] = [r.T for r in res]
            else:
                g[n], delta[n], new_m[n], new_v[n] = _adamw_halves(place, w[n], o, s, m[n], v[n], HALF_AXIS[n],
                                                                   "adamw_" + n, after)

    late_grads, late_parts = _exchange_wait("grad_scatter_late_wait", late_scatter, late_plan, gx)
    own_late = [_chip_sum_direct(place, gr, pt, HALF_AXIS[n], "grad_chip_sum_" + n, n in TRANSPOSED_UPDATE)
                for n, gr, pt in zip(LATE, late_grads, late_parts)]
    early_grads = [gp1[n] for n in EARLY]
    theirs, small_gathered = _pair_exchange(EARLY, early_grads, "grad_pair_exchange_early",
                                            small_vec.reshape(8, n_small // 8))
    small_all = small_gathered.reshape(N_DEV, n_small)
    sib_late = _sibling_join(own_late, "grad_sibling_join_late", small_all)
    early_sums = [_pair_sum(place, gr, th, HALF_AXIS[n], "grad_pair_sum_" + n)
                  for n, gr, th in zip(EARLY, early_grads, theirs)]
    early_scatter = _exchange_start(
        "grad_scatter_early_start", early_sums,
        [lax.empty((N_CHIPS - 1,) + s.shape[1:], BF16) for s in early_sums], _late_scatter_plan, 3 * len(EARLY),
        sib_late[0], small_all)
    small_all = early_scatter[4]
    update(LATE, own_late, sib_late, small_all)

    *small_out, loss_row = _adamw_small([small[n] for n in SMALL], small_all, [m[n].reshape(1, -1) for n in SMALL],
                                        [v[n].reshape(1, -1) for n in SMALL], offs, loss_off)
    loss = loss_row[0, 0]
    for d, outs_d in zip((g, delta, new_m, new_v), small_out):
        d.update({n: o.reshape(w[n].shape) for n, o in zip(SMALL, outs_d)})

    dmod_all = small_all[:, :N_MOD * D_MODEL]
    g["w_ada"] = _ada_bwd(c_all, lax.dynamic_slice_in_dim(dmod_all, chip * ada_cols, ada_cols, axis=1))
    delta["w_ada"], new_m["w_ada"], new_v["w_ada"] = _adamw(w["w_ada"], g["w_ada"], m["w_ada"], v["w_ada"], "adamw_w_ada")

    early_sums, early_parts = _exchange_wait("grad_scatter_early_wait", early_scatter, _late_scatter_plan,
                                             delta["w_ada"])
    own_early = [_chip_sum(place, ps, pt, "grad_chip_sum_" + n, n in TRANSPOSED_UPDATE)
                 for n, ps, pt in zip(EARLY, early_sums, early_parts)]
    sib_early = _sibling_join(own_early, "grad_sibling_join_early", delta["w_ada"])
    update(EARLY, own_early, sib_early, sib_early[0])

    def outs(d):
        return [d[n][None] for n in WEIGHTS]

    return (loss, gx[None], *outs(g), *outs(delta), *outs(new_m), *outs(new_v))
```

```python
import numpy as np
import jax
import jax.numpy as jnp
from jax import lax
from jax.experimental import pallas as pl
from jax.experimental.pallas import tpu as pltpu

F32 = jnp.float32
BF16 = jnp.bfloat16
MESH = pl.DeviceIdType.MESH
ANY = pl.BlockSpec(memory_space=pl.ANY)

D_MODEL = 1024
SB_HEADS = 8
SB_HEAD_DIM = 64
SB_WIDTH = 512
MLA_HEADS = 4
MLA_NOPE = 128
MLA_ROPE = 64
MLA_QK = 192
MLA_V = 128
MLA_Q_RANK = 384
MLA_KV_RANK = 256
D_FF = 2816
N_MOD = 6
ROPE_THETA = 10000.0
EPS = 1e-6
LANES = 128

ADAM_LR = 0.001
ADAM_B1 = 0.9
ADAM_B2 = 0.999
ADAM_EPS = 1e-08
ADAM_WD = 0.01
ADAM_STEP = 10

N_CHIPS = 4
N_DEV = 8
ROW_TILE = 512
MM_ROW_TILE = 512
ATT_BLK = 256
MM_VMEM_LIMIT = 56 * 1024 * 1024
FF_SHARD = D_FF // N_CHIPS
FF_SHARD_PAD = 768


def _mm(a, b, mode, name, tm, tn, out_dtype=F32):
    if mode == "nn":
        (m, k), n = a.shape, b.shape[1]
        a_spec = pl.BlockSpec((tm, k), lambda j, i: (i, 0))
        b_spec = pl.BlockSpec((k, tn), lambda j, i: (0, j))
        dims = (((1,), (0,)), ((), ()))
    elif mode == "nt":
        (m, k), n = a.shape, b.shape[0]
        a_spec = pl.BlockSpec((tm, k), lambda j, i: (i, 0))
        b_spec = pl.BlockSpec((tn, k), lambda j, i: (j, 0))
        dims = (((1,), (1,)), ((), ()))
    else:
        (k, m), n = a.shape, b.shape[1]
        a_spec = pl.BlockSpec((k, tm), lambda j, i: (0, i))
        b_spec = pl.BlockSpec((k, tn), lambda j, i: (0, j))
        dims = (((0,), (0,)), ((), ()))
    assert m % tm == 0 and n % tn == 0, (name, m, n, tm, tn)

    def body(a_ref, b_ref, o_ref):
        o_ref[...] = lax.dot_general(a_ref[...].astype(BF16), b_ref[...].astype(BF16), dims,
                                     preferred_element_type=F32).astype(out_dtype)

    return pl.pallas_call(
        body, name=name, grid=(n // tn, m // tm),
        in_specs=[a_spec, b_spec],
        out_specs=pl.BlockSpec((tm, tn), lambda j, i: (i, j)),
        out_shape=jax.ShapeDtypeStruct((m, n), out_dtype),
        compiler_params=pltpu.CompilerParams(dimension_semantics=("arbitrary", "arbitrary"),
                                             vmem_limit_bytes=MM_VMEM_LIMIT),
    )(a, b)


def _make_linear(name, tk_w, tn_w):
    @jax.custom_vjp
    def op(a, w):
        return _mm(a, w, "nn", name + "_fwd", MM_ROW_TILE, w.shape[1])

    def fwd(a, w):
        return op(a, w), (a, w)

    def bwd(res, dy):
        a, w = res
        da = _mm(dy, w, "nt", name + "_dx", MM_ROW_TILE, w.shape[0])
        dw = _mm(a, dy, "tn", name + "_dw", tk_w, tn_w, out_dtype=BF16)
        return da, dw

    op.defvjp(fwd, bwd)
    return op


def _make_linear_split(name, widths, tk_w):
    starts = [sum(widths[:g]) for g in range(len(widths))]

    def call_fwd(a, w):
        t, k = a.shape
        n = w.shape[1]

        def body(a_ref, w_ref, *o_refs):
            y = jnp.dot(a_ref[...].astype(BF16), w_ref[...], preferred_element_type=F32)
            for o_ref, s0, wd in zip(o_refs, starts, widths):
                o_ref[...] = y[:, s0:s0 + wd]

        return pl.pallas_call(
            body, name=name + "_fwd", grid=(t // MM_ROW_TILE,),
            in_specs=[pl.BlockSpec((MM_ROW_TILE, k), lambda i: (i, 0)), pl.BlockSpec((k, n), lambda i: (0, 0))],
            out_specs=[pl.BlockSpec((MM_ROW_TILE, wd), lambda i: (i, 0)) for wd in widths],
            out_shape=[jax.ShapeDtypeStruct((t, wd), F32) for wd in widths],
            compiler_params=pltpu.CompilerParams(dimension_semantics=("arbitrary",), vmem_limit_bytes=MM_VMEM_LIMIT),
        )(a, w)

    def call_dx(dys, w):
        t = dys[0].shape[0]
        k, n = w.shape

        def body(*refs):
            dy_refs, w_ref, o_ref = refs[:-2], refs[-2], refs[-1]
            acc = jnp.zeros((MM_ROW_TILE, k), F32)
            for dy_ref, s0, wd in zip(dy_refs, starts, widths):
                acc = acc + _nt(dy_ref[...].astype(BF16), w_ref[:, s0:s0 + wd])
            o_ref[...] = acc

        return pl.pallas_call(
            body, name=name + "_dx", grid=(t // MM_ROW_TILE,),
            in_specs=[pl.BlockSpec((MM_ROW_TILE, wd), lambda i: (i, 0)) for wd in widths]
            + [pl.BlockSpec((k, n), lambda i: (0, 0))],
            out_specs=pl.BlockSpec((MM_ROW_TILE, k), lambda i: (i, 0)),
            out_shape=jax.ShapeDtypeStruct((t, k), F32),
            compiler_params=pltpu.CompilerParams(dimension_semantics=("arbitrary",), vmem_limit_bytes=MM_VMEM_LIMIT),
        )(*dys, w)

    def call_dw(a, dys, w):
        t, k = a.shape
        n = w.shape[1]

        def body(a_ref, *refs):
            dy_refs, o_ref = refs[:-1], refs[-1]
            ab = a_ref[...].astype(BF16)
            for dy_ref, s0, wd in zip(dy_refs, starts, widths):
                o_ref[:, s0:s0 + wd] = _tn(ab, dy_ref[...].astype(BF16)).astype(BF16)
            if starts[-1] + widths[-1] < n:
                o_ref[:, starts[-1] + widths[-1]:] = jnp.zeros((tk_w, n - starts[-1] - widths[-1]), BF16)

        return pl.pallas_call(
            body, name=name + "_dw", grid=(k // tk_w,),
            in_specs=[pl.BlockSpec((t, tk_w), lambda i: (0, i))]
            + [pl.BlockSpec((t, wd), lambda i: (0, 0)) for wd in widths],
            out_specs=pl.BlockSpec((tk_w, n), lambda i: (i, 0)),
            out_shape=jax.ShapeDtypeStruct((k, n), BF16),
            compiler_params=pltpu.CompilerParams(dimension_semantics=("arbitrary",), vmem_limit_bytes=MM_VMEM_LIMIT),
        )(a, *dys)

    @jax.custom_vjp
    def op(a, w):
        return tuple(call_fwd(a, w))

    def fwd(a, w):
        return op(a, w), (a, w)

    def bwd(res, dys):
        a, w = res
        return call_dx(dys, w), call_dw(a, dys, w)

    op.defvjp(fwd, bwd)
    return op


def _row_spec(arr, tb):
    return pl.BlockSpec((tb, arr.shape[1]), lambda i: (i, 0))


def _full_spec(arr):
    return pl.BlockSpec(arr.shape, lambda i: (0, 0))


def _make_rowwise(name, f, n_rows, n_params, out_cols, diff_rows, out_dtypes=None, grad_dtypes=None):
    n_out = len(out_cols)
    out_dtypes = out_dtypes or [F32] * n_out
    grad_dtypes = grad_dtypes or [F32] * sum(diff_rows)

    def call_fwd(rows, params):
        t = rows[0].shape[0]

        def body(*refs):
            ins = [r[...] for r in refs[:n_rows + n_params]]
            outs = f(*ins)
            for o_ref, o in zip(refs[n_rows + n_params:], outs):
                o_ref[...] = o.astype(o_ref.dtype)

        return pl.pallas_call(
            body, name=name + "_fwd", grid=(t // ROW_TILE,),
            in_specs=[_row_spec(a, ROW_TILE) for a in rows] + [_full_spec(p) for p in params],
            out_specs=[pl.BlockSpec((ROW_TILE, n), lambda i: (i, 0)) for n in out_cols],
            out_shape=[jax.ShapeDtypeStruct((t, n), dt) for n, dt in zip(out_cols, out_dtypes)],
            compiler_params=pltpu.CompilerParams(dimension_semantics=("arbitrary",),
                                                 vmem_limit_bytes=MM_VMEM_LIMIT),
        )(*rows, *params)

    def call_bwd(rows, params, cts):
        t = rows[0].shape[0]
        d_rows = [a for a, d in zip(rows, diff_rows) if d]
        n_in = n_rows + n_params + n_out

        def body(*refs):
            ins = [r[...] for r in refs[:n_rows + n_params]]
            ct = tuple(r[...].astype(F32) for r in refs[n_rows + n_params:n_in])
            _, vjp = jax.vjp(f, *ins)
            grads = vjp(ct)
            out_refs = refs[n_in:]
            g_rows = [g for g, d in zip(grads[:n_rows], diff_rows) if d]
            for o_ref, g in zip(out_refs[:len(g_rows)], g_rows):
                o_ref[...] = g.astype(o_ref.dtype)
            p_refs = out_refs[len(g_rows):]

            if p_refs:
                @pl.when(pl.program_id(0) == 0)
                def _():
                    for p_ref in p_refs:
                        p_ref[...] = jnp.zeros_like(p_ref)

                for p_ref, g in zip(p_refs, grads[n_rows:]):
                    p_ref[...] += g

        return pl.pallas_call(
            body, name=name + "_bwd", grid=(t // ROW_TILE,),
            in_specs=[_row_spec(a, ROW_TILE) for a in rows] + [_full_spec(p) for p in params]
            + [_row_spec(c, ROW_TILE) for c in cts],
            out_specs=[_row_spec(a, ROW_TILE) for a in d_rows] + [_full_spec(p) for p in params],
            out_shape=[jax.ShapeDtypeStruct(a.shape, dt) for a, dt in zip(d_rows, grad_dtypes)]
            + [jax.ShapeDtypeStruct(p.shape, F32) for p in params],
            compiler_params=pltpu.CompilerParams(dimension_semantics=("arbitrary",),
                                                 vmem_limit_bytes=MM_VMEM_LIMIT),
        )(*rows, *params, *cts)

    @jax.custom_vjp
    def op(*args):
        return tuple(call_fwd(args[:n_rows], args[n_rows:]))

    def fwd(*args):
        return op(*args), args

    def bwd(args, cts):
        rows, params = args[:n_rows], args[n_rows:]
        outs = call_bwd(rows, params, cts)
        it = iter(outs)
        g_rows = [next(it) if d else jnp.zeros_like(a) for a, d in zip(rows, diff_rows)]
        return tuple(g_rows) + tuple(it)

    op.defvjp(fwd, bwd)
    return op


def _rms(x, g, n):
    return x * lax.rsqrt(jnp.sum(x * x, axis=-1, keepdims=True) * (1.0 / n) + EPS) * g


def _f_pre_attn(x, g, scale, shift):
    return (_rms(x, g, D_MODEL) * (1.0 + scale) + shift,)


def _f_mla_a(cq, ckv, gq, gkv):
    return _rms(cq, gq, MLA_Q_RANK), _rms(ckv, gkv, MLA_KV_RANK)


@jax.custom_vjp
def _split_lanes(x):
    return tuple(x[:, i * LANES:(i + 1) * LANES] for i in range(x.shape[1] // LANES))


def _split_lanes_fwd(x):
    return _split_lanes(x), None


def _split_lanes_bwd(_, cts):
    return (jnp.concatenate(cts, axis=1),)


_split_lanes.defvjp(_split_lanes_fwd, _split_lanes_bwd)


def _f_mla_b(qall, kn_all, kr, kr_sw, cos, sin, gqn, gqr, gqr_sw, gkn, gkr, gkr_sw):
    q = _split_lanes(qall)
    kn = _split_lanes(kn_all)
    qn_o, qr_o, kn_o = [], [], []
    for h in range(MLA_HEADS):
        qn, qr, qs = q[h], q[MLA_HEADS + h], q[2 * MLA_HEADS + h]
        ss = jnp.sum(qn * qn, axis=-1, keepdims=True) + jnp.sum(qr * qr, axis=-1, keepdims=True)
        rs = lax.rsqrt(ss * (1.0 / MLA_QK) + EPS)
        qn_o.append(qn * rs * gqn)
        qr_o.append((qr * rs * gqr) * cos + (qs * rs * gqr_sw) * sin)
        kn_o.append(_rms(kn[h], gkn, MLA_NOPE))
    rs = lax.rsqrt(jnp.sum(kr * kr, axis=-1, keepdims=True) * (1.0 / MLA_ROPE) + EPS)
    kr_o = (kr * rs * gkr) * cos + (kr_sw * rs * gkr_sw) * sin
    return (jnp.concatenate(qn_o, axis=1), jnp.concatenate(qr_o, axis=1), jnp.concatenate(kn_o, axis=1), kr_o)


def _f_post_attn(o_sb, o_mla, g_sb, g_mla):
    return (jnp.concatenate([_rms(o_sb, g_sb, SB_WIDTH), _rms(o_mla, g_mla, SB_WIDTH)], axis=1),)


def _f_pre_ffn(x, attn, gate, g, scale, shift):
    x2 = x + gate * attn
    return x2, _rms(x2, g, D_MODEL) * (1.0 + scale) + shift


def _f_swiglu(gt, up):
    return (gt / (1.0 + jnp.exp(-gt)) * up,)


def _f_loss(x2, ffn, target, gate):
    err = x2 + gate * ffn - target
    return (jnp.sum(err * err, axis=-1, keepdims=True) * (1.0 / D_MODEL),)


def _rope_tables(pos_col, freqs, sign):
    t = pos_col.shape[0]

    def body(p_ref, f_ref, s_ref, cos_ref, sin_ref):
        ang = p_ref[...].astype(F32) * f_ref[...]
        live = jnp.abs(s_ref[...])
        cos_ref[...] = jnp.cos(ang) * live
        sin_ref[...] = jnp.sin(ang) * s_ref[...]

    return pl.pallas_call(
        body, name="rope_tables", grid=(t // ROW_TILE,),
        in_specs=[pl.BlockSpec((ROW_TILE, 1), lambda i: (i, 0)), _full_spec(freqs), _full_spec(sign)],
        out_specs=[pl.BlockSpec((ROW_TILE, LANES), lambda i: (i, 0))] * 2,
        out_shape=[jax.ShapeDtypeStruct((t, LANES), F32)] * 2,
    )(pos_col, freqs, sign)


def _hi_lo_dot(x, tri):
    hi = x.astype(BF16)
    lo = (x - hi.astype(F32)).astype(BF16)
    return (jnp.dot(hi, tri, preferred_element_type=F32) + jnp.dot(lo, tri, preferred_element_type=F32))


def _tri(cmp):
    r = lax.broadcasted_iota(jnp.int32, (ATT_BLK, ATT_BLK), 0)
    c = lax.broadcasted_iota(jnp.int32, (ATT_BLK, ATT_BLK), 1)
    return cmp(r, c).astype(BF16)


def _nt(a, b):
    return lax.dot_general(a, b, (((1,), (1,)), ((), ())), preferred_element_type=F32)


def _tn(a, b):
    return lax.dot_general(a, b, (((0,), (0,)), ((), ())), preferred_element_type=F32)


def _sb_logs(z):
    lb = jnp.minimum(z, 0.0) - jnp.log(1.0 + jnp.exp(-jnp.abs(z)))
    return lb, lb - z


def _sb_fwd(q, k, v):
    t = q.shape[0]
    nq = t // ATT_BLK
    scale = SB_HEAD_DIM ** -0.5

    def body(q_ref, k_ref, v_ref, o_ref, tot_ref):
        qi = pl.program_id(1)
        lane = lax.broadcasted_iota(jnp.int32, (ATT_BLK, LANES), 1)
        tri = _tri(lambda r, c: r > c)
        qv = q_ref[...] * scale
        heads = [(lane // SB_HEAD_DIM) == hh for hh in range(2)]
        qms = [jnp.where(mine, qv, 0.0).astype(BF16) for mine in heads]

        def blocks(kbs, carry, diagonal):
            acc = carry[0]
            nb = len(kbs)
            chains = [(b, hh) for b in range(nb) for hh in range(2)]
            offs = [pl.multiple_of(kb * ATT_BLK, ATT_BLK) for kb in kbs]
            kks = [k_ref[pl.ds(off, ATT_BLK), :].astype(BF16) for off in offs]
            v_blks = [v_ref[pl.ds(off, ATT_BLK), :] for off in offs]
            if any(diagonal):
                valid = (lax.broadcasted_iota(jnp.int32, (ATT_BLK, ATT_BLK), 1)
                         < lax.broadcasted_iota(jnp.int32, (ATT_BLK, ATT_BLK), 0))
            zs = {ch: _nt(qms[ch[1]], kks[ch[0]]) for ch in chains}
            vvs = {(b, hh): jnp.where(heads[hh], v_blks[b], 0.0).astype(BF16) for b, hh in chains}
            logs = {ch: _sb_logs(zs[ch]) for ch in chains}
            l1ms = {ch: jnp.where(valid, logs[ch][1], 0.0) if diagonal[ch[0]] else logs[ch][1] for ch in chains}
            run = {(0, hh): carry[1 + hh] for hh in range(2)}
            for b, hh in chains:
                run[(b + 1, hh)] = run[(b, hh)] + jnp.sum(l1ms[(b, hh)], axis=-1, keepdims=True)
            afters = {ch: _hi_lo_dot(l1ms[ch], tri) for ch in chains}
            ws = {ch: jnp.exp(logs[ch][0] + (afters[ch] + run[ch])) for ch in chains}
            ws = {ch: jnp.where(valid, ws[ch], 0.0) if diagonal[ch[0]] else ws[ch] for ch in chains}
            for ch in chains:
                acc = acc + jnp.dot(ws[ch].astype(BF16), vvs[ch], preferred_element_type=F32)
            return (acc, run[(nb, 0)], run[(nb, 1)])

        zero = jnp.zeros((ATT_BLK, 1), F32)
        init = (jnp.zeros((ATT_BLK, LANES), F32), zero, zero)
        carry = lax.cond(qi % 2 == 1, lambda cr: blocks([qi, qi - 1], cr, (True, False)),
                         lambda cr: blocks([qi], cr, (True,)), init)
        top = qi - 1 - qi % 2
        carry = lax.fori_loop(0, qi // 2, lambda pr, cr: blocks([top - 2 * pr, top - 1 - 2 * pr], cr, (False, False)),
                              carry)
        o_ref[...] = carry[0]
        for hh in range(2):
            tot_ref[:, hh * LANES:(hh + 1) * LANES] = jnp.broadcast_to(carry[1 + hh], (ATT_BLK, LANES))

    return pl.pallas_call(
        body, name="sb_attn_fwd", grid=(SB_HEADS // 2, nq),
        in_specs=[pl.BlockSpec((ATT_BLK, LANES), lambda p, i: (i, p)),
                  pl.BlockSpec((t, LANES), lambda p, i: (0, p)),
                  pl.BlockSpec((t, LANES), lambda p, i: (0, p))],
        out_specs=[pl.BlockSpec((ATT_BLK, LANES), lambda p, i: (i, p)),
                   pl.BlockSpec((ATT_BLK, 2 * LANES), lambda p, i: (i, p))],
        out_shape=[jax.ShapeDtypeStruct((t, SB_WIDTH), F32), jax.ShapeDtypeStruct((t, SB_HEADS * LANES), F32)],
        compiler_params=pltpu.CompilerParams(dimension_semantics=("arbitrary", "arbitrary")),
    )(q, k, v)


def _sb_bwd(q, k, v, tot, do):
    t = q.shape[0]
    nq = t // ATT_BLK
    scale = SB_HEAD_DIM ** -0.5

    def body(q_ref, k_ref, v_ref, tot_ref, do_ref, dq_ref, dk_ref, dv_ref):
        qi = pl.program_id(1)

        @pl.when(qi == 0)
        def _():
            dk_ref[...] = jnp.zeros_like(dk_ref)
            dv_ref[...] = jnp.zeros_like(dv_ref)

        lane = lax.broadcasted_iota(jnp.int32, (ATT_BLK, LANES), 1)
        tri_incl = _tri(lambda r, c: r <= c)
        tri_lt = _tri(lambda r, c: r < c)
        qv = q_ref[...] * scale
        dov = do_ref[...]
        heads = [(lane // SB_HEAD_DIM) == hh for hh in range(2)]
        qms = [jnp.where(mine, qv, 0.0).astype(BF16) for mine in heads]
        doms = [jnp.where(mine, dov, 0.0).astype(BF16) for mine in heads]
        tots = [tot_ref[:, hh * LANES:hh * LANES + 1] for hh in range(2)]

        def blocks(kbs, carry, diagonal):
            dq = carry[0]
            nb = len(kbs)
            chains = [(b, hh) for b in range(nb) for hh in range(2)]
            offs = [pl.multiple_of(kb * ATT_BLK, ATT_BLK) for kb in kbs]
            k_blks = [k_ref[pl.ds(off, ATT_BLK), :] for off in offs]
            vvs = [v_ref[pl.ds(off, ATT_BLK), :].astype(BF16) for off in offs]
            if any(diagonal):
                valid = (lax.broadcasted_iota(jnp.int32, (ATT_BLK, ATT_BLK), 1)
                         < lax.broadcasted_iota(jnp.int32, (ATT_BLK, ATT_BLK), 0))
            kks = {(b, hh): jnp.where(heads[hh], k_blks[b], 0.0).astype(BF16) for b, hh in chains}
            zs = {ch: _nt(qms[ch[1]], kks[ch]) for ch in chains}
            dws = {ch: _nt(doms[ch[1]], vvs[ch[0]]) for ch in chains}
            logs = {ch: _sb_logs(zs[ch]) for ch in chains}
            lbs = {ch: logs[ch][0] for ch in chains}
            l1m_all = {ch: logs[ch][1] for ch in chains}
            l1ms = {ch: jnp.where(valid, l1m_all[ch], 0.0) if diagonal[ch[0]] else l1m_all[ch] for ch in chains}
            pre, c_de = {}, {}
            for hh in range(2):
                pre[(0, hh)], c_de[(0, hh)] = carry[1 + 2 * hh], carry[2 + 2 * hh]
            for b, hh in chains:
                pre[(b + 1, hh)] = pre[(b, hh)] + jnp.sum(l1ms[(b, hh)], axis=-1, keepdims=True)
            prefix = {ch: _hi_lo_dot(l1ms[ch], tri_incl) for ch in chains}
            ws = {ch: jnp.exp(lbs[ch] + (tots[ch[1]] - (prefix[ch] + pre[ch]))) for ch in chains}
            ws = {ch: jnp.where(valid, ws[ch], 0.0) if diagonal[ch[0]] else ws[ch] for ch in chains}
            d_es = {ch: ws[ch] * dws[ch] for ch in chains}
            for b, hh in chains:
                c_de[(b + 1, hh)] = c_de[(b, hh)] + jnp.sum(d_es[(b, hh)], axis=-1, keepdims=True)
            dvs = [_tn(ws[(b, 0)].astype(BF16), doms[0]) + _tn(ws[(b, 1)].astype(BF16), doms[1]) for b in range(nb)]
            dl1ms = {ch: jnp.dot(d_es[ch].astype(BF16), tri_lt, preferred_element_type=F32) + c_de[ch] for ch in chains}
            dzs = {ch: d_es[ch] * jnp.exp(l1m_all[ch]) - dl1ms[ch] * jnp.exp(lbs[ch]) for ch in chains}
            dzs = {ch: jnp.where(valid, dzs[ch], 0.0) if diagonal[ch[0]] else dzs[ch] for ch in chains}
            dzs = {ch: dzs[ch].astype(BF16) for ch in chains}
            for ch in chains:
                dq = dq + jnp.dot(dzs[ch], kks[ch], preferred_element_type=F32)
            for b in range(nb):
                dk_ref[pl.ds(offs[b], ATT_BLK), :] += _tn(dzs[(b, 0)], qms[0]) + _tn(dzs[(b, 1)], qms[1])
                dv_ref[pl.ds(offs[b], ATT_BLK), :] += dvs[b]
            return (dq, pre[(nb, 0)], c_de[(nb, 0)], pre[(nb, 1)], c_de[(nb, 1)])

        zero = jnp.zeros((ATT_BLK, 1), F32)
        carry = lax.fori_loop(0, qi // 2, lambda pr, cr: blocks([2 * pr, 2 * pr + 1], cr, (False, False)),
                              (jnp.zeros((ATT_BLK, LANES), F32), zero, zero, zero, zero))
        carry = lax.cond(qi % 2 == 1, lambda cr: blocks([qi - 1, qi], cr, (False, True)),
                         lambda cr: blocks([qi], cr, (True,)), carry)
        dq_ref[...] = carry[0] * scale

    return pl.pallas_call(
        body, name="sb_attn_bwd", grid=(SB_HEADS // 2, nq),
        in_specs=[pl.BlockSpec((ATT_BLK, LANES), lambda p, i: (i, p)),
                  pl.BlockSpec((t, LANES), lambda p, i: (0, p)),
                  pl.BlockSpec((t, LANES), lambda p, i: (0, p)),
                  pl.BlockSpec((ATT_BLK, 2 * LANES), lambda p, i: (i, p)),
                  pl.BlockSpec((ATT_BLK, LANES), lambda p, i: (i, p))],
        out_specs=[pl.BlockSpec((ATT_BLK, LANES), lambda p, i: (i, p)),
                   pl.BlockSpec((t, LANES), lambda p, i: (0, p)),
                   pl.BlockSpec((t, LANES), lambda p, i: (0, p))],
        out_shape=[jax.ShapeDtypeStruct((t, SB_WIDTH), F32)] * 3,
        compiler_params=pltpu.CompilerParams(dimension_semantics=("arbitrary", "arbitrary")),
    )(q, k, v, tot, do)


@jax.custom_vjp
def _sb_attention(q, k, v):
    return _sb_fwd(q, k, v)[0]


def _sb_attention_fwd(q, k, v):
    o, tot = _sb_fwd(q, k, v)
    return o, (q, k, v, tot)


def _sb_attention_bwd(res, do):
    return tuple(_sb_bwd(*res, do))


_sb_attention.defvjp(_sb_attention_fwd, _sb_attention_bwd)


def _mla_fwd(qn, qr, kn, kr, v):
    t = qn.shape[0]
    nq = t // ATT_BLK
    scale = MLA_QK ** -0.5

    def body(qn_ref, qr_ref, kn_ref, kr_ref, v_ref, o_ref, lse_ref):
        qi = pl.program_id(1)
        lanes = [slice(hh * LANES, (hh + 1) * LANES) for hh in range(2)]
        qnb = [qn_ref[:, sl].astype(BF16) for sl in lanes]
        qrb = [qr_ref[:, sl].astype(BF16) for sl in lanes]

        def blocks(kbs, carry, diagonal):
            nb = len(kbs)
            chains = [(b, hh) for b in range(nb) for hh in range(2)]
            offs = [pl.multiple_of(kb * ATT_BLK, ATT_BLK) for kb in kbs]
            krbs = [kr_ref[pl.ds(off, ATT_BLK), :].astype(BF16) for off in offs]
            accs, ms, ls = [carry[0], carry[3]], [carry[1], carry[4]], [carry[2], carry[5]]
            ss = {(b, hh): (_nt(qnb[hh], kn_ref[pl.ds(offs[b], ATT_BLK), lanes[hh]].astype(BF16))
                            + _nt(qrb[hh], krbs[b])) * scale for b, hh in chains}
            if any(diagonal):
                causal = (lax.broadcasted_iota(jnp.int32, (ATT_BLK, ATT_BLK), 1)
                          <= lax.broadcasted_iota(jnp.int32, (ATT_BLK, ATT_BLK), 0))
                ss = {ch: jnp.where(causal, ss[ch], -jnp.inf) if diagonal[ch[0]] else ss[ch] for ch in chains}
            m_new = list(ms)
            for b, hh in chains:
                m_new[hh] = jnp.maximum(m_new[hh], jnp.max(ss[(b, hh)], axis=-1, keepdims=True))
            ps = {(b, hh): jnp.exp(ss[(b, hh)] - m_new[hh]) for b, hh in chains}
            alphas = [jnp.exp(ms[hh] - m_new[hh]) for hh in range(2)]
            pvs = {(b, hh): jnp.dot(ps[(b, hh)].astype(BF16), v_ref[pl.ds(offs[b], ATT_BLK), lanes[hh]].astype(BF16),
                                    preferred_element_type=F32) for b, hh in chains}
            out = []
            for hh in range(2):
                acc, l = accs[hh] * alphas[hh], ls[hh] * alphas[hh]
                for b in range(nb):
                    acc, l = acc + pvs[(b, hh)], l + jnp.sum(ps[(b, hh)], axis=-1, keepdims=True)
                out += [acc, m_new[hh], l]
            return tuple(out)

        init = (jnp.zeros((ATT_BLK, LANES), F32), jnp.full((ATT_BLK, 1), -jnp.inf, F32), jnp.zeros((ATT_BLK, 1), F32))
        carry = lax.cond(qi % 2 == 1, lambda cr: blocks([qi, qi - 1], cr, (True, False)),
                         lambda cr: blocks([qi], cr, (True,)), init + init)
        carry = lax.fori_loop(0, qi // 2, lambda pr, cr: blocks([2 * pr, 2 * pr + 1], cr, (False, False)), carry)
        for hh in range(2):
            acc, m, l = carry[3 * hh:3 * hh + 3]
            o_ref[:, lanes[hh]] = acc / l
            lse_ref[:, lanes[hh]] = jnp.broadcast_to(m + jnp.log(l), (ATT_BLK, LANES))

    blk = pl.BlockSpec((ATT_BLK, 2 * LANES), lambda p, i: (i, p))
    full = pl.BlockSpec((t, 2 * LANES), lambda p, i: (0, p))
    return pl.pallas_call(
        body, name="mla_attn_fwd", grid=(MLA_HEADS // 2, nq),
        in_specs=[blk, blk, full, pl.BlockSpec((t, LANES), lambda p, i: (0, 0)), full],
        out_specs=[blk, blk],
        out_shape=[jax.ShapeDtypeStruct((t, MLA_HEADS * LANES), F32)] * 2,
        compiler_params=pltpu.CompilerParams(dimension_semantics=("arbitrary", "arbitrary")),
    )(qn, qr, kn, kr, v)


def _mla_bwd(qn, qr, kn, kr, v, o, lse, do):
    t = qn.shape[0]
    nq = t // ATT_BLK
    scale = MLA_QK ** -0.5

    def body(qn_ref, qr_ref, kn_ref, kr_ref, v_ref, o_ref, lse_ref, do_ref,
             dqn_ref, dqr_ref, dkn_ref, dkr_ref, dv_ref):
        pair = pl.program_id(0)
        qi = pl.program_id(1)

        @pl.when(qi == 0)
        def _():
            dkn_ref[...] = jnp.zeros_like(dkn_ref)
            dv_ref[...] = jnp.zeros_like(dv_ref)

        @pl.when((qi == 0) & (pair == 0))
        def _():
            dkr_ref[...] = jnp.zeros_like(dkr_ref)

        lanes = [slice(hh * LANES, (hh + 1) * LANES) for hh in range(2)]
        qnb = [qn_ref[:, sl].astype(BF16) for sl in lanes]
        qrb = [qr_ref[:, sl].astype(BF16) for sl in lanes]
        dob = [do_ref[:, sl].astype(BF16) for sl in lanes]
        delta = [jnp.sum(do_ref[:, sl] * o_ref[:, sl], axis=-1, keepdims=True) for sl in lanes]
        lse_v = [lse_ref[:, hh * LANES:hh * LANES + 1] for hh in range(2)]

        def blocks(kbs, carry, diagonal):
            nb = len(kbs)
            chains = [(b, hh) for b in range(nb) for hh in range(2)]
            offs = [pl.multiple_of(kb * ATT_BLK, ATT_BLK) for kb in kbs]
            krbs = [kr_ref[pl.ds(off, ATT_BLK), :].astype(BF16) for off in offs]
            knb = {(b, hh): kn_ref[pl.ds(offs[b], ATT_BLK), lanes[hh]].astype(BF16) for b, hh in chains}
            vb = {(b, hh): v_ref[pl.ds(offs[b], ATT_BLK), lanes[hh]].astype(BF16) for b, hh in chains}
            ss = {(b, hh): _nt(qnb[hh], knb[(b, hh)]) + _nt(qrb[hh], krbs[b]) for b, hh in chains}
            dps = {(b, hh): _nt(dob[hh], vb[(b, hh)]) for b, hh in chains}
            ps = {(b, hh): jnp.exp(ss[(b, hh)] * scale - lse_v[hh]) for b, hh in chains}
            if any(diagonal):
                causal = (lax.broadcasted_iota(jnp.int32, (ATT_BLK, ATT_BLK), 1)
                          <= lax.broadcasted_iota(jnp.int32, (ATT_BLK, ATT_BLK), 0))
                ps = {ch: jnp.where(causal, ps[ch], 0.0) if diagonal[ch[0]] else ps[ch] for ch in chains}
            dss = {(b, hh): (ps[(b, hh)] * (dps[(b, hh)] - delta[hh]) * scale).astype(BF16) for b, hh in chains}
            for b, hh in chains:
                dv_ref[pl.ds(offs[b], ATT_BLK), lanes[hh]] += _tn(ps[(b, hh)].astype(BF16), dob[hh])
            for b, hh in chains:
                dkn_ref[pl.ds(offs[b], ATT_BLK), lanes[hh]] += _tn(dss[(b, hh)], qnb[hh])
            for b in range(nb):
                dkr_ref[pl.ds(offs[b], ATT_BLK), :] += _tn(dss[(b, 0)], qrb[0]) + _tn(dss[(b, 1)], qrb[1])
            out = list(carry)
            for b, hh in chains:
                out[2 * hh] = out[2 * hh] + jnp.dot(dss[(b, hh)], knb[(b, hh)], preferred_element_type=F32)
                out[2 * hh + 1] = out[2 * hh + 1] + jnp.dot(dss[(b, hh)], krbs[b], preferred_element_type=F32)
            return tuple(out)

        zero = jnp.zeros((ATT_BLK, LANES), F32)
        carry = lax.fori_loop(0, qi // 2, lambda pr, cr: blocks([2 * pr, 2 * pr + 1], cr, (False, False)),
                              (zero, zero, zero, zero))
        carry = lax.cond(qi % 2 == 1, lambda cr: blocks([qi - 1, qi], cr, (False, True)),
                         lambda cr: blocks([qi], cr, (True,)), carry)
        for hh in range(2):
            dqn_ref[:, lanes[hh]] = carry[2 * hh]
            dqr_ref[:, lanes[hh]] = carry[2 * hh + 1]

    blk = pl.BlockSpec((ATT_BLK, 2 * LANES), lambda p, i: (i, p))
    full = pl.BlockSpec((t, 2 * LANES), lambda p, i: (0, p))
    shared = pl.BlockSpec((t, LANES), lambda p, i: (0, 0))
    wide = jax.ShapeDtypeStruct((t, MLA_HEADS * LANES), F32)
    return pl.pallas_call(
        body, name="mla_attn_bwd", grid=(MLA_HEADS // 2, nq),
        in_specs=[blk, blk, full, shared, full, blk, blk, blk],
        out_specs=[blk, blk, full, shared, full],
        out_shape=[wide, wide, wide, jax.ShapeDtypeStruct((t, LANES), F32), wide],
        compiler_params=pltpu.CompilerParams(dimension_semantics=("arbitrary", "arbitrary")),
    )(qn, qr, kn, kr, v, o, lse, do)


@jax.custom_vjp
def _mla_attention(qn, qr, kn, kr, v):
    return _mla_fwd(qn, qr, kn, kr, v)[0]


def _mla_attention_fwd(qn, qr, kn, kr, v):
    o, lse = _mla_fwd(qn, qr, kn, kr, v)
    return o, (qn, qr, kn, kr, v, o, lse)


def _mla_attention_bwd(res, do):
    return tuple(_mla_bwd(*res, do))


_mla_attention.defvjp(_mla_attention_fwd, _mla_attention_bwd)


def _ffn_in(h, wg, wu):
    t, k = h.shape
    n_sh, _, cc = wg.shape

    def body(h_ref, wg_ref, wu_ref, g_ref, u_ref, a_ref):
        hb = h_ref[...].astype(BF16)
        for j in range(n_sh):
            cols = slice(j * cc, (j + 1) * cc)
            g = jnp.dot(hb, wg_ref[j], preferred_element_type=F32)
            u = jnp.dot(hb, wu_ref[j], preferred_element_type=F32)
            g_ref[:, cols] = g.astype(BF16)
            u_ref[:, cols] = u.astype(BF16)
            a_ref[:, cols] = _f_swiglu(g, u)[0].astype(BF16)

    w_spec = pl.BlockSpec((n_sh, k, cc), lambda i: (0, 0, 0))
    o_spec = pl.BlockSpec((MM_ROW_TILE, n_sh * cc), lambda i: (i, 0))
    wide = jax.ShapeDtypeStruct((t, n_sh * cc), BF16)
    return pl.pallas_call(
        body, name="ffn_in_fwd", grid=(t // MM_ROW_TILE,),
        in_specs=[pl.BlockSpec((MM_ROW_TILE, k), lambda i: (i, 0)), w_spec, w_spec],
        out_specs=[o_spec, o_spec, o_spec],
        out_shape=[wide, wide, wide],
        compiler_params=pltpu.CompilerParams(dimension_semantics=("arbitrary",), vmem_limit_bytes=MM_VMEM_LIMIT),
    )(h, wg, wu)


def _ffn_mid_bwd(dy, wd, g, u):
    t, n = dy.shape
    n_sh, cc, _ = wd.shape

    def body(dy_ref, wd_ref, g_ref, u_ref, dg_ref, du_ref):
        d_act = _nt(dy_ref[...].astype(BF16), wd_ref[...])
        _, vjp = jax.vjp(_f_swiglu, g_ref[...].astype(F32), u_ref[...].astype(F32))
        dg, du = vjp((d_act,))
        dg_ref[...] = dg.astype(BF16)
        du_ref[...] = du.astype(BF16)

    blk = pl.BlockSpec((MM_ROW_TILE, cc), lambda j, i: (i, j))
    wide = jax.ShapeDtypeStruct((t, n_sh * cc), BF16)
    return pl.pallas_call(
        body, name="ffn_mid_bwd", grid=(n_sh, t // MM_ROW_TILE),
        in_specs=[pl.BlockSpec((MM_ROW_TILE, n), lambda j, i: (i, 0)),
                  pl.BlockSpec((None, cc, n), lambda j, i: (j, 0, 0)), blk, blk],
        out_specs=[blk, blk], out_shape=[wide, wide],
        compiler_params=pltpu.CompilerParams(dimension_semantics=("arbitrary", "arbitrary"),
                                             vmem_limit_bytes=MM_VMEM_LIMIT),
    )(dy, wd, g, u)


def _ffn_dh(dg, du, wg, wu):
    t = dg.shape[0]
    n_sh, k, cc = wg.shape

    def body(dg_ref, du_ref, wg_ref, wu_ref, o_ref):
        acc = jnp.zeros((MM_ROW_TILE, k), F32)
        for j in range(n_sh):
            cols = slice(j * cc, (j + 1) * cc)
            acc = acc + _nt(dg_ref[:, cols], wg_ref[j]) + _nt(du_ref[:, cols], wu_ref[j])
        o_ref[...] = acc

    blk = pl.BlockSpec((MM_ROW_TILE, n_sh * cc), lambda i: (i, 0))
    w_spec = pl.BlockSpec((n_sh, k, cc), lambda i: (0, 0, 0))
    return pl.pallas_call(
        body, name="ffn_dh", grid=(t // MM_ROW_TILE,),
        in_specs=[blk, blk, w_spec, w_spec],
        out_specs=pl.BlockSpec((MM_ROW_TILE, k), lambda i: (i, 0)),
        out_shape=jax.ShapeDtypeStruct((t, k), F32),
        compiler_params=pltpu.CompilerParams(dimension_semantics=("arbitrary",), vmem_limit_bytes=MM_VMEM_LIMIT),
    )(dg, du, wg, wu)


def _ffn_dw_in(h, dy, n_sh, name):
    t, k = h.shape
    cc = dy.shape[1] // n_sh
    tk = 512

    def body(h_ref, dy_ref, o_ref):
        o_ref[...] = _tn(h_ref[...].astype(BF16), dy_ref[...]).astype(BF16)

    return pl.pallas_call(
        body, name=name, grid=(n_sh, k // tk),
        in_specs=[pl.BlockSpec((t, tk), lambda j, i: (0, i)), pl.BlockSpec((t, cc), lambda j, i: (0, j))],
        out_specs=pl.BlockSpec((None, tk, cc), lambda j, i: (j, i, 0)),
        out_shape=jax.ShapeDtypeStruct((n_sh, k, cc), BF16),
        compiler_params=pltpu.CompilerParams(dimension_semantics=("arbitrary", "arbitrary"),
                                             vmem_limit_bytes=MM_VMEM_LIMIT),
    )(h, dy)


@jax.custom_vjp
def _ffn_block(h, wg, wu, wd):
    act = _ffn_in(h, wg, wu)[2]
    return _mm(act, wd.reshape(-1, wd.shape[2]), "nn", "ffn_down_fwd", MM_ROW_TILE, wd.shape[2])


def _ffn_block_fwd(h, wg, wu, wd):
    g, u, act = _ffn_in(h, wg, wu)
    y = _mm(act, wd.reshape(-1, wd.shape[2]), "nn", "ffn_down_fwd", MM_ROW_TILE, wd.shape[2])
    return y, (h, wg, wu, wd, g, u, act)


def _ffn_block_bwd(res, dy):
    h, wg, wu, wd, g, u, act = res
    dg, du = _ffn_mid_bwd(dy, wd, g, u)
    dh = _ffn_dh(dg, du, wg, wu)
    n_sh = wg.shape[0]
    dwg = _ffn_dw_in(h, dg, n_sh, "ffn_gate_dw")
    dwu = _ffn_dw_in(h, du, n_sh, "ffn_up_dw")
    dwd = _mm(act, dy, "tn", "ffn_down_dw", 256, wd.shape[2], out_dtype=BF16).reshape(wd.shape)
    return dh, dwg, dwu, dwd


_ffn_block.defvjp(_ffn_block_fwd, _ffn_block_bwd)


def _swap_halves(w):
    half = w.shape[-1] // 2
    return jnp.concatenate([w[..., half:], w[..., :half]], axis=-1)


def _pad_lanes(w):
    return jnp.concatenate([w, jnp.zeros(w.shape[:-1] + (LANES - w.shape[-1],), w.dtype)], axis=-1)


def _join_cols(shards):
    return shards.transpose(1, 0, 2).reshape(shards.shape[1], -1)


def _mod_parts(mod):
    return [mod[:, i * D_MODEL:(i + 1) * D_MODEL] for i in range(N_MOD)]


def _mixing_stage(x, mod, p, cos, sin):
    shift1, scale1 = _mod_parts(mod)[:2]

    w_in = _join_cols(p["w_in"])
    k_rope_w = w_in[:, 2176:2240]
    w_in_ext = jnp.concatenate([w_in[:, :2176], _pad_lanes(k_rope_w), _pad_lanes(_swap_halves(k_rope_w)),
                                jnp.zeros((D_MODEL, LANES), w_in.dtype)], axis=1)
    (h1,) = _make_rowwise("pre_attn", _f_pre_attn, 1, 3, [D_MODEL], [True], out_dtypes=[BF16])(
        x, p["norm_attn"], scale1, shift1)
    q_sb, k_sb, v_sb, cq, ckv, kr, kr_sw = _make_linear_split(
        "in_proj", (SB_WIDTH, SB_WIDTH, SB_WIDTH, MLA_Q_RANK, MLA_KV_RANK, LANES, LANES), 512)(h1, w_in_ext)

    o_sb = _sb_attention(q_sb, k_sb, v_sb)

    wq = _join_cols(p["w_q_up"]).reshape(MLA_Q_RANK, MLA_HEADS, MLA_QK)
    wq_n, wq_r = wq[:, :, :MLA_NOPE], wq[:, :, MLA_NOPE:]
    w_q_ext = jnp.concatenate([wq_n.reshape(MLA_Q_RANK, -1), _pad_lanes(wq_r).reshape(MLA_Q_RANK, -1),
                               _pad_lanes(_swap_halves(wq_r)).reshape(MLA_Q_RANK, -1)], axis=1)
    wkv = _join_cols(p["w_kv_up"]).reshape(MLA_KV_RANK, MLA_HEADS, MLA_NOPE + MLA_V)
    w_kv_ext = jnp.concatenate([wkv[:, :, :MLA_NOPE].reshape(MLA_KV_RANK, -1),
                                wkv[:, :, MLA_NOPE:].reshape(MLA_KV_RANK, -1)], axis=1)
    cqn, ckvn = _make_rowwise("mla_a", _f_mla_a, 2, 2, [MLA_Q_RANK, MLA_KV_RANK], [True, True],
                              out_dtypes=[BF16, BF16], grad_dtypes=[BF16, BF16])(
        cq, ckv, p["q_a_norm"], p["kv_a_norm"])
    qall = _make_linear("q_up", 384, 768)(cqn, w_q_ext)
    kn_all, v_mla = _make_linear_split("kv_up", (MLA_HEADS * MLA_NOPE, MLA_HEADS * MLA_V), MLA_KV_RANK)(ckvn, w_kv_ext)
    gq = p["q_norm"]
    gkr = p["k_rope_norm"]
    qn, qr, kn, krr = _make_rowwise("mla_b", _f_mla_b, 6, 6, [512, 512, 512, LANES],
                                    [True, True, True, True, False, False],
                                    out_dtypes=[BF16] * 4, grad_dtypes=[BF16] * 4)(
        qall, kn_all, kr, kr_sw, cos, sin,
        gq[:, :MLA_NOPE], _pad_lanes(gq[:, MLA_NOPE:]), _pad_lanes(_swap_halves(gq[:, MLA_NOPE:])),
        p["k_nope_norm"], _pad_lanes(gkr), _pad_lanes(_swap_halves(gkr)))
    o_mla = _mla_attention(qn, qr, kn, krr, v_mla)

    (mixed,) = _make_rowwise("post_attn", _f_post_attn, 2, 2, [D_MODEL], [True, True])(
        o_sb, o_mla, p["out_norm_sb"], p["out_norm_mla"])
    return mixed


def _ffn_stage(x, mixed, mod, p):
    _, _, gate1, shift2, scale2, _ = _mod_parts(mod)
    attn = _make_linear("out_proj", 512, 512)(mixed, p["w_out"].reshape(D_MODEL, D_MODEL))

    x2, h2 = _make_rowwise("pre_ffn", _f_pre_ffn, 2, 4, [D_MODEL, D_MODEL], [True, True],
                           out_dtypes=[F32, BF16], grad_dtypes=[F32, BF16])(
        x, attn, gate1, p["norm_ffn"], scale2, shift2)
    return x2, _ffn_block(h2, p["w_gate"], p["w_up"], p["w_down"])


def _my_place():
    return lax.axis_index("x"), lax.axis_index("y"), lax.axis_index("c")


def _small_gather(x_ref, out_ref, send_sems, recv_sems, base, local_sem):
    m_per = x_ref.shape[0]
    x, y, c = _my_place()
    me, sibling = (x, y, c), (x, y, 1 - c)
    chips = [(1 - x, y), (x, 1 - y), (1 - x, 1 - y)]

    def rows(px, py, pc):
        return out_ref.at[pl.ds((4 * px + 2 * py + pc) * m_per, m_per), :]

    def copy(k, blk, to, src=None):
        return _remote(rows(*blk) if src is None else src, rows(*blk), send_sems, recv_sems, base + k, to)

    mine = pltpu.make_async_copy(x_ref, rows(*me), local_sem)
    first = [copy(0, me, sibling, src=x_ref)] + [copy(1 + j, me, (*chip, c), src=x_ref) for j, chip in enumerate(chips)]
    passed = [copy(4 + j, (*chip, c), sibling) for j, chip in enumerate(chips)]

    def start():
        mine.start()
        for cp in first:
            cp.start()

    def finish():
        for j, chip in enumerate(chips):
            copy(1 + j, (*chip, c), me).wait_recv()
            passed[j].start()
        copy(0, sibling, me).wait_recv()
        for j, chip in enumerate(chips):
            copy(4 + j, (*chip, 1 - c), me).wait_recv()
        for cp in first + passed:
            cp.wait_send()
        mine.wait()

    return start, finish


EARLY =("w_in", "w_q_up", "w_kv_up")
LATE = ("w_out", "w_gate", "w_up", "w_down")
BIG = EARLY + LATE
TRANSPOSED_UPDATE = ("w_in", "w_gate", "w_up")
HALF_AXIS = {"w_in": 0, "w_q_up": 0, "w_kv_up": 0, "w_out": 0, "w_gate": 0, "w_up": 0, "w_down": 1}


def _half(ref, h, axis, lead=()):
    trail = ref.shape[len(lead):]
    idx = list(lead) + [slice(None)] * len(trail)
    at = len(trail) - 2 + axis
    n2 = trail[at] // 2
    idx[len(lead) + at] = pl.ds(h * n2, n2)
    return ref.at[tuple(idx)]


def _half_shape(shape, axis):
    shape = list(shape)
    shape[len(shape) - 2 + axis] //= 2
    return tuple(shape)


def _remote(src, dst, send_sems, recv_sems, k, to):
    return pltpu.make_async_remote_copy(src_ref=src, dst_ref=dst, send_sem=send_sems.at[k],
                                        recv_sem=recv_sems.at[k], device_id=to, device_id_type=MESH)


def _gather_weights(names, shards, small_block):
    n_w = len(shards)
    axes = [HALF_AXIS[n] for n in names]

    def body(*refs):
        w_refs, small_ref = refs[:n_w], refs[n_w]
        out_refs, token, small_out = refs[n_w + 1:2 * n_w + 1], refs[2 * n_w + 1], refs[2 * n_w + 2]
        send_sems, recv_sems, local_sems = refs[2 * n_w + 3:]
        token[...] = jnp.zeros_like(token)
        x, y, c = _my_place()
        sibling = (x, y, 1 - c)
        chips = [(1 - x, y), (x, 1 - y), (1 - x, 1 - y)]
        me = 2 * x + y
        small_start, small_finish = _small_gather(small_ref, small_out, send_sems, recv_sems, 6 * n_w,
                                                  local_sems.at[n_w])
        small_start()
        mine = [pltpu.make_async_copy(w, o.at[me], local_sems.at[i]) for i, (w, o) in enumerate(zip(w_refs, out_refs))]
        for cp in mine:
            cp.start()
        first = [_remote(_half(w_refs[i], c, axes[i]), _half(out_refs[i], c, axes[i], (me,)),
                         send_sems, recv_sems, 6 * i + j, (*chip, c))
                 for i in range(n_w) for j, chip in enumerate(chips)]
        for cp in first:
            cp.start()
        small_finish()
        passed = []
        for j, (cx, cy) in enumerate(chips):
            for i in range(n_w):
                blk = _half(out_refs[i], c, axes[i], (2 * cx + cy,))
                _remote(blk, blk, send_sems, recv_sems, 6 * i + j, (cx, cy, c)).wait_recv()
                cp = _remote(blk, blk, send_sems, recv_sems, 6 * i + 3 + j, sibling)
                cp.start()
                passed.append(cp)
        for j, (cx, cy) in enumerate(chips):
            for i in range(n_w):
                blk = _half(out_refs[i], 1 - c, axes[i], (2 * cx + cy,))
                _remote(blk, blk, send_sems, recv_sems, 6 * i + 3 + j, sibling).wait_recv()
        for cp in first + passed:
            cp.wait_send()
        for cp in mine:
            cp.wait()

    outs = pl.pallas_call(
        body, name="gather_weights",
        out_shape=[jax.ShapeDtypeStruct((N_CHIPS,) + s.shape, s.dtype) for s in shards]
        + [jax.ShapeDtypeStruct((8, LANES), F32),
           jax.ShapeDtypeStruct((N_DEV * small_block.shape[0], small_block.shape[1]), small_block.dtype)],
        in_specs=[ANY] * (n_w + 1), out_specs=[ANY] * n_w + [pl.BlockSpec(memory_space=pltpu.VMEM), ANY],
        scratch_shapes=[pltpu.SemaphoreType.DMA((6 * n_w + 7,)), pltpu.SemaphoreType.DMA((6 * n_w + 7,)),
                        pltpu.SemaphoreType.DMA((n_w + 1,))],
    )(*shards, small_block)
    return outs[:n_w], outs[n_w], outs[n_w + 1]


def _pair_exchange(names, grads, call_name, small_block):
    n_w = len(grads)
    axes = [HALF_AXIS[n] for n in names]

    def body(*refs):
        g_refs, small_ref = refs[:n_w], refs[n_w]
        t_refs, small_out = refs[n_w + 1:2 * n_w + 1], refs[2 * n_w + 1]
        send_sems, recv_sems, local_sem = refs[2 * n_w + 2:]
        x, y, c = _my_place()
        small_start, small_finish = _small_gather(small_ref, small_out, send_sems, recv_sems, n_w, local_sem)
        small_start()
        sends = [_remote(_half(g_refs[i], 1 - c, axes[i]), t_refs[i], send_sems, recv_sems, i, (x, y, 1 - c))
                 for i in range(n_w)]
        for cp in sends:
            cp.start()
        small_finish()
        for cp in sends:
            cp.wait_recv()
        for cp in sends:
            cp.wait_send()

    outs = pl.pallas_call(
        body, name=call_name,
        out_shape=[jax.ShapeDtypeStruct(_half_shape(g.shape, a), g.dtype) for g, a in zip(grads, axes)]
        + [jax.ShapeDtypeStruct((N_DEV * small_block.shape[0], small_block.shape[1]), small_block.dtype)],
        in_specs=[ANY] * (n_w + 1), out_specs=[ANY] * (n_w + 1),
        scratch_shapes=[pltpu.SemaphoreType.DMA((n_w + 7,)), pltpu.SemaphoreType.DMA((n_w + 7,)),
                        pltpu.SemaphoreType.DMA],
    )(*grads, small_block)
    return outs[:n_w], outs[n_w]


def _sibling_join(halves, name, after):
    n_w = len(halves)

    def body(*refs):
        s_refs, j_refs = refs[:n_w], refs[n_w + 1:2 * n_w + 1]
        send_sems, recv_sems = refs[2 * n_w + 1:]
        x, y, c = _my_place()
        sends = [_remote(s_refs[i], j_refs[i], send_sems, recv_sems, i, (x, y, 1 - c)) for i in range(n_w)]
        for cp in sends:
            cp.start()
        for cp in sends:
            cp.wait_recv()
        for cp in sends:
            cp.wait_send()

    return pl.pallas_call(
        body, name=name,
        out_shape=[jax.ShapeDtypeStruct(s.shape, s.dtype) for s in halves],
        in_specs=[ANY] * (n_w + 1), out_specs=[ANY] * n_w,
        scratch_shapes=[pltpu.SemaphoreType.DMA((n_w,)), pltpu.SemaphoreType.DMA((n_w,))],
    )(*halves, after)


HBM_SPEC = pl.BlockSpec(memory_space=pltpu.HBM)
SEM_SPEC = pl.BlockSpec(memory_space=pltpu.SEMAPHORE)
DATAFLOW = pltpu.SideEffectType.DATAFLOW_SIDE_EFFECTING


def _in_hbm(a):
    return pltpu.with_memory_space_constraint(a, pltpu.HBM)


def _exchange_start(name, srcs, lands, plan, n_copies, after, thru):
    n = len(srcs)

    def body(*refs):
        src_refs, land_refs = refs[:n], refs[n:2 * n]
        send_sems, recv_sems = refs[2 * n + 2], refs[2 * n + 3]
        for k, (src, dst, to, k_recv) in enumerate(plan(src_refs, land_refs)):
            pltpu.make_async_remote_copy(src_ref=src, dst_ref=dst, send_sem=send_sems.at[k],
                                         recv_sem=recv_sems.at[k_recv], device_id=to, device_id_type=MESH).start()

    outs = pl.pallas_call(
        body, name=name,
        out_shape=(pltpu.SemaphoreType.DMA((n_copies,)), pltpu.SemaphoreType.DMA((n_copies,)),
                   *[pltpu.HBM(a.shape, a.dtype) for a in list(srcs) + list(lands) + [thru]]),
        in_specs=[HBM_SPEC] * (2 * n + 1) + [ANY],
        out_specs=(SEM_SPEC, SEM_SPEC, *[HBM_SPEC] * (2 * n + 1)),
        input_output_aliases={i: 2 + i for i in range(2 * n + 1)},
        compiler_params=pltpu.CompilerParams(has_side_effects=DATAFLOW),
    )(*[_in_hbm(a) for a in list(srcs) + list(lands) + [thru]], after)
    return outs[0], outs[1], outs[2:2 + n], outs[2 + n:2 + 2 * n], outs[2 + 2 * n]


def _exchange_wait(name, started, plan, after):
    send_sems, recv_sems, srcs, lands, _ = started
    n = len(srcs)

    def body(*refs):
        src_refs, land_refs = refs[:n], refs[n:2 * n]
        s_sems, r_sems = refs[2 * n], refs[2 * n + 1]
        for k, (src, dst, to, _) in enumerate(plan(src_refs, land_refs)):
            cp = _remote(src, dst, s_sems, r_sems, k, to)
            cp.wait_send()
            cp.wait_recv()

    outs = pl.pallas_call(
        body, name=name,
        out_shape=tuple(pltpu.HBM(a.shape, a.dtype) for a in list(srcs) + list(lands)),
        in_specs=[HBM_SPEC] * (2 * n) + [SEM_SPEC, SEM_SPEC, ANY],
        out_specs=tuple([HBM_SPEC] * (2 * n)),
        input_output_aliases={i: i for i in range(2 * n)},
        compiler_params=pltpu.CompilerParams(has_side_effects=DATAFLOW),
    )(*srcs, *lands, send_sems, recv_sems, after)
    return outs[:n], outs[n:]


def _late_gather_plan(src_refs, land_refs):
    x, y, c = _my_place()
    chips = [(1 - x, y), (x, 1 - y), (1 - x, 1 - y)]
    plan = [(src, land.at[2 * x + y], (cx, cy, c)) for src, land in zip(src_refs, land_refs) for cx, cy in chips]
    return [entry + (k,) for k, entry in enumerate(plan)]


def _late_scatter_plan(src_refs, land_refs):
    x, y, c = _my_place()
    chips = [(1 - x, y), (x, 1 - y), (1 - x, 1 - y)]
    plan = [(src.at[2 * cx + cy], land.at[j], (cx, cy, c))
            for src, land in zip(src_refs, land_refs) for j, (cx, cy) in enumerate(chips)]
    return [entry + (k,) for k, entry in enumerate(plan)]


def _direct_scatter_plan(names):
    axes = [HALF_AXIS[n] for n in names]

    def plan(src_refs, land_refs):
        x, y, c = _my_place()
        chips = [(1 - x, y), (x, 1 - y), (1 - x, 1 - y)]
        out = []
        for i, (src, land) in enumerate(zip(src_refs, land_refs)):
            for f, (cx, cy) in enumerate(chips):
                for core in range(2):
                    out.append((_half(src, core, axes[i], (2 * cx + cy,)), land.at[2 * f + c], (cx, cy, core),
                                7 * i + 2 * f + c))
            out.append((_half(src, 1 - c, axes[i], (2 * x + y,)), land.at[6], (x, y, 1 - c), 7 * i + 6))
        return out

    return plan


def _row_tile(rows, mult=16, limit=ROW_TILE):
    return max(d for d in range(mult, limit + 1, mult) if rows % d == 0)


def _pair_sum(place, g, theirs, axis, name):
    nj, rr, cc = theirs.shape
    tr = _row_tile(rr, limit=1024)
    nb = rr // tr
    if axis == 0:
        g_map = lambda j, i, pr: (j, pr[0] * nb + i, 0)
    else:
        g_map = lambda j, i, pr: (j, i, pr[0])

    def body(pr, g_ref, t_ref, o_ref):
        o_ref[...] = (g_ref[...].astype(F32) + t_ref[...].astype(F32)).astype(BF16)

    spec = pl.BlockSpec((None, tr, cc), lambda j, i, pr: (j, i, 0))
    return pl.pallas_call(
        body, name=name,
        grid_spec=pltpu.PrefetchScalarGridSpec(
            num_scalar_prefetch=1, grid=(nj, nb),
            in_specs=[pl.BlockSpec((None, tr, cc), g_map), spec], out_specs=spec),
        out_shape=jax.ShapeDtypeStruct(theirs.shape, BF16))(place, g, theirs)


def _chip_sum(place, pair_sums, parts, name, transposed):
    _, rr, cc = parts.shape
    tr = _row_tile(rr, LANES) if transposed else _row_tile(rr, limit=1024)

    def body(pr, h_ref, p_ref, o_ref):
        acc = p_ref[0].astype(F32)
        for j in range(1, N_CHIPS - 1):
            acc = acc + p_ref[j].astype(F32)
        acc = acc + h_ref[...].astype(F32)
        o_ref[...] = (acc.T if transposed else acc).astype(BF16)

    out_spec = pl.BlockSpec((cc, tr), lambda i, pr: (0, i)) if transposed else pl.BlockSpec((tr, cc), lambda i, pr: (i, 0))
    return pl.pallas_call(
        body, name=name,
        grid_spec=pltpu.PrefetchScalarGridSpec(
            num_scalar_prefetch=1, grid=(rr // tr,),
            in_specs=[pl.BlockSpec((None, tr, cc), lambda i, pr: (pr[1], i, 0)),
                      pl.BlockSpec((N_CHIPS - 1, tr, cc), lambda i, pr: (0, i, 0))],
            out_specs=out_spec),
        out_shape=jax.ShapeDtypeStruct((cc, rr) if transposed else (rr, cc), BF16))(place, pair_sums, parts)


def _chip_sum_direct(place, g, parts, axis, name, transposed):
    n_parts, rr, cc = parts.shape
    tr = _row_tile(rr, LANES) if transposed else _row_tile(rr, limit=1024)
    nb = rr // tr
    if axis == 0:
        g_map = lambda i, pr: (pr[1], pr[0] * nb + i, 0)
    else:
        g_map = lambda i, pr: (pr[1], i, pr[0])

    def body(pr, g_ref, p_ref, o_ref):
        acc = p_ref[0].astype(F32)
        for j in range(1, n_parts):
            acc = acc + p_ref[j].astype(F32)
        acc = acc + g_ref[...].astype(F32)
        o_ref[...] = (acc.T if transposed else acc).astype(BF16)

    out_spec = pl.BlockSpec((cc, tr), lambda i, pr: (0, i)) if transposed else pl.BlockSpec((tr, cc), lambda i, pr: (i, 0))
    return pl.pallas_call(
        body, name=name,
        grid_spec=pltpu.PrefetchScalarGridSpec(
            num_scalar_prefetch=1, grid=(nb,),
            in_specs=[pl.BlockSpec((None, tr, cc), g_map), pl.BlockSpec((n_parts, tr, cc), lambda i, pr: (0, i, 0))],
            out_specs=out_spec),
        out_shape=jax.ShapeDtypeStruct((cc, rr) if transposed else (rr, cc), BF16))(place, g, parts)


def _silu(v):
    return v / (1.0 + jnp.exp(-v))


def _ada_fwd(c_all, w_shard, b_shard):
    n_seq, n_cols = c_all.shape[0], w_shard.shape[1]

    def body(c_ref, w_ref, b_ref, o_ref, mine_ref, send_sems, recv_sems, local_sem):
        mine_ref[...] = jnp.dot(_silu(c_ref[...]), w_ref[...], precision=lax.Precision.HIGHEST,
                                preferred_element_type=F32) + b_ref[...]
        start, finish = _small_gather(mine_ref, o_ref, send_sems, recv_sems, 0, local_sem)
        start()
        finish()

    return pl.pallas_call(
        body, name="ada_fwd", out_shape=jax.ShapeDtypeStruct((N_DEV * n_seq, n_cols), F32),
        scratch_shapes=[pltpu.VMEM((n_seq, n_cols), F32), pltpu.SemaphoreType.DMA((7,)), pltpu.SemaphoreType.DMA((7,)),
                        pltpu.SemaphoreType.DMA],
        compiler_params=pltpu.CompilerParams(vmem_limit_bytes=MM_VMEM_LIMIT))(c_all, w_shard, b_shard)


def _loss_and_grads(x2, ffn, target, gate):
    t, d = x2.shape

    def half_loss(x2_blk, ffn_blk, gate_row, target_blk):
        return 0.5 * _f_loss(x2_blk, ffn_blk, target_blk, gate_row)[0]

    def body(x2_ref, ffn_ref, tgt_ref, gate_ref, loss_ref, dx2_ref, dffn_ref, dgate_ref):
        rows, vjp = jax.vjp(lambda a, b, g: half_loss(a, b, g, tgt_ref[...]), x2_ref[...], ffn_ref[...], gate_ref[...])
        loss_ref[...] = rows
        dx2_ref[...], dffn_ref[...], dgate = vjp(jnp.ones_like(rows))

        @pl.when(pl.program_id(0) == 0)
        def _():
            dgate_ref[...] = jnp.zeros_like(dgate_ref)

        dgate_ref[...] += dgate

    blk = pl.BlockSpec((ROW_TILE, d), lambda i: (i, 0))
    row = pl.BlockSpec((1, d), lambda i: (0, 0))
    return pl.pallas_call(
        body, name="loss_and_grads", grid=(t // ROW_TILE,),
        in_specs=[blk, blk, blk, row],
        out_specs=[pl.BlockSpec((ROW_TILE, 1), lambda i: (i, 0)), blk, blk, row],
        out_shape=[jax.ShapeDtypeStruct((t, 1), F32), jax.ShapeDtypeStruct((t, d), F32), jax.ShapeDtypeStruct((t, d), F32),
                   jax.ShapeDtypeStruct((1, d), F32)],
        compiler_params=pltpu.CompilerParams(dimension_semantics=("arbitrary",), vmem_limit_bytes=MM_VMEM_LIMIT),
    )(x2, ffn, target, gate)


def _ada_bwd(c_all, dmod_cols):
    def body(c_ref, d_ref, o_ref):
        o_ref[...] = lax.dot_general(_silu(c_ref[...]), d_ref[...], (((0,), (0,)), ((), ())),
                                     precision=lax.Precision.HIGHEST, preferred_element_type=F32)

    return pl.pallas_call(body, name="ada_bwd", out_shape=jax.ShapeDtypeStruct((c_all.shape[1], dmod_cols.shape[1]), F32),
                          compiler_params=pltpu.CompilerParams(vmem_limit_bytes=MM_VMEM_LIMIT))(c_all, dmod_cols)


def _adamw_math(w, g, m, v):
    m = ADAM_B1 * m + (1.0 - ADAM_B1) * g
    v = ADAM_B2 * v + (1.0 - ADAM_B2) * (g * g)
    m_hat = m / (1.0 - ADAM_B1 ** ADAM_STEP)
    v_hat = v / (1.0 - ADAM_B2 ** ADAM_STEP)
    delta = -ADAM_LR * (m_hat / (jnp.sqrt(v_hat) + ADAM_EPS) + ADAM_WD * w)
    return delta, m, v


def _adamw(w, g, m, v, name):
    r, ccols = w.shape
    tr = max(d for d in range(8, ROW_TILE + 1, 8) if r % d == 0)
    spec = pl.BlockSpec((tr, ccols), lambda i: (i, 0))

    def body(w_ref, g_ref, m_ref, v_ref, d_ref, nm_ref, nv_ref):
        d_ref[...], nm_ref[...], nv_ref[...] = _adamw_math(w_ref[...], g_ref[...], m_ref[...], v_ref[...])

    return pl.pallas_call(body, name=name, grid=(r // tr,), in_specs=[spec] * 4, out_specs=[spec] * 3,
                          out_shape=[jax.ShapeDtypeStruct(w.shape, F32)] * 3,
                          compiler_params=pltpu.CompilerParams(vmem_limit_bytes=MM_VMEM_LIMIT))(w, g, m, v)


def _small_layout(sizes):
    offs, off = [], 0
    for n in sizes:
        offs.append(off)
        off += -(-n // LANES) * LANES
    total = -(-(off + LANES) // (8 * LANES)) * (8 * LANES)
    return offs, off, total


def _adamw_small(ws, g_all, ms, vs, offs, loss_off):
    n_p = len(ws)

    def device_sum(g_ref, off, width):
        blk = g_ref[:, off:off + width]
        acc = blk[0:1]
        for d in range(1, N_DEV):
            acc = acc + blk[d:d + 1]
        return acc

    def body(*refs):
        w_refs, m_refs, v_refs = refs[:n_p], refs[n_p:2 * n_p], refs[2 * n_p:3 * n_p]
        g_ref = refs[3 * n_p]
        outs = refs[3 * n_p + 1:]
        for i in range(n_p):
            n = w_refs[i].shape[1]
            g = device_sum(g_ref, offs[i], -(-n // LANES) * LANES)[:, :n]
            outs[i][...] = g
            outs[n_p + i][...], outs[2 * n_p + i][...], outs[3 * n_p + i][...] = _adamw_math(
                w_refs[i][...], g, m_refs[i][...], v_refs[i][...])
        outs[4 * n_p][...] = device_sum(g_ref, loss_off, LANES)

    res = pl.pallas_call(
        body, name="adamw_small",
        out_shape=[jax.ShapeDtypeStruct(a.shape, F32) for a in list(ws) * 4] + [jax.ShapeDtypeStruct((1, LANES), F32)],
    )(*ws, *ms, *vs, g_all)
    return res[:n_p], res[n_p:2 * n_p], res[2 * n_p:3 * n_p], res[3 * n_p:4 * n_p], res[4 * n_p]


def _adamw_halves(place, w, own, sib, m, v, axis, name, after):
    r, cc = w.shape
    if axis == 0:
        rows, gc = own.shape[0], own.shape[1]
        tr = _row_tile(rows)
        nb = rows // tr
        w_spec = pl.BlockSpec((tr, cc), lambda h, i, pr: (h * nb + i, 0))
        g_spec = pl.BlockSpec((tr, gc), lambda h, i, pr: (i, 0))
    else:
        tr = _row_tile(r)
        nb = r // tr
        gc = own.shape[1]
        w_spec = pl.BlockSpec((tr, gc), lambda h, i, pr: (i, h))
        g_spec = pl.BlockSpec((tr, gc), lambda h, i, pr: (i, 0))
    wc = w_spec.block_shape[1]

    def body(pr, w_ref, o_ref, s_ref, m_ref, v_ref, after_ref, g_ref, d_ref, nm_ref, nv_ref):
        g = jnp.where(pl.program_id(0) == pr[0], o_ref[...], s_ref[...]).astype(F32)[:, :wc]
        g_ref[...] = g
        d_ref[...], nm_ref[...], nv_ref[...] = _adamw_math(w_ref[...], g, m_ref[...], v_ref[...])

    return pl.pallas_call(
        body, name=name,
        grid_spec=pltpu.PrefetchScalarGridSpec(
            num_scalar_prefetch=1, grid=(2, nb),
            in_specs=[w_spec, g_spec, g_spec, w_spec, w_spec, ANY], out_specs=[w_spec] * 4),
        out_shape=[jax.ShapeDtypeStruct(w.shape, F32)] * 4,
        compiler_params=pltpu.CompilerParams(vmem_limit_bytes=MM_VMEM_LIMIT))(place, w, own, sib, m, v, after)


SMALL = ("b_ada", "norm_attn", "norm_ffn", "q_a_norm", "kv_a_norm", "q_norm", "k_nope_norm", "k_rope_norm",
         "out_norm_sb", "out_norm_mla")
WEIGHTS = ("w_ada", "b_ada", "norm_attn", "norm_ffn", "w_in", "q_a_norm", "w_q_up", "kv_a_norm", "w_kv_up",
           "q_norm", "k_nope_norm", "k_rope_norm", "out_norm_sb", "out_norm_mla", "w_out", "w_gate", "w_up",
           "w_down")


def kernel(x, c, positions, w_ada, b_ada, norm_attn, norm_ffn, w_in, q_a_norm, w_q_up, kv_a_norm, w_kv_up, q_norm, k_nope_norm, k_rope_norm, out_norm_sb, out_norm_mla, w_out, w_gate, w_up, w_down, loss_target, m_w_ada, m_b_ada, m_norm_attn, m_norm_ffn, m_w_in, m_q_a_norm, m_w_q_up, m_kv_a_norm, m_w_kv_up, m_q_norm, m_k_nope_norm, m_k_rope_norm, m_out_norm_sb, m_out_norm_mla, m_w_out, m_w_gate, m_w_up, m_w_down, v_w_ada, v_b_ada, v_norm_attn, v_norm_ffn, v_w_in, v_q_a_norm, v_w_q_up, v_kv_a_norm, v_w_kv_up, v_q_norm, v_k_nope_norm, v_k_rope_norm, v_out_norm_sb, v_out_norm_mla, v_w_out, v_w_gate, v_w_up, v_w_down):
    local = dict(locals())
    w = {n: local[n][0] for n in WEIGHTS}
    m = {n: local["m_" + n][0] for n in WEIGHTS}
    v = {n: local["v_" + n][0] for n in WEIGHTS}
    small = {n: w[n].reshape(1, -1) for n in SMALL}
    ix, iy, ic = _my_place()
    chip = 2 * ix + iy
    dev = 2 * chip + ic
    xs, target = x[0], loss_target[0]
    seq = xs.shape[0]

    ff_pad = FF_SHARD_PAD - FF_SHARD
    pads = {"w_gate": ((0, 0), (0, ff_pad)), "w_up": ((0, 0), (0, ff_pad)), "w_down": ((0, ff_pad), (0, 0))}
    shards = {n: jnp.pad(w[n].astype(BF16), pads[n]) if n in pads else w[n].astype(BF16) for n in BIG}
    early, early_done, c_gathered = _gather_weights(EARLY, [shards[n] for n in EARLY], c.reshape(8, LANES))
    gathered = dict(zip(EARLY, early))

    c_all = c_gathered.reshape(N_DEV, D_MODEL)
    ada_cols = w["w_ada"].shape[1]
    b_cols = lax.dynamic_slice_in_dim(small["b_ada"], chip * ada_cols, ada_cols, axis=1)
    mod_all = _ada_fwd(c_all, w["w_ada"], b_cols).reshape(N_CHIPS, 2, N_DEV, ada_cols)
    mod = lax.dynamic_index_in_dim(mod_all[:, 0], dev, axis=1, keepdims=False).reshape(1, N_MOD * D_MODEL)

    lands = [lax.dynamic_update_index_in_dim(lax.empty((N_CHIPS,) + shards[n].shape, BF16), shards[n], chip, 0)
             for n in LATE]
    late_gather = _exchange_start("gather_late_start", [shards[n] for n in LATE], lands, _late_gather_plan,
                                  3 * len(LATE), early_done, mod)
    mod = late_gather[4]

    half = MLA_ROPE // 2
    freqs = 1.0 / (ROPE_THETA ** (np.arange(half, dtype=np.float32) / half))
    zeros = np.zeros(LANES - MLA_ROPE, np.float32)
    freqs_row = jnp.asarray(np.concatenate([freqs, freqs, zeros]).astype(np.float32)[None])
    sign_row = jnp.asarray(np.concatenate([-np.ones(half), np.ones(half), zeros]).astype(np.float32)[None])
    cos, sin = _rope_tables(positions.reshape(seq, 1), freqs_row, sign_row)

    place = jnp.stack([ic, chip]).astype(jnp.int32)
    small_params = {n: small[n] for n in SMALL if n != "b_ada"}

    p1 = {**{n: gathered[n] for n in EARLY}, **small_params}
    mixed, mixing_vjp = jax.vjp(lambda x_, mod_, p_: _mixing_stage(x_, mod_, p_, cos, sin), xs, mod, p1)
    _, landed = _exchange_wait("gather_late_wait", late_gather, _late_gather_plan, mixed)
    p2 = {**dict(zip(LATE, landed)), **small_params}
    (x2, ffn), ffn_vjp = jax.vjp(_ffn_stage, xs, mixed, mod, p2)
    loss_rows, g_x2, g_ffn, g_gate2 = _loss_and_grads(x2, ffn, target, _mod_parts(mod)[5])
    loss_part = jnp.sum(loss_rows)
    gx2, gmixed, gmod2, gp2 = ffn_vjp((g_x2, g_ffn))
    gmod2 = gmod2 + jnp.concatenate([jnp.zeros((1, (N_MOD - 1) * D_MODEL), F32), g_gate2], axis=1)
    late_grads = [gp2[n] for n in LATE]
    late_plan = _direct_scatter_plan(LATE)
    late_scatter = _exchange_start(
        "grad_scatter_late_start", late_grads,
        [lax.empty((7,) + _half_shape(gr.shape[1:], HALF_AXIS[n]), BF16) for n, gr in zip(LATE, late_grads)],
        late_plan, 7 * len(LATE), gx2, gmixed)
    gx1, gmod1, gp1 = mixing_vjp(late_scatter[4])
    gx = gx1 + gx2
    gmod = gmod1 + gmod2
    gp = {n: gp1[n] + gp2[n] for n in small_params}

    sizes = [w[n].size for n in SMALL]
    offs, loss_off, n_small = _small_layout(sizes)
    pieces = []
    for n, size in zip(SMALL, sizes):
        pieces.append(gmod if n == "b_ada" else gp[n])
        if size % LANES:
            pieces.append(jnp.zeros((1, LANES - size % LANES), F32))
    pieces += [jnp.full((1, LANES), loss_part), jnp.zeros((1, n_small - loss_off - LANES), F32)]
    small_vec = jnp.concatenate(pieces, axis=1)

    g, delta, new_m, new_v = {}, {}, {}, {}

    def update(names, own, sib, after):
        for n, o, s in zip(names, own, sib):
            if n in TRANSPOSED_UPDATE:
                res = _adamw_halves(place, w[n].T, o, s, m[n].T, v[n].T, 1, "adamw_" + n, after)
                g[n], delta[n], new_m[n], new_v[n] = [r.T for r in res]
            else:
                g[n], delta[n], new_m[n], new_v[n] = _adamw_halves(place, w[n], o, s, m[n], v[n], HALF_AXIS[n],
                                                                   "adamw_" + n, after)

    late_grads, late_parts = _exchange_wait("grad_scatter_late_wait", late_scatter, late_plan, gx)
    own_late = [_chip_sum_direct(place, gr, pt, HALF_AXIS[n], "grad_chip_sum_" + n, n in TRANSPOSED_UPDATE)
                for n, gr, pt in zip(LATE, late_grads, late_parts)]
    early_grads = [gp1[n] for n in EARLY]
    theirs, small_gathered = _pair_exchange(EARLY, early_grads, "grad_pair_exchange_early",
                                            small_vec.reshape(8, n_small // 8))
    small_all = small_gathered.reshape(N_DEV, n_small)
    sib_late = _sibling_join(own_late, "grad_sibling_join_late", small_all)
    early_sums = [_pair_sum(place, gr, th, HALF_AXIS[n], "grad_pair_sum_" + n)
                  for n, gr, th in zip(EARLY, early_grads, theirs)]
    early_scatter = _exchange_start(
        "grad_scatter_early_start", early_sums,
        [lax.empty((N_CHIPS - 1,) + s.shape[1:], BF16) for s in early_sums], _late_scatter_plan, 3 * len(EARLY),
        sib_late[0], small_all)
    small_all = early_scatter[4]
    update(LATE, own_late, sib_late, small_all)

    *small_out, loss_row = _adamw_small([small[n] for n in SMALL], small_all, [m[n].reshape(1, -1) for n in SMALL],
                                        [v[n].reshape(1, -1) for n in SMALL], offs, loss_off)
    loss = loss_row[0, 0]
    for d, outs_d in zip((g, delta, new_m, new_v), small_out):
        d.update({n: o.reshape(w[n].shape) for n, o in zip(SMALL, outs_d)})

    dmod_all = small_all[:, :N_MOD * D_MODEL]
    g["w_ada"] = _ada_bwd(c_all, lax.dynamic_slice_in_dim(dmod_all, chip * ada_cols, ada_cols, axis=1))
    delta["w_ada"], new_m["w_ada"], new_v["w_ada"] = _adamw(w["w_ada"], g["w_ada"], m["w_ada"], v["w_ada"], "adamw_w_ada")

    early_sums, early_parts = _exchange_wait("grad_scatter_early_wait", early_scatter, _late_scatter_plan,
                                             delta["w_ada"])
    own_early = [_chip_sum(place, ps, pt, "grad_chip_sum_" + n, n in TRANSPOSED_UPDATE)
                 for n, ps, pt in zip(EARLY, early_sums, early_parts)]
    sib_early = _sibling_join(own_early, "grad_sibling_join_early", delta["w_ada"])
    update(EARLY, own_early, sib_early, sib_early[0])

    def outs(d):
        return [d[n][None] for n in WEIGHTS]

    return (loss, gx[None], *outs(g), *outs(delta), *outs(new_m), *outs(new_v))
```

```python
import numpy as np
import jax
import jax.numpy as jnp
from jax import lax
from jax.experimental import pallas as pl
from jax.experimental.pallas import tpu as pltpu

F32 = jnp.float32
BF16 = jnp.bfloat16
MESH = pl.DeviceIdType.MESH
ANY = pl.BlockSpec(memory_space=pl.ANY)

D_MODEL = 1024
SB_HEADS = 8
SB_HEAD_DIM = 64
SB_WIDTH = 512
MLA_HEADS = 4
MLA_NOPE = 128
MLA_ROPE = 64
MLA_QK = 192
MLA_V = 128
MLA_Q_RANK = 384
MLA_KV_RANK = 256
D_FF = 2816
N_MOD = 6
ROPE_THETA = 10000.0
EPS = 1e-6
LANES = 128

ADAM_LR = 0.001
ADAM_B1 = 0.9
ADAM_B2 = 0.999
ADAM_EPS = 1e-08
ADAM_WD = 0.01
ADAM_STEP = 10

N_CHIPS = 4
N_DEV = 8
ROW_TILE = 512
MM_ROW_TILE = 512
ATT_BLK = 256
MM_VMEM_LIMIT = 56 * 1024 * 1024
FF_SHARD = D_FF // N_CHIPS
FF_SHARD_PAD = 768


def _mm(a, b, mode, name, tm, tn, out_dtype=F32):
    if mode == "nn":
        (m, k), n = a.shape, b.shape[1]
        a_spec = pl.BlockSpec((tm, k), lambda j, i: (i, 0))
        b_spec = pl.BlockSpec((k, tn), lambda j, i: (0, j))
        dims = (((1,), (0,)), ((), ()))
    elif mode == "nt":
        (m, k), n = a.shape, b.shape[0]
        a_spec = pl.BlockSpec((tm, k), lambda j, i: (i, 0))
        b_spec = pl.BlockSpec((tn, k), lambda j, i: (j, 0))
        dims = (((1,), (1,)), ((), ()))
    else:
        (k, m), n = a.shape, b.shape[1]
        a_spec = pl.BlockSpec((k, tm), lambda j, i: (0, i))
        b_spec = pl.BlockSpec((k, tn), lambda j, i: (0, j))
        dims = (((0,), (0,)), ((), ()))
    assert m % tm == 0 and n % tn == 0, (name, m, n, tm, tn)

    def body(a_ref, b_ref, o_ref):
        o_ref[...] = lax.dot_general(a_ref[...].astype(BF16), b_ref[...].astype(BF16), dims,
                                     preferred_element_type=F32).astype(out_dtype)

    return pl.pallas_call(
        body, name=name, grid=(n // tn, m // tm),
        in_specs=[a_spec, b_spec],
        out_specs=pl.BlockSpec((tm, tn), lambda j, i: (i, j)),
        out_shape=jax.ShapeDtypeStruct((m, n), out_dtype),
        compiler_params=pltpu.CompilerParams(dimension_semantics=("arbitrary", "arbitrary"),
                                             vmem_limit_bytes=MM_VMEM_LIMIT),
    )(a, b)


def _make_linear(name, tk_w, tn_w):
    @jax.custom_vjp
    def op(a, w):
        return _mm(a, w, "nn", name + "_fwd", MM_ROW_TILE, w.shape[1])

    def fwd(a, w):
        return op(a, w), (a, w)

    def bwd(res, dy):
        a, w = res
        da = _mm(dy, w, "nt", name + "_dx", MM_ROW_TILE, w.shape[0])
        dw = _mm(a, dy, "tn", name + "_dw", tk_w, tn_w, out_dtype=BF16)
        return da, dw

    op.defvjp(fwd, bwd)
    return op


def _make_linear_split(name, widths, tk_w):
    starts = [sum(widths[:g]) for g in range(len(widths))]

    def call_fwd(a, w):
        t, k = a.shape
        n = w.shape[1]

        def body(a_ref, w_ref, *o_refs):
            y = jnp.dot(a_ref[...].astype(BF16), w_ref[...], preferred_element_type=F32)
            for o_ref, s0, wd in zip(o_refs, starts, widths):
                o_ref[...] = y[:, s0:s0 + wd]

        return pl.pallas_call(
            body, name=name + "_fwd", grid=(t // MM_ROW_TILE,),
            in_specs=[pl.BlockSpec((MM_ROW_TILE, k), lambda i: (i, 0)), pl.BlockSpec((k, n), lambda i: (0, 0))],
            out_specs=[pl.BlockSpec((MM_ROW_TILE, wd), lambda i: (i, 0)) for wd in widths],
            out_shape=[jax.ShapeDtypeStruct((t, wd), F32) for wd in widths],
            compiler_params=pltpu.CompilerParams(dimension_semantics=("arbitrary",), vmem_limit_bytes=MM_VMEM_LIMIT),
        )(a, w)

    def call_dx(dys, w):
        t = dys[0].shape[0]
        k, n = w.shape

        def body(*refs):
            dy_refs, w_ref, o_ref = refs[:-2], refs[-2], refs[-1]
            acc = jnp.zeros((MM_ROW_TILE, k), F32)
            for dy_ref, s0, wd in zip(dy_refs, starts, widths):
                acc = acc + _nt(dy_ref[...].astype(BF16), w_ref[:, s0:s0 + wd])
            o_ref[...] = acc

        return pl.pallas_call(
            body, name=name + "_dx", grid=(t // MM_ROW_TILE,),
            in_specs=[pl.BlockSpec((MM_ROW_TILE, wd), lambda i: (i, 0)) for wd in widths]
            + [pl.BlockSpec((k, n), lambda i: (0, 0))],
            out_specs=pl.BlockSpec((MM_ROW_TILE, k), lambda i: (i, 0)),
            out_shape=jax.ShapeDtypeStruct((t, k), F32),
            compiler_params=pltpu.CompilerParams(dimension_semantics=("arbitrary",), vmem_limit_bytes=MM_VMEM_LIMIT),
        )(*dys, w)

    def call_dw(a, dys, w):
        t, k = a.shape
        n = w.shape[1]

        def body(a_ref, *refs):
            dy_refs, o_ref = refs[:-1], refs[-1]
            ab = a_ref[...].astype(BF16)
            for dy_ref, s0, wd in zip(dy_refs, starts, widths):
                o_ref[:, s0:s0 + wd] = _tn(ab, dy_ref[...].astype(BF16)).astype(BF16)
            if starts[-1] + widths[-1] < n:
                o_ref[:, starts[-1] + widths[-1]:] = jnp.zeros((tk_w, n - starts[-1] - widths[-1]), BF16)

        return pl.pallas_call(
            body, name=name + "_dw", grid=(k // tk_w,),
            in_specs=[pl.BlockSpec((t, tk_w), lambda i: (0, i))]
            + [pl.BlockSpec((t, wd), lambda i: (0, 0)) for wd in widths],
            out_specs=pl.BlockSpec((tk_w, n), lambda i: (i, 0)),
            out_shape=jax.ShapeDtypeStruct((k, n), BF16),
            compiler_params=pltpu.CompilerParams(dimension_semantics=("arbitrary",), vmem_limit_bytes=MM_VMEM_LIMIT),
        )(a, *dys)

    @jax.custom_vjp
    def op(a, w):
        return tuple(call_fwd(a, w))

    def fwd(a, w):
        return op(a, w), (a, w)

    def bwd(res, dys):
        a, w = res
        return call_dx(dys, w), call_dw(a, dys, w)

    op.defvjp(fwd, bwd)
    return op


def _row_spec(arr, tb):
    return pl.BlockSpec((tb, arr.shape[1]), lambda i: (i, 0))


def _full_spec(arr):
    return pl.BlockSpec(arr.shape, lambda i: (0, 0))


def _make_rowwise(name, f, n_rows, n_params, out_cols, diff_rows, out_dtypes=None, grad_dtypes=None):
    n_out = len(out_cols)
    out_dtypes = out_dtypes or [F32] * n_out
    grad_dtypes = grad_dtypes or [F32] * sum(diff_rows)

    def call_fwd(rows, params):
        t = rows[0].shape[0]

        def body(*refs):
            ins = [r[...] for r in refs[:n_rows + n_params]]
            outs = f(*ins)
            for o_ref, o in zip(refs[n_rows + n_params:], outs):
                o_ref[...] = o.astype(o_ref.dtype)

        return pl.pallas_call(
            body, name=name + "_fwd", grid=(t // ROW_TILE,),
            in_specs=[_row_spec(a, ROW_TILE) for a in rows] + [_full_spec(p) for p in params],
            out_specs=[pl.BlockSpec((ROW_TILE, n), lambda i: (i, 0)) for n in out_cols],
            out_shape=[jax.ShapeDtypeStruct((t, n), dt) for n, dt in zip(out_cols, out_dtypes)],
            compiler_params=pltpu.CompilerParams(dimension_semantics=("arbitrary",),
                                                 vmem_limit_bytes=MM_VMEM_LIMIT),
        )(*rows, *params)

    def call_bwd(rows, params, cts):
        t = rows[0].shape[0]
        d_rows = [a for a, d in zip(rows, diff_rows) if d]
        n_in = n_rows + n_params + n_out

        def body(*refs):
            ins = [r[...] for r in refs[:n_rows + n_params]]
            ct = tuple(r[...].astype(F32) for r in refs[n_rows + n_params:n_in])
            _, vjp = jax.vjp(f, *ins)
            grads = vjp(ct)
            out_refs = refs[n_in:]
            g_rows = [g for g, d in zip(grads[:n_rows], diff_rows) if d]
            for o_ref, g in zip(out_refs[:len(g_rows)], g_rows):
                o_ref[...] = g.astype(o_ref.dtype)
            p_refs = out_refs[len(g_rows):]

            if p_refs:
                @pl.when(pl.program_id(0) == 0)
                def _():
                    for p_ref in p_refs:
                        p_ref[...] = jnp.zeros_like(p_ref)

                for p_ref, g in zip(p_refs, grads[n_rows:]):
                    p_ref[...] += g

        return pl.pallas_call(
            body, name=name + "_bwd", grid=(t // ROW_TILE,),
            in_specs=[_row_spec(a, ROW_TILE) for a in rows] + [_full_spec(p) for p in params]
            + [_row_spec(c, ROW_TILE) for c in cts],
            out_specs=[_row_spec(a, ROW_TILE) for a in d_rows] + [_full_spec(p) for p in params],
            out_shape=[jax.ShapeDtypeStruct(a.shape, dt) for a, dt in zip(d_rows, grad_dtypes)]
            + [jax.ShapeDtypeStruct(p.shape, F32) for p in params],
            compiler_params=pltpu.CompilerParams(dimension_semantics=("arbitrary",),
                                                 vmem_limit_bytes=MM_VMEM_LIMIT),
        )(*rows, *params, *cts)

    @jax.custom_vjp
    def op(*args):
        return tuple(call_fwd(args[:n_rows], args[n_rows:]))

    def fwd(*args):
        return op(*args), args

    def bwd(args, cts):
        rows, params = args[:n_rows], args[n_rows:]
        outs = call_bwd(rows, params, cts)
        it = iter(outs)
        g_rows = [next(it) if d else jnp.zeros_like(a) for a, d in zip(rows, diff_rows)]
        return tuple(g_rows) + tuple(it)

    op.defvjp(fwd, bwd)
    return op


def _rms(x, g, n):
    return x * lax.rsqrt(jnp.sum(x * x, axis=-1, keepdims=True) * (1.0 / n) + EPS) * g


def _f_pre_attn(x, g, scale, shift):
    return _rms(x, g, D_MODEL) * (1.0 + scale) + shift, x


def _f_mla_a(cq, ckv, gq, gkv):
    return _rms(cq, gq, MLA_Q_RANK), _rms(ckv, gkv, MLA_KV_RANK)


@jax.custom_vjp
def _split_lanes(x):
    return tuple(x[:, i * LANES:(i + 1) * LANES] for i in range(x.shape[1] // LANES))


def _split_lanes_fwd(x):
    return _split_lanes(x), None


def _split_lanes_bwd(_, cts):
    return (jnp.concatenate(cts, axis=1),)


_split_lanes.defvjp(_split_lanes_fwd, _split_lanes_bwd)


def _f_mla_b(qall, kn_all, kr, kr_sw, cos, sin, gqn, gqr, gqr_sw, gkn, gkr, gkr_sw):
    q = _split_lanes(qall)
    kn = _split_lanes(kn_all)
    qn_o, qr_o, kn_o = [], [], []
    for h in range(MLA_HEADS):
        qn, qr, qs = q[h], q[MLA_HEADS + h], q[2 * MLA_HEADS + h]
        ss = jnp.sum(qn * qn, axis=-1, keepdims=True) + jnp.sum(qr * qr, axis=-1, keepdims=True)
        rs = lax.rsqrt(ss * (1.0 / MLA_QK) + EPS)
        qn_o.append(qn * rs * gqn)
        qr_o.append((qr * rs * gqr) * cos + (qs * rs * gqr_sw) * sin)
        kn_o.append(_rms(kn[h], gkn, MLA_NOPE))
    rs = lax.rsqrt(jnp.sum(kr * kr, axis=-1, keepdims=True) * (1.0 / MLA_ROPE) + EPS)
    kr_o = (kr * rs * gkr) * cos + (kr_sw * rs * gkr_sw) * sin
    return (jnp.concatenate(qn_o, axis=1), jnp.concatenate(qr_o, axis=1), jnp.concatenate(kn_o, axis=1), kr_o)


def _f_post_attn(o_sb, o_mla, g_sb, g_mla):
    return (jnp.concatenate([_rms(o_sb, g_sb, SB_WIDTH), _rms(o_mla, g_mla, SB_WIDTH)], axis=1),)


def _f_pre_ffn(x, attn, gate, g, scale, shift):
    x2 = x + gate * attn
    return x2, _rms(x2, g, D_MODEL) * (1.0 + scale) + shift


def _f_swiglu(gt, up):
    return (gt / (1.0 + jnp.exp(-gt)) * up,)


def _f_loss(x2, ffn, target, gate):
    err = x2 + gate * ffn - target
    return (jnp.sum(err * err, axis=-1, keepdims=True) * (1.0 / D_MODEL),)


def _rope_tables(pos_col, freqs, sign):
    t = pos_col.shape[0]

    def body(p_ref, f_ref, s_ref, cos_ref, sin_ref):
        ang = p_ref[...].astype(F32) * f_ref[...]
        live = jnp.abs(s_ref[...])
        cos_ref[...] = jnp.cos(ang) * live
        sin_ref[...] = jnp.sin(ang) * s_ref[...]

    return pl.pallas_call(
        body, name="rope_tables", grid=(t // ROW_TILE,),
        in_specs=[pl.BlockSpec((ROW_TILE, 1), lambda i: (i, 0)), _full_spec(freqs), _full_spec(sign)],
        out_specs=[pl.BlockSpec((ROW_TILE, LANES), lambda i: (i, 0))] * 2,
        out_shape=[jax.ShapeDtypeStruct((t, LANES), F32)] * 2,
    )(pos_col, freqs, sign)


def _hi_lo_dot(x, tri):
    hi = x.astype(BF16)
    lo = (x - hi.astype(F32)).astype(BF16)
    return (jnp.dot(hi, tri, preferred_element_type=F32) + jnp.dot(lo, tri, preferred_element_type=F32))


def _tri(cmp):
    r = lax.broadcasted_iota(jnp.int32, (ATT_BLK, ATT_BLK), 0)
    c = lax.broadcasted_iota(jnp.int32, (ATT_BLK, ATT_BLK), 1)
    return cmp(r, c).astype(BF16)


def _nt(a, b):
    return lax.dot_general(a, b, (((1,), (1,)), ((), ())), preferred_element_type=F32)


def _tn(a, b):
    return lax.dot_general(a, b, (((0,), (0,)), ((), ())), preferred_element_type=F32)


def _sb_logs(z):
    lb = jnp.minimum(z, 0.0) - jnp.log(1.0 + jnp.exp(-jnp.abs(z)))
    return lb, lb - z


def _sb_fwd(q, k, v):
    t = q.shape[0]
    nq = t // ATT_BLK
    scale = SB_HEAD_DIM ** -0.5

    def body(q_ref, k_ref, v_ref, o_ref, tot_ref):
        qi = pl.program_id(1)
        lane = lax.broadcasted_iota(jnp.int32, (ATT_BLK, LANES), 1)
        tri = _tri(lambda r, c: r > c)
        qv = q_ref[...] * scale
        heads = [(lane // SB_HEAD_DIM) == hh for hh in range(2)]
        qms = [jnp.where(mine, qv, 0.0).astype(BF16) for mine in heads]

        def blocks(kbs, carry, diagonal):
            acc = carry[0]
            nb = len(kbs)
            chains = [(b, hh) for b in range(nb) for hh in range(2)]
            offs = [pl.multiple_of(kb * ATT_BLK, ATT_BLK) for kb in kbs]
            kks = [k_ref[pl.ds(off, ATT_BLK), :].astype(BF16) for off in offs]
            v_blks = [v_ref[pl.ds(off, ATT_BLK), :] for off in offs]
            if any(diagonal):
                valid = (lax.broadcasted_iota(jnp.int32, (ATT_BLK, ATT_BLK), 1)
                         < lax.broadcasted_iota(jnp.int32, (ATT_BLK, ATT_BLK), 0))
            zs = {ch: _nt(qms[ch[1]], kks[ch[0]]) for ch in chains}
            vvs = {(b, hh): jnp.where(heads[hh], v_blks[b], 0.0).astype(BF16) for b, hh in chains}
            logs = {ch: _sb_logs(zs[ch]) for ch in chains}
            l1ms = {ch: jnp.where(valid, logs[ch][1], 0.0) if diagonal[ch[0]] else logs[ch][1] for ch in chains}
            run = {(0, hh): carry[1 + hh] for hh in range(2)}
            for b, hh in chains:
                run[(b + 1, hh)] = run[(b, hh)] + jnp.sum(l1ms[(b, hh)], axis=-1, keepdims=True)
            afters = {ch: _hi_lo_dot(l1ms[ch], tri) for ch in chains}
            ws = {ch: jnp.exp(logs[ch][0] + (afters[ch] + run[ch])) for ch in chains}
            ws = {ch: jnp.where(valid, ws[ch], 0.0) if diagonal[ch[0]] else ws[ch] for ch in chains}
            for ch in chains:
                acc = acc + jnp.dot(ws[ch].astype(BF16), vvs[ch], preferred_element_type=F32)
            return (acc, run[(nb, 0)], run[(nb, 1)])

        zero = jnp.zeros((ATT_BLK, 1), F32)
        init = (jnp.zeros((ATT_BLK, LANES), F32), zero, zero)
        carry = lax.cond(qi % 2 == 1, lambda cr: blocks([qi, qi - 1], cr, (True, False)),
                         lambda cr: blocks([qi], cr, (True,)), init)
        top = qi - 1 - qi % 2
        carry = lax.fori_loop(0, qi // 2, lambda pr, cr: blocks([top - 2 * pr, top - 1 - 2 * pr], cr, (False, False)),
                              carry)
        o_ref[...] = carry[0]
        for hh in range(2):
            tot_ref[:, hh * LANES:(hh + 1) * LANES] = jnp.broadcast_to(carry[1 + hh], (ATT_BLK, LANES))

    return pl.pallas_call(
        body, name="sb_attn_fwd", grid=(SB_HEADS // 2, nq),
        in_specs=[pl.BlockSpec((ATT_BLK, LANES), lambda p, i: (i, p)),
                  pl.BlockSpec((t, LANES), lambda p, i: (0, p)),
                  pl.BlockSpec((t, LANES), lambda p, i: (0, p))],
        out_specs=[pl.BlockSpec((ATT_BLK, LANES), lambda p, i: (i, p)),
                   pl.BlockSpec((ATT_BLK, 2 * LANES), lambda p, i: (i, p))],
        out_shape=[jax.ShapeDtypeStruct((t, SB_WIDTH), F32), jax.ShapeDtypeStruct((t, SB_HEADS * LANES), F32)],
        compiler_params=pltpu.CompilerParams(dimension_semantics=("arbitrary", "arbitrary")),
    )(q, k, v)


def _sb_bwd(q, k, v, tot, do):
    t = q.shape[0]
    nq = t // ATT_BLK
    scale = SB_HEAD_DIM ** -0.5

    def body(q_ref, k_ref, v_ref, tot_ref, do_ref, dq_ref, dk_ref, dv_ref):
        qi = pl.program_id(1)

        @pl.when(qi == 0)
        def _():
            dk_ref[...] = jnp.zeros_like(dk_ref)
            dv_ref[...] = jnp.zeros_like(dv_ref)

        lane = lax.broadcasted_iota(jnp.int32, (ATT_BLK, LANES), 1)
        tri_incl = _tri(lambda r, c: r <= c)
        tri_lt = _tri(lambda r, c: r < c)
        qv = q_ref[...] * scale
        dov = do_ref[...]
        heads = [(lane // SB_HEAD_DIM) == hh for hh in range(2)]
        qms = [jnp.where(mine, qv, 0.0).astype(BF16) for mine in heads]
        doms = [jnp.where(mine, dov, 0.0).astype(BF16) for mine in heads]
        tots = [tot_ref[:, hh * LANES:hh * LANES + 1] for hh in range(2)]

        def blocks(kbs, carry, diagonal):
            dq = carry[0]
            nb = len(kbs)
            chains = [(b, hh) for b in range(nb) for hh in range(2)]
            offs = [pl.multiple_of(kb * ATT_BLK, ATT_BLK) for kb in kbs]
            k_blks = [k_ref[pl.ds(off, ATT_BLK), :] for off in offs]
            vvs = [v_ref[pl.ds(off, ATT_BLK), :].astype(BF16) for off in offs]
            if any(diagonal):
                valid = (lax.broadcasted_iota(jnp.int32, (ATT_BLK, ATT_BLK), 1)
                         < lax.broadcasted_iota(jnp.int32, (ATT_BLK, ATT_BLK), 0))
            kks = {(b, hh): jnp.where(heads[hh], k_blks[b], 0.0).astype(BF16) for b, hh in chains}
            zs = {ch: _nt(qms[ch[1]], kks[ch]) for ch in chains}
            dws = {ch: _nt(doms[ch[1]], vvs[ch[0]]) for ch in chains}
            logs = {ch: _sb_logs(zs[ch]) for ch in chains}
            lbs = {ch: logs[ch][0] for ch in chains}
            l1m_all = {ch: logs[ch][1] for ch in chains}
            l1ms = {ch: jnp.where(valid, l1m_all[ch], 0.0) if diagonal[ch[0]] else l1m_all[ch] for ch in chains}
            pre, c_de = {}, {}
            for hh in range(2):
                pre[(0, hh)], c_de[(0, hh)] = carry[1 + 2 * hh], carry[2 + 2 * hh]
            for b, hh in chains:
                pre[(b + 1, hh)] = pre[(b, hh)] + jnp.sum(l1ms[(b, hh)], axis=-1, keepdims=True)
            prefix = {ch: _hi_lo_dot(l1ms[ch], tri_incl) for ch in chains}
            ws = {ch: jnp.exp(lbs[ch] + (tots[ch[1]] - (prefix[ch] + pre[ch]))) for ch in chains}
            ws = {ch: jnp.where(valid, ws[ch], 0.0) if diagonal[ch[0]] else ws[ch] for ch in chains}
            d_es = {ch: ws[ch] * dws[ch] for ch in chains}
            for b, hh in chains:
                c_de[(b + 1, hh)] = c_de[(b, hh)] + jnp.sum(d_es[(b, hh)], axis=-1, keepdims=True)
            dvs = [_tn(ws[(b, 0)].astype(BF16), doms[0]) + _tn(ws[(b, 1)].astype(BF16), doms[1]) for b in range(nb)]
            dl1ms = {ch: jnp.dot(d_es[ch].astype(BF16), tri_lt, preferred_element_type=F32) + c_de[ch] for ch in chains}
            dzs = {ch: d_es[ch] * jnp.exp(l1m_all[ch]) - dl1ms[ch] * jnp.exp(lbs[ch]) for ch in chains}
            dzs = {ch: jnp.where(valid, dzs[ch], 0.0) if diagonal[ch[0]] else dzs[ch] for ch in chains}
            dzs = {ch: dzs[ch].astype(BF16) for ch in chains}
            for ch in chains:
                dq = dq + jnp.dot(dzs[ch], kks[ch], preferred_element_type=F32)
            for b in range(nb):
                dk_ref[pl.ds(offs[b], ATT_BLK), :] += _tn(dzs[(b, 0)], qms[0]) + _tn(dzs[(b, 1)], qms[1])
                dv_ref[pl.ds(offs[b], ATT_BLK), :] += dvs[b]
            return (dq, pre[(nb, 0)], c_de[(nb, 0)], pre[(nb, 1)], c_de[(nb, 1)])

        zero = jnp.zeros((ATT_BLK, 1), F32)
        carry = lax.fori_loop(0, qi // 2, lambda pr, cr: blocks([2 * pr, 2 * pr + 1], cr, (False, False)),
                              (jnp.zeros((ATT_BLK, LANES), F32), zero, zero, zero, zero))
        carry = lax.cond(qi % 2 == 1, lambda cr: blocks([qi - 1, qi], cr, (False, True)),
                         lambda cr: blocks([qi], cr, (True,)), carry)
        dq_ref[...] = carry[0] * scale

    return pl.pallas_call(
        body, name="sb_attn_bwd", grid=(SB_HEADS // 2, nq),
        in_specs=[pl.BlockSpec((ATT_BLK, LANES), lambda p, i: (i, p)),
                  pl.BlockSpec((t, LANES), lambda p, i: (0, p)),
                  pl.BlockSpec((t, LANES), lambda p, i: (0, p)),
                  pl.BlockSpec((ATT_BLK, 2 * LANES), lambda p, i: (i, p)),
                  pl.BlockSpec((ATT_BLK, LANES), lambda p, i: (i, p))],
        out_specs=[pl.BlockSpec((ATT_BLK, LANES), lambda p, i: (i, p)),
                   pl.BlockSpec((t, LANES), lambda p, i: (0, p)),
                   pl.BlockSpec((t, LANES), lambda p, i: (0, p))],
        out_shape=[jax.ShapeDtypeStruct((t, SB_WIDTH), F32)] * 3,
        compiler_params=pltpu.CompilerParams(dimension_semantics=("arbitrary", "arbitrary")),
    )(q, k, v, tot, do)


@jax.custom_vjp
def _sb_attention(q, k, v):
    return _sb_fwd(q, k, v)[0]


def _sb_attention_fwd(q, k, v):
    o, tot = _sb_fwd(q, k, v)
    return o, (q, k, v, tot)


def _sb_attention_bwd(res, do):
    return tuple(_sb_bwd(*res, do))


_sb_attention.defvjp(_sb_attention_fwd, _sb_attention_bwd)


def _mla_fwd(qn, qr, kn, kr, v):
    t = qn.shape[0]
    nq = t // ATT_BLK
    scale = MLA_QK ** -0.5

    def body(qn_ref, qr_ref, kn_ref, kr_ref, v_ref, o_ref, lse_ref):
        qi = pl.program_id(1)
        lanes = [slice(hh * LANES, (hh + 1) * LANES) for hh in range(2)]
        qnb = [qn_ref[:, sl].astype(BF16) for sl in lanes]
        qrb = [qr_ref[:, sl].astype(BF16) for sl in lanes]

        def blocks(kbs, carry, diagonal):
            nb = len(kbs)
            chains = [(b, hh) for b in range(nb) for hh in range(2)]
            offs = [pl.multiple_of(kb * ATT_BLK, ATT_BLK) for kb in kbs]
            krbs = [kr_ref[pl.ds(off, ATT_BLK), :].astype(BF16) for off in offs]
            accs, ms, ls = [carry[0], carry[3]], [carry[1], carry[4]], [carry[2], carry[5]]
            ss = {(b, hh): (_nt(qnb[hh], kn_ref[pl.ds(offs[b], ATT_BLK), lanes[hh]].astype(BF16))
                            + _nt(qrb[hh], krbs[b])) * scale for b, hh in chains}
            if any(diagonal):
                causal = (lax.broadcasted_iota(jnp.int32, (ATT_BLK, ATT_BLK), 1)
                          <= lax.broadcasted_iota(jnp.int32, (ATT_BLK, ATT_BLK), 0))
                ss = {ch: jnp.where(causal, ss[ch], -jnp.inf) if diagonal[ch[0]] else ss[ch] for ch in chains}
            m_new = list(ms)
            for b, hh in chains:
                m_new[hh] = jnp.maximum(m_new[hh], jnp.max(ss[(b, hh)], axis=-1, keepdims=True))
            ps = {(b, hh): jnp.exp(ss[(b, hh)] - m_new[hh]) for b, hh in chains}
            alphas = [jnp.exp(ms[hh] - m_new[hh]) for hh in range(2)]
            pvs = {(b, hh): jnp.dot(ps[(b, hh)].astype(BF16), v_ref[pl.ds(offs[b], ATT_BLK), lanes[hh]].astype(BF16),
                                    preferred_element_type=F32) for b, hh in chains}
            out = []
            for hh in range(2):
                acc, l = accs[hh] * alphas[hh], ls[hh] * alphas[hh]
                for b in range(nb):
                    acc, l = acc + pvs[(b, hh)], l + jnp.sum(ps[(b, hh)], axis=-1, keepdims=True)
                out += [acc, m_new[hh], l]
            return tuple(out)

        init = (jnp.zeros((ATT_BLK, LANES), F32), jnp.full((ATT_BLK, 1), -jnp.inf, F32), jnp.zeros((ATT_BLK, 1), F32))
        carry = lax.cond(qi % 2 == 1, lambda cr: blocks([qi, qi - 1], cr, (True, False)),
                         lambda cr: blocks([qi], cr, (True,)), init + init)
        carry = lax.fori_loop(0, qi // 2, lambda pr, cr: blocks([2 * pr, 2 * pr + 1], cr, (False, False)), carry)
        for hh in range(2):
            acc, m, l = carry[3 * hh:3 * hh + 3]
            o_ref[:, lanes[hh]] = acc / l
            lse_ref[:, lanes[hh]] = jnp.broadcast_to(m + jnp.log(l), (ATT_BLK, LANES))

    blk = pl.BlockSpec((ATT_BLK, 2 * LANES), lambda p, i: (i, p))
    full = pl.BlockSpec((t, 2 * LANES), lambda p, i: (0, p))
    return pl.pallas_call(
        body, name="mla_attn_fwd", grid=(MLA_HEADS // 2, nq),
        in_specs=[blk, blk, full, pl.BlockSpec((t, LANES), lambda p, i: (0, 0)), full],
        out_specs=[blk, blk],
        out_shape=[jax.ShapeDtypeStruct((t, MLA_HEADS * LANES), F32)] * 2,
        compiler_params=pltpu.CompilerParams(dimension_semantics=("arbitrary", "arbitrary")),
    )(qn, qr, kn, kr, v)


def _mla_bwd(qn, qr, kn, kr, v, o, lse, do):
    t = qn.shape[0]
    nq = t // ATT_BLK
    scale = MLA_QK ** -0.5

    def body(qn_ref, qr_ref, kn_ref, kr_ref, v_ref, o_ref, lse_ref, do_ref,
             dqn_ref, dqr_ref, dkn_ref, dkr_ref, dv_ref):
        pair = pl.program_id(0)
        qi = pl.program_id(1)

        @pl.when(qi == 0)
        def _():
            dkn_ref[...] = jnp.zeros_like(dkn_ref)
            dv_ref[...] = jnp.zeros_like(dv_ref)

        @pl.when((qi == 0) & (pair == 0))
        def _():
            dkr_ref[...] = jnp.zeros_like(dkr_ref)

        lanes = [slice(hh * LANES, (hh + 1) * LANES) for hh in range(2)]
        qnb = [qn_ref[:, sl].astype(BF16) for sl in lanes]
        qrb = [qr_ref[:, sl].astype(BF16) for sl in lanes]
        dob = [do_ref[:, sl].astype(BF16) for sl in lanes]
        delta = [jnp.sum(do_ref[:, sl] * o_ref[:, sl], axis=-1, keepdims=True) for sl in lanes]
        lse_v = [lse_ref[:, hh * LANES:hh * LANES + 1] for hh in range(2)]

        def blocks(kbs, carry, diagonal):
            nb = len(kbs)
            chains = [(b, hh) for b in range(nb) for hh in range(2)]
            offs = [pl.multiple_of(kb * ATT_BLK, ATT_BLK) for kb in kbs]
            krbs = [kr_ref[pl.ds(off, ATT_BLK), :].astype(BF16) for off in offs]
            knb = {(b, hh): kn_ref[pl.ds(offs[b], ATT_BLK), lanes[hh]].astype(BF16) for b, hh in chains}
            vb = {(b, hh): v_ref[pl.ds(offs[b], ATT_BLK), lanes[hh]].astype(BF16) for b, hh in chains}
            ss = {(b, hh): _nt(qnb[hh], knb[(b, hh)]) + _nt(qrb[hh], krbs[b]) for b, hh in chains}
            dps = {(b, hh): _nt(dob[hh], vb[(b, hh)]) for b, hh in chains}
            ps = {(b, hh): jnp.exp(ss[(b, hh)] * scale - lse_v[hh]) for b, hh in chains}
            if any(diagonal):
                causal = (lax.broadcasted_iota(jnp.int32, (ATT_BLK, ATT_BLK), 1)
                          <= lax.broadcasted_iota(jnp.int32, (ATT_BLK, ATT_BLK), 0))
                ps = {ch: jnp.where(causal, ps[ch], 0.0) if diagonal[ch[0]] else ps[ch] for ch in chains}
            dss = {(b, hh): (ps[(b, hh)] * (dps[(b, hh)] - delta[hh]) * scale).astype(BF16) for b, hh in chains}
            for b, hh in chains:
                dv_ref[pl.ds(offs[b], ATT_BLK), lanes[hh]] += _tn(ps[(b, hh)].astype(BF16), dob[hh])
            for b, hh in chains:
                dkn_ref[pl.ds(offs[b], ATT_BLK), lanes[hh]] += _tn(dss[(b, hh)], qnb[hh])
            for b in range(nb):
                dkr_ref[pl.ds(offs[b], ATT_BLK), :] += _tn(dss[(b, 0)], qrb[0]) + _tn(dss[(b, 1)], qrb[1])
            out = list(carry)
            for b, hh in chains:
                out[2 * hh] = out[2 * hh] + jnp.dot(dss[(b, hh)], knb[(b, hh)], preferred_element_type=F32)
                out[2 * hh + 1] = out[2 * hh + 1] + jnp.dot(dss[(b, hh)], krbs[b], preferred_element_type=F32)
            return tuple(out)

        zero = jnp.zeros((ATT_BLK, LANES), F32)
        carry = lax.fori_loop(0, qi // 2, lambda pr, cr: blocks([2 * pr, 2 * pr + 1], cr, (False, False)),
                              (zero, zero, zero, zero))
        carry = lax.cond(qi % 2 == 1, lambda cr: blocks([qi - 1, qi], cr, (False, True)),
                         lambda cr: blocks([qi], cr, (True,)), carry)
        for hh in range(2):
            dqn_ref[:, lanes[hh]] = carry[2 * hh]
            dqr_ref[:, lanes[hh]] = carry[2 * hh + 1]

    blk = pl.BlockSpec((ATT_BLK, 2 * LANES), lambda p, i: (i, p))
    full = pl.BlockSpec((t, 2 * LANES), lambda p, i: (0, p))
    shared = pl.BlockSpec((t, LANES), lambda p, i: (0, 0))
    wide = jax.ShapeDtypeStruct((t, MLA_HEADS * LANES), F32)
    return pl.pallas_call(
        body, name="mla_attn_bwd", grid=(MLA_HEADS // 2, nq),
        in_specs=[blk, blk, full, shared, full, blk, blk, blk],
        out_specs=[blk, blk, full, shared, full],
        out_shape=[wide, wide, wide, jax.ShapeDtypeStruct((t, LANES), F32), wide],
        compiler_params=pltpu.CompilerParams(dimension_semantics=("arbitrary", "arbitrary")),
    )(qn, qr, kn, kr, v, o, lse, do)


@jax.custom_vjp
def _mla_attention(qn, qr, kn, kr, v):
    return _mla_fwd(qn, qr, kn, kr, v)[0]


def _mla_attention_fwd(qn, qr, kn, kr, v):
    o, lse = _mla_fwd(qn, qr, kn, kr, v)
    return o, (qn, qr, kn, kr, v, o, lse)


def _mla_attention_bwd(res, do):
    return tuple(_mla_bwd(*res, do))


_mla_attention.defvjp(_mla_attention_fwd, _mla_attention_bwd)


def _ffn_in(h, wg, wu):
    t, k = h.shape
    n_sh, _, cc = wg.shape

    def body(h_ref, wg_ref, wu_ref, g_ref, u_ref, a_ref):
        hb = h_ref[...].astype(BF16)
        for j in range(n_sh):
            cols = slice(j * cc, (j + 1) * cc)
            g = jnp.dot(hb, wg_ref[j], preferred_element_type=F32)
            u = jnp.dot(hb, wu_ref[j], preferred_element_type=F32)
            g_ref[:, cols] = g.astype(BF16)
            u_ref[:, cols] = u.astype(BF16)
            a_ref[:, cols] = _f_swiglu(g, u)[0].astype(BF16)

    w_spec = pl.BlockSpec((n_sh, k, cc), lambda i: (0, 0, 0))
    o_spec = pl.BlockSpec((MM_ROW_TILE, n_sh * cc), lambda i: (i, 0))
    wide = jax.ShapeDtypeStruct((t, n_sh * cc), BF16)
    return pl.pallas_call(
        body, name="ffn_in_fwd", grid=(t // MM_ROW_TILE,),
        in_specs=[pl.BlockSpec((MM_ROW_TILE, k), lambda i: (i, 0)), w_spec, w_spec],
        out_specs=[o_spec, o_spec, o_spec],
        out_shape=[wide, wide, wide],
        compiler_params=pltpu.CompilerParams(dimension_semantics=("arbitrary",), vmem_limit_bytes=MM_VMEM_LIMIT),
    )(h, wg, wu)


def _ffn_mid_bwd(dy, wd, g, u):
    t, n = dy.shape
    n_sh, cc, _ = wd.shape

    def body(dy_ref, wd_ref, g_ref, u_ref, dg_ref, du_ref):
        d_act = _nt(dy_ref[...].astype(BF16), wd_ref[...])
        g = g_ref[...].astype(F32)
        sig = 1.0 / (1.0 + jnp.exp(-g))
        dg_ref[...] = (d_act * u_ref[...].astype(F32) * (sig * (1.0 + g * (1.0 - sig)))).astype(BF16)
        du_ref[...] = (d_act * (g * sig)).astype(BF16)

    blk = pl.BlockSpec((MM_ROW_TILE, cc), lambda j, i: (i, j))
    wide = jax.ShapeDtypeStruct((t, n_sh * cc), BF16)
    return pl.pallas_call(
        body, name="ffn_mid_bwd", grid=(n_sh, t // MM_ROW_TILE),
        in_specs=[pl.BlockSpec((MM_ROW_TILE, n), lambda j, i: (i, 0)),
                  pl.BlockSpec((None, cc, n), lambda j, i: (j, 0, 0)), blk, blk],
        out_specs=[blk, blk], out_shape=[wide, wide],
        compiler_params=pltpu.CompilerParams(dimension_semantics=("arbitrary", "arbitrary"),
                                             vmem_limit_bytes=MM_VMEM_LIMIT),
    )(dy, wd, g, u)


def _ffn_dh(dg, du, wg, wu):
    t = dg.shape[0]
    n_sh, k, cc = wg.shape

    def body(dg_ref, du_ref, wg_ref, wu_ref, o_ref):
        acc = jnp.zeros((MM_ROW_TILE, k), F32)
        for j in range(n_sh):
            cols = slice(j * cc, (j + 1) * cc)
            acc = acc + _nt(dg_ref[:, cols], wg_ref[j]) + _nt(du_ref[:, cols], wu_ref[j])
        o_ref[...] = acc

    blk = pl.BlockSpec((MM_ROW_TILE, n_sh * cc), lambda i: (i, 0))
    w_spec = pl.BlockSpec((n_sh, k, cc), lambda i: (0, 0, 0))
    return pl.pallas_call(
        body, name="ffn_dh", grid=(t // MM_ROW_TILE,),
        in_specs=[blk, blk, w_spec, w_spec],
        out_specs=pl.BlockSpec((MM_ROW_TILE, k), lambda i: (i, 0)),
        out_shape=jax.ShapeDtypeStruct((t, k), F32),
        compiler_params=pltpu.CompilerParams(dimension_semantics=("arbitrary",), vmem_limit_bytes=MM_VMEM_LIMIT),
    )(dg, du, wg, wu)


def _ffn_dw_in(h, dy, n_sh, name):
    t, k = h.shape
    cc = dy.shape[1] // n_sh
    tk = 512

    def body(h_ref, dy_ref, o_ref):
        o_ref[...] = _tn(h_ref[...].astype(BF16), dy_ref[...]).astype(BF16)

    return pl.pallas_call(
        body, name=name, grid=(n_sh, k // tk),
        in_specs=[pl.BlockSpec((t, tk), lambda j, i: (0, i)), pl.BlockSpec((t, cc), lambda j, i: (0, j))],
        out_specs=pl.BlockSpec((None, tk, cc), lambda j, i: (j, i, 0)),
        out_shape=jax.ShapeDtypeStruct((n_sh, k, cc), BF16),
        compiler_params=pltpu.CompilerParams(dimension_semantics=("arbitrary", "arbitrary"),
                                             vmem_limit_bytes=MM_VMEM_LIMIT),
    )(h, dy)


@jax.custom_vjp
def _ffn_block(h, wg, wu, wd):
    act = _ffn_in(h, wg, wu)[2]
    return _mm(act, wd.reshape(-1, wd.shape[2]), "nn", "ffn_down_fwd", MM_ROW_TILE, wd.shape[2])


def _ffn_block_fwd(h, wg, wu, wd):
    g, u, act = _ffn_in(h, wg, wu)
    y = _mm(act, wd.reshape(-1, wd.shape[2]), "nn", "ffn_down_fwd", MM_ROW_TILE, wd.shape[2])
    return y, (h, wg, wu, wd, g, u, act)


def _ffn_block_bwd(res, dy):
    h, wg, wu, wd, g, u, act = res
    dg, du = _ffn_mid_bwd(dy, wd, g, u)
    dh = _ffn_dh(dg, du, wg, wu)
    n_sh = wg.shape[0]
    dwg = _ffn_dw_in(h, dg, n_sh, "ffn_gate_dw")
    dwu = _ffn_dw_in(h, du, n_sh, "ffn_up_dw")
    dwd = _mm(act, dy, "tn", "ffn_down_dw", 256, wd.shape[2], out_dtype=BF16).reshape(wd.shape)
    return dh, dwg, dwu, dwd


_ffn_block.defvjp(_ffn_block_fwd, _ffn_block_bwd)


def _swap_halves(w):
    half = w.shape[-1] // 2
    return jnp.concatenate([w[..., half:], w[..., :half]], axis=-1)


def _pad_lanes(w):
    return jnp.concatenate([w, jnp.zeros(w.shape[:-1] + (LANES - w.shape[-1],), w.dtype)], axis=-1)


def _join_cols(shards):
    return shards.transpose(1, 0, 2).reshape(shards.shape[1], -1)


def _mod_parts(mod):
    return [mod[:, i * D_MODEL:(i + 1) * D_MODEL] for i in range(N_MOD)]


def _mixing_stage(x, mod, p, cos, sin):
    shift1, scale1 = _mod_parts(mod)[:2]

    w_in = _join_cols(p["w_in"])
    k_rope_w = w_in[:, 2176:2240]
    w_in_ext = jnp.concatenate([w_in[:, :2176], _pad_lanes(k_rope_w), _pad_lanes(_swap_halves(k_rope_w)),
                                jnp.zeros((D_MODEL, LANES), w_in.dtype)], axis=1)
    h1, x_res = _make_rowwise("pre_attn", _f_pre_attn, 1, 3, [D_MODEL, D_MODEL], [True], out_dtypes=[BF16, F32])(
        x, p["norm_attn"], scale1, shift1)
    q_sb, k_sb, v_sb, cq, ckv, kr, kr_sw = _make_linear_split(
        "in_proj", (SB_WIDTH, SB_WIDTH, SB_WIDTH, MLA_Q_RANK, MLA_KV_RANK, LANES, LANES), 512)(h1, w_in_ext)

    o_sb = _sb_attention(q_sb, k_sb, v_sb)

    wq = _join_cols(p["w_q_up"]).reshape(MLA_Q_RANK, MLA_HEADS, MLA_QK)
    wq_n, wq_r = wq[:, :, :MLA_NOPE], wq[:, :, MLA_NOPE:]
    w_q_ext = jnp.concatenate([wq_n.reshape(MLA_Q_RANK, -1), _pad_lanes(wq_r).reshape(MLA_Q_RANK, -1),
                               _pad_lanes(_swap_halves(wq_r)).reshape(MLA_Q_RANK, -1)], axis=1)
    wkv = _join_cols(p["w_kv_up"]).reshape(MLA_KV_RANK, MLA_HEADS, MLA_NOPE + MLA_V)
    w_kv_ext = jnp.concatenate([wkv[:, :, :MLA_NOPE].reshape(MLA_KV_RANK, -1),
                                wkv[:, :, MLA_NOPE:].reshape(MLA_KV_RANK, -1)], axis=1)
    cqn, ckvn = _make_rowwise("mla_a", _f_mla_a, 2, 2, [MLA_Q_RANK, MLA_KV_RANK], [True, True],
                              out_dtypes=[BF16, BF16], grad_dtypes=[BF16, BF16])(
        cq, ckv, p["q_a_norm"], p["kv_a_norm"])
    qall = _make_linear("q_up", 384, 768)(cqn, w_q_ext)
    kn_all, v_mla = _make_linear_split("kv_up", (MLA_HEADS * MLA_NOPE, MLA_HEADS * MLA_V), MLA_KV_RANK)(ckvn, w_kv_ext)
    gq = p["q_norm"]
    gkr = p["k_rope_norm"]
    qn, qr, kn, krr = _make_rowwise("mla_b", _f_mla_b, 6, 6, [512, 512, 512, LANES],
                                    [True, True, True, True, False, False],
                                    out_dtypes=[BF16] * 4, grad_dtypes=[BF16] * 4)(
        qall, kn_all, kr, kr_sw, cos, sin,
        gq[:, :MLA_NOPE], _pad_lanes(gq[:, MLA_NOPE:]), _pad_lanes(_swap_halves(gq[:, MLA_NOPE:])),
        p["k_nope_norm"], _pad_lanes(gkr), _pad_lanes(_swap_halves(gkr)))
    o_mla = _mla_attention(qn, qr, kn, krr, v_mla)

    (mixed,) = _make_rowwise("post_attn", _f_post_attn, 2, 2, [D_MODEL], [True, True])(
        o_sb, o_mla, p["out_norm_sb"], p["out_norm_mla"])
    return mixed, x_res


def _ffn_stage(x, mixed, mod, p):
    _, _, gate1, shift2, scale2, _ = _mod_parts(mod)
    attn = _make_linear("out_proj", 512, 512)(mixed, p["w_out"].reshape(D_MODEL, D_MODEL))

    x2, h2 = _make_rowwise("pre_ffn", _f_pre_ffn, 2, 4, [D_MODEL, D_MODEL], [True, True],
                           out_dtypes=[F32, BF16], grad_dtypes=[F32, BF16])(
        x, attn, gate1, p["norm_ffn"], scale2, shift2)
    return x2, _ffn_block(h2, p["w_gate"], p["w_up"], p["w_down"])


def _my_place():
    return lax.axis_index("x"), lax.axis_index("y"), lax.axis_index("c")


def _small_gather(x_ref, out_ref, send_sems, recv_sems, base, local_sem):
    m_per = x_ref.shape[0]
    x, y, c = _my_place()
    me, sibling = (x, y, c), (x, y, 1 - c)
    chips = [(1 - x, y), (x, 1 - y), (1 - x, 1 - y)]

    def rows(px, py, pc):
        return out_ref.at[pl.ds((4 * px + 2 * py + pc) * m_per, m_per), :]

    def copy(k, blk, to, src=None):
        return _remote(rows(*blk) if src is None else src, rows(*blk), send_sems, recv_sems, base + k, to)

    mine = pltpu.make_async_copy(x_ref, rows(*me), local_sem)
    first = [copy(0, me, sibling, src=x_ref)] + [copy(1 + j, me, (*chip, c), src=x_ref) for j, chip in enumerate(chips)]
    passed = [copy(4 + j, (*chip, c), sibling) for j, chip in enumerate(chips)]

    def start():
        mine.start()
        for cp in first:
            cp.start()

    def finish():
        for j, chip in enumerate(chips):
            copy(1 + j, (*chip, c), me).wait_recv()
            passed[j].start()
        copy(0, sibling, me).wait_recv()
        for j, chip in enumerate(chips):
            copy(4 + j, (*chip, 1 - c), me).wait_recv()
        for cp in first + passed:
            cp.wait_send()
        mine.wait()

    return start, finish


EARLY =("w_in", "w_q_up", "w_kv_up")
LATE = ("w_out", "w_gate", "w_up", "w_down")
BIG = EARLY + LATE
TRANSPOSED_UPDATE = ("w_in", "w_gate", "w_up")
HALF_AXIS = {"w_in": 0, "w_q_up": 0, "w_kv_up": 0, "w_out": 0, "w_gate": 0, "w_up": 0, "w_down": 1}


def _half(ref, h, axis, lead=()):
    trail = ref.shape[len(lead):]
    idx = list(lead) + [slice(None)] * len(trail)
    at = len(trail) - 2 + axis
    n2 = trail[at] // 2
    idx[len(lead) + at] = pl.ds(h * n2, n2)
    return ref.at[tuple(idx)]


def _half_shape(shape, axis):
    shape = list(shape)
    shape[len(shape) - 2 + axis] //= 2
    return tuple(shape)


def _remote(src, dst, send_sems, recv_sems, k, to):
    return pltpu.make_async_remote_copy(src_ref=src, dst_ref=dst, send_sem=send_sems.at[k],
                                        recv_sem=recv_sems.at[k], device_id=to, device_id_type=MESH)


def _gather_weights(names, shards, small_block):
    n_w = len(shards)
    axes = [HALF_AXIS[n] for n in names]

    def body(*refs):
        w_refs, small_ref = refs[:n_w], refs[n_w]
        out_refs, token, small_out = refs[n_w + 1:2 * n_w + 1], refs[2 * n_w + 1], refs[2 * n_w + 2]
        send_sems, recv_sems, local_sems = refs[2 * n_w + 3:]
        token[...] = jnp.zeros_like(token)
        x, y, c = _my_place()
        sibling = (x, y, 1 - c)
        chips = [(1 - x, y), (x, 1 - y), (1 - x, 1 - y)]
        me = 2 * x + y
        small_start, small_finish = _small_gather(small_ref, small_out, send_sems, recv_sems, 6 * n_w,
                                                  local_sems.at[n_w])
        small_start()
        mine = [pltpu.make_async_copy(w, o.at[me], local_sems.at[i]) for i, (w, o) in enumerate(zip(w_refs, out_refs))]
        for cp in mine:
            cp.start()
        first = [_remote(_half(w_refs[i], c, axes[i]), _half(out_refs[i], c, axes[i], (me,)),
                         send_sems, recv_sems, 6 * i + j, (*chip, c))
                 for i in range(n_w) for j, chip in enumerate(chips)]
        for cp in first:
            cp.start()
        small_finish()
        passed = []
        for j, (cx, cy) in enumerate(chips):
            for i in range(n_w):
                blk = _half(out_refs[i], c, axes[i], (2 * cx + cy,))
                _remote(blk, blk, send_sems, recv_sems, 6 * i + j, (cx, cy, c)).wait_recv()
                cp = _remote(blk, blk, send_sems, recv_sems, 6 * i + 3 + j, sibling)
                cp.start()
                passed.append(cp)
        for j, (cx, cy) in enumerate(chips):
            for i in range(n_w):
                blk = _half(out_refs[i], 1 - c, axes[i], (2 * cx + cy,))
                _remote(blk, blk, send_sems, recv_sems, 6 * i + 3 + j, sibling).wait_recv()
        for cp in first + passed:
            cp.wait_send()
        for cp in mine:
            cp.wait()

    outs = pl.pallas_call(
        body, name="gather_weights",
        out_shape=[jax.ShapeDtypeStruct((N_CHIPS,) + s.shape, s.dtype) for s in shards]
        + [jax.ShapeDtypeStruct((8, LANES), F32),
           jax.ShapeDtypeStruct((N_DEV * small_block.shape[0], small_block.shape[1]), small_block.dtype)],
        in_specs=[ANY] * (n_w + 1), out_specs=[ANY] * n_w + [pl.BlockSpec(memory_space=pltpu.VMEM), ANY],
        scratch_shapes=[pltpu.SemaphoreType.DMA((6 * n_w + 7,)), pltpu.SemaphoreType.DMA((6 * n_w + 7,)),
                        pltpu.SemaphoreType.DMA((n_w + 1,))],
    )(*shards, small_block)
    return outs[:n_w], outs[n_w], outs[n_w + 1]


def _pair_exchange(names, grads, call_name, small_block):
    n_w = len(grads)
    axes = [HALF_AXIS[n] for n in names]

    def body(*refs):
        g_refs, small_ref = refs[:n_w], refs[n_w]
        t_refs, small_out = refs[n_w + 1:2 * n_w + 1], refs[2 * n_w + 1]
        send_sems, recv_sems, local_sem = refs[2 * n_w + 2:]
        x, y, c = _my_place()
        small_start, small_finish = _small_gather(small_ref, small_out, send_sems, recv_sems, n_w, local_sem)
        small_start()
        sends = [_remote(_half(g_refs[i], 1 - c, axes[i]), t_refs[i], send_sems, recv_sems, i, (x, y, 1 - c))
                 for i in range(n_w)]
        for cp in sends:
            cp.start()
        small_finish()
        for cp in sends:
            cp.wait_recv()
        for cp in sends:
            cp.wait_send()

    outs = pl.pallas_call(
        body, name=call_name,
        out_shape=[jax.ShapeDtypeStruct(_half_shape(g.shape, a), g.dtype) for g, a in zip(grads, axes)]
        + [jax.ShapeDtypeStruct((N_DEV * small_block.shape[0], small_block.shape[1]), small_block.dtype)],
        in_specs=[ANY] * (n_w + 1), out_specs=[ANY] * (n_w + 1),
        scratch_shapes=[pltpu.SemaphoreType.DMA((n_w + 7,)), pltpu.SemaphoreType.DMA((n_w + 7,)),
                        pltpu.SemaphoreType.DMA],
    )(*grads, small_block)
    return outs[:n_w], outs[n_w]


def _sibling_join(halves, name, after):
    n_w = len(halves)

    def body(*refs):
        s_refs, j_refs = refs[:n_w], refs[n_w + 1:2 * n_w + 1]
        send_sems, recv_sems = refs[2 * n_w + 1:]
        x, y, c = _my_place()
        sends = [_remote(s_refs[i], j_refs[i], send_sems, recv_sems, i, (x, y, 1 - c)) for i in range(n_w)]
        for cp in sends:
            cp.start()
        for cp in sends:
            cp.wait_recv()
        for cp in sends:
            cp.wait_send()

    return pl.pallas_call(
        body, name=name,
        out_shape=[jax.ShapeDtypeStruct(s.shape, s.dtype) for s in halves],
        in_specs=[ANY] * (n_w + 1), out_specs=[ANY] * n_w,
        scratch_shapes=[pltpu.SemaphoreType.DMA((n_w,)), pltpu.SemaphoreType.DMA((n_w,))],
    )(*halves, after)


HBM_SPEC = pl.BlockSpec(memory_space=pltpu.HBM)
SEM_SPEC = pl.BlockSpec(memory_space=pltpu.SEMAPHORE)
DATAFLOW = pltpu.SideEffectType.DATAFLOW_SIDE_EFFECTING


def _in_hbm(a):
    return pltpu.with_memory_space_constraint(a, pltpu.HBM)


def _exchange_start(name, srcs, lands, plan, n_copies, after, thru):
    n = len(srcs)

    def body(*refs):
        src_refs, land_refs = refs[:n], refs[n:2 * n]
        send_sems, recv_sems = refs[2 * n + 2], refs[2 * n + 3]
        for k, (src, dst, to, k_recv) in enumerate(plan(src_refs, land_refs)):
            pltpu.make_async_remote_copy(src_ref=src, dst_ref=dst, send_sem=send_sems.at[k],
                                         recv_sem=recv_sems.at[k_recv], device_id=to, device_id_type=MESH).start()

    outs = pl.pallas_call(
        body, name=name,
        out_shape=(pltpu.SemaphoreType.DMA((n_copies,)), pltpu.SemaphoreType.DMA((n_copies,)),
                   *[pltpu.HBM(a.shape, a.dtype) for a in list(srcs) + list(lands) + [thru]]),
        in_specs=[HBM_SPEC] * (2 * n + 1) + [ANY],
        out_specs=(SEM_SPEC, SEM_SPEC, *[HBM_SPEC] * (2 * n + 1)),
        input_output_aliases={i: 2 + i for i in range(2 * n + 1)},
        compiler_params=pltpu.CompilerParams(has_side_effects=DATAFLOW),
    )(*[_in_hbm(a) for a in list(srcs) + list(lands) + [thru]], after)
    return outs[0], outs[1], outs[2:2 + n], outs[2 + n:2 + 2 * n], outs[2 + 2 * n]


def _exchange_wait(name, started, plan, after):
    send_sems, recv_sems, srcs, lands, _ = started
    n = len(srcs)

    def body(*refs):
        src_refs, land_refs = refs[:n], refs[n:2 * n]
        s_sems, r_sems = refs[2 * n], refs[2 * n + 1]
        for k, (src, dst, to, _) in enumerate(plan(src_refs, land_refs)):
            cp = _remote(src, dst, s_sems, r_sems, k, to)
            cp.wait_send()
            cp.wait_recv()

    outs = pl.pallas_call(
        body, name=name,
        out_shape=tuple(pltpu.HBM(a.shape, a.dtype) for a in list(srcs) + list(lands)),
        in_specs=[HBM_SPEC] * (2 * n) + [SEM_SPEC, SEM_SPEC, ANY],
        out_specs=tuple([HBM_SPEC] * (2 * n)),
        input_output_aliases={i: i for i in range(2 * n)},
        compiler_params=pltpu.CompilerParams(has_side_effects=DATAFLOW),
    )(*srcs, *lands, send_sems, recv_sems, after)
    return outs[:n], outs[n:]


def _late_gather_plan(src_refs, land_refs):
    x, y, c = _my_place()
    chips = [(1 - x, y), (x, 1 - y), (1 - x, 1 - y)]
    plan = [(src, land.at[2 * x + y], (cx, cy, c)) for src, land in zip(src_refs, land_refs) for cx, cy in chips]
    return [entry + (k,) for k, entry in enumerate(plan)]


def _late_scatter_plan(src_refs, land_refs):
    x, y, c = _my_place()
    chips = [(1 - x, y), (x, 1 - y), (1 - x, 1 - y)]
    plan = [(src.at[2 * cx + cy], land.at[j], (cx, cy, c))
            for src, land in zip(src_refs, land_refs) for j, (cx, cy) in enumerate(chips)]
    return [entry + (k,) for k, entry in enumerate(plan)]


def _direct_scatter_plan(names):
    axes = [HALF_AXIS[n] for n in names]

    def plan(src_refs, land_refs):
        x, y, c = _my_place()
        chips = [(1 - x, y), (x, 1 - y), (1 - x, 1 - y)]
        out = []
        for i, (src, land) in enumerate(zip(src_refs, land_refs)):
            for f, (cx, cy) in enumerate(chips):
                for core in range(2):
                    out.append((_half(src, core, axes[i], (2 * cx + cy,)), land.at[2 * f + c], (cx, cy, core),
                                7 * i + 2 * f + c))
            out.append((_half(src, 1 - c, axes[i], (2 * x + y,)), land.at[6], (x, y, 1 - c), 7 * i + 6))
        return out

    return plan


def _row_tile(rows, mult=16, limit=ROW_TILE):
    return max(d for d in range(mult, limit + 1, mult) if rows % d == 0)


def _pair_sum(place, g, theirs, axis, name):
    nj, rr, cc = theirs.shape
    tr = _row_tile(rr, limit=1024)
    nb = rr // tr
    if axis == 0:
        g_map = lambda j, i, pr: (j, pr[0] * nb + i, 0)
    else:
        g_map = lambda j, i, pr: (j, i, pr[0])

    def body(pr, g_ref, t_ref, o_ref):
        o_ref[...] = (g_ref[...].astype(F32) + t_ref[...].astype(F32)).astype(BF16)

    spec = pl.BlockSpec((None, tr, cc), lambda j, i, pr: (j, i, 0))
    return pl.pallas_call(
        body, name=name,
        grid_spec=pltpu.PrefetchScalarGridSpec(
            num_scalar_prefetch=1, grid=(nj, nb),
            in_specs=[pl.BlockSpec((None, tr, cc), g_map), spec], out_specs=spec),
        out_shape=jax.ShapeDtypeStruct(theirs.shape, BF16))(place, g, theirs)


def _chip_sum(place, pair_sums, parts, name, transposed):
    _, rr, cc = parts.shape
    tr = _row_tile(rr, LANES) if transposed else _row_tile(rr, limit=1024)

    def body(pr, h_ref, p_ref, o_ref):
        acc = p_ref[0].astype(F32)
        for j in range(1, N_CHIPS - 1):
            acc = acc + p_ref[j].astype(F32)
        acc = acc + h_ref[...].astype(F32)
        o_ref[...] = (acc.T if transposed else acc).astype(BF16)

    out_spec = pl.BlockSpec((cc, tr), lambda i, pr: (0, i)) if transposed else pl.BlockSpec((tr, cc), lambda i, pr: (i, 0))
    return pl.pallas_call(
        body, name=name,
        grid_spec=pltpu.PrefetchScalarGridSpec(
            num_scalar_prefetch=1, grid=(rr // tr,),
            in_specs=[pl.BlockSpec((None, tr, cc), lambda i, pr: (pr[1], i, 0)),
                      pl.BlockSpec((N_CHIPS - 1, tr, cc), lambda i, pr: (0, i, 0))],
            out_specs=out_spec),
        out_shape=jax.ShapeDtypeStruct((cc, rr) if transposed else (rr, cc), BF16))(place, pair_sums, parts)


def _chip_sum_direct(place, g, parts, axis, name, transposed):
    n_parts, rr, cc = parts.shape
    tr = _row_tile(rr, LANES) if transposed else _row_tile(rr, limit=1024)
    nb = rr // tr
    if axis == 0:
        g_map = lambda i, pr: (pr[1], pr[0] * nb + i, 0)
    else:
        g_map = lambda i, pr: (pr[1], i, pr[0])

    def body(pr, g_ref, p_ref, o_ref):
        acc = p_ref[0].astype(F32)
        for j in range(1, n_parts):
            acc = acc + p_ref[j].astype(F32)
        acc = acc + g_ref[...].astype(F32)
        o_ref[...] = (acc.T if transposed else acc).astype(BF16)

    out_spec = pl.BlockSpec((cc, tr), lambda i, pr: (0, i)) if transposed else pl.BlockSpec((tr, cc), lambda i, pr: (i, 0))
    return pl.pallas_call(
        body, name=name,
        grid_spec=pltpu.PrefetchScalarGridSpec(
            num_scalar_prefetch=1, grid=(nb,),
            in_specs=[pl.BlockSpec((None, tr, cc), g_map), pl.BlockSpec((n_parts, tr, cc), lambda i, pr: (0, i, 0))],
            out_specs=out_spec),
        out_shape=jax.ShapeDtypeStruct((cc, rr) if transposed else (rr, cc), BF16))(place, g, parts)


def _silu(v):
    return v / (1.0 + jnp.exp(-v))


def _ada_fwd(c_all, w_shard, b_shard):
    n_seq, n_cols = c_all.shape[0], w_shard.shape[1]

    def body(c_ref, w_ref, b_ref, o_ref, mine_ref, send_sems, recv_sems, local_sem):
        mine_ref[...] = jnp.dot(_silu(c_ref[...]), w_ref[...], precision=lax.Precision.HIGHEST,
                                preferred_element_type=F32) + b_ref[...]
        start, finish = _small_gather(mine_ref, o_ref, send_sems, recv_sems, 0, local_sem)
        start()
        finish()

    return pl.pallas_call(
        body, name="ada_fwd", out_shape=jax.ShapeDtypeStruct((N_DEV * n_seq, n_cols), F32),
        scratch_shapes=[pltpu.VMEM((n_seq, n_cols), F32), pltpu.SemaphoreType.DMA((7,)), pltpu.SemaphoreType.DMA((7,)),
                        pltpu.SemaphoreType.DMA],
        compiler_params=pltpu.CompilerParams(vmem_limit_bytes=MM_VMEM_LIMIT))(c_all, w_shard, b_shard)


def _loss_and_grads(x2, ffn, target, gate):
    t, d = x2.shape

    def half_loss(x2_blk, ffn_blk, gate_row, target_blk):
        return 0.5 * _f_loss(x2_blk, ffn_blk, target_blk, gate_row)[0]

    def body(x2_ref, ffn_ref, tgt_ref, gate_ref, loss_ref, dx2_ref, dffn_ref, dgate_ref):
        rows, vjp = jax.vjp(lambda a, b, g: half_loss(a, b, g, tgt_ref[...]), x2_ref[...], ffn_ref[...], gate_ref[...])
        loss_ref[...] = rows
        dx2_ref[...], dffn_ref[...], dgate = vjp(jnp.ones_like(rows))

        @pl.when(pl.program_id(0) == 0)
        def _():
            dgate_ref[...] = jnp.zeros_like(dgate_ref)

        dgate_ref[...] += dgate

    blk = pl.BlockSpec((ROW_TILE, d), lambda i: (i, 0))
    row = pl.BlockSpec((1, d), lambda i: (0, 0))
    return pl.pallas_call(
        body, name="loss_and_grads", grid=(t // ROW_TILE,),
        in_specs=[blk, blk, blk, row],
        out_specs=[pl.BlockSpec((ROW_TILE, 1), lambda i: (i, 0)), blk, blk, row],
        out_shape=[jax.ShapeDtypeStruct((t, 1), F32), jax.ShapeDtypeStruct((t, d), F32), jax.ShapeDtypeStruct((t, d), F32),
                   jax.ShapeDtypeStruct((1, d), F32)],
        compiler_params=pltpu.CompilerParams(dimension_semantics=("arbitrary",), vmem_limit_bytes=MM_VMEM_LIMIT),
    )(x2, ffn, target, gate)


def _ada_bwd(c_all, dmod_cols):
    def body(c_ref, d_ref, o_ref):
        o_ref[...] = lax.dot_general(_silu(c_ref[...]), d_ref[...], (((0,), (0,)), ((), ())),
                                     precision=lax.Precision.HIGHEST, preferred_element_type=F32)

    return pl.pallas_call(body, name="ada_bwd", out_shape=jax.ShapeDtypeStruct((c_all.shape[1], dmod_cols.shape[1]), F32),
                          compiler_params=pltpu.CompilerParams(vmem_limit_bytes=MM_VMEM_LIMIT))(c_all, dmod_cols)


def _adamw_math(w, g, m, v):
    m = ADAM_B1 * m + (1.0 - ADAM_B1) * g
    v = ADAM_B2 * v + (1.0 - ADAM_B2) * (g * g)
    m_hat = m / (1.0 - ADAM_B1 ** ADAM_STEP)
    v_hat = v / (1.0 - ADAM_B2 ** ADAM_STEP)
    delta = -ADAM_LR * (m_hat / (jnp.sqrt(v_hat) + ADAM_EPS) + ADAM_WD * w)
    return delta, m, v


def _adamw(w, g, m, v, name):
    r, ccols = w.shape
    tr = max(d for d in range(8, ROW_TILE + 1, 8) if r % d == 0)
    spec = pl.BlockSpec((tr, ccols), lambda i: (i, 0))

    def body(w_ref, g_ref, m_ref, v_ref, d_ref, nm_ref, nv_ref):
        d_ref[...], nm_ref[...], nv_ref[...] = _adamw_math(w_ref[...], g_ref[...], m_ref[...], v_ref[...])

    return pl.pallas_call(body, name=name, grid=(r // tr,), in_specs=[spec] * 4, out_specs=[spec] * 3,
                          out_shape=[jax.ShapeDtypeStruct(w.shape, F32)] * 3,
                          compiler_params=pltpu.CompilerParams(vmem_limit_bytes=MM_VMEM_LIMIT))(w, g, m, v)


def _small_layout(sizes):
    offs, off = [], 0
    for n in sizes:
        offs.append(off)
        off += -(-n // LANES) * LANES
    total = -(-(off + LANES) // (8 * LANES)) * (8 * LANES)
    return offs, off, total


def _adamw_small(ws, g_all, ms, vs, offs, loss_off):
    n_p = len(ws)

    def device_sum(g_ref, off, width):
        blk = g_ref[:, off:off + width]
        acc = blk[0:1]
        for d in range(1, N_DEV):
            acc = acc + blk[d:d + 1]
        return acc

    def body(*refs):
        w_refs, m_refs, v_refs = refs[:n_p], refs[n_p:2 * n_p], refs[2 * n_p:3 * n_p]
        g_ref = refs[3 * n_p]
        outs = refs[3 * n_p + 1:]
        for i in range(n_p):
            n = w_refs[i].shape[1]
            g = device_sum(g_ref, offs[i], -(-n // LANES) * LANES)[:, :n]
            outs[i][...] = g
            outs[n_p + i][...], outs[2 * n_p + i][...], outs[3 * n_p + i][...] = _adamw_math(
                w_refs[i][...], g, m_refs[i][...], v_refs[i][...])
        outs[4 * n_p][...] = device_sum(g_ref, loss_off, LANES)

    res = pl.pallas_call(
        body, name="adamw_small",
        out_shape=[jax.ShapeDtypeStruct(a.shape, F32) for a in list(ws) * 4] + [jax.ShapeDtypeStruct((1, LANES), F32)],
    )(*ws, *ms, *vs, g_all)
    return res[:n_p], res[n_p:2 * n_p], res[2 * n_p:3 * n_p], res[3 * n_p:4 * n_p], res[4 * n_p]


def _adamw_halves(place, w, own, sib, m, v, axis, name, after):
    r, cc = w.shape
    if axis == 0:
        rows, gc = own.shape[0], own.shape[1]
        tr = _row_tile(rows)
        nb = rows // tr
        w_spec = pl.BlockSpec((tr, cc), lambda h, i, pr: (h * nb + i, 0))
        g_spec = pl.BlockSpec((tr, gc), lambda h, i, pr: (i, 0))
    else:
        tr = _row_tile(r)
        nb = r // tr
        gc = own.shape[1]
        w_spec = pl.BlockSpec((tr, gc), lambda h, i, pr: (i, h))
        g_spec = pl.BlockSpec((tr, gc), lambda h, i, pr: (i, 0))
    wc = w_spec.block_shape[1]

    def body(pr, w_ref, o_ref, s_ref, m_ref, v_ref, after_ref, g_ref, d_ref, nm_ref, nv_ref):
        g = jnp.where(pl.program_id(0) == pr[0], o_ref[...], s_ref[...]).astype(F32)[:, :wc]
        g_ref[...] = g
        d_ref[...], nm_ref[...], nv_ref[...] = _adamw_math(w_ref[...], g, m_ref[...], v_ref[...])

    return pl.pallas_call(
        body, name=name,
        grid_spec=pltpu.PrefetchScalarGridSpec(
            num_scalar_prefetch=1, grid=(2, nb),
            in_specs=[w_spec, g_spec, g_spec, w_spec, w_spec, ANY], out_specs=[w_spec] * 4),
        out_shape=[jax.ShapeDtypeStruct(w.shape, F32)] * 4,
        compiler_params=pltpu.CompilerParams(vmem_limit_bytes=MM_VMEM_LIMIT))(place, w, own, sib, m, v, after)


SMALL = ("b_ada", "norm_attn", "norm_ffn", "q_a_norm", "kv_a_norm", "q_norm", "k_nope_norm", "k_rope_norm",
         "out_norm_sb", "out_norm_mla")
WEIGHTS = ("w_ada", "b_ada", "norm_attn", "norm_ffn", "w_in", "q_a_norm", "w_q_up", "kv_a_norm", "w_kv_up",
           "q_norm", "k_nope_norm", "k_rope_norm", "out_norm_sb", "out_norm_mla", "w_out", "w_gate", "w_up",
           "w_down")


def kernel(x, c, positions, w_ada, b_ada, norm_attn, norm_ffn, w_in, q_a_norm, w_q_up, kv_a_norm, w_kv_up, q_norm, k_nope_norm, k_rope_norm, out_norm_sb, out_norm_mla, w_out, w_gate, w_up, w_down, loss_target, m_w_ada, m_b_ada, m_norm_attn, m_norm_ffn, m_w_in, m_q_a_norm, m_w_q_up, m_kv_a_norm, m_w_kv_up, m_q_norm, m_k_nope_norm, m_k_rope_norm, m_out_norm_sb, m_out_norm_mla, m_w_out, m_w_gate, m_w_up, m_w_down, v_w_ada, v_b_ada, v_norm_attn, v_norm_ffn, v_w_in, v_q_a_norm, v_w_q_up, v_kv_a_norm, v_w_kv_up, v_q_norm, v_k_nope_norm, v_k_rope_norm, v_out_norm_sb, v_out_norm_mla, v_w_out, v_w_gate, v_w_up, v_w_down):
    local = dict(locals())
    w = {n: local[n][0] for n in WEIGHTS}
    m = {n: local["m_" + n][0] for n in WEIGHTS}
    v = {n: local["v_" + n][0] for n in WEIGHTS}
    small = {n: w[n].reshape(1, -1) for n in SMALL}
    ix, iy, ic = _my_place()
    chip = 2 * ix + iy
    dev = 2 * chip + ic
    xs, target = x[0], loss_target[0]
    seq = xs.shape[0]

    ff_pad = FF_SHARD_PAD - FF_SHARD
    pads = {"w_gate": ((0, 0), (0, ff_pad)), "w_up": ((0, 0), (0, ff_pad)), "w_down": ((0, ff_pad), (0, 0))}
    shards = {n: jnp.pad(w[n].astype(BF16), pads[n]) if n in pads else w[n].astype(BF16) for n in BIG}
    early, early_done, c_gathered = _gather_weights(EARLY, [shards[n] for n in EARLY], c.reshape(8, LANES))
    gathered = dict(zip(EARLY, early))

    c_all = c_gathered.reshape(N_DEV, D_MODEL)
    ada_cols = w["w_ada"].shape[1]
    b_cols = lax.dynamic_slice_in_dim(small["b_ada"], chip * ada_cols, ada_cols, axis=1)
    mod_all = _ada_fwd(c_all, w["w_ada"], b_cols).reshape(N_CHIPS, 2, N_DEV, ada_cols)
    mod = lax.dynamic_index_in_dim(mod_all[:, 0], dev, axis=1, keepdims=False).reshape(1, N_MOD * D_MODEL)

    lands = [lax.dynamic_update_index_in_dim(lax.empty((N_CHIPS,) + shards[n].shape, BF16), shards[n], chip, 0)
             for n in LATE]
    late_gather = _exchange_start("gather_late_start", [shards[n] for n in LATE], lands, _late_gather_plan,
                                  3 * len(LATE), early_done, mod)
    mod = late_gather[4]

    half = MLA_ROPE // 2
    freqs = 1.0 / (ROPE_THETA ** (np.arange(half, dtype=np.float32) / half))
    zeros = np.zeros(LANES - MLA_ROPE, np.float32)
    freqs_row = jnp.asarray(np.concatenate([freqs, freqs, zeros]).astype(np.float32)[None])
    sign_row = jnp.asarray(np.concatenate([-np.ones(half), np.ones(half), zeros]).astype(np.float32)[None])
    cos, sin = _rope_tables(positions.reshape(seq, 1), freqs_row, sign_row)

    place = jnp.stack([ic, chip]).astype(jnp.int32)
    small_params = {n: small[n] for n in SMALL if n != "b_ada"}

    p1 = {**{n: gathered[n] for n in EARLY}, **small_params}
    (mixed, x_res), mixing_vjp = jax.vjp(lambda x_, mod_, p_: _mixing_stage(x_, mod_, p_, cos, sin), xs, mod, p1)
    _, landed = _exchange_wait("gather_late_wait", late_gather, _late_gather_plan, mixed)
    p2 = {**dict(zip(LATE, landed)), **small_params}
    (x2, ffn), ffn_vjp = jax.vjp(_ffn_stage, x_res, mixed, mod, p2)
    loss_rows, g_x2, g_ffn, g_gate2 = _loss_and_grads(x2, ffn, target, _mod_parts(mod)[5])
    loss_part = jnp.sum(loss_rows)
    gx2, gmixed, gmod2, gp2 = ffn_vjp((g_x2, g_ffn))
    gmod2 = gmod2 + jnp.concatenate([jnp.zeros((1, (N_MOD - 1) * D_MODEL), F32), g_gate2], axis=1)
    late_grads = [gp2[n] for n in LATE]
    late_plan = _direct_scatter_plan(LATE)
    late_scatter = _exchange_start(
        "grad_scatter_late_start", late_grads,
        [lax.empty((7,) + _half_shape(gr.shape[1:], HALF_AXIS[n]), BF16) for n, gr in zip(LATE, late_grads)],
        late_plan, 7 * len(LATE), gx2, gmixed)
    gx, gmod1, gp1 = mixing_vjp((late_scatter[4], gx2))
    gmod = gmod1 + gmod2
    gp = {n: gp1[n] + gp2[n] for n in small_params}

    sizes = [w[n].size for n in SMALL]
    offs, loss_off, n_small = _small_layout(sizes)
    pieces = []
    for n, size in zip(SMALL, sizes):
        pieces.append(gmod if n == "b_ada" else gp[n])
        if size % LANES:
            pieces.append(jnp.zeros((1, LANES - size % LANES), F32))
    pieces += [jnp.full((1, LANES), loss_part), jnp.zeros((1, n_small - loss_off - LANES), F32)]
    small_vec = jnp.concatenate(pieces, axis=1)

    g, delta, new_m, new_v = {}, {}, {}, {}

    def update(names, own, sib, after):
        for n, o, s in zip(names, own, sib):
            if n in TRANSPOSED_UPDATE:
                res = _adamw_halves(place, w[n].T, o, s, m[n].T, v[n].T, 1, "adamw_" + n, after)
                g[n], delta[n], new_m[n], new_v[n] = [r.T for r in res]
            else:
                g[n], delta[n], new_m[n], new_v[n] = _adamw_halves(place, w[n], o, s, m[n], v[n], HALF_AXIS[n],
                                                                   "adamw_" + n, after)

    late_grads, late_parts = _exchange_wait("grad_scatter_late_wait", late_scatter, late_plan, gx)
    own_late = [_chip_sum_direct(place, gr, pt, HALF_AXIS[n], "grad_chip_sum_" + n, n in TRANSPOSED_UPDATE)
                for n, gr, pt in zip(LATE, late_grads, late_parts)]
    early_grads = [gp1[n] for n in EARLY]
    theirs, small_gathered = _pair_exchange(EARLY, early_grads, "grad_pair_exchange_early",
                                            small_vec.reshape(8, n_small // 8))
    small_all = small_gathered.reshape(N_DEV, n_small)
    sib_late = _sibling_join(own_late, "grad_sibling_join_late", small_all)
    early_sums = [_pair_sum(place, gr, th, HALF_AXIS[n], "grad_pair_sum_" + n)
                  for n, gr, th in zip(EARLY, early_grads, theirs)]
    early_scatter = _exchange_start(
        "grad_scatter_early_start", early_sums,
        [lax.empty((N_CHIPS - 1,) + s.shape[1:], BF16) for s in early_sums], _late_scatter_plan, 3 * len(EARLY),
        sib_late[0], small_all)
    small_all = early_scatter[4]
    update(LATE, own_late, sib_late, small_all)

    *small_out, loss_row = _adamw_small([small[n] for n in SMALL], small_all, [m[n].reshape(1, -1) for n in SMALL],
                                        [v[n].reshape(1, -1) for n in SMALL], offs, loss_off)
    loss = loss_row[0, 0]
    for d, outs_d in zip((g, delta, new_m, new_v), small_out):
        d.update({n: o.reshape(w[n].shape) for n, o in zip(SMALL, outs_d)})

    dmod_all = small_all[:, :N_MOD * D_MODEL]
    g["w_ada"] = _ada_bwd(c_all, lax.dynamic_slice_in_dim(dmod_all, chip * ada_cols, ada_cols, axis=1))
    delta["w_ada"], new_m["w_ada"], new_v["w_ada"] = _adamw(w["w_ada"], g["w_ada"], m["w_ada"], v["w_ada"], "adamw_w_ada")

    early_sums, early_parts = _exchange_wait("grad_scatter_early_wait", early_scatter, _late_scatter_plan,
                                             delta["w_ada"])
    own_early = [_chip_sum(place, ps, pt, "grad_chip_sum_" + n, n in TRANSPOSED_UPDATE)
                 for n, ps, pt in zip(EARLY, early_sums, early_parts)]
    sib_early = _sibling_join(own_early, "grad_sibling_join_early", delta["w_ada"])
    update(EARLY, own_early, sib_early, sib_early[0])

    def outs(d):
        return [d[n][None] for n in WEIGHTS]

    return (loss, gx[None], *outs(g), *outs(delta), *outs(new_m), *outs(new_v))
```

```python
import numpy as np
import jax
import jax.numpy as jnp
from jax import lax
from jax.experimental import pallas as pl
from jax.experimental.pallas import tpu as pltpu

F32 = jnp.float32
BF16 = jnp.bfloat16
MESH = pl.DeviceIdType.MESH
ANY = pl.BlockSpec(memory_space=pl.ANY)

D_MODEL = 1024
SB_HEADS = 8
SB_HEAD_DIM = 64
SB_WIDTH = 512
MLA_HEADS = 4
MLA_NOPE = 128
MLA_ROPE = 64
MLA_QK = 192
MLA_V = 128
MLA_Q_RANK = 384
MLA_KV_RANK = 256
D_FF = 2816
N_MOD = 6
ROPE_THETA = 10000.0
EPS = 1e-6
LANES = 128

ADAM_LR = 0.001
ADAM_B1 = 0.9
ADAM_B2 = 0.999
ADAM_EPS = 1e-08
ADAM_WD = 0.01
ADAM_STEP = 10

N_CHIPS = 4
N_DEV = 8
ROW_TILE = 512
MM_ROW_TILE = 512
ATT_BLK = 256
MM_VMEM_LIMIT = 56 * 1024 * 1024
FF_SHARD = D_FF // N_CHIPS
FF_SHARD_PAD = 768


def _mm(a, b, mode, name, tm, tn, out_dtype=F32):
    if mode == "nn":
        (m, k), n = a.shape, b.shape[1]
        a_spec = pl.BlockSpec((tm, k), lambda j, i: (i, 0))
        b_spec = pl.BlockSpec((k, tn), lambda j, i: (0, j))
        dims = (((1,), (0,)), ((), ()))
    elif mode == "nt":
        (m, k), n = a.shape, b.shape[0]
        a_spec = pl.BlockSpec((tm, k), lambda j, i: (i, 0))
        b_spec = pl.BlockSpec((tn, k), lambda j, i: (j, 0))
        dims = (((1,), (1,)), ((), ()))
    else:
        (k, m), n = a.shape, b.shape[1]
        a_spec = pl.BlockSpec((k, tm), lambda j, i: (0, i))
        b_spec = pl.BlockSpec((k, tn), lambda j, i: (0, j))
        dims = (((0,), (0,)), ((), ()))
    assert m % tm == 0 and n % tn == 0, (name, m, n, tm, tn)

    def body(a_ref, b_ref, o_ref):
        o_ref[...] = lax.dot_general(a_ref[...].astype(BF16), b_ref[...].astype(BF16), dims,
                                     preferred_element_type=F32).astype(out_dtype)

    return pl.pallas_call(
        body, name=name, grid=(n // tn, m // tm),
        in_specs=[a_spec, b_spec],
        out_specs=pl.BlockSpec((tm, tn), lambda j, i: (i, j)),
        out_shape=jax.ShapeDtypeStruct((m, n), out_dtype),
        compiler_params=pltpu.CompilerParams(dimension_semantics=("arbitrary", "arbitrary"),
                                             vmem_limit_bytes=MM_VMEM_LIMIT),
    )(a, b)


def _make_linear(name, tk_w, tn_w):
    @jax.custom_vjp
    def op(a, w):
        return _mm(a, w, "nn", name + "_fwd", MM_ROW_TILE, w.shape[1])

    def fwd(a, w):
        return op(a, w), (a, w)

    def bwd(res, dy):
        a, w = res
        da = _mm(dy, w, "nt", name + "_dx", MM_ROW_TILE, w.shape[0])
        dw = _mm(a, dy, "tn", name + "_dw", tk_w, tn_w, out_dtype=BF16)
        return da, dw

    op.defvjp(fwd, bwd)
    return op


def _make_linear_split_t(name, widths, tk_w):
    starts = [sum(widths[:g]) for g in range(len(widths))]

    def call_fwd(a, wt):
        t, k = a.shape
        n = wt.shape[0]

        def body(a_ref, w_ref, *o_refs):
            y = _nt(a_ref[...].astype(BF16), w_ref[...])
            for o_ref, s0, wd in zip(o_refs, starts, widths):
                o_ref[...] = y[:, s0:s0 + wd]

        return pl.pallas_call(
            body, name=name + "_fwd", grid=(t // MM_ROW_TILE,),
            in_specs=[pl.BlockSpec((MM_ROW_TILE, k), lambda i: (i, 0)), pl.BlockSpec((n, k), lambda i: (0, 0))],
            out_specs=[pl.BlockSpec((MM_ROW_TILE, wd), lambda i: (i, 0)) for wd in widths],
            out_shape=[jax.ShapeDtypeStruct((t, wd), F32) for wd in widths],
            compiler_params=pltpu.CompilerParams(dimension_semantics=("arbitrary",), vmem_limit_bytes=MM_VMEM_LIMIT),
        )(a, wt)

    def call_dx(dys, wt):
        t = dys[0].shape[0]
        n, k = wt.shape

        def body(*refs):
            dy_refs, w_ref, o_ref = refs[:-2], refs[-2], refs[-1]
            acc = jnp.zeros((MM_ROW_TILE, k), F32)
            for dy_ref, s0, wd in zip(dy_refs, starts, widths):
                acc = acc + jnp.dot(dy_ref[...].astype(BF16), w_ref[s0:s0 + wd, :], preferred_element_type=F32)
            o_ref[...] = acc

        return pl.pallas_call(
            body, name=name + "_dx", grid=(t // MM_ROW_TILE,),
            in_specs=[pl.BlockSpec((MM_ROW_TILE, wd), lambda i: (i, 0)) for wd in widths]
            + [pl.BlockSpec((n, k), lambda i: (0, 0))],
            out_specs=pl.BlockSpec((MM_ROW_TILE, k), lambda i: (i, 0)),
            out_shape=jax.ShapeDtypeStruct((t, k), F32),
            compiler_params=pltpu.CompilerParams(dimension_semantics=("arbitrary",), vmem_limit_bytes=MM_VMEM_LIMIT),
        )(*dys, wt)

    def call_dw(a, dys, wt):
        t, k = a.shape
        n = wt.shape[0]

        def body(a_ref, *refs):
            dy_refs, o_ref = refs[:-1], refs[-1]
            ab = a_ref[...].astype(BF16)
            for dy_ref, s0, wd in zip(dy_refs, starts, widths):
                o_ref[s0:s0 + wd, :] = _tn(dy_ref[...].astype(BF16), ab).astype(BF16)
            if starts[-1] + widths[-1] < n:
                o_ref[starts[-1] + widths[-1]:, :] = jnp.zeros((n - starts[-1] - widths[-1], tk_w), BF16)

        return pl.pallas_call(
            body, name=name + "_dw", grid=(k // tk_w,),
            in_specs=[pl.BlockSpec((t, tk_w), lambda i: (0, i))]
            + [pl.BlockSpec((t, wd), lambda i: (0, 0)) for wd in widths],
            out_specs=pl.BlockSpec((n, tk_w), lambda i: (0, i)),
            out_shape=jax.ShapeDtypeStruct((n, k), BF16),
            compiler_params=pltpu.CompilerParams(dimension_semantics=("arbitrary",), vmem_limit_bytes=MM_VMEM_LIMIT),
        )(a, *dys)

    @jax.custom_vjp
    def op(a, wt):
        return tuple(call_fwd(a, wt))

    def fwd(a, wt):
        return op(a, wt), (a, wt)

    def bwd(res, dys):
        a, wt = res
        return call_dx(dys, wt), call_dw(a, dys, wt)

    op.defvjp(fwd, bwd)
    return op


def _make_linear_split(name, widths, tk_w):
    starts = [sum(widths[:g]) for g in range(len(widths))]

    def call_fwd(a, w):
        t, k = a.shape
        n = w.shape[1]

        def body(a_ref, w_ref, *o_refs):
            y = jnp.dot(a_ref[...].astype(BF16), w_ref[...], preferred_element_type=F32)
            for o_ref, s0, wd in zip(o_refs, starts, widths):
                o_ref[...] = y[:, s0:s0 + wd]

        return pl.pallas_call(
            body, name=name + "_fwd", grid=(t // MM_ROW_TILE,),
            in_specs=[pl.BlockSpec((MM_ROW_TILE, k), lambda i: (i, 0)), pl.BlockSpec((k, n), lambda i: (0, 0))],
            out_specs=[pl.BlockSpec((MM_ROW_TILE, wd), lambda i: (i, 0)) for wd in widths],
            out_shape=[jax.ShapeDtypeStruct((t, wd), F32) for wd in widths],
            compiler_params=pltpu.CompilerParams(dimension_semantics=("arbitrary",), vmem_limit_bytes=MM_VMEM_LIMIT),
        )(a, w)

    def call_dx(dys, w):
        t = dys[0].shape[0]
        k, n = w.shape

        def body(*refs):
            dy_refs, w_ref, o_ref = refs[:-2], refs[-2], refs[-1]
            acc = jnp.zeros((MM_ROW_TILE, k), F32)
            for dy_ref, s0, wd in zip(dy_refs, starts, widths):
                acc = acc + _nt(dy_ref[...].astype(BF16), w_ref[:, s0:s0 + wd])
            o_ref[...] = acc

        return pl.pallas_call(
            body, name=name + "_dx", grid=(t // MM_ROW_TILE,),
            in_specs=[pl.BlockSpec((MM_ROW_TILE, wd), lambda i: (i, 0)) for wd in widths]
            + [pl.BlockSpec((k, n), lambda i: (0, 0))],
            out_specs=pl.BlockSpec((MM_ROW_TILE, k), lambda i: (i, 0)),
            out_shape=jax.ShapeDtypeStruct((t, k), F32),
            compiler_params=pltpu.CompilerParams(dimension_semantics=("arbitrary",), vmem_limit_bytes=MM_VMEM_LIMIT),
        )(*dys, w)

    def call_dw(a, dys, w):
        t, k = a.shape
        n = w.shape[1]

        def body(a_ref, *refs):
            dy_refs, o_ref = refs[:-1], refs[-1]
            ab = a_ref[...].astype(BF16)
            for dy_ref, s0, wd in zip(dy_refs, starts, widths):
                o_ref[:, s0:s0 + wd] = _tn(ab, dy_ref[...].astype(BF16)).astype(BF16)
            if starts[-1] + widths[-1] < n:
                o_ref[:, starts[-1] + widths[-1]:] = jnp.zeros((tk_w, n - starts[-1] - widths[-1]), BF16)

        return pl.pallas_call(
            body, name=name + "_dw", grid=(k // tk_w,),
            in_specs=[pl.BlockSpec((t, tk_w), lambda i: (0, i))]
            + [pl.BlockSpec((t, wd), lambda i: (0, 0)) for wd in widths],
            out_specs=pl.BlockSpec((tk_w, n), lambda i: (i, 0)),
            out_shape=jax.ShapeDtypeStruct((k, n), BF16),
            compiler_params=pltpu.CompilerParams(dimension_semantics=("arbitrary",), vmem_limit_bytes=MM_VMEM_LIMIT),
        )(a, *dys)

    @jax.custom_vjp
    def op(a, w):
        return tuple(call_fwd(a, w))

    def fwd(a, w):
        return op(a, w), (a, w)

    def bwd(res, dys):
        a, w = res
        return call_dx(dys, w), call_dw(a, dys, w)

    op.defvjp(fwd, bwd)
    return op


def _row_spec(arr, tb):
    return pl.BlockSpec((tb, arr.shape[1]), lambda i: (i, 0))


def _full_spec(arr):
    return pl.BlockSpec(arr.shape, lambda i: (0, 0))


def _make_rowwise(name, f, n_rows, n_params, out_cols, diff_rows, out_dtypes=None, grad_dtypes=None):
    n_out = len(out_cols)
    out_dtypes = out_dtypes or [F32] * n_out
    grad_dtypes = grad_dtypes or [F32] * sum(diff_rows)

    def call_fwd(rows, params):
        t = rows[0].shape[0]

        def body(*refs):
            ins = [r[...] for r in refs[:n_rows + n_params]]
            outs = f(*ins)
            for o_ref, o in zip(refs[n_rows + n_params:], outs):
                o_ref[...] = o.astype(o_ref.dtype)

        return pl.pallas_call(
            body, name=name + "_fwd", grid=(t // ROW_TILE,),
            in_specs=[_row_spec(a, ROW_TILE) for a in rows] + [_full_spec(p) for p in params],
            out_specs=[pl.BlockSpec((ROW_TILE, n), lambda i: (i, 0)) for n in out_cols],
            out_shape=[jax.ShapeDtypeStruct((t, n), dt) for n, dt in zip(out_cols, out_dtypes)],
            compiler_params=pltpu.CompilerParams(dimension_semantics=("arbitrary",),
                                                 vmem_limit_bytes=MM_VMEM_LIMIT),
        )(*rows, *params)

    def call_bwd(rows, params, cts):
        t = rows[0].shape[0]
        d_rows = [a for a, d in zip(rows, diff_rows) if d]
        n_in = n_rows + n_params + n_out

        def body(*refs):
            ins = [r[...] for r in refs[:n_rows + n_params]]
            ct = tuple(r[...].astype(F32) for r in refs[n_rows + n_params:n_in])
            _, vjp = jax.vjp(f, *ins)
            grads = vjp(ct)
            out_refs = refs[n_in:]
            g_rows = [g for g, d in zip(grads[:n_rows], diff_rows) if d]
            for o_ref, g in zip(out_refs[:len(g_rows)], g_rows):
                o_ref[...] = g.astype(o_ref.dtype)
            p_refs = out_refs[len(g_rows):]

            if p_refs:
                @pl.when(pl.program_id(0) == 0)
                def _():
                    for p_ref in p_refs:
                        p_ref[...] = jnp.zeros_like(p_ref)

                for p_ref, g in zip(p_refs, grads[n_rows:]):
                    p_ref[...] += g

        return pl.pallas_call(
            body, name=name + "_bwd", grid=(t // ROW_TILE,),
            in_specs=[_row_spec(a, ROW_TILE) for a in rows] + [_full_spec(p) for p in params]
            + [_row_spec(c, ROW_TILE) for c in cts],
            out_specs=[_row_spec(a, ROW_TILE) for a in d_rows] + [_full_spec(p) for p in params],
            out_shape=[jax.ShapeDtypeStruct(a.shape, dt) for a, dt in zip(d_rows, grad_dtypes)]
            + [jax.ShapeDtypeStruct(p.shape, F32) for p in params],
            compiler_params=pltpu.CompilerParams(dimension_semantics=("arbitrary",),
                                                 vmem_limit_bytes=MM_VMEM_LIMIT),
        )(*rows, *params, *cts)

    @jax.custom_vjp
    def op(*args):
        return tuple(call_fwd(args[:n_rows], args[n_rows:]))

    def fwd(*args):
        return op(*args), args

    def bwd(args, cts):
        rows, params = args[:n_rows], args[n_rows:]
        outs = call_bwd(rows, params, cts)
        it = iter(outs)
        g_rows = [next(it) if d else jnp.zeros_like(a) for a, d in zip(rows, diff_rows)]
        return tuple(g_rows) + tuple(it)

    op.defvjp(fwd, bwd)
    return op


def _rms(x, g, n):
    return x * lax.rsqrt(jnp.sum(x * x, axis=-1, keepdims=True) * (1.0 / n) + EPS) * g


def _f_pre_attn(x, g, scale, shift):
    return _rms(x, g, D_MODEL) * (1.0 + scale) + shift, x


def _f_mla_a(cq, ckv, gq, gkv):
    return _rms(cq, gq, MLA_Q_RANK), _rms(ckv, gkv, MLA_KV_RANK)


@jax.custom_vjp
def _split_lanes(x):
    return tuple(x[:, i * LANES:(i + 1) * LANES] for i in range(x.shape[1] // LANES))


def _split_lanes_fwd(x):
    return _split_lanes(x), None


def _split_lanes_bwd(_, cts):
    return (jnp.concatenate(cts, axis=1),)


_split_lanes.defvjp(_split_lanes_fwd, _split_lanes_bwd)


def _f_mla_b(qall, kn_all, kr, kr_sw, cos, sin, gqn, gqr, gqr_sw, gkn, gkr, gkr_sw):
    q = _split_lanes(qall)
    kn = _split_lanes(kn_all)
    qn_o, qr_o, kn_o = [], [], []
    for h in range(MLA_HEADS):
        qn, qr, qs = q[h], q[MLA_HEADS + h], q[2 * MLA_HEADS + h]
        ss = jnp.sum(qn * qn, axis=-1, keepdims=True) + jnp.sum(qr * qr, axis=-1, keepdims=True)
        rs = lax.rsqrt(ss * (1.0 / MLA_QK) + EPS)
        qn_o.append(qn * rs * gqn)
        qr_o.append((qr * rs * gqr) * cos + (qs * rs * gqr_sw) * sin)
        kn_o.append(_rms(kn[h], gkn, MLA_NOPE))
    rs = lax.rsqrt(jnp.sum(kr * kr, axis=-1, keepdims=True) * (1.0 / MLA_ROPE) + EPS)
    kr_o = (kr * rs * gkr) * cos + (kr_sw * rs * gkr_sw) * sin
    return (jnp.concatenate(qn_o, axis=1), jnp.concatenate(qr_o, axis=1), jnp.concatenate(kn_o, axis=1), kr_o)


def _f_post_attn(o_sb, o_mla, g_sb, g_mla):
    return (jnp.concatenate([_rms(o_sb, g_sb, SB_WIDTH), _rms(o_mla, g_mla, SB_WIDTH)], axis=1),)


def _f_pre_ffn(x, attn, gate, g, scale, shift):
    x2 = x + gate * attn
    return x2, _rms(x2, g, D_MODEL) * (1.0 + scale) + shift


def _f_swiglu(gt, up):
    return (gt / (1.0 + jnp.exp(-gt)) * up,)


def _f_loss(x2, ffn, target, gate):
    err = x2 + gate * ffn - target
    return (jnp.sum(err * err, axis=-1, keepdims=True) * (1.0 / D_MODEL),)


def _rope_tables(pos_col, freqs, sign):
    t = pos_col.shape[0]

    def body(p_ref, f_ref, s_ref, cos_ref, sin_ref):
        ang = p_ref[...].astype(F32) * f_ref[...]
        live = jnp.abs(s_ref[...])
        cos_ref[...] = jnp.cos(ang) * live
        sin_ref[...] = jnp.sin(ang) * s_ref[...]

    return pl.pallas_call(
        body, name="rope_tables", grid=(t // ROW_TILE,),
        in_specs=[pl.BlockSpec((ROW_TILE, 1), lambda i: (i, 0)), _full_spec(freqs), _full_spec(sign)],
        out_specs=[pl.BlockSpec((ROW_TILE, LANES), lambda i: (i, 0))] * 2,
        out_shape=[jax.ShapeDtypeStruct((t, LANES), F32)] * 2,
    )(pos_col, freqs, sign)


def _hi_lo_dot(x, tri):
    hi = x.astype(BF16)
    lo = (x - hi.astype(F32)).astype(BF16)
    return (jnp.dot(hi, tri, preferred_element_type=F32) + jnp.dot(lo, tri, preferred_element_type=F32))


def _tri(cmp):
    r = lax.broadcasted_iota(jnp.int32, (ATT_BLK, ATT_BLK), 0)
    c = lax.broadcasted_iota(jnp.int32, (ATT_BLK, ATT_BLK), 1)
    return cmp(r, c).astype(BF16)


def _nt(a, b):
    return lax.dot_general(a, b, (((1,), (1,)), ((), ())), preferred_element_type=F32)


def _tn(a, b):
    return lax.dot_general(a, b, (((0,), (0,)), ((), ())), preferred_element_type=F32)


def _sb_logs(z):
    lb = jnp.minimum(z, 0.0) - jnp.log(1.0 + jnp.exp(-jnp.abs(z)))
    return lb, lb - z


def _sb_fwd(q, k, v):
    t = q.shape[0]
    nq = t // ATT_BLK
    scale = SB_HEAD_DIM ** -0.5

    def body(q_ref, k_ref, v_ref, o_ref, tot_ref):
        qi = pl.program_id(1)
        lane = lax.broadcasted_iota(jnp.int32, (ATT_BLK, LANES), 1)
        tri = _tri(lambda r, c: r > c)
        qv = q_ref[...] * scale
        heads = [(lane // SB_HEAD_DIM) == hh for hh in range(2)]
        qms = [jnp.where(mine, qv, 0.0).astype(BF16) for mine in heads]

        def blocks(kbs, carry, diagonal):
            acc = carry[0]
            nb = len(kbs)
            chains = [(b, hh) for b in range(nb) for hh in range(2)]
            offs = [pl.multiple_of(kb * ATT_BLK, ATT_BLK) for kb in kbs]
            kks = [k_ref[pl.ds(off, ATT_BLK), :].astype(BF16) for off in offs]
            v_blks = [v_ref[pl.ds(off, ATT_BLK), :] for off in offs]
            if any(diagonal):
                valid = (lax.broadcasted_iota(jnp.int32, (ATT_BLK, ATT_BLK), 1)
                         < lax.broadcasted_iota(jnp.int32, (ATT_BLK, ATT_BLK), 0))
            zs = {ch: _nt(qms[ch[1]], kks[ch[0]]) for ch in chains}
            vvs = {(b, hh): jnp.where(heads[hh], v_blks[b], 0.0).astype(BF16) for b, hh in chains}
            logs = {ch: _sb_logs(zs[ch]) for ch in chains}
            l1ms = {ch: jnp.where(valid, logs[ch][1], 0.0) if diagonal[ch[0]] else logs[ch][1] for ch in chains}
            run = {(0, hh): carry[1 + hh] for hh in range(2)}
            for b, hh in chains:
                run[(b + 1, hh)] = run[(b, hh)] + jnp.sum(l1ms[(b, hh)], axis=-1, keepdims=True)
            afters = {ch: _hi_lo_dot(l1ms[ch], tri) for ch in chains}
            ws = {ch: jnp.exp(logs[ch][0] + (afters[ch] + run[ch])) for ch in chains}
            ws = {ch: jnp.where(valid, ws[ch], 0.0) if diagonal[ch[0]] else ws[ch] for ch in chains}
            for ch in chains:
                acc = acc + jnp.dot(ws[ch].astype(BF16), vvs[ch], preferred_element_type=F32)
            return (acc, run[(nb, 0)], run[(nb, 1)])

        zero = jnp.zeros((ATT_BLK, 1), F32)
        init = (jnp.zeros((ATT_BLK, LANES), F32), zero, zero)
        carry = lax.cond(qi % 2 == 1, lambda cr: blocks([qi, qi - 1], cr, (True, False)),
                         lambda cr: blocks([qi], cr, (True,)), init)
        top = qi - 1 - qi % 2
        carry = lax.fori_loop(0, qi // 2, lambda pr, cr: blocks([top - 2 * pr, top - 1 - 2 * pr], cr, (False, False)),
                              carry)
        o_ref[...] = carry[0]
        for hh in range(2):
            tot_ref[:, hh * LANES:(hh + 1) * LANES] = jnp.broadcast_to(carry[1 + hh], (ATT_BLK, LANES))

    return pl.pallas_call(
        body, name="sb_attn_fwd", grid=(SB_HEADS // 2, nq),
        in_specs=[pl.BlockSpec((ATT_BLK, LANES), lambda p, i: (i, p)),
                  pl.BlockSpec((t, LANES), lambda p, i: (0, p)),
                  pl.BlockSpec((t, LANES), lambda p, i: (0, p))],
        out_specs=[pl.BlockSpec((ATT_BLK, LANES), lambda p, i: (i, p)),
                   pl.BlockSpec((ATT_BLK, 2 * LANES), lambda p, i: (i, p))],
        out_shape=[jax.ShapeDtypeStruct((t, SB_WIDTH), F32), jax.ShapeDtypeStruct((t, SB_HEADS * LANES), F32)],
        compiler_params=pltpu.CompilerParams(dimension_semantics=("arbitrary", "arbitrary")),
    )(q, k, v)


def _sb_bwd(q, k, v, tot, do):
    t = q.shape[0]
    nq = t // ATT_BLK
    scale = SB_HEAD_DIM ** -0.5

    def body(q_ref, k_ref, v_ref, tot_ref, do_ref, dq_ref, dk_ref, dv_ref):
        qi = pl.program_id(1)

        @pl.when(qi == 0)
        def _():
            dk_ref[...] = jnp.zeros_like(dk_ref)
            dv_ref[...] = jnp.zeros_like(dv_ref)

        lane = lax.broadcasted_iota(jnp.int32, (ATT_BLK, LANES), 1)
        tri_incl = _tri(lambda r, c: r <= c)
        tri_lt = _tri(lambda r, c: r < c)
        qv = q_ref[...] * scale
        dov = do_ref[...]
        heads = [(lane // SB_HEAD_DIM) == hh for hh in range(2)]
        qms = [jnp.where(mine, qv, 0.0).astype(BF16) for mine in heads]
        doms = [jnp.where(mine, dov, 0.0).astype(BF16) for mine in heads]
        tots = [tot_ref[:, hh * LANES:hh * LANES + 1] for hh in range(2)]

        def blocks(kbs, carry, diagonal):
            dq = carry[0]
            nb = len(kbs)
            chains = [(b, hh) for b in range(nb) for hh in range(2)]
            offs = [pl.multiple_of(kb * ATT_BLK, ATT_BLK) for kb in kbs]
            k_blks = [k_ref[pl.ds(off, ATT_BLK), :] for off in offs]
            vvs = [v_ref[pl.ds(off, ATT_BLK), :].astype(BF16) for off in offs]
            if any(diagonal):
                valid = (lax.broadcasted_iota(jnp.int32, (ATT_BLK, ATT_BLK), 1)
                         < lax.broadcasted_iota(jnp.int32, (ATT_BLK, ATT_BLK), 0))
            kks = {(b, hh): jnp.where(heads[hh], k_blks[b], 0.0).astype(BF16) for b, hh in chains}
            zs = {ch: _nt(qms[ch[1]], kks[ch]) for ch in chains}
            dws = {ch: _nt(doms[ch[1]], vvs[ch[0]]) for ch in chains}
            logs = {ch: _sb_logs(zs[ch]) for ch in chains}
            lbs = {ch: logs[ch][0] for ch in chains}
            l1m_all = {ch: logs[ch][1] for ch in chains}
            l1ms = {ch: jnp.where(valid, l1m_all[ch], 0.0) if diagonal[ch[0]] else l1m_all[ch] for ch in chains}
            pre, c_de = {}, {}
            for hh in range(2):
                pre[(0, hh)], c_de[(0, hh)] = carry[1 + 2 * hh], carry[2 + 2 * hh]
            for b, hh in chains:
                pre[(b + 1, hh)] = pre[(b, hh)] + jnp.sum(l1ms[(b, hh)], axis=-1, keepdims=True)
            prefix = {ch: _hi_lo_dot(l1ms[ch], tri_incl) for ch in chains}
            ws = {ch: jnp.exp(lbs[ch] + (tots[ch[1]] - (prefix[ch] + pre[ch]))) for ch in chains}
            ws = {ch: jnp.where(valid, ws[ch], 0.0) if diagonal[ch[0]] else ws[ch] for ch in chains}
            d_es = {ch: ws[ch] * dws[ch] for ch in chains}
            for b, hh in chains:
                c_de[(b + 1, hh)] = c_de[(b, hh)] + jnp.sum(d_es[(b, hh)], axis=-1, keepdims=True)
            dvs = [_tn(ws[(b, 0)].astype(BF16), doms[0]) + _tn(ws[(b, 1)].astype(BF16), doms[1]) for b in range(nb)]
            dl1ms = {ch: jnp.dot(d_es[ch].astype(BF16), tri_lt, preferred_element_type=F32) + c_de[ch] for ch in chains}
            dzs = {ch: d_es[ch] * jnp.exp(l1m_all[ch]) - dl1ms[ch] * jnp.exp(lbs[ch]) for ch in chains}
            dzs = {ch: jnp.where(valid, dzs[ch], 0.0) if diagonal[ch[0]] else dzs[ch] for ch in chains}
            dzs = {ch: dzs[ch].astype(BF16) for ch in chains}
            for ch in chains:
                dq = dq + jnp.dot(dzs[ch], kks[ch], preferred_element_type=F32)
            for b in range(nb):
                dk_ref[pl.ds(offs[b], ATT_BLK), :] += _tn(dzs[(b, 0)], qms[0]) + _tn(dzs[(b, 1)], qms[1])
                dv_ref[pl.ds(offs[b], ATT_BLK), :] += dvs[b]
            return (dq, pre[(nb, 0)], c_de[(nb, 0)], pre[(nb, 1)], c_de[(nb, 1)])

        zero = jnp.zeros((ATT_BLK, 1), F32)
        carry = lax.fori_loop(0, qi // 2, lambda pr, cr: blocks([2 * pr, 2 * pr + 1], cr, (False, False)),
                              (jnp.zeros((ATT_BLK, LANES), F32), zero, zero, zero, zero))
        carry = lax.cond(qi % 2 == 1, lambda cr: blocks([qi - 1, qi], cr, (False, True)),
                         lambda cr: blocks([qi], cr, (True,)), carry)
        dq_ref[...] = carry[0] * scale

    return pl.pallas_call(
        body, name="sb_attn_bwd", grid=(SB_HEADS // 2, nq),
        in_specs=[pl.BlockSpec((ATT_BLK, LANES), lambda p, i: (i, p)),
                  pl.BlockSpec((t, LANES), lambda p, i: (0, p)),
                  pl.BlockSpec((t, LANES), lambda p, i: (0, p)),
                  pl.BlockSpec((ATT_BLK, 2 * LANES), lambda p, i: (i, p)),
                  pl.BlockSpec((ATT_BLK, LANES), lambda p, i: (i, p))],
        out_specs=[pl.BlockSpec((ATT_BLK, LANES), lambda p, i: (i, p)),
                   pl.BlockSpec((t, LANES), lambda p, i: (0, p)),
                   pl.BlockSpec((t, LANES), lambda p, i: (0, p))],
        out_shape=[jax.ShapeDtypeStruct((t, SB_WIDTH), F32)] * 3,
        compiler_params=pltpu.CompilerParams(dimension_semantics=("arbitrary", "arbitrary")),
    )(q, k, v, tot, do)


@jax.custom_vjp
def _sb_attention(q, k, v):
    return _sb_fwd(q, k, v)[0]


def _sb_attention_fwd(q, k, v):
    o, tot = _sb_fwd(q, k, v)
    return o, (q, k, v, tot)


def _sb_attention_bwd(res, do):
    return tuple(_sb_bwd(*res, do))


_sb_attention.defvjp(_sb_attention_fwd, _sb_attention_bwd)


def _mla_fwd(qn, qr, kn, kr, v):
    t = qn.shape[0]
    nq = t // ATT_BLK
    scale = MLA_QK ** -0.5

    def body(qn_ref, qr_ref, kn_ref, kr_ref, v_ref, o_ref, lse_ref):
        qi = pl.program_id(1)
        lanes = [slice(hh * LANES, (hh + 1) * LANES) for hh in range(2)]
        qnb = [qn_ref[:, sl].astype(BF16) for sl in lanes]
        qrb = [qr_ref[:, sl].astype(BF16) for sl in lanes]

        def blocks(kbs, carry, diagonal):
            nb = len(kbs)
            chains = [(b, hh) for b in range(nb) for hh in range(2)]
            offs = [pl.multiple_of(kb * ATT_BLK, ATT_BLK) for kb in kbs]
            krbs = [kr_ref[pl.ds(off, ATT_BLK), :].astype(BF16) for off in offs]
            accs, ms, ls = [carry[0], carry[3]], [carry[1], carry[4]], [carry[2], carry[5]]
            ss = {(b, hh): (_nt(qnb[hh], kn_ref[pl.ds(offs[b], ATT_BLK), lanes[hh]].astype(BF16))
                            + _nt(qrb[hh], krbs[b])) * scale for b, hh in chains}
            if any(diagonal):
                causal = (lax.broadcasted_iota(jnp.int32, (ATT_BLK, ATT_BLK), 1)
                          <= lax.broadcasted_iota(jnp.int32, (ATT_BLK, ATT_BLK), 0))
                ss = {ch: jnp.where(causal, ss[ch], -jnp.inf) if diagonal[ch[0]] else ss[ch] for ch in chains}
            m_new = list(ms)
            for b, hh in chains:
                m_new[hh] = jnp.maximum(m_new[hh], jnp.max(ss[(b, hh)], axis=-1, keepdims=True))
            ps = {(b, hh): jnp.exp(ss[(b, hh)] - m_new[hh]) for b, hh in chains}
            alphas = [jnp.exp(ms[hh] - m_new[hh]) for hh in range(2)]
            pvs = {(b, hh): jnp.dot(ps[(b, hh)].astype(BF16), v_ref[pl.ds(offs[b], ATT_BLK), lanes[hh]].astype(BF16),
                                    preferred_element_type=F32) for b, hh in chains}
            out = []
            for hh in range(2):
                acc, l = accs[hh] * alphas[hh], ls[hh] * alphas[hh]
                for b in range(nb):
                    acc, l = acc + pvs[(b, hh)], l + jnp.sum(ps[(b, hh)], axis=-1, keepdims=True)
                out += [acc, m_new[hh], l]
            return tuple(out)

        init = (jnp.zeros((ATT_BLK, LANES), F32), jnp.full((ATT_BLK, 1), -jnp.inf, F32), jnp.zeros((ATT_BLK, 1), F32))
        carry = lax.cond(qi % 2 == 1, lambda cr: blocks([qi, qi - 1], cr, (True, False)),
                         lambda cr: blocks([qi], cr, (True,)), init + init)
        carry = lax.fori_loop(0, qi // 2, lambda pr, cr: blocks([2 * pr, 2 * pr + 1], cr, (False, False)), carry)
        for hh in range(2):
            acc, m, l = carry[3 * hh:3 * hh + 3]
            o_ref[:, lanes[hh]] = acc / l
            lse_ref[:, lanes[hh]] = jnp.broadcast_to(m + jnp.log(l), (ATT_BLK, LANES))

    blk = pl.BlockSpec((ATT_BLK, 2 * LANES), lambda p, i: (i, p))
    full = pl.BlockSpec((t, 2 * LANES), lambda p, i: (0, p))
    return pl.pallas_call(
        body, name="mla_attn_fwd", grid=(MLA_HEADS // 2, nq),
        in_specs=[blk, blk, full, pl.BlockSpec((t, LANES), lambda p, i: (0, 0)), full],
        out_specs=[blk, blk],
        out_shape=[jax.ShapeDtypeStruct((t, MLA_HEADS * LANES), F32)] * 2,
        compiler_params=pltpu.CompilerParams(dimension_semantics=("arbitrary", "arbitrary")),
    )(qn, qr, kn, kr, v)


def _mla_bwd(qn, qr, kn, kr, v, o, lse, do):
    t = qn.shape[0]
    nq = t // ATT_BLK
    scale = MLA_QK ** -0.5

    def body(qn_ref, qr_ref, kn_ref, kr_ref, v_ref, o_ref, lse_ref, do_ref,
             dqn_ref, dqr_ref, dkn_ref, dkr_ref, dv_ref):
        pair = pl.program_id(0)
        qi = pl.program_id(1)

        @pl.when(qi == 0)
        def _():
            dkn_ref[...] = jnp.zeros_like(dkn_ref)
            dv_ref[...] = jnp.zeros_like(dv_ref)

        @pl.when((qi == 0) & (pair == 0))
        def _():
            dkr_ref[...] = jnp.zeros_like(dkr_ref)

        lanes = [slice(hh * LANES, (hh + 1) * LANES) for hh in range(2)]
        qnb = [qn_ref[:, sl].astype(BF16) for sl in lanes]
        qrb = [qr_ref[:, sl].astype(BF16) for sl in lanes]
        dob = [do_ref[:, sl].astype(BF16) for sl in lanes]
        delta = [jnp.sum(do_ref[:, sl] * o_ref[:, sl], axis=-1, keepdims=True) for sl in lanes]
        lse_v = [lse_ref[:, hh * LANES:hh * LANES + 1] for hh in range(2)]

        def blocks(kbs, carry, diagonal):
            nb = len(kbs)
            chains = [(b, hh) for b in range(nb) for hh in range(2)]
            offs = [pl.multiple_of(kb * ATT_BLK, ATT_BLK) for kb in kbs]
            krbs = [kr_ref[pl.ds(off, ATT_BLK), :].astype(BF16) for off in offs]
            knb = {(b, hh): kn_ref[pl.ds(offs[b], ATT_BLK), lanes[hh]].astype(BF16) for b, hh in chains}
            vb = {(b, hh): v_ref[pl.ds(offs[b], ATT_BLK), lanes[hh]].astype(BF16) for b, hh in chains}
            ss = {(b, hh): _nt(qnb[hh], knb[(b, hh)]) + _nt(qrb[hh], krbs[b]) for b, hh in chains}
            dps = {(b, hh): _nt(dob[hh], vb[(b, hh)]) for b, hh in chains}
            ps = {(b, hh): jnp.exp(ss[(b, hh)] * scale - lse_v[hh]) for b, hh in chains}
            if any(diagonal):
                causal = (lax.broadcasted_iota(jnp.int32, (ATT_BLK, ATT_BLK), 1)
                          <= lax.broadcasted_iota(jnp.int32, (ATT_BLK, ATT_BLK), 0))
                ps = {ch: jnp.where(causal, ps[ch], 0.0) if diagonal[ch[0]] else ps[ch] for ch in chains}
            dss = {(b, hh): (ps[(b, hh)] * (dps[(b, hh)] - delta[hh]) * scale).astype(BF16) for b, hh in chains}
            for b, hh in chains:
                dv_ref[pl.ds(offs[b], ATT_BLK), lanes[hh]] += _tn(ps[(b, hh)].astype(BF16), dob[hh])
            for b, hh in chains:
                dkn_ref[pl.ds(offs[b], ATT_BLK), lanes[hh]] += _tn(dss[(b, hh)], qnb[hh])
            for b in range(nb):
                dkr_ref[pl.ds(offs[b], ATT_BLK), :] += _tn(dss[(b, 0)], qrb[0]) + _tn(dss[(b, 1)], qrb[1])
            out = list(carry)
            for b, hh in chains:
                out[2 * hh] = out[2 * hh] + jnp.dot(dss[(b, hh)], knb[(b, hh)], preferred_element_type=F32)
                out[2 * hh + 1] = out[2 * hh + 1] + jnp.dot(dss[(b, hh)], krbs[b], preferred_element_type=F32)
            return tuple(out)

        zero = jnp.zeros((ATT_BLK, LANES), F32)
        carry = lax.fori_loop(0, qi // 2, lambda pr, cr: blocks([2 * pr, 2 * pr + 1], cr, (False, False)),
                              (zero, zero, zero, zero))
        carry = lax.cond(qi % 2 == 1, lambda cr: blocks([qi - 1, qi], cr, (False, True)),
                         lambda cr: blocks([qi], cr, (True,)), carry)
        for hh in range(2):
            dqn_ref[:, lanes[hh]] = carry[2 * hh]
            dqr_ref[:, lanes[hh]] = carry[2 * hh + 1]

    blk = pl.BlockSpec((ATT_BLK, 2 * LANES), lambda p, i: (i, p))
    full = pl.BlockSpec((t, 2 * LANES), lambda p, i: (0, p))
    shared = pl.BlockSpec((t, LANES), lambda p, i: (0, 0))
    wide = jax.ShapeDtypeStruct((t, MLA_HEADS * LANES), F32)
    return pl.pallas_call(
        body, name="mla_attn_bwd", grid=(MLA_HEADS // 2, nq),
        in_specs=[blk, blk, full, shared, full, blk, blk, blk],
        out_specs=[blk, blk, full, shared, full],
        out_shape=[wide, wide, wide, jax.ShapeDtypeStruct((t, LANES), F32), wide],
        compiler_params=pltpu.CompilerParams(dimension_semantics=("arbitrary", "arbitrary")),
    )(qn, qr, kn, kr, v, o, lse, do)


@jax.custom_vjp
def _mla_attention(qn, qr, kn, kr, v):
    return _mla_fwd(qn, qr, kn, kr, v)[0]


def _mla_attention_fwd(qn, qr, kn, kr, v):
    o, lse = _mla_fwd(qn, qr, kn, kr, v)
    return o, (qn, qr, kn, kr, v, o, lse)


def _mla_attention_bwd(res, do):
    return tuple(_mla_bwd(*res, do))


_mla_attention.defvjp(_mla_attention_fwd, _mla_attention_bwd)


def _ffn_in(h, wg, wu):
    t, k = h.shape
    n_sh, _, cc = wg.shape

    def body(h_ref, wg_ref, wu_ref, g_ref, u_ref, a_ref):
        hb = h_ref[...].astype(BF16)
        for j in range(n_sh):
            cols = slice(j * cc, (j + 1) * cc)
            g = jnp.dot(hb, wg_ref[j], preferred_element_type=F32)
            u = jnp.dot(hb, wu_ref[j], preferred_element_type=F32)
            g_ref[:, cols] = g.astype(BF16)
            u_ref[:, cols] = u.astype(BF16)
            a_ref[:, cols] = _f_swiglu(g, u)[0].astype(BF16)

    w_spec = pl.BlockSpec((n_sh, k, cc), lambda i: (0, 0, 0))
    o_spec = pl.BlockSpec((MM_ROW_TILE, n_sh * cc), lambda i: (i, 0))
    wide = jax.ShapeDtypeStruct((t, n_sh * cc), BF16)
    return pl.pallas_call(
        body, name="ffn_in_fwd", grid=(t // MM_ROW_TILE,),
        in_specs=[pl.BlockSpec((MM_ROW_TILE, k), lambda i: (i, 0)), w_spec, w_spec],
        out_specs=[o_spec, o_spec, o_spec],
        out_shape=[wide, wide, wide],
        compiler_params=pltpu.CompilerParams(dimension_semantics=("arbitrary",), vmem_limit_bytes=MM_VMEM_LIMIT),
    )(h, wg, wu)


def _ffn_mid_bwd(dy, wd, g, u):
    t, n = dy.shape
    n_sh, cc, _ = wd.shape

    def body(dy_ref, wd_ref, g_ref, u_ref, dg_ref, du_ref):
        d_act = _nt(dy_ref[...].astype(BF16), wd_ref[...])
        g = g_ref[...].astype(F32)
        sig = 1.0 / (1.0 + jnp.exp(-g))
        dg_ref[...] = (d_act * u_ref[...].astype(F32) * (sig * (1.0 + g * (1.0 - sig)))).astype(BF16)
        du_ref[...] = (d_act * (g * sig)).astype(BF16)

    blk = pl.BlockSpec((MM_ROW_TILE, cc), lambda j, i: (i, j))
    wide = jax.ShapeDtypeStruct((t, n_sh * cc), BF16)
    return pl.pallas_call(
        body, name="ffn_mid_bwd", grid=(n_sh, t // MM_ROW_TILE),
        in_specs=[pl.BlockSpec((MM_ROW_TILE, n), lambda j, i: (i, 0)),
                  pl.BlockSpec((None, cc, n), lambda j, i: (j, 0, 0)), blk, blk],
        out_specs=[blk, blk], out_shape=[wide, wide],
        compiler_params=pltpu.CompilerParams(dimension_semantics=("arbitrary", "arbitrary"),
                                             vmem_limit_bytes=MM_VMEM_LIMIT),
    )(dy, wd, g, u)


def _ffn_dh(dg, du, wg, wu):
    t = dg.shape[0]
    n_sh, k, cc = wg.shape

    def body(dg_ref, du_ref, wg_ref, wu_ref, o_ref):
        acc = jnp.zeros((MM_ROW_TILE, k), F32)
        for j in range(n_sh):
            cols = slice(j * cc, (j + 1) * cc)
            acc = acc + _nt(dg_ref[:, cols], wg_ref[j]) + _nt(du_ref[:, cols], wu_ref[j])
        o_ref[...] = acc

    blk = pl.BlockSpec((MM_ROW_TILE, n_sh * cc), lambda i: (i, 0))
    w_spec = pl.BlockSpec((n_sh, k, cc), lambda i: (0, 0, 0))
    return pl.pallas_call(
        body, name="ffn_dh", grid=(t // MM_ROW_TILE,),
        in_specs=[blk, blk, w_spec, w_spec],
        out_specs=pl.BlockSpec((MM_ROW_TILE, k), lambda i: (i, 0)),
        out_shape=jax.ShapeDtypeStruct((t, k), F32),
        compiler_params=pltpu.CompilerParams(dimension_semantics=("arbitrary",), vmem_limit_bytes=MM_VMEM_LIMIT),
    )(dg, du, wg, wu)


def _ffn_dw_in(h, dy, n_sh, name):
    t, k = h.shape
    cc = dy.shape[1] // n_sh
    tk = 512

    def body(h_ref, dy_ref, o_ref):
        o_ref[...] = _tn(h_ref[...].astype(BF16), dy_ref[...]).astype(BF16)

    return pl.pallas_call(
        body, name=name, grid=(n_sh, k // tk),
        in_specs=[pl.BlockSpec((t, tk), lambda j, i: (0, i)), pl.BlockSpec((t, cc), lambda j, i: (0, j))],
        out_specs=pl.BlockSpec((None, tk, cc), lambda j, i: (j, i, 0)),
        out_shape=jax.ShapeDtypeStruct((n_sh, k, cc), BF16),
        compiler_params=pltpu.CompilerParams(dimension_semantics=("arbitrary", "arbitrary"),
                                             vmem_limit_bytes=MM_VMEM_LIMIT),
    )(h, dy)


@jax.custom_vjp
def _ffn_block(h, wg, wu, wd):
    act = _ffn_in(h, wg, wu)[2]
    return _mm(act, wd.reshape(-1, wd.shape[2]), "nn", "ffn_down_fwd", MM_ROW_TILE, wd.shape[2])


def _ffn_block_fwd(h, wg, wu, wd):
    g, u, act = _ffn_in(h, wg, wu)
    y = _mm(act, wd.reshape(-1, wd.shape[2]), "nn", "ffn_down_fwd", MM_ROW_TILE, wd.shape[2])
    return y, (h, wg, wu, wd, g, u, act)


def _ffn_block_bwd(res, dy):
    h, wg, wu, wd, g, u, act = res
    dg, du = _ffn_mid_bwd(dy, wd, g, u)
    dh = _ffn_dh(dg, du, wg, wu)
    n_sh = wg.shape[0]
    dwg = _ffn_dw_in(h, dg, n_sh, "ffn_gate_dw")
    dwu = _ffn_dw_in(h, du, n_sh, "ffn_up_dw")
    dwd = _mm(act, dy, "tn", "ffn_down_dw", 256, wd.shape[2], out_dtype=BF16).reshape(wd.shape)
    return dh, dwg, dwu, dwd


_ffn_block.defvjp(_ffn_block_fwd, _ffn_block_bwd)


def _swap_halves(w):
    half = w.shape[-1] // 2
    return jnp.concatenate([w[..., half:], w[..., :half]], axis=-1)


def _pad_lanes(w):
    return jnp.concatenate([w, jnp.zeros(w.shape[:-1] + (LANES - w.shape[-1],), w.dtype)], axis=-1)


def _join_cols(shards):
    return shards.transpose(1, 0, 2).reshape(shards.shape[1], -1)


def _mod_parts(mod):
    return [mod[:, i * D_MODEL:(i + 1) * D_MODEL] for i in range(N_MOD)]


def _mixing_stage(x, mod, p, cos, sin):
    shift1, scale1 = _mod_parts(mod)[:2]

    w_in_t = p["w_in"].reshape(-1, D_MODEL)
    k_rope_rows = w_in_t[2176:2240]

    def pad_rows(a):
        return jnp.concatenate([a, jnp.zeros((LANES - a.shape[0], D_MODEL), a.dtype)], axis=0)

    swapped = jnp.concatenate([k_rope_rows[MLA_ROPE // 2:], k_rope_rows[:MLA_ROPE // 2]], axis=0)
    w_in_ext = jnp.concatenate([w_in_t[:2176], pad_rows(k_rope_rows), pad_rows(swapped),
                                jnp.zeros((LANES, D_MODEL), w_in_t.dtype)], axis=0)
    h1, x_res = _make_rowwise("pre_attn", _f_pre_attn, 1, 3, [D_MODEL, D_MODEL], [True], out_dtypes=[BF16, F32])(
        x, p["norm_attn"], scale1, shift1)
    q_sb, k_sb, v_sb, cq, ckv, kr, kr_sw = _make_linear_split_t(
        "in_proj", (SB_WIDTH, SB_WIDTH, SB_WIDTH, MLA_Q_RANK, MLA_KV_RANK, LANES, LANES), 512)(h1, w_in_ext)

    o_sb = _sb_attention(q_sb, k_sb, v_sb)

    wq = _join_cols(p["w_q_up"]).reshape(MLA_Q_RANK, MLA_HEADS, MLA_QK)
    wq_n, wq_r = wq[:, :, :MLA_NOPE], wq[:, :, MLA_NOPE:]
    w_q_ext = jnp.concatenate([wq_n.reshape(MLA_Q_RANK, -1), _pad_lanes(wq_r).reshape(MLA_Q_RANK, -1),
                               _pad_lanes(_swap_halves(wq_r)).reshape(MLA_Q_RANK, -1)], axis=1)
    wkv = _join_cols(p["w_kv_up"]).reshape(MLA_KV_RANK, MLA_HEADS, MLA_NOPE + MLA_V)
    w_kv_ext = jnp.concatenate([wkv[:, :, :MLA_NOPE].reshape(MLA_KV_RANK, -1),
                                wkv[:, :, MLA_NOPE:].reshape(MLA_KV_RANK, -1)], axis=1)
    cqn, ckvn = _make_rowwise("mla_a", _f_mla_a, 2, 2, [MLA_Q_RANK, MLA_KV_RANK], [True, True],
                              out_dtypes=[BF16, BF16], grad_dtypes=[BF16, BF16])(
        cq, ckv, p["q_a_norm"], p["kv_a_norm"])
    qall = _make_linear("q_up", 384, 768)(cqn, w_q_ext)
    kn_all, v_mla = _make_linear_split("kv_up", (MLA_HEADS * MLA_NOPE, MLA_HEADS * MLA_V), MLA_KV_RANK)(ckvn, w_kv_ext)
    gq = p["q_norm"]
    gkr = p["k_rope_norm"]
    qn, qr, kn, krr = _make_rowwise("mla_b", _f_mla_b, 6, 6, [512, 512, 512, LANES],
                                    [True, True, True, True, False, False],
                                    out_dtypes=[BF16] * 4, grad_dtypes=[BF16] * 4)(
        qall, kn_all, kr, kr_sw, cos, sin,
        gq[:, :MLA_NOPE], _pad_lanes(gq[:, MLA_NOPE:]), _pad_lanes(_swap_halves(gq[:, MLA_NOPE:])),
        p["k_nope_norm"], _pad_lanes(gkr), _pad_lanes(_swap_halves(gkr)))
    o_mla = _mla_attention(qn, qr, kn, krr, v_mla)

    (mixed,) = _make_rowwise("post_attn", _f_post_attn, 2, 2, [D_MODEL], [True, True])(
        o_sb, o_mla, p["out_norm_sb"], p["out_norm_mla"])
    return mixed, x_res


def _ffn_stage(x, mixed, mod, p):
    _, _, gate1, shift2, scale2, _ = _mod_parts(mod)
    attn = _make_linear("out_proj", 512, 512)(mixed, p["w_out"].reshape(D_MODEL, D_MODEL))

    x2, h2 = _make_rowwise("pre_ffn", _f_pre_ffn, 2, 4, [D_MODEL, D_MODEL], [True, True],
                           out_dtypes=[F32, BF16], grad_dtypes=[F32, BF16])(
        x, attn, gate1, p["norm_ffn"], scale2, shift2)
    return x2, _ffn_block(h2, p["w_gate"], p["w_up"], p["w_down"])


def _my_place():
    return lax.axis_index("x"), lax.axis_index("y"), lax.axis_index("c")


def _small_gather(x_ref, out_ref, send_sems, recv_sems, base, local_sem):
    m_per = x_ref.shape[0]
    x, y, c = _my_place()
    me, sibling = (x, y, c), (x, y, 1 - c)
    chips = [(1 - x, y), (x, 1 - y), (1 - x, 1 - y)]

    def rows(px, py, pc):
        return out_ref.at[pl.ds((4 * px + 2 * py + pc) * m_per, m_per), :]

    def copy(k, blk, to, src=None):
        return _remote(rows(*blk) if src is None else src, rows(*blk), send_sems, recv_sems, base + k, to)

    mine = pltpu.make_async_copy(x_ref, rows(*me), local_sem)
    first = [copy(0, me, sibling, src=x_ref)] + [copy(1 + j, me, (*chip, c), src=x_ref) for j, chip in enumerate(chips)]
    passed = [copy(4 + j, (*chip, c), sibling) for j, chip in enumerate(chips)]

    def start():
        mine.start()
        for cp in first:
            cp.start()

    def finish():
        for j, chip in enumerate(chips):
            copy(1 + j, (*chip, c), me).wait_recv()
            passed[j].start()
        copy(0, sibling, me).wait_recv()
        for j, chip in enumerate(chips):
            copy(4 + j, (*chip, 1 - c), me).wait_recv()
        for cp in first + passed:
            cp.wait_send()
        mine.wait()

    return start, finish


EARLY =("w_in", "w_q_up", "w_kv_up")
LATE = ("w_out", "w_gate", "w_up", "w_down")
BIG = EARLY + LATE
TRANSPOSED_UPDATE = ("w_in", "w_gate", "w_up")
HALF_AXIS = {"w_in": 1, "w_q_up": 0, "w_kv_up": 0, "w_out": 0, "w_gate": 0, "w_up": 0, "w_down": 1}
TRAVELS_TRANSPOSED = ("w_in",)


def _half(ref, h, axis, lead=()):
    trail = ref.shape[len(lead):]
    idx = list(lead) + [slice(None)] * len(trail)
    at = len(trail) - 2 + axis
    n2 = trail[at] // 2
    idx[len(lead) + at] = pl.ds(h * n2, n2)
    return ref.at[tuple(idx)]


def _half_shape(shape, axis):
    shape = list(shape)
    shape[len(shape) - 2 + axis] //= 2
    return tuple(shape)


def _remote(src, dst, send_sems, recv_sems, k, to):
    return pltpu.make_async_remote_copy(src_ref=src, dst_ref=dst, send_sem=send_sems.at[k],
                                        recv_sem=recv_sems.at[k], device_id=to, device_id_type=MESH)


def _gather_weights(names, shards, small_block):
    n_w = len(shards)
    axes = [HALF_AXIS[n] for n in names]

    def body(*refs):
        w_refs, small_ref = refs[:n_w], refs[n_w]
        out_refs, token, small_out = refs[n_w + 1:2 * n_w + 1], refs[2 * n_w + 1], refs[2 * n_w + 2]
        send_sems, recv_sems, local_sems = refs[2 * n_w + 3:]
        token[...] = jnp.zeros_like(token)
        x, y, c = _my_place()
        sibling = (x, y, 1 - c)
        chips = [(1 - x, y), (x, 1 - y), (1 - x, 1 - y)]
        me = 2 * x + y
        small_start, small_finish = _small_gather(small_ref, small_out, send_sems, recv_sems, 6 * n_w,
                                                  local_sems.at[n_w])
        small_start()
        mine = [pltpu.make_async_copy(w, o.at[me], local_sems.at[i]) for i, (w, o) in enumerate(zip(w_refs, out_refs))]
        for cp in mine:
            cp.start()
        first = [_remote(_half(w_refs[i], c, axes[i]), _half(out_refs[i], c, axes[i], (me,)),
                         send_sems, recv_sems, 6 * i + j, (*chip, c))
                 for i in range(n_w) for j, chip in enumerate(chips)]
        for cp in first:
            cp.start()
        small_finish()
        passed = []
        for j, (cx, cy) in enumerate(chips):
            for i in range(n_w):
                blk = _half(out_refs[i], c, axes[i], (2 * cx + cy,))
                _remote(blk, blk, send_sems, recv_sems, 6 * i + j, (cx, cy, c)).wait_recv()
                cp = _remote(blk, blk, send_sems, recv_sems, 6 * i + 3 + j, sibling)
                cp.start()
                passed.append(cp)
        for j, (cx, cy) in enumerate(chips):
            for i in range(n_w):
                blk = _half(out_refs[i], 1 - c, axes[i], (2 * cx + cy,))
                _remote(blk, blk, send_sems, recv_sems, 6 * i + 3 + j, sibling).wait_recv()
        for cp in first + passed:
            cp.wait_send()
        for cp in mine:
            cp.wait()

    outs = pl.pallas_call(
        body, name="gather_weights",
        out_shape=[jax.ShapeDtypeStruct((N_CHIPS,) + s.shape, s.dtype) for s in shards]
        + [jax.ShapeDtypeStruct((8, LANES), F32),
           jax.ShapeDtypeStruct((N_DEV * small_block.shape[0], small_block.shape[1]), small_block.dtype)],
        in_specs=[ANY] * (n_w + 1), out_specs=[ANY] * n_w + [pl.BlockSpec(memory_space=pltpu.VMEM), ANY],
        scratch_shapes=[pltpu.SemaphoreType.DMA((6 * n_w + 7,)), pltpu.SemaphoreType.DMA((6 * n_w + 7,)),
                        pltpu.SemaphoreType.DMA((n_w + 1,))],
    )(*shards, small_block)
    return outs[:n_w], outs[n_w], outs[n_w + 1]


def _pair_exchange(names, grads, call_name, small_block):
    n_w = len(grads)
    axes = [HALF_AXIS[n] for n in names]

    def body(*refs):
        g_refs, small_ref = refs[:n_w], refs[n_w]
        t_refs, small_out = refs[n_w + 1:2 * n_w + 1], refs[2 * n_w + 1]
        send_sems, recv_sems, local_sem = refs[2 * n_w + 2:]
        x, y, c = _my_place()
        small_start, small_finish = _small_gather(small_ref, small_out, send_sems, recv_sems, n_w, local_sem)
        small_start()
        sends = [_remote(_half(g_refs[i], 1 - c, axes[i]), t_refs[i], send_sems, recv_sems, i, (x, y, 1 - c))
                 for i in range(n_w)]
        for cp in sends:
            cp.start()
        small_finish()
        for cp in sends:
            cp.wait_recv()
        for cp in sends:
            cp.wait_send()

    outs = pl.pallas_call(
        body, name=call_name,
        out_shape=[jax.ShapeDtypeStruct(_half_shape(g.shape, a), g.dtype) for g, a in zip(grads, axes)]
        + [jax.ShapeDtypeStruct((N_DEV * small_block.shape[0], small_block.shape[1]), small_block.dtype)],
        in_specs=[ANY] * (n_w + 1), out_specs=[ANY] * (n_w + 1),
        scratch_shapes=[pltpu.SemaphoreType.DMA((n_w + 7,)), pltpu.SemaphoreType.DMA((n_w + 7,)),
                        pltpu.SemaphoreType.DMA],
    )(*grads, small_block)
    return outs[:n_w], outs[n_w]


def _sibling_join(halves, name, after):
    n_w = len(halves)

    def body(*refs):
        s_refs, j_refs = refs[:n_w], refs[n_w + 1:2 * n_w + 1]
        send_sems, recv_sems = refs[2 * n_w + 1:]
        x, y, c = _my_place()
        sends = [_remote(s_refs[i], j_refs[i], send_sems, recv_sems, i, (x, y, 1 - c)) for i in range(n_w)]
        for cp in sends:
            cp.start()
        for cp in sends:
            cp.wait_recv()
        for cp in sends:
            cp.wait_send()

    return pl.pallas_call(
        body, name=name,
        out_shape=[jax.ShapeDtypeStruct(s.shape, s.dtype) for s in halves],
        in_specs=[ANY] * (n_w + 1), out_specs=[ANY] * n_w,
        scratch_shapes=[pltpu.SemaphoreType.DMA((n_w,)), pltpu.SemaphoreType.DMA((n_w,))],
    )(*halves, after)


HBM_SPEC = pl.BlockSpec(memory_space=pltpu.HBM)
SEM_SPEC = pl.BlockSpec(memory_space=pltpu.SEMAPHORE)
DATAFLOW = pltpu.SideEffectType.DATAFLOW_SIDE_EFFECTING


def _in_hbm(a):
    return pltpu.with_memory_space_constraint(a, pltpu.HBM)


def _exchange_start(name, srcs, lands, plan, n_copies, after, thru):
    n = len(srcs)

    def body(*refs):
        src_refs, land_refs = refs[:n], refs[n:2 * n]
        send_sems, recv_sems = refs[2 * n + 2], refs[2 * n + 3]
        for k, (src, dst, to, k_recv) in enumerate(plan(src_refs, land_refs)):
            pltpu.make_async_remote_copy(src_ref=src, dst_ref=dst, send_sem=send_sems.at[k],
                                         recv_sem=recv_sems.at[k_recv], device_id=to, device_id_type=MESH).start()

    outs = pl.pallas_call(
        body, name=name,
        out_shape=(pltpu.SemaphoreType.DMA((n_copies,)), pltpu.SemaphoreType.DMA((n_copies,)),
                   *[pltpu.HBM(a.shape, a.dtype) for a in list(srcs) + list(lands) + [thru]]),
        in_specs=[HBM_SPEC] * (2 * n + 1) + [ANY],
        out_specs=(SEM_SPEC, SEM_SPEC, *[HBM_SPEC] * (2 * n + 1)),
        input_output_aliases={i: 2 + i for i in range(2 * n + 1)},
        compiler_params=pltpu.CompilerParams(has_side_effects=DATAFLOW),
    )(*[_in_hbm(a) for a in list(srcs) + list(lands) + [thru]], after)
    return outs[0], outs[1], outs[2:2 + n], outs[2 + n:2 + 2 * n], outs[2 + 2 * n]


def _exchange_wait(name, started, plan, after):
    send_sems, recv_sems, srcs, lands, _ = started
    n = len(srcs)

    def body(*refs):
        src_refs, land_refs = refs[:n], refs[n:2 * n]
        s_sems, r_sems = refs[2 * n], refs[2 * n + 1]
        for k, (src, dst, to, _) in enumerate(plan(src_refs, land_refs)):
            cp = _remote(src, dst, s_sems, r_sems, k, to)
            cp.wait_send()
            cp.wait_recv()

    outs = pl.pallas_call(
        body, name=name,
        out_shape=tuple(pltpu.HBM(a.shape, a.dtype) for a in list(srcs) + list(lands)),
        in_specs=[HBM_SPEC] * (2 * n) + [SEM_SPEC, SEM_SPEC, ANY],
        out_specs=tuple([HBM_SPEC] * (2 * n)),
        input_output_aliases={i: i for i in range(2 * n)},
        compiler_params=pltpu.CompilerParams(has_side_effects=DATAFLOW),
    )(*srcs, *lands, send_sems, recv_sems, after)
    return outs[:n], outs[n:]


def _late_gather_plan(src_refs, land_refs):
    x, y, c = _my_place()
    chips = [(1 - x, y), (x, 1 - y), (1 - x, 1 - y)]
    plan = [(src, land.at[2 * x + y], (cx, cy, c)) for src, land in zip(src_refs, land_refs) for cx, cy in chips]
    return [entry + (k,) for k, entry in enumerate(plan)]


def _late_scatter_plan(src_refs, land_refs):
    x, y, c = _my_place()
    chips = [(1 - x, y), (x, 1 - y), (1 - x, 1 - y)]
    plan = [(src.at[2 * cx + cy], land.at[j], (cx, cy, c))
            for src, land in zip(src_refs, land_refs) for j, (cx, cy) in enumerate(chips)]
    return [entry + (k,) for k, entry in enumerate(plan)]


def _direct_scatter_plan(names):
    axes = [HALF_AXIS[n] for n in names]

    def plan(src_refs, land_refs):
        x, y, c = _my_place()
        chips = [(1 - x, y), (x, 1 - y), (1 - x, 1 - y)]
        out = []
        for i, (src, land) in enumerate(zip(src_refs, land_refs)):
            for f, (cx, cy) in enumerate(chips):
                for core in range(2):
                    out.append((_half(src, core, axes[i], (2 * cx + cy,)), land.at[2 * f + c], (cx, cy, core),
                                7 * i + 2 * f + c))
            out.append((_half(src, 1 - c, axes[i], (2 * x + y,)), land.at[6], (x, y, 1 - c), 7 * i + 6))
        return out

    return plan


def _row_tile(rows, mult=16, limit=ROW_TILE):
    return max(d for d in range(mult, limit + 1, mult) if rows % d == 0)


def _pair_sum(place, g, theirs, axis, name):
    nj, rr, cc = theirs.shape
    tr = _row_tile(rr, limit=1024)
    nb = rr // tr
    if axis == 0:
        g_map = lambda j, i, pr: (j, pr[0] * nb + i, 0)
    else:
        g_map = lambda j, i, pr: (j, i, pr[0])

    def body(pr, g_ref, t_ref, o_ref):
        o_ref[...] = (g_ref[...].astype(F32) + t_ref[...].astype(F32)).astype(BF16)

    spec = pl.BlockSpec((None, tr, cc), lambda j, i, pr: (j, i, 0))
    return pl.pallas_call(
        body, name=name,
        grid_spec=pltpu.PrefetchScalarGridSpec(
            num_scalar_prefetch=1, grid=(nj, nb),
            in_specs=[pl.BlockSpec((None, tr, cc), g_map), spec], out_specs=spec),
        out_shape=jax.ShapeDtypeStruct(theirs.shape, BF16))(place, g, theirs)


def _chip_sum(place, pair_sums, parts, name, transposed):
    _, rr, cc = parts.shape
    tr = _row_tile(rr, LANES) if transposed else _row_tile(rr, limit=1024)

    def body(pr, h_ref, p_ref, o_ref):
        acc = p_ref[0].astype(F32)
        for j in range(1, N_CHIPS - 1):
            acc = acc + p_ref[j].astype(F32)
        acc = acc + h_ref[...].astype(F32)
        o_ref[...] = (acc.T if transposed else acc).astype(BF16)

    out_spec = pl.BlockSpec((cc, tr), lambda i, pr: (0, i)) if transposed else pl.BlockSpec((tr, cc), lambda i, pr: (i, 0))
    return pl.pallas_call(
        body, name=name,
        grid_spec=pltpu.PrefetchScalarGridSpec(
            num_scalar_prefetch=1, grid=(rr // tr,),
            in_specs=[pl.BlockSpec((None, tr, cc), lambda i, pr: (pr[1], i, 0)),
                      pl.BlockSpec((N_CHIPS - 1, tr, cc), lambda i, pr: (0, i, 0))],
            out_specs=out_spec),
        out_shape=jax.ShapeDtypeStruct((cc, rr) if transposed else (rr, cc), BF16))(place, pair_sums, parts)


def _chip_sum_direct(place, g, parts, axis, name, transposed):
    n_parts, rr, cc = parts.shape
    tr = _row_tile(rr, LANES) if transposed else _row_tile(rr, limit=1024)
    nb = rr // tr
    if axis == 0:
        g_map = lambda i, pr: (pr[1], pr[0] * nb + i, 0)
    else:
        g_map = lambda i, pr: (pr[1], i, pr[0])

    def body(pr, g_ref, p_ref, o_ref):
        acc = p_ref[0].astype(F32)
        for j in range(1, n_parts):
            acc = acc + p_ref[j].astype(F32)
        acc = acc + g_ref[...].astype(F32)
        o_ref[...] = (acc.T if transposed else acc).astype(BF16)

    out_spec = pl.BlockSpec((cc, tr), lambda i, pr: (0, i)) if transposed else pl.BlockSpec((tr, cc), lambda i, pr: (i, 0))
    return pl.pallas_call(
        body, name=name,
        grid_spec=pltpu.PrefetchScalarGridSpec(
            num_scalar_prefetch=1, grid=(nb,),
            in_specs=[pl.BlockSpec((None, tr, cc), g_map), pl.BlockSpec((n_parts, tr, cc), lambda i, pr: (0, i, 0))],
            out_specs=out_spec),
        out_shape=jax.ShapeDtypeStruct((cc, rr) if transposed else (rr, cc), BF16))(place, g, parts)


def _silu(v):
    return v / (1.0 + jnp.exp(-v))


def _ada_fwd(c_all, w_shard, b_shard):
    n_seq, n_cols = c_all.shape[0], w_shard.shape[1]

    def body(c_ref, w_ref, b_ref, o_ref, mine_ref, send_sems, recv_sems, local_sem):
        mine_ref[...] = jnp.dot(_silu(c_ref[...]), w_ref[...], precision=lax.Precision.HIGHEST,
                                preferred_element_type=F32) + b_ref[...]
        start, finish = _small_gather(mine_ref, o_ref, send_sems, recv_sems, 0, local_sem)
        start()
        finish()

    return pl.pallas_call(
        body, name="ada_fwd", out_shape=jax.ShapeDtypeStruct((N_DEV * n_seq, n_cols), F32),
        scratch_shapes=[pltpu.VMEM((n_seq, n_cols), F32), pltpu.SemaphoreType.DMA((7,)), pltpu.SemaphoreType.DMA((7,)),
                        pltpu.SemaphoreType.DMA],
        compiler_params=pltpu.CompilerParams(vmem_limit_bytes=MM_VMEM_LIMIT))(c_all, w_shard, b_shard)


def _loss_and_grads(x2, ffn, target, gate):
    t, d = x2.shape

    def half_loss(x2_blk, ffn_blk, gate_row, target_blk):
        return 0.5 * _f_loss(x2_blk, ffn_blk, target_blk, gate_row)[0]

    def body(x2_ref, ffn_ref, tgt_ref, gate_ref, loss_ref, dx2_ref, dffn_ref, dgate_ref):
        rows, vjp = jax.vjp(lambda a, b, g: half_loss(a, b, g, tgt_ref[...]), x2_ref[...], ffn_ref[...], gate_ref[...])
        loss_ref[...] = rows
        dx2_ref[...], dffn_ref[...], dgate = vjp(jnp.ones_like(rows))

        @pl.when(pl.program_id(0) == 0)
        def _():
            dgate_ref[...] = jnp.zeros_like(dgate_ref)

        dgate_ref[...] += dgate

    blk = pl.BlockSpec((ROW_TILE, d), lambda i: (i, 0))
    row = pl.BlockSpec((1, d), lambda i: (0, 0))
    return pl.pallas_call(
        body, name="loss_and_grads", grid=(t // ROW_TILE,),
        in_specs=[blk, blk, blk, row],
        out_specs=[pl.BlockSpec((ROW_TILE, 1), lambda i: (i, 0)), blk, blk, row],
        out_shape=[jax.ShapeDtypeStruct((t, 1), F32), jax.ShapeDtypeStruct((t, d), F32), jax.ShapeDtypeStruct((t, d), F32),
                   jax.ShapeDtypeStruct((1, d), F32)],
        compiler_params=pltpu.CompilerParams(dimension_semantics=("arbitrary",), vmem_limit_bytes=MM_VMEM_LIMIT),
    )(x2, ffn, target, gate)


def _ada_bwd(c_all, dmod_cols):
    def body(c_ref, d_ref, o_ref):
        o_ref[...] = lax.dot_general(_silu(c_ref[...]), d_ref[...], (((0,), (0,)), ((), ())),
                                     precision=lax.Precision.HIGHEST, preferred_element_type=F32)

    return pl.pallas_call(body, name="ada_bwd", out_shape=jax.ShapeDtypeStruct((c_all.shape[1], dmod_cols.shape[1]), F32),
                          compiler_params=pltpu.CompilerParams(vmem_limit_bytes=MM_VMEM_LIMIT))(c_all, dmod_cols)


def _adamw_math(w, g, m, v):
    m = ADAM_B1 * m + (1.0 - ADAM_B1) * g
    v = ADAM_B2 * v + (1.0 - ADAM_B2) * (g * g)
    m_hat = m / (1.0 - ADAM_B1 ** ADAM_STEP)
    v_hat = v / (1.0 - ADAM_B2 ** ADAM_STEP)
    delta = -ADAM_LR * (m_hat / (jnp.sqrt(v_hat) + ADAM_EPS) + ADAM_WD * w)
    return delta, m, v


def _adamw(w, g, m, v, name):
    r, ccols = w.shape
    tr = max(d for d in range(8, ROW_TILE + 1, 8) if r % d == 0)
    spec = pl.BlockSpec((tr, ccols), lambda i: (i, 0))

    def body(w_ref, g_ref, m_ref, v_ref, d_ref, nm_ref, nv_ref):
        d_ref[...], nm_ref[...], nv_ref[...] = _adamw_math(w_ref[...], g_ref[...], m_ref[...], v_ref[...])

    return pl.pallas_call(body, name=name, grid=(r // tr,), in_specs=[spec] * 4, out_specs=[spec] * 3,
                          out_shape=[jax.ShapeDtypeStruct(w.shape, F32)] * 3,
                          compiler_params=pltpu.CompilerParams(vmem_limit_bytes=MM_VMEM_LIMIT))(w, g, m, v)


def _small_layout(sizes):
    offs, off = [], 0
    for n in sizes:
        offs.append(off)
        off += -(-n // LANES) * LANES
    total = -(-(off + LANES) // (8 * LANES)) * (8 * LANES)
    return offs, off, total


def _adamw_small(ws, g_all, ms, vs, offs, loss_off):
    n_p = len(ws)

    def device_sum(g_ref, off, width):
        blk = g_ref[:, off:off + width]
        acc = blk[0:1]
        for d in range(1, N_DEV):
            acc = acc + blk[d:d + 1]
        return acc

    def body(*refs):
        w_refs, m_refs, v_refs = refs[:n_p], refs[n_p:2 * n_p], refs[2 * n_p:3 * n_p]
        g_ref = refs[3 * n_p]
        outs = refs[3 * n_p + 1:]
        for i in range(n_p):
            n = w_refs[i].shape[1]
            g = device_sum(g_ref, offs[i], -(-n // LANES) * LANES)[:, :n]
            outs[i][...] = g
            outs[n_p + i][...], outs[2 * n_p + i][...], outs[3 * n_p + i][...] = _adamw_math(
                w_refs[i][...], g, m_refs[i][...], v_refs[i][...])
        outs[4 * n_p][...] = device_sum(g_ref, loss_off, LANES)

    res = pl.pallas_call(
        body, name="adamw_small",
        out_shape=[jax.ShapeDtypeStruct(a.shape, F32) for a in list(ws) * 4] + [jax.ShapeDtypeStruct((1, LANES), F32)],
    )(*ws, *ms, *vs, g_all)
    return res[:n_p], res[n_p:2 * n_p], res[2 * n_p:3 * n_p], res[3 * n_p:4 * n_p], res[4 * n_p]


def _adamw_halves(place, w, own, sib, m, v, axis, name, after):
    r, cc = w.shape
    if axis == 0:
        rows, gc = own.shape[0], own.shape[1]
        tr = _row_tile(rows)
        nb = rows // tr
        w_spec = pl.BlockSpec((tr, cc), lambda h, i, pr: (h * nb + i, 0))
        g_spec = pl.BlockSpec((tr, gc), lambda h, i, pr: (i, 0))
    else:
        tr = _row_tile(r)
        nb = r // tr
        gc = own.shape[1]
        w_spec = pl.BlockSpec((tr, gc), lambda h, i, pr: (i, h))
        g_spec = pl.BlockSpec((tr, gc), lambda h, i, pr: (i, 0))
    wc = w_spec.block_shape[1]

    def body(pr, w_ref, o_ref, s_ref, m_ref, v_ref, after_ref, g_ref, d_ref, nm_ref, nv_ref):
        g = jnp.where(pl.program_id(0) == pr[0], o_ref[...], s_ref[...]).astype(F32)[:, :wc]
        g_ref[...] = g
        d_ref[...], nm_ref[...], nv_ref[...] = _adamw_math(w_ref[...], g, m_ref[...], v_ref[...])

    return pl.pallas_call(
        body, name=name,
        grid_spec=pltpu.PrefetchScalarGridSpec(
            num_scalar_prefetch=1, grid=(2, nb),
            in_specs=[w_spec, g_spec, g_spec, w_spec, w_spec, ANY], out_specs=[w_spec] * 4),
        out_shape=[jax.ShapeDtypeStruct(w.shape, F32)] * 4,
        compiler_params=pltpu.CompilerParams(vmem_limit_bytes=MM_VMEM_LIMIT))(place, w, own, sib, m, v, after)


SMALL = ("b_ada", "norm_attn", "norm_ffn", "q_a_norm", "kv_a_norm", "q_norm", "k_nope_norm", "k_rope_norm",
         "out_norm_sb", "out_norm_mla")
WEIGHTS = ("w_ada", "b_ada", "norm_attn", "norm_ffn", "w_in", "q_a_norm", "w_q_up", "kv_a_norm", "w_kv_up",
           "q_norm", "k_nope_norm", "k_rope_norm", "out_norm_sb", "out_norm_mla", "w_out", "w_gate", "w_up",
           "w_down")


def kernel(x, c, positions, w_ada, b_ada, norm_attn, norm_ffn, w_in, q_a_norm, w_q_up, kv_a_norm, w_kv_up, q_norm, k_nope_norm, k_rope_norm, out_norm_sb, out_norm_mla, w_out, w_gate, w_up, w_down, loss_target, m_w_ada, m_b_ada, m_norm_attn, m_norm_ffn, m_w_in, m_q_a_norm, m_w_q_up, m_kv_a_norm, m_w_kv_up, m_q_norm, m_k_nope_norm, m_k_rope_norm, m_out_norm_sb, m_out_norm_mla, m_w_out, m_w_gate, m_w_up, m_w_down, v_w_ada, v_b_ada, v_norm_attn, v_norm_ffn, v_w_in, v_q_a_norm, v_w_q_up, v_kv_a_norm, v_w_kv_up, v_q_norm, v_k_nope_norm, v_k_rope_norm, v_out_norm_sb, v_out_norm_mla, v_w_out, v_w_gate, v_w_up, v_w_down):
    local = dict(locals())
    w = {n: local[n][0] for n in WEIGHTS}
    m = {n: local["m_" + n][0] for n in WEIGHTS}
    v = {n: local["v_" + n][0] for n in WEIGHTS}
    small = {n: w[n].reshape(1, -1) for n in SMALL}
    ix, iy, ic = _my_place()
    chip = 2 * ix + iy
    dev = 2 * chip + ic
    xs, target = x[0], loss_target[0]
    seq = xs.shape[0]

    ff_pad = FF_SHARD_PAD - FF_SHARD
    pads = {"w_gate": ((0, 0), (0, ff_pad)), "w_up": ((0, 0), (0, ff_pad)), "w_down": ((0, ff_pad), (0, 0))}
    shards = {n: jnp.pad(w[n].astype(BF16), pads[n]) if n in pads else w[n].astype(BF16) for n in BIG}
    shards.update({n: w[n].T.astype(BF16) for n in TRAVELS_TRANSPOSED})
    early, early_done, c_gathered = _gather_weights(EARLY, [shards[n] for n in EARLY], c.reshape(8, LANES))
    gathered = dict(zip(EARLY, early))

    c_all = c_gathered.reshape(N_DEV, D_MODEL)
    ada_cols = w["w_ada"].shape[1]
    b_cols = lax.dynamic_slice_in_dim(small["b_ada"], chip * ada_cols, ada_cols, axis=1)
    mod_all = _ada_fwd(c_all, w["w_ada"], b_cols).reshape(N_CHIPS, 2, N_DEV, ada_cols)
    mod = lax.dynamic_index_in_dim(mod_all[:, 0], dev, axis=1, keepdims=False).reshape(1, N_MOD * D_MODEL)

    lands = [lax.dynamic_update_index_in_dim(lax.empty((N_CHIPS,) + shards[n].shape, BF16), shards[n], chip, 0)
             for n in LATE]
    late_gather = _exchange_start("gather_late_start", [shards[n] for n in LATE], lands, _late_gather_plan,
                                  3 * len(LATE), early_done, mod)
    mod = late_gather[4]

    half = MLA_ROPE // 2
    freqs = 1.0 / (ROPE_THETA ** (np.arange(half, dtype=np.float32) / half))
    zeros = np.zeros(LANES - MLA_ROPE, np.float32)
    freqs_row = jnp.asarray(np.concatenate([freqs, freqs, zeros]).astype(np.float32)[None])
    sign_row = jnp.asarray(np.concatenate([-np.ones(half), np.ones(half), zeros]).astype(np.float32)[None])
    cos, sin = _rope_tables(positions.reshape(seq, 1), freqs_row, sign_row)

    place = jnp.stack([ic, chip]).astype(jnp.int32)
    small_params = {n: small[n] for n in SMALL if n != "b_ada"}

    p1 = {**{n: gathered[n] for n in EARLY}, **small_params}
    (mixed, x_res), mixing_vjp = jax.vjp(lambda x_, mod_, p_: _mixing_stage(x_, mod_, p_, cos, sin), xs, mod, p1)
    _, landed = _exchange_wait("gather_late_wait", late_gather, _late_gather_plan, mixed)
    p2 = {**dict(zip(LATE, landed)), **small_params}
    (x2, ffn), ffn_vjp = jax.vjp(_ffn_stage, x_res, mixed, mod, p2)
    loss_rows, g_x2, g_ffn, g_gate2 = _loss_and_grads(x2, ffn, target, _mod_parts(mod)[5])
    loss_part = jnp.sum(loss_rows)
    gx2, gmixed, gmod2, gp2 = ffn_vjp((g_x2, g_ffn))
    gmod2 = gmod2 + jnp.concatenate([jnp.zeros((1, (N_MOD - 1) * D_MODEL), F32), g_gate2], axis=1)
    late_grads = [gp2[n] for n in LATE]
    late_plan = _direct_scatter_plan(LATE)
    late_scatter = _exchange_start(
        "grad_scatter_late_start", late_grads,
        [lax.empty((7,) + _half_shape(gr.shape[1:], HALF_AXIS[n]), BF16) for n, gr in zip(LATE, late_grads)],
        late_plan, 7 * len(LATE), gx2, gmixed)
    gx, gmod1, gp1 = mixing_vjp((late_scatter[4], gx2))
    gmod = gmod1 + gmod2
    gp = {n: gp1[n] + gp2[n] for n in small_params}

    sizes = [w[n].size for n in SMALL]
    offs, loss_off, n_small = _small_layout(sizes)
    pieces = []
    for n, size in zip(SMALL, sizes):
        pieces.append(gmod if n == "b_ada" else gp[n])
        if size % LANES:
            pieces.append(jnp.zeros((1, LANES - size % LANES), F32))
    pieces += [jnp.full((1, LANES), loss_part), jnp.zeros((1, n_small - loss_off - LANES), F32)]
    small_vec = jnp.concatenate(pieces, axis=1)

    g, delta, new_m, new_v = {}, {}, {}, {}

    def update(names, own, sib, after):
        for n, o, s in zip(names, own, sib):
            if n in TRANSPOSED_UPDATE:
                res = _adamw_halves(place, w[n].T, o, s, m[n].T, v[n].T, 1, "adamw_" + n, after)
                g[n], delta[n], new_m[n], new_v[n] = [r.T for r in res]
            else:
                g[n], delta[n], new_m[n], new_v[n] = _adamw_halves(place, w[n], o, s, m[n], v[n], HALF_AXIS[n],
                                                                   "adamw_" + n, after)

    late_grads, late_parts = _exchange_wait("grad_scatter_late_wait", late_scatter, late_plan, gx)
    own_late = [_chip_sum_direct(place, gr, pt, HALF_AXIS[n], "grad_chip_sum_" + n, n in TRANSPOSED_UPDATE)
                for n, gr, pt in zip(LATE, late_grads, late_parts)]
    early_grads = [gp1[n] for n in EARLY]
    theirs, small_gathered = _pair_exchange(EARLY, early_grads, "grad_pair_exchange_early",
                                            small_vec.reshape(8, n_small // 8))
    small_all = small_gathered.reshape(N_DEV, n_small)
    sib_late = _sibling_join(own_late, "grad_sibling_join_late", small_all)
    early_sums = [_pair_sum(place, gr, th, HALF_AXIS[n], "grad_pair_sum_" + n)
                  for n, gr, th in zip(EARLY, early_grads, theirs)]
    early_scatter = _exchange_start(
        "grad_scatter_early_start", early_sums,
        [lax.empty((N_CHIPS - 1,) + s.shape[1:], BF16) for s in early_sums], _late_scatter_plan, 3 * len(EARLY),
        sib_late[0], small_all)
    small_all = early_scatter[4]
    update(LATE, own_late, sib_late, small_all)

    *small_out, loss_row = _adamw_small([small[n] for n in SMALL], small_all, [m[n].reshape(1, -1) for n in SMALL],
                                        [v[n].reshape(1, -1) for n in SMALL], offs, loss_off)
    loss = loss_row[0, 0]
    for d, outs_d in zip((g, delta, new_m, new_v), small_out):
        d.update({n: o.reshape(w[n].shape) for n, o in zip(SMALL, outs_d)})

    dmod_all = small_all[:, :N_MOD * D_MODEL]
    g["w_ada"] = _ada_bwd(c_all, lax.dynamic_slice_in_dim(dmod_all, chip * ada_cols, ada_cols, axis=1))
    delta["w_ada"], new_m["w_ada"], new_v["w_ada"] = _adamw(w["w_ada"], g["w_ada"], m["w_ada"], v["w_ada"], "adamw_w_ada")

    early_sums, early_parts = _exchange_wait("grad_scatter_early_wait", early_scatter, _late_scatter_plan,
                                             delta["w_ada"])
    own_early = [_chip_sum(place, ps, pt, "grad_chip_sum_" + n, n in TRANSPOSED_UPDATE and n not in TRAVELS_TRANSPOSED)
                 for n, ps, pt in zip(EARLY, early_sums, early_parts)]
    sib_early = _sibling_join(own_early, "grad_sibling_join_early", delta["w_ada"])
    update(EARLY, own_early, sib_early, sib_early[0])

    def outs(d):
        return [d[n][None] for n in WEIGHTS]

    return (loss, gx[None], *outs(g), *outs(delta), *outs(new_m), *outs(new_v))
```

```python
import numpy as np
import jax
import jax.numpy as jnp
from jax import lax
from jax.experimental import pallas as pl
from jax.experimental.pallas import tpu as pltpu

F32 = jnp.float32
BF16 = jnp.bfloat16
MESH = pl.DeviceIdType.MESH
ANY = pl.BlockSpec(memory_space=pl.ANY)

D_MODEL = 1024
SB_HEADS = 8
SB_HEAD_DIM = 64
SB_WIDTH = 512
MLA_HEADS = 4
MLA_NOPE = 128
MLA_ROPE = 64
MLA_QK = 192
MLA_V = 128
MLA_Q_RANK = 384
MLA_KV_RANK = 256
D_FF = 2816
N_MOD = 6
ROPE_THETA = 10000.0
EPS = 1e-6
LANES = 128

ADAM_LR = 0.001
ADAM_B1 = 0.9
ADAM_B2 = 0.999
ADAM_EPS = 1e-08
ADAM_WD = 0.01
ADAM_STEP = 10

N_CHIPS = 4
N_DEV = 8
ROW_TILE = 512
MM_ROW_TILE = 512
ATT_BLK = 256
MM_VMEM_LIMIT = 56 * 1024 * 1024
FF_SHARD = D_FF // N_CHIPS
FF_SHARD_PAD = 768


def _mm(a, b, mode, name, tm, tn, out_dtype=F32):
    if mode == "nn":
        (m, k), n = a.shape, b.shape[1]
        a_spec = pl.BlockSpec((tm, k), lambda j, i: (i, 0))
        b_spec = pl.BlockSpec((k, tn), lambda j, i: (0, j))
        dims = (((1,), (0,)), ((), ()))
    elif mode == "nt":
        (m, k), n = a.shape, b.shape[0]
        a_spec = pl.BlockSpec((tm, k), lambda j, i: (i, 0))
        b_spec = pl.BlockSpec((tn, k), lambda j, i: (j, 0))
        dims = (((1,), (1,)), ((), ()))
    else:
        (k, m), n = a.shape, b.shape[1]
        a_spec = pl.BlockSpec((k, tm), lambda j, i: (0, i))
        b_spec = pl.BlockSpec((k, tn), lambda j, i: (0, j))
        dims = (((0,), (0,)), ((), ()))
    assert m % tm == 0 and n % tn == 0, (name, m, n, tm, tn)

    def body(a_ref, b_ref, o_ref):
        o_ref[...] = lax.dot_general(a_ref[...].astype(BF16), b_ref[...].astype(BF16), dims,
                                     preferred_element_type=F32).astype(out_dtype)

    return pl.pallas_call(
        body, name=name, grid=(n // tn, m // tm),
        in_specs=[a_spec, b_spec],
        out_specs=pl.BlockSpec((tm, tn), lambda j, i: (i, j)),
        out_shape=jax.ShapeDtypeStruct((m, n), out_dtype),
        compiler_params=pltpu.CompilerParams(dimension_semantics=("arbitrary", "arbitrary"),
                                             vmem_limit_bytes=MM_VMEM_LIMIT),
    )(a, b)


def _make_linear(name, tk_w, tn_w):
    @jax.custom_vjp
    def op(a, w):
        return _mm(a, w, "nn", name + "_fwd", MM_ROW_TILE, w.shape[1])

    def fwd(a, w):
        return op(a, w), (a, w)

    def bwd(res, dy):
        a, w = res
        da = _mm(dy, w, "nt", name + "_dx", MM_ROW_TILE, w.shape[0])
        dw = _mm(a, dy, "tn", name + "_dw", tk_w, tn_w, out_dtype=BF16)
        return da, dw

    op.defvjp(fwd, bwd)
    return op


def _make_linear_split_t(name, widths, tk_w):
    starts = [sum(widths[:g]) for g in range(len(widths))]

    def call_fwd(a, wt):
        t, k = a.shape
        n = wt.shape[0]

        def body(a_ref, w_ref, *o_refs):
            y = _nt(a_ref[...].astype(BF16), w_ref[...])
            for o_ref, s0, wd in zip(o_refs, starts, widths):
                o_ref[...] = y[:, s0:s0 + wd]

        return pl.pallas_call(
            body, name=name + "_fwd", grid=(t // MM_ROW_TILE,),
            in_specs=[pl.BlockSpec((MM_ROW_TILE, k), lambda i: (i, 0)), pl.BlockSpec((n, k), lambda i: (0, 0))],
            out_specs=[pl.BlockSpec((MM_ROW_TILE, wd), lambda i: (i, 0)) for wd in widths],
            out_shape=[jax.ShapeDtypeStruct((t, wd), F32) for wd in widths],
            compiler_params=pltpu.CompilerParams(dimension_semantics=("arbitrary",), vmem_limit_bytes=MM_VMEM_LIMIT),
        )(a, wt)

    def call_dx(dys, wt):
        t = dys[0].shape[0]
        n, k = wt.shape

        def body(*refs):
            dy_refs, w_ref, o_ref = refs[:-2], refs[-2], refs[-1]
            acc = jnp.zeros((MM_ROW_TILE, k), F32)
            for dy_ref, s0, wd in zip(dy_refs, starts, widths):
                acc = acc + jnp.dot(dy_ref[...].astype(BF16), w_ref[s0:s0 + wd, :], preferred_element_type=F32)
            o_ref[...] = acc

        return pl.pallas_call(
            body, name=name + "_dx", grid=(t // MM_ROW_TILE,),
            in_specs=[pl.BlockSpec((MM_ROW_TILE, wd), lambda i: (i, 0)) for wd in widths]
            + [pl.BlockSpec((n, k), lambda i: (0, 0))],
            out_specs=pl.BlockSpec((MM_ROW_TILE, k), lambda i: (i, 0)),
            out_shape=jax.ShapeDtypeStruct((t, k), F32),
            compiler_params=pltpu.CompilerParams(dimension_semantics=("arbitrary",), vmem_limit_bytes=MM_VMEM_LIMIT),
        )(*dys, wt)

    def call_dw(a, dys, wt):
        t, k = a.shape
        n = wt.shape[0]

        def body(a_ref, *refs):
            dy_refs, o_ref = refs[:-1], refs[-1]
            ab = a_ref[...].astype(BF16)
            for dy_ref, s0, wd in zip(dy_refs, starts, widths):
                o_ref[s0:s0 + wd, :] = _tn(dy_ref[...].astype(BF16), ab).astype(BF16)
            if starts[-1] + widths[-1] < n:
                o_ref[starts[-1] + widths[-1]:, :] = jnp.zeros((n - starts[-1] - widths[-1], tk_w), BF16)

        return pl.pallas_call(
            body, name=name + "_dw", grid=(k // tk_w,),
            in_specs=[pl.BlockSpec((t, tk_w), lambda i: (0, i))]
            + [pl.BlockSpec((t, wd), lambda i: (0, 0)) for wd in widths],
            out_specs=pl.BlockSpec((n, tk_w), lambda i: (0, i)),
            out_shape=jax.ShapeDtypeStruct((n, k), BF16),
            compiler_params=pltpu.CompilerParams(dimension_semantics=("arbitrary",), vmem_limit_bytes=MM_VMEM_LIMIT),
        )(a, *dys)

    @jax.custom_vjp
    def op(a, wt):
        return tuple(call_fwd(a, wt))

    def fwd(a, wt):
        return op(a, wt), (a, wt)

    def bwd(res, dys):
        a, wt = res
        return call_dx(dys, wt), call_dw(a, dys, wt)

    op.defvjp(fwd, bwd)
    return op


def _make_linear_split(name, widths, tk_w):
    starts = [sum(widths[:g]) for g in range(len(widths))]

    def call_fwd(a, w):
        t, k = a.shape
        n = w.shape[1]

        def body(a_ref, w_ref, *o_refs):
            y = jnp.dot(a_ref[...].astype(BF16), w_ref[...], preferred_element_type=F32)
            for o_ref, s0, wd in zip(o_refs, starts, widths):
                o_ref[...] = y[:, s0:s0 + wd]

        return pl.pallas_call(
            body, name=name + "_fwd", grid=(t // MM_ROW_TILE,),
            in_specs=[pl.BlockSpec((MM_ROW_TILE, k), lambda i: (i, 0)), pl.BlockSpec((k, n), lambda i: (0, 0))],
            out_specs=[pl.BlockSpec((MM_ROW_TILE, wd), lambda i: (i, 0)) for wd in widths],
            out_shape=[jax.ShapeDtypeStruct((t, wd), F32) for wd in widths],
            compiler_params=pltpu.CompilerParams(dimension_semantics=("arbitrary",), vmem_limit_bytes=MM_VMEM_LIMIT),
        )(a, w)

    def call_dx(dys, w):
        t = dys[0].shape[0]
        k, n = w.shape

        def body(*refs):
            dy_refs, w_ref, o_ref = refs[:-2], refs[-2], refs[-1]
            acc = jnp.zeros((MM_ROW_TILE, k), F32)
            for dy_ref, s0, wd in zip(dy_refs, starts, widths):
                acc = acc + _nt(dy_ref[...].astype(BF16), w_ref[:, s0:s0 + wd])
            o_ref[...] = acc

        return pl.pallas_call(
            body, name=name + "_dx", grid=(t // MM_ROW_TILE,),
            in_specs=[pl.BlockSpec((MM_ROW_TILE, wd), lambda i: (i, 0)) for wd in widths]
            + [pl.BlockSpec((k, n), lambda i: (0, 0))],
            out_specs=pl.BlockSpec((MM_ROW_TILE, k), lambda i: (i, 0)),
            out_shape=jax.ShapeDtypeStruct((t, k), F32),
            compiler_params=pltpu.CompilerParams(dimension_semantics=("arbitrary",), vmem_limit_bytes=MM_VMEM_LIMIT),
        )(*dys, w)

    def call_dw(a, dys, w):
        t, k = a.shape
        n = w.shape[1]

        def body(a_ref, *refs):
            dy_refs, o_ref = refs[:-1], refs[-1]
            ab = a_ref[...].astype(BF16)
            for dy_ref, s0, wd in zip(dy_refs, starts, widths):
                o_ref[:, s0:s0 + wd] = _tn(ab, dy_ref[...].astype(BF16)).astype(BF16)
            if starts[-1] + widths[-1] < n:
                o_ref[:, starts[-1] + widths[-1]:] = jnp.zeros((tk_w, n - starts[-1] - widths[-1]), BF16)

        return pl.pallas_call(
            body, name=name + "_dw", grid=(k // tk_w,),
            in_specs=[pl.BlockSpec((t, tk_w), lambda i: (0, i))]
            + [pl.BlockSpec((t, wd), lambda i: (0, 0)) for wd in widths],
            out_specs=pl.BlockSpec((tk_w, n), lambda i: (i, 0)),
            out_shape=jax.ShapeDtypeStruct((k, n), BF16),
            compiler_params=pltpu.CompilerParams(dimension_semantics=("arbitrary",), vmem_limit_bytes=MM_VMEM_LIMIT),
        )(a, *dys)

    @jax.custom_vjp
    def op(a, w):
        return tuple(call_fwd(a, w))

    def fwd(a, w):
        return op(a, w), (a, w)

    def bwd(res, dys):
        a, w = res
        return call_dx(dys, w), call_dw(a, dys, w)

    op.defvjp(fwd, bwd)
    return op


def _row_spec(arr, tb):
    return pl.BlockSpec((tb, arr.shape[1]), lambda i: (i, 0))


def _full_spec(arr):
    return pl.BlockSpec(arr.shape, lambda i: (0, 0))


def _make_rowwise(name, f, n_rows, n_params, out_cols, diff_rows, out_dtypes=None, grad_dtypes=None):
    n_out = len(out_cols)
    out_dtypes = out_dtypes or [F32] * n_out
    grad_dtypes = grad_dtypes or [F32] * sum(diff_rows)

    def call_fwd(rows, params):
        t = rows[0].shape[0]

        def body(*refs):
            ins = [r[...] for r in refs[:n_rows + n_params]]
            outs = f(*ins)
            for o_ref, o in zip(refs[n_rows + n_params:], outs):
                o_ref[...] = o.astype(o_ref.dtype)

        return pl.pallas_call(
            body, name=name + "_fwd", grid=(t // ROW_TILE,),
            in_specs=[_row_spec(a, ROW_TILE) for a in rows] + [_full_spec(p) for p in params],
            out_specs=[pl.BlockSpec((ROW_TILE, n), lambda i: (i, 0)) for n in out_cols],
            out_shape=[jax.ShapeDtypeStruct((t, n), dt) for n, dt in zip(out_cols, out_dtypes)],
            compiler_params=pltpu.CompilerParams(dimension_semantics=("arbitrary",),
                                                 vmem_limit_bytes=MM_VMEM_LIMIT),
        )(*rows, *params)

    def call_bwd(rows, params, cts):
        t = rows[0].shape[0]
        d_rows = [a for a, d in zip(rows, diff_rows) if d]
        n_in = n_rows + n_params + n_out

        def body(*refs):
            ins = [r[...] for r in refs[:n_rows + n_params]]
            ct = tuple(r[...].astype(F32) for r in refs[n_rows + n_params:n_in])
            _, vjp = jax.vjp(f, *ins)
            grads = vjp(ct)
            out_refs = refs[n_in:]
            g_rows = [g for g, d in zip(grads[:n_rows], diff_rows) if d]
            for o_ref, g in zip(out_refs[:len(g_rows)], g_rows):
                o_ref[...] = g.astype(o_ref.dtype)
            p_refs = out_refs[len(g_rows):]

            if p_refs:
                @pl.when(pl.program_id(0) == 0)
                def _():
                    for p_ref in p_refs:
                        p_ref[...] = jnp.zeros_like(p_ref)

                for p_ref, g in zip(p_refs, grads[n_rows:]):
                    p_ref[...] += g

        return pl.pallas_call(
            body, name=name + "_bwd", grid=(t // ROW_TILE,),
            in_specs=[_row_spec(a, ROW_TILE) for a in rows] + [_full_spec(p) for p in params]
            + [_row_spec(c, ROW_TILE) for c in cts],
            out_specs=[_row_spec(a, ROW_TILE) for a in d_rows] + [_full_spec(p) for p in params],
            out_shape=[jax.ShapeDtypeStruct(a.shape, dt) for a, dt in zip(d_rows, grad_dtypes)]
            + [jax.ShapeDtypeStruct(p.shape, F32) for p in params],
            compiler_params=pltpu.CompilerParams(dimension_semantics=("arbitrary",),
                                                 vmem_limit_bytes=MM_VMEM_LIMIT),
        )(*rows, *params, *cts)

    @jax.custom_vjp
    def op(*args):
        return tuple(call_fwd(args[:n_rows], args[n_rows:]))

    def fwd(*args):
        return op(*args), args

    def bwd(args, cts):
        rows, params = args[:n_rows], args[n_rows:]
        outs = call_bwd(rows, params, cts)
        it = iter(outs)
        g_rows = [next(it) if d else jnp.zeros_like(a) for a, d in zip(rows, diff_rows)]
        return tuple(g_rows) + tuple(it)

    op.defvjp(fwd, bwd)
    return op


def _rms(x, g, n):
    return x * lax.rsqrt(jnp.sum(x * x, axis=-1, keepdims=True) * (1.0 / n) + EPS) * g


def _f_pre_attn(x, g, scale, shift):
    return _rms(x, g, D_MODEL) * (1.0 + scale) + shift, x


def _f_mla_a(cq, ckv, gq, gkv):
    return _rms(cq, gq, MLA_Q_RANK), _rms(ckv, gkv, MLA_KV_RANK)


@jax.custom_vjp
def _split_lanes(x):
    return tuple(x[:, i * LANES:(i + 1) * LANES] for i in range(x.shape[1] // LANES))


def _split_lanes_fwd(x):
    return _split_lanes(x), None


def _split_lanes_bwd(_, cts):
    return (jnp.concatenate(cts, axis=1),)


_split_lanes.defvjp(_split_lanes_fwd, _split_lanes_bwd)


def _f_mla_b(qall, kn_all, kr, kr_sw, cos, sin, gqn, gqr, gqr_sw, gkn, gkr, gkr_sw):
    q = _split_lanes(qall)
    kn = _split_lanes(kn_all)
    qn_o, qr_o, kn_o = [], [], []
    for h in range(MLA_HEADS):
        qn, qr, qs = q[h], q[MLA_HEADS + h], q[2 * MLA_HEADS + h]
        ss = jnp.sum(qn * qn, axis=-1, keepdims=True) + jnp.sum(qr * qr, axis=-1, keepdims=True)
        rs = lax.rsqrt(ss * (1.0 / MLA_QK) + EPS)
        qn_o.append(qn * rs * gqn)
        qr_o.append((qr * rs * gqr) * cos + (qs * rs * gqr_sw) * sin)
        kn_o.append(_rms(kn[h], gkn, MLA_NOPE))
    rs = lax.rsqrt(jnp.sum(kr * kr, axis=-1, keepdims=True) * (1.0 / MLA_ROPE) + EPS)
    kr_o = (kr * rs * gkr) * cos + (kr_sw * rs * gkr_sw) * sin
    return (jnp.concatenate(qn_o, axis=1), jnp.concatenate(qr_o, axis=1), jnp.concatenate(kn_o, axis=1), kr_o)


def _f_post_attn(o_sb, o_mla, g_sb, g_mla):
    return (jnp.concatenate([_rms(o_sb, g_sb, SB_WIDTH), _rms(o_mla, g_mla, SB_WIDTH)], axis=1),)


def _f_pre_ffn(x, attn, gate, g, scale, shift):
    x2 = x + gate * attn
    return x2, _rms(x2, g, D_MODEL) * (1.0 + scale) + shift


def _f_swiglu(gt, up):
    return (gt / (1.0 + jnp.exp(-gt)) * up,)


def _f_loss(x2, ffn, target, gate):
    err = x2 + gate * ffn - target
    return (jnp.sum(err * err, axis=-1, keepdims=True) * (1.0 / D_MODEL),)


def _rope_tables(pos_col, freqs, sign):
    t = pos_col.shape[0]

    def body(p_ref, f_ref, s_ref, cos_ref, sin_ref):
        ang = p_ref[...].astype(F32) * f_ref[...]
        live = jnp.abs(s_ref[...])
        cos_ref[...] = jnp.cos(ang) * live
        sin_ref[...] = jnp.sin(ang) * s_ref[...]

    return pl.pallas_call(
        body, name="rope_tables", grid=(t // ROW_TILE,),
        in_specs=[pl.BlockSpec((ROW_TILE, 1), lambda i: (i, 0)), _full_spec(freqs), _full_spec(sign)],
        out_specs=[pl.BlockSpec((ROW_TILE, LANES), lambda i: (i, 0))] * 2,
        out_shape=[jax.ShapeDtypeStruct((t, LANES), F32)] * 2,
    )(pos_col, freqs, sign)


def _hi_lo_dot(x, tri):
    hi = x.astype(BF16)
    lo = (x - hi.astype(F32)).astype(BF16)
    return (jnp.dot(hi, tri, preferred_element_type=F32) + jnp.dot(lo, tri, preferred_element_type=F32))


def _tri(cmp):
    r = lax.broadcasted_iota(jnp.int32, (ATT_BLK, ATT_BLK), 0)
    c = lax.broadcasted_iota(jnp.int32, (ATT_BLK, ATT_BLK), 1)
    return cmp(r, c).astype(BF16)


def _nt(a, b):
    return lax.dot_general(a, b, (((1,), (1,)), ((), ())), preferred_element_type=F32)


def _tn(a, b):
    return lax.dot_general(a, b, (((0,), (0,)), ((), ())), preferred_element_type=F32)


def _sb_logs(z):
    lb = jnp.minimum(z, 0.0) - jnp.log(1.0 + jnp.exp(-jnp.abs(z)))
    return lb, lb - z


def _sb_fwd(q, k, v):
    t = q.shape[0]
    nq = t // ATT_BLK
    scale = SB_HEAD_DIM ** -0.5

    def body(q_ref, k_ref, v_ref, o_ref, tot_ref):
        qi = pl.program_id(1)
        lane = lax.broadcasted_iota(jnp.int32, (ATT_BLK, LANES), 1)
        tri = _tri(lambda r, c: r > c)
        qv = q_ref[...] * scale
        heads = [(lane // SB_HEAD_DIM) == hh for hh in range(2)]
        qms = [jnp.where(mine, qv, 0.0).astype(BF16) for mine in heads]

        def blocks(kbs, carry, diagonal):
            acc = carry[0]
            nb = len(kbs)
            chains = [(b, hh) for b in range(nb) for hh in range(2)]
            offs = [pl.multiple_of(kb * ATT_BLK, ATT_BLK) for kb in kbs]
            kks = [k_ref[pl.ds(off, ATT_BLK), :].astype(BF16) for off in offs]
            v_blks = [v_ref[pl.ds(off, ATT_BLK), :] for off in offs]
            if any(diagonal):
                valid = (lax.broadcasted_iota(jnp.int32, (ATT_BLK, ATT_BLK), 1)
                         < lax.broadcasted_iota(jnp.int32, (ATT_BLK, ATT_BLK), 0))
            zs = {ch: _nt(qms[ch[1]], kks[ch[0]]) for ch in chains}
            vvs = {(b, hh): jnp.where(heads[hh], v_blks[b], 0.0).astype(BF16) for b, hh in chains}
            logs = {ch: _sb_logs(zs[ch]) for ch in chains}
            l1ms = {ch: jnp.where(valid, logs[ch][1], 0.0) if diagonal[ch[0]] else logs[ch][1] for ch in chains}
            run = {(0, hh): carry[1 + hh] for hh in range(2)}
            for b, hh in chains:
                run[(b + 1, hh)] = run[(b, hh)] + jnp.sum(l1ms[(b, hh)], axis=-1, keepdims=True)
            afters = {ch: _hi_lo_dot(l1ms[ch], tri) for ch in chains}
            ws = {ch: jnp.exp(logs[ch][0] + (afters[ch] + run[ch])) for ch in chains}
            ws = {ch: jnp.where(valid, ws[ch], 0.0) if diagonal[ch[0]] else ws[ch] for ch in chains}
            for ch in chains:
                acc = acc + jnp.dot(ws[ch].astype(BF16), vvs[ch], preferred_element_type=F32)
            return (acc, run[(nb, 0)], run[(nb, 1)])

        zero = jnp.zeros((ATT_BLK, 1), F32)
        init = (jnp.zeros((ATT_BLK, LANES), F32), zero, zero)
        carry = lax.cond(qi % 2 == 1, lambda cr: blocks([qi, qi - 1], cr, (True, False)),
                         lambda cr: blocks([qi], cr, (True,)), init)
        top = qi - 1 - qi % 2
        carry = lax.fori_loop(0, qi // 2, lambda pr, cr: blocks([top - 2 * pr, top - 1 - 2 * pr], cr, (False, False)),
                              carry)
        o_ref[...] = carry[0]
        for hh in range(2):
            tot_ref[:, hh * LANES:(hh + 1) * LANES] = jnp.broadcast_to(carry[1 + hh], (ATT_BLK, LANES))

    return pl.pallas_call(
        body, name="sb_attn_fwd", grid=(SB_HEADS // 2, nq),
        in_specs=[pl.BlockSpec((ATT_BLK, LANES), lambda p, i: (i, p)),
                  pl.BlockSpec((t, LANES), lambda p, i: (0, p)),
                  pl.BlockSpec((t, LANES), lambda p, i: (0, p))],
        out_specs=[pl.BlockSpec((ATT_BLK, LANES), lambda p, i: (i, p)),
                   pl.BlockSpec((ATT_BLK, 2 * LANES), lambda p, i: (i, p))],
        out_shape=[jax.ShapeDtypeStruct((t, SB_WIDTH), F32), jax.ShapeDtypeStruct((t, SB_HEADS * LANES), F32)],
        compiler_params=pltpu.CompilerParams(dimension_semantics=("arbitrary", "arbitrary")),
    )(q, k, v)


def _sb_bwd(q, k, v, tot, do):
    t = q.shape[0]
    nq = t // ATT_BLK
    scale = SB_HEAD_DIM ** -0.5

    def body(q_ref, k_ref, v_ref, tot_ref, do_ref, dq_ref, dk_ref, dv_ref):
        qi = pl.program_id(1)

        @pl.when(qi == 0)
        def _():
            dk_ref[...] = jnp.zeros_like(dk_ref)
            dv_ref[...] = jnp.zeros_like(dv_ref)

        lane = lax.broadcasted_iota(jnp.int32, (ATT_BLK, LANES), 1)
        tri_incl = _tri(lambda r, c: r <= c)
        tri_lt = _tri(lambda r, c: r < c)
        qv = q_ref[...] * scale
        dov = do_ref[...]
        heads = [(lane // SB_HEAD_DIM) == hh for hh in range(2)]
        qms = [jnp.where(mine, qv, 0.0).astype(BF16) for mine in heads]
        doms = [jnp.where(mine, dov, 0.0).astype(BF16) for mine in heads]
        tots = [tot_ref[:, hh * LANES:hh * LANES + 1] for hh in range(2)]

        def blocks(kbs, carry, diagonal):
            dq = carry[0]
            nb = len(kbs)
            chains = [(b, hh) for b in range(nb) for hh in range(2)]
            offs = [pl.multiple_of(kb * ATT_BLK, ATT_BLK) for kb in kbs]
            k_blks = [k_ref[pl.ds(off, ATT_BLK), :] for off in offs]
            vvs = [v_ref[pl.ds(off, ATT_BLK), :].astype(BF16) for off in offs]
            if any(diagonal):
                valid = (lax.broadcasted_iota(jnp.int32, (ATT_BLK, ATT_BLK), 1)
                         < lax.broadcasted_iota(jnp.int32, (ATT_BLK, ATT_BLK), 0))
            kks = {(b, hh): jnp.where(heads[hh], k_blks[b], 0.0).astype(BF16) for b, hh in chains}
            zs = {ch: _nt(qms[ch[1]], kks[ch]) for ch in chains}
            dws = {ch: _nt(doms[ch[1]], vvs[ch[0]]) for ch in chains}
            logs = {ch: _sb_logs(zs[ch]) for ch in chains}
            lbs = {ch: logs[ch][0] for ch in chains}
            l1m_all = {ch: logs[ch][1] for ch in chains}
            l1ms = {ch: jnp.where(valid, l1m_all[ch], 0.0) if diagonal[ch[0]] else l1m_all[ch] for ch in chains}
            pre, c_de = {}, {}
            for hh in range(2):
                pre[(0, hh)], c_de[(0, hh)] = carry[1 + 2 * hh], carry[2 + 2 * hh]
            for b, hh in chains:
                pre[(b + 1, hh)] = pre[(b, hh)] + jnp.sum(l1ms[(b, hh)], axis=-1, keepdims=True)
            prefix = {ch: _hi_lo_dot(l1ms[ch], tri_incl) for ch in chains}
            ws = {ch: jnp.exp(lbs[ch] + (tots[ch[1]] - (prefix[ch] + pre[ch]))) for ch in chains}
            ws = {ch: jnp.where(valid, ws[ch], 0.0) if diagonal[ch[0]] else ws[ch] for ch in chains}
            d_es = {ch: ws[ch] * dws[ch] for ch in chains}
            for b, hh in chains:
                c_de[(b + 1, hh)] = c_de[(b, hh)] + jnp.sum(d_es[(b, hh)], axis=-1, keepdims=True)
            dvs = [_tn(ws[(b, 0)].astype(BF16), doms[0]) + _tn(ws[(b, 1)].astype(BF16), doms[1]) for b in range(nb)]
            dl1ms = {ch: jnp.dot(d_es[ch].astype(BF16), tri_lt, preferred_element_type=F32) + c_de[ch] for ch in chains}
            dzs = {ch: d_es[ch] * jnp.exp(l1m_all[ch]) - dl1ms[ch] * jnp.exp(lbs[ch]) for ch in chains}
            dzs = {ch: jnp.where(valid, dzs[ch], 0.0) if diagonal[ch[0]] else dzs[ch] for ch in chains}
            dzs = {ch: dzs[ch].astype(BF16) for ch in chains}
            for ch in chains:
                dq = dq + jnp.dot(dzs[ch], kks[ch], preferred_element_type=F32)
            for b in range(nb):
                dk_ref[pl.ds(offs[b], ATT_BLK), :] += _tn(dzs[(b, 0)], qms[0]) + _tn(dzs[(b, 1)], qms[1])
                dv_ref[pl.ds(offs[b], ATT_BLK), :] += dvs[b]
            return (dq, pre[(nb, 0)], c_de[(nb, 0)], pre[(nb, 1)], c_de[(nb, 1)])

        zero = jnp.zeros((ATT_BLK, 1), F32)
        carry = lax.fori_loop(0, qi // 2, lambda pr, cr: blocks([2 * pr, 2 * pr + 1], cr, (False, False)),
                              (jnp.zeros((ATT_BLK, LANES), F32), zero, zero, zero, zero))
        carry = lax.cond(qi % 2 == 1, lambda cr: blocks([qi - 1, qi], cr, (False, True)),
                         lambda cr: blocks([qi], cr, (True,)), carry)
        dq_ref[...] = carry[0] * scale

    return pl.pallas_call(
        body, name="sb_attn_bwd", grid=(SB_HEADS // 2, nq),
        in_specs=[pl.BlockSpec((ATT_BLK, LANES), lambda p, i: (i, p)),
                  pl.BlockSpec((t, LANES), lambda p, i: (0, p)),
                  pl.BlockSpec((t, LANES), lambda p, i: (0, p)),
                  pl.BlockSpec((ATT_BLK, 2 * LANES), lambda p, i: (i, p)),
                  pl.BlockSpec((ATT_BLK, LANES), lambda p, i: (i, p))],
        out_specs=[pl.BlockSpec((ATT_BLK, LANES), lambda p, i: (i, p)),
                   pl.BlockSpec((t, LANES), lambda p, i: (0, p)),
                   pl.BlockSpec((t, LANES), lambda p, i: (0, p))],
        out_shape=[jax.ShapeDtypeStruct((t, SB_WIDTH), F32)] * 3,
        compiler_params=pltpu.CompilerParams(dimension_semantics=("arbitrary", "arbitrary")),
    )(q, k, v, tot, do)


@jax.custom_vjp
def _sb_attention(q, k, v):
    return _sb_fwd(q, k, v)[0]


def _sb_attention_fwd(q, k, v):
    o, tot = _sb_fwd(q, k, v)
    return o, (q, k, v, tot)


def _sb_attention_bwd(res, do):
    return tuple(_sb_bwd(*res, do))


_sb_attention.defvjp(_sb_attention_fwd, _sb_attention_bwd)


def _mla_fwd(qn, qr, kn, kr, v):
    t = qn.shape[0]
    nq = t // ATT_BLK
    scale = MLA_QK ** -0.5

    def body(qn_ref, qr_ref, kn_ref, kr_ref, v_ref, o_ref, lse_ref):
        qi = pl.program_id(1)
        lanes = [slice(hh * LANES, (hh + 1) * LANES) for hh in range(2)]
        qnb = [qn_ref[:, sl].astype(BF16) for sl in lanes]
        qrb = [qr_ref[:, sl].astype(BF16) for sl in lanes]

        def blocks(kbs, carry, diagonal):
            nb = len(kbs)
            chains = [(b, hh) for b in range(nb) for hh in range(2)]
            offs = [pl.multiple_of(kb * ATT_BLK, ATT_BLK) for kb in kbs]
            krbs = [kr_ref[pl.ds(off, ATT_BLK), :].astype(BF16) for off in offs]
            accs, ms, ls = [carry[0], carry[3]], [carry[1], carry[4]], [carry[2], carry[5]]
            ss = {(b, hh): (_nt(qnb[hh], kn_ref[pl.ds(offs[b], ATT_BLK), lanes[hh]].astype(BF16))
                            + _nt(qrb[hh], krbs[b])) * scale for b, hh in chains}
            if any(diagonal):
                causal = (lax.broadcasted_iota(jnp.int32, (ATT_BLK, ATT_BLK), 1)
                          <= lax.broadcasted_iota(jnp.int32, (ATT_BLK, ATT_BLK), 0))
                ss = {ch: jnp.where(causal, ss[ch], -jnp.inf) if diagonal[ch[0]] else ss[ch] for ch in chains}
            m_new = list(ms)
            for b, hh in chains:
                m_new[hh] = jnp.maximum(m_new[hh], jnp.max(ss[(b, hh)], axis=-1, keepdims=True))
            ps = {(b, hh): jnp.exp(ss[(b, hh)] - m_new[hh]) for b, hh in chains}
            alphas = [jnp.exp(ms[hh] - m_new[hh]) for hh in range(2)]
            pvs = {(b, hh): jnp.dot(ps[(b, hh)].astype(BF16), v_ref[pl.ds(offs[b], ATT_BLK), lanes[hh]].astype(BF16),
                                    preferred_element_type=F32) for b, hh in chains}
            out = []
            for hh in range(2):
                acc, l = accs[hh] * alphas[hh], ls[hh] * alphas[hh]
                for b in range(nb):
                    acc, l = acc + pvs[(b, hh)], l + jnp.sum(ps[(b, hh)], axis=-1, keepdims=True)
                out += [acc, m_new[hh], l]
            return tuple(out)

        init = (jnp.zeros((ATT_BLK, LANES), F32), jnp.full((ATT_BLK, 1), -jnp.inf, F32), jnp.zeros((ATT_BLK, 1), F32))
        carry = lax.cond(qi % 2 == 1, lambda cr: blocks([qi, qi - 1], cr, (True, False)),
                         lambda cr: blocks([qi], cr, (True,)), init + init)
        carry = lax.fori_loop(0, qi // 2, lambda pr, cr: blocks([2 * pr, 2 * pr + 1], cr, (False, False)), carry)
        for hh in range(2):
            acc, m, l = carry[3 * hh:3 * hh + 3]
            o_ref[:, lanes[hh]] = acc / l
            lse_ref[:, lanes[hh]] = jnp.broadcast_to(m + jnp.log(l), (ATT_BLK, LANES))

    blk = pl.BlockSpec((ATT_BLK, 2 * LANES), lambda p, i: (i, p))
    full = pl.BlockSpec((t, 2 * LANES), lambda p, i: (0, p))
    return pl.pallas_call(
        body, name="mla_attn_fwd", grid=(MLA_HEADS // 2, nq),
        in_specs=[blk, blk, full, pl.BlockSpec((t, LANES), lambda p, i: (0, 0)), full],
        out_specs=[blk, blk],
        out_shape=[jax.ShapeDtypeStruct((t, MLA_HEADS * LANES), F32)] * 2,
        compiler_params=pltpu.CompilerParams(dimension_semantics=("arbitrary", "arbitrary")),
    )(qn, qr, kn, kr, v)


def _mla_bwd(qn, qr, kn, kr, v, o, lse, do):
    t = qn.shape[0]
    nq = t // ATT_BLK
    scale = MLA_QK ** -0.5

    def body(qn_ref, qr_ref, kn_ref, kr_ref, v_ref, o_ref, lse_ref, do_ref,
             dqn_ref, dqr_ref, dkn_ref, dkr_ref, dv_ref):
        pair = pl.program_id(0)
        qi = pl.program_id(1)

        @pl.when(qi == 0)
        def _():
            dkn_ref[...] = jnp.zeros_like(dkn_ref)
            dv_ref[...] = jnp.zeros_like(dv_ref)

        @pl.when((qi == 0) & (pair == 0))
        def _():
            dkr_ref[...] = jnp.zeros_like(dkr_ref)

        lanes = [slice(hh * LANES, (hh + 1) * LANES) for hh in range(2)]
        qnb = [qn_ref[:, sl].astype(BF16) for sl in lanes]
        qrb = [qr_ref[:, sl].astype(BF16) for sl in lanes]
        dob = [do_ref[:, sl].astype(BF16) for sl in lanes]
        delta = [jnp.sum(do_ref[:, sl] * o_ref[:, sl], axis=-1, keepdims=True) for sl in lanes]
        lse_v = [lse_ref[:, hh * LANES:hh * LANES + 1] for hh in range(2)]

        def blocks(kbs, carry, diagonal):
            nb = len(kbs)
            chains = [(b, hh) for b in range(nb) for hh in range(2)]
            offs = [pl.multiple_of(kb * ATT_BLK, ATT_BLK) for kb in kbs]
            krbs = [kr_ref[pl.ds(off, ATT_BLK), :].astype(BF16) for off in offs]
            knb = {(b, hh): kn_ref[pl.ds(offs[b], ATT_BLK), lanes[hh]].astype(BF16) for b, hh in chains}
            vb = {(b, hh): v_ref[pl.ds(offs[b], ATT_BLK), lanes[hh]].astype(BF16) for b, hh in chains}
            ss = {(b, hh): _nt(qnb[hh], knb[(b, hh)]) + _nt(qrb[hh], krbs[b]) for b, hh in chains}
            dps = {(b, hh): _nt(dob[hh], vb[(b, hh)]) for b, hh in chains}
            ps = {(b, hh): jnp.exp(ss[(b, hh)] * scale - lse_v[hh]) for b, hh in chains}
            if any(diagonal):
                causal = (lax.broadcasted_iota(jnp.int32, (ATT_BLK, ATT_BLK), 1)
                          <= lax.broadcasted_iota(jnp.int32, (ATT_BLK, ATT_BLK), 0))
                ps = {ch: jnp.where(causal, ps[ch], 0.0) if diagonal[ch[0]] else ps[ch] for ch in chains}
            dss = {(b, hh): (ps[(b, hh)] * (dps[(b, hh)] - delta[hh]) * scale).astype(BF16) for b, hh in chains}
            for b, hh in chains:
                dv_ref[pl.ds(offs[b], ATT_BLK), lanes[hh]] += _tn(ps[(b, hh)].astype(BF16), dob[hh])
            for b, hh in chains:
                dkn_ref[pl.ds(offs[b], ATT_BLK), lanes[hh]] += _tn(dss[(b, hh)], qnb[hh])
            for b in range(nb):
                dkr_ref[pl.ds(offs[b], ATT_BLK), :] += _tn(dss[(b, 0)], qrb[0]) + _tn(dss[(b, 1)], qrb[1])
            out = list(carry)
            for b, hh in chains:
                out[2 * hh] = out[2 * hh] + jnp.dot(dss[(b, hh)], knb[(b, hh)], preferred_element_type=F32)
                out[2 * hh + 1] = out[2 * hh + 1] + jnp.dot(dss[(b, hh)], krbs[b], preferred_element_type=F32)
            return tuple(out)

        zero = jnp.zeros((ATT_BLK, LANES), F32)
        carry = lax.fori_loop(0, qi // 2, lambda pr, cr: blocks([2 * pr, 2 * pr + 1], cr, (False, False)),
                              (zero, zero, zero, zero))
        carry = lax.cond(qi % 2 == 1, lambda cr: blocks([qi - 1, qi], cr, (False, True)),
                         lambda cr: blocks([qi], cr, (True,)), carry)
        for hh in range(2):
            dqn_ref[:, lanes[hh]] = carry[2 * hh]
            dqr_ref[:, lanes[hh]] = carry[2 * hh + 1]

    blk = pl.BlockSpec((ATT_BLK, 2 * LANES), lambda p, i: (i, p))
    full = pl.BlockSpec((t, 2 * LANES), lambda p, i: (0, p))
    shared = pl.BlockSpec((t, LANES), lambda p, i: (0, 0))
    wide = jax.ShapeDtypeStruct((t, MLA_HEADS * LANES), F32)
    return pl.pallas_call(
        body, name="mla_attn_bwd", grid=(MLA_HEADS // 2, nq),
        in_specs=[blk, blk, full, shared, full, blk, blk, blk],
        out_specs=[blk, blk, full, shared, full],
        out_shape=[wide, wide, wide, jax.ShapeDtypeStruct((t, LANES), F32), wide],
        compiler_params=pltpu.CompilerParams(dimension_semantics=("arbitrary", "arbitrary")),
    )(qn, qr, kn, kr, v, o, lse, do)


@jax.custom_vjp
def _mla_attention(qn, qr, kn, kr, v):
    return _mla_fwd(qn, qr, kn, kr, v)[0]


def _mla_attention_fwd(qn, qr, kn, kr, v):
    o, lse = _mla_fwd(qn, qr, kn, kr, v)
    return o, (qn, qr, kn, kr, v, o, lse)


def _mla_attention_bwd(res, do):
    return tuple(_mla_bwd(*res, do))


_mla_attention.defvjp(_mla_attention_fwd, _mla_attention_bwd)


def _ffn_in(h, wg, wu):
    t, k = h.shape
    n_sh, _, cc = wg.shape

    def body(h_ref, wg_ref, wu_ref, g_ref, u_ref, a_ref):
        hb = h_ref[...].astype(BF16)
        for j in range(n_sh):
            cols = slice(j * cc, (j + 1) * cc)
            g = jnp.dot(hb, wg_ref[j], preferred_element_type=F32)
            u = jnp.dot(hb, wu_ref[j], preferred_element_type=F32)
            g_ref[:, cols] = g.astype(BF16)
            u_ref[:, cols] = u.astype(BF16)
            a_ref[:, cols] = _f_swiglu(g, u)[0].astype(BF16)

    w_spec = pl.BlockSpec((n_sh, k, cc), lambda i: (0, 0, 0))
    o_spec = pl.BlockSpec((MM_ROW_TILE, n_sh * cc), lambda i: (i, 0))
    wide = jax.ShapeDtypeStruct((t, n_sh * cc), BF16)
    return pl.pallas_call(
        body, name="ffn_in_fwd", grid=(t // MM_ROW_TILE,),
        in_specs=[pl.BlockSpec((MM_ROW_TILE, k), lambda i: (i, 0)), w_spec, w_spec],
        out_specs=[o_spec, o_spec, o_spec],
        out_shape=[wide, wide, wide],
        compiler_params=pltpu.CompilerParams(dimension_semantics=("arbitrary",), vmem_limit_bytes=MM_VMEM_LIMIT),
    )(h, wg, wu)


def _ffn_mid_bwd(dy, wd, g, u):
    t, n = dy.shape
    n_sh, cc, _ = wd.shape

    def body(dy_ref, wd_ref, g_ref, u_ref, dg_ref, du_ref):
        d_act = _nt(dy_ref[...].astype(BF16), wd_ref[...])
        g = g_ref[...].astype(F32)
        sig = 1.0 / (1.0 + jnp.exp(-g))
        dg_ref[...] = (d_act * u_ref[...].astype(F32) * (sig * (1.0 + g * (1.0 - sig)))).astype(BF16)
        du_ref[...] = (d_act * (g * sig)).astype(BF16)

    blk = pl.BlockSpec((MM_ROW_TILE, cc), lambda j, i: (i, j))
    wide = jax.ShapeDtypeStruct((t, n_sh * cc), BF16)
    return pl.pallas_call(
        body, name="ffn_mid_bwd", grid=(n_sh, t // MM_ROW_TILE),
        in_specs=[pl.BlockSpec((MM_ROW_TILE, n), lambda j, i: (i, 0)),
                  pl.BlockSpec((None, cc, n), lambda j, i: (j, 0, 0)), blk, blk],
        out_specs=[blk, blk], out_shape=[wide, wide],
        compiler_params=pltpu.CompilerParams(dimension_semantics=("arbitrary", "arbitrary"),
                                             vmem_limit_bytes=MM_VMEM_LIMIT),
    )(dy, wd, g, u)


def _ffn_dh(dg, du, wg, wu):
    t = dg.shape[0]
    n_sh, k, cc = wg.shape

    def body(dg_ref, du_ref, wg_ref, wu_ref, o_ref):
        acc = jnp.zeros((MM_ROW_TILE, k), F32)
        for j in range(n_sh):
            cols = slice(j * cc, (j + 1) * cc)
            acc = acc + _nt(dg_ref[:, cols], wg_ref[j]) + _nt(du_ref[:, cols], wu_ref[j])
        o_ref[...] = acc

    blk = pl.BlockSpec((MM_ROW_TILE, n_sh * cc), lambda i: (i, 0))
    w_spec = pl.BlockSpec((n_sh, k, cc), lambda i: (0, 0, 0))
    return pl.pallas_call(
        body, name="ffn_dh", grid=(t // MM_ROW_TILE,),
        in_specs=[blk, blk, w_spec, w_spec],
        out_specs=pl.BlockSpec((MM_ROW_TILE, k), lambda i: (i, 0)),
        out_shape=jax.ShapeDtypeStruct((t, k), F32),
        compiler_params=pltpu.CompilerParams(dimension_semantics=("arbitrary",), vmem_limit_bytes=MM_VMEM_LIMIT),
    )(dg, du, wg, wu)


def _ffn_dw_in(h, dg, du, n_sh):
    t, k = h.shape
    cc = dg.shape[1] // n_sh
    tk = 512

    def body(h_ref, dg_ref, du_ref, og_ref, ou_ref):
        hb = h_ref[...].astype(BF16)
        og_ref[...] = _tn(hb, dg_ref[...]).astype(BF16)
        ou_ref[...] = _tn(hb, du_ref[...]).astype(BF16)

    d_spec = pl.BlockSpec((t, cc), lambda i, j: (0, j))
    o_spec = pl.BlockSpec((None, tk, cc), lambda i, j: (j, i, 0))
    out = jax.ShapeDtypeStruct((n_sh, k, cc), BF16)
    return pl.pallas_call(
        body, name="ffn_gate_up_dw", grid=(k // tk, n_sh),
        in_specs=[pl.BlockSpec((t, tk), lambda i, j: (0, i)), d_spec, d_spec],
        out_specs=[o_spec, o_spec], out_shape=[out, out],
        compiler_params=pltpu.CompilerParams(dimension_semantics=("arbitrary", "arbitrary"),
                                             vmem_limit_bytes=MM_VMEM_LIMIT),
    )(h, dg, du)


@jax.custom_vjp
def _ffn_block(h, wg, wu, wd):
    act = _ffn_in(h, wg, wu)[2]
    return _mm(act, wd.reshape(-1, wd.shape[2]), "nn", "ffn_down_fwd", MM_ROW_TILE, wd.shape[2])


def _ffn_block_fwd(h, wg, wu, wd):
    g, u, act = _ffn_in(h, wg, wu)
    y = _mm(act, wd.reshape(-1, wd.shape[2]), "nn", "ffn_down_fwd", MM_ROW_TILE, wd.shape[2])
    return y, (h, wg, wu, wd, g, u, act)


def _ffn_block_bwd(res, dy):
    h, wg, wu, wd, g, u, act = res
    dg, du = _ffn_mid_bwd(dy, wd, g, u)
    dh = _ffn_dh(dg, du, wg, wu)
    n_sh = wg.shape[0]
    dwg, dwu = _ffn_dw_in(h, dg, du, n_sh)
    dwd = _mm(act, dy, "tn", "ffn_down_dw", 256, wd.shape[2], out_dtype=BF16).reshape(wd.shape)
    return dh, dwg, dwu, dwd


_ffn_block.defvjp(_ffn_block_fwd, _ffn_block_bwd)


def _swap_halves(w):
    half = w.shape[-1] // 2
    return jnp.concatenate([w[..., half:], w[..., :half]], axis=-1)


def _pad_lanes(w):
    return jnp.concatenate([w, jnp.zeros(w.shape[:-1] + (LANES - w.shape[-1],), w.dtype)], axis=-1)


def _join_cols(shards):
    return shards.transpose(1, 0, 2).reshape(shards.shape[1], -1)


def _mod_parts(mod):
    return [mod[:, i * D_MODEL:(i + 1) * D_MODEL] for i in range(N_MOD)]


def _mixing_stage(x, mod, p, cos, sin):
    shift1, scale1 = _mod_parts(mod)[:2]

    w_in_t = p["w_in"].reshape(-1, D_MODEL)
    k_rope_rows = w_in_t[2176:2240]

    def pad_rows(a):
        return jnp.concatenate([a, jnp.zeros((LANES - a.shape[0], D_MODEL), a.dtype)], axis=0)

    swapped = jnp.concatenate([k_rope_rows[MLA_ROPE // 2:], k_rope_rows[:MLA_ROPE // 2]], axis=0)
    w_in_ext = jnp.concatenate([w_in_t[:2176], pad_rows(k_rope_rows), pad_rows(swapped),
                                jnp.zeros((LANES, D_MODEL), w_in_t.dtype)], axis=0)
    h1, x_res = _make_rowwise("pre_attn", _f_pre_attn, 1, 3, [D_MODEL, D_MODEL], [True], out_dtypes=[BF16, F32])(
        x, p["norm_attn"], scale1, shift1)
    q_sb, k_sb, v_sb, cq, ckv, kr, kr_sw = _make_linear_split_t(
        "in_proj", (SB_WIDTH, SB_WIDTH, SB_WIDTH, MLA_Q_RANK, MLA_KV_RANK, LANES, LANES), 512)(h1, w_in_ext)

    o_sb = _sb_attention(q_sb, k_sb, v_sb)

    wq = _join_cols(p["w_q_up"]).reshape(MLA_Q_RANK, MLA_HEADS, MLA_QK)
    wq_n, wq_r = wq[:, :, :MLA_NOPE], wq[:, :, MLA_NOPE:]
    w_q_ext = jnp.concatenate([wq_n.reshape(MLA_Q_RANK, -1), _pad_lanes(wq_r).reshape(MLA_Q_RANK, -1),
                               _pad_lanes(_swap_halves(wq_r)).reshape(MLA_Q_RANK, -1)], axis=1)
    wkv = _join_cols(p["w_kv_up"]).reshape(MLA_KV_RANK, MLA_HEADS, MLA_NOPE + MLA_V)
    w_kv_ext = jnp.concatenate([wkv[:, :, :MLA_NOPE].reshape(MLA_KV_RANK, -1),
                                wkv[:, :, MLA_NOPE:].reshape(MLA_KV_RANK, -1)], axis=1)
    cqn, ckvn = _make_rowwise("mla_a", _f_mla_a, 2, 2, [MLA_Q_RANK, MLA_KV_RANK], [True, True],
                              out_dtypes=[BF16, BF16], grad_dtypes=[BF16, BF16])(
        cq, ckv, p["q_a_norm"], p["kv_a_norm"])
    qall = _make_linear("q_up", 384, 768)(cqn, w_q_ext)
    kn_all, v_mla = _make_linear_split("kv_up", (MLA_HEADS * MLA_NOPE, MLA_HEADS * MLA_V), MLA_KV_RANK)(ckvn, w_kv_ext)
    gq = p["q_norm"]
    gkr = p["k_rope_norm"]
    qn, qr, kn, krr = _make_rowwise("mla_b", _f_mla_b, 6, 6, [512, 512, 512, LANES],
                                    [True, True, True, True, False, False],
                                    out_dtypes=[BF16] * 4, grad_dtypes=[BF16] * 4)(
        qall, kn_all, kr, kr_sw, cos, sin,
        gq[:, :MLA_NOPE], _pad_lanes(gq[:, MLA_NOPE:]), _pad_lanes(_swap_halves(gq[:, MLA_NOPE:])),
        p["k_nope_norm"], _pad_lanes(gkr), _pad_lanes(_swap_halves(gkr)))
    o_mla = _mla_attention(qn, qr, kn, krr, v_mla)

    (mixed,) = _make_rowwise("post_attn", _f_post_attn, 2, 2, [D_MODEL], [True, True])(
        o_sb, o_mla, p["out_norm_sb"], p["out_norm_mla"])
    return mixed, x_res


def _ffn_stage(x, mixed, mod, p):
    _, _, gate1, shift2, scale2, _ = _mod_parts(mod)
    attn = _make_linear("out_proj", 512, 512)(mixed, p["w_out"].reshape(D_MODEL, D_MODEL))

    x2, h2 = _make_rowwise("pre_ffn", _f_pre_ffn, 2, 4, [D_MODEL, D_MODEL], [True, True],
                           out_dtypes=[F32, BF16], grad_dtypes=[F32, BF16])(
        x, attn, gate1, p["norm_ffn"], scale2, shift2)
    return x2, _ffn_block(h2, p["w_gate"], p["w_up"], p["w_down"])


def _my_place():
    return lax.axis_index("x"), lax.axis_index("y"), lax.axis_index("c")


def _small_gather(x_ref, out_ref, send_sems, recv_sems, base, local_sem):
    m_per = x_ref.shape[0]
    x, y, c = _my_place()
    me, sibling = (x, y, c), (x, y, 1 - c)
    chips = [(1 - x, y), (x, 1 - y), (1 - x, 1 - y)]

    def rows(px, py, pc):
        return out_ref.at[pl.ds((4 * px + 2 * py + pc) * m_per, m_per), :]

    def copy(k, blk, to, src=None):
        return _remote(rows(*blk) if src is None else src, rows(*blk), send_sems, recv_sems, base + k, to)

    mine = pltpu.make_async_copy(x_ref, rows(*me), local_sem)
    first = [copy(0, me, sibling, src=x_ref)] + [copy(1 + j, me, (*chip, c), src=x_ref) for j, chip in enumerate(chips)]
    passed = [copy(4 + j, (*chip, c), sibling) for j, chip in enumerate(chips)]

    def start():
        mine.start()
        for cp in first:
            cp.start()

    def finish():
        for j, chip in enumerate(chips):
            copy(1 + j, (*chip, c), me).wait_recv()
            passed[j].start()
        copy(0, sibling, me).wait_recv()
        for j, chip in enumerate(chips):
            copy(4 + j, (*chip, 1 - c), me).wait_recv()
        for cp in first + passed:
            cp.wait_send()
        mine.wait()

    return start, finish


EARLY =("w_in", "w_q_up", "w_kv_up")
LATE = ("w_out", "w_gate", "w_up", "w_down")
BIG = EARLY + LATE
TRANSPOSED_UPDATE = ("w_in", "w_gate", "w_up")
HALF_AXIS = {"w_in": 1, "w_q_up": 0, "w_kv_up": 0, "w_out": 0, "w_gate": 0, "w_up": 0, "w_down": 1}
TRAVELS_TRANSPOSED = ("w_in",)


def _half(ref, h, axis, lead=()):
    trail = ref.shape[len(lead):]
    idx = list(lead) + [slice(None)] * len(trail)
    at = len(trail) - 2 + axis
    n2 = trail[at] // 2
    idx[len(lead) + at] = pl.ds(h * n2, n2)
    return ref.at[tuple(idx)]


def _half_shape(shape, axis):
    shape = list(shape)
    shape[len(shape) - 2 + axis] //= 2
    return tuple(shape)


def _remote(src, dst, send_sems, recv_sems, k, to):
    return pltpu.make_async_remote_copy(src_ref=src, dst_ref=dst, send_sem=send_sems.at[k],
                                        recv_sem=recv_sems.at[k], device_id=to, device_id_type=MESH)


def _gather_weights(names, shards, small_block):
    n_w = len(shards)
    axes = [HALF_AXIS[n] for n in names]

    def body(*refs):
        w_refs, small_ref = refs[:n_w], refs[n_w]
        out_refs, token, small_out = refs[n_w + 1:2 * n_w + 1], refs[2 * n_w + 1], refs[2 * n_w + 2]
        send_sems, recv_sems, local_sems = refs[2 * n_w + 3:]
        token[...] = jnp.zeros_like(token)
        x, y, c = _my_place()
        sibling = (x, y, 1 - c)
        chips = [(1 - x, y), (x, 1 - y), (1 - x, 1 - y)]
        me = 2 * x + y
        small_start, small_finish = _small_gather(small_ref, small_out, send_sems, recv_sems, 6 * n_w,
                                                  local_sems.at[n_w])
        small_start()
        mine = [pltpu.make_async_copy(w, o.at[me], local_sems.at[i]) for i, (w, o) in enumerate(zip(w_refs, out_refs))]
        for cp in mine:
            cp.start()
        first = [_remote(_half(w_refs[i], c, axes[i]), _half(out_refs[i], c, axes[i], (me,)),
                         send_sems, recv_sems, 6 * i + j, (*chip, c))
                 for i in range(n_w) for j, chip in enumerate(chips)]
        for cp in first:
            cp.start()
        small_finish()
        passed = []
        for j, (cx, cy) in enumerate(chips):
            for i in range(n_w):
                blk = _half(out_refs[i], c, axes[i], (2 * cx + cy,))
                _remote(blk, blk, send_sems, recv_sems, 6 * i + j, (cx, cy, c)).wait_recv()
                cp = _remote(blk, blk, send_sems, recv_sems, 6 * i + 3 + j, sibling)
                cp.start()
                passed.append(cp)
        for j, (cx, cy) in enumerate(chips):
            for i in range(n_w):
                blk = _half(out_refs[i], 1 - c, axes[i], (2 * cx + cy,))
                _remote(blk, blk, send_sems, recv_sems, 6 * i + 3 + j, sibling).wait_recv()
        for cp in first + passed:
            cp.wait_send()
        for cp in mine:
            cp.wait()

    outs = pl.pallas_call(
        body, name="gather_weights",
        out_shape=[jax.ShapeDtypeStruct((N_CHIPS,) + s.shape, s.dtype) for s in shards]
        + [jax.ShapeDtypeStruct((8, LANES), F32),
           jax.ShapeDtypeStruct((N_DEV * small_block.shape[0], small_block.shape[1]), small_block.dtype)],
        in_specs=[ANY] * (n_w + 1), out_specs=[ANY] * n_w + [pl.BlockSpec(memory_space=pltpu.VMEM), ANY],
        scratch_shapes=[pltpu.SemaphoreType.DMA((6 * n_w + 7,)), pltpu.SemaphoreType.DMA((6 * n_w + 7,)),
                        pltpu.SemaphoreType.DMA((n_w + 1,))],
    )(*shards, small_block)
    return outs[:n_w], outs[n_w], outs[n_w + 1]


def _pair_exchange(names, grads, call_name, small_block):
    n_w = len(grads)
    axes = [HALF_AXIS[n] for n in names]

    def body(*refs):
        g_refs, small_ref = refs[:n_w], refs[n_w]
        t_refs, small_out = refs[n_w + 1:2 * n_w + 1], refs[2 * n_w + 1]
        send_sems, recv_sems, local_sem = refs[2 * n_w + 2:]
        x, y, c = _my_place()
        small_start, small_finish = _small_gather(small_ref, small_out, send_sems, recv_sems, n_w, local_sem)
        small_start()
        sends = [_remote(_half(g_refs[i], 1 - c, axes[i]), t_refs[i], send_sems, recv_sems, i, (x, y, 1 - c))
                 for i in range(n_w)]
        for cp in sends:
            cp.start()
        small_finish()
        for cp in sends:
            cp.wait_recv()
        for cp in sends:
            cp.wait_send()

    outs = pl.pallas_call(
        body, name=call_name,
        out_shape=[jax.ShapeDtypeStruct(_half_shape(g.shape, a), g.dtype) for g, a in zip(grads, axes)]
        + [jax.ShapeDtypeStruct((N_DEV * small_block.shape[0], small_block.shape[1]), small_block.dtype)],
        in_specs=[ANY] * (n_w + 1), out_specs=[ANY] * (n_w + 1),
        scratch_shapes=[pltpu.SemaphoreType.DMA((n_w + 7,)), pltpu.SemaphoreType.DMA((n_w + 7,)),
                        pltpu.SemaphoreType.DMA],
    )(*grads, small_block)
    return outs[:n_w], outs[n_w]


def _sibling_join(halves, name, after):
    n_w = len(halves)

    def body(*refs):
        s_refs, j_refs = refs[:n_w], refs[n_w + 1:2 * n_w + 1]
        send_sems, recv_sems = refs[2 * n_w + 1:]
        x, y, c = _my_place()
        sends = [_remote(s_refs[i], j_refs[i], send_sems, recv_sems, i, (x, y, 1 - c)) for i in range(n_w)]
        for cp in sends:
            cp.start()
        for cp in sends:
            cp.wait_recv()
        for cp in sends:
            cp.wait_send()

    return pl.pallas_call(
        body, name=name,
        out_shape=[jax.ShapeDtypeStruct(s.shape, s.dtype) for s in halves],
        in_specs=[ANY] * (n_w + 1), out_specs=[ANY] * n_w,
        scratch_shapes=[pltpu.SemaphoreType.DMA((n_w,)), pltpu.SemaphoreType.DMA((n_w,))],
    )(*halves, after)


HBM_SPEC = pl.BlockSpec(memory_space=pltpu.HBM)
SEM_SPEC = pl.BlockSpec(memory_space=pltpu.SEMAPHORE)
DATAFLOW = pltpu.SideEffectType.DATAFLOW_SIDE_EFFECTING


def _in_hbm(a):
    return pltpu.with_memory_space_constraint(a, pltpu.HBM)


def _exchange_start(name, srcs, lands, plan, n_copies, after, thru):
    n = len(srcs)

    def body(*refs):
        src_refs, land_refs = refs[:n], refs[n:2 * n]
        send_sems, recv_sems = refs[2 * n + 2], refs[2 * n + 3]
        for k, (src, dst, to, k_recv) in enumerate(plan(src_refs, land_refs)):
            pltpu.make_async_remote_copy(src_ref=src, dst_ref=dst, send_sem=send_sems.at[k],
                                         recv_sem=recv_sems.at[k_recv], device_id=to, device_id_type=MESH).start()

    outs = pl.pallas_call(
        body, name=name,
        out_shape=(pltpu.SemaphoreType.DMA((n_copies,)), pltpu.SemaphoreType.DMA((n_copies,)),
                   *[pltpu.HBM(a.shape, a.dtype) for a in list(srcs) + list(lands) + [thru]]),
        in_specs=[HBM_SPEC] * (2 * n + 1) + [ANY],
        out_specs=(SEM_SPEC, SEM_SPEC, *[HBM_SPEC] * (2 * n + 1)),
        input_output_aliases={i: 2 + i for i in range(2 * n + 1)},
        compiler_params=pltpu.CompilerParams(has_side_effects=DATAFLOW),
    )(*[_in_hbm(a) for a in list(srcs) + list(lands) + [thru]], after)
    return outs[0], outs[1], outs[2:2 + n], outs[2 + n:2 + 2 * n], outs[2 + 2 * n]


def _exchange_wait(name, started, plan, after):
    send_sems, recv_sems, srcs, lands, _ = started
    n = len(srcs)

    def body(*refs):
        src_refs, land_refs = refs[:n], refs[n:2 * n]
        s_sems, r_sems = refs[2 * n], refs[2 * n + 1]
        for k, (src, dst, to, _) in enumerate(plan(src_refs, land_refs)):
            cp = _remote(src, dst, s_sems, r_sems, k, to)
            cp.wait_send()
            cp.wait_recv()

    outs = pl.pallas_call(
        body, name=name,
        out_shape=tuple(pltpu.HBM(a.shape, a.dtype) for a in list(srcs) + list(lands)),
        in_specs=[HBM_SPEC] * (2 * n) + [SEM_SPEC, SEM_SPEC, ANY],
        out_specs=tuple([HBM_SPEC] * (2 * n)),
        input_output_aliases={i: i for i in range(2 * n)},
        compiler_params=pltpu.CompilerParams(has_side_effects=DATAFLOW),
    )(*srcs, *lands, send_sems, recv_sems, after)
    return outs[:n], outs[n:]


def _late_gather_plan(src_refs, land_refs):
    x, y, c = _my_place()
    chips = [(1 - x, y), (x, 1 - y), (1 - x, 1 - y)]
    plan = [(src, land.at[2 * x + y], (cx, cy, c)) for src, land in zip(src_refs, land_refs) for cx, cy in chips]
    return [entry + (k,) for k, entry in enumerate(plan)]


def _late_scatter_plan(src_refs, land_refs):
    x, y, c = _my_place()
    chips = [(1 - x, y), (x, 1 - y), (1 - x, 1 - y)]
    plan = [(src.at[2 * cx + cy], land.at[j], (cx, cy, c))
            for src, land in zip(src_refs, land_refs) for j, (cx, cy) in enumerate(chips)]
    return [entry + (k,) for k, entry in enumerate(plan)]


def _direct_scatter_plan(names):
    axes = [HALF_AXIS[n] for n in names]

    def plan(src_refs, land_refs):
        x, y, c = _my_place()
        chips = [(1 - x, y), (x, 1 - y), (1 - x, 1 - y)]
        out = []
        for i, (src, land) in enumerate(zip(src_refs, land_refs)):
            for f, (cx, cy) in enumerate(chips):
                for core in range(2):
                    out.append((_half(src, core, axes[i], (2 * cx + cy,)), land.at[2 * f + c], (cx, cy, core),
                                7 * i + 2 * f + c))
            out.append((_half(src, 1 - c, axes[i], (2 * x + y,)), land.at[6], (x, y, 1 - c), 7 * i + 6))
        return out

    return plan


def _row_tile(rows, mult=16, limit=ROW_TILE):
    return max(d for d in range(mult, limit + 1, mult) if rows % d == 0)


def _pair_sum(place, g, theirs, axis, name):
    nj, rr, cc = theirs.shape
    tr = _row_tile(rr, limit=1024)
    nb = rr // tr
    if axis == 0:
        g_map = lambda j, i, pr: (j, pr[0] * nb + i, 0)
    else:
        g_map = lambda j, i, pr: (j, i, pr[0])

    def body(pr, g_ref, t_ref, o_ref):
        o_ref[...] = (g_ref[...].astype(F32) + t_ref[...].astype(F32)).astype(BF16)

    spec = pl.BlockSpec((None, tr, cc), lambda j, i, pr: (j, i, 0))
    return pl.pallas_call(
        body, name=name,
        grid_spec=pltpu.PrefetchScalarGridSpec(
            num_scalar_prefetch=1, grid=(nj, nb),
            in_specs=[pl.BlockSpec((None, tr, cc), g_map), spec], out_specs=spec),
        out_shape=jax.ShapeDtypeStruct(theirs.shape, BF16))(place, g, theirs)


def _chip_sum(place, pair_sums, parts, name, transposed):
    _, rr, cc = parts.shape
    tr = _row_tile(rr, LANES) if transposed else _row_tile(rr, limit=1024)

    def body(pr, h_ref, p_ref, o_ref):
        acc = p_ref[0].astype(F32)
        for j in range(1, N_CHIPS - 1):
            acc = acc + p_ref[j].astype(F32)
        acc = acc + h_ref[...].astype(F32)
        o_ref[...] = (acc.T if transposed else acc).astype(BF16)

    out_spec = pl.BlockSpec((cc, tr), lambda i, pr: (0, i)) if transposed else pl.BlockSpec((tr, cc), lambda i, pr: (i, 0))
    return pl.pallas_call(
        body, name=name,
        grid_spec=pltpu.PrefetchScalarGridSpec(
            num_scalar_prefetch=1, grid=(rr // tr,),
            in_specs=[pl.BlockSpec((None, tr, cc), lambda i, pr: (pr[1], i, 0)),
                      pl.BlockSpec((N_CHIPS - 1, tr, cc), lambda i, pr: (0, i, 0))],
            out_specs=out_spec),
        out_shape=jax.ShapeDtypeStruct((cc, rr) if transposed else (rr, cc), BF16))(place, pair_sums, parts)


def _chip_sum_direct(place, g, parts, axis, name, transposed):
    n_parts, rr, cc = parts.shape
    tr = _row_tile(rr, LANES) if transposed else _row_tile(rr, limit=1024)
    nb = rr // tr
    if axis == 0:
        g_map = lambda i, pr: (pr[1], pr[0] * nb + i, 0)
    else:
        g_map = lambda i, pr: (pr[1], i, pr[0])

    def body(pr, g_ref, p_ref, o_ref):
        acc = p_ref[0].astype(F32)
        for j in range(1, n_parts):
            acc = acc + p_ref[j].astype(F32)
        acc = acc + g_ref[...].astype(F32)
        o_ref[...] = (acc.T if transposed else acc).astype(BF16)

    out_spec = pl.BlockSpec((cc, tr), lambda i, pr: (0, i)) if transposed else pl.BlockSpec((tr, cc), lambda i, pr: (i, 0))
    return pl.pallas_call(
        body, name=name,
        grid_spec=pltpu.PrefetchScalarGridSpec(
            num_scalar_prefetch=1, grid=(nb,),
            in_specs=[pl.BlockSpec((None, tr, cc), g_map), pl.BlockSpec((n_parts, tr, cc), lambda i, pr: (0, i, 0))],
            out_specs=out_spec),
        out_shape=jax.ShapeDtypeStruct((cc, rr) if transposed else (rr, cc), BF16))(place, g, parts)


def _silu(v):
    return v / (1.0 + jnp.exp(-v))


def _ada_fwd(c_all, w_shard, b_shard):
    n_seq, n_cols = c_all.shape[0], w_shard.shape[1]

    def body(c_ref, w_ref, b_ref, o_ref, mine_ref, send_sems, recv_sems, local_sem):
        mine_ref[...] = jnp.dot(_silu(c_ref[...]), w_ref[...], precision=lax.Precision.HIGHEST,
                                preferred_element_type=F32) + b_ref[...]
        start, finish = _small_gather(mine_ref, o_ref, send_sems, recv_sems, 0, local_sem)
        start()
        finish()

    return pl.pallas_call(
        body, name="ada_fwd", out_shape=jax.ShapeDtypeStruct((N_DEV * n_seq, n_cols), F32),
        scratch_shapes=[pltpu.VMEM((n_seq, n_cols), F32), pltpu.SemaphoreType.DMA((7,)), pltpu.SemaphoreType.DMA((7,)),
                        pltpu.SemaphoreType.DMA],
        compiler_params=pltpu.CompilerParams(vmem_limit_bytes=MM_VMEM_LIMIT))(c_all, w_shard, b_shard)


def _loss_and_grads(x2, ffn, target, gate):
    t, d = x2.shape

    def half_loss(x2_blk, ffn_blk, gate_row, target_blk):
        return 0.5 * _f_loss(x2_blk, ffn_blk, target_blk, gate_row)[0]

    def body(x2_ref, ffn_ref, tgt_ref, gate_ref, loss_ref, dx2_ref, dffn_ref, dgate_ref):
        rows, vjp = jax.vjp(lambda a, b, g: half_loss(a, b, g, tgt_ref[...]), x2_ref[...], ffn_ref[...], gate_ref[...])
        loss_ref[...] = rows
        dx2_ref[...], dffn_ref[...], dgate = vjp(jnp.ones_like(rows))

        @pl.when(pl.program_id(0) == 0)
        def _():
            dgate_ref[...] = jnp.zeros_like(dgate_ref)

        dgate_ref[...] += dgate

    blk = pl.BlockSpec((ROW_TILE, d), lambda i: (i, 0))
    row = pl.BlockSpec((1, d), lambda i: (0, 0))
    return pl.pallas_call(
        body, name="loss_and_grads", grid=(t // ROW_TILE,),
        in_specs=[blk, blk, blk, row],
        out_specs=[pl.BlockSpec((ROW_TILE, 1), lambda i: (i, 0)), blk, blk, row],
        out_shape=[jax.ShapeDtypeStruct((t, 1), F32), jax.ShapeDtypeStruct((t, d), F32), jax.ShapeDtypeStruct((t, d), F32),
                   jax.ShapeDtypeStruct((1, d), F32)],
        compiler_params=pltpu.CompilerParams(dimension_semantics=("arbitrary",), vmem_limit_bytes=MM_VMEM_LIMIT),
    )(x2, ffn, target, gate)


def _ada_bwd(c_all, dmod_cols):
    def body(c_ref, d_ref, o_ref):
        o_ref[...] = lax.dot_general(_silu(c_ref[...]), d_ref[...], (((0,), (0,)), ((), ())),
                                     precision=lax.Precision.HIGHEST, preferred_element_type=F32)

    return pl.pallas_call(body, name="ada_bwd", out_shape=jax.ShapeDtypeStruct((c_all.shape[1], dmod_cols.shape[1]), F32),
                          compiler_params=pltpu.CompilerParams(vmem_limit_bytes=MM_VMEM_LIMIT))(c_all, dmod_cols)


def _adamw_math(w, g, m, v):
    m = ADAM_B1 * m + (1.0 - ADAM_B1) * g
    v = ADAM_B2 * v + (1.0 - ADAM_B2) * (g * g)
    m_hat = m / (1.0 - ADAM_B1 ** ADAM_STEP)
    v_hat = v / (1.0 - ADAM_B2 ** ADAM_STEP)
    delta = -ADAM_LR * (m_hat / (jnp.sqrt(v_hat) + ADAM_EPS) + ADAM_WD * w)
    return delta, m, v


def _adamw(w, g, m, v, name):
    r, ccols = w.shape
    tr = max(d for d in range(8, ROW_TILE + 1, 8) if r % d == 0)
    spec = pl.BlockSpec((tr, ccols), lambda i: (i, 0))

    def body(w_ref, g_ref, m_ref, v_ref, d_ref, nm_ref, nv_ref):
        d_ref[...], nm_ref[...], nv_ref[...] = _adamw_math(w_ref[...], g_ref[...], m_ref[...], v_ref[...])

    return pl.pallas_call(body, name=name, grid=(r // tr,), in_specs=[spec] * 4, out_specs=[spec] * 3,
                          out_shape=[jax.ShapeDtypeStruct(w.shape, F32)] * 3,
                          compiler_params=pltpu.CompilerParams(vmem_limit_bytes=MM_VMEM_LIMIT))(w, g, m, v)


def _small_layout(sizes):
    offs, off = [], 0
    for n in sizes:
        offs.append(off)
        off += -(-n // LANES) * LANES
    total = -(-(off + LANES) // (8 * LANES)) * (8 * LANES)
    return offs, off, total


def _adamw_small(ws, g_all, ms, vs, offs, loss_off):
    n_p = len(ws)

    def device_sum(g_ref, off, width):
        blk = g_ref[:, off:off + width]
        acc = blk[0:1]
        for d in range(1, N_DEV):
            acc = acc + blk[d:d + 1]
        return acc

    def body(*refs):
        w_refs, m_refs, v_refs = refs[:n_p], refs[n_p:2 * n_p], refs[2 * n_p:3 * n_p]
        g_ref = refs[3 * n_p]
        outs = refs[3 * n_p + 1:]
        for i in range(n_p):
            n = w_refs[i].shape[1]
            g = device_sum(g_ref, offs[i], -(-n // LANES) * LANES)[:, :n]
            outs[i][...] = g
            outs[n_p + i][...], outs[2 * n_p + i][...], outs[3 * n_p + i][...] = _adamw_math(
                w_refs[i][...], g, m_refs[i][...], v_refs[i][...])
        outs[4 * n_p][...] = device_sum(g_ref, loss_off, LANES)

    res = pl.pallas_call(
        body, name="adamw_small",
        out_shape=[jax.ShapeDtypeStruct(a.shape, F32) for a in list(ws) * 4] + [jax.ShapeDtypeStruct((1, LANES), F32)],
    )(*ws, *ms, *vs, g_all)
    return res[:n_p], res[n_p:2 * n_p], res[2 * n_p:3 * n_p], res[3 * n_p:4 * n_p], res[4 * n_p]


def _adamw_halves(place, w, own, sib, m, v, axis, name, after):
    r, cc = w.shape
    if axis == 0:
        rows, gc = own.shape[0], own.shape[1]
        tr = _row_tile(rows)
        nb = rows // tr
        w_spec = pl.BlockSpec((tr, cc), lambda h, i, pr: (h * nb + i, 0))
        g_spec = pl.BlockSpec((tr, gc), lambda h, i, pr: (i, 0))
    else:
        tr = _row_tile(r)
        nb = r // tr
        gc = own.shape[1]
        w_spec = pl.BlockSpec((tr, gc), lambda h, i, pr: (i, h))
        g_spec = pl.BlockSpec((tr, gc), lambda h, i, pr: (i, 0))
    wc = w_spec.block_shape[1]

    def body(pr, w_ref, o_ref, s_ref, m_ref, v_ref, after_ref, g_ref, d_ref, nm_ref, nv_ref):
        g = jnp.where(pl.program_id(0) == pr[0], o_ref[...], s_ref[...]).astype(F32)[:, :wc]
        g_ref[...] = g
        d_ref[...], nm_ref[...], nv_ref[...] = _adamw_math(w_ref[...], g, m_ref[...], v_ref[...])

    return pl.pallas_call(
        body, name=name,
        grid_spec=pltpu.PrefetchScalarGridSpec(
            num_scalar_prefetch=1, grid=(2, nb),
            in_specs=[w_spec, g_spec, g_spec, w_spec, w_spec, ANY], out_specs=[w_spec] * 4),
        out_shape=[jax.ShapeDtypeStruct(w.shape, F32)] * 4,
        compiler_params=pltpu.CompilerParams(vmem_limit_bytes=MM_VMEM_LIMIT))(place, w, own, sib, m, v, after)


SMALL = ("b_ada", "norm_attn", "norm_ffn", "q_a_norm", "kv_a_norm", "q_norm", "k_nope_norm", "k_rope_norm",
         "out_norm_sb", "out_norm_mla")
WEIGHTS = ("w_ada", "b_ada", "norm_attn", "norm_ffn", "w_in", "q_a_norm", "w_q_up", "kv_a_norm", "w_kv_up",
           "q_norm", "k_nope_norm", "k_rope_norm", "out_norm_sb", "out_norm_mla", "w_out", "w_gate", "w_up",
           "w_down")


def kernel(x, c, positions, w_ada, b_ada, norm_attn, norm_ffn, w_in, q_a_norm, w_q_up, kv_a_norm, w_kv_up, q_norm, k_nope_norm, k_rope_norm, out_norm_sb, out_norm_mla, w_out, w_gate, w_up, w_down, loss_target, m_w_ada, m_b_ada, m_norm_attn, m_norm_ffn, m_w_in, m_q_a_norm, m_w_q_up, m_kv_a_norm, m_w_kv_up, m_q_norm, m_k_nope_norm, m_k_rope_norm, m_out_norm_sb, m_out_norm_mla, m_w_out, m_w_gate, m_w_up, m_w_down, v_w_ada, v_b_ada, v_norm_attn, v_norm_ffn, v_w_in, v_q_a_norm, v_w_q_up, v_kv_a_norm, v_w_kv_up, v_q_norm, v_k_nope_norm, v_k_rope_norm, v_out_norm_sb, v_out_norm_mla, v_w_out, v_w_gate, v_w_up, v_w_down):
    local = dict(locals())
    w = {n: local[n][0] for n in WEIGHTS}
    m = {n: local["m_" + n][0] for n in WEIGHTS}
    v = {n: local["v_" + n][0] for n in WEIGHTS}
    small = {n: w[n].reshape(1, -1) for n in SMALL}
    ix, iy, ic = _my_place()
    chip = 2 * ix + iy
    dev = 2 * chip + ic
    xs, target = x[0], loss_target[0]
    seq = xs.shape[0]

    ff_pad = FF_SHARD_PAD - FF_SHARD
    pads = {"w_gate": ((0, 0), (0, ff_pad)), "w_up": ((0, 0), (0, ff_pad)), "w_down": ((0, ff_pad), (0, 0))}
    shards = {n: jnp.pad(w[n].astype(BF16), pads[n]) if n in pads else w[n].astype(BF16) for n in BIG}
    shards.update({n: w[n].T.astype(BF16) for n in TRAVELS_TRANSPOSED})
    early, early_done, c_gathered = _gather_weights(EARLY, [shards[n] for n in EARLY], c.reshape(8, LANES))
    gathered = dict(zip(EARLY, early))

    c_all = c_gathered.reshape(N_DEV, D_MODEL)
    ada_cols = w["w_ada"].shape[1]
    b_cols = lax.dynamic_slice_in_dim(small["b_ada"], chip * ada_cols, ada_cols, axis=1)
    mod_all = _ada_fwd(c_all, w["w_ada"], b_cols).reshape(N_CHIPS, 2, N_DEV, ada_cols)
    mod = lax.dynamic_index_in_dim(mod_all[:, 0], dev, axis=1, keepdims=False).reshape(1, N_MOD * D_MODEL)

    lands = [lax.dynamic_update_index_in_dim(lax.empty((N_CHIPS,) + shards[n].shape, BF16), shards[n], chip, 0)
             for n in LATE]
    late_gather = _exchange_start("gather_late_start", [shards[n] for n in LATE], lands, _late_gather_plan,
                                  3 * len(LATE), early_done, mod)
    mod = late_gather[4]

    half = MLA_ROPE // 2
    freqs = 1.0 / (ROPE_THETA ** (np.arange(half, dtype=np.float32) / half))
    zeros = np.zeros(LANES - MLA_ROPE, np.float32)
    freqs_row = jnp.asarray(np.concatenate([freqs, freqs, zeros]).astype(np.float32)[None])
    sign_row = jnp.asarray(np.concatenate([-np.ones(half), np.ones(half), zeros]).astype(np.float32)[None])
    cos, sin = _rope_tables(positions.reshape(seq, 1), freqs_row, sign_row)

    place = jnp.stack([ic, chip]).astype(jnp.int32)
    small_params = {n: small[n] for n in SMALL if n != "b_ada"}

    p1 = {**{n: gathered[n] for n in EARLY}, **small_params}
    (mixed, x_res), mixing_vjp = jax.vjp(lambda x_, mod_, p_: _mixing_stage(x_, mod_, p_, cos, sin), xs, mod, p1)
    _, landed = _exchange_wait("gather_late_wait", late_gather, _late_gather_plan, mixed)
    p2 = {**dict(zip(LATE, landed)), **small_params}
    (x2, ffn), ffn_vjp = jax.vjp(_ffn_stage, x_res, mixed, mod, p2)
    loss_rows, g_x2, g_ffn, g_gate2 = _loss_and_grads(x2, ffn, target, _mod_parts(mod)[5])
    loss_part = jnp.sum(loss_rows)
    gx2, gmixed, gmod2, gp2 = ffn_vjp((g_x2, g_ffn))
    gmod2 = gmod2 + jnp.concatenate([jnp.zeros((1, (N_MOD - 1) * D_MODEL), F32), g_gate2], axis=1)
    late_grads = [gp2[n] for n in LATE]
    late_plan = _direct_scatter_plan(LATE)
    late_scatter = _exchange_start(
        "grad_scatter_late_start", late_grads,
        [lax.empty((7,) + _half_shape(gr.shape[1:], HALF_AXIS[n]), BF16) for n, gr in zip(LATE, late_grads)],
        late_plan, 7 * len(LATE), gx2, gmixed)
    gx, gmod1, gp1 = mixing_vjp((late_scatter[4], gx2))
    gmod = gmod1 + gmod2
    gp = {n: gp1[n] + gp2[n] for n in small_params}

    sizes = [w[n].size for n in SMALL]
    offs, loss_off, n_small = _small_layout(sizes)
    pieces = []
    for n, size in zip(SMALL, sizes):
        pieces.append(gmod if n == "b_ada" else gp[n])
        if size % LANES:
            pieces.append(jnp.zeros((1, LANES - size % LANES), F32))
    pieces += [jnp.full((1, LANES), loss_part), jnp.zeros((1, n_small - loss_off - LANES), F32)]
    small_vec = jnp.concatenate(pieces, axis=1)

    g, delta, new_m, new_v = {}, {}, {}, {}

    def update(names, own, sib, after):
        for n, o, s in zip(names, own, sib):
            if n in TRANSPOSED_UPDATE:
                res = _adamw_halves(place, w[n].T, o, s, m[n].T, v[n].T, 1, "adamw_" + n, after)
                g[n], delta[n], new_m[n], new_v[n] = [r.T for r in res]
            else:
                g[n], delta[n], new_m[n], new_v[n] = _adamw_halves(place, w[n], o, s, m[n], v[n], HALF_AXIS[n],
                                                                   "adamw_" + n, after)

    late_grads, late_parts = _exchange_wait("grad_scatter_late_wait", late_scatter, late_plan, gx)
    own_late = [_chip_sum_direct(place, gr, pt, HALF_AXIS[n], "grad_chip_sum_" + n, n in TRANSPOSED_UPDATE)
                for n, gr, pt in zip(LATE, late_grads, late_parts)]
    early_grads = [gp1[n] for n in EARLY]
    theirs, small_gathered = _pair_exchange(EARLY, early_grads, "grad_pair_exchange_early",
                                            small_vec.reshape(8, n_small // 8))
    small_all = small_gathered.reshape(N_DEV, n_small)
    sib_late = _sibling_join(own_late, "grad_sibling_join_late", small_all)
    early_sums = [_pair_sum(place, gr, th, HALF_AXIS[n], "grad_pair_sum_" + n)
                  for n, gr, th in zip(EARLY, early_grads, theirs)]
    early_scatter = _exchange_start(
        "grad_scatter_early_start", early_sums,
        [lax.empty((N_CHIPS - 1,) + s.shape[1:], BF16) for s in early_sums], _late_scatter_plan, 3 * len(EARLY),
        sib_late[0], small_all)
    small_all = early_scatter[4]
    update(LATE, own_late, sib_late, small_all)

    *small_out, loss_row = _adamw_small([small[n] for n in SMALL], small_all, [m[n].reshape(1, -1) for n in SMALL],
                                        [v[n].reshape(1, -1) for n in SMALL], offs, loss_off)
    loss = loss_row[0, 0]
    for d, outs_d in zip((g, delta, new_m, new_v), small_out):
        d.update({n: o.reshape(w[n].shape) for n, o in zip(SMALL, outs_d)})

    dmod_all = small_all[:, :N_MOD * D_MODEL]
    g["w_ada"] = _ada_bwd(c_all, lax.dynamic_slice_in_dim(dmod_all, chip * ada_cols, ada_cols, axis=1))
    delta["w_ada"], new_m["w_ada"], new_v["w_ada"] = _adamw(w["w_ada"], g["w_ada"], m["w_ada"], v["w_ada"], "adamw_w_ada")

    early_sums, early_parts = _exchange_wait("grad_scatter_early_wait", early_scatter, _late_scatter_plan,
                                             delta["w_ada"])
    own_early = [_chip_sum(place, ps, pt, "grad_chip_sum_" + n, n in TRANSPOSED_UPDATE and n not in TRAVELS_TRANSPOSED)
                 for n, ps, pt in zip(EARLY, early_sums, early_parts)]
    sib_early = _sibling_join(own_early, "grad_sibling_join_early", delta["w_ada"])
    update(EARLY, own_early, sib_early, sib_early[0])

    def outs(d):
        return [d[n][None] for n in WEIGHTS]

    return (loss, gx[None], *outs(g), *outs(delta), *outs(new_m), *outs(new_v))
```

```python
import numpy as np
import jax
import jax.numpy as jnp
from jax import lax
from jax.experimental import pallas as pl
from jax.experimental.pallas import tpu as pltpu

F32 = jnp.float32
BF16 = jnp.bfloat16
MESH = pl.DeviceIdType.MESH
ANY = pl.BlockSpec(memory_space=pl.ANY)

D_MODEL = 1024
SB_HEADS = 8
SB_HEAD_DIM = 64
SB_WIDTH = 512
MLA_HEADS = 4
MLA_NOPE = 128
MLA_ROPE = 64
MLA_QK = 192
MLA_V = 128
MLA_Q_RANK = 384
MLA_KV_RANK = 256
D_FF = 2816
N_MOD = 6
ROPE_THETA = 10000.0
EPS = 1e-6
LANES = 128

ADAM_LR = 0.001
ADAM_B1 = 0.9
ADAM_B2 = 0.999
ADAM_EPS = 1e-08
ADAM_WD = 0.01
ADAM_STEP = 10

N_CHIPS = 4
N_DEV = 8
ROW_TILE = 512
MM_ROW_TILE = 512
ATT_BLK = 256
MM_VMEM_LIMIT = 56 * 1024 * 1024
FF_SHARD = D_FF // N_CHIPS
FF_SHARD_PAD = 768


def _mm(a, b, mode, name, tm, tn, out_dtype=F32):
    if mode == "nn":
        (m, k), n = a.shape, b.shape[1]
        a_spec = pl.BlockSpec((tm, k), lambda j, i: (i, 0))
        b_spec = pl.BlockSpec((k, tn), lambda j, i: (0, j))
        dims = (((1,), (0,)), ((), ()))
    elif mode == "nt":
        (m, k), n = a.shape, b.shape[0]
        a_spec = pl.BlockSpec((tm, k), lambda j, i: (i, 0))
        b_spec = pl.BlockSpec((tn, k), lambda j, i: (j, 0))
        dims = (((1,), (1,)), ((), ()))
    else:
        (k, m), n = a.shape, b.shape[1]
        a_spec = pl.BlockSpec((k, tm), lambda j, i: (0, i))
        b_spec = pl.BlockSpec((k, tn), lambda j, i: (0, j))
        dims = (((0,), (0,)), ((), ()))
    assert m % tm == 0 and n % tn == 0, (name, m, n, tm, tn)

    def body(a_ref, b_ref, o_ref):
        o_ref[...] = lax.dot_general(a_ref[...].astype(BF16), b_ref[...].astype(BF16), dims,
                                     preferred_element_type=F32).astype(out_dtype)

    return pl.pallas_call(
        body, name=name, grid=(n // tn, m // tm),
        in_specs=[a_spec, b_spec],
        out_specs=pl.BlockSpec((tm, tn), lambda j, i: (i, j)),
        out_shape=jax.ShapeDtypeStruct((m, n), out_dtype),
        compiler_params=pltpu.CompilerParams(dimension_semantics=("arbitrary", "arbitrary"),
                                             vmem_limit_bytes=MM_VMEM_LIMIT),
    )(a, b)


def _make_linear(name, tk_w, tn_w):
    @jax.custom_vjp
    def op(a, w):
        return _mm(a, w, "nn", name + "_fwd", MM_ROW_TILE, w.shape[1])

    def fwd(a, w):
        return op(a, w), (a, w)

    def bwd(res, dy):
        a, w = res
        da = _mm(dy, w, "nt", name + "_dx", MM_ROW_TILE, w.shape[0])
        dw = _mm(a, dy, "tn", name + "_dw", tk_w, tn_w, out_dtype=BF16)
        return da, dw

    op.defvjp(fwd, bwd)
    return op


def _make_linear_split_t(name, widths, tk_w):
    starts = [sum(widths[:g]) for g in range(len(widths))]

    def call_fwd(a, wt):
        t, k = a.shape
        n = wt.shape[0]

        def body(a_ref, w_ref, *o_refs):
            y = _nt(a_ref[...].astype(BF16), w_ref[...])
            for o_ref, s0, wd in zip(o_refs, starts, widths):
                o_ref[...] = y[:, s0:s0 + wd]

        return pl.pallas_call(
            body, name=name + "_fwd", grid=(t // MM_ROW_TILE,),
            in_specs=[pl.BlockSpec((MM_ROW_TILE, k), lambda i: (i, 0)), pl.BlockSpec((n, k), lambda i: (0, 0))],
            out_specs=[pl.BlockSpec((MM_ROW_TILE, wd), lambda i: (i, 0)) for wd in widths],
            out_shape=[jax.ShapeDtypeStruct((t, wd), F32) for wd in widths],
            compiler_params=pltpu.CompilerParams(dimension_semantics=("arbitrary",), vmem_limit_bytes=MM_VMEM_LIMIT),
        )(a, wt)

    def call_dx(dys, wt):
        t = dys[0].shape[0]
        n, k = wt.shape

        def body(*refs):
            dy_refs, w_ref, o_ref = refs[:-2], refs[-2], refs[-1]
            acc = jnp.zeros((MM_ROW_TILE, k), F32)
            for dy_ref, s0, wd in zip(dy_refs, starts, widths):
                acc = acc + jnp.dot(dy_ref[...].astype(BF16), w_ref[s0:s0 + wd, :], preferred_element_type=F32)
            o_ref[...] = acc

        return pl.pallas_call(
            body, name=name + "_dx", grid=(t // MM_ROW_TILE,),
            in_specs=[pl.BlockSpec((MM_ROW_TILE, wd), lambda i: (i, 0)) for wd in widths]
            + [pl.BlockSpec((n, k), lambda i: (0, 0))],
            out_specs=pl.BlockSpec((MM_ROW_TILE, k), lambda i: (i, 0)),
            out_shape=jax.ShapeDtypeStruct((t, k), F32),
            compiler_params=pltpu.CompilerParams(dimension_semantics=("arbitrary",), vmem_limit_bytes=MM_VMEM_LIMIT),
        )(*dys, wt)

    def call_dw(a, dys, wt):
        t, k = a.shape
        n = wt.shape[0]

        def body(a_ref, *refs):
            dy_refs, o_ref = refs[:-1], refs[-1]
            ab = a_ref[...].astype(BF16)
            for dy_ref, s0, wd in zip(dy_refs, starts, widths):
                o_ref[s0:s0 + wd, :] = _tn(dy_ref[...].astype(BF16), ab).astype(BF16)
            if starts[-1] + widths[-1] < n:
                o_ref[starts[-1] + widths[-1]:, :] = jnp.zeros((n - starts[-1] - widths[-1], tk_w), BF16)

        return pl.pallas_call(
            body, name=name + "_dw", grid=(k // tk_w,),
            in_specs=[pl.BlockSpec((t, tk_w), lambda i: (0, i))]
            + [pl.BlockSpec((t, wd), lambda i: (0, 0)) for wd in widths],
            out_specs=pl.BlockSpec((n, tk_w), lambda i: (0, i)),
            out_shape=jax.ShapeDtypeStruct((n, k), BF16),
            compiler_params=pltpu.CompilerParams(dimension_semantics=("arbitrary",), vmem_limit_bytes=MM_VMEM_LIMIT),
        )(a, *dys)

    @jax.custom_vjp
    def op(a, wt):
        return tuple(call_fwd(a, wt))

    def fwd(a, wt):
        return op(a, wt), (a, wt)

    def bwd(res, dys):
        a, wt = res
        return call_dx(dys, wt), call_dw(a, dys, wt)

    op.defvjp(fwd, bwd)
    return op


def _make_linear_split(name, widths, tk_w):
    starts = [sum(widths[:g]) for g in range(len(widths))]

    def call_fwd(a, w):
        t, k = a.shape
        n = w.shape[1]

        def body(a_ref, w_ref, *o_refs):
            y = jnp.dot(a_ref[...].astype(BF16), w_ref[...], preferred_element_type=F32)
            for o_ref, s0, wd in zip(o_refs, starts, widths):
                o_ref[...] = y[:, s0:s0 + wd]

        return pl.pallas_call(
            body, name=name + "_fwd", grid=(t // MM_ROW_TILE,),
            in_specs=[pl.BlockSpec((MM_ROW_TILE, k), lambda i: (i, 0)), pl.BlockSpec((k, n), lambda i: (0, 0))],
            out_specs=[pl.BlockSpec((MM_ROW_TILE, wd), lambda i: (i, 0)) for wd in widths],
            out_shape=[jax.ShapeDtypeStruct((t, wd), F32) for wd in widths],
            compiler_params=pltpu.CompilerParams(dimension_semantics=("arbitrary",), vmem_limit_bytes=MM_VMEM_LIMIT),
        )(a, w)

    def call_dx(dys, w):
        t = dys[0].shape[0]
        k, n = w.shape

        def body(*refs):
            dy_refs, w_ref, o_ref = refs[:-2], refs[-2], refs[-1]
            acc = jnp.zeros((MM_ROW_TILE, k), F32)
            for dy_ref, s0, wd in zip(dy_refs, starts, widths):
                acc = acc + _nt(dy_ref[...].astype(BF16), w_ref[:, s0:s0 + wd])
            o_ref[...] = acc

        return pl.pallas_call(
            body, name=name + "_dx", grid=(t // MM_ROW_TILE,),
            in_specs=[pl.BlockSpec((MM_ROW_TILE, wd), lambda i: (i, 0)) for wd in widths]
            + [pl.BlockSpec((k, n), lambda i: (0, 0))],
            out_specs=pl.BlockSpec((MM_ROW_TILE, k), lambda i: (i, 0)),
            out_shape=jax.ShapeDtypeStruct((t, k), F32),
            compiler_params=pltpu.CompilerParams(dimension_semantics=("arbitrary",), vmem_limit_bytes=MM_VMEM_LIMIT),
        )(*dys, w)

    def call_dw(a, dys, w):
        t, k = a.shape
        n = w.shape[1]

        def body(a_ref, *refs):
            dy_refs, o_ref = refs[:-1], refs[-1]
            ab = a_ref[...].astype(BF16)
            for dy_ref, s0, wd in zip(dy_refs, starts, widths):
                o_ref[:, s0:s0 + wd] = _tn(ab, dy_ref[...].astype(BF16)).astype(BF16)
            if starts[-1] + widths[-1] < n:
                o_ref[:, starts[-1] + widths[-1]:] = jnp.zeros((tk_w, n - starts[-1] - widths[-1]), BF16)

        return pl.pallas_call(
            body, name=name + "_dw", grid=(k // tk_w,),
            in_specs=[pl.BlockSpec((t, tk_w), lambda i: (0, i))]
            + [pl.BlockSpec((t, wd), lambda i: (0, 0)) for wd in widths],
            out_specs=pl.BlockSpec((tk_w, n), lambda i: (i, 0)),
            out_shape=jax.ShapeDtypeStruct((k, n), BF16),
            compiler_params=pltpu.CompilerParams(dimension_semantics=("arbitrary",), vmem_limit_bytes=MM_VMEM_LIMIT),
        )(a, *dys)

    @jax.custom_vjp
    def op(a, w):
        return tuple(call_fwd(a, w))

    def fwd(a, w):
        return op(a, w), (a, w)

    def bwd(res, dys):
        a, w = res
        return call_dx(dys, w), call_dw(a, dys, w)

    op.defvjp(fwd, bwd)
    return op


def _row_spec(arr, tb):
    return pl.BlockSpec((tb, arr.shape[1]), lambda i: (i, 0))


def _full_spec(arr):
    return pl.BlockSpec(arr.shape, lambda i: (0, 0))


def _make_rowwise(name, f, n_rows, n_params, out_cols, diff_rows, out_dtypes=None, grad_dtypes=None):
    n_out = len(out_cols)
    out_dtypes = out_dtypes or [F32] * n_out
    grad_dtypes = grad_dtypes or [F32] * sum(diff_rows)

    def call_fwd(rows, params):
        t = rows[0].shape[0]

        def body(*refs):
            ins = [r[...] for r in refs[:n_rows + n_params]]
            outs = f(*ins)
            for o_ref, o in zip(refs[n_rows + n_params:], outs):
                o_ref[...] = o.astype(o_ref.dtype)

        return pl.pallas_call(
            body, name=name + "_fwd", grid=(t // ROW_TILE,),
            in_specs=[_row_spec(a, ROW_TILE) for a in rows] + [_full_spec(p) for p in params],
            out_specs=[pl.BlockSpec((ROW_TILE, n), lambda i: (i, 0)) for n in out_cols],
            out_shape=[jax.ShapeDtypeStruct((t, n), dt) for n, dt in zip(out_cols, out_dtypes)],
            compiler_params=pltpu.CompilerParams(dimension_semantics=("arbitrary",),
                                                 vmem_limit_bytes=MM_VMEM_LIMIT),
        )(*rows, *params)

    def call_bwd(rows, params, cts):
        t = rows[0].shape[0]
        d_rows = [a for a, d in zip(rows, diff_rows) if d]
        n_in = n_rows + n_params + n_out

        def body(*refs):
            ins = [r[...] for r in refs[:n_rows + n_params]]
            ct = tuple(r[...].astype(F32) for r in refs[n_rows + n_params:n_in])
            _, vjp = jax.vjp(f, *ins)
            grads = vjp(ct)
            out_refs = refs[n_in:]
            g_rows = [g for g, d in zip(grads[:n_rows], diff_rows) if d]
            for o_ref, g in zip(out_refs[:len(g_rows)], g_rows):
                o_ref[...] = g.astype(o_ref.dtype)
            p_refs = out_refs[len(g_rows):]

            if p_refs:
                @pl.when(pl.program_id(0) == 0)
                def _():
                    for p_ref in p_refs:
                        p_ref[...] = jnp.zeros_like(p_ref)

                for p_ref, g in zip(p_refs, grads[n_rows:]):
                    p_ref[...] += g

        return pl.pallas_call(
            body, name=name + "_bwd", grid=(t // ROW_TILE,),
            in_specs=[_row_spec(a, ROW_TILE) for a in rows] + [_full_spec(p) for p in params]
            + [_row_spec(c, ROW_TILE) for c in cts],
            out_specs=[_row_spec(a, ROW_TILE) for a in d_rows] + [_full_spec(p) for p in params],
            out_shape=[jax.ShapeDtypeStruct(a.shape, dt) for a, dt in zip(d_rows, grad_dtypes)]
            + [jax.ShapeDtypeStruct(p.shape, F32) for p in params],
            compiler_params=pltpu.CompilerParams(dimension_semantics=("arbitrary",),
                                                 vmem_limit_bytes=MM_VMEM_LIMIT),
        )(*rows, *params, *cts)

    @jax.custom_vjp
    def op(*args):
        return tuple(call_fwd(args[:n_rows], args[n_rows:]))

    def fwd(*args):
        return op(*args), args

    def bwd(args, cts):
        rows, params = args[:n_rows], args[n_rows:]
        outs = call_bwd(rows, params, cts)
        it = iter(outs)
        g_rows = [next(it) if d else jnp.zeros_like(a) for a, d in zip(rows, diff_rows)]
        return tuple(g_rows) + tuple(it)

    op.defvjp(fwd, bwd)
    return op


def _rms(x, g, n):
    return x * lax.rsqrt(jnp.sum(x * x, axis=-1, keepdims=True) * (1.0 / n) + EPS) * g


def _f_pre_attn(x, g, scale, shift):
    return _rms(x, g, D_MODEL) * (1.0 + scale) + shift, x


def _f_mla_a(cq, ckv, gq, gkv):
    return _rms(cq, gq, MLA_Q_RANK), _rms(ckv, gkv, MLA_KV_RANK)


@jax.custom_vjp
def _split_lanes(x):
    return tuple(x[:, i * LANES:(i + 1) * LANES] for i in range(x.shape[1] // LANES))


def _split_lanes_fwd(x):
    return _split_lanes(x), None


def _split_lanes_bwd(_, cts):
    return (jnp.concatenate(cts, axis=1),)


_split_lanes.defvjp(_split_lanes_fwd, _split_lanes_bwd)


def _f_mla_b(qall, kn_all, kr, kr_sw, cos, sin, gqn, gqr, gqr_sw, gkn, gkr, gkr_sw):
    q = _split_lanes(qall)
    kn = _split_lanes(kn_all)
    qn_o, qr_o, kn_o = [], [], []
    for h in range(MLA_HEADS):
        qn, qr, qs = q[h], q[MLA_HEADS + h], q[2 * MLA_HEADS + h]
        ss = jnp.sum(qn * qn, axis=-1, keepdims=True) + jnp.sum(qr * qr, axis=-1, keepdims=True)
        rs = lax.rsqrt(ss * (1.0 / MLA_QK) + EPS)
        qn_o.append(qn * rs * gqn)
        qr_o.append((qr * rs * gqr) * cos + (qs * rs * gqr_sw) * sin)
        kn_o.append(_rms(kn[h], gkn, MLA_NOPE))
    rs = lax.rsqrt(jnp.sum(kr * kr, axis=-1, keepdims=True) * (1.0 / MLA_ROPE) + EPS)
    kr_o = (kr * rs * gkr) * cos + (kr_sw * rs * gkr_sw) * sin
    return (jnp.concatenate(qn_o, axis=1), jnp.concatenate(qr_o, axis=1), jnp.concatenate(kn_o, axis=1), kr_o)


def _f_post_attn(o_sb, o_mla, g_sb, g_mla):
    return (jnp.concatenate([_rms(o_sb, g_sb, SB_WIDTH), _rms(o_mla, g_mla, SB_WIDTH)], axis=1),)


def _f_pre_ffn(x, attn, gate, g, scale, shift):
    x2 = x + gate * attn
    return x2, _rms(x2, g, D_MODEL) * (1.0 + scale) + shift


def _f_swiglu(gt, up):
    return (gt / (1.0 + jnp.exp(-gt)) * up,)


def _f_loss(x2, ffn, target, gate):
    err = x2 + gate * ffn - target
    return (jnp.sum(err * err, axis=-1, keepdims=True) * (1.0 / D_MODEL),)


def _rope_tables(pos_col, freqs, sign):
    t = pos_col.shape[0]

    def body(p_ref, f_ref, s_ref, cos_ref, sin_ref):
        ang = p_ref[...].astype(F32) * f_ref[...]
        live = jnp.abs(s_ref[...])
        cos_ref[...] = jnp.cos(ang) * live
        sin_ref[...] = jnp.sin(ang) * s_ref[...]

    return pl.pallas_call(
        body, name="rope_tables", grid=(t // ROW_TILE,),
        in_specs=[pl.BlockSpec((ROW_TILE, 1), lambda i: (i, 0)), _full_spec(freqs), _full_spec(sign)],
        out_specs=[pl.BlockSpec((ROW_TILE, LANES), lambda i: (i, 0))] * 2,
        out_shape=[jax.ShapeDtypeStruct((t, LANES), F32)] * 2,
    )(pos_col, freqs, sign)


def _hi_lo_dot(x, tri):
    hi = x.astype(BF16)
    lo = (x - hi.astype(F32)).astype(BF16)
    return (jnp.dot(hi, tri, preferred_element_type=F32) + jnp.dot(lo, tri, preferred_element_type=F32))


def _tri(cmp):
    r = lax.broadcasted_iota(jnp.int32, (ATT_BLK, ATT_BLK), 0)
    c = lax.broadcasted_iota(jnp.int32, (ATT_BLK, ATT_BLK), 1)
    return cmp(r, c).astype(BF16)


def _nt(a, b):
    return lax.dot_general(a, b, (((1,), (1,)), ((), ())), preferred_element_type=F32)


def _tn(a, b):
    return lax.dot_general(a, b, (((0,), (0,)), ((), ())), preferred_element_type=F32)


def _sb_logs(z):
    lb = jnp.minimum(z, 0.0) - jnp.log(1.0 + jnp.exp(-jnp.abs(z)))
    return lb, lb - z


def _sb_fwd(q, k, v):
    t = q.shape[0]
    nq = t // ATT_BLK
    scale = SB_HEAD_DIM ** -0.5

    def body(q_ref, k_ref, v_ref, o_ref, tot_ref):
        qi = pl.program_id(1)
        lane = lax.broadcasted_iota(jnp.int32, (ATT_BLK, LANES), 1)
        tri = _tri(lambda r, c: r > c)
        qv = q_ref[...] * scale
        heads = [(lane // SB_HEAD_DIM) == hh for hh in range(2)]
        qms = [jnp.where(mine, qv, 0.0).astype(BF16) for mine in heads]

        def blocks(kbs, carry, diagonal):
            acc = carry[0]
            nb = len(kbs)
            chains = [(b, hh) for b in range(nb) for hh in range(2)]
            offs = [pl.multiple_of(kb * ATT_BLK, ATT_BLK) for kb in kbs]
            kks = [k_ref[pl.ds(off, ATT_BLK), :].astype(BF16) for off in offs]
            v_blks = [v_ref[pl.ds(off, ATT_BLK), :] for off in offs]
            if any(diagonal):
                valid = (lax.broadcasted_iota(jnp.int32, (ATT_BLK, ATT_BLK), 1)
                         < lax.broadcasted_iota(jnp.int32, (ATT_BLK, ATT_BLK), 0))
            zs = {ch: _nt(qms[ch[1]], kks[ch[0]]) for ch in chains}
            vvs = {(b, hh): jnp.where(heads[hh], v_blks[b], 0.0).astype(BF16) for b, hh in chains}
            logs = {ch: _sb_logs(zs[ch]) for ch in chains}
            l1ms = {ch: jnp.where(valid, logs[ch][1], 0.0) if diagonal[ch[0]] else logs[ch][1] for ch in chains}
            run = {(0, hh): carry[1 + hh] for hh in range(2)}
            for b, hh in chains:
                run[(b + 1, hh)] = run[(b, hh)] + jnp.sum(l1ms[(b, hh)], axis=-1, keepdims=True)
            afters = {ch: _hi_lo_dot(l1ms[ch], tri) for ch in chains}
            ws = {ch: jnp.exp(logs[ch][0] + (afters[ch] + run[ch])) for ch in chains}
            ws = {ch: jnp.where(valid, ws[ch], 0.0) if diagonal[ch[0]] else ws[ch] for ch in chains}
            for ch in chains:
                acc = acc + jnp.dot(ws[ch].astype(BF16), vvs[ch], preferred_element_type=F32)
            return (acc, run[(nb, 0)], run[(nb, 1)])

        zero = jnp.zeros((ATT_BLK, 1), F32)
        init = (jnp.zeros((ATT_BLK, LANES), F32), zero, zero)
        carry = lax.cond(qi % 2 == 1, lambda cr: blocks([qi, qi - 1], cr, (True, False)),
                         lambda cr: blocks([qi], cr, (True,)), init)
        top = qi - 1 - qi % 2
        carry = lax.fori_loop(0, qi // 2, lambda pr, cr: blocks([top - 2 * pr, top - 1 - 2 * pr], cr, (False, False)),
                              carry)
        o_ref[...] = carry[0]
        for hh in range(2):
            tot_ref[:, hh * LANES:(hh + 1) * LANES] = jnp.broadcast_to(carry[1 + hh], (ATT_BLK, LANES))

    return pl.pallas_call(
        body, name="sb_attn_fwd", grid=(SB_HEADS // 2, nq),
        in_specs=[pl.BlockSpec((ATT_BLK, LANES), lambda p, i: (i, p)),
                  pl.BlockSpec((t, LANES), lambda p, i: (0, p)),
                  pl.BlockSpec((t, LANES), lambda p, i: (0, p))],
        out_specs=[pl.BlockSpec((ATT_BLK, LANES), lambda p, i: (i, p)),
                   pl.BlockSpec((ATT_BLK, 2 * LANES), lambda p, i: (i, p))],
        out_shape=[jax.ShapeDtypeStruct((t, SB_WIDTH), F32), jax.ShapeDtypeStruct((t, SB_HEADS * LANES), F32)],
        compiler_params=pltpu.CompilerParams(dimension_semantics=("arbitrary", "arbitrary")),
    )(q, k, v)


def _sb_bwd(q, k, v, tot, do):
    t = q.shape[0]
    nq = t // ATT_BLK
    scale = SB_HEAD_DIM ** -0.5

    def body(q_ref, k_ref, v_ref, tot_ref, do_ref, dq_ref, dk_ref, dv_ref):
        qi = pl.program_id(1)

        @pl.when(qi == 0)
        def _():
            dk_ref[...] = jnp.zeros_like(dk_ref)
            dv_ref[...] = jnp.zeros_like(dv_ref)

        lane = lax.broadcasted_iota(jnp.int32, (ATT_BLK, LANES), 1)
        tri_incl = _tri(lambda r, c: r <= c)
        tri_lt = _tri(lambda r, c: r < c)
        qv = q_ref[...] * scale
        dov = do_ref[...]
        heads = [(lane // SB_HEAD_DIM) == hh for hh in range(2)]
        qms = [jnp.where(mine, qv, 0.0).astype(BF16) for mine in heads]
        doms = [jnp.where(mine, dov, 0.0).astype(BF16) for mine in heads]
        tots = [tot_ref[:, hh * LANES:hh * LANES + 1] for hh in range(2)]

        def blocks(kbs, carry, diagonal):
            dq = carry[0]
            nb = len(kbs)
            chains = [(b, hh) for b in range(nb) for hh in range(2)]
            offs = [pl.multiple_of(kb * ATT_BLK, ATT_BLK) for kb in kbs]
            k_blks = [k_ref[pl.ds(off, ATT_BLK), :] for off in offs]
            vvs = [v_ref[pl.ds(off, ATT_BLK), :].astype(BF16) for off in offs]
            if any(diagonal):
                valid = (lax.broadcasted_iota(jnp.int32, (ATT_BLK, ATT_BLK), 1)
                         < lax.broadcasted_iota(jnp.int32, (ATT_BLK, ATT_BLK), 0))
            kks = {(b, hh): jnp.where(heads[hh], k_blks[b], 0.0).astype(BF16) for b, hh in chains}
            zs = {ch: _nt(qms[ch[1]], kks[ch]) for ch in chains}
            dws = {ch: _nt(doms[ch[1]], vvs[ch[0]]) for ch in chains}
            logs = {ch: _sb_logs(zs[ch]) for ch in chains}
            lbs = {ch: logs[ch][0] for ch in chains}
            l1m_all = {ch: logs[ch][1] for ch in chains}
            l1ms = {ch: jnp.where(valid, l1m_all[ch], 0.0) if diagonal[ch[0]] else l1m_all[ch] for ch in chains}
            pre, c_de = {}, {}
            for hh in range(2):
                pre[(0, hh)], c_de[(0, hh)] = carry[1 + 2 * hh], carry[2 + 2 * hh]
            for b, hh in chains:
                pre[(b + 1, hh)] = pre[(b, hh)] + jnp.sum(l1ms[(b, hh)], axis=-1, keepdims=True)
            prefix = {ch: _hi_lo_dot(l1ms[ch], tri_incl) for ch in chains}
            ws = {ch: jnp.exp(lbs[ch] + (tots[ch[1]] - (prefix[ch] + pre[ch]))) for ch in chains}
            ws = {ch: jnp.where(valid, ws[ch], 0.0) if diagonal[ch[0]] else ws[ch] for ch in chains}
            d_es = {ch: ws[ch] * dws[ch] for ch in chains}
            for b, hh in chains:
                c_de[(b + 1, hh)] = c_de[(b, hh)] + jnp.sum(d_es[(b, hh)], axis=-1, keepdims=True)
            dvs = [_tn(ws[(b, 0)].astype(BF16), doms[0]) + _tn(ws[(b, 1)].astype(BF16), doms[1]) for b in range(nb)]
            dl1ms = {ch: jnp.dot(d_es[ch].astype(BF16), tri_lt, preferred_element_type=F32) + c_de[ch] for ch in chains}
            dzs = {ch: d_es[ch] * jnp.exp(l1m_all[ch]) - dl1ms[ch] * jnp.exp(lbs[ch]) for ch in chains}
            dzs = {ch: jnp.where(valid, dzs[ch], 0.0) if diagonal[ch[0]] else dzs[ch] for ch in chains}
            dzs = {ch: dzs[ch].astype(BF16) for ch in chains}
            for ch in chains:
                dq = dq + jnp.dot(dzs[ch], kks[ch], preferred_element_type=F32)
            for b in range(nb):
                dk_ref[pl.ds(offs[b], ATT_BLK), :] += _tn(dzs[(b, 0)], qms[0]) + _tn(dzs[(b, 1)], qms[1])
                dv_ref[pl.ds(offs[b], ATT_BLK), :] += dvs[b]
            return (dq, pre[(nb, 0)], c_de[(nb, 0)], pre[(nb, 1)], c_de[(nb, 1)])

        zero = jnp.zeros((ATT_BLK, 1), F32)
        carry = lax.fori_loop(0, qi // 2, lambda pr, cr: blocks([2 * pr, 2 * pr + 1], cr, (False, False)),
                              (jnp.zeros((ATT_BLK, LANES), F32), zero, zero, zero, zero))
        carry = lax.cond(qi % 2 == 1, lambda cr: blocks([qi - 1, qi], cr, (False, True)),
                         lambda cr: blocks([qi], cr, (True,)), carry)
        dq_ref[...] = carry[0] * scale

    return pl.pallas_call(
        body, name="sb_attn_bwd", grid=(SB_HEADS // 2, nq),
        in_specs=[pl.BlockSpec((ATT_BLK, LANES), lambda p, i: (i, p)),
                  pl.BlockSpec((t, LANES), lambda p, i: (0, p)),
                  pl.BlockSpec((t, LANES), lambda p, i: (0, p)),
                  pl.BlockSpec((ATT_BLK, 2 * LANES), lambda p, i: (i, p)),
                  pl.BlockSpec((ATT_BLK, LANES), lambda p, i: (i, p))],
        out_specs=[pl.BlockSpec((ATT_BLK, LANES), lambda p, i: (i, p)),
                   pl.BlockSpec((t, LANES), lambda p, i: (0, p)),
                   pl.BlockSpec((t, LANES), lambda p, i: (0, p))],
        out_shape=[jax.ShapeDtypeStruct((t, SB_WIDTH), F32)] * 3,
        compiler_params=pltpu.CompilerParams(dimension_semantics=("arbitrary", "arbitrary")),
    )(q, k, v, tot, do)


@jax.custom_vjp
def _sb_attention(q, k, v):
    return _sb_fwd(q, k, v)[0]


def _sb_attention_fwd(q, k, v):
    o, tot = _sb_fwd(q, k, v)
    return o, (q, k, v, tot)


def _sb_attention_bwd(res, do):
    return tuple(_sb_bwd(*res, do))


_sb_attention.defvjp(_sb_attention_fwd, _sb_attention_bwd)


def _mla_fwd(qn, qr, kn, kr, v):
    t = qn.shape[0]
    nq = t // ATT_BLK
    scale = MLA_QK ** -0.5

    def body(qn_ref, qr_ref, kn_ref, kr_ref, v_ref, o_ref, lse_ref):
        qi = pl.program_id(1)
        lanes = [slice(hh * LANES, (hh + 1) * LANES) for hh in range(2)]
        qnb = [qn_ref[:, sl].astype(BF16) for sl in lanes]
        qrb = [qr_ref[:, sl].astype(BF16) for sl in lanes]

        def blocks(kbs, carry, diagonal):
            nb = len(kbs)
            chains = [(b, hh) for b in range(nb) for hh in range(2)]
            offs = [pl.multiple_of(kb * ATT_BLK, ATT_BLK) for kb in kbs]
            krbs = [kr_ref[pl.ds(off, ATT_BLK), :].astype(BF16) for off in offs]
            accs, ms, ls = [carry[0], carry[3]], [carry[1], carry[4]], [carry[2], carry[5]]
            ss = {(b, hh): (_nt(qnb[hh], kn_ref[pl.ds(offs[b], ATT_BLK), lanes[hh]].astype(BF16))
                            + _nt(qrb[hh], krbs[b])) * scale for b, hh in chains}
            if any(diagonal):
                causal = (lax.broadcasted_iota(jnp.int32, (ATT_BLK, ATT_BLK), 1)
                          <= lax.broadcasted_iota(jnp.int32, (ATT_BLK, ATT_BLK), 0))
                ss = {ch: jnp.where(causal, ss[ch], -jnp.inf) if diagonal[ch[0]] else ss[ch] for ch in chains}
            m_new = list(ms)
            for b, hh in chains:
                m_new[hh] = jnp.maximum(m_new[hh], jnp.max(ss[(b, hh)], axis=-1, keepdims=True))
            ps = {(b, hh): jnp.exp(ss[(b, hh)] - m_new[hh]) for b, hh in chains}
            alphas = [jnp.exp(ms[hh] - m_new[hh]) for hh in range(2)]
            pvs = {(b, hh): jnp.dot(ps[(b, hh)].astype(BF16), v_ref[pl.ds(offs[b], ATT_BLK), lanes[hh]].astype(BF16),
                                    preferred_element_type=F32) for b, hh in chains}
            out = []
            for hh in range(2):
                acc, l = accs[hh] * alphas[hh], ls[hh] * alphas[hh]
                for b in range(nb):
                    acc, l = acc + pvs[(b, hh)], l + jnp.sum(ps[(b, hh)], axis=-1, keepdims=True)
                out += [acc, m_new[hh], l]
            return tuple(out)

        init = (jnp.zeros((ATT_BLK, LANES), F32), jnp.full((ATT_BLK, 1), -jnp.inf, F32), jnp.zeros((ATT_BLK, 1), F32))
        carry = lax.cond(qi % 2 == 1, lambda cr: blocks([qi, qi - 1], cr, (True, False)),
                         lambda cr: blocks([qi], cr, (True,)), init + init)
        carry = lax.fori_loop(0, qi // 2, lambda pr, cr: blocks([2 * pr, 2 * pr + 1], cr, (False, False)), carry)
        for hh in range(2):
            acc, m, l = carry[3 * hh:3 * hh + 3]
            o_ref[:, lanes[hh]] = acc / l
            lse_ref[:, lanes[hh]] = jnp.broadcast_to(m + jnp.log(l), (ATT_BLK, LANES))

    blk = pl.BlockSpec((ATT_BLK, 2 * LANES), lambda p, i: (i, p))
    full = pl.BlockSpec((t, 2 * LANES), lambda p, i: (0, p))
    return pl.pallas_call(
        body, name="mla_attn_fwd", grid=(MLA_HEADS // 2, nq),
        in_specs=[blk, blk, full, pl.BlockSpec((t, LANES), lambda p, i: (0, 0)), full],
        out_specs=[blk, blk],
        out_shape=[jax.ShapeDtypeStruct((t, MLA_HEADS * LANES), F32)] * 2,
        compiler_params=pltpu.CompilerParams(dimension_semantics=("arbitrary", "arbitrary")),
    )(qn, qr, kn, kr, v)


def _mla_bwd(qn, qr, kn, kr, v, o, lse, do):
    t = qn.shape[0]
    nq = t // ATT_BLK
    scale = MLA_QK ** -0.5

    def body(qn_ref, qr_ref, kn_ref, kr_ref, v_ref, o_ref, lse_ref, do_ref,
             dqn_ref, dqr_ref, dkn_ref, dkr_ref, dv_ref):
        pair = pl.program_id(0)
        qi = pl.program_id(1)

        @pl.when(qi == 0)
        def _():
            dkn_ref[...] = jnp.zeros_like(dkn_ref)
            dv_ref[...] = jnp.zeros_like(dv_ref)

        @pl.when((qi == 0) & (pair == 0))
        def _():
            dkr_ref[...] = jnp.zeros_like(dkr_ref)

        lanes = [slice(hh * LANES, (hh + 1) * LANES) for hh in range(2)]
        qnb = [qn_ref[:, sl].astype(BF16) for sl in lanes]
        qrb = [qr_ref[:, sl].astype(BF16) for sl in lanes]
        dob = [do_ref[:, sl].astype(BF16) for sl in lanes]
        delta = [jnp.sum(do_ref[:, sl] * o_ref[:, sl], axis=-1, keepdims=True) for sl in lanes]
        lse_v = [lse_ref[:, hh * LANES:hh * LANES + 1] for hh in range(2)]

        def blocks(kbs, carry, diagonal):
            nb = len(kbs)
            chains = [(b, hh) for b in range(nb) for hh in range(2)]
            offs = [pl.multiple_of(kb * ATT_BLK, ATT_BLK) for kb in kbs]
            krbs = [kr_ref[pl.ds(off, ATT_BLK), :].astype(BF16) for off in offs]
            knb = {(b, hh): kn_ref[pl.ds(offs[b], ATT_BLK), lanes[hh]].astype(BF16) for b, hh in chains}
            vb = {(b, hh): v_ref[pl.ds(offs[b], ATT_BLK), lanes[hh]].astype(BF16) for b, hh in chains}
            ss = {(b, hh): _nt(qnb[hh], knb[(b, hh)]) + _nt(qrb[hh], krbs[b]) for b, hh in chains}
            dps = {(b, hh): _nt(dob[hh], vb[(b, hh)]) for b, hh in chains}
            ps = {(b, hh): jnp.exp(ss[(b, hh)] * scale - lse_v[hh]) for b, hh in chains}
            if any(diagonal):
                causal = (lax.broadcasted_iota(jnp.int32, (ATT_BLK, ATT_BLK), 1)
                          <= lax.broadcasted_iota(jnp.int32, (ATT_BLK, ATT_BLK), 0))
                ps = {ch: jnp.where(causal, ps[ch], 0.0) if diagonal[ch[0]] else ps[ch] for ch in chains}
            dss = {(b, hh): (ps[(b, hh)] * (dps[(b, hh)] - delta[hh]) * scale).astype(BF16) for b, hh in chains}
            for b, hh in chains:
                dv_ref[pl.ds(offs[b], ATT_BLK), lanes[hh]] += _tn(ps[(b, hh)].astype(BF16), dob[hh])
            for b, hh in chains:
                dkn_ref[pl.ds(offs[b], ATT_BLK), lanes[hh]] += _tn(dss[(b, hh)], qnb[hh])
            for b in range(nb):
                dkr_ref[pl.ds(offs[b], ATT_BLK), :] += _tn(dss[(b, 0)], qrb[0]) + _tn(dss[(b, 1)], qrb[1])
            out = list(carry)
            for b, hh in chains:
                out[2 * hh] = out[2 * hh] + jnp.dot(dss[(b, hh)], knb[(b, hh)], preferred_element_type=F32)
                out[2 * hh + 1] = out[2 * hh + 1] + jnp.dot(dss[(b, hh)], krbs[b], preferred_element_type=F32)
            return tuple(out)

        zero = jnp.zeros((ATT_BLK, LANES), F32)
        carry = lax.fori_loop(0, qi // 2, lambda pr, cr: blocks([2 * pr, 2 * pr + 1], cr, (False, False)),
                              (zero, zero, zero, zero))
        carry = lax.cond(qi % 2 == 1, lambda cr: blocks([qi - 1, qi], cr, (False, True)),
                         lambda cr: blocks([qi], cr, (True,)), carry)
        for hh in range(2):
            dqn_ref[:, lanes[hh]] = carry[2 * hh]
            dqr_ref[:, lanes[hh]] = carry[2 * hh + 1]

    blk = pl.BlockSpec((ATT_BLK, 2 * LANES), lambda p, i: (i, p))
    full = pl.BlockSpec((t, 2 * LANES), lambda p, i: (0, p))
    shared = pl.BlockSpec((t, LANES), lambda p, i: (0, 0))
    wide = jax.ShapeDtypeStruct((t, MLA_HEADS * LANES), F32)
    return pl.pallas_call(
        body, name="mla_attn_bwd", grid=(MLA_HEADS // 2, nq),
        in_specs=[blk, blk, full, shared, full, blk, blk, blk],
        out_specs=[blk, blk, full, shared, full],
        out_shape=[wide, wide, wide, jax.ShapeDtypeStruct((t, LANES), F32), wide],
        compiler_params=pltpu.CompilerParams(dimension_semantics=("arbitrary", "arbitrary")),
    )(qn, qr, kn, kr, v, o, lse, do)


@jax.custom_vjp
def _mla_attention(qn, qr, kn, kr, v):
    return _mla_fwd(qn, qr, kn, kr, v)[0]


def _mla_attention_fwd(qn, qr, kn, kr, v):
    o, lse = _mla_fwd(qn, qr, kn, kr, v)
    return o, (qn, qr, kn, kr, v, o, lse)


def _mla_attention_bwd(res, do):
    return tuple(_mla_bwd(*res, do))


_mla_attention.defvjp(_mla_attention_fwd, _mla_attention_bwd)


def _ffn_in(h, wg, wu):
    t, k = h.shape
    n_sh, cc, _ = wg.shape

    def body(h_ref, wg_ref, wu_ref, g_ref, u_ref, a_ref):
        hb = h_ref[...].astype(BF16)
        for j in range(n_sh):
            cols = slice(j * cc, (j + 1) * cc)
            g = _nt(hb, wg_ref[j])
            u = _nt(hb, wu_ref[j])
            g_ref[:, cols] = g.astype(BF16)
            u_ref[:, cols] = u.astype(BF16)
            a_ref[:, cols] = _f_swiglu(g, u)[0].astype(BF16)

    w_spec = pl.BlockSpec((n_sh, cc, k), lambda i: (0, 0, 0))
    o_spec = pl.BlockSpec((MM_ROW_TILE, n_sh * cc), lambda i: (i, 0))
    wide = jax.ShapeDtypeStruct((t, n_sh * cc), BF16)
    return pl.pallas_call(
        body, name="ffn_in_fwd", grid=(t // MM_ROW_TILE,),
        in_specs=[pl.BlockSpec((MM_ROW_TILE, k), lambda i: (i, 0)), w_spec, w_spec],
        out_specs=[o_spec, o_spec, o_spec],
        out_shape=[wide, wide, wide],
        compiler_params=pltpu.CompilerParams(dimension_semantics=("arbitrary",), vmem_limit_bytes=MM_VMEM_LIMIT),
    )(h, wg, wu)


def _ffn_mid_bwd(dy, wd, g, u):
    t, n = dy.shape
    n_sh, cc, _ = wd.shape

    def body(dy_ref, wd_ref, g_ref, u_ref, dg_ref, du_ref):
        d_act = _nt(dy_ref[...].astype(BF16), wd_ref[...])
        g = g_ref[...].astype(F32)
        sig = 1.0 / (1.0 + jnp.exp(-g))
        dg_ref[...] = (d_act * u_ref[...].astype(F32) * (sig * (1.0 + g * (1.0 - sig)))).astype(BF16)
        du_ref[...] = (d_act * (g * sig)).astype(BF16)

    blk = pl.BlockSpec((MM_ROW_TILE, cc), lambda j, i: (i, j))
    wide = jax.ShapeDtypeStruct((t, n_sh * cc), BF16)
    return pl.pallas_call(
        body, name="ffn_mid_bwd", grid=(n_sh, t // MM_ROW_TILE),
        in_specs=[pl.BlockSpec((MM_ROW_TILE, n), lambda j, i: (i, 0)),
                  pl.BlockSpec((None, cc, n), lambda j, i: (j, 0, 0)), blk, blk],
        out_specs=[blk, blk], out_shape=[wide, wide],
        compiler_params=pltpu.CompilerParams(dimension_semantics=("arbitrary", "arbitrary"),
                                             vmem_limit_bytes=MM_VMEM_LIMIT),
    )(dy, wd, g, u)


def _ffn_dh(dg, du, wg, wu):
    t = dg.shape[0]
    n_sh, cc, k = wg.shape

    def body(dg_ref, du_ref, wg_ref, wu_ref, o_ref):
        acc = jnp.zeros((MM_ROW_TILE, k), F32)
        for j in range(n_sh):
            cols = slice(j * cc, (j + 1) * cc)
            acc = (acc + jnp.dot(dg_ref[:, cols], wg_ref[j], preferred_element_type=F32)
                   + jnp.dot(du_ref[:, cols], wu_ref[j], preferred_element_type=F32))
        o_ref[...] = acc

    blk = pl.BlockSpec((MM_ROW_TILE, n_sh * cc), lambda i: (i, 0))
    w_spec = pl.BlockSpec((n_sh, cc, k), lambda i: (0, 0, 0))
    return pl.pallas_call(
        body, name="ffn_dh", grid=(t // MM_ROW_TILE,),
        in_specs=[blk, blk, w_spec, w_spec],
        out_specs=pl.BlockSpec((MM_ROW_TILE, k), lambda i: (i, 0)),
        out_shape=jax.ShapeDtypeStruct((t, k), F32),
        compiler_params=pltpu.CompilerParams(dimension_semantics=("arbitrary",), vmem_limit_bytes=MM_VMEM_LIMIT),
    )(dg, du, wg, wu)


def _ffn_dw_in(h, dg, du, n_sh):
    t, k = h.shape
    cc = dg.shape[1] // n_sh
    tk = 512

    def body(h_ref, dg_ref, du_ref, og_ref, ou_ref):
        hb = h_ref[...].astype(BF16)
        og_ref[...] = _tn(dg_ref[...], hb).astype(BF16)
        ou_ref[...] = _tn(du_ref[...], hb).astype(BF16)

    d_spec = pl.BlockSpec((t, cc), lambda i, j: (0, j))
    o_spec = pl.BlockSpec((None, cc, tk), lambda i, j: (j, 0, i))
    out = jax.ShapeDtypeStruct((n_sh, cc, k), BF16)
    return pl.pallas_call(
        body, name="ffn_gate_up_dw", grid=(k // tk, n_sh),
        in_specs=[pl.BlockSpec((t, tk), lambda i, j: (0, i)), d_spec, d_spec],
        out_specs=[o_spec, o_spec], out_shape=[out, out],
        compiler_params=pltpu.CompilerParams(dimension_semantics=("arbitrary", "arbitrary"),
                                             vmem_limit_bytes=MM_VMEM_LIMIT),
    )(h, dg, du)


@jax.custom_vjp
def _ffn_block(h, wg, wu, wd):
    act = _ffn_in(h, wg, wu)[2]
    return _mm(act, wd.reshape(-1, wd.shape[2]), "nn", "ffn_down_fwd", MM_ROW_TILE, wd.shape[2])


def _ffn_block_fwd(h, wg, wu, wd):
    g, u, act = _ffn_in(h, wg, wu)
    y = _mm(act, wd.reshape(-1, wd.shape[2]), "nn", "ffn_down_fwd", MM_ROW_TILE, wd.shape[2])
    return y, (h, wg, wu, wd, g, u, act)


def _ffn_block_bwd(res, dy):
    h, wg, wu, wd, g, u, act = res
    dg, du = _ffn_mid_bwd(dy, wd, g, u)
    dh = _ffn_dh(dg, du, wg, wu)
    n_sh = wg.shape[0]
    dwg, dwu = _ffn_dw_in(h, dg, du, n_sh)
    dwd = _mm(act, dy, "tn", "ffn_down_dw", 256, wd.shape[2], out_dtype=BF16).reshape(wd.shape)
    return dh, dwg, dwu, dwd


_ffn_block.defvjp(_ffn_block_fwd, _ffn_block_bwd)


def _swap_halves(w):
    half = w.shape[-1] // 2
    return jnp.concatenate([w[..., half:], w[..., :half]], axis=-1)


def _pad_lanes(w):
    return jnp.concatenate([w, jnp.zeros(w.shape[:-1] + (LANES - w.shape[-1],), w.dtype)], axis=-1)


def _join_cols(shards):
    return shards.transpose(1, 0, 2).reshape(shards.shape[1], -1)


def _mod_parts(mod):
    return [mod[:, i * D_MODEL:(i + 1) * D_MODEL] for i in range(N_MOD)]


def _mixing_stage(x, mod, p, cos, sin):
    shift1, scale1 = _mod_parts(mod)[:2]

    w_in_t = p["w_in"].reshape(-1, D_MODEL)
    k_rope_rows = w_in_t[2176:2240]

    def pad_rows(a):
        return jnp.concatenate([a, jnp.zeros((LANES - a.shape[0], D_MODEL), a.dtype)], axis=0)

    swapped = jnp.concatenate([k_rope_rows[MLA_ROPE // 2:], k_rope_rows[:MLA_ROPE // 2]], axis=0)
    w_in_ext = jnp.concatenate([w_in_t[:2176], pad_rows(k_rope_rows), pad_rows(swapped),
                                jnp.zeros((LANES, D_MODEL), w_in_t.dtype)], axis=0)
    h1, x_res = _make_rowwise("pre_attn", _f_pre_attn, 1, 3, [D_MODEL, D_MODEL], [True], out_dtypes=[BF16, F32])(
        x, p["norm_attn"], scale1, shift1)
    q_sb, k_sb, v_sb, cq, ckv, kr, kr_sw = _make_linear_split_t(
        "in_proj", (SB_WIDTH, SB_WIDTH, SB_WIDTH, MLA_Q_RANK, MLA_KV_RANK, LANES, LANES), 512)(h1, w_in_ext)

    o_sb = _sb_attention(q_sb, k_sb, v_sb)

    wq = _join_cols(p["w_q_up"]).reshape(MLA_Q_RANK, MLA_HEADS, MLA_QK)
    wq_n, wq_r = wq[:, :, :MLA_NOPE], wq[:, :, MLA_NOPE:]
    w_q_ext = jnp.concatenate([wq_n.reshape(MLA_Q_RANK, -1), _pad_lanes(wq_r).reshape(MLA_Q_RANK, -1),
                               _pad_lanes(_swap_halves(wq_r)).reshape(MLA_Q_RANK, -1)], axis=1)
    wkv = _join_cols(p["w_kv_up"]).reshape(MLA_KV_RANK, MLA_HEADS, MLA_NOPE + MLA_V)
    w_kv_ext = jnp.concatenate([wkv[:, :, :MLA_NOPE].reshape(MLA_KV_RANK, -1),
                                wkv[:, :, MLA_NOPE:].reshape(MLA_KV_RANK, -1)], axis=1)
    cqn, ckvn = _make_rowwise("mla_a", _f_mla_a, 2, 2, [MLA_Q_RANK, MLA_KV_RANK], [True, True],
                              out_dtypes=[BF16, BF16], grad_dtypes=[BF16, BF16])(
        cq, ckv, p["q_a_norm"], p["kv_a_norm"])
    qall = _make_linear("q_up", 384, 768)(cqn, w_q_ext)
    kn_all, v_mla = _make_linear_split("kv_up", (MLA_HEADS * MLA_NOPE, MLA_HEADS * MLA_V), MLA_KV_RANK)(ckvn, w_kv_ext)
    gq = p["q_norm"]
    gkr = p["k_rope_norm"]
    qn, qr, kn, krr = _make_rowwise("mla_b", _f_mla_b, 6, 6, [512, 512, 512, LANES],
                                    [True, True, True, True, False, False],
                                    out_dtypes=[BF16] * 4, grad_dtypes=[BF16] * 4)(
        qall, kn_all, kr, kr_sw, cos, sin,
        gq[:, :MLA_NOPE], _pad_lanes(gq[:, MLA_NOPE:]), _pad_lanes(_swap_halves(gq[:, MLA_NOPE:])),
        p["k_nope_norm"], _pad_lanes(gkr), _pad_lanes(_swap_halves(gkr)))
    o_mla = _mla_attention(qn, qr, kn, krr, v_mla)

    (mixed,) = _make_rowwise("post_attn", _f_post_attn, 2, 2, [D_MODEL], [True, True])(
        o_sb, o_mla, p["out_norm_sb"], p["out_norm_mla"])
    return mixed, x_res


def _ffn_stage(x, mixed, mod, p):
    _, _, gate1, shift2, scale2, _ = _mod_parts(mod)
    attn = _make_linear("out_proj", 512, 512)(mixed, p["w_out"].reshape(D_MODEL, D_MODEL))

    x2, h2 = _make_rowwise("pre_ffn", _f_pre_ffn, 2, 4, [D_MODEL, D_MODEL], [True, True],
                           out_dtypes=[F32, BF16], grad_dtypes=[F32, BF16])(
        x, attn, gate1, p["norm_ffn"], scale2, shift2)
    return x2, _ffn_block(h2, p["w_gate"], p["w_up"], p["w_down"])


def _my_place():
    return lax.axis_index("x"), lax.axis_index("y"), lax.axis_index("c")


def _small_gather(x_ref, out_ref, send_sems, recv_sems, base, local_sem):
    m_per = x_ref.shape[0]
    x, y, c = _my_place()
    me, sibling = (x, y, c), (x, y, 1 - c)
    chips = [(1 - x, y), (x, 1 - y), (1 - x, 1 - y)]

    def rows(px, py, pc):
        return out_ref.at[pl.ds((4 * px + 2 * py + pc) * m_per, m_per), :]

    def copy(k, blk, to, src=None):
        return _remote(rows(*blk) if src is None else src, rows(*blk), send_sems, recv_sems, base + k, to)

    mine = pltpu.make_async_copy(x_ref, rows(*me), local_sem)
    first = [copy(0, me, sibling, src=x_ref)] + [copy(1 + j, me, (*chip, c), src=x_ref) for j, chip in enumerate(chips)]
    passed = [copy(4 + j, (*chip, c), sibling) for j, chip in enumerate(chips)]

    def start():
        mine.start()
        for cp in first:
            cp.start()

    def finish():
        for j, chip in enumerate(chips):
            copy(1 + j, (*chip, c), me).wait_recv()
            passed[j].start()
        copy(0, sibling, me).wait_recv()
        for j, chip in enumerate(chips):
            copy(4 + j, (*chip, 1 - c), me).wait_recv()
        for cp in first + passed:
            cp.wait_send()
        mine.wait()

    return start, finish


EARLY =("w_in", "w_q_up", "w_kv_up")
LATE = ("w_out", "w_gate", "w_up", "w_down")
BIG = EARLY + LATE
TRANSPOSED_UPDATE = ("w_in", "w_gate", "w_up")
HALF_AXIS = {"w_in": 1, "w_q_up": 0, "w_kv_up": 0, "w_out": 0, "w_gate": 1, "w_up": 1, "w_down": 1}
TRAVELS_TRANSPOSED = ("w_in", "w_gate", "w_up")


def _half(ref, h, axis, lead=()):
    trail = ref.shape[len(lead):]
    idx = list(lead) + [slice(None)] * len(trail)
    at = len(trail) - 2 + axis
    n2 = trail[at] // 2
    idx[len(lead) + at] = pl.ds(h * n2, n2)
    return ref.at[tuple(idx)]


def _half_shape(shape, axis):
    shape = list(shape)
    shape[len(shape) - 2 + axis] //= 2
    return tuple(shape)


def _remote(src, dst, send_sems, recv_sems, k, to):
    return pltpu.make_async_remote_copy(src_ref=src, dst_ref=dst, send_sem=send_sems.at[k],
                                        recv_sem=recv_sems.at[k], device_id=to, device_id_type=MESH)


def _gather_weights(names, shards, small_block):
    n_w = len(shards)
    axes = [HALF_AXIS[n] for n in names]

    def body(*refs):
        w_refs, small_ref = refs[:n_w], refs[n_w]
        out_refs, token, small_out = refs[n_w + 1:2 * n_w + 1], refs[2 * n_w + 1], refs[2 * n_w + 2]
        send_sems, recv_sems, local_sems = refs[2 * n_w + 3:]
        token[...] = jnp.zeros_like(token)
        x, y, c = _my_place()
        sibling = (x, y, 1 - c)
        chips = [(1 - x, y), (x, 1 - y), (1 - x, 1 - y)]
        me = 2 * x + y
        small_start, small_finish = _small_gather(small_ref, small_out, send_sems, recv_sems, 6 * n_w,
                                                  local_sems.at[n_w])
        small_start()
        mine = [pltpu.make_async_copy(w, o.at[me], local_sems.at[i]) for i, (w, o) in enumerate(zip(w_refs, out_refs))]
        for cp in mine:
            cp.start()
        first = [_remote(_half(w_refs[i], c, axes[i]), _half(out_refs[i], c, axes[i], (me,)),
                         send_sems, recv_sems, 6 * i + j, (*chip, c))
                 for i in range(n_w) for j, chip in enumerate(chips)]
        for cp in first:
            cp.start()
        small_finish()
        passed = []
        for j, (cx, cy) in enumerate(chips):
            for i in range(n_w):
                blk = _half(out_refs[i], c, axes[i], (2 * cx + cy,))
                _remote(blk, blk, send_sems, recv_sems, 6 * i + j, (cx, cy, c)).wait_recv()
                cp = _remote(blk, blk, send_sems, recv_sems, 6 * i + 3 + j, sibling)
                cp.start()
                passed.append(cp)
        for j, (cx, cy) in enumerate(chips):
            for i in range(n_w):
                blk = _half(out_refs[i], 1 - c, axes[i], (2 * cx + cy,))
                _remote(blk, blk, send_sems, recv_sems, 6 * i + 3 + j, sibling).wait_recv()
        for cp in first + passed:
            cp.wait_send()
        for cp in mine:
            cp.wait()

    outs = pl.pallas_call(
        body, name="gather_weights",
        out_shape=[jax.ShapeDtypeStruct((N_CHIPS,) + s.shape, s.dtype) for s in shards]
        + [jax.ShapeDtypeStruct((8, LANES), F32),
           jax.ShapeDtypeStruct((N_DEV * small_block.shape[0], small_block.shape[1]), small_block.dtype)],
        in_specs=[ANY] * (n_w + 1), out_specs=[ANY] * n_w + [pl.BlockSpec(memory_space=pltpu.VMEM), ANY],
        scratch_shapes=[pltpu.SemaphoreType.DMA((6 * n_w + 7,)), pltpu.SemaphoreType.DMA((6 * n_w + 7,)),
                        pltpu.SemaphoreType.DMA((n_w + 1,))],
    )(*shards, small_block)
    return outs[:n_w], outs[n_w], outs[n_w + 1]


def _pair_exchange(names, grads, call_name, small_block):
    n_w = len(grads)
    axes = [HALF_AXIS[n] for n in names]

    def body(*refs):
        g_refs, small_ref = refs[:n_w], refs[n_w]
        t_refs, small_out = refs[n_w + 1:2 * n_w + 1], refs[2 * n_w + 1]
        send_sems, recv_sems, local_sem = refs[2 * n_w + 2:]
        x, y, c = _my_place()
        small_start, small_finish = _small_gather(small_ref, small_out, send_sems, recv_sems, n_w, local_sem)
        small_start()
        sends = [_remote(_half(g_refs[i], 1 - c, axes[i]), t_refs[i], send_sems, recv_sems, i, (x, y, 1 - c))
                 for i in range(n_w)]
        for cp in sends:
            cp.start()
        small_finish()
        for cp in sends:
            cp.wait_recv()
        for cp in sends:
            cp.wait_send()

    outs = pl.pallas_call(
        body, name=call_name,
        out_shape=[jax.ShapeDtypeStruct(_half_shape(g.shape, a), g.dtype) for g, a in zip(grads, axes)]
        + [jax.ShapeDtypeStruct((N_DEV * small_block.shape[0], small_block.shape[1]), small_block.dtype)],
        in_specs=[ANY] * (n_w + 1), out_specs=[ANY] * (n_w + 1),
        scratch_shapes=[pltpu.SemaphoreType.DMA((n_w + 7,)), pltpu.SemaphoreType.DMA((n_w + 7,)),
                        pltpu.SemaphoreType.DMA],
    )(*grads, small_block)
    return outs[:n_w], outs[n_w]


def _sibling_join(halves, name, after):
    n_w = len(halves)

    def body(*refs):
        s_refs, j_refs = refs[:n_w], refs[n_w + 1:2 * n_w + 1]
        send_sems, recv_sems = refs[2 * n_w + 1:]
        x, y, c = _my_place()
        sends = [_remote(s_refs[i], j_refs[i], send_sems, recv_sems, i, (x, y, 1 - c)) for i in range(n_w)]
        for cp in sends:
            cp.start()
        for cp in sends:
            cp.wait_recv()
        for cp in sends:
            cp.wait_send()

    return pl.pallas_call(
        body, name=name,
        out_shape=[jax.ShapeDtypeStruct(s.shape, s.dtype) for s in halves],
        in_specs=[ANY] * (n_w + 1), out_specs=[ANY] * n_w,
        scratch_shapes=[pltpu.SemaphoreType.DMA((n_w,)), pltpu.SemaphoreType.DMA((n_w,))],
    )(*halves, after)


HBM_SPEC = pl.BlockSpec(memory_space=pltpu.HBM)
SEM_SPEC = pl.BlockSpec(memory_space=pltpu.SEMAPHORE)
DATAFLOW = pltpu.SideEffectType.DATAFLOW_SIDE_EFFECTING


def _in_hbm(a):
    return pltpu.with_memory_space_constraint(a, pltpu.HBM)


def _exchange_start(name, srcs, lands, plan, n_copies, after, thru):
    n = len(srcs)

    def body(*refs):
        src_refs, land_refs = refs[:n], refs[n:2 * n]
        send_sems, recv_sems = refs[2 * n + 2], refs[2 * n + 3]
        for k, (src, dst, to, k_recv) in enumerate(plan(src_refs, land_refs)):
            pltpu.make_async_remote_copy(src_ref=src, dst_ref=dst, send_sem=send_sems.at[k],
                                         recv_sem=recv_sems.at[k_recv], device_id=to, device_id_type=MESH).start()

    outs = pl.pallas_call(
        body, name=name,
        out_shape=(pltpu.SemaphoreType.DMA((n_copies,)), pltpu.SemaphoreType.DMA((n_copies,)),
                   *[pltpu.HBM(a.shape, a.dtype) for a in list(srcs) + list(lands) + [thru]]),
        in_specs=[HBM_SPEC] * (2 * n + 1) + [ANY],
        out_specs=(SEM_SPEC, SEM_SPEC, *[HBM_SPEC] * (2 * n + 1)),
        input_output_aliases={i: 2 + i for i in range(2 * n + 1)},
        compiler_params=pltpu.CompilerParams(has_side_effects=DATAFLOW),
    )(*[_in_hbm(a) for a in list(srcs) + list(lands) + [thru]], after)
    return outs[0], outs[1], outs[2:2 + n], outs[2 + n:2 + 2 * n], outs[2 + 2 * n]


def _exchange_wait(name, started, plan, after):
    send_sems, recv_sems, srcs, lands, _ = started
    n = len(srcs)

    def body(*refs):
        src_refs, land_refs = refs[:n], refs[n:2 * n]
        s_sems, r_sems = refs[2 * n], refs[2 * n + 1]
        for k, (src, dst, to, _) in enumerate(plan(src_refs, land_refs)):
            cp = _remote(src, dst, s_sems, r_sems, k, to)
            cp.wait_send()
            cp.wait_recv()

    outs = pl.pallas_call(
        body, name=name,
        out_shape=tuple(pltpu.HBM(a.shape, a.dtype) for a in list(srcs) + list(lands)),
        in_specs=[HBM_SPEC] * (2 * n) + [SEM_SPEC, SEM_SPEC, ANY],
        out_specs=tuple([HBM_SPEC] * (2 * n)),
        input_output_aliases={i: i for i in range(2 * n)},
        compiler_params=pltpu.CompilerParams(has_side_effects=DATAFLOW),
    )(*srcs, *lands, send_sems, recv_sems, after)
    return outs[:n], outs[n:]


def _late_gather_plan(src_refs, land_refs):
    x, y, c = _my_place()
    chips = [(1 - x, y), (x, 1 - y), (1 - x, 1 - y)]
    plan = [(src, land.at[2 * x + y], (cx, cy, c)) for src, land in zip(src_refs, land_refs) for cx, cy in chips]
    return [entry + (k,) for k, entry in enumerate(plan)]


def _late_scatter_plan(src_refs, land_refs):
    x, y, c = _my_place()
    chips = [(1 - x, y), (x, 1 - y), (1 - x, 1 - y)]
    plan = [(src.at[2 * cx + cy], land.at[j], (cx, cy, c))
            for src, land in zip(src_refs, land_refs) for j, (cx, cy) in enumerate(chips)]
    return [entry + (k,) for k, entry in enumerate(plan)]


def _direct_scatter_plan(names):
    axes = [HALF_AXIS[n] for n in names]

    def plan(src_refs, land_refs):
        x, y, c = _my_place()
        chips = [(1 - x, y), (x, 1 - y), (1 - x, 1 - y)]
        out = []
        for i, (src, land) in enumerate(zip(src_refs, land_refs)):
            for f, (cx, cy) in enumerate(chips):
                for core in range(2):
                    out.append((_half(src, core, axes[i], (2 * cx + cy,)), land.at[2 * f + c], (cx, cy, core),
                                7 * i + 2 * f + c))
            out.append((_half(src, 1 - c, axes[i], (2 * x + y,)), land.at[6], (x, y, 1 - c), 7 * i + 6))
        return out

    return plan


def _row_tile(rows, mult=16, limit=ROW_TILE):
    return max(d for d in range(mult, limit + 1, mult) if rows % d == 0)


def _pair_sum(place, g, theirs, axis, name):
    nj, rr, cc = theirs.shape
    tr = _row_tile(rr, limit=1024)
    nb = rr // tr
    if axis == 0:
        g_map = lambda j, i, pr: (j, pr[0] * nb + i, 0)
    else:
        g_map = lambda j, i, pr: (j, i, pr[0])

    def body(pr, g_ref, t_ref, o_ref):
        o_ref[...] = (g_ref[...].astype(F32) + t_ref[...].astype(F32)).astype(BF16)

    spec = pl.BlockSpec((None, tr, cc), lambda j, i, pr: (j, i, 0))
    return pl.pallas_call(
        body, name=name,
        grid_spec=pltpu.PrefetchScalarGridSpec(
            num_scalar_prefetch=1, grid=(nj, nb),
            in_specs=[pl.BlockSpec((None, tr, cc), g_map), spec], out_specs=spec),
        out_shape=jax.ShapeDtypeStruct(theirs.shape, BF16))(place, g, theirs)


def _chip_sum(place, pair_sums, parts, name, transposed):
    _, rr, cc = parts.shape
    tr = _row_tile(rr, LANES) if transposed else _row_tile(rr, limit=1024)

    def body(pr, h_ref, p_ref, o_ref):
        acc = p_ref[0].astype(F32)
        for j in range(1, N_CHIPS - 1):
            acc = acc + p_ref[j].astype(F32)
        acc = acc + h_ref[...].astype(F32)
        o_ref[...] = (acc.T if transposed else acc).astype(BF16)

    out_spec = pl.BlockSpec((cc, tr), lambda i, pr: (0, i)) if transposed else pl.BlockSpec((tr, cc), lambda i, pr: (i, 0))
    return pl.pallas_call(
        body, name=name,
        grid_spec=pltpu.PrefetchScalarGridSpec(
            num_scalar_prefetch=1, grid=(rr // tr,),
            in_specs=[pl.BlockSpec((None, tr, cc), lambda i, pr: (pr[1], i, 0)),
                      pl.BlockSpec((N_CHIPS - 1, tr, cc), lambda i, pr: (0, i, 0))],
            out_specs=out_spec),
        out_shape=jax.ShapeDtypeStruct((cc, rr) if transposed else (rr, cc), BF16))(place, pair_sums, parts)


def _chip_sum_direct(place, g, parts, axis, name, transposed):
    n_parts, rr, cc = parts.shape
    tr = _row_tile(rr, LANES) if transposed else _row_tile(rr, limit=1024)
    nb = rr // tr
    if axis == 0:
        g_map = lambda i, pr: (pr[1], pr[0] * nb + i, 0)
    else:
        g_map = lambda i, pr: (pr[1], i, pr[0])

    def body(pr, g_ref, p_ref, o_ref):
        acc = p_ref[0].astype(F32)
        for j in range(1, n_parts):
            acc = acc + p_ref[j].astype(F32)
        acc = acc + g_ref[...].astype(F32)
        o_ref[...] = (acc.T if transposed else acc).astype(BF16)

    out_spec = pl.BlockSpec((cc, tr), lambda i, pr: (0, i)) if transposed else pl.BlockSpec((tr, cc), lambda i, pr: (i, 0))
    return pl.pallas_call(
        body, name=name,
        grid_spec=pltpu.PrefetchScalarGridSpec(
            num_scalar_prefetch=1, grid=(nb,),
            in_specs=[pl.BlockSpec((None, tr, cc), g_map), pl.BlockSpec((n_parts, tr, cc), lambda i, pr: (0, i, 0))],
            out_specs=out_spec),
        out_shape=jax.ShapeDtypeStruct((cc, rr) if transposed else (rr, cc), BF16))(place, g, parts)


def _silu(v):
    return v / (1.0 + jnp.exp(-v))


def _ada_fwd(c_all, w_shard, b_shard):
    n_seq, n_cols = c_all.shape[0], w_shard.shape[1]

    def body(c_ref, w_ref, b_ref, o_ref, mine_ref, send_sems, recv_sems, local_sem):
        mine_ref[...] = jnp.dot(_silu(c_ref[...]), w_ref[...], precision=lax.Precision.HIGHEST,
                                preferred_element_type=F32) + b_ref[...]
        start, finish = _small_gather(mine_ref, o_ref, send_sems, recv_sems, 0, local_sem)
        start()
        finish()

    return pl.pallas_call(
        body, name="ada_fwd", out_shape=jax.ShapeDtypeStruct((N_DEV * n_seq, n_cols), F32),
        scratch_shapes=[pltpu.VMEM((n_seq, n_cols), F32), pltpu.SemaphoreType.DMA((7,)), pltpu.SemaphoreType.DMA((7,)),
                        pltpu.SemaphoreType.DMA],
        compiler_params=pltpu.CompilerParams(vmem_limit_bytes=MM_VMEM_LIMIT))(c_all, w_shard, b_shard)


def _loss_and_grads(x2, ffn, target, gate):
    t, d = x2.shape

    def half_loss(x2_blk, ffn_blk, gate_row, target_blk):
        return 0.5 * _f_loss(x2_blk, ffn_blk, target_blk, gate_row)[0]

    def body(x2_ref, ffn_ref, tgt_ref, gate_ref, loss_ref, dx2_ref, dffn_ref, dgate_ref):
        rows, vjp = jax.vjp(lambda a, b, g: half_loss(a, b, g, tgt_ref[...]), x2_ref[...], ffn_ref[...], gate_ref[...])
        loss_ref[...] = rows
        dx2_ref[...], dffn_ref[...], dgate = vjp(jnp.ones_like(rows))

        @pl.when(pl.program_id(0) == 0)
        def _():
            dgate_ref[...] = jnp.zeros_like(dgate_ref)

        dgate_ref[...] += dgate

    blk = pl.BlockSpec((ROW_TILE, d), lambda i: (i, 0))
    row = pl.BlockSpec((1, d), lambda i: (0, 0))
    return pl.pallas_call(
        body, name="loss_and_grads", grid=(t // ROW_TILE,),
        in_specs=[blk, blk, blk, row],
        out_specs=[pl.BlockSpec((ROW_TILE, 1), lambda i: (i, 0)), blk, blk, row],
        out_shape=[jax.ShapeDtypeStruct((t, 1), F32), jax.ShapeDtypeStruct((t, d), F32), jax.ShapeDtypeStruct((t, d), F32),
                   jax.ShapeDtypeStruct((1, d), F32)],
        compiler_params=pltpu.CompilerParams(dimension_semantics=("arbitrary",), vmem_limit_bytes=MM_VMEM_LIMIT),
    )(x2, ffn, target, gate)


def _ada_bwd(c_all, dmod_cols):
    def body(c_ref, d_ref, o_ref):
        o_ref[...] = lax.dot_general(_silu(c_ref[...]), d_ref[...], (((0,), (0,)), ((), ())),
                                     precision=lax.Precision.HIGHEST, preferred_element_type=F32)

    return pl.pallas_call(body, name="ada_bwd", out_shape=jax.ShapeDtypeStruct((c_all.shape[1], dmod_cols.shape[1]), F32),
                          compiler_params=pltpu.CompilerParams(vmem_limit_bytes=MM_VMEM_LIMIT))(c_all, dmod_cols)


def _adamw_math(w, g, m, v):
    m = ADAM_B1 * m + (1.0 - ADAM_B1) * g
    v = ADAM_B2 * v + (1.0 - ADAM_B2) * (g * g)
    m_hat = m / (1.0 - ADAM_B1 ** ADAM_STEP)
    v_hat = v / (1.0 - ADAM_B2 ** ADAM_STEP)
    delta = -ADAM_LR * (m_hat / (jnp.sqrt(v_hat) + ADAM_EPS) + ADAM_WD * w)
    return delta, m, v


def _adamw(w, g, m, v, name):
    r, ccols = w.shape
    tr = max(d for d in range(8, ROW_TILE + 1, 8) if r % d == 0)
    spec = pl.BlockSpec((tr, ccols), lambda i: (i, 0))

    def body(w_ref, g_ref, m_ref, v_ref, d_ref, nm_ref, nv_ref):
        d_ref[...], nm_ref[...], nv_ref[...] = _adamw_math(w_ref[...], g_ref[...], m_ref[...], v_ref[...])

    return pl.pallas_call(body, name=name, grid=(r // tr,), in_specs=[spec] * 4, out_specs=[spec] * 3,
                          out_shape=[jax.ShapeDtypeStruct(w.shape, F32)] * 3,
                          compiler_params=pltpu.CompilerParams(vmem_limit_bytes=MM_VMEM_LIMIT))(w, g, m, v)


def _small_layout(sizes):
    offs, off = [], 0
    for n in sizes:
        offs.append(off)
        off += -(-n // LANES) * LANES
    total = -(-(off + LANES) // (8 * LANES)) * (8 * LANES)
    return offs, off, total


def _adamw_small(ws, g_all, ms, vs, offs, loss_off):
    n_p = len(ws)

    def device_sum(g_ref, off, width):
        blk = g_ref[:, off:off + width]
        acc = blk[0:1]
        for d in range(1, N_DEV):
            acc = acc + blk[d:d + 1]
        return acc

    def body(*refs):
        w_refs, m_refs, v_refs = refs[:n_p], refs[n_p:2 * n_p], refs[2 * n_p:3 * n_p]
        g_ref = refs[3 * n_p]
        outs = refs[3 * n_p + 1:]
        for i in range(n_p):
            n = w_refs[i].shape[1]
            g = device_sum(g_ref, offs[i], -(-n // LANES) * LANES)[:, :n]
            outs[i][...] = g
            outs[n_p + i][...], outs[2 * n_p + i][...], outs[3 * n_p + i][...] = _adamw_math(
                w_refs[i][...], g, m_refs[i][...], v_refs[i][...])
        outs[4 * n_p][...] = device_sum(g_ref, loss_off, LANES)

    res = pl.pallas_call(
        body, name="adamw_small",
        out_shape=[jax.ShapeDtypeStruct(a.shape, F32) for a in list(ws) * 4] + [jax.ShapeDtypeStruct((1, LANES), F32)],
    )(*ws, *ms, *vs, g_all)
    return res[:n_p], res[n_p:2 * n_p], res[2 * n_p:3 * n_p], res[3 * n_p:4 * n_p], res[4 * n_p]


def _adamw_halves(place, w, own, sib, m, v, axis, name, after):
    r, cc = w.shape
    if axis == 0:
        rows, gc = own.shape[0], own.shape[1]
        tr = _row_tile(rows)
        nb = rows // tr
        w_spec = pl.BlockSpec((tr, cc), lambda h, i, pr: (h * nb + i, 0))
        g_spec = pl.BlockSpec((tr, gc), lambda h, i, pr: (i, 0))
    else:
        tr = _row_tile(r)
        nb = r // tr
        gc = own.shape[1]
        w_spec = pl.BlockSpec((tr, gc), lambda h, i, pr: (i, h))
        g_spec = pl.BlockSpec((tr, gc), lambda h, i, pr: (i, 0))
    wc = w_spec.block_shape[1]

    def body(pr, w_ref, o_ref, s_ref, m_ref, v_ref, after_ref, g_ref, d_ref, nm_ref, nv_ref):
        g = jnp.where(pl.program_id(0) == pr[0], o_ref[...], s_ref[...]).astype(F32)[:, :wc]
        g_ref[...] = g
        d_ref[...], nm_ref[...], nv_ref[...] = _adamw_math(w_ref[...], g, m_ref[...], v_ref[...])

    return pl.pallas_call(
        body, name=name,
        grid_spec=pltpu.PrefetchScalarGridSpec(
            num_scalar_prefetch=1, grid=(2, nb),
            in_specs=[w_spec, g_spec, g_spec, w_spec, w_spec, ANY], out_specs=[w_spec] * 4),
        out_shape=[jax.ShapeDtypeStruct(w.shape, F32)] * 4,
        compiler_params=pltpu.CompilerParams(vmem_limit_bytes=MM_VMEM_LIMIT))(place, w, own, sib, m, v, after)


SMALL = ("b_ada", "norm_attn", "norm_ffn", "q_a_norm", "kv_a_norm", "q_norm", "k_nope_norm", "k_rope_norm",
         "out_norm_sb", "out_norm_mla")
WEIGHTS = ("w_ada", "b_ada", "norm_attn", "norm_ffn", "w_in", "q_a_norm", "w_q_up", "kv_a_norm", "w_kv_up",
           "q_norm", "k_nope_norm", "k_rope_norm", "out_norm_sb", "out_norm_mla", "w_out", "w_gate", "w_up",
           "w_down")


def kernel(x, c, positions, w_ada, b_ada, norm_attn, norm_ffn, w_in, q_a_norm, w_q_up, kv_a_norm, w_kv_up, q_norm, k_nope_norm, k_rope_norm, out_norm_sb, out_norm_mla, w_out, w_gate, w_up, w_down, loss_target, m_w_ada, m_b_ada, m_norm_attn, m_norm_ffn, m_w_in, m_q_a_norm, m_w_q_up, m_kv_a_norm, m_w_kv_up, m_q_norm, m_k_nope_norm, m_k_rope_norm, m_out_norm_sb, m_out_norm_mla, m_w_out, m_w_gate, m_w_up, m_w_down, v_w_ada, v_b_ada, v_norm_attn, v_norm_ffn, v_w_in, v_q_a_norm, v_w_q_up, v_kv_a_norm, v_w_kv_up, v_q_norm, v_k_nope_norm, v_k_rope_norm, v_out_norm_sb, v_out_norm_mla, v_w_out, v_w_gate, v_w_up, v_w_down):
    local = dict(locals())
    w = {n: local[n][0] for n in WEIGHTS}
    m = {n: local["m_" + n][0] for n in WEIGHTS}
    v = {n: local["v_" + n][0] for n in WEIGHTS}
    small = {n: w[n].reshape(1, -1) for n in SMALL}
    ix, iy, ic = _my_place()
    chip = 2 * ix + iy
    dev = 2 * chip + ic
    xs, target = x[0], loss_target[0]
    seq = xs.shape[0]

    ff_pad = FF_SHARD_PAD - FF_SHARD
    shards = {n: (w[n].T if n in TRAVELS_TRANSPOSED else w[n]).astype(BF16) for n in BIG}
    for n in ("w_gate", "w_up", "w_down"):
        shards[n] = jnp.pad(shards[n], ((0, ff_pad), (0, 0)))
    early, early_done, c_gathered = _gather_weights(EARLY, [shards[n] for n in EARLY], c.reshape(8, LANES))
    gathered = dict(zip(EARLY, early))

    c_all = c_gathered.reshape(N_DEV, D_MODEL)
    ada_cols = w["w_ada"].shape[1]
    b_cols = lax.dynamic_slice_in_dim(small["b_ada"], chip * ada_cols, ada_cols, axis=1)
    mod_all = _ada_fwd(c_all, w["w_ada"], b_cols).reshape(N_CHIPS, 2, N_DEV, ada_cols)
    mod = lax.dynamic_index_in_dim(mod_all[:, 0], dev, axis=1, keepdims=False).reshape(1, N_MOD * D_MODEL)

    lands = [lax.dynamic_update_index_in_dim(lax.empty((N_CHIPS,) + shards[n].shape, BF16), shards[n], chip, 0)
             for n in LATE]
    late_gather = _exchange_start("gather_late_start", [shards[n] for n in LATE], lands, _late_gather_plan,
                                  3 * len(LATE), early_done, mod)
    mod = late_gather[4]

    half = MLA_ROPE // 2
    freqs = 1.0 / (ROPE_THETA ** (np.arange(half, dtype=np.float32) / half))
    zeros = np.zeros(LANES - MLA_ROPE, np.float32)
    freqs_row = jnp.asarray(np.concatenate([freqs, freqs, zeros]).astype(np.float32)[None])
    sign_row = jnp.asarray(np.concatenate([-np.ones(half), np.ones(half), zeros]).astype(np.float32)[None])
    cos, sin = _rope_tables(positions.reshape(seq, 1), freqs_row, sign_row)

    place = jnp.stack([ic, chip]).astype(jnp.int32)
    small_params = {n: small[n] for n in SMALL if n != "b_ada"}

    p1 = {**{n: gathered[n] for n in EARLY}, **small_params}
    (mixed, x_res), mixing_vjp = jax.vjp(lambda x_, mod_, p_: _mixing_stage(x_, mod_, p_, cos, sin), xs, mod, p1)
    _, landed = _exchange_wait("gather_late_wait", late_gather, _late_gather_plan, mixed)
    p2 = {**dict(zip(LATE, landed)), **small_params}
    (x2, ffn), ffn_vjp = jax.vjp(_ffn_stage, x_res, mixed, mod, p2)
    loss_rows, g_x2, g_ffn, g_gate2 = _loss_and_grads(x2, ffn, target, _mod_parts(mod)[5])
    loss_part = jnp.sum(loss_rows)
    gx2, gmixed, gmod2, gp2 = ffn_vjp((g_x2, g_ffn))
    gmod2 = gmod2 + jnp.concatenate([jnp.zeros((1, (N_MOD - 1) * D_MODEL), F32), g_gate2], axis=1)
    late_grads = [gp2[n] for n in LATE]
    late_plan = _direct_scatter_plan(LATE)
    late_scatter = _exchange_start(
        "grad_scatter_late_start", late_grads,
        [lax.empty((7,) + _half_shape(gr.shape[1:], HALF_AXIS[n]), BF16) for n, gr in zip(LATE, late_grads)],
        late_plan, 7 * len(LATE), gx2, gmixed)
    gx, gmod1, gp1 = mixing_vjp((late_scatter[4], gx2))
    gmod = gmod1 + gmod2
    gp = {n: gp1[n] + gp2[n] for n in small_params}

    sizes = [w[n].size for n in SMALL]
    offs, loss_off, n_small = _small_layout(sizes)
    pieces = []
    for n, size in zip(SMALL, sizes):
        pieces.append(gmod if n == "b_ada" else gp[n])
        if size % LANES:
            pieces.append(jnp.zeros((1, LANES - size % LANES), F32))
    pieces += [jnp.full((1, LANES), loss_part), jnp.zeros((1, n_small - loss_off - LANES), F32)]
    small_vec = jnp.concatenate(pieces, axis=1)

    g, delta, new_m, new_v = {}, {}, {}, {}

    def update(names, own, sib, after):
        for n, o, s in zip(names, own, sib):
            if n in TRANSPOSED_UPDATE:
                res = _adamw_halves(place, w[n].T, o, s, m[n].T, v[n].T, 1, "adamw_" + n, after)
                g[n], delta[n], new_m[n], new_v[n] = [r.T for r in res]
            else:
                g[n], delta[n], new_m[n], new_v[n] = _adamw_halves(place, w[n], o, s, m[n], v[n], HALF_AXIS[n],
                                                                   "adamw_" + n, after)

    late_grads, late_parts = _exchange_wait("grad_scatter_late_wait", late_scatter, late_plan, gx)
    own_late = [_chip_sum_direct(place, gr, pt, HALF_AXIS[n], "grad_chip_sum_" + n,
                                 n in TRANSPOSED_UPDATE and n not in TRAVELS_TRANSPOSED)
                for n, gr, pt in zip(LATE, late_grads, late_parts)]
    early_grads = [gp1[n] for n in EARLY]
    theirs, small_gathered = _pair_exchange(EARLY, early_grads, "grad_pair_exchange_early",
                                            small_vec.reshape(8, n_small // 8))
    small_all = small_gathered.reshape(N_DEV, n_small)
    sib_late = _sibling_join(own_late, "grad_sibling_join_late", small_all)
    early_sums = [_pair_sum(place, gr, th, HALF_AXIS[n], "grad_pair_sum_" + n)
                  for n, gr, th in zip(EARLY, early_grads, theirs)]
    early_scatter = _exchange_start(
        "grad_scatter_early_start", early_sums,
        [lax.empty((N_CHIPS - 1,) + s.shape[1:], BF16) for s in early_sums], _late_scatter_plan, 3 * len(EARLY),
        sib_late[0], small_all)
    small_all = early_scatter[4]
    update(LATE, own_late, sib_late, small_all)

    *small_out, loss_row = _adamw_small([small[n] for n in SMALL], small_all, [m[n].reshape(1, -1) for n in SMALL],
                                        [v[n].reshape(1, -1) for n in SMALL], offs, loss_off)
    loss = loss_row[0, 0]
    for d, outs_d in zip((g, delta, new_m, new_v), small_out):
        d.update({n: o.reshape(w[n].shape) for n, o in zip(SMALL, outs_d)})

    dmod_all = small_all[:, :N_MOD * D_MODEL]
    g["w_ada"] = _ada_bwd(c_all, lax.dynamic_slice_in_dim(dmod_all, chip * ada_cols, ada_cols, axis=1))
    delta["w_ada"], new_m["w_ada"], new_v["w_ada"] = _adamw(w["w_ada"], g["w_ada"], m["w_ada"], v["w_ada"], "adamw_w_ada")

    early_sums, early_parts = _exchange_wait("grad_scatter_early_wait", early_scatter, _late_scatter_plan,
                                             delta["w_ada"])
    own_early = [_chip_sum(place, ps, pt, "grad_chip_sum_" + n, n in TRANSPOSED_UPDATE and n not in TRAVELS_TRANSPOSED)
                 for n, ps, pt in zip(EARLY, early_sums, early_parts)]
    sib_early = _sibling_join(own_early, "grad_sibling_join_early", delta["w_ada"])
    update(EARLY, own_early, sib_early, sib_early[0])

    def outs(d):
        return [d[n][None] for n in WEIGHTS]

    return (loss, gx[None], *outs(g), *outs(delta), *outs(new_m), *outs(new_v))
```

```python
import numpy as np
import jax
import jax.numpy as jnp
from jax import lax
from jax.experimental import pallas as pl
from jax.experimental.pallas import tpu as pltpu

F32 = jnp.float32
BF16 = jnp.bfloat16
MESH = pl.DeviceIdType.MESH
ANY = pl.BlockSpec(memory_space=pl.ANY)

D_MODEL = 1024
SB_HEADS = 8
SB_HEAD_DIM = 64
SB_WIDTH = 512
MLA_HEADS = 4
MLA_NOPE = 128
MLA_ROPE = 64
MLA_QK = 192
MLA_V = 128
MLA_Q_RANK = 384
MLA_KV_RANK = 256
D_FF = 2816
N_MOD = 6
ROPE_THETA = 10000.0
EPS = 1e-6
LANES = 128

ADAM_LR = 0.001
ADAM_B1 = 0.9
ADAM_B2 = 0.999
ADAM_EPS = 1e-08
ADAM_WD = 0.01
ADAM_STEP = 10

N_CHIPS = 4
N_DEV = 8
ROW_TILE = 512
MM_ROW_TILE = 512
ATT_BLK = 256
MM_VMEM_LIMIT = 56 * 1024 * 1024
FF_SHARD = D_FF // N_CHIPS
FF_SHARD_PAD = 768


def _mm(a, b, mode, name, tm, tn, out_dtype=F32):
    if mode == "nn":
        (m, k), n = a.shape, b.shape[1]
        a_spec = pl.BlockSpec((tm, k), lambda j, i: (i, 0))
        b_spec = pl.BlockSpec((k, tn), lambda j, i: (0, j))
        dims = (((1,), (0,)), ((), ()))
    elif mode == "nt":
        (m, k), n = a.shape, b.shape[0]
        a_spec = pl.BlockSpec((tm, k), lambda j, i: (i, 0))
        b_spec = pl.BlockSpec((tn, k), lambda j, i: (j, 0))
        dims = (((1,), (1,)), ((), ()))
    else:
        (k, m), n = a.shape, b.shape[1]
        a_spec = pl.BlockSpec((k, tm), lambda j, i: (0, i))
        b_spec = pl.BlockSpec((k, tn), lambda j, i: (0, j))
        dims = (((0,), (0,)), ((), ()))
    assert m % tm == 0 and n % tn == 0, (name, m, n, tm, tn)

    def body(a_ref, b_ref, o_ref):
        o_ref[...] = lax.dot_general(a_ref[...].astype(BF16), b_ref[...].astype(BF16), dims,
                                     preferred_element_type=F32).astype(out_dtype)

    return pl.pallas_call(
        body, name=name, grid=(n // tn, m // tm),
        in_specs=[a_spec, b_spec],
        out_specs=pl.BlockSpec((tm, tn), lambda j, i: (i, j)),
        out_shape=jax.ShapeDtypeStruct((m, n), out_dtype),
        compiler_params=pltpu.CompilerParams(dimension_semantics=("arbitrary", "arbitrary"),
                                             vmem_limit_bytes=MM_VMEM_LIMIT),
    )(a, b)


def _make_linear(name, tk_w, tn_w):
    @jax.custom_vjp
    def op(a, w):
        return _mm(a, w, "nn", name + "_fwd", MM_ROW_TILE, w.shape[1])

    def fwd(a, w):
        return op(a, w), (a, w)

    def bwd(res, dy):
        a, w = res
        da = _mm(dy, w, "nt", name + "_dx", MM_ROW_TILE, w.shape[0])
        dw = _mm(a, dy, "tn", name + "_dw", tk_w, tn_w, out_dtype=BF16)
        return da, dw

    op.defvjp(fwd, bwd)
    return op


def _make_linear_split_t(name, widths, tk_w):
    starts = [sum(widths[:g]) for g in range(len(widths))]

    def call_fwd(a, wt):
        t, k = a.shape
        n = wt.shape[0]

        def body(a_ref, w_ref, *o_refs):
            y = _nt(a_ref[...].astype(BF16), w_ref[...])
            for o_ref, s0, wd in zip(o_refs, starts, widths):
                o_ref[...] = y[:, s0:s0 + wd]

        return pl.pallas_call(
            body, name=name + "_fwd", grid=(t // MM_ROW_TILE,),
            in_specs=[pl.BlockSpec((MM_ROW_TILE, k), lambda i: (i, 0)), pl.BlockSpec((n, k), lambda i: (0, 0))],
            out_specs=[pl.BlockSpec((MM_ROW_TILE, wd), lambda i: (i, 0)) for wd in widths],
            out_shape=[jax.ShapeDtypeStruct((t, wd), F32) for wd in widths],
            compiler_params=pltpu.CompilerParams(dimension_semantics=("arbitrary",), vmem_limit_bytes=MM_VMEM_LIMIT),
        )(a, wt)

    def call_dx(dys, wt):
        t = dys[0].shape[0]
        n, k = wt.shape

        def body(*refs):
            dy_refs, w_ref, o_ref = refs[:-2], refs[-2], refs[-1]
            acc = jnp.zeros((MM_ROW_TILE, k), F32)
            for dy_ref, s0, wd in zip(dy_refs, starts, widths):
                acc = acc + jnp.dot(dy_ref[...].astype(BF16), w_ref[s0:s0 + wd, :], preferred_element_type=F32)
            o_ref[...] = acc

        return pl.pallas_call(
            body, name=name + "_dx", grid=(t // MM_ROW_TILE,),
            in_specs=[pl.BlockSpec((MM_ROW_TILE, wd), lambda i: (i, 0)) for wd in widths]
            + [pl.BlockSpec((n, k), lambda i: (0, 0))],
            out_specs=pl.BlockSpec((MM_ROW_TILE, k), lambda i: (i, 0)),
            out_shape=jax.ShapeDtypeStruct((t, k), F32),
            compiler_params=pltpu.CompilerParams(dimension_semantics=("arbitrary",), vmem_limit_bytes=MM_VMEM_LIMIT),
        )(*dys, wt)

    def call_dw(a, dys, wt):
        t, k = a.shape
        n = wt.shape[0]

        def body(a_ref, *refs):
            dy_refs, o_ref = refs[:-1], refs[-1]
            ab = a_ref[...].astype(BF16)
            for dy_ref, s0, wd in zip(dy_refs, starts, widths):
                o_ref[s0:s0 + wd, :] = _tn(dy_ref[...].astype(BF16), ab).astype(BF16)
            if starts[-1] + widths[-1] < n:
                o_ref[starts[-1] + widths[-1]:, :] = jnp.zeros((n - starts[-1] - widths[-1], tk_w), BF16)

        return pl.pallas_call(
            body, name=name + "_dw", grid=(k // tk_w,),
            in_specs=[pl.BlockSpec((t, tk_w), lambda i: (0, i))]
            + [pl.BlockSpec((t, wd), lambda i: (0, 0)) for wd in widths],
            out_specs=pl.BlockSpec((n, tk_w), lambda i: (0, i)),
            out_shape=jax.ShapeDtypeStruct((n, k), BF16),
            compiler_params=pltpu.CompilerParams(dimension_semantics=("arbitrary",), vmem_limit_bytes=MM_VMEM_LIMIT),
        )(a, *dys)

    @jax.custom_vjp
    def op(a, wt):
        return tuple(call_fwd(a, wt))

    def fwd(a, wt):
        return op(a, wt), (a, wt)

    def bwd(res, dys):
        a, wt = res
        return call_dx(dys, wt), call_dw(a, dys, wt)

    op.defvjp(fwd, bwd)
    return op


def _make_linear_split(name, widths, tk_w):
    starts = [sum(widths[:g]) for g in range(len(widths))]

    def call_fwd(a, w):
        t, k = a.shape
        n = w.shape[1]

        def body(a_ref, w_ref, *o_refs):
            y = jnp.dot(a_ref[...].astype(BF16), w_ref[...], preferred_element_type=F32)
            for o_ref, s0, wd in zip(o_refs, starts, widths):
                o_ref[...] = y[:, s0:s0 + wd]

        return pl.pallas_call(
            body, name=name + "_fwd", grid=(t // MM_ROW_TILE,),
            in_specs=[pl.BlockSpec((MM_ROW_TILE, k), lambda i: (i, 0)), pl.BlockSpec((k, n), lambda i: (0, 0))],
            out_specs=[pl.BlockSpec((MM_ROW_TILE, wd), lambda i: (i, 0)) for wd in widths],
            out_shape=[jax.ShapeDtypeStruct((t, wd), F32) for wd in widths],
            compiler_params=pltpu.CompilerParams(dimension_semantics=("arbitrary",), vmem_limit_bytes=MM_VMEM_LIMIT),
        )(a, w)

    def call_dx(dys, w):
        t = dys[0].shape[0]
        k, n = w.shape

        def body(*refs):
            dy_refs, w_ref, o_ref = refs[:-2], refs[-2], refs[-1]
            acc = jnp.zeros((MM_ROW_TILE, k), F32)
            for dy_ref, s0, wd in zip(dy_refs, starts, widths):
                acc = acc + _nt(dy_ref[...].astype(BF16), w_ref[:, s0:s0 + wd])
            o_ref[...] = acc

        return pl.pallas_call(
            body, name=name + "_dx", grid=(t // MM_ROW_TILE,),
            in_specs=[pl.BlockSpec((MM_ROW_TILE, wd), lambda i: (i, 0)) for wd in widths]
            + [pl.BlockSpec((k, n), lambda i: (0, 0))],
            out_specs=pl.BlockSpec((MM_ROW_TILE, k), lambda i: (i, 0)),
            out_shape=jax.ShapeDtypeStruct((t, k), F32),
            compiler_params=pltpu.CompilerParams(dimension_semantics=("arbitrary",), vmem_limit_bytes=MM_VMEM_LIMIT),
        )(*dys, w)

    def call_dw(a, dys, w):
        t, k = a.shape
        n = w.shape[1]

        def body(a_ref, *refs):
            dy_refs, o_ref = refs[:-1], refs[-1]
            ab = a_ref[...].astype(BF16)
            for dy_ref, s0, wd in zip(dy_refs, starts, widths):
                o_ref[:, s0:s0 + wd] = _tn(ab, dy_ref[...].astype(BF16)).astype(BF16)
            if starts[-1] + widths[-1] < n:
                o_ref[:, starts[-1] + widths[-1]:] = jnp.zeros((tk_w, n - starts[-1] - widths[-1]), BF16)

        return pl.pallas_call(
            body, name=name + "_dw", grid=(k // tk_w,),
            in_specs=[pl.BlockSpec((t, tk_w), lambda i: (0, i))]
            + [pl.BlockSpec((t, wd), lambda i: (0, 0)) for wd in widths],
            out_specs=pl.BlockSpec((tk_w, n), lambda i: (i, 0)),
            out_shape=jax.ShapeDtypeStruct((k, n), BF16),
            compiler_params=pltpu.CompilerParams(dimension_semantics=("arbitrary",), vmem_limit_bytes=MM_VMEM_LIMIT),
        )(a, *dys)

    @jax.custom_vjp
    def op(a, w):
        return tuple(call_fwd(a, w))

    def fwd(a, w):
        return op(a, w), (a, w)

    def bwd(res, dys):
        a, w = res
        return call_dx(dys, w), call_dw(a, dys, w)

    op.defvjp(fwd, bwd)
    return op


def _row_spec(arr, tb):
    return pl.BlockSpec((tb, arr.shape[1]), lambda i: (i, 0))


def _full_spec(arr):
    return pl.BlockSpec(arr.shape, lambda i: (0, 0))


def _make_rowwise(name, f, n_rows, n_params, out_cols, diff_rows, out_dtypes=None, grad_dtypes=None):
    n_out = len(out_cols)
    out_dtypes = out_dtypes or [F32] * n_out
    grad_dtypes = grad_dtypes or [F32] * sum(diff_rows)

    def call_fwd(rows, params):
        t = rows[0].shape[0]

        def body(*refs):
            ins = [r[...] for r in refs[:n_rows + n_params]]
            outs = f(*ins)
            for o_ref, o in zip(refs[n_rows + n_params:], outs):
                o_ref[...] = o.astype(o_ref.dtype)

        return pl.pallas_call(
            body, name=name + "_fwd", grid=(t // ROW_TILE,),
            in_specs=[_row_spec(a, ROW_TILE) for a in rows] + [_full_spec(p) for p in params],
            out_specs=[pl.BlockSpec((ROW_TILE, n), lambda i: (i, 0)) for n in out_cols],
            out_shape=[jax.ShapeDtypeStruct((t, n), dt) for n, dt in zip(out_cols, out_dtypes)],
            compiler_params=pltpu.CompilerParams(dimension_semantics=("arbitrary",),
                                                 vmem_limit_bytes=MM_VMEM_LIMIT),
        )(*rows, *params)

    def call_bwd(rows, params, cts):
        t = rows[0].shape[0]
        d_rows = [a for a, d in zip(rows, diff_rows) if d]
        n_in = n_rows + n_params + n_out

        def body(*refs):
            ins = [r[...] for r in refs[:n_rows + n_params]]
            ct = tuple(r[...].astype(F32) for r in refs[n_rows + n_params:n_in])
            _, vjp = jax.vjp(f, *ins)
            grads = vjp(ct)
            out_refs = refs[n_in:]
            g_rows = [g for g, d in zip(grads[:n_rows], diff_rows) if d]
            for o_ref, g in zip(out_refs[:len(g_rows)], g_rows):
                o_ref[...] = g.astype(o_ref.dtype)
            p_refs = out_refs[len(g_rows):]

            if p_refs:
                @pl.when(pl.program_id(0) == 0)
                def _():
                    for p_ref in p_refs:
                        p_ref[...] = jnp.zeros_like(p_ref)

                for p_ref, g in zip(p_refs, grads[n_rows:]):
                    p_ref[...] += g

        return pl.pallas_call(
            body, name=name + "_bwd", grid=(t // ROW_TILE,),
            in_specs=[_row_spec(a, ROW_TILE) for a in rows] + [_full_spec(p) for p in params]
            + [_row_spec(c, ROW_TILE) for c in cts],
            out_specs=[_row_spec(a, ROW_TILE) for a in d_rows] + [_full_spec(p) for p in params],
            out_shape=[jax.ShapeDtypeStruct(a.shape, dt) for a, dt in zip(d_rows, grad_dtypes)]
            + [jax.ShapeDtypeStruct(p.shape, F32) for p in params],
            compiler_params=pltpu.CompilerParams(dimension_semantics=("arbitrary",),
                                                 vmem_limit_bytes=MM_VMEM_LIMIT),
        )(*rows, *params, *cts)

    @jax.custom_vjp
    def op(*args):
        return tuple(call_fwd(args[:n_rows], args[n_rows:]))

    def fwd(*args):
        return op(*args), args

    def bwd(args, cts):
        rows, params = args[:n_rows], args[n_rows:]
        outs = call_bwd(rows, params, cts)
        it = iter(outs)
        g_rows = [next(it) if d else jnp.zeros_like(a) for a, d in zip(rows, diff_rows)]
        return tuple(g_rows) + tuple(it)

    op.defvjp(fwd, bwd)
    return op


def _rms(x, g, n):
    return x * lax.rsqrt(jnp.sum(x * x, axis=-1, keepdims=True) * (1.0 / n) + EPS) * g


def _f_pre_attn(x, g, scale, shift):
    return _rms(x, g, D_MODEL) * (1.0 + scale) + shift, x


def _f_mla_a(cq, ckv, gq, gkv):
    return _rms(cq, gq, MLA_Q_RANK), _rms(ckv, gkv, MLA_KV_RANK)


@jax.custom_vjp
def _split_lanes(x):
    return tuple(x[:, i * LANES:(i + 1) * LANES] for i in range(x.shape[1] // LANES))


def _split_lanes_fwd(x):
    return _split_lanes(x), None


def _split_lanes_bwd(_, cts):
    return (jnp.concatenate(cts, axis=1),)


_split_lanes.defvjp(_split_lanes_fwd, _split_lanes_bwd)


def _f_mla_b(qall, kn_all, kr, kr_sw, cos, sin, gqn, gqr, gqr_sw, gkn, gkr, gkr_sw):
    q = _split_lanes(qall)
    kn = _split_lanes(kn_all)
    qn_o, qr_o, kn_o = [], [], []
    for h in range(MLA_HEADS):
        qn, qr, qs = q[h], q[MLA_HEADS + h], q[2 * MLA_HEADS + h]
        ss = jnp.sum(qn * qn, axis=-1, keepdims=True) + jnp.sum(qr * qr, axis=-1, keepdims=True)
        rs = lax.rsqrt(ss * (1.0 / MLA_QK) + EPS)
        qn_o.append(qn * rs * gqn)
        qr_o.append((qr * rs * gqr) * cos + (qs * rs * gqr_sw) * sin)
        kn_o.append(_rms(kn[h], gkn, MLA_NOPE))
    rs = lax.rsqrt(jnp.sum(kr * kr, axis=-1, keepdims=True) * (1.0 / MLA_ROPE) + EPS)
    kr_o = (kr * rs * gkr) * cos + (kr_sw * rs * gkr_sw) * sin
    return (jnp.concatenate(qn_o, axis=1), jnp.concatenate(qr_o, axis=1), jnp.concatenate(kn_o, axis=1), kr_o)


def _f_post_attn(o_sb, o_mla, g_sb, g_mla):
    return (jnp.concatenate([_rms(o_sb, g_sb, SB_WIDTH), _rms(o_mla, g_mla, SB_WIDTH)], axis=1),)


def _f_pre_ffn(x, attn, gate, g, scale, shift):
    x2 = x + gate * attn
    return x2, _rms(x2, g, D_MODEL) * (1.0 + scale) + shift


def _f_swiglu(gt, up):
    return (gt / (1.0 + jnp.exp(-gt)) * up,)


def _f_loss(x2, ffn, target, gate):
    err = x2 + gate * ffn - target
    return (jnp.sum(err * err, axis=-1, keepdims=True) * (1.0 / D_MODEL),)


def _rope_tables(pos_col, freqs, sign):
    t = pos_col.shape[0]

    def body(p_ref, f_ref, s_ref, cos_ref, sin_ref):
        ang = p_ref[...].astype(F32) * f_ref[...]
        live = jnp.abs(s_ref[...])
        cos_ref[...] = jnp.cos(ang) * live
        sin_ref[...] = jnp.sin(ang) * s_ref[...]

    return pl.pallas_call(
        body, name="rope_tables", grid=(t // ROW_TILE,),
        in_specs=[pl.BlockSpec((ROW_TILE, 1), lambda i: (i, 0)), _full_spec(freqs), _full_spec(sign)],
        out_specs=[pl.BlockSpec((ROW_TILE, LANES), lambda i: (i, 0))] * 2,
        out_shape=[jax.ShapeDtypeStruct((t, LANES), F32)] * 2,
    )(pos_col, freqs, sign)


def _hi_lo_dot(x, tri):
    hi = x.astype(BF16)
    lo = (x - hi.astype(F32)).astype(BF16)
    return (jnp.dot(hi, tri, preferred_element_type=F32) + jnp.dot(lo, tri, preferred_element_type=F32))


def _tri(cmp):
    r = lax.broadcasted_iota(jnp.int32, (ATT_BLK, ATT_BLK), 0)
    c = lax.broadcasted_iota(jnp.int32, (ATT_BLK, ATT_BLK), 1)
    return cmp(r, c).astype(BF16)


def _nt(a, b):
    return lax.dot_general(a, b, (((1,), (1,)), ((), ())), preferred_element_type=F32)


def _tn(a, b):
    return lax.dot_general(a, b, (((0,), (0,)), ((), ())), preferred_element_type=F32)


def _sb_logs(z):
    lb = jnp.minimum(z, 0.0) - jnp.log(1.0 + jnp.exp(-jnp.abs(z)))
    return lb, lb - z


def _sb_fwd(q, k, v):
    t = q.shape[0]
    nq = t // ATT_BLK
    scale = SB_HEAD_DIM ** -0.5

    def body(q_ref, k_ref, v_ref, o_ref, tot_ref):
        qi = pl.program_id(1)
        lane = lax.broadcasted_iota(jnp.int32, (ATT_BLK, LANES), 1)
        tri = _tri(lambda r, c: r > c)
        qv = q_ref[...] * scale
        heads = [(lane // SB_HEAD_DIM) == hh for hh in range(2)]
        qms = [jnp.where(mine, qv, 0.0).astype(BF16) for mine in heads]

        def blocks(kbs, carry, diagonal):
            acc = carry[0]
            nb = len(kbs)
            chains = [(b, hh) for b in range(nb) for hh in range(2)]
            offs = [pl.multiple_of(kb * ATT_BLK, ATT_BLK) for kb in kbs]
            kks = [k_ref[pl.ds(off, ATT_BLK), :].astype(BF16) for off in offs]
            v_blks = [v_ref[pl.ds(off, ATT_BLK), :] for off in offs]
            if any(diagonal):
                valid = (lax.broadcasted_iota(jnp.int32, (ATT_BLK, ATT_BLK), 1)
                         < lax.broadcasted_iota(jnp.int32, (ATT_BLK, ATT_BLK), 0))
            zs = {ch: _nt(qms[ch[1]], kks[ch[0]]) for ch in chains}
            vvs = {(b, hh): jnp.where(heads[hh], v_blks[b], 0.0).astype(BF16) for b, hh in chains}
            logs = {ch: _sb_logs(zs[ch]) for ch in chains}
            l1ms = {ch: jnp.where(valid, logs[ch][1], 0.0) if diagonal[ch[0]] else logs[ch][1] for ch in chains}
            run = {(0, hh): carry[1 + hh] for hh in range(2)}
            for b, hh in chains:
                run[(b + 1, hh)] = run[(b, hh)] + jnp.sum(l1ms[(b, hh)], axis=-1, keepdims=True)
            afters = {ch: _hi_lo_dot(l1ms[ch], tri) for ch in chains}
            ws = {ch: jnp.exp(logs[ch][0] + (afters[ch] + run[ch])) for ch in chains}
            ws = {ch: jnp.where(valid, ws[ch], 0.0) if diagonal[ch[0]] else ws[ch] for ch in chains}
            for ch in chains:
                acc = acc + jnp.dot(ws[ch].astype(BF16), vvs[ch], preferred_element_type=F32)
            return (acc, run[(nb, 0)], run[(nb, 1)])

        zero = jnp.zeros((ATT_BLK, 1), F32)
        init = (jnp.zeros((ATT_BLK, LANES), F32), zero, zero)
        carry = lax.cond(qi % 2 == 1, lambda cr: blocks([qi, qi - 1], cr, (True, False)),
                         lambda cr: blocks([qi], cr, (True,)), init)
        top = qi - 1 - qi % 2
        carry = lax.fori_loop(0, qi // 2, lambda pr, cr: blocks([top - 2 * pr, top - 1 - 2 * pr], cr, (False, False)),
                              carry)
        o_ref[...] = carry[0]
        for hh in range(2):
            tot_ref[:, hh * LANES:(hh + 1) * LANES] = jnp.broadcast_to(carry[1 + hh], (ATT_BLK, LANES))

    return pl.pallas_call(
        body, name="sb_attn_fwd", grid=(SB_HEADS // 2, nq),
        in_specs=[pl.BlockSpec((ATT_BLK, LANES), lambda p, i: (i, p)),
                  pl.BlockSpec((t, LANES), lambda p, i: (0, p)),
                  pl.BlockSpec((t, LANES), lambda p, i: (0, p))],
        out_specs=[pl.BlockSpec((ATT_BLK, LANES), lambda p, i: (i, p)),
                   pl.BlockSpec((ATT_BLK, 2 * LANES), lambda p, i: (i, p))],
        out_shape=[jax.ShapeDtypeStruct((t, SB_WIDTH), F32), jax.ShapeDtypeStruct((t, SB_HEADS * LANES), F32)],
        compiler_params=pltpu.CompilerParams(dimension_semantics=("arbitrary", "arbitrary")),
    )(q, k, v)


def _sb_bwd(q, k, v, tot, do):
    t = q.shape[0]
    nq = t // ATT_BLK
    scale = SB_HEAD_DIM ** -0.5

    def body(q_ref, k_ref, v_ref, tot_ref, do_ref, dq_ref, dk_ref, dv_ref):
        qi = pl.program_id(1)

        @pl.when(qi == 0)
        def _():
            dk_ref[...] = jnp.zeros_like(dk_ref)
            dv_ref[...] = jnp.zeros_like(dv_ref)

        lane = lax.broadcasted_iota(jnp.int32, (ATT_BLK, LANES), 1)
        tri_incl = _tri(lambda r, c: r <= c)
        tri_lt = _tri(lambda r, c: r < c)
        qv = q_ref[...] * scale
        dov = do_ref[...]
        heads = [(lane // SB_HEAD_DIM) == hh for hh in range(2)]
        qms = [jnp.where(mine, qv, 0.0).astype(BF16) for mine in heads]
        doms = [jnp.where(mine, dov, 0.0).astype(BF16) for mine in heads]
        tots = [tot_ref[:, hh * LANES:hh * LANES + 1] for hh in range(2)]

        def blocks(kbs, carry, diagonal):
            dq = carry[0]
            nb = len(kbs)
            chains = [(b, hh) for b in range(nb) for hh in range(2)]
            offs = [pl.multiple_of(kb * ATT_BLK, ATT_BLK) for kb in kbs]
            k_blks = [k_ref[pl.ds(off, ATT_BLK), :] for off in offs]
            vvs = [v_ref[pl.ds(off, ATT_BLK), :].astype(BF16) for off in offs]
            if any(diagonal):
                valid = (lax.broadcasted_iota(jnp.int32, (ATT_BLK, ATT_BLK), 1)
                         < lax.broadcasted_iota(jnp.int32, (ATT_BLK, ATT_BLK), 0))
            kks = {(b, hh): jnp.where(heads[hh], k_blks[b], 0.0).astype(BF16) for b, hh in chains}
            zs = {ch: _nt(qms[ch[1]], kks[ch]) for ch in chains}
            dws = {ch: _nt(doms[ch[1]], vvs[ch[0]]) for ch in chains}
            logs = {ch: _sb_logs(zs[ch]) for ch in chains}
            lbs = {ch: logs[ch][0] for ch in chains}
            l1m_all = {ch: logs[ch][1] for ch in chains}
            l1ms = {ch: jnp.where(valid, l1m_all[ch], 0.0) if diagonal[ch[0]] else l1m_all[ch] for ch in chains}
            pre, c_de = {}, {}
            for hh in range(2):
                pre[(0, hh)], c_de[(0, hh)] = carry[1 + 2 * hh], carry[2 + 2 * hh]
            for b, hh in chains:
                pre[(b + 1, hh)] = pre[(b, hh)] + jnp.sum(l1ms[(b, hh)], axis=-1, keepdims=True)
            prefix = {ch: _hi_lo_dot(l1ms[ch], tri_incl) for ch in chains}
            ws = {ch: jnp.exp(lbs[ch] + (tots[ch[1]] - (prefix[ch] + pre[ch]))) for ch in chains}
            ws = {ch: jnp.where(valid, ws[ch], 0.0) if diagonal[ch[0]] else ws[ch] for ch in chains}
            d_es = {ch: ws[ch] * dws[ch] for ch in chains}
            for b, hh in chains:
                c_de[(b + 1, hh)] = c_de[(b, hh)] + jnp.sum(d_es[(b, hh)], axis=-1, keepdims=True)
            dvs = [_tn(ws[(b, 0)].astype(BF16), doms[0]) + _tn(ws[(b, 1)].astype(BF16), doms[1]) for b in range(nb)]
            dl1ms = {ch: jnp.dot(d_es[ch].astype(BF16), tri_lt, preferred_element_type=F32) + c_de[ch] for ch in chains}
            dzs = {ch: d_es[ch] * jnp.exp(l1m_all[ch]) - dl1ms[ch] * jnp.exp(lbs[ch]) for ch in chains}
            dzs = {ch: jnp.where(valid, dzs[ch], 0.0) if diagonal[ch[0]] else dzs[ch] for ch in chains}
            dzs = {ch: dzs[ch].astype(BF16) for ch in chains}
            for ch in chains:
                dq = dq + jnp.dot(dzs[ch], kks[ch], preferred_element_type=F32)
            for b in range(nb):
                dk_ref[pl.ds(offs[b], ATT_BLK), :] += _tn(dzs[(b, 0)], qms[0]) + _tn(dzs[(b, 1)], qms[1])
                dv_ref[pl.ds(offs[b], ATT_BLK), :] += dvs[b]
            return (dq, pre[(nb, 0)], c_de[(nb, 0)], pre[(nb, 1)], c_de[(nb, 1)])

        zero = jnp.zeros((ATT_BLK, 1), F32)
        carry = lax.fori_loop(0, qi // 2, lambda pr, cr: blocks([2 * pr, 2 * pr + 1], cr, (False, False)),
                              (jnp.zeros((ATT_BLK, LANES), F32), zero, zero, zero, zero))
        carry = lax.cond(qi % 2 == 1, lambda cr: blocks([qi - 1, qi], cr, (False, True)),
                         lambda cr: blocks([qi], cr, (True,)), carry)
        dq_ref[...] = carry[0] * scale

    return pl.pallas_call(
        body, name="sb_attn_bwd", grid=(SB_HEADS // 2, nq),
        in_specs=[pl.BlockSpec((ATT_BLK, LANES), lambda p, i: (i, p)),
                  pl.BlockSpec((t, LANES), lambda p, i: (0, p)),
                  pl.BlockSpec((t, LANES), lambda p, i: (0, p)),
                  pl.BlockSpec((ATT_BLK, 2 * LANES), lambda p, i: (i, p)),
                  pl.BlockSpec((ATT_BLK, LANES), lambda p, i: (i, p))],
        out_specs=[pl.BlockSpec((ATT_BLK, LANES), lambda p, i: (i, p)),
                   pl.BlockSpec((t, LANES), lambda p, i: (0, p)),
                   pl.BlockSpec((t, LANES), lambda p, i: (0, p))],
        out_shape=[jax.ShapeDtypeStruct((t, SB_WIDTH), F32)] * 3,
        compiler_params=pltpu.CompilerParams(dimension_semantics=("arbitrary", "arbitrary")),
    )(q, k, v, tot, do)


@jax.custom_vjp
def _sb_attention(q, k, v):
    return _sb_fwd(q, k, v)[0]


def _sb_attention_fwd(q, k, v):
    o, tot = _sb_fwd(q, k, v)
    return o, (q, k, v, tot)


def _sb_attention_bwd(res, do):
    return tuple(_sb_bwd(*res, do))


_sb_attention.defvjp(_sb_attention_fwd, _sb_attention_bwd)


def _mla_fwd(qn, qr, kn, kr, v):
    t = qn.shape[0]
    nq = t // ATT_BLK
    scale = MLA_QK ** -0.5

    def body(qn_ref, qr_ref, kn_ref, kr_ref, v_ref, o_ref, lse_ref):
        qi = pl.program_id(1)
        lanes = [slice(hh * LANES, (hh + 1) * LANES) for hh in range(2)]
        qnb = [qn_ref[:, sl].astype(BF16) for sl in lanes]
        qrb = [qr_ref[:, sl].astype(BF16) for sl in lanes]

        def blocks(kbs, carry, diagonal):
            nb = len(kbs)
            chains = [(b, hh) for b in range(nb) for hh in range(2)]
            offs = [pl.multiple_of(kb * ATT_BLK, ATT_BLK) for kb in kbs]
            krbs = [kr_ref[pl.ds(off, ATT_BLK), :].astype(BF16) for off in offs]
            accs, ms, ls = [carry[0], carry[3]], [carry[1], carry[4]], [carry[2], carry[5]]
            ss = {(b, hh): (_nt(qnb[hh], kn_ref[pl.ds(offs[b], ATT_BLK), lanes[hh]].astype(BF16))
                            + _nt(qrb[hh], krbs[b])) * scale for b, hh in chains}
            if any(diagonal):
                causal = (lax.broadcasted_iota(jnp.int32, (ATT_BLK, ATT_BLK), 1)
                          <= lax.broadcasted_iota(jnp.int32, (ATT_BLK, ATT_BLK), 0))
                ss = {ch: jnp.where(causal, ss[ch], -jnp.inf) if diagonal[ch[0]] else ss[ch] for ch in chains}
            m_new = list(ms)
            for b, hh in chains:
                m_new[hh] = jnp.maximum(m_new[hh], jnp.max(ss[(b, hh)], axis=-1, keepdims=True))
            ps = {(b, hh): jnp.exp(ss[(b, hh)] - m_new[hh]) for b, hh in chains}
            alphas = [jnp.exp(ms[hh] - m_new[hh]) for hh in range(2)]
            pvs = {(b, hh): jnp.dot(ps[(b, hh)].astype(BF16), v_ref[pl.ds(offs[b], ATT_BLK), lanes[hh]].astype(BF16),
                                    preferred_element_type=F32) for b, hh in chains}
            out = []
            for hh in range(2):
                acc, l = accs[hh] * alphas[hh], ls[hh] * alphas[hh]
                for b in range(nb):
                    acc, l = acc + pvs[(b, hh)], l + jnp.sum(ps[(b, hh)], axis=-1, keepdims=True)
                out += [acc, m_new[hh], l]
            return tuple(out)

        init = (jnp.zeros((ATT_BLK, LANES), F32), jnp.full((ATT_BLK, 1), -jnp.inf, F32), jnp.zeros((ATT_BLK, 1), F32))
        carry = lax.cond(qi % 2 == 1, lambda cr: blocks([qi, qi - 1], cr, (True, False)),
                         lambda cr: blocks([qi], cr, (True,)), init + init)
        carry = lax.fori_loop(0, qi // 2, lambda pr, cr: blocks([2 * pr, 2 * pr + 1], cr, (False, False)), carry)
        for hh in range(2):
            acc, m, l = carry[3 * hh:3 * hh + 3]
            o_ref[:, lanes[hh]] = acc / l
            lse_ref[:, lanes[hh]] = jnp.broadcast_to(m + jnp.log(l), (ATT_BLK, LANES))

    blk = pl.BlockSpec((ATT_BLK, 2 * LANES), lambda p, i: (i, p))
    full = pl.BlockSpec((t, 2 * LANES), lambda p, i: (0, p))
    return pl.pallas_call(
        body, name="mla_attn_fwd", grid=(MLA_HEADS // 2, nq),
        in_specs=[blk, blk, full, pl.BlockSpec((t, LANES), lambda p, i: (0, 0)), full],
        out_specs=[blk, blk],
        out_shape=[jax.ShapeDtypeStruct((t, MLA_HEADS * LANES), F32)] * 2,
        compiler_params=pltpu.CompilerParams(dimension_semantics=("arbitrary", "arbitrary")),
    )(qn, qr, kn, kr, v)


def _mla_bwd(qn, qr, kn, kr, v, o, lse, do):
    t = qn.shape[0]
    nq = t // ATT_BLK
    scale = MLA_QK ** -0.5

    def body(qn_ref, qr_ref, kn_ref, kr_ref, v_ref, o_ref, lse_ref, do_ref,
             dqn_ref, dqr_ref, dkn_ref, dkr_ref, dv_ref):
        pair = pl.program_id(0)
        qi = pl.program_id(1)

        @pl.when(qi == 0)
        def _():
            dkn_ref[...] = jnp.zeros_like(dkn_ref)
            dv_ref[...] = jnp.zeros_like(dv_ref)

        @pl.when((qi == 0) & (pair == 0))
        def _():
            dkr_ref[...] = jnp.zeros_like(dkr_ref)

        lanes = [slice(hh * LANES, (hh + 1) * LANES) for hh in range(2)]
        qnb = [qn_ref[:, sl].astype(BF16) for sl in lanes]
        qrb = [qr_ref[:, sl].astype(BF16) for sl in lanes]
        dob = [do_ref[:, sl].astype(BF16) for sl in lanes]
        delta = [jnp.sum(do_ref[:, sl] * o_ref[:, sl], axis=-1, keepdims=True) for sl in lanes]
        lse_v = [lse_ref[:, hh * LANES:hh * LANES + 1] for hh in range(2)]

        def blocks(kbs, carry, diagonal):
            nb = len(kbs)
            chains = [(b, hh) for b in range(nb) for hh in range(2)]
            offs = [pl.multiple_of(kb * ATT_BLK, ATT_BLK) for kb in kbs]
            krbs = [kr_ref[pl.ds(off, ATT_BLK), :].astype(BF16) for off in offs]
            knb = {(b, hh): kn_ref[pl.ds(offs[b], ATT_BLK), lanes[hh]].astype(BF16) for b, hh in chains}
            vb = {(b, hh): v_ref[pl.ds(offs[b], ATT_BLK), lanes[hh]].astype(BF16) for b, hh in chains}
            ss = {(b, hh): _nt(qnb[hh], knb[(b, hh)]) + _nt(qrb[hh], krbs[b]) for b, hh in chains}
            dps = {(b, hh): _nt(dob[hh], vb[(b, hh)]) for b, hh in chains}
            ps = {(b, hh): jnp.exp(ss[(b, hh)] * scale - lse_v[hh]) for b, hh in chains}
            if any(diagonal):
                causal = (lax.broadcasted_iota(jnp.int32, (ATT_BLK, ATT_BLK), 1)
                          <= lax.broadcasted_iota(jnp.int32, (ATT_BLK, ATT_BLK), 0))
                ps = {ch: jnp.where(causal, ps[ch], 0.0) if diagonal[ch[0]] else ps[ch] for ch in chains}
            dss = {(b, hh): (ps[(b, hh)] * (dps[(b, hh)] - delta[hh]) * scale).astype(BF16) for b, hh in chains}
            for b, hh in chains:
                dv_ref[pl.ds(offs[b], ATT_BLK), lanes[hh]] += _tn(ps[(b, hh)].astype(BF16), dob[hh])
            for b, hh in chains:
                dkn_ref[pl.ds(offs[b], ATT_BLK), lanes[hh]] += _tn(dss[(b, hh)], qnb[hh])
            for b in range(nb):
                dkr_ref[pl.ds(offs[b], ATT_BLK), :] += _tn(dss[(b, 0)], qrb[0]) + _tn(dss[(b, 1)], qrb[1])
            out = list(carry)
            for b, hh in chains:
                out[2 * hh] = out[2 * hh] + jnp.dot(dss[(b, hh)], knb[(b, hh)], preferred_element_type=F32)
                out[2 * hh + 1] = out[2 * hh + 1] + jnp.dot(dss[(b, hh)], krbs[b], preferred_element_type=F32)
            return tuple(out)

        zero = jnp.zeros((ATT_BLK, LANES), F32)
        carry = lax.fori_loop(0, qi // 2, lambda pr, cr: blocks([2 * pr, 2 * pr + 1], cr, (False, False)),
                              (zero, zero, zero, zero))
        carry = lax.cond(qi % 2 == 1, lambda cr: blocks([qi - 1, qi], cr, (False, True)),
                         lambda cr: blocks([qi], cr, (True,)), carry)
        for hh in range(2):
            dqn_ref[:, lanes[hh]] = carry[2 * hh]
            dqr_ref[:, lanes[hh]] = carry[2 * hh + 1]

    blk = pl.BlockSpec((ATT_BLK, 2 * LANES), lambda p, i: (i, p))
    full = pl.BlockSpec((t, 2 * LANES), lambda p, i: (0, p))
    shared = pl.BlockSpec((t, LANES), lambda p, i: (0, 0))
    wide = jax.ShapeDtypeStruct((t, MLA_HEADS * LANES), F32)
    return pl.pallas_call(
        body, name="mla_attn_bwd", grid=(MLA_HEADS // 2, nq),
        in_specs=[blk, blk, full, shared, full, blk, blk, blk],
        out_specs=[blk, blk, full, shared, full],
        out_shape=[wide, wide, wide, jax.ShapeDtypeStruct((t, LANES), F32), wide],
        compiler_params=pltpu.CompilerParams(dimension_semantics=("arbitrary", "arbitrary")),
    )(qn, qr, kn, kr, v, o, lse, do)


@jax.custom_vjp
def _mla_attention(qn, qr, kn, kr, v):
    return _mla_fwd(qn, qr, kn, kr, v)[0]


def _mla_attention_fwd(qn, qr, kn, kr, v):
    o, lse = _mla_fwd(qn, qr, kn, kr, v)
    return o, (qn, qr, kn, kr, v, o, lse)


def _mla_attention_bwd(res, do):
    return tuple(_mla_bwd(*res, do))


_mla_attention.defvjp(_mla_attention_fwd, _mla_attention_bwd)


def _ffn_in(h, wg, wu):
    t, k = h.shape
    n_sh, cc, _ = wg.shape

    def body(h_ref, wg_ref, wu_ref, g_ref, u_ref, a_ref):
        hb = h_ref[...].astype(BF16)
        for j in range(n_sh):
            cols = slice(j * cc, (j + 1) * cc)
            g = _nt(hb, wg_ref[j])
            u = _nt(hb, wu_ref[j])
            g_ref[:, cols] = g.astype(BF16)
            u_ref[:, cols] = u.astype(BF16)
            a_ref[:, cols] = _f_swiglu(g, u)[0].astype(BF16)

    w_spec = pl.BlockSpec((n_sh, cc, k), lambda i: (0, 0, 0))
    o_spec = pl.BlockSpec((MM_ROW_TILE, n_sh * cc), lambda i: (i, 0))
    wide = jax.ShapeDtypeStruct((t, n_sh * cc), BF16)
    return pl.pallas_call(
        body, name="ffn_in_fwd", grid=(t // MM_ROW_TILE,),
        in_specs=[pl.BlockSpec((MM_ROW_TILE, k), lambda i: (i, 0)), w_spec, w_spec],
        out_specs=[o_spec, o_spec, o_spec],
        out_shape=[wide, wide, wide],
        compiler_params=pltpu.CompilerParams(dimension_semantics=("arbitrary",), vmem_limit_bytes=MM_VMEM_LIMIT),
    )(h, wg, wu)


def _ffn_mid_bwd(dy, wd, g, u):
    t, n = dy.shape
    n_sh, cc, _ = wd.shape

    def body(dy_ref, wd_ref, g_ref, u_ref, dg_ref, du_ref):
        d_act = _nt(dy_ref[...].astype(BF16), wd_ref[...])
        g = g_ref[...].astype(F32)
        sig = 1.0 / (1.0 + jnp.exp(-g))
        dg_ref[...] = (d_act * u_ref[...].astype(F32) * (sig * (1.0 + g * (1.0 - sig)))).astype(BF16)
        du_ref[...] = (d_act * (g * sig)).astype(BF16)

    blk = pl.BlockSpec((MM_ROW_TILE, cc), lambda j, i: (i, j))
    wide = jax.ShapeDtypeStruct((t, n_sh * cc), BF16)
    return pl.pallas_call(
        body, name="ffn_mid_bwd", grid=(n_sh, t // MM_ROW_TILE),
        in_specs=[pl.BlockSpec((MM_ROW_TILE, n), lambda j, i: (i, 0)),
                  pl.BlockSpec((None, cc, n), lambda j, i: (j, 0, 0)), blk, blk],
        out_specs=[blk, blk], out_shape=[wide, wide],
        compiler_params=pltpu.CompilerParams(dimension_semantics=("arbitrary", "arbitrary"),
                                             vmem_limit_bytes=MM_VMEM_LIMIT),
    )(dy, wd, g, u)


def _ffn_dh(dg, du, wg, wu):
    t = dg.shape[0]
    n_sh, cc, k = wg.shape

    def body(dg_ref, du_ref, wg_ref, wu_ref, o_ref):
        acc = jnp.zeros((MM_ROW_TILE, k), F32)
        for j in range(n_sh):
            cols = slice(j * cc, (j + 1) * cc)
            acc = (acc + jnp.dot(dg_ref[:, cols], wg_ref[j], preferred_element_type=F32)
                   + jnp.dot(du_ref[:, cols], wu_ref[j], preferred_element_type=F32))
        o_ref[...] = acc

    blk = pl.BlockSpec((MM_ROW_TILE, n_sh * cc), lambda i: (i, 0))
    w_spec = pl.BlockSpec((n_sh, cc, k), lambda i: (0, 0, 0))
    return pl.pallas_call(
        body, name="ffn_dh", grid=(t // MM_ROW_TILE,),
        in_specs=[blk, blk, w_spec, w_spec],
        out_specs=pl.BlockSpec((MM_ROW_TILE, k), lambda i: (i, 0)),
        out_shape=jax.ShapeDtypeStruct((t, k), F32),
        compiler_params=pltpu.CompilerParams(dimension_semantics=("arbitrary",), vmem_limit_bytes=MM_VMEM_LIMIT),
    )(dg, du, wg, wu)


def _ffn_dw_in(h, dg, du, n_sh):
    t, k = h.shape
    cc = dg.shape[1] // n_sh
    tk = 512

    def body(h_ref, dg_ref, du_ref, og_ref, ou_ref):
        hb = h_ref[...].astype(BF16)
        og_ref[...] = _tn(dg_ref[...], hb).astype(BF16)
        ou_ref[...] = _tn(du_ref[...], hb).astype(BF16)

    d_spec = pl.BlockSpec((t, cc), lambda i, j: (0, j))
    o_spec = pl.BlockSpec((None, cc, tk), lambda i, j: (j, 0, i))
    out = jax.ShapeDtypeStruct((n_sh, cc, k), BF16)
    return pl.pallas_call(
        body, name="ffn_gate_up_dw", grid=(k // tk, n_sh),
        in_specs=[pl.BlockSpec((t, tk), lambda i, j: (0, i)), d_spec, d_spec],
        out_specs=[o_spec, o_spec], out_shape=[out, out],
        compiler_params=pltpu.CompilerParams(dimension_semantics=("arbitrary", "arbitrary"),
                                             vmem_limit_bytes=MM_VMEM_LIMIT),
    )(h, dg, du)


@jax.custom_vjp
def _ffn_block(h, wg, wu, wd):
    act = _ffn_in(h, wg, wu)[2]
    return _mm(act, wd.reshape(-1, wd.shape[2]), "nn", "ffn_down_fwd", MM_ROW_TILE, wd.shape[2])


def _ffn_block_fwd(h, wg, wu, wd):
    g, u, act = _ffn_in(h, wg, wu)
    y = _mm(act, wd.reshape(-1, wd.shape[2]), "nn", "ffn_down_fwd", MM_ROW_TILE, wd.shape[2])
    return y, (h, wg, wu, wd, g, u, act)


def _ffn_block_bwd(res, dy):
    h, wg, wu, wd, g, u, act = res
    dg, du = _ffn_mid_bwd(dy, wd, g, u)
    dh = _ffn_dh(dg, du, wg, wu)
    n_sh = wg.shape[0]
    dwg, dwu = _ffn_dw_in(h, dg, du, n_sh)
    dwd = _mm(act, dy, "tn", "ffn_down_dw", 256, wd.shape[2], out_dtype=BF16).reshape(wd.shape)
    return dh, dwg, dwu, dwd


_ffn_block.defvjp(_ffn_block_fwd, _ffn_block_bwd)


def _swap_halves(w):
    half = w.shape[-1] // 2
    return jnp.concatenate([w[..., half:], w[..., :half]], axis=-1)


def _pad_lanes(w):
    return jnp.concatenate([w, jnp.zeros(w.shape[:-1] + (LANES - w.shape[-1],), w.dtype)], axis=-1)


def _join_cols(shards):
    return shards.transpose(1, 0, 2).reshape(shards.shape[1], -1)


def _mod_parts(mod):
    return [mod[:, i * D_MODEL:(i + 1) * D_MODEL] for i in range(N_MOD)]


def _mixing_stage(x, mod, p, cos, sin):
    shift1, scale1 = _mod_parts(mod)[:2]

    w_in_t = p["w_in"].reshape(-1, D_MODEL)
    k_rope_rows = w_in_t[2176:2240]

    def pad_rows(a):
        return jnp.concatenate([a, jnp.zeros((LANES - a.shape[0], D_MODEL), a.dtype)], axis=0)

    swapped = jnp.concatenate([k_rope_rows[MLA_ROPE // 2:], k_rope_rows[:MLA_ROPE // 2]], axis=0)
    w_in_ext = jnp.concatenate([w_in_t[:2176], pad_rows(k_rope_rows), pad_rows(swapped),
                                jnp.zeros((LANES, D_MODEL), w_in_t.dtype)], axis=0)
    h1, x_res = _make_rowwise("pre_attn", _f_pre_attn, 1, 3, [D_MODEL, D_MODEL], [True], out_dtypes=[BF16, F32])(
        x, p["norm_attn"], scale1, shift1)
    q_sb, k_sb, v_sb, cq, ckv, kr, kr_sw = _make_linear_split_t(
        "in_proj", (SB_WIDTH, SB_WIDTH, SB_WIDTH, MLA_Q_RANK, MLA_KV_RANK, LANES, LANES), 512)(h1, w_in_ext)

    o_sb = _sb_attention(q_sb, k_sb, v_sb)

    wq = _join_cols(p["w_q_up"]).reshape(MLA_Q_RANK, MLA_HEADS, MLA_QK)
    wq_n, wq_r = wq[:, :, :MLA_NOPE], wq[:, :, MLA_NOPE:]
    w_q_ext = jnp.concatenate([wq_n.reshape(MLA_Q_RANK, -1), _pad_lanes(wq_r).reshape(MLA_Q_RANK, -1),
                               _pad_lanes(_swap_halves(wq_r)).reshape(MLA_Q_RANK, -1)], axis=1)
    wkv = _join_cols(p["w_kv_up"]).reshape(MLA_KV_RANK, MLA_HEADS, MLA_NOPE + MLA_V)
    w_kv_ext = jnp.concatenate([wkv[:, :, :MLA_NOPE].reshape(MLA_KV_RANK, -1),
                                wkv[:, :, MLA_NOPE:].reshape(MLA_KV_RANK, -1)], axis=1)
    cqn, ckvn = _make_rowwise("mla_a", _f_mla_a, 2, 2, [MLA_Q_RANK, MLA_KV_RANK], [True, True],
                              out_dtypes=[BF16, BF16], grad_dtypes=[BF16, BF16])(
        cq, ckv, p["q_a_norm"], p["kv_a_norm"])
    qall = _make_linear("q_up", 384, 768)(cqn, w_q_ext)
    kn_all, v_mla = _make_linear_split("kv_up", (MLA_HEADS * MLA_NOPE, MLA_HEADS * MLA_V), MLA_KV_RANK)(ckvn, w_kv_ext)
    gq = p["q_norm"]
    gkr = p["k_rope_norm"]
    qn, qr, kn, krr = _make_rowwise("mla_b", _f_mla_b, 6, 6, [512, 512, 512, LANES],
                                    [True, True, True, True, False, False],
                                    out_dtypes=[BF16] * 4, grad_dtypes=[BF16] * 4)(
        qall, kn_all, kr, kr_sw, cos, sin,
        gq[:, :MLA_NOPE], _pad_lanes(gq[:, MLA_NOPE:]), _pad_lanes(_swap_halves(gq[:, MLA_NOPE:])),
        p["k_nope_norm"], _pad_lanes(gkr), _pad_lanes(_swap_halves(gkr)))
    o_mla = _mla_attention(qn, qr, kn, krr, v_mla)

    (mixed,) = _make_rowwise("post_attn", _f_post_attn, 2, 2, [D_MODEL], [True, True])(
        o_sb, o_mla, p["out_norm_sb"], p["out_norm_mla"])
    return mixed, x_res


def _ffn_stage(x, mixed, mod, p):
    _, _, gate1, shift2, scale2, _ = _mod_parts(mod)
    attn = _make_linear("out_proj", 512, 512)(mixed, p["w_out"].reshape(D_MODEL, D_MODEL))

    x2, h2 = _make_rowwise("pre_ffn", _f_pre_ffn, 2, 4, [D_MODEL, D_MODEL], [True, True],
                           out_dtypes=[F32, BF16], grad_dtypes=[F32, BF16])(
        x, attn, gate1, p["norm_ffn"], scale2, shift2)
    return x2, _ffn_block(h2, p["w_gate"], p["w_up"], p["w_down"])


def _my_place():
    return lax.axis_index("x"), lax.axis_index("y"), lax.axis_index("c")


def _small_gather(x_ref, out_ref, send_sems, recv_sems, base, local_sem):
    m_per = x_ref.shape[0]
    x, y, c = _my_place()
    me, sibling = (x, y, c), (x, y, 1 - c)
    chips = [(1 - x, y), (x, 1 - y), (1 - x, 1 - y)]

    def rows(px, py, pc):
        return out_ref.at[pl.ds((4 * px + 2 * py + pc) * m_per, m_per), :]

    def copy(k, blk, to, src=None):
        return _remote(rows(*blk) if src is None else src, rows(*blk), send_sems, recv_sems, base + k, to)

    mine = pltpu.make_async_copy(x_ref, rows(*me), local_sem)
    first = [copy(0, me, sibling, src=x_ref)] + [copy(1 + j, me, (*chip, c), src=x_ref) for j, chip in enumerate(chips)]
    passed = [copy(4 + j, (*chip, c), sibling) for j, chip in enumerate(chips)]

    def start():
        mine.start()
        for cp in first:
            cp.start()

    def finish():
        for j, chip in enumerate(chips):
            copy(1 + j, (*chip, c), me).wait_recv()
            passed[j].start()
        copy(0, sibling, me).wait_recv()
        for j, chip in enumerate(chips):
            copy(4 + j, (*chip, 1 - c), me).wait_recv()
        for cp in first + passed:
            cp.wait_send()
        mine.wait()

    return start, finish


EARLY =("w_in", "w_q_up", "w_kv_up")
LATE = ("w_out", "w_gate", "w_up", "w_down")
BIG = EARLY + LATE
TRAVELS_TRANSPOSED = ("w_in", "w_gate", "w_up")
HALF_AXIS = {"w_in": 1, "w_q_up": 0, "w_kv_up": 0, "w_out": 0, "w_gate": 1, "w_up": 1, "w_down": 1}


def _half(ref, h, axis, lead=()):
    trail = ref.shape[len(lead):]
    idx = list(lead) + [slice(None)] * len(trail)
    at = len(trail) - 2 + axis
    n2 = trail[at] // 2
    idx[len(lead) + at] = pl.ds(h * n2, n2)
    return ref.at[tuple(idx)]


def _half_shape(shape, axis):
    shape = list(shape)
    shape[len(shape) - 2 + axis] //= 2
    return tuple(shape)


def _remote(src, dst, send_sems, recv_sems, k, to):
    return pltpu.make_async_remote_copy(src_ref=src, dst_ref=dst, send_sem=send_sems.at[k],
                                        recv_sem=recv_sems.at[k], device_id=to, device_id_type=MESH)


def _gather_weights(names, shards, lands, small_block):
    n_w = len(shards)
    axes = [HALF_AXIS[n] for n in names]

    def body(*refs):
        w_refs, small_ref = refs[:n_w], refs[2 * n_w]
        out_refs, token, small_out = refs[2 * n_w + 1:3 * n_w + 1], refs[3 * n_w + 1], refs[3 * n_w + 2]
        send_sems, recv_sems, local_sem = refs[3 * n_w + 3:]
        token[...] = jnp.zeros_like(token)
        x, y, c = _my_place()
        sibling = (x, y, 1 - c)
        chips = [(1 - x, y), (x, 1 - y), (1 - x, 1 - y)]
        me = 2 * x + y
        small_start, small_finish = _small_gather(small_ref, small_out, send_sems, recv_sems, 6 * n_w, local_sem)
        small_start()
        first = [_remote(_half(w_refs[i], c, axes[i]), _half(out_refs[i], c, axes[i], (me,)),
                         send_sems, recv_sems, 6 * i + j, (*chip, c))
                 for i in range(n_w) for j, chip in enumerate(chips)]
        for cp in first:
            cp.start()
        small_finish()
        passed = []
        for j, (cx, cy) in enumerate(chips):
            for i in range(n_w):
                blk = _half(out_refs[i], c, axes[i], (2 * cx + cy,))
                _remote(blk, blk, send_sems, recv_sems, 6 * i + j, (cx, cy, c)).wait_recv()
                cp = _remote(blk, blk, send_sems, recv_sems, 6 * i + 3 + j, sibling)
                cp.start()
                passed.append(cp)
        for j, (cx, cy) in enumerate(chips):
            for i in range(n_w):
                blk = _half(out_refs[i], 1 - c, axes[i], (2 * cx + cy,))
                _remote(blk, blk, send_sems, recv_sems, 6 * i + 3 + j, sibling).wait_recv()
        for cp in first + passed:
            cp.wait_send()

    outs = pl.pallas_call(
        body, name="gather_weights",
        out_shape=[jax.ShapeDtypeStruct(a.shape, a.dtype) for a in lands]
        + [jax.ShapeDtypeStruct((8, LANES), F32),
           jax.ShapeDtypeStruct((N_DEV * small_block.shape[0], small_block.shape[1]), small_block.dtype)],
        in_specs=[ANY] * (2 * n_w + 1), out_specs=[ANY] * n_w + [pl.BlockSpec(memory_space=pltpu.VMEM), ANY],
        input_output_aliases={n_w + i: i for i in range(n_w)},
        scratch_shapes=[pltpu.SemaphoreType.DMA((6 * n_w + 7,)), pltpu.SemaphoreType.DMA((6 * n_w + 7,)),
                        pltpu.SemaphoreType.DMA],
    )(*shards, *lands, small_block)
    return outs[:n_w], outs[n_w], outs[n_w + 1]


def _pair_exchange(names, grads, call_name, small_block):
    n_w = len(grads)
    axes = [HALF_AXIS[n] for n in names]

    def body(*refs):
        g_refs, small_ref = refs[:n_w], refs[n_w]
        t_refs, small_out = refs[n_w + 1:2 * n_w + 1], refs[2 * n_w + 1]
        send_sems, recv_sems, local_sem = refs[2 * n_w + 2:]
        x, y, c = _my_place()
        small_start, small_finish = _small_gather(small_ref, small_out, send_sems, recv_sems, n_w, local_sem)
        small_start()
        sends = [_remote(_half(g_refs[i], 1 - c, axes[i]), t_refs[i], send_sems, recv_sems, i, (x, y, 1 - c))
                 for i in range(n_w)]
        for cp in sends:
            cp.start()
        small_finish()
        for cp in sends:
            cp.wait_recv()
        for cp in sends:
            cp.wait_send()

    outs = pl.pallas_call(
        body, name=call_name,
        out_shape=[jax.ShapeDtypeStruct(_half_shape(g.shape, a), g.dtype) for g, a in zip(grads, axes)]
        + [jax.ShapeDtypeStruct((N_DEV * small_block.shape[0], small_block.shape[1]), small_block.dtype)],
        in_specs=[ANY] * (n_w + 1), out_specs=[ANY] * (n_w + 1),
        scratch_shapes=[pltpu.SemaphoreType.DMA((n_w + 7,)), pltpu.SemaphoreType.DMA((n_w + 7,)),
                        pltpu.SemaphoreType.DMA],
    )(*grads, small_block)
    return outs[:n_w], outs[n_w]


def _sibling_join(halves, name, after):
    n_w = len(halves)

    def body(*refs):
        s_refs, j_refs = refs[:n_w], refs[n_w + 1:2 * n_w + 1]
        send_sems, recv_sems = refs[2 * n_w + 1:]
        x, y, c = _my_place()
        sends = [_remote(s_refs[i], j_refs[i], send_sems, recv_sems, i, (x, y, 1 - c)) for i in range(n_w)]
        for cp in sends:
            cp.start()
        for cp in sends:
            cp.wait_recv()
        for cp in sends:
            cp.wait_send()

    return pl.pallas_call(
        body, name=name,
        out_shape=[jax.ShapeDtypeStruct(s.shape, s.dtype) for s in halves],
        in_specs=[ANY] * (n_w + 1), out_specs=[ANY] * n_w,
        scratch_shapes=[pltpu.SemaphoreType.DMA((n_w,)), pltpu.SemaphoreType.DMA((n_w,))],
    )(*halves, after)


HBM_SPEC = pl.BlockSpec(memory_space=pltpu.HBM)
SEM_SPEC = pl.BlockSpec(memory_space=pltpu.SEMAPHORE)
DATAFLOW = pltpu.SideEffectType.DATAFLOW_SIDE_EFFECTING


def _in_hbm(a):
    return pltpu.with_memory_space_constraint(a, pltpu.HBM)


def _exchange_start(name, srcs, lands, plan, n_copies, after, thru):
    n = len(srcs)

    def body(*refs):
        src_refs, land_refs = refs[:n], refs[n:2 * n]
        send_sems, recv_sems = refs[2 * n + 2], refs[2 * n + 3]
        for k, (src, dst, to, k_recv) in enumerate(plan(src_refs, land_refs)):
            pltpu.make_async_remote_copy(src_ref=src, dst_ref=dst, send_sem=send_sems.at[k],
                                         recv_sem=recv_sems.at[k_recv], device_id=to, device_id_type=MESH).start()

    outs = pl.pallas_call(
        body, name=name,
        out_shape=(pltpu.SemaphoreType.DMA((n_copies,)), pltpu.SemaphoreType.DMA((n_copies,)),
                   *[pltpu.HBM(a.shape, a.dtype) for a in list(srcs) + list(lands) + [thru]]),
        in_specs=[HBM_SPEC] * (2 * n + 1) + [ANY],
        out_specs=(SEM_SPEC, SEM_SPEC, *[HBM_SPEC] * (2 * n + 1)),
        input_output_aliases={i: 2 + i for i in range(2 * n + 1)},
        compiler_params=pltpu.CompilerParams(has_side_effects=DATAFLOW),
    )(*[_in_hbm(a) for a in list(srcs) + list(lands) + [thru]], after)
    return outs[0], outs[1], outs[2:2 + n], outs[2 + n:2 + 2 * n], outs[2 + 2 * n]


def _exchange_wait(name, started, plan, after):
    send_sems, recv_sems, srcs, lands, _ = started
    n = len(srcs)

    def body(*refs):
        src_refs, land_refs = refs[:n], refs[n:2 * n]
        s_sems, r_sems = refs[2 * n], refs[2 * n + 1]
        for k, (src, dst, to, _) in enumerate(plan(src_refs, land_refs)):
            cp = _remote(src, dst, s_sems, r_sems, k, to)
            cp.wait_send()
            cp.wait_recv()

    outs = pl.pallas_call(
        body, name=name,
        out_shape=tuple(pltpu.HBM(a.shape, a.dtype) for a in list(srcs) + list(lands)),
        in_specs=[HBM_SPEC] * (2 * n) + [SEM_SPEC, SEM_SPEC, ANY],
        out_specs=tuple([HBM_SPEC] * (2 * n)),
        input_output_aliases={i: i for i in range(2 * n)},
        compiler_params=pltpu.CompilerParams(has_side_effects=DATAFLOW),
    )(*srcs, *lands, send_sems, recv_sems, after)
    return outs[:n], outs[n:]


def _late_gather_plan(src_refs, land_refs):
    x, y, c = _my_place()
    chips = [(1 - x, y), (x, 1 - y), (1 - x, 1 - y)]
    plan = [(src, land.at[2 * x + y], (cx, cy, c)) for src, land in zip(src_refs, land_refs) for cx, cy in chips]
    return [entry + (k,) for k, entry in enumerate(plan)]


def _late_scatter_plan(src_refs, land_refs):
    x, y, c = _my_place()
    chips = [(1 - x, y), (x, 1 - y), (1 - x, 1 - y)]
    plan = [(src.at[2 * cx + cy], land.at[j], (cx, cy, c))
            for src, land in zip(src_refs, land_refs) for j, (cx, cy) in enumerate(chips)]
    return [entry + (k,) for k, entry in enumerate(plan)]


def _direct_scatter_plan(names):
    axes = [HALF_AXIS[n] for n in names]

    def plan(src_refs, land_refs):
        x, y, c = _my_place()
        chips = [(1 - x, y), (x, 1 - y), (1 - x, 1 - y)]
        out = []
        for i, (src, land) in enumerate(zip(src_refs, land_refs)):
            for f, (cx, cy) in enumerate(chips):
                for core in range(2):
                    out.append((_half(src, core, axes[i], (2 * cx + cy,)), land.at[2 * f + c], (cx, cy, core),
                                7 * i + 2 * f + c))
            out.append((_half(src, 1 - c, axes[i], (2 * x + y,)), land.at[6], (x, y, 1 - c), 7 * i + 6))
        return out

    return plan


def _row_tile(rows, mult=16, limit=ROW_TILE):
    return max(d for d in range(mult, limit + 1, mult) if rows % d == 0)


def _pair_sum(place, g, theirs, axis, name):
    nj, rr, cc = theirs.shape
    tr = _row_tile(rr, limit=1024)
    nb = rr // tr
    if axis == 0:
        g_map = lambda j, i, pr: (j, pr[0] * nb + i, 0)
    else:
        g_map = lambda j, i, pr: (j, i, pr[0])

    def body(pr, g_ref, t_ref, o_ref):
        o_ref[...] = (g_ref[...].astype(F32) + t_ref[...].astype(F32)).astype(BF16)

    spec = pl.BlockSpec((None, tr, cc), lambda j, i, pr: (j, i, 0))
    return pl.pallas_call(
        body, name=name,
        grid_spec=pltpu.PrefetchScalarGridSpec(
            num_scalar_prefetch=1, grid=(nj, nb),
            in_specs=[pl.BlockSpec((None, tr, cc), g_map), spec], out_specs=spec),
        out_shape=jax.ShapeDtypeStruct(theirs.shape, BF16))(place, g, theirs)


def _chip_sum(place, pair_sums, parts, name):
    _, rr, cc = parts.shape
    tr = _row_tile(rr, limit=1024)

    def body(pr, h_ref, p_ref, o_ref):
        acc = p_ref[0].astype(F32)
        for j in range(1, N_CHIPS - 1):
            acc = acc + p_ref[j].astype(F32)
        o_ref[...] = (acc + h_ref[...].astype(F32)).astype(BF16)

    return pl.pallas_call(
        body, name=name,
        grid_spec=pltpu.PrefetchScalarGridSpec(
            num_scalar_prefetch=1, grid=(rr // tr,),
            in_specs=[pl.BlockSpec((None, tr, cc), lambda i, pr: (pr[1], i, 0)),
                      pl.BlockSpec((N_CHIPS - 1, tr, cc), lambda i, pr: (0, i, 0))],
            out_specs=pl.BlockSpec((tr, cc), lambda i, pr: (i, 0))),
        out_shape=jax.ShapeDtypeStruct((rr, cc), BF16))(place, pair_sums, parts)


def _chip_sum_direct(place, g, parts, axis, name):
    n_parts, rr, cc = parts.shape
    tr = _row_tile(rr, limit=1024)
    nb = rr // tr
    if axis == 0:
        g_map = lambda i, pr: (pr[1], pr[0] * nb + i, 0)
    else:
        g_map = lambda i, pr: (pr[1], i, pr[0])

    def body(pr, g_ref, p_ref, o_ref):
        acc = p_ref[0].astype(F32)
        for j in range(1, n_parts):
            acc = acc + p_ref[j].astype(F32)
        o_ref[...] = (acc + g_ref[...].astype(F32)).astype(BF16)

    return pl.pallas_call(
        body, name=name,
        grid_spec=pltpu.PrefetchScalarGridSpec(
            num_scalar_prefetch=1, grid=(nb,),
            in_specs=[pl.BlockSpec((None, tr, cc), g_map), pl.BlockSpec((n_parts, tr, cc), lambda i, pr: (0, i, 0))],
            out_specs=pl.BlockSpec((tr, cc), lambda i, pr: (i, 0))),
        out_shape=jax.ShapeDtypeStruct((rr, cc), BF16))(place, g, parts)


def _silu(v):
    return v / (1.0 + jnp.exp(-v))


def _ada_fwd(c_all, w_shard, b_shard):
    n_seq, n_cols = c_all.shape[0], w_shard.shape[1]

    def body(c_ref, w_ref, b_ref, o_ref, mine_ref, send_sems, recv_sems, local_sem):
        mine_ref[...] = jnp.dot(_silu(c_ref[...]), w_ref[...], precision=lax.Precision.HIGHEST,
                                preferred_element_type=F32) + b_ref[...]
        start, finish = _small_gather(mine_ref, o_ref, send_sems, recv_sems, 0, local_sem)
        start()
        finish()

    return pl.pallas_call(
        body, name="ada_fwd", out_shape=jax.ShapeDtypeStruct((N_DEV * n_seq, n_cols), F32),
        scratch_shapes=[pltpu.VMEM((n_seq, n_cols), F32), pltpu.SemaphoreType.DMA((7,)), pltpu.SemaphoreType.DMA((7,)),
                        pltpu.SemaphoreType.DMA],
        compiler_params=pltpu.CompilerParams(vmem_limit_bytes=MM_VMEM_LIMIT))(c_all, w_shard, b_shard)


def _loss_and_grads(x2, ffn, target, gate):
    t, d = x2.shape

    def half_loss(x2_blk, ffn_blk, gate_row, target_blk):
        return 0.5 * _f_loss(x2_blk, ffn_blk, target_blk, gate_row)[0]

    def body(x2_ref, ffn_ref, tgt_ref, gate_ref, loss_ref, dx2_ref, dffn_ref, dgate_ref):
        rows, vjp = jax.vjp(lambda a, b, g: half_loss(a, b, g, tgt_ref[...]), x2_ref[...], ffn_ref[...], gate_ref[...])
        loss_ref[...] = rows
        dx2_ref[...], dffn_ref[...], dgate = vjp(jnp.ones_like(rows))

        @pl.when(pl.program_id(0) == 0)
        def _():
            dgate_ref[...] = jnp.zeros_like(dgate_ref)

        dgate_ref[...] += dgate

    blk = pl.BlockSpec((ROW_TILE, d), lambda i: (i, 0))
    row = pl.BlockSpec((1, d), lambda i: (0, 0))
    return pl.pallas_call(
        body, name="loss_and_grads", grid=(t // ROW_TILE,),
        in_specs=[blk, blk, blk, row],
        out_specs=[pl.BlockSpec((ROW_TILE, 1), lambda i: (i, 0)), blk, blk, row],
        out_shape=[jax.ShapeDtypeStruct((t, 1), F32), jax.ShapeDtypeStruct((t, d), F32), jax.ShapeDtypeStruct((t, d), F32),
                   jax.ShapeDtypeStruct((1, d), F32)],
        compiler_params=pltpu.CompilerParams(dimension_semantics=("arbitrary",), vmem_limit_bytes=MM_VMEM_LIMIT),
    )(x2, ffn, target, gate)


def _ada_bwd(c_all, dmod_cols):
    def body(c_ref, d_ref, o_ref):
        o_ref[...] = lax.dot_general(_silu(c_ref[...]), d_ref[...], (((0,), (0,)), ((), ())),
                                     precision=lax.Precision.HIGHEST, preferred_element_type=F32)

    return pl.pallas_call(body, name="ada_bwd", out_shape=jax.ShapeDtypeStruct((c_all.shape[1], dmod_cols.shape[1]), F32),
                          compiler_params=pltpu.CompilerParams(vmem_limit_bytes=MM_VMEM_LIMIT))(c_all, dmod_cols)


def _adamw_math(w, g, m, v):
    m = ADAM_B1 * m + (1.0 - ADAM_B1) * g
    v = ADAM_B2 * v + (1.0 - ADAM_B2) * (g * g)
    m_hat = m / (1.0 - ADAM_B1 ** ADAM_STEP)
    v_hat = v / (1.0 - ADAM_B2 ** ADAM_STEP)
    delta = -ADAM_LR * (m_hat / (jnp.sqrt(v_hat) + ADAM_EPS) + ADAM_WD * w)
    return delta, m, v


def _adamw(w, g, m, v, name):
    r, ccols = w.shape
    tr = max(d for d in range(8, ROW_TILE + 1, 8) if r % d == 0)
    spec = pl.BlockSpec((tr, ccols), lambda i: (i, 0))

    def body(w_ref, g_ref, m_ref, v_ref, d_ref, nm_ref, nv_ref):
        d_ref[...], nm_ref[...], nv_ref[...] = _adamw_math(w_ref[...], g_ref[...], m_ref[...], v_ref[...])

    return pl.pallas_call(body, name=name, grid=(r // tr,), in_specs=[spec] * 4, out_specs=[spec] * 3,
                          out_shape=[jax.ShapeDtypeStruct(w.shape, F32)] * 3,
                          compiler_params=pltpu.CompilerParams(vmem_limit_bytes=MM_VMEM_LIMIT))(w, g, m, v)


def _small_layout(sizes):
    offs, off = [], 0
    for n in sizes:
        offs.append(off)
        off += -(-n // LANES) * LANES
    total = -(-(off + LANES) // (8 * LANES)) * (8 * LANES)
    return offs, off, total


def _adamw_small(ws, g_all, ms, vs, offs, loss_off):
    n_p = len(ws)

    def device_sum(g_ref, off, width):
        blk = g_ref[:, off:off + width]
        acc = blk[0:1]
        for d in range(1, N_DEV):
            acc = acc + blk[d:d + 1]
        return acc

    def body(*refs):
        w_refs, m_refs, v_refs = refs[:n_p], refs[n_p:2 * n_p], refs[2 * n_p:3 * n_p]
        g_ref = refs[3 * n_p]
        outs = refs[3 * n_p + 1:]
        for i in range(n_p):
            n = w_refs[i].shape[1]
            g = device_sum(g_ref, offs[i], -(-n // LANES) * LANES)[:, :n]
            outs[i][...] = g
            outs[n_p + i][...], outs[2 * n_p + i][...], outs[3 * n_p + i][...] = _adamw_math(
                w_refs[i][...], g, m_refs[i][...], v_refs[i][...])
        outs[4 * n_p][...] = device_sum(g_ref, loss_off, LANES)

    res = pl.pallas_call(
        body, name="adamw_small",
        out_shape=[jax.ShapeDtypeStruct(a.shape, F32) for a in list(ws) * 4] + [jax.ShapeDtypeStruct((1, LANES), F32)],
    )(*ws, *ms, *vs, g_all)
    return res[:n_p], res[n_p:2 * n_p], res[2 * n_p:3 * n_p], res[3 * n_p:4 * n_p], res[4 * n_p]


def _adamw_halves(place, w, own, sib, m, v, axis, name, after):
    r, cc = w.shape
    if axis == 0:
        rows, gc = own.shape[0], own.shape[1]
        tr = _row_tile(rows)
        nb = rows // tr
        w_spec = pl.BlockSpec((tr, cc), lambda h, i, pr: (h * nb + i, 0))
        g_spec = pl.BlockSpec((tr, gc), lambda h, i, pr: (i, 0))
    else:
        tr = _row_tile(r)
        nb = r // tr
        gc = own.shape[1]
        w_spec = pl.BlockSpec((tr, gc), lambda h, i, pr: (i, h))
        g_spec = pl.BlockSpec((tr, gc), lambda h, i, pr: (i, 0))
    wc = w_spec.block_shape[1]

    def body(pr, w_ref, o_ref, s_ref, m_ref, v_ref, after_ref, g_ref, d_ref, nm_ref, nv_ref):
        g = jnp.where(pl.program_id(0) == pr[0], o_ref[...], s_ref[...]).astype(F32)[:, :wc]
        g_ref[...] = g
        d_ref[...], nm_ref[...], nv_ref[...] = _adamw_math(w_ref[...], g, m_ref[...], v_ref[...])

    return pl.pallas_call(
        body, name=name,
        grid_spec=pltpu.PrefetchScalarGridSpec(
            num_scalar_prefetch=1, grid=(2, nb),
            in_specs=[w_spec, g_spec, g_spec, w_spec, w_spec, ANY], out_specs=[w_spec] * 4),
        out_shape=[jax.ShapeDtypeStruct(w.shape, F32)] * 4,
        compiler_params=pltpu.CompilerParams(vmem_limit_bytes=MM_VMEM_LIMIT))(place, w, own, sib, m, v, after)


SMALL = ("b_ada", "norm_attn", "norm_ffn", "q_a_norm", "kv_a_norm", "q_norm", "k_nope_norm", "k_rope_norm",
         "out_norm_sb", "out_norm_mla")
WEIGHTS = ("w_ada", "b_ada", "norm_attn", "norm_ffn", "w_in", "q_a_norm", "w_q_up", "kv_a_norm", "w_kv_up",
           "q_norm", "k_nope_norm", "k_rope_norm", "out_norm_sb", "out_norm_mla", "w_out", "w_gate", "w_up",
           "w_down")


def kernel(x, c, positions, w_ada, b_ada, norm_attn, norm_ffn, w_in, q_a_norm, w_q_up, kv_a_norm, w_kv_up, q_norm, k_nope_norm, k_rope_norm, out_norm_sb, out_norm_mla, w_out, w_gate, w_up, w_down, loss_target, m_w_ada, m_b_ada, m_norm_attn, m_norm_ffn, m_w_in, m_q_a_norm, m_w_q_up, m_kv_a_norm, m_w_kv_up, m_q_norm, m_k_nope_norm, m_k_rope_norm, m_out_norm_sb, m_out_norm_mla, m_w_out, m_w_gate, m_w_up, m_w_down, v_w_ada, v_b_ada, v_norm_attn, v_norm_ffn, v_w_in, v_q_a_norm, v_w_q_up, v_kv_a_norm, v_w_kv_up, v_q_norm, v_k_nope_norm, v_k_rope_norm, v_out_norm_sb, v_out_norm_mla, v_w_out, v_w_gate, v_w_up, v_w_down):
    local = dict(locals())
    w = {n: local[n][0] for n in WEIGHTS}
    m = {n: local["m_" + n][0] for n in WEIGHTS}
    v = {n: local["v_" + n][0] for n in WEIGHTS}
    small = {n: w[n].reshape(1, -1) for n in SMALL}
    ix, iy, ic = _my_place()
    chip = 2 * ix + iy
    dev = 2 * chip + ic
    xs, target = x[0], loss_target[0]
    seq = xs.shape[0]

    ff_pad = FF_SHARD_PAD - FF_SHARD
    shards = {n: (w[n].T if n in TRAVELS_TRANSPOSED else w[n]).astype(BF16) for n in BIG}
    for n in ("w_gate", "w_up", "w_down"):
        shards[n] = jnp.pad(shards[n], ((0, ff_pad), (0, 0)))
    def landing(names):
        return [lax.dynamic_update_index_in_dim(lax.empty((N_CHIPS,) + shards[n].shape, BF16), shards[n], chip, 0)
                for n in names]

    early, early_done, c_gathered = _gather_weights(EARLY, [shards[n] for n in EARLY], landing(EARLY),
                                                    c.reshape(8, LANES))
    gathered = dict(zip(EARLY, early))

    c_all = c_gathered.reshape(N_DEV, D_MODEL)
    ada_cols = w["w_ada"].shape[1]
    b_cols = lax.dynamic_slice_in_dim(small["b_ada"], chip * ada_cols, ada_cols, axis=1)
    mod_all = _ada_fwd(c_all, w["w_ada"], b_cols).reshape(N_CHIPS, 2, N_DEV, ada_cols)
    mod = lax.dynamic_index_in_dim(mod_all[:, 0], dev, axis=1, keepdims=False).reshape(1, N_MOD * D_MODEL)

    late_gather = _exchange_start("gather_late_start", [shards[n] for n in LATE], landing(LATE), _late_gather_plan,
                                  3 * len(LATE), early_done, mod)
    mod = late_gather[4]

    half = MLA_ROPE // 2
    freqs = 1.0 / (ROPE_THETA ** (np.arange(half, dtype=np.float32) / half))
    zeros = np.zeros(LANES - MLA_ROPE, np.float32)
    freqs_row = jnp.asarray(np.concatenate([freqs, freqs, zeros]).astype(np.float32)[None])
    sign_row = jnp.asarray(np.concatenate([-np.ones(half), np.ones(half), zeros]).astype(np.float32)[None])
    cos, sin = _rope_tables(positions.reshape(seq, 1), freqs_row, sign_row)

    place = jnp.stack([ic, chip]).astype(jnp.int32)
    small_params = {n: small[n] for n in SMALL if n != "b_ada"}

    p1 = {**{n: gathered[n] for n in EARLY}, **small_params}
    (mixed, x_res), mixing_vjp = jax.vjp(lambda x_, mod_, p_: _mixing_stage(x_, mod_, p_, cos, sin), xs, mod, p1)
    _, landed = _exchange_wait("gather_late_wait", late_gather, _late_gather_plan, mixed)
    p2 = {**dict(zip(LATE, landed)), **small_params}
    (x2, ffn), ffn_vjp = jax.vjp(_ffn_stage, x_res, mixed, mod, p2)
    loss_rows, g_x2, g_ffn, g_gate2 = _loss_and_grads(x2, ffn, target, _mod_parts(mod)[5])
    loss_part = jnp.sum(loss_rows)
    gx2, gmixed, gmod2, gp2 = ffn_vjp((g_x2, g_ffn))
    gmod2 = gmod2 + jnp.concatenate([jnp.zeros((1, (N_MOD - 1) * D_MODEL), F32), g_gate2], axis=1)
    late_grads = [gp2[n] for n in LATE]
    late_plan = _direct_scatter_plan(LATE)
    late_scatter = _exchange_start(
        "grad_scatter_late_start", late_grads,
        [lax.empty((7,) + _half_shape(gr.shape[1:], HALF_AXIS[n]), BF16) for n, gr in zip(LATE, late_grads)],
        late_plan, 7 * len(LATE), gx2, gmixed)
    gx, gmod1, gp1 = mixing_vjp((late_scatter[4], gx2))
    gmod = gmod1 + gmod2
    gp = {n: gp1[n] + gp2[n] for n in small_params}

    sizes = [w[n].size for n in SMALL]
    offs, loss_off, n_small = _small_layout(sizes)
    pieces = []
    for n, size in zip(SMALL, sizes):
        pieces.append(gmod if n == "b_ada" else gp[n])
        if size % LANES:
            pieces.append(jnp.zeros((1, LANES - size % LANES), F32))
    pieces += [jnp.full((1, LANES), loss_part), jnp.zeros((1, n_small - loss_off - LANES), F32)]
    small_vec = jnp.concatenate(pieces, axis=1)

    g, delta, new_m, new_v = {}, {}, {}, {}

    def update(names, own, sib, after):
        for n, o, s in zip(names, own, sib):
            if n in TRAVELS_TRANSPOSED:
                res = _adamw_halves(place, w[n].T, o, s, m[n].T, v[n].T, HALF_AXIS[n], "adamw_" + n, after)
                g[n], delta[n], new_m[n], new_v[n] = [r.T for r in res]
            else:
                g[n], delta[n], new_m[n], new_v[n] = _adamw_halves(place, w[n], o, s, m[n], v[n], HALF_AXIS[n],
                                                                   "adamw_" + n, after)

    late_grads, late_parts = _exchange_wait("grad_scatter_late_wait", late_scatter, late_plan, gx)
    own_late = [_chip_sum_direct(place, gr, pt, HALF_AXIS[n], "grad_chip_sum_" + n)
                for n, gr, pt in zip(LATE, late_grads, late_parts)]
    early_grads = [gp1[n] for n in EARLY]
    theirs, small_gathered = _pair_exchange(EARLY, early_grads, "grad_pair_exchange_early",
                                            small_vec.reshape(8, n_small // 8))
    small_all = small_gathered.reshape(N_DEV, n_small)
    sib_late = _sibling_join(own_late, "grad_sibling_join_late", small_all)
    early_sums = [_pair_sum(place, gr, th, HALF_AXIS[n], "grad_pair_sum_" + n)
                  for n, gr, th in zip(EARLY, early_grads, theirs)]
    early_scatter = _exchange_start(
        "grad_scatter_early_start", early_sums,
        [lax.empty((N_CHIPS - 1,) + s.shape[1:], BF16) for s in early_sums], _late_scatter_plan, 3 * len(EARLY),
        sib_late[0], small_all)
    small_all = early_scatter[4]
    update(LATE, own_late, sib_late, small_all)

    *small_out, loss_row = _adamw_small([small[n] for n in SMALL], small_all, [m[n].reshape(1, -1) for n in SMALL],
                                        [v[n].reshape(1, -1) for n in SMALL], offs, loss_off)
    loss = loss_row[0, 0]
    for d, outs_d in zip((g, delta, new_m, new_v), small_out):
        d.update({n: o.reshape(w[n].shape) for n, o in zip(SMALL, outs_d)})

    dmod_all = small_all[:, :N_MOD * D_MODEL]
    g["w_ada"] = _ada_bwd(c_all, lax.dynamic_slice_in_dim(dmod_all, chip * ada_cols, ada_cols, axis=1))
    delta["w_ada"], new_m["w_ada"], new_v["w_ada"] = _adamw(w["w_ada"], g["w_ada"], m["w_ada"], v["w_ada"], "adamw_w_ada")

    early_sums, early_parts = _exchange_wait("grad_scatter_early_wait", early_scatter, _late_scatter_plan,
                                             delta["w_ada"])
    own_early = [_chip_sum(place, ps, pt, "grad_chip_sum_" + n)
                 for n, ps, pt in zip(EARLY, early_sums, early_parts)]
    sib_early = _sibling_join(own_early, "grad_sibling_join_early", delta["w_ada"])
    update(EARLY, own_early, sib_early, sib_early[0])

    def outs(d):
        return [d[n][None] for n in WEIGHTS]

    return (loss, gx[None], *outs(g), *outs(delta), *outs(new_m), *outs(new_v))
```

```python
import numpy as np
import jax
import jax.numpy as jnp
from jax import lax
from jax.experimental import pallas as pl
from jax.experimental.pallas import tpu as pltpu

F32 = jnp.float32
BF16 = jnp.bfloat16
MESH = pl.DeviceIdType.MESH
ANY = pl.BlockSpec(memory_space=pl.ANY)

D_MODEL = 1024
SB_HEADS = 8
SB_HEAD_DIM = 64
SB_WIDTH = 512
MLA_HEADS = 4
MLA_NOPE = 128
MLA_ROPE = 64
MLA_QK = 192
MLA_V = 128
MLA_Q_RANK = 384
MLA_KV_RANK = 256
D_FF = 2816
N_MOD = 6
ROPE_THETA = 10000.0
EPS = 1e-6
LANES = 128

ADAM_LR = 0.001
ADAM_B1 = 0.9
ADAM_B2 = 0.999
ADAM_EPS = 1e-08
ADAM_WD = 0.01
ADAM_STEP = 10

N_CHIPS = 4
N_DEV = 8
ROW_TILE = 512
MM_ROW_TILE = 512
ATT_BLK = 256
MM_VMEM_LIMIT = 56 * 1024 * 1024
FF_SHARD = D_FF // N_CHIPS
FF_SHARD_PAD = 768


def _mm(a, b, mode, name, tm, tn, out_dtype=F32):
    if mode == "nn":
        (m, k), n = a.shape, b.shape[1]
        a_spec = pl.BlockSpec((tm, k), lambda j, i: (i, 0))
        b_spec = pl.BlockSpec((k, tn), lambda j, i: (0, j))
        dims = (((1,), (0,)), ((), ()))
    elif mode == "nt":
        (m, k), n = a.shape, b.shape[0]
        a_spec = pl.BlockSpec((tm, k), lambda j, i: (i, 0))
        b_spec = pl.BlockSpec((tn, k), lambda j, i: (j, 0))
        dims = (((1,), (1,)), ((), ()))
    else:
        (k, m), n = a.shape, b.shape[1]
        a_spec = pl.BlockSpec((k, tm), lambda j, i: (0, i))
        b_spec = pl.BlockSpec((k, tn), lambda j, i: (0, j))
        dims = (((0,), (0,)), ((), ()))
    assert m % tm == 0 and n % tn == 0, (name, m, n, tm, tn)

    def body(a_ref, b_ref, o_ref):
        o_ref[...] = lax.dot_general(a_ref[...].astype(BF16), b_ref[...].astype(BF16), dims,
                                     preferred_element_type=F32).astype(out_dtype)

    return pl.pallas_call(
        body, name=name, grid=(n // tn, m // tm),
        in_specs=[a_spec, b_spec],
        out_specs=pl.BlockSpec((tm, tn), lambda j, i: (i, j)),
        out_shape=jax.ShapeDtypeStruct((m, n), out_dtype),
        compiler_params=pltpu.CompilerParams(dimension_semantics=("arbitrary", "arbitrary"),
                                             vmem_limit_bytes=MM_VMEM_LIMIT),
    )(a, b)


def _make_linear(name, tk_w, tn_w):
    @jax.custom_vjp
    def op(a, w):
        return _mm(a, w, "nn", name + "_fwd", MM_ROW_TILE, w.shape[1])

    def fwd(a, w):
        return op(a, w), (a, w)

    def bwd(res, dy):
        a, w = res
        da = _mm(dy, w, "nt", name + "_dx", MM_ROW_TILE, w.shape[0])
        dw = _mm(a, dy, "tn", name + "_dw", tk_w, tn_w, out_dtype=BF16)
        return da, dw

    op.defvjp(fwd, bwd)
    return op


def _make_linear_split_t(name, widths, tk_w):
    starts = [sum(widths[:g]) for g in range(len(widths))]

    def call_fwd(a, wt):
        t, k = a.shape
        n = wt.shape[0]

        def body(a_ref, w_ref, *o_refs):
            y = _nt(a_ref[...].astype(BF16), w_ref[...])
            for o_ref, s0, wd in zip(o_refs, starts, widths):
                o_ref[...] = y[:, s0:s0 + wd]

        return pl.pallas_call(
            body, name=name + "_fwd", grid=(t // MM_ROW_TILE,),
            in_specs=[pl.BlockSpec((MM_ROW_TILE, k), lambda i: (i, 0)), pl.BlockSpec((n, k), lambda i: (0, 0))],
            out_specs=[pl.BlockSpec((MM_ROW_TILE, wd), lambda i: (i, 0)) for wd in widths],
            out_shape=[jax.ShapeDtypeStruct((t, wd), F32) for wd in widths],
            compiler_params=pltpu.CompilerParams(dimension_semantics=("arbitrary",), vmem_limit_bytes=MM_VMEM_LIMIT),
        )(a, wt)

    def call_dx(dys, wt):
        t = dys[0].shape[0]
        n, k = wt.shape

        def body(*refs):
            dy_refs, w_ref, o_ref = refs[:-2], refs[-2], refs[-1]
            acc = jnp.zeros((MM_ROW_TILE, k), F32)
            for dy_ref, s0, wd in zip(dy_refs, starts, widths):
                acc = acc + jnp.dot(dy_ref[...].astype(BF16), w_ref[s0:s0 + wd, :], preferred_element_type=F32)
            o_ref[...] = acc

        return pl.pallas_call(
            body, name=name + "_dx", grid=(t // MM_ROW_TILE,),
            in_specs=[pl.BlockSpec((MM_ROW_TILE, wd), lambda i: (i, 0)) for wd in widths]
            + [pl.BlockSpec((n, k), lambda i: (0, 0))],
            out_specs=pl.BlockSpec((MM_ROW_TILE, k), lambda i: (i, 0)),
            out_shape=jax.ShapeDtypeStruct((t, k), F32),
            compiler_params=pltpu.CompilerParams(dimension_semantics=("arbitrary",), vmem_limit_bytes=MM_VMEM_LIMIT),
        )(*dys, wt)

    def call_dw(a, dys, wt):
        t, k = a.shape
        n = wt.shape[0]

        def body(a_ref, *refs):
            dy_refs, o_ref = refs[:-1], refs[-1]
            ab = a_ref[...].astype(BF16)
            for dy_ref, s0, wd in zip(dy_refs, starts, widths):
                o_ref[s0:s0 + wd, :] = _tn(dy_ref[...].astype(BF16), ab).astype(BF16)
            if starts[-1] + widths[-1] < n:
                o_ref[starts[-1] + widths[-1]:, :] = jnp.zeros((n - starts[-1] - widths[-1], tk_w), BF16)

        return pl.pallas_call(
            body, name=name + "_dw", grid=(k // tk_w,),
            in_specs=[pl.BlockSpec((t, tk_w), lambda i: (0, i))]
            + [pl.BlockSpec((t, wd), lambda i: (0, 0)) for wd in widths],
            out_specs=pl.BlockSpec((n, tk_w), lambda i: (0, i)),
            out_shape=jax.ShapeDtypeStruct((n, k), BF16),
            compiler_params=pltpu.CompilerParams(dimension_semantics=("arbitrary",), vmem_limit_bytes=MM_VMEM_LIMIT),
        )(a, *dys)

    @jax.custom_vjp
    def op(a, wt):
        return tuple(call_fwd(a, wt))

    def fwd(a, wt):
        return op(a, wt), (a, wt)

    def bwd(res, dys):
        a, wt = res
        return call_dx(dys, wt), call_dw(a, dys, wt)

    op.defvjp(fwd, bwd)
    return op


def _make_linear_split(name, widths, tk_w):
    starts = [sum(widths[:g]) for g in range(len(widths))]

    def call_fwd(a, w):
        t, k = a.shape
        n = w.shape[1]

        def body(a_ref, w_ref, *o_refs):
            y = jnp.dot(a_ref[...].astype(BF16), w_ref[...], preferred_element_type=F32)
            for o_ref, s0, wd in zip(o_refs, starts, widths):
                o_ref[...] = y[:, s0:s0 + wd]

        return pl.pallas_call(
            body, name=name + "_fwd", grid=(t // MM_ROW_TILE,),
            in_specs=[pl.BlockSpec((MM_ROW_TILE, k), lambda i: (i, 0)), pl.BlockSpec((k, n), lambda i: (0, 0))],
            out_specs=[pl.BlockSpec((MM_ROW_TILE, wd), lambda i: (i, 0)) for wd in widths],
            out_shape=[jax.ShapeDtypeStruct((t, wd), F32) for wd in widths],
            compiler_params=pltpu.CompilerParams(dimension_semantics=("arbitrary",), vmem_limit_bytes=MM_VMEM_LIMIT),
        )(a, w)

    def call_dx(dys, w):
        t = dys[0].shape[0]
        k, n = w.shape

        def body(*refs):
            dy_refs, w_ref, o_ref = refs[:-2], refs[-2], refs[-1]
            acc = jnp.zeros((MM_ROW_TILE, k), F32)
            for dy_ref, s0, wd in zip(dy_refs, starts, widths):
                acc = acc + _nt(dy_ref[...].astype(BF16), w_ref[:, s0:s0 + wd])
            o_ref[...] = acc

        return pl.pallas_call(
            body, name=name + "_dx", grid=(t // MM_ROW_TILE,),
            in_specs=[pl.BlockSpec((MM_ROW_TILE, wd), lambda i: (i, 0)) for wd in widths]
            + [pl.BlockSpec((k, n), lambda i: (0, 0))],
            out_specs=pl.BlockSpec((MM_ROW_TILE, k), lambda i: (i, 0)),
            out_shape=jax.ShapeDtypeStruct((t, k), F32),
            compiler_params=pltpu.CompilerParams(dimension_semantics=("arbitrary",), vmem_limit_bytes=MM_VMEM_LIMIT),
        )(*dys, w)

    def call_dw(a, dys, w):
        t, k = a.shape
        n = w.shape[1]

        def body(a_ref, *refs):
            dy_refs, o_ref = refs[:-1], refs[-1]
            ab = a_ref[...].astype(BF16)
            for dy_ref, s0, wd in zip(dy_refs, starts, widths):
                o_ref[:, s0:s0 + wd] = _tn(ab, dy_ref[...].astype(BF16)).astype(BF16)
            if starts[-1] + widths[-1] < n:
                o_ref[:, starts[-1] + widths[-1]:] = jnp.zeros((tk_w, n - starts[-1] - widths[-1]), BF16)

        return pl.pallas_call(
            body, name=name + "_dw", grid=(k // tk_w,),
            in_specs=[pl.BlockSpec((t, tk_w), lambda i: (0, i))]
            + [pl.BlockSpec((t, wd), lambda i: (0, 0)) for wd in widths],
            out_specs=pl.BlockSpec((tk_w, n), lambda i: (i, 0)),
            out_shape=jax.ShapeDtypeStruct((k, n), BF16),
            compiler_params=pltpu.CompilerParams(dimension_semantics=("arbitrary",), vmem_limit_bytes=MM_VMEM_LIMIT),
        )(a, *dys)

    @jax.custom_vjp
    def op(a, w):
        return tuple(call_fwd(a, w))

    def fwd(a, w):
        return op(a, w), (a, w)

    def bwd(res, dys):
        a, w = res
        return call_dx(dys, w), call_dw(a, dys, w)

    op.defvjp(fwd, bwd)
    return op


def _row_spec(arr, tb):
    return pl.BlockSpec((tb, arr.shape[1]), lambda i: (i, 0))


def _full_spec(arr):
    return pl.BlockSpec(arr.shape, lambda i: (0, 0))


def _make_rowwise(name, f, n_rows, n_params, out_cols, diff_rows, out_dtypes=None, grad_dtypes=None):
    n_out = len(out_cols)
    out_dtypes = out_dtypes or [F32] * n_out
    grad_dtypes = grad_dtypes or [F32] * sum(diff_rows)

    def call_fwd(rows, params):
        t = rows[0].shape[0]

        def body(*refs):
            ins = [r[...] for r in refs[:n_rows + n_params]]
            outs = f(*ins)
            for o_ref, o in zip(refs[n_rows + n_params:], outs):
                o_ref[...] = o.astype(o_ref.dtype)

        return pl.pallas_call(
            body, name=name + "_fwd", grid=(t // ROW_TILE,),
            in_specs=[_row_spec(a, ROW_TILE) for a in rows] + [_full_spec(p) for p in params],
            out_specs=[pl.BlockSpec((ROW_TILE, n), lambda i: (i, 0)) for n in out_cols],
            out_shape=[jax.ShapeDtypeStruct((t, n), dt) for n, dt in zip(out_cols, out_dtypes)],
            compiler_params=pltpu.CompilerParams(dimension_semantics=("arbitrary",),
                                                 vmem_limit_bytes=MM_VMEM_LIMIT),
        )(*rows, *params)

    def call_bwd(rows, params, cts):
        t = rows[0].shape[0]
        d_rows = [a for a, d in zip(rows, diff_rows) if d]
        n_in = n_rows + n_params + n_out

        def body(*refs):
            ins = [r[...] for r in refs[:n_rows + n_params]]
            ct = tuple(r[...].astype(F32) for r in refs[n_rows + n_params:n_in])
            _, vjp = jax.vjp(f, *ins)
            grads = vjp(ct)
            out_refs = refs[n_in:]
            g_rows = [g for g, d in zip(grads[:n_rows], diff_rows) if d]
            for o_ref, g in zip(out_refs[:len(g_rows)], g_rows):
                o_ref[...] = g.astype(o_ref.dtype)
            p_refs = out_refs[len(g_rows):]

            if p_refs:
                @pl.when(pl.program_id(0) == 0)
                def _():
                    for p_ref in p_refs:
                        p_ref[...] = jnp.zeros_like(p_ref)

                for p_ref, g in zip(p_refs, grads[n_rows:]):
                    p_ref[...] += g

        return pl.pallas_call(
            body, name=name + "_bwd", grid=(t // ROW_TILE,),
            in_specs=[_row_spec(a, ROW_TILE) for a in rows] + [_full_spec(p) for p in params]
            + [_row_spec(c, ROW_TILE) for c in cts],
            out_specs=[_row_spec(a, ROW_TILE) for a in d_rows] + [_full_spec(p) for p in params],
            out_shape=[jax.ShapeDtypeStruct(a.shape, dt) for a, dt in zip(d_rows, grad_dtypes)]
            + [jax.ShapeDtypeStruct(p.shape, F32) for p in params],
            compiler_params=pltpu.CompilerParams(dimension_semantics=("arbitrary",),
                                                 vmem_limit_bytes=MM_VMEM_LIMIT),
        )(*rows, *params, *cts)

    @jax.custom_vjp
    def op(*args):
        return tuple(call_fwd(args[:n_rows], args[n_rows:]))

    def fwd(*args):
        return op(*args), args

    def bwd(args, cts):
        rows, params = args[:n_rows], args[n_rows:]
        outs = call_bwd(rows, params, cts)
        it = iter(outs)
        g_rows = [next(it) if d else jnp.zeros_like(a) for a, d in zip(rows, diff_rows)]
        return tuple(g_rows) + tuple(it)

    op.defvjp(fwd, bwd)
    return op


def _rms(x, g, n):
    return x * lax.rsqrt(jnp.sum(x * x, axis=-1, keepdims=True) * (1.0 / n) + EPS) * g


def _f_pre_attn(x, g, scale, shift):
    return _rms(x, g, D_MODEL) * (1.0 + scale) + shift, x


def _f_mla_a(cq, ckv, gq, gkv):
    return _rms(cq, gq, MLA_Q_RANK), _rms(ckv, gkv, MLA_KV_RANK)


@jax.custom_vjp
def _split_lanes(x):
    return tuple(x[:, i * LANES:(i + 1) * LANES] for i in range(x.shape[1] // LANES))


def _split_lanes_fwd(x):
    return _split_lanes(x), None


def _split_lanes_bwd(_, cts):
    return (jnp.concatenate(cts, axis=1),)


_split_lanes.defvjp(_split_lanes_fwd, _split_lanes_bwd)


def _f_mla_b(qall, kn_all, kr, kr_sw, cos, sin, gqn, gqr, gqr_sw, gkn, gkr, gkr_sw):
    q = _split_lanes(qall)
    kn = _split_lanes(kn_all)
    qn_o, qr_o, kn_o = [], [], []
    for h in range(MLA_HEADS):
        qn, qr, qs = q[h], q[MLA_HEADS + h], q[2 * MLA_HEADS + h]
        ss = jnp.sum(qn * qn, axis=-1, keepdims=True) + jnp.sum(qr * qr, axis=-1, keepdims=True)
        rs = lax.rsqrt(ss * (1.0 / MLA_QK) + EPS)
        qn_o.append(qn * rs * gqn)
        qr_o.append((qr * rs * gqr) * cos + (qs * rs * gqr_sw) * sin)
        kn_o.append(_rms(kn[h], gkn, MLA_NOPE))
    rs = lax.rsqrt(jnp.sum(kr * kr, axis=-1, keepdims=True) * (1.0 / MLA_ROPE) + EPS)
    kr_o = (kr * rs * gkr) * cos + (kr_sw * rs * gkr_sw) * sin
    return (jnp.concatenate(qn_o, axis=1), jnp.concatenate(qr_o, axis=1), jnp.concatenate(kn_o, axis=1), kr_o)


def _f_post_attn(o_sb, o_mla, g_sb, g_mla):
    return (jnp.concatenate([_rms(o_sb, g_sb, SB_WIDTH), _rms(o_mla, g_mla, SB_WIDTH)], axis=1),)


def _f_pre_ffn(x, attn, gate, g, scale, shift):
    x2 = x + gate * attn
    return x2, _rms(x2, g, D_MODEL) * (1.0 + scale) + shift


def _f_swiglu(gt, up):
    return (gt / (1.0 + jnp.exp(-gt)) * up,)


def _f_loss(x2, ffn, target, gate):
    err = x2 + gate * ffn - target
    return (jnp.sum(err * err, axis=-1, keepdims=True) * (1.0 / D_MODEL),)


def _rope_tables(pos_col, freqs, sign):
    t = pos_col.shape[0]

    def body(p_ref, f_ref, s_ref, cos_ref, sin_ref):
        ang = p_ref[...].astype(F32) * f_ref[...]
        live = jnp.abs(s_ref[...])
        cos_ref[...] = jnp.cos(ang) * live
        sin_ref[...] = jnp.sin(ang) * s_ref[...]

    return pl.pallas_call(
        body, name="rope_tables", grid=(t // ROW_TILE,),
        in_specs=[pl.BlockSpec((ROW_TILE, 1), lambda i: (i, 0)), _full_spec(freqs), _full_spec(sign)],
        out_specs=[pl.BlockSpec((ROW_TILE, LANES), lambda i: (i, 0))] * 2,
        out_shape=[jax.ShapeDtypeStruct((t, LANES), F32)] * 2,
    )(pos_col, freqs, sign)


def _hi_lo_dot(x, tri):
    hi = x.astype(BF16)
    lo = (x - hi.astype(F32)).astype(BF16)
    return (jnp.dot(hi, tri, preferred_element_type=F32) + jnp.dot(lo, tri, preferred_element_type=F32))


def _tri(cmp):
    r = lax.broadcasted_iota(jnp.int32, (ATT_BLK, ATT_BLK), 0)
    c = lax.broadcasted_iota(jnp.int32, (ATT_BLK, ATT_BLK), 1)
    return cmp(r, c).astype(BF16)


def _nt(a, b):
    return lax.dot_general(a, b, (((1,), (1,)), ((), ())), preferred_element_type=F32)


def _tn(a, b):
    return lax.dot_general(a, b, (((0,), (0,)), ((), ())), preferred_element_type=F32)


def _sb_logs(z):
    lb = jnp.minimum(z, 0.0) - jnp.log(1.0 + jnp.exp(-jnp.abs(z)))
    return lb, lb - z


def _sb_fwd(q, k, v):
    t = q.shape[0]
    nq = t // ATT_BLK
    scale = SB_HEAD_DIM ** -0.5

    def body(q_ref, k_ref, v_ref, o_ref, tot_ref):
        qi = pl.program_id(1)
        lane = lax.broadcasted_iota(jnp.int32, (ATT_BLK, LANES), 1)
        tri = _tri(lambda r, c: r > c)
        qv = q_ref[...] * scale
        heads = [(lane // SB_HEAD_DIM) == hh for hh in range(2)]
        qms = [jnp.where(mine, qv, 0.0).astype(BF16) for mine in heads]

        def blocks(kbs, carry, diagonal):
            acc = carry[0]
            nb = len(kbs)
            chains = [(b, hh) for b in range(nb) for hh in range(2)]
            offs = [pl.multiple_of(kb * ATT_BLK, ATT_BLK) for kb in kbs]
            kks = [k_ref[pl.ds(off, ATT_BLK), :].astype(BF16) for off in offs]
            v_blks = [v_ref[pl.ds(off, ATT_BLK), :] for off in offs]
            if any(diagonal):
                valid = (lax.broadcasted_iota(jnp.int32, (ATT_BLK, ATT_BLK), 1)
                         < lax.broadcasted_iota(jnp.int32, (ATT_BLK, ATT_BLK), 0))
            zs = {ch: _nt(qms[ch[1]], kks[ch[0]]) for ch in chains}
            vvs = {(b, hh): jnp.where(heads[hh], v_blks[b], 0.0).astype(BF16) for b, hh in chains}
            logs = {ch: _sb_logs(zs[ch]) for ch in chains}
            l1ms = {ch: jnp.where(valid, logs[ch][1], 0.0) if diagonal[ch[0]] else logs[ch][1] for ch in chains}
            run = {(0, hh): carry[1 + hh] for hh in range(2)}
            for b, hh in chains:
                run[(b + 1, hh)] = run[(b, hh)] + jnp.sum(l1ms[(b, hh)], axis=-1, keepdims=True)
            afters = {ch: _hi_lo_dot(l1ms[ch], tri) for ch in chains}
            ws = {ch: jnp.exp(logs[ch][0] + (afters[ch] + run[ch])) for ch in chains}
            ws = {ch: jnp.where(valid, ws[ch], 0.0) if diagonal[ch[0]] else ws[ch] for ch in chains}
            for ch in chains:
                acc = acc + jnp.dot(ws[ch].astype(BF16), vvs[ch], preferred_element_type=F32)
            return (acc, run[(nb, 0)], run[(nb, 1)])

        zero = jnp.zeros((ATT_BLK, 1), F32)
        init = (jnp.zeros((ATT_BLK, LANES), F32), zero, zero)
        carry = lax.cond(qi % 2 == 1, lambda cr: blocks([qi, qi - 1], cr, (True, False)),
                         lambda cr: blocks([qi], cr, (True,)), init)
        top = qi - 1 - qi % 2
        carry = lax.fori_loop(0, qi // 2, lambda pr, cr: blocks([top - 2 * pr, top - 1 - 2 * pr], cr, (False, False)),
                              carry)
        o_ref[...] = carry[0]
        for hh in range(2):
            tot_ref[:, hh * LANES:(hh + 1) * LANES] = jnp.broadcast_to(carry[1 + hh], (ATT_BLK, LANES))

    return pl.pallas_call(
        body, name="sb_attn_fwd", grid=(SB_HEADS // 2, nq),
        in_specs=[pl.BlockSpec((ATT_BLK, LANES), lambda p, i: (i, p)),
                  pl.BlockSpec((t, LANES), lambda p, i: (0, p)),
                  pl.BlockSpec((t, LANES), lambda p, i: (0, p))],
        out_specs=[pl.BlockSpec((ATT_BLK, LANES), lambda p, i: (i, p)),
                   pl.BlockSpec((ATT_BLK, 2 * LANES), lambda p, i: (i, p))],
        out_shape=[jax.ShapeDtypeStruct((t, SB_WIDTH), F32), jax.ShapeDtypeStruct((t, SB_HEADS * LANES), F32)],
        compiler_params=pltpu.CompilerParams(dimension_semantics=("arbitrary", "arbitrary")),
    )(q, k, v)


def _sb_bwd(q, k, v, tot, do):
    t = q.shape[0]
    nq = t // ATT_BLK
    scale = SB_HEAD_DIM ** -0.5

    def body(q_ref, k_ref, v_ref, tot_ref, do_ref, dq_ref, dk_ref, dv_ref):
        qi = pl.program_id(1)

        @pl.when(qi == 0)
        def _():
            dk_ref[...] = jnp.zeros_like(dk_ref)
            dv_ref[...] = jnp.zeros_like(dv_ref)

        lane = lax.broadcasted_iota(jnp.int32, (ATT_BLK, LANES), 1)
        tri_incl = _tri(lambda r, c: r <= c)
        tri_lt = _tri(lambda r, c: r < c)
        qv = q_ref[...] * scale
        dov = do_ref[...]
        heads = [(lane // SB_HEAD_DIM) == hh for hh in range(2)]
        qms = [jnp.where(mine, qv, 0.0).astype(BF16) for mine in heads]
        doms = [jnp.where(mine, dov, 0.0).astype(BF16) for mine in heads]
        tots = [tot_ref[:, hh * LANES:hh * LANES + 1] for hh in range(2)]

        def blocks(kbs, carry, diagonal):
            dq = carry[0]
            nb = len(kbs)
            chains = [(b, hh) for b in range(nb) for hh in range(2)]
            offs = [pl.multiple_of(kb * ATT_BLK, ATT_BLK) for kb in kbs]
            k_blks = [k_ref[pl.ds(off, ATT_BLK), :] for off in offs]
            vvs = [v_ref[pl.ds(off, ATT_BLK), :].astype(BF16) for off in offs]
            if any(diagonal):
                valid = (lax.broadcasted_iota(jnp.int32, (ATT_BLK, ATT_BLK), 1)
                         < lax.broadcasted_iota(jnp.int32, (ATT_BLK, ATT_BLK), 0))
            kks = {(b, hh): jnp.where(heads[hh], k_blks[b], 0.0).astype(BF16) for b, hh in chains}
            zs = {ch: _nt(qms[ch[1]], kks[ch]) for ch in chains}
            dws = {ch: _nt(doms[ch[1]], vvs[ch[0]]) for ch in chains}
            logs = {ch: _sb_logs(zs[ch]) for ch in chains}
            lbs = {ch: logs[ch][0] for ch in chains}
            l1m_all = {ch: logs[ch][1] for ch in chains}
            l1ms = {ch: jnp.where(valid, l1m_all[ch], 0.0) if diagonal[ch[0]] else l1m_all[ch] for ch in chains}
            pre, c_de = {}, {}
            for hh in range(2):
                pre[(0, hh)], c_de[(0, hh)] = carry[1 + 2 * hh], carry[2 + 2 * hh]
            for b, hh in chains:
                pre[(b + 1, hh)] = pre[(b, hh)] + jnp.sum(l1ms[(b, hh)], axis=-1, keepdims=True)
            prefix = {ch: _hi_lo_dot(l1ms[ch], tri_incl) for ch in chains}
            ws = {ch: jnp.exp(lbs[ch] + (tots[ch[1]] - (prefix[ch] + pre[ch]))) for ch in chains}
            ws = {ch: jnp.where(valid, ws[ch], 0.0) if diagonal[ch[0]] else ws[ch] for ch in chains}
            d_es = {ch: ws[ch] * dws[ch] for ch in chains}
            for b, hh in chains:
                c_de[(b + 1, hh)] = c_de[(b, hh)] + jnp.sum(d_es[(b, hh)], axis=-1, keepdims=True)
            dvs = [_tn(ws[(b, 0)].astype(BF16), doms[0]) + _tn(ws[(b, 1)].astype(BF16), doms[1]) for b in range(nb)]
            dl1ms = {ch: jnp.dot(d_es[ch].astype(BF16), tri_lt, preferred_element_type=F32) + c_de[ch] for ch in chains}
            dzs = {ch: d_es[ch] * jnp.exp(l1m_all[ch]) - dl1ms[ch] * jnp.exp(lbs[ch]) for ch in chains}
            dzs = {ch: jnp.where(valid, dzs[ch], 0.0) if diagonal[ch[0]] else dzs[ch] for ch in chains}
            dzs = {ch: dzs[ch].astype(BF16) for ch in chains}
            for ch in chains:
                dq = dq + jnp.dot(dzs[ch], kks[ch], preferred_element_type=F32)
            for b in range(nb):
                dk_ref[pl.ds(offs[b], ATT_BLK), :] += _tn(dzs[(b, 0)], qms[0]) + _tn(dzs[(b, 1)], qms[1])
                dv_ref[pl.ds(offs[b], ATT_BLK), :] += dvs[b]
            return (dq, pre[(nb, 0)], c_de[(nb, 0)], pre[(nb, 1)], c_de[(nb, 1)])

        zero = jnp.zeros((ATT_BLK, 1), F32)
        carry = lax.fori_loop(0, qi // 2, lambda pr, cr: blocks([2 * pr, 2 * pr + 1], cr, (False, False)),
                              (jnp.zeros((ATT_BLK, LANES), F32), zero, zero, zero, zero))
        carry = lax.cond(qi % 2 == 1, lambda cr: blocks([qi - 1, qi], cr, (False, True)),
                         lambda cr: blocks([qi], cr, (True,)), carry)
        dq_ref[...] = carry[0] * scale

    return pl.pallas_call(
        body, name="sb_attn_bwd", grid=(SB_HEADS // 2, nq),
        in_specs=[pl.BlockSpec((ATT_BLK, LANES), lambda p, i: (i, p)),
                  pl.BlockSpec((t, LANES), lambda p, i: (0, p)),
                  pl.BlockSpec((t, LANES), lambda p, i: (0, p)),
                  pl.BlockSpec((ATT_BLK, 2 * LANES), lambda p, i: (i, p)),
                  pl.BlockSpec((ATT_BLK, LANES), lambda p, i: (i, p))],
        out_specs=[pl.BlockSpec((ATT_BLK, LANES), lambda p, i: (i, p)),
                   pl.BlockSpec((t, LANES), lambda p, i: (0, p)),
                   pl.BlockSpec((t, LANES), lambda p, i: (0, p))],
        out_shape=[jax.ShapeDtypeStruct((t, SB_WIDTH), F32)] * 3,
        compiler_params=pltpu.CompilerParams(dimension_semantics=("arbitrary", "arbitrary")),
    )(q, k, v, tot, do)


@jax.custom_vjp
def _sb_attention(q, k, v):
    return _sb_fwd(q, k, v)[0]


def _sb_attention_fwd(q, k, v):
    o, tot = _sb_fwd(q, k, v)
    return o, (q, k, v, tot)


def _sb_attention_bwd(res, do):
    return tuple(_sb_bwd(*res, do))


_sb_attention.defvjp(_sb_attention_fwd, _sb_attention_bwd)


def _mla_fwd(qn, qr, kn, kr, v):
    t = qn.shape[0]
    nq = t // ATT_BLK
    scale = MLA_QK ** -0.5

    def body(qn_ref, qr_ref, kn_ref, kr_ref, v_ref, o_ref, lse_ref):
        qi = pl.program_id(1)
        lanes = [slice(hh * LANES, (hh + 1) * LANES) for hh in range(2)]
        qnb = [qn_ref[:, sl].astype(BF16) for sl in lanes]
        qrb = [qr_ref[:, sl].astype(BF16) for sl in lanes]

        def blocks(kbs, carry, diagonal):
            nb = len(kbs)
            chains = [(b, hh) for b in range(nb) for hh in range(2)]
            offs = [pl.multiple_of(kb * ATT_BLK, ATT_BLK) for kb in kbs]
            krbs = [kr_ref[pl.ds(off, ATT_BLK), :].astype(BF16) for off in offs]
            accs, ms, ls = [carry[0], carry[3]], [carry[1], carry[4]], [carry[2], carry[5]]
            ss = {(b, hh): (_nt(qnb[hh], kn_ref[pl.ds(offs[b], ATT_BLK), lanes[hh]].astype(BF16))
                            + _nt(qrb[hh], krbs[b])) * scale for b, hh in chains}
            if any(diagonal):
                causal = (lax.broadcasted_iota(jnp.int32, (ATT_BLK, ATT_BLK), 1)
                          <= lax.broadcasted_iota(jnp.int32, (ATT_BLK, ATT_BLK), 0))
                ss = {ch: jnp.where(causal, ss[ch], -jnp.inf) if diagonal[ch[0]] else ss[ch] for ch in chains}
            m_new = list(ms)
            for b, hh in chains:
                m_new[hh] = jnp.maximum(m_new[hh], jnp.max(ss[(b, hh)], axis=-1, keepdims=True))
            ps = {(b, hh): jnp.exp(ss[(b, hh)] - m_new[hh]) for b, hh in chains}
            alphas = [jnp.exp(ms[hh] - m_new[hh]) for hh in range(2)]
            pvs = {(b, hh): jnp.dot(ps[(b, hh)].astype(BF16), v_ref[pl.ds(offs[b], ATT_BLK), lanes[hh]].astype(BF16),
                                    preferred_element_type=F32) for b, hh in chains}
            out = []
            for hh in range(2):
                acc, l = accs[hh] * alphas[hh], ls[hh] * alphas[hh]
                for b in range(nb):
                    acc, l = acc + pvs[(b, hh)], l + jnp.sum(ps[(b, hh)], axis=-1, keepdims=True)
                out += [acc, m_new[hh], l]
            return tuple(out)

        init = (jnp.zeros((ATT_BLK, LANES), F32), jnp.full((ATT_BLK, 1), -jnp.inf, F32), jnp.zeros((ATT_BLK, 1), F32))
        carry = lax.cond(qi % 2 == 1, lambda cr: blocks([qi, qi - 1], cr, (True, False)),
                         lambda cr: blocks([qi], cr, (True,)), init + init)
        carry = lax.fori_loop(0, qi // 2, lambda pr, cr: blocks([2 * pr, 2 * pr + 1], cr, (False, False)), carry)
        for hh in range(2):
            acc, m, l = carry[3 * hh:3 * hh + 3]
            o_ref[:, lanes[hh]] = acc / l
            lse_ref[:, lanes[hh]] = jnp.broadcast_to(m + jnp.log(l), (ATT_BLK, LANES))

    blk = pl.BlockSpec((ATT_BLK, 2 * LANES), lambda p, i: (i, p))
    full = pl.BlockSpec((t, 2 * LANES), lambda p, i: (0, p))
    return pl.pallas_call(
        body, name="mla_attn_fwd", grid=(MLA_HEADS // 2, nq),
        in_specs=[blk, blk, full, pl.BlockSpec((t, LANES), lambda p, i: (0, 0)), full],
        out_specs=[blk, blk],
        out_shape=[jax.ShapeDtypeStruct((t, MLA_HEADS * LANES), F32)] * 2,
        compiler_params=pltpu.CompilerParams(dimension_semantics=("arbitrary", "arbitrary")),
    )(qn, qr, kn, kr, v)


def _mla_bwd(qn, qr, kn, kr, v, o, lse, do):
    t = qn.shape[0]
    nq = t // ATT_BLK
    scale = MLA_QK ** -0.5

    def body(qn_ref, qr_ref, kn_ref, kr_ref, v_ref, o_ref, lse_ref, do_ref,
             dqn_ref, dqr_ref, dkn_ref, dkr_ref, dv_ref):
        pair = pl.program_id(0)
        qi = pl.program_id(1)

        @pl.when(qi == 0)
        def _():
            dkn_ref[...] = jnp.zeros_like(dkn_ref)
            dv_ref[...] = jnp.zeros_like(dv_ref)

        @pl.when((qi == 0) & (pair == 0))
        def _():
            dkr_ref[...] = jnp.zeros_like(dkr_ref)

        lanes = [slice(hh * LANES, (hh + 1) * LANES) for hh in range(2)]
        qnb = [qn_ref[:, sl].astype(BF16) for sl in lanes]
        qrb = [qr_ref[:, sl].astype(BF16) for sl in lanes]
        dob = [do_ref[:, sl].astype(BF16) for sl in lanes]
        delta = [jnp.sum(do_ref[:, sl] * o_ref[:, sl], axis=-1, keepdims=True) for sl in lanes]
        lse_v = [lse_ref[:, hh * LANES:hh * LANES + 1] for hh in range(2)]

        def blocks(kbs, carry, diagonal):
            nb = len(kbs)
            chains = [(b, hh) for b in range(nb) for hh in range(2)]
            offs = [pl.multiple_of(kb * ATT_BLK, ATT_BLK) for kb in kbs]
            krbs = [kr_ref[pl.ds(off, ATT_BLK), :].astype(BF16) for off in offs]
            knb = {(b, hh): kn_ref[pl.ds(offs[b], ATT_BLK), lanes[hh]].astype(BF16) for b, hh in chains}
            vb = {(b, hh): v_ref[pl.ds(offs[b], ATT_BLK), lanes[hh]].astype(BF16) for b, hh in chains}
            ss = {(b, hh): _nt(qnb[hh], knb[(b, hh)]) + _nt(qrb[hh], krbs[b]) for b, hh in chains}
            dps = {(b, hh): _nt(dob[hh], vb[(b, hh)]) for b, hh in chains}
            ps = {(b, hh): jnp.exp(ss[(b, hh)] * scale - lse_v[hh]) for b, hh in chains}
            if any(diagonal):
                causal = (lax.broadcasted_iota(jnp.int32, (ATT_BLK, ATT_BLK), 1)
                          <= lax.broadcasted_iota(jnp.int32, (ATT_BLK, ATT_BLK), 0))
                ps = {ch: jnp.where(causal, ps[ch], 0.0) if diagonal[ch[0]] else ps[ch] for ch in chains}
            dss = {(b, hh): (ps[(b, hh)] * (dps[(b, hh)] - delta[hh]) * scale).astype(BF16) for b, hh in chains}
            for b, hh in chains:
                dv_ref[pl.ds(offs[b], ATT_BLK), lanes[hh]] += _tn(ps[(b, hh)].astype(BF16), dob[hh])
            for b, hh in chains:
                dkn_ref[pl.ds(offs[b], ATT_BLK), lanes[hh]] += _tn(dss[(b, hh)], qnb[hh])
            for b in range(nb):
                dkr_ref[pl.ds(offs[b], ATT_BLK), :] += _tn(dss[(b, 0)], qrb[0]) + _tn(dss[(b, 1)], qrb[1])
            out = list(carry)
            for b, hh in chains:
                out[2 * hh] = out[2 * hh] + jnp.dot(dss[(b, hh)], knb[(b, hh)], preferred_element_type=F32)
                out[2 * hh + 1] = out[2 * hh + 1] + jnp.dot(dss[(b, hh)], krbs[b], preferred_element_type=F32)
            return tuple(out)

        zero = jnp.zeros((ATT_BLK, LANES), F32)
        carry = lax.fori_loop(0, qi // 2, lambda pr, cr: blocks([2 * pr, 2 * pr + 1], cr, (False, False)),
                              (zero, zero, zero, zero))
        carry = lax.cond(qi % 2 == 1, lambda cr: blocks([qi - 1, qi], cr, (False, True)),
                         lambda cr: blocks([qi], cr, (True,)), carry)
        for hh in range(2):
            dqn_ref[:, lanes[hh]] = carry[2 * hh]
            dqr_ref[:, lanes[hh]] = carry[2 * hh + 1]

    blk = pl.BlockSpec((ATT_BLK, 2 * LANES), lambda p, i: (i, p))
    full = pl.BlockSpec((t, 2 * LANES), lambda p, i: (0, p))
    shared = pl.BlockSpec((t, LANES), lambda p, i: (0, 0))
    wide = jax.ShapeDtypeStruct((t, MLA_HEADS * LANES), F32)
    return pl.pallas_call(
        body, name="mla_attn_bwd", grid=(MLA_HEADS // 2, nq),
        in_specs=[blk, blk, full, shared, full, blk, blk, blk],
        out_specs=[blk, blk, full, shared, full],
        out_shape=[wide, wide, wide, jax.ShapeDtypeStruct((t, LANES), F32), wide],
        compiler_params=pltpu.CompilerParams(dimension_semantics=("arbitrary", "arbitrary")),
    )(qn, qr, kn, kr, v, o, lse, do)


@jax.custom_vjp
def _mla_attention(qn, qr, kn, kr, v):
    return _mla_fwd(qn, qr, kn, kr, v)[0]


def _mla_attention_fwd(qn, qr, kn, kr, v):
    o, lse = _mla_fwd(qn, qr, kn, kr, v)
    return o, (qn, qr, kn, kr, v, o, lse)


def _mla_attention_bwd(res, do):
    return tuple(_mla_bwd(*res, do))


_mla_attention.defvjp(_mla_attention_fwd, _mla_attention_bwd)


def _ffn_in(h, wg, wu):
    t, k = h.shape
    n_sh, cc, _ = wg.shape

    def body(h_ref, wg_ref, wu_ref, g_ref, u_ref, a_ref):
        hb = h_ref[...].astype(BF16)
        for j in range(n_sh):
            cols = slice(j * cc, (j + 1) * cc)
            g = _nt(hb, wg_ref[j])
            u = _nt(hb, wu_ref[j])
            g_ref[:, cols] = g.astype(BF16)
            u_ref[:, cols] = u.astype(BF16)
            a_ref[:, cols] = _f_swiglu(g, u)[0].astype(BF16)

    w_spec = pl.BlockSpec((n_sh, cc, k), lambda i: (0, 0, 0))
    o_spec = pl.BlockSpec((MM_ROW_TILE, n_sh * cc), lambda i: (i, 0))
    wide = jax.ShapeDtypeStruct((t, n_sh * cc), BF16)
    return pl.pallas_call(
        body, name="ffn_in_fwd", grid=(t // MM_ROW_TILE,),
        in_specs=[pl.BlockSpec((MM_ROW_TILE, k), lambda i: (i, 0)), w_spec, w_spec],
        out_specs=[o_spec, o_spec, o_spec],
        out_shape=[wide, wide, wide],
        compiler_params=pltpu.CompilerParams(dimension_semantics=("arbitrary",), vmem_limit_bytes=MM_VMEM_LIMIT),
    )(h, wg, wu)


def _ffn_mid_bwd(dy, wd, g, u):
    t, n = dy.shape
    n_sh, cc, _ = wd.shape

    def body(dy_ref, wd_ref, g_ref, u_ref, dg_ref, du_ref):
        d_act = _nt(dy_ref[...].astype(BF16), wd_ref[...])
        g = g_ref[...].astype(F32)
        sig = 1.0 / (1.0 + jnp.exp(-g))
        dg_ref[...] = (d_act * u_ref[...].astype(F32) * (sig * (1.0 + g * (1.0 - sig)))).astype(BF16)
        du_ref[...] = (d_act * (g * sig)).astype(BF16)

    blk = pl.BlockSpec((MM_ROW_TILE, cc), lambda j, i: (i, j))
    wide = jax.ShapeDtypeStruct((t, n_sh * cc), BF16)
    return pl.pallas_call(
        body, name="ffn_mid_bwd", grid=(n_sh, t // MM_ROW_TILE),
        in_specs=[pl.BlockSpec((MM_ROW_TILE, n), lambda j, i: (i, 0)),
                  pl.BlockSpec((None, cc, n), lambda j, i: (j, 0, 0)), blk, blk],
        out_specs=[blk, blk], out_shape=[wide, wide],
        compiler_params=pltpu.CompilerParams(dimension_semantics=("arbitrary", "arbitrary"),
                                             vmem_limit_bytes=MM_VMEM_LIMIT),
    )(dy, wd, g, u)


def _ffn_dh(dg, du, wg, wu):
    t = dg.shape[0]
    n_sh, cc, k = wg.shape

    def body(dg_ref, du_ref, wg_ref, wu_ref, o_ref):
        acc = jnp.zeros((MM_ROW_TILE, k), F32)
        for j in range(n_sh):
            cols = slice(j * cc, (j + 1) * cc)
            acc = (acc + jnp.dot(dg_ref[:, cols], wg_ref[j], preferred_element_type=F32)
                   + jnp.dot(du_ref[:, cols], wu_ref[j], preferred_element_type=F32))
        o_ref[...] = acc

    blk = pl.BlockSpec((MM_ROW_TILE, n_sh * cc), lambda i: (i, 0))
    w_spec = pl.BlockSpec((n_sh, cc, k), lambda i: (0, 0, 0))
    return pl.pallas_call(
        body, name="ffn_dh", grid=(t // MM_ROW_TILE,),
        in_specs=[blk, blk, w_spec, w_spec],
        out_specs=pl.BlockSpec((MM_ROW_TILE, k), lambda i: (i, 0)),
        out_shape=jax.ShapeDtypeStruct((t, k), F32),
        compiler_params=pltpu.CompilerParams(dimension_semantics=("arbitrary",), vmem_limit_bytes=MM_VMEM_LIMIT),
    )(dg, du, wg, wu)


def _ffn_dw_in(h, dg, du, n_sh):
    t, k = h.shape
    cc = dg.shape[1] // n_sh
    tk = 512

    def body(h_ref, dg_ref, du_ref, og_ref, ou_ref):
        hb = h_ref[...].astype(BF16)
        og_ref[...] = _tn(dg_ref[...], hb).astype(BF16)
        ou_ref[...] = _tn(du_ref[...], hb).astype(BF16)

    d_spec = pl.BlockSpec((t, cc), lambda i, j: (0, j))
    o_spec = pl.BlockSpec((None, cc, tk), lambda i, j: (j, 0, i))
    out = jax.ShapeDtypeStruct((n_sh, cc, k), BF16)
    return pl.pallas_call(
        body, name="ffn_gate_up_dw", grid=(k // tk, n_sh),
        in_specs=[pl.BlockSpec((t, tk), lambda i, j: (0, i)), d_spec, d_spec],
        out_specs=[o_spec, o_spec], out_shape=[out, out],
        compiler_params=pltpu.CompilerParams(dimension_semantics=("arbitrary", "arbitrary"),
                                             vmem_limit_bytes=MM_VMEM_LIMIT),
    )(h, dg, du)


@jax.custom_vjp
def _ffn_block(h, wg, wu, wd):
    act = _ffn_in(h, wg, wu)[2]
    return _mm(act, wd.reshape(-1, wd.shape[2]), "nn", "ffn_down_fwd", MM_ROW_TILE, wd.shape[2])


def _ffn_block_fwd(h, wg, wu, wd):
    g, u, act = _ffn_in(h, wg, wu)
    y = _mm(act, wd.reshape(-1, wd.shape[2]), "nn", "ffn_down_fwd", MM_ROW_TILE, wd.shape[2])
    return y, (h, wg, wu, wd, g, u, act)


def _ffn_block_bwd(res, dy):
    h, wg, wu, wd, g, u, act = res
    dg, du = _ffn_mid_bwd(dy, wd, g, u)
    dh = _ffn_dh(dg, du, wg, wu)
    n_sh = wg.shape[0]
    dwg, dwu = _ffn_dw_in(h, dg, du, n_sh)
    dwd = _mm(act, dy, "tn", "ffn_down_dw", 256, wd.shape[2], out_dtype=BF16).reshape(wd.shape)
    return dh, dwg, dwu, dwd


_ffn_block.defvjp(_ffn_block_fwd, _ffn_block_bwd)


def _swap_halves(w):
    half = w.shape[-1] // 2
    return jnp.concatenate([w[..., half:], w[..., :half]], axis=-1)


def _pad_lanes(w):
    return jnp.concatenate([w, jnp.zeros(w.shape[:-1] + (LANES - w.shape[-1],), w.dtype)], axis=-1)


def _join_cols(shards):
    return shards.transpose(1, 0, 2).reshape(shards.shape[1], -1)


def _mod_parts(mod):
    return [mod[:, i * D_MODEL:(i + 1) * D_MODEL] for i in range(N_MOD)]


def _mixing_stage(x, mod, p, cos, sin):
    shift1, scale1 = _mod_parts(mod)[:2]

    w_in_t = p["w_in"].reshape(-1, D_MODEL)
    k_rope_rows = w_in_t[2176:2240]

    def pad_rows(a):
        return jnp.concatenate([a, jnp.zeros((LANES - a.shape[0], D_MODEL), a.dtype)], axis=0)

    swapped = jnp.concatenate([k_rope_rows[MLA_ROPE // 2:], k_rope_rows[:MLA_ROPE // 2]], axis=0)
    w_in_ext = jnp.concatenate([w_in_t[:2176], pad_rows(k_rope_rows), pad_rows(swapped),
                                jnp.zeros((LANES, D_MODEL), w_in_t.dtype)], axis=0)
    h1, x_res = _make_rowwise("pre_attn", _f_pre_attn, 1, 3, [D_MODEL, D_MODEL], [True], out_dtypes=[BF16, F32])(
        x, p["norm_attn"], scale1, shift1)
    q_sb, k_sb, v_sb, cq, ckv, kr, kr_sw = _make_linear_split_t(
        "in_proj", (SB_WIDTH, SB_WIDTH, SB_WIDTH, MLA_Q_RANK, MLA_KV_RANK, LANES, LANES), 512)(h1, w_in_ext)

    o_sb = _sb_attention(q_sb, k_sb, v_sb)

    wq = _join_cols(p["w_q_up"]).reshape(MLA_Q_RANK, MLA_HEADS, MLA_QK)
    wq_n, wq_r = wq[:, :, :MLA_NOPE], wq[:, :, MLA_NOPE:]
    w_q_ext = jnp.concatenate([wq_n.reshape(MLA_Q_RANK, -1), _pad_lanes(wq_r).reshape(MLA_Q_RANK, -1),
                               _pad_lanes(_swap_halves(wq_r)).reshape(MLA_Q_RANK, -1)], axis=1)
    wkv = _join_cols(p["w_kv_up"]).reshape(MLA_KV_RANK, MLA_HEADS, MLA_NOPE + MLA_V)
    w_kv_ext = jnp.concatenate([wkv[:, :, :MLA_NOPE].reshape(MLA_KV_RANK, -1),
                                wkv[:, :, MLA_NOPE:].reshape(MLA_KV_RANK, -1)], axis=1)
    cqn, ckvn = _make_rowwise("mla_a", _f_mla_a, 2, 2, [MLA_Q_RANK, MLA_KV_RANK], [True, True],
                              out_dtypes=[BF16, BF16], grad_dtypes=[BF16, BF16])(
        cq, ckv, p["q_a_norm"], p["kv_a_norm"])
    qall = _make_linear("q_up", 384, 768)(cqn, w_q_ext)
    kn_all, v_mla = _make_linear_split("kv_up", (MLA_HEADS * MLA_NOPE, MLA_HEADS * MLA_V), MLA_KV_RANK)(ckvn, w_kv_ext)
    gq = p["q_norm"]
    gkr = p["k_rope_norm"]
    qn, qr, kn, krr = _make_rowwise("mla_b", _f_mla_b, 6, 6, [512, 512, 512, LANES],
                                    [True, True, True, True, False, False],
                                    out_dtypes=[BF16] * 4, grad_dtypes=[BF16] * 4)(
        qall, kn_all, kr, kr_sw, cos, sin,
        gq[:, :MLA_NOPE], _pad_lanes(gq[:, MLA_NOPE:]), _pad_lanes(_swap_halves(gq[:, MLA_NOPE:])),
        p["k_nope_norm"], _pad_lanes(gkr), _pad_lanes(_swap_halves(gkr)))
    o_mla = _mla_attention(qn, qr, kn, krr, v_mla)

    (mixed,) = _make_rowwise("post_attn", _f_post_attn, 2, 2, [D_MODEL], [True, True])(
        o_sb, o_mla, p["out_norm_sb"], p["out_norm_mla"])
    return mixed, x_res


def _ffn_stage(x, mixed, mod, p):
    _, _, gate1, shift2, scale2, _ = _mod_parts(mod)
    attn = _make_linear("out_proj", 512, 512)(mixed, p["w_out"].reshape(D_MODEL, D_MODEL))

    x2, h2 = _make_rowwise("pre_ffn", _f_pre_ffn, 2, 4, [D_MODEL, D_MODEL], [True, True],
                           out_dtypes=[F32, BF16], grad_dtypes=[F32, BF16])(
        x, attn, gate1, p["norm_ffn"], scale2, shift2)
    return x2, _ffn_block(h2, p["w_gate"], p["w_up"], p["w_down"])


def _my_place():
    return lax.axis_index("x"), lax.axis_index("y"), lax.axis_index("c")


def _small_gather(x_ref, out_ref, send_sems, recv_sems, base, local_sem):
    m_per = x_ref.shape[0]
    x, y, c = _my_place()
    me, sibling = (x, y, c), (x, y, 1 - c)
    chips = [(1 - x, y), (x, 1 - y), (1 - x, 1 - y)]

    def rows(px, py, pc):
        return out_ref.at[pl.ds((4 * px + 2 * py + pc) * m_per, m_per), :]

    def copy(k, blk, to, src=None):
        return _remote(rows(*blk) if src is None else src, rows(*blk), send_sems, recv_sems, base + k, to)

    mine = pltpu.make_async_copy(x_ref, rows(*me), local_sem)
    first = [copy(0, me, sibling, src=x_ref)] + [copy(1 + j, me, (*chip, c), src=x_ref) for j, chip in enumerate(chips)]
    passed = [copy(4 + j, (*chip, c), sibling) for j, chip in enumerate(chips)]

    def start():
        mine.start()
        for cp in first:
            cp.start()

    def finish():
        for j, chip in enumerate(chips):
            copy(1 + j, (*chip, c), me).wait_recv()
            passed[j].start()
        copy(0, sibling, me).wait_recv()
        for j, chip in enumerate(chips):
            copy(4 + j, (*chip, 1 - c), me).wait_recv()
        for cp in first + passed:
            cp.wait_send()
        mine.wait()

    return start, finish


EARLY =("w_in", "w_q_up", "w_kv_up")
LATE = ("w_out", "w_gate", "w_up", "w_down")
BIG = EARLY + LATE
TRAVELS_TRANSPOSED = ("w_in", "w_gate", "w_up")
HALF_AXIS = {"w_in": 1, "w_q_up": 0, "w_kv_up": 0, "w_out": 0, "w_gate": 1, "w_up": 1, "w_down": 1}


def _half(ref, h, axis, lead=()):
    trail = ref.shape[len(lead):]
    idx = list(lead) + [slice(None)] * len(trail)
    at = len(trail) - 2 + axis
    n2 = trail[at] // 2
    idx[len(lead) + at] = pl.ds(h * n2, n2)
    return ref.at[tuple(idx)]


def _half_shape(shape, axis):
    shape = list(shape)
    shape[len(shape) - 2 + axis] //= 2
    return tuple(shape)


def _remote(src, dst, send_sems, recv_sems, k, to):
    return pltpu.make_async_remote_copy(src_ref=src, dst_ref=dst, send_sem=send_sems.at[k],
                                        recv_sem=recv_sems.at[k], device_id=to, device_id_type=MESH)


def _gather_weights(names, shards, lands, small_block, w_proj, b_proj):
    n_w = len(shards)
    axes = [HALF_AXIS[n] for n in names]
    n_vec, n_proj = w_proj.shape
    base_small, base_proj = 6 * n_w, 6 * n_w + 7

    def body(*refs):
        w_refs, small_ref, wp_ref, b_ref = refs[:n_w], refs[2 * n_w], refs[2 * n_w + 1], refs[2 * n_w + 2]
        o = 2 * n_w + 3
        out_refs, token, small_out, proj_out = refs[o:o + n_w], refs[o + n_w], refs[o + n_w + 1], refs[o + n_w + 2]
        send_sems, recv_sems, local_sems, wp_vmem, vec_vmem, mine_vmem = refs[o + n_w + 3:]
        token[...] = jnp.zeros_like(token)
        x, y, c = _my_place()
        sibling = (x, y, 1 - c)
        chips = [(1 - x, y), (x, 1 - y), (1 - x, 1 - y)]
        me = 2 * x + y
        small_start, small_finish = _small_gather(small_ref, small_out, send_sems, recv_sems, base_small,
                                                  local_sems.at[0])
        small_start()
        first = [_remote(_half(w_refs[i], c, axes[i]), _half(out_refs[i], c, axes[i], (me,)),
                         send_sems, recv_sems, 6 * i + j, (*chip, c))
                 for i in range(n_w) for j, chip in enumerate(chips)]
        for cp in first:
            cp.start()
        load_w = pltpu.make_async_copy(wp_ref, wp_vmem, local_sems.at[2])
        load_w.start()
        small_finish()
        load_vec = pltpu.make_async_copy(small_out, vec_vmem, local_sems.at[3])
        load_vec.start()
        load_vec.wait()
        load_w.wait()
        proj = jnp.dot(_silu(vec_vmem[...]), wp_vmem[...], precision=lax.Precision.HIGHEST,
                       preferred_element_type=F32) + b_ref[...]
        for d in range(N_DEV):
            mine_vmem[d:d + 1, :] = proj[8 * d:8 * d + 1, :]
        proj_start, proj_finish = _small_gather(mine_vmem, proj_out, send_sems, recv_sems, base_proj, local_sems.at[1])
        proj_start()
        passed = []
        for j, (cx, cy) in enumerate(chips):
            for i in range(n_w):
                blk = _half(out_refs[i], c, axes[i], (2 * cx + cy,))
                _remote(blk, blk, send_sems, recv_sems, 6 * i + j, (cx, cy, c)).wait_recv()
                cp = _remote(blk, blk, send_sems, recv_sems, 6 * i + 3 + j, sibling)
                cp.start()
                passed.append(cp)
        proj_finish()
        for j, (cx, cy) in enumerate(chips):
            for i in range(n_w):
                blk = _half(out_refs[i], 1 - c, axes[i], (2 * cx + cy,))
                _remote(blk, blk, send_sems, recv_sems, 6 * i + 3 + j, sibling).wait_recv()
        for cp in first + passed:
            cp.wait_send()

    n_rows = N_DEV * small_block.shape[0]
    outs = pl.pallas_call(
        body, name="gather_weights",
        out_shape=[jax.ShapeDtypeStruct(a.shape, a.dtype) for a in lands]
        + [jax.ShapeDtypeStruct((8, LANES), F32), jax.ShapeDtypeStruct((n_rows, n_vec), small_block.dtype),
           jax.ShapeDtypeStruct((N_DEV * N_DEV, n_proj), F32)],
        in_specs=[ANY] * (2 * n_w + 2) + [pl.BlockSpec(memory_space=pltpu.VMEM)],
        out_specs=[ANY] * n_w + [pl.BlockSpec(memory_space=pltpu.VMEM), ANY, ANY],
        input_output_aliases={n_w + i: i for i in range(n_w)},
        scratch_shapes=[pltpu.SemaphoreType.DMA((6 * n_w + 14,)), pltpu.SemaphoreType.DMA((6 * n_w + 14,)),
                        pltpu.SemaphoreType.DMA((4,)), pltpu.VMEM((n_vec, n_proj), F32), pltpu.VMEM((n_rows, n_vec), F32),
                        pltpu.VMEM((N_DEV, n_proj), F32)],
        compiler_params=pltpu.CompilerParams(vmem_limit_bytes=MM_VMEM_LIMIT),
    )(*shards, *lands, small_block, w_proj, b_proj)
    return outs[:n_w], outs[n_w], outs[n_w + 1], outs[n_w + 2]


def _pair_exchange(names, grads, call_name, small_block):
    n_w = len(grads)
    axes = [HALF_AXIS[n] for n in names]

    def body(*refs):
        g_refs, small_ref = refs[:n_w], refs[n_w]
        t_refs, small_out = refs[n_w + 1:2 * n_w + 1], refs[2 * n_w + 1]
        send_sems, recv_sems, local_sem = refs[2 * n_w + 2:]
        x, y, c = _my_place()
        small_start, small_finish = _small_gather(small_ref, small_out, send_sems, recv_sems, n_w, local_sem)
        small_start()
        sends = [_remote(_half(g_refs[i], 1 - c, axes[i]), t_refs[i], send_sems, recv_sems, i, (x, y, 1 - c))
                 for i in range(n_w)]
        for cp in sends:
            cp.start()
        small_finish()
        for cp in sends:
            cp.wait_recv()
        for cp in sends:
            cp.wait_send()

    outs = pl.pallas_call(
        body, name=call_name,
        out_shape=[jax.ShapeDtypeStruct(_half_shape(g.shape, a), g.dtype) for g, a in zip(grads, axes)]
        + [jax.ShapeDtypeStruct((N_DEV * small_block.shape[0], small_block.shape[1]), small_block.dtype)],
        in_specs=[ANY] * (n_w + 1), out_specs=[ANY] * (n_w + 1),
        scratch_shapes=[pltpu.SemaphoreType.DMA((n_w + 7,)), pltpu.SemaphoreType.DMA((n_w + 7,)),
                        pltpu.SemaphoreType.DMA],
    )(*grads, small_block)
    return outs[:n_w], outs[n_w]


def _sibling_join(halves, name, after):
    n_w = len(halves)

    def body(*refs):
        s_refs, j_refs = refs[:n_w], refs[n_w + 1:2 * n_w + 1]
        send_sems, recv_sems = refs[2 * n_w + 1:]
        x, y, c = _my_place()
        sends = [_remote(s_refs[i], j_refs[i], send_sems, recv_sems, i, (x, y, 1 - c)) for i in range(n_w)]
        for cp in sends:
            cp.start()
        for cp in sends:
            cp.wait_recv()
        for cp in sends:
            cp.wait_send()

    return pl.pallas_call(
        body, name=name,
        out_shape=[jax.ShapeDtypeStruct(s.shape, s.dtype) for s in halves],
        in_specs=[ANY] * (n_w + 1), out_specs=[ANY] * n_w,
        scratch_shapes=[pltpu.SemaphoreType.DMA((n_w,)), pltpu.SemaphoreType.DMA((n_w,))],
    )(*halves, after)


HBM_SPEC = pl.BlockSpec(memory_space=pltpu.HBM)
SEM_SPEC = pl.BlockSpec(memory_space=pltpu.SEMAPHORE)
DATAFLOW = pltpu.SideEffectType.DATAFLOW_SIDE_EFFECTING


def _in_hbm(a):
    return pltpu.with_memory_space_constraint(a, pltpu.HBM)


def _exchange_start(name, srcs, lands, plan, n_copies, after, thru):
    n = len(srcs)

    def body(*refs):
        src_refs, land_refs = refs[:n], refs[n:2 * n]
        send_sems, recv_sems = refs[2 * n + 2], refs[2 * n + 3]
        for k, (src, dst, to, k_recv) in enumerate(plan(src_refs, land_refs)):
            pltpu.make_async_remote_copy(src_ref=src, dst_ref=dst, send_sem=send_sems.at[k],
                                         recv_sem=recv_sems.at[k_recv], device_id=to, device_id_type=MESH).start()

    outs = pl.pallas_call(
        body, name=name,
        out_shape=(pltpu.SemaphoreType.DMA((n_copies,)), pltpu.SemaphoreType.DMA((n_copies,)),
                   *[pltpu.HBM(a.shape, a.dtype) for a in list(srcs) + list(lands) + [thru]]),
        in_specs=[HBM_SPEC] * (2 * n + 1) + [ANY],
        out_specs=(SEM_SPEC, SEM_SPEC, *[HBM_SPEC] * (2 * n + 1)),
        input_output_aliases={i: 2 + i for i in range(2 * n + 1)},
        compiler_params=pltpu.CompilerParams(has_side_effects=DATAFLOW),
    )(*[_in_hbm(a) for a in list(srcs) + list(lands) + [thru]], after)
    return outs[0], outs[1], outs[2:2 + n], outs[2 + n:2 + 2 * n], outs[2 + 2 * n]


def _exchange_wait(name, started, plan, after):
    send_sems, recv_sems, srcs, lands, _ = started
    n = len(srcs)

    def body(*refs):
        src_refs, land_refs = refs[:n], refs[n:2 * n]
        s_sems, r_sems = refs[2 * n], refs[2 * n + 1]
        for k, (src, dst, to, _) in enumerate(plan(src_refs, land_refs)):
            cp = _remote(src, dst, s_sems, r_sems, k, to)
            cp.wait_send()
            cp.wait_recv()

    outs = pl.pallas_call(
        body, name=name,
        out_shape=tuple(pltpu.HBM(a.shape, a.dtype) for a in list(srcs) + list(lands)),
        in_specs=[HBM_SPEC] * (2 * n) + [SEM_SPEC, SEM_SPEC, ANY],
        out_specs=tuple([HBM_SPEC] * (2 * n)),
        input_output_aliases={i: i for i in range(2 * n)},
        compiler_params=pltpu.CompilerParams(has_side_effects=DATAFLOW),
    )(*srcs, *lands, send_sems, recv_sems, after)
    return outs[:n], outs[n:]


def _late_gather_plan(src_refs, land_refs):
    x, y, c = _my_place()
    chips = [(1 - x, y), (x, 1 - y), (1 - x, 1 - y)]
    plan = [(src, land.at[2 * x + y], (cx, cy, c)) for src, land in zip(src_refs, land_refs) for cx, cy in chips]
    return [entry + (k,) for k, entry in enumerate(plan)]


def _late_scatter_plan(src_refs, land_refs):
    x, y, c = _my_place()
    chips = [(1 - x, y), (x, 1 - y), (1 - x, 1 - y)]
    plan = [(src.at[2 * cx + cy], land.at[j], (cx, cy, c))
            for src, land in zip(src_refs, land_refs) for j, (cx, cy) in enumerate(chips)]
    return [entry + (k,) for k, entry in enumerate(plan)]


def _direct_scatter_plan(names):
    axes = [HALF_AXIS[n] for n in names]

    def plan(src_refs, land_refs):
        x, y, c = _my_place()
        chips = [(1 - x, y), (x, 1 - y), (1 - x, 1 - y)]
        out = []
        for i, (src, land) in enumerate(zip(src_refs, land_refs)):
            for f, (cx, cy) in enumerate(chips):
                for core in range(2):
                    out.append((_half(src, core, axes[i], (2 * cx + cy,)), land.at[2 * f + c], (cx, cy, core),
                                7 * i + 2 * f + c))
            out.append((_half(src, 1 - c, axes[i], (2 * x + y,)), land.at[6], (x, y, 1 - c), 7 * i + 6))
        return out

    return plan


def _row_tile(rows, mult=16, limit=ROW_TILE):
    return max(d for d in range(mult, limit + 1, mult) if rows % d == 0)


def _pair_sum(place, g, theirs, axis, name):
    nj, rr, cc = theirs.shape
    tr = _row_tile(rr, limit=1024)
    nb = rr // tr
    if axis == 0:
        g_map = lambda j, i, pr: (j, pr[0] * nb + i, 0)
    else:
        g_map = lambda j, i, pr: (j, i, pr[0])

    def body(pr, g_ref, t_ref, o_ref):
        o_ref[...] = (g_ref[...].astype(F32) + t_ref[...].astype(F32)).astype(BF16)

    spec = pl.BlockSpec((None, tr, cc), lambda j, i, pr: (j, i, 0))
    return pl.pallas_call(
        body, name=name,
        grid_spec=pltpu.PrefetchScalarGridSpec(
            num_scalar_prefetch=1, grid=(nj, nb),
            in_specs=[pl.BlockSpec((None, tr, cc), g_map), spec], out_specs=spec),
        out_shape=jax.ShapeDtypeStruct(theirs.shape, BF16))(place, g, theirs)


def _chip_sum(place, pair_sums, parts, name):
    _, rr, cc = parts.shape
    tr = _row_tile(rr, limit=1024)

    def body(pr, h_ref, p_ref, o_ref):
        acc = p_ref[0].astype(F32)
        for j in range(1, N_CHIPS - 1):
            acc = acc + p_ref[j].astype(F32)
        o_ref[...] = (acc + h_ref[...].astype(F32)).astype(BF16)

    return pl.pallas_call(
        body, name=name,
        grid_spec=pltpu.PrefetchScalarGridSpec(
            num_scalar_prefetch=1, grid=(rr // tr,),
            in_specs=[pl.BlockSpec((None, tr, cc), lambda i, pr: (pr[1], i, 0)),
                      pl.BlockSpec((N_CHIPS - 1, tr, cc), lambda i, pr: (0, i, 0))],
            out_specs=pl.BlockSpec((tr, cc), lambda i, pr: (i, 0))),
        out_shape=jax.ShapeDtypeStruct((rr, cc), BF16))(place, pair_sums, parts)


def _chip_sum_direct(place, g, parts, axis, name):
    n_parts, rr, cc = parts.shape
    tr = _row_tile(rr, limit=1024)
    nb = rr // tr
    if axis == 0:
        g_map = lambda i, pr: (pr[1], pr[0] * nb + i, 0)
    else:
        g_map = lambda i, pr: (pr[1], i, pr[0])

    def body(pr, g_ref, p_ref, o_ref):
        acc = p_ref[0].astype(F32)
        for j in range(1, n_parts):
            acc = acc + p_ref[j].astype(F32)
        o_ref[...] = (acc + g_ref[...].astype(F32)).astype(BF16)

    return pl.pallas_call(
        body, name=name,
        grid_spec=pltpu.PrefetchScalarGridSpec(
            num_scalar_prefetch=1, grid=(nb,),
            in_specs=[pl.BlockSpec((None, tr, cc), g_map), pl.BlockSpec((n_parts, tr, cc), lambda i, pr: (0, i, 0))],
            out_specs=pl.BlockSpec((tr, cc), lambda i, pr: (i, 0))),
        out_shape=jax.ShapeDtypeStruct((rr, cc), BF16))(place, g, parts)


def _silu(v):
    return v / (1.0 + jnp.exp(-v))


def _loss_and_grads(x2, ffn, target, gate):
    t, d = x2.shape

    def half_loss(x2_blk, ffn_blk, gate_row, target_blk):
        return 0.5 * _f_loss(x2_blk, ffn_blk, target_blk, gate_row)[0]

    def body(x2_ref, ffn_ref, tgt_ref, gate_ref, loss_ref, dx2_ref, dffn_ref, dgate_ref):
        rows, vjp = jax.vjp(lambda a, b, g: half_loss(a, b, g, tgt_ref[...]), x2_ref[...], ffn_ref[...], gate_ref[...])
        loss_ref[...] = rows
        dx2_ref[...], dffn_ref[...], dgate = vjp(jnp.ones_like(rows))

        @pl.when(pl.program_id(0) == 0)
        def _():
            dgate_ref[...] = jnp.zeros_like(dgate_ref)

        dgate_ref[...] += dgate

    blk = pl.BlockSpec((ROW_TILE, d), lambda i: (i, 0))
    row = pl.BlockSpec((1, d), lambda i: (0, 0))
    return pl.pallas_call(
        body, name="loss_and_grads", grid=(t // ROW_TILE,),
        in_specs=[blk, blk, blk, row],
        out_specs=[pl.BlockSpec((ROW_TILE, 1), lambda i: (i, 0)), blk, blk, row],
        out_shape=[jax.ShapeDtypeStruct((t, 1), F32), jax.ShapeDtypeStruct((t, d), F32), jax.ShapeDtypeStruct((t, d), F32),
                   jax.ShapeDtypeStruct((1, d), F32)],
        compiler_params=pltpu.CompilerParams(dimension_semantics=("arbitrary",), vmem_limit_bytes=MM_VMEM_LIMIT),
    )(x2, ffn, target, gate)


def _ada_bwd(c_all, dmod_cols):
    def body(c_ref, d_ref, o_ref):
        o_ref[...] = lax.dot_general(_silu(c_ref[...]), d_ref[...], (((0,), (0,)), ((), ())),
                                     precision=lax.Precision.HIGHEST, preferred_element_type=F32)

    return pl.pallas_call(body, name="ada_bwd", out_shape=jax.ShapeDtypeStruct((c_all.shape[1], dmod_cols.shape[1]), F32),
                          compiler_params=pltpu.CompilerParams(vmem_limit_bytes=MM_VMEM_LIMIT))(c_all, dmod_cols)


def _adamw_math(w, g, m, v):
    m = ADAM_B1 * m + (1.0 - ADAM_B1) * g
    v = ADAM_B2 * v + (1.0 - ADAM_B2) * (g * g)
    m_hat = m / (1.0 - ADAM_B1 ** ADAM_STEP)
    v_hat = v / (1.0 - ADAM_B2 ** ADAM_STEP)
    delta = -ADAM_LR * (m_hat / (jnp.sqrt(v_hat) + ADAM_EPS) + ADAM_WD * w)
    return delta, m, v


def _adamw(w, g, m, v, name):
    r, ccols = w.shape
    tr = max(d for d in range(8, ROW_TILE + 1, 8) if r % d == 0)
    spec = pl.BlockSpec((tr, ccols), lambda i: (i, 0))

    def body(w_ref, g_ref, m_ref, v_ref, d_ref, nm_ref, nv_ref):
        d_ref[...], nm_ref[...], nv_ref[...] = _adamw_math(w_ref[...], g_ref[...], m_ref[...], v_ref[...])

    return pl.pallas_call(body, name=name, grid=(r // tr,), in_specs=[spec] * 4, out_specs=[spec] * 3,
                          out_shape=[jax.ShapeDtypeStruct(w.shape, F32)] * 3,
                          compiler_params=pltpu.CompilerParams(vmem_limit_bytes=MM_VMEM_LIMIT))(w, g, m, v)


def _small_layout(sizes):
    offs, off = [], 0
    for n in sizes:
        offs.append(off)
        off += -(-n // LANES) * LANES
    total = -(-(off + LANES) // (8 * LANES)) * (8 * LANES)
    return offs, off, total


def _adamw_small(ws, g_all, ms, vs, offs, loss_off):
    n_p = len(ws)

    def device_sum(g_ref, off, width):
        blk = g_ref[:, off:off + width]
        acc = blk[0:1]
        for d in range(1, N_DEV):
            acc = acc + blk[d:d + 1]
        return acc

    def body(*refs):
        w_refs, m_refs, v_refs = refs[:n_p], refs[n_p:2 * n_p], refs[2 * n_p:3 * n_p]
        g_ref = refs[3 * n_p]
        outs = refs[3 * n_p + 1:]
        for i in range(n_p):
            n = w_refs[i].shape[1]
            g = device_sum(g_ref, offs[i], -(-n // LANES) * LANES)[:, :n]
            outs[i][...] = g
            outs[n_p + i][...], outs[2 * n_p + i][...], outs[3 * n_p + i][...] = _adamw_math(
                w_refs[i][...], g, m_refs[i][...], v_refs[i][...])
        outs[4 * n_p][...] = device_sum(g_ref, loss_off, LANES)

    res = pl.pallas_call(
        body, name="adamw_small",
        out_shape=[jax.ShapeDtypeStruct(a.shape, F32) for a in list(ws) * 4] + [jax.ShapeDtypeStruct((1, LANES), F32)],
    )(*ws, *ms, *vs, g_all)
    return res[:n_p], res[n_p:2 * n_p], res[2 * n_p:3 * n_p], res[3 * n_p:4 * n_p], res[4 * n_p]


def _adamw_halves(place, w, own, sib, m, v, axis, name, after):
    r, cc = w.shape
    if axis == 0:
        rows, gc = own.shape[0], own.shape[1]
        tr = _row_tile(rows)
        nb = rows // tr
        w_spec = pl.BlockSpec((tr, cc), lambda h, i, pr: (h * nb + i, 0))
        g_spec = pl.BlockSpec((tr, gc), lambda h, i, pr: (i, 0))
    else:
        tr = _row_tile(r)
        nb = r // tr
        gc = own.shape[1]
        w_spec = pl.BlockSpec((tr, gc), lambda h, i, pr: (i, h))
        g_spec = pl.BlockSpec((tr, gc), lambda h, i, pr: (i, 0))
    wc = w_spec.block_shape[1]

    def body(pr, w_ref, o_ref, s_ref, m_ref, v_ref, after_ref, g_ref, d_ref, nm_ref, nv_ref):
        g = jnp.where(pl.program_id(0) == pr[0], o_ref[...], s_ref[...]).astype(F32)[:, :wc]
        g_ref[...] = g
        d_ref[...], nm_ref[...], nv_ref[...] = _adamw_math(w_ref[...], g, m_ref[...], v_ref[...])

    return pl.pallas_call(
        body, name=name,
        grid_spec=pltpu.PrefetchScalarGridSpec(
            num_scalar_prefetch=1, grid=(2, nb),
            in_specs=[w_spec, g_spec, g_spec, w_spec, w_spec, ANY], out_specs=[w_spec] * 4),
        out_shape=[jax.ShapeDtypeStruct(w.shape, F32)] * 4,
        compiler_params=pltpu.CompilerParams(vmem_limit_bytes=MM_VMEM_LIMIT))(place, w, own, sib, m, v, after)


SMALL = ("b_ada", "norm_attn", "norm_ffn", "q_a_norm", "kv_a_norm", "q_norm", "k_nope_norm", "k_rope_norm",
         "out_norm_sb", "out_norm_mla")
WEIGHTS = ("w_ada", "b_ada", "norm_attn", "norm_ffn", "w_in", "q_a_norm", "w_q_up", "kv_a_norm", "w_kv_up",
           "q_norm", "k_nope_norm", "k_rope_norm", "out_norm_sb", "out_norm_mla", "w_out", "w_gate", "w_up",
           "w_down")


def kernel(x, c, positions, w_ada, b_ada, norm_attn, norm_ffn, w_in, q_a_norm, w_q_up, kv_a_norm, w_kv_up, q_norm, k_nope_norm, k_rope_norm, out_norm_sb, out_norm_mla, w_out, w_gate, w_up, w_down, loss_target, m_w_ada, m_b_ada, m_norm_attn, m_norm_ffn, m_w_in, m_q_a_norm, m_w_q_up, m_kv_a_norm, m_w_kv_up, m_q_norm, m_k_nope_norm, m_k_rope_norm, m_out_norm_sb, m_out_norm_mla, m_w_out, m_w_gate, m_w_up, m_w_down, v_w_ada, v_b_ada, v_norm_attn, v_norm_ffn, v_w_in, v_q_a_norm, v_w_q_up, v_kv_a_norm, v_w_kv_up, v_q_norm, v_k_nope_norm, v_k_rope_norm, v_out_norm_sb, v_out_norm_mla, v_w_out, v_w_gate, v_w_up, v_w_down):
    local = dict(locals())
    w = {n: local[n][0] for n in WEIGHTS}
    m = {n: local["m_" + n][0] for n in WEIGHTS}
    v = {n: local["v_" + n][0] for n in WEIGHTS}
    small = {n: w[n].reshape(1, -1) for n in SMALL}
    ix, iy, ic = _my_place()
    chip = 2 * ix + iy
    dev = 2 * chip + ic
    xs, target = x[0], loss_target[0]
    seq = xs.shape[0]

    ff_pad = FF_SHARD_PAD - FF_SHARD
    shards = {n: (w[n].T if n in TRAVELS_TRANSPOSED else w[n]).astype(BF16) for n in BIG}
    for n in ("w_gate", "w_up", "w_down"):
        shards[n] = jnp.pad(shards[n], ((0, ff_pad), (0, 0)))
    def landing(names):
        return [lax.dynamic_update_index_in_dim(lax.empty((N_CHIPS,) + shards[n].shape, BF16), shards[n], chip, 0)
                for n in names]

    ada_cols = w["w_ada"].shape[1]
    b_cols = lax.dynamic_slice_in_dim(small["b_ada"], chip * ada_cols, ada_cols, axis=1)
    early, early_done, c_gathered, mod_all = _gather_weights(
        EARLY, [shards[n] for n in EARLY], landing(EARLY), jnp.broadcast_to(c.reshape(1, D_MODEL), (8, D_MODEL)),
        w["w_ada"], b_cols)
    gathered = dict(zip(EARLY, early))
    c_all = c_gathered.reshape(N_DEV, 8, D_MODEL)[:, 0]
    mod_all = mod_all.reshape(N_CHIPS, 2, N_DEV, ada_cols)
    mod = lax.dynamic_index_in_dim(mod_all[:, 0], dev, axis=1, keepdims=False).reshape(1, N_MOD * D_MODEL)

    late_gather = _exchange_start("gather_late_start", [shards[n] for n in LATE], landing(LATE), _late_gather_plan,
                                  3 * len(LATE), early_done, mod)
    mod = late_gather[4]

    half = MLA_ROPE // 2
    freqs = 1.0 / (ROPE_THETA ** (np.arange(half, dtype=np.float32) / half))
    zeros = np.zeros(LANES - MLA_ROPE, np.float32)
    freqs_row = jnp.asarray(np.concatenate([freqs, freqs, zeros]).astype(np.float32)[None])
    sign_row = jnp.asarray(np.concatenate([-np.ones(half), np.ones(half), zeros]).astype(np.float32)[None])
    cos, sin = _rope_tables(positions.reshape(seq, 1), freqs_row, sign_row)

    place = jnp.stack([ic, chip]).astype(jnp.int32)
    small_params = {n: small[n] for n in SMALL if n != "b_ada"}

    p1 = {**{n: gathered[n] for n in EARLY}, **small_params}
    (mixed, x_res), mixing_vjp = jax.vjp(lambda x_, mod_, p_: _mixing_stage(x_, mod_, p_, cos, sin), xs, mod, p1)
    _, landed = _exchange_wait("gather_late_wait", late_gather, _late_gather_plan, mixed)
    p2 = {**dict(zip(LATE, landed)), **small_params}
    (x2, ffn), ffn_vjp = jax.vjp(_ffn_stage, x_res, mixed, mod, p2)
    loss_rows, g_x2, g_ffn, g_gate2 = _loss_and_grads(x2, ffn, target, _mod_parts(mod)[5])
    loss_part = jnp.sum(loss_rows)
    gx2, gmixed, gmod2, gp2 = ffn_vjp((g_x2, g_ffn))
    gmod2 = gmod2 + jnp.concatenate([jnp.zeros((1, (N_MOD - 1) * D_MODEL), F32), g_gate2], axis=1)
    late_grads = [gp2[n] for n in LATE]
    late_plan = _direct_scatter_plan(LATE)
    late_scatter = _exchange_start(
        "grad_scatter_late_start", late_grads,
        [lax.empty((7,) + _half_shape(gr.shape[1:], HALF_AXIS[n]), BF16) for n, gr in zip(LATE, late_grads)],
        late_plan, 7 * len(LATE), gx2, gmixed)
    gx, gmod1, gp1 = mixing_vjp((late_scatter[4], gx2))
    gmod = gmod1 + gmod2
    gp = {n: gp1[n] + gp2[n] for n in small_params}

    sizes = [w[n].size for n in SMALL]
    offs, loss_off, n_small = _small_layout(sizes)
    pieces = []
    for n, size in zip(SMALL, sizes):
        pieces.append(gmod if n == "b_ada" else gp[n])
        if size % LANES:
            pieces.append(jnp.zeros((1, LANES - size % LANES), F32))
    pieces += [jnp.full((1, LANES), loss_part), jnp.zeros((1, n_small - loss_off - LANES), F32)]
    small_vec = jnp.concatenate(pieces, axis=1)

    g, delta, new_m, new_v = {}, {}, {}, {}

    def update(names, own, sib, after):
        for n, o, s in zip(names, own, sib):
            if n in TRAVELS_TRANSPOSED:
                res = _adamw_halves(place, w[n].T, o, s, m[n].T, v[n].T, HALF_AXIS[n], "adamw_" + n, after)
                g[n], delta[n], new_m[n], new_v[n] = [r.T for r in res]
            else:
                g[n], delta[n], new_m[n], new_v[n] = _adamw_halves(place, w[n], o, s, m[n], v[n], HALF_AXIS[n],
                                                                   "adamw_" + n, after)

    late_grads, late_parts = _exchange_wait("grad_scatter_late_wait", late_scatter, late_plan, gx)
    own_late = [_chip_sum_direct(place, gr, pt, HALF_AXIS[n], "grad_chip_sum_" + n)
                for n, gr, pt in zip(LATE, late_grads, late_parts)]
    early_grads = [gp1[n] for n in EARLY]
    theirs, small_gathered = _pair_exchange(EARLY, early_grads, "grad_pair_exchange_early",
                                            small_vec.reshape(8, n_small // 8))
    small_all = small_gathered.reshape(N_DEV, n_small)
    sib_late = _sibling_join(own_late, "grad_sibling_join_late", small_all)
    early_sums = [_pair_sum(place, gr, th, HALF_AXIS[n], "grad_pair_sum_" + n)
                  for n, gr, th in zip(EARLY, early_grads, theirs)]
    early_scatter = _exchange_start(
        "grad_scatter_early_start", early_sums,
        [lax.empty((N_CHIPS - 1,) + s.shape[1:], BF16) for s in early_sums], _late_scatter_plan, 3 * len(EARLY),
        sib_late[0], small_all)
    small_all = early_scatter[4]
    update(LATE, own_late, sib_late, small_all)

    *small_out, loss_row = _adamw_small([small[n] for n in SMALL], small_all, [m[n].reshape(1, -1) for n in SMALL],
                                        [v[n].reshape(1, -1) for n in SMALL], offs, loss_off)
    loss = loss_row[0, 0]
    for d, outs_d in zip((g, delta, new_m, new_v), small_out):
        d.update({n: o.reshape(w[n].shape) for n, o in zip(SMALL, outs_d)})

    dmod_all = small_all[:, :N_MOD * D_MODEL]
    g["w_ada"] = _ada_bwd(c_all, lax.dynamic_slice_in_dim(dmod_all, chip * ada_cols, ada_cols, axis=1))
    delta["w_ada"], new_m["w_ada"], new_v["w_ada"] = _adamw(w["w_ada"], g["w_ada"], m["w_ada"], v["w_ada"], "adamw_w_ada")

    early_sums, early_parts = _exchange_wait("grad_scatter_early_wait", early_scatter, _late_scatter_plan,
                                             delta["w_ada"])
    own_early = [_chip_sum(place, ps, pt, "grad_chip_sum_" + n)
                 for n, ps, pt in zip(EARLY, early_sums, early_parts)]
    sib_early = _sibling_join(own_early, "grad_sibling_join_early", delta["w_ada"])
    update(EARLY, own_early, sib_early, sib_early[0])

    def outs(d):
        return [d[n][None] for n in WEIGHTS]

    return (loss, gx[None], *outs(g), *outs(delta), *outs(new_m), *outs(new_v))
```

```python
import numpy as np
import jax
import jax.numpy as jnp
from jax import lax
from jax.experimental import pallas as pl
from jax.experimental.pallas import tpu as pltpu

F32 = jnp.float32
BF16 = jnp.bfloat16
MESH = pl.DeviceIdType.MESH
ANY = pl.BlockSpec(memory_space=pl.ANY)

D_MODEL = 1024
SB_HEADS = 8
SB_HEAD_DIM = 64
SB_WIDTH = 512
MLA_HEADS = 4
MLA_NOPE = 128
MLA_ROPE = 64
MLA_QK = 192
MLA_V = 128
MLA_Q_RANK = 384
MLA_KV_RANK = 256
D_FF = 2816
N_MOD = 6
ROPE_THETA = 10000.0
EPS = 1e-6
LANES = 128

ADAM_LR = 0.001
ADAM_B1 = 0.9
ADAM_B2 = 0.999
ADAM_EPS = 1e-08
ADAM_WD = 0.01
ADAM_STEP = 10

N_CHIPS = 4
N_DEV = 8
ROW_TILE = 512
MM_ROW_TILE = 512
ATT_BLK = 256
MM_VMEM_LIMIT = 56 * 1024 * 1024
FF_SHARD = D_FF // N_CHIPS
FF_SHARD_PAD = 768


def _mm(a, b, mode, name, tm, tn, out_dtype=F32):
    if mode == "nn":
        (m, k), n = a.shape, b.shape[1]
        a_spec = pl.BlockSpec((tm, k), lambda j, i: (i, 0))
        b_spec = pl.BlockSpec((k, tn), lambda j, i: (0, j))
        dims = (((1,), (0,)), ((), ()))
    elif mode == "nt":
        (m, k), n = a.shape, b.shape[0]
        a_spec = pl.BlockSpec((tm, k), lambda j, i: (i, 0))
        b_spec = pl.BlockSpec((tn, k), lambda j, i: (j, 0))
        dims = (((1,), (1,)), ((), ()))
    else:
        (k, m), n = a.shape, b.shape[1]
        a_spec = pl.BlockSpec((k, tm), lambda j, i: (0, i))
        b_spec = pl.BlockSpec((k, tn), lambda j, i: (0, j))
        dims = (((0,), (0,)), ((), ()))
    assert m % tm == 0 and n % tn == 0, (name, m, n, tm, tn)

    def body(a_ref, b_ref, o_ref):
        o_ref[...] = lax.dot_general(a_ref[...].astype(BF16), b_ref[...].astype(BF16), dims,
                                     preferred_element_type=F32).astype(out_dtype)

    return pl.pallas_call(
        body, name=name, grid=(n // tn, m // tm),
        in_specs=[a_spec, b_spec],
        out_specs=pl.BlockSpec((tm, tn), lambda j, i: (i, j)),
        out_shape=jax.ShapeDtypeStruct((m, n), out_dtype),
        compiler_params=pltpu.CompilerParams(dimension_semantics=("arbitrary", "arbitrary"),
                                             vmem_limit_bytes=MM_VMEM_LIMIT),
    )(a, b)


def _make_linear(name, tk_w, tn_w):
    @jax.custom_vjp
    def op(a, w):
        return _mm(a, w, "nn", name + "_fwd", MM_ROW_TILE, w.shape[1])

    def fwd(a, w):
        return op(a, w), (a, w)

    def bwd(res, dy):
        a, w = res
        da = _mm(dy, w, "nt", name + "_dx", MM_ROW_TILE, w.shape[0])
        dw = _mm(a, dy, "tn", name + "_dw", tk_w, tn_w, out_dtype=BF16)
        return da, dw

    op.defvjp(fwd, bwd)
    return op


def _make_linear_split_t(name, widths, tk_w):
    starts = [sum(widths[:g]) for g in range(len(widths))]

    def call_fwd(a, wt):
        t, k = a.shape
        n = wt.shape[0]

        def body(a_ref, w_ref, *o_refs):
            y = _nt(a_ref[...].astype(BF16), w_ref[...])
            for o_ref, s0, wd in zip(o_refs, starts, widths):
                o_ref[...] = y[:, s0:s0 + wd]

        return pl.pallas_call(
            body, name=name + "_fwd", grid=(t // MM_ROW_TILE,),
            in_specs=[pl.BlockSpec((MM_ROW_TILE, k), lambda i: (i, 0)), pl.BlockSpec((n, k), lambda i: (0, 0))],
            out_specs=[pl.BlockSpec((MM_ROW_TILE, wd), lambda i: (i, 0)) for wd in widths],
            out_shape=[jax.ShapeDtypeStruct((t, wd), F32) for wd in widths],
            compiler_params=pltpu.CompilerParams(dimension_semantics=("arbitrary",), vmem_limit_bytes=MM_VMEM_LIMIT),
        )(a, wt)

    def call_dx(dys, wt):
        t = dys[0].shape[0]
        n, k = wt.shape

        def body(*refs):
            dy_refs, w_ref, o_ref = refs[:-2], refs[-2], refs[-1]
            acc = jnp.zeros((MM_ROW_TILE, k), F32)
            for dy_ref, s0, wd in zip(dy_refs, starts, widths):
                acc = acc + jnp.dot(dy_ref[...].astype(BF16), w_ref[s0:s0 + wd, :], preferred_element_type=F32)
            o_ref[...] = acc

        return pl.pallas_call(
            body, name=name + "_dx", grid=(t // MM_ROW_TILE,),
            in_specs=[pl.BlockSpec((MM_ROW_TILE, wd), lambda i: (i, 0)) for wd in widths]
            + [pl.BlockSpec((n, k), lambda i: (0, 0))],
            out_specs=pl.BlockSpec((MM_ROW_TILE, k), lambda i: (i, 0)),
            out_shape=jax.ShapeDtypeStruct((t, k), F32),
            compiler_params=pltpu.CompilerParams(dimension_semantics=("arbitrary",), vmem_limit_bytes=MM_VMEM_LIMIT),
        )(*dys, wt)

    def call_dw(a, dys, wt):
        t, k = a.shape
        n = wt.shape[0]

        def body(a_ref, *refs):
            dy_refs, o_ref = refs[:-1], refs[-1]
            ab = a_ref[...].astype(BF16)
            for dy_ref, s0, wd in zip(dy_refs, starts, widths):
                o_ref[s0:s0 + wd, :] = _tn(dy_ref[...].astype(BF16), ab).astype(BF16)
            if starts[-1] + widths[-1] < n:
                o_ref[starts[-1] + widths[-1]:, :] = jnp.zeros((n - starts[-1] - widths[-1], tk_w), BF16)

        return pl.pallas_call(
            body, name=name + "_dw", grid=(k // tk_w,),
            in_specs=[pl.BlockSpec((t, tk_w), lambda i: (0, i))]
            + [pl.BlockSpec((t, wd), lambda i: (0, 0)) for wd in widths],
            out_specs=pl.BlockSpec((n, tk_w), lambda i: (0, i)),
            out_shape=jax.ShapeDtypeStruct((n, k), BF16),
            compiler_params=pltpu.CompilerParams(dimension_semantics=("arbitrary",), vmem_limit_bytes=MM_VMEM_LIMIT),
        )(a, *dys)

    @jax.custom_vjp
    def op(a, wt):
        return tuple(call_fwd(a, wt))

    def fwd(a, wt):
        return op(a, wt), (a, wt)

    def bwd(res, dys):
        a, wt = res
        return call_dx(dys, wt), call_dw(a, dys, wt)

    op.defvjp(fwd, bwd)
    return op


def _make_linear_split(name, widths, tk_w):
    starts = [sum(widths[:g]) for g in range(len(widths))]

    def call_fwd(a, w):
        t, k = a.shape
        n = w.shape[1]

        def body(a_ref, w_ref, *o_refs):
            y = jnp.dot(a_ref[...].astype(BF16), w_ref[...], preferred_element_type=F32)
            for o_ref, s0, wd in zip(o_refs, starts, widths):
                o_ref[...] = y[:, s0:s0 + wd]

        return pl.pallas_call(
            body, name=name + "_fwd", grid=(t // MM_ROW_TILE,),
            in_specs=[pl.BlockSpec((MM_ROW_TILE, k), lambda i: (i, 0)), pl.BlockSpec((k, n), lambda i: (0, 0))],
            out_specs=[pl.BlockSpec((MM_ROW_TILE, wd), lambda i: (i, 0)) for wd in widths],
            out_shape=[jax.ShapeDtypeStruct((t, wd), F32) for wd in widths],
            compiler_params=pltpu.CompilerParams(dimension_semantics=("arbitrary",), vmem_limit_bytes=MM_VMEM_LIMIT),
        )(a, w)

    def call_dx(dys, w):
        t = dys[0].shape[0]
        k, n = w.shape

        def body(*refs):
            dy_refs, w_ref, o_ref = refs[:-2], refs[-2], refs[-1]
            acc = jnp.zeros((MM_ROW_TILE, k), F32)
            for dy_ref, s0, wd in zip(dy_refs, starts, widths):
                acc = acc + _nt(dy_ref[...].astype(BF16), w_ref[:, s0:s0 + wd])
            o_ref[...] = acc

        return pl.pallas_call(
            body, name=name + "_dx", grid=(t // MM_ROW_TILE,),
            in_specs=[pl.BlockSpec((MM_ROW_TILE, wd), lambda i: (i, 0)) for wd in widths]
            + [pl.BlockSpec((k, n), lambda i: (0, 0))],
            out_specs=pl.BlockSpec((MM_ROW_TILE, k), lambda i: (i, 0)),
            out_shape=jax.ShapeDtypeStruct((t, k), F32),
            compiler_params=pltpu.CompilerParams(dimension_semantics=("arbitrary",), vmem_limit_bytes=MM_VMEM_LIMIT),
        )(*dys, w)

    def call_dw(a, dys, w):
        t, k = a.shape
        n = w.shape[1]

        def body(a_ref, *refs):
            dy_refs, o_ref = refs[:-1], refs[-1]
            ab = a_ref[...].astype(BF16)
            for dy_ref, s0, wd in zip(dy_refs, starts, widths):
                o_ref[:, s0:s0 + wd] = _tn(ab, dy_ref[...].astype(BF16)).astype(BF16)
            if starts[-1] + widths[-1] < n:
                o_ref[:, starts[-1] + widths[-1]:] = jnp.zeros((tk_w, n - starts[-1] - widths[-1]), BF16)

        return pl.pallas_call(
            body, name=name + "_dw", grid=(k // tk_w,),
            in_specs=[pl.BlockSpec((t, tk_w), lambda i: (0, i))]
            + [pl.BlockSpec((t, wd), lambda i: (0, 0)) for wd in widths],
            out_specs=pl.BlockSpec((tk_w, n), lambda i: (i, 0)),
            out_shape=jax.ShapeDtypeStruct((k, n), BF16),
            compiler_params=pltpu.CompilerParams(dimension_semantics=("arbitrary",), vmem_limit_bytes=MM_VMEM_LIMIT),
        )(a, *dys)

    @jax.custom_vjp
    def op(a, w):
        return tuple(call_fwd(a, w))

    def fwd(a, w):
        return op(a, w), (a, w)

    def bwd(res, dys):
        a, w = res
        return call_dx(dys, w), call_dw(a, dys, w)

    op.defvjp(fwd, bwd)
    return op


def _row_spec(arr, tb):
    return pl.BlockSpec((tb, arr.shape[1]), lambda i: (i, 0))


def _full_spec(arr):
    return pl.BlockSpec(arr.shape, lambda i: (0, 0))


def _make_rowwise(name, f, n_rows, n_params, out_cols, diff_rows, out_dtypes=None, grad_dtypes=None):
    n_out = len(out_cols)
    out_dtypes = out_dtypes or [F32] * n_out
    grad_dtypes = grad_dtypes or [F32] * sum(diff_rows)

    def call_fwd(rows, params):
        t = rows[0].shape[0]

        def body(*refs):
            ins = [r[...] for r in refs[:n_rows + n_params]]
            outs = f(*ins)
            for o_ref, o in zip(refs[n_rows + n_params:], outs):
                o_ref[...] = o.astype(o_ref.dtype)

        return pl.pallas_call(
            body, name=name + "_fwd", grid=(t // ROW_TILE,),
            in_specs=[_row_spec(a, ROW_TILE) for a in rows] + [_full_spec(p) for p in params],
            out_specs=[pl.BlockSpec((ROW_TILE, n), lambda i: (i, 0)) for n in out_cols],
            out_shape=[jax.ShapeDtypeStruct((t, n), dt) for n, dt in zip(out_cols, out_dtypes)],
            compiler_params=pltpu.CompilerParams(dimension_semantics=("arbitrary",),
                                                 vmem_limit_bytes=MM_VMEM_LIMIT),
        )(*rows, *params)

    def call_bwd(rows, params, cts):
        t = rows[0].shape[0]
        d_rows = [a for a, d in zip(rows, diff_rows) if d]
        n_in = n_rows + n_params + n_out

        def body(*refs):
            ins = [r[...] for r in refs[:n_rows + n_params]]
            ct = tuple(r[...].astype(F32) for r in refs[n_rows + n_params:n_in])
            _, vjp = jax.vjp(f, *ins)
            grads = vjp(ct)
            out_refs = refs[n_in:]
            g_rows = [g for g, d in zip(grads[:n_rows], diff_rows) if d]
            for o_ref, g in zip(out_refs[:len(g_rows)], g_rows):
                o_ref[...] = g.astype(o_ref.dtype)
            p_refs = out_refs[len(g_rows):]

            if p_refs:
                @pl.when(pl.program_id(0) == 0)
                def _():
                    for p_ref in p_refs:
                        p_ref[...] = jnp.zeros_like(p_ref)

                for p_ref, g in zip(p_refs, grads[n_rows:]):
                    p_ref[...] += g

        return pl.pallas_call(
            body, name=name + "_bwd", grid=(t // ROW_TILE,),
            in_specs=[_row_spec(a, ROW_TILE) for a in rows] + [_full_spec(p) for p in params]
            + [_row_spec(c, ROW_TILE) for c in cts],
            out_specs=[_row_spec(a, ROW_TILE) for a in d_rows] + [_full_spec(p) for p in params],
            out_shape=[jax.ShapeDtypeStruct(a.shape, dt) for a, dt in zip(d_rows, grad_dtypes)]
            + [jax.ShapeDtypeStruct(p.shape, F32) for p in params],
            compiler_params=pltpu.CompilerParams(dimension_semantics=("arbitrary",),
                                                 vmem_limit_bytes=MM_VMEM_LIMIT),
        )(*rows, *params, *cts)

    @jax.custom_vjp
    def op(*args):
        return tuple(call_fwd(args[:n_rows], args[n_rows:]))

    def fwd(*args):
        return op(*args), args

    def bwd(args, cts):
        rows, params = args[:n_rows], args[n_rows:]
        outs = call_bwd(rows, params, cts)
        it = iter(outs)
        g_rows = [next(it) if d else jnp.zeros_like(a) for a, d in zip(rows, diff_rows)]
        return tuple(g_rows) + tuple(it)

    op.defvjp(fwd, bwd)
    return op


def _rms(x, g, n):
    return x * lax.rsqrt(jnp.sum(x * x, axis=-1, keepdims=True) * (1.0 / n) + EPS) * g


def _f_pre_attn(x, g, scale, shift):
    return _rms(x, g, D_MODEL) * (1.0 + scale) + shift, x


def _f_mla_a(cq, ckv, gq, gkv):
    return _rms(cq, gq, MLA_Q_RANK), _rms(ckv, gkv, MLA_KV_RANK)


@jax.custom_vjp
def _split_lanes(x):
    return tuple(x[:, i * LANES:(i + 1) * LANES] for i in range(x.shape[1] // LANES))


def _split_lanes_fwd(x):
    return _split_lanes(x), None


def _split_lanes_bwd(_, cts):
    return (jnp.concatenate(cts, axis=1),)


_split_lanes.defvjp(_split_lanes_fwd, _split_lanes_bwd)


def _f_mla_b(qall, kn_all, kr, kr_sw, cos, sin, gqn, gqr, gqr_sw, gkn, gkr, gkr_sw):
    q = _split_lanes(qall)
    kn = _split_lanes(kn_all)
    qn_o, qr_o, kn_o = [], [], []
    for h in range(MLA_HEADS):
        qn, qr, qs = q[h], q[MLA_HEADS + h], q[2 * MLA_HEADS + h]
        ss = jnp.sum(qn * qn, axis=-1, keepdims=True) + jnp.sum(qr * qr, axis=-1, keepdims=True)
        rs = lax.rsqrt(ss * (1.0 / MLA_QK) + EPS)
        qn_o.append(qn * rs * gqn)
        qr_o.append((qr * rs * gqr) * cos + (qs * rs * gqr_sw) * sin)
        kn_o.append(_rms(kn[h], gkn, MLA_NOPE))
    rs = lax.rsqrt(jnp.sum(kr * kr, axis=-1, keepdims=True) * (1.0 / MLA_ROPE) + EPS)
    kr_o = (kr * rs * gkr) * cos + (kr_sw * rs * gkr_sw) * sin
    return (jnp.concatenate(qn_o, axis=1), jnp.concatenate(qr_o, axis=1), jnp.concatenate(kn_o, axis=1), kr_o)


def _f_post_attn(o_sb, o_mla, g_sb, g_mla):
    return (jnp.concatenate([_rms(o_sb, g_sb, SB_WIDTH), _rms(o_mla, g_mla, SB_WIDTH)], axis=1),)


def _f_pre_ffn(x, attn, gate, g, scale, shift):
    x2 = x + gate * attn
    return x2, _rms(x2, g, D_MODEL) * (1.0 + scale) + shift


def _f_swiglu(gt, up):
    return (gt / (1.0 + jnp.exp(-gt)) * up,)


def _f_loss(x2, ffn, target, gate):
    err = x2 + gate * ffn - target
    return (jnp.sum(err * err, axis=-1, keepdims=True) * (1.0 / D_MODEL),)


def _hi_lo_dot(x, tri):
    hi = x.astype(BF16)
    lo = (x - hi.astype(F32)).astype(BF16)
    return (jnp.dot(hi, tri, preferred_element_type=F32) + jnp.dot(lo, tri, preferred_element_type=F32))


def _tri(cmp):
    r = lax.broadcasted_iota(jnp.int32, (ATT_BLK, ATT_BLK), 0)
    c = lax.broadcasted_iota(jnp.int32, (ATT_BLK, ATT_BLK), 1)
    return cmp(r, c).astype(BF16)


def _nt(a, b):
    return lax.dot_general(a, b, (((1,), (1,)), ((), ())), preferred_element_type=F32)


def _tn(a, b):
    return lax.dot_general(a, b, (((0,), (0,)), ((), ())), preferred_element_type=F32)


def _sb_logs(z):
    lb = jnp.minimum(z, 0.0) - jnp.log(1.0 + jnp.exp(-jnp.abs(z)))
    return lb, lb - z


def _sb_fwd(q, k, v):
    t = q.shape[0]
    nq = t // ATT_BLK
    scale = SB_HEAD_DIM ** -0.5

    def body(q_ref, k_ref, v_ref, o_ref, tot_ref):
        qi = pl.program_id(1)
        lane = lax.broadcasted_iota(jnp.int32, (ATT_BLK, LANES), 1)
        tri = _tri(lambda r, c: r > c)
        qv = q_ref[...] * scale
        heads = [(lane // SB_HEAD_DIM) == hh for hh in range(2)]
        qms = [jnp.where(mine, qv, 0.0).astype(BF16) for mine in heads]

        def blocks(kbs, carry, diagonal):
            acc = carry[0]
            nb = len(kbs)
            chains = [(b, hh) for b in range(nb) for hh in range(2)]
            offs = [pl.multiple_of(kb * ATT_BLK, ATT_BLK) for kb in kbs]
            kks = [k_ref[pl.ds(off, ATT_BLK), :].astype(BF16) for off in offs]
            v_blks = [v_ref[pl.ds(off, ATT_BLK), :] for off in offs]
            if any(diagonal):
                valid = (lax.broadcasted_iota(jnp.int32, (ATT_BLK, ATT_BLK), 1)
                         < lax.broadcasted_iota(jnp.int32, (ATT_BLK, ATT_BLK), 0))
            zs = {ch: _nt(qms[ch[1]], kks[ch[0]]) for ch in chains}
            vvs = {(b, hh): jnp.where(heads[hh], v_blks[b], 0.0).astype(BF16) for b, hh in chains}
            logs = {ch: _sb_logs(zs[ch]) for ch in chains}
            l1ms = {ch: jnp.where(valid, logs[ch][1], 0.0) if diagonal[ch[0]] else logs[ch][1] for ch in chains}
            run = {(0, hh): carry[1 + hh] for hh in range(2)}
            for b, hh in chains:
                run[(b + 1, hh)] = run[(b, hh)] + jnp.sum(l1ms[(b, hh)], axis=-1, keepdims=True)
            afters = {ch: _hi_lo_dot(l1ms[ch], tri) for ch in chains}
            ws = {ch: jnp.exp(logs[ch][0] + (afters[ch] + run[ch])) for ch in chains}
            ws = {ch: jnp.where(valid, ws[ch], 0.0) if diagonal[ch[0]] else ws[ch] for ch in chains}
            for ch in chains:
                acc = acc + jnp.dot(ws[ch].astype(BF16), vvs[ch], preferred_element_type=F32)
            return (acc, run[(nb, 0)], run[(nb, 1)])

        zero = jnp.zeros((ATT_BLK, 1), F32)
        init = (jnp.zeros((ATT_BLK, LANES), F32), zero, zero)
        carry = lax.cond(qi % 2 == 1, lambda cr: blocks([qi, qi - 1], cr, (True, False)),
                         lambda cr: blocks([qi], cr, (True,)), init)
        top = qi - 1 - qi % 2
        carry = lax.fori_loop(0, qi // 2, lambda pr, cr: blocks([top - 2 * pr, top - 1 - 2 * pr], cr, (False, False)),
                              carry)
        o_ref[...] = carry[0]
        for hh in range(2):
            tot_ref[:, hh * LANES:(hh + 1) * LANES] = jnp.broadcast_to(carry[1 + hh], (ATT_BLK, LANES))

    return pl.pallas_call(
        body, name="sb_attn_fwd", grid=(SB_HEADS // 2, nq),
        in_specs=[pl.BlockSpec((ATT_BLK, LANES), lambda p, i: (i, p)),
                  pl.BlockSpec((t, LANES), lambda p, i: (0, p)),
                  pl.BlockSpec((t, LANES), lambda p, i: (0, p))],
        out_specs=[pl.BlockSpec((ATT_BLK, LANES), lambda p, i: (i, p)),
                   pl.BlockSpec((ATT_BLK, 2 * LANES), lambda p, i: (i, p))],
        out_shape=[jax.ShapeDtypeStruct((t, SB_WIDTH), F32), jax.ShapeDtypeStruct((t, SB_HEADS * LANES), F32)],
        compiler_params=pltpu.CompilerParams(dimension_semantics=("arbitrary", "arbitrary")),
    )(q, k, v)


def _sb_bwd(q, k, v, tot, do):
    t = q.shape[0]
    nq = t // ATT_BLK
    scale = SB_HEAD_DIM ** -0.5

    def body(q_ref, k_ref, v_ref, tot_ref, do_ref, dq_ref, dk_ref, dv_ref):
        qi = pl.program_id(1)

        @pl.when(qi == 0)
        def _():
            dk_ref[...] = jnp.zeros_like(dk_ref)
            dv_ref[...] = jnp.zeros_like(dv_ref)

        lane = lax.broadcasted_iota(jnp.int32, (ATT_BLK, LANES), 1)
        tri_incl = _tri(lambda r, c: r <= c)
        tri_lt = _tri(lambda r, c: r < c)
        qv = q_ref[...] * scale
        dov = do_ref[...]
        heads = [(lane // SB_HEAD_DIM) == hh for hh in range(2)]
        qms = [jnp.where(mine, qv, 0.0).astype(BF16) for mine in heads]
        doms = [jnp.where(mine, dov, 0.0).astype(BF16) for mine in heads]
        tots = [tot_ref[:, hh * LANES:hh * LANES + 1] for hh in range(2)]

        def blocks(kbs, carry, diagonal):
            dq = carry[0]
            nb = len(kbs)
            chains = [(b, hh) for b in range(nb) for hh in range(2)]
            offs = [pl.multiple_of(kb * ATT_BLK, ATT_BLK) for kb in kbs]
            k_blks = [k_ref[pl.ds(off, ATT_BLK), :] for off in offs]
            vvs = [v_ref[pl.ds(off, ATT_BLK), :].astype(BF16) for off in offs]
            if any(diagonal):
                valid = (lax.broadcasted_iota(jnp.int32, (ATT_BLK, ATT_BLK), 1)
                         < lax.broadcasted_iota(jnp.int32, (ATT_BLK, ATT_BLK), 0))
            kks = {(b, hh): jnp.where(heads[hh], k_blks[b], 0.0).astype(BF16) for b, hh in chains}
            zs = {ch: _nt(qms[ch[1]], kks[ch]) for ch in chains}
            dws = {ch: _nt(doms[ch[1]], vvs[ch[0]]) for ch in chains}
            logs = {ch: _sb_logs(zs[ch]) for ch in chains}
            lbs = {ch: logs[ch][0] for ch in chains}
            l1m_all = {ch: logs[ch][1] for ch in chains}
            l1ms = {ch: jnp.where(valid, l1m_all[ch], 0.0) if diagonal[ch[0]] else l1m_all[ch] for ch in chains}
            pre, c_de = {}, {}
            for hh in range(2):
                pre[(0, hh)], c_de[(0, hh)] = carry[1 + 2 * hh], carry[2 + 2 * hh]
            for b, hh in chains:
                pre[(b + 1, hh)] = pre[(b, hh)] + jnp.sum(l1ms[(b, hh)], axis=-1, keepdims=True)
            prefix = {ch: _hi_lo_dot(l1ms[ch], tri_incl) for ch in chains}
            ws = {ch: jnp.exp(lbs[ch] + (tots[ch[1]] - (prefix[ch] + pre[ch]))) for ch in chains}
            ws = {ch: jnp.where(valid, ws[ch], 0.0) if diagonal[ch[0]] else ws[ch] for ch in chains}
            d_es = {ch: ws[ch] * dws[ch] for ch in chains}
            for b, hh in chains:
                c_de[(b + 1, hh)] = c_de[(b, hh)] + jnp.sum(d_es[(b, hh)], axis=-1, keepdims=True)
            dvs = [_tn(ws[(b, 0)].astype(BF16), doms[0]) + _tn(ws[(b, 1)].astype(BF16), doms[1]) for b in range(nb)]
            dl1ms = {ch: jnp.dot(d_es[ch].astype(BF16), tri_lt, preferred_element_type=F32) + c_de[ch] for ch in chains}
            dzs = {ch: d_es[ch] * jnp.exp(l1m_all[ch]) - dl1ms[ch] * jnp.exp(lbs[ch]) for ch in chains}
            dzs = {ch: jnp.where(valid, dzs[ch], 0.0) if diagonal[ch[0]] else dzs[ch] for ch in chains}
            dzs = {ch: dzs[ch].astype(BF16) for ch in chains}
            for ch in chains:
                dq = dq + jnp.dot(dzs[ch], kks[ch], preferred_element_type=F32)
            for b in range(nb):
                dk_ref[pl.ds(offs[b], ATT_BLK), :] += _tn(dzs[(b, 0)], qms[0]) + _tn(dzs[(b, 1)], qms[1])
                dv_ref[pl.ds(offs[b], ATT_BLK), :] += dvs[b]
            return (dq, pre[(nb, 0)], c_de[(nb, 0)], pre[(nb, 1)], c_de[(nb, 1)])

        zero = jnp.zeros((ATT_BLK, 1), F32)
        carry = lax.fori_loop(0, qi // 2, lambda pr, cr: blocks([2 * pr, 2 * pr + 1], cr, (False, False)),
                              (jnp.zeros((ATT_BLK, LANES), F32), zero, zero, zero, zero))
        carry = lax.cond(qi % 2 == 1, lambda cr: blocks([qi - 1, qi], cr, (False, True)),
                         lambda cr: blocks([qi], cr, (True,)), carry)
        dq_ref[...] = carry[0] * scale

    return pl.pallas_call(
        body, name="sb_attn_bwd", grid=(SB_HEADS // 2, nq),
        in_specs=[pl.BlockSpec((ATT_BLK, LANES), lambda p, i: (i, p)),
                  pl.BlockSpec((t, LANES), lambda p, i: (0, p)),
                  pl.BlockSpec((t, LANES), lambda p, i: (0, p)),
                  pl.BlockSpec((ATT_BLK, 2 * LANES), lambda p, i: (i, p)),
                  pl.BlockSpec((ATT_BLK, LANES), lambda p, i: (i, p))],
        out_specs=[pl.BlockSpec((ATT_BLK, LANES), lambda p, i: (i, p)),
                   pl.BlockSpec((t, LANES), lambda p, i: (0, p)),
                   pl.BlockSpec((t, LANES), lambda p, i: (0, p))],
        out_shape=[jax.ShapeDtypeStruct((t, SB_WIDTH), F32)] * 3,
        compiler_params=pltpu.CompilerParams(dimension_semantics=("arbitrary", "arbitrary")),
    )(q, k, v, tot, do)


@jax.custom_vjp
def _sb_attention(q, k, v):
    return _sb_fwd(q, k, v)[0]


def _sb_attention_fwd(q, k, v):
    o, tot = _sb_fwd(q, k, v)
    return o, (q, k, v, tot)


def _sb_attention_bwd(res, do):
    return tuple(_sb_bwd(*res, do))


_sb_attention.defvjp(_sb_attention_fwd, _sb_attention_bwd)


def _mla_fwd(qn, qr, kn, kr, v):
    t = qn.shape[0]
    nq = t // ATT_BLK
    scale = MLA_QK ** -0.5

    def body(qn_ref, qr_ref, kn_ref, kr_ref, v_ref, o_ref, lse_ref):
        qi = pl.program_id(1)
        lanes = [slice(hh * LANES, (hh + 1) * LANES) for hh in range(2)]
        qnb = [qn_ref[:, sl].astype(BF16) for sl in lanes]
        qrb = [qr_ref[:, sl].astype(BF16) for sl in lanes]

        def blocks(kbs, carry, diagonal):
            nb = len(kbs)
            chains = [(b, hh) for b in range(nb) for hh in range(2)]
            offs = [pl.multiple_of(kb * ATT_BLK, ATT_BLK) for kb in kbs]
            krbs = [kr_ref[pl.ds(off, ATT_BLK), :].astype(BF16) for off in offs]
            accs, ms, ls = [carry[0], carry[3]], [carry[1], carry[4]], [carry[2], carry[5]]
            ss = {(b, hh): (_nt(qnb[hh], kn_ref[pl.ds(offs[b], ATT_BLK), lanes[hh]].astype(BF16))
                            + _nt(qrb[hh], krbs[b])) * scale for b, hh in chains}
            if any(diagonal):
                causal = (lax.broadcasted_iota(jnp.int32, (ATT_BLK, ATT_BLK), 1)
                          <= lax.broadcasted_iota(jnp.int32, (ATT_BLK, ATT_BLK), 0))
                ss = {ch: jnp.where(causal, ss[ch], -jnp.inf) if diagonal[ch[0]] else ss[ch] for ch in chains}
            m_new = list(ms)
            for b, hh in chains:
                m_new[hh] = jnp.maximum(m_new[hh], jnp.max(ss[(b, hh)], axis=-1, keepdims=True))
            ps = {(b, hh): jnp.exp(ss[(b, hh)] - m_new[hh]) for b, hh in chains}
            alphas = [jnp.exp(ms[hh] - m_new[hh]) for hh in range(2)]
            pvs = {(b, hh): jnp.dot(ps[(b, hh)].astype(BF16), v_ref[pl.ds(offs[b], ATT_BLK), lanes[hh]].astype(BF16),
                                    preferred_element_type=F32) for b, hh in chains}
            out = []
            for hh in range(2):
                acc, l = accs[hh] * alphas[hh], ls[hh] * alphas[hh]
                for b in range(nb):
                    acc, l = acc + pvs[(b, hh)], l + jnp.sum(ps[(b, hh)], axis=-1, keepdims=True)
                out += [acc, m_new[hh], l]
            return tuple(out)

        init = (jnp.zeros((ATT_BLK, LANES), F32), jnp.full((ATT_BLK, 1), -jnp.inf, F32), jnp.zeros((ATT_BLK, 1), F32))
        carry = lax.cond(qi % 2 == 1, lambda cr: blocks([qi, qi - 1], cr, (True, False)),
                         lambda cr: blocks([qi], cr, (True,)), init + init)
        carry = lax.fori_loop(0, qi // 2, lambda pr, cr: blocks([2 * pr, 2 * pr + 1], cr, (False, False)), carry)
        for hh in range(2):
            acc, m, l = carry[3 * hh:3 * hh + 3]
            o_ref[:, lanes[hh]] = acc / l
            lse_ref[:, lanes[hh]] = jnp.broadcast_to(m + jnp.log(l), (ATT_BLK, LANES))

    blk = pl.BlockSpec((ATT_BLK, 2 * LANES), lambda p, i: (i, p))
    full = pl.BlockSpec((t, 2 * LANES), lambda p, i: (0, p))
    return pl.pallas_call(
        body, name="mla_attn_fwd", grid=(MLA_HEADS // 2, nq),
        in_specs=[blk, blk, full, pl.BlockSpec((t, LANES), lambda p, i: (0, 0)), full],
        out_specs=[blk, blk],
        out_shape=[jax.ShapeDtypeStruct((t, MLA_HEADS * LANES), F32)] * 2,
        compiler_params=pltpu.CompilerParams(dimension_semantics=("arbitrary", "arbitrary")),
    )(qn, qr, kn, kr, v)


def _mla_bwd(qn, qr, kn, kr, v, o, lse, do):
    t = qn.shape[0]
    nq = t // ATT_BLK
    scale = MLA_QK ** -0.5

    def body(qn_ref, qr_ref, kn_ref, kr_ref, v_ref, o_ref, lse_ref, do_ref,
             dqn_ref, dqr_ref, dkn_ref, dkr_ref, dv_ref):
        pair = pl.program_id(0)
        qi = pl.program_id(1)

        @pl.when(qi == 0)
        def _():
            dkn_ref[...] = jnp.zeros_like(dkn_ref)
            dv_ref[...] = jnp.zeros_like(dv_ref)

        @pl.when((qi == 0) & (pair == 0))
        def _():
            dkr_ref[...] = jnp.zeros_like(dkr_ref)

        lanes = [slice(hh * LANES, (hh + 1) * LANES) for hh in range(2)]
        qnb = [qn_ref[:, sl].astype(BF16) for sl in lanes]
        qrb = [qr_ref[:, sl].astype(BF16) for sl in lanes]
        dob = [do_ref[:, sl].astype(BF16) for sl in lanes]
        delta = [jnp.sum(do_ref[:, sl] * o_ref[:, sl], axis=-1, keepdims=True) for sl in lanes]
        lse_v = [lse_ref[:, hh * LANES:hh * LANES + 1] for hh in range(2)]

        def blocks(kbs, carry, diagonal):
            nb = len(kbs)
            chains = [(b, hh) for b in range(nb) for hh in range(2)]
            offs = [pl.multiple_of(kb * ATT_BLK, ATT_BLK) for kb in kbs]
            krbs = [kr_ref[pl.ds(off, ATT_BLK), :].astype(BF16) for off in offs]
            knb = {(b, hh): kn_ref[pl.ds(offs[b], ATT_BLK), lanes[hh]].astype(BF16) for b, hh in chains}
            vb = {(b, hh): v_ref[pl.ds(offs[b], ATT_BLK), lanes[hh]].astype(BF16) for b, hh in chains}
            ss = {(b, hh): _nt(qnb[hh], knb[(b, hh)]) + _nt(qrb[hh], krbs[b]) for b, hh in chains}
            dps = {(b, hh): _nt(dob[hh], vb[(b, hh)]) for b, hh in chains}
            ps = {(b, hh): jnp.exp(ss[(b, hh)] * scale - lse_v[hh]) for b, hh in chains}
            if any(diagonal):
                causal = (lax.broadcasted_iota(jnp.int32, (ATT_BLK, ATT_BLK), 1)
                          <= lax.broadcasted_iota(jnp.int32, (ATT_BLK, ATT_BLK), 0))
                ps = {ch: jnp.where(causal, ps[ch], 0.0) if diagonal[ch[0]] else ps[ch] for ch in chains}
            dss = {(b, hh): (ps[(b, hh)] * (dps[(b, hh)] - delta[hh]) * scale).astype(BF16) for b, hh in chains}
            for b, hh in chains:
                dv_ref[pl.ds(offs[b], ATT_BLK), lanes[hh]] += _tn(ps[(b, hh)].astype(BF16), dob[hh])
            for b, hh in chains:
                dkn_ref[pl.ds(offs[b], ATT_BLK), lanes[hh]] += _tn(dss[(b, hh)], qnb[hh])
            for b in range(nb):
                dkr_ref[pl.ds(offs[b], ATT_BLK), :] += _tn(dss[(b, 0)], qrb[0]) + _tn(dss[(b, 1)], qrb[1])
            out = list(carry)
            for b, hh in chains:
                out[2 * hh] = out[2 * hh] + jnp.dot(dss[(b, hh)], knb[(b, hh)], preferred_element_type=F32)
                out[2 * hh + 1] = out[2 * hh + 1] + jnp.dot(dss[(b, hh)], krbs[b], preferred_element_type=F32)
            return tuple(out)

        zero = jnp.zeros((ATT_BLK, LANES), F32)
        carry = lax.fori_loop(0, qi // 2, lambda pr, cr: blocks([2 * pr, 2 * pr + 1], cr, (False, False)),
                              (zero, zero, zero, zero))
        carry = lax.cond(qi % 2 == 1, lambda cr: blocks([qi - 1, qi], cr, (False, True)),
                         lambda cr: blocks([qi], cr, (True,)), carry)
        for hh in range(2):
            dqn_ref[:, lanes[hh]] = carry[2 * hh]
            dqr_ref[:, lanes[hh]] = carry[2 * hh + 1]

    blk = pl.BlockSpec((ATT_BLK, 2 * LANES), lambda p, i: (i, p))
    full = pl.BlockSpec((t, 2 * LANES), lambda p, i: (0, p))
    shared = pl.BlockSpec((t, LANES), lambda p, i: (0, 0))
    wide = jax.ShapeDtypeStruct((t, MLA_HEADS * LANES), F32)
    return pl.pallas_call(
        body, name="mla_attn_bwd", grid=(MLA_HEADS // 2, nq),
        in_specs=[blk, blk, full, shared, full, blk, blk, blk],
        out_specs=[blk, blk, full, shared, full],
        out_shape=[wide, wide, wide, jax.ShapeDtypeStruct((t, LANES), F32), wide],
        compiler_params=pltpu.CompilerParams(dimension_semantics=("arbitrary", "arbitrary")),
    )(qn, qr, kn, kr, v, o, lse, do)


@jax.custom_vjp
def _mla_attention(qn, qr, kn, kr, v):
    return _mla_fwd(qn, qr, kn, kr, v)[0]


def _mla_attention_fwd(qn, qr, kn, kr, v):
    o, lse = _mla_fwd(qn, qr, kn, kr, v)
    return o, (qn, qr, kn, kr, v, o, lse)


def _mla_attention_bwd(res, do):
    return tuple(_mla_bwd(*res, do))


_mla_attention.defvjp(_mla_attention_fwd, _mla_attention_bwd)


def _ffn_in(h, wg, wu):
    t, k = h.shape
    n_sh, cc, _ = wg.shape

    def body(h_ref, wg_ref, wu_ref, g_ref, u_ref, a_ref):
        hb = h_ref[...].astype(BF16)
        for j in range(n_sh):
            cols = slice(j * cc, (j + 1) * cc)
            g = _nt(hb, wg_ref[j])
            u = _nt(hb, wu_ref[j])
            g_ref[:, cols] = g.astype(BF16)
            u_ref[:, cols] = u.astype(BF16)
            a_ref[:, cols] = _f_swiglu(g, u)[0].astype(BF16)

    w_spec = pl.BlockSpec((n_sh, cc, k), lambda i: (0, 0, 0))
    o_spec = pl.BlockSpec((MM_ROW_TILE, n_sh * cc), lambda i: (i, 0))
    wide = jax.ShapeDtypeStruct((t, n_sh * cc), BF16)
    return pl.pallas_call(
        body, name="ffn_in_fwd", grid=(t // MM_ROW_TILE,),
        in_specs=[pl.BlockSpec((MM_ROW_TILE, k), lambda i: (i, 0)), w_spec, w_spec],
        out_specs=[o_spec, o_spec, o_spec],
        out_shape=[wide, wide, wide],
        compiler_params=pltpu.CompilerParams(dimension_semantics=("arbitrary",), vmem_limit_bytes=MM_VMEM_LIMIT),
    )(h, wg, wu)


def _ffn_mid_bwd(dy, wd, g, u):
    t, n = dy.shape
    n_sh, cc, _ = wd.shape

    def body(dy_ref, wd_ref, g_ref, u_ref, dg_ref, du_ref):
        d_act = _nt(dy_ref[...].astype(BF16), wd_ref[...])
        g = g_ref[...].astype(F32)
        sig = 1.0 / (1.0 + jnp.exp(-g))
        dg_ref[...] = (d_act * u_ref[...].astype(F32) * (sig * (1.0 + g * (1.0 - sig)))).astype(BF16)
        du_ref[...] = (d_act * (g * sig)).astype(BF16)

    blk = pl.BlockSpec((MM_ROW_TILE, cc), lambda j, i: (i, j))
    wide = jax.ShapeDtypeStruct((t, n_sh * cc), BF16)
    return pl.pallas_call(
        body, name="ffn_mid_bwd", grid=(n_sh, t // MM_ROW_TILE),
        in_specs=[pl.BlockSpec((MM_ROW_TILE, n), lambda j, i: (i, 0)),
                  pl.BlockSpec((None, cc, n), lambda j, i: (j, 0, 0)), blk, blk],
        out_specs=[blk, blk], out_shape=[wide, wide],
        compiler_params=pltpu.CompilerParams(dimension_semantics=("arbitrary", "arbitrary"),
                                             vmem_limit_bytes=MM_VMEM_LIMIT),
    )(dy, wd, g, u)


def _ffn_dh(dg, du, wg, wu):
    t = dg.shape[0]
    n_sh, cc, k = wg.shape

    def body(dg_ref, du_ref, wg_ref, wu_ref, o_ref):
        acc = jnp.zeros((MM_ROW_TILE, k), F32)
        for j in range(n_sh):
            cols = slice(j * cc, (j + 1) * cc)
            acc = (acc + jnp.dot(dg_ref[:, cols], wg_ref[j], preferred_element_type=F32)
                   + jnp.dot(du_ref[:, cols], wu_ref[j], preferred_element_type=F32))
        o_ref[...] = acc

    blk = pl.BlockSpec((MM_ROW_TILE, n_sh * cc), lambda i: (i, 0))
    w_spec = pl.BlockSpec((n_sh, cc, k), lambda i: (0, 0, 0))
    return pl.pallas_call(
        body, name="ffn_dh", grid=(t // MM_ROW_TILE,),
        in_specs=[blk, blk, w_spec, w_spec],
        out_specs=pl.BlockSpec((MM_ROW_TILE, k), lambda i: (i, 0)),
        out_shape=jax.ShapeDtypeStruct((t, k), F32),
        compiler_params=pltpu.CompilerParams(dimension_semantics=("arbitrary",), vmem_limit_bytes=MM_VMEM_LIMIT),
    )(dg, du, wg, wu)


def _ffn_dw_in(h, dg, du, n_sh):
    t, k = h.shape
    cc = dg.shape[1] // n_sh
    tk = 512

    def body(h_ref, dg_ref, du_ref, og_ref, ou_ref):
        hb = h_ref[...].astype(BF16)
        og_ref[...] = _tn(dg_ref[...], hb).astype(BF16)
        ou_ref[...] = _tn(du_ref[...], hb).astype(BF16)

    d_spec = pl.BlockSpec((t, cc), lambda i, j: (0, j))
    o_spec = pl.BlockSpec((None, cc, tk), lambda i, j: (j, 0, i))
    out = jax.ShapeDtypeStruct((n_sh, cc, k), BF16)
    return pl.pallas_call(
        body, name="ffn_gate_up_dw", grid=(k // tk, n_sh),
        in_specs=[pl.BlockSpec((t, tk), lambda i, j: (0, i)), d_spec, d_spec],
        out_specs=[o_spec, o_spec], out_shape=[out, out],
        compiler_params=pltpu.CompilerParams(dimension_semantics=("arbitrary", "arbitrary"),
                                             vmem_limit_bytes=MM_VMEM_LIMIT),
    )(h, dg, du)


@jax.custom_vjp
def _ffn_block(h, wg, wu, wd):
    act = _ffn_in(h, wg, wu)[2]
    return _mm(act, wd.reshape(-1, wd.shape[2]), "nn", "ffn_down_fwd", MM_ROW_TILE, wd.shape[2])


def _ffn_block_fwd(h, wg, wu, wd):
    g, u, act = _ffn_in(h, wg, wu)
    y = _mm(act, wd.reshape(-1, wd.shape[2]), "nn", "ffn_down_fwd", MM_ROW_TILE, wd.shape[2])
    return y, (h, wg, wu, wd, g, u, act)


def _ffn_block_bwd(res, dy):
    h, wg, wu, wd, g, u, act = res
    dg, du = _ffn_mid_bwd(dy, wd, g, u)
    dh = _ffn_dh(dg, du, wg, wu)
    n_sh = wg.shape[0]
    dwg, dwu = _ffn_dw_in(h, dg, du, n_sh)
    dwd = _mm(act, dy, "tn", "ffn_down_dw", 256, wd.shape[2], out_dtype=BF16).reshape(wd.shape)
    return dh, dwg, dwu, dwd


_ffn_block.defvjp(_ffn_block_fwd, _ffn_block_bwd)


def _swap_halves(w):
    half = w.shape[-1] // 2
    return jnp.concatenate([w[..., half:], w[..., :half]], axis=-1)


def _pad_lanes(w):
    return jnp.concatenate([w, jnp.zeros(w.shape[:-1] + (LANES - w.shape[-1],), w.dtype)], axis=-1)


def _join_cols(shards):
    return shards.transpose(1, 0, 2).reshape(shards.shape[1], -1)


def _mod_parts(mod):
    return [mod[:, i * D_MODEL:(i + 1) * D_MODEL] for i in range(N_MOD)]


def _mixing_stage(x, mod, p, cos, sin):
    shift1, scale1 = _mod_parts(mod)[:2]

    w_in_t = p["w_in"].reshape(-1, D_MODEL)
    k_rope_rows = w_in_t[2176:2240]

    def pad_rows(a):
        return jnp.concatenate([a, jnp.zeros((LANES - a.shape[0], D_MODEL), a.dtype)], axis=0)

    swapped = jnp.concatenate([k_rope_rows[MLA_ROPE // 2:], k_rope_rows[:MLA_ROPE // 2]], axis=0)
    w_in_ext = jnp.concatenate([w_in_t[:2176], pad_rows(k_rope_rows), pad_rows(swapped),
                                jnp.zeros((LANES, D_MODEL), w_in_t.dtype)], axis=0)
    h1, x_res = _make_rowwise("pre_attn", _f_pre_attn, 1, 3, [D_MODEL, D_MODEL], [True], out_dtypes=[BF16, F32])(
        x, p["norm_attn"], scale1, shift1)
    q_sb, k_sb, v_sb, cq, ckv, kr, kr_sw = _make_linear_split_t(
        "in_proj", (SB_WIDTH, SB_WIDTH, SB_WIDTH, MLA_Q_RANK, MLA_KV_RANK, LANES, LANES), 512)(h1, w_in_ext)

    o_sb = _sb_attention(q_sb, k_sb, v_sb)

    wq = _join_cols(p["w_q_up"]).reshape(MLA_Q_RANK, MLA_HEADS, MLA_QK)
    wq_n, wq_r = wq[:, :, :MLA_NOPE], wq[:, :, MLA_NOPE:]
    w_q_ext = jnp.concatenate([wq_n.reshape(MLA_Q_RANK, -1), _pad_lanes(wq_r).reshape(MLA_Q_RANK, -1),
                               _pad_lanes(_swap_halves(wq_r)).reshape(MLA_Q_RANK, -1)], axis=1)
    wkv = _join_cols(p["w_kv_up"]).reshape(MLA_KV_RANK, MLA_HEADS, MLA_NOPE + MLA_V)
    w_kv_ext = jnp.concatenate([wkv[:, :, :MLA_NOPE].reshape(MLA_KV_RANK, -1),
                                wkv[:, :, MLA_NOPE:].reshape(MLA_KV_RANK, -1)], axis=1)
    cqn, ckvn = _make_rowwise("mla_a", _f_mla_a, 2, 2, [MLA_Q_RANK, MLA_KV_RANK], [True, True],
                              out_dtypes=[BF16, BF16], grad_dtypes=[BF16, BF16])(
        cq, ckv, p["q_a_norm"], p["kv_a_norm"])
    qall = _make_linear("q_up", 384, 768)(cqn, w_q_ext)
    kn_all, v_mla = _make_linear_split("kv_up", (MLA_HEADS * MLA_NOPE, MLA_HEADS * MLA_V), MLA_KV_RANK)(ckvn, w_kv_ext)
    gq = p["q_norm"]
    gkr = p["k_rope_norm"]
    qn, qr, kn, krr = _make_rowwise("mla_b", _f_mla_b, 6, 6, [512, 512, 512, LANES],
                                    [True, True, True, True, False, False],
                                    out_dtypes=[BF16] * 4, grad_dtypes=[BF16] * 4)(
        qall, kn_all, kr, kr_sw, cos, sin,
        gq[:, :MLA_NOPE], _pad_lanes(gq[:, MLA_NOPE:]), _pad_lanes(_swap_halves(gq[:, MLA_NOPE:])),
        p["k_nope_norm"], _pad_lanes(gkr), _pad_lanes(_swap_halves(gkr)))
    o_mla = _mla_attention(qn, qr, kn, krr, v_mla)

    (mixed,) = _make_rowwise("post_attn", _f_post_attn, 2, 2, [D_MODEL], [True, True])(
        o_sb, o_mla, p["out_norm_sb"], p["out_norm_mla"])
    return mixed, x_res


def _ffn_stage(x, mixed, mod, p):
    _, _, gate1, shift2, scale2, _ = _mod_parts(mod)
    attn = _make_linear("out_proj", 512, 512)(mixed, p["w_out"].reshape(D_MODEL, D_MODEL))

    x2, h2 = _make_rowwise("pre_ffn", _f_pre_ffn, 2, 4, [D_MODEL, D_MODEL], [True, True],
                           out_dtypes=[F32, BF16], grad_dtypes=[F32, BF16])(
        x, attn, gate1, p["norm_ffn"], scale2, shift2)
    return x2, _ffn_block(h2, p["w_gate"], p["w_up"], p["w_down"])


def _my_place():
    return lax.axis_index("x"), lax.axis_index("y"), lax.axis_index("c")


def _small_gather(x_ref, out_ref, send_sems, recv_sems, base, local_sem):
    m_per = x_ref.shape[0]
    x, y, c = _my_place()
    me, sibling = (x, y, c), (x, y, 1 - c)
    chips = [(1 - x, y), (x, 1 - y), (1 - x, 1 - y)]

    def rows(px, py, pc):
        return out_ref.at[pl.ds((4 * px + 2 * py + pc) * m_per, m_per), :]

    def copy(k, blk, to, src=None):
        return _remote(rows(*blk) if src is None else src, rows(*blk), send_sems, recv_sems, base + k, to)

    mine = pltpu.make_async_copy(x_ref, rows(*me), local_sem)
    first = [copy(0, me, sibling, src=x_ref)] + [copy(1 + j, me, (*chip, c), src=x_ref) for j, chip in enumerate(chips)]
    passed = [copy(4 + j, (*chip, c), sibling) for j, chip in enumerate(chips)]

    def start():
        mine.start()
        for cp in first:
            cp.start()

    def finish():
        for j, chip in enumerate(chips):
            copy(1 + j, (*chip, c), me).wait_recv()
            passed[j].start()
        copy(0, sibling, me).wait_recv()
        for j, chip in enumerate(chips):
            copy(4 + j, (*chip, 1 - c), me).wait_recv()
        for cp in first + passed:
            cp.wait_send()
        mine.wait()

    return start, finish


EARLY =("w_in", "w_q_up", "w_kv_up")
LATE = ("w_out", "w_gate", "w_up", "w_down")
BIG = EARLY + LATE
TRAVELS_TRANSPOSED = ("w_in", "w_gate", "w_up")
HALF_AXIS = {"w_in": 1, "w_q_up": 0, "w_kv_up": 0, "w_out": 0, "w_gate": 1, "w_up": 1, "w_down": 1}


def _half(ref, h, axis, lead=()):
    trail = ref.shape[len(lead):]
    idx = list(lead) + [slice(None)] * len(trail)
    at = len(trail) - 2 + axis
    n2 = trail[at] // 2
    idx[len(lead) + at] = pl.ds(h * n2, n2)
    return ref.at[tuple(idx)]


def _half_shape(shape, axis):
    shape = list(shape)
    shape[len(shape) - 2 + axis] //= 2
    return tuple(shape)


def _remote(src, dst, send_sems, recv_sems, k, to):
    return pltpu.make_async_remote_copy(src_ref=src, dst_ref=dst, send_sem=send_sems.at[k],
                                        recv_sem=recv_sems.at[k], device_id=to, device_id_type=MESH)


def _gather_weights(names, shards, lands, small_block, w_proj, b_proj, pos_col, freqs, sign):
    n_w = len(shards)
    axes = [HALF_AXIS[n] for n in names]
    n_vec, n_proj = w_proj.shape
    base_small, base_proj = 6 * n_w, 6 * n_w + 7

    def body(*refs):
        w_refs, small_ref, wp_ref, b_ref = refs[:n_w], refs[2 * n_w], refs[2 * n_w + 1], refs[2 * n_w + 2]
        pos_ref, freqs_ref, sign_ref = refs[2 * n_w + 3:2 * n_w + 6]
        o = 2 * n_w + 6
        out_refs, token, small_out, proj_out = refs[o:o + n_w], refs[o + n_w], refs[o + n_w + 1], refs[o + n_w + 2]
        cos_ref, sin_ref = refs[o + n_w + 3:o + n_w + 5]
        send_sems, recv_sems, local_sems, wp_vmem, vec_vmem, mine_vmem = refs[o + n_w + 5:]
        token[...] = jnp.zeros_like(token)
        x, y, c = _my_place()
        sibling = (x, y, 1 - c)
        chips = [(1 - x, y), (x, 1 - y), (1 - x, 1 - y)]
        me = 2 * x + y
        small_start, small_finish = _small_gather(small_ref, small_out, send_sems, recv_sems, base_small,
                                                  local_sems.at[0])
        small_start()
        first = [_remote(_half(w_refs[i], c, axes[i]), _half(out_refs[i], c, axes[i], (me,)),
                         send_sems, recv_sems, 6 * i + j, (*chip, c))
                 for i in range(n_w) for j, chip in enumerate(chips)]
        for cp in first:
            cp.start()
        load_w = pltpu.make_async_copy(wp_ref, wp_vmem, local_sems.at[2])
        load_w.start()
        small_finish()
        load_vec = pltpu.make_async_copy(small_out, vec_vmem, local_sems.at[3])
        load_vec.start()
        load_vec.wait()
        load_w.wait()
        proj = jnp.dot(_silu(vec_vmem[...]), wp_vmem[...], precision=lax.Precision.HIGHEST,
                       preferred_element_type=F32) + b_ref[...]
        for d in range(N_DEV):
            mine_vmem[d:d + 1, :] = proj[8 * d:8 * d + 1, :]
        proj_start, proj_finish = _small_gather(mine_vmem, proj_out, send_sems, recv_sems, base_proj, local_sems.at[1])
        proj_start()
        for r in range(0, pos_ref.shape[0], ROW_TILE):
            rows = pl.ds(r, ROW_TILE)
            ang = pos_ref[rows, :].astype(F32) * freqs_ref[...]
            cos_ref[rows, :] = jnp.cos(ang) * jnp.abs(sign_ref[...])
            sin_ref[rows, :] = jnp.sin(ang) * sign_ref[...]
        passed = []
        for j, (cx, cy) in enumerate(chips):
            for i in range(n_w):
                blk = _half(out_refs[i], c, axes[i], (2 * cx + cy,))
                _remote(blk, blk, send_sems, recv_sems, 6 * i + j, (cx, cy, c)).wait_recv()
                cp = _remote(blk, blk, send_sems, recv_sems, 6 * i + 3 + j, sibling)
                cp.start()
                passed.append(cp)
        proj_finish()
        for j, (cx, cy) in enumerate(chips):
            for i in range(n_w):
                blk = _half(out_refs[i], 1 - c, axes[i], (2 * cx + cy,))
                _remote(blk, blk, send_sems, recv_sems, 6 * i + 3 + j, sibling).wait_recv()
        for cp in first + passed:
            cp.wait_send()

    n_rows = N_DEV * small_block.shape[0]
    outs = pl.pallas_call(
        body, name="gather_weights",
        out_shape=[jax.ShapeDtypeStruct(a.shape, a.dtype) for a in lands]
        + [jax.ShapeDtypeStruct((8, LANES), F32), jax.ShapeDtypeStruct((n_rows, n_vec), small_block.dtype),
           jax.ShapeDtypeStruct((N_DEV * N_DEV, n_proj), F32)]
        + [jax.ShapeDtypeStruct((pos_col.shape[0], LANES), F32)] * 2,
        in_specs=[ANY] * (2 * n_w + 2) + [pl.BlockSpec(memory_space=pltpu.VMEM)] * 4,
        out_specs=[ANY] * n_w + [pl.BlockSpec(memory_space=pltpu.VMEM), ANY, ANY] + [pl.BlockSpec(memory_space=pltpu.VMEM)] * 2,
        input_output_aliases={n_w + i: i for i in range(n_w)},
        scratch_shapes=[pltpu.SemaphoreType.DMA((6 * n_w + 14,)), pltpu.SemaphoreType.DMA((6 * n_w + 14,)),
                        pltpu.SemaphoreType.DMA((4,)), pltpu.VMEM((n_vec, n_proj), F32), pltpu.VMEM((n_rows, n_vec), F32),
                        pltpu.VMEM((N_DEV, n_proj), F32)],
        compiler_params=pltpu.CompilerParams(vmem_limit_bytes=MM_VMEM_LIMIT),
    )(*shards, *lands, small_block, w_proj, b_proj, pos_col, freqs, sign)
    return outs[:n_w], outs[n_w], outs[n_w + 1], outs[n_w + 2], outs[n_w + 3], outs[n_w + 4]


def _pair_exchange(names, grads, call_name, small_block):
    n_w = len(grads)
    axes = [HALF_AXIS[n] for n in names]

    def body(*refs):
        g_refs, small_ref = refs[:n_w], refs[n_w]
        t_refs, small_out = refs[n_w + 1:2 * n_w + 1], refs[2 * n_w + 1]
        send_sems, recv_sems, local_sem = refs[2 * n_w + 2:]
        x, y, c = _my_place()
        small_start, small_finish = _small_gather(small_ref, small_out, send_sems, recv_sems, n_w, local_sem)
        small_start()
        sends = [_remote(_half(g_refs[i], 1 - c, axes[i]), t_refs[i], send_sems, recv_sems, i, (x, y, 1 - c))
                 for i in range(n_w)]
        for cp in sends:
            cp.start()
        small_finish()
        for cp in sends:
            cp.wait_recv()
        for cp in sends:
            cp.wait_send()

    outs = pl.pallas_call(
        body, name=call_name,
        out_shape=[jax.ShapeDtypeStruct(_half_shape(g.shape, a), g.dtype) for g, a in zip(grads, axes)]
        + [jax.ShapeDtypeStruct((N_DEV * small_block.shape[0], small_block.shape[1]), small_block.dtype)],
        in_specs=[ANY] * (n_w + 1), out_specs=[ANY] * (n_w + 1),
        scratch_shapes=[pltpu.SemaphoreType.DMA((n_w + 7,)), pltpu.SemaphoreType.DMA((n_w + 7,)),
                        pltpu.SemaphoreType.DMA],
    )(*grads, small_block)
    return outs[:n_w], outs[n_w]


def _sibling_join(halves, name, after):
    n_w = len(halves)

    def body(*refs):
        s_refs, j_refs = refs[:n_w], refs[n_w + 1:2 * n_w + 1]
        send_sems, recv_sems = refs[2 * n_w + 1:]
        x, y, c = _my_place()
        sends = [_remote(s_refs[i], j_refs[i], send_sems, recv_sems, i, (x, y, 1 - c)) for i in range(n_w)]
        for cp in sends:
            cp.start()
        for cp in sends:
            cp.wait_recv()
        for cp in sends:
            cp.wait_send()

    return pl.pallas_call(
        body, name=name,
        out_shape=[jax.ShapeDtypeStruct(s.shape, s.dtype) for s in halves],
        in_specs=[ANY] * (n_w + 1), out_specs=[ANY] * n_w,
        scratch_shapes=[pltpu.SemaphoreType.DMA((n_w,)), pltpu.SemaphoreType.DMA((n_w,))],
    )(*halves, after)


HBM_SPEC = pl.BlockSpec(memory_space=pltpu.HBM)
SEM_SPEC = pl.BlockSpec(memory_space=pltpu.SEMAPHORE)
DATAFLOW = pltpu.SideEffectType.DATAFLOW_SIDE_EFFECTING


def _in_hbm(a):
    return pltpu.with_memory_space_constraint(a, pltpu.HBM)


def _exchange_start(name, srcs, lands, plan, n_copies, after, thru):
    n = len(srcs)

    def body(*refs):
        src_refs, land_refs = refs[:n], refs[n:2 * n]
        send_sems, recv_sems = refs[2 * n + 2], refs[2 * n + 3]
        for k, (src, dst, to, k_recv) in enumerate(plan(src_refs, land_refs)):
            pltpu.make_async_remote_copy(src_ref=src, dst_ref=dst, send_sem=send_sems.at[k],
                                         recv_sem=recv_sems.at[k_recv], device_id=to, device_id_type=MESH).start()

    outs = pl.pallas_call(
        body, name=name,
        out_shape=(pltpu.SemaphoreType.DMA((n_copies,)), pltpu.SemaphoreType.DMA((n_copies,)),
                   *[pltpu.HBM(a.shape, a.dtype) for a in list(srcs) + list(lands) + [thru]]),
        in_specs=[HBM_SPEC] * (2 * n + 1) + [ANY],
        out_specs=(SEM_SPEC, SEM_SPEC, *[HBM_SPEC] * (2 * n + 1)),
        input_output_aliases={i: 2 + i for i in range(2 * n + 1)},
        compiler_params=pltpu.CompilerParams(has_side_effects=DATAFLOW),
    )(*[_in_hbm(a) for a in list(srcs) + list(lands) + [thru]], after)
    return outs[0], outs[1], outs[2:2 + n], outs[2 + n:2 + 2 * n], outs[2 + 2 * n]


def _exchange_wait(name, started, plan, after):
    send_sems, recv_sems, srcs, lands, _ = started
    n = len(srcs)

    def body(*refs):
        src_refs, land_refs = refs[:n], refs[n:2 * n]
        s_sems, r_sems = refs[2 * n], refs[2 * n + 1]
        for k, (src, dst, to, _) in enumerate(plan(src_refs, land_refs)):
            cp = _remote(src, dst, s_sems, r_sems, k, to)
            cp.wait_send()
            cp.wait_recv()

    outs = pl.pallas_call(
        body, name=name,
        out_shape=tuple(pltpu.HBM(a.shape, a.dtype) for a in list(srcs) + list(lands)),
        in_specs=[HBM_SPEC] * (2 * n) + [SEM_SPEC, SEM_SPEC, ANY],
        out_specs=tuple([HBM_SPEC] * (2 * n)),
        input_output_aliases={i: i for i in range(2 * n)},
        compiler_params=pltpu.CompilerParams(has_side_effects=DATAFLOW),
    )(*srcs, *lands, send_sems, recv_sems, after)
    return outs[:n], outs[n:]


def _late_gather_plan(src_refs, land_refs):
    x, y, c = _my_place()
    chips = [(1 - x, y), (x, 1 - y), (1 - x, 1 - y)]
    plan = [(src, land.at[2 * x + y], (cx, cy, c)) for src, land in zip(src_refs, land_refs) for cx, cy in chips]
    return [entry + (k,) for k, entry in enumerate(plan)]


def _late_scatter_plan(src_refs, land_refs):
    x, y, c = _my_place()
    chips = [(1 - x, y), (x, 1 - y), (1 - x, 1 - y)]
    plan = [(src.at[2 * cx + cy], land.at[j], (cx, cy, c))
            for src, land in zip(src_refs, land_refs) for j, (cx, cy) in enumerate(chips)]
    return [entry + (k,) for k, entry in enumerate(plan)]


def _direct_scatter_plan(names):
    axes = [HALF_AXIS[n] for n in names]

    def plan(src_refs, land_refs):
        x, y, c = _my_place()
        chips = [(1 - x, y), (x, 1 - y), (1 - x, 1 - y)]
        out = []
        for i, (src, land) in enumerate(zip(src_refs, land_refs)):
            for f, (cx, cy) in enumerate(chips):
                for core in range(2):
                    out.append((_half(src, core, axes[i], (2 * cx + cy,)), land.at[2 * f + c], (cx, cy, core),
                                7 * i + 2 * f + c))
            out.append((_half(src, 1 - c, axes[i], (2 * x + y,)), land.at[6], (x, y, 1 - c), 7 * i + 6))
        return out

    return plan


def _row_tile(rows, mult=16, limit=ROW_TILE):
    return max(d for d in range(mult, limit + 1, mult) if rows % d == 0)


def _pair_sum(place, g, theirs, axis, name):
    nj, rr, cc = theirs.shape
    tr = _row_tile(rr, limit=1024)
    nb = rr // tr
    if axis == 0:
        g_map = lambda j, i, pr: (j, pr[0] * nb + i, 0)
    else:
        g_map = lambda j, i, pr: (j, i, pr[0])

    def body(pr, g_ref, t_ref, o_ref):
        o_ref[...] = (g_ref[...].astype(F32) + t_ref[...].astype(F32)).astype(BF16)

    spec = pl.BlockSpec((None, tr, cc), lambda j, i, pr: (j, i, 0))
    return pl.pallas_call(
        body, name=name,
        grid_spec=pltpu.PrefetchScalarGridSpec(
            num_scalar_prefetch=1, grid=(nj, nb),
            in_specs=[pl.BlockSpec((None, tr, cc), g_map), spec], out_specs=spec),
        out_shape=jax.ShapeDtypeStruct(theirs.shape, BF16))(place, g, theirs)


def _chip_sum(place, pair_sums, parts, name):
    _, rr, cc = parts.shape
    tr = _row_tile(rr, limit=1024)

    def body(pr, h_ref, p_ref, o_ref):
        acc = p_ref[0].astype(F32)
        for j in range(1, N_CHIPS - 1):
            acc = acc + p_ref[j].astype(F32)
        o_ref[...] = (acc + h_ref[...].astype(F32)).astype(BF16)

    return pl.pallas_call(
        body, name=name,
        grid_spec=pltpu.PrefetchScalarGridSpec(
            num_scalar_prefetch=1, grid=(rr // tr,),
            in_specs=[pl.BlockSpec((None, tr, cc), lambda i, pr: (pr[1], i, 0)),
                      pl.BlockSpec((N_CHIPS - 1, tr, cc), lambda i, pr: (0, i, 0))],
            out_specs=pl.BlockSpec((tr, cc), lambda i, pr: (i, 0))),
        out_shape=jax.ShapeDtypeStruct((rr, cc), BF16))(place, pair_sums, parts)


def _chip_sum_direct(place, g, parts, axis, name):
    n_parts, rr, cc = parts.shape
    tr = _row_tile(rr, limit=1024)
    nb = rr // tr
    if axis == 0:
        g_map = lambda i, pr: (pr[1], pr[0] * nb + i, 0)
    else:
        g_map = lambda i, pr: (pr[1], i, pr[0])

    def body(pr, g_ref, p_ref, o_ref):
        acc = p_ref[0].astype(F32)
        for j in range(1, n_parts):
            acc = acc + p_ref[j].astype(F32)
        o_ref[...] = (acc + g_ref[...].astype(F32)).astype(BF16)

    return pl.pallas_call(
        body, name=name,
        grid_spec=pltpu.PrefetchScalarGridSpec(
            num_scalar_prefetch=1, grid=(nb,),
            in_specs=[pl.BlockSpec((None, tr, cc), g_map), pl.BlockSpec((n_parts, tr, cc), lambda i, pr: (0, i, 0))],
            out_specs=pl.BlockSpec((tr, cc), lambda i, pr: (i, 0))),
        out_shape=jax.ShapeDtypeStruct((rr, cc), BF16))(place, g, parts)


def _silu(v):
    return v / (1.0 + jnp.exp(-v))


def _loss_and_grads(x2, ffn, target, gate):
    t, d = x2.shape

    def half_loss(x2_blk, ffn_blk, gate_row, target_blk):
        return 0.5 * _f_loss(x2_blk, ffn_blk, target_blk, gate_row)[0]

    def body(x2_ref, ffn_ref, tgt_ref, gate_ref, loss_ref, dx2_ref, dffn_ref, dgate_ref):
        rows, vjp = jax.vjp(lambda a, b, g: half_loss(a, b, g, tgt_ref[...]), x2_ref[...], ffn_ref[...], gate_ref[...])
        loss_ref[...] = rows
        dx2_ref[...], dffn_ref[...], dgate = vjp(jnp.ones_like(rows))

        @pl.when(pl.program_id(0) == 0)
        def _():
            dgate_ref[...] = jnp.zeros_like(dgate_ref)

        dgate_ref[...] += dgate

    blk = pl.BlockSpec((ROW_TILE, d), lambda i: (i, 0))
    row = pl.BlockSpec((1, d), lambda i: (0, 0))
    return pl.pallas_call(
        body, name="loss_and_grads", grid=(t // ROW_TILE,),
        in_specs=[blk, blk, blk, row],
        out_specs=[pl.BlockSpec((ROW_TILE, 1), lambda i: (i, 0)), blk, blk, row],
        out_shape=[jax.ShapeDtypeStruct((t, 1), F32), jax.ShapeDtypeStruct((t, d), F32), jax.ShapeDtypeStruct((t, d), F32),
                   jax.ShapeDtypeStruct((1, d), F32)],
        compiler_params=pltpu.CompilerParams(dimension_semantics=("arbitrary",), vmem_limit_bytes=MM_VMEM_LIMIT),
    )(x2, ffn, target, gate)


def _ada_bwd(c_all, dmod_cols):
    def body(c_ref, d_ref, o_ref):
        o_ref[...] = lax.dot_general(_silu(c_ref[...]), d_ref[...], (((0,), (0,)), ((), ())),
                                     precision=lax.Precision.HIGHEST, preferred_element_type=F32)

    return pl.pallas_call(body, name="ada_bwd", out_shape=jax.ShapeDtypeStruct((c_all.shape[1], dmod_cols.shape[1]), F32),
                          compiler_params=pltpu.CompilerParams(vmem_limit_bytes=MM_VMEM_LIMIT))(c_all, dmod_cols)


def _adamw_math(w, g, m, v):
    m = ADAM_B1 * m + (1.0 - ADAM_B1) * g
    v = ADAM_B2 * v + (1.0 - ADAM_B2) * (g * g)
    m_hat = m / (1.0 - ADAM_B1 ** ADAM_STEP)
    v_hat = v / (1.0 - ADAM_B2 ** ADAM_STEP)
    delta = -ADAM_LR * (m_hat / (jnp.sqrt(v_hat) + ADAM_EPS) + ADAM_WD * w)
    return delta, m, v


def _adamw(w, g, m, v, name):
    r, ccols = w.shape
    tr = max(d for d in range(8, ROW_TILE + 1, 8) if r % d == 0)
    spec = pl.BlockSpec((tr, ccols), lambda i: (i, 0))

    def body(w_ref, g_ref, m_ref, v_ref, d_ref, nm_ref, nv_ref):
        d_ref[...], nm_ref[...], nv_ref[...] = _adamw_math(w_ref[...], g_ref[...], m_ref[...], v_ref[...])

    return pl.pallas_call(body, name=name, grid=(r // tr,), in_specs=[spec] * 4, out_specs=[spec] * 3,
                          out_shape=[jax.ShapeDtypeStruct(w.shape, F32)] * 3,
                          compiler_params=pltpu.CompilerParams(vmem_limit_bytes=MM_VMEM_LIMIT))(w, g, m, v)


def _small_layout(sizes):
    offs, off = [], 0
    for n in sizes:
        offs.append(off)
        off += -(-n // LANES) * LANES
    total = -(-(off + LANES) // (8 * LANES)) * (8 * LANES)
    return offs, off, total


def _adamw_small(ws, g_all, ms, vs, offs, loss_off):
    n_p = len(ws)

    def device_sum(g_ref, off, width):
        blk = g_ref[:, off:off + width]
        acc = blk[0:1]
        for d in range(1, N_DEV):
            acc = acc + blk[d:d + 1]
        return acc

    def body(*refs):
        w_refs, m_refs, v_refs = refs[:n_p], refs[n_p:2 * n_p], refs[2 * n_p:3 * n_p]
        g_ref = refs[3 * n_p]
        outs = refs[3 * n_p + 1:]
        for i in range(n_p):
            n = w_refs[i].shape[1]
            g = device_sum(g_ref, offs[i], -(-n // LANES) * LANES)[:, :n]
            outs[i][...] = g
            outs[n_p + i][...], outs[2 * n_p + i][...], outs[3 * n_p + i][...] = _adamw_math(
                w_refs[i][...], g, m_refs[i][...], v_refs[i][...])
        outs[4 * n_p][...] = device_sum(g_ref, loss_off, LANES)

    res = pl.pallas_call(
        body, name="adamw_small",
        out_shape=[jax.ShapeDtypeStruct(a.shape, F32) for a in list(ws) * 4] + [jax.ShapeDtypeStruct((1, LANES), F32)],
    )(*ws, *ms, *vs, g_all)
    return res[:n_p], res[n_p:2 * n_p], res[2 * n_p:3 * n_p], res[3 * n_p:4 * n_p], res[4 * n_p]


def _adamw_halves(place, w, own, sib, m, v, axis, name, after):
    r, cc = w.shape
    if axis == 0:
        rows, gc = own.shape[0], own.shape[1]
        tr = _row_tile(rows)
        nb = rows // tr
        w_spec = pl.BlockSpec((tr, cc), lambda h, i, pr: (h * nb + i, 0))
        g_spec = pl.BlockSpec((tr, gc), lambda h, i, pr: (i, 0))
    else:
        tr = _row_tile(r)
        nb = r // tr
        gc = own.shape[1]
        w_spec = pl.BlockSpec((tr, gc), lambda h, i, pr: (i, h))
        g_spec = pl.BlockSpec((tr, gc), lambda h, i, pr: (i, 0))
    wc = w_spec.block_shape[1]

    def body(pr, w_ref, o_ref, s_ref, m_ref, v_ref, after_ref, g_ref, d_ref, nm_ref, nv_ref):
        g = jnp.where(pl.program_id(0) == pr[0], o_ref[...], s_ref[...]).astype(F32)[:, :wc]
        g_ref[...] = g
        d_ref[...], nm_ref[...], nv_ref[...] = _adamw_math(w_ref[...], g, m_ref[...], v_ref[...])

    return pl.pallas_call(
        body, name=name,
        grid_spec=pltpu.PrefetchScalarGridSpec(
            num_scalar_prefetch=1, grid=(2, nb),
            in_specs=[w_spec, g_spec, g_spec, w_spec, w_spec, ANY], out_specs=[w_spec] * 4),
        out_shape=[jax.ShapeDtypeStruct(w.shape, F32)] * 4,
        compiler_params=pltpu.CompilerParams(vmem_limit_bytes=MM_VMEM_LIMIT))(place, w, own, sib, m, v, after)


SMALL = ("b_ada", "norm_attn", "norm_ffn", "q_a_norm", "kv_a_norm", "q_norm", "k_nope_norm", "k_rope_norm",
         "out_norm_sb", "out_norm_mla")
WEIGHTS = ("w_ada", "b_ada", "norm_attn", "norm_ffn", "w_in", "q_a_norm", "w_q_up", "kv_a_norm", "w_kv_up",
           "q_norm", "k_nope_norm", "k_rope_norm", "out_norm_sb", "out_norm_mla", "w_out", "w_gate", "w_up",
           "w_down")


def kernel(x, c, positions, w_ada, b_ada, norm_attn, norm_ffn, w_in, q_a_norm, w_q_up, kv_a_norm, w_kv_up, q_norm, k_nope_norm, k_rope_norm, out_norm_sb, out_norm_mla, w_out, w_gate, w_up, w_down, loss_target, m_w_ada, m_b_ada, m_norm_attn, m_norm_ffn, m_w_in, m_q_a_norm, m_w_q_up, m_kv_a_norm, m_w_kv_up, m_q_norm, m_k_nope_norm, m_k_rope_norm, m_out_norm_sb, m_out_norm_mla, m_w_out, m_w_gate, m_w_up, m_w_down, v_w_ada, v_b_ada, v_norm_attn, v_norm_ffn, v_w_in, v_q_a_norm, v_w_q_up, v_kv_a_norm, v_w_kv_up, v_q_norm, v_k_nope_norm, v_k_rope_norm, v_out_norm_sb, v_out_norm_mla, v_w_out, v_w_gate, v_w_up, v_w_down):
    local = dict(locals())
    w = {n: local[n][0] for n in WEIGHTS}
    m = {n: local["m_" + n][0] for n in WEIGHTS}
    v = {n: local["v_" + n][0] for n in WEIGHTS}
    small = {n: w[n].reshape(1, -1) for n in SMALL}
    ix, iy, ic = _my_place()
    chip = 2 * ix + iy
    dev = 2 * chip + ic
    xs, target = x[0], loss_target[0]
    seq = xs.shape[0]

    ff_pad = FF_SHARD_PAD - FF_SHARD
    shards = {n: (w[n].T if n in TRAVELS_TRANSPOSED else w[n]).astype(BF16) for n in BIG}
    for n in ("w_gate", "w_up", "w_down"):
        shards[n] = jnp.pad(shards[n], ((0, ff_pad), (0, 0)))
    def landing(names):
        return [lax.dynamic_update_index_in_dim(lax.empty((N_CHIPS,) + shards[n].shape, BF16), shards[n], chip, 0)
                for n in names]

    ada_cols = w["w_ada"].shape[1]
    b_cols = lax.dynamic_slice_in_dim(small["b_ada"], chip * ada_cols, ada_cols, axis=1)
    half = MLA_ROPE // 2
    freqs = 1.0 / (ROPE_THETA ** (np.arange(half, dtype=np.float32) / half))
    zeros = np.zeros(LANES - MLA_ROPE, np.float32)
    freqs_row = jnp.asarray(np.concatenate([freqs, freqs, zeros]).astype(np.float32)[None])
    sign_row = jnp.asarray(np.concatenate([-np.ones(half), np.ones(half), zeros]).astype(np.float32)[None])
    early, early_done, c_gathered, mod_all, cos, sin = _gather_weights(
        EARLY, [shards[n] for n in EARLY], landing(EARLY), jnp.broadcast_to(c.reshape(1, D_MODEL), (8, D_MODEL)),
        w["w_ada"], b_cols, positions.reshape(seq, 1), freqs_row, sign_row)
    gathered = dict(zip(EARLY, early))
    c_all = c_gathered.reshape(N_DEV, 8, D_MODEL)[:, 0]
    mod_all = mod_all.reshape(N_CHIPS, 2, N_DEV, ada_cols)
    mod = lax.dynamic_index_in_dim(mod_all[:, 0], dev, axis=1, keepdims=False).reshape(1, N_MOD * D_MODEL)

    late_gather = _exchange_start("gather_late_start", [shards[n] for n in LATE], landing(LATE), _late_gather_plan,
                                  3 * len(LATE), early_done, mod)
    mod = late_gather[4]

    place = jnp.stack([ic, chip]).astype(jnp.int32)
    small_params = {n: small[n] for n in SMALL if n != "b_ada"}

    p1 = {**{n: gathered[n] for n in EARLY}, **small_params}
    (mixed, x_res), mixing_vjp = jax.vjp(lambda x_, mod_, p_: _mixing_stage(x_, mod_, p_, cos, sin), xs, mod, p1)
    _, landed = _exchange_wait("gather_late_wait", late_gather, _late_gather_plan, mixed)
    p2 = {**dict(zip(LATE, landed)), **small_params}
    (x2, ffn), ffn_vjp = jax.vjp(_ffn_stage, x_res, mixed, mod, p2)
    loss_rows, g_x2, g_ffn, g_gate2 = _loss_and_grads(x2, ffn, target, _mod_parts(mod)[5])
    loss_part = jnp.sum(loss_rows)
    gx2, gmixed, gmod2, gp2 = ffn_vjp((g_x2, g_ffn))
    gmod2 = gmod2 + jnp.concatenate([jnp.zeros((1, (N_MOD - 1) * D_MODEL), F32), g_gate2], axis=1)
    late_grads = [gp2[n] for n in LATE]
    late_plan = _direct_scatter_plan(LATE)
    late_scatter = _exchange_start(
        "grad_scatter_late_start", late_grads,
        [lax.empty((7,) + _half_shape(gr.shape[1:], HALF_AXIS[n]), BF16) for n, gr in zip(LATE, late_grads)],
        late_plan, 7 * len(LATE), gx2, gmixed)
    gx, gmod1, gp1 = mixing_vjp((late_scatter[4], gx2))
    gmod = gmod1 + gmod2
    gp = {n: gp1[n] + gp2[n] for n in small_params}

    sizes = [w[n].size for n in SMALL]
    offs, loss_off, n_small = _small_layout(sizes)
    pieces = []
    for n, size in zip(SMALL, sizes):
        pieces.append(gmod if n == "b_ada" else gp[n])
        if size % LANES:
            pieces.append(jnp.zeros((1, LANES - size % LANES), F32))
    pieces += [jnp.full((1, LANES), loss_part), jnp.zeros((1, n_small - loss_off - LANES), F32)]
    small_vec = jnp.concatenate(pieces, axis=1)

    g, delta, new_m, new_v = {}, {}, {}, {}

    def update(names, own, sib, after):
        for n, o, s in zip(names, own, sib):
            if n in TRAVELS_TRANSPOSED:
                res = _adamw_halves(place, w[n].T, o, s, m[n].T, v[n].T, HALF_AXIS[n], "adamw_" + n, after)
                g[n], delta[n], new_m[n], new_v[n] = [r.T for r in res]
            else:
                g[n], delta[n], new_m[n], new_v[n] = _adamw_halves(place, w[n], o, s, m[n], v[n], HALF_AXIS[n],
                                                                   "adamw_" + n, after)

    late_grads, late_parts = _exchange_wait("grad_scatter_late_wait", late_scatter, late_plan, gx)
    own_late = [_chip_sum_direct(place, gr, pt, HALF_AXIS[n], "grad_chip_sum_" + n)
                for n, gr, pt in zip(LATE, late_grads, late_parts)]
    early_grads = [gp1[n] for n in EARLY]
    theirs, small_gathered = _pair_exchange(EARLY, early_grads, "grad_pair_exchange_early",
                                            small_vec.reshape(8, n_small // 8))
    small_all = small_gathered.reshape(N_DEV, n_small)
    sib_late = _sibling_join(own_late, "grad_sibling_join_late", small_all)
    early_sums = [_pair_sum(place, gr, th, HALF_AXIS[n], "grad_pair_sum_" + n)
                  for n, gr, th in zip(EARLY, early_grads, theirs)]
    early_scatter = _exchange_start(
        "grad_scatter_early_start", early_sums,
        [lax.empty((N_CHIPS - 1,) + s.shape[1:], BF16) for s in early_sums], _late_scatter_plan, 3 * len(EARLY),
        sib_late[0], small_all)
    small_all = early_scatter[4]
    update(LATE, own_late, sib_late, small_all)

    *small_out, loss_row = _adamw_small([small[n] for n in SMALL], small_all, [m[n].reshape(1, -1) for n in SMALL],
                                        [v[n].reshape(1, -1) for n in SMALL], offs, loss_off)
    loss = loss_row[0, 0]
    for d, outs_d in zip((g, delta, new_m, new_v), small_out):
        d.update({n: o.reshape(w[n].shape) for n, o in zip(SMALL, outs_d)})

    dmod_all = small_all[:, :N_MOD * D_MODEL]
    g["w_ada"] = _ada_bwd(c_all, lax.dynamic_slice_in_dim(dmod_all, chip * ada_cols, ada_cols, axis=1))
    delta["w_ada"], new_m["w_ada"], new_v["w_ada"] = _adamw(w["w_ada"], g["w_ada"], m["w_ada"], v["w_ada"], "adamw_w_ada")

    early_sums, early_parts = _exchange_wait("grad_scatter_early_wait", early_scatter, _late_scatter_plan,
                                             delta["w_ada"])
    own_early = [_chip_sum(place, ps, pt, "grad_chip_sum_" + n)
                 for n, ps, pt in zip(EARLY, early_sums, early_parts)]
    sib_early = _sibling_join(own_early, "grad_sibling_join_early", delta["w_ada"])
    update(EARLY, own_early, sib_early, sib_early[0])

    def outs(d):
        return [d[n][None] for n in WEIGHTS]

    return (loss, gx[None], *outs(g), *outs(delta), *outs(new_m), *outs(new_v))
```

```python
import numpy as np
import jax
import jax.numpy as jnp
from jax import lax
from jax.experimental import pallas as pl
from jax.experimental.pallas import tpu as pltpu

F32 = jnp.float32
BF16 = jnp.bfloat16
MESH = pl.DeviceIdType.MESH
ANY = pl.BlockSpec(memory_space=pl.ANY)

D_MODEL = 1024
SB_HEADS = 8
SB_HEAD_DIM = 64
SB_WIDTH = 512
MLA_HEADS = 4
MLA_NOPE = 128
MLA_ROPE = 64
MLA_QK = 192
MLA_V = 128
MLA_Q_RANK = 384
MLA_KV_RANK = 256
D_FF = 2816
N_MOD = 6
ROPE_THETA = 10000.0
EPS = 1e-6
LANES = 128

ADAM_LR = 0.001
ADAM_B1 = 0.9
ADAM_B2 = 0.999
ADAM_EPS = 1e-08
ADAM_WD = 0.01
ADAM_STEP = 10

N_CHIPS = 4
N_DEV = 8
ROW_TILE = 512
MM_ROW_TILE = 512
ATT_BLK = 256
MM_VMEM_LIMIT = 56 * 1024 * 1024
FF_SHARD = D_FF // N_CHIPS
FF_SHARD_PAD = 768


def _mm(a, b, mode, name, tm, tn, out_dtype=F32):
    if mode == "nn":
        (m, k), n = a.shape, b.shape[1]
        a_spec = pl.BlockSpec((tm, k), lambda j, i: (i, 0))
        b_spec = pl.BlockSpec((k, tn), lambda j, i: (0, j))
        dims = (((1,), (0,)), ((), ()))
    elif mode == "nt":
        (m, k), n = a.shape, b.shape[0]
        a_spec = pl.BlockSpec((tm, k), lambda j, i: (i, 0))
        b_spec = pl.BlockSpec((tn, k), lambda j, i: (j, 0))
        dims = (((1,), (1,)), ((), ()))
    else:
        (k, m), n = a.shape, b.shape[1]
        a_spec = pl.BlockSpec((k, tm), lambda j, i: (0, i))
        b_spec = pl.BlockSpec((k, tn), lambda j, i: (0, j))
        dims = (((0,), (0,)), ((), ()))
    assert m % tm == 0 and n % tn == 0, (name, m, n, tm, tn)

    def body(a_ref, b_ref, o_ref):
        o_ref[...] = lax.dot_general(a_ref[...].astype(BF16), b_ref[...].astype(BF16), dims,
                                     preferred_element_type=F32).astype(out_dtype)

    return pl.pallas_call(
        body, name=name, grid=(n // tn, m // tm),
        in_specs=[a_spec, b_spec],
        out_specs=pl.BlockSpec((tm, tn), lambda j, i: (i, j)),
        out_shape=jax.ShapeDtypeStruct((m, n), out_dtype),
        compiler_params=pltpu.CompilerParams(dimension_semantics=("arbitrary", "arbitrary"),
                                             vmem_limit_bytes=MM_VMEM_LIMIT),
    )(a, b)


def _make_linear(name, tk_w, tn_w):
    @jax.custom_vjp
    def op(a, w):
        return _mm(a, w, "nn", name + "_fwd", MM_ROW_TILE, w.shape[1])

    def fwd(a, w):
        return op(a, w), (a, w)

    def bwd(res, dy):
        a, w = res
        da = _mm(dy, w, "nt", name + "_dx", MM_ROW_TILE, w.shape[0])
        dw = _mm(a, dy, "tn", name + "_dw", tk_w, tn_w, out_dtype=BF16)
        return da, dw

    op.defvjp(fwd, bwd)
    return op


def _make_linear_split_t(name, widths, tk_w):
    starts = [sum(widths[:g]) for g in range(len(widths))]

    def call_fwd(a, wt):
        t, k = a.shape
        n = wt.shape[0]

        def body(a_ref, w_ref, *o_refs):
            y = _nt(a_ref[...].astype(BF16), w_ref[...])
            for o_ref, s0, wd in zip(o_refs, starts, widths):
                o_ref[...] = y[:, s0:s0 + wd]

        return pl.pallas_call(
            body, name=name + "_fwd", grid=(t // MM_ROW_TILE,),
            in_specs=[pl.BlockSpec((MM_ROW_TILE, k), lambda i: (i, 0)), pl.BlockSpec((n, k), lambda i: (0, 0))],
            out_specs=[pl.BlockSpec((MM_ROW_TILE, wd), lambda i: (i, 0)) for wd in widths],
            out_shape=[jax.ShapeDtypeStruct((t, wd), F32) for wd in widths],
            compiler_params=pltpu.CompilerParams(dimension_semantics=("arbitrary",), vmem_limit_bytes=MM_VMEM_LIMIT),
        )(a, wt)

    def call_dx(dys, wt):
        t = dys[0].shape[0]
        n, k = wt.shape

        def body(*refs):
            dy_refs, w_ref, o_ref = refs[:-2], refs[-2], refs[-1]
            acc = jnp.zeros((MM_ROW_TILE, k), F32)
            for dy_ref, s0, wd in zip(dy_refs, starts, widths):
                acc = acc + jnp.dot(dy_ref[...].astype(BF16), w_ref[s0:s0 + wd, :], preferred_element_type=F32)
            o_ref[...] = acc

        return pl.pallas_call(
            body, name=name + "_dx", grid=(t // MM_ROW_TILE,),
            in_specs=[pl.BlockSpec((MM_ROW_TILE, wd), lambda i: (i, 0)) for wd in widths]
            + [pl.BlockSpec((n, k), lambda i: (0, 0))],
            out_specs=pl.BlockSpec((MM_ROW_TILE, k), lambda i: (i, 0)),
            out_shape=jax.ShapeDtypeStruct((t, k), F32),
            compiler_params=pltpu.CompilerParams(dimension_semantics=("arbitrary",), vmem_limit_bytes=MM_VMEM_LIMIT),
        )(*dys, wt)

    def call_dw(a, dys, wt):
        t, k = a.shape
        n = wt.shape[0]

        def body(a_ref, *refs):
            dy_refs, o_ref = refs[:-1], refs[-1]
            ab = a_ref[...].astype(BF16)
            for dy_ref, s0, wd in zip(dy_refs, starts, widths):
                o_ref[s0:s0 + wd, :] = _tn(dy_ref[...].astype(BF16), ab).astype(BF16)
            if starts[-1] + widths[-1] < n:
                o_ref[starts[-1] + widths[-1]:, :] = jnp.zeros((n - starts[-1] - widths[-1], tk_w), BF16)

        return pl.pallas_call(
            body, name=name + "_dw", grid=(k // tk_w,),
            in_specs=[pl.BlockSpec((t, tk_w), lambda i: (0, i))]
            + [pl.BlockSpec((t, wd), lambda i: (0, 0)) for wd in widths],
            out_specs=pl.BlockSpec((n, tk_w), lambda i: (0, i)),
            out_shape=jax.ShapeDtypeStruct((n, k), BF16),
            compiler_params=pltpu.CompilerParams(dimension_semantics=("arbitrary",), vmem_limit_bytes=MM_VMEM_LIMIT),
        )(a, *dys)

    @jax.custom_vjp
    def op(a, wt):
        return tuple(call_fwd(a, wt))

    def fwd(a, wt):
        return op(a, wt), (a, wt)

    def bwd(res, dys):
        a, wt = res
        return call_dx(dys, wt), call_dw(a, dys, wt)

    op.defvjp(fwd, bwd)
    return op


def _make_linear_split(name, widths, tk_w):
    starts = [sum(widths[:g]) for g in range(len(widths))]

    def call_fwd(a, w):
        t, k = a.shape
        n = w.shape[1]

        def body(a_ref, w_ref, *o_refs):
            y = jnp.dot(a_ref[...].astype(BF16), w_ref[...], preferred_element_type=F32)
            for o_ref, s0, wd in zip(o_refs, starts, widths):
                o_ref[...] = y[:, s0:s0 + wd]

        return pl.pallas_call(
            body, name=name + "_fwd", grid=(t // MM_ROW_TILE,),
            in_specs=[pl.BlockSpec((MM_ROW_TILE, k), lambda i: (i, 0)), pl.BlockSpec((k, n), lambda i: (0, 0))],
            out_specs=[pl.BlockSpec((MM_ROW_TILE, wd), lambda i: (i, 0)) for wd in widths],
            out_shape=[jax.ShapeDtypeStruct((t, wd), F32) for wd in widths],
            compiler_params=pltpu.CompilerParams(dimension_semantics=("arbitrary",), vmem_limit_bytes=MM_VMEM_LIMIT),
        )(a, w)

    def call_dx(dys, w):
        t = dys[0].shape[0]
        k, n = w.shape

        def body(*refs):
            dy_refs, w_ref, o_ref = refs[:-2], refs[-2], refs[-1]
            acc = jnp.zeros((MM_ROW_TILE, k), F32)
            for dy_ref, s0, wd in zip(dy_refs, starts, widths):
                acc = acc + _nt(dy_ref[...].astype(BF16), w_ref[:, s0:s0 + wd])
            o_ref[...] = acc

        return pl.pallas_call(
            body, name=name + "_dx", grid=(t // MM_ROW_TILE,),
            in_specs=[pl.BlockSpec((MM_ROW_TILE, wd), lambda i: (i, 0)) for wd in widths]
            + [pl.BlockSpec((k, n), lambda i: (0, 0))],
            out_specs=pl.BlockSpec((MM_ROW_TILE, k), lambda i: (i, 0)),
            out_shape=jax.ShapeDtypeStruct((t, k), F32),
            compiler_params=pltpu.CompilerParams(dimension_semantics=("arbitrary",), vmem_limit_bytes=MM_VMEM_LIMIT),
        )(*dys, w)

    def call_dw(a, dys, w):
        t, k = a.shape
        n = w.shape[1]

        def body(a_ref, *refs):
            dy_refs, o_ref = refs[:-1], refs[-1]
            ab = a_ref[...].astype(BF16)
            for dy_ref, s0, wd in zip(dy_refs, starts, widths):
                o_ref[:, s0:s0 + wd] = _tn(ab, dy_ref[...].astype(BF16)).astype(BF16)
            if starts[-1] + widths[-1] < n:
                o_ref[:, starts[-1] + widths[-1]:] = jnp.zeros((tk_w, n - starts[-1] - widths[-1]), BF16)

        return pl.pallas_call(
            body, name=name + "_dw", grid=(k // tk_w,),
            in_specs=[pl.BlockSpec((t, tk_w), lambda i: (0, i))]
            + [pl.BlockSpec((t, wd), lambda i: (0, 0)) for wd in widths],
            out_specs=pl.BlockSpec((tk_w, n), lambda i: (i, 0)),
            out_shape=jax.ShapeDtypeStruct((k, n), BF16),
            compiler_params=pltpu.CompilerParams(dimension_semantics=("arbitrary",), vmem_limit_bytes=MM_VMEM_LIMIT),
        )(a, *dys)

    @jax.custom_vjp
    def op(a, w):
        return tuple(call_fwd(a, w))

    def fwd(a, w):
        return op(a, w), (a, w)

    def bwd(res, dys):
        a, w = res
        return call_dx(dys, w), call_dw(a, dys, w)

    op.defvjp(fwd, bwd)
    return op


def _row_spec(arr, tb):
    return pl.BlockSpec((tb, arr.shape[1]), lambda i: (i, 0))


def _full_spec(arr):
    return pl.BlockSpec(arr.shape, lambda i: (0, 0))


def _make_rowwise(name, f, n_rows, n_params, out_cols, diff_rows, out_dtypes=None, grad_dtypes=None):
    n_out = len(out_cols)
    out_dtypes = out_dtypes or [F32] * n_out
    grad_dtypes = grad_dtypes or [F32] * sum(diff_rows)

    def call_fwd(rows, params):
        t = rows[0].shape[0]

        def body(*refs):
            ins = [r[...] for r in refs[:n_rows + n_params]]
            outs = f(*ins)
            for o_ref, o in zip(refs[n_rows + n_params:], outs):
                o_ref[...] = o.astype(o_ref.dtype)

        return pl.pallas_call(
            body, name=name + "_fwd", grid=(t // ROW_TILE,),
            in_specs=[_row_spec(a, ROW_TILE) for a in rows] + [_full_spec(p) for p in params],
            out_specs=[pl.BlockSpec((ROW_TILE, n), lambda i: (i, 0)) for n in out_cols],
            out_shape=[jax.ShapeDtypeStruct((t, n), dt) for n, dt in zip(out_cols, out_dtypes)],
            compiler_params=pltpu.CompilerParams(dimension_semantics=("arbitrary",),
                                                 vmem_limit_bytes=MM_VMEM_LIMIT),
        )(*rows, *params)

    def call_bwd(rows, params, cts):
        t = rows[0].shape[0]
        d_rows = [a for a, d in zip(rows, diff_rows) if d]
        n_in = n_rows + n_params + n_out

        def body(*refs):
            ins = [r[...] for r in refs[:n_rows + n_params]]
            ct = tuple(r[...].astype(F32) for r in refs[n_rows + n_params:n_in])
            _, vjp = jax.vjp(f, *ins)
            grads = vjp(ct)
            out_refs = refs[n_in:]
            g_rows = [g for g, d in zip(grads[:n_rows], diff_rows) if d]
            for o_ref, g in zip(out_refs[:len(g_rows)], g_rows):
                o_ref[...] = g.astype(o_ref.dtype)
            p_refs = out_refs[len(g_rows):]

            if p_refs:
                @pl.when(pl.program_id(0) == 0)
                def _():
                    for p_ref in p_refs:
                        p_ref[...] = jnp.zeros_like(p_ref)

                for p_ref, g in zip(p_refs, grads[n_rows:]):
                    p_ref[...] += g

        return pl.pallas_call(
            body, name=name + "_bwd", grid=(t // ROW_TILE,),
            in_specs=[_row_spec(a, ROW_TILE) for a in rows] + [_full_spec(p) for p in params]
            + [_row_spec(c, ROW_TILE) for c in cts],
            out_specs=[_row_spec(a, ROW_TILE) for a in d_rows] + [_full_spec(p) for p in params],
            out_shape=[jax.ShapeDtypeStruct(a.shape, dt) for a, dt in zip(d_rows, grad_dtypes)]
            + [jax.ShapeDtypeStruct(p.shape, F32) for p in params],
            compiler_params=pltpu.CompilerParams(dimension_semantics=("arbitrary",),
                                                 vmem_limit_bytes=MM_VMEM_LIMIT),
        )(*rows, *params, *cts)

    @jax.custom_vjp
    def op(*args):
        return tuple(call_fwd(args[:n_rows], args[n_rows:]))

    def fwd(*args):
        return op(*args), args

    def bwd(args, cts):
        rows, params = args[:n_rows], args[n_rows:]
        outs = call_bwd(rows, params, cts)
        it = iter(outs)
        g_rows = [next(it) if d else jnp.zeros_like(a) for a, d in zip(rows, diff_rows)]
        return tuple(g_rows) + tuple(it)

    op.defvjp(fwd, bwd)
    return op


def _rms(x, g, n):
    return x * lax.rsqrt(jnp.sum(x * x, axis=-1, keepdims=True) * (1.0 / n) + EPS) * g


def _f_pre_attn(x, g, scale, shift):
    return _rms(x, g, D_MODEL) * (1.0 + scale) + shift, x


def _f_mla_a(cq, ckv, gq, gkv):
    return _rms(cq, gq, MLA_Q_RANK), _rms(ckv, gkv, MLA_KV_RANK)


@jax.custom_vjp
def _split_lanes(x):
    return tuple(x[:, i * LANES:(i + 1) * LANES] for i in range(x.shape[1] // LANES))


def _split_lanes_fwd(x):
    return _split_lanes(x), None


def _split_lanes_bwd(_, cts):
    return (jnp.concatenate(cts, axis=1),)


_split_lanes.defvjp(_split_lanes_fwd, _split_lanes_bwd)


def _f_mla_b(qall, kn_all, kr, kr_sw, cos, sin, gqn, gqr, gqr_sw, gkn, gkr, gkr_sw):
    q = _split_lanes(qall)
    kn = _split_lanes(kn_all)
    qn_o, qr_o, kn_o = [], [], []
    for h in range(MLA_HEADS):
        qn, qr, qs = q[h], q[MLA_HEADS + h], q[2 * MLA_HEADS + h]
        ss = jnp.sum(qn * qn, axis=-1, keepdims=True) + jnp.sum(qr * qr, axis=-1, keepdims=True)
        rs = lax.rsqrt(ss * (1.0 / MLA_QK) + EPS)
        qn_o.append(qn * rs * gqn)
        qr_o.append((qr * rs * gqr) * cos + (qs * rs * gqr_sw) * sin)
        kn_o.append(_rms(kn[h], gkn, MLA_NOPE))
    rs = lax.rsqrt(jnp.sum(kr * kr, axis=-1, keepdims=True) * (1.0 / MLA_ROPE) + EPS)
    kr_o = (kr * rs * gkr) * cos + (kr_sw * rs * gkr_sw) * sin
    return (jnp.concatenate(qn_o, axis=1), jnp.concatenate(qr_o, axis=1), jnp.concatenate(kn_o, axis=1), kr_o)


def _f_post_attn(o_sb, o_mla, g_sb, g_mla):
    return (jnp.concatenate([_rms(o_sb, g_sb, SB_WIDTH), _rms(o_mla, g_mla, SB_WIDTH)], axis=1),)


def _f_pre_ffn(x, attn, gate, g, scale, shift):
    x2 = x + gate * attn
    return x2, _rms(x2, g, D_MODEL) * (1.0 + scale) + shift


def _f_swiglu(gt, up):
    return (gt / (1.0 + jnp.exp(-gt)) * up,)


def _f_loss(x2, ffn, target, gate):
    err = x2 + gate * ffn - target
    return (jnp.sum(err * err, axis=-1, keepdims=True) * (1.0 / D_MODEL),)


def _hi_lo_dot(x, tri):
    hi = x.astype(BF16)
    lo = (x - hi.astype(F32)).astype(BF16)
    return (jnp.dot(hi, tri, preferred_element_type=F32) + jnp.dot(lo, tri, preferred_element_type=F32))


def _tri(cmp):
    r = lax.broadcasted_iota(jnp.int32, (ATT_BLK, ATT_BLK), 0)
    c = lax.broadcasted_iota(jnp.int32, (ATT_BLK, ATT_BLK), 1)
    return cmp(r, c).astype(BF16)


def _nt(a, b):
    return lax.dot_general(a, b, (((1,), (1,)), ((), ())), preferred_element_type=F32)


def _tn(a, b):
    return lax.dot_general(a, b, (((0,), (0,)), ((), ())), preferred_element_type=F32)


def _sb_logs(z):
    lb = jnp.minimum(z, 0.0) - jnp.log(1.0 + jnp.exp(-jnp.abs(z)))
    return lb, lb - z


def _sb_fwd(q, k, v):
    t = q.shape[0]
    nq = t // ATT_BLK
    scale = SB_HEAD_DIM ** -0.5

    def body(q_ref, k_ref, v_ref, o_ref, tot_ref):
        qi = pl.program_id(1)
        lane = lax.broadcasted_iota(jnp.int32, (ATT_BLK, LANES), 1)
        tri = _tri(lambda r, c: r > c)
        qv = q_ref[...] * scale
        heads = [(lane // SB_HEAD_DIM) == hh for hh in range(2)]
        qms = [jnp.where(mine, qv, 0.0).astype(BF16) for mine in heads]

        def blocks(kbs, carry, diagonal):
            acc = carry[0]
            nb = len(kbs)
            chains = [(b, hh) for b in range(nb) for hh in range(2)]
            offs = [pl.multiple_of(kb * ATT_BLK, ATT_BLK) for kb in kbs]
            kks = [k_ref[pl.ds(off, ATT_BLK), :].astype(BF16) for off in offs]
            v_blks = [v_ref[pl.ds(off, ATT_BLK), :] for off in offs]
            if any(diagonal):
                valid = (lax.broadcasted_iota(jnp.int32, (ATT_BLK, ATT_BLK), 1)
                         < lax.broadcasted_iota(jnp.int32, (ATT_BLK, ATT_BLK), 0))
            zs = {ch: _nt(qms[ch[1]], kks[ch[0]]) for ch in chains}
            vvs = {(b, hh): jnp.where(heads[hh], v_blks[b], 0.0).astype(BF16) for b, hh in chains}
            logs = {ch: _sb_logs(zs[ch]) for ch in chains}
            l1ms = {ch: jnp.where(valid, logs[ch][1], 0.0) if diagonal[ch[0]] else logs[ch][1] for ch in chains}
            run = {(0, hh): carry[1 + hh] for hh in range(2)}
            for b, hh in chains:
                run[(b + 1, hh)] = run[(b, hh)] + jnp.sum(l1ms[(b, hh)], axis=-1, keepdims=True)
            afters = {ch: _hi_lo_dot(l1ms[ch], tri) for ch in chains}
            ws = {ch: jnp.exp(logs[ch][0] + (afters[ch] + run[ch])) for ch in chains}
            ws = {ch: jnp.where(valid, ws[ch], 0.0) if diagonal[ch[0]] else ws[ch] for ch in chains}
            for ch in chains:
                acc = acc + jnp.dot(ws[ch].astype(BF16), vvs[ch], preferred_element_type=F32)
            return (acc, run[(nb, 0)], run[(nb, 1)])

        zero = jnp.zeros((ATT_BLK, 1), F32)
        init = (jnp.zeros((ATT_BLK, LANES), F32), zero, zero)
        carry = lax.cond(qi % 2 == 1, lambda cr: blocks([qi, qi - 1], cr, (True, False)),
                         lambda cr: blocks([qi], cr, (True,)), init)
        top = qi - 1 - qi % 2
        carry = lax.fori_loop(0, qi // 2, lambda pr, cr: blocks([top - 2 * pr, top - 1 - 2 * pr], cr, (False, False)),
                              carry)
        o_ref[...] = carry[0]
        for hh in range(2):
            tot_ref[:, hh * LANES:(hh + 1) * LANES] = jnp.broadcast_to(carry[1 + hh], (ATT_BLK, LANES))

    return pl.pallas_call(
        body, name="sb_attn_fwd", grid=(SB_HEADS // 2, nq),
        in_specs=[pl.BlockSpec((ATT_BLK, LANES), lambda p, i: (i, p)),
                  pl.BlockSpec((t, LANES), lambda p, i: (0, p)),
                  pl.BlockSpec((t, LANES), lambda p, i: (0, p))],
        out_specs=[pl.BlockSpec((ATT_BLK, LANES), lambda p, i: (i, p)),
                   pl.BlockSpec((ATT_BLK, 2 * LANES), lambda p, i: (i, p))],
        out_shape=[jax.ShapeDtypeStruct((t, SB_WIDTH), F32), jax.ShapeDtypeStruct((t, SB_HEADS * LANES), F32)],
        compiler_params=pltpu.CompilerParams(dimension_semantics=("arbitrary", "arbitrary")),
    )(q, k, v)


def _sb_bwd(q, k, v, tot, do):
    t = q.shape[0]
    nq = t // ATT_BLK
    scale = SB_HEAD_DIM ** -0.5

    def body(q_ref, k_ref, v_ref, tot_ref, do_ref, dq_ref, dk_ref, dv_ref):
        qi = pl.program_id(1)

        @pl.when(qi == 0)
        def _():
            dk_ref[...] = jnp.zeros_like(dk_ref)
            dv_ref[...] = jnp.zeros_like(dv_ref)

        lane = lax.broadcasted_iota(jnp.int32, (ATT_BLK, LANES), 1)
        tri_incl = _tri(lambda r, c: r <= c)
        tri_lt = _tri(lambda r, c: r < c)
        qv = q_ref[...] * scale
        dov = do_ref[...]
        heads = [(lane // SB_HEAD_DIM) == hh for hh in range(2)]
        qms = [jnp.where(mine, qv, 0.0).astype(BF16) for mine in heads]
        doms = [jnp.where(mine, dov, 0.0).astype(BF16) for mine in heads]
        tots = [tot_ref[:, hh * LANES:hh * LANES + 1] for hh in range(2)]

        def blocks(kbs, carry, diagonal):
            dq = carry[0]
            nb = len(kbs)
            chains = [(b, hh) for b in range(nb) for hh in range(2)]
            offs = [pl.multiple_of(kb * ATT_BLK, ATT_BLK) for kb in kbs]
            k_blks = [k_ref[pl.ds(off, ATT_BLK), :] for off in offs]
            vvs = [v_ref[pl.ds(off, ATT_BLK), :].astype(BF16) for off in offs]
            if any(diagonal):
                valid = (lax.broadcasted_iota(jnp.int32, (ATT_BLK, ATT_BLK), 1)
                         < lax.broadcasted_iota(jnp.int32, (ATT_BLK, ATT_BLK), 0))
            kks = {(b, hh): jnp.where(heads[hh], k_blks[b], 0.0).astype(BF16) for b, hh in chains}
            zs = {ch: _nt(qms[ch[1]], kks[ch]) for ch in chains}
            dws = {ch: _nt(doms[ch[1]], vvs[ch[0]]) for ch in chains}
            logs = {ch: _sb_logs(zs[ch]) for ch in chains}
            lbs = {ch: logs[ch][0] for ch in chains}
            l1m_all = {ch: logs[ch][1] for ch in chains}
            l1ms = {ch: jnp.where(valid, l1m_all[ch], 0.0) if diagonal[ch[0]] else l1m_all[ch] for ch in chains}
            pre, c_de = {}, {}
            for hh in range(2):
                pre[(0, hh)], c_de[(0, hh)] = carry[1 + 2 * hh], carry[2 + 2 * hh]
            for b, hh in chains:
                pre[(b + 1, hh)] = pre[(b, hh)] + jnp.sum(l1ms[(b, hh)], axis=-1, keepdims=True)
            prefix = {ch: _hi_lo_dot(l1ms[ch], tri_incl) for ch in chains}
            ws = {ch: jnp.exp(lbs[ch] + (tots[ch[1]] - (prefix[ch] + pre[ch]))) for ch in chains}
            ws = {ch: jnp.where(valid, ws[ch], 0.0) if diagonal[ch[0]] else ws[ch] for ch in chains}
            d_es = {ch: ws[ch] * dws[ch] for ch in chains}
            for b, hh in chains:
                c_de[(b + 1, hh)] = c_de[(b, hh)] + jnp.sum(d_es[(b, hh)], axis=-1, keepdims=True)
            dvs = [_tn(ws[(b, 0)].astype(BF16), doms[0]) + _tn(ws[(b, 1)].astype(BF16), doms[1]) for b in range(nb)]
            dl1ms = {ch: jnp.dot(d_es[ch].astype(BF16), tri_lt, preferred_element_type=F32) + c_de[ch] for ch in chains}
            dzs = {ch: d_es[ch] * jnp.exp(l1m_all[ch]) - dl1ms[ch] * jnp.exp(lbs[ch]) for ch in chains}
            dzs = {ch: jnp.where(valid, dzs[ch], 0.0) if diagonal[ch[0]] else dzs[ch] for ch in chains}
            dzs = {ch: dzs[ch].astype(BF16) for ch in chains}
            for ch in chains:
                dq = dq + jnp.dot(dzs[ch], kks[ch], preferred_element_type=F32)
            for b in range(nb):
                dk_ref[pl.ds(offs[b], ATT_BLK), :] += _tn(dzs[(b, 0)], qms[0]) + _tn(dzs[(b, 1)], qms[1])
                dv_ref[pl.ds(offs[b], ATT_BLK), :] += dvs[b]
            return (dq, pre[(nb, 0)], c_de[(nb, 0)], pre[(nb, 1)], c_de[(nb, 1)])

        zero = jnp.zeros((ATT_BLK, 1), F32)
        carry = lax.fori_loop(0, qi // 2, lambda pr, cr: blocks([2 * pr, 2 * pr + 1], cr, (False, False)),
                              (jnp.zeros((ATT_BLK, LANES), F32), zero, zero, zero, zero))
        carry = lax.cond(qi % 2 == 1, lambda cr: blocks([qi - 1, qi], cr, (False, True)),
                         lambda cr: blocks([qi], cr, (True,)), carry)
        dq_ref[...] = carry[0] * scale

    return pl.pallas_call(
        body, name="sb_attn_bwd", grid=(SB_HEADS // 2, nq),
        in_specs=[pl.BlockSpec((ATT_BLK, LANES), lambda p, i: (i, p)),
                  pl.BlockSpec((t, LANES), lambda p, i: (0, p)),
                  pl.BlockSpec((t, LANES), lambda p, i: (0, p)),
                  pl.BlockSpec((ATT_BLK, 2 * LANES), lambda p, i: (i, p)),
                  pl.BlockSpec((ATT_BLK, LANES), lambda p, i: (i, p))],
        out_specs=[pl.BlockSpec((ATT_BLK, LANES), lambda p, i: (i, p)),
                   pl.BlockSpec((t, LANES), lambda p, i: (0, p)),
                   pl.BlockSpec((t, LANES), lambda p, i: (0, p))],
        out_shape=[jax.ShapeDtypeStruct((t, SB_WIDTH), F32)] * 3,
        compiler_params=pltpu.CompilerParams(dimension_semantics=("arbitrary", "arbitrary")),
    )(q, k, v, tot, do)


@jax.custom_vjp
def _sb_attention(q, k, v):
    return _sb_fwd(q, k, v)[0]


def _sb_attention_fwd(q, k, v):
    o, tot = _sb_fwd(q, k, v)
    return o, (q, k, v, tot)


def _sb_attention_bwd(res, do):
    return tuple(_sb_bwd(*res, do))


_sb_attention.defvjp(_sb_attention_fwd, _sb_attention_bwd)


def _mla_fwd(qn, qr, kn, kr, v):
    t = qn.shape[0]
    nq = t // ATT_BLK
    scale = MLA_QK ** -0.5

    def body(qn_ref, qr_ref, kn_ref, kr_ref, v_ref, o_ref, lse_ref):
        qi = pl.program_id(1)
        lanes = [slice(hh * LANES, (hh + 1) * LANES) for hh in range(2)]
        qnb = [qn_ref[:, sl].astype(BF16) for sl in lanes]
        qrb = [qr_ref[:, sl].astype(BF16) for sl in lanes]

        def blocks(kbs, carry, diagonal):
            nb = len(kbs)
            chains = [(b, hh) for b in range(nb) for hh in range(2)]
            offs = [pl.multiple_of(kb * ATT_BLK, ATT_BLK) for kb in kbs]
            krbs = [kr_ref[pl.ds(off, ATT_BLK), :].astype(BF16) for off in offs]
            accs, ms, ls = [carry[0], carry[3]], [carry[1], carry[4]], [carry[2], carry[5]]
            ss = {(b, hh): (_nt(qnb[hh], kn_ref[pl.ds(offs[b], ATT_BLK), lanes[hh]].astype(BF16))
                            + _nt(qrb[hh], krbs[b])) * scale for b, hh in chains}
            if any(diagonal):
                causal = (lax.broadcasted_iota(jnp.int32, (ATT_BLK, ATT_BLK), 1)
                          <= lax.broadcasted_iota(jnp.int32, (ATT_BLK, ATT_BLK), 0))
                ss = {ch: jnp.where(causal, ss[ch], -jnp.inf) if diagonal[ch[0]] else ss[ch] for ch in chains}
            m_new = list(ms)
            for b, hh in chains:
                m_new[hh] = jnp.maximum(m_new[hh], jnp.max(ss[(b, hh)], axis=-1, keepdims=True))
            ps = {(b, hh): jnp.exp(ss[(b, hh)] - m_new[hh]) for b, hh in chains}
            alphas = [jnp.exp(ms[hh] - m_new[hh]) for hh in range(2)]
            pvs = {(b, hh): jnp.dot(ps[(b, hh)].astype(BF16), v_ref[pl.ds(offs[b], ATT_BLK), lanes[hh]].astype(BF16),
                                    preferred_element_type=F32) for b, hh in chains}
            out = []
            for hh in range(2):
                acc, l = accs[hh] * alphas[hh], ls[hh] * alphas[hh]
                for b in range(nb):
                    acc, l = acc + pvs[(b, hh)], l + jnp.sum(ps[(b, hh)], axis=-1, keepdims=True)
                out += [acc, m_new[hh], l]
            return tuple(out)

        init = (jnp.zeros((ATT_BLK, LANES), F32), jnp.full((ATT_BLK, 1), -jnp.inf, F32), jnp.zeros((ATT_BLK, 1), F32))
        carry = lax.cond(qi % 2 == 1, lambda cr: blocks([qi, qi - 1], cr, (True, False)),
                         lambda cr: blocks([qi], cr, (True,)), init + init)
        carry = lax.fori_loop(0, qi // 2, lambda pr, cr: blocks([2 * pr, 2 * pr + 1], cr, (False, False)), carry)
        for hh in range(2):
            acc, m, l = carry[3 * hh:3 * hh + 3]
            o_ref[:, lanes[hh]] = acc / l
            lse_ref[:, lanes[hh]] = jnp.broadcast_to(m + jnp.log(l), (ATT_BLK, LANES))

    blk = pl.BlockSpec((ATT_BLK, 2 * LANES), lambda p, i: (i, p))
    full = pl.BlockSpec((t, 2 * LANES), lambda p, i: (0, p))
    return pl.pallas_call(
        body, name="mla_attn_fwd", grid=(MLA_HEADS // 2, nq),
        in_specs=[blk, blk, full, pl.BlockSpec((t, LANES), lambda p, i: (0, 0)), full],
        out_specs=[blk, blk],
        out_shape=[jax.ShapeDtypeStruct((t, MLA_HEADS * LANES), F32)] * 2,
        compiler_params=pltpu.CompilerParams(dimension_semantics=("arbitrary", "arbitrary")),
    )(qn, qr, kn, kr, v)


def _mla_bwd(qn, qr, kn, kr, v, o, lse, do):
    t = qn.shape[0]
    nq = t // ATT_BLK
    scale = MLA_QK ** -0.5

    def body(qn_ref, qr_ref, kn_ref, kr_ref, v_ref, o_ref, lse_ref, do_ref,
             dqn_ref, dqr_ref, dkn_ref, dkr_ref, dv_ref):
        pair = pl.program_id(0)
        qi = pl.program_id(1)

        @pl.when(qi == 0)
        def _():
            dkn_ref[...] = jnp.zeros_like(dkn_ref)
            dv_ref[...] = jnp.zeros_like(dv_ref)

        @pl.when((qi == 0) & (pair == 0))
        def _():
            dkr_ref[...] = jnp.zeros_like(dkr_ref)

        lanes = [slice(hh * LANES, (hh + 1) * LANES) for hh in range(2)]
        qnb = [qn_ref[:, sl].astype(BF16) for sl in lanes]
        qrb = [qr_ref[:, sl].astype(BF16) for sl in lanes]
        dob = [do_ref[:, sl].astype(BF16) for sl in lanes]
        delta = [jnp.sum(do_ref[:, sl] * o_ref[:, sl], axis=-1, keepdims=True) for sl in lanes]
        lse_v = [lse_ref[:, hh * LANES:hh * LANES + 1] for hh in range(2)]

        def blocks(kbs, carry, diagonal):
            nb = len(kbs)
            chains = [(b, hh) for b in range(nb) for hh in range(2)]
            offs = [pl.multiple_of(kb * ATT_BLK, ATT_BLK) for kb in kbs]
            krbs = [kr_ref[pl.ds(off, ATT_BLK), :].astype(BF16) for off in offs]
            knb = {(b, hh): kn_ref[pl.ds(offs[b], ATT_BLK), lanes[hh]].astype(BF16) for b, hh in chains}
            vb = {(b, hh): v_ref[pl.ds(offs[b], ATT_BLK), lanes[hh]].astype(BF16) for b, hh in chains}
            ss = {(b, hh): _nt(qnb[hh], knb[(b, hh)]) + _nt(qrb[hh], krbs[b]) for b, hh in chains}
            dps = {(b, hh): _nt(dob[hh], vb[(b, hh)]) for b, hh in chains}
            ps = {(b, hh): jnp.exp(ss[(b, hh)] * scale - lse_v[hh]) for b, hh in chains}
            if any(diagonal):
                causal = (lax.broadcasted_iota(jnp.int32, (ATT_BLK, ATT_BLK), 1)
                          <= lax.broadcasted_iota(jnp.int32, (ATT_BLK, ATT_BLK), 0))
                ps = {ch: jnp.where(causal, ps[ch], 0.0) if diagonal[ch[0]] else ps[ch] for ch in chains}
            dss = {(b, hh): (ps[(b, hh)] * (dps[(b, hh)] - delta[hh]) * scale).astype(BF16) for b, hh in chains}
            for b, hh in chains:
                dv_ref[pl.ds(offs[b], ATT_BLK), lanes[hh]] += _tn(ps[(b, hh)].astype(BF16), dob[hh])
            for b, hh in chains:
                dkn_ref[pl.ds(offs[b], ATT_BLK), lanes[hh]] += _tn(dss[(b, hh)], qnb[hh])
            for b in range(nb):
                dkr_ref[pl.ds(offs[b], ATT_BLK), :] += _tn(dss[(b, 0)], qrb[0]) + _tn(dss[(b, 1)], qrb[1])
            out = list(carry)
            for b, hh in chains:
                out[2 * hh] = out[2 * hh] + jnp.dot(dss[(b, hh)], knb[(b, hh)], preferred_element_type=F32)
                out[2 * hh + 1] = out[2 * hh + 1] + jnp.dot(dss[(b, hh)], krbs[b], preferred_element_type=F32)
            return tuple(out)

        zero = jnp.zeros((ATT_BLK, LANES), F32)
        carry = lax.fori_loop(0, qi // 2, lambda pr, cr: blocks([2 * pr, 2 * pr + 1], cr, (False, False)),
                              (zero, zero, zero, zero))
        carry = lax.cond(qi % 2 == 1, lambda cr: blocks([qi - 1, qi], cr, (False, True)),
                         lambda cr: blocks([qi], cr, (True,)), carry)
        for hh in range(2):
            dqn_ref[:, lanes[hh]] = carry[2 * hh]
            dqr_ref[:, lanes[hh]] = carry[2 * hh + 1]

    blk = pl.BlockSpec((ATT_BLK, 2 * LANES), lambda p, i: (i, p))
    full = pl.BlockSpec((t, 2 * LANES), lambda p, i: (0, p))
    shared = pl.BlockSpec((t, LANES), lambda p, i: (0, 0))
    wide = jax.ShapeDtypeStruct((t, MLA_HEADS * LANES), F32)
    return pl.pallas_call(
        body, name="mla_attn_bwd", grid=(MLA_HEADS // 2, nq),
        in_specs=[blk, blk, full, shared, full, blk, blk, blk],
        out_specs=[blk, blk, full, shared, full],
        out_shape=[wide, wide, wide, jax.ShapeDtypeStruct((t, LANES), F32), wide],
        compiler_params=pltpu.CompilerParams(dimension_semantics=("arbitrary", "arbitrary")),
    )(qn, qr, kn, kr, v, o, lse, do)


@jax.custom_vjp
def _mla_attention(qn, qr, kn, kr, v):
    return _mla_fwd(qn, qr, kn, kr, v)[0]


def _mla_attention_fwd(qn, qr, kn, kr, v):
    o, lse = _mla_fwd(qn, qr, kn, kr, v)
    return o, (qn, qr, kn, kr, v, o, lse)


def _mla_attention_bwd(res, do):
    return tuple(_mla_bwd(*res, do))


_mla_attention.defvjp(_mla_attention_fwd, _mla_attention_bwd)


def _ffn_in(h, wg, wu):
    t, k = h.shape
    n_sh, cc, _ = wg.shape

    def body(h_ref, wg_ref, wu_ref, g_ref, u_ref, a_ref):
        hb = h_ref[...].astype(BF16)
        for j in range(n_sh):
            cols = slice(j * cc, (j + 1) * cc)
            g = _nt(hb, wg_ref[j])
            u = _nt(hb, wu_ref[j])
            g_ref[:, cols] = g.astype(BF16)
            u_ref[:, cols] = u.astype(BF16)
            a_ref[:, cols] = _f_swiglu(g, u)[0].astype(BF16)

    w_spec = pl.BlockSpec((n_sh, cc, k), lambda i: (0, 0, 0))
    o_spec = pl.BlockSpec((MM_ROW_TILE, n_sh * cc), lambda i: (i, 0))
    wide = jax.ShapeDtypeStruct((t, n_sh * cc), BF16)
    return pl.pallas_call(
        body, name="ffn_in_fwd", grid=(t // MM_ROW_TILE,),
        in_specs=[pl.BlockSpec((MM_ROW_TILE, k), lambda i: (i, 0)), w_spec, w_spec],
        out_specs=[o_spec, o_spec, o_spec],
        out_shape=[wide, wide, wide],
        compiler_params=pltpu.CompilerParams(dimension_semantics=("arbitrary",), vmem_limit_bytes=MM_VMEM_LIMIT),
    )(h, wg, wu)


def _ffn_mid_bwd(dy, wd, g, u):
    t, n = dy.shape
    n_sh, cc, _ = wd.shape

    def body(dy_ref, wd_ref, g_ref, u_ref, dg_ref, du_ref):
        d_act = _nt(dy_ref[...].astype(BF16), wd_ref[...])
        g = g_ref[...].astype(F32)
        sig = 1.0 / (1.0 + jnp.exp(-g))
        dg_ref[...] = (d_act * u_ref[...].astype(F32) * (sig * (1.0 + g * (1.0 - sig)))).astype(BF16)
        du_ref[...] = (d_act * (g * sig)).astype(BF16)

    blk = pl.BlockSpec((MM_ROW_TILE, cc), lambda j, i: (i, j))
    wide = jax.ShapeDtypeStruct((t, n_sh * cc), BF16)
    return pl.pallas_call(
        body, name="ffn_mid_bwd", grid=(n_sh, t // MM_ROW_TILE),
        in_specs=[pl.BlockSpec((MM_ROW_TILE, n), lambda j, i: (i, 0)),
                  pl.BlockSpec((None, cc, n), lambda j, i: (j, 0, 0)), blk, blk],
        out_specs=[blk, blk], out_shape=[wide, wide],
        compiler_params=pltpu.CompilerParams(dimension_semantics=("arbitrary", "arbitrary"),
                                             vmem_limit_bytes=MM_VMEM_LIMIT),
    )(dy, wd, g, u)


def _ffn_dh(dg, du, wg, wu):
    t = dg.shape[0]
    n_sh, cc, k = wg.shape

    def body(dg_ref, du_ref, wg_ref, wu_ref, o_ref):
        acc = jnp.zeros((MM_ROW_TILE, k), F32)
        for j in range(n_sh):
            cols = slice(j * cc, (j + 1) * cc)
            acc = (acc + jnp.dot(dg_ref[:, cols], wg_ref[j], preferred_element_type=F32)
                   + jnp.dot(du_ref[:, cols], wu_ref[j], preferred_element_type=F32))
        o_ref[...] = acc

    blk = pl.BlockSpec((MM_ROW_TILE, n_sh * cc), lambda i: (i, 0))
    w_spec = pl.BlockSpec((n_sh, cc, k), lambda i: (0, 0, 0))
    return pl.pallas_call(
        body, name="ffn_dh", grid=(t // MM_ROW_TILE,),
        in_specs=[blk, blk, w_spec, w_spec],
        out_specs=pl.BlockSpec((MM_ROW_TILE, k), lambda i: (i, 0)),
        out_shape=jax.ShapeDtypeStruct((t, k), F32),
        compiler_params=pltpu.CompilerParams(dimension_semantics=("arbitrary",), vmem_limit_bytes=MM_VMEM_LIMIT),
    )(dg, du, wg, wu)


def _ffn_dw_in(h, dg, du, n_sh):
    t, k = h.shape
    cc = dg.shape[1] // n_sh
    tk = 512

    def body(h_ref, dg_ref, du_ref, og_ref, ou_ref):
        hb = h_ref[...].astype(BF16)
        og_ref[...] = _tn(dg_ref[...], hb).astype(BF16)
        ou_ref[...] = _tn(du_ref[...], hb).astype(BF16)

    d_spec = pl.BlockSpec((t, cc), lambda i, j: (0, j))
    o_spec = pl.BlockSpec((None, cc, tk), lambda i, j: (j, 0, i))
    out = jax.ShapeDtypeStruct((n_sh, cc, k), BF16)
    return pl.pallas_call(
        body, name="ffn_gate_up_dw", grid=(k // tk, n_sh),
        in_specs=[pl.BlockSpec((t, tk), lambda i, j: (0, i)), d_spec, d_spec],
        out_specs=[o_spec, o_spec], out_shape=[out, out],
        compiler_params=pltpu.CompilerParams(dimension_semantics=("arbitrary", "arbitrary"),
                                             vmem_limit_bytes=MM_VMEM_LIMIT),
    )(h, dg, du)


@jax.custom_vjp
def _ffn_block(h, wg, wu, wd):
    act = _ffn_in(h, wg, wu)[2]
    return _mm(act, wd.reshape(-1, wd.shape[2]), "nn", "ffn_down_fwd", MM_ROW_TILE, wd.shape[2])


def _ffn_block_fwd(h, wg, wu, wd):
    g, u, act = _ffn_in(h, wg, wu)
    y = _mm(act, wd.reshape(-1, wd.shape[2]), "nn", "ffn_down_fwd", MM_ROW_TILE, wd.shape[2])
    return y, (h, wg, wu, wd, g, u, act)


def _ffn_block_bwd(res, dy):
    h, wg, wu, wd, g, u, act = res
    dg, du = _ffn_mid_bwd(dy, wd, g, u)
    dh = _ffn_dh(dg, du, wg, wu)
    n_sh = wg.shape[0]
    dwg, dwu = _ffn_dw_in(h, dg, du, n_sh)
    dwd = _mm(act, dy, "tn", "ffn_down_dw", 256, wd.shape[2], out_dtype=BF16).reshape(wd.shape)
    return dh, dwg, dwu, dwd


_ffn_block.defvjp(_ffn_block_fwd, _ffn_block_bwd)


def _swap_halves(w):
    half = w.shape[-1] // 2
    return jnp.concatenate([w[..., half:], w[..., :half]], axis=-1)


def _pad_lanes(w):
    return jnp.concatenate([w, jnp.zeros(w.shape[:-1] + (LANES - w.shape[-1],), w.dtype)], axis=-1)


def _join_cols(shards):
    return shards.transpose(1, 0, 2).reshape(shards.shape[1], -1)


def _mod_parts(mod):
    return [mod[:, i * D_MODEL:(i + 1) * D_MODEL] for i in range(N_MOD)]


def _mixing_stage(x, mod, p, cos, sin):
    shift1, scale1 = _mod_parts(mod)[:2]

    w_in_t = p["w_in"].reshape(-1, D_MODEL)
    k_rope_rows = w_in_t[2176:2240]

    def pad_rows(a):
        return jnp.concatenate([a, jnp.zeros((LANES - a.shape[0], D_MODEL), a.dtype)], axis=0)

    swapped = jnp.concatenate([k_rope_rows[MLA_ROPE // 2:], k_rope_rows[:MLA_ROPE // 2]], axis=0)
    w_in_ext = jnp.concatenate([w_in_t[:2176], pad_rows(k_rope_rows), pad_rows(swapped),
                                jnp.zeros((LANES, D_MODEL), w_in_t.dtype)], axis=0)
    h1, x_res = _make_rowwise("pre_attn", _f_pre_attn, 1, 3, [D_MODEL, D_MODEL], [True], out_dtypes=[BF16, F32])(
        x, p["norm_attn"], scale1, shift1)
    q_sb, k_sb, v_sb, cq, ckv, kr, kr_sw = _make_linear_split_t(
        "in_proj", (SB_WIDTH, SB_WIDTH, SB_WIDTH, MLA_Q_RANK, MLA_KV_RANK, LANES, LANES), 512)(h1, w_in_ext)

    o_sb = _sb_attention(q_sb, k_sb, v_sb)

    wq = _join_cols(p["w_q_up"]).reshape(MLA_Q_RANK, MLA_HEADS, MLA_QK)
    wq_n, wq_r = wq[:, :, :MLA_NOPE], wq[:, :, MLA_NOPE:]
    w_q_ext = jnp.concatenate([wq_n.reshape(MLA_Q_RANK, -1), _pad_lanes(wq_r).reshape(MLA_Q_RANK, -1),
                               _pad_lanes(_swap_halves(wq_r)).reshape(MLA_Q_RANK, -1)], axis=1)
    wkv = _join_cols(p["w_kv_up"]).reshape(MLA_KV_RANK, MLA_HEADS, MLA_NOPE + MLA_V)
    w_kv_ext = jnp.concatenate([wkv[:, :, :MLA_NOPE].reshape(MLA_KV_RANK, -1),
                                wkv[:, :, MLA_NOPE:].reshape(MLA_KV_RANK, -1)], axis=1)
    cqn, ckvn = _make_rowwise("mla_a", _f_mla_a, 2, 2, [MLA_Q_RANK, MLA_KV_RANK], [True, True],
                              out_dtypes=[BF16, BF16], grad_dtypes=[BF16, BF16])(
        cq, ckv, p["q_a_norm"], p["kv_a_norm"])
    qall = _make_linear("q_up", 384, 768)(cqn, w_q_ext)
    kn_all, v_mla = _make_linear_split("kv_up", (MLA_HEADS * MLA_NOPE, MLA_HEADS * MLA_V), MLA_KV_RANK)(ckvn, w_kv_ext)
    gq = p["q_norm"]
    gkr = p["k_rope_norm"]
    qn, qr, kn, krr = _make_rowwise("mla_b", _f_mla_b, 6, 6, [512, 512, 512, LANES],
                                    [True, True, True, True, False, False],
                                    out_dtypes=[BF16] * 4, grad_dtypes=[BF16] * 4)(
        qall, kn_all, kr, kr_sw, cos, sin,
        gq[:, :MLA_NOPE], _pad_lanes(gq[:, MLA_NOPE:]), _pad_lanes(_swap_halves(gq[:, MLA_NOPE:])),
        p["k_nope_norm"], _pad_lanes(gkr), _pad_lanes(_swap_halves(gkr)))
    o_mla = _mla_attention(qn, qr, kn, krr, v_mla)

    (mixed,) = _make_rowwise("post_attn", _f_post_attn, 2, 2, [D_MODEL], [True, True])(
        o_sb, o_mla, p["out_norm_sb"], p["out_norm_mla"])
    return mixed, x_res


def _ffn_stage(x, mixed, mod, p):
    _, _, gate1, shift2, scale2, _ = _mod_parts(mod)
    attn = _make_linear("out_proj", 512, 512)(mixed, p["w_out"].reshape(D_MODEL, D_MODEL))

    x2, h2 = _make_rowwise("pre_ffn", _f_pre_ffn, 2, 4, [D_MODEL, D_MODEL], [True, True],
                           out_dtypes=[F32, BF16], grad_dtypes=[F32, BF16])(
        x, attn, gate1, p["norm_ffn"], scale2, shift2)
    return x2, _ffn_block(h2, p["w_gate"], p["w_up"], p["w_down"])


def _my_place():
    return lax.axis_index("x"), lax.axis_index("y"), lax.axis_index("c")


def _small_gather(x_ref, out_ref, send_sems, recv_sems, base, local_sem):
    m_per = x_ref.shape[0]
    x, y, c = _my_place()
    me, sibling = (x, y, c), (x, y, 1 - c)
    chips = [(1 - x, y), (x, 1 - y), (1 - x, 1 - y)]

    def rows(px, py, pc):
        return out_ref.at[pl.ds((4 * px + 2 * py + pc) * m_per, m_per), :]

    def copy(k, blk, to, src=None):
        return _remote(rows(*blk) if src is None else src, rows(*blk), send_sems, recv_sems, base + k, to)

    mine = pltpu.make_async_copy(x_ref, rows(*me), local_sem)
    first = [copy(0, me, sibling, src=x_ref)] + [copy(1 + j, me, (*chip, c), src=x_ref) for j, chip in enumerate(chips)]
    passed = [copy(4 + j, (*chip, c), sibling) for j, chip in enumerate(chips)]

    def start():
        mine.start()
        for cp in first:
            cp.start()

    def finish():
        for j, chip in enumerate(chips):
            copy(1 + j, (*chip, c), me).wait_recv()
            passed[j].start()
        copy(0, sibling, me).wait_recv()
        for j, chip in enumerate(chips):
            copy(4 + j, (*chip, 1 - c), me).wait_recv()
        for cp in first + passed:
            cp.wait_send()
        mine.wait()

    return start, finish


EARLY =("w_in", "w_q_up", "w_kv_up")
LATE = ("w_out", "w_gate", "w_up", "w_down")
BIG = EARLY + LATE
TRAVELS_TRANSPOSED = ("w_in", "w_gate", "w_up")
HALF_AXIS = {"w_in": 1, "w_q_up": 0, "w_kv_up": 0, "w_out": 0, "w_gate": 1, "w_up": 1, "w_down": 1}


def _half(ref, h, axis, lead=()):
    trail = ref.shape[len(lead):]
    idx = list(lead) + [slice(None)] * len(trail)
    at = len(trail) - 2 + axis
    n2 = trail[at] // 2
    idx[len(lead) + at] = pl.ds(h * n2, n2)
    return ref.at[tuple(idx)]


def _half_shape(shape, axis):
    shape = list(shape)
    shape[len(shape) - 2 + axis] //= 2
    return tuple(shape)


def _remote(src, dst, send_sems, recv_sems, k, to):
    return pltpu.make_async_remote_copy(src_ref=src, dst_ref=dst, send_sem=send_sems.at[k],
                                        recv_sem=recv_sems.at[k], device_id=to, device_id_type=MESH)


def _gather_weights(names, shards, lands, small_block, w_proj, b_proj, pos_col, freqs, sign):
    n_w = len(shards)
    axes = [HALF_AXIS[n] for n in names]
    n_vec, n_proj = w_proj.shape
    base_small, base_proj = 6 * n_w, 6 * n_w + 7

    def body(*refs):
        w_refs, small_ref, wp_ref, b_ref = refs[:n_w], refs[2 * n_w], refs[2 * n_w + 1], refs[2 * n_w + 2]
        pos_ref, freqs_ref, sign_ref = refs[2 * n_w + 3:2 * n_w + 6]
        o = 2 * n_w + 6
        out_refs, token, small_out, proj_out = refs[o:o + n_w], refs[o + n_w], refs[o + n_w + 1], refs[o + n_w + 2]
        cos_ref, sin_ref = refs[o + n_w + 3:o + n_w + 5]
        send_sems, recv_sems, local_sems, wp_vmem, vec_vmem, mine_vmem = refs[o + n_w + 5:]
        token[...] = jnp.zeros_like(token)
        x, y, c = _my_place()
        sibling = (x, y, 1 - c)
        chips = [(1 - x, y), (x, 1 - y), (1 - x, 1 - y)]
        me = 2 * x + y
        small_start, small_finish = _small_gather(small_ref, small_out, send_sems, recv_sems, base_small,
                                                  local_sems.at[0])
        small_start()
        first = [_remote(_half(w_refs[i], c, axes[i]), _half(out_refs[i], c, axes[i], (me,)),
                         send_sems, recv_sems, 6 * i + j, (*chip, c))
                 for i in range(n_w) for j, chip in enumerate(chips)]
        for cp in first:
            cp.start()
        load_w = pltpu.make_async_copy(wp_ref, wp_vmem, local_sems.at[2])
        load_w.start()
        for r in range(0, pos_ref.shape[0], ROW_TILE):
            rows = pl.ds(r, ROW_TILE)
            ang = pos_ref[rows, :].astype(F32) * freqs_ref[...]
            cos_ref[rows, :] = jnp.cos(ang) * jnp.abs(sign_ref[...])
            sin_ref[rows, :] = jnp.sin(ang) * sign_ref[...]
        small_finish()
        load_vec = pltpu.make_async_copy(small_out, vec_vmem, local_sems.at[3])
        load_vec.start()
        load_vec.wait()
        load_w.wait()
        proj = jnp.dot(_silu(vec_vmem[...]), wp_vmem[...], precision=lax.Precision.HIGHEST,
                       preferred_element_type=F32) + b_ref[...]
        for d in range(N_DEV):
            mine_vmem[d:d + 1, :] = proj[8 * d:8 * d + 1, :]
        proj_start, proj_finish = _small_gather(mine_vmem, proj_out, send_sems, recv_sems, base_proj, local_sems.at[1])
        proj_start()
        passed = []
        for j, (cx, cy) in enumerate(chips):
            for i in range(n_w):
                blk = _half(out_refs[i], c, axes[i], (2 * cx + cy,))
                _remote(blk, blk, send_sems, recv_sems, 6 * i + j, (cx, cy, c)).wait_recv()
                cp = _remote(blk, blk, send_sems, recv_sems, 6 * i + 3 + j, sibling)
                cp.start()
                passed.append(cp)
        proj_finish()
        for j, (cx, cy) in enumerate(chips):
            for i in range(n_w):
                blk = _half(out_refs[i], 1 - c, axes[i], (2 * cx + cy,))
                _remote(blk, blk, send_sems, recv_sems, 6 * i + 3 + j, sibling).wait_recv()
        for cp in first + passed:
            cp.wait_send()

    n_rows = N_DEV * small_block.shape[0]
    outs = pl.pallas_call(
        body, name="gather_weights",
        out_shape=[jax.ShapeDtypeStruct(a.shape, a.dtype) for a in lands]
        + [jax.ShapeDtypeStruct((8, LANES), F32), jax.ShapeDtypeStruct((n_rows, n_vec), small_block.dtype),
           jax.ShapeDtypeStruct((N_DEV * N_DEV, n_proj), F32)]
        + [jax.ShapeDtypeStruct((pos_col.shape[0], LANES), F32)] * 2,
        in_specs=[ANY] * (2 * n_w + 2) + [pl.BlockSpec(memory_space=pltpu.VMEM)] * 4,
        out_specs=[ANY] * n_w + [pl.BlockSpec(memory_space=pltpu.VMEM), ANY, ANY] + [pl.BlockSpec(memory_space=pltpu.VMEM)] * 2,
        input_output_aliases={n_w + i: i for i in range(n_w)},
        scratch_shapes=[pltpu.SemaphoreType.DMA((6 * n_w + 14,)), pltpu.SemaphoreType.DMA((6 * n_w + 14,)),
                        pltpu.SemaphoreType.DMA((4,)), pltpu.VMEM((n_vec, n_proj), F32), pltpu.VMEM((n_rows, n_vec), F32),
                        pltpu.VMEM((N_DEV, n_proj), F32)],
        compiler_params=pltpu.CompilerParams(vmem_limit_bytes=MM_VMEM_LIMIT),
    )(*shards, *lands, small_block, w_proj, b_proj, pos_col, freqs, sign)
    return outs[:n_w], outs[n_w], outs[n_w + 1], outs[n_w + 2], outs[n_w + 3], outs[n_w + 4]


def _pair_exchange(names, grads, call_name, small_block):
    n_w = len(grads)
    axes = [HALF_AXIS[n] for n in names]

    def body(*refs):
        g_refs, small_ref = refs[:n_w], refs[n_w]
        t_refs, small_out = refs[n_w + 1:2 * n_w + 1], refs[2 * n_w + 1]
        send_sems, recv_sems, local_sem = refs[2 * n_w + 2:]
        x, y, c = _my_place()
        small_start, small_finish = _small_gather(small_ref, small_out, send_sems, recv_sems, n_w, local_sem)
        small_start()
        sends = [_remote(_half(g_refs[i], 1 - c, axes[i]), t_refs[i], send_sems, recv_sems, i, (x, y, 1 - c))
                 for i in range(n_w)]
        for cp in sends:
            cp.start()
        small_finish()
        for cp in sends:
            cp.wait_recv()
        for cp in sends:
            cp.wait_send()

    outs = pl.pallas_call(
        body, name=call_name,
        out_shape=[jax.ShapeDtypeStruct(_half_shape(g.shape, a), g.dtype) for g, a in zip(grads, axes)]
        + [jax.ShapeDtypeStruct((N_DEV * small_block.shape[0], small_block.shape[1]), small_block.dtype)],
        in_specs=[ANY] * (n_w + 1), out_specs=[ANY] * (n_w + 1),
        scratch_shapes=[pltpu.SemaphoreType.DMA((n_w + 7,)), pltpu.SemaphoreType.DMA((n_w + 7,)),
                        pltpu.SemaphoreType.DMA],
    )(*grads, small_block)
    return outs[:n_w], outs[n_w]


def _sibling_join(halves, name, after):
    n_w = len(halves)

    def body(*refs):
        s_refs, j_refs = refs[:n_w], refs[n_w + 1:2 * n_w + 1]
        send_sems, recv_sems = refs[2 * n_w + 1:]
        x, y, c = _my_place()
        sends = [_remote(s_refs[i], j_refs[i], send_sems, recv_sems, i, (x, y, 1 - c)) for i in range(n_w)]
        for cp in sends:
            cp.start()
        for cp in sends:
            cp.wait_recv()
        for cp in sends:
            cp.wait_send()

    return pl.pallas_call(
        body, name=name,
        out_shape=[jax.ShapeDtypeStruct(s.shape, s.dtype) for s in halves],
        in_specs=[ANY] * (n_w + 1), out_specs=[ANY] * n_w,
        scratch_shapes=[pltpu.SemaphoreType.DMA((n_w,)), pltpu.SemaphoreType.DMA((n_w,))],
    )(*halves, after)


HBM_SPEC = pl.BlockSpec(memory_space=pltpu.HBM)
SEM_SPEC = pl.BlockSpec(memory_space=pltpu.SEMAPHORE)
DATAFLOW = pltpu.SideEffectType.DATAFLOW_SIDE_EFFECTING


def _in_hbm(a):
    return pltpu.with_memory_space_constraint(a, pltpu.HBM)


def _exchange_start(name, srcs, lands, plan, n_copies, after, thru):
    n = len(srcs)

    def body(*refs):
        src_refs, land_refs = refs[:n], refs[n:2 * n]
        send_sems, recv_sems = refs[2 * n + 2], refs[2 * n + 3]
        for k, (src, dst, to, k_recv) in enumerate(plan(src_refs, land_refs)):
            pltpu.make_async_remote_copy(src_ref=src, dst_ref=dst, send_sem=send_sems.at[k],
                                         recv_sem=recv_sems.at[k_recv], device_id=to, device_id_type=MESH).start()

    outs = pl.pallas_call(
        body, name=name,
        out_shape=(pltpu.SemaphoreType.DMA((n_copies,)), pltpu.SemaphoreType.DMA((n_copies,)),
                   *[pltpu.HBM(a.shape, a.dtype) for a in list(srcs) + list(lands) + [thru]]),
        in_specs=[HBM_SPEC] * (2 * n + 1) + [ANY],
        out_specs=(SEM_SPEC, SEM_SPEC, *[HBM_SPEC] * (2 * n + 1)),
        input_output_aliases={i: 2 + i for i in range(2 * n + 1)},
        compiler_params=pltpu.CompilerParams(has_side_effects=DATAFLOW),
    )(*[_in_hbm(a) for a in list(srcs) + list(lands) + [thru]], after)
    return outs[0], outs[1], outs[2:2 + n], outs[2 + n:2 + 2 * n], outs[2 + 2 * n]


def _exchange_wait(name, started, plan, after):
    send_sems, recv_sems, srcs, lands, _ = started
    n = len(srcs)

    def body(*refs):
        src_refs, land_refs = refs[:n], refs[n:2 * n]
        s_sems, r_sems = refs[2 * n], refs[2 * n + 1]
        for k, (src, dst, to, _) in enumerate(plan(src_refs, land_refs)):
            cp = _remote(src, dst, s_sems, r_sems, k, to)
            cp.wait_send()
            cp.wait_recv()

    outs = pl.pallas_call(
        body, name=name,
        out_shape=tuple(pltpu.HBM(a.shape, a.dtype) for a in list(srcs) + list(lands)),
        in_specs=[HBM_SPEC] * (2 * n) + [SEM_SPEC, SEM_SPEC, ANY],
        out_specs=tuple([HBM_SPEC] * (2 * n)),
        input_output_aliases={i: i for i in range(2 * n)},
        compiler_params=pltpu.CompilerParams(has_side_effects=DATAFLOW),
    )(*srcs, *lands, send_sems, recv_sems, after)
    return outs[:n], outs[n:]


def _late_gather_plan(src_refs, land_refs):
    x, y, c = _my_place()
    chips = [(1 - x, y), (x, 1 - y), (1 - x, 1 - y)]
    plan = [(src, land.at[2 * x + y], (cx, cy, c)) for src, land in zip(src_refs, land_refs) for cx, cy in chips]
    return [entry + (k,) for k, entry in enumerate(plan)]


def _late_scatter_plan(src_refs, land_refs):
    x, y, c = _my_place()
    chips = [(1 - x, y), (x, 1 - y), (1 - x, 1 - y)]
    plan = [(src.at[2 * cx + cy], land.at[j], (cx, cy, c))
            for src, land in zip(src_refs, land_refs) for j, (cx, cy) in enumerate(chips)]
    return [entry + (k,) for k, entry in enumerate(plan)]


def _direct_scatter_plan(names):
    axes = [HALF_AXIS[n] for n in names]

    def plan(src_refs, land_refs):
        x, y, c = _my_place()
        chips = [(1 - x, y), (x, 1 - y), (1 - x, 1 - y)]
        out = []
        for i, (src, land) in enumerate(zip(src_refs, land_refs)):
            for f, (cx, cy) in enumerate(chips):
                for core in range(2):
                    out.append((_half(src, core, axes[i], (2 * cx + cy,)), land.at[2 * f + c], (cx, cy, core),
                                7 * i + 2 * f + c))
            out.append((_half(src, 1 - c, axes[i], (2 * x + y,)), land.at[6], (x, y, 1 - c), 7 * i + 6))
        return out

    return plan


def _row_tile(rows, mult=16, limit=ROW_TILE):
    return max(d for d in range(mult, limit + 1, mult) if rows % d == 0)


def _pair_sum(place, g, theirs, axis, name):
    nj, rr, cc = theirs.shape
    tr = _row_tile(rr, limit=1024)
    nb = rr // tr
    if axis == 0:
        g_map = lambda j, i, pr: (j, pr[0] * nb + i, 0)
    else:
        g_map = lambda j, i, pr: (j, i, pr[0])

    def body(pr, g_ref, t_ref, o_ref):
        o_ref[...] = (g_ref[...].astype(F32) + t_ref[...].astype(F32)).astype(BF16)

    spec = pl.BlockSpec((None, tr, cc), lambda j, i, pr: (j, i, 0))
    return pl.pallas_call(
        body, name=name,
        grid_spec=pltpu.PrefetchScalarGridSpec(
            num_scalar_prefetch=1, grid=(nj, nb),
            in_specs=[pl.BlockSpec((None, tr, cc), g_map), spec], out_specs=spec),
        out_shape=jax.ShapeDtypeStruct(theirs.shape, BF16))(place, g, theirs)


def _chip_sum(place, pair_sums, parts, name):
    _, rr, cc = parts.shape
    tr = _row_tile(rr, limit=1024)

    def body(pr, h_ref, p_ref, o_ref):
        acc = p_ref[0].astype(F32)
        for j in range(1, N_CHIPS - 1):
            acc = acc + p_ref[j].astype(F32)
        o_ref[...] = (acc + h_ref[...].astype(F32)).astype(BF16)

    return pl.pallas_call(
        body, name=name,
        grid_spec=pltpu.PrefetchScalarGridSpec(
            num_scalar_prefetch=1, grid=(rr // tr,),
            in_specs=[pl.BlockSpec((None, tr, cc), lambda i, pr: (pr[1], i, 0)),
                      pl.BlockSpec((N_CHIPS - 1, tr, cc), lambda i, pr: (0, i, 0))],
            out_specs=pl.BlockSpec((tr, cc), lambda i, pr: (i, 0))),
        out_shape=jax.ShapeDtypeStruct((rr, cc), BF16))(place, pair_sums, parts)


def _chip_sum_direct(place, g, parts, axis, name):
    n_parts, rr, cc = parts.shape
    tr = _row_tile(rr, limit=1024)
    nb = rr // tr
    if axis == 0:
        g_map = lambda i, pr: (pr[1], pr[0] * nb + i, 0)
    else:
        g_map = lambda i, pr: (pr[1], i, pr[0])

    def body(pr, g_ref, p_ref, o_ref):
        acc = p_ref[0].astype(F32)
        for j in range(1, n_parts):
            acc = acc + p_ref[j].astype(F32)
        o_ref[...] = (acc + g_ref[...].astype(F32)).astype(BF16)

    return pl.pallas_call(
        body, name=name,
        grid_spec=pltpu.PrefetchScalarGridSpec(
            num_scalar_prefetch=1, grid=(nb,),
            in_specs=[pl.BlockSpec((None, tr, cc), g_map), pl.BlockSpec((n_parts, tr, cc), lambda i, pr: (0, i, 0))],
            out_specs=pl.BlockSpec((tr, cc), lambda i, pr: (i, 0))),
        out_shape=jax.ShapeDtypeStruct((rr, cc), BF16))(place, g, parts)


def _silu(v):
    return v / (1.0 + jnp.exp(-v))


def _loss_and_grads(x2, ffn, target, gate):
    t, d = x2.shape

    def half_loss(x2_blk, ffn_blk, gate_row, target_blk):
        return 0.5 * _f_loss(x2_blk, ffn_blk, target_blk, gate_row)[0]

    def body(x2_ref, ffn_ref, tgt_ref, gate_ref, loss_ref, dx2_ref, dffn_ref, dgate_ref):
        rows, vjp = jax.vjp(lambda a, b, g: half_loss(a, b, g, tgt_ref[...]), x2_ref[...], ffn_ref[...], gate_ref[...])
        loss_ref[...] = rows
        dx2_ref[...], dffn_ref[...], dgate = vjp(jnp.ones_like(rows))

        @pl.when(pl.program_id(0) == 0)
        def _():
            dgate_ref[...] = jnp.zeros_like(dgate_ref)

        dgate_ref[...] += dgate

    blk = pl.BlockSpec((ROW_TILE, d), lambda i: (i, 0))
    row = pl.BlockSpec((1, d), lambda i: (0, 0))
    return pl.pallas_call(
        body, name="loss_and_grads", grid=(t // ROW_TILE,),
        in_specs=[blk, blk, blk, row],
        out_specs=[pl.BlockSpec((ROW_TILE, 1), lambda i: (i, 0)), blk, blk, row],
        out_shape=[jax.ShapeDtypeStruct((t, 1), F32), jax.ShapeDtypeStruct((t, d), F32), jax.ShapeDtypeStruct((t, d), F32),
                   jax.ShapeDtypeStruct((1, d), F32)],
        compiler_params=pltpu.CompilerParams(dimension_semantics=("arbitrary",), vmem_limit_bytes=MM_VMEM_LIMIT),
    )(x2, ffn, target, gate)


def _ada_bwd(c_all, dmod_cols):
    def body(c_ref, d_ref, o_ref):
        o_ref[...] = lax.dot_general(_silu(c_ref[...]), d_ref[...], (((0,), (0,)), ((), ())),
                                     precision=lax.Precision.HIGHEST, preferred_element_type=F32)

    return pl.pallas_call(body, name="ada_bwd", out_shape=jax.ShapeDtypeStruct((c_all.shape[1], dmod_cols.shape[1]), F32),
                          compiler_params=pltpu.CompilerParams(vmem_limit_bytes=MM_VMEM_LIMIT))(c_all, dmod_cols)


def _adamw_math(w, g, m, v):
    m = ADAM_B1 * m + (1.0 - ADAM_B1) * g
    v = ADAM_B2 * v + (1.0 - ADAM_B2) * (g * g)
    m_hat = m / (1.0 - ADAM_B1 ** ADAM_STEP)
    v_hat = v / (1.0 - ADAM_B2 ** ADAM_STEP)
    delta = -ADAM_LR * (m_hat / (jnp.sqrt(v_hat) + ADAM_EPS) + ADAM_WD * w)
    return delta, m, v


def _adamw(w, g, m, v, name):
    r, ccols = w.shape
    tr = max(d for d in range(8, ROW_TILE + 1, 8) if r % d == 0)
    spec = pl.BlockSpec((tr, ccols), lambda i: (i, 0))

    def body(w_ref, g_ref, m_ref, v_ref, d_ref, nm_ref, nv_ref):
        d_ref[...], nm_ref[...], nv_ref[...] = _adamw_math(w_ref[...], g_ref[...], m_ref[...], v_ref[...])

    return pl.pallas_call(body, name=name, grid=(r // tr,), in_specs=[spec] * 4, out_specs=[spec] * 3,
                          out_shape=[jax.ShapeDtypeStruct(w.shape, F32)] * 3,
                          compiler_params=pltpu.CompilerParams(vmem_limit_bytes=MM_VMEM_LIMIT))(w, g, m, v)


def _small_layout(sizes):
    offs, off = [], 0
    for n in sizes:
        offs.append(off)
        off += -(-n // LANES) * LANES
    total = -(-(off + LANES) // (8 * LANES)) * (8 * LANES)
    return offs, off, total


def _adamw_small(ws, g_all, ms, vs, offs, loss_off):
    n_p = len(ws)

    def device_sum(g_ref, off, width):
        blk = g_ref[:, off:off + width]
        acc = blk[0:1]
        for d in range(1, N_DEV):
            acc = acc + blk[d:d + 1]
        return acc

    def body(*refs):
        w_refs, m_refs, v_refs = refs[:n_p], refs[n_p:2 * n_p], refs[2 * n_p:3 * n_p]
        g_ref = refs[3 * n_p]
        outs = refs[3 * n_p + 1:]
        for i in range(n_p):
            n = w_refs[i].shape[1]
            g = device_sum(g_ref, offs[i], -(-n // LANES) * LANES)[:, :n]
            outs[i][...] = g
            outs[n_p + i][...], outs[2 * n_p + i][...], outs[3 * n_p + i][...] = _adamw_math(
                w_refs[i][...], g, m_refs[i][...], v_refs[i][...])
        outs[4 * n_p][...] = device_sum(g_ref, loss_off, LANES)

    res = pl.pallas_call(
        body, name="adamw_small",
        out_shape=[jax.ShapeDtypeStruct(a.shape, F32) for a in list(ws) * 4] + [jax.ShapeDtypeStruct((1, LANES), F32)],
    )(*ws, *ms, *vs, g_all)
    return res[:n_p], res[n_p:2 * n_p], res[2 * n_p:3 * n_p], res[3 * n_p:4 * n_p], res[4 * n_p]


def _adamw_halves(place, w, own, sib, m, v, axis, name, after):
    r, cc = w.shape
    if axis == 0:
        rows, gc = own.shape[0], own.shape[1]
        tr = _row_tile(rows)
        nb = rows // tr
        w_spec = pl.BlockSpec((tr, cc), lambda h, i, pr: (h * nb + i, 0))
        g_spec = pl.BlockSpec((tr, gc), lambda h, i, pr: (i, 0))
    else:
        tr = _row_tile(r)
        nb = r // tr
        gc = own.shape[1]
        w_spec = pl.BlockSpec((tr, gc), lambda h, i, pr: (i, h))
        g_spec = pl.BlockSpec((tr, gc), lambda h, i, pr: (i, 0))
    wc = w_spec.block_shape[1]

    def body(pr, w_ref, o_ref, s_ref, m_ref, v_ref, after_ref, g_ref, d_ref, nm_ref, nv_ref):
        g = jnp.where(pl.program_id(0) == pr[0], o_ref[...], s_ref[...]).astype(F32)[:, :wc]
        g_ref[...] = g
        d_ref[...], nm_ref[...], nv_ref[...] = _adamw_math(w_ref[...], g, m_ref[...], v_ref[...])

    return pl.pallas_call(
        body, name=name,
        grid_spec=pltpu.PrefetchScalarGridSpec(
            num_scalar_prefetch=1, grid=(2, nb),
            in_specs=[w_spec, g_spec, g_spec, w_spec, w_spec, ANY], out_specs=[w_spec] * 4),
        out_shape=[jax.ShapeDtypeStruct(w.shape, F32)] * 4,
        compiler_params=pltpu.CompilerParams(vmem_limit_bytes=MM_VMEM_LIMIT))(place, w, own, sib, m, v, after)


SMALL = ("b_ada", "norm_attn", "norm_ffn", "q_a_norm", "kv_a_norm", "q_norm", "k_nope_norm", "k_rope_norm",
         "out_norm_sb", "out_norm_mla")
WEIGHTS = ("w_ada", "b_ada", "norm_attn", "norm_ffn", "w_in", "q_a_norm", "w_q_up", "kv_a_norm", "w_kv_up",
           "q_norm", "k_nope_norm", "k_rope_norm", "out_norm_sb", "out_norm_mla", "w_out", "w_gate", "w_up",
           "w_down")


def kernel(x, c, positions, w_ada, b_ada, norm_attn, norm_ffn, w_in, q_a_norm, w_q_up, kv_a_norm, w_kv_up, q_norm, k_nope_norm, k_rope_norm, out_norm_sb, out_norm_mla, w_out, w_gate, w_up, w_down, loss_target, m_w_ada, m_b_ada, m_norm_attn, m_norm_ffn, m_w_in, m_q_a_norm, m_w_q_up, m_kv_a_norm, m_w_kv_up, m_q_norm, m_k_nope_norm, m_k_rope_norm, m_out_norm_sb, m_out_norm_mla, m_w_out, m_w_gate, m_w_up, m_w_down, v_w_ada, v_b_ada, v_norm_attn, v_norm_ffn, v_w_in, v_q_a_norm, v_w_q_up, v_kv_a_norm, v_w_kv_up, v_q_norm, v_k_nope_norm, v_k_rope_norm, v_out_norm_sb, v_out_norm_mla, v_w_out, v_w_gate, v_w_up, v_w_down):
    local = dict(locals())
    w = {n: local[n][0] for n in WEIGHTS}
    m = {n: local["m_" + n][0] for n in WEIGHTS}
    v = {n: local["v_" + n][0] for n in WEIGHTS}
    small = {n: w[n].reshape(1, -1) for n in SMALL}
    ix, iy, ic = _my_place()
    chip = 2 * ix + iy
    dev = 2 * chip + ic
    xs, target = x[0], loss_target[0]
    seq = xs.shape[0]

    ff_pad = FF_SHARD_PAD - FF_SHARD
    shards = {n: (w[n].T if n in TRAVELS_TRANSPOSED else w[n]).astype(BF16) for n in BIG}
    for n in ("w_gate", "w_up", "w_down"):
        shards[n] = jnp.pad(shards[n], ((0, ff_pad), (0, 0)))
    def landing(names):
        return [lax.dynamic_update_index_in_dim(lax.empty((N_CHIPS,) + shards[n].shape, BF16), shards[n], chip, 0)
                for n in names]

    ada_cols = w["w_ada"].shape[1]
    b_cols = lax.dynamic_slice_in_dim(small["b_ada"], chip * ada_cols, ada_cols, axis=1)
    half = MLA_ROPE // 2
    freqs = 1.0 / (ROPE_THETA ** (np.arange(half, dtype=np.float32) / half))
    zeros = np.zeros(LANES - MLA_ROPE, np.float32)
    freqs_row = jnp.asarray(np.concatenate([freqs, freqs, zeros]).astype(np.float32)[None])
    sign_row = jnp.asarray(np.concatenate([-np.ones(half), np.ones(half), zeros]).astype(np.float32)[None])
    early, early_done, c_gathered, mod_all, cos, sin = _gather_weights(
        EARLY, [shards[n] for n in EARLY], landing(EARLY), jnp.broadcast_to(c.reshape(1, D_MODEL), (8, D_MODEL)),
        w["w_ada"], b_cols, positions.reshape(seq, 1), freqs_row, sign_row)
    gathered = dict(zip(EARLY, early))
    c_all = c_gathered.reshape(N_DEV, 8, D_MODEL)[:, 0]
    mod_all = mod_all.reshape(N_CHIPS, 2, N_DEV, ada_cols)
    mod = lax.dynamic_index_in_dim(mod_all[:, 0], dev, axis=1, keepdims=False).reshape(1, N_MOD * D_MODEL)

    late_gather = _exchange_start("gather_late_start", [shards[n] for n in LATE], landing(LATE), _late_gather_plan,
                                  3 * len(LATE), early_done, mod)
    mod = late_gather[4]

    place = jnp.stack([ic, chip]).astype(jnp.int32)
    small_params = {n: small[n] for n in SMALL if n != "b_ada"}

    p1 = {**{n: gathered[n] for n in EARLY}, **small_params}
    (mixed, x_res), mixing_vjp = jax.vjp(lambda x_, mod_, p_: _mixing_stage(x_, mod_, p_, cos, sin), xs, mod, p1)
    _, landed = _exchange_wait("gather_late_wait", late_gather, _late_gather_plan, mixed)
    p2 = {**dict(zip(LATE, landed)), **small_params}
    (x2, ffn), ffn_vjp = jax.vjp(_ffn_stage, x_res, mixed, mod, p2)
    loss_rows, g_x2, g_ffn, g_gate2 = _loss_and_grads(x2, ffn, target, _mod_parts(mod)[5])
    loss_part = jnp.sum(loss_rows)
    gx2, gmixed, gmod2, gp2 = ffn_vjp((g_x2, g_ffn))
    gmod2 = gmod2 + jnp.concatenate([jnp.zeros((1, (N_MOD - 1) * D_MODEL), F32), g_gate2], axis=1)
    late_grads = [gp2[n] for n in LATE]
    late_plan = _direct_scatter_plan(LATE)
    late_scatter = _exchange_start(
        "grad_scatter_late_start", late_grads,
        [lax.empty((7,) + _half_shape(gr.shape[1:], HALF_AXIS[n]), BF16) for n, gr in zip(LATE, late_grads)],
        late_plan, 7 * len(LATE), gx2, gmixed)
    gx, gmod1, gp1 = mixing_vjp((late_scatter[4], gx2))
    gmod = gmod1 + gmod2
    gp = {n: gp1[n] + gp2[n] for n in small_params}

    sizes = [w[n].size for n in SMALL]
    offs, loss_off, n_small = _small_layout(sizes)
    pieces = []
    for n, size in zip(SMALL, sizes):
        pieces.append(gmod if n == "b_ada" else gp[n])
        if size % LANES:
            pieces.append(jnp.zeros((1, LANES - size % LANES), F32))
    pieces += [jnp.full((1, LANES), loss_part), jnp.zeros((1, n_small - loss_off - LANES), F32)]
    small_vec = jnp.concatenate(pieces, axis=1)

    g, delta, new_m, new_v = {}, {}, {}, {}

    def update(names, own, sib, after):
        for n, o, s in zip(names, own, sib):
            if n in TRAVELS_TRANSPOSED:
                res = _adamw_halves(place, w[n].T, o, s, m[n].T, v[n].T, HALF_AXIS[n], "adamw_" + n, after)
                g[n], delta[n], new_m[n], new_v[n] = [r.T for r in res]
            else:
                g[n], delta[n], new_m[n], new_v[n] = _adamw_halves(place, w[n], o, s, m[n], v[n], HALF_AXIS[n],
                                                                   "adamw_" + n, after)

    late_grads, late_parts = _exchange_wait("grad_scatter_late_wait", late_scatter, late_plan, gx)
    own_late = [_chip_sum_direct(place, gr, pt, HALF_AXIS[n], "grad_chip_sum_" + n)
                for n, gr, pt in zip(LATE, late_grads, late_parts)]
    early_grads = [gp1[n] for n in EARLY]
    theirs, small_gathered = _pair_exchange(EARLY, early_grads, "grad_pair_exchange_early",
                                            small_vec.reshape(8, n_small // 8))
    small_all = small_gathered.reshape(N_DEV, n_small)
    sib_late = _sibling_join(own_late, "grad_sibling_join_late", small_all)
    early_sums = [_pair_sum(place, gr, th, HALF_AXIS[n], "grad_pair_sum_" + n)
                  for n, gr, th in zip(EARLY, early_grads, theirs)]
    early_scatter = _exchange_start(
        "grad_scatter_early_start", early_sums,
        [lax.empty((N_CHIPS - 1,) + s.shape[1:], BF16) for s in early_sums], _late_scatter_plan, 3 * len(EARLY),
        sib_late[0], small_all)
    small_all = early_scatter[4]
    update(LATE, own_late, sib_late, small_all)

    *small_out, loss_row = _adamw_small([small[n] for n in SMALL], small_all, [m[n].reshape(1, -1) for n in SMALL],
                                        [v[n].reshape(1, -1) for n in SMALL], offs, loss_off)
    loss = loss_row[0, 0]
    for d, outs_d in zip((g, delta, new_m, new_v), small_out):
        d.update({n: o.reshape(w[n].shape) for n, o in zip(SMALL, outs_d)})

    dmod_all = small_all[:, :N_MOD * D_MODEL]
    g["w_ada"] = _ada_bwd(c_all, lax.dynamic_slice_in_dim(dmod_all, chip * ada_cols, ada_cols, axis=1))
    delta["w_ada"], new_m["w_ada"], new_v["w_ada"] = _adamw(w["w_ada"], g["w_ada"], m["w_ada"], v["w_ada"], "adamw_w_ada")

    early_sums, early_parts = _exchange_wait("grad_scatter_early_wait", early_scatter, _late_scatter_plan,
                                             delta["w_ada"])
    own_early = [_chip_sum(place, ps, pt, "grad_chip_sum_" + n)
                 for n, ps, pt in zip(EARLY, early_sums, early_parts)]
    sib_early = _sibling_join(own_early, "grad_sibling_join_early", delta["w_ada"])
    update(EARLY, own_early, sib_early, sib_early[0])

    def outs(d):
        return [d[n][None] for n in WEIGHTS]

    return (loss, gx[None], *outs(g), *outs(delta), *outs(new_m), *outs(new_v))
```
